```python
import jax, jax.numpy as jnp
from jax import lax
import numpy as np

D_MODEL = 1024
BATCH = 16
SEQ = 2048
DEPTH = 1

CONV_WIDTH = D_MODEL
CONV_K = 3
MLA_HEADS = 8
QK_NOPE = 128
QK_ROPE = 64
V_HEAD = 128
Q_LORA = 384
KV_LORA = 256
MLA_WIDTH = MLA_HEADS * V_HEAD
ROPE_THETA = 10000.0
Q_BLOCK = 128
EPS = 1e-6

_SEGMENTS = [CONV_WIDTH, CONV_WIDTH, CONV_WIDTH, CONV_WIDTH,
             Q_LORA, KV_LORA, QK_ROPE, MLA_WIDTH,
             D_MODEL, D_MODEL]
N_IN = int(sum(_SEGMENTS))
SPLIT_POINTS = [int(v) for v in np.cumsum(_SEGMENTS)[:-1]]

kernel_name = "hybrid_conv_mla_gated_merge"


def rms_norm(t, g):
    t32 = t.astype(jnp.float32)
    out = t32 * lax.rsqrt(jnp.mean(t32 * t32, axis=-1, keepdims=True) + EPS) * g.astype(jnp.float32)
    return out.astype(t.dtype)


def rope_tables(positions):
    inv_freq = ROPE_THETA ** (-jnp.arange(0, QK_ROPE, 2, dtype=jnp.float32) / QK_ROPE)
    ang = positions.astype(jnp.float32)[..., None] * inv_freq
    return jnp.cos(ang), jnp.sin(ang)


def apply_rope(t, cos, sin):
    half = t.shape[-1] // 2
    t1, t2 = t[..., :half], t[..., half:]
    cos = cos.astype(t.dtype)
    sin = sin.astype(t.dtype)
    return jnp.concatenate([t1 * cos - t2 * sin, t1 * sin + t2 * cos], axis=-1)


def causal_depthwise_conv(u, w):
    k = w.shape[0]
    s = u.shape[1]
    up = jnp.pad(u, ((0, 0), (k - 1, 0), (0, 0)))
    out = up[:, 0:s] * w[0]
    for j in range(1, k):
        out = out + up[:, j:j + s] * w[j]
    return out


def short_conv_branch(a_v, a_b, a_c, a_z, conv_w, w_conv_out):
    y = a_b * causal_depthwise_conv(a_c * a_v, conv_w)
    y = y * jax.nn.silu(a_z)
    return y @ w_conv_out


def causal_mla_attention(q_nope, q_pe, k_nope, k_pe, v):
    s_len = q_nope.shape[1]
    scale = (QK_NOPE + QK_ROPE) ** -0.5
    outs = []
    for i in range(s_len // Q_BLOCK):
        q0, q1 = i * Q_BLOCK, (i + 1) * Q_BLOCK
        sc = (jnp.einsum('bqhd,bkhd->bhqk', q_nope[:, q0:q1], k_nope[:, :q1])
              + jnp.einsum('bqhr,bkr->bhqk', q_pe[:, q0:q1], k_pe[:, :q1]))
        sc = sc.astype(jnp.float32) * scale
        qpos = jnp.arange(q0, q1)[:, None]
        kpos = jnp.arange(q1)[None, :]
        sc = jnp.where(kpos <= qpos, sc, -jnp.inf)
        p = jax.nn.softmax(sc, axis=-1).astype(v.dtype)
        outs.append(jnp.einsum('bhqk,bkhd->bqhd', p, v[:, :q1]))
    return jnp.concatenate(outs, axis=1)


def mla_branch(q_lat, kv_lat, k_rope, b_z, cos, sin, g_q, w_uq, g_kv, w_ukv, w_mla_out):
    bsz, s_len = q_lat.shape[0], q_lat.shape[1]
    q = (rms_norm(q_lat, g_q) @ w_uq).reshape(bsz, s_len, MLA_HEADS, QK_NOPE + QK_ROPE)
    q_nope = q[..., :QK_NOPE]
    q_pe = apply_rope(q[..., QK_NOPE:], cos[:, :, None, :], sin[:, :, None, :])
    kv = (rms_norm(kv_lat, g_kv) @ w_ukv).reshape(bsz, s_len, MLA_HEADS, QK_NOPE + V_HEAD)
    k_nope = kv[..., :QK_NOPE]
    v = kv[..., QK_NOPE:]
    k_pe = apply_rope(k_rope, cos, sin)
    o = causal_mla_attention(q_nope, q_pe, k_nope, k_pe, v).reshape(bsz, s_len, MLA_WIDTH)
    o = o * jax.nn.silu(b_z)
    return o @ w_mla_out


def _fwd_setup_inputs(seed: int = 0) -> dict:
    key = jax.random.key(seed)
    ks = jax.random.split(key, 20)
    f32 = jnp.float32

    def w(k, shape, fan_in):
        return jax.random.normal(k, shape, f32) * (fan_in ** -0.5)

    def gain(k, shape):
        return 1.0 + 0.01 * jax.random.normal(k, shape, f32)

    x = jax.random.normal(ks[0], (BATCH, SEQ, D_MODEL), f32)
    c = jax.random.normal(ks[1], (BATCH, D_MODEL), f32)
    positions = jnp.broadcast_to(jnp.arange(SEQ, dtype=jnp.int32)[None, :], (BATCH, SEQ))
    return {
        "x": x,
        "c": c,
        "positions": positions,
        "w_ada": w(ks[2], (DEPTH, D_MODEL, 3 * D_MODEL), D_MODEL) * 0.1,
        "b_ada": 0.01 * jax.random.normal(ks[3], (DEPTH, 3 * D_MODEL), f32),
        "g_pre": gain(ks[4], (DEPTH, D_MODEL)),
        "w_in": w(ks[5], (DEPTH, D_MODEL, N_IN), D_MODEL),
        "conv_w": w(ks[6], (DEPTH, CONV_K, CONV_WIDTH), CONV_K),
        "w_conv_out": w(ks[7], (DEPTH, CONV_WIDTH, D_MODEL), CONV_WIDTH),
        "g_q": gain(ks[8], (DEPTH, Q_LORA)),
        "w_uq": w(ks[9], (DEPTH, Q_LORA, MLA_HEADS * (QK_NOPE + QK_ROPE)), Q_LORA),
        "g_kv": gain(ks[10], (DEPTH, KV_LORA)),
        "w_ukv": w(ks[11], (DEPTH, KV_LORA, MLA_HEADS * (QK_NOPE + V_HEAD)), KV_LORA),
        "w_mla_out": w(ks[12], (DEPTH, MLA_WIDTH, D_MODEL), MLA_WIDTH),
        "w_out": w(ks[13], (DEPTH, D_MODEL, D_MODEL), D_MODEL),
        "g_post": gain(ks[14], (DEPTH, D_MODEL)),
    }


def _fwd_reference(x, c, positions, w_ada, b_ada, g_pre, w_in, conv_w, w_conv_out,
              g_q, w_uq, g_kv, w_ukv, w_mla_out, w_out, g_post):
    cos, sin = rope_tables(positions)
    for l in range(DEPTH):
        mod = c @ w_ada[l] + b_ada[l]
        shift, scale, gate = jnp.split(mod, 3, axis=-1)
        h = rms_norm(x, g_pre[l]) * (1.0 + scale[:, None, :]) + shift[:, None, :]
        proj = h @ w_in[l]
        (a_v, a_b, a_c, a_z, q_lat, kv_lat, k_rope, b_z,
         gate_a, gate_b) = jnp.split(proj, SPLIT_POINTS, axis=-1)
        y_a = short_conv_branch(a_v, a_b, a_c, a_z, conv_w[l], w_conv_out[l])
        y_b = mla_branch(q_lat, kv_lat, k_rope, b_z, cos, sin,
                         g_q[l], w_uq[l], g_kv[l], w_ukv[l], w_mla_out[l])
        m = jax.nn.sigmoid(gate_a) * y_a + jax.nn.sigmoid(gate_b) * y_b
        o = m @ w_out[l]
        x = x + gate[:, None, :] * rms_norm(o, g_post[l])
    return x


import jax as _jax
import jax.numpy as _jnp

TWIN_FORMAT = 'train_step'
FWD_PARAMS = ['x', 'c', 'positions', 'w_ada', 'b_ada', 'g_pre', 'w_in', 'conv_w', 'w_conv_out', 'g_q', 'w_uq', 'g_kv', 'w_ukv', 'w_mla_out', 'w_out', 'g_post']
TWIN_WEIGHTS = ['w_ada', 'b_ada', 'g_pre', 'w_in', 'conv_w', 'w_conv_out', 'g_q', 'w_uq', 'g_kv', 'w_ukv', 'w_mla_out', 'w_out', 'g_post']
TWIN_DIFF_INPUT = 'x'
TWIN_INPUTS = ['x', 'c', 'positions', 'w_ada', 'b_ada', 'g_pre', 'w_in', 'conv_w', 'w_conv_out', 'g_q', 'w_uq', 'g_kv', 'w_ukv', 'w_mla_out', 'w_out', 'g_post', 'loss_target', 'm_w_ada', 'm_b_ada', 'm_g_pre', 'm_w_in', 'm_conv_w', 'm_w_conv_out', 'm_g_q', 'm_w_uq', 'm_g_kv', 'm_w_ukv', 'm_w_mla_out', 'm_w_out', 'm_g_post', 'v_w_ada', 'v_b_ada', 'v_g_pre', 'v_w_in', 'v_conv_w', 'v_w_conv_out', 'v_g_q', 'v_w_uq', 'v_g_kv', 'v_w_ukv', 'v_w_mla_out', 'v_w_out', 'v_g_post']
TWIN_OUTPUTS = ['loss', 'grad_x', 'grad_w_ada', 'grad_b_ada', 'grad_g_pre', 'grad_w_in', 'grad_conv_w', 'grad_w_conv_out', 'grad_g_q', 'grad_w_uq', 'grad_g_kv', 'grad_w_ukv', 'grad_w_mla_out', 'grad_w_out', 'grad_g_post', 'delta_w_ada', 'delta_b_ada', 'delta_g_pre', 'delta_w_in', 'delta_conv_w', 'delta_w_conv_out', 'delta_g_q', 'delta_w_uq', 'delta_g_kv', 'delta_w_ukv', 'delta_w_mla_out', 'delta_w_out', 'delta_g_post', 'new_m_w_ada', 'new_m_b_ada', 'new_m_g_pre', 'new_m_w_in', 'new_m_conv_w', 'new_m_w_conv_out', 'new_m_g_q', 'new_m_w_uq', 'new_m_g_kv', 'new_m_w_ukv', 'new_m_w_mla_out', 'new_m_w_out', 'new_m_g_post', 'new_v_w_ada', 'new_v_b_ada', 'new_v_g_pre', 'new_v_w_in', 'new_v_conv_w', 'new_v_w_conv_out', 'new_v_g_q', 'new_v_w_uq', 'new_v_g_kv', 'new_v_w_ukv', 'new_v_w_mla_out', 'new_v_w_out', 'new_v_g_post']
TWIN_LEAF_KINDS = {'loss': 'loss', 'grad_x': 'grad_x', 'grad_w_ada': 'grad_w', 'grad_b_ada': 'grad_w', 'grad_g_pre': 'grad_w', 'grad_w_in': 'grad_w', 'grad_conv_w': 'grad_w', 'grad_w_conv_out': 'grad_w', 'grad_g_q': 'grad_w', 'grad_w_uq': 'grad_w', 'grad_g_kv': 'grad_w', 'grad_w_ukv': 'grad_w', 'grad_w_mla_out': 'grad_w', 'grad_w_out': 'grad_w', 'grad_g_post': 'grad_w', 'delta_w_ada': 'delta_w', 'delta_b_ada': 'delta_w', 'delta_g_pre': 'delta_w', 'delta_w_in': 'delta_w', 'delta_conv_w': 'delta_w', 'delta_w_conv_out': 'delta_w', 'delta_g_q': 'delta_w', 'delta_w_uq': 'delta_w', 'delta_g_kv': 'delta_w', 'delta_w_ukv': 'delta_w', 'delta_w_mla_out': 'delta_w', 'delta_w_out': 'delta_w', 'delta_g_post': 'delta_w', 'new_m_w_ada': 'new_m', 'new_m_b_ada': 'new_m', 'new_m_g_pre': 'new_m', 'new_m_w_in': 'new_m', 'new_m_conv_w': 'new_m', 'new_m_w_conv_out': 'new_m', 'new_m_g_q': 'new_m', 'new_m_w_uq': 'new_m', 'new_m_g_kv': 'new_m', 'new_m_w_ukv': 'new_m', 'new_m_w_mla_out': 'new_m', 'new_m_w_out': 'new_m', 'new_m_g_post': 'new_m', 'new_v_w_ada': 'new_v', 'new_v_b_ada': 'new_v', 'new_v_g_pre': 'new_v', 'new_v_w_in': 'new_v', 'new_v_conv_w': 'new_v', 'new_v_w_conv_out': 'new_v', 'new_v_g_q': 'new_v', 'new_v_w_uq': 'new_v', 'new_v_g_kv': 'new_v', 'new_v_w_ukv': 'new_v', 'new_v_w_mla_out': 'new_v', 'new_v_w_out': 'new_v', 'new_v_g_post': 'new_v'}


def _forward(args):
    return _fwd_reference(*[args[k] for k in FWD_PARAMS])


def _output_shape():
    out = _jax.eval_shape(lambda: _forward(_fwd_setup_inputs(0)))
    return out.shape, out.dtype

N_MICROBATCH = 1
ADAM_LR = 0.001
ADAM_B1 = 0.9
ADAM_B2 = 0.999
ADAM_EPS = 1e-08
ADAM_WD = 0.01
ADAM_STEP = 10
PER_EXAMPLE_BATCH_AXIS = {'x': 0, 'c': 0, 'positions': 0, 'loss_target': 0}
SHARED_INPUTS = []
_WEIGHT_DTYPES = {'w_ada': _jnp.float32, 'b_ada': _jnp.float32, 'g_pre': _jnp.float32, 'w_in': _jnp.float32, 'conv_w': _jnp.float32, 'w_conv_out': _jnp.float32, 'g_q': _jnp.float32, 'w_uq': _jnp.float32, 'g_kv': _jnp.float32, 'w_ukv': _jnp.float32, 'w_mla_out': _jnp.float32, 'w_out': _jnp.float32, 'g_post': _jnp.float32}
MOMENT_SCALE = {'w_ada': 7.938805e-01, 'b_ada': 8.516950e-01, 'g_pre': 6.023282e-02, 'w_in': 1.982962e-02, 'conv_w': 2.637362e-02, 'w_conv_out': 2.669307e-02, 'g_q': 7.869763e-03, 'w_uq': 3.892567e-03, 'g_kv': 1.500874e-02, 'w_ukv': 5.282203e-03, 'w_mla_out': 6.388264e-03, 'w_out': 2.682820e-02, 'g_post': 4.202690e-01}


def _to_microbatches(a, axis):
    t = _jnp.moveaxis(a, axis, 0)
    t = t.reshape((N_MICROBATCH, t.shape[0] // N_MICROBATCH) + t.shape[1:])
    return _jnp.moveaxis(t, 1, axis + 1)


def setup_inputs(seed: int = 0) -> dict:
    inp = _fwd_setup_inputs(seed)
    key = _jax.random.fold_in(_jax.random.key(seed), 7919)
    shape, _ = _output_shape()
    out = dict(inp)
    out["loss_target"] = _jax.random.normal(_jax.random.fold_in(key, 0), shape, _jnp.float32)
    for i, name in enumerate(TWIN_WEIGHTS):
        w = inp[name].astype(_jnp.float32)
        if MOMENT_SCALE is None:
            s = _jnp.sqrt(_jnp.mean(_jnp.square(w)) + 1e-30)
        else:
            s = MOMENT_SCALE[name]
        km, kv = _jax.random.split(_jax.random.fold_in(key, i + 1))
        out[name] = w
        out["m_" + name] = s * _jax.random.normal(km, w.shape, _jnp.float32)
        out["v_" + name] = (s * s) * _jax.random.uniform(kv, w.shape, _jnp.float32, 0.5, 1.5)
    if N_MICROBATCH > 1:
        for name, axis in PER_EXAMPLE_BATCH_AXIS.items():
            out[name] = _to_microbatches(out[name], axis)
    return {'x': out['x'], 'c': out['c'], 'positions': out['positions'], 'w_ada': out['w_ada'], 'b_ada': out['b_ada'], 'g_pre': out['g_pre'], 'w_in': out['w_in'], 'conv_w': out['conv_w'], 'w_conv_out': out['w_conv_out'], 'g_q': out['g_q'], 'w_uq': out['w_uq'], 'g_kv': out['g_kv'], 'w_ukv': out['w_ukv'], 'w_mla_out': out['w_mla_out'], 'w_out': out['w_out'], 'g_post': out['g_post'], 'loss_target': out['loss_target'], 'm_w_ada': out['m_w_ada'], 'm_b_ada': out['m_b_ada'], 'm_g_pre': out['m_g_pre'], 'm_w_in': out['m_w_in'], 'm_conv_w': out['m_conv_w'], 'm_w_conv_out': out['m_w_conv_out'], 'm_g_q': out['m_g_q'], 'm_w_uq': out['m_w_uq'], 'm_g_kv': out['m_g_kv'], 'm_w_ukv': out['m_w_ukv'], 'm_w_mla_out': out['m_w_mla_out'], 'm_w_out': out['m_w_out'], 'm_g_post': out['m_g_post'], 'v_w_ada': out['v_w_ada'], 'v_b_ada': out['v_b_ada'], 'v_g_pre': out['v_g_pre'], 'v_w_in': out['v_w_in'], 'v_conv_w': out['v_conv_w'], 'v_w_conv_out': out['v_w_conv_out'], 'v_g_q': out['v_g_q'], 'v_w_uq': out['v_w_uq'], 'v_g_kv': out['v_g_kv'], 'v_w_ukv': out['v_w_ukv'], 'v_w_mla_out': out['v_w_mla_out'], 'v_w_out': out['v_w_out'], 'v_g_post': out['v_g_post']}


def _loss(weights, diff, rest, loss_target):
    with _jax.named_scope("forward"):
        args = {**rest, TWIN_DIFF_INPUT: diff, **{k: w.astype(_WEIGHT_DTYPES[k]) for k, w in weights.items()}}
        y = _forward(args)
    with _jax.named_scope("loss_head"):
        err = _jnp.square(y.astype(_jnp.float32) - loss_target)
        return 0.5 * _jnp.sum(_jnp.mean(err, axis=-1)) if err.ndim else 0.5 * err


def _adamw(w, g, m, v):
    m = ADAM_B1 * m + (1.0 - ADAM_B1) * g
    v = ADAM_B2 * v + (1.0 - ADAM_B2) * _jnp.square(g)
    m_hat = m / (1.0 - ADAM_B1 ** ADAM_STEP)
    v_hat = v / (1.0 - ADAM_B2 ** ADAM_STEP)
    delta = -ADAM_LR * (m_hat / (_jnp.sqrt(v_hat) + ADAM_EPS) + ADAM_WD * w)
    return delta, m, v


def reference(x, c, positions, w_ada, b_ada, g_pre, w_in, conv_w, w_conv_out, g_q, w_uq, g_kv, w_ukv, w_mla_out, w_out, g_post, loss_target, m_w_ada, m_b_ada, m_g_pre, m_w_in, m_conv_w, m_w_conv_out, m_g_q, m_w_uq, m_g_kv, m_w_ukv, m_w_mla_out, m_w_out, m_g_post, v_w_ada, v_b_ada, v_g_pre, v_w_in, v_conv_w, v_w_conv_out, v_g_q, v_w_uq, v_g_kv, v_w_ukv, v_w_mla_out, v_w_out, v_g_post):
    given = dict(x=x, c=c, positions=positions, w_ada=w_ada, b_ada=b_ada, g_pre=g_pre, w_in=w_in, conv_w=conv_w, w_conv_out=w_conv_out, g_q=g_q, w_uq=w_uq, g_kv=g_kv, w_ukv=w_ukv, w_mla_out=w_mla_out, w_out=w_out, g_post=g_post, loss_target=loss_target, m_w_ada=m_w_ada, m_b_ada=m_b_ada, m_g_pre=m_g_pre, m_w_in=m_w_in, m_conv_w=m_conv_w, m_w_conv_out=m_w_conv_out, m_g_q=m_g_q, m_w_uq=m_w_uq, m_g_kv=m_g_kv, m_w_ukv=m_w_ukv, m_w_mla_out=m_w_mla_out, m_w_out=m_w_out, m_g_post=m_g_post, v_w_ada=v_w_ada, v_b_ada=v_b_ada, v_g_pre=v_g_pre, v_w_in=v_w_in, v_conv_w=v_conv_w, v_w_conv_out=v_w_conv_out, v_g_q=v_g_q, v_w_uq=v_w_uq, v_g_kv=v_g_kv, v_w_ukv=v_w_ukv, v_w_mla_out=v_w_mla_out, v_w_out=v_w_out, v_g_post=v_g_post)
    weights = {n: given[n] for n in TWIN_WEIGHTS}
    shared = {n: given[n] for n in SHARED_INPUTS}
    per_example = {n: given[n] for n in ['x', 'c', 'positions']}
    grad_fn = _jax.value_and_grad(_loss, argnums=(0, 1))

    def one_microbatch(ex, loss_target):
        ex = dict(ex)
        diff = ex.pop(TWIN_DIFF_INPUT)
        return grad_fn(weights, diff, {**shared, **ex}, loss_target)

    if N_MICROBATCH == 1:
        loss, (grad_w, grad_x) = one_microbatch(per_example, given["loss_target"])
    else:
        def body(carry, xs):
            loss_sum, grad_sum = carry
            l_k, (gw_k, gx_k) = one_microbatch(xs[0], xs[1])
            with _jax.named_scope("update"):
                return (loss_sum + l_k, _jax.tree.map(_jnp.add, grad_sum, gw_k)), gx_k

        init = (_jnp.zeros((), _jnp.float32), _jax.tree.map(_jnp.zeros_like, weights))
        (loss, grad_w), grad_x = _jax.lax.scan(body, init, (per_example, given["loss_target"]))
    with _jax.named_scope("update"):
        delta_w, new_m, new_v = {}, {}, {}
        for n in TWIN_WEIGHTS:
            delta_w[n], new_m[n], new_v[n] = _adamw(weights[n], grad_w[n], given["m_" + n], given["v_" + n])
    return (loss, grad_x, *[grad_w[n] for n in TWIN_WEIGHTS], *[delta_w[n] for n in TWIN_WEIGHTS],
            *[new_m[n] for n in TWIN_WEIGHTS], *[new_v[n] for n in TWIN_WEIGHTS])
```

```python
import functools

import numpy as np
import jax
import jax.numpy as jnp
from jax import lax
from jax.experimental import pallas as pl
from jax.experimental.pallas import tpu as pltpu

F32 = jnp.float32
BF16 = jnp.bfloat16
MESH = pl.DeviceIdType.MESH

D = 1024
H = 8
QL = 384
KVL = 256
ROPE = 64
HALF = ROPE // 2
DQK = 256
DV = 128
NSEG = 8
NP = NSEG * D
EPS = 1e-6
ROPE_THETA = 10000.0
SM_SCALE = (128 + ROPE) ** -0.5

SEG_BZ, SEG_GA, SEG_GB, SEG_LAT, SEG_V = 0, 1, 2, 3, 4

ADAM_LR = 0.001
ADAM_B1 = 0.9
ADAM_B2 = 0.999
ADAM_EPS = 1e-08
ADAM_WD = 0.01
ADAM_STEP = 10

VMEM_LIMIT = 56 * 1024 * 1024


def _params(sem=None, vmem=VMEM_LIMIT):
    kw = dict(vmem_limit_bytes=vmem)
    if sem is not None:
        kw["dimension_semantics"] = sem
    return pltpu.CompilerParams(**kw)


def _sig(v):
    return 1.0 / (1.0 + jnp.exp(-v))


def _dot(a, b):
    return jnp.dot(a, b, preferred_element_type=F32)


def _dot_nt(a, b):
    return lax.dot_general(a, b, (((1,), (1,)), ((), ())), preferred_element_type=F32)


def _dot_tn(a, b):
    return lax.dot_general(a, b, (((0,), (0,)), ((), ())), preferred_element_type=F32)


_AXIS_POS = {"x": 0, "y": 1, "c": 2}


def _coords():
    return lax.axis_index("x"), lax.axis_index("y"), lax.axis_index("c")


def _partner(axis):
    p = list(_coords())
    p[_AXIS_POS[axis]] = 1 - p[_AXIS_POS[axis]]
    return tuple(p)


def small_allgather(v, name):
    rows = v.shape[0]

    def body(v_ref, out_ref, send_sems, recv_sems):
        x, y, c = _coords()
        me = 4 * x + 2 * y + c
        out_ref[me] = v_ref[...]
        copies = []
        for k in range(1, 8):
            peer = (1 - x if k & 4 else x, 1 - y if k & 2 else y, 1 - c if k & 1 else c)
            cp = pltpu.make_async_remote_copy(
                src_ref=v_ref, dst_ref=out_ref.at[me],
                send_sem=send_sems.at[k - 1], recv_sem=recv_sems.at[k - 1],
                device_id=peer, device_id_type=MESH)
            cp.start()
            copies.append(cp)
        for cp in copies:
            cp.wait()

    return pl.pallas_call(
        body, name=name,
        out_shape=jax.ShapeDtypeStruct((8, rows, 128), F32),
        in_specs=[pl.BlockSpec(memory_space=pltpu.VMEM)],
        out_specs=pl.BlockSpec(memory_space=pltpu.VMEM),
        scratch_shapes=[pltpu.SemaphoreType.DMA((7,)), pltpu.SemaphoreType.DMA((7,))],
    )(v)


def allgather_big(arrs, orders, name):
    n = len(arrs)

    def body(*refs):
        ins, outs = refs[:n], refs[n:2 * n]
        send_sems, recv_sems, loc_sems = refs[2 * n:]
        x, y, c = _coords()
        co = {"x": x, "y": y, "c": c}

        def rcopy(a, stage, src, dst, axis):
            return pltpu.make_async_remote_copy(
                src_ref=src, dst_ref=dst,
                send_sem=send_sems.at[a, stage], recv_sem=recv_sems.at[a, stage],
                device_id=_partner(axis), device_id_type=MESH)

        local, first, second, third = [], [], [], []
        for a in range(n):
            a1, a2, a3 = orders[a]
            slot = 4 * co[a3] + 2 * co[a2] + co[a1]
            lc = pltpu.make_async_copy(ins[a], outs[a].at[slot], loc_sems.at[a])
            lc.start()
            local.append(lc)
            cp = rcopy(a, 0, ins[a], outs[a].at[slot], a1)
            cp.start()
            first.append(cp)
        for a in range(n):
            a1, a2, a3 = orders[a]
            first[a].wait_recv()
            local[a].wait()
            pair = outs[a].at[pl.ds(4 * co[a3] + 2 * co[a2], 2)]
            cp = rcopy(a, 1, pair, pair, a2)
            cp.start()
            second.append(cp)
        for a in range(n):
            a1, a2, a3 = orders[a]
            second[a].wait_recv()
            quad = outs[a].at[pl.ds(4 * co[a3], 4)]
            cp = rcopy(a, 2, quad, quad, a3)
            cp.start()
            third.append(cp)
        for a in range(n):
            third[a].wait_recv()
        for a in range(n):
            first[a].wait_send()
            second[a].wait_send()
            third[a].wait_send()

    any_spec = pl.BlockSpec(memory_space=pl.ANY)
    return pl.pallas_call(
        body, name=name,
        out_shape=[jax.ShapeDtypeStruct((8,) + a.shape, a.dtype) for a in arrs],
        in_specs=[any_spec] * n,
        out_specs=[any_spec] * n,
        scratch_shapes=[pltpu.SemaphoreType.DMA((n, 3)), pltpu.SemaphoreType.DMA((n, 3)),
                        pltpu.SemaphoreType.DMA((n,))],
    )(*arrs)


def exchange(arrs, axes, halves, name):
    n = len(arrs)

    def body(*refs):
        ins, outs = refs[:n], refs[n:2 * n]
        send_sems, recv_sems = refs[2 * n:]
        x, y, c = _coords()
        co = {"x": x, "y": y, "c": c}
        copies = []
        for a in range(n):
            src = ins[a].at[1 - co[axes[a]]] if halves else ins[a]
            cp = pltpu.make_async_remote_copy(
                src_ref=src, dst_ref=outs[a],
                send_sem=send_sems.at[a], recv_sem=recv_sems.at[a],
                device_id=_partner(axes[a]), device_id_type=MESH)
            cp.start()
            copies.append(cp)
        for cp in copies:
            cp.wait()

    any_spec = pl.BlockSpec(memory_space=pl.ANY)
    return pl.pallas_call(
        body, name=name,
        out_shape=[jax.ShapeDtypeStruct(a.shape[1:] if halves else a.shape, a.dtype) for a in arrs],
        in_specs=[any_spec] * n,
        out_specs=[any_spec] * n,
        scratch_shapes=[pltpu.SemaphoreType.DMA((n,)), pltpu.SemaphoreType.DMA((n,))],
    )(*arrs)


def rs_add_first(g, r, sel, name):
    _, _, _, rows, cols = g.shape
    tr = rows // 2

    def body(sel_ref, gk_ref, rk_ref, gs_ref, rs_ref, keep_ref, send_ref):
        keep_ref[...] = gk_ref[...] + rk_ref[...]
        send_ref[...] = (gs_ref[...] + rs_ref[...]).astype(BF16)

    blk = (None, None, None, tr, cols)
    rblk = (None, None, tr, cols)
    oblk = (None, tr, cols)
    return pl.pallas_call(
        body, name=name,
        grid_spec=pltpu.PrefetchScalarGridSpec(
            num_scalar_prefetch=1, grid=(2, 2),
            in_specs=[
                pl.BlockSpec(blk, lambda j, i, s: (s[0], s[1], j, i, 0)),
                pl.BlockSpec(rblk, lambda j, i, s: (s[1], j, i, 0)),
                pl.BlockSpec(blk, lambda j, i, s: (s[0], 1 - s[1], j, i, 0)),
                pl.BlockSpec(rblk, lambda j, i, s: (1 - s[1], j, i, 0)),
            ],
            out_specs=[pl.BlockSpec(oblk, lambda j, i, s: (j, i, 0)),
                       pl.BlockSpec(oblk, lambda j, i, s: (j, i, 0))]),
        out_shape=[jax.ShapeDtypeStruct((2, rows, cols), F32),
                   jax.ShapeDtypeStruct((2, rows, cols), BF16)],
        compiler_params=_params(),
    )(sel, g, r, g, r)


def rs_add_second(k, r, sel, name):
    _, rows, cols = k.shape
    tr = rows // 2

    def body(sel_ref, kk_ref, rk_ref, ks_ref, rs_ref, keep_ref, send_ref):
        keep_ref[...] = kk_ref[...] + rk_ref[...].astype(F32)
        send_ref[...] = (ks_ref[...] + rs_ref[...].astype(F32)).astype(BF16)

    blk = (None, tr, cols)
    oblk = (tr, cols)
    return pl.pallas_call(
        body, name=name,
        grid_spec=pltpu.PrefetchScalarGridSpec(
            num_scalar_prefetch=1, grid=(2,),
            in_specs=[
                pl.BlockSpec(blk, lambda i, s: (s[0], i, 0)),
                pl.BlockSpec(blk, lambda i, s: (s[0], i, 0)),
                pl.BlockSpec(blk, lambda i, s: (1 - s[0], i, 0)),
                pl.BlockSpec(blk, lambda i, s: (1 - s[0], i, 0)),
            ],
            out_specs=[pl.BlockSpec(oblk, lambda i, s: (i, 0)),
                       pl.BlockSpec(oblk, lambda i, s: (i, 0))]),
        out_shape=[jax.ShapeDtypeStruct((rows, cols), F32),
                   jax.ShapeDtypeStruct((rows, cols), BF16)],
        compiler_params=_params(),
    )(sel, k, r, k, r)


def proj_matmul(h, w):
    t = h.shape[0]
    tm = min(1024, t)

    def body(h_ref, w_ref, o_ref):
        o_ref[...] = _dot(h_ref[...], w_ref[...]).astype(BF16)

    return pl.pallas_call(
        body, name="proj_matmul", grid=(NSEG, t // tm),
        in_specs=[pl.BlockSpec((tm, D), lambda j, i: (i, 0)),
                  pl.BlockSpec((D, D), lambda j, i: (0, j))],
        out_specs=pl.BlockSpec((None, tm, D), lambda j, i: (j, i, 0)),
        out_shape=jax.ShapeDtypeStruct((NSEG, t, D), BF16),
        compiler_params=_params(("parallel", "parallel")),
    )(h, w)


def dh_matmul(dproj, w):
    t = dproj.shape[1]
    tm = min(1024, t)

    def body(d_ref, w_ref, o_ref, acc_ref):
        k = pl.program_id(1)

        @pl.when(k == 0)
        def _():
            acc_ref[...] = jnp.zeros_like(acc_ref)

        acc_ref[...] += _dot_nt(d_ref[...], w_ref[...])

        @pl.when(k == NSEG - 1)
        def _():
            o_ref[...] = acc_ref[...]

    return pl.pallas_call(
        body, name="dh_matmul", grid=(t // tm, NSEG),
        in_specs=[pl.BlockSpec((None, tm, D), lambda i, k: (k, i, 0)),
                  pl.BlockSpec((D, D), lambda i, k: (0, k))],
        out_specs=pl.BlockSpec((tm, D), lambda i, k: (i, 0)),
        out_shape=jax.ShapeDtypeStruct((t, D), F32),
        scratch_shapes=[pltpu.VMEM((tm, D), F32)],
        compiler_params=_params(("parallel", "arbitrary")),
    )(dproj, w)


def win_grad_matmul(h, dproj):
    t = h.shape[0]
    tk = min(1024, t)
    nk = t // tk

    def body(h_ref, d_ref, o_ref, acc_ref):
        k = pl.program_id(1)

        @pl.when(k == 0)
        def _():
            acc_ref[...] = jnp.zeros_like(acc_ref)

        acc_ref[...] += _dot_tn(h_ref[...], d_ref[...])

        @pl.when(k == nk - 1)
        def _():
            o_ref[...] = acc_ref[...]

    return pl.pallas_call(
        body, name="win_grad_matmul", grid=(NSEG, nk),
        in_specs=[pl.BlockSpec((tk, D), lambda j, k: (k, 0)),
                  pl.BlockSpec((None, tk, D), lambda j, k: (j, k, 0))],
        out_specs=pl.BlockSpec((D, D), lambda j, k: (0, j)),
        out_shape=jax.ShapeDtypeStruct((D, NP), F32),
        scratch_shapes=[pltpu.VMEM((D, D), F32)],
        compiler_params=_params(("parallel", "arbitrary")),
    )(h, dproj)


def grad_matmul(a, b, name):
    t, m = a.shape
    n = b.shape[1]
    tk = min(1024, t)
    nk = t // tk

    def body(a_ref, b_ref, o_ref, acc_ref):
        k = pl.program_id(0)

        @pl.when(k == 0)
        def _():
            acc_ref[...] = jnp.zeros_like(acc_ref)

        acc_ref[...] += _dot_tn(a_ref[...], b_ref[...])

        @pl.when(k == nk - 1)
        def _():
            o_ref[...] = acc_ref[...]

    return pl.pallas_call(
        body, name=name, grid=(nk,),
        in_specs=[pl.BlockSpec((tk, m), lambda k: (k, 0)),
                  pl.BlockSpec((tk, n), lambda k: (k, 0))],
        out_specs=pl.BlockSpec((m, n), lambda k: (0, 0)),
        out_shape=jax.ShapeDtypeStruct((m, n), F32),
        scratch_shapes=[pltpu.VMEM((m, n), F32)],
        compiler_params=_params(("arbitrary",)),
    )(a, b)


def ada_fwd(c_all, w_ada, b_cols):
    def body(c_ref, w_ref, b_ref, o_ref):
        o_ref[...] = _dot(c_ref[...].astype(BF16), w_ref[...].astype(BF16)) + b_ref[...]

    return pl.pallas_call(
        body, name="ada_fwd",
        out_shape=jax.ShapeDtypeStruct((c_all.shape[0], w_ada.shape[1]), F32),
        compiler_params=_params(),
    )(c_all, w_ada, b_cols)


def ada_bwd(c_all, dmod_cols):
    def body(c_ref, d_ref, o_ref):
        o_ref[...] = _dot_tn(c_ref[...].astype(BF16), d_ref[...].astype(BF16))

    return pl.pallas_call(
        body, name="ada_bwd",
        out_shape=jax.ShapeDtypeStruct((c_all.shape[1], dmod_cols.shape[1]), F32),
        compiler_params=_params(),
    )(c_all, dmod_cols)


def slot_sum(g):
    def body(g_ref, o_ref):
        acc = g_ref[0]
        for s in range(1, 8):
            acc = acc + g_ref[s]
        o_ref[...] = acc

    return pl.pallas_call(
        body, name="slot_sum",
        out_shape=jax.ShapeDtypeStruct(g.shape[1:], F32),
    )(g)


def prenorm_fwd(x2, scale, shift, g_pre, seq):
    t = x2.shape[0]
    tm = min(512, seq)
    tpb = seq // tm

    def body(x_ref, sc_ref, sh_ref, g_ref, h_ref):
        xv = x_ref[...]
        r = lax.rsqrt(jnp.mean(xv * xv, axis=-1, keepdims=True) + EPS)
        hv = (xv * r * g_ref[...]) * (1.0 + sc_ref[...]) + sh_ref[...]
        h_ref[...] = hv.astype(BF16)

    per_batch = pl.BlockSpec((None, 1, D), lambda i: (i // tpb, 0, 0))
    return pl.pallas_call(
        body, name="prenorm_fwd", grid=(t // tm,),
        in_specs=[pl.BlockSpec((tm, D), lambda i: (i, 0)), per_batch, per_batch,
                  pl.BlockSpec((1, D), lambda i: (0, 0))],
        out_specs=pl.BlockSpec((tm, D), lambda i: (i, 0)),
        out_shape=jax.ShapeDtypeStruct((t, D), BF16),
        compiler_params=_params(("parallel",)),
    )(x2, scale, shift, g_pre)


def prenorm_bwd(dh, x2, dout, scale, g_pre, seq):
    t = x2.shape[0]
    nb = t // seq
    tm = min(512, seq)
    tpb = seq // tm

    def body(dh_ref, x_ref, do_ref, sc_ref, g_ref, gx_ref, dsh_ref, dsc_ref, dg_ref):
        i = pl.program_id(0)
        xv = x_ref[...]
        dhv = dh_ref[...]
        g = g_ref[...]
        r = lax.rsqrt(jnp.mean(xv * xv, axis=-1, keepdims=True) + EPS)
        nrm = xv * r
        dxn = dhv * (1.0 + sc_ref[...])
        dn = dxn * g
        dx = r * (dn - nrm * jnp.mean(dn * nrm, axis=-1, keepdims=True))
        gx_ref[...] = dx + do_ref[...]

        @pl.when(i % tpb == 0)
        def _():
            dsh_ref[...] = jnp.zeros_like(dsh_ref)
            dsc_ref[...] = jnp.zeros_like(dsc_ref)

        @pl.when(i == 0)
        def _():
            dg_ref[...] = jnp.zeros_like(dg_ref)

        dsh_ref[...] += jnp.sum(dhv, axis=0, keepdims=True)
        dsc_ref[...] += jnp.sum(dhv * (nrm * g), axis=0, keepdims=True)
        dg_ref[...] += jnp.sum(dxn * nrm, axis=0, keepdims=True)

    row = pl.BlockSpec((tm, D), lambda i: (i, 0))
    per_batch = pl.BlockSpec((None, 1, D), lambda i: (i // tpb, 0, 0))
    vec = pl.BlockSpec((1, D), lambda i: (0, 0))
    return pl.pallas_call(
        body, name="prenorm_bwd", grid=(t // tm,),
        in_specs=[row, row, row, per_batch, vec],
        out_specs=[row, per_batch, per_batch, vec],
        out_shape=[jax.ShapeDtypeStruct((t, D), F32),
                   jax.ShapeDtypeStruct((nb, 1, D), F32),
                   jax.ShapeDtypeStruct((nb, 1, D), F32),
                   jax.ShapeDtypeStruct((1, D), F32)],
        compiler_params=_params(("arbitrary",)),
    )(dh, x2, dout, scale, g_pre)


CONV_TC = 128


def _shift_down(u, k, rows):
    idx = lax.broadcasted_iota(jnp.int32, u.shape, 0)
    return jnp.where(idx >= k, pltpu.roll(u, k, 0), 0.0)


def _shift_up(u, k, rows):
    idx = lax.broadcasted_iota(jnp.int32, u.shape, 0)
    return jnp.where(idx < rows - k, pltpu.roll(u, rows - k, 0), 0.0)


def conv_fwd(proj, conv_w, seq):
    t = proj.shape[1]
    nb = t // seq

    def body(p_ref, w_ref, y_ref):
        av = p_ref[0].astype(F32)
        ab = p_ref[1].astype(F32)
        ac = p_ref[2].astype(F32)
        az = p_ref[3].astype(F32)
        w = w_ref[...]
        u = ac * av
        y1 = _shift_down(u, 2, seq) * w[0:1] + _shift_down(u, 1, seq) * w[1:2] + u * w[2:3]
        y_ref[...] = (ab * y1 * (az * _sig(az))).astype(BF16)

    return pl.pallas_call(
        body, name="conv_fwd", grid=(nb, D // CONV_TC),
        in_specs=[pl.BlockSpec((4, seq, CONV_TC), lambda b, ci: (1, b, ci)),
                  pl.BlockSpec((8, CONV_TC), lambda b, ci: (0, ci))],
        out_specs=pl.BlockSpec((seq, CONV_TC), lambda b, ci: (b, ci)),
        out_shape=jax.ShapeDtypeStruct((t, D), BF16),
        compiler_params=_params(("parallel", "parallel")),
    )(proj, conv_w)


def conv_bwd(dproj, proj, dy, conv_w, seq):
    t = proj.shape[1]
    nb = t // seq

    def body(dp_in_ref, p_ref, dy_ref, w_ref, dp_ref, dw_ref):
        b = pl.program_id(1)
        av = p_ref[0].astype(F32)
        ab = p_ref[1].astype(F32)
        ac = p_ref[2].astype(F32)
        az = p_ref[3].astype(F32)
        dyv = dy_ref[...].astype(F32)
        w = w_ref[...]
        u = ac * av
        u1 = _shift_down(u, 1, seq)
        u2 = _shift_down(u, 2, seq)
        y1 = u2 * w[0:1] + u1 * w[1:2] + u * w[2:3]
        sz = _sig(az)
        silu = az * sz
        dy1 = dyv * ab * silu
        du = dy1 * w[2:3] + _shift_up(dy1, 1, seq) * w[1:2] + _shift_up(dy1, 2, seq) * w[0:1]
        dp_ref[0] = (du * ac).astype(BF16)
        dp_ref[1] = (dyv * y1 * silu).astype(BF16)
        dp_ref[2] = (du * av).astype(BF16)
        dp_ref[3] = (dyv * ab * y1 * (sz * (1.0 + az * (1.0 - sz)))).astype(BF16)

        @pl.when(b == 0)
        def _():
            dw_ref[...] = jnp.zeros_like(dw_ref)

        dw_ref[0:1, :] += jnp.sum(dy1 * u2, axis=0, keepdims=True)
        dw_ref[1:2, :] += jnp.sum(dy1 * u1, axis=0, keepdims=True)
        dw_ref[2:3, :] += jnp.sum(dy1 * u, axis=0, keepdims=True)

    return pl.pallas_call(
        body, name="conv_bwd", grid=(D // CONV_TC, nb),
        in_specs=[pl.BlockSpec(memory_space=pl.ANY),
                  pl.BlockSpec((4, seq, CONV_TC), lambda ci, b: (1, b, ci)),
                  pl.BlockSpec((seq, CONV_TC), lambda ci, b: (b, ci)),
                  pl.BlockSpec((8, CONV_TC), lambda ci, b: (0, ci))],
        out_specs=[pl.BlockSpec((4, seq, CONV_TC), lambda ci, b: (1, b, ci)),
                   pl.BlockSpec((8, CONV_TC), lambda ci, b: (0, ci))],
        out_shape=[jax.ShapeDtypeStruct(dproj.shape, BF16),
                   jax.ShapeDtypeStruct((8, D), F32)],
        input_output_aliases={0: 0},
        compiler_params=_params(("parallel", "arbitrary")),
    )(dproj, proj, dy, conv_w)


def _rope_tables(pos_ref, invf_ref, ma_ref, mb_ref):
    ang = pos_ref[...].astype(F32) * invf_ref[...]
    cs = jnp.cos(ang)
    sn = jnp.sin(ang)
    return cs, sn * ma_ref[...], sn * mb_ref[...]


def _head_tables(cs, sa, sb):
    one = jnp.ones_like(cs)
    zero = jnp.zeros_like(cs)
    return (jnp.tile(jnp.concatenate([one, cs], axis=1), (1, H)),
            jnp.tile(jnp.concatenate([zero, sa], axis=1), (1, H)),
            jnp.tile(jnp.concatenate([zero, sb], axis=1), (1, H)))


def _rotate(v, cs, sa, sb, sign):
    width = v.shape[1]
    return v * cs + sign * (pltpu.roll(v, width - HALF, 1) * sa + pltpu.roll(v, HALF, 1) * sb)


MLA_TM = 256


def mla_prep_fwd(proj, pos, g_q, g_kv, wuq, wukv, tabs):
    t = proj.shape[1]
    tm = min(MLA_TM, t)

    def body(lat_ref, pos_ref, gq_ref, gkv_ref, wuq_ref, wukv_ref, invf_ref, ma_ref, mb_ref,
             q_ref, k_ref, kv_ref, qn_ref, kvn_ref):
        lat = lat_ref[...].astype(F32)
        ql = lat[:, :QL]
        kl = lat[:, QL:QL + KVL]
        kr = lat[:, QL + KVL:QL + KVL + 128]
        qn = (ql * lax.rsqrt(jnp.mean(ql * ql, axis=-1, keepdims=True) + EPS) * gq_ref[...]).astype(BF16)
        kvn = (kl * lax.rsqrt(jnp.mean(kl * kl, axis=-1, keepdims=True) + EPS) * gkv_ref[...]).astype(BF16)
        qn_ref[...] = qn
        kvn_ref[...] = kvn
        cs, sa, sb = _rope_tables(pos_ref, invf_ref, ma_ref, mb_ref)
        hc, ha, hb = _head_tables(cs, sa, sb)
        q = _dot(qn, wuq_ref[...])
        q_ref[...] = _rotate(q, hc, ha, hb, 1.0).astype(BF16)
        kv = _dot(kvn, wukv_ref[...]).astype(BF16)
        kv_ref[...] = kv
        kpe = _rotate(kr, cs, sa, sb, 1.0).astype(BF16)
        for hh in range(H):
            k_ref[:, hh * DQK:hh * DQK + 128] = kv[:, hh * DQK:hh * DQK + 128]
            k_ref[:, hh * DQK + 128:(hh + 1) * DQK] = kpe

    row = lambda w: pl.BlockSpec((tm, w), lambda i: (i, 0))
    const = lambda a: pl.BlockSpec(a.shape, lambda i: (0,) * a.ndim)
    return pl.pallas_call(
        body, name="mla_prep_fwd", grid=(t // tm,),
        in_specs=[pl.BlockSpec((None, tm, D), lambda i: (SEG_LAT, i, 0)), row(1),
                  const(g_q), const(g_kv), const(wuq), const(wukv)] + [const(a) for a in tabs],
        out_specs=[row(H * DQK), row(H * DQK), row(H * DQK), row(QL), row(KVL)],
        out_shape=[jax.ShapeDtypeStruct((t, H * DQK), BF16)] * 3
        + [jax.ShapeDtypeStruct((t, QL), BF16), jax.ShapeDtypeStruct((t, KVL), BF16)],
        compiler_params=_params(("parallel",)),
    )(proj, pos, g_q, g_kv, wuq, wukv, *tabs)


def mla_prep_bwd(dproj, proj, dq_rot, dk, dv, pos, g_q, g_kv, wuq, wukv, tabs):
    t = proj.shape[1]
    tm = min(MLA_TM, t)

    def body(dp_in_ref, lat_ref, dqr_ref, dk_ref, dv_ref, pos_ref, gq_ref, gkv_ref, wuq_ref, wukv_ref,
             invf_ref, ma_ref, mb_ref, dp_ref, dq_ref, dkv_ref, dgq_ref, dgkv_ref):
        i = pl.program_id(0)
        lat = lat_ref[...].astype(F32)
        ql = lat[:, :QL]
        kl = lat[:, QL:QL + KVL]
        rq = lax.rsqrt(jnp.mean(ql * ql, axis=-1, keepdims=True) + EPS)
        rk = lax.rsqrt(jnp.mean(kl * kl, axis=-1, keepdims=True) + EPS)
        nq = ql * rq
        nk = kl * rk
        cs, sa, sb = _rope_tables(pos_ref, invf_ref, ma_ref, mb_ref)
        hc, ha, hb = _head_tables(cs, sa, sb)
        dq = _rotate(dqr_ref[...], hc, ha, hb, -1.0).astype(BF16)
        dq_ref[...] = dq
        dkpe = jnp.zeros((tm, 128), F32)
        for hh in range(H):
            dkv_ref[:, hh * DQK:hh * DQK + 128] = dk_ref[:, hh * DQK:hh * DQK + 128]
            dkv_ref[:, hh * DQK + 128:(hh + 1) * DQK] = dv_ref[:, hh * DV:(hh + 1) * DV]
            dkpe = dkpe + dk_ref[:, hh * DQK + 128:(hh + 1) * DQK].astype(F32)
        lane = lax.broadcasted_iota(jnp.int32, (tm, 128), 1)
        dkr = jnp.where(lane < ROPE, _rotate(dkpe, cs, sa, sb, -1.0), 0.0)
        dqn = _dot_nt(dq, wuq_ref[...])
        dkvn = _dot_nt(dkv_ref[...], wukv_ref[...])
        gq = gq_ref[...]
        gkv = gkv_ref[...]
        dnq = dqn * gq
        dnk = dkvn * gkv
        dql = rq * (dnq - nq * jnp.mean(dnq * nq, axis=-1, keepdims=True))
        dkl = rk * (dnk - nk * jnp.mean(dnk * nk, axis=-1, keepdims=True))
        dp_ref[:, :QL] = dql.astype(BF16)
        dp_ref[:, QL:QL + KVL] = dkl.astype(BF16)
        dp_ref[:, QL + KVL:QL + KVL + 128] = dkr.astype(BF16)
        dp_ref[:, QL + KVL + 128:] = jnp.zeros((tm, D - QL - KVL - 128), BF16)

        @pl.when(i == 0)
        def _():
            dgq_ref[...] = jnp.zeros_like(dgq_ref)
            dgkv_ref[...] = jnp.zeros_like(dgkv_ref)

        dgq_ref[...] += jnp.sum(dqn * nq, axis=0, keepdims=True)
        dgkv_ref[...] += jnp.sum(dkvn * nk, axis=0, keepdims=True)

    row = lambda w: pl.BlockSpec((tm, w), lambda i: (i, 0))
    const = lambda a: pl.BlockSpec(a.shape, lambda i: (0,) * a.ndim)
    seg = pl.BlockSpec((None, tm, D), lambda i: (SEG_LAT, i, 0))
    return pl.pallas_call(
        body, name="mla_prep_bwd", grid=(t // tm,),
        in_specs=[pl.BlockSpec(memory_space=pl.ANY), seg, row(H * DQK), row(H * DQK), row(H * DV), row(1),
                  const(g_q), const(g_kv), const(wuq), const(wukv)] + [const(a) for a in tabs],
        out_specs=[seg, row(H * DQK), row(H * DQK),
                   pl.BlockSpec((1, QL), lambda i: (0, 0)), pl.BlockSpec((1, KVL), lambda i: (0, 0))],
        out_shape=[jax.ShapeDtypeStruct(dproj.shape, BF16),
                   jax.ShapeDtypeStruct((t, H * DQK), BF16), jax.ShapeDtypeStruct((t, H * DQK), BF16),
                   jax.ShapeDtypeStruct((1, QL), F32), jax.ShapeDtypeStruct((1, KVL), F32)],
        input_output_aliases={0: 0},
        compiler_params=_params(("arbitrary",)),
    )(dproj, proj, dq_rot, dk, dv, pos, g_q, g_kv, wuq, wukv, *tabs)


def _causal_mask(s, n):
    row = lax.broadcasted_iota(jnp.int32, (n, n), 0)
    col = lax.broadcasted_iota(jnp.int32, (n, n), 1)
    return jnp.where(col <= row, s, -1e30)


def flash_fwd(q, k, kv, nb, seq):
    t = q.shape[0]
    tq = min(512, seq)
    nq = seq // tq

    def body(q_ref, k_ref, v_ref, o_ref, lse_ref):
        qi = pl.program_id(2)
        qv = q_ref[...]

        def step(j, carry, masked):
            m, l, acc = carry
            sl = pl.ds(pl.multiple_of(j * tq, tq), tq)
            s = _dot_nt(qv, k_ref[sl, :]) * SM_SCALE
            if masked:
                s = _causal_mask(s, tq)
            m_new = jnp.maximum(m, jnp.max(s, axis=1, keepdims=True))
            p = jnp.exp(s - m_new)
            alpha = jnp.exp(m - m_new)
            l = alpha * l + jnp.sum(p, axis=1, keepdims=True)
            acc = alpha * acc + _dot(p.astype(BF16), v_ref[sl, :])
            return m_new, l, acc

        init = (jnp.full((tq, 1), -1e30, F32), jnp.zeros((tq, 1), F32), jnp.zeros((tq, DV), F32))
        carry = lax.fori_loop(0, qi, lambda j, cr: step(j, cr, False), init)
        m, l, acc = step(qi, carry, True)
        o_ref[...] = (acc / l).astype(BF16)
        lse_ref[...] = jnp.broadcast_to(m + jnp.log(l), (tq, DV))

    out_blk = pl.BlockSpec((tq, DV), lambda b, h, i: (b * nq + i, h))
    return pl.pallas_call(
        body, name="flash_fwd", grid=(nb, H, nq),
        in_specs=[pl.BlockSpec((tq, DQK), lambda b, h, i: (b * nq + i, h)),
                  pl.BlockSpec((seq, DQK), lambda b, h, i: (b, h)),
                  pl.BlockSpec((seq, DV), lambda b, h, i: (b, 2 * h + 1))],
        out_specs=[out_blk, out_blk],
        out_shape=[jax.ShapeDtypeStruct((t, H * DV), BF16), jax.ShapeDtypeStruct((t, H * DV), F32)],
        compiler_params=_params(("parallel", "parallel", "arbitrary")),
    )(q, k, kv)


def flash_bwd(q, k, kv, o, do, lse, nb, seq):
    t = q.shape[0]
    tq = min(512, seq)
    nq = seq // tq

    def body(q_ref, k_ref, v_ref, o_ref, do_ref, lse_ref, dq_ref, dk_ref, dv_ref):
        ki = pl.program_id(2)

        @pl.when(ki == 0)
        def _():
            dq_ref[...] = jnp.zeros_like(dq_ref)

        kb = k_ref[...]
        vb = v_ref[...]

        def step(qi, carry, masked):
            dk, dv = carry
            sl = pl.ds(pl.multiple_of(qi * tq, tq), tq)
            qv = q_ref[sl, :]
            dov = do_ref[sl, :]
            delta = jnp.sum(dov.astype(F32) * o_ref[sl, :].astype(F32), axis=1, keepdims=True)
            s = _dot_nt(qv, kb) * SM_SCALE
            if masked:
                s = _causal_mask(s, tq)
            p = jnp.exp(s - lse_ref[sl, :][:, :1])
            dp = _dot_nt(dov, vb)
            ds = (p * (dp - delta) * SM_SCALE).astype(BF16)
            dv = dv + _dot_tn(p.astype(BF16), dov)
            dk = dk + _dot_tn(ds, qv)
            dq_ref[sl, :] += _dot(ds, kb)
            return dk, dv

        carry = step(ki, (jnp.zeros((tq, DQK), F32), jnp.zeros((tq, DV), F32)), True)
        dk, dv = lax.fori_loop(ki + 1, nq, lambda qi, cr: step(qi, cr, False), carry)
        dk_ref[...] = dk.astype(BF16)
        dv_ref[...] = dv.astype(BF16)

    full = lambda w, col: pl.BlockSpec((seq, w), col)
    return pl.pallas_call(
        body, name="flash_bwd", grid=(nb, H, nq),
        in_specs=[full(DQK, lambda b, h, i: (b, h)),
                  pl.BlockSpec((tq, DQK), lambda b, h, i: (b * nq + i, h)),
                  pl.BlockSpec((tq, DV), lambda b, h, i: (b * nq + i, 2 * h + 1)),
                  full(DV, lambda b, h, i: (b, h)), full(DV, lambda b, h, i: (b, h)),
                  full(DV, lambda b, h, i: (b, h))],
        out_specs=[full(DQK, lambda b, h, i: (b, h)),
                   pl.BlockSpec((tq, DQK), lambda b, h, i: (b * nq + i, h)),
                   pl.BlockSpec((tq, DV), lambda b, h, i: (b * nq + i, h))],
        out_shape=[jax.ShapeDtypeStruct((t, H * DQK), F32), jax.ShapeDtypeStruct((t, H * DQK), BF16),
                   jax.ShapeDtypeStruct((t, H * DV), BF16)],
        compiler_params=_params(("parallel", "parallel", "arbitrary")),
    )(q, k, kv, o, do, lse)


TAIL_TM = 256


def tail_fwd(y, attn, proj, x2, tgt, gate, g_post, wco, wmo, wout, seq):
    t = y.shape[0]
    nb = t // seq
    tm = min(TAIL_TM, seq)
    tpb = seq // tm

    def body(y_ref, at_ref, p_ref, x_ref, t_ref, gate_ref, gp_ref, wco_ref, wmo_ref, wout_ref,
             o_ref, ya_ref, yb_ref, m_ref, do2_ref, dout_ref, dgate_ref, dgp_ref, loss_ref):
        i = pl.program_id(0)
        bz = p_ref[0].astype(F32)
        ga = p_ref[1].astype(F32)
        gb = p_ref[2].astype(F32)
        ov = (at_ref[...].astype(F32) * (bz * _sig(bz))).astype(BF16)
        o_ref[...] = ov
        ya = _dot(y_ref[...], wco_ref[...])
        yb = _dot(ov, wmo_ref[...])
        ya_ref[...] = ya.astype(BF16)
        yb_ref[...] = yb.astype(BF16)
        mv = (_sig(ga) * ya + _sig(gb) * yb).astype(BF16)
        m_ref[...] = mv
        o2 = _dot(mv, wout_ref[...])
        r = lax.rsqrt(jnp.mean(o2 * o2, axis=-1, keepdims=True) + EPS)
        nrm = o2 * r
        gp = gp_ref[...]
        gate_v = gate_ref[...]
        rn = nrm * gp
        err = x_ref[...] + gate_v * rn - t_ref[...]
        dout = err * (1.0 / D)
        dout_ref[...] = dout
        dn = dout * gate_v * gp
        do2_ref[...] = (r * (dn - nrm * jnp.mean(dn * nrm, axis=-1, keepdims=True))).astype(BF16)

        @pl.when(i % tpb == 0)
        def _():
            dgate_ref[...] = jnp.zeros_like(dgate_ref)

        @pl.when(i == 0)
        def _():
            dgp_ref[...] = jnp.zeros_like(dgp_ref)
            loss_ref[...] = jnp.zeros_like(loss_ref)

        dgate_ref[...] += jnp.sum(dout * rn, axis=0, keepdims=True)
        dgp_ref[...] += jnp.sum(dout * gate_v * nrm, axis=0, keepdims=True)
        loss_ref[...] += 0.5 * jnp.sum(jnp.mean(err * err, axis=-1, keepdims=True), axis=0, keepdims=True)

    row = pl.BlockSpec((tm, D), lambda i: (i, 0))
    per_batch = pl.BlockSpec((None, 1, D), lambda i: (i // tpb, 0, 0))
    vec = pl.BlockSpec((1, D), lambda i: (0, 0))
    wgt = pl.BlockSpec((D, D), lambda i: (0, 0))
    act = jax.ShapeDtypeStruct((t, D), BF16)
    return pl.pallas_call(
        body, name="tail_fwd", grid=(t // tm,),
        in_specs=[row, row, pl.BlockSpec((3, tm, D), lambda i: (0, i, 0)), row, row, per_batch, vec,
                  wgt, wgt, wgt],
        out_specs=[row, row, row, row, row, row, per_batch, vec, pl.BlockSpec((1, 1), lambda i: (0, 0))],
        out_shape=[act, act, act, act, act, jax.ShapeDtypeStruct((t, D), F32),
                   jax.ShapeDtypeStruct((nb, 1, D), F32), jax.ShapeDtypeStruct((1, D), F32),
                   jax.ShapeDtypeStruct((1, 1), F32)],
        compiler_params=_params(("arbitrary",)),
    )(y, attn, proj, x2, tgt, gate, g_post, wco, wmo, wout)


def tail_bwd(do2, proj, ya, yb, attn, wout, wmo, wco):
    t = do2.shape[0]
    tm = min(TAIL_TM, t)

    def body(do2_ref, p_ref, ya_ref, yb_ref, at_ref, wout_ref, wmo_ref, wco_ref,
             dp_ref, dya_ref, dyb_ref, dat_ref, dy_ref):
        bz = p_ref[0].astype(F32)
        ga = p_ref[1].astype(F32)
        gb = p_ref[2].astype(F32)
        dm = _dot_nt(do2_ref[...], wout_ref[...])
        sa = _sig(ga)
        sb = _sig(gb)
        dya = (dm * sa).astype(BF16)
        dyb = (dm * sb).astype(BF16)
        dya_ref[...] = dya
        dyb_ref[...] = dyb
        dp_ref[1] = (dm * ya_ref[...].astype(F32) * (sa * (1.0 - sa))).astype(BF16)
        dp_ref[2] = (dm * yb_ref[...].astype(F32) * (sb * (1.0 - sb))).astype(BF16)
        dov = _dot_nt(dyb, wmo_ref[...])
        sz = _sig(bz)
        dat_ref[...] = (dov * (bz * sz)).astype(BF16)
        dp_ref[0] = (dov * at_ref[...].astype(F32) * (sz * (1.0 + bz * (1.0 - sz)))).astype(BF16)
        dy_ref[...] = _dot_nt(dya, wco_ref[...]).astype(BF16)

    row = pl.BlockSpec((tm, D), lambda i: (i, 0))
    seg3 = pl.BlockSpec((3, tm, D), lambda i: (0, i, 0))
    wgt = pl.BlockSpec((D, D), lambda i: (0, 0))
    act = jax.ShapeDtypeStruct((t, D), BF16)
    return pl.pallas_call(
        body, name="tail_bwd", grid=(t // tm,),
        in_specs=[row, seg3, row, row, row, wgt, wgt, wgt],
        out_specs=[seg3, row, row, row, row],
        out_shape=[jax.ShapeDtypeStruct((NSEG, t, D), BF16), act, act, act, act],
        compiler_params=_params(("parallel",)),
    )(do2, proj, ya, yb, attn, wout, wmo, wco)


def adamw(w, m, v, g, g2, name):
    rows, cols = w.shape
    tr = rows
    for cand in (256, 128, 64, 32, 16, 8):
        if rows % cand == 0 and rows > cand:
            tr = cand
            break
    has2 = g2 is not None

    def body(*refs):
        if has2:
            w_ref, m_ref, v_ref, g_ref, g2_ref, go_ref, d_ref, mo_ref, vo_ref = refs
            grad = g_ref[...] + g2_ref[...].astype(F32)
        else:
            w_ref, m_ref, v_ref, g_ref, go_ref, d_ref, mo_ref, vo_ref = refs
            grad = g_ref[...]
        mn = ADAM_B1 * m_ref[...] + (1.0 - ADAM_B1) * grad
        vn = ADAM_B2 * v_ref[...] + (1.0 - ADAM_B2) * (grad * grad)
        m_hat = mn / (1.0 - ADAM_B1 ** ADAM_STEP)
        v_hat = vn / (1.0 - ADAM_B2 ** ADAM_STEP)
        go_ref[...] = grad
        d_ref[...] = -ADAM_LR * (m_hat / (jnp.sqrt(v_hat) + ADAM_EPS) + ADAM_WD * w_ref[...])
        mo_ref[...] = mn
        vo_ref[...] = vn

    blk = pl.BlockSpec((tr, cols), lambda i: (i, 0))
    ins = [w, m, v, g] + ([g2] if has2 else [])
    return pl.pallas_call(
        body, name=name, grid=(rows // tr,),
        in_specs=[blk] * len(ins), out_specs=[blk] * 4,
        out_shape=[jax.ShapeDtypeStruct((rows, cols), F32)] * 4,
        compiler_params=_params(("parallel",)),
    )(*ins)


_ORD_A = ("x", "y", "c")
_ORD_B = ("y", "x", "c")


def _from_slots(g, order, col_sharded):
    r, cc = g.shape[1:]
    g = g.reshape(2, 2, 2, r, cc)
    names = (order[2], order[1], order[0])
    perm = tuple(names.index(a) for a in ("x", "y", "c"))
    if col_sharded:
        return g.transpose((3,) + perm + (4,)).reshape(r, 8 * cc)
    return g.transpose(perm + (3, 4)).reshape(8 * r, cc)


def _to_slots(full, order, col_sharded):
    if col_sharded:
        r = full.shape[0]
        cc = full.shape[1] // 8
        g = full.reshape(r, 2, 2, 2, cc).transpose(1, 2, 3, 0, 4)
    else:
        r = full.shape[0] // 8
        cc = full.shape[1]
        g = full.reshape(2, 2, 2, r, cc)
    names = ("x", "y", "c")
    perm = tuple(names.index(a) for a in order)
    return g.transpose(perm + (3, 4))


def _pad_win(w):
    v_bcz = w[:, 0:4096]
    lat = w[:, 4096:4800]
    bz_g = w[:, 4800:7872]
    z = jnp.zeros((w.shape[0], D - 704), w.dtype)
    return jnp.concatenate([bz_g, lat, z, v_bcz], axis=1)


def _unpad_win(g):
    return jnp.concatenate([g[:, 4096:8192], g[:, 3072:3072 + 704], g[:, 0:3072]], axis=1)


def _rows128(a, rows):
    flat = a.reshape(-1)
    return jnp.pad(flat, (0, rows * 128 - flat.shape[0])).reshape(rows, 128)


def kernel(x, c, positions, w_ada, b_ada, g_pre, w_in, conv_w, w_conv_out, g_q, w_uq, g_kv, w_ukv, w_mla_out, w_out, g_post, loss_target, m_w_ada, m_b_ada, m_g_pre, m_w_in, m_conv_w, m_w_conv_out, m_g_q, m_w_uq, m_g_kv, m_w_ukv, m_w_mla_out, m_w_out, m_g_post, v_w_ada, v_b_ada, v_g_pre, v_w_in, v_conv_w, v_w_conv_out, v_g_q, v_w_uq, v_g_kv, v_w_ukv, v_w_mla_out, v_w_out, v_g_post):
    nb, seq, _ = x.shape
    t = nb * seq
    mx, my, mc = lax.axis_index("x"), lax.axis_index("y"), lax.axis_index("c")
    me = 4 * mx + 2 * my + mc
    co = {"x": mx, "y": my, "c": mc}

    x2 = x.reshape(t, D)
    tgt2 = loss_target.reshape(t, D)
    pos2 = positions.reshape(t, 1)

    packed = jnp.concatenate([c.reshape(2 * D // 128, 128), _rows128(conv_w[0], 8)], axis=0)
    gath = small_allgather(packed, "gather_cond")
    c_all = gath[:, :16].reshape(8 * nb, D)
    conv_full = gath[:, 16:19].reshape(8, 3, 128).transpose(1, 0, 2).reshape(3, D)
    conv_full8 = jnp.pad(conv_full, ((0, 5), (0, 0)))
    ada_cols = w_ada.shape[2]
    b_cols = lax.dynamic_slice(b_ada, (0, me * ada_cols), (1, ada_cols))
    mod_part = ada_fwd(c_all, w_ada[0], b_cols)
    mod_g = small_allgather(mod_part.reshape(8 * nb * ada_cols // 128, 128), "gather_mod")
    mod_all = mod_g.reshape(8, 8 * nb, ada_cols).transpose(1, 0, 2).reshape(8 * nb, 8 * ada_cols)
    mod = lax.dynamic_slice(mod_all, (me * nb, 0), (nb, 3 * D))
    shift = mod[:, 0:D].reshape(nb, 1, D)
    scale = mod[:, D:2 * D].reshape(nb, 1, D)
    gate = mod[:, 2 * D:3 * D].reshape(nb, 1, D)

    win_b = w_in[0].astype(BF16)
    half = win_b.shape[0] // 2
    shards = [win_b[:half], w_conv_out[0].astype(BF16), w_mla_out[0].astype(BF16),
              win_b[half:], w_out[0].astype(BF16), w_uq[0].astype(BF16), w_ukv[0].astype(BF16)]
    orders = [_ORD_A] * 3 + [_ORD_B] * 4
    gw = allgather_big(shards, orders, "gather_weights")
    win_full = jnp.concatenate([_from_slots(gw[0], _ORD_A, True), _from_slots(gw[3], _ORD_B, True)], axis=0)
    win_p = _pad_win(win_full)
    wco = _from_slots(gw[1], _ORD_A, False)
    wmo = _from_slots(gw[2], _ORD_A, False)
    wout = _from_slots(gw[4], _ORD_B, False)
    wuq_full = _from_slots(gw[5], _ORD_B, True)
    wuq_p = jnp.pad(wuq_full.reshape(QL, H, 192), ((0, 0), (0, 0), (0, DQK - 192))).reshape(QL, H * DQK)
    wukv = _from_slots(gw[6], _ORD_B, True)

    inv_freq = ROPE_THETA ** (-jnp.arange(0, ROPE, 2, dtype=F32) / ROPE)
    invf = jnp.concatenate([inv_freq, inv_freq, jnp.zeros((128 - ROPE,), F32)]).reshape(1, 128)
    lane = np.arange(128)
    tabs = (invf,
            jnp.asarray(np.where(lane < HALF, -1.0, 0.0).reshape(1, 128), F32),
            jnp.asarray(np.where((lane >= HALF) & (lane < ROPE), 1.0, 0.0).reshape(1, 128), F32))

    h = prenorm_fwd(x2, scale, shift, g_pre, seq)
    proj = proj_matmul(h, win_p)
    y = conv_fwd(proj, conv_full8, seq)
    q_rot, k_cat, kv, qn, kvn = mla_prep_fwd(proj, pos2, g_q, g_kv, wuq_p, wukv, tabs)
    attn, lse = flash_fwd(q_rot, k_cat, kv, nb, seq)
    o, ya, yb, m, do2, dout, dgate, dg_post, loss_part = tail_fwd(
        y, attn, proj, x2, tgt2, gate, g_post, wco, wmo, wout, seq)

    dproj, dya, dyb, dattn, dy = tail_bwd(do2, proj, ya, yb, attn, wout, wmo, wco)
    g_wout = grad_matmul(m, do2, "grad_w_out")
    g_wmo = grad_matmul(o, dyb, "grad_w_mla_out")
    g_wco = grad_matmul(y, dya, "grad_w_conv_out")
    dproj, dconv = conv_bwd(dproj, proj, dy, conv_full8, seq)
    dq_rot, dk, dv = flash_bwd(q_rot, k_cat, kv, attn, dattn, lse, nb, seq)
    dproj, dq, dkv, dg_q, dg_kv = mla_prep_bwd(dproj, proj, dq_rot, dk, dv, pos2, g_q, g_kv, wuq_p, wukv, tabs)
    g_wuq_p = grad_matmul(qn, dq, "grad_w_uq")
    g_wukv = grad_matmul(kvn, dkv, "grad_w_ukv")
    g_win_p = win_grad_matmul(h, dproj)
    dh = dh_matmul(dproj, win_p)
    grad_x2, dshift, dscale, dg_pre = prenorm_bwd(dh, x2, dout, scale, g_pre, seq)

    dmod = jnp.concatenate([dshift, dscale, dgate], axis=2).reshape(nb * 3 * D // 128, 128)
    small = jnp.concatenate([
        dmod, _rows128(dg_pre, 8), _rows128(dg_post, 8), _rows128(dg_q, 8), _rows128(dg_kv, 8),
        dconv[0:3].reshape(24, 128), _rows128(loss_part, 8)], axis=0)
    small_g = small_allgather(small, "gather_small_grads")
    sums = slot_sum(small_g)
    dmod_all = small_g[:, 0:48].reshape(8 * nb, 3 * D)
    g_bada = (sums[0:24] + sums[24:48]).reshape(1, 3 * D)
    g_gpre = sums[48:56].reshape(1, D)
    g_gpost = sums[56:64].reshape(1, D)
    g_gq = sums[64:67].reshape(1, QL)
    g_gkv = sums[72:74].reshape(1, KVL)
    g_conv_full = sums[80:104].reshape(3, D)
    loss = sums[104, 0]
    g_conv = lax.dynamic_slice(g_conv_full, (0, me * 128), (3, 128))
    dmod_cols = lax.dynamic_slice(dmod_all, (0, me * ada_cols), (8 * nb, ada_cols))
    g_wada = ada_bwd(c_all, dmod_cols)

    g_win = _unpad_win(g_win_p)
    g_wuq = g_wuq_p.reshape(QL, H, DQK)[:, :, :192].reshape(QL, H * 192)
    rs_a = ("c", "y", "x")
    rs_b = ("c", "x", "y")
    flat = lambda s: s.reshape(2, 2, 2, -1, 128)
    rest_a = jnp.concatenate([flat(_to_slots(g_wco, rs_a, False)), flat(_to_slots(g_wmo, rs_a, False))], axis=3)
    rest_b = jnp.concatenate([flat(_to_slots(g_wout, rs_b, False)), flat(_to_slots(g_wuq, rs_b, True)),
                              flat(_to_slots(g_wukv, rs_b, True))], axis=3)
    gs = [_to_slots(g_win[:half], rs_a, True), rest_a, _to_slots(g_win[half:], rs_b, True), rest_b]
    ords = [rs_a, rs_a, rs_b, rs_b]
    sel1 = [jnp.stack([co[o[0]], co[o[1]]]).astype(jnp.int32) for o in ords]
    sel2 = [jnp.stack([co[o[2]]]).astype(jnp.int32) for o in ords]
    r1 = exchange(gs, [o[0] for o in ords], True, "rs_exchange_c")
    keep1, send1 = zip(*[rs_add_first(gs[a], r1[a], sel1[a], "rs_add_first_%d" % a) for a in range(4)])
    r2 = exchange(list(send1), [o[1] for o in ords], False, "rs_exchange_first_ici")
    keep2, send2 = zip(*[rs_add_second(keep1[a], r2[a], sel2[a], "rs_add_second_%d" % a) for a in range(4)])
    r3 = exchange(list(send2), [o[2] for o in ords], False, "rs_exchange_second_ici")

    gk_win = jnp.concatenate([keep2[0], keep2[2]], axis=0)
    gr_win = jnp.concatenate([r3[0], r3[2]], axis=0)
    n_sq = D * 128 // 128
    unflat = lambda a, lo, shape: a[lo:lo + shape[0] * shape[1] // 128].reshape(shape)
    sq = (128, D)
    uq_s = (QL, 192)
    ukv_s = (KVL, 256)
    parts = {
        "w_conv_out": (unflat(keep2[1], 0, sq), unflat(r3[1], 0, sq)),
        "w_mla_out": (unflat(keep2[1], n_sq, sq), unflat(r3[1], n_sq, sq)),
        "w_out": (unflat(keep2[3], 0, sq), unflat(r3[3], 0, sq)),
        "w_uq": (unflat(keep2[3], n_sq, uq_s), unflat(r3[3], n_sq, uq_s)),
        "w_ukv": (unflat(keep2[3], n_sq + QL * 192 // 128, ukv_s), unflat(r3[3], n_sq + QL * 192 // 128, ukv_s)),
        "w_in": (gk_win, gr_win),
        "w_ada": (g_wada, None),
    }

    weights = {"w_ada": (w_ada, m_w_ada, v_w_ada), "w_in": (w_in, m_w_in, v_w_in),
               "w_conv_out": (w_conv_out, m_w_conv_out, v_w_conv_out), "w_uq": (w_uq, m_w_uq, v_w_uq),
               "w_ukv": (w_ukv, m_w_ukv, v_w_ukv), "w_mla_out": (w_mla_out, m_w_mla_out, v_w_mla_out),
               "w_out": (w_out, m_w_out, v_w_out)}
    res = {}
    for nm, (wv, mv, vv) in weights.items():
        ga, gb = parts[nm]
        outs = adamw(wv[0], mv[0], vv[0], ga, gb, "adamw_" + nm)
        res[nm] = [o_[None] for o_ in outs]

    def pack(b_, gp_, gpo_, gq_, gkv_, cw_):
        return jnp.concatenate([_rows128(b_, 24), _rows128(gp_, 8), _rows128(gpo_, 8), _rows128(gq_, 8),
                                _rows128(gkv_, 8), _rows128(cw_, 8)], axis=0)

    sw = pack(b_ada, g_pre, g_post, g_q, g_kv, conv_w)
    sm = pack(m_b_ada, m_g_pre, m_g_post, m_g_q, m_g_kv, m_conv_w)
    sv = pack(v_b_ada, v_g_pre, v_g_post, v_g_q, v_g_kv, v_conv_w)
    sg = pack(g_bada, g_gpre, g_gpost, g_gq, g_gkv, g_conv)
    small_out = adamw(sw, sm, sv, sg, None, "adamw_small")

    def unpack(a):
        return {"b_ada": a[0:24].reshape(1, 3 * D), "g_pre": a[24:32].reshape(1, D),
                "g_post": a[32:40].reshape(1, D), "g_q": a[40:43].reshape(1, QL),
                "g_kv": a[48:50].reshape(1, KVL), "conv_w": a[56:59].reshape(-1)[:3 * 128].reshape(1, 3, 128)}

    for nm in ("b_ada", "g_pre", "g_post", "g_q", "g_kv", "conv_w"):
        res[nm] = [unpack(a)[nm] for a in small_out]

    order = ["w_ada", "b_ada", "g_pre", "w_in", "conv_w", "w_conv_out", "g_q", "w_uq", "g_kv", "w_ukv",
             "w_mla_out", "w_out", "g_post"]
    out = [loss, grad_x2.reshape(nb, seq, D)]
    for k_ in range(4):
        out += [res[nm][k_] for nm in order]
    return tuple(out)
```

```python
import functools

import numpy as np
import jax
import jax.numpy as jnp
from jax import lax
from jax.experimental import pallas as pl
from jax.experimental.pallas import tpu as pltpu

F32 = jnp.float32
BF16 = jnp.bfloat16
MESH = pl.DeviceIdType.MESH

D = 1024
H = 8
QL = 384
KVL = 256
ROPE = 64
HALF = ROPE // 2
DQK = 256
DV = 128
NSEG = 8
NP = NSEG * D
EPS = 1e-6
ROPE_THETA = 10000.0
SM_SCALE = (128 + ROPE) ** -0.5
LOG2E = 1.4426950408889634
LN2 = 0.6931471805599453
FLASH_TQ = 512

SEG_BZ, SEG_GA, SEG_GB, SEG_LAT, SEG_V = 0, 1, 2, 3, 4

ADAM_LR = 0.001
ADAM_B1 = 0.9
ADAM_B2 = 0.999
ADAM_EPS = 1e-08
ADAM_WD = 0.01
ADAM_STEP = 10

VMEM_LIMIT = 56 * 1024 * 1024


def _params(sem=None, vmem=VMEM_LIMIT):
    kw = dict(vmem_limit_bytes=vmem)
    if sem is not None:
        kw["dimension_semantics"] = sem
    return pltpu.CompilerParams(**kw)


def _sig(v):
    return 1.0 / (1.0 + jnp.exp(-v))


def _dot(a, b):
    return jnp.dot(a, b, preferred_element_type=F32)


def _dot_nt(a, b):
    return lax.dot_general(a, b, (((1,), (1,)), ((), ())), preferred_element_type=F32)


def _dot_tn(a, b):
    return lax.dot_general(a, b, (((0,), (0,)), ((), ())), preferred_element_type=F32)


_AXIS_POS = {"x": 0, "y": 1, "c": 2}


def _coords():
    return lax.axis_index("x"), lax.axis_index("y"), lax.axis_index("c")


def _partner(axis):
    p = list(_coords())
    p[_AXIS_POS[axis]] = 1 - p[_AXIS_POS[axis]]
    return tuple(p)


def small_allgather(v, name):
    rows = v.shape[0]

    def body(v_ref, out_ref, send_sems, recv_sems):
        x, y, c = _coords()
        me = 4 * x + 2 * y + c
        out_ref[me] = v_ref[...]
        copies = []
        for k in range(1, 8):
            peer = (1 - x if k & 4 else x, 1 - y if k & 2 else y, 1 - c if k & 1 else c)
            cp = pltpu.make_async_remote_copy(
                src_ref=v_ref, dst_ref=out_ref.at[me],
                send_sem=send_sems.at[k - 1], recv_sem=recv_sems.at[k - 1],
                device_id=peer, device_id_type=MESH)
            cp.start()
            copies.append(cp)
        for cp in copies:
            cp.wait()

    return pl.pallas_call(
        body, name=name,
        out_shape=jax.ShapeDtypeStruct((8, rows, 128), F32),
        in_specs=[pl.BlockSpec(memory_space=pltpu.VMEM)],
        out_specs=pl.BlockSpec(memory_space=pltpu.VMEM),
        scratch_shapes=[pltpu.SemaphoreType.DMA((7,)), pltpu.SemaphoreType.DMA((7,))],
    )(v)


def allgather_big(arrs, orders, name):
    n = len(arrs)

    def body(*refs):
        ins, outs = refs[:n], refs[n:2 * n]
        send_sems, recv_sems, loc_sems = refs[2 * n:]
        x, y, c = _coords()
        co = {"x": x, "y": y, "c": c}

        def rcopy(a, stage, src, dst, axis):
            return pltpu.make_async_remote_copy(
                src_ref=src, dst_ref=dst,
                send_sem=send_sems.at[a, stage], recv_sem=recv_sems.at[a, stage],
                device_id=_partner(axis), device_id_type=MESH)

        local, first, second, third = [], [], [], []
        for a in range(n):
            a1, a2, a3 = orders[a]
            slot = 4 * co[a3] + 2 * co[a2] + co[a1]
            lc = pltpu.make_async_copy(ins[a], outs[a].at[slot], loc_sems.at[a])
            lc.start()
            local.append(lc)
            cp = rcopy(a, 0, ins[a], outs[a].at[slot], a1)
            cp.start()
            first.append(cp)
        for a in range(n):
            a1, a2, a3 = orders[a]
            first[a].wait_recv()
            local[a].wait()
            pair = outs[a].at[pl.ds(4 * co[a3] + 2 * co[a2], 2)]
            cp = rcopy(a, 1, pair, pair, a2)
            cp.start()
            second.append(cp)
        for a in range(n):
            a1, a2, a3 = orders[a]
            second[a].wait_recv()
            quad = outs[a].at[pl.ds(4 * co[a3], 4)]
            cp = rcopy(a, 2, quad, quad, a3)
            cp.start()
            third.append(cp)
        for a in range(n):
            third[a].wait_recv()
        for a in range(n):
            first[a].wait_send()
            second[a].wait_send()
            third[a].wait_send()

    any_spec = pl.BlockSpec(memory_space=pl.ANY)
    return pl.pallas_call(
        body, name=name,
        out_shape=[jax.ShapeDtypeStruct((8,) + a.shape, a.dtype) for a in arrs],
        in_specs=[any_spec] * n,
        out_specs=[any_spec] * n,
        scratch_shapes=[pltpu.SemaphoreType.DMA((n, 3)), pltpu.SemaphoreType.DMA((n, 3)),
                        pltpu.SemaphoreType.DMA((n,))],
    )(*arrs)


def exchange(arrs, axes, halves, name):
    n = len(arrs)

    def body(*refs):
        ins, outs = refs[:n], refs[n:2 * n]
        send_sems, recv_sems = refs[2 * n:]
        x, y, c = _coords()
        co = {"x": x, "y": y, "c": c}
        copies = []
        for a in range(n):
            src = ins[a].at[1 - co[axes[a]]] if halves else ins[a]
            cp = pltpu.make_async_remote_copy(
                src_ref=src, dst_ref=outs[a],
                send_sem=send_sems.at[a], recv_sem=recv_sems.at[a],
                device_id=_partner(axes[a]), device_id_type=MESH)
            cp.start()
            copies.append(cp)
        for cp in copies:
            cp.wait()

    any_spec = pl.BlockSpec(memory_space=pl.ANY)
    return pl.pallas_call(
        body, name=name,
        out_shape=[jax.ShapeDtypeStruct(a.shape[1:] if halves else a.shape, a.dtype) for a in arrs],
        in_specs=[any_spec] * n,
        out_specs=[any_spec] * n,
        scratch_shapes=[pltpu.SemaphoreType.DMA((n,)), pltpu.SemaphoreType.DMA((n,))],
    )(*arrs)


def rs_add_first(g, r, sel, name):
    _, _, _, rows, cols = g.shape
    tr = rows // 2

    def body(sel_ref, gk_ref, rk_ref, gs_ref, rs_ref, keep_ref, send_ref):
        keep_ref[...] = gk_ref[...] + rk_ref[...]
        send_ref[...] = (gs_ref[...] + rs_ref[...]).astype(BF16)

    blk = (None, None, None, tr, cols)
    rblk = (None, None, tr, cols)
    oblk = (None, tr, cols)
    return pl.pallas_call(
        body, name=name,
        grid_spec=pltpu.PrefetchScalarGridSpec(
            num_scalar_prefetch=1, grid=(2, 2),
            in_specs=[
                pl.BlockSpec(blk, lambda j, i, s: (s[0], s[1], j, i, 0)),
                pl.BlockSpec(rblk, lambda j, i, s: (s[1], j, i, 0)),
                pl.BlockSpec(blk, lambda j, i, s: (s[0], 1 - s[1], j, i, 0)),
                pl.BlockSpec(rblk, lambda j, i, s: (1 - s[1], j, i, 0)),
            ],
            out_specs=[pl.BlockSpec(oblk, lambda j, i, s: (j, i, 0)),
                       pl.BlockSpec(oblk, lambda j, i, s: (j, i, 0))]),
        out_shape=[jax.ShapeDtypeStruct((2, rows, cols), F32),
                   jax.ShapeDtypeStruct((2, rows, cols), BF16)],
        compiler_params=_params(),
    )(sel, g, r, g, r)


def rs_add_second(k, r, sel, name):
    _, rows, cols = k.shape
    tr = rows // 2

    def body(sel_ref, kk_ref, rk_ref, ks_ref, rs_ref, keep_ref, send_ref):
        keep_ref[...] = kk_ref[...] + rk_ref[...].astype(F32)
        send_ref[...] = (ks_ref[...] + rs_ref[...].astype(F32)).astype(BF16)

    blk = (None, tr, cols)
    oblk = (tr, cols)
    return pl.pallas_call(
        body, name=name,
        grid_spec=pltpu.PrefetchScalarGridSpec(
            num_scalar_prefetch=1, grid=(2,),
            in_specs=[
                pl.BlockSpec(blk, lambda i, s: (s[0], i, 0)),
                pl.BlockSpec(blk, lambda i, s: (s[0], i, 0)),
                pl.BlockSpec(blk, lambda i, s: (1 - s[0], i, 0)),
                pl.BlockSpec(blk, lambda i, s: (1 - s[0], i, 0)),
            ],
            out_specs=[pl.BlockSpec(oblk, lambda i, s: (i, 0)),
                       pl.BlockSpec(oblk, lambda i, s: (i, 0))]),
        out_shape=[jax.ShapeDtypeStruct((rows, cols), F32),
                   jax.ShapeDtypeStruct((rows, cols), BF16)],
        compiler_params=_params(),
    )(sel, k, r, k, r)


def proj_matmul(h, w):
    t = h.shape[0]
    tm = min(1024, t)

    def body(h_ref, w_ref, o_ref):
        o_ref[...] = _dot(h_ref[...], w_ref[...]).astype(BF16)

    return pl.pallas_call(
        body, name="proj_matmul", grid=(NSEG, t // tm),
        in_specs=[pl.BlockSpec((tm, D), lambda j, i: (i, 0)),
                  pl.BlockSpec((D, D), lambda j, i: (0, j))],
        out_specs=pl.BlockSpec((None, tm, D), lambda j, i: (j, i, 0)),
        out_shape=jax.ShapeDtypeStruct((NSEG, t, D), BF16),
        compiler_params=_params(("parallel", "parallel")),
    )(h, w)


def dh_matmul(dproj, w):
    t = dproj.shape[1]
    tm = min(1024, t)

    def body(d_ref, w_ref, o_ref, acc_ref):
        k = pl.program_id(1)

        @pl.when(k == 0)
        def _():
            acc_ref[...] = jnp.zeros_like(acc_ref)

        acc_ref[...] += _dot_nt(d_ref[...], w_ref[...])

        @pl.when(k == NSEG - 1)
        def _():
            o_ref[...] = acc_ref[...]

    return pl.pallas_call(
        body, name="dh_matmul", grid=(t // tm, NSEG),
        in_specs=[pl.BlockSpec((None, tm, D), lambda i, k: (k, i, 0)),
                  pl.BlockSpec((D, D), lambda i, k: (0, k))],
        out_specs=pl.BlockSpec((tm, D), lambda i, k: (i, 0)),
        out_shape=jax.ShapeDtypeStruct((t, D), F32),
        scratch_shapes=[pltpu.VMEM((tm, D), F32)],
        compiler_params=_params(("parallel", "arbitrary")),
    )(dproj, w)


def win_grad_matmul(h, dproj):
    t = h.shape[0]
    tk = min(1024, t)
    nk = t // tk

    def body(h_ref, d_ref, o_ref, acc_ref):
        k = pl.program_id(1)

        @pl.when(k == 0)
        def _():
            acc_ref[...] = jnp.zeros_like(acc_ref)

        acc_ref[...] += _dot_tn(h_ref[...], d_ref[...])

        @pl.when(k == nk - 1)
        def _():
            o_ref[...] = acc_ref[...]

    return pl.pallas_call(
        body, name="win_grad_matmul", grid=(NSEG, nk),
        in_specs=[pl.BlockSpec((tk, D), lambda j, k: (k, 0)),
                  pl.BlockSpec((None, tk, D), lambda j, k: (j, k, 0))],
        out_specs=pl.BlockSpec((D, D), lambda j, k: (0, j)),
        out_shape=jax.ShapeDtypeStruct((D, NP), F32),
        scratch_shapes=[pltpu.VMEM((D, D), F32)],
        compiler_params=_params(("parallel", "arbitrary")),
    )(h, dproj)


def grad_matmul(a, b, name):
    t, m = a.shape
    n = b.shape[1]
    tk = min(1024, t)
    nk = t // tk

    def body(a_ref, b_ref, o_ref, acc_ref):
        k = pl.program_id(0)

        @pl.when(k == 0)
        def _():
            acc_ref[...] = jnp.zeros_like(acc_ref)

        acc_ref[...] += _dot_tn(a_ref[...], b_ref[...])

        @pl.when(k == nk - 1)
        def _():
            o_ref[...] = acc_ref[...]

    return pl.pallas_call(
        body, name=name, grid=(nk,),
        in_specs=[pl.BlockSpec((tk, m), lambda k: (k, 0)),
                  pl.BlockSpec((tk, n), lambda k: (k, 0))],
        out_specs=pl.BlockSpec((m, n), lambda k: (0, 0)),
        out_shape=jax.ShapeDtypeStruct((m, n), F32),
        scratch_shapes=[pltpu.VMEM((m, n), F32)],
        compiler_params=_params(("arbitrary",)),
    )(a, b)


def ada_fwd(c_all, w_ada, b_cols):
    def body(c_ref, w_ref, b_ref, o_ref):
        o_ref[...] = _dot(c_ref[...].astype(BF16), w_ref[...].astype(BF16)) + b_ref[...]

    return pl.pallas_call(
        body, name="ada_fwd",
        out_shape=jax.ShapeDtypeStruct((c_all.shape[0], w_ada.shape[1]), F32),
        compiler_params=_params(),
    )(c_all, w_ada, b_cols)


def ada_bwd(c_all, dmod_cols):
    def body(c_ref, d_ref, o_ref):
        o_ref[...] = _dot_tn(c_ref[...].astype(BF16), d_ref[...].astype(BF16))

    return pl.pallas_call(
        body, name="ada_bwd",
        out_shape=jax.ShapeDtypeStruct((c_all.shape[1], dmod_cols.shape[1]), F32),
        compiler_params=_params(),
    )(c_all, dmod_cols)


def slot_sum(g):
    def body(g_ref, o_ref):
        acc = g_ref[0]
        for s in range(1, 8):
            acc = acc + g_ref[s]
        o_ref[...] = acc

    return pl.pallas_call(
        body, name="slot_sum",
        out_shape=jax.ShapeDtypeStruct(g.shape[1:], F32),
    )(g)


def prenorm_fwd(x2, scale, shift, g_pre, seq):
    t = x2.shape[0]
    tm = min(512, seq)
    tpb = seq // tm

    def body(x_ref, sc_ref, sh_ref, g_ref, h_ref):
        xv = x_ref[...]
        r = lax.rsqrt(jnp.mean(xv * xv, axis=-1, keepdims=True) + EPS)
        hv = (xv * r * g_ref[...]) * (1.0 + sc_ref[...]) + sh_ref[...]
        h_ref[...] = hv.astype(BF16)

    per_batch = pl.BlockSpec((None, 1, D), lambda i: (i // tpb, 0, 0))
    return pl.pallas_call(
        body, name="prenorm_fwd", grid=(t // tm,),
        in_specs=[pl.BlockSpec((tm, D), lambda i: (i, 0)), per_batch, per_batch,
                  pl.BlockSpec((1, D), lambda i: (0, 0))],
        out_specs=pl.BlockSpec((tm, D), lambda i: (i, 0)),
        out_shape=jax.ShapeDtypeStruct((t, D), BF16),
        compiler_params=_params(("parallel",)),
    )(x2, scale, shift, g_pre)


def prenorm_bwd(dh, x2, dout, scale, g_pre, seq):
    t = x2.shape[0]
    nb = t // seq
    tm = min(512, seq)
    tpb = seq // tm

    def body(dh_ref, x_ref, do_ref, sc_ref, g_ref, gx_ref, dsh_ref, dsc_ref, dg_ref):
        i = pl.program_id(0)
        xv = x_ref[...]
        dhv = dh_ref[...]
        g = g_ref[...]
        r = lax.rsqrt(jnp.mean(xv * xv, axis=-1, keepdims=True) + EPS)
        nrm = xv * r
        dxn = dhv * (1.0 + sc_ref[...])
        dn = dxn * g
        dx = r * (dn - nrm * jnp.mean(dn * nrm, axis=-1, keepdims=True))
        gx_ref[...] = dx + do_ref[...]

        @pl.when(i % tpb == 0)
        def _():
            dsh_ref[...] = jnp.zeros_like(dsh_ref)
            dsc_ref[...] = jnp.zeros_like(dsc_ref)

        @pl.when(i == 0)
        def _():
            dg_ref[...] = jnp.zeros_like(dg_ref)

        dsh_ref[...] += jnp.sum(dhv, axis=0, keepdims=True)
        dsc_ref[...] += jnp.sum(dhv * (nrm * g), axis=0, keepdims=True)
        dg_ref[...] += jnp.sum(dxn * nrm, axis=0, keepdims=True)

    row = pl.BlockSpec((tm, D), lambda i: (i, 0))
    per_batch = pl.BlockSpec((None, 1, D), lambda i: (i // tpb, 0, 0))
    vec = pl.BlockSpec((1, D), lambda i: (0, 0))
    return pl.pallas_call(
        body, name="prenorm_bwd", grid=(t // tm,),
        in_specs=[row, row, row, per_batch, vec],
        out_specs=[row, per_batch, per_batch, vec],
        out_shape=[jax.ShapeDtypeStruct((t, D), F32),
                   jax.ShapeDtypeStruct((nb, 1, D), F32),
                   jax.ShapeDtypeStruct((nb, 1, D), F32),
                   jax.ShapeDtypeStruct((1, D), F32)],
        compiler_params=_params(("arbitrary",)),
    )(dh, x2, dout, scale, g_pre)


CONV_TC = 128


def _shift_down(u, k, rows):
    idx = lax.broadcasted_iota(jnp.int32, u.shape, 0)
    return jnp.where(idx >= k, pltpu.roll(u, k, 0), 0.0)


def _shift_up(u, k, rows):
    idx = lax.broadcasted_iota(jnp.int32, u.shape, 0)
    return jnp.where(idx < rows - k, pltpu.roll(u, rows - k, 0), 0.0)


def conv_fwd(proj, conv_w, seq):
    t = proj.shape[1]
    nb = t // seq

    def body(p_ref, w_ref, y_ref):
        av = p_ref[0].astype(F32)
        ab = p_ref[1].astype(F32)
        ac = p_ref[2].astype(F32)
        az = p_ref[3].astype(F32)
        w = w_ref[...]
        u = ac * av
        y1 = _shift_down(u, 2, seq) * w[0:1] + _shift_down(u, 1, seq) * w[1:2] + u * w[2:3]
        y_ref[...] = (ab * y1 * (az * _sig(az))).astype(BF16)

    return pl.pallas_call(
        body, name="conv_fwd", grid=(nb, D // CONV_TC),
        in_specs=[pl.BlockSpec((4, seq, CONV_TC), lambda b, ci: (1, b, ci)),
                  pl.BlockSpec((8, CONV_TC), lambda b, ci: (0, ci))],
        out_specs=pl.BlockSpec((seq, CONV_TC), lambda b, ci: (b, ci)),
        out_shape=jax.ShapeDtypeStruct((t, D), BF16),
        compiler_params=_params(("parallel", "parallel")),
    )(proj, conv_w)


def conv_bwd(dproj, proj, dy, conv_w, seq):
    t = proj.shape[1]
    nb = t // seq

    def body(dp_in_ref, p_ref, dy_ref, w_ref, dp_ref, dw_ref):
        b = pl.program_id(1)
        av = p_ref[0].astype(F32)
        ab = p_ref[1].astype(F32)
        ac = p_ref[2].astype(F32)
        az = p_ref[3].astype(F32)
        dyv = dy_ref[...].astype(F32)
        w = w_ref[...]
        u = ac * av
        u1 = _shift_down(u, 1, seq)
        u2 = _shift_down(u, 2, seq)
        y1 = u2 * w[0:1] + u1 * w[1:2] + u * w[2:3]
        sz = _sig(az)
        silu = az * sz
        dy1 = dyv * ab * silu
        du = dy1 * w[2:3] + _shift_up(dy1, 1, seq) * w[1:2] + _shift_up(dy1, 2, seq) * w[0:1]
        dp_ref[0] = (du * ac).astype(BF16)
        dp_ref[1] = (dyv * y1 * silu).astype(BF16)
        dp_ref[2] = (du * av).astype(BF16)
        dp_ref[3] = (dyv * ab * y1 * (sz * (1.0 + az * (1.0 - sz)))).astype(BF16)

        @pl.when(b == 0)
        def _():
            dw_ref[...] = jnp.zeros_like(dw_ref)

        dw_ref[0:1, :] += jnp.sum(dy1 * u2, axis=0, keepdims=True)
        dw_ref[1:2, :] += jnp.sum(dy1 * u1, axis=0, keepdims=True)
        dw_ref[2:3, :] += jnp.sum(dy1 * u, axis=0, keepdims=True)

    return pl.pallas_call(
        body, name="conv_bwd", grid=(D // CONV_TC, nb),
        in_specs=[pl.BlockSpec(memory_space=pl.ANY),
                  pl.BlockSpec((4, seq, CONV_TC), lambda ci, b: (1, b, ci)),
                  pl.BlockSpec((seq, CONV_TC), lambda ci, b: (b, ci)),
                  pl.BlockSpec((8, CONV_TC), lambda ci, b: (0, ci))],
        out_specs=[pl.BlockSpec((4, seq, CONV_TC), lambda ci, b: (1, b, ci)),
                   pl.BlockSpec((8, CONV_TC), lambda ci, b: (0, ci))],
        out_shape=[jax.ShapeDtypeStruct(dproj.shape, BF16),
                   jax.ShapeDtypeStruct((8, D), F32)],
        input_output_aliases={0: 0},
        compiler_params=_params(("parallel", "arbitrary")),
    )(dproj, proj, dy, conv_w)


def _rope_tables(pos_ref, invf_ref, ma_ref, mb_ref):
    ang = pos_ref[...].astype(F32) * invf_ref[...]
    cs = jnp.cos(ang)
    sn = jnp.sin(ang)
    return cs, sn * ma_ref[...], sn * mb_ref[...]


def _head_tables(cs, sa, sb):
    one = jnp.ones_like(cs)
    zero = jnp.zeros_like(cs)
    return (jnp.tile(jnp.concatenate([one, cs], axis=1), (1, H)),
            jnp.tile(jnp.concatenate([zero, sa], axis=1), (1, H)),
            jnp.tile(jnp.concatenate([zero, sb], axis=1), (1, H)))


def _rotate(v, cs, sa, sb, sign):
    width = v.shape[1]
    return v * cs + sign * (pltpu.roll(v, width - HALF, 1) * sa + pltpu.roll(v, HALF, 1) * sb)


MLA_TM = 256


def mla_prep_fwd(proj, pos, g_q, g_kv, wuq, wukv, tabs):
    t = proj.shape[1]
    tm = min(MLA_TM, t)

    def body(lat_ref, pos_ref, gq_ref, gkv_ref, wuq_ref, wukv_ref, invf_ref, ma_ref, mb_ref,
             q_ref, k_ref, kv_ref, qn_ref, kvn_ref):
        lat = lat_ref[...].astype(F32)
        ql = lat[:, :QL]
        kl = lat[:, QL:QL + KVL]
        kr = lat[:, QL + KVL:QL + KVL + 128]
        qn = (ql * lax.rsqrt(jnp.mean(ql * ql, axis=-1, keepdims=True) + EPS) * gq_ref[...]).astype(BF16)
        kvn = (kl * lax.rsqrt(jnp.mean(kl * kl, axis=-1, keepdims=True) + EPS) * gkv_ref[...]).astype(BF16)
        qn_ref[...] = qn
        kvn_ref[...] = kvn
        cs, sa, sb = _rope_tables(pos_ref, invf_ref, ma_ref, mb_ref)
        hc, ha, hb = _head_tables(cs, sa, sb)
        q = _dot(qn, wuq_ref[...])
        q_ref[...] = (_rotate(q, hc, ha, hb, 1.0) * (SM_SCALE * LOG2E)).astype(BF16)
        kv = _dot(kvn, wukv_ref[...]).astype(BF16)
        kv_ref[...] = kv
        kpe = _rotate(kr, cs, sa, sb, 1.0).astype(BF16)
        for hh in range(H):
            k_ref[:, hh * DQK:hh * DQK + 128] = kv[:, hh * DQK:hh * DQK + 128]
            k_ref[:, hh * DQK + 128:(hh + 1) * DQK] = kpe

    row = lambda w: pl.BlockSpec((tm, w), lambda i: (i, 0))
    const = lambda a: pl.BlockSpec(a.shape, lambda i: (0,) * a.ndim)
    return pl.pallas_call(
        body, name="mla_prep_fwd", grid=(t // tm,),
        in_specs=[pl.BlockSpec((None, tm, D), lambda i: (SEG_LAT, i, 0)), row(1),
                  const(g_q), const(g_kv), const(wuq), const(wukv)] + [const(a) for a in tabs],
        out_specs=[row(H * DQK), row(H * DQK), row(H * DQK), row(QL), row(KVL)],
        out_shape=[jax.ShapeDtypeStruct((t, H * DQK), BF16)] * 3
        + [jax.ShapeDtypeStruct((t, QL), BF16), jax.ShapeDtypeStruct((t, KVL), BF16)],
        compiler_params=_params(("parallel",)),
    )(proj, pos, g_q, g_kv, wuq, wukv, *tabs)


def mla_prep_bwd(dproj, proj, dq_rot, dk, dv, pos, g_q, g_kv, wuq, wukv, tabs):
    t = proj.shape[1]
    tm = min(MLA_TM, t)

    def body(dp_in_ref, lat_ref, dqr_ref, dk_ref, dv_ref, pos_ref, gq_ref, gkv_ref, wuq_ref, wukv_ref,
             invf_ref, ma_ref, mb_ref, dp_ref, dq_ref, dkv_ref, dgq_ref, dgkv_ref):
        i = pl.program_id(0)
        lat = lat_ref[...].astype(F32)
        ql = lat[:, :QL]
        kl = lat[:, QL:QL + KVL]
        rq = lax.rsqrt(jnp.mean(ql * ql, axis=-1, keepdims=True) + EPS)
        rk = lax.rsqrt(jnp.mean(kl * kl, axis=-1, keepdims=True) + EPS)
        nq = ql * rq
        nk = kl * rk
        cs, sa, sb = _rope_tables(pos_ref, invf_ref, ma_ref, mb_ref)
        hc, ha, hb = _head_tables(cs, sa, sb)
        dq = _rotate(dqr_ref[...] * SM_SCALE, hc, ha, hb, -1.0).astype(BF16)
        dq_ref[...] = dq
        dkpe = jnp.zeros((tm, 128), F32)
        for hh in range(H):
            dkv_ref[:, hh * DQK:hh * DQK + 128] = dk_ref[:, hh * DQK:hh * DQK + 128]
            dkv_ref[:, hh * DQK + 128:(hh + 1) * DQK] = dv_ref[:, hh * DV:(hh + 1) * DV]
            dkpe = dkpe + dk_ref[:, hh * DQK + 128:(hh + 1) * DQK].astype(F32)
        lane = lax.broadcasted_iota(jnp.int32, (tm, 128), 1)
        dkr = jnp.where(lane < ROPE, _rotate(dkpe, cs, sa, sb, -1.0), 0.0)
        dqn = _dot_nt(dq, wuq_ref[...])
        dkvn = _dot_nt(dkv_ref[...], wukv_ref[...])
        gq = gq_ref[...]
        gkv = gkv_ref[...]
        dnq = dqn * gq
        dnk = dkvn * gkv
        dql = rq * (dnq - nq * jnp.mean(dnq * nq, axis=-1, keepdims=True))
        dkl = rk * (dnk - nk * jnp.mean(dnk * nk, axis=-1, keepdims=True))
        dp_ref[:, :QL] = dql.astype(BF16)
        dp_ref[:, QL:QL + KVL] = dkl.astype(BF16)
        dp_ref[:, QL + KVL:QL + KVL + 128] = dkr.astype(BF16)
        dp_ref[:, QL + KVL + 128:] = jnp.zeros((tm, D - QL - KVL - 128), BF16)

        @pl.when(i == 0)
        def _():
            dgq_ref[...] = jnp.zeros_like(dgq_ref)
            dgkv_ref[...] = jnp.zeros_like(dgkv_ref)

        dgq_ref[...] += jnp.sum(dqn * nq, axis=0, keepdims=True)
        dgkv_ref[...] += jnp.sum(dkvn * nk, axis=0, keepdims=True)

    row = lambda w: pl.BlockSpec((tm, w), lambda i: (i, 0))
    const = lambda a: pl.BlockSpec(a.shape, lambda i: (0,) * a.ndim)
    seg = pl.BlockSpec((None, tm, D), lambda i: (SEG_LAT, i, 0))
    return pl.pallas_call(
        body, name="mla_prep_bwd", grid=(t // tm,),
        in_specs=[pl.BlockSpec(memory_space=pl.ANY), seg, row(H * DQK), row(H * DQK), row(H * DV), row(1),
                  const(g_q), const(g_kv), const(wuq), const(wukv)] + [const(a) for a in tabs],
        out_specs=[seg, row(H * DQK), row(H * DQK),
                   pl.BlockSpec((1, QL), lambda i: (0, 0)), pl.BlockSpec((1, KVL), lambda i: (0, 0))],
        out_shape=[jax.ShapeDtypeStruct(dproj.shape, BF16),
                   jax.ShapeDtypeStruct((t, H * DQK), BF16), jax.ShapeDtypeStruct((t, H * DQK), BF16),
                   jax.ShapeDtypeStruct((1, QL), F32), jax.ShapeDtypeStruct((1, KVL), F32)],
        input_output_aliases={0: 0},
        compiler_params=_params(("arbitrary",)),
    )(dproj, proj, dq_rot, dk, dv, pos, g_q, g_kv, wuq, wukv, *tabs)


def _causal_mask(s, n):
    row = lax.broadcasted_iota(jnp.int32, (n, n), 0)
    col = lax.broadcasted_iota(jnp.int32, (n, n), 1)
    return jnp.where(col <= row, s, -1e30)


def flash_fwd(q, k, kv, nb, seq):
    t = q.shape[0]
    tq = min(FLASH_TQ, seq)
    nq = seq // tq

    def body(q_ref, k_ref, v_ref, o_ref, lse_ref):
        for qi in range(nq):
            qs = slice(qi * tq, (qi + 1) * tq)
            qv = q_ref[qs, :]
            m = jnp.full((tq, 1), -1e30, F32)
            l = jnp.zeros((tq, 1), F32)
            acc = jnp.zeros((tq, DV), F32)
            for j in range(qi + 1):
                ks = slice(j * tq, (j + 1) * tq)
                s = _dot_nt(qv, k_ref[ks, :])
                if j == qi:
                    s = _causal_mask(s, tq)
                m_new = jnp.maximum(m, jnp.max(s, axis=1, keepdims=True))
                p = jnp.exp2(s - m_new)
                alpha = jnp.exp2(m - m_new)
                l = alpha * l + jnp.sum(p, axis=1, keepdims=True)
                acc = alpha * acc + _dot(p.astype(BF16), v_ref[ks, :])
                m = m_new
            o_ref[qs, :] = (acc / l).astype(BF16)
            lse_ref[qs, :] = jnp.broadcast_to(m + jnp.log(l) * LOG2E, (tq, DV))

    out_blk = pl.BlockSpec((seq, DV), lambda b, h: (b, h))
    return pl.pallas_call(
        body, name="flash_fwd", grid=(nb, H),
        in_specs=[pl.BlockSpec((seq, DQK), lambda b, h: (b, h)),
                  pl.BlockSpec((seq, DQK), lambda b, h: (b, h)),
                  pl.BlockSpec((seq, DV), lambda b, h: (b, 2 * h + 1))],
        out_specs=[out_blk, out_blk],
        out_shape=[jax.ShapeDtypeStruct((t, H * DV), BF16), jax.ShapeDtypeStruct((t, H * DV), F32)],
        compiler_params=_params(("parallel", "parallel")),
    )(q, k, kv)


def flash_bwd(q, k, kv, o, do, lse, nb, seq):
    t = q.shape[0]
    tq = min(FLASH_TQ, seq)
    nq = seq // tq

    def body(q_ref, k_ref, v_ref, o_ref, do_ref, lse_ref, dq_ref, dk_ref, dv_ref):
        delta = []
        for qi in range(nq):
            qs = slice(qi * tq, (qi + 1) * tq)
            delta.append(jnp.sum(do_ref[qs, :].astype(F32) * o_ref[qs, :].astype(F32), axis=1, keepdims=True))
        for ki in range(nq):
            ks = slice(ki * tq, (ki + 1) * tq)
            kb = k_ref[ks, :]
            vb = v_ref[ks, :]
            dk = jnp.zeros((tq, DQK), F32)
            dv = jnp.zeros((tq, DV), F32)
            for qi in range(ki, nq):
                qs = slice(qi * tq, (qi + 1) * tq)
                qv = q_ref[qs, :]
                dov = do_ref[qs, :]
                s = _dot_nt(qv, kb)
                if qi == ki:
                    s = _causal_mask(s, tq)
                p = jnp.exp2(s - lse_ref[qs, :][:, :1])
                dp = _dot_nt(dov, vb)
                dz = (p * (dp - delta[qi])).astype(BF16)
                dv = dv + _dot_tn(p.astype(BF16), dov)
                dk = dk + _dot_tn(dz, qv)
                dqb = _dot(dz, kb)
                if ki == 0:
                    dq_ref[qs, :] = dqb
                else:
                    dq_ref[qs, :] += dqb
            dk_ref[ks, :] = (dk * LN2).astype(BF16)
            dv_ref[ks, :] = dv.astype(BF16)

    full = lambda w, col: pl.BlockSpec((seq, w), col)
    same = lambda b, h: (b, h)
    return pl.pallas_call(
        body, name="flash_bwd", grid=(nb, H),
        in_specs=[full(DQK, same), full(DQK, same), full(DV, lambda b, h: (b, 2 * h + 1)),
                  full(DV, same), full(DV, same), full(DV, same)],
        out_specs=[full(DQK, same), full(DQK, same), full(DV, same)],
        out_shape=[jax.ShapeDtypeStruct((t, H * DQK), F32), jax.ShapeDtypeStruct((t, H * DQK), BF16),
                   jax.ShapeDtypeStruct((t, H * DV), BF16)],
        compiler_params=_params(("parallel", "parallel")),
    )(q, k, kv, o, do, lse)


TAIL_TM = 256


def tail_fwd(y, attn, proj, x2, tgt, gate, g_post, wco, wmo, wout, seq):
    t = y.shape[0]
    nb = t // seq
    tm = min(TAIL_TM, seq)
    tpb = seq // tm

    def body(y_ref, at_ref, p_ref, x_ref, t_ref, gate_ref, gp_ref, wco_ref, wmo_ref, wout_ref,
             o_ref, ya_ref, yb_ref, m_ref, do2_ref, dout_ref, dgate_ref, dgp_ref, loss_ref):
        i = pl.program_id(0)
        bz = p_ref[0].astype(F32)
        ga = p_ref[1].astype(F32)
        gb = p_ref[2].astype(F32)
        ov = (at_ref[...].astype(F32) * (bz * _sig(bz))).astype(BF16)
        o_ref[...] = ov
        ya = _dot(y_ref[...], wco_ref[...])
        yb = _dot(ov, wmo_ref[...])
        ya_ref[...] = ya.astype(BF16)
        yb_ref[...] = yb.astype(BF16)
        mv = (_sig(ga) * ya + _sig(gb) * yb).astype(BF16)
        m_ref[...] = mv
        o2 = _dot(mv, wout_ref[...])
        r = lax.rsqrt(jnp.mean(o2 * o2, axis=-1, keepdims=True) + EPS)
        nrm = o2 * r
        gp = gp_ref[...]
        gate_v = gate_ref[...]
        rn = nrm * gp
        err = x_ref[...] + gate_v * rn - t_ref[...]
        dout = err * (1.0 / D)
        dout_ref[...] = dout
        dn = dout * gate_v * gp
        do2_ref[...] = (r * (dn - nrm * jnp.mean(dn * nrm, axis=-1, keepdims=True))).astype(BF16)

        @pl.when(i % tpb == 0)
        def _():
            dgate_ref[...] = jnp.zeros_like(dgate_ref)

        @pl.when(i == 0)
        def _():
            dgp_ref[...] = jnp.zeros_like(dgp_ref)
            loss_ref[...] = jnp.zeros_like(loss_ref)

        dgate_ref[...] += jnp.sum(dout * rn, axis=0, keepdims=True)
        dgp_ref[...] += jnp.sum(dout * gate_v * nrm, axis=0, keepdims=True)
        loss_ref[...] += 0.5 * jnp.sum(jnp.mean(err * err, axis=-1, keepdims=True), axis=0, keepdims=True)

    row = pl.BlockSpec((tm, D), lambda i: (i, 0))
    per_batch = pl.BlockSpec((None, 1, D), lambda i: (i // tpb, 0, 0))
    vec = pl.BlockSpec((1, D), lambda i: (0, 0))
    wgt = pl.BlockSpec((D, D), lambda i: (0, 0))
    act = jax.ShapeDtypeStruct((t, D), BF16)
    return pl.pallas_call(
        body, name="tail_fwd", grid=(t // tm,),
        in_specs=[row, row, pl.BlockSpec((3, tm, D), lambda i: (0, i, 0)), row, row, per_batch, vec,
                  wgt, wgt, wgt],
        out_specs=[row, row, row, row, row, row, per_batch, vec, pl.BlockSpec((1, 1), lambda i: (0, 0))],
        out_shape=[act, act, act, act, act, jax.ShapeDtypeStruct((t, D), F32),
                   jax.ShapeDtypeStruct((nb, 1, D), F32), jax.ShapeDtypeStruct((1, D), F32),
                   jax.ShapeDtypeStruct((1, 1), F32)],
        compiler_params=_params(("arbitrary",)),
    )(y, attn, proj, x2, tgt, gate, g_post, wco, wmo, wout)


def tail_bwd(do2, proj, ya, yb, attn, wout, wmo, wco):
    t = do2.shape[0]
    tm = min(TAIL_TM, t)

    def body(do2_ref, p_ref, ya_ref, yb_ref, at_ref, wout_ref, wmo_ref, wco_ref,
             dp_ref, dya_ref, dyb_ref, dat_ref, dy_ref):
        bz = p_ref[0].astype(F32)
        ga = p_ref[1].astype(F32)
        gb = p_ref[2].astype(F32)
        dm = _dot_nt(do2_ref[...], wout_ref[...])
        sa = _sig(ga)
        sb = _sig(gb)
        dya = (dm * sa).astype(BF16)
        dyb = (dm * sb).astype(BF16)
        dya_ref[...] = dya
        dyb_ref[...] = dyb
        dp_ref[1] = (dm * ya_ref[...].astype(F32) * (sa * (1.0 - sa))).astype(BF16)
        dp_ref[2] = (dm * yb_ref[...].astype(F32) * (sb * (1.0 - sb))).astype(BF16)
        dov = _dot_nt(dyb, wmo_ref[...])
        sz = _sig(bz)
        dat_ref[...] = (dov * (bz * sz)).astype(BF16)
        dp_ref[0] = (dov * at_ref[...].astype(F32) * (sz * (1.0 + bz * (1.0 - sz)))).astype(BF16)
        dy_ref[...] = _dot_nt(dya, wco_ref[...]).astype(BF16)

    row = pl.BlockSpec((tm, D), lambda i: (i, 0))
    seg3 = pl.BlockSpec((3, tm, D), lambda i: (0, i, 0))
    wgt = pl.BlockSpec((D, D), lambda i: (0, 0))
    act = jax.ShapeDtypeStruct((t, D), BF16)
    return pl.pallas_call(
        body, name="tail_bwd", grid=(t // tm,),
        in_specs=[row, seg3, row, row, row, wgt, wgt, wgt],
        out_specs=[seg3, row, row, row, row],
        out_shape=[jax.ShapeDtypeStruct((NSEG, t, D), BF16), act, act, act, act],
        compiler_params=_params(("parallel",)),
    )(do2, proj, ya, yb, attn, wout, wmo, wco)


def adamw(w, m, v, g, g2, name):
    rows, cols = w.shape
    tr = rows
    for cand in (256, 128, 64, 32, 16, 8):
        if rows % cand == 0 and rows > cand:
            tr = cand
            break
    has2 = g2 is not None

    def body(*refs):
        if has2:
            w_ref, m_ref, v_ref, g_ref, g2_ref, go_ref, d_ref, mo_ref, vo_ref = refs
            grad = g_ref[...] + g2_ref[...].astype(F32)
        else:
            w_ref, m_ref, v_ref, g_ref, go_ref, d_ref, mo_ref, vo_ref = refs
            grad = g_ref[...]
        mn = ADAM_B1 * m_ref[...] + (1.0 - ADAM_B1) * grad
        vn = ADAM_B2 * v_ref[...] + (1.0 - ADAM_B2) * (grad * grad)
        m_hat = mn / (1.0 - ADAM_B1 ** ADAM_STEP)
        v_hat = vn / (1.0 - ADAM_B2 ** ADAM_STEP)
        go_ref[...] = grad
        d_ref[...] = -ADAM_LR * (m_hat / (jnp.sqrt(v_hat) + ADAM_EPS) + ADAM_WD * w_ref[...])
        mo_ref[...] = mn
        vo_ref[...] = vn

    blk = pl.BlockSpec((tr, cols), lambda i: (i, 0))
    ins = [w, m, v, g] + ([g2] if has2 else [])
    return pl.pallas_call(
        body, name=name, grid=(rows // tr,),
        in_specs=[blk] * len(ins), out_specs=[blk] * 4,
        out_shape=[jax.ShapeDtypeStruct((rows, cols), F32)] * 4,
        compiler_params=_params(("parallel",)),
    )(*ins)


_ORD_A = ("x", "y", "c")
_ORD_B = ("y", "x", "c")


def _from_slots(g, order, col_sharded):
    r, cc = g.shape[1:]
    g = g.reshape(2, 2, 2, r, cc)
    names = (order[2], order[1], order[0])
    perm = tuple(names.index(a) for a in ("x", "y", "c"))
    if col_sharded:
        return g.transpose((3,) + perm + (4,)).reshape(r, 8 * cc)
    return g.transpose(perm + (3, 4)).reshape(8 * r, cc)


def _to_slots(full, order, col_sharded):
    if col_sharded:
        r = full.shape[0]
        cc = full.shape[1] // 8
        g = full.reshape(r, 2, 2, 2, cc).transpose(1, 2, 3, 0, 4)
    else:
        r = full.shape[0] // 8
        cc = full.shape[1]
        g = full.reshape(2, 2, 2, r, cc)
    names = ("x", "y", "c")
    perm = tuple(names.index(a) for a in order)
    return g.transpose(perm + (3, 4))


def _pad_win(w):
    v_bcz = w[:, 0:4096]
    lat = w[:, 4096:4800]
    bz_g = w[:, 4800:7872]
    z = jnp.zeros((w.shape[0], D - 704), w.dtype)
    return jnp.concatenate([bz_g, lat, z, v_bcz], axis=1)


def _unpad_win(g):
    return jnp.concatenate([g[:, 4096:8192], g[:, 3072:3072 + 704], g[:, 0:3072]], axis=1)


def _rows128(a, rows):
    flat = a.reshape(-1)
    return jnp.pad(flat, (0, rows * 128 - flat.shape[0])).reshape(rows, 128)


def kernel(x, c, positions, w_ada, b_ada, g_pre, w_in, conv_w, w_conv_out, g_q, w_uq, g_kv, w_ukv, w_mla_out, w_out, g_post, loss_target, m_w_ada, m_b_ada, m_g_pre, m_w_in, m_conv_w, m_w_conv_out, m_g_q, m_w_uq, m_g_kv, m_w_ukv, m_w_mla_out, m_w_out, m_g_post, v_w_ada, v_b_ada, v_g_pre, v_w_in, v_conv_w, v_w_conv_out, v_g_q, v_w_uq, v_g_kv, v_w_ukv, v_w_mla_out, v_w_out, v_g_post):
    nb, seq, _ = x.shape
    t = nb * seq
    mx, my, mc = lax.axis_index("x"), lax.axis_index("y"), lax.axis_index("c")
    me = 4 * mx + 2 * my + mc
    co = {"x": mx, "y": my, "c": mc}

    x2 = x.reshape(t, D)
    tgt2 = loss_target.reshape(t, D)
    pos2 = positions.reshape(t, 1)

    packed = jnp.concatenate([c.reshape(2 * D // 128, 128), _rows128(conv_w[0], 8)], axis=0)
    gath = small_allgather(packed, "gather_cond")
    c_all = gath[:, :16].reshape(8 * nb, D)
    conv_full = gath[:, 16:19].reshape(8, 3, 128).transpose(1, 0, 2).reshape(3, D)
    conv_full8 = jnp.pad(conv_full, ((0, 5), (0, 0)))
    ada_cols = w_ada.shape[2]
    b_cols = lax.dynamic_slice(b_ada, (0, me * ada_cols), (1, ada_cols))
    mod_part = ada_fwd(c_all, w_ada[0], b_cols)
    mod_g = small_allgather(mod_part.reshape(8 * nb * ada_cols // 128, 128), "gather_mod")
    mod_all = mod_g.reshape(8, 8 * nb, ada_cols).transpose(1, 0, 2).reshape(8 * nb, 8 * ada_cols)
    mod = lax.dynamic_slice(mod_all, (me * nb, 0), (nb, 3 * D))
    shift = mod[:, 0:D].reshape(nb, 1, D)
    scale = mod[:, D:2 * D].reshape(nb, 1, D)
    gate = mod[:, 2 * D:3 * D].reshape(nb, 1, D)

    win_b = w_in[0].astype(BF16)
    half = win_b.shape[0] // 2
    shards = [win_b[:half], w_conv_out[0].astype(BF16), w_mla_out[0].astype(BF16),
              win_b[half:], w_out[0].astype(BF16), w_uq[0].astype(BF16), w_ukv[0].astype(BF16)]
    orders = [_ORD_A] * 3 + [_ORD_B] * 4
    gw = allgather_big(shards, orders, "gather_weights")
    win_full = jnp.concatenate([_from_slots(gw[0], _ORD_A, True), _from_slots(gw[3], _ORD_B, True)], axis=0)
    win_p = _pad_win(win_full)
    wco = _from_slots(gw[1], _ORD_A, False)
    wmo = _from_slots(gw[2], _ORD_A, False)
    wout = _from_slots(gw[4], _ORD_B, False)
    wuq_full = _from_slots(gw[5], _ORD_B, True)
    wuq_p = jnp.pad(wuq_full.reshape(QL, H, 192), ((0, 0), (0, 0), (0, DQK - 192))).reshape(QL, H * DQK)
    wukv = _from_slots(gw[6], _ORD_B, True)

    inv_freq = ROPE_THETA ** (-jnp.arange(0, ROPE, 2, dtype=F32) / ROPE)
    invf = jnp.concatenate([inv_freq, inv_freq, jnp.zeros((128 - ROPE,), F32)]).reshape(1, 128)
    lane = np.arange(128)
    tabs = (invf,
            jnp.asarray(np.where(lane < HALF, -1.0, 0.0).reshape(1, 128), F32),
            jnp.asarray(np.where((lane >= HALF) & (lane < ROPE), 1.0, 0.0).reshape(1, 128), F32))

    h = prenorm_fwd(x2, scale, shift, g_pre, seq)
    proj = proj_matmul(h, win_p)
    y = conv_fwd(proj, conv_full8, seq)
    q_rot, k_cat, kv, qn, kvn = mla_prep_fwd(proj, pos2, g_q, g_kv, wuq_p, wukv, tabs)
    attn, lse = flash_fwd(q_rot, k_cat, kv, nb, seq)
    o, ya, yb, m, do2, dout, dgate, dg_post, loss_part = tail_fwd(
        y, attn, proj, x2, tgt2, gate, g_post, wco, wmo, wout, seq)

    dproj, dya, dyb, dattn, dy = tail_bwd(do2, proj, ya, yb, attn, wout, wmo, wco)
    g_wout = grad_matmul(m, do2, "grad_w_out")
    g_wmo = grad_matmul(o, dyb, "grad_w_mla_out")
    g_wco = grad_matmul(y, dya, "grad_w_conv_out")
    dproj, dconv = conv_bwd(dproj, proj, dy, conv_full8, seq)
    dq_rot, dk, dv = flash_bwd(q_rot, k_cat, kv, attn, dattn, lse, nb, seq)
    dproj, dq, dkv, dg_q, dg_kv = mla_prep_bwd(dproj, proj, dq_rot, dk, dv, pos2, g_q, g_kv, wuq_p, wukv, tabs)
    g_wuq_p = grad_matmul(qn, dq, "grad_w_uq")
    g_wukv = grad_matmul(kvn, dkv, "grad_w_ukv")
    g_win_p = win_grad_matmul(h, dproj)
    dh = dh_matmul(dproj, win_p)
    grad_x2, dshift, dscale, dg_pre = prenorm_bwd(dh, x2, dout, scale, g_pre, seq)

    dmod = jnp.concatenate([dshift, dscale, dgate], axis=2).reshape(nb * 3 * D // 128, 128)
    small = jnp.concatenate([
        dmod, _rows128(dg_pre, 8), _rows128(dg_post, 8), _rows128(dg_q, 8), _rows128(dg_kv, 8),
        dconv[0:3].reshape(24, 128), _rows128(loss_part, 8)], axis=0)
    small_g = small_allgather(small, "gather_small_grads")
    sums = slot_sum(small_g)
    dmod_all = small_g[:, 0:48].reshape(8 * nb, 3 * D)
    g_bada = (sums[0:24] + sums[24:48]).reshape(1, 3 * D)
    g_gpre = sums[48:56].reshape(1, D)
    g_gpost = sums[56:64].reshape(1, D)
    g_gq = sums[64:67].reshape(1, QL)
    g_gkv = sums[72:74].reshape(1, KVL)
    g_conv_full = sums[80:104].reshape(3, D)
    loss = sums[104, 0]
    g_conv = lax.dynamic_slice(g_conv_full, (0, me * 128), (3, 128))
    dmod_cols = lax.dynamic_slice(dmod_all, (0, me * ada_cols), (8 * nb, ada_cols))
    g_wada = ada_bwd(c_all, dmod_cols)

    g_win = _unpad_win(g_win_p)
    g_wuq = g_wuq_p.reshape(QL, H, DQK)[:, :, :192].reshape(QL, H * 192)
    rs_a = ("c", "y", "x")
    rs_b = ("c", "x", "y")
    flat = lambda s: s.reshape(2, 2, 2, -1, 128)
    rest_a = jnp.concatenate([flat(_to_slots(g_wco, rs_a, False)), flat(_to_slots(g_wmo, rs_a, False))], axis=3)
    rest_b = jnp.concatenate([flat(_to_slots(g_wout, rs_b, False)), flat(_to_slots(g_wuq, rs_b, True)),
                              flat(_to_slots(g_wukv, rs_b, True))], axis=3)
    gs = [_to_slots(g_win[:half], rs_a, True), rest_a, _to_slots(g_win[half:], rs_b, True), rest_b]
    ords = [rs_a, rs_a, rs_b, rs_b]
    sel1 = [jnp.stack([co[o[0]], co[o[1]]]).astype(jnp.int32) for o in ords]
    sel2 = [jnp.stack([co[o[2]]]).astype(jnp.int32) for o in ords]
    r1 = exchange(gs, [o[0] for o in ords], True, "rs_exchange_c")
    keep1, send1 = zip(*[rs_add_first(gs[a], r1[a], sel1[a], "rs_add_first_%d" % a) for a in range(4)])
    r2 = exchange(list(send1), [o[1] for o in ords], False, "rs_exchange_first_ici")
    keep2, send2 = zip(*[rs_add_second(keep1[a], r2[a], sel2[a], "rs_add_second_%d" % a) for a in range(4)])
    r3 = exchange(list(send2), [o[2] for o in ords], False, "rs_exchange_second_ici")

    gk_win = jnp.concatenate([keep2[0], keep2[2]], axis=0)
    gr_win = jnp.concatenate([r3[0], r3[2]], axis=0)
    n_sq = D * 128 // 128
    unflat = lambda a, lo, shape: a[lo:lo + shape[0] * shape[1] // 128].reshape(shape)
    sq = (128, D)
    uq_s = (QL, 192)
    ukv_s = (KVL, 256)
    parts = {
        "w_conv_out": (unflat(keep2[1], 0, sq), unflat(r3[1], 0, sq)),
        "w_mla_out": (unflat(keep2[1], n_sq, sq), unflat(r3[1], n_sq, sq)),
        "w_out": (unflat(keep2[3], 0, sq), unflat(r3[3], 0, sq)),
        "w_uq": (unflat(keep2[3], n_sq, uq_s), unflat(r3[3], n_sq, uq_s)),
        "w_ukv": (unflat(keep2[3], n_sq + QL * 192 // 128, ukv_s), unflat(r3[3], n_sq + QL * 192 // 128, ukv_s)),
        "w_in": (gk_win, gr_win),
        "w_ada": (g_wada, None),
    }

    weights = {"w_ada": (w_ada, m_w_ada, v_w_ada), "w_in": (w_in, m_w_in, v_w_in),
               "w_conv_out": (w_conv_out, m_w_conv_out, v_w_conv_out), "w_uq": (w_uq, m_w_uq, v_w_uq),
               "w_ukv": (w_ukv, m_w_ukv, v_w_ukv), "w_mla_out": (w_mla_out, m_w_mla_out, v_w_mla_out),
               "w_out": (w_out, m_w_out, v_w_out)}
    res = {}
    for nm, (wv, mv, vv) in weights.items():
        ga, gb = parts[nm]
        outs = adamw(wv[0], mv[0], vv[0], ga, gb, "adamw_" + nm)
        res[nm] = [o_[None] for o_ in outs]

    def pack(b_, gp_, gpo_, gq_, gkv_, cw_):
        return jnp.concatenate([_rows128(b_, 24), _rows128(gp_, 8), _rows128(gpo_, 8), _rows128(gq_, 8),
                                _rows128(gkv_, 8), _rows128(cw_, 8)], axis=0)

    sw = pack(b_ada, g_pre, g_post, g_q, g_kv, conv_w)
    sm = pack(m_b_ada, m_g_pre, m_g_post, m_g_q, m_g_kv, m_conv_w)
    sv = pack(v_b_ada, v_g_pre, v_g_post, v_g_q, v_g_kv, v_conv_w)
    sg = pack(g_bada, g_gpre, g_gpost, g_gq, g_gkv, g_conv)
    small_out = adamw(sw, sm, sv, sg, None, "adamw_small")

    def unpack(a):
        return {"b_ada": a[0:24].reshape(1, 3 * D), "g_pre": a[24:32].reshape(1, D),
                "g_post": a[32:40].reshape(1, D), "g_q": a[40:43].reshape(1, QL),
                "g_kv": a[48:50].reshape(1, KVL), "conv_w": a[56:59].reshape(-1)[:3 * 128].reshape(1, 3, 128)}

    for nm in ("b_ada", "g_pre", "g_post", "g_q", "g_kv", "conv_w"):
        res[nm] = [unpack(a)[nm] for a in small_out]

    order = ["w_ada", "b_ada", "g_pre", "w_in", "conv_w", "w_conv_out", "g_q", "w_uq", "g_kv", "w_ukv",
             "w_mla_out", "w_out", "g_post"]
    out = [loss, grad_x2.reshape(nb, seq, D)]
    for k_ in range(4):
        out += [res[nm][k_] for nm in order]
    return tuple(out)
```

```python
import functools

import numpy as np
import jax
import jax.numpy as jnp
from jax import lax
from jax.experimental import pallas as pl
from jax.experimental.pallas import tpu as pltpu

F32 = jnp.float32
BF16 = jnp.bfloat16
MESH = pl.DeviceIdType.MESH

D = 1024
H = 8
QL = 384
KVL = 256
ROPE = 64
HALF = ROPE // 2
DQK = 256
DV = 128
NSEG = 8
NP = NSEG * D
EPS = 1e-6
ROPE_THETA = 10000.0
SM_SCALE = (128 + ROPE) ** -0.5
LOG2E = 1.4426950408889634
LN2 = 0.6931471805599453
FLASH_TQ = 512

SEG_BZ, SEG_GA, SEG_GB, SEG_LAT, SEG_V = 0, 1, 2, 3, 4

ADAM_LR = 0.001
ADAM_B1 = 0.9
ADAM_B2 = 0.999
ADAM_EPS = 1e-08
ADAM_WD = 0.01
ADAM_STEP = 10

VMEM_LIMIT = 56 * 1024 * 1024


def _params(sem=None, vmem=VMEM_LIMIT):
    kw = dict(vmem_limit_bytes=vmem)
    if sem is not None:
        kw["dimension_semantics"] = sem
    return pltpu.CompilerParams(**kw)


def _sig(v):
    return 1.0 / (1.0 + jnp.exp(-v))


def _dot(a, b):
    return jnp.dot(a, b, preferred_element_type=F32)


def _dot_nt(a, b):
    return lax.dot_general(a, b, (((1,), (1,)), ((), ())), preferred_element_type=F32)


def _dot_tn(a, b):
    return lax.dot_general(a, b, (((0,), (0,)), ((), ())), preferred_element_type=F32)


_AXIS_POS = {"x": 0, "y": 1, "c": 2}


def _coords():
    return lax.axis_index("x"), lax.axis_index("y"), lax.axis_index("c")


def _partner(axis):
    p = list(_coords())
    p[_AXIS_POS[axis]] = 1 - p[_AXIS_POS[axis]]
    return tuple(p)


def small_allgather(v, name):
    rows = v.shape[0]

    def body(v_ref, out_ref, send_sems, recv_sems):
        x, y, c = _coords()
        me = 4 * x + 2 * y + c
        out_ref[me] = v_ref[...]
        copies = []
        for k in range(1, 8):
            peer = (1 - x if k & 4 else x, 1 - y if k & 2 else y, 1 - c if k & 1 else c)
            cp = pltpu.make_async_remote_copy(
                src_ref=v_ref, dst_ref=out_ref.at[me],
                send_sem=send_sems.at[k - 1], recv_sem=recv_sems.at[k - 1],
                device_id=peer, device_id_type=MESH)
            cp.start()
            copies.append(cp)
        for cp in copies:
            cp.wait()

    return pl.pallas_call(
        body, name=name,
        out_shape=jax.ShapeDtypeStruct((8, rows, 128), F32),
        in_specs=[pl.BlockSpec(memory_space=pltpu.VMEM)],
        out_specs=pl.BlockSpec(memory_space=pltpu.VMEM),
        scratch_shapes=[pltpu.SemaphoreType.DMA((7,)), pltpu.SemaphoreType.DMA((7,))],
    )(v)


def allgather_big(arrs, plan, name):
    n = len(arrs)
    m = len(plan)

    def body(*refs):
        ins, outs = refs[:n], refs[n:2 * n]
        send_sems, recv_sems, loc_sems = refs[2 * n:]
        x, y, c = _coords()
        co = {"x": x, "y": y, "c": c}

        def window(ref, lead, cols):
            tail = ref.shape[len(lead):]
            idx = tuple(lead) + (slice(None),) * (len(tail) - 1)
            idx += (slice(None),) if cols is None else (pl.ds(cols[0], cols[1]),)
            return ref.at[idx]

        def held(e, free):
            i, cols, _ = plan[e]
            lead = [slice(None) if ax in free else co[ax] for ax in ("x", "y", "c")]
            return window(outs[i], lead, cols)

        def rcopy(e, stage, src, dst, axis):
            return pltpu.make_async_remote_copy(
                src_ref=src, dst_ref=dst,
                send_sem=send_sems.at[e, stage], recv_sem=recv_sems.at[e, stage],
                device_id=_partner(axis), device_id_type=MESH)

        local, stages = [], [[], [], []]
        for e, (i, cols, order) in enumerate(plan):
            mine = window(ins[i], [], cols)
            lc = pltpu.make_async_copy(mine, held(e, ()), loc_sems.at[e])
            lc.start()
            local.append(lc)
            cp = rcopy(e, 0, mine, held(e, ()), order[0])
            cp.start()
            stages[0].append(cp)
        for s in (1, 2):
            for e, (i, cols, order) in enumerate(plan):
                stages[s - 1][e].wait_recv()
                if s == 1:
                    local[e].wait()
                blk = held(e, order[:s])
                cp = rcopy(e, s, blk, blk, order[s])
                cp.start()
                stages[s].append(cp)
        for e in range(m):
            stages[2][e].wait_recv()
        for e in range(m):
            for s in range(3):
                stages[s][e].wait_send()

    any_spec = pl.BlockSpec(memory_space=pl.ANY)
    return pl.pallas_call(
        body, name=name,
        out_shape=[jax.ShapeDtypeStruct((2, 2, 2) + a.shape, a.dtype) for a in arrs],
        in_specs=[any_spec] * n,
        out_specs=[any_spec] * n,
        scratch_shapes=[pltpu.SemaphoreType.DMA((m, 3)), pltpu.SemaphoreType.DMA((m, 3)),
                        pltpu.SemaphoreType.DMA((m,))],
    )(*arrs)


def exchange(arrs, axes, picks, out_shapes, name):
    n = len(arrs)

    def body(*refs):
        ins, outs = refs[:n], refs[n:2 * n]
        send_sems, recv_sems = refs[2 * n:]
        x, y, c = _coords()
        co = {"x": x, "y": y, "c": c}
        copies = []
        for a in range(n):
            src = ins[a] if picks[a] is None else picks[a](ins[a], co)
            cp = pltpu.make_async_remote_copy(
                src_ref=src, dst_ref=outs[a],
                send_sem=send_sems.at[a], recv_sem=recv_sems.at[a],
                device_id=_partner(axes[a]), device_id_type=MESH)
            cp.start()
            copies.append(cp)
        for cp in copies:
            cp.wait()

    any_spec = pl.BlockSpec(memory_space=pl.ANY)
    return pl.pallas_call(
        body, name=name,
        out_shape=[jax.ShapeDtypeStruct(s, a.dtype) for s, a in zip(out_shapes, arrs)],
        in_specs=[any_spec] * n,
        out_specs=[any_spec] * n,
        scratch_shapes=[pltpu.SemaphoreType.DMA((n,)), pltpu.SemaphoreType.DMA((n,))],
    )(*arrs)


def rs_win_add_first(g, r, sel, next_dim, col, name):
    rows, cols = r.shape[2:]

    def body(sel_ref, gk_ref, rk_ref, gs_ref, rs_ref, keep_ref, send_ref):
        keep_ref[...] = gk_ref[...] + rk_ref[...]
        send_ref[...] = (gs_ref[...] + rs_ref[...]).astype(BF16)

    def g_map(flip):
        def f(j, s):
            nxt = 1 - s[next_dim] if flip else s[next_dim]
            return (nxt, j, s[2], 0, col) if next_dim == 0 else (j, nxt, s[2], 0, col)
        return f

    def r_map(flip):
        def f(j, s):
            nxt = 1 - s[next_dim] if flip else s[next_dim]
            return (nxt, j, 0, 0) if next_dim == 0 else (j, nxt, 0, 0)
        return f

    gblk = (None, None, None, rows, cols)
    rblk = (None, None, rows, cols)
    oblk = (None, rows, cols)
    return pl.pallas_call(
        body, name=name,
        grid_spec=pltpu.PrefetchScalarGridSpec(
            num_scalar_prefetch=1, grid=(2,),
            in_specs=[pl.BlockSpec(gblk, g_map(False)), pl.BlockSpec(rblk, r_map(False)),
                      pl.BlockSpec(gblk, g_map(True)), pl.BlockSpec(rblk, r_map(True))],
            out_specs=[pl.BlockSpec(oblk, lambda j, s: (j, 0, 0)),
                       pl.BlockSpec(oblk, lambda j, s: (j, 0, 0))]),
        out_shape=[jax.ShapeDtypeStruct((2, rows, cols), F32),
                   jax.ShapeDtypeStruct((2, rows, cols), BF16)],
        compiler_params=_params(),
    )(sel, g, r, g, r)


def rs_add_first(g, r, sel, name):
    _, _, _, rows, cols = g.shape
    tr = rows // 2

    def body(sel_ref, gk_ref, rk_ref, gs_ref, rs_ref, keep_ref, send_ref):
        keep_ref[...] = gk_ref[...] + rk_ref[...]
        send_ref[...] = (gs_ref[...] + rs_ref[...]).astype(BF16)

    blk = (None, None, None, tr, cols)
    rblk = (None, None, tr, cols)
    oblk = (None, tr, cols)
    return pl.pallas_call(
        body, name=name,
        grid_spec=pltpu.PrefetchScalarGridSpec(
            num_scalar_prefetch=1, grid=(2, 2),
            in_specs=[
                pl.BlockSpec(blk, lambda j, i, s: (s[0], s[1], j, i, 0)),
                pl.BlockSpec(rblk, lambda j, i, s: (s[1], j, i, 0)),
                pl.BlockSpec(blk, lambda j, i, s: (s[0], 1 - s[1], j, i, 0)),
                pl.BlockSpec(rblk, lambda j, i, s: (1 - s[1], j, i, 0)),
            ],
            out_specs=[pl.BlockSpec(oblk, lambda j, i, s: (j, i, 0)),
                       pl.BlockSpec(oblk, lambda j, i, s: (j, i, 0))]),
        out_shape=[jax.ShapeDtypeStruct((2, rows, cols), F32),
                   jax.ShapeDtypeStruct((2, rows, cols), BF16)],
        compiler_params=_params(),
    )(sel, g, r, g, r)


def rs_add_second(k, r, sel, name):
    _, rows, cols = k.shape
    tr = rows // 2 if rows % 32 == 0 else rows
    nt = rows // tr

    def body(sel_ref, kk_ref, rk_ref, ks_ref, rs_ref, keep_ref, send_ref):
        keep_ref[...] = kk_ref[...] + rk_ref[...].astype(F32)
        send_ref[...] = (ks_ref[...] + rs_ref[...].astype(F32)).astype(BF16)

    blk = (None, tr, cols)
    oblk = (tr, cols)
    return pl.pallas_call(
        body, name=name,
        grid_spec=pltpu.PrefetchScalarGridSpec(
            num_scalar_prefetch=1, grid=(nt,),
            in_specs=[
                pl.BlockSpec(blk, lambda i, s: (s[0], i, 0)),
                pl.BlockSpec(blk, lambda i, s: (s[0], i, 0)),
                pl.BlockSpec(blk, lambda i, s: (1 - s[0], i, 0)),
                pl.BlockSpec(blk, lambda i, s: (1 - s[0], i, 0)),
            ],
            out_specs=[pl.BlockSpec(oblk, lambda i, s: (i, 0)),
                       pl.BlockSpec(oblk, lambda i, s: (i, 0))]),
        out_shape=[jax.ShapeDtypeStruct((rows, cols), F32),
                   jax.ShapeDtypeStruct((rows, cols), BF16)],
        compiler_params=_params(),
    )(sel, k, r, k, r)


SEG_ROWS = (4800, 5824, 6848, 4096, 0, 1024, 2048, 3072)
LAT_ROWS = QL + KVL + ROPE
N_IN = 7872


def unpack_win(wt_bits):
    def body(w_hbm, o_ref, buf, sem):
        j = pl.program_id(0)
        off = jnp.where(j < 3, 4800 + 1024 * j, jnp.where(j == 3, 4096, (j - 4) * 1024))
        cp = pltpu.make_async_copy(w_hbm.at[pl.ds(pl.multiple_of(off, 8), D)], buf, sem)
        cp.start()
        cp.wait()
        bits = pltpu.bitcast(buf[...], jnp.uint32)
        row = lax.broadcasted_iota(jnp.int32, (D, D // 2), 0)
        live = jnp.logical_or(j != SEG_LAT, row < LAT_ROWS)
        lo = pltpu.bitcast(bits << 16, F32)
        hi = pltpu.bitcast(bits & jnp.uint32(0xFFFF0000), F32)
        o_ref[:, :D // 2] = jnp.where(live, lo, 0.0).astype(BF16)
        o_ref[:, D // 2:] = jnp.where(live, hi, 0.0).astype(BF16)

    return pl.pallas_call(
        body, name="unpack_win", grid=(NSEG,),
        in_specs=[pl.BlockSpec(memory_space=pl.ANY)],
        out_specs=pl.BlockSpec((D, D), lambda j: (j, 0)),
        out_shape=jax.ShapeDtypeStruct((NP, D), BF16),
        scratch_shapes=[pltpu.VMEM((D, D // 2), F32), pltpu.SemaphoreType.DMA],
        compiler_params=_params(("arbitrary",)),
    )(wt_bits)


def unpad_win_grad(g):
    def body(g_hbm, o_hbm, sems):
        copies = []
        for j in range(NSEG):
            rows = LAT_ROWS if j == SEG_LAT else D
            cp = pltpu.make_async_copy(g_hbm.at[pl.ds(j * D, rows)], o_hbm.at[pl.ds(SEG_ROWS[j], rows)],
                                       sems.at[j])
            cp.start()
            copies.append(cp)
        for cp in copies:
            cp.wait()

    return pl.pallas_call(
        body, name="unpad_win_grad",
        in_specs=[pl.BlockSpec(memory_space=pl.ANY)],
        out_specs=pl.BlockSpec(memory_space=pl.ANY),
        out_shape=jax.ShapeDtypeStruct((N_IN, D), F32),
        scratch_shapes=[pltpu.SemaphoreType.DMA((NSEG,))],
    )(g)


def proj_matmul(h, wt):
    t = h.shape[0]
    tm = min(1024, t)

    def body(h_ref, w_ref, o_ref):
        o_ref[...] = _dot_nt(h_ref[...], w_ref[...]).astype(BF16)

    return pl.pallas_call(
        body, name="proj_matmul", grid=(NSEG, t // tm),
        in_specs=[pl.BlockSpec((tm, D), lambda j, i: (i, 0)),
                  pl.BlockSpec((D, D), lambda j, i: (j, 0))],
        out_specs=pl.BlockSpec((None, tm, D), lambda j, i: (j, i, 0)),
        out_shape=jax.ShapeDtypeStruct((NSEG, t, D), BF16),
        compiler_params=_params(("parallel", "parallel")),
    )(h, wt)


def dh_matmul(dproj, wt):
    t = dproj.shape[1]
    tm = min(1024, t)

    def body(d_ref, w_ref, o_ref, acc_ref):
        k = pl.program_id(1)

        @pl.when(k == 0)
        def _():
            acc_ref[...] = jnp.zeros_like(acc_ref)

        acc_ref[...] += _dot(d_ref[...], w_ref[...])

        @pl.when(k == NSEG - 1)
        def _():
            o_ref[...] = acc_ref[...]

    return pl.pallas_call(
        body, name="dh_matmul", grid=(t // tm, NSEG),
        in_specs=[pl.BlockSpec((None, tm, D), lambda i, k: (k, i, 0)),
                  pl.BlockSpec((D, D), lambda i, k: (k, 0))],
        out_specs=pl.BlockSpec((tm, D), lambda i, k: (i, 0)),
        out_shape=jax.ShapeDtypeStruct((t, D), F32),
        scratch_shapes=[pltpu.VMEM((tm, D), F32)],
        compiler_params=_params(("parallel", "arbitrary")),
    )(dproj, wt)


def win_grad_matmul(h, dproj):
    t = h.shape[0]
    tk = min(1024, t)
    nk = t // tk

    def body(h_ref, d_ref, o_ref, acc_ref):
        k = pl.program_id(1)

        @pl.when(k == 0)
        def _():
            acc_ref[...] = jnp.zeros_like(acc_ref)

        acc_ref[...] += _dot_tn(d_ref[...], h_ref[...])

        @pl.when(k == nk - 1)
        def _():
            o_ref[...] = acc_ref[...]

    return pl.pallas_call(
        body, name="win_grad_matmul", grid=(NSEG, nk),
        in_specs=[pl.BlockSpec((tk, D), lambda j, k: (k, 0)),
                  pl.BlockSpec((None, tk, D), lambda j, k: (j, k, 0))],
        out_specs=pl.BlockSpec((D, D), lambda j, k: (j, 0)),
        out_shape=jax.ShapeDtypeStruct((NP, D), F32),
        scratch_shapes=[pltpu.VMEM((D, D), F32)],
        compiler_params=_params(("parallel", "arbitrary")),
    )(h, dproj)


def grad_matmul(a, b, name):
    t, m = a.shape
    n = b.shape[1]
    tk = min(1024, t)
    nk = t // tk

    def body(a_ref, b_ref, o_ref, acc_ref):
        k = pl.program_id(0)

        @pl.when(k == 0)
        def _():
            acc_ref[...] = jnp.zeros_like(acc_ref)

        acc_ref[...] += _dot_tn(a_ref[...], b_ref[...])

        @pl.when(k == nk - 1)
        def _():
            o_ref[...] = acc_ref[...]

    return pl.pallas_call(
        body, name=name, grid=(nk,),
        in_specs=[pl.BlockSpec((tk, m), lambda k: (k, 0)),
                  pl.BlockSpec((tk, n), lambda k: (k, 0))],
        out_specs=pl.BlockSpec((m, n), lambda k: (0, 0)),
        out_shape=jax.ShapeDtypeStruct((m, n), F32),
        scratch_shapes=[pltpu.VMEM((m, n), F32)],
        compiler_params=_params(("arbitrary",)),
    )(a, b)


def ada_fwd(c_all, w_ada, b_cols):
    def body(c_ref, w_ref, b_ref, o_ref):
        o_ref[...] = _dot(c_ref[...].astype(BF16), w_ref[...].astype(BF16)) + b_ref[...]

    return pl.pallas_call(
        body, name="ada_fwd",
        out_shape=jax.ShapeDtypeStruct((c_all.shape[0], w_ada.shape[1]), F32),
        compiler_params=_params(),
    )(c_all, w_ada, b_cols)


def ada_bwd(c_all, dmod_cols):
    def body(c_ref, d_ref, o_ref):
        o_ref[...] = _dot_tn(c_ref[...].astype(BF16), d_ref[...].astype(BF16))

    return pl.pallas_call(
        body, name="ada_bwd",
        out_shape=jax.ShapeDtypeStruct((c_all.shape[1], dmod_cols.shape[1]), F32),
        compiler_params=_params(),
    )(c_all, dmod_cols)


def slot_sum(g):
    def body(g_ref, o_ref):
        acc = g_ref[0]
        for s in range(1, 8):
            acc = acc + g_ref[s]
        o_ref[...] = acc

    return pl.pallas_call(
        body, name="slot_sum",
        out_shape=jax.ShapeDtypeStruct(g.shape[1:], F32),
    )(g)


def prenorm_fwd(x2, scale, shift, g_pre, seq):
    t = x2.shape[0]
    tm = min(512, seq)
    tpb = seq // tm

    def body(x_ref, sc_ref, sh_ref, g_ref, h_ref):
        xv = x_ref[...]
        r = lax.rsqrt(jnp.mean(xv * xv, axis=-1, keepdims=True) + EPS)
        hv = (xv * r * g_ref[...]) * (1.0 + sc_ref[...]) + sh_ref[...]
        h_ref[...] = hv.astype(BF16)

    per_batch = pl.BlockSpec((None, 1, D), lambda i: (i // tpb, 0, 0))
    return pl.pallas_call(
        body, name="prenorm_fwd", grid=(t // tm,),
        in_specs=[pl.BlockSpec((tm, D), lambda i: (i, 0)), per_batch, per_batch,
                  pl.BlockSpec((1, D), lambda i: (0, 0))],
        out_specs=pl.BlockSpec((tm, D), lambda i: (i, 0)),
        out_shape=jax.ShapeDtypeStruct((t, D), BF16),
        compiler_params=_params(("parallel",)),
    )(x2, scale, shift, g_pre)


def prenorm_bwd(dh, x2, dout, scale, g_pre, seq):
    t = x2.shape[0]
    nb = t // seq
    tm = min(512, seq)
    tpb = seq // tm

    def body(dh_ref, x_ref, do_ref, sc_ref, g_ref, gx_ref, dsh_ref, dsc_ref, dg_ref):
        i = pl.program_id(0)
        xv = x_ref[...]
        dhv = dh_ref[...]
        g = g_ref[...]
        r = lax.rsqrt(jnp.mean(xv * xv, axis=-1, keepdims=True) + EPS)
        nrm = xv * r
        dxn = dhv * (1.0 + sc_ref[...])
        dn = dxn * g
        dx = r * (dn - nrm * jnp.mean(dn * nrm, axis=-1, keepdims=True))
        gx_ref[...] = dx + do_ref[...]

        @pl.when(i % tpb == 0)
        def _():
            dsh_ref[...] = jnp.zeros_like(dsh_ref)
            dsc_ref[...] = jnp.zeros_like(dsc_ref)

        @pl.when(i == 0)
        def _():
            dg_ref[...] = jnp.zeros_like(dg_ref)

        dsh_ref[...] += jnp.sum(dhv, axis=0, keepdims=True)
        dsc_ref[...] += jnp.sum(dhv * (nrm * g), axis=0, keepdims=True)
        dg_ref[...] += jnp.sum(dxn * nrm, axis=0, keepdims=True)

    row = pl.BlockSpec((tm, D), lambda i: (i, 0))
    per_batch = pl.BlockSpec((None, 1, D), lambda i: (i // tpb, 0, 0))
    vec = pl.BlockSpec((1, D), lambda i: (0, 0))
    return pl.pallas_call(
        body, name="prenorm_bwd", grid=(t // tm,),
        in_specs=[row, row, row, per_batch, vec],
        out_specs=[row, per_batch, per_batch, vec],
        out_shape=[jax.ShapeDtypeStruct((t, D), F32),
                   jax.ShapeDtypeStruct((nb, 1, D), F32),
                   jax.ShapeDtypeStruct((nb, 1, D), F32),
                   jax.ShapeDtypeStruct((1, D), F32)],
        compiler_params=_params(("arbitrary",)),
    )(dh, x2, dout, scale, g_pre)


CONV_TC = 128


def _shift_down(u, k, rows):
    idx = lax.broadcasted_iota(jnp.int32, u.shape, 0)
    return jnp.where(idx >= k, pltpu.roll(u, k, 0), 0.0)


def _shift_up(u, k, rows):
    idx = lax.broadcasted_iota(jnp.int32, u.shape, 0)
    return jnp.where(idx < rows - k, pltpu.roll(u, rows - k, 0), 0.0)


def conv_fwd(proj, conv_w, seq):
    t = proj.shape[1]
    nb = t // seq

    def body(p_ref, w_ref, y_ref):
        av = p_ref[0].astype(F32)
        ab = p_ref[1].astype(F32)
        ac = p_ref[2].astype(F32)
        az = p_ref[3].astype(F32)
        w = w_ref[...]
        u = ac * av
        y1 = _shift_down(u, 2, seq) * w[0:1] + _shift_down(u, 1, seq) * w[1:2] + u * w[2:3]
        y_ref[...] = (ab * y1 * (az * _sig(az))).astype(BF16)

    return pl.pallas_call(
        body, name="conv_fwd", grid=(nb, D // CONV_TC),
        in_specs=[pl.BlockSpec((4, seq, CONV_TC), lambda b, ci: (1, b, ci)),
                  pl.BlockSpec((8, CONV_TC), lambda b, ci: (0, ci))],
        out_specs=pl.BlockSpec((seq, CONV_TC), lambda b, ci: (b, ci)),
        out_shape=jax.ShapeDtypeStruct((t, D), BF16),
        compiler_params=_params(("parallel", "parallel")),
    )(proj, conv_w)


def conv_bwd(dproj, proj, dy, conv_w, seq):
    t = proj.shape[1]
    nb = t // seq

    def body(dp_in_ref, p_ref, dy_ref, w_ref, dp_ref, dw_ref):
        b = pl.program_id(1)
        av = p_ref[0].astype(F32)
        ab = p_ref[1].astype(F32)
        ac = p_ref[2].astype(F32)
        az = p_ref[3].astype(F32)
        dyv = dy_ref[...].astype(F32)
        w = w_ref[...]
        u = ac * av
        u1 = _shift_down(u, 1, seq)
        u2 = _shift_down(u, 2, seq)
        y1 = u2 * w[0:1] + u1 * w[1:2] + u * w[2:3]
        sz = _sig(az)
        silu = az * sz
        dy1 = dyv * ab * silu
        du = dy1 * w[2:3] + _shift_up(dy1, 1, seq) * w[1:2] + _shift_up(dy1, 2, seq) * w[0:1]
        dp_ref[0] = (du * ac).astype(BF16)
        dp_ref[1] = (dyv * y1 * silu).astype(BF16)
        dp_ref[2] = (du * av).astype(BF16)
        dp_ref[3] = (dyv * ab * y1 * (sz * (1.0 + az * (1.0 - sz)))).astype(BF16)

        @pl.when(b == 0)
        def _():
            dw_ref[...] = jnp.zeros_like(dw_ref)

        dw_ref[0:1, :] += jnp.sum(dy1 * u2, axis=0, keepdims=True)
        dw_ref[1:2, :] += jnp.sum(dy1 * u1, axis=0, keepdims=True)
        dw_ref[2:3, :] += jnp.sum(dy1 * u, axis=0, keepdims=True)

    return pl.pallas_call(
        body, name="conv_bwd", grid=(D // CONV_TC, nb),
        in_specs=[pl.BlockSpec(memory_space=pl.ANY),
                  pl.BlockSpec((4, seq, CONV_TC), lambda ci, b: (1, b, ci)),
                  pl.BlockSpec((seq, CONV_TC), lambda ci, b: (b, ci)),
                  pl.BlockSpec((8, CONV_TC), lambda ci, b: (0, ci))],
        out_specs=[pl.BlockSpec((4, seq, CONV_TC), lambda ci, b: (1, b, ci)),
                   pl.BlockSpec((8, CONV_TC), lambda ci, b: (0, ci))],
        out_shape=[jax.ShapeDtypeStruct(dproj.shape, BF16),
                   jax.ShapeDtypeStruct((8, D), F32)],
        input_output_aliases={0: 0},
        compiler_params=_params(("parallel", "arbitrary")),
    )(dproj, proj, dy, conv_w)


def _rope_tables(pos_ref, invf_ref, ma_ref, mb_ref):
    ang = pos_ref[...].astype(F32) * invf_ref[...]
    cs = jnp.cos(ang)
    sn = jnp.sin(ang)
    return cs, sn * ma_ref[...], sn * mb_ref[...]


def _head_tables(cs, sa, sb):
    one = jnp.ones_like(cs)
    zero = jnp.zeros_like(cs)
    return (jnp.tile(jnp.concatenate([one, cs], axis=1), (1, H)),
            jnp.tile(jnp.concatenate([zero, sa], axis=1), (1, H)),
            jnp.tile(jnp.concatenate([zero, sb], axis=1), (1, H)))


def _rotate(v, cs, sa, sb, sign):
    width = v.shape[1]
    return v * cs + sign * (pltpu.roll(v, width - HALF, 1) * sa + pltpu.roll(v, HALF, 1) * sb)


MLA_TM = 256


def mla_prep_fwd(proj, pos, g_q, g_kv, wuq, wukv, tabs):
    t = proj.shape[1]
    tm = min(MLA_TM, t)

    def body(lat_ref, pos_ref, gq_ref, gkv_ref, wuq_ref, wukv_ref, invf_ref, ma_ref, mb_ref,
             q_ref, k_ref, kv_ref, qn_ref, kvn_ref):
        lat = lat_ref[...].astype(F32)
        ql = lat[:, :QL]
        kl = lat[:, QL:QL + KVL]
        kr = lat[:, QL + KVL:QL + KVL + 128]
        qn = (ql * lax.rsqrt(jnp.mean(ql * ql, axis=-1, keepdims=True) + EPS) * gq_ref[...]).astype(BF16)
        kvn = (kl * lax.rsqrt(jnp.mean(kl * kl, axis=-1, keepdims=True) + EPS) * gkv_ref[...]).astype(BF16)
        qn_ref[...] = qn
        kvn_ref[...] = kvn
        cs, sa, sb = _rope_tables(pos_ref, invf_ref, ma_ref, mb_ref)
        hc, ha, hb = _head_tables(cs, sa, sb)
        q = _dot(qn, wuq_ref[...])
        q_ref[...] = (_rotate(q, hc, ha, hb, 1.0) * (SM_SCALE * LOG2E)).astype(BF16)
        kv = _dot(kvn, wukv_ref[...]).astype(BF16)
        kv_ref[...] = kv
        kpe = _rotate(kr, cs, sa, sb, 1.0).astype(BF16)
        for hh in range(H):
            k_ref[:, hh * DQK:hh * DQK + 128] = kv[:, hh * DQK:hh * DQK + 128]
            k_ref[:, hh * DQK + 128:(hh + 1) * DQK] = kpe

    row = lambda w: pl.BlockSpec((tm, w), lambda i: (i, 0))
    const = lambda a: pl.BlockSpec(a.shape, lambda i: (0,) * a.ndim)
    return pl.pallas_call(
        body, name="mla_prep_fwd", grid=(t // tm,),
        in_specs=[pl.BlockSpec((None, tm, D), lambda i: (SEG_LAT, i, 0)), row(1),
                  const(g_q), const(g_kv), const(wuq), const(wukv)] + [const(a) for a in tabs],
        out_specs=[row(H * DQK), row(H * DQK), row(H * DQK), row(QL), row(KVL)],
        out_shape=[jax.ShapeDtypeStruct((t, H * DQK), BF16)] * 3
        + [jax.ShapeDtypeStruct((t, QL), BF16), jax.ShapeDtypeStruct((t, KVL), BF16)],
        compiler_params=_params(("parallel",)),
    )(proj, pos, g_q, g_kv, wuq, wukv, *tabs)


def mla_prep_bwd(dproj, proj, dq_rot, dk, dv, pos, g_q, g_kv, wuq, wukv, tabs):
    t = proj.shape[1]
    tm = min(MLA_TM, t)

    def body(dp_in_ref, lat_ref, dqr_ref, dk_ref, dv_ref, pos_ref, gq_ref, gkv_ref, wuq_ref, wukv_ref,
             invf_ref, ma_ref, mb_ref, dp_ref, dq_ref, dkv_ref, dgq_ref, dgkv_ref):
        i = pl.program_id(0)
        lat = lat_ref[...].astype(F32)
        ql = lat[:, :QL]
        kl = lat[:, QL:QL + KVL]
        rq = lax.rsqrt(jnp.mean(ql * ql, axis=-1, keepdims=True) + EPS)
        rk = lax.rsqrt(jnp.mean(kl * kl, axis=-1, keepdims=True) + EPS)
        nq = ql * rq
        nk = kl * rk
        cs, sa, sb = _rope_tables(pos_ref, invf_ref, ma_ref, mb_ref)
        hc, ha, hb = _head_tables(cs, sa, sb)
        dq = _rotate(dqr_ref[...] * SM_SCALE, hc, ha, hb, -1.0).astype(BF16)
        dq_ref[...] = dq
        dkpe = jnp.zeros((tm, 128), F32)
        for hh in range(H):
            dkv_ref[:, hh * DQK:hh * DQK + 128] = dk_ref[:, hh * DQK:hh * DQK + 128]
            dkv_ref[:, hh * DQK + 128:(hh + 1) * DQK] = dv_ref[:, hh * DV:(hh + 1) * DV]
            dkpe = dkpe + dk_ref[:, hh * DQK + 128:(hh + 1) * DQK].astype(F32)
        lane = lax.broadcasted_iota(jnp.int32, (tm, 128), 1)
        dkr = jnp.where(lane < ROPE, _rotate(dkpe, cs, sa, sb, -1.0), 0.0)
        dqn = _dot_nt(dq, wuq_ref[...])
        dkvn = _dot_nt(dkv_ref[...], wukv_ref[...])
        gq = gq_ref[...]
        gkv = gkv_ref[...]
        dnq = dqn * gq
        dnk = dkvn * gkv
        dql = rq * (dnq - nq * jnp.mean(dnq * nq, axis=-1, keepdims=True))
        dkl = rk * (dnk - nk * jnp.mean(dnk * nk, axis=-1, keepdims=True))
        dp_ref[:, :QL] = dql.astype(BF16)
        dp_ref[:, QL:QL + KVL] = dkl.astype(BF16)
        dp_ref[:, QL + KVL:QL + KVL + 128] = dkr.astype(BF16)
        dp_ref[:, QL + KVL + 128:] = jnp.zeros((tm, D - QL - KVL - 128), BF16)

        @pl.when(i == 0)
        def _():
            dgq_ref[...] = jnp.zeros_like(dgq_ref)
            dgkv_ref[...] = jnp.zeros_like(dgkv_ref)

        dgq_ref[...] += jnp.sum(dqn * nq, axis=0, keepdims=True)
        dgkv_ref[...] += jnp.sum(dkvn * nk, axis=0, keepdims=True)

    row = lambda w: pl.BlockSpec((tm, w), lambda i: (i, 0))
    const = lambda a: pl.BlockSpec(a.shape, lambda i: (0,) * a.ndim)
    seg = pl.BlockSpec((None, tm, D), lambda i: (SEG_LAT, i, 0))
    return pl.pallas_call(
        body, name="mla_prep_bwd", grid=(t // tm,),
        in_specs=[pl.BlockSpec(memory_space=pl.ANY), seg, row(H * DQK), row(H * DQK), row(H * DV), row(1),
                  const(g_q), const(g_kv), const(wuq), const(wukv)] + [const(a) for a in tabs],
        out_specs=[seg, row(H * DQK), row(H * DQK),
                   pl.BlockSpec((1, QL), lambda i: (0, 0)), pl.BlockSpec((1, KVL), lambda i: (0, 0))],
        out_shape=[jax.ShapeDtypeStruct(dproj.shape, BF16),
                   jax.ShapeDtypeStruct((t, H * DQK), BF16), jax.ShapeDtypeStruct((t, H * DQK), BF16),
                   jax.ShapeDtypeStruct((1, QL), F32), jax.ShapeDtypeStruct((1, KVL), F32)],
        input_output_aliases={0: 0},
        compiler_params=_params(("arbitrary",)),
    )(dproj, proj, dq_rot, dk, dv, pos, g_q, g_kv, wuq, wukv, *tabs)


def _causal_mask(s, n):
    row = lax.broadcasted_iota(jnp.int32, (n, n), 0)
    col = lax.broadcasted_iota(jnp.int32, (n, n), 1)
    return jnp.where(col <= row, s, -1e30)


def flash_fwd(q, k, kv, nb, seq):
    t = q.shape[0]
    tq = min(FLASH_TQ, seq)
    nq = seq // tq

    def body(q_ref, k_ref, v_ref, o_ref, lse_ref):
        for qi in range(nq):
            qs = slice(qi * tq, (qi + 1) * tq)
            qv = q_ref[qs, :]
            m = jnp.full((tq, 1), -1e30, F32)
            l = jnp.zeros((tq, 1), F32)
            acc = jnp.zeros((tq, DV), F32)
            for j in range(qi + 1):
                ks = slice(j * tq, (j + 1) * tq)
                s = _dot_nt(qv, k_ref[ks, :])
                if j == qi:
                    s = _causal_mask(s, tq)
                m_new = jnp.maximum(m, jnp.max(s, axis=1, keepdims=True))
                p = jnp.exp2(s - m_new)
                alpha = jnp.exp2(m - m_new)
                l = alpha * l + jnp.sum(p, axis=1, keepdims=True)
                acc = alpha * acc + _dot(p.astype(BF16), v_ref[ks, :])
                m = m_new
            o_ref[qs, :] = (acc / l).astype(BF16)
            lse_ref[qs, :] = jnp.broadcast_to(m + jnp.log(l) * LOG2E, (tq, DV))

    out_blk = pl.BlockSpec((seq, DV), lambda b, h: (b, h))
    return pl.pallas_call(
        body, name="flash_fwd", grid=(nb, H),
        in_specs=[pl.BlockSpec((seq, DQK), lambda b, h: (b, h)),
                  pl.BlockSpec((seq, DQK), lambda b, h: (b, h)),
                  pl.BlockSpec((seq, DV), lambda b, h: (b, 2 * h + 1))],
        out_specs=[out_blk, out_blk],
        out_shape=[jax.ShapeDtypeStruct((t, H * DV), BF16), jax.ShapeDtypeStruct((t, H * DV), F32)],
        compiler_params=_params(("parallel", "parallel")),
    )(q, k, kv)


def flash_bwd(q, k, kv, o, do, lse, nb, seq):
    t = q.shape[0]
    tq = min(FLASH_TQ, seq)
    nq = seq // tq

    def body(q_ref, k_ref, v_ref, o_ref, do_ref, lse_ref, dq_ref, dk_ref, dv_ref):
        delta = []
        for qi in range(nq):
            qs = slice(qi * tq, (qi + 1) * tq)
            delta.append(jnp.sum(do_ref[qs, :].astype(F32) * o_ref[qs, :].astype(F32), axis=1, keepdims=True))
        for ki in range(nq):
            ks = slice(ki * tq, (ki + 1) * tq)
            kb = k_ref[ks, :]
            vb = v_ref[ks, :]
            dk = jnp.zeros((tq, DQK), F32)
            dv = jnp.zeros((tq, DV), F32)
            for qi in range(ki, nq):
                qs = slice(qi * tq, (qi + 1) * tq)
                qv = q_ref[qs, :]
                dov = do_ref[qs, :]
                s = _dot_nt(qv, kb)
                if qi == ki:
                    s = _causal_mask(s, tq)
                p = jnp.exp2(s - lse_ref[qs, :][:, :1])
                dp = _dot_nt(dov, vb)
                dz = (p * (dp - delta[qi])).astype(BF16)
                dv = dv + _dot_tn(p.astype(BF16), dov)
                dk = dk + _dot_tn(dz, qv)
                dqb = _dot(dz, kb)
                if ki == 0:
                    dq_ref[qs, :] = dqb
                else:
                    dq_ref[qs, :] += dqb
            dk_ref[ks, :] = (dk * LN2).astype(BF16)
            dv_ref[ks, :] = dv.astype(BF16)

    full = lambda w, col: pl.BlockSpec((seq, w), col)
    same = lambda b, h: (b, h)
    return pl.pallas_call(
        body, name="flash_bwd", grid=(nb, H),
        in_specs=[full(DQK, same), full(DQK, same), full(DV, lambda b, h: (b, 2 * h + 1)),
                  full(DV, same), full(DV, same), full(DV, same)],
        out_specs=[full(DQK, same), full(DQK, same), full(DV, same)],
        out_shape=[jax.ShapeDtypeStruct((t, H * DQK), F32), jax.ShapeDtypeStruct((t, H * DQK), BF16),
                   jax.ShapeDtypeStruct((t, H * DV), BF16)],
        compiler_params=_params(("parallel", "parallel")),
    )(q, k, kv, o, do, lse)


TAIL_TM = 256


def tail_fwd(y, attn, proj, x2, tgt, gate, g_post, wco, wmo, wout, seq):
    t = y.shape[0]
    nb = t // seq
    tm = min(TAIL_TM, seq)
    tpb = seq // tm

    def body(y_ref, at_ref, p_ref, x_ref, t_ref, gate_ref, gp_ref, wco_ref, wmo_ref, wout_ref,
             o_ref, ya_ref, yb_ref, m_ref, do2_ref, dout_ref, dgate_ref, dgp_ref, loss_ref):
        i = pl.program_id(0)
        bz = p_ref[0].astype(F32)
        ga = p_ref[1].astype(F32)
        gb = p_ref[2].astype(F32)
        ov = (at_ref[...].astype(F32) * (bz * _sig(bz))).astype(BF16)
        o_ref[...] = ov
        ya = _dot(y_ref[...], wco_ref[...])
        yb = _dot(ov, wmo_ref[...])
        ya_ref[...] = ya.astype(BF16)
        yb_ref[...] = yb.astype(BF16)
        mv = (_sig(ga) * ya + _sig(gb) * yb).astype(BF16)
        m_ref[...] = mv
        o2 = _dot(mv, wout_ref[...])
        r = lax.rsqrt(jnp.mean(o2 * o2, axis=-1, keepdims=True) + EPS)
        nrm = o2 * r
        gp = gp_ref[...]
        gate_v = gate_ref[...]
        rn = nrm * gp
        err = x_ref[...] + gate_v * rn - t_ref[...]
        dout = err * (1.0 / D)
        dout_ref[...] = dout
        dn = dout * gate_v * gp
        do2_ref[...] = (r * (dn - nrm * jnp.mean(dn * nrm, axis=-1, keepdims=True))).astype(BF16)

        @pl.when(i % tpb == 0)
        def _():
            dgate_ref[...] = jnp.zeros_like(dgate_ref)

        @pl.when(i == 0)
        def _():
            dgp_ref[...] = jnp.zeros_like(dgp_ref)
            loss_ref[...] = jnp.zeros_like(loss_ref)

        dgate_ref[...] += jnp.sum(dout * rn, axis=0, keepdims=True)
        dgp_ref[...] += jnp.sum(dout * gate_v * nrm, axis=0, keepdims=True)
        loss_ref[...] += 0.5 * jnp.sum(jnp.mean(err * err, axis=-1, keepdims=True), axis=0, keepdims=True)

    row = pl.BlockSpec((tm, D), lambda i: (i, 0))
    per_batch = pl.BlockSpec((None, 1, D), lambda i: (i // tpb, 0, 0))
    vec = pl.BlockSpec((1, D), lambda i: (0, 0))
    wgt = pl.BlockSpec((D, D), lambda i: (0, 0))
    act = jax.ShapeDtypeStruct((t, D), BF16)
    return pl.pallas_call(
        body, name="tail_fwd", grid=(t // tm,),
        in_specs=[row, row, pl.BlockSpec((3, tm, D), lambda i: (0, i, 0)), row, row, per_batch, vec,
                  wgt, wgt, wgt],
        out_specs=[row, row, row, row, row, row, per_batch, vec, pl.BlockSpec((1, 1), lambda i: (0, 0))],
        out_shape=[act, act, act, act, act, jax.ShapeDtypeStruct((t, D), F32),
                   jax.ShapeDtypeStruct((nb, 1, D), F32), jax.ShapeDtypeStruct((1, D), F32),
                   jax.ShapeDtypeStruct((1, 1), F32)],
        compiler_params=_params(("arbitrary",)),
    )(y, attn, proj, x2, tgt, gate, g_post, wco, wmo, wout)


def tail_bwd(do2, proj, ya, yb, attn, wout, wmo, wco):
    t = do2.shape[0]
    tm = min(TAIL_TM, t)

    def body(do2_ref, p_ref, ya_ref, yb_ref, at_ref, wout_ref, wmo_ref, wco_ref,
             dp_ref, dya_ref, dyb_ref, dat_ref, dy_ref):
        bz = p_ref[0].astype(F32)
        ga = p_ref[1].astype(F32)
        gb = p_ref[2].astype(F32)
        dm = _dot_nt(do2_ref[...], wout_ref[...])
        sa = _sig(ga)
        sb = _sig(gb)
        dya = (dm * sa).astype(BF16)
        dyb = (dm * sb).astype(BF16)
        dya_ref[...] = dya
        dyb_ref[...] = dyb
        dp_ref[1] = (dm * ya_ref[...].astype(F32) * (sa * (1.0 - sa))).astype(BF16)
        dp_ref[2] = (dm * yb_ref[...].astype(F32) * (sb * (1.0 - sb))).astype(BF16)
        dov = _dot_nt(dyb, wmo_ref[...])
        sz = _sig(bz)
        dat_ref[...] = (dov * (bz * sz)).astype(BF16)
        dp_ref[0] = (dov * at_ref[...].astype(F32) * (sz * (1.0 + bz * (1.0 - sz)))).astype(BF16)
        dy_ref[...] = _dot_nt(dya, wco_ref[...]).astype(BF16)

    row = pl.BlockSpec((tm, D), lambda i: (i, 0))
    seg3 = pl.BlockSpec((3, tm, D), lambda i: (0, i, 0))
    wgt = pl.BlockSpec((D, D), lambda i: (0, 0))
    act = jax.ShapeDtypeStruct((t, D), BF16)
    return pl.pallas_call(
        body, name="tail_bwd", grid=(t // tm,),
        in_specs=[row, seg3, row, row, row, wgt, wgt, wgt],
        out_specs=[seg3, row, row, row, row],
        out_shape=[jax.ShapeDtypeStruct((NSEG, t, D), BF16), act, act, act, act],
        compiler_params=_params(("parallel",)),
    )(do2, proj, ya, yb, attn, wout, wmo, wco)


def adamw(w, m, v, g, g2, name):
    rows, cols = w.shape
    tr = rows
    for cand in (256, 128, 64, 32, 16, 8):
        if rows % cand == 0 and rows > cand:
            tr = cand
            break
    has2 = g2 is not None

    def body(*refs):
        if has2:
            w_ref, m_ref, v_ref, g_ref, g2_ref, go_ref, d_ref, mo_ref, vo_ref = refs
            grad = g_ref[...] + g2_ref[...].astype(F32)
        else:
            w_ref, m_ref, v_ref, g_ref, go_ref, d_ref, mo_ref, vo_ref = refs
            grad = g_ref[...]
        mn = ADAM_B1 * m_ref[...] + (1.0 - ADAM_B1) * grad
        vn = ADAM_B2 * v_ref[...] + (1.0 - ADAM_B2) * (grad * grad)
        m_hat = mn / (1.0 - ADAM_B1 ** ADAM_STEP)
        v_hat = vn / (1.0 - ADAM_B2 ** ADAM_STEP)
        go_ref[...] = grad
        d_ref[...] = -ADAM_LR * (m_hat / (jnp.sqrt(v_hat) + ADAM_EPS) + ADAM_WD * w_ref[...])
        mo_ref[...] = mn
        vo_ref[...] = vn

    blk = pl.BlockSpec((tr, cols), lambda i: (i, 0))
    ins = [w, m, v, g] + ([g2] if has2 else [])
    return pl.pallas_call(
        body, name=name, grid=(rows // tr,),
        in_specs=[blk] * len(ins), out_specs=[blk] * 4,
        out_shape=[jax.ShapeDtypeStruct((rows, cols), F32)] * 4,
        compiler_params=_params(("parallel",)),
    )(*ins)


_ORD_A = ("x", "y", "c")
_ORD_B = ("y", "x", "c")


def _to_slots(full, order, col_sharded):
    if col_sharded:
        r = full.shape[0]
        cc = full.shape[1] // 8
        g = full.reshape(r, 2, 2, 2, cc).transpose(1, 2, 3, 0, 4)
    else:
        r = full.shape[0] // 8
        cc = full.shape[1]
        g = full.reshape(2, 2, 2, r, cc)
    names = ("x", "y", "c")
    perm = tuple(names.index(a) for a in order)
    return g.transpose(perm + (3, 4))


def _rows128(a, rows):
    flat = a.reshape(-1)
    return jnp.pad(flat, (0, rows * 128 - flat.shape[0])).reshape(rows, 128)


def kernel(x, c, positions, w_ada, b_ada, g_pre, w_in, conv_w, w_conv_out, g_q, w_uq, g_kv, w_ukv, w_mla_out, w_out, g_post, loss_target, m_w_ada, m_b_ada, m_g_pre, m_w_in, m_conv_w, m_w_conv_out, m_g_q, m_w_uq, m_g_kv, m_w_ukv, m_w_mla_out, m_w_out, m_g_post, v_w_ada, v_b_ada, v_g_pre, v_w_in, v_conv_w, v_w_conv_out, v_g_q, v_w_uq, v_g_kv, v_w_ukv, v_w_mla_out, v_w_out, v_g_post):
    nb, seq, _ = x.shape
    t = nb * seq
    mx, my, mc = lax.axis_index("x"), lax.axis_index("y"), lax.axis_index("c")
    me = 4 * mx + 2 * my + mc
    co = {"x": mx, "y": my, "c": mc}

    x2 = x.reshape(t, D)
    tgt2 = loss_target.reshape(t, D)
    pos2 = positions.reshape(t, 1)

    packed = jnp.concatenate([c.reshape(2 * D // 128, 128), _rows128(conv_w[0], 8)], axis=0)
    gath = small_allgather(packed, "gather_cond")
    c_all = gath[:, :16].reshape(8 * nb, D)
    conv_full = gath[:, 16:19].reshape(8, 3, 128).transpose(1, 0, 2).reshape(3, D)
    conv_full8 = jnp.pad(conv_full, ((0, 5), (0, 0)))
    ada_cols = w_ada.shape[2]
    b_cols = lax.dynamic_slice(b_ada, (0, me * ada_cols), (1, ada_cols))
    mod_part = ada_fwd(c_all, w_ada[0], b_cols)
    mod_g = small_allgather(mod_part.reshape(8 * nb * ada_cols // 128, 128), "gather_mod")
    mod_all = mod_g.reshape(8, 8 * nb, ada_cols).transpose(1, 0, 2).reshape(8 * nb, 8 * ada_cols)
    mod = lax.dynamic_slice(mod_all, (me * nb, 0), (nb, 3 * D))
    shift = mod[:, 0:D].reshape(nb, 1, D)
    scale = mod[:, D:2 * D].reshape(nb, 1, D)
    gate = mod[:, 2 * D:3 * D].reshape(nb, 1, D)

    wt = w_in[0].T.astype(BF16)
    lo = lax.bitcast_convert_type(wt[:, :D // 2], jnp.uint16).astype(jnp.uint32)
    hi = lax.bitcast_convert_type(wt[:, D // 2:], jnp.uint16).astype(jnp.uint32)
    wt_bits = lax.bitcast_convert_type(lo | (hi << 16), F32)
    shards = [wt_bits, w_conv_out[0].astype(BF16), w_mla_out[0].astype(BF16), w_out[0].astype(BF16),
              w_uq[0].astype(BF16), w_ukv[0].astype(BF16)]
    q4 = D // 4
    plan = [(0, (0, q4), _ORD_A), (0, (q4, q4), _ORD_B), (1, None, _ORD_A), (2, None, _ORD_A),
            (3, None, _ORD_B), (4, None, _ORD_B), (5, None, _ORD_B)]
    gw = allgather_big(shards, plan, "gather_weights")
    wt_p = unpack_win(gw[0].reshape(N_IN, D // 2))
    wco = gw[1].reshape(D, D)
    wmo = gw[2].reshape(D, D)
    wout = gw[3].reshape(D, D)
    wuq_full = gw[4].reshape(8, QL, 192).transpose(1, 0, 2)
    wuq_p = jnp.pad(wuq_full, ((0, 0), (0, 0), (0, DQK - 192))).reshape(QL, H * DQK)
    wukv = gw[5].reshape(8, KVL, 256).transpose(1, 0, 2).reshape(KVL, H * 256)

    inv_freq = ROPE_THETA ** (-jnp.arange(0, ROPE, 2, dtype=F32) / ROPE)
    invf = jnp.concatenate([inv_freq, inv_freq, jnp.zeros((128 - ROPE,), F32)]).reshape(1, 128)
    lane = np.arange(128)
    tabs = (invf,
            jnp.asarray(np.where(lane < HALF, -1.0, 0.0).reshape(1, 128), F32),
            jnp.asarray(np.where((lane >= HALF) & (lane < ROPE), 1.0, 0.0).reshape(1, 128), F32))

    h = prenorm_fwd(x2, scale, shift, g_pre, seq)
    proj = proj_matmul(h, wt_p)
    y = conv_fwd(proj, conv_full8, seq)
    q_rot, k_cat, kv, qn, kvn = mla_prep_fwd(proj, pos2, g_q, g_kv, wuq_p, wukv, tabs)
    attn, lse = flash_fwd(q_rot, k_cat, kv, nb, seq)
    o, ya, yb, m, do2, dout, dgate, dg_post, loss_part = tail_fwd(
        y, attn, proj, x2, tgt2, gate, g_post, wco, wmo, wout, seq)

    dproj, dya, dyb, dattn, dy = tail_bwd(do2, proj, ya, yb, attn, wout, wmo, wco)
    g_wout = grad_matmul(m, do2, "grad_w_out")
    g_wmo = grad_matmul(o, dyb, "grad_w_mla_out")
    g_wco = grad_matmul(y, dya, "grad_w_conv_out")
    dproj, dconv = conv_bwd(dproj, proj, dy, conv_full8, seq)
    dq_rot, dk, dv = flash_bwd(q_rot, k_cat, kv, attn, dattn, lse, nb, seq)
    dproj, dq, dkv, dg_q, dg_kv = mla_prep_bwd(dproj, proj, dq_rot, dk, dv, pos2, g_q, g_kv, wuq_p, wukv, tabs)
    g_wuq_p = grad_matmul(qn, dq, "grad_w_uq")
    g_wukv = grad_matmul(kvn, dkv, "grad_w_ukv")
    g_win_p = win_grad_matmul(h, dproj)
    dh = dh_matmul(dproj, wt_p)
    grad_x2, dshift, dscale, dg_pre = prenorm_bwd(dh, x2, dout, scale, g_pre, seq)

    dmod = jnp.concatenate([dshift, dscale, dgate], axis=2).reshape(nb * 3 * D // 128, 128)
    small = jnp.concatenate([
        dmod, _rows128(dg_pre, 8), _rows128(dg_post, 8), _rows128(dg_q, 8), _rows128(dg_kv, 8),
        dconv[0:3].reshape(24, 128), _rows128(loss_part, 8)], axis=0)
    small_g = small_allgather(small, "gather_small_grads")
    sums = slot_sum(small_g)
    dmod_all = small_g[:, 0:48].reshape(8 * nb, 3 * D)
    g_bada = (sums[0:24] + sums[24:48]).reshape(1, 3 * D)
    g_gpre = sums[48:56].reshape(1, D)
    g_gpost = sums[56:64].reshape(1, D)
    g_gq = sums[64:67].reshape(1, QL)
    g_gkv = sums[72:74].reshape(1, KVL)
    g_conv_full = sums[80:104].reshape(3, D)
    loss = sums[104, 0]
    g_conv = lax.dynamic_slice(g_conv_full, (0, me * 128), (3, 128))
    dmod_cols = lax.dynamic_slice(dmod_all, (0, me * ada_cols), (8 * nb, ada_cols))
    g_wada = ada_bwd(c_all, dmod_cols)

    g_wt = unpad_win_grad(g_win_p).reshape(2, 2, 2, N_IN // 8, D)
    g_wuq = g_wuq_p.reshape(QL, H, DQK)[:, :, :192].reshape(QL, H * 192)
    rs_a = ("c", "y", "x")
    rs_b = ("c", "x", "y")
    flat = lambda s: s.reshape(2, 2, 2, -1, 128)
    rest_a = jnp.concatenate([flat(_to_slots(g_wco, rs_a, False)), flat(_to_slots(g_wmo, rs_a, False))], axis=3)
    rest_b = jnp.concatenate([flat(_to_slots(g_wout, rs_b, False)), flat(_to_slots(g_wuq, rs_b, True)),
                              flat(_to_slots(g_wukv, rs_b, True))], axis=3)
    ords = [rs_a, rs_a, rs_b, rs_b]
    hc = D // 2
    win_shape = (2, 2, N_IN // 8, hc)
    pick_w = lambda col: (lambda ref, cc: ref.at[:, :, 1 - cc["c"], :, pl.ds(col * hc, hc)])
    pick_h = lambda ref, cc: ref.at[1 - cc["c"]]
    r1 = exchange([g_wt, rest_a, g_wt, rest_b], ["c"] * 4, [pick_w(0), pick_h, pick_w(1), pick_h],
                  [win_shape, rest_a.shape[1:], win_shape, rest_b.shape[1:]], "rs_exchange_c")
    sel_xyc = jnp.stack([mx, my, mc]).astype(jnp.int32)
    sel1 = [jnp.stack([co[o[0]], co[o[1]]]).astype(jnp.int32) for o in ords]
    sel2 = [jnp.stack([co[o[2]]]).astype(jnp.int32) for o in ords]
    first = [rs_win_add_first(g_wt, r1[0], sel_xyc, 1, 0, "rs_add_first_0"),
             rs_add_first(rest_a, r1[1], sel1[1], "rs_add_first_1"),
             rs_win_add_first(g_wt, r1[2], sel_xyc, 0, 1, "rs_add_first_2"),
             rs_add_first(rest_b, r1[3], sel1[3], "rs_add_first_3")]
    keep1, send1 = zip(*first)
    r2 = exchange(list(send1), [o[1] for o in ords], [None] * 4, [s.shape for s in send1],
                  "rs_exchange_first_ici")
    keep2, send2 = zip(*[rs_add_second(keep1[a], r2[a], sel2[a], "rs_add_second_%d" % a) for a in range(4)])
    r3 = exchange(list(send2), [o[2] for o in ords], [None] * 4, [s.shape for s in send2],
                  "rs_exchange_second_ici")

    gk_win = jnp.concatenate([keep2[0], keep2[2]], axis=1).T
    gr_win = jnp.concatenate([r3[0], r3[2]], axis=1).T
    n_sq = D * 128 // 128
    unflat = lambda a, lo, shape: a[lo:lo + shape[0] * shape[1] // 128].reshape(shape)
    sq = (128, D)
    uq_s = (QL, 192)
    ukv_s = (KVL, 256)
    parts = {
        "w_conv_out": (unflat(keep2[1], 0, sq), unflat(r3[1], 0, sq)),
        "w_mla_out": (unflat(keep2[1], n_sq, sq), unflat(r3[1], n_sq, sq)),
        "w_out": (unflat(keep2[3], 0, sq), unflat(r3[3], 0, sq)),
        "w_uq": (unflat(keep2[3], n_sq, uq_s), unflat(r3[3], n_sq, uq_s)),
        "w_ukv": (unflat(keep2[3], n_sq + QL * 192 // 128, ukv_s), unflat(r3[3], n_sq + QL * 192 // 128, ukv_s)),
        "w_in": (gk_win, gr_win),
        "w_ada": (g_wada, None),
    }

    weights = {"w_ada": (w_ada, m_w_ada, v_w_ada), "w_in": (w_in, m_w_in, v_w_in),
               "w_conv_out": (w_conv_out, m_w_conv_out, v_w_conv_out), "w_uq": (w_uq, m_w_uq, v_w_uq),
               "w_ukv": (w_ukv, m_w_ukv, v_w_ukv), "w_mla_out": (w_mla_out, m_w_mla_out, v_w_mla_out),
               "w_out": (w_out, m_w_out, v_w_out)}
    res = {}
    for nm, (wv, mv, vv) in weights.items():
        ga, gb = parts[nm]
        outs = adamw(wv[0], mv[0], vv[0], ga, gb, "adamw_" + nm)
        res[nm] = [o_[None] for o_ in outs]

    def pack(b_, gp_, gpo_, gq_, gkv_, cw_):
        return jnp.concatenate([_rows128(b_, 24), _rows128(gp_, 8), _rows128(gpo_, 8), _rows128(gq_, 8),
                                _rows128(gkv_, 8), _rows128(cw_, 8)], axis=0)

    sw = pack(b_ada, g_pre, g_post, g_q, g_kv, conv_w)
    sm = pack(m_b_ada, m_g_pre, m_g_post, m_g_q, m_g_kv, m_conv_w)
    sv = pack(v_b_ada, v_g_pre, v_g_post, v_g_q, v_g_kv, v_conv_w)
    sg = pack(g_bada, g_gpre, g_gpost, g_gq, g_gkv, g_conv)
    small_out = adamw(sw, sm, sv, sg, None, "adamw_small")

    def unpack(a):
        return {"b_ada": a[0:24].reshape(1, 3 * D), "g_pre": a[24:32].reshape(1, D),
                "g_post": a[32:40].reshape(1, D), "g_q": a[40:43].reshape(1, QL),
                "g_kv": a[48:50].reshape(1, KVL), "conv_w": a[56:59].reshape(-1)[:3 * 128].reshape(1, 3, 128)}

    for nm in ("b_ada", "g_pre", "g_post", "g_q", "g_kv", "conv_w"):
        res[nm] = [unpack(a)[nm] for a in small_out]

    order = ["w_ada", "b_ada", "g_pre", "w_in", "conv_w", "w_conv_out", "g_q", "w_uq", "g_kv", "w_ukv",
             "w_mla_out", "w_out", "g_post"]
    out = [loss, grad_x2.reshape(nb, seq, D)]
    for k_ in range(4):
        out += [res[nm][k_] for nm in order]
    return tuple(out)
```

```python
import functools

import numpy as np
import jax
import jax.numpy as jnp
from jax import lax
from jax.experimental import pallas as pl
from jax.experimental.pallas import tpu as pltpu

F32 = jnp.float32
BF16 = jnp.bfloat16
MESH = pl.DeviceIdType.MESH

D = 1024
H = 8
QL = 384
KVL = 256
ROPE = 64
HALF = ROPE // 2
DQK = 256
DV = 128
NSEG = 8
NP = NSEG * D
EPS = 1e-6
ROPE_THETA = 10000.0
SM_SCALE = (128 + ROPE) ** -0.5
LOG2E = 1.4426950408889634
LN2 = 0.6931471805599453
FLASH_TQ = 512

SEG_BZ, SEG_GA, SEG_GB, SEG_LAT, SEG_V = 0, 1, 2, 3, 4

ADAM_LR = 0.001
ADAM_B1 = 0.9
ADAM_B2 = 0.999
ADAM_EPS = 1e-08
ADAM_WD = 0.01
ADAM_STEP = 10

VMEM_LIMIT = 56 * 1024 * 1024


def _params(sem=None, vmem=VMEM_LIMIT):
    kw = dict(vmem_limit_bytes=vmem)
    if sem is not None:
        kw["dimension_semantics"] = sem
    return pltpu.CompilerParams(**kw)


def _sig(v):
    return 1.0 / (1.0 + jnp.exp(-v))


def _dot(a, b):
    return jnp.dot(a, b, preferred_element_type=F32)


def _dot_nt(a, b):
    return lax.dot_general(a, b, (((1,), (1,)), ((), ())), preferred_element_type=F32)


def _dot_tn(a, b):
    return lax.dot_general(a, b, (((0,), (0,)), ((), ())), preferred_element_type=F32)


_AXIS_POS = {"x": 0, "y": 1, "c": 2}


def _coords():
    return lax.axis_index("x"), lax.axis_index("y"), lax.axis_index("c")


def _partner(axis):
    p = list(_coords())
    p[_AXIS_POS[axis]] = 1 - p[_AXIS_POS[axis]]
    return tuple(p)


def small_allgather(v, name):
    rows = v.shape[0]

    def body(v_ref, out_ref, send_sems, recv_sems):
        x, y, c = _coords()
        me = 4 * x + 2 * y + c
        out_ref[me] = v_ref[...]
        copies = []
        for k in range(1, 8):
            peer = (1 - x if k & 4 else x, 1 - y if k & 2 else y, 1 - c if k & 1 else c)
            cp = pltpu.make_async_remote_copy(
                src_ref=v_ref, dst_ref=out_ref.at[me],
                send_sem=send_sems.at[k - 1], recv_sem=recv_sems.at[k - 1],
                device_id=peer, device_id_type=MESH)
            cp.start()
            copies.append(cp)
        for cp in copies:
            cp.wait()

    return pl.pallas_call(
        body, name=name,
        out_shape=jax.ShapeDtypeStruct((8, rows, 128), F32),
        in_specs=[pl.BlockSpec(memory_space=pltpu.VMEM)],
        out_specs=pl.BlockSpec(memory_space=pltpu.VMEM),
        scratch_shapes=[pltpu.SemaphoreType.DMA((7,)), pltpu.SemaphoreType.DMA((7,))],
    )(v)


def allgather_big(arrs, plan, name):
    n = len(arrs)
    m = len(plan)

    def body(*refs):
        ins, outs = refs[:n], refs[n:2 * n]
        send_sems, recv_sems, loc_sems = refs[2 * n:]
        x, y, c = _coords()
        co = {"x": x, "y": y, "c": c}

        def window(ref, lead, cols):
            tail = ref.shape[len(lead):]
            idx = tuple(lead) + (slice(None),) * (len(tail) - 1)
            idx += (slice(None),) if cols is None else (pl.ds(cols[0], cols[1]),)
            return ref.at[idx]

        def held(e, free):
            i, cols, _ = plan[e]
            lead = [slice(None) if ax in free else co[ax] for ax in ("x", "y", "c")]
            return window(outs[i], lead, cols)

        def rcopy(e, stage, src, dst, axis):
            return pltpu.make_async_remote_copy(
                src_ref=src, dst_ref=dst,
                send_sem=send_sems.at[e, stage], recv_sem=recv_sems.at[e, stage],
                device_id=_partner(axis), device_id_type=MESH)

        local, stages = [], [[], [], []]
        for e, (i, cols, order) in enumerate(plan):
            mine = window(ins[i], [], cols)
            lc = pltpu.make_async_copy(mine, held(e, ()), loc_sems.at[e])
            lc.start()
            local.append(lc)
            cp = rcopy(e, 0, mine, held(e, ()), order[0])
            cp.start()
            stages[0].append(cp)
        for s in (1, 2):
            for e, (i, cols, order) in enumerate(plan):
                stages[s - 1][e].wait_recv()
                if s == 1:
                    local[e].wait()
                blk = held(e, order[:s])
                cp = rcopy(e, s, blk, blk, order[s])
                cp.start()
                stages[s].append(cp)
        for e in range(m):
            stages[2][e].wait_recv()
        for e in range(m):
            for s in range(3):
                stages[s][e].wait_send()

    any_spec = pl.BlockSpec(memory_space=pl.ANY)
    return pl.pallas_call(
        body, name=name,
        out_shape=[jax.ShapeDtypeStruct((2, 2, 2) + a.shape, a.dtype) for a in arrs],
        in_specs=[any_spec] * n,
        out_specs=[any_spec] * n,
        scratch_shapes=[pltpu.SemaphoreType.DMA((m, 3)), pltpu.SemaphoreType.DMA((m, 3)),
                        pltpu.SemaphoreType.DMA((m,))],
    )(*arrs)


def exchange(arrs, axes, picks, out_shapes, name):
    n = len(arrs)

    def body(*refs):
        ins, outs = refs[:n], refs[n:2 * n]
        send_sems, recv_sems = refs[2 * n:]
        x, y, c = _coords()
        co = {"x": x, "y": y, "c": c}
        copies = []
        for a in range(n):
            src = ins[a] if picks[a] is None else picks[a](ins[a], co)
            cp = pltpu.make_async_remote_copy(
                src_ref=src, dst_ref=outs[a],
                send_sem=send_sems.at[a], recv_sem=recv_sems.at[a],
                device_id=_partner(axes[a]), device_id_type=MESH)
            cp.start()
            copies.append(cp)
        for cp in copies:
            cp.wait()

    any_spec = pl.BlockSpec(memory_space=pl.ANY)
    return pl.pallas_call(
        body, name=name,
        out_shape=[jax.ShapeDtypeStruct(s, a.dtype) for s, a in zip(out_shapes, arrs)],
        in_specs=[any_spec] * n,
        out_specs=[any_spec] * n,
        scratch_shapes=[pltpu.SemaphoreType.DMA((n,)), pltpu.SemaphoreType.DMA((n,))],
    )(*arrs)


def rs_win_add_first(g, r, sel, next_dim, col, name):
    rows, cols = r.shape[2:]

    def body(sel_ref, gk_ref, rk_ref, gs_ref, rs_ref, keep_ref, send_ref):
        keep_ref[...] = gk_ref[...] + rk_ref[...]
        send_ref[...] = (gs_ref[...] + rs_ref[...]).astype(BF16)

    def g_map(flip):
        def f(j, s):
            nxt = 1 - s[next_dim] if flip else s[next_dim]
            return (nxt, j, s[2], 0, col) if next_dim == 0 else (j, nxt, s[2], 0, col)
        return f

    def r_map(flip):
        def f(j, s):
            nxt = 1 - s[next_dim] if flip else s[next_dim]
            return (nxt, j, 0, 0) if next_dim == 0 else (j, nxt, 0, 0)
        return f

    gblk = (None, None, None, rows, cols)
    rblk = (None, None, rows, cols)
    oblk = (None, rows, cols)
    return pl.pallas_call(
        body, name=name,
        grid_spec=pltpu.PrefetchScalarGridSpec(
            num_scalar_prefetch=1, grid=(2,),
            in_specs=[pl.BlockSpec(gblk, g_map(False)), pl.BlockSpec(rblk, r_map(False)),
                      pl.BlockSpec(gblk, g_map(True)), pl.BlockSpec(rblk, r_map(True))],
            out_specs=[pl.BlockSpec(oblk, lambda j, s: (j, 0, 0)),
                       pl.BlockSpec(oblk, lambda j, s: (j, 0, 0))]),
        out_shape=[jax.ShapeDtypeStruct((2, rows, cols), F32),
                   jax.ShapeDtypeStruct((2, rows, cols), BF16)],
        compiler_params=_params(),
    )(sel, g, r, g, r)


def rs_add_first(g, r, sel, name):
    _, _, _, rows, cols = g.shape
    tr = rows // 2

    def body(sel_ref, gk_ref, rk_ref, gs_ref, rs_ref, keep_ref, send_ref):
        keep_ref[...] = gk_ref[...] + rk_ref[...]
        send_ref[...] = (gs_ref[...] + rs_ref[...]).astype(BF16)

    blk = (None, None, None, tr, cols)
    rblk = (None, None, tr, cols)
    oblk = (None, tr, cols)
    return pl.pallas_call(
        body, name=name,
        grid_spec=pltpu.PrefetchScalarGridSpec(
            num_scalar_prefetch=1, grid=(2, 2),
            in_specs=[
                pl.BlockSpec(blk, lambda j, i, s: (s[0], s[1], j, i, 0)),
                pl.BlockSpec(rblk, lambda j, i, s: (s[1], j, i, 0)),
                pl.BlockSpec(blk, lambda j, i, s: (s[0], 1 - s[1], j, i, 0)),
                pl.BlockSpec(rblk, lambda j, i, s: (1 - s[1], j, i, 0)),
            ],
            out_specs=[pl.BlockSpec(oblk, lambda j, i, s: (j, i, 0)),
                       pl.BlockSpec(oblk, lambda j, i, s: (j, i, 0))]),
        out_shape=[jax.ShapeDtypeStruct((2, rows, cols), F32),
                   jax.ShapeDtypeStruct((2, rows, cols), BF16)],
        compiler_params=_params(),
    )(sel, g, r, g, r)


def rs_add_second(k, r, sel, name):
    _, rows, cols = k.shape
    tr = rows // 2 if rows % 32 == 0 else rows
    nt = rows // tr

    def body(sel_ref, kk_ref, rk_ref, ks_ref, rs_ref, keep_ref, send_ref):
        keep_ref[...] = kk_ref[...] + rk_ref[...].astype(F32)
        send_ref[...] = (ks_ref[...] + rs_ref[...].astype(F32)).astype(BF16)

    blk = (None, tr, cols)
    oblk = (tr, cols)
    return pl.pallas_call(
        body, name=name,
        grid_spec=pltpu.PrefetchScalarGridSpec(
            num_scalar_prefetch=1, grid=(nt,),
            in_specs=[
                pl.BlockSpec(blk, lambda i, s: (s[0], i, 0)),
                pl.BlockSpec(blk, lambda i, s: (s[0], i, 0)),
                pl.BlockSpec(blk, lambda i, s: (1 - s[0], i, 0)),
                pl.BlockSpec(blk, lambda i, s: (1 - s[0], i, 0)),
            ],
            out_specs=[pl.BlockSpec(oblk, lambda i, s: (i, 0)),
                       pl.BlockSpec(oblk, lambda i, s: (i, 0))]),
        out_shape=[jax.ShapeDtypeStruct((rows, cols), F32),
                   jax.ShapeDtypeStruct((rows, cols), BF16)],
        compiler_params=_params(),
    )(sel, k, r, k, r)


SEG_ROWS = (4800, 5824, 6848, 4096, 0, 1024, 2048, 3072)
LAT_ROWS = QL + KVL + ROPE
N_IN = 7872


def _seg_row(j):
    return pl.multiple_of(jnp.where(j < 3, 4800 + 1024 * j, jnp.where(j == 3, 4096, (j - 4) * 1024)), 8)


def proj_matmul(h, wt_bits):
    t = h.shape[0]
    tm = min(1024, t)

    def body(h_ref, w_hbm, o_ref, wt_ref, buf, sem):
        j = pl.program_id(0)

        @pl.when(pl.program_id(1) == 0)
        def _():
            cp = pltpu.make_async_copy(w_hbm.at[pl.ds(_seg_row(j), D)], buf, sem)
            cp.start()
            cp.wait()
            bits = pltpu.bitcast(buf[...], jnp.uint32)
            row = lax.broadcasted_iota(jnp.int32, (D, D // 2), 0)
            live = jnp.logical_or(j != SEG_LAT, row < LAT_ROWS)
            lo = pltpu.bitcast(bits << 16, F32)
            hi = pltpu.bitcast(bits & jnp.uint32(0xFFFF0000), F32)
            wt_ref[:, :D // 2] = jnp.where(live, lo, 0.0).astype(BF16)
            wt_ref[:, D // 2:] = jnp.where(live, hi, 0.0).astype(BF16)

        o_ref[...] = _dot_nt(h_ref[...], wt_ref[...]).astype(BF16)

    return pl.pallas_call(
        body, name="proj_matmul", grid=(NSEG, t // tm),
        in_specs=[pl.BlockSpec((tm, D), lambda j, i: (i, 0)),
                  pl.BlockSpec(memory_space=pl.ANY)],
        out_specs=[pl.BlockSpec((None, tm, D), lambda j, i: (j, i, 0)),
                   pl.BlockSpec((D, D), lambda j, i: (j, 0))],
        out_shape=[jax.ShapeDtypeStruct((NSEG, t, D), BF16), jax.ShapeDtypeStruct((NP, D), BF16)],
        scratch_shapes=[pltpu.VMEM((D, D // 2), F32), pltpu.SemaphoreType.DMA],
        compiler_params=_params(("arbitrary", "arbitrary")),
    )(h, wt_bits)


def dh_matmul(dproj, wt):
    t = dproj.shape[1]
    tm = min(1024, t)

    def body(d_ref, w_ref, o_ref, acc_ref):
        k = pl.program_id(1)

        @pl.when(k == 0)
        def _():
            acc_ref[...] = jnp.zeros_like(acc_ref)

        acc_ref[...] += _dot(d_ref[...], w_ref[...])

        @pl.when(k == NSEG - 1)
        def _():
            o_ref[...] = acc_ref[...]

    return pl.pallas_call(
        body, name="dh_matmul", grid=(t // tm, NSEG),
        in_specs=[pl.BlockSpec((None, tm, D), lambda i, k: (k, i, 0)),
                  pl.BlockSpec((D, D), lambda i, k: (k, 0))],
        out_specs=pl.BlockSpec((tm, D), lambda i, k: (i, 0)),
        out_shape=jax.ShapeDtypeStruct((t, D), F32),
        scratch_shapes=[pltpu.VMEM((tm, D), F32)],
        compiler_params=_params(("parallel", "arbitrary")),
    )(dproj, wt)


def win_grad_matmul(h, dproj):
    t = h.shape[0]
    tk = min(1024, t)
    nk = t // tk

    def body(h_ref, d_ref, o_hbm, acc_ref, sem):
        j = pl.program_id(0)
        k = pl.program_id(1)

        @pl.when(k == 0)
        def _():
            acc_ref[...] = jnp.zeros_like(acc_ref)

        acc_ref[...] += _dot_tn(d_ref[...], h_ref[...])

        @pl.when(jnp.logical_and(k == nk - 1, j != SEG_LAT))
        def _():
            cp = pltpu.make_async_copy(acc_ref, o_hbm.at[pl.ds(_seg_row(j), D)], sem)
            cp.start()
            cp.wait()

        @pl.when(jnp.logical_and(k == nk - 1, j == SEG_LAT))
        def _():
            cp = pltpu.make_async_copy(acc_ref.at[pl.ds(0, LAT_ROWS)],
                                       o_hbm.at[pl.ds(SEG_ROWS[SEG_LAT], LAT_ROWS)], sem)
            cp.start()
            cp.wait()

    return pl.pallas_call(
        body, name="win_grad_matmul", grid=(NSEG, nk),
        in_specs=[pl.BlockSpec((tk, D), lambda j, k: (k, 0)),
                  pl.BlockSpec((None, tk, D), lambda j, k: (j, k, 0))],
        out_specs=pl.BlockSpec(memory_space=pl.ANY),
        out_shape=jax.ShapeDtypeStruct((N_IN, D), F32),
        scratch_shapes=[pltpu.VMEM((D, D), F32), pltpu.SemaphoreType.DMA],
        compiler_params=_params(("arbitrary", "arbitrary")),
    )(h, dproj)


def grad_matmul(a, b, name):
    t, m = a.shape
    n = b.shape[1]
    tk = min(1024, t)
    nk = t // tk

    def body(a_ref, b_ref, o_ref, acc_ref):
        k = pl.program_id(0)

        @pl.when(k == 0)
        def _():
            acc_ref[...] = jnp.zeros_like(acc_ref)

        acc_ref[...] += _dot_tn(a_ref[...], b_ref[...])

        @pl.when(k == nk - 1)
        def _():
            o_ref[...] = acc_ref[...]

    return pl.pallas_call(
        body, name=name, grid=(nk,),
        in_specs=[pl.BlockSpec((tk, m), lambda k: (k, 0)),
                  pl.BlockSpec((tk, n), lambda k: (k, 0))],
        out_specs=pl.BlockSpec((m, n), lambda k: (0, 0)),
        out_shape=jax.ShapeDtypeStruct((m, n), F32),
        scratch_shapes=[pltpu.VMEM((m, n), F32)],
        compiler_params=_params(("arbitrary",)),
    )(a, b)


def ada_fwd(c_all, w_ada, b_cols):
    def body(c_ref, w_ref, b_ref, o_ref):
        o_ref[...] = _dot(c_ref[...].astype(BF16), w_ref[...].astype(BF16)) + b_ref[...]

    return pl.pallas_call(
        body, name="ada_fwd",
        out_shape=jax.ShapeDtypeStruct((c_all.shape[0], w_ada.shape[1]), F32),
        compiler_params=_params(),
    )(c_all, w_ada, b_cols)


def ada_bwd(c_all, dmod_cols):
    def body(c_ref, d_ref, o_ref):
        o_ref[...] = _dot_tn(c_ref[...].astype(BF16), d_ref[...].astype(BF16))

    return pl.pallas_call(
        body, name="ada_bwd",
        out_shape=jax.ShapeDtypeStruct((c_all.shape[1], dmod_cols.shape[1]), F32),
        compiler_params=_params(),
    )(c_all, dmod_cols)


def slot_sum(g):
    def body(g_ref, o_ref):
        acc = g_ref[0]
        for s in range(1, 8):
            acc = acc + g_ref[s]
        o_ref[...] = acc

    return pl.pallas_call(
        body, name="slot_sum",
        out_shape=jax.ShapeDtypeStruct(g.shape[1:], F32),
    )(g)


def prenorm_fwd(x2, scale, shift, g_pre, seq):
    t = x2.shape[0]
    tm = min(512, seq)
    tpb = seq // tm

    def body(x_ref, sc_ref, sh_ref, g_ref, h_ref):
        xv = x_ref[...]
        r = lax.rsqrt(jnp.mean(xv * xv, axis=-1, keepdims=True) + EPS)
        hv = (xv * r * g_ref[...]) * (1.0 + sc_ref[...]) + sh_ref[...]
        h_ref[...] = hv.astype(BF16)

    per_batch = pl.BlockSpec((None, 1, D), lambda i: (i // tpb, 0, 0))
    return pl.pallas_call(
        body, name="prenorm_fwd", grid=(t // tm,),
        in_specs=[pl.BlockSpec((tm, D), lambda i: (i, 0)), per_batch, per_batch,
                  pl.BlockSpec((1, D), lambda i: (0, 0))],
        out_specs=pl.BlockSpec((tm, D), lambda i: (i, 0)),
        out_shape=jax.ShapeDtypeStruct((t, D), BF16),
        compiler_params=_params(("parallel",)),
    )(x2, scale, shift, g_pre)


def prenorm_bwd(dh, x2, dout, scale, g_pre, seq):
    t = x2.shape[0]
    nb = t // seq
    tm = min(512, seq)
    tpb = seq // tm

    def body(dh_ref, x_ref, do_ref, sc_ref, g_ref, gx_ref, dsh_ref, dsc_ref, dg_ref):
        i = pl.program_id(0)
        xv = x_ref[...]
        dhv = dh_ref[...]
        g = g_ref[...]
        r = lax.rsqrt(jnp.mean(xv * xv, axis=-1, keepdims=True) + EPS)
        nrm = xv * r
        dxn = dhv * (1.0 + sc_ref[...])
        dn = dxn * g
        dx = r * (dn - nrm * jnp.mean(dn * nrm, axis=-1, keepdims=True))
        gx_ref[...] = dx + do_ref[...]

        @pl.when(i % tpb == 0)
        def _():
            dsh_ref[...] = jnp.zeros_like(dsh_ref)
            dsc_ref[...] = jnp.zeros_like(dsc_ref)

        @pl.when(i == 0)
        def _():
            dg_ref[...] = jnp.zeros_like(dg_ref)

        dsh_ref[...] += jnp.sum(dhv, axis=0, keepdims=True)
        dsc_ref[...] += jnp.sum(dhv * (nrm * g), axis=0, keepdims=True)
        dg_ref[...] += jnp.sum(dxn * nrm, axis=0, keepdims=True)

    row = pl.BlockSpec((tm, D), lambda i: (i, 0))
    per_batch = pl.BlockSpec((None, 1, D), lambda i: (i // tpb, 0, 0))
    vec = pl.BlockSpec((1, D), lambda i: (0, 0))
    return pl.pallas_call(
        body, name="prenorm_bwd", grid=(t // tm,),
        in_specs=[row, row, row, per_batch, vec],
        out_specs=[row, per_batch, per_batch, vec],
        out_shape=[jax.ShapeDtypeStruct((t, D), F32),
                   jax.ShapeDtypeStruct((nb, 1, D), F32),
                   jax.ShapeDtypeStruct((nb, 1, D), F32),
                   jax.ShapeDtypeStruct((1, D), F32)],
        compiler_params=_params(("arbitrary",)),
    )(dh, x2, dout, scale, g_pre)


CONV_TC = 128


def _shift_down(u, k, rows):
    idx = lax.broadcasted_iota(jnp.int32, u.shape, 0)
    return jnp.where(idx >= k, pltpu.roll(u, k, 0), 0.0)


def _shift_up(u, k, rows):
    idx = lax.broadcasted_iota(jnp.int32, u.shape, 0)
    return jnp.where(idx < rows - k, pltpu.roll(u, rows - k, 0), 0.0)


def conv_fwd(proj, conv_w, seq):
    t = proj.shape[1]
    nb = t // seq

    def body(p_ref, w_ref, y_ref):
        av = p_ref[0].astype(F32)
        ab = p_ref[1].astype(F32)
        ac = p_ref[2].astype(F32)
        az = p_ref[3].astype(F32)
        w = w_ref[...]
        u = ac * av
        y1 = _shift_down(u, 2, seq) * w[0:1] + _shift_down(u, 1, seq) * w[1:2] + u * w[2:3]
        y_ref[...] = (ab * y1 * (az * _sig(az))).astype(BF16)

    return pl.pallas_call(
        body, name="conv_fwd", grid=(nb, D // CONV_TC),
        in_specs=[pl.BlockSpec((4, seq, CONV_TC), lambda b, ci: (1, b, ci)),
                  pl.BlockSpec((8, CONV_TC), lambda b, ci: (0, ci))],
        out_specs=pl.BlockSpec((seq, CONV_TC), lambda b, ci: (b, ci)),
        out_shape=jax.ShapeDtypeStruct((t, D), BF16),
        compiler_params=_params(("parallel", "parallel")),
    )(proj, conv_w)


def conv_bwd(dproj, proj, dy, conv_w, seq):
    t = proj.shape[1]
    nb = t // seq

    def body(dp_in_ref, p_ref, dy_ref, w_ref, dp_ref, dw_ref):
        b = pl.program_id(1)
        av = p_ref[0].astype(F32)
        ab = p_ref[1].astype(F32)
        ac = p_ref[2].astype(F32)
        az = p_ref[3].astype(F32)
        dyv = dy_ref[...].astype(F32)
        w = w_ref[...]
        u = ac * av
        u1 = _shift_down(u, 1, seq)
        u2 = _shift_down(u, 2, seq)
        y1 = u2 * w[0:1] + u1 * w[1:2] + u * w[2:3]
        sz = _sig(az)
        silu = az * sz
        dy1 = dyv * ab * silu
        du = dy1 * w[2:3] + _shift_up(dy1, 1, seq) * w[1:2] + _shift_up(dy1, 2, seq) * w[0:1]
        dp_ref[0] = (du * ac).astype(BF16)
        dp_ref[1] = (dyv * y1 * silu).astype(BF16)
        dp_ref[2] = (du * av).astype(BF16)
        dp_ref[3] = (dyv * ab * y1 * (sz * (1.0 + az * (1.0 - sz)))).astype(BF16)

        @pl.when(b == 0)
        def _():
            dw_ref[...] = jnp.zeros_like(dw_ref)

        dw_ref[0:1, :] += jnp.sum(dy1 * u2, axis=0, keepdims=True)
        dw_ref[1:2, :] += jnp.sum(dy1 * u1, axis=0, keepdims=True)
        dw_ref[2:3, :] += jnp.sum(dy1 * u, axis=0, keepdims=True)

    return pl.pallas_call(
        body, name="conv_bwd", grid=(D // CONV_TC, nb),
        in_specs=[pl.BlockSpec(memory_space=pl.ANY),
                  pl.BlockSpec((4, seq, CONV_TC), lambda ci, b: (1, b, ci)),
                  pl.BlockSpec((seq, CONV_TC), lambda ci, b: (b, ci)),
                  pl.BlockSpec((8, CONV_TC), lambda ci, b: (0, ci))],
        out_specs=[pl.BlockSpec((4, seq, CONV_TC), lambda ci, b: (1, b, ci)),
                   pl.BlockSpec((8, CONV_TC), lambda ci, b: (0, ci))],
        out_shape=[jax.ShapeDtypeStruct(dproj.shape, BF16),
                   jax.ShapeDtypeStruct((8, D), F32)],
        input_output_aliases={0: 0},
        compiler_params=_params(("parallel", "arbitrary")),
    )(dproj, proj, dy, conv_w)


def _rope_tables(pos_ref, invf_ref, ma_ref, mb_ref):
    ang = pos_ref[...].astype(F32) * invf_ref[...]
    cs = jnp.cos(ang)
    sn = jnp.sin(ang)
    return cs, sn * ma_ref[...], sn * mb_ref[...]


def _head_tables(cs, sa, sb):
    one = jnp.ones_like(cs)
    zero = jnp.zeros_like(cs)
    return (jnp.tile(jnp.concatenate([one, cs], axis=1), (1, H)),
            jnp.tile(jnp.concatenate([zero, sa], axis=1), (1, H)),
            jnp.tile(jnp.concatenate([zero, sb], axis=1), (1, H)))


def _rotate(v, cs, sa, sb, sign):
    width = v.shape[1]
    return v * cs + sign * (pltpu.roll(v, width - HALF, 1) * sa + pltpu.roll(v, HALF, 1) * sb)


MLA_TM = 256


def mla_prep_fwd(proj, pos, g_q, g_kv, wuq, wukv, tabs):
    t = proj.shape[1]
    tm = min(MLA_TM, t)

    def body(lat_ref, pos_ref, gq_ref, gkv_ref, wuq_ref, wukv_ref, invf_ref, ma_ref, mb_ref,
             q_ref, k_ref, kv_ref, qn_ref, kvn_ref):
        lat = lat_ref[...].astype(F32)
        ql = lat[:, :QL]
        kl = lat[:, QL:QL + KVL]
        kr = lat[:, QL + KVL:QL + KVL + 128]
        qn = (ql * lax.rsqrt(jnp.mean(ql * ql, axis=-1, keepdims=True) + EPS) * gq_ref[...]).astype(BF16)
        kvn = (kl * lax.rsqrt(jnp.mean(kl * kl, axis=-1, keepdims=True) + EPS) * gkv_ref[...]).astype(BF16)
        qn_ref[...] = qn
        kvn_ref[...] = kvn
        cs, sa, sb = _rope_tables(pos_ref, invf_ref, ma_ref, mb_ref)
        hc, ha, hb = _head_tables(cs, sa, sb)
        q = _dot(qn, wuq_ref[...])
        q_ref[...] = (_rotate(q, hc, ha, hb, 1.0) * (SM_SCALE * LOG2E)).astype(BF16)
        kv = _dot(kvn, wukv_ref[...]).astype(BF16)
        kv_ref[...] = kv
        kpe = _rotate(kr, cs, sa, sb, 1.0).astype(BF16)
        for hh in range(H):
            k_ref[:, hh * DQK:hh * DQK + 128] = kv[:, hh * DQK:hh * DQK + 128]
            k_ref[:, hh * DQK + 128:(hh + 1) * DQK] = kpe

    row = lambda w: pl.BlockSpec((tm, w), lambda i: (i, 0))
    const = lambda a: pl.BlockSpec(a.shape, lambda i: (0,) * a.ndim)
    return pl.pallas_call(
        body, name="mla_prep_fwd", grid=(t // tm,),
        in_specs=[pl.BlockSpec((None, tm, D), lambda i: (SEG_LAT, i, 0)), row(1),
                  const(g_q), const(g_kv), const(wuq), const(wukv)] + [const(a) for a in tabs],
        out_specs=[row(H * DQK), row(H * DQK), row(H * DQK), row(QL), row(KVL)],
        out_shape=[jax.ShapeDtypeStruct((t, H * DQK), BF16)] * 3
        + [jax.ShapeDtypeStruct((t, QL), BF16), jax.ShapeDtypeStruct((t, KVL), BF16)],
        compiler_params=_params(("parallel",)),
    )(proj, pos, g_q, g_kv, wuq, wukv, *tabs)


def mla_prep_bwd(dproj, proj, dq_rot, dk, dv, pos, g_q, g_kv, wuq, wukv, tabs):
    t = proj.shape[1]
    tm = min(MLA_TM, t)

    def body(dp_in_ref, lat_ref, dqr_ref, dk_ref, dv_ref, pos_ref, gq_ref, gkv_ref, wuq_ref, wukv_ref,
             invf_ref, ma_ref, mb_ref, dp_ref, dq_ref, dkv_ref, dgq_ref, dgkv_ref):
        i = pl.program_id(0)
        lat = lat_ref[...].astype(F32)
        ql = lat[:, :QL]
        kl = lat[:, QL:QL + KVL]
        rq = lax.rsqrt(jnp.mean(ql * ql, axis=-1, keepdims=True) + EPS)
        rk = lax.rsqrt(jnp.mean(kl * kl, axis=-1, keepdims=True) + EPS)
        nq = ql * rq
        nk = kl * rk
        cs, sa, sb = _rope_tables(pos_ref, invf_ref, ma_ref, mb_ref)
        hc, ha, hb = _head_tables(cs, sa, sb)
        dq = _rotate(dqr_ref[...] * SM_SCALE, hc, ha, hb, -1.0).astype(BF16)
        dq_ref[...] = dq
        dkpe = jnp.zeros((tm, 128), F32)
        for hh in range(H):
            dkv_ref[:, hh * DQK:hh * DQK + 128] = dk_ref[:, hh * DQK:hh * DQK + 128]
            dkv_ref[:, hh * DQK + 128:(hh + 1) * DQK] = dv_ref[:, hh * DV:(hh + 1) * DV]
            dkpe = dkpe + dk_ref[:, hh * DQK + 128:(hh + 1) * DQK].astype(F32)
        lane = lax.broadcasted_iota(jnp.int32, (tm, 128), 1)
        dkr = jnp.where(lane < ROPE, _rotate(dkpe, cs, sa, sb, -1.0), 0.0)
        dqn = _dot_nt(dq, wuq_ref[...])
        dkvn = _dot_nt(dkv_ref[...], wukv_ref[...])
        gq = gq_ref[...]
        gkv = gkv_ref[...]
        dnq = dqn * gq
        dnk = dkvn * gkv
        dql = rq * (dnq - nq * jnp.mean(dnq * nq, axis=-1, keepdims=True))
        dkl = rk * (dnk - nk * jnp.mean(dnk * nk, axis=-1, keepdims=True))
        dp_ref[:, :QL] = dql.astype(BF16)
        dp_ref[:, QL:QL + KVL] = dkl.astype(BF16)
        dp_ref[:, QL + KVL:QL + KVL + 128] = dkr.astype(BF16)
        dp_ref[:, QL + KVL + 128:] = jnp.zeros((tm, D - QL - KVL - 128), BF16)

        @pl.when(i == 0)
        def _():
            dgq_ref[...] = jnp.zeros_like(dgq_ref)
            dgkv_ref[...] = jnp.zeros_like(dgkv_ref)

        dgq_ref[...] += jnp.sum(dqn * nq, axis=0, keepdims=True)
        dgkv_ref[...] += jnp.sum(dkvn * nk, axis=0, keepdims=True)

    row = lambda w: pl.BlockSpec((tm, w), lambda i: (i, 0))
    const = lambda a: pl.BlockSpec(a.shape, lambda i: (0,) * a.ndim)
    seg = pl.BlockSpec((None, tm, D), lambda i: (SEG_LAT, i, 0))
    return pl.pallas_call(
        body, name="mla_prep_bwd", grid=(t // tm,),
        in_specs=[pl.BlockSpec(memory_space=pl.ANY), seg, row(H * DQK), row(H * DQK), row(H * DV), row(1),
                  const(g_q), const(g_kv), const(wuq), const(wukv)] + [const(a) for a in tabs],
        out_specs=[seg, row(H * DQK), row(H * DQK),
                   pl.BlockSpec((1, QL), lambda i: (0, 0)), pl.BlockSpec((1, KVL), lambda i: (0, 0))],
        out_shape=[jax.ShapeDtypeStruct(dproj.shape, BF16),
                   jax.ShapeDtypeStruct((t, H * DQK), BF16), jax.ShapeDtypeStruct((t, H * DQK), BF16),
                   jax.ShapeDtypeStruct((1, QL), F32), jax.ShapeDtypeStruct((1, KVL), F32)],
        input_output_aliases={0: 0},
        compiler_params=_params(("arbitrary",)),
    )(dproj, proj, dq_rot, dk, dv, pos, g_q, g_kv, wuq, wukv, *tabs)


def _causal_mask(s, n):
    row = lax.broadcasted_iota(jnp.int32, (n, n), 0)
    col = lax.broadcasted_iota(jnp.int32, (n, n), 1)
    return jnp.where(col <= row, s, -1e30)


def flash_fwd(q, k, kv, nb, seq):
    t = q.shape[0]
    tq = min(FLASH_TQ, seq)
    nq = seq // tq

    def body(q_ref, k_ref, v_ref, o_ref, lse_ref):
        for qi in range(nq):
            qs = slice(qi * tq, (qi + 1) * tq)
            qv = q_ref[qs, :]
            m = jnp.full((tq, 1), -1e30, F32)
            l = jnp.zeros((tq, 1), F32)
            acc = jnp.zeros((tq, DV), F32)
            for j in range(qi + 1):
                ks = slice(j * tq, (j + 1) * tq)
                s = _dot_nt(qv, k_ref[ks, :])
                if j == qi:
                    s = _causal_mask(s, tq)
                m_new = jnp.maximum(m, jnp.max(s, axis=1, keepdims=True))
                p = jnp.exp2(s - m_new)
                alpha = jnp.exp2(m - m_new)
                l = alpha * l + jnp.sum(p, axis=1, keepdims=True)
                acc = alpha * acc + _dot(p.astype(BF16), v_ref[ks, :])
                m = m_new
            o_ref[qs, :] = (acc / l).astype(BF16)
            lse_ref[qs, :] = jnp.broadcast_to(m + jnp.log(l) * LOG2E, (tq, DV))

    out_blk = pl.BlockSpec((seq, DV), lambda b, h: (b, h))
    return pl.pallas_call(
        body, name="flash_fwd", grid=(nb, H),
        in_specs=[pl.BlockSpec((seq, DQK), lambda b, h: (b, h)),
                  pl.BlockSpec((seq, DQK), lambda b, h: (b, h)),
                  pl.BlockSpec((seq, DV), lambda b, h: (b, 2 * h + 1))],
        out_specs=[out_blk, out_blk],
        out_shape=[jax.ShapeDtypeStruct((t, H * DV), BF16), jax.ShapeDtypeStruct((t, H * DV), F32)],
        compiler_params=_params(("parallel", "parallel")),
    )(q, k, kv)


def flash_bwd(q, k, kv, o, do, lse, nb, seq):
    t = q.shape[0]
    tq = min(FLASH_TQ, seq)
    nq = seq // tq

    def body(q_ref, k_ref, v_ref, o_ref, do_ref, lse_ref, dq_ref, dk_ref, dv_ref):
        delta = []
        for qi in range(nq):
            qs = slice(qi * tq, (qi + 1) * tq)
            delta.append(jnp.sum(do_ref[qs, :].astype(F32) * o_ref[qs, :].astype(F32), axis=1, keepdims=True))
        for ki in range(nq):
            ks = slice(ki * tq, (ki + 1) * tq)
            kb = k_ref[ks, :]
            vb = v_ref[ks, :]
            dk = jnp.zeros((tq, DQK), F32)
            dv = jnp.zeros((tq, DV), F32)
            for qi in range(ki, nq):
                qs = slice(qi * tq, (qi + 1) * tq)
                qv = q_ref[qs, :]
                dov = do_ref[qs, :]
                s = _dot_nt(qv, kb)
                if qi == ki:
                    s = _causal_mask(s, tq)
                p = jnp.exp2(s - lse_ref[qs, :][:, :1])
                dp = _dot_nt(dov, vb)
                dz = (p * (dp - delta[qi])).astype(BF16)
                dv = dv + _dot_tn(p.astype(BF16), dov)
                dk = dk + _dot_tn(dz, qv)
                dqb = _dot(dz, kb)
                if ki == 0:
                    dq_ref[qs, :] = dqb
                else:
                    dq_ref[qs, :] += dqb
            dk_ref[ks, :] = (dk * LN2).astype(BF16)
            dv_ref[ks, :] = dv.astype(BF16)

    full = lambda w, col: pl.BlockSpec((seq, w), col)
    same = lambda b, h: (b, h)
    return pl.pallas_call(
        body, name="flash_bwd", grid=(nb, H),
        in_specs=[full(DQK, same), full(DQK, same), full(DV, lambda b, h: (b, 2 * h + 1)),
                  full(DV, same), full(DV, same), full(DV, same)],
        out_specs=[full(DQK, same), full(DQK, same), full(DV, same)],
        out_shape=[jax.ShapeDtypeStruct((t, H * DQK), F32), jax.ShapeDtypeStruct((t, H * DQK), BF16),
                   jax.ShapeDtypeStruct((t, H * DV), BF16)],
        compiler_params=_params(("parallel", "parallel")),
    )(q, k, kv, o, do, lse)


TAIL_TM = 256


def tail_fwd(y, attn, proj, x2, tgt, gate, g_post, wco, wmo, wout, seq):
    t = y.shape[0]
    nb = t // seq
    tm = min(TAIL_TM, seq)
    tpb = seq // tm

    def body(y_ref, at_ref, p_ref, x_ref, t_ref, gate_ref, gp_ref, wco_ref, wmo_ref, wout_ref,
             o_ref, ya_ref, yb_ref, m_ref, do2_ref, dout_ref, dgate_ref, dgp_ref, loss_ref):
        i = pl.program_id(0)
        bz = p_ref[0].astype(F32)
        ga = p_ref[1].astype(F32)
        gb = p_ref[2].astype(F32)
        ov = (at_ref[...].astype(F32) * (bz * _sig(bz))).astype(BF16)
        o_ref[...] = ov
        ya = _dot(y_ref[...], wco_ref[...])
        yb = _dot(ov, wmo_ref[...])
        ya_ref[...] = ya.astype(BF16)
        yb_ref[...] = yb.astype(BF16)
        mv = (_sig(ga) * ya + _sig(gb) * yb).astype(BF16)
        m_ref[...] = mv
        o2 = _dot(mv, wout_ref[...])
        r = lax.rsqrt(jnp.mean(o2 * o2, axis=-1, keepdims=True) + EPS)
        nrm = o2 * r
        gp = gp_ref[...]
        gate_v = gate_ref[...]
        rn = nrm * gp
        err = x_ref[...] + gate_v * rn - t_ref[...]
        dout = err * (1.0 / D)
        dout_ref[...] = dout
        dn = dout * gate_v * gp
        do2_ref[...] = (r * (dn - nrm * jnp.mean(dn * nrm, axis=-1, keepdims=True))).astype(BF16)

        @pl.when(i % tpb == 0)
        def _():
            dgate_ref[...] = jnp.zeros_like(dgate_ref)

        @pl.when(i == 0)
        def _():
            dgp_ref[...] = jnp.zeros_like(dgp_ref)
            loss_ref[...] = jnp.zeros_like(loss_ref)

        dgate_ref[...] += jnp.sum(dout * rn, axis=0, keepdims=True)
        dgp_ref[...] += jnp.sum(dout * gate_v * nrm, axis=0, keepdims=True)
        loss_ref[...] += 0.5 * jnp.sum(jnp.mean(err * err, axis=-1, keepdims=True), axis=0, keepdims=True)

    row = pl.BlockSpec((tm, D), lambda i: (i, 0))
    per_batch = pl.BlockSpec((None, 1, D), lambda i: (i // tpb, 0, 0))
    vec = pl.BlockSpec((1, D), lambda i: (0, 0))
    wgt = pl.BlockSpec((D, D), lambda i: (0, 0))
    act = jax.ShapeDtypeStruct((t, D), BF16)
    return pl.pallas_call(
        body, name="tail_fwd", grid=(t // tm,),
        in_specs=[row, row, pl.BlockSpec((3, tm, D), lambda i: (0, i, 0)), row, row, per_batch, vec,
                  wgt, wgt, wgt],
        out_specs=[row, row, row, row, row, row, per_batch, vec, pl.BlockSpec((1, 1), lambda i: (0, 0))],
        out_shape=[act, act, act, act, act, jax.ShapeDtypeStruct((t, D), F32),
                   jax.ShapeDtypeStruct((nb, 1, D), F32), jax.ShapeDtypeStruct((1, D), F32),
                   jax.ShapeDtypeStruct((1, 1), F32)],
        compiler_params=_params(("arbitrary",)),
    )(y, attn, proj, x2, tgt, gate, g_post, wco, wmo, wout)


def tail_bwd(do2, proj, ya, yb, attn, wout, wmo, wco):
    t = do2.shape[0]
    tm = min(TAIL_TM, t)

    def body(do2_ref, p_ref, ya_ref, yb_ref, at_ref, wout_ref, wmo_ref, wco_ref,
             dp_ref, dya_ref, dyb_ref, dat_ref, dy_ref):
        bz = p_ref[0].astype(F32)
        ga = p_ref[1].astype(F32)
        gb = p_ref[2].astype(F32)
        dm = _dot_nt(do2_ref[...], wout_ref[...])
        sa = _sig(ga)
        sb = _sig(gb)
        dya = (dm * sa).astype(BF16)
        dyb = (dm * sb).astype(BF16)
        dya_ref[...] = dya
        dyb_ref[...] = dyb
        dp_ref[1] = (dm * ya_ref[...].astype(F32) * (sa * (1.0 - sa))).astype(BF16)
        dp_ref[2] = (dm * yb_ref[...].astype(F32) * (sb * (1.0 - sb))).astype(BF16)
        dov = _dot_nt(dyb, wmo_ref[...])
        sz = _sig(bz)
        dat_ref[...] = (dov * (bz * sz)).astype(BF16)
        dp_ref[0] = (dov * at_ref[...].astype(F32) * (sz * (1.0 + bz * (1.0 - sz)))).astype(BF16)
        dy_ref[...] = _dot_nt(dya, wco_ref[...]).astype(BF16)

    row = pl.BlockSpec((tm, D), lambda i: (i, 0))
    seg3 = pl.BlockSpec((3, tm, D), lambda i: (0, i, 0))
    wgt = pl.BlockSpec((D, D), lambda i: (0, 0))
    act = jax.ShapeDtypeStruct((t, D), BF16)
    return pl.pallas_call(
        body, name="tail_bwd", grid=(t // tm,),
        in_specs=[row, seg3, row, row, row, wgt, wgt, wgt],
        out_specs=[seg3, row, row, row, row],
        out_shape=[jax.ShapeDtypeStruct((NSEG, t, D), BF16), act, act, act, act],
        compiler_params=_params(("parallel",)),
    )(do2, proj, ya, yb, attn, wout, wmo, wco)


def adamw(w, m, v, g, g2, name):
    rows, cols = w.shape
    tr = rows
    for cand in (256, 128, 64, 32, 16, 8):
        if rows % cand == 0 and rows > cand:
            tr = cand
            break
    has2 = g2 is not None

    def body(*refs):
        if has2:
            w_ref, m_ref, v_ref, g_ref, g2_ref, go_ref, d_ref, mo_ref, vo_ref = refs
            grad = g_ref[...] + g2_ref[...].astype(F32)
        else:
            w_ref, m_ref, v_ref, g_ref, go_ref, d_ref, mo_ref, vo_ref = refs
            grad = g_ref[...]
        mn = ADAM_B1 * m_ref[...] + (1.0 - ADAM_B1) * grad
        vn = ADAM_B2 * v_ref[...] + (1.0 - ADAM_B2) * (grad * grad)
        m_hat = mn / (1.0 - ADAM_B1 ** ADAM_STEP)
        v_hat = vn / (1.0 - ADAM_B2 ** ADAM_STEP)
        go_ref[...] = grad
        d_ref[...] = -ADAM_LR * (m_hat / (jnp.sqrt(v_hat) + ADAM_EPS) + ADAM_WD * w_ref[...])
        mo_ref[...] = mn
        vo_ref[...] = vn

    blk = pl.BlockSpec((tr, cols), lambda i: (i, 0))
    ins = [w, m, v, g] + ([g2] if has2 else [])
    return pl.pallas_call(
        body, name=name, grid=(rows // tr,),
        in_specs=[blk] * len(ins), out_specs=[blk] * 4,
        out_shape=[jax.ShapeDtypeStruct((rows, cols), F32)] * 4,
        compiler_params=_params(("parallel",)),
    )(*ins)


_ORD_A = ("x", "y", "c")
_ORD_B = ("y", "x", "c")


def _to_slots(full, order, col_sharded):
    if col_sharded:
        r = full.shape[0]
        cc = full.shape[1] // 8
        g = full.reshape(r, 2, 2, 2, cc).transpose(1, 2, 3, 0, 4)
    else:
        r = full.shape[0] // 8
        cc = full.shape[1]
        g = full.reshape(2, 2, 2, r, cc)
    names = ("x", "y", "c")
    perm = tuple(names.index(a) for a in order)
    return g.transpose(perm + (3, 4))


def _rows128(a, rows):
    flat = a.reshape(-1)
    return jnp.pad(flat, (0, rows * 128 - flat.shape[0])).reshape(rows, 128)


def kernel(x, c, positions, w_ada, b_ada, g_pre, w_in, conv_w, w_conv_out, g_q, w_uq, g_kv, w_ukv, w_mla_out, w_out, g_post, loss_target, m_w_ada, m_b_ada, m_g_pre, m_w_in, m_conv_w, m_w_conv_out, m_g_q, m_w_uq, m_g_kv, m_w_ukv, m_w_mla_out, m_w_out, m_g_post, v_w_ada, v_b_ada, v_g_pre, v_w_in, v_conv_w, v_w_conv_out, v_g_q, v_w_uq, v_g_kv, v_w_ukv, v_w_mla_out, v_w_out, v_g_post):
    nb, seq, _ = x.shape
    t = nb * seq
    mx, my, mc = lax.axis_index("x"), lax.axis_index("y"), lax.axis_index("c")
    me = 4 * mx + 2 * my + mc
    co = {"x": mx, "y": my, "c": mc}

    x2 = x.reshape(t, D)
    tgt2 = loss_target.reshape(t, D)
    pos2 = positions.reshape(t, 1)

    packed = jnp.concatenate([c.reshape(2 * D // 128, 128), _rows128(conv_w[0], 8)], axis=0)
    gath = small_allgather(packed, "gather_cond")
    c_all = gath[:, :16].reshape(8 * nb, D)
    conv_full = gath[:, 16:19].reshape(8, 3, 128).transpose(1, 0, 2).reshape(3, D)
    conv_full8 = jnp.pad(conv_full, ((0, 5), (0, 0)))
    ada_cols = w_ada.shape[2]
    b_cols = lax.dynamic_slice(b_ada, (0, me * ada_cols), (1, ada_cols))
    mod_part = ada_fwd(c_all, w_ada[0], b_cols)
    mod_g = small_allgather(mod_part.reshape(8 * nb * ada_cols // 128, 128), "gather_mod")
    mod_all = mod_g.reshape(8, 8 * nb, ada_cols).transpose(1, 0, 2).reshape(8 * nb, 8 * ada_cols)
    mod = lax.dynamic_slice(mod_all, (me * nb, 0), (nb, 3 * D))
    shift = mod[:, 0:D].reshape(nb, 1, D)
    scale = mod[:, D:2 * D].reshape(nb, 1, D)
    gate = mod[:, 2 * D:3 * D].reshape(nb, 1, D)

    wt = w_in[0].T.astype(BF16)
    lo = lax.bitcast_convert_type(wt[:, :D // 2], jnp.uint16).astype(jnp.uint32)
    hi = lax.bitcast_convert_type(wt[:, D // 2:], jnp.uint16).astype(jnp.uint32)
    wt_bits = lax.bitcast_convert_type(lo | (hi << 16), F32)
    shards = [wt_bits, w_conv_out[0].astype(BF16), w_mla_out[0].astype(BF16), w_out[0].astype(BF16),
              w_uq[0].astype(BF16), w_ukv[0].astype(BF16)]
    q4 = D // 4
    plan = [(0, (0, q4), _ORD_A), (0, (q4, q4), _ORD_B), (1, None, _ORD_A), (2, None, _ORD_A),
            (3, None, _ORD_B), (4, None, _ORD_B), (5, None, _ORD_B)]
    gw = allgather_big(shards, plan, "gather_weights")
    wt_bits_all = gw[0].reshape(N_IN, D // 2)
    wco = gw[1].reshape(D, D)
    wmo = gw[2].reshape(D, D)
    wout = gw[3].reshape(D, D)
    wuq_full = gw[4].reshape(8, QL, 192).transpose(1, 0, 2)
    wuq_p = jnp.pad(wuq_full, ((0, 0), (0, 0), (0, DQK - 192))).reshape(QL, H * DQK)
    wukv = gw[5].reshape(8, KVL, 256).transpose(1, 0, 2).reshape(KVL, H * 256)

    inv_freq = ROPE_THETA ** (-jnp.arange(0, ROPE, 2, dtype=F32) / ROPE)
    invf = jnp.concatenate([inv_freq, inv_freq, jnp.zeros((128 - ROPE,), F32)]).reshape(1, 128)
    lane = np.arange(128)
    tabs = (invf,
            jnp.asarray(np.where(lane < HALF, -1.0, 0.0).reshape(1, 128), F32),
            jnp.asarray(np.where((lane >= HALF) & (lane < ROPE), 1.0, 0.0).reshape(1, 128), F32))

    h = prenorm_fwd(x2, scale, shift, g_pre, seq)
    proj, wt_p = proj_matmul(h, wt_bits_all)
    y = conv_fwd(proj, conv_full8, seq)
    q_rot, k_cat, kv, qn, kvn = mla_prep_fwd(proj, pos2, g_q, g_kv, wuq_p, wukv, tabs)
    attn, lse = flash_fwd(q_rot, k_cat, kv, nb, seq)
    o, ya, yb, m, do2, dout, dgate, dg_post, loss_part = tail_fwd(
        y, attn, proj, x2, tgt2, gate, g_post, wco, wmo, wout, seq)

    dproj, dya, dyb, dattn, dy = tail_bwd(do2, proj, ya, yb, attn, wout, wmo, wco)
    g_wout = grad_matmul(m, do2, "grad_w_out")
    g_wmo = grad_matmul(o, dyb, "grad_w_mla_out")
    g_wco = grad_matmul(y, dya, "grad_w_conv_out")
    dproj, dconv = conv_bwd(dproj, proj, dy, conv_full8, seq)
    dq_rot, dk, dv = flash_bwd(q_rot, k_cat, kv, attn, dattn, lse, nb, seq)
    dproj, dq, dkv, dg_q, dg_kv = mla_prep_bwd(dproj, proj, dq_rot, dk, dv, pos2, g_q, g_kv, wuq_p, wukv, tabs)
    g_wuq_p = grad_matmul(qn, dq, "grad_w_uq")
    g_wukv = grad_matmul(kvn, dkv, "grad_w_ukv")
    g_win_p = win_grad_matmul(h, dproj)
    dh = dh_matmul(dproj, wt_p)
    grad_x2, dshift, dscale, dg_pre = prenorm_bwd(dh, x2, dout, scale, g_pre, seq)

    dmod = jnp.concatenate([dshift, dscale, dgate], axis=2).reshape(nb * 3 * D // 128, 128)
    small = jnp.concatenate([
        dmod, _rows128(dg_pre, 8), _rows128(dg_post, 8), _rows128(dg_q, 8), _rows128(dg_kv, 8),
        dconv[0:3].reshape(24, 128), _rows128(loss_part, 8)], axis=0)
    small_g = small_allgather(small, "gather_small_grads")
    sums = slot_sum(small_g)
    dmod_all = small_g[:, 0:48].reshape(8 * nb, 3 * D)
    g_bada = (sums[0:24] + sums[24:48]).reshape(1, 3 * D)
    g_gpre = sums[48:56].reshape(1, D)
    g_gpost = sums[56:64].reshape(1, D)
    g_gq = sums[64:67].reshape(1, QL)
    g_gkv = sums[72:74].reshape(1, KVL)
    g_conv_full = sums[80:104].reshape(3, D)
    loss = sums[104, 0]
    g_conv = lax.dynamic_slice(g_conv_full, (0, me * 128), (3, 128))
    dmod_cols = lax.dynamic_slice(dmod_all, (0, me * ada_cols), (8 * nb, ada_cols))
    g_wada = ada_bwd(c_all, dmod_cols)

    g_wt = g_win_p.reshape(2, 2, 2, N_IN // 8, D)
    g_wuq = g_wuq_p.reshape(QL, H, DQK)[:, :, :192].reshape(QL, H * 192)
    rs_a = ("c", "y", "x")
    rs_b = ("c", "x", "y")
    flat = lambda s: s.reshape(2, 2, 2, -1, 128)
    rest_a = jnp.concatenate([flat(_to_slots(g_wco, rs_a, False)), flat(_to_slots(g_wmo, rs_a, False))], axis=3)
    rest_b = jnp.concatenate([flat(_to_slots(g_wout, rs_b, False)), flat(_to_slots(g_wuq, rs_b, True)),
                              flat(_to_slots(g_wukv, rs_b, True))], axis=3)
    ords = [rs_a, rs_a, rs_b, rs_b]
    hc = D // 2
    win_shape = (2, 2, N_IN // 8, hc)
    pick_w = lambda col: (lambda ref, cc: ref.at[:, :, 1 - cc["c"], :, pl.ds(col * hc, hc)])
    pick_h = lambda ref, cc: ref.at[1 - cc["c"]]
    r1 = exchange([g_wt, rest_a, g_wt, rest_b], ["c"] * 4, [pick_w(0), pick_h, pick_w(1), pick_h],
                  [win_shape, rest_a.shape[1:], win_shape, rest_b.shape[1:]], "rs_exchange_c")
    sel_xyc = jnp.stack([mx, my, mc]).astype(jnp.int32)
    sel1 = [jnp.stack([co[o[0]], co[o[1]]]).astype(jnp.int32) for o in ords]
    sel2 = [jnp.stack([co[o[2]]]).astype(jnp.int32) for o in ords]
    first = [rs_win_add_first(g_wt, r1[0], sel_xyc, 1, 0, "rs_add_first_0"),
             rs_add_first(rest_a, r1[1], sel1[1], "rs_add_first_1"),
             rs_win_add_first(g_wt, r1[2], sel_xyc, 0, 1, "rs_add_first_2"),
             rs_add_first(rest_b, r1[3], sel1[3], "rs_add_first_3")]
    keep1, send1 = zip(*first)
    r2 = exchange(list(send1), [o[1] for o in ords], [None] * 4, [s.shape for s in send1],
                  "rs_exchange_first_ici")
    keep2, send2 = zip(*[rs_add_second(keep1[a], r2[a], sel2[a], "rs_add_second_%d" % a) for a in range(4)])
    r3 = exchange(list(send2), [o[2] for o in ords], [None] * 4, [s.shape for s in send2],
                  "rs_exchange_second_ici")

    gk_win = jnp.concatenate([keep2[0], keep2[2]], axis=1).T
    gr_win = jnp.concatenate([r3[0], r3[2]], axis=1).T
    n_sq = D * 128 // 128
    unflat = lambda a, lo, shape: a[lo:lo + shape[0] * shape[1] // 128].reshape(shape)
    sq = (128, D)
    uq_s = (QL, 192)
    ukv_s = (KVL, 256)
    parts = {
        "w_conv_out": (unflat(keep2[1], 0, sq), unflat(r3[1], 0, sq)),
        "w_mla_out": (unflat(keep2[1], n_sq, sq), unflat(r3[1], n_sq, sq)),
        "w_out": (unflat(keep2[3], 0, sq), unflat(r3[3], 0, sq)),
        "w_uq": (unflat(keep2[3], n_sq, uq_s), unflat(r3[3], n_sq, uq_s)),
        "w_ukv": (unflat(keep2[3], n_sq + QL * 192 // 128, ukv_s), unflat(r3[3], n_sq + QL * 192 // 128, ukv_s)),
        "w_in": (gk_win, gr_win),
        "w_ada": (g_wada, None),
    }

    weights = {"w_ada": (w_ada, m_w_ada, v_w_ada), "w_in": (w_in, m_w_in, v_w_in),
               "w_conv_out": (w_conv_out, m_w_conv_out, v_w_conv_out), "w_uq": (w_uq, m_w_uq, v_w_uq),
               "w_ukv": (w_ukv, m_w_ukv, v_w_ukv), "w_mla_out": (w_mla_out, m_w_mla_out, v_w_mla_out),
               "w_out": (w_out, m_w_out, v_w_out)}
    res = {}
    for nm, (wv, mv, vv) in weights.items():
        ga, gb = parts[nm]
        outs = adamw(wv[0], mv[0], vv[0], ga, gb, "adamw_" + nm)
        res[nm] = [o_[None] for o_ in outs]

    def pack(b_, gp_, gpo_, gq_, gkv_, cw_):
        return jnp.concatenate([_rows128(b_, 24), _rows128(gp_, 8), _rows128(gpo_, 8), _rows128(gq_, 8),
                                _rows128(gkv_, 8), _rows128(cw_, 8)], axis=0)

    sw = pack(b_ada, g_pre, g_post, g_q, g_kv, conv_w)
    sm = pack(m_b_ada, m_g_pre, m_g_post, m_g_q, m_g_kv, m_conv_w)
    sv = pack(v_b_ada, v_g_pre, v_g_post, v_g_q, v_g_kv, v_conv_w)
    sg = pack(g_bada, g_gpre, g_gpost, g_gq, g_gkv, g_conv)
    small_out = adamw(sw, sm, sv, sg, None, "adamw_small")

    def unpack(a):
        return {"b_ada": a[0:24].reshape(1, 3 * D), "g_pre": a[24:32].reshape(1, D),
                "g_post": a[32:40].reshape(1, D), "g_q": a[40:43].reshape(1, QL),
                "g_kv": a[48:50].reshape(1, KVL), "conv_w": a[56:59].reshape(-1)[:3 * 128].reshape(1, 3, 128)}

    for nm in ("b_ada", "g_pre", "g_post", "g_q", "g_kv", "conv_w"):
        res[nm] = [unpack(a)[nm] for a in small_out]

    order = ["w_ada", "b_ada", "g_pre", "w_in", "conv_w", "w_conv_out", "g_q", "w_uq", "g_kv", "w_ukv",
             "w_mla_out", "w_out", "g_post"]
    out = [loss, grad_x2.reshape(nb, seq, D)]
    for k_ in range(4):
        out += [res[nm][k_] for nm in order]
    return tuple(out)
```

```python
import functools

import numpy as np
import jax
import jax.numpy as jnp
from jax import lax
from jax.experimental import pallas as pl
from jax.experimental.pallas import tpu as pltpu

F32 = jnp.float32
BF16 = jnp.bfloat16
MESH = pl.DeviceIdType.MESH

D = 1024
H = 8
QL = 384
KVL = 256
ROPE = 64
HALF = ROPE // 2
DQK = 256
DV = 128
NSEG = 8
NP = NSEG * D
EPS = 1e-6
ROPE_THETA = 10000.0
SM_SCALE = (128 + ROPE) ** -0.5
LOG2E = 1.4426950408889634
LN2 = 0.6931471805599453
FLASH_TQ = 512

SEG_BZ, SEG_GA, SEG_GB, SEG_LAT, SEG_V = 0, 1, 2, 3, 4

ADAM_LR = 0.001
ADAM_B1 = 0.9
ADAM_B2 = 0.999
ADAM_EPS = 1e-08
ADAM_WD = 0.01
ADAM_STEP = 10

VMEM_LIMIT = 56 * 1024 * 1024


def _params(sem=None, vmem=VMEM_LIMIT):
    kw = dict(vmem_limit_bytes=vmem)
    if sem is not None:
        kw["dimension_semantics"] = sem
    return pltpu.CompilerParams(**kw)


def _sig(v):
    return 1.0 / (1.0 + jnp.exp(-v))


def _dot(a, b):
    return jnp.dot(a, b, preferred_element_type=F32)


def _dot_nt(a, b):
    return lax.dot_general(a, b, (((1,), (1,)), ((), ())), preferred_element_type=F32)


def _dot_tn(a, b):
    return lax.dot_general(a, b, (((0,), (0,)), ((), ())), preferred_element_type=F32)


_AXIS_POS = {"x": 0, "y": 1, "c": 2}


def _coords():
    return lax.axis_index("x"), lax.axis_index("y"), lax.axis_index("c")


def _partner(axis):
    p = list(_coords())
    p[_AXIS_POS[axis]] = 1 - p[_AXIS_POS[axis]]
    return tuple(p)


def small_allgather(v, name):
    rows = v.shape[0]

    def body(v_ref, out_ref, send_sems, recv_sems):
        x, y, c = _coords()
        me = 4 * x + 2 * y + c
        out_ref[me] = v_ref[...]
        copies = []
        for k in range(1, 8):
            peer = (1 - x if k & 4 else x, 1 - y if k & 2 else y, 1 - c if k & 1 else c)
            cp = pltpu.make_async_remote_copy(
                src_ref=v_ref, dst_ref=out_ref.at[me],
                send_sem=send_sems.at[k - 1], recv_sem=recv_sems.at[k - 1],
                device_id=peer, device_id_type=MESH)
            cp.start()
            copies.append(cp)
        for cp in copies:
            cp.wait()

    return pl.pallas_call(
        body, name=name,
        out_shape=jax.ShapeDtypeStruct((8, rows, 128), F32),
        in_specs=[pl.BlockSpec(memory_space=pltpu.VMEM)],
        out_specs=pl.BlockSpec(memory_space=pltpu.VMEM),
        scratch_shapes=[pltpu.SemaphoreType.DMA((7,)), pltpu.SemaphoreType.DMA((7,))],
    )(v)


def allgather_big(arrs, plan, name):
    n = len(arrs)
    m = len(plan)

    def body(*refs):
        ins, outs = refs[:n], refs[n:2 * n]
        send_sems, recv_sems, loc_sems = refs[2 * n:]
        x, y, c = _coords()
        co = {"x": x, "y": y, "c": c}

        def window(ref, lead, cols):
            tail = ref.shape[len(lead):]
            idx = tuple(lead) + (slice(None),) * (len(tail) - 1)
            idx += (slice(None),) if cols is None else (pl.ds(cols[0], cols[1]),)
            return ref.at[idx]

        def held(e, free):
            i, cols, _ = plan[e]
            lead = [slice(None) if ax in free else co[ax] for ax in ("x", "y", "c")]
            return window(outs[i], lead, cols)

        def rcopy(e, stage, src, dst, axis):
            return pltpu.make_async_remote_copy(
                src_ref=src, dst_ref=dst,
                send_sem=send_sems.at[e, stage], recv_sem=recv_sems.at[e, stage],
                device_id=_partner(axis), device_id_type=MESH)

        local, stages = [], [[], [], []]
        for e, (i, cols, order) in enumerate(plan):
            mine = window(ins[i], [], cols)
            lc = pltpu.make_async_copy(mine, held(e, ()), loc_sems.at[e])
            lc.start()
            local.append(lc)
            cp = rcopy(e, 0, mine, held(e, ()), order[0])
            cp.start()
            stages[0].append(cp)
        for s in (1, 2):
            for e, (i, cols, order) in enumerate(plan):
                stages[s - 1][e].wait_recv()
                if s == 1:
                    local[e].wait()
                blk = held(e, order[:s])
                cp = rcopy(e, s, blk, blk, order[s])
                cp.start()
                stages[s].append(cp)
        for e in range(m):
            stages[2][e].wait_recv()
        for e in range(m):
            for s in range(3):
                stages[s][e].wait_send()

    any_spec = pl.BlockSpec(memory_space=pl.ANY)
    return pl.pallas_call(
        body, name=name,
        out_shape=[jax.ShapeDtypeStruct((2, 2, 2) + a.shape, a.dtype) for a in arrs],
        in_specs=[any_spec] * n,
        out_specs=[any_spec] * n,
        scratch_shapes=[pltpu.SemaphoreType.DMA((m, 3)), pltpu.SemaphoreType.DMA((m, 3)),
                        pltpu.SemaphoreType.DMA((m,))],
    )(*arrs)


def exchange(arrs, axes, picks, out_shapes, name):
    n = len(arrs)

    def body(*refs):
        ins, outs = refs[:n], refs[n:2 * n]
        send_sems, recv_sems = refs[2 * n:]
        x, y, c = _coords()
        co = {"x": x, "y": y, "c": c}
        copies = []
        for a in range(n):
            src = ins[a] if picks[a] is None else picks[a](ins[a], co)
            cp = pltpu.make_async_remote_copy(
                src_ref=src, dst_ref=outs[a],
                send_sem=send_sems.at[a], recv_sem=recv_sems.at[a],
                device_id=_partner(axes[a]), device_id_type=MESH)
            cp.start()
            copies.append(cp)
        for cp in copies:
            cp.wait()

    any_spec = pl.BlockSpec(memory_space=pl.ANY)
    return pl.pallas_call(
        body, name=name,
        out_shape=[jax.ShapeDtypeStruct(s, a.dtype) for s, a in zip(out_shapes, arrs)],
        in_specs=[any_spec] * n,
        out_specs=[any_spec] * n,
        scratch_shapes=[pltpu.SemaphoreType.DMA((n,)), pltpu.SemaphoreType.DMA((n,))],
    )(*arrs)


_HBM = pl.BlockSpec(memory_space=pltpu.HBM)
_SEM = pl.BlockSpec(memory_space=pltpu.SEMAPHORE)


def _swap_copies(srcs, lands, send_sems, recv_sems, axes, picks):
    x, y, c = _coords()
    co = {"x": x, "y": y, "c": c}
    return [pltpu.make_async_remote_copy(
        src_ref=srcs[a] if picks[a] is None else picks[a](srcs[a], co), dst_ref=lands[a],
        send_sem=send_sems.at[a], recv_sem=recv_sems.at[a],
        device_id=_partner(axes[a]), device_id_type=MESH) for a in range(len(srcs))]


def swap_start(arrs, which, axes, picks, out_shapes, name):
    ns, n = len(arrs), len(which)

    def body(*refs):
        srcs, lands = refs[:ns], refs[ns:ns + n]
        send_sems, recv_sems = refs[ns + n:ns + n + 2]
        token = refs[-1]
        for cp in _swap_copies([srcs[i] for i in which], lands, send_sems, recv_sems, axes, picks):
            cp.start()
        token[...] = jnp.zeros_like(token)

    lands = [lax.empty(s, arrs[i].dtype) for s, i in zip(out_shapes, which)]
    ops = [pltpu.with_memory_space_constraint(a, pltpu.HBM) for a in list(arrs) + lands]
    out = pl.pallas_call(
        body, name=name,
        out_shape=[pltpu.SemaphoreType.DMA((n,)), pltpu.SemaphoreType.DMA((n,))]
        + [pltpu.HBM(o.shape, o.dtype) for o in ops] + [jax.ShapeDtypeStruct((8, 128), F32)],
        in_specs=[_HBM] * (ns + n),
        out_specs=[_SEM, _SEM] + [_HBM] * (ns + n) + [pl.BlockSpec(memory_space=pltpu.VMEM)],
        input_output_aliases={i: 2 + i for i in range(ns + n)},
        compiler_params=pltpu.CompilerParams(has_side_effects=pltpu.SideEffectType.DATAFLOW_SIDE_EFFECTING),
    )(*ops)
    return out[:-1], out[-1]


def swap_wait(state, after, which, axes, picks, name):
    n = len(which)
    ns = len(state) - 2 - n

    def body(*refs):
        srcs, lands = refs[:ns], refs[ns:ns + n]
        send_sems, recv_sems = refs[ns + n:ns + n + 2]
        for cp in _swap_copies([srcs[i] for i in which], lands, send_sems, recv_sems, axes, picks):
            cp.wait_send()
            cp.wait_recv()

    thru = list(state[2:])
    out = pl.pallas_call(
        body, name=name,
        out_shape=[pltpu.HBM(o.shape, o.dtype) for o in thru],
        in_specs=[_HBM] * (ns + n) + [_SEM, _SEM, pl.BlockSpec(memory_space=pl.ANY)],
        out_specs=[_HBM] * (ns + n),
        input_output_aliases={i: i for i in range(ns + n)},
        compiler_params=pltpu.CompilerParams(has_side_effects=pltpu.SideEffectType.DATAFLOW_SIDE_EFFECTING),
    )(*thru, state[0], state[1], after)
    return out[:ns], out[ns:]


def rs_win_add_first(g, r, sel, next_dim, col, name):
    rows, cols = r.shape[2:]

    def body(sel_ref, gk_ref, rk_ref, gs_ref, rs_ref, keep_ref, send_ref):
        keep_ref[...] = gk_ref[...] + rk_ref[...]
        send_ref[...] = (gs_ref[...] + rs_ref[...]).astype(BF16)

    def g_map(flip):
        def f(j, s):
            nxt = 1 - s[next_dim] if flip else s[next_dim]
            return (nxt, j, s[2], 0, col) if next_dim == 0 else (j, nxt, s[2], 0, col)
        return f

    def r_map(flip):
        def f(j, s):
            nxt = 1 - s[next_dim] if flip else s[next_dim]
            return (nxt, j, 0, 0) if next_dim == 0 else (j, nxt, 0, 0)
        return f

    gblk = (None, None, None, rows, cols)
    rblk = (None, None, rows, cols)
    oblk = (None, rows, cols)
    return pl.pallas_call(
        body, name=name,
        grid_spec=pltpu.PrefetchScalarGridSpec(
            num_scalar_prefetch=1, grid=(2,),
            in_specs=[pl.BlockSpec(gblk, g_map(False)), pl.BlockSpec(rblk, r_map(False)),
                      pl.BlockSpec(gblk, g_map(True)), pl.BlockSpec(rblk, r_map(True))],
            out_specs=[pl.BlockSpec(oblk, lambda j, s: (j, 0, 0)),
                       pl.BlockSpec(oblk, lambda j, s: (j, 0, 0))]),
        out_shape=[jax.ShapeDtypeStruct((2, rows, cols), F32),
                   jax.ShapeDtypeStruct((2, rows, cols), BF16)],
        compiler_params=_params(),
    )(sel, g, r, g, r)


def rs_add_first(g, r, sel, name):
    _, _, _, rows, cols = g.shape
    tr = rows // 2

    def body(sel_ref, gk_ref, rk_ref, gs_ref, rs_ref, keep_ref, send_ref):
        keep_ref[...] = gk_ref[...] + rk_ref[...]
        send_ref[...] = (gs_ref[...] + rs_ref[...]).astype(BF16)

    blk = (None, None, None, tr, cols)
    rblk = (None, None, tr, cols)
    oblk = (None, tr, cols)
    return pl.pallas_call(
        body, name=name,
        grid_spec=pltpu.PrefetchScalarGridSpec(
            num_scalar_prefetch=1, grid=(2, 2),
            in_specs=[
                pl.BlockSpec(blk, lambda j, i, s: (s[0], s[1], j, i, 0)),
                pl.BlockSpec(rblk, lambda j, i, s: (s[1], j, i, 0)),
                pl.BlockSpec(blk, lambda j, i, s: (s[0], 1 - s[1], j, i, 0)),
                pl.BlockSpec(rblk, lambda j, i, s: (1 - s[1], j, i, 0)),
            ],
            out_specs=[pl.BlockSpec(oblk, lambda j, i, s: (j, i, 0)),
                       pl.BlockSpec(oblk, lambda j, i, s: (j, i, 0))]),
        out_shape=[jax.ShapeDtypeStruct((2, rows, cols), F32),
                   jax.ShapeDtypeStruct((2, rows, cols), BF16)],
        compiler_params=_params(),
    )(sel, g, r, g, r)


def rs_add_second(k, r, sel, name):
    _, rows, cols = k.shape
    tr = rows // 2 if rows % 32 == 0 else rows
    nt = rows // tr

    def body(sel_ref, kk_ref, rk_ref, ks_ref, rs_ref, keep_ref, send_ref):
        keep_ref[...] = kk_ref[...] + rk_ref[...].astype(F32)
        send_ref[...] = (ks_ref[...] + rs_ref[...].astype(F32)).astype(BF16)

    blk = (None, tr, cols)
    oblk = (tr, cols)
    return pl.pallas_call(
        body, name=name,
        grid_spec=pltpu.PrefetchScalarGridSpec(
            num_scalar_prefetch=1, grid=(nt,),
            in_specs=[
                pl.BlockSpec(blk, lambda i, s: (s[0], i, 0)),
                pl.BlockSpec(blk, lambda i, s: (s[0], i, 0)),
                pl.BlockSpec(blk, lambda i, s: (1 - s[0], i, 0)),
                pl.BlockSpec(blk, lambda i, s: (1 - s[0], i, 0)),
            ],
            out_specs=[pl.BlockSpec(oblk, lambda i, s: (i, 0)),
                       pl.BlockSpec(oblk, lambda i, s: (i, 0))]),
        out_shape=[jax.ShapeDtypeStruct((rows, cols), F32),
                   jax.ShapeDtypeStruct((rows, cols), BF16)],
        compiler_params=_params(),
    )(sel, k, r, k, r)


SEG_ROWS = (4800, 5824, 6848, 4096, 0, 1024, 2048, 3072)
LAT_ROWS = QL + KVL + ROPE
N_IN = 7872


def _seg_row(j):
    return pl.multiple_of(jnp.where(j < 3, 4800 + 1024 * j, jnp.where(j == 3, 4096, (j - 4) * 1024)), 8)


def proj_matmul(h, wt_bits):
    t = h.shape[0]
    tm = min(1024, t)

    def body(h_ref, w_hbm, o_ref, wt_ref, buf, sem):
        j = pl.program_id(0)

        @pl.when(pl.program_id(1) == 0)
        def _():
            cp = pltpu.make_async_copy(w_hbm.at[pl.ds(_seg_row(j), D)], buf, sem)
            cp.start()
            cp.wait()
            bits = pltpu.bitcast(buf[...], jnp.uint32)
            row = lax.broadcasted_iota(jnp.int32, (D, D // 2), 0)
            live = jnp.logical_or(j != SEG_LAT, row < LAT_ROWS)
            lo = pltpu.bitcast(bits << 16, F32)
            hi = pltpu.bitcast(bits & jnp.uint32(0xFFFF0000), F32)
            wt_ref[:, :D // 2] = jnp.where(live, lo, 0.0).astype(BF16)
            wt_ref[:, D // 2:] = jnp.where(live, hi, 0.0).astype(BF16)

        o_ref[...] = _dot_nt(h_ref[...], wt_ref[...]).astype(BF16)

    return pl.pallas_call(
        body, name="proj_matmul", grid=(NSEG, t // tm),
        in_specs=[pl.BlockSpec((tm, D), lambda j, i: (i, 0)),
                  pl.BlockSpec(memory_space=pl.ANY)],
        out_specs=[pl.BlockSpec((None, tm, D), lambda j, i: (j, i, 0)),
                   pl.BlockSpec((D, D), lambda j, i: (j, 0))],
        out_shape=[jax.ShapeDtypeStruct((NSEG, t, D), BF16), jax.ShapeDtypeStruct((NP, D), BF16)],
        scratch_shapes=[pltpu.VMEM((D, D // 2), F32), pltpu.SemaphoreType.DMA],
        compiler_params=_params(("arbitrary", "arbitrary")),
    )(h, wt_bits)


def dh_matmul(dproj, wt, token):
    t = dproj.shape[1]
    tm = min(1024, t)

    def body(d_ref, w_ref, tok_ref, o_ref, acc_ref):
        k = pl.program_id(1)

        @pl.when(k == 0)
        def _():
            acc_ref[...] = jnp.zeros_like(acc_ref)

        acc_ref[...] += _dot(d_ref[...], w_ref[...])

        @pl.when(k == NSEG - 1)
        def _():
            o_ref[...] = acc_ref[...]

    return pl.pallas_call(
        body, name="dh_matmul", grid=(t // tm, NSEG),
        in_specs=[pl.BlockSpec((None, tm, D), lambda i, k: (k, i, 0)),
                  pl.BlockSpec((D, D), lambda i, k: (k, 0)),
                  pl.BlockSpec((8, 128), lambda i, k: (0, 0))],
        out_specs=pl.BlockSpec((tm, D), lambda i, k: (i, 0)),
        out_shape=jax.ShapeDtypeStruct((t, D), F32),
        scratch_shapes=[pltpu.VMEM((tm, D), F32)],
        compiler_params=_params(("parallel", "arbitrary")),
    )(dproj, wt, token)


def win_grad_matmul(h, dproj):
    t = h.shape[0]
    tk = min(1024, t)
    nk = t // tk

    def body(h_ref, d_ref, o_hbm, acc_ref, sem):
        j = pl.program_id(0)
        k = pl.program_id(1)

        @pl.when(k == 0)
        def _():
            acc_ref[...] = jnp.zeros_like(acc_ref)

        acc_ref[...] += _dot_tn(d_ref[...], h_ref[...])

        @pl.when(jnp.logical_and(k == nk - 1, j != SEG_LAT))
        def _():
            cp = pltpu.make_async_copy(acc_ref, o_hbm.at[pl.ds(_seg_row(j), D)], sem)
            cp.start()
            cp.wait()

        @pl.when(jnp.logical_and(k == nk - 1, j == SEG_LAT))
        def _():
            cp = pltpu.make_async_copy(acc_ref.at[pl.ds(0, LAT_ROWS)],
                                       o_hbm.at[pl.ds(SEG_ROWS[SEG_LAT], LAT_ROWS)], sem)
            cp.start()
            cp.wait()

    return pl.pallas_call(
        body, name="win_grad_matmul", grid=(NSEG, nk),
        in_specs=[pl.BlockSpec((tk, D), lambda j, k: (k, 0)),
                  pl.BlockSpec((None, tk, D), lambda j, k: (j, k, 0))],
        out_specs=pl.BlockSpec(memory_space=pl.ANY),
        out_shape=jax.ShapeDtypeStruct((N_IN, D), F32),
        scratch_shapes=[pltpu.VMEM((D, D), F32), pltpu.SemaphoreType.DMA],
        compiler_params=_params(("arbitrary", "arbitrary")),
    )(h, dproj)


def grad_matmul(a, b, name):
    t, m = a.shape
    n = b.shape[1]
    tk = min(1024, t)
    nk = t // tk

    def body(a_ref, b_ref, o_ref, acc_ref):
        k = pl.program_id(0)

        @pl.when(k == 0)
        def _():
            acc_ref[...] = jnp.zeros_like(acc_ref)

        acc_ref[...] += _dot_tn(a_ref[...], b_ref[...])

        @pl.when(k == nk - 1)
        def _():
            o_ref[...] = acc_ref[...]

    return pl.pallas_call(
        body, name=name, grid=(nk,),
        in_specs=[pl.BlockSpec((tk, m), lambda k: (k, 0)),
                  pl.BlockSpec((tk, n), lambda k: (k, 0))],
        out_specs=pl.BlockSpec((m, n), lambda k: (0, 0)),
        out_shape=jax.ShapeDtypeStruct((m, n), F32),
        scratch_shapes=[pltpu.VMEM((m, n), F32)],
        compiler_params=_params(("arbitrary",)),
    )(a, b)


def ada_fwd(c_all, w_ada, b_cols):
    def body(c_ref, w_ref, b_ref, o_ref):
        o_ref[...] = _dot(c_ref[...].astype(BF16), w_ref[...].astype(BF16)) + b_ref[...]

    return pl.pallas_call(
        body, name="ada_fwd",
        out_shape=jax.ShapeDtypeStruct((c_all.shape[0], w_ada.shape[1]), F32),
        compiler_params=_params(),
    )(c_all, w_ada, b_cols)


def ada_bwd(c_all, dmod_cols):
    def body(c_ref, d_ref, o_ref):
        o_ref[...] = _dot_tn(c_ref[...].astype(BF16), d_ref[...].astype(BF16))

    return pl.pallas_call(
        body, name="ada_bwd",
        out_shape=jax.ShapeDtypeStruct((c_all.shape[1], dmod_cols.shape[1]), F32),
        compiler_params=_params(),
    )(c_all, dmod_cols)


def slot_sum(g):
    def body(g_ref, o_ref):
        acc = g_ref[0]
        for s in range(1, 8):
            acc = acc + g_ref[s]
        o_ref[...] = acc

    return pl.pallas_call(
        body, name="slot_sum",
        out_shape=jax.ShapeDtypeStruct(g.shape[1:], F32),
    )(g)


def prenorm_fwd(x2, scale, shift, g_pre, seq):
    t = x2.shape[0]
    tm = min(512, seq)
    tpb = seq // tm

    def body(x_ref, sc_ref, sh_ref, g_ref, h_ref):
        xv = x_ref[...]
        r = lax.rsqrt(jnp.mean(xv * xv, axis=-1, keepdims=True) + EPS)
        hv = (xv * r * g_ref[...]) * (1.0 + sc_ref[...]) + sh_ref[...]
        h_ref[...] = hv.astype(BF16)

    per_batch = pl.BlockSpec((None, 1, D), lambda i: (i // tpb, 0, 0))
    return pl.pallas_call(
        body, name="prenorm_fwd", grid=(t // tm,),
        in_specs=[pl.BlockSpec((tm, D), lambda i: (i, 0)), per_batch, per_batch,
                  pl.BlockSpec((1, D), lambda i: (0, 0))],
        out_specs=pl.BlockSpec((tm, D), lambda i: (i, 0)),
        out_shape=jax.ShapeDtypeStruct((t, D), BF16),
        compiler_params=_params(("parallel",)),
    )(x2, scale, shift, g_pre)


def prenorm_bwd(dh, x2, dout, scale, g_pre, seq, token):
    t = x2.shape[0]
    nb = t // seq
    tm = min(512, seq)
    tpb = seq // tm

    def body(dh_ref, x_ref, do_ref, sc_ref, g_ref, tok_ref, gx_ref, dsh_ref, dsc_ref, dg_ref):
        i = pl.program_id(0)
        xv = x_ref[...]
        dhv = dh_ref[...]
        g = g_ref[...]
        r = lax.rsqrt(jnp.mean(xv * xv, axis=-1, keepdims=True) + EPS)
        nrm = xv * r
        dxn = dhv * (1.0 + sc_ref[...])
        dn = dxn * g
        dx = r * (dn - nrm * jnp.mean(dn * nrm, axis=-1, keepdims=True))
        gx_ref[...] = dx + do_ref[...]

        @pl.when(i % tpb == 0)
        def _():
            dsh_ref[...] = jnp.zeros_like(dsh_ref)
            dsc_ref[...] = jnp.zeros_like(dsc_ref)

        @pl.when(i == 0)
        def _():
            dg_ref[...] = jnp.zeros_like(dg_ref)

        dsh_ref[...] += jnp.sum(dhv, axis=0, keepdims=True)
        dsc_ref[...] += jnp.sum(dhv * (nrm * g), axis=0, keepdims=True)
        dg_ref[...] += jnp.sum(dxn * nrm, axis=0, keepdims=True)

    row = pl.BlockSpec((tm, D), lambda i: (i, 0))
    per_batch = pl.BlockSpec((None, 1, D), lambda i: (i // tpb, 0, 0))
    vec = pl.BlockSpec((1, D), lambda i: (0, 0))
    return pl.pallas_call(
        body, name="prenorm_bwd", grid=(t // tm,),
        in_specs=[row, row, row, per_batch, vec, pl.BlockSpec((8, 128), lambda i: (0, 0))],
        out_specs=[row, per_batch, per_batch, vec],
        out_shape=[jax.ShapeDtypeStruct((t, D), F32),
                   jax.ShapeDtypeStruct((nb, 1, D), F32),
                   jax.ShapeDtypeStruct((nb, 1, D), F32),
                   jax.ShapeDtypeStruct((1, D), F32)],
        compiler_params=_params(("arbitrary",)),
    )(dh, x2, dout, scale, g_pre, token)


CONV_TC = 128


def _shift_down(u, k, rows):
    idx = lax.broadcasted_iota(jnp.int32, u.shape, 0)
    return jnp.where(idx >= k, pltpu.roll(u, k, 0), 0.0)


def _shift_up(u, k, rows):
    idx = lax.broadcasted_iota(jnp.int32, u.shape, 0)
    return jnp.where(idx < rows - k, pltpu.roll(u, rows - k, 0), 0.0)


def conv_fwd(proj, conv_w, seq):
    t = proj.shape[1]
    nb = t // seq

    def body(p_ref, w_ref, y_ref):
        av = p_ref[0].astype(F32)
        ab = p_ref[1].astype(F32)
        ac = p_ref[2].astype(F32)
        az = p_ref[3].astype(F32)
        w = w_ref[...]
        u = ac * av
        y1 = _shift_down(u, 2, seq) * w[0:1] + _shift_down(u, 1, seq) * w[1:2] + u * w[2:3]
        y_ref[...] = (ab * y1 * (az * _sig(az))).astype(BF16)

    return pl.pallas_call(
        body, name="conv_fwd", grid=(nb, D // CONV_TC),
        in_specs=[pl.BlockSpec((4, seq, CONV_TC), lambda b, ci: (1, b, ci)),
                  pl.BlockSpec((8, CONV_TC), lambda b, ci: (0, ci))],
        out_specs=pl.BlockSpec((seq, CONV_TC), lambda b, ci: (b, ci)),
        out_shape=jax.ShapeDtypeStruct((t, D), BF16),
        compiler_params=_params(("parallel", "parallel")),
    )(proj, conv_w)


def conv_bwd(dproj, proj, dy, conv_w, seq):
    t = proj.shape[1]
    nb = t // seq

    def body(dp_in_ref, p_ref, dy_ref, w_ref, dp_ref, dw_ref):
        b = pl.program_id(1)
        av = p_ref[0].astype(F32)
        ab = p_ref[1].astype(F32)
        ac = p_ref[2].astype(F32)
        az = p_ref[3].astype(F32)
        dyv = dy_ref[...].astype(F32)
        w = w_ref[...]
        u = ac * av
        u1 = _shift_down(u, 1, seq)
        u2 = _shift_down(u, 2, seq)
        y1 = u2 * w[0:1] + u1 * w[1:2] + u * w[2:3]
        sz = _sig(az)
        silu = az * sz
        dy1 = dyv * ab * silu
        du = dy1 * w[2:3] + _shift_up(dy1, 1, seq) * w[1:2] + _shift_up(dy1, 2, seq) * w[0:1]
        dp_ref[0] = (du * ac).astype(BF16)
        dp_ref[1] = (dyv * y1 * silu).astype(BF16)
        dp_ref[2] = (du * av).astype(BF16)
        dp_ref[3] = (dyv * ab * y1 * (sz * (1.0 + az * (1.0 - sz)))).astype(BF16)

        @pl.when(b == 0)
        def _():
            dw_ref[...] = jnp.zeros_like(dw_ref)

        dw_ref[0:1, :] += jnp.sum(dy1 * u2, axis=0, keepdims=True)
        dw_ref[1:2, :] += jnp.sum(dy1 * u1, axis=0, keepdims=True)
        dw_ref[2:3, :] += jnp.sum(dy1 * u, axis=0, keepdims=True)

    return pl.pallas_call(
        body, name="conv_bwd", grid=(D // CONV_TC, nb),
        in_specs=[pl.BlockSpec(memory_space=pl.ANY),
                  pl.BlockSpec((4, seq, CONV_TC), lambda ci, b: (1, b, ci)),
                  pl.BlockSpec((seq, CONV_TC), lambda ci, b: (b, ci)),
                  pl.BlockSpec((8, CONV_TC), lambda ci, b: (0, ci))],
        out_specs=[pl.BlockSpec((4, seq, CONV_TC), lambda ci, b: (1, b, ci)),
                   pl.BlockSpec((8, CONV_TC), lambda ci, b: (0, ci))],
        out_shape=[jax.ShapeDtypeStruct(dproj.shape, BF16),
                   jax.ShapeDtypeStruct((8, D), F32)],
        input_output_aliases={0: 0},
        compiler_params=_params(("parallel", "arbitrary")),
    )(dproj, proj, dy, conv_w)


def _rope_tables(pos_ref, invf_ref, ma_ref, mb_ref):
    ang = pos_ref[...].astype(F32) * invf_ref[...]
    cs = jnp.cos(ang)
    sn = jnp.sin(ang)
    return cs, sn * ma_ref[...], sn * mb_ref[...]


def _head_tables(cs, sa, sb):
    one = jnp.ones_like(cs)
    zero = jnp.zeros_like(cs)
    return (jnp.tile(jnp.concatenate([one, cs], axis=1), (1, H)),
            jnp.tile(jnp.concatenate([zero, sa], axis=1), (1, H)),
            jnp.tile(jnp.concatenate([zero, sb], axis=1), (1, H)))


def _rotate(v, cs, sa, sb, sign):
    width = v.shape[1]
    return v * cs + sign * (pltpu.roll(v, width - HALF, 1) * sa + pltpu.roll(v, HALF, 1) * sb)


MLA_TM = 256


def mla_prep_fwd(proj, pos, g_q, g_kv, wuq, wukv, tabs):
    t = proj.shape[1]
    tm = min(MLA_TM, t)

    def body(lat_ref, pos_ref, gq_ref, gkv_ref, wuq_ref, wukv_ref, invf_ref, ma_ref, mb_ref,
             q_ref, k_ref, kv_ref, qn_ref, kvn_ref):
        lat = lat_ref[...].astype(F32)
        ql = lat[:, :QL]
        kl = lat[:, QL:QL + KVL]
        kr = lat[:, QL + KVL:QL + KVL + 128]
        qn = (ql * lax.rsqrt(jnp.mean(ql * ql, axis=-1, keepdims=True) + EPS) * gq_ref[...]).astype(BF16)
        kvn = (kl * lax.rsqrt(jnp.mean(kl * kl, axis=-1, keepdims=True) + EPS) * gkv_ref[...]).astype(BF16)
        qn_ref[...] = qn
        kvn_ref[...] = kvn
        cs, sa, sb = _rope_tables(pos_ref, invf_ref, ma_ref, mb_ref)
        hc, ha, hb = _head_tables(cs, sa, sb)
        q = _dot(qn, wuq_ref[...])
        q_ref[...] = (_rotate(q, hc, ha, hb, 1.0) * (SM_SCALE * LOG2E)).astype(BF16)
        kv = _dot(kvn, wukv_ref[...]).astype(BF16)
        kv_ref[...] = kv
        kpe = _rotate(kr, cs, sa, sb, 1.0).astype(BF16)
        for hh in range(H):
            k_ref[:, hh * DQK:hh * DQK + 128] = kv[:, hh * DQK:hh * DQK + 128]
            k_ref[:, hh * DQK + 128:(hh + 1) * DQK] = kpe

    row = lambda w: pl.BlockSpec((tm, w), lambda i: (i, 0))
    const = lambda a: pl.BlockSpec(a.shape, lambda i: (0,) * a.ndim)
    return pl.pallas_call(
        body, name="mla_prep_fwd", grid=(t // tm,),
        in_specs=[pl.BlockSpec((None, tm, D), lambda i: (SEG_LAT, i, 0)), row(1),
                  const(g_q), const(g_kv), const(wuq), const(wukv)] + [const(a) for a in tabs],
        out_specs=[row(H * DQK), row(H * DQK), row(H * DQK), row(QL), row(KVL)],
        out_shape=[jax.ShapeDtypeStruct((t, H * DQK), BF16)] * 3
        + [jax.ShapeDtypeStruct((t, QL), BF16), jax.ShapeDtypeStruct((t, KVL), BF16)],
        compiler_params=_params(("parallel",)),
    )(proj, pos, g_q, g_kv, wuq, wukv, *tabs)


def mla_prep_bwd(dproj, proj, dq_rot, dk, dv, pos, g_q, g_kv, wuq, wukv, tabs):
    t = proj.shape[1]
    tm = min(MLA_TM, t)

    def body(dp_in_ref, lat_ref, dqr_ref, dk_ref, dv_ref, pos_ref, gq_ref, gkv_ref, wuq_ref, wukv_ref,
             invf_ref, ma_ref, mb_ref, dp_ref, dq_ref, dkv_ref, dgq_ref, dgkv_ref):
        i = pl.program_id(0)
        lat = lat_ref[...].astype(F32)
        ql = lat[:, :QL]
        kl = lat[:, QL:QL + KVL]
        rq = lax.rsqrt(jnp.mean(ql * ql, axis=-1, keepdims=True) + EPS)
        rk = lax.rsqrt(jnp.mean(kl * kl, axis=-1, keepdims=True) + EPS)
        nq = ql * rq
        nk = kl * rk
        cs, sa, sb = _rope_tables(pos_ref, invf_ref, ma_ref, mb_ref)
        hc, ha, hb = _head_tables(cs, sa, sb)
        dq = _rotate(dqr_ref[...] * SM_SCALE, hc, ha, hb, -1.0).astype(BF16)
        dq_ref[...] = dq
        dkpe = jnp.zeros((tm, 128), F32)
        for hh in range(H):
            dkv_ref[:, hh * DQK:hh * DQK + 128] = dk_ref[:, hh * DQK:hh * DQK + 128]
            dkv_ref[:, hh * DQK + 128:(hh + 1) * DQK] = dv_ref[:, hh * DV:(hh + 1) * DV]
            dkpe = dkpe + dk_ref[:, hh * DQK + 128:(hh + 1) * DQK].astype(F32)
        lane = lax.broadcasted_iota(jnp.int32, (tm, 128), 1)
        dkr = jnp.where(lane < ROPE, _rotate(dkpe, cs, sa, sb, -1.0), 0.0)
        dqn = _dot_nt(dq, wuq_ref[...])
        dkvn = _dot_nt(dkv_ref[...], wukv_ref[...])
        gq = gq_ref[...]
        gkv = gkv_ref[...]
        dnq = dqn * gq
        dnk = dkvn * gkv
        dql = rq * (dnq - nq * jnp.mean(dnq * nq, axis=-1, keepdims=True))
        dkl = rk * (dnk - nk * jnp.mean(dnk * nk, axis=-1, keepdims=True))
        dp_ref[:, :QL] = dql.astype(BF16)
        dp_ref[:, QL:QL + KVL] = dkl.astype(BF16)
        dp_ref[:, QL + KVL:QL + KVL + 128] = dkr.astype(BF16)
        dp_ref[:, QL + KVL + 128:] = jnp.zeros((tm, D - QL - KVL - 128), BF16)

        @pl.when(i == 0)
        def _():
            dgq_ref[...] = jnp.zeros_like(dgq_ref)
            dgkv_ref[...] = jnp.zeros_like(dgkv_ref)

        dgq_ref[...] += jnp.sum(dqn * nq, axis=0, keepdims=True)
        dgkv_ref[...] += jnp.sum(dkvn * nk, axis=0, keepdims=True)

    row = lambda w: pl.BlockSpec((tm, w), lambda i: (i, 0))
    const = lambda a: pl.BlockSpec(a.shape, lambda i: (0,) * a.ndim)
    seg = pl.BlockSpec((None, tm, D), lambda i: (SEG_LAT, i, 0))
    return pl.pallas_call(
        body, name="mla_prep_bwd", grid=(t // tm,),
        in_specs=[pl.BlockSpec(memory_space=pl.ANY), seg, row(H * DQK), row(H * DQK), row(H * DV), row(1),
                  const(g_q), const(g_kv), const(wuq), const(wukv)] + [const(a) for a in tabs],
        out_specs=[seg, row(H * DQK), row(H * DQK),
                   pl.BlockSpec((1, QL), lambda i: (0, 0)), pl.BlockSpec((1, KVL), lambda i: (0, 0))],
        out_shape=[jax.ShapeDtypeStruct(dproj.shape, BF16),
                   jax.ShapeDtypeStruct((t, H * DQK), BF16), jax.ShapeDtypeStruct((t, H * DQK), BF16),
                   jax.ShapeDtypeStruct((1, QL), F32), jax.ShapeDtypeStruct((1, KVL), F32)],
        input_output_aliases={0: 0},
        compiler_params=_params(("arbitrary",)),
    )(dproj, proj, dq_rot, dk, dv, pos, g_q, g_kv, wuq, wukv, *tabs)


def _causal_mask(s, n):
    row = lax.broadcasted_iota(jnp.int32, (n, n), 0)
    col = lax.broadcasted_iota(jnp.int32, (n, n), 1)
    return jnp.where(col <= row, s, -1e30)


def flash_fwd(q, k, kv, nb, seq):
    t = q.shape[0]
    tq = min(FLASH_TQ, seq)
    nq = seq // tq

    def body(q_ref, k_ref, v_ref, o_ref, lse_ref):
        for qi in range(nq):
            qs = slice(qi * tq, (qi + 1) * tq)
            qv = q_ref[qs, :]
            m = jnp.full((tq, 1), -1e30, F32)
            l = jnp.zeros((tq, 1), F32)
            acc = jnp.zeros((tq, DV), F32)
            for j in range(qi + 1):
                ks = slice(j * tq, (j + 1) * tq)
                s = _dot_nt(qv, k_ref[ks, :])
                if j == qi:
                    s = _causal_mask(s, tq)
                m_new = jnp.maximum(m, jnp.max(s, axis=1, keepdims=True))
                p = jnp.exp2(s - m_new)
                alpha = jnp.exp2(m - m_new)
                l = alpha * l + jnp.sum(p, axis=1, keepdims=True)
                acc = alpha * acc + _dot(p.astype(BF16), v_ref[ks, :])
                m = m_new
            o_ref[qs, :] = (acc / l).astype(BF16)
            lse_ref[qs, :] = jnp.broadcast_to(m + jnp.log(l) * LOG2E, (tq, DV))

    out_blk = pl.BlockSpec((seq, DV), lambda b, h: (b, h))
    return pl.pallas_call(
        body, name="flash_fwd", grid=(nb, H),
        in_specs=[pl.BlockSpec((seq, DQK), lambda b, h: (b, h)),
                  pl.BlockSpec((seq, DQK), lambda b, h: (b, h)),
                  pl.BlockSpec((seq, DV), lambda b, h: (b, 2 * h + 1))],
        out_specs=[out_blk, out_blk],
        out_shape=[jax.ShapeDtypeStruct((t, H * DV), BF16), jax.ShapeDtypeStruct((t, H * DV), F32)],
        compiler_params=_params(("parallel", "parallel")),
    )(q, k, kv)


def flash_bwd(q, k, kv, o, do, lse, nb, seq):
    t = q.shape[0]
    tq = min(FLASH_TQ, seq)
    nq = seq // tq

    def body(q_ref, k_ref, v_ref, o_ref, do_ref, lse_ref, dq_ref, dk_ref, dv_ref):
        delta = []
        for qi in range(nq):
            qs = slice(qi * tq, (qi + 1) * tq)
            delta.append(jnp.sum(do_ref[qs, :].astype(F32) * o_ref[qs, :].astype(F32), axis=1, keepdims=True))
        for ki in range(nq):
            ks = slice(ki * tq, (ki + 1) * tq)
            kb = k_ref[ks, :]
            vb = v_ref[ks, :]
            dk = jnp.zeros((tq, DQK), F32)
            dv = jnp.zeros((tq, DV), F32)
            for qi in range(ki, nq):
                qs = slice(qi * tq, (qi + 1) * tq)
                qv = q_ref[qs, :]
                dov = do_ref[qs, :]
                s = _dot_nt(qv, kb)
                if qi == ki:
                    s = _causal_mask(s, tq)
                p = jnp.exp2(s - lse_ref[qs, :][:, :1])
                dp = _dot_nt(dov, vb)
                dz = (p * (dp - delta[qi])).astype(BF16)
                dv = dv + _dot_tn(p.astype(BF16), dov)
                dk = dk + _dot_tn(dz, qv)
                dqb = _dot(dz, kb)
                if ki == 0:
                    dq_ref[qs, :] = dqb
                else:
                    dq_ref[qs, :] += dqb
            dk_ref[ks, :] = (dk * LN2).astype(BF16)
            dv_ref[ks, :] = dv.astype(BF16)

    full = lambda w, col: pl.BlockSpec((seq, w), col)
    same = lambda b, h: (b, h)
    return pl.pallas_call(
        body, name="flash_bwd", grid=(nb, H),
        in_specs=[full(DQK, same), full(DQK, same), full(DV, lambda b, h: (b, 2 * h + 1)),
                  full(DV, same), full(DV, same), full(DV, same)],
        out_specs=[full(DQK, same), full(DQK, same), full(DV, same)],
        out_shape=[jax.ShapeDtypeStruct((t, H * DQK), F32), jax.ShapeDtypeStruct((t, H * DQK), BF16),
                   jax.ShapeDtypeStruct((t, H * DV), BF16)],
        compiler_params=_params(("parallel", "parallel")),
    )(q, k, kv, o, do, lse)


TAIL_TM = 256


def tail_fwd(y, attn, proj, x2, tgt, gate, g_post, wco, wmo, wout, seq):
    t = y.shape[0]
    nb = t // seq
    tm = min(TAIL_TM, seq)
    tpb = seq // tm

    def body(y_ref, at_ref, p_ref, x_ref, t_ref, gate_ref, gp_ref, wco_ref, wmo_ref, wout_ref,
             o_ref, ya_ref, yb_ref, m_ref, do2_ref, dout_ref, dgate_ref, dgp_ref, loss_ref):
        i = pl.program_id(0)
        bz = p_ref[0].astype(F32)
        ga = p_ref[1].astype(F32)
        gb = p_ref[2].astype(F32)
        ov = (at_ref[...].astype(F32) * (bz * _sig(bz))).astype(BF16)
        o_ref[...] = ov
        ya = _dot(y_ref[...], wco_ref[...])
        yb = _dot(ov, wmo_ref[...])
        ya_ref[...] = ya.astype(BF16)
        yb_ref[...] = yb.astype(BF16)
        mv = (_sig(ga) * ya + _sig(gb) * yb).astype(BF16)
        m_ref[...] = mv
        o2 = _dot(mv, wout_ref[...])
        r = lax.rsqrt(jnp.mean(o2 * o2, axis=-1, keepdims=True) + EPS)
        nrm = o2 * r
        gp = gp_ref[...]
        gate_v = gate_ref[...]
        rn = nrm * gp
        err = x_ref[...] + gate_v * rn - t_ref[...]
        dout = err * (1.0 / D)
        dout_ref[...] = dout
        dn = dout * gate_v * gp
        do2_ref[...] = (r * (dn - nrm * jnp.mean(dn * nrm, axis=-1, keepdims=True))).astype(BF16)

        @pl.when(i % tpb == 0)
        def _():
            dgate_ref[...] = jnp.zeros_like(dgate_ref)

        @pl.when(i == 0)
        def _():
            dgp_ref[...] = jnp.zeros_like(dgp_ref)
            loss_ref[...] = jnp.zeros_like(loss_ref)

        dgate_ref[...] += jnp.sum(dout * rn, axis=0, keepdims=True)
        dgp_ref[...] += jnp.sum(dout * gate_v * nrm, axis=0, keepdims=True)
        loss_ref[...] += 0.5 * jnp.sum(jnp.mean(err * err, axis=-1, keepdims=True), axis=0, keepdims=True)

    row = pl.BlockSpec((tm, D), lambda i: (i, 0))
    per_batch = pl.BlockSpec((None, 1, D), lambda i: (i // tpb, 0, 0))
    vec = pl.BlockSpec((1, D), lambda i: (0, 0))
    wgt = pl.BlockSpec((D, D), lambda i: (0, 0))
    act = jax.ShapeDtypeStruct((t, D), BF16)
    return pl.pallas_call(
        body, name="tail_fwd", grid=(t // tm,),
        in_specs=[row, row, pl.BlockSpec((3, tm, D), lambda i: (0, i, 0)), row, row, per_batch, vec,
                  wgt, wgt, wgt],
        out_specs=[row, row, row, row, row, row, per_batch, vec, pl.BlockSpec((1, 1), lambda i: (0, 0))],
        out_shape=[act, act, act, act, act, jax.ShapeDtypeStruct((t, D), F32),
                   jax.ShapeDtypeStruct((nb, 1, D), F32), jax.ShapeDtypeStruct((1, D), F32),
                   jax.ShapeDtypeStruct((1, 1), F32)],
        compiler_params=_params(("arbitrary",)),
    )(y, attn, proj, x2, tgt, gate, g_post, wco, wmo, wout)


def tail_bwd(do2, proj, ya, yb, attn, wout, wmo, wco):
    t = do2.shape[0]
    tm = min(TAIL_TM, t)

    def body(do2_ref, p_ref, ya_ref, yb_ref, at_ref, wout_ref, wmo_ref, wco_ref,
             dp_ref, dya_ref, dyb_ref, dat_ref, dy_ref):
        bz = p_ref[0].astype(F32)
        ga = p_ref[1].astype(F32)
        gb = p_ref[2].astype(F32)
        dm = _dot_nt(do2_ref[...], wout_ref[...])
        sa = _sig(ga)
        sb = _sig(gb)
        dya = (dm * sa).astype(BF16)
        dyb = (dm * sb).astype(BF16)
        dya_ref[...] = dya
        dyb_ref[...] = dyb
        dp_ref[1] = (dm * ya_ref[...].astype(F32) * (sa * (1.0 - sa))).astype(BF16)
        dp_ref[2] = (dm * yb_ref[...].astype(F32) * (sb * (1.0 - sb))).astype(BF16)
        dov = _dot_nt(dyb, wmo_ref[...])
        sz = _sig(bz)
        dat_ref[...] = (dov * (bz * sz)).astype(BF16)
        dp_ref[0] = (dov * at_ref[...].astype(F32) * (sz * (1.0 + bz * (1.0 - sz)))).astype(BF16)
        dy_ref[...] = _dot_nt(dya, wco_ref[...]).astype(BF16)

    row = pl.BlockSpec((tm, D), lambda i: (i, 0))
    seg3 = pl.BlockSpec((3, tm, D), lambda i: (0, i, 0))
    wgt = pl.BlockSpec((D, D), lambda i: (0, 0))
    act = jax.ShapeDtypeStruct((t, D), BF16)
    return pl.pallas_call(
        body, name="tail_bwd", grid=(t // tm,),
        in_specs=[row, seg3, row, row, row, wgt, wgt, wgt],
        out_specs=[seg3, row, row, row, row],
        out_shape=[jax.ShapeDtypeStruct((NSEG, t, D), BF16), act, act, act, act],
        compiler_params=_params(("parallel",)),
    )(do2, proj, ya, yb, attn, wout, wmo, wco)


def adamw(w, m, v, g, g2, name, token=None):
    rows, cols = w.shape
    tr = rows
    for cand in (256, 128, 64, 32, 16, 8):
        if rows % cand == 0 and rows > cand:
            tr = cand
            break
    has2 = g2 is not None
    n_in = 4 + has2

    def body(*refs):
        w_ref, m_ref, v_ref, g_ref = refs[:4]
        go_ref, d_ref, mo_ref, vo_ref = refs[-4:]
        grad = g_ref[...] + refs[4][...].astype(F32) if has2 else g_ref[...]
        mn = ADAM_B1 * m_ref[...] + (1.0 - ADAM_B1) * grad
        vn = ADAM_B2 * v_ref[...] + (1.0 - ADAM_B2) * (grad * grad)
        m_hat = mn / (1.0 - ADAM_B1 ** ADAM_STEP)
        v_hat = vn / (1.0 - ADAM_B2 ** ADAM_STEP)
        go_ref[...] = grad
        d_ref[...] = -ADAM_LR * (m_hat / (jnp.sqrt(v_hat) + ADAM_EPS) + ADAM_WD * w_ref[...])
        mo_ref[...] = mn
        vo_ref[...] = vn

    blk = pl.BlockSpec((tr, cols), lambda i: (i, 0))
    ins = [w, m, v, g] + ([g2] if has2 else [])
    specs = [blk] * n_in
    if token is not None:
        ins.append(token)
        specs.append(pl.BlockSpec((8, 128), lambda i: (0, 0)))
    return pl.pallas_call(
        body, name=name, grid=(rows // tr,),
        in_specs=specs, out_specs=[blk] * 4,
        out_shape=[jax.ShapeDtypeStruct((rows, cols), F32)] * 4,
        compiler_params=_params(("parallel",)),
    )(*ins)


_ORD_A = ("x", "y", "c")
_ORD_B = ("y", "x", "c")


def _to_slots(full, order, col_sharded):
    if col_sharded:
        r = full.shape[0]
        cc = full.shape[1] // 8
        g = full.reshape(r, 2, 2, 2, cc).transpose(1, 2, 3, 0, 4)
    else:
        r = full.shape[0] // 8
        cc = full.shape[1]
        g = full.reshape(2, 2, 2, r, cc)
    names = ("x", "y", "c")
    perm = tuple(names.index(a) for a in order)
    return g.transpose(perm + (3, 4))


def _rows128(a, rows):
    flat = a.reshape(-1)
    return jnp.pad(flat, (0, rows * 128 - flat.shape[0])).reshape(rows, 128)


def kernel(x, c, positions, w_ada, b_ada, g_pre, w_in, conv_w, w_conv_out, g_q, w_uq, g_kv, w_ukv, w_mla_out, w_out, g_post, loss_target, m_w_ada, m_b_ada, m_g_pre, m_w_in, m_conv_w, m_w_conv_out, m_g_q, m_w_uq, m_g_kv, m_w_ukv, m_w_mla_out, m_w_out, m_g_post, v_w_ada, v_b_ada, v_g_pre, v_w_in, v_conv_w, v_w_conv_out, v_g_q, v_w_uq, v_g_kv, v_w_ukv, v_w_mla_out, v_w_out, v_g_post):
    nb, seq, _ = x.shape
    t = nb * seq
    mx, my, mc = lax.axis_index("x"), lax.axis_index("y"), lax.axis_index("c")
    me = 4 * mx + 2 * my + mc
    co = {"x": mx, "y": my, "c": mc}

    x2 = x.reshape(t, D)
    tgt2 = loss_target.reshape(t, D)
    pos2 = positions.reshape(t, 1)

    packed = jnp.concatenate([c.reshape(2 * D // 128, 128), _rows128(conv_w[0], 8)], axis=0)
    gath = small_allgather(packed, "gather_cond")
    c_all = gath[:, :16].reshape(8 * nb, D)
    conv_full = gath[:, 16:19].reshape(8, 3, 128).transpose(1, 0, 2).reshape(3, D)
    conv_full8 = jnp.pad(conv_full, ((0, 5), (0, 0)))
    ada_cols = w_ada.shape[2]
    b_cols = lax.dynamic_slice(b_ada, (0, me * ada_cols), (1, ada_cols))
    mod_part = ada_fwd(c_all, w_ada[0], b_cols)
    mod_g = small_allgather(mod_part.reshape(8 * nb * ada_cols // 128, 128), "gather_mod")
    mod_all = mod_g.reshape(8, 8 * nb, ada_cols).transpose(1, 0, 2).reshape(8 * nb, 8 * ada_cols)
    mod = lax.dynamic_slice(mod_all, (me * nb, 0), (nb, 3 * D))
    shift = mod[:, 0:D].reshape(nb, 1, D)
    scale = mod[:, D:2 * D].reshape(nb, 1, D)
    gate = mod[:, 2 * D:3 * D].reshape(nb, 1, D)

    wt = w_in[0].T.astype(BF16)
    lo = lax.bitcast_convert_type(wt[:, :D // 2], jnp.uint16).astype(jnp.uint32)
    hi = lax.bitcast_convert_type(wt[:, D // 2:], jnp.uint16).astype(jnp.uint32)
    wt_bits = lax.bitcast_convert_type(lo | (hi << 16), F32)
    shards = [wt_bits, w_conv_out[0].astype(BF16), w_mla_out[0].astype(BF16), w_out[0].astype(BF16),
              w_uq[0].astype(BF16), w_ukv[0].astype(BF16)]
    q4 = D // 4
    plan = [(0, (0, q4), _ORD_A), (0, (q4, q4), _ORD_B), (1, None, _ORD_A), (2, None, _ORD_A),
            (3, None, _ORD_B), (4, None, _ORD_B), (5, None, _ORD_B)]
    gw = allgather_big(shards, plan, "gather_weights")
    wt_bits_all = gw[0].reshape(N_IN, D // 2)
    wco = gw[1].reshape(D, D)
    wmo = gw[2].reshape(D, D)
    wout = gw[3].reshape(D, D)
    wuq_full = gw[4].reshape(8, QL, 192).transpose(1, 0, 2)
    wuq_p = jnp.pad(wuq_full, ((0, 0), (0, 0), (0, DQK - 192))).reshape(QL, H * DQK)
    wukv = gw[5].reshape(8, KVL, 256).transpose(1, 0, 2).reshape(KVL, H * 256)

    inv_freq = ROPE_THETA ** (-jnp.arange(0, ROPE, 2, dtype=F32) / ROPE)
    invf = jnp.concatenate([inv_freq, inv_freq, jnp.zeros((128 - ROPE,), F32)]).reshape(1, 128)
    lane = np.arange(128)
    tabs = (invf,
            jnp.asarray(np.where(lane < HALF, -1.0, 0.0).reshape(1, 128), F32),
            jnp.asarray(np.where((lane >= HALF) & (lane < ROPE), 1.0, 0.0).reshape(1, 128), F32))

    h = prenorm_fwd(x2, scale, shift, g_pre, seq)
    proj, wt_p = proj_matmul(h, wt_bits_all)
    y = conv_fwd(proj, conv_full8, seq)
    q_rot, k_cat, kv, qn, kvn = mla_prep_fwd(proj, pos2, g_q, g_kv, wuq_p, wukv, tabs)
    attn, lse = flash_fwd(q_rot, k_cat, kv, nb, seq)
    o, ya, yb, m, do2, dout, dgate, dg_post, loss_part = tail_fwd(
        y, attn, proj, x2, tgt2, gate, g_post, wco, wmo, wout, seq)

    dproj, dya, dyb, dattn, dy = tail_bwd(do2, proj, ya, yb, attn, wout, wmo, wco)
    g_wout = grad_matmul(m, do2, "grad_w_out")
    g_wmo = grad_matmul(o, dyb, "grad_w_mla_out")
    g_wco = grad_matmul(y, dya, "grad_w_conv_out")
    dproj, dconv = conv_bwd(dproj, proj, dy, conv_full8, seq)
    dq_rot, dk, dv = flash_bwd(q_rot, k_cat, kv, attn, dattn, lse, nb, seq)
    dproj, dq, dkv, dg_q, dg_kv = mla_prep_bwd(dproj, proj, dq_rot, dk, dv, pos2, g_q, g_kv, wuq_p, wukv, tabs)
    g_wuq_p = grad_matmul(qn, dq, "grad_w_uq")
    g_wukv = grad_matmul(kvn, dkv, "grad_w_ukv")
    g_win_p = win_grad_matmul(h, dproj)

    g_wt = g_win_p.reshape(2, 2, 2, N_IN // 8, D)
    g_wuq = g_wuq_p.reshape(QL, H, DQK)[:, :, :192].reshape(QL, H * 192)
    rs_a = ("c", "y", "x")
    rs_b = ("c", "x", "y")
    flat = lambda s: s.reshape(2, 2, 2, -1, 128)
    rest_a = jnp.concatenate([flat(_to_slots(g_wco, rs_a, False)), flat(_to_slots(g_wmo, rs_a, False))], axis=3)
    rest_b = jnp.concatenate([flat(_to_slots(g_wout, rs_b, False)), flat(_to_slots(g_wuq, rs_b, True)),
                              flat(_to_slots(g_wukv, rs_b, True))], axis=3)
    ords = [rs_a, rs_a, rs_b, rs_b]
    hc = D // 2
    win_shape = (2, 2, N_IN // 8, hc)
    pick_w = lambda col: (lambda ref, cc: ref.at[:, :, 1 - cc["c"], :, pl.ds(col * hc, hc)])
    pick_h = lambda ref, cc: ref.at[1 - cc["c"]]
    which1 = [0, 1, 0, 2]
    picks1 = [pick_w(0), pick_h, pick_w(1), pick_h]
    shapes1 = [win_shape, rest_a.shape[1:], win_shape, rest_b.shape[1:]]
    st1, tok1 = swap_start([g_wt, rest_a, rest_b], which1, ["c"] * 4, picks1, shapes1, "rs_c_start")
    dh = dh_matmul(dproj, wt_p, tok1)
    (g_wt, rest_a, rest_b), r1 = swap_wait(st1, dh, which1, ["c"] * 4, picks1, "rs_c_wait")
    sel_xyc = jnp.stack([mx, my, mc]).astype(jnp.int32)
    sel1 = [jnp.stack([co[o[0]], co[o[1]]]).astype(jnp.int32) for o in ords]
    sel2 = [jnp.stack([co[o[2]]]).astype(jnp.int32) for o in ords]
    first = [rs_win_add_first(g_wt, r1[0], sel_xyc, 1, 0, "rs_add_first_0"),
             rs_add_first(rest_a, r1[1], sel1[1], "rs_add_first_1"),
             rs_win_add_first(g_wt, r1[2], sel_xyc, 0, 1, "rs_add_first_2"),
             rs_add_first(rest_b, r1[3], sel1[3], "rs_add_first_3")]
    keep1, send1 = zip(*first)
    all4 = [0, 1, 2, 3]
    none4 = [None] * 4
    axes2 = [o[1] for o in ords]
    st2, tok2 = swap_start(list(send1), all4, axes2, none4, [s.shape for s in send1], "rs_ici1_start")

    grad_x2, dshift, dscale, dg_pre = prenorm_bwd(dh, x2, dout, scale, g_pre, seq, tok2)

    dmod = jnp.concatenate([dshift, dscale, dgate], axis=2).reshape(nb * 3 * D // 128, 128)
    small = jnp.concatenate([
        dmod, _rows128(dg_pre, 8), _rows128(dg_post, 8), _rows128(dg_q, 8), _rows128(dg_kv, 8),
        dconv[0:3].reshape(24, 128), _rows128(loss_part, 8)], axis=0)
    small_g = small_allgather(small, "gather_small_grads")
    sums = slot_sum(small_g)
    dmod_all = small_g[:, 0:48].reshape(8 * nb, 3 * D)
    g_bada = (sums[0:24] + sums[24:48]).reshape(1, 3 * D)
    g_gpre = sums[48:56].reshape(1, D)
    g_gpost = sums[56:64].reshape(1, D)
    g_gq = sums[64:67].reshape(1, QL)
    g_gkv = sums[72:74].reshape(1, KVL)
    g_conv_full = sums[80:104].reshape(3, D)
    loss = sums[104, 0]
    g_conv = lax.dynamic_slice(g_conv_full, (0, me * 128), (3, 128))
    dmod_cols = lax.dynamic_slice(dmod_all, (0, me * ada_cols), (8 * nb, ada_cols))
    g_wada = ada_bwd(c_all, dmod_cols)

    _, r2 = swap_wait(st2, g_wada, all4, axes2, none4, "rs_ici1_wait")
    keep2, send2 = zip(*[rs_add_second(keep1[a], r2[a], sel2[a], "rs_add_second_%d" % a) for a in range(4)])
    axes3 = [o[2] for o in ords]
    st3, tok3 = swap_start(list(send2), all4, axes3, none4, [s.shape for s in send2], "rs_ici2_start")

    res = {}
    res["w_ada"] = [o_[None] for o_ in adamw(w_ada[0], m_w_ada[0], v_w_ada[0], g_wada, None, "adamw_w_ada", tok3)]

    def pack(b_, gp_, gpo_, gq_, gkv_, cw_):
        return jnp.concatenate([_rows128(b_, 24), _rows128(gp_, 8), _rows128(gpo_, 8), _rows128(gq_, 8),
                                _rows128(gkv_, 8), _rows128(cw_, 8)], axis=0)

    sw = pack(b_ada, g_pre, g_post, g_q, g_kv, conv_w)
    sm = pack(m_b_ada, m_g_pre, m_g_post, m_g_q, m_g_kv, m_conv_w)
    sv = pack(v_b_ada, v_g_pre, v_g_post, v_g_q, v_g_kv, v_conv_w)
    sg = pack(g_bada, g_gpre, g_gpost, g_gq, g_gkv, g_conv)
    small_out = adamw(sw, sm, sv, sg, None, "adamw_small", tok3)

    _, r3 = swap_wait(st3, small_out[0], all4, axes3, none4, "rs_ici2_wait")

    gk_win = jnp.concatenate([keep2[0], keep2[2]], axis=1).T
    gr_win = jnp.concatenate([r3[0], r3[2]], axis=1).T
    n_sq = D * 128 // 128
    unflat = lambda a, lo, shape: a[lo:lo + shape[0] * shape[1] // 128].reshape(shape)
    sq = (128, D)
    uq_s = (QL, 192)
    ukv_s = (KVL, 256)
    parts = {
        "w_conv_out": (unflat(keep2[1], 0, sq), unflat(r3[1], 0, sq)),
        "w_mla_out": (unflat(keep2[1], n_sq, sq), unflat(r3[1], n_sq, sq)),
        "w_out": (unflat(keep2[3], 0, sq), unflat(r3[3], 0, sq)),
        "w_uq": (unflat(keep2[3], n_sq, uq_s), unflat(r3[3], n_sq, uq_s)),
        "w_ukv": (unflat(keep2[3], n_sq + QL * 192 // 128, ukv_s), unflat(r3[3], n_sq + QL * 192 // 128, ukv_s)),
        "w_in": (gk_win, gr_win),
    }

    weights = {"w_in": (w_in, m_w_in, v_w_in),
               "w_conv_out": (w_conv_out, m_w_conv_out, v_w_conv_out), "w_uq": (w_uq, m_w_uq, v_w_uq),
               "w_ukv": (w_ukv, m_w_ukv, v_w_ukv), "w_mla_out": (w_mla_out, m_w_mla_out, v_w_mla_out),
               "w_out": (w_out, m_w_out, v_w_out)}
    for nm, (wv, mv, vv) in weights.items():
        ga, gb = parts[nm]
        outs = adamw(wv[0], mv[0], vv[0], ga, gb, "adamw_" + nm)
        res[nm] = [o_[None] for o_ in outs]

    def unpack(a):
        return {"b_ada": a[0:24].reshape(1, 3 * D), "g_pre": a[24:32].reshape(1, D),
                "g_post": a[32:40].reshape(1, D), "g_q": a[40:43].reshape(1, QL),
                "g_kv": a[48:50].reshape(1, KVL), "conv_w": a[56:59].reshape(-1)[:3 * 128].reshape(1, 3, 128)}

    for nm in ("b_ada", "g_pre", "g_post", "g_q", "g_kv", "conv_w"):
        res[nm] = [unpack(a)[nm] for a in small_out]

    order = ["w_ada", "b_ada", "g_pre", "w_in", "conv_w", "w_conv_out", "g_q", "w_uq", "g_kv", "w_ukv",
             "w_mla_out", "w_out", "g_post"]
    out = [loss, grad_x2.reshape(nb, seq, D)]
    for k_ in range(4):
        out += [res[nm][k_] for nm in order]
    return tuple(out)
```

```python
import functools

import numpy as np
import jax
import jax.numpy as jnp
from jax import lax
from jax.experimental import pallas as pl
from jax.experimental.pallas import tpu as pltpu

F32 = jnp.float32
BF16 = jnp.bfloat16
MESH = pl.DeviceIdType.MESH

D = 1024
H = 8
QL = 384
KVL = 256
ROPE = 64
HALF = ROPE // 2
DQK = 256
DV = 128
NSEG = 8
NP = NSEG * D
EPS = 1e-6
ROPE_THETA = 10000.0
SM_SCALE = (128 + ROPE) ** -0.5
LOG2E = 1.4426950408889634
LN2 = 0.6931471805599453
FLASH_TQ = 512

SEG_BZ, SEG_GA, SEG_GB, SEG_LAT, SEG_V = 0, 1, 2, 3, 4

ADAM_LR = 0.001
ADAM_B1 = 0.9
ADAM_B2 = 0.999
ADAM_EPS = 1e-08
ADAM_WD = 0.01
ADAM_STEP = 10

VMEM_LIMIT = 56 * 1024 * 1024


def _params(sem=None, vmem=VMEM_LIMIT):
    kw = dict(vmem_limit_bytes=vmem)
    if sem is not None:
        kw["dimension_semantics"] = sem
    return pltpu.CompilerParams(**kw)


def _sig(v):
    return 1.0 / (1.0 + jnp.exp(-v))


def _dot(a, b):
    return jnp.dot(a, b, preferred_element_type=F32)


def _dot_nt(a, b):
    return lax.dot_general(a, b, (((1,), (1,)), ((), ())), preferred_element_type=F32)


def _dot_tn(a, b):
    return lax.dot_general(a, b, (((0,), (0,)), ((), ())), preferred_element_type=F32)


_AXIS_POS = {"x": 0, "y": 1, "c": 2}


def _coords():
    return lax.axis_index("x"), lax.axis_index("y"), lax.axis_index("c")


def _partner(axis):
    p = list(_coords())
    p[_AXIS_POS[axis]] = 1 - p[_AXIS_POS[axis]]
    return tuple(p)


def small_allgather(v, name):
    rows = v.shape[0]

    def body(v_ref, out_ref, send_sems, recv_sems):
        x, y, c = _coords()
        me = 4 * x + 2 * y + c
        out_ref[me] = v_ref[...]
        copies = []
        for k in range(1, 8):
            peer = (1 - x if k & 4 else x, 1 - y if k & 2 else y, 1 - c if k & 1 else c)
            cp = pltpu.make_async_remote_copy(
                src_ref=v_ref, dst_ref=out_ref.at[me],
                send_sem=send_sems.at[k - 1], recv_sem=recv_sems.at[k - 1],
                device_id=peer, device_id_type=MESH)
            cp.start()
            copies.append(cp)
        for cp in copies:
            cp.wait()

    return pl.pallas_call(
        body, name=name,
        out_shape=jax.ShapeDtypeStruct((8, rows, 128), F32),
        in_specs=[pl.BlockSpec(memory_space=pltpu.VMEM)],
        out_specs=pl.BlockSpec(memory_space=pltpu.VMEM),
        scratch_shapes=[pltpu.SemaphoreType.DMA((7,)), pltpu.SemaphoreType.DMA((7,))],
    )(v)


def allgather_big(arrs, plan, name):
    n = len(arrs)
    m = len(plan)

    def body(*refs):
        ins, outs = refs[:n], refs[n:2 * n]
        send_sems, recv_sems, loc_sems = refs[2 * n:]
        x, y, c = _coords()
        co = {"x": x, "y": y, "c": c}

        def window(ref, lead, cols):
            tail = ref.shape[len(lead):]
            idx = tuple(lead) + (slice(None),) * (len(tail) - 1)
            idx += (slice(None),) if cols is None else (pl.ds(cols[0], cols[1]),)
            return ref.at[idx]

        def held(e, free):
            i, cols, _ = plan[e]
            lead = [slice(None) if ax in free else co[ax] for ax in ("x", "y", "c")]
            return window(outs[i], lead, cols)

        def rcopy(e, stage, src, dst, axis):
            return pltpu.make_async_remote_copy(
                src_ref=src, dst_ref=dst,
                send_sem=send_sems.at[e, stage], recv_sem=recv_sems.at[e, stage],
                device_id=_partner(axis), device_id_type=MESH)

        local, stages = [], [[], [], []]
        for e, (i, cols, order) in enumerate(plan):
            mine = window(ins[i], [], cols)
            lc = pltpu.make_async_copy(mine, held(e, ()), loc_sems.at[e])
            lc.start()
            local.append(lc)
            cp = rcopy(e, 0, mine, held(e, ()), order[0])
            cp.start()
            stages[0].append(cp)
        for s in (1, 2):
            for e, (i, cols, order) in enumerate(plan):
                stages[s - 1][e].wait_recv()
                if s == 1:
                    local[e].wait()
                blk = held(e, order[:s])
                cp = rcopy(e, s, blk, blk, order[s])
                cp.start()
                stages[s].append(cp)
        for e in range(m):
            stages[2][e].wait_recv()
        for e in range(m):
            for s in range(3):
                stages[s][e].wait_send()

    any_spec = pl.BlockSpec(memory_space=pl.ANY)
    return pl.pallas_call(
        body, name=name,
        out_shape=[jax.ShapeDtypeStruct((2, 2, 2) + a.shape, a.dtype) for a in arrs],
        in_specs=[any_spec] * n,
        out_specs=[any_spec] * n,
        scratch_shapes=[pltpu.SemaphoreType.DMA((m, 3)), pltpu.SemaphoreType.DMA((m, 3)),
                        pltpu.SemaphoreType.DMA((m,))],
    )(*arrs)


def exchange(arrs, axes, picks, out_shapes, name):
    n = len(arrs)

    def body(*refs):
        ins, outs = refs[:n], refs[n:2 * n]
        send_sems, recv_sems = refs[2 * n:]
        x, y, c = _coords()
        co = {"x": x, "y": y, "c": c}
        copies = []
        for a in range(n):
            src = ins[a] if picks[a] is None else picks[a](ins[a], co)
            cp = pltpu.make_async_remote_copy(
                src_ref=src, dst_ref=outs[a],
                send_sem=send_sems.at[a], recv_sem=recv_sems.at[a],
                device_id=_partner(axes[a]), device_id_type=MESH)
            cp.start()
            copies.append(cp)
        for cp in copies:
            cp.wait()

    any_spec = pl.BlockSpec(memory_space=pl.ANY)
    return pl.pallas_call(
        body, name=name,
        out_shape=[jax.ShapeDtypeStruct(s, a.dtype) for s, a in zip(out_shapes, arrs)],
        in_specs=[any_spec] * n,
        out_specs=[any_spec] * n,
        scratch_shapes=[pltpu.SemaphoreType.DMA((n,)), pltpu.SemaphoreType.DMA((n,))],
    )(*arrs)


_HBM = pl.BlockSpec(memory_space=pltpu.HBM)
_SEM = pl.BlockSpec(memory_space=pltpu.SEMAPHORE)


def _swap_copies(srcs, lands, send_sems, recv_sems, axes, picks):
    x, y, c = _coords()
    co = {"x": x, "y": y, "c": c}
    return [pltpu.make_async_remote_copy(
        src_ref=srcs[a] if picks[a] is None else picks[a](srcs[a], co), dst_ref=lands[a],
        send_sem=send_sems.at[a], recv_sem=recv_sems.at[a],
        device_id=_partner(axes[a]), device_id_type=MESH) for a in range(len(srcs))]


def swap_start(arrs, which, axes, picks, out_shapes, name):
    ns, n = len(arrs), len(which)

    def body(*refs):
        srcs, lands = refs[:ns], refs[ns:ns + n]
        send_sems, recv_sems = refs[ns + n:ns + n + 2]
        token = refs[-1]
        for cp in _swap_copies([srcs[i] for i in which], lands, send_sems, recv_sems, axes, picks):
            cp.start()
        token[...] = jnp.zeros_like(token)

    lands = [lax.empty(s, arrs[i].dtype) for s, i in zip(out_shapes, which)]
    ops = [pltpu.with_memory_space_constraint(a, pltpu.HBM) for a in list(arrs) + lands]
    out = pl.pallas_call(
        body, name=name,
        out_shape=[pltpu.SemaphoreType.DMA((n,)), pltpu.SemaphoreType.DMA((n,))]
        + [pltpu.HBM(o.shape, o.dtype) for o in ops] + [jax.ShapeDtypeStruct((8, 128), F32)],
        in_specs=[_HBM] * (ns + n),
        out_specs=[_SEM, _SEM] + [_HBM] * (ns + n) + [pl.BlockSpec(memory_space=pltpu.VMEM)],
        input_output_aliases={i: 2 + i for i in range(ns + n)},
        compiler_params=pltpu.CompilerParams(has_side_effects=pltpu.SideEffectType.DATAFLOW_SIDE_EFFECTING),
    )(*ops)
    return out[:-1], out[-1]


def swap_wait(state, after, which, axes, picks, name):
    n = len(which)
    ns = len(state) - 2 - n

    def body(*refs):
        srcs, lands = refs[:ns], refs[ns:ns + n]
        send_sems, recv_sems = refs[ns + n:ns + n + 2]
        for cp in _swap_copies([srcs[i] for i in which], lands, send_sems, recv_sems, axes, picks):
            cp.wait_send()
            cp.wait_recv()

    thru = list(state[2:])
    out = pl.pallas_call(
        body, name=name,
        out_shape=[pltpu.HBM(o.shape, o.dtype) for o in thru],
        in_specs=[_HBM] * (ns + n) + [_SEM, _SEM, pl.BlockSpec(memory_space=pl.ANY)],
        out_specs=[_HBM] * (ns + n),
        input_output_aliases={i: i for i in range(ns + n)},
        compiler_params=pltpu.CompilerParams(has_side_effects=pltpu.SideEffectType.DATAFLOW_SIDE_EFFECTING),
    )(*thru, state[0], state[1], after)
    return out[:ns], out[ns:]


def _gather_copies(shards, lands, send_sems, recv_sems):
    x, y, c = _coords()
    copies = []
    for a in range(len(shards)):
        for k in range(1, 8):
            peer = (1 - x if k & 4 else x, 1 - y if k & 2 else y, 1 - c if k & 1 else c)
            copies.append(pltpu.make_async_remote_copy(
                src_ref=shards[a], dst_ref=lands[a].at[x, y, c],
                send_sem=send_sems.at[7 * a + k - 1], recv_sem=recv_sems.at[7 * a + k - 1],
                device_id=peer, device_id_type=MESH))
    return copies


def gather_start(shards, name):
    n = len(shards)
    x, y, c = _coords()

    def body(*refs):
        srcs, lands = refs[:n], refs[n:2 * n]
        send_sems, recv_sems = refs[2 * n:2 * n + 2]
        token = refs[-1]
        for cp in _gather_copies(srcs, lands, send_sems, recv_sems):
            cp.start()
        token[...] = jnp.zeros_like(token)

    lands = [lax.dynamic_update_slice(jnp.zeros((2, 2, 2) + s.shape, s.dtype), s[None, None, None],
                                      (x, y, c) + (0,) * s.ndim) for s in shards]
    ops = [pltpu.with_memory_space_constraint(a, pltpu.HBM) for a in list(shards) + lands]
    out = pl.pallas_call(
        body, name=name,
        out_shape=[pltpu.SemaphoreType.DMA((7 * n,)), pltpu.SemaphoreType.DMA((7 * n,))]
        + [pltpu.HBM(o.shape, o.dtype) for o in ops] + [jax.ShapeDtypeStruct((8, 128), F32)],
        in_specs=[_HBM] * (2 * n),
        out_specs=[_SEM, _SEM] + [_HBM] * (2 * n) + [pl.BlockSpec(memory_space=pltpu.VMEM)],
        input_output_aliases={i: 2 + i for i in range(2 * n)},
        compiler_params=pltpu.CompilerParams(has_side_effects=pltpu.SideEffectType.DATAFLOW_SIDE_EFFECTING),
    )(*ops)
    return out[:-1], out[-1]


def gather_wait(state, after, name):
    n = (len(state) - 2) // 2

    def body(*refs):
        srcs, lands = refs[:n], refs[n:2 * n]
        send_sems, recv_sems = refs[2 * n:2 * n + 2]
        for cp in _gather_copies(srcs, lands, send_sems, recv_sems):
            cp.wait_send()
            cp.wait_recv()

    thru = list(state[2:])
    out = pl.pallas_call(
        body, name=name,
        out_shape=[pltpu.HBM(o.shape, o.dtype) for o in thru],
        in_specs=[_HBM] * (2 * n) + [_SEM, _SEM, pl.BlockSpec(memory_space=pl.ANY)],
        out_specs=[_HBM] * (2 * n),
        input_output_aliases={i: i for i in range(2 * n)},
        compiler_params=pltpu.CompilerParams(has_side_effects=pltpu.SideEffectType.DATAFLOW_SIDE_EFFECTING),
    )(*thru, state[0], state[1], after)
    return out[n:]


def rs_win_add_first(g, r, sel, next_dim, col, name):
    rows, cols = r.shape[2:]

    def body(sel_ref, gk_ref, rk_ref, gs_ref, rs_ref, keep_ref, send_ref):
        keep_ref[...] = gk_ref[...] + rk_ref[...]
        send_ref[...] = (gs_ref[...] + rs_ref[...]).astype(BF16)

    def g_map(flip):
        def f(j, s):
            nxt = 1 - s[next_dim] if flip else s[next_dim]
            return (nxt, j, s[2], 0, col) if next_dim == 0 else (j, nxt, s[2], 0, col)
        return f

    def r_map(flip):
        def f(j, s):
            nxt = 1 - s[next_dim] if flip else s[next_dim]
            return (nxt, j, 0, 0) if next_dim == 0 else (j, nxt, 0, 0)
        return f

    gblk = (None, None, None, rows, cols)
    rblk = (None, None, rows, cols)
    oblk = (None, rows, cols)
    return pl.pallas_call(
        body, name=name,
        grid_spec=pltpu.PrefetchScalarGridSpec(
            num_scalar_prefetch=1, grid=(2,),
            in_specs=[pl.BlockSpec(gblk, g_map(False)), pl.BlockSpec(rblk, r_map(False)),
                      pl.BlockSpec(gblk, g_map(True)), pl.BlockSpec(rblk, r_map(True))],
            out_specs=[pl.BlockSpec(oblk, lambda j, s: (j, 0, 0)),
                       pl.BlockSpec(oblk, lambda j, s: (j, 0, 0))]),
        out_shape=[jax.ShapeDtypeStruct((2, rows, cols), F32),
                   jax.ShapeDtypeStruct((2, rows, cols), BF16)],
        compiler_params=_params(),
    )(sel, g, r, g, r)


def rs_add_first(g, r, sel, name):
    _, _, _, rows, cols = g.shape
    tr = rows // 2

    def body(sel_ref, gk_ref, rk_ref, gs_ref, rs_ref, keep_ref, send_ref):
        keep_ref[...] = gk_ref[...] + rk_ref[...]
        send_ref[...] = (gs_ref[...] + rs_ref[...]).astype(BF16)

    blk = (None, None, None, tr, cols)
    rblk = (None, None, tr, cols)
    oblk = (None, tr, cols)
    return pl.pallas_call(
        body, name=name,
        grid_spec=pltpu.PrefetchScalarGridSpec(
            num_scalar_prefetch=1, grid=(2, 2),
            in_specs=[
                pl.BlockSpec(blk, lambda j, i, s: (s[0], s[1], j, i, 0)),
                pl.BlockSpec(rblk, lambda j, i, s: (s[1], j, i, 0)),
                pl.BlockSpec(blk, lambda j, i, s: (s[0], 1 - s[1], j, i, 0)),
                pl.BlockSpec(rblk, lambda j, i, s: (1 - s[1], j, i, 0)),
            ],
            out_specs=[pl.BlockSpec(oblk, lambda j, i, s: (j, i, 0)),
                       pl.BlockSpec(oblk, lambda j, i, s: (j, i, 0))]),
        out_shape=[jax.ShapeDtypeStruct((2, rows, cols), F32),
                   jax.ShapeDtypeStruct((2, rows, cols), BF16)],
        compiler_params=_params(),
    )(sel, g, r, g, r)


def rs_add_second(k, r, sel, name):
    _, rows, cols = k.shape
    tr = rows // 2 if rows % 32 == 0 else rows
    nt = rows // tr

    def body(sel_ref, kk_ref, rk_ref, ks_ref, rs_ref, keep_ref, send_ref):
        keep_ref[...] = kk_ref[...] + rk_ref[...].astype(F32)
        send_ref[...] = (ks_ref[...] + rs_ref[...].astype(F32)).astype(BF16)

    blk = (None, tr, cols)
    oblk = (tr, cols)
    return pl.pallas_call(
        body, name=name,
        grid_spec=pltpu.PrefetchScalarGridSpec(
            num_scalar_prefetch=1, grid=(nt,),
            in_specs=[
                pl.BlockSpec(blk, lambda i, s: (s[0], i, 0)),
                pl.BlockSpec(blk, lambda i, s: (s[0], i, 0)),
                pl.BlockSpec(blk, lambda i, s: (1 - s[0], i, 0)),
                pl.BlockSpec(blk, lambda i, s: (1 - s[0], i, 0)),
            ],
            out_specs=[pl.BlockSpec(oblk, lambda i, s: (i, 0)),
                       pl.BlockSpec(oblk, lambda i, s: (i, 0))]),
        out_shape=[jax.ShapeDtypeStruct((rows, cols), F32),
                   jax.ShapeDtypeStruct((rows, cols), BF16)],
        compiler_params=_params(),
    )(sel, k, r, k, r)


SEG_ROWS = (4800, 5824, 6848, 4096, 0, 1024, 2048, 3072)
LAT_ROWS = QL + KVL + ROPE
N_IN = 7872


def _seg_row(j):
    return pl.multiple_of(jnp.where(j < 3, 4800 + 1024 * j, jnp.where(j == 3, 4096, (j - 4) * 1024)), 8)


def proj_matmul(h, wt_bits, token):
    t = h.shape[0]
    tm = min(1024, t)

    def body(h_ref, w_hbm, tok_ref, o_ref, wt_ref, buf, sem):
        j = pl.program_id(0)

        @pl.when(pl.program_id(1) == 0)
        def _():
            cp = pltpu.make_async_copy(w_hbm.at[pl.ds(_seg_row(j), D)], buf, sem)
            cp.start()
            cp.wait()
            bits = pltpu.bitcast(buf[...], jnp.uint32)
            row = lax.broadcasted_iota(jnp.int32, (D, D // 2), 0)
            live = jnp.logical_or(j != SEG_LAT, row < LAT_ROWS)
            lo = pltpu.bitcast(bits << 16, F32)
            hi = pltpu.bitcast(bits & jnp.uint32(0xFFFF0000), F32)
            wt_ref[:, :D // 2] = jnp.where(live, lo, 0.0).astype(BF16)
            wt_ref[:, D // 2:] = jnp.where(live, hi, 0.0).astype(BF16)

        o_ref[...] = _dot_nt(h_ref[...], wt_ref[...]).astype(BF16)

    return pl.pallas_call(
        body, name="proj_matmul", grid=(NSEG, t // tm),
        in_specs=[pl.BlockSpec((tm, D), lambda j, i: (i, 0)),
                  pl.BlockSpec(memory_space=pl.ANY),
                  pl.BlockSpec((8, 128), lambda j, i: (0, 0))],
        out_specs=[pl.BlockSpec((None, tm, D), lambda j, i: (j, i, 0)),
                   pl.BlockSpec((D, D), lambda j, i: (j, 0))],
        out_shape=[jax.ShapeDtypeStruct((NSEG, t, D), BF16), jax.ShapeDtypeStruct((NP, D), BF16)],
        scratch_shapes=[pltpu.VMEM((D, D // 2), F32), pltpu.SemaphoreType.DMA],
        compiler_params=_params(("arbitrary", "arbitrary")),
    )(h, wt_bits, token)


def dh_matmul(dproj, wt, token):
    t = dproj.shape[1]
    tm = min(1024, t)

    def body(d_ref, w_ref, tok_ref, o_ref, acc_ref):
        k = pl.program_id(1)

        @pl.when(k == 0)
        def _():
            acc_ref[...] = jnp.zeros_like(acc_ref)

        acc_ref[...] += _dot(d_ref[...], w_ref[...])

        @pl.when(k == NSEG - 1)
        def _():
            o_ref[...] = acc_ref[...]

    return pl.pallas_call(
        body, name="dh_matmul", grid=(t // tm, NSEG),
        in_specs=[pl.BlockSpec((None, tm, D), lambda i, k: (k, i, 0)),
                  pl.BlockSpec((D, D), lambda i, k: (k, 0)),
                  pl.BlockSpec((8, 128), lambda i, k: (0, 0))],
        out_specs=pl.BlockSpec((tm, D), lambda i, k: (i, 0)),
        out_shape=jax.ShapeDtypeStruct((t, D), F32),
        scratch_shapes=[pltpu.VMEM((tm, D), F32)],
        compiler_params=_params(("parallel", "arbitrary")),
    )(dproj, wt, token)


def win_grad_matmul(h, dproj):
    t = h.shape[0]
    tk = min(1024, t)
    nk = t // tk

    def body(h_ref, d_ref, o_hbm, acc_ref, sem):
        j = pl.program_id(0)
        k = pl.program_id(1)

        @pl.when(k == 0)
        def _():
            acc_ref[...] = jnp.zeros_like(acc_ref)

        acc_ref[...] += _dot_tn(d_ref[...], h_ref[...])

        @pl.when(jnp.logical_and(k == nk - 1, j != SEG_LAT))
        def _():
            cp = pltpu.make_async_copy(acc_ref, o_hbm.at[pl.ds(_seg_row(j), D)], sem)
            cp.start()
            cp.wait()

        @pl.when(jnp.logical_and(k == nk - 1, j == SEG_LAT))
        def _():
            cp = pltpu.make_async_copy(acc_ref.at[pl.ds(0, LAT_ROWS)],
                                       o_hbm.at[pl.ds(SEG_ROWS[SEG_LAT], LAT_ROWS)], sem)
            cp.start()
            cp.wait()

    return pl.pallas_call(
        body, name="win_grad_matmul", grid=(NSEG, nk),
        in_specs=[pl.BlockSpec((tk, D), lambda j, k: (k, 0)),
                  pl.BlockSpec((None, tk, D), lambda j, k: (j, k, 0))],
        out_specs=pl.BlockSpec(memory_space=pl.ANY),
        out_shape=jax.ShapeDtypeStruct((N_IN, D), F32),
        scratch_shapes=[pltpu.VMEM((D, D), F32), pltpu.SemaphoreType.DMA],
        compiler_params=_params(("arbitrary", "arbitrary")),
    )(h, dproj)


def grad_matmul(a, b, name):
    t, m = a.shape
    n = b.shape[1]
    tk = min(1024, t)
    nk = t // tk

    def body(a_ref, b_ref, o_ref, acc_ref):
        k = pl.program_id(0)

        @pl.when(k == 0)
        def _():
            acc_ref[...] = jnp.zeros_like(acc_ref)

        acc_ref[...] += _dot_tn(a_ref[...], b_ref[...])

        @pl.when(k == nk - 1)
        def _():
            o_ref[...] = acc_ref[...]

    return pl.pallas_call(
        body, name=name, grid=(nk,),
        in_specs=[pl.BlockSpec((tk, m), lambda k: (k, 0)),
                  pl.BlockSpec((tk, n), lambda k: (k, 0))],
        out_specs=pl.BlockSpec((m, n), lambda k: (0, 0)),
        out_shape=jax.ShapeDtypeStruct((m, n), F32),
        scratch_shapes=[pltpu.VMEM((m, n), F32)],
        compiler_params=_params(("arbitrary",)),
    )(a, b)


def ada_fwd(c_all, w_ada, b_cols):
    def body(c_ref, w_ref, b_ref, o_ref):
        o_ref[...] = _dot(c_ref[...].astype(BF16), w_ref[...].astype(BF16)) + b_ref[...]

    return pl.pallas_call(
        body, name="ada_fwd",
        out_shape=jax.ShapeDtypeStruct((c_all.shape[0], w_ada.shape[1]), F32),
        compiler_params=_params(),
    )(c_all, w_ada, b_cols)


def ada_bwd(c_all, dmod_cols):
    def body(c_ref, d_ref, o_ref):
        o_ref[...] = _dot_tn(c_ref[...].astype(BF16), d_ref[...].astype(BF16))

    return pl.pallas_call(
        body, name="ada_bwd",
        out_shape=jax.ShapeDtypeStruct((c_all.shape[1], dmod_cols.shape[1]), F32),
        compiler_params=_params(),
    )(c_all, dmod_cols)


def slot_sum(g):
    def body(g_ref, o_ref):
        acc = g_ref[0]
        for s in range(1, 8):
            acc = acc + g_ref[s]
        o_ref[...] = acc

    return pl.pallas_call(
        body, name="slot_sum",
        out_shape=jax.ShapeDtypeStruct(g.shape[1:], F32),
    )(g)


def prenorm_fwd(x2, scale, shift, g_pre, seq):
    t = x2.shape[0]
    tm = min(512, seq)
    tpb = seq // tm

    def body(x_ref, sc_ref, sh_ref, g_ref, h_ref):
        xv = x_ref[...]
        r = lax.rsqrt(jnp.mean(xv * xv, axis=-1, keepdims=True) + EPS)
        hv = (xv * r * g_ref[...]) * (1.0 + sc_ref[...]) + sh_ref[...]
        h_ref[...] = hv.astype(BF16)

    per_batch = pl.BlockSpec((None, 1, D), lambda i: (i // tpb, 0, 0))
    return pl.pallas_call(
        body, name="prenorm_fwd", grid=(t // tm,),
        in_specs=[pl.BlockSpec((tm, D), lambda i: (i, 0)), per_batch, per_batch,
                  pl.BlockSpec((1, D), lambda i: (0, 0))],
        out_specs=pl.BlockSpec((tm, D), lambda i: (i, 0)),
        out_shape=jax.ShapeDtypeStruct((t, D), BF16),
        compiler_params=_params(("parallel",)),
    )(x2, scale, shift, g_pre)


def prenorm_bwd(dh, x2, dout, scale, g_pre, seq, token):
    t = x2.shape[0]
    nb = t // seq
    tm = min(512, seq)
    tpb = seq // tm

    def body(dh_ref, x_ref, do_ref, sc_ref, g_ref, tok_ref, gx_ref, dsh_ref, dsc_ref, dg_ref):
        i = pl.program_id(0)
        xv = x_ref[...]
        dhv = dh_ref[...]
        g = g_ref[...]
        r = lax.rsqrt(jnp.mean(xv * xv, axis=-1, keepdims=True) + EPS)
        nrm = xv * r
        dxn = dhv * (1.0 + sc_ref[...])
        dn = dxn * g
        dx = r * (dn - nrm * jnp.mean(dn * nrm, axis=-1, keepdims=True))
        gx_ref[...] = dx + do_ref[...]

        @pl.when(i % tpb == 0)
        def _():
            dsh_ref[...] = jnp.zeros_like(dsh_ref)
            dsc_ref[...] = jnp.zeros_like(dsc_ref)

        @pl.when(i == 0)
        def _():
            dg_ref[...] = jnp.zeros_like(dg_ref)

        dsh_ref[...] += jnp.sum(dhv, axis=0, keepdims=True)
        dsc_ref[...] += jnp.sum(dhv * (nrm * g), axis=0, keepdims=True)
        dg_ref[...] += jnp.sum(dxn * nrm, axis=0, keepdims=True)

    row = pl.BlockSpec((tm, D), lambda i: (i, 0))
    per_batch = pl.BlockSpec((None, 1, D), lambda i: (i // tpb, 0, 0))
    vec = pl.BlockSpec((1, D), lambda i: (0, 0))
    return pl.pallas_call(
        body, name="prenorm_bwd", grid=(t // tm,),
        in_specs=[row, row, row, per_batch, vec, pl.BlockSpec((8, 128), lambda i: (0, 0))],
        out_specs=[row, per_batch, per_batch, vec],
        out_shape=[jax.ShapeDtypeStruct((t, D), F32),
                   jax.ShapeDtypeStruct((nb, 1, D), F32),
                   jax.ShapeDtypeStruct((nb, 1, D), F32),
                   jax.ShapeDtypeStruct((1, D), F32)],
        compiler_params=_params(("arbitrary",)),
    )(dh, x2, dout, scale, g_pre, token)


CONV_TC = 128


def _shift_down(u, k, rows):
    idx = lax.broadcasted_iota(jnp.int32, u.shape, 0)
    return jnp.where(idx >= k, pltpu.roll(u, k, 0), 0.0)


def _shift_up(u, k, rows):
    idx = lax.broadcasted_iota(jnp.int32, u.shape, 0)
    return jnp.where(idx < rows - k, pltpu.roll(u, rows - k, 0), 0.0)


def conv_fwd(proj, conv_w, seq):
    t = proj.shape[1]
    nb = t // seq

    def body(p_ref, w_ref, y_ref):
        av = p_ref[0].astype(F32)
        ab = p_ref[1].astype(F32)
        ac = p_ref[2].astype(F32)
        az = p_ref[3].astype(F32)
        w = w_ref[...]
        u = ac * av
        y1 = _shift_down(u, 2, seq) * w[0:1] + _shift_down(u, 1, seq) * w[1:2] + u * w[2:3]
        y_ref[...] = (ab * y1 * (az * _sig(az))).astype(BF16)

    return pl.pallas_call(
        body, name="conv_fwd", grid=(nb, D // CONV_TC),
        in_specs=[pl.BlockSpec((4, seq, CONV_TC), lambda b, ci: (1, b, ci)),
                  pl.BlockSpec((8, CONV_TC), lambda b, ci: (0, ci))],
        out_specs=pl.BlockSpec((seq, CONV_TC), lambda b, ci: (b, ci)),
        out_shape=jax.ShapeDtypeStruct((t, D), BF16),
        compiler_params=_params(("parallel", "parallel")),
    )(proj, conv_w)


def conv_bwd(dproj, proj, dy, conv_w, seq):
    t = proj.shape[1]
    nb = t // seq

    def body(dp_in_ref, p_ref, dy_ref, w_ref, dp_ref, dw_ref):
        b = pl.program_id(1)
        av = p_ref[0].astype(F32)
        ab = p_ref[1].astype(F32)
        ac = p_ref[2].astype(F32)
        az = p_ref[3].astype(F32)
        dyv = dy_ref[...].astype(F32)
        w = w_ref[...]
        u = ac * av
        u1 = _shift_down(u, 1, seq)
        u2 = _shift_down(u, 2, seq)
        y1 = u2 * w[0:1] + u1 * w[1:2] + u * w[2:3]
        sz = _sig(az)
        silu = az * sz
        dy1 = dyv * ab * silu
        du = dy1 * w[2:3] + _shift_up(dy1, 1, seq) * w[1:2] + _shift_up(dy1, 2, seq) * w[0:1]
        dp_ref[0] = (du * ac).astype(BF16)
        dp_ref[1] = (dyv * y1 * silu).astype(BF16)
        dp_ref[2] = (du * av).astype(BF16)
        dp_ref[3] = (dyv * ab * y1 * (sz * (1.0 + az * (1.0 - sz)))).astype(BF16)

        @pl.when(b == 0)
        def _():
            dw_ref[...] = jnp.zeros_like(dw_ref)

        dw_ref[0:1, :] += jnp.sum(dy1 * u2, axis=0, keepdims=True)
        dw_ref[1:2, :] += jnp.sum(dy1 * u1, axis=0, keepdims=True)
        dw_ref[2:3, :] += jnp.sum(dy1 * u, axis=0, keepdims=True)

    return pl.pallas_call(
        body, name="conv_bwd", grid=(D // CONV_TC, nb),
        in_specs=[pl.BlockSpec(memory_space=pl.ANY),
                  pl.BlockSpec((4, seq, CONV_TC), lambda ci, b: (1, b, ci)),
                  pl.BlockSpec((seq, CONV_TC), lambda ci, b: (b, ci)),
                  pl.BlockSpec((8, CONV_TC), lambda ci, b: (0, ci))],
        out_specs=[pl.BlockSpec((4, seq, CONV_TC), lambda ci, b: (1, b, ci)),
                   pl.BlockSpec((8, CONV_TC), lambda ci, b: (0, ci))],
        out_shape=[jax.ShapeDtypeStruct(dproj.shape, BF16),
                   jax.ShapeDtypeStruct((8, D), F32)],
        input_output_aliases={0: 0},
        compiler_params=_params(("parallel", "arbitrary")),
    )(dproj, proj, dy, conv_w)


def _rope_tables(pos_ref, invf_ref, ma_ref, mb_ref):
    ang = pos_ref[...].astype(F32) * invf_ref[...]
    cs = jnp.cos(ang)
    sn = jnp.sin(ang)
    return cs, sn * ma_ref[...], sn * mb_ref[...]


def _head_tables(cs, sa, sb):
    one = jnp.ones_like(cs)
    zero = jnp.zeros_like(cs)
    return (jnp.tile(jnp.concatenate([one, cs], axis=1), (1, H)),
            jnp.tile(jnp.concatenate([zero, sa], axis=1), (1, H)),
            jnp.tile(jnp.concatenate([zero, sb], axis=1), (1, H)))


def _rotate(v, cs, sa, sb, sign):
    width = v.shape[1]
    return v * cs + sign * (pltpu.roll(v, width - HALF, 1) * sa + pltpu.roll(v, HALF, 1) * sb)


MLA_TM = 256


def mla_prep_fwd(proj, pos, g_q, g_kv, wuq, wukv, tabs):
    t = proj.shape[1]
    tm = min(MLA_TM, t)

    def body(lat_ref, pos_ref, gq_ref, gkv_ref, wuq_ref, wukv_ref, invf_ref, ma_ref, mb_ref,
             q_ref, k_ref, kv_ref, qn_ref, kvn_ref):
        lat = lat_ref[...].astype(F32)
        ql = lat[:, :QL]
        kl = lat[:, QL:QL + KVL]
        kr = lat[:, QL + KVL:QL + KVL + 128]
        qn = (ql * lax.rsqrt(jnp.mean(ql * ql, axis=-1, keepdims=True) + EPS) * gq_ref[...]).astype(BF16)
        kvn = (kl * lax.rsqrt(jnp.mean(kl * kl, axis=-1, keepdims=True) + EPS) * gkv_ref[...]).astype(BF16)
        qn_ref[...] = qn
        kvn_ref[...] = kvn
        cs, sa, sb = _rope_tables(pos_ref, invf_ref, ma_ref, mb_ref)
        hc, ha, hb = _head_tables(cs, sa, sb)
        q = _dot(qn, wuq_ref[...])
        q_ref[...] = (_rotate(q, hc, ha, hb, 1.0) * (SM_SCALE * LOG2E)).astype(BF16)
        kv = _dot(kvn, wukv_ref[...]).astype(BF16)
        kv_ref[...] = kv
        kpe = _rotate(kr, cs, sa, sb, 1.0).astype(BF16)
        for hh in range(H):
            k_ref[:, hh * DQK:hh * DQK + 128] = kv[:, hh * DQK:hh * DQK + 128]
            k_ref[:, hh * DQK + 128:(hh + 1) * DQK] = kpe

    row = lambda w: pl.BlockSpec((tm, w), lambda i: (i, 0))
    const = lambda a: pl.BlockSpec(a.shape, lambda i: (0,) * a.ndim)
    return pl.pallas_call(
        body, name="mla_prep_fwd", grid=(t // tm,),
        in_specs=[pl.BlockSpec((None, tm, D), lambda i: (SEG_LAT, i, 0)), row(1),
                  const(g_q), const(g_kv), const(wuq), const(wukv)] + [const(a) for a in tabs],
        out_specs=[row(H * DQK), row(H * DQK), row(H * DQK), row(QL), row(KVL)],
        out_shape=[jax.ShapeDtypeStruct((t, H * DQK), BF16)] * 3
        + [jax.ShapeDtypeStruct((t, QL), BF16), jax.ShapeDtypeStruct((t, KVL), BF16)],
        compiler_params=_params(("parallel",)),
    )(proj, pos, g_q, g_kv, wuq, wukv, *tabs)


def mla_prep_bwd(dproj, proj, dq_rot, dk, dv, pos, g_q, g_kv, wuq, wukv, tabs):
    t = proj.shape[1]
    tm = min(MLA_TM, t)

    def body(dp_in_ref, lat_ref, dqr_ref, dk_ref, dv_ref, pos_ref, gq_ref, gkv_ref, wuq_ref, wukv_ref,
             invf_ref, ma_ref, mb_ref, dp_ref, dq_ref, dkv_ref, dgq_ref, dgkv_ref):
        i = pl.program_id(0)
        lat = lat_ref[...].astype(F32)
        ql = lat[:, :QL]
        kl = lat[:, QL:QL + KVL]
        rq = lax.rsqrt(jnp.mean(ql * ql, axis=-1, keepdims=True) + EPS)
        rk = lax.rsqrt(jnp.mean(kl * kl, axis=-1, keepdims=True) + EPS)
        nq = ql * rq
        nk = kl * rk
        cs, sa, sb = _rope_tables(pos_ref, invf_ref, ma_ref, mb_ref)
        hc, ha, hb = _head_tables(cs, sa, sb)
        dq = _rotate(dqr_ref[...] * SM_SCALE, hc, ha, hb, -1.0).astype(BF16)
        dq_ref[...] = dq
        dkpe = jnp.zeros((tm, 128), F32)
        for hh in range(H):
            dkv_ref[:, hh * DQK:hh * DQK + 128] = dk_ref[:, hh * DQK:hh * DQK + 128]
            dkv_ref[:, hh * DQK + 128:(hh + 1) * DQK] = dv_ref[:, hh * DV:(hh + 1) * DV]
            dkpe = dkpe + dk_ref[:, hh * DQK + 128:(hh + 1) * DQK].astype(F32)
        lane = lax.broadcasted_iota(jnp.int32, (tm, 128), 1)
        dkr = jnp.where(lane < ROPE, _rotate(dkpe, cs, sa, sb, -1.0), 0.0)
        dqn = _dot_nt(dq, wuq_ref[...])
        dkvn = _dot_nt(dkv_ref[...], wukv_ref[...])
        gq = gq_ref[...]
        gkv = gkv_ref[...]
        dnq = dqn * gq
        dnk = dkvn * gkv
        dql = rq * (dnq - nq * jnp.mean(dnq * nq, axis=-1, keepdims=True))
        dkl = rk * (dnk - nk * jnp.mean(dnk * nk, axis=-1, keepdims=True))
        dp_ref[:, :QL] = dql.astype(BF16)
        dp_ref[:, QL:QL + KVL] = dkl.astype(BF16)
        dp_ref[:, QL + KVL:QL + KVL + 128] = dkr.astype(BF16)
        dp_ref[:, QL + KVL + 128:] = jnp.zeros((tm, D - QL - KVL - 128), BF16)

        @pl.when(i == 0)
        def _():
            dgq_ref[...] = jnp.zeros_like(dgq_ref)
            dgkv_ref[...] = jnp.zeros_like(dgkv_ref)

        dgq_ref[...] += jnp.sum(dqn * nq, axis=0, keepdims=True)
        dgkv_ref[...] += jnp.sum(dkvn * nk, axis=0, keepdims=True)

    row = lambda w: pl.BlockSpec((tm, w), lambda i: (i, 0))
    const = lambda a: pl.BlockSpec(a.shape, lambda i: (0,) * a.ndim)
    seg = pl.BlockSpec((None, tm, D), lambda i: (SEG_LAT, i, 0))
    return pl.pallas_call(
        body, name="mla_prep_bwd", grid=(t // tm,),
        in_specs=[pl.BlockSpec(memory_space=pl.ANY), seg, row(H * DQK), row(H * DQK), row(H * DV), row(1),
                  const(g_q), const(g_kv), const(wuq), const(wukv)] + [const(a) for a in tabs],
        out_specs=[seg, row(H * DQK), row(H * DQK),
                   pl.BlockSpec((1, QL), lambda i: (0, 0)), pl.BlockSpec((1, KVL), lambda i: (0, 0))],
        out_shape=[jax.ShapeDtypeStruct(dproj.shape, BF16),
                   jax.ShapeDtypeStruct((t, H * DQK), BF16), jax.ShapeDtypeStruct((t, H * DQK), BF16),
                   jax.ShapeDtypeStruct((1, QL), F32), jax.ShapeDtypeStruct((1, KVL), F32)],
        input_output_aliases={0: 0},
        compiler_params=_params(("arbitrary",)),
    )(dproj, proj, dq_rot, dk, dv, pos, g_q, g_kv, wuq, wukv, *tabs)


def _causal_mask(s, n):
    row = lax.broadcasted_iota(jnp.int32, (n, n), 0)
    col = lax.broadcasted_iota(jnp.int32, (n, n), 1)
    return jnp.where(col <= row, s, -1e30)


def flash_fwd(q, k, kv, nb, seq):
    t = q.shape[0]
    tq = min(FLASH_TQ, seq)
    nq = seq // tq

    def body(q_ref, k_ref, v_ref, o_ref, lse_ref):
        for qi in range(nq):
            qs = slice(qi * tq, (qi + 1) * tq)
            qv = q_ref[qs, :]
            m = jnp.full((tq, 1), -1e30, F32)
            l = jnp.zeros((tq, 1), F32)
            acc = jnp.zeros((tq, DV), F32)
            for j in range(qi + 1):
                ks = slice(j * tq, (j + 1) * tq)
                s = _dot_nt(qv, k_ref[ks, :])
                if j == qi:
                    s = _causal_mask(s, tq)
                m_new = jnp.maximum(m, jnp.max(s, axis=1, keepdims=True))
                p = jnp.exp2(s - m_new)
                alpha = jnp.exp2(m - m_new)
                l = alpha * l + jnp.sum(p, axis=1, keepdims=True)
                acc = alpha * acc + _dot(p.astype(BF16), v_ref[ks, :])
                m = m_new
            o_ref[qs, :] = (acc / l).astype(BF16)
            lse_ref[qs, :] = jnp.broadcast_to(m + jnp.log(l) * LOG2E, (tq, DV))

    out_blk = pl.BlockSpec((seq, DV), lambda b, h: (b, h))
    return pl.pallas_call(
        body, name="flash_fwd", grid=(nb, H),
        in_specs=[pl.BlockSpec((seq, DQK), lambda b, h: (b, h)),
                  pl.BlockSpec((seq, DQK), lambda b, h: (b, h)),
                  pl.BlockSpec((seq, DV), lambda b, h: (b, 2 * h + 1))],
        out_specs=[out_blk, out_blk],
        out_shape=[jax.ShapeDtypeStruct((t, H * DV), BF16), jax.ShapeDtypeStruct((t, H * DV), F32)],
        compiler_params=_params(("parallel", "parallel")),
    )(q, k, kv)


def flash_bwd(q, k, kv, o, do, lse, nb, seq):
    t = q.shape[0]
    tq = min(FLASH_TQ, seq)
    nq = seq // tq

    def body(q_ref, k_ref, v_ref, o_ref, do_ref, lse_ref, dq_ref, dk_ref, dv_ref):
        delta = []
        for qi in range(nq):
            qs = slice(qi * tq, (qi + 1) * tq)
            delta.append(jnp.sum(do_ref[qs, :].astype(F32) * o_ref[qs, :].astype(F32), axis=1, keepdims=True))
        for ki in range(nq):
            ks = slice(ki * tq, (ki + 1) * tq)
            kb = k_ref[ks, :]
            vb = v_ref[ks, :]
            dk = jnp.zeros((tq, DQK), F32)
            dv = jnp.zeros((tq, DV), F32)
            for qi in range(ki, nq):
                qs = slice(qi * tq, (qi + 1) * tq)
                qv = q_ref[qs, :]
                dov = do_ref[qs, :]
                s = _dot_nt(qv, kb)
                if qi == ki:
                    s = _causal_mask(s, tq)
                p = jnp.exp2(s - lse_ref[qs, :][:, :1])
                dp = _dot_nt(dov, vb)
                dz = (p * (dp - delta[qi])).astype(BF16)
                dv = dv + _dot_tn(p.astype(BF16), dov)
                dk = dk + _dot_tn(dz, qv)
                dqb = _dot(dz, kb)
                if ki == 0:
                    dq_ref[qs, :] = dqb
                else:
                    dq_ref[qs, :] += dqb
            dk_ref[ks, :] = (dk * LN2).astype(BF16)
            dv_ref[ks, :] = dv.astype(BF16)

    full = lambda w, col: pl.BlockSpec((seq, w), col)
    same = lambda b, h: (b, h)
    return pl.pallas_call(
        body, name="flash_bwd", grid=(nb, H),
        in_specs=[full(DQK, same), full(DQK, same), full(DV, lambda b, h: (b, 2 * h + 1)),
                  full(DV, same), full(DV, same), full(DV, same)],
        out_specs=[full(DQK, same), full(DQK, same), full(DV, same)],
        out_shape=[jax.ShapeDtypeStruct((t, H * DQK), F32), jax.ShapeDtypeStruct((t, H * DQK), BF16),
                   jax.ShapeDtypeStruct((t, H * DV), BF16)],
        compiler_params=_params(("parallel", "parallel")),
    )(q, k, kv, o, do, lse)


TAIL_TM = 256


def tail_fwd(y, attn, proj, x2, tgt, gate, g_post, wco, wmo, wout, seq):
    t = y.shape[0]
    nb = t // seq
    tm = min(TAIL_TM, seq)
    tpb = seq // tm

    def body(y_ref, at_ref, p_ref, x_ref, t_ref, gate_ref, gp_ref, wco_ref, wmo_ref, wout_ref,
             o_ref, ya_ref, yb_ref, m_ref, do2_ref, dout_ref, dgate_ref, dgp_ref, loss_ref):
        i = pl.program_id(0)
        bz = p_ref[0].astype(F32)
        ga = p_ref[1].astype(F32)
        gb = p_ref[2].astype(F32)
        ov = (at_ref[...].astype(F32) * (bz * _sig(bz))).astype(BF16)
        o_ref[...] = ov
        ya = _dot(y_ref[...], wco_ref[...])
        yb = _dot(ov, wmo_ref[...])
        ya_ref[...] = ya.astype(BF16)
        yb_ref[...] = yb.astype(BF16)
        mv = (_sig(ga) * ya + _sig(gb) * yb).astype(BF16)
        m_ref[...] = mv
        o2 = _dot(mv, wout_ref[...])
        r = lax.rsqrt(jnp.mean(o2 * o2, axis=-1, keepdims=True) + EPS)
        nrm = o2 * r
        gp = gp_ref[...]
        gate_v = gate_ref[...]
        rn = nrm * gp
        err = x_ref[...] + gate_v * rn - t_ref[...]
        dout = err * (1.0 / D)
        dout_ref[...] = dout
        dn = dout * gate_v * gp
        do2_ref[...] = (r * (dn - nrm * jnp.mean(dn * nrm, axis=-1, keepdims=True))).astype(BF16)

        @pl.when(i % tpb == 0)
        def _():
            dgate_ref[...] = jnp.zeros_like(dgate_ref)

        @pl.when(i == 0)
        def _():
            dgp_ref[...] = jnp.zeros_like(dgp_ref)
            loss_ref[...] = jnp.zeros_like(loss_ref)

        dgate_ref[...] += jnp.sum(dout * rn, axis=0, keepdims=True)
        dgp_ref[...] += jnp.sum(dout * gate_v * nrm, axis=0, keepdims=True)
        loss_ref[...] += 0.5 * jnp.sum(jnp.mean(err * err, axis=-1, keepdims=True), axis=0, keepdims=True)

    row = pl.BlockSpec((tm, D), lambda i: (i, 0))
    per_batch = pl.BlockSpec((None, 1, D), lambda i: (i // tpb, 0, 0))
    vec = pl.BlockSpec((1, D), lambda i: (0, 0))
    wgt = pl.BlockSpec((D, D), lambda i: (0, 0))
    act = jax.ShapeDtypeStruct((t, D), BF16)
    return pl.pallas_call(
        body, name="tail_fwd", grid=(t // tm,),
        in_specs=[row, row, pl.BlockSpec((3, tm, D), lambda i: (0, i, 0)), row, row, per_batch, vec,
                  wgt, wgt, wgt],
        out_specs=[row, row, row, row, row, row, per_batch, vec, pl.BlockSpec((1, 1), lambda i: (0, 0))],
        out_shape=[act, act, act, act, act, jax.ShapeDtypeStruct((t, D), F32),
                   jax.ShapeDtypeStruct((nb, 1, D), F32), jax.ShapeDtypeStruct((1, D), F32),
                   jax.ShapeDtypeStruct((1, 1), F32)],
        compiler_params=_params(("arbitrary",)),
    )(y, attn, proj, x2, tgt, gate, g_post, wco, wmo, wout)


def tail_bwd(do2, proj, ya, yb, attn, wout, wmo, wco):
    t = do2.shape[0]
    tm = min(TAIL_TM, t)

    def body(do2_ref, p_ref, ya_ref, yb_ref, at_ref, wout_ref, wmo_ref, wco_ref,
             dp_ref, dya_ref, dyb_ref, dat_ref, dy_ref):
        bz = p_ref[0].astype(F32)
        ga = p_ref[1].astype(F32)
        gb = p_ref[2].astype(F32)
        dm = _dot_nt(do2_ref[...], wout_ref[...])
        sa = _sig(ga)
        sb = _sig(gb)
        dya = (dm * sa).astype(BF16)
        dyb = (dm * sb).astype(BF16)
        dya_ref[...] = dya
        dyb_ref[...] = dyb
        dp_ref[1] = (dm * ya_ref[...].astype(F32) * (sa * (1.0 - sa))).astype(BF16)
        dp_ref[2] = (dm * yb_ref[...].astype(F32) * (sb * (1.0 - sb))).astype(BF16)
        dov = _dot_nt(dyb, wmo_ref[...])
        sz = _sig(bz)
        dat_ref[...] = (dov * (bz * sz)).astype(BF16)
        dp_ref[0] = (dov * at_ref[...].astype(F32) * (sz * (1.0 + bz * (1.0 - sz)))).astype(BF16)
        dy_ref[...] = _dot_nt(dya, wco_ref[...]).astype(BF16)

    row = pl.BlockSpec((tm, D), lambda i: (i, 0))
    seg3 = pl.BlockSpec((3, tm, D), lambda i: (0, i, 0))
    wgt = pl.BlockSpec((D, D), lambda i: (0, 0))
    act = jax.ShapeDtypeStruct((t, D), BF16)
    return pl.pallas_call(
        body, name="tail_bwd", grid=(t // tm,),
        in_specs=[row, seg3, row, row, row, wgt, wgt, wgt],
        out_specs=[seg3, row, row, row, row],
        out_shape=[jax.ShapeDtypeStruct((NSEG, t, D), BF16), act, act, act, act],
        compiler_params=_params(("parallel",)),
    )(do2, proj, ya, yb, attn, wout, wmo, wco)


def adamw(w, m, v, g, g2, name, token=None):
    rows, cols = w.shape
    tr = rows
    for cand in (256, 128, 64, 32, 16, 8):
        if rows % cand == 0 and rows > cand:
            tr = cand
            break
    has2 = g2 is not None
    n_in = 4 + has2

    def body(*refs):
        w_ref, m_ref, v_ref, g_ref = refs[:4]
        go_ref, d_ref, mo_ref, vo_ref = refs[-4:]
        grad = g_ref[...] + refs[4][...].astype(F32) if has2 else g_ref[...]
        mn = ADAM_B1 * m_ref[...] + (1.0 - ADAM_B1) * grad
        vn = ADAM_B2 * v_ref[...] + (1.0 - ADAM_B2) * (grad * grad)
        m_hat = mn / (1.0 - ADAM_B1 ** ADAM_STEP)
        v_hat = vn / (1.0 - ADAM_B2 ** ADAM_STEP)
        go_ref[...] = grad
        d_ref[...] = -ADAM_LR * (m_hat / (jnp.sqrt(v_hat) + ADAM_EPS) + ADAM_WD * w_ref[...])
        mo_ref[...] = mn
        vo_ref[...] = vn

    blk = pl.BlockSpec((tr, cols), lambda i: (i, 0))
    ins = [w, m, v, g] + ([g2] if has2 else [])
    specs = [blk] * n_in
    if token is not None:
        ins.append(token)
        specs.append(pl.BlockSpec((8, 128), lambda i: (0, 0)))
    return pl.pallas_call(
        body, name=name, grid=(rows // tr,),
        in_specs=specs, out_specs=[blk] * 4,
        out_shape=[jax.ShapeDtypeStruct((rows, cols), F32)] * 4,
        compiler_params=_params(("parallel",)),
    )(*ins)


def adamw_win(wt, mt, vt, ka, ra, kb, rb):
    rows = wt.shape[0]
    tc = 256
    nh = (D // 2) // tc

    def body(w_ref, m_ref, v_ref, ka_ref, ra_ref, kb_ref, rb_ref, go_ref, d_ref, mo_ref, vo_ref):
        first = pl.program_id(0) < nh
        grad = jnp.where(first, ka_ref[...] + ra_ref[...].astype(F32), kb_ref[...] + rb_ref[...].astype(F32))
        mn = ADAM_B1 * m_ref[...] + (1.0 - ADAM_B1) * grad
        vn = ADAM_B2 * v_ref[...] + (1.0 - ADAM_B2) * (grad * grad)
        m_hat = mn / (1.0 - ADAM_B1 ** ADAM_STEP)
        v_hat = vn / (1.0 - ADAM_B2 ** ADAM_STEP)
        go_ref[...] = grad
        d_ref[...] = -ADAM_LR * (m_hat / (jnp.sqrt(v_hat) + ADAM_EPS) + ADAM_WD * w_ref[...])
        mo_ref[...] = mn
        vo_ref[...] = vn

    blk = pl.BlockSpec((rows, tc), lambda j: (0, j))
    lo = pl.BlockSpec((rows, tc), lambda j: (0, jnp.minimum(j, nh - 1)))
    hi = pl.BlockSpec((rows, tc), lambda j: (0, jnp.maximum(j - nh, 0)))
    return pl.pallas_call(
        body, name="adamw_w_in", grid=(D // tc,),
        in_specs=[blk, blk, blk, lo, lo, hi, hi], out_specs=[blk] * 4,
        out_shape=[jax.ShapeDtypeStruct((rows, D), F32)] * 4,
        compiler_params=_params(("parallel",)),
    )(wt, mt, vt, ka, ra, kb, rb)


_ORD_A = ("x", "y", "c")
_ORD_B = ("y", "x", "c")


def _to_slots(full, order, col_sharded):
    if col_sharded:
        r = full.shape[0]
        cc = full.shape[1] // 8
        g = full.reshape(r, 2, 2, 2, cc).transpose(1, 2, 3, 0, 4)
    else:
        r = full.shape[0] // 8
        cc = full.shape[1]
        g = full.reshape(2, 2, 2, r, cc)
    names = ("x", "y", "c")
    perm = tuple(names.index(a) for a in order)
    return g.transpose(perm + (3, 4))


def _rows128(a, rows):
    flat = a.reshape(-1)
    return jnp.pad(flat, (0, rows * 128 - flat.shape[0])).reshape(rows, 128)


def kernel(x, c, positions, w_ada, b_ada, g_pre, w_in, conv_w, w_conv_out, g_q, w_uq, g_kv, w_ukv, w_mla_out, w_out, g_post, loss_target, m_w_ada, m_b_ada, m_g_pre, m_w_in, m_conv_w, m_w_conv_out, m_g_q, m_w_uq, m_g_kv, m_w_ukv, m_w_mla_out, m_w_out, m_g_post, v_w_ada, v_b_ada, v_g_pre, v_w_in, v_conv_w, v_w_conv_out, v_g_q, v_w_uq, v_g_kv, v_w_ukv, v_w_mla_out, v_w_out, v_g_post):
    nb, seq, _ = x.shape
    t = nb * seq
    mx, my, mc = lax.axis_index("x"), lax.axis_index("y"), lax.axis_index("c")
    me = 4 * mx + 2 * my + mc
    co = {"x": mx, "y": my, "c": mc}

    x2 = x.reshape(t, D)
    tgt2 = loss_target.reshape(t, D)
    pos2 = positions.reshape(t, 1)

    packed = jnp.concatenate([c.reshape(2 * D // 128, 128), _rows128(conv_w[0], 8)], axis=0)
    gath = small_allgather(packed, "gather_cond")
    c_all = gath[:, :16].reshape(8 * nb, D)
    conv_full = gath[:, 16:19].reshape(8, 3, 128).transpose(1, 0, 2).reshape(3, D)
    conv_full8 = jnp.pad(conv_full, ((0, 5), (0, 0)))
    ada_cols = w_ada.shape[2]
    b_cols = lax.dynamic_slice(b_ada, (0, me * ada_cols), (1, ada_cols))
    mod_part = ada_fwd(c_all, w_ada[0], b_cols)
    mod_g = small_allgather(mod_part.reshape(8 * nb * ada_cols // 128, 128), "gather_mod")
    mod_all = mod_g.reshape(8, 8 * nb, ada_cols).transpose(1, 0, 2).reshape(8 * nb, 8 * ada_cols)
    mod = lax.dynamic_slice(mod_all, (me * nb, 0), (nb, 3 * D))
    shift = mod[:, 0:D].reshape(nb, 1, D)
    scale = mod[:, D:2 * D].reshape(nb, 1, D)
    gate = mod[:, 2 * D:3 * D].reshape(nb, 1, D)

    wt = w_in[0].T.astype(BF16)
    lo = lax.bitcast_convert_type(wt[:, :D // 2], jnp.uint16).astype(jnp.uint32)
    hi = lax.bitcast_convert_type(wt[:, D // 2:], jnp.uint16).astype(jnp.uint32)
    wt_bits = lax.bitcast_convert_type(lo | (hi << 16), F32)
    q4 = D // 4
    gw = allgather_big([wt_bits], [(0, (0, q4), _ORD_A), (0, (q4, q4), _ORD_B)], "gather_w_in")
    late = [w_conv_out[0].astype(BF16), w_mla_out[0].astype(BF16), w_out[0].astype(BF16),
            w_uq[0].astype(BF16), w_ukv[0].astype(BF16)]
    gw0, late = lax.optimization_barrier((gw[0], late))
    late_state, late_token = gather_start(late, "gather_late_start")
    wt_bits_all = gw0.reshape(N_IN, D // 2)

    inv_freq = ROPE_THETA ** (-jnp.arange(0, ROPE, 2, dtype=F32) / ROPE)
    invf = jnp.concatenate([inv_freq, inv_freq, jnp.zeros((128 - ROPE,), F32)]).reshape(1, 128)
    lane = np.arange(128)
    tabs = (invf,
            jnp.asarray(np.where(lane < HALF, -1.0, 0.0).reshape(1, 128), F32),
            jnp.asarray(np.where((lane >= HALF) & (lane < ROPE), 1.0, 0.0).reshape(1, 128), F32))

    h = prenorm_fwd(x2, scale, shift, g_pre, seq)
    proj, wt_p = proj_matmul(h, wt_bits_all, late_token)
    y = conv_fwd(proj, conv_full8, seq)
    gl = gather_wait(late_state, y, "gather_late_wait")
    wco = gl[0].reshape(D, D)
    wmo = gl[1].reshape(D, D)
    wout = gl[2].reshape(D, D)
    wuq_full = gl[3].reshape(8, QL, 192).transpose(1, 0, 2)
    wuq_p = jnp.pad(wuq_full, ((0, 0), (0, 0), (0, DQK - 192))).reshape(QL, H * DQK)
    wukv = gl[4].reshape(8, KVL, 256).transpose(1, 0, 2).reshape(KVL, H * 256)
    q_rot, k_cat, kv, qn, kvn = mla_prep_fwd(proj, pos2, g_q, g_kv, wuq_p, wukv, tabs)
    attn, lse = flash_fwd(q_rot, k_cat, kv, nb, seq)
    o, ya, yb, m, do2, dout, dgate, dg_post, loss_part = tail_fwd(
        y, attn, proj, x2, tgt2, gate, g_post, wco, wmo, wout, seq)

    dproj, dya, dyb, dattn, dy = tail_bwd(do2, proj, ya, yb, attn, wout, wmo, wco)
    g_wout = grad_matmul(m, do2, "grad_w_out")
    g_wmo = grad_matmul(o, dyb, "grad_w_mla_out")
    g_wco = grad_matmul(y, dya, "grad_w_conv_out")
    dproj, dconv = conv_bwd(dproj, proj, dy, conv_full8, seq)
    dq_rot, dk, dv = flash_bwd(q_rot, k_cat, kv, attn, dattn, lse, nb, seq)
    dproj, dq, dkv, dg_q, dg_kv = mla_prep_bwd(dproj, proj, dq_rot, dk, dv, pos2, g_q, g_kv, wuq_p, wukv, tabs)
    g_wuq_p = grad_matmul(qn, dq, "grad_w_uq")
    g_wukv = grad_matmul(kvn, dkv, "grad_w_ukv")
    g_win_p = win_grad_matmul(h, dproj)

    g_wt = g_win_p.reshape(2, 2, 2, N_IN // 8, D)
    g_wuq = g_wuq_p.reshape(QL, H, DQK)[:, :, :192].reshape(QL, H * 192)
    rs_a = ("c", "y", "x")
    rs_b = ("c", "x", "y")
    flat = lambda s: s.reshape(2, 2, 2, -1, 128)
    rest_a = jnp.concatenate([flat(_to_slots(g_wco, rs_a, False)), flat(_to_slots(g_wmo, rs_a, False))], axis=3)
    rest_b = jnp.concatenate([flat(_to_slots(g_wout, rs_b, False)), flat(_to_slots(g_wuq, rs_b, True)),
                              flat(_to_slots(g_wukv, rs_b, True))], axis=3)
    ords = [rs_a, rs_a, rs_b, rs_b]
    hc = D // 2
    win_shape = (2, 2, N_IN // 8, hc)
    pick_w = lambda col: (lambda ref, cc: ref.at[:, :, 1 - cc["c"], :, pl.ds(col * hc, hc)])
    pick_h = lambda ref, cc: ref.at[1 - cc["c"]]
    which1 = [0, 1, 0, 2]
    picks1 = [pick_w(0), pick_h, pick_w(1), pick_h]
    shapes1 = [win_shape, rest_a.shape[1:], win_shape, rest_b.shape[1:]]
    st1, tok1 = swap_start([g_wt, rest_a, rest_b], which1, ["c"] * 4, picks1, shapes1, "rs_c_start")
    dh = dh_matmul(dproj, wt_p, tok1)
    (g_wt, rest_a, rest_b), r1 = swap_wait(st1, dh, which1, ["c"] * 4, picks1, "rs_c_wait")
    sel_xyc = jnp.stack([mx, my, mc]).astype(jnp.int32)
    sel1 = [jnp.stack([co[o[0]], co[o[1]]]).astype(jnp.int32) for o in ords]
    sel2 = [jnp.stack([co[o[2]]]).astype(jnp.int32) for o in ords]
    first = [rs_win_add_first(g_wt, r1[0], sel_xyc, 1, 0, "rs_add_first_0"),
             rs_add_first(rest_a, r1[1], sel1[1], "rs_add_first_1"),
             rs_win_add_first(g_wt, r1[2], sel_xyc, 0, 1, "rs_add_first_2"),
             rs_add_first(rest_b, r1[3], sel1[3], "rs_add_first_3")]
    keep1, send1 = zip(*first)
    all4 = [0, 1, 2, 3]
    none4 = [None] * 4
    axes2 = [o[1] for o in ords]
    st2, tok2 = swap_start(list(send1), all4, axes2, none4, [s.shape for s in send1], "rs_ici1_start")

    grad_x2, dshift, dscale, dg_pre = prenorm_bwd(dh, x2, dout, scale, g_pre, seq, tok2)

    dmod = jnp.concatenate([dshift, dscale, dgate], axis=2).reshape(nb * 3 * D // 128, 128)
    small = jnp.concatenate([
        dmod, _rows128(dg_pre, 8), _rows128(dg_post, 8), _rows128(dg_q, 8), _rows128(dg_kv, 8),
        dconv[0:3].reshape(24, 128), _rows128(loss_part, 8)], axis=0)
    small_g = small_allgather(small, "gather_small_grads")
    sums = slot_sum(small_g)
    dmod_all = small_g[:, 0:48].reshape(8 * nb, 3 * D)
    g_bada = (sums[0:24] + sums[24:48]).reshape(1, 3 * D)
    g_gpre = sums[48:56].reshape(1, D)
    g_gpost = sums[56:64].reshape(1, D)
    g_gq = sums[64:67].reshape(1, QL)
    g_gkv = sums[72:74].reshape(1, KVL)
    g_conv_full = sums[80:104].reshape(3, D)
    loss = sums[104, 0]
    g_conv = lax.dynamic_slice(g_conv_full, (0, me * 128), (3, 128))
    dmod_cols = lax.dynamic_slice(dmod_all, (0, me * ada_cols), (8 * nb, ada_cols))
    g_wada = ada_bwd(c_all, dmod_cols)

    _, r2 = swap_wait(st2, g_wada, all4, axes2, none4, "rs_ici1_wait")
    keep2, send2 = zip(*[rs_add_second(keep1[a], r2[a], sel2[a], "rs_add_second_%d" % a) for a in range(4)])
    axes3 = [o[2] for o in ords]
    st3, tok3 = swap_start(list(send2), all4, axes3, none4, [s.shape for s in send2], "rs_ici2_start")

    res = {}
    res["w_ada"] = [o_[None] for o_ in adamw(w_ada[0], m_w_ada[0], v_w_ada[0], g_wada, None, "adamw_w_ada", tok3)]

    def pack(b_, gp_, gpo_, gq_, gkv_, cw_):
        return jnp.concatenate([_rows128(b_, 24), _rows128(gp_, 8), _rows128(gpo_, 8), _rows128(gq_, 8),
                                _rows128(gkv_, 8), _rows128(cw_, 8)], axis=0)

    sw = pack(b_ada, g_pre, g_post, g_q, g_kv, conv_w)
    sm = pack(m_b_ada, m_g_pre, m_g_post, m_g_q, m_g_kv, m_conv_w)
    sv = pack(v_b_ada, v_g_pre, v_g_post, v_g_q, v_g_kv, v_conv_w)
    sg = pack(g_bada, g_gpre, g_gpost, g_gq, g_gkv, g_conv)
    small_out = adamw(sw, sm, sv, sg, None, "adamw_small", tok3)

    _, r3 = swap_wait(st3, small_out[0], all4, axes3, none4, "rs_ici2_wait")

    n_sq = D * 128 // 128
    unflat = lambda a, lo, shape: a[lo:lo + shape[0] * shape[1] // 128].reshape(shape)
    sq = (128, D)
    uq_s = (QL, 192)
    ukv_s = (KVL, 256)
    parts = {
        "w_conv_out": (unflat(keep2[1], 0, sq), unflat(r3[1], 0, sq)),
        "w_mla_out": (unflat(keep2[1], n_sq, sq), unflat(r3[1], n_sq, sq)),
        "w_out": (unflat(keep2[3], 0, sq), unflat(r3[3], 0, sq)),
        "w_uq": (unflat(keep2[3], n_sq, uq_s), unflat(r3[3], n_sq, uq_s)),
        "w_ukv": (unflat(keep2[3], n_sq + QL * 192 // 128, ukv_s), unflat(r3[3], n_sq + QL * 192 // 128, ukv_s)),
    }

    res["w_in"] = [o_.T[None] for o_ in adamw_win(w_in[0].T, m_w_in[0].T, v_w_in[0].T,
                                                  keep2[0], r3[0], keep2[2], r3[2])]
    ga, gb = parts["w_uq"]
    res["w_uq"] = [o_.T[None] for o_ in adamw(w_uq[0].T, m_w_uq[0].T, v_w_uq[0].T, ga.T, gb.T, "adamw_w_uq")]
    weights = {"w_conv_out": (w_conv_out, m_w_conv_out, v_w_conv_out),
               "w_ukv": (w_ukv, m_w_ukv, v_w_ukv), "w_mla_out": (w_mla_out, m_w_mla_out, v_w_mla_out),
               "w_out": (w_out, m_w_out, v_w_out)}
    for nm, (wv, mv, vv) in weights.items():
        ga, gb = parts[nm]
        outs = adamw(wv[0], mv[0], vv[0], ga, gb, "adamw_" + nm)
        res[nm] = [o_[None] for o_ in outs]

    def unpack(a):
        return {"b_ada": a[0:24].reshape(1, 3 * D), "g_pre": a[24:32].reshape(1, D),
                "g_post": a[32:40].reshape(1, D), "g_q": a[40:43].reshape(1, QL),
                "g_kv": a[48:50].reshape(1, KVL), "conv_w": a[56:59].reshape(-1)[:3 * 128].reshape(1, 3, 128)}

    for nm in ("b_ada", "g_pre", "g_post", "g_q", "g_kv", "conv_w"):
        res[nm] = [unpack(a)[nm] for a in small_out]

    order = ["w_ada", "b_ada", "g_pre", "w_in", "conv_w", "w_conv_out", "g_q", "w_uq", "g_kv", "w_ukv",
             "w_mla_out", "w_out", "g_post"]
    out = [loss, grad_x2.reshape(nb, seq, D)]
    for k_ in range(4):
        out += [res[nm][k_] for nm in order]
    return tuple(out)
```

```python
import functools

import numpy as np
import jax
import jax.numpy as jnp
from jax import lax
from jax.experimental import pallas as pl
from jax.experimental.pallas import tpu as pltpu

F32 = jnp.float32
BF16 = jnp.bfloat16
MESH = pl.DeviceIdType.MESH

D = 1024
H = 8
QL = 384
KVL = 256
ROPE = 64
HALF = ROPE // 2
DQK = 256
DV = 128
NSEG = 8
NP = NSEG * D
EPS = 1e-6
ROPE_THETA = 10000.0
SM_SCALE = (128 + ROPE) ** -0.5
LOG2E = 1.4426950408889634
LN2 = 0.6931471805599453
FLASH_TQ = 512

SEG_BZ, SEG_GA, SEG_GB, SEG_LAT, SEG_V = 0, 1, 2, 3, 4

ADAM_LR = 0.001
ADAM_B1 = 0.9
ADAM_B2 = 0.999
ADAM_EPS = 1e-08
ADAM_WD = 0.01
ADAM_STEP = 10

VMEM_LIMIT = 56 * 1024 * 1024


def _params(sem=None, vmem=VMEM_LIMIT):
    kw = dict(vmem_limit_bytes=vmem)
    if sem is not None:
        kw["dimension_semantics"] = sem
    return pltpu.CompilerParams(**kw)


def _sig(v):
    return 1.0 / (1.0 + jnp.exp(-v))


def _dot(a, b):
    return jnp.dot(a, b, preferred_element_type=F32)


def _dot_nt(a, b):
    return lax.dot_general(a, b, (((1,), (1,)), ((), ())), preferred_element_type=F32)


def _dot_tn(a, b):
    return lax.dot_general(a, b, (((0,), (0,)), ((), ())), preferred_element_type=F32)


_AXIS_POS = {"x": 0, "y": 1, "c": 2}


def _coords():
    return lax.axis_index("x"), lax.axis_index("y"), lax.axis_index("c")


def _partner(axis):
    p = list(_coords())
    p[_AXIS_POS[axis]] = 1 - p[_AXIS_POS[axis]]
    return tuple(p)


def small_allgather(v, name):
    rows = v.shape[0]

    def body(v_ref, out_ref, send_sems, recv_sems):
        x, y, c = _coords()
        me = 4 * x + 2 * y + c
        out_ref[me] = v_ref[...]
        copies = []
        for k in range(1, 8):
            peer = (1 - x if k & 4 else x, 1 - y if k & 2 else y, 1 - c if k & 1 else c)
            cp = pltpu.make_async_remote_copy(
                src_ref=v_ref, dst_ref=out_ref.at[me],
                send_sem=send_sems.at[k - 1], recv_sem=recv_sems.at[k - 1],
                device_id=peer, device_id_type=MESH)
            cp.start()
            copies.append(cp)
        for cp in copies:
            cp.wait()

    return pl.pallas_call(
        body, name=name,
        out_shape=jax.ShapeDtypeStruct((8, rows, 128), F32),
        in_specs=[pl.BlockSpec(memory_space=pltpu.VMEM)],
        out_specs=pl.BlockSpec(memory_space=pltpu.VMEM),
        scratch_shapes=[pltpu.SemaphoreType.DMA((7,)), pltpu.SemaphoreType.DMA((7,))],
    )(v)


def _own_block_placed(s):
    x, y, c = _coords()
    return lax.dynamic_update_slice(lax.empty((2, 2, 2) + s.shape, s.dtype), s[None, None, None],
                                    (x, y, c) + (0,) * s.ndim)


def allgather_big(arrs, plan, name):
    n = len(arrs)
    m = len(plan)

    def body(*refs):
        ins, outs = refs[n:2 * n], refs[2 * n:3 * n]
        send_sems, recv_sems = refs[3 * n:]
        x, y, c = _coords()
        co = {"x": x, "y": y, "c": c}

        def window(ref, lead, rows, cols):
            win = tuple(slice(None) if w is None else pl.ds(w[0], w[1]) for w in (rows, cols))
            return ref.at[tuple(lead) + win]

        def held(e, free):
            i, rows, cols, _ = plan[e]
            lead = [slice(None) if ax in free else co[ax] for ax in ("x", "y", "c")]
            return window(outs[i], lead, rows, cols)

        def rcopy(e, stage, src, dst, axis):
            return pltpu.make_async_remote_copy(
                src_ref=src, dst_ref=dst,
                send_sem=send_sems.at[e, stage], recv_sem=recv_sems.at[e, stage],
                device_id=_partner(axis), device_id_type=MESH)

        stages = [[], [], []]
        for e, (i, rows, cols, order) in enumerate(plan):
            cp = rcopy(e, 0, window(ins[i], [], rows, cols), held(e, ()), order[0])
            cp.start()
            stages[0].append(cp)
        for s in (1, 2):
            for e, (i, rows, cols, order) in enumerate(plan):
                stages[s - 1][e].wait_recv()
                blk = held(e, order[:s])
                cp = rcopy(e, s, blk, blk, order[s])
                cp.start()
                stages[s].append(cp)
        for e in range(m):
            stages[2][e].wait_recv()
        for e in range(m):
            for s in range(3):
                stages[s][e].wait_send()

    any_spec = pl.BlockSpec(memory_space=pl.ANY)
    lands = [_own_block_placed(a) for a in arrs]
    return pl.pallas_call(
        body, name=name,
        out_shape=[jax.ShapeDtypeStruct(l.shape, l.dtype) for l in lands],
        in_specs=[any_spec] * (2 * n),
        out_specs=[any_spec] * n,
        input_output_aliases={i: i for i in range(n)},
        scratch_shapes=[pltpu.SemaphoreType.DMA((m, 3)), pltpu.SemaphoreType.DMA((m, 3))],
    )(*lands, *arrs)


def exchange(arrs, axes, picks, out_shapes, name):
    n = len(arrs)

    def body(*refs):
        ins, outs = refs[:n], refs[n:2 * n]
        send_sems, recv_sems = refs[2 * n:]
        x, y, c = _coords()
        co = {"x": x, "y": y, "c": c}
        copies = []
        for a in range(n):
            src = ins[a] if picks[a] is None else picks[a](ins[a], co)
            cp = pltpu.make_async_remote_copy(
                src_ref=src, dst_ref=outs[a],
                send_sem=send_sems.at[a], recv_sem=recv_sems.at[a],
                device_id=_partner(axes[a]), device_id_type=MESH)
            cp.start()
            copies.append(cp)
        for cp in copies:
            cp.wait()

    any_spec = pl.BlockSpec(memory_space=pl.ANY)
    return pl.pallas_call(
        body, name=name,
        out_shape=[jax.ShapeDtypeStruct(s, a.dtype) for s, a in zip(out_shapes, arrs)],
        in_specs=[any_spec] * n,
        out_specs=[any_spec] * n,
        scratch_shapes=[pltpu.SemaphoreType.DMA((n,)), pltpu.SemaphoreType.DMA((n,))],
    )(*arrs)


_HBM = pl.BlockSpec(memory_space=pltpu.HBM)
_SEM = pl.BlockSpec(memory_space=pltpu.SEMAPHORE)


def _swap_copies(srcs, lands, send_sems, recv_sems, axes, picks):
    x, y, c = _coords()
    co = {"x": x, "y": y, "c": c}
    return [pltpu.make_async_remote_copy(
        src_ref=srcs[a] if picks[a] is None else picks[a](srcs[a], co), dst_ref=lands[a],
        send_sem=send_sems.at[a], recv_sem=recv_sems.at[a],
        device_id=_partner(axes[a]), device_id_type=MESH) for a in range(len(srcs))]


def swap_start(arrs, which, axes, picks, out_shapes, name):
    ns, n = len(arrs), len(which)

    def body(*refs):
        srcs, lands = refs[:ns], refs[ns:ns + n]
        send_sems, recv_sems = refs[ns + n:ns + n + 2]
        token = refs[-1]
        for cp in _swap_copies([srcs[i] for i in which], lands, send_sems, recv_sems, axes, picks):
            cp.start()
        token[...] = jnp.zeros_like(token)

    lands = [lax.empty(s, arrs[i].dtype) for s, i in zip(out_shapes, which)]
    ops = [pltpu.with_memory_space_constraint(a, pltpu.HBM) for a in list(arrs) + lands]
    out = pl.pallas_call(
        body, name=name,
        out_shape=[pltpu.SemaphoreType.DMA((n,)), pltpu.SemaphoreType.DMA((n,))]
        + [pltpu.HBM(o.shape, o.dtype) for o in ops] + [jax.ShapeDtypeStruct((8, 128), F32)],
        in_specs=[_HBM] * (ns + n),
        out_specs=[_SEM, _SEM] + [_HBM] * (ns + n) + [pl.BlockSpec(memory_space=pltpu.VMEM)],
        input_output_aliases={i: 2 + i for i in range(ns + n)},
        compiler_params=pltpu.CompilerParams(has_side_effects=pltpu.SideEffectType.DATAFLOW_SIDE_EFFECTING),
    )(*ops)
    return out[:-1], out[-1]


def swap_wait(state, after, which, axes, picks, name):
    n = len(which)
    ns = len(state) - 2 - n

    def body(*refs):
        srcs, lands = refs[:ns], refs[ns:ns + n]
        send_sems, recv_sems = refs[ns + n:ns + n + 2]
        for cp in _swap_copies([srcs[i] for i in which], lands, send_sems, recv_sems, axes, picks):
            cp.wait_send()
            cp.wait_recv()

    thru = list(state[2:])
    after = list(after) if isinstance(after, (list, tuple)) else [after]
    out = pl.pallas_call(
        body, name=name,
        out_shape=[pltpu.HBM(o.shape, o.dtype) for o in thru],
        in_specs=[_HBM] * (ns + n) + [_SEM, _SEM] + [pl.BlockSpec(memory_space=pl.ANY)] * len(after),
        out_specs=[_HBM] * (ns + n),
        input_output_aliases={i: i for i in range(ns + n)},
        compiler_params=pltpu.CompilerParams(has_side_effects=pltpu.SideEffectType.DATAFLOW_SIDE_EFFECTING),
    )(*thru, state[0], state[1], *after)
    return out[:ns], out[ns:]


def _gather_copies(shards, lands, send_sems, recv_sems):
    x, y, c = _coords()
    copies = []
    for a in range(len(shards)):
        for k in range(1, 8):
            peer = (1 - x if k & 4 else x, 1 - y if k & 2 else y, 1 - c if k & 1 else c)
            copies.append(pltpu.make_async_remote_copy(
                src_ref=shards[a], dst_ref=lands[a].at[x, y, c],
                send_sem=send_sems.at[7 * a + k - 1], recv_sem=recv_sems.at[7 * a + k - 1],
                device_id=peer, device_id_type=MESH))
    return copies


def gather_start(shards, name):
    n = len(shards)
    x, y, c = _coords()

    def body(*refs):
        srcs, lands = refs[:n], refs[n:2 * n]
        send_sems, recv_sems = refs[2 * n:2 * n + 2]
        token = refs[-1]
        for cp in _gather_copies(srcs, lands, send_sems, recv_sems):
            cp.start()
        token[...] = jnp.zeros_like(token)

    lands = [_own_block_placed(s) for s in shards]
    ops = [pltpu.with_memory_space_constraint(a, pltpu.HBM) for a in list(shards) + lands]
    out = pl.pallas_call(
        body, name=name,
        out_shape=[pltpu.SemaphoreType.DMA((7 * n,)), pltpu.SemaphoreType.DMA((7 * n,))]
        + [pltpu.HBM(o.shape, o.dtype) for o in ops] + [jax.ShapeDtypeStruct((8, 128), F32)],
        in_specs=[_HBM] * (2 * n),
        out_specs=[_SEM, _SEM] + [_HBM] * (2 * n) + [pl.BlockSpec(memory_space=pltpu.VMEM)],
        input_output_aliases={i: 2 + i for i in range(2 * n)},
        compiler_params=pltpu.CompilerParams(has_side_effects=pltpu.SideEffectType.DATAFLOW_SIDE_EFFECTING),
    )(*ops)
    return out[:-1], out[-1]


def gather_wait(state, after, name):
    n = (len(state) - 2) // 2

    def body(*refs):
        srcs, lands = refs[:n], refs[n:2 * n]
        send_sems, recv_sems = refs[2 * n:2 * n + 2]
        for cp in _gather_copies(srcs, lands, send_sems, recv_sems):
            cp.wait_send()
            cp.wait_recv()

    thru = list(state[2:])
    out = pl.pallas_call(
        body, name=name,
        out_shape=[pltpu.HBM(o.shape, o.dtype) for o in thru],
        in_specs=[_HBM] * (2 * n) + [_SEM, _SEM, pl.BlockSpec(memory_space=pl.ANY)],
        out_specs=[_HBM] * (2 * n),
        input_output_aliases={i: i for i in range(2 * n)},
        compiler_params=pltpu.CompilerParams(has_side_effects=pltpu.SideEffectType.DATAFLOW_SIDE_EFFECTING),
    )(*thru, state[0], state[1], after)
    return out[n:]


def rs_win_add_first(g, r, sel, next_dim, col, name):
    rows, cols = r.shape[2:]

    def body(sel_ref, gk_ref, rk_ref, gs_ref, rs_ref, keep_ref, send_ref):
        keep_ref[...] = gk_ref[...] + rk_ref[...]
        send_ref[...] = (gs_ref[...] + rs_ref[...]).astype(BF16)

    def g_map(flip):
        def f(j, s):
            nxt = 1 - s[next_dim] if flip else s[next_dim]
            return (nxt, j, s[2], 0, col) if next_dim == 0 else (j, nxt, s[2], 0, col)
        return f

    def r_map(flip):
        def f(j, s):
            nxt = 1 - s[next_dim] if flip else s[next_dim]
            return (nxt, j, 0, 0) if next_dim == 0 else (j, nxt, 0, 0)
        return f

    gblk = (None, None, None, rows, cols)
    rblk = (None, None, rows, cols)
    oblk = (None, rows, cols)
    return pl.pallas_call(
        body, name=name,
        grid_spec=pltpu.PrefetchScalarGridSpec(
            num_scalar_prefetch=1, grid=(2,),
            in_specs=[pl.BlockSpec(gblk, g_map(False)), pl.BlockSpec(rblk, r_map(False)),
                      pl.BlockSpec(gblk, g_map(True)), pl.BlockSpec(rblk, r_map(True))],
            out_specs=[pl.BlockSpec(oblk, lambda j, s: (j, 0, 0)),
                       pl.BlockSpec(oblk, lambda j, s: (j, 0, 0))]),
        out_shape=[jax.ShapeDtypeStruct((2, rows, cols), F32),
                   jax.ShapeDtypeStruct((2, rows, cols), BF16)],
        compiler_params=_params(),
    )(sel, g, r, g, r)


def rs_add_first(g, r, sel, name):
    _, _, _, rows, cols = g.shape
    tr = rows // 2

    def body(sel_ref, gk_ref, rk_ref, gs_ref, rs_ref, keep_ref, send_ref):
        keep_ref[...] = gk_ref[...] + rk_ref[...]
        send_ref[...] = (gs_ref[...] + rs_ref[...]).astype(BF16)

    blk = (None, None, None, tr, cols)
    rblk = (None, None, tr, cols)
    oblk = (None, tr, cols)
    return pl.pallas_call(
        body, name=name,
        grid_spec=pltpu.PrefetchScalarGridSpec(
            num_scalar_prefetch=1, grid=(2, 2),
            in_specs=[
                pl.BlockSpec(blk, lambda j, i, s: (s[0], s[1], j, i, 0)),
                pl.BlockSpec(rblk, lambda j, i, s: (s[1], j, i, 0)),
                pl.BlockSpec(blk, lambda j, i, s: (s[0], 1 - s[1], j, i, 0)),
                pl.BlockSpec(rblk, lambda j, i, s: (1 - s[1], j, i, 0)),
            ],
            out_specs=[pl.BlockSpec(oblk, lambda j, i, s: (j, i, 0)),
                       pl.BlockSpec(oblk, lambda j, i, s: (j, i, 0))]),
        out_shape=[jax.ShapeDtypeStruct((2, rows, cols), F32),
                   jax.ShapeDtypeStruct((2, rows, cols), BF16)],
        compiler_params=_params(),
    )(sel, g, r, g, r)


def rs_add_second(k, r, sel, name):
    _, rows, cols = k.shape
    tr = rows // 2 if rows % 32 == 0 else rows
    nt = rows // tr

    def body(sel_ref, kk_ref, rk_ref, ks_ref, rs_ref, keep_ref, send_ref):
        keep_ref[...] = kk_ref[...] + rk_ref[...].astype(F32)
        send_ref[...] = (ks_ref[...] + rs_ref[...].astype(F32)).astype(BF16)

    blk = (None, tr, cols)
    oblk = (tr, cols)
    return pl.pallas_call(
        body, name=name,
        grid_spec=pltpu.PrefetchScalarGridSpec(
            num_scalar_prefetch=1, grid=(nt,),
            in_specs=[
                pl.BlockSpec(blk, lambda i, s: (s[0], i, 0)),
                pl.BlockSpec(blk, lambda i, s: (s[0], i, 0)),
                pl.BlockSpec(blk, lambda i, s: (1 - s[0], i, 0)),
                pl.BlockSpec(blk, lambda i, s: (1 - s[0], i, 0)),
            ],
            out_specs=[pl.BlockSpec(oblk, lambda i, s: (i, 0)),
                       pl.BlockSpec(oblk, lambda i, s: (i, 0))]),
        out_shape=[jax.ShapeDtypeStruct((rows, cols), F32),
                   jax.ShapeDtypeStruct((rows, cols), BF16)],
        compiler_params=_params(),
    )(sel, k, r, k, r)


SEG_ROWS = (4800, 5824, 6848, 4096, 0, 1024, 2048, 3072)
LAT_ROWS = QL + KVL + ROPE
N_IN = 7872


def _seg_row(j):
    return pl.multiple_of(jnp.where(j < 3, 4800 + 1024 * j, jnp.where(j == 3, 4096, (j - 4) * 1024)), 8)


def proj_matmul(h, wt_bits, token):
    t = h.shape[0]
    tm = min(1024, t)

    def body(h_ref, w_hbm, tok_ref, o_ref, wt_ref, buf, sem):
        j = pl.program_id(0)

        @pl.when(pl.program_id(1) == 0)
        def _():
            cp = pltpu.make_async_copy(w_hbm.at[pl.ds(_seg_row(j), D)], buf, sem)
            cp.start()
            cp.wait()
            bits = pltpu.bitcast(buf[...], jnp.uint32)
            row = lax.broadcasted_iota(jnp.int32, (D, D // 2), 0)
            live = jnp.logical_or(j != SEG_LAT, row < LAT_ROWS)
            lo = pltpu.bitcast(bits << 16, F32)
            hi = pltpu.bitcast(bits & jnp.uint32(0xFFFF0000), F32)
            wt_ref[:, :D // 2] = jnp.where(live, lo, 0.0).astype(BF16)
            wt_ref[:, D // 2:] = jnp.where(live, hi, 0.0).astype(BF16)

        o_ref[...] = _dot_nt(h_ref[...], wt_ref[...]).astype(BF16)

    return pl.pallas_call(
        body, name="proj_matmul", grid=(NSEG, t // tm),
        in_specs=[pl.BlockSpec((tm, D), lambda j, i: (i, 0)),
                  pl.BlockSpec(memory_space=pl.ANY),
                  pl.BlockSpec((8, 128), lambda j, i: (0, 0))],
        out_specs=[pl.BlockSpec((None, tm, D), lambda j, i: (j, i, 0)),
                   pl.BlockSpec((D, D), lambda j, i: (j, 0))],
        out_shape=[jax.ShapeDtypeStruct((NSEG, t, D), BF16), jax.ShapeDtypeStruct((NP, D), BF16)],
        scratch_shapes=[pltpu.VMEM((D, D // 2), F32), pltpu.SemaphoreType.DMA],
        compiler_params=_params(("arbitrary", "arbitrary")),
    )(h, wt_bits, token)


def dh_matmul(dproj, wt, token, seq, b):
    tm = min(1024, seq)
    nblk = seq // tm

    def body(d_ref, w_ref, tok_ref, o_ref, acc_ref):
        k = pl.program_id(1)

        @pl.when(k == 0)
        def _():
            acc_ref[...] = jnp.zeros_like(acc_ref)

        acc_ref[...] += _dot(d_ref[...], w_ref[...])

        @pl.when(k == NSEG - 1)
        def _():
            o_ref[...] = acc_ref[...]

    return pl.pallas_call(
        body, name="dh_matmul_%d" % b, grid=(nblk, NSEG),
        in_specs=[pl.BlockSpec((None, tm, D), lambda i, k: (k, b * nblk + i, 0)),
                  pl.BlockSpec((D, D), lambda i, k: (k, 0)),
                  pl.BlockSpec((8, 128), lambda i, k: (0, 0))],
        out_specs=pl.BlockSpec((tm, D), lambda i, k: (i, 0)),
        out_shape=jax.ShapeDtypeStruct((seq, D), F32),
        scratch_shapes=[pltpu.VMEM((tm, D), F32)],
        compiler_params=_params(("parallel", "arbitrary")),
    )(dproj, wt, token)


def win_grad_matmul(h, dproj):
    t = h.shape[0]
    tk = min(1024, t)
    nk = t // tk

    def body(h_ref, d_ref, o_hbm, acc_ref, sem):
        j = pl.program_id(0)
        k = pl.program_id(1)

        @pl.when(k == 0)
        def _():
            acc_ref[...] = jnp.zeros_like(acc_ref)

        acc_ref[...] += _dot_tn(d_ref[...], h_ref[...])

        @pl.when(jnp.logical_and(k == nk - 1, j != SEG_LAT))
        def _():
            cp = pltpu.make_async_copy(acc_ref, o_hbm.at[pl.ds(_seg_row(j), D)], sem)
            cp.start()
            cp.wait()

        @pl.when(jnp.logical_and(k == nk - 1, j == SEG_LAT))
        def _():
            cp = pltpu.make_async_copy(acc_ref.at[pl.ds(0, LAT_ROWS)],
                                       o_hbm.at[pl.ds(SEG_ROWS[SEG_LAT], LAT_ROWS)], sem)
            cp.start()
            cp.wait()

    return pl.pallas_call(
        body, name="win_grad_matmul", grid=(NSEG, nk),
        in_specs=[pl.BlockSpec((tk, D), lambda j, k: (k, 0)),
                  pl.BlockSpec((None, tk, D), lambda j, k: (j, k, 0))],
        out_specs=pl.BlockSpec(memory_space=pl.ANY),
        out_shape=jax.ShapeDtypeStruct((N_IN, D), F32),
        scratch_shapes=[pltpu.VMEM((D, D), F32), pltpu.SemaphoreType.DMA],
        compiler_params=_params(("arbitrary", "arbitrary")),
    )(h, dproj)


def grad_matmul(a, b, name):
    t, m = a.shape
    n = b.shape[1]
    tk = min(1024, t)
    nk = t // tk

    def body(a_ref, b_ref, o_ref, acc_ref):
        k = pl.program_id(0)

        @pl.when(k == 0)
        def _():
            acc_ref[...] = jnp.zeros_like(acc_ref)

        acc_ref[...] += _dot_tn(a_ref[...], b_ref[...])

        @pl.when(k == nk - 1)
        def _():
            o_ref[...] = acc_ref[...]

    return pl.pallas_call(
        body, name=name, grid=(nk,),
        in_specs=[pl.BlockSpec((tk, m), lambda k: (k, 0)),
                  pl.BlockSpec((tk, n), lambda k: (k, 0))],
        out_specs=pl.BlockSpec((m, n), lambda k: (0, 0)),
        out_shape=jax.ShapeDtypeStruct((m, n), F32),
        scratch_shapes=[pltpu.VMEM((m, n), F32)],
        compiler_params=_params(("arbitrary",)),
    )(a, b)


def ada_fwd(c_all, w_ada, b_cols):
    def body(c_ref, w_ref, b_ref, o_ref):
        o_ref[...] = _dot(c_ref[...].astype(BF16), w_ref[...].astype(BF16)) + b_ref[...]

    return pl.pallas_call(
        body, name="ada_fwd",
        out_shape=jax.ShapeDtypeStruct((c_all.shape[0], w_ada.shape[1]), F32),
        compiler_params=_params(),
    )(c_all, w_ada, b_cols)


def ada_bwd(c_all, dmod_cols):
    def body(c_ref, d_ref, o_ref):
        o_ref[...] = _dot_tn(c_ref[...].astype(BF16), d_ref[...].astype(BF16))

    return pl.pallas_call(
        body, name="ada_bwd",
        out_shape=jax.ShapeDtypeStruct((c_all.shape[1], dmod_cols.shape[1]), F32),
        compiler_params=_params(),
    )(c_all, dmod_cols)


def slot_sum(g):
    def body(g_ref, o_ref):
        acc = g_ref[0]
        for s in range(1, 8):
            acc = acc + g_ref[s]
        o_ref[...] = acc

    return pl.pallas_call(
        body, name="slot_sum",
        out_shape=jax.ShapeDtypeStruct(g.shape[1:], F32),
    )(g)


def prenorm_fwd(x2, scale, shift, g_pre, seq):
    t = x2.shape[0]
    tm = min(512, seq)
    tpb = seq // tm

    def body(x_ref, sc_ref, sh_ref, g_ref, h_ref):
        xv = x_ref[...]
        r = lax.rsqrt(jnp.mean(xv * xv, axis=-1, keepdims=True) + EPS)
        hv = (xv * r * g_ref[...]) * (1.0 + sc_ref[...]) + sh_ref[...]
        h_ref[...] = hv.astype(BF16)

    per_batch = pl.BlockSpec((None, 1, D), lambda i: (i // tpb, 0, 0))
    return pl.pallas_call(
        body, name="prenorm_fwd", grid=(t // tm,),
        in_specs=[pl.BlockSpec((tm, D), lambda i: (i, 0)), per_batch, per_batch,
                  pl.BlockSpec((1, D), lambda i: (0, 0))],
        out_specs=pl.BlockSpec((tm, D), lambda i: (i, 0)),
        out_shape=jax.ShapeDtypeStruct((t, D), BF16),
        compiler_params=_params(("parallel",)),
    )(x2, scale, shift, g_pre)


def prenorm_bwd(dh, x2, dout, scale, g_pre, seq, token, b, gx_prev):
    t = x2.shape[0]
    tm = min(512, seq)
    tpb = seq // tm

    def body(dh_ref, x_ref, do_ref, sc_ref, g_ref, tok_ref, *rest):
        gx_ref, dsh_ref, dsc_ref, dg_ref = rest[-4:]
        i = pl.program_id(0)
        xv = x_ref[...]
        dhv = dh_ref[...]
        g = g_ref[...]
        r = lax.rsqrt(jnp.mean(xv * xv, axis=-1, keepdims=True) + EPS)
        nrm = xv * r
        dxn = dhv * (1.0 + sc_ref[...])
        dn = dxn * g
        dx = r * (dn - nrm * jnp.mean(dn * nrm, axis=-1, keepdims=True))
        gx_ref[...] = dx + do_ref[...]

        @pl.when(i == 0)
        def _():
            dsh_ref[...] = jnp.zeros_like(dsh_ref)
            dsc_ref[...] = jnp.zeros_like(dsc_ref)
            dg_ref[...] = jnp.zeros_like(dg_ref)

        dsh_ref[...] += jnp.sum(dhv, axis=0, keepdims=True)
        dsc_ref[...] += jnp.sum(dhv * (nrm * g), axis=0, keepdims=True)
        dg_ref[...] += jnp.sum(dxn * nrm, axis=0, keepdims=True)

    row = pl.BlockSpec((tm, D), lambda i: (i, 0))
    grow = pl.BlockSpec((tm, D), lambda i: (b * tpb + i, 0))
    per_batch = pl.BlockSpec((None, 1, D), lambda i: (b, 0, 0))
    vec = pl.BlockSpec((1, D), lambda i: (0, 0))
    ins = [dh, x2, dout, scale, g_pre, token]
    specs = [row, grow, grow, per_batch, vec, pl.BlockSpec((8, 128), lambda i: (0, 0))]
    alias = {}
    if gx_prev is not None:
        ins.append(gx_prev)
        specs.append(pl.BlockSpec(memory_space=pl.ANY))
        alias = {6: 0}
    return pl.pallas_call(
        body, name="prenorm_bwd_%d" % b, grid=(tpb,),
        in_specs=specs,
        out_specs=[grow, vec, vec, vec],
        out_shape=[jax.ShapeDtypeStruct((t, D), F32), jax.ShapeDtypeStruct((1, D), F32),
                   jax.ShapeDtypeStruct((1, D), F32), jax.ShapeDtypeStruct((1, D), F32)],
        input_output_aliases=alias,
        compiler_params=_params(("arbitrary",)),
    )(*ins)


CONV_TC = 128


def _shift_down(u, k, rows):
    idx = lax.broadcasted_iota(jnp.int32, u.shape, 0)
    return jnp.where(idx >= k, pltpu.roll(u, k, 0), 0.0)


def _shift_up(u, k, rows):
    idx = lax.broadcasted_iota(jnp.int32, u.shape, 0)
    return jnp.where(idx < rows - k, pltpu.roll(u, rows - k, 0), 0.0)


def conv_fwd(proj, conv_w, seq):
    t = proj.shape[1]
    nb = t // seq

    def body(p_ref, w_ref, y_ref):
        av = p_ref[0].astype(F32)
        ab = p_ref[1].astype(F32)
        ac = p_ref[2].astype(F32)
        az = p_ref[3].astype(F32)
        w = w_ref[...]
        u = ac * av
        y1 = _shift_down(u, 2, seq) * w[0:1] + _shift_down(u, 1, seq) * w[1:2] + u * w[2:3]
        y_ref[...] = (ab * y1 * (az * _sig(az))).astype(BF16)

    return pl.pallas_call(
        body, name="conv_fwd", grid=(nb, D // CONV_TC),
        in_specs=[pl.BlockSpec((4, seq, CONV_TC), lambda b, ci: (1, b, ci)),
                  pl.BlockSpec((8, CONV_TC), lambda b, ci: (0, ci))],
        out_specs=pl.BlockSpec((seq, CONV_TC), lambda b, ci: (b, ci)),
        out_shape=jax.ShapeDtypeStruct((t, D), BF16),
        compiler_params=_params(("parallel", "parallel")),
    )(proj, conv_w)


def conv_bwd(dproj, proj, dy, conv_w, seq):
    t = proj.shape[1]
    nb = t // seq

    def body(dp_in_ref, p_ref, dy_ref, w_ref, dp_ref, dw_ref):
        b = pl.program_id(1)
        av = p_ref[0].astype(F32)
        ab = p_ref[1].astype(F32)
        ac = p_ref[2].astype(F32)
        az = p_ref[3].astype(F32)
        dyv = dy_ref[...].astype(F32)
        w = w_ref[...]
        u = ac * av
        u1 = _shift_down(u, 1, seq)
        u2 = _shift_down(u, 2, seq)
        y1 = u2 * w[0:1] + u1 * w[1:2] + u * w[2:3]
        sz = _sig(az)
        silu = az * sz
        dy1 = dyv * ab * silu
        du = dy1 * w[2:3] + _shift_up(dy1, 1, seq) * w[1:2] + _shift_up(dy1, 2, seq) * w[0:1]
        dp_ref[0] = (du * ac).astype(BF16)
        dp_ref[1] = (dyv * y1 * silu).astype(BF16)
        dp_ref[2] = (du * av).astype(BF16)
        dp_ref[3] = (dyv * ab * y1 * (sz * (1.0 + az * (1.0 - sz)))).astype(BF16)

        @pl.when(b == 0)
        def _():
            dw_ref[...] = jnp.zeros_like(dw_ref)

        dw_ref[0:1, :] += jnp.sum(dy1 * u2, axis=0, keepdims=True)
        dw_ref[1:2, :] += jnp.sum(dy1 * u1, axis=0, keepdims=True)
        dw_ref[2:3, :] += jnp.sum(dy1 * u, axis=0, keepdims=True)

    return pl.pallas_call(
        body, name="conv_bwd", grid=(D // CONV_TC, nb),
        in_specs=[pl.BlockSpec(memory_space=pl.ANY),
                  pl.BlockSpec((4, seq, CONV_TC), lambda ci, b: (1, b, ci)),
                  pl.BlockSpec((seq, CONV_TC), lambda ci, b: (b, ci)),
                  pl.BlockSpec((8, CONV_TC), lambda ci, b: (0, ci))],
        out_specs=[pl.BlockSpec((4, seq, CONV_TC), lambda ci, b: (1, b, ci)),
                   pl.BlockSpec((8, CONV_TC), lambda ci, b: (0, ci))],
        out_shape=[jax.ShapeDtypeStruct(dproj.shape, BF16),
                   jax.ShapeDtypeStruct((8, D), F32)],
        input_output_aliases={0: 0},
        compiler_params=_params(("parallel", "arbitrary")),
    )(dproj, proj, dy, conv_w)


def _rope_tables(pos_ref, invf_ref, ma_ref, mb_ref):
    ang = pos_ref[...].astype(F32) * invf_ref[...]
    cs = jnp.cos(ang)
    sn = jnp.sin(ang)
    return cs, sn * ma_ref[...], sn * mb_ref[...]


def _head_tables(cs, sa, sb):
    one = jnp.ones_like(cs)
    zero = jnp.zeros_like(cs)
    return (jnp.tile(jnp.concatenate([one, cs], axis=1), (1, H)),
            jnp.tile(jnp.concatenate([zero, sa], axis=1), (1, H)),
            jnp.tile(jnp.concatenate([zero, sb], axis=1), (1, H)))


def _rotate(v, cs, sa, sb, sign):
    width = v.shape[1]
    return v * cs + sign * (pltpu.roll(v, width - HALF, 1) * sa + pltpu.roll(v, HALF, 1) * sb)


MLA_TM = 256


def mla_prep_fwd(proj, pos, g_q, g_kv, wuq, wukv, tabs):
    t = proj.shape[1]
    tm = min(MLA_TM, t)

    def body(lat_ref, pos_ref, gq_ref, gkv_ref, wuq_ref, wukv_ref, invf_ref, ma_ref, mb_ref,
             q_ref, k_ref, kv_ref, qn_ref, kvn_ref):
        lat = lat_ref[...].astype(F32)
        ql = lat[:, :QL]
        kl = lat[:, QL:QL + KVL]
        kr = lat[:, QL + KVL:QL + KVL + 128]
        qn = (ql * lax.rsqrt(jnp.mean(ql * ql, axis=-1, keepdims=True) + EPS) * gq_ref[...]).astype(BF16)
        kvn = (kl * lax.rsqrt(jnp.mean(kl * kl, axis=-1, keepdims=True) + EPS) * gkv_ref[...]).astype(BF16)
        qn_ref[...] = qn
        kvn_ref[...] = kvn
        cs, sa, sb = _rope_tables(pos_ref, invf_ref, ma_ref, mb_ref)
        hc, ha, hb = _head_tables(cs, sa, sb)
        q = _dot(qn, wuq_ref[...])
        q_ref[...] = (_rotate(q, hc, ha, hb, 1.0) * (SM_SCALE * LOG2E)).astype(BF16)
        kv = _dot(kvn, wukv_ref[...]).astype(BF16)
        kv_ref[...] = kv
        kpe = _rotate(kr, cs, sa, sb, 1.0).astype(BF16)
        for hh in range(H):
            k_ref[:, hh * DQK:hh * DQK + 128] = kv[:, hh * DQK:hh * DQK + 128]
            k_ref[:, hh * DQK + 128:(hh + 1) * DQK] = kpe

    row = lambda w: pl.BlockSpec((tm, w), lambda i: (i, 0))
    const = lambda a: pl.BlockSpec(a.shape, lambda i: (0,) * a.ndim)
    return pl.pallas_call(
        body, name="mla_prep_fwd", grid=(t // tm,),
        in_specs=[pl.BlockSpec((None, tm, D), lambda i: (SEG_LAT, i, 0)), row(1),
                  const(g_q), const(g_kv), const(wuq), const(wukv)] + [const(a) for a in tabs],
        out_specs=[row(H * DQK), row(H * DQK), row(H * DQK), row(QL), row(KVL)],
        out_shape=[jax.ShapeDtypeStruct((t, H * DQK), BF16)] * 3
        + [jax.ShapeDtypeStruct((t, QL), BF16), jax.ShapeDtypeStruct((t, KVL), BF16)],
        compiler_params=_params(("parallel",)),
    )(proj, pos, g_q, g_kv, wuq, wukv, *tabs)


def mla_prep_bwd(dproj, proj, dq_rot, dk, dv, pos, g_q, g_kv, wuq, wukv, tabs):
    t = proj.shape[1]
    tm = min(MLA_TM, t)

    def body(dp_in_ref, lat_ref, dqr_ref, dk_ref, dv_ref, pos_ref, gq_ref, gkv_ref, wuq_ref, wukv_ref,
             invf_ref, ma_ref, mb_ref, dp_ref, dq_ref, dkv_ref, dgq_ref, dgkv_ref):
        i = pl.program_id(0)
        lat = lat_ref[...].astype(F32)
        ql = lat[:, :QL]
        kl = lat[:, QL:QL + KVL]
        rq = lax.rsqrt(jnp.mean(ql * ql, axis=-1, keepdims=True) + EPS)
        rk = lax.rsqrt(jnp.mean(kl * kl, axis=-1, keepdims=True) + EPS)
        nq = ql * rq
        nk = kl * rk
        cs, sa, sb = _rope_tables(pos_ref, invf_ref, ma_ref, mb_ref)
        hc, ha, hb = _head_tables(cs, sa, sb)
        dq = _rotate(dqr_ref[...] * SM_SCALE, hc, ha, hb, -1.0).astype(BF16)
        dq_ref[...] = dq
        dkpe = jnp.zeros((tm, 128), F32)
        for hh in range(H):
            dkv_ref[:, hh * DQK:hh * DQK + 128] = dk_ref[:, hh * DQK:hh * DQK + 128]
            dkv_ref[:, hh * DQK + 128:(hh + 1) * DQK] = dv_ref[:, hh * DV:(hh + 1) * DV]
            dkpe = dkpe + dk_ref[:, hh * DQK + 128:(hh + 1) * DQK].astype(F32)
        lane = lax.broadcasted_iota(jnp.int32, (tm, 128), 1)
        dkr = jnp.where(lane < ROPE, _rotate(dkpe, cs, sa, sb, -1.0), 0.0)
        dqn = _dot_nt(dq, wuq_ref[...])
        dkvn = _dot_nt(dkv_ref[...], wukv_ref[...])
        gq = gq_ref[...]
        gkv = gkv_ref[...]
        dnq = dqn * gq
        dnk = dkvn * gkv
        dql = rq * (dnq - nq * jnp.mean(dnq * nq, axis=-1, keepdims=True))
        dkl = rk * (dnk - nk * jnp.mean(dnk * nk, axis=-1, keepdims=True))
        dp_ref[:, :QL] = dql.astype(BF16)
        dp_ref[:, QL:QL + KVL] = dkl.astype(BF16)
        dp_ref[:, QL + KVL:QL + KVL + 128] = dkr.astype(BF16)
        dp_ref[:, QL + KVL + 128:] = jnp.zeros((tm, D - QL - KVL - 128), BF16)

        @pl.when(i == 0)
        def _():
            dgq_ref[...] = jnp.zeros_like(dgq_ref)
            dgkv_ref[...] = jnp.zeros_like(dgkv_ref)

        dgq_ref[...] += jnp.sum(dqn * nq, axis=0, keepdims=True)
        dgkv_ref[...] += jnp.sum(dkvn * nk, axis=0, keepdims=True)

    row = lambda w: pl.BlockSpec((tm, w), lambda i: (i, 0))
    const = lambda a: pl.BlockSpec(a.shape, lambda i: (0,) * a.ndim)
    seg = pl.BlockSpec((None, tm, D), lambda i: (SEG_LAT, i, 0))
    return pl.pallas_call(
        body, name="mla_prep_bwd", grid=(t // tm,),
        in_specs=[pl.BlockSpec(memory_space=pl.ANY), seg, row(H * DQK), row(H * DQK), row(H * DV), row(1),
                  const(g_q), const(g_kv), const(wuq), const(wukv)] + [const(a) for a in tabs],
        out_specs=[seg, row(H * DQK), row(H * DQK),
                   pl.BlockSpec((1, QL), lambda i: (0, 0)), pl.BlockSpec((1, KVL), lambda i: (0, 0))],
        out_shape=[jax.ShapeDtypeStruct(dproj.shape, BF16),
                   jax.ShapeDtypeStruct((t, H * DQK), BF16), jax.ShapeDtypeStruct((t, H * DQK), BF16),
                   jax.ShapeDtypeStruct((1, QL), F32), jax.ShapeDtypeStruct((1, KVL), F32)],
        input_output_aliases={0: 0},
        compiler_params=_params(("arbitrary",)),
    )(dproj, proj, dq_rot, dk, dv, pos, g_q, g_kv, wuq, wukv, *tabs)


def _causal_mask(s, n):
    row = lax.broadcasted_iota(jnp.int32, (n, n), 0)
    col = lax.broadcasted_iota(jnp.int32, (n, n), 1)
    return jnp.where(col <= row, s, -1e30)


def flash_fwd(q, k, kv, nb, seq):
    t = q.shape[0]
    tq = min(FLASH_TQ, seq)
    nq = seq // tq

    def body(q_ref, k_ref, v_ref, o_ref, lse_ref):
        for qi in range(nq):
            qs = slice(qi * tq, (qi + 1) * tq)
            qv = q_ref[qs, :]
            m = jnp.full((tq, 1), -1e30, F32)
            l = jnp.zeros((tq, 1), F32)
            acc = jnp.zeros((tq, DV), F32)
            for j in range(qi + 1):
                ks = slice(j * tq, (j + 1) * tq)
                s = _dot_nt(qv, k_ref[ks, :])
                if j == qi:
                    s = _causal_mask(s, tq)
                m_new = jnp.maximum(m, jnp.max(s, axis=1, keepdims=True))
                p = jnp.exp2(s - m_new)
                alpha = jnp.exp2(m - m_new)
                l = alpha * l + jnp.sum(p, axis=1, keepdims=True)
                acc = alpha * acc + _dot(p.astype(BF16), v_ref[ks, :])
                m = m_new
            o_ref[qs, :] = (acc / l).astype(BF16)
            lse_ref[qs, :] = jnp.broadcast_to(m + jnp.log(l) * LOG2E, (tq, DV))

    out_blk = pl.BlockSpec((seq, DV), lambda b, h: (b, h))
    return pl.pallas_call(
        body, name="flash_fwd", grid=(nb, H),
        in_specs=[pl.BlockSpec((seq, DQK), lambda b, h: (b, h)),
                  pl.BlockSpec((seq, DQK), lambda b, h: (b, h)),
                  pl.BlockSpec((seq, DV), lambda b, h: (b, 2 * h + 1))],
        out_specs=[out_blk, out_blk],
        out_shape=[jax.ShapeDtypeStruct((t, H * DV), BF16), jax.ShapeDtypeStruct((t, H * DV), F32)],
        compiler_params=_params(("parallel", "parallel")),
    )(q, k, kv)


def flash_bwd(q, k, kv, o, do, lse, nb, seq):
    t = q.shape[0]
    tq = min(FLASH_TQ, seq)
    nq = seq // tq

    def body(q_ref, k_ref, v_ref, o_ref, do_ref, lse_ref, dq_ref, dk_ref, dv_ref):
        delta = []
        for qi in range(nq):
            qs = slice(qi * tq, (qi + 1) * tq)
            delta.append(jnp.sum(do_ref[qs, :].astype(F32) * o_ref[qs, :].astype(F32), axis=1, keepdims=True))
        for ki in range(nq):
            ks = slice(ki * tq, (ki + 1) * tq)
            kb = k_ref[ks, :]
            vb = v_ref[ks, :]
            dk = jnp.zeros((tq, DQK), F32)
            dv = jnp.zeros((tq, DV), F32)
            for qi in range(ki, nq):
                qs = slice(qi * tq, (qi + 1) * tq)
                qv = q_ref[qs, :]
                dov = do_ref[qs, :]
                s = _dot_nt(qv, kb)
                if qi == ki:
                    s = _causal_mask(s, tq)
                p = jnp.exp2(s - lse_ref[qs, :][:, :1])
                dp = _dot_nt(dov, vb)
                dz = (p * (dp - delta[qi])).astype(BF16)
                dv = dv + _dot_tn(p.astype(BF16), dov)
                dk = dk + _dot_tn(dz, qv)
                dqb = _dot(dz, kb)
                if ki == 0:
                    dq_ref[qs, :] = dqb
                else:
                    dq_ref[qs, :] += dqb
            dk_ref[ks, :] = (dk * LN2).astype(BF16)
            dv_ref[ks, :] = dv.astype(BF16)

    full = lambda w, col: pl.BlockSpec((seq, w), col)
    same = lambda b, h: (b, h)
    return pl.pallas_call(
        body, name="flash_bwd", grid=(nb, H),
        in_specs=[full(DQK, same), full(DQK, same), full(DV, lambda b, h: (b, 2 * h + 1)),
                  full(DV, same), full(DV, same), full(DV, same)],
        out_specs=[full(DQK, same), full(DQK, same), full(DV, same)],
        out_shape=[jax.ShapeDtypeStruct((t, H * DQK), F32), jax.ShapeDtypeStruct((t, H * DQK), BF16),
                   jax.ShapeDtypeStruct((t, H * DV), BF16)],
        compiler_params=_params(("parallel", "parallel")),
    )(q, k, kv, o, do, lse)


TAIL_TM = 256


def tail_fwd(y, attn, proj, x2, tgt, gate, g_post, wco, wmo, wout, seq):
    t = y.shape[0]
    nb = t // seq
    tm = min(TAIL_TM, seq)
    tpb = seq // tm

    def body(y_ref, at_ref, p_ref, x_ref, t_ref, gate_ref, gp_ref, wco_ref, wmo_ref, wout_ref,
             o_ref, ya_ref, yb_ref, m_ref, do2_ref, dout_ref, dgate_ref, dgp_ref, loss_ref):
        i = pl.program_id(0)
        bz = p_ref[0].astype(F32)
        ga = p_ref[1].astype(F32)
        gb = p_ref[2].astype(F32)
        ov = (at_ref[...].astype(F32) * (bz * _sig(bz))).astype(BF16)
        o_ref[...] = ov
        ya = _dot(y_ref[...], wco_ref[...])
        yb = _dot(ov, wmo_ref[...])
        ya_ref[...] = ya.astype(BF16)
        yb_ref[...] = yb.astype(BF16)
        mv = (_sig(ga) * ya + _sig(gb) * yb).astype(BF16)
        m_ref[...] = mv
        o2 = _dot(mv, wout_ref[...])
        r = lax.rsqrt(jnp.mean(o2 * o2, axis=-1, keepdims=True) + EPS)
        nrm = o2 * r
        gp = gp_ref[...]
        gate_v = gate_ref[...]
        rn = nrm * gp
        err = x_ref[...] + gate_v * rn - t_ref[...]
        dout = err * (1.0 / D)
        dout_ref[...] = dout
        dn = dout * gate_v * gp
        do2_ref[...] = (r * (dn - nrm * jnp.mean(dn * nrm, axis=-1, keepdims=True))).astype(BF16)

        @pl.when(i % tpb == 0)
        def _():
            dgate_ref[...] = jnp.zeros_like(dgate_ref)

        @pl.when(i == 0)
        def _():
            dgp_ref[...] = jnp.zeros_like(dgp_ref)
            loss_ref[...] = jnp.zeros_like(loss_ref)

        dgate_ref[...] += jnp.sum(dout * rn, axis=0, keepdims=True)
        dgp_ref[...] += jnp.sum(dout * gate_v * nrm, axis=0, keepdims=True)
        loss_ref[...] += 0.5 * jnp.sum(jnp.mean(err * err, axis=-1, keepdims=True), axis=0, keepdims=True)

    row = pl.BlockSpec((tm, D), lambda i: (i, 0))
    per_batch = pl.BlockSpec((None, 1, D), lambda i: (i // tpb, 0, 0))
    vec = pl.BlockSpec((1, D), lambda i: (0, 0))
    wgt = pl.BlockSpec((D, D), lambda i: (0, 0))
    act = jax.ShapeDtypeStruct((t, D), BF16)
    return pl.pallas_call(
        body, name="tail_fwd", grid=(t // tm,),
        in_specs=[row, row, pl.BlockSpec((3, tm, D), lambda i: (0, i, 0)), row, row, per_batch, vec,
                  wgt, wgt, wgt],
        out_specs=[row, row, row, row, row, row, per_batch, vec, pl.BlockSpec((1, 1), lambda i: (0, 0))],
        out_shape=[act, act, act, act, act, jax.ShapeDtypeStruct((t, D), F32),
                   jax.ShapeDtypeStruct((nb, 1, D), F32), jax.ShapeDtypeStruct((1, D), F32),
                   jax.ShapeDtypeStruct((1, 1), F32)],
        compiler_params=_params(("arbitrary",)),
    )(y, attn, proj, x2, tgt, gate, g_post, wco, wmo, wout)


def tail_bwd(do2, proj, ya, yb, attn, wout, wmo, wco):
    t = do2.shape[0]
    tm = min(TAIL_TM, t)

    def body(do2_ref, p_ref, ya_ref, yb_ref, at_ref, wout_ref, wmo_ref, wco_ref,
             dp_ref, dya_ref, dyb_ref, dat_ref, dy_ref):
        bz = p_ref[0].astype(F32)
        ga = p_ref[1].astype(F32)
        gb = p_ref[2].astype(F32)
        dm = _dot_nt(do2_ref[...], wout_ref[...])
        sa = _sig(ga)
        sb = _sig(gb)
        dya = (dm * sa).astype(BF16)
        dyb = (dm * sb).astype(BF16)
        dya_ref[...] = dya
        dyb_ref[...] = dyb
        dp_ref[1] = (dm * ya_ref[...].astype(F32) * (sa * (1.0 - sa))).astype(BF16)
        dp_ref[2] = (dm * yb_ref[...].astype(F32) * (sb * (1.0 - sb))).astype(BF16)
        dov = _dot_nt(dyb, wmo_ref[...])
        sz = _sig(bz)
        dat_ref[...] = (dov * (bz * sz)).astype(BF16)
        dp_ref[0] = (dov * at_ref[...].astype(F32) * (sz * (1.0 + bz * (1.0 - sz)))).astype(BF16)
        dy_ref[...] = _dot_nt(dya, wco_ref[...]).astype(BF16)

    row = pl.BlockSpec((tm, D), lambda i: (i, 0))
    seg3 = pl.BlockSpec((3, tm, D), lambda i: (0, i, 0))
    wgt = pl.BlockSpec((D, D), lambda i: (0, 0))
    act = jax.ShapeDtypeStruct((t, D), BF16)
    return pl.pallas_call(
        body, name="tail_bwd", grid=(t // tm,),
        in_specs=[row, seg3, row, row, row, wgt, wgt, wgt],
        out_specs=[seg3, row, row, row, row],
        out_shape=[jax.ShapeDtypeStruct((NSEG, t, D), BF16), act, act, act, act],
        compiler_params=_params(("parallel",)),
    )(do2, proj, ya, yb, attn, wout, wmo, wco)


def adamw(w, m, v, g, g2, name, token=None):
    rows, cols = w.shape
    tr = rows
    for cand in (256, 128, 64, 32, 16, 8):
        if rows % cand == 0 and rows > cand:
            tr = cand
            break
    has2 = g2 is not None
    n_in = 4 + has2

    def body(*refs):
        w_ref, m_ref, v_ref, g_ref = refs[:4]
        go_ref, d_ref, mo_ref, vo_ref = refs[-4:]
        grad = g_ref[...] + refs[4][...].astype(F32) if has2 else g_ref[...]
        mn = ADAM_B1 * m_ref[...] + (1.0 - ADAM_B1) * grad
        vn = ADAM_B2 * v_ref[...] + (1.0 - ADAM_B2) * (grad * grad)
        m_hat = mn / (1.0 - ADAM_B1 ** ADAM_STEP)
        v_hat = vn / (1.0 - ADAM_B2 ** ADAM_STEP)
        go_ref[...] = grad
        d_ref[...] = -ADAM_LR * (m_hat / (jnp.sqrt(v_hat) + ADAM_EPS) + ADAM_WD * w_ref[...])
        mo_ref[...] = mn
        vo_ref[...] = vn

    blk = pl.BlockSpec((tr, cols), lambda i: (i, 0))
    ins = [w, m, v, g] + ([g2] if has2 else [])
    specs = [blk] * n_in
    if token is not None:
        ins.append(token)
        specs.append(pl.BlockSpec((8, 128), lambda i: (0, 0)))
    return pl.pallas_call(
        body, name=name, grid=(rows // tr,),
        in_specs=specs, out_specs=[blk] * 4,
        out_shape=[jax.ShapeDtypeStruct((rows, cols), F32)] * 4,
        compiler_params=_params(("parallel",)),
    )(*ins)


def adamw_win(wt, mt, vt, ka, ra, kb, rb):
    rows = wt.shape[0]
    tc = 256
    nh = (D // 2) // tc

    def body(w_ref, m_ref, v_ref, ka_ref, ra_ref, kb_ref, rb_ref, go_ref, d_ref, mo_ref, vo_ref):
        first = pl.program_id(0) < nh
        grad = jnp.where(first, ka_ref[...] + ra_ref[...].astype(F32), kb_ref[...] + rb_ref[...].astype(F32))
        mn = ADAM_B1 * m_ref[...] + (1.0 - ADAM_B1) * grad
        vn = ADAM_B2 * v_ref[...] + (1.0 - ADAM_B2) * (grad * grad)
        m_hat = mn / (1.0 - ADAM_B1 ** ADAM_STEP)
        v_hat = vn / (1.0 - ADAM_B2 ** ADAM_STEP)
        go_ref[...] = grad
        d_ref[...] = -ADAM_LR * (m_hat / (jnp.sqrt(v_hat) + ADAM_EPS) + ADAM_WD * w_ref[...])
        mo_ref[...] = mn
        vo_ref[...] = vn

    blk = pl.BlockSpec((rows, tc), lambda j: (0, j))
    lo = pl.BlockSpec((rows, tc), lambda j: (0, jnp.minimum(j, nh - 1)))
    hi = pl.BlockSpec((rows, tc), lambda j: (0, jnp.maximum(j - nh, 0)))
    return pl.pallas_call(
        body, name="adamw_w_in", grid=(D // tc,),
        in_specs=[blk, blk, blk, lo, lo, hi, hi], out_specs=[blk] * 4,
        out_shape=[jax.ShapeDtypeStruct((rows, D), F32)] * 4,
        compiler_params=_params(("parallel",)),
    )(wt, mt, vt, ka, ra, kb, rb)


_ORD_A = ("x", "y", "c")
_ORD_B = ("y", "x", "c")


def _to_slots(full, order, col_sharded):
    if col_sharded:
        r = full.shape[0]
        cc = full.shape[1] // 8
        g = full.reshape(r, 2, 2, 2, cc).transpose(1, 2, 3, 0, 4)
    else:
        r = full.shape[0] // 8
        cc = full.shape[1]
        g = full.reshape(2, 2, 2, r, cc)
    names = ("x", "y", "c")
    perm = tuple(names.index(a) for a in order)
    return g.transpose(perm + (3, 4))


def _rows128(a, rows):
    flat = a.reshape(-1)
    return jnp.pad(flat, (0, rows * 128 - flat.shape[0])).reshape(rows, 128)


def kernel(x, c, positions, w_ada, b_ada, g_pre, w_in, conv_w, w_conv_out, g_q, w_uq, g_kv, w_ukv, w_mla_out, w_out, g_post, loss_target, m_w_ada, m_b_ada, m_g_pre, m_w_in, m_conv_w, m_w_conv_out, m_g_q, m_w_uq, m_g_kv, m_w_ukv, m_w_mla_out, m_w_out, m_g_post, v_w_ada, v_b_ada, v_g_pre, v_w_in, v_conv_w, v_w_conv_out, v_g_q, v_w_uq, v_g_kv, v_w_ukv, v_w_mla_out, v_w_out, v_g_post):
    nb, seq, _ = x.shape
    t = nb * seq
    mx, my, mc = lax.axis_index("x"), lax.axis_index("y"), lax.axis_index("c")
    me = 4 * mx + 2 * my + mc
    co = {"x": mx, "y": my, "c": mc}

    x2 = x.reshape(t, D)
    tgt2 = loss_target.reshape(t, D)
    pos2 = positions.reshape(t, 1)

    packed = jnp.concatenate([c.reshape(2 * D // 128, 128), _rows128(conv_w[0], 8)], axis=0)
    gath = small_allgather(packed, "gather_cond")
    c_all = gath[:, :16].reshape(8 * nb, D)
    conv_full = gath[:, 16:19].reshape(8, 3, 128).transpose(1, 0, 2).reshape(3, D)
    conv_full8 = jnp.pad(conv_full, ((0, 5), (0, 0)))
    ada_cols = w_ada.shape[2]
    b_cols = lax.dynamic_slice(b_ada, (0, me * ada_cols), (1, ada_cols))
    mod_part = ada_fwd(c_all, w_ada[0], b_cols)
    mod_g = small_allgather(mod_part.reshape(8 * nb * ada_cols // 128, 128), "gather_mod")
    mod_all = mod_g.reshape(8, 8 * nb, ada_cols).transpose(1, 0, 2).reshape(8 * nb, 8 * ada_cols)
    mod = lax.dynamic_slice(mod_all, (me * nb, 0), (nb, 3 * D))
    shift = mod[:, 0:D].reshape(nb, 1, D)
    scale = mod[:, D:2 * D].reshape(nb, 1, D)
    gate = mod[:, 2 * D:3 * D].reshape(nb, 1, D)

    wt = w_in[0].T.astype(BF16)
    lo = lax.bitcast_convert_type(wt[:, :D // 2], jnp.uint16).astype(jnp.uint32)
    hi = lax.bitcast_convert_type(wt[:, D // 2:], jnp.uint16).astype(jnp.uint32)
    wt_bits = lax.bitcast_convert_type(lo | (hi << 16), F32)
    q4 = D // 4
    r3rd = wt_bits.shape[0] // 3
    plan = [(0, (k * r3rd, r3rd), (g * q4, q4), (_ORD_A, _ORD_B)[g]) for k in range(3) for g in range(2)]
    gw = allgather_big([wt_bits], plan, "gather_w_in")
    late = [w_conv_out[0].astype(BF16), w_mla_out[0].astype(BF16), w_out[0].astype(BF16),
            w_uq[0].astype(BF16), w_ukv[0].astype(BF16)]
    gw0, late = lax.optimization_barrier((gw[0], late))
    late_state, late_token = gather_start(late, "gather_late_start")
    wt_bits_all = gw0.reshape(N_IN, D // 2)

    inv_freq = ROPE_THETA ** (-jnp.arange(0, ROPE, 2, dtype=F32) / ROPE)
    invf = jnp.concatenate([inv_freq, inv_freq, jnp.zeros((128 - ROPE,), F32)]).reshape(1, 128)
    lane = np.arange(128)
    tabs = (invf,
            jnp.asarray(np.where(lane < HALF, -1.0, 0.0).reshape(1, 128), F32),
            jnp.asarray(np.where((lane >= HALF) & (lane < ROPE), 1.0, 0.0).reshape(1, 128), F32))

    h = prenorm_fwd(x2, scale, shift, g_pre, seq)
    proj, wt_p = proj_matmul(h, wt_bits_all, late_token)
    y = conv_fwd(proj, conv_full8, seq)
    gl = gather_wait(late_state, y, "gather_late_wait")
    wco = gl[0].reshape(D, D)
    wmo = gl[1].reshape(D, D)
    wout = gl[2].reshape(D, D)
    wuq_full = gl[3].reshape(8, QL, 192).transpose(1, 0, 2)
    wuq_p = jnp.pad(wuq_full, ((0, 0), (0, 0), (0, DQK - 192))).reshape(QL, H * DQK)
    wukv = gl[4].reshape(8, KVL, 256).transpose(1, 0, 2).reshape(KVL, H * 256)
    q_rot, k_cat, kv, qn, kvn = mla_prep_fwd(proj, pos2, g_q, g_kv, wuq_p, wukv, tabs)
    attn, lse = flash_fwd(q_rot, k_cat, kv, nb, seq)
    o, ya, yb, m, do2, dout, dgate, dg_post, loss_part = tail_fwd(
        y, attn, proj, x2, tgt2, gate, g_post, wco, wmo, wout, seq)

    dproj, dya, dyb, dattn, dy = tail_bwd(do2, proj, ya, yb, attn, wout, wmo, wco)
    g_wout = grad_matmul(m, do2, "grad_w_out")
    g_wmo = grad_matmul(o, dyb, "grad_w_mla_out")
    g_wco = grad_matmul(y, dya, "grad_w_conv_out")
    dproj, dconv = conv_bwd(dproj, proj, dy, conv_full8, seq)
    dq_rot, dk, dv = flash_bwd(q_rot, k_cat, kv, attn, dattn, lse, nb, seq)
    dproj, dq, dkv, dg_q, dg_kv = mla_prep_bwd(dproj, proj, dq_rot, dk, dv, pos2, g_q, g_kv, wuq_p, wukv, tabs)
    g_wuq_p = grad_matmul(qn, dq, "grad_w_uq")
    g_wukv = grad_matmul(kvn, dkv, "grad_w_ukv")
    g_win_p = win_grad_matmul(h, dproj)

    g_wt = g_win_p.reshape(2, 2, 2, N_IN // 8, D)
    g_wuq = g_wuq_p.reshape(QL, H, DQK)[:, :, :192].reshape(QL, H * 192)
    rs_a = ("c", "y", "x")
    rs_b = ("c", "x", "y")
    flat = lambda s: s.reshape(2, 2, 2, -1, 128)
    rest_a = jnp.concatenate([flat(_to_slots(g_wco, rs_a, False)), flat(_to_slots(g_wmo, rs_a, False))], axis=3)
    rest_b = jnp.concatenate([flat(_to_slots(g_wout, rs_b, False)), flat(_to_slots(g_wuq, rs_b, True)),
                              flat(_to_slots(g_wukv, rs_b, True))], axis=3)
    ords = [rs_a, rs_a, rs_b, rs_b]
    hc = D // 2
    win_shape = (2, 2, N_IN // 8, hc)
    pick_w = lambda col: (lambda ref, cc: ref.at[:, :, 1 - cc["c"], :, pl.ds(col * hc, hc)])
    pick_h = lambda ref, cc: ref.at[1 - cc["c"]]
    which1 = [0, 1, 0, 2]
    picks1 = [pick_w(0), pick_h, pick_w(1), pick_h]
    shapes1 = [win_shape, rest_a.shape[1:], win_shape, rest_b.shape[1:]]
    st1, tok1 = swap_start([g_wt, rest_a, rest_b], which1, ["c"] * 4, picks1, shapes1, "rs_c_start")
    assert nb == 2
    dh0 = dh_matmul(dproj, wt_p, tok1, seq, 0)
    (g_wt, rest_a, rest_b), r1 = swap_wait(st1, dh0, which1, ["c"] * 4, picks1, "rs_c_wait")
    sel_xyc = jnp.stack([mx, my, mc]).astype(jnp.int32)
    sel1 = [jnp.stack([co[o[0]], co[o[1]]]).astype(jnp.int32) for o in ords]
    sel2 = [jnp.stack([co[o[2]]]).astype(jnp.int32) for o in ords]
    first = [rs_win_add_first(g_wt, r1[0], sel_xyc, 1, 0, "rs_add_first_0"),
             rs_add_first(rest_a, r1[1], sel1[1], "rs_add_first_1"),
             rs_win_add_first(g_wt, r1[2], sel_xyc, 0, 1, "rs_add_first_2"),
             rs_add_first(rest_b, r1[3], sel1[3], "rs_add_first_3")]
    keep1, send1 = zip(*first)
    all4 = [0, 1, 2, 3]
    none4 = [None] * 4
    axes2 = [o[1] for o in ords]
    st2, tok2 = swap_start(list(send1), all4, axes2, none4, [s.shape for s in send1], "rs_ici1_start")

    dh1 = dh_matmul(dproj, wt_p, tok2, seq, 1)
    gx0, dsh0, dsc0, dgp0 = prenorm_bwd(dh0, x2, dout, scale, g_pre, seq, tok2, 0, None)
    _, r2 = swap_wait(st2, (gx0, dh1), all4, axes2, none4, "rs_ici1_wait")
    keep2, send2 = zip(*[rs_add_second(keep1[a], r2[a], sel2[a], "rs_add_second_%d" % a) for a in range(4)])
    axes3 = [o[2] for o in ords]
    st3, tok3 = swap_start(list(send2), all4, axes3, none4, [s.shape for s in send2], "rs_ici2_start")
    grad_x2, dsh1, dsc1, dgp1 = prenorm_bwd(dh1, x2, dout, scale, g_pre, seq, tok3, 1, gx0)
    dshift = jnp.stack([dsh0, dsh1])
    dscale = jnp.stack([dsc0, dsc1])
    dg_pre = dgp0 + dgp1

    dmod = jnp.concatenate([dshift, dscale, dgate], axis=2).reshape(nb * 3 * D // 128, 128)
    small = jnp.concatenate([
        dmod, _rows128(dg_pre, 8), _rows128(dg_post, 8), _rows128(dg_q, 8), _rows128(dg_kv, 8),
        dconv[0:3].reshape(24, 128), _rows128(loss_part, 8)], axis=0)
    small_g = small_allgather(small, "gather_small_grads")
    sums = slot_sum(small_g)
    dmod_all = small_g[:, 0:48].reshape(8 * nb, 3 * D)
    g_bada = (sums[0:24] + sums[24:48]).reshape(1, 3 * D)
    g_gpre = sums[48:56].reshape(1, D)
    g_gpost = sums[56:64].reshape(1, D)
    g_gq = sums[64:67].reshape(1, QL)
    g_gkv = sums[72:74].reshape(1, KVL)
    g_conv_full = sums[80:104].reshape(3, D)
    loss = sums[104, 0]
    g_conv = lax.dynamic_slice(g_conv_full, (0, me * 128), (3, 128))
    dmod_cols = lax.dynamic_slice(dmod_all, (0, me * ada_cols), (8 * nb, ada_cols))
    g_wada = ada_bwd(c_all, dmod_cols)

    res = {}
    res["w_ada"] = [o_[None] for o_ in adamw(w_ada[0], m_w_ada[0], v_w_ada[0], g_wada, None, "adamw_w_ada", tok3)]

    def pack(b_, gp_, gpo_, gq_, gkv_, cw_):
        return jnp.concatenate([_rows128(b_, 24), _rows128(gp_, 8), _rows128(gpo_, 8), _rows128(gq_, 8),
                                _rows128(gkv_, 8), _rows128(cw_, 8)], axis=0)

    sw = pack(b_ada, g_pre, g_post, g_q, g_kv, conv_w)
    sm = pack(m_b_ada, m_g_pre, m_g_post, m_g_q, m_g_kv, m_conv_w)
    sv = pack(v_b_ada, v_g_pre, v_g_post, v_g_q, v_g_kv, v_conv_w)
    sg = pack(g_bada, g_gpre, g_gpost, g_gq, g_gkv, g_conv)
    small_out = adamw(sw, sm, sv, sg, None, "adamw_small", tok3)

    _, r3 = swap_wait(st3, small_out[0], all4, axes3, none4, "rs_ici2_wait")

    n_sq = D * 128 // 128
    unflat = lambda a, lo, shape: a[lo:lo + shape[0] * shape[1] // 128].reshape(shape)
    sq = (128, D)
    uq_s = (QL, 192)
    ukv_s = (KVL, 256)
    parts = {
        "w_conv_out": (unflat(keep2[1], 0, sq), unflat(r3[1], 0, sq)),
        "w_mla_out": (unflat(keep2[1], n_sq, sq), unflat(r3[1], n_sq, sq)),
        "w_out": (unflat(keep2[3], 0, sq), unflat(r3[3], 0, sq)),
        "w_uq": (unflat(keep2[3], n_sq, uq_s), unflat(r3[3], n_sq, uq_s)),
        "w_ukv": (unflat(keep2[3], n_sq + QL * 192 // 128, ukv_s), unflat(r3[3], n_sq + QL * 192 // 128, ukv_s)),
    }

    res["w_in"] = [o_.T[None] for o_ in adamw_win(w_in[0].T, m_w_in[0].T, v_w_in[0].T,
                                                  keep2[0], r3[0], keep2[2], r3[2])]
    ga, gb = parts["w_uq"]
    res["w_uq"] = [o_.T[None] for o_ in adamw(w_uq[0].T, m_w_uq[0].T, v_w_uq[0].T, ga.T, gb.T, "adamw_w_uq")]
    weights = {"w_conv_out": (w_conv_out, m_w_conv_out, v_w_conv_out),
               "w_ukv": (w_ukv, m_w_ukv, v_w_ukv), "w_mla_out": (w_mla_out, m_w_mla_out, v_w_mla_out),
               "w_out": (w_out, m_w_out, v_w_out)}
    for nm, (wv, mv, vv) in weights.items():
        ga, gb = parts[nm]
        outs = adamw(wv[0], mv[0], vv[0], ga, gb, "adamw_" + nm)
        res[nm] = [o_[None] for o_ in outs]

    def unpack(a):
        return {"b_ada": a[0:24].reshape(1, 3 * D), "g_pre": a[24:32].reshape(1, D),
                "g_post": a[32:40].reshape(1, D), "g_q": a[40:43].reshape(1, QL),
                "g_kv": a[48:50].reshape(1, KVL), "conv_w": a[56:59].reshape(-1)[:3 * 128].reshape(1, 3, 128)}

    for nm in ("b_ada", "g_pre", "g_post", "g_q", "g_kv", "conv_w"):
        res[nm] = [unpack(a)[nm] for a in small_out]

    order = ["w_ada", "b_ada", "g_pre", "w_in", "conv_w", "w_conv_out", "g_q", "w_uq", "g_kv", "w_ukv",
             "w_mla_out", "w_out", "g_post"]
    out = [loss, grad_x2.reshape(nb, seq, D)]
    for k_ in range(4):
        out += [res[nm][k_] for nm in order]
    return tuple(out)
```

```python
import functools

import numpy as np
import jax
import jax.numpy as jnp
from jax import lax
from jax.experimental import pallas as pl
from jax.experimental.pallas import tpu as pltpu

F32 = jnp.float32
BF16 = jnp.bfloat16
MESH = pl.DeviceIdType.MESH

D = 1024
H = 8
QL = 384
KVL = 256
ROPE = 64
HALF = ROPE // 2
DQK = 256
DV = 128
NSEG = 8
NP = NSEG * D
EPS = 1e-6
ROPE_THETA = 10000.0
SM_SCALE = (128 + ROPE) ** -0.5
LOG2E = 1.4426950408889634
LN2 = 0.6931471805599453
FLASH_TQ = 512

SEG_BZ, SEG_GA, SEG_GB, SEG_LAT, SEG_V = 0, 1, 2, 3, 4

ADAM_LR = 0.001
ADAM_B1 = 0.9
ADAM_B2 = 0.999
ADAM_EPS = 1e-08
ADAM_WD = 0.01
ADAM_STEP = 10

VMEM_LIMIT = 56 * 1024 * 1024


def _params(sem=None, vmem=VMEM_LIMIT):
    kw = dict(vmem_limit_bytes=vmem)
    if sem is not None:
        kw["dimension_semantics"] = sem
    return pltpu.CompilerParams(**kw)


def _sig(v):
    return 1.0 / (1.0 + jnp.exp(-v))


def _dot(a, b):
    return jnp.dot(a, b, preferred_element_type=F32)


def _dot_nt(a, b):
    return lax.dot_general(a, b, (((1,), (1,)), ((), ())), preferred_element_type=F32)


def _dot_tn(a, b):
    return lax.dot_general(a, b, (((0,), (0,)), ((), ())), preferred_element_type=F32)


_AXIS_POS = {"x": 0, "y": 1, "c": 2}


def _coords():
    return lax.axis_index("x"), lax.axis_index("y"), lax.axis_index("c")


def _partner(axis):
    p = list(_coords())
    p[_AXIS_POS[axis]] = 1 - p[_AXIS_POS[axis]]
    return tuple(p)


def small_allgather(v, name):
    rows = v.shape[0]

    def body(v_ref, out_ref, send_sems, recv_sems):
        x, y, c = _coords()
        me = 4 * x + 2 * y + c
        out_ref[me] = v_ref[...]
        copies = []
        for k in range(1, 8):
            peer = (1 - x if k & 4 else x, 1 - y if k & 2 else y, 1 - c if k & 1 else c)
            cp = pltpu.make_async_remote_copy(
                src_ref=v_ref, dst_ref=out_ref.at[me],
                send_sem=send_sems.at[k - 1], recv_sem=recv_sems.at[k - 1],
                device_id=peer, device_id_type=MESH)
            cp.start()
            copies.append(cp)
        for cp in copies:
            cp.wait()

    return pl.pallas_call(
        body, name=name,
        out_shape=jax.ShapeDtypeStruct((8, rows, 128), F32),
        in_specs=[pl.BlockSpec(memory_space=pltpu.VMEM)],
        out_specs=pl.BlockSpec(memory_space=pltpu.VMEM),
        scratch_shapes=[pltpu.SemaphoreType.DMA((7,)), pltpu.SemaphoreType.DMA((7,))],
    )(v)


def _own_block_placed(s):
    x, y, c = _coords()
    return lax.dynamic_update_slice(lax.empty((2, 2, 2) + s.shape, s.dtype), s[None, None, None],
                                    (x, y, c) + (0,) * s.ndim)


def allgather_big(arrs, plan, name):
    n = len(arrs)
    m = len(plan)

    def body(*refs):
        ins, outs = refs[n:2 * n], refs[2 * n:3 * n]
        send_sems, recv_sems = refs[3 * n:]
        x, y, c = _coords()
        co = {"x": x, "y": y, "c": c}

        def window(ref, lead, rows, cols):
            win = tuple(slice(None) if w is None else pl.ds(w[0], w[1]) for w in (rows, cols))
            return ref.at[tuple(lead) + win]

        def held(e, free):
            i, rows, cols, _ = plan[e]
            lead = [slice(None) if ax in free else co[ax] for ax in ("x", "y", "c")]
            return window(outs[i], lead, rows, cols)

        def rcopy(e, stage, src, dst, axis):
            return pltpu.make_async_remote_copy(
                src_ref=src, dst_ref=dst,
                send_sem=send_sems.at[e, stage], recv_sem=recv_sems.at[e, stage],
                device_id=_partner(axis), device_id_type=MESH)

        stages = [[], [], []]
        for e, (i, rows, cols, order) in enumerate(plan):
            cp = rcopy(e, 0, window(ins[i], [], rows, cols), held(e, ()), order[0])
            cp.start()
            stages[0].append(cp)
        for s in (1, 2):
            for e, (i, rows, cols, order) in enumerate(plan):
                stages[s - 1][e].wait_recv()
                blk = held(e, order[:s])
                cp = rcopy(e, s, blk, blk, order[s])
                cp.start()
                stages[s].append(cp)
        for e in range(m):
            stages[2][e].wait_recv()
        for e in range(m):
            for s in range(3):
                stages[s][e].wait_send()

    any_spec = pl.BlockSpec(memory_space=pl.ANY)
    lands = [_own_block_placed(a) for a in arrs]
    return pl.pallas_call(
        body, name=name,
        out_shape=[jax.ShapeDtypeStruct(l.shape, l.dtype) for l in lands],
        in_specs=[any_spec] * (2 * n),
        out_specs=[any_spec] * n,
        input_output_aliases={i: i for i in range(n)},
        scratch_shapes=[pltpu.SemaphoreType.DMA((m, 3)), pltpu.SemaphoreType.DMA((m, 3))],
    )(*lands, *arrs)


def exchange(arrs, axes, picks, out_shapes, name):
    n = len(arrs)

    def body(*refs):
        ins, outs = refs[:n], refs[n:2 * n]
        send_sems, recv_sems = refs[2 * n:]
        x, y, c = _coords()
        co = {"x": x, "y": y, "c": c}
        copies = []
        for a in range(n):
            src = ins[a] if picks[a] is None else picks[a](ins[a], co)
            cp = pltpu.make_async_remote_copy(
                src_ref=src, dst_ref=outs[a],
                send_sem=send_sems.at[a], recv_sem=recv_sems.at[a],
                device_id=_partner(axes[a]), device_id_type=MESH)
            cp.start()
            copies.append(cp)
        for cp in copies:
            cp.wait()

    any_spec = pl.BlockSpec(memory_space=pl.ANY)
    return pl.pallas_call(
        body, name=name,
        out_shape=[jax.ShapeDtypeStruct(s, a.dtype) for s, a in zip(out_shapes, arrs)],
        in_specs=[any_spec] * n,
        out_specs=[any_spec] * n,
        scratch_shapes=[pltpu.SemaphoreType.DMA((n,)), pltpu.SemaphoreType.DMA((n,))],
    )(*arrs)


_HBM = pl.BlockSpec(memory_space=pltpu.HBM)
_SEM = pl.BlockSpec(memory_space=pltpu.SEMAPHORE)


def _swap_copies(srcs, lands, send_sems, recv_sems, axes, picks):
    x, y, c = _coords()
    co = {"x": x, "y": y, "c": c}
    return [pltpu.make_async_remote_copy(
        src_ref=srcs[a] if picks[a] is None else picks[a](srcs[a], co), dst_ref=lands[a],
        send_sem=send_sems.at[a], recv_sem=recv_sems.at[a],
        device_id=_partner(axes[a]), device_id_type=MESH) for a in range(len(srcs))]


def swap_start(arrs, which, axes, picks, out_shapes, name):
    ns, n = len(arrs), len(which)

    def body(*refs):
        srcs, lands = refs[:ns], refs[ns:ns + n]
        send_sems, recv_sems = refs[ns + n:ns + n + 2]
        token = refs[-1]
        for cp in _swap_copies([srcs[i] for i in which], lands, send_sems, recv_sems, axes, picks):
            cp.start()
        token[...] = jnp.zeros_like(token)

    lands = [lax.empty(s, arrs[i].dtype) for s, i in zip(out_shapes, which)]
    ops = [pltpu.with_memory_space_constraint(a, pltpu.HBM) for a in list(arrs) + lands]
    out = pl.pallas_call(
        body, name=name,
        out_shape=[pltpu.SemaphoreType.DMA((n,)), pltpu.SemaphoreType.DMA((n,))]
        + [pltpu.HBM(o.shape, o.dtype) for o in ops] + [jax.ShapeDtypeStruct((8, 128), F32)],
        in_specs=[_HBM] * (ns + n),
        out_specs=[_SEM, _SEM] + [_HBM] * (ns + n) + [pl.BlockSpec(memory_space=pltpu.VMEM)],
        input_output_aliases={i: 2 + i for i in range(ns + n)},
        compiler_params=pltpu.CompilerParams(has_side_effects=pltpu.SideEffectType.DATAFLOW_SIDE_EFFECTING),
    )(*ops)
    return out[:-1], out[-1]


def swap_wait(state, after, which, axes, picks, name):
    n = len(which)
    ns = len(state) - 2 - n

    def body(*refs):
        srcs, lands = refs[:ns], refs[ns:ns + n]
        send_sems, recv_sems = refs[ns + n:ns + n + 2]
        for cp in _swap_copies([srcs[i] for i in which], lands, send_sems, recv_sems, axes, picks):
            cp.wait_send()
            cp.wait_recv()

    thru = list(state[2:])
    after = list(after) if isinstance(after, (list, tuple)) else [after]
    out = pl.pallas_call(
        body, name=name,
        out_shape=[pltpu.HBM(o.shape, o.dtype) for o in thru],
        in_specs=[_HBM] * (ns + n) + [_SEM, _SEM] + [pl.BlockSpec(memory_space=pl.ANY)] * len(after),
        out_specs=[_HBM] * (ns + n),
        input_output_aliases={i: i for i in range(ns + n)},
        compiler_params=pltpu.CompilerParams(has_side_effects=pltpu.SideEffectType.DATAFLOW_SIDE_EFFECTING),
    )(*thru, state[0], state[1], *after)
    return out[:ns], out[ns:]


def _gather_copies(shards, lands, send_sems, recv_sems):
    x, y, c = _coords()
    copies = []
    for a in range(len(shards)):
        for k in range(1, 8):
            peer = (1 - x if k & 4 else x, 1 - y if k & 2 else y, 1 - c if k & 1 else c)
            copies.append(pltpu.make_async_remote_copy(
                src_ref=shards[a], dst_ref=lands[a].at[x, y, c],
                send_sem=send_sems.at[7 * a + k - 1], recv_sem=recv_sems.at[7 * a + k - 1],
                device_id=peer, device_id_type=MESH))
    return copies


def gather_start(shards, name):
    n = len(shards)
    x, y, c = _coords()

    def body(*refs):
        srcs, lands = refs[:n], refs[n:2 * n]
        send_sems, recv_sems = refs[2 * n:2 * n + 2]
        token = refs[-1]
        for cp in _gather_copies(srcs, lands, send_sems, recv_sems):
            cp.start()
        token[...] = jnp.zeros_like(token)

    lands = [_own_block_placed(s) for s in shards]
    ops = [pltpu.with_memory_space_constraint(a, pltpu.HBM) for a in list(shards) + lands]
    out = pl.pallas_call(
        body, name=name,
        out_shape=[pltpu.SemaphoreType.DMA((7 * n,)), pltpu.SemaphoreType.DMA((7 * n,))]
        + [pltpu.HBM(o.shape, o.dtype) for o in ops] + [jax.ShapeDtypeStruct((8, 128), F32)],
        in_specs=[_HBM] * (2 * n),
        out_specs=[_SEM, _SEM] + [_HBM] * (2 * n) + [pl.BlockSpec(memory_space=pltpu.VMEM)],
        input_output_aliases={i: 2 + i for i in range(2 * n)},
        compiler_params=pltpu.CompilerParams(has_side_effects=pltpu.SideEffectType.DATAFLOW_SIDE_EFFECTING),
    )(*ops)
    return out[:-1], out[-1]


def gather_wait(state, after, name):
    n = (len(state) - 2) // 2

    def body(*refs):
        srcs, lands = refs[:n], refs[n:2 * n]
        send_sems, recv_sems = refs[2 * n:2 * n + 2]
        for cp in _gather_copies(srcs, lands, send_sems, recv_sems):
            cp.wait_send()
            cp.wait_recv()

    thru = list(state[2:])
    out = pl.pallas_call(
        body, name=name,
        out_shape=[pltpu.HBM(o.shape, o.dtype) for o in thru],
        in_specs=[_HBM] * (2 * n) + [_SEM, _SEM, pl.BlockSpec(memory_space=pl.ANY)],
        out_specs=[_HBM] * (2 * n),
        input_output_aliases={i: i for i in range(2 * n)},
        compiler_params=pltpu.CompilerParams(has_side_effects=pltpu.SideEffectType.DATAFLOW_SIDE_EFFECTING),
    )(*thru, state[0], state[1], after)
    return out[n:]


def _scatter_copies(grads, lands, send_sems, recv_sems):
    x, y, c = _coords()
    me = 4 * x + 2 * y + c
    copies = []
    for a in range(len(grads)):
        r = grads[a].shape[0] // 8
        for k in range(1, 8):
            px, py, pc = (1 - x if k & 4 else x, 1 - y if k & 2 else y, 1 - c if k & 1 else c)
            rows = pl.ds(pl.multiple_of((4 * px + 2 * py + pc) * r, r), r)
            copies.append(pltpu.make_async_remote_copy(
                src_ref=grads[a].at[rows], dst_ref=lands[a].at[me],
                send_sem=send_sems.at[7 * a + k - 1], recv_sem=recv_sems.at[7 * a + k - 1],
                device_id=(px, py, pc), device_id_type=MESH))
    return copies


def scatter_start(grads, name):
    n = len(grads)

    def body(*refs):
        srcs, lands = refs[:n], refs[n:2 * n]
        send_sems, recv_sems = refs[2 * n:2 * n + 2]
        token = refs[-1]
        for cp in _scatter_copies(srcs, lands, send_sems, recv_sems):
            cp.start()
        token[...] = jnp.zeros_like(token)

    lands = [jnp.zeros((8, g.shape[0] // 8, g.shape[1]), g.dtype) for g in grads]
    ops = [pltpu.with_memory_space_constraint(a, pltpu.HBM) for a in list(grads) + lands]
    out = pl.pallas_call(
        body, name=name,
        out_shape=[pltpu.SemaphoreType.DMA((7 * n,)), pltpu.SemaphoreType.DMA((7 * n,))]
        + [pltpu.HBM(o.shape, o.dtype) for o in ops] + [jax.ShapeDtypeStruct((8, 128), F32)],
        in_specs=[_HBM] * (2 * n),
        out_specs=[_SEM, _SEM] + [_HBM] * (2 * n) + [pl.BlockSpec(memory_space=pltpu.VMEM)],
        input_output_aliases={i: 2 + i for i in range(2 * n)},
        compiler_params=pltpu.CompilerParams(has_side_effects=pltpu.SideEffectType.DATAFLOW_SIDE_EFFECTING),
    )(*ops)
    return out[:-1], out[-1]


def scatter_wait(state, after, name):
    n = (len(state) - 2) // 2

    def body(*refs):
        srcs, lands = refs[:n], refs[n:2 * n]
        send_sems, recv_sems = refs[2 * n:2 * n + 2]
        for cp in _scatter_copies(srcs, lands, send_sems, recv_sems):
            cp.wait_send()
            cp.wait_recv()

    thru = list(state[2:])
    after = list(after) if isinstance(after, (list, tuple)) else [after]
    out = pl.pallas_call(
        body, name=name,
        out_shape=[pltpu.HBM(o.shape, o.dtype) for o in thru],
        in_specs=[_HBM] * (2 * n) + [_SEM, _SEM] + [pl.BlockSpec(memory_space=pl.ANY)] * len(after),
        out_specs=[_HBM] * (2 * n),
        input_output_aliases={i: i for i in range(2 * n)},
        compiler_params=pltpu.CompilerParams(has_side_effects=pltpu.SideEffectType.DATAFLOW_SIDE_EFFECTING),
    )(*thru, state[0], state[1], *after)
    return out[:n], out[n:]


def rs_win_add_first(g, r, sel, next_dim, col, name):
    rows, cols = r.shape[2:]

    def body(sel_ref, gk_ref, rk_ref, gs_ref, rs_ref, keep_ref, send_ref):
        keep_ref[...] = gk_ref[...] + rk_ref[...]
        send_ref[...] = (gs_ref[...] + rs_ref[...]).astype(BF16)

    def g_map(flip):
        def f(j, s):
            nxt = 1 - s[next_dim] if flip else s[next_dim]
            return (nxt, j, s[2], 0, col) if next_dim == 0 else (j, nxt, s[2], 0, col)
        return f

    def r_map(flip):
        def f(j, s):
            nxt = 1 - s[next_dim] if flip else s[next_dim]
            return (nxt, j, 0, 0) if next_dim == 0 else (j, nxt, 0, 0)
        return f

    gblk = (None, None, None, rows, cols)
    rblk = (None, None, rows, cols)
    oblk = (None, rows, cols)
    return pl.pallas_call(
        body, name=name,
        grid_spec=pltpu.PrefetchScalarGridSpec(
            num_scalar_prefetch=1, grid=(2,),
            in_specs=[pl.BlockSpec(gblk, g_map(False)), pl.BlockSpec(rblk, r_map(False)),
                      pl.BlockSpec(gblk, g_map(True)), pl.BlockSpec(rblk, r_map(True))],
            out_specs=[pl.BlockSpec(oblk, lambda j, s: (j, 0, 0)),
                       pl.BlockSpec(oblk, lambda j, s: (j, 0, 0))]),
        out_shape=[jax.ShapeDtypeStruct((2, rows, cols), F32),
                   jax.ShapeDtypeStruct((2, rows, cols), BF16)],
        compiler_params=_params(),
    )(sel, g, r, g, r)


def rs_add_first(g, r, sel, name):
    _, _, _, rows, cols = g.shape
    tr = rows // 2

    def body(sel_ref, gk_ref, rk_ref, gs_ref, rs_ref, keep_ref, send_ref):
        keep_ref[...] = gk_ref[...] + rk_ref[...]
        send_ref[...] = (gs_ref[...] + rs_ref[...]).astype(BF16)

    blk = (None, None, None, tr, cols)
    rblk = (None, None, tr, cols)
    oblk = (None, tr, cols)
    return pl.pallas_call(
        body, name=name,
        grid_spec=pltpu.PrefetchScalarGridSpec(
            num_scalar_prefetch=1, grid=(2, 2),
            in_specs=[
                pl.BlockSpec(blk, lambda j, i, s: (s[0], s[1], j, i, 0)),
                pl.BlockSpec(rblk, lambda j, i, s: (s[1], j, i, 0)),
                pl.BlockSpec(blk, lambda j, i, s: (s[0], 1 - s[1], j, i, 0)),
                pl.BlockSpec(rblk, lambda j, i, s: (1 - s[1], j, i, 0)),
            ],
            out_specs=[pl.BlockSpec(oblk, lambda j, i, s: (j, i, 0)),
                       pl.BlockSpec(oblk, lambda j, i, s: (j, i, 0))]),
        out_shape=[jax.ShapeDtypeStruct((2, rows, cols), F32),
                   jax.ShapeDtypeStruct((2, rows, cols), BF16)],
        compiler_params=_params(),
    )(sel, g, r, g, r)


def rs_add_second(k, r, sel, name):
    _, rows, cols = k.shape
    tr = rows // 2 if rows % 32 == 0 else rows
    nt = rows // tr

    def body(sel_ref, kk_ref, rk_ref, ks_ref, rs_ref, keep_ref, send_ref):
        keep_ref[...] = kk_ref[...] + rk_ref[...].astype(F32)
        send_ref[...] = (ks_ref[...] + rs_ref[...].astype(F32)).astype(BF16)

    blk = (None, tr, cols)
    oblk = (tr, cols)
    return pl.pallas_call(
        body, name=name,
        grid_spec=pltpu.PrefetchScalarGridSpec(
            num_scalar_prefetch=1, grid=(nt,),
            in_specs=[
                pl.BlockSpec(blk, lambda i, s: (s[0], i, 0)),
                pl.BlockSpec(blk, lambda i, s: (s[0], i, 0)),
                pl.BlockSpec(blk, lambda i, s: (1 - s[0], i, 0)),
                pl.BlockSpec(blk, lambda i, s: (1 - s[0], i, 0)),
            ],
            out_specs=[pl.BlockSpec(oblk, lambda i, s: (i, 0)),
                       pl.BlockSpec(oblk, lambda i, s: (i, 0))]),
        out_shape=[jax.ShapeDtypeStruct((rows, cols), F32),
                   jax.ShapeDtypeStruct((rows, cols), BF16)],
        compiler_params=_params(),
    )(sel, k, r, k, r)


SEG_ROWS = (4800, 5824, 6848, 4096, 0, 1024, 2048, 3072)
LAT_ROWS = QL + KVL + ROPE
N_IN = 7872


def _seg_row(j):
    return pl.multiple_of(jnp.where(j < 3, 4800 + 1024 * j, jnp.where(j == 3, 4096, (j - 4) * 1024)), 8)


def proj_matmul(h, wt_bits, token):
    t = h.shape[0]
    tm = min(1024, t)

    def body(h_ref, w_hbm, tok_ref, o_ref, wt_ref, buf, sems):
        j = pl.program_id(0)
        slot = j % 2

        def fetch(seg, into):
            return pltpu.make_async_copy(w_hbm.at[pl.ds(_seg_row(seg), D)], buf.at[into], sems.at[into])

        @pl.when(pl.program_id(1) == 0)
        def _():
            @pl.when(j == 0)
            def _():
                fetch(j, slot).start()

            fetch(j, slot).wait()

            @pl.when(j + 1 < NSEG)
            def _():
                fetch(j + 1, 1 - slot).start()

            bits = pltpu.bitcast(buf[slot], jnp.uint32)
            row = lax.broadcasted_iota(jnp.int32, (D, D // 2), 0)
            live = jnp.logical_or(j != SEG_LAT, row < LAT_ROWS)
            lo = pltpu.bitcast(bits << 16, F32)
            hi = pltpu.bitcast(bits & jnp.uint32(0xFFFF0000), F32)
            wt_ref[:, :D // 2] = jnp.where(live, lo, 0.0).astype(BF16)
            wt_ref[:, D // 2:] = jnp.where(live, hi, 0.0).astype(BF16)

        o_ref[...] = _dot_nt(h_ref[...], wt_ref[...]).astype(BF16)

    return pl.pallas_call(
        body, name="proj_matmul", grid=(NSEG, t // tm),
        in_specs=[pl.BlockSpec((tm, D), lambda j, i: (i, 0)),
                  pl.BlockSpec(memory_space=pl.ANY),
                  pl.BlockSpec((8, 128), lambda j, i: (0, 0))],
        out_specs=[pl.BlockSpec((None, tm, D), lambda j, i: (j, i, 0)),
                   pl.BlockSpec((D, D), lambda j, i: (j, 0))],
        out_shape=[jax.ShapeDtypeStruct((NSEG, t, D), BF16), jax.ShapeDtypeStruct((NP, D), BF16)],
        scratch_shapes=[pltpu.VMEM((2, D, D // 2), F32), pltpu.SemaphoreType.DMA((2,))],
        compiler_params=_params(("arbitrary", "arbitrary")),
    )(h, wt_bits, token)


def dh_matmul(dproj, wt, token, seq, b):
    tm = min(1024, seq)
    nblk = seq // tm

    def body(d_ref, w_ref, tok_ref, o_ref, acc_ref):
        k = pl.program_id(1)

        @pl.when(k == 0)
        def _():
            acc_ref[...] = jnp.zeros_like(acc_ref)

        acc_ref[...] += _dot(d_ref[...], w_ref[...])

        @pl.when(k == NSEG - 1)
        def _():
            o_ref[...] = acc_ref[...]

    return pl.pallas_call(
        body, name="dh_matmul_%d" % b, grid=(nblk, NSEG),
        in_specs=[pl.BlockSpec((None, tm, D), lambda i, k: (k, b * nblk + i, 0)),
                  pl.BlockSpec((D, D), lambda i, k: (k, 0)),
                  pl.BlockSpec((8, 128), lambda i, k: (0, 0))],
        out_specs=pl.BlockSpec((tm, D), lambda i, k: (i, 0)),
        out_shape=jax.ShapeDtypeStruct((seq, D), F32),
        scratch_shapes=[pltpu.VMEM((tm, D), F32)],
        compiler_params=_params(("parallel", "arbitrary")),
    )(dproj, wt, token)


def win_grad_matmul(h, dproj, token):
    t = h.shape[0]
    tk = min(1024, t)
    nk = t // tk

    def body(h_ref, d_ref, tok_ref, o_hbm, acc_ref, sem):
        j = pl.program_id(0)
        k = pl.program_id(1)

        @pl.when(k == 0)
        def _():
            acc_ref[...] = jnp.zeros_like(acc_ref)

        acc_ref[...] += _dot_tn(d_ref[...], h_ref[...])

        @pl.when(jnp.logical_and(k == nk - 1, j != SEG_LAT))
        def _():
            cp = pltpu.make_async_copy(acc_ref, o_hbm.at[pl.ds(_seg_row(j), D)], sem)
            cp.start()
            cp.wait()

        @pl.when(jnp.logical_and(k == nk - 1, j == SEG_LAT))
        def _():
            cp = pltpu.make_async_copy(acc_ref.at[pl.ds(0, LAT_ROWS)],
                                       o_hbm.at[pl.ds(SEG_ROWS[SEG_LAT], LAT_ROWS)], sem)
            cp.start()
            cp.wait()

    return pl.pallas_call(
        body, name="win_grad_matmul", grid=(NSEG, nk),
        in_specs=[pl.BlockSpec((tk, D), lambda j, k: (k, 0)),
                  pl.BlockSpec((None, tk, D), lambda j, k: (j, k, 0)),
                  pl.BlockSpec((8, 128), lambda j, k: (0, 0))],
        out_specs=pl.BlockSpec(memory_space=pl.ANY),
        out_shape=jax.ShapeDtypeStruct((N_IN, D), F32),
        scratch_shapes=[pltpu.VMEM((D, D), F32), pltpu.SemaphoreType.DMA],
        compiler_params=_params(("arbitrary", "arbitrary")),
    )(h, dproj, token)


def grad_matmul(a, b, name):
    t, m = a.shape
    n = b.shape[1]
    tk = min(1024, t)
    nk = t // tk

    def body(a_ref, b_ref, o_ref, acc_ref):
        k = pl.program_id(0)

        @pl.when(k == 0)
        def _():
            acc_ref[...] = jnp.zeros_like(acc_ref)

        acc_ref[...] += _dot_tn(a_ref[...], b_ref[...])

        @pl.when(k == nk - 1)
        def _():
            o_ref[...] = acc_ref[...].astype(BF16)

    return pl.pallas_call(
        body, name=name, grid=(nk,),
        in_specs=[pl.BlockSpec((tk, m), lambda k: (k, 0)),
                  pl.BlockSpec((tk, n), lambda k: (k, 0))],
        out_specs=pl.BlockSpec((m, n), lambda k: (0, 0)),
        out_shape=jax.ShapeDtypeStruct((m, n), BF16),
        scratch_shapes=[pltpu.VMEM((m, n), F32)],
        compiler_params=_params(("arbitrary",)),
    )(a, b)


def ada_fwd(c_all, w_ada, b_cols):
    def body(c_ref, w_ref, b_ref, o_ref):
        o_ref[...] = _dot(c_ref[...].astype(BF16), w_ref[...].astype(BF16)) + b_ref[...]

    return pl.pallas_call(
        body, name="ada_fwd",
        out_shape=jax.ShapeDtypeStruct((c_all.shape[0], w_ada.shape[1]), F32),
        compiler_params=_params(),
    )(c_all, w_ada, b_cols)


def ada_bwd(c_all, dmod_cols):
    def body(c_ref, d_ref, o_ref):
        o_ref[...] = _dot_tn(c_ref[...].astype(BF16), d_ref[...].astype(BF16))

    return pl.pallas_call(
        body, name="ada_bwd",
        out_shape=jax.ShapeDtypeStruct((c_all.shape[1], dmod_cols.shape[1]), F32),
        compiler_params=_params(),
    )(c_all, dmod_cols)


def slot_sum(g):
    def body(g_ref, o_ref):
        acc = g_ref[0]
        for s in range(1, 8):
            acc = acc + g_ref[s]
        o_ref[...] = acc

    return pl.pallas_call(
        body, name="slot_sum",
        out_shape=jax.ShapeDtypeStruct(g.shape[1:], F32),
    )(g)


def prenorm_fwd(x2, scale, shift, g_pre, seq):
    t = x2.shape[0]
    tm = min(512, seq)
    tpb = seq // tm

    def body(x_ref, sc_ref, sh_ref, g_ref, h_ref):
        xv = x_ref[...]
        r = lax.rsqrt(jnp.mean(xv * xv, axis=-1, keepdims=True) + EPS)
        hv = (xv * r * g_ref[...]) * (1.0 + sc_ref[...]) + sh_ref[...]
        h_ref[...] = hv.astype(BF16)

    per_batch = pl.BlockSpec((None, 1, D), lambda i: (i // tpb, 0, 0))
    return pl.pallas_call(
        body, name="prenorm_fwd", grid=(t // tm,),
        in_specs=[pl.BlockSpec((tm, D), lambda i: (i, 0)), per_batch, per_batch,
                  pl.BlockSpec((1, D), lambda i: (0, 0))],
        out_specs=pl.BlockSpec((tm, D), lambda i: (i, 0)),
        out_shape=jax.ShapeDtypeStruct((t, D), BF16),
        compiler_params=_params(("parallel",)),
    )(x2, scale, shift, g_pre)


def prenorm_bwd(dh, x2, dout, scale, g_pre, seq, token, b, gx_prev):
    t = x2.shape[0]
    tm = min(512, seq)
    tpb = seq // tm

    def body(dh_ref, x_ref, do_ref, sc_ref, g_ref, tok_ref, *rest):
        gx_ref, dsh_ref, dsc_ref, dg_ref = rest[-4:]
        i = pl.program_id(0)
        xv = x_ref[...]
        dhv = dh_ref[...]
        g = g_ref[...]
        r = lax.rsqrt(jnp.mean(xv * xv, axis=-1, keepdims=True) + EPS)
        nrm = xv * r
        dxn = dhv * (1.0 + sc_ref[...])
        dn = dxn * g
        dx = r * (dn - nrm * jnp.mean(dn * nrm, axis=-1, keepdims=True))
        gx_ref[...] = dx + do_ref[...]

        @pl.when(i == 0)
        def _():
            dsh_ref[...] = jnp.zeros_like(dsh_ref)
            dsc_ref[...] = jnp.zeros_like(dsc_ref)
            dg_ref[...] = jnp.zeros_like(dg_ref)

        dsh_ref[...] += jnp.sum(dhv, axis=0, keepdims=True)
        dsc_ref[...] += jnp.sum(dhv * (nrm * g), axis=0, keepdims=True)
        dg_ref[...] += jnp.sum(dxn * nrm, axis=0, keepdims=True)

    row = pl.BlockSpec((tm, D), lambda i: (i, 0))
    grow = pl.BlockSpec((tm, D), lambda i: (b * tpb + i, 0))
    per_batch = pl.BlockSpec((None, 1, D), lambda i: (b, 0, 0))
    vec = pl.BlockSpec((1, D), lambda i: (0, 0))
    ins = [dh, x2, dout, scale, g_pre, token]
    specs = [row, grow, grow, per_batch, vec, pl.BlockSpec((8, 128), lambda i: (0, 0))]
    alias = {}
    if gx_prev is not None:
        ins.append(gx_prev)
        specs.append(pl.BlockSpec(memory_space=pl.ANY))
        alias = {6: 0}
    return pl.pallas_call(
        body, name="prenorm_bwd_%d" % b, grid=(tpb,),
        in_specs=specs,
        out_specs=[grow, vec, vec, vec],
        out_shape=[jax.ShapeDtypeStruct((t, D), F32), jax.ShapeDtypeStruct((1, D), F32),
                   jax.ShapeDtypeStruct((1, D), F32), jax.ShapeDtypeStruct((1, D), F32)],
        input_output_aliases=alias,
        compiler_params=_params(("arbitrary",)),
    )(*ins)


CONV_TC = 128


def _shift_down(u, k, rows):
    idx = lax.broadcasted_iota(jnp.int32, u.shape, 0)
    return jnp.where(idx >= k, pltpu.roll(u, k, 0), 0.0)


def _shift_up(u, k, rows):
    idx = lax.broadcasted_iota(jnp.int32, u.shape, 0)
    return jnp.where(idx < rows - k, pltpu.roll(u, rows - k, 0), 0.0)


def conv_fwd(proj, conv_w, seq):
    t = proj.shape[1]
    nb = t // seq

    def body(p_ref, w_ref, y_ref):
        av = p_ref[0].astype(F32)
        ab = p_ref[1].astype(F32)
        ac = p_ref[2].astype(F32)
        az = p_ref[3].astype(F32)
        w = w_ref[...]
        u = ac * av
        y1 = _shift_down(u, 2, seq) * w[0:1] + _shift_down(u, 1, seq) * w[1:2] + u * w[2:3]
        y_ref[...] = (ab * y1 * (az * _sig(az))).astype(BF16)

    return pl.pallas_call(
        body, name="conv_fwd", grid=(nb, D // CONV_TC),
        in_specs=[pl.BlockSpec((4, seq, CONV_TC), lambda b, ci: (1, b, ci)),
                  pl.BlockSpec((8, CONV_TC), lambda b, ci: (0, ci))],
        out_specs=pl.BlockSpec((seq, CONV_TC), lambda b, ci: (b, ci)),
        out_shape=jax.ShapeDtypeStruct((t, D), BF16),
        compiler_params=_params(("parallel", "parallel")),
    )(proj, conv_w)


def conv_bwd(dproj, proj, dy, conv_w, seq):
    t = proj.shape[1]
    nb = t // seq

    def body(dp_in_ref, p_ref, dy_ref, w_ref, dp_ref, dw_ref):
        b = pl.program_id(1)
        av = p_ref[0].astype(F32)
        ab = p_ref[1].astype(F32)
        ac = p_ref[2].astype(F32)
        az = p_ref[3].astype(F32)
        dyv = dy_ref[...].astype(F32)
        w = w_ref[...]
        u = ac * av
        u1 = _shift_down(u, 1, seq)
        u2 = _shift_down(u, 2, seq)
        y1 = u2 * w[0:1] + u1 * w[1:2] + u * w[2:3]
        sz = _sig(az)
        silu = az * sz
        dy1 = dyv * ab * silu
        du = dy1 * w[2:3] + _shift_up(dy1, 1, seq) * w[1:2] + _shift_up(dy1, 2, seq) * w[0:1]
        dp_ref[0] = (du * ac).astype(BF16)
        dp_ref[1] = (dyv * y1 * silu).astype(BF16)
        dp_ref[2] = (du * av).astype(BF16)
        dp_ref[3] = (dyv * ab * y1 * (sz * (1.0 + az * (1.0 - sz)))).astype(BF16)

        @pl.when(b == 0)
        def _():
            dw_ref[...] = jnp.zeros_like(dw_ref)

        dw_ref[0:1, :] += jnp.sum(dy1 * u2, axis=0, keepdims=True)
        dw_ref[1:2, :] += jnp.sum(dy1 * u1, axis=0, keepdims=True)
        dw_ref[2:3, :] += jnp.sum(dy1 * u, axis=0, keepdims=True)

    return pl.pallas_call(
        body, name="conv_bwd", grid=(D // CONV_TC, nb),
        in_specs=[pl.BlockSpec(memory_space=pl.ANY),
                  pl.BlockSpec((4, seq, CONV_TC), lambda ci, b: (1, b, ci)),
                  pl.BlockSpec((seq, CONV_TC), lambda ci, b: (b, ci)),
                  pl.BlockSpec((8, CONV_TC), lambda ci, b: (0, ci))],
        out_specs=[pl.BlockSpec((4, seq, CONV_TC), lambda ci, b: (1, b, ci)),
                   pl.BlockSpec((8, CONV_TC), lambda ci, b: (0, ci))],
        out_shape=[jax.ShapeDtypeStruct(dproj.shape, BF16),
                   jax.ShapeDtypeStruct((8, D), F32)],
        input_output_aliases={0: 0},
        compiler_params=_params(("parallel", "arbitrary")),
    )(dproj, proj, dy, conv_w)


def _rope_tables(pos_ref, invf_ref, ma_ref, mb_ref):
    ang = pos_ref[...].astype(F32) * invf_ref[...]
    cs = jnp.cos(ang)
    sn = jnp.sin(ang)
    return cs, sn * ma_ref[...], sn * mb_ref[...]


def _head_tables(cs, sa, sb):
    one = jnp.ones_like(cs)
    zero = jnp.zeros_like(cs)
    return (jnp.tile(jnp.concatenate([one, cs], axis=1), (1, H)),
            jnp.tile(jnp.concatenate([zero, sa], axis=1), (1, H)),
            jnp.tile(jnp.concatenate([zero, sb], axis=1), (1, H)))


def _rotate(v, cs, sa, sb, sign):
    width = v.shape[1]
    return v * cs + sign * (pltpu.roll(v, width - HALF, 1) * sa + pltpu.roll(v, HALF, 1) * sb)


MLA_TM = 256


def mla_prep_fwd(proj, pos, g_q, g_kv, wuq, wukv, tabs):
    t = proj.shape[1]
    tm = min(MLA_TM, t)

    def body(lat_ref, pos_ref, gq_ref, gkv_ref, wuq_ref, wukv_ref, invf_ref, ma_ref, mb_ref,
             q_ref, k_ref, kv_ref, qn_ref, kvn_ref):
        lat = lat_ref[...].astype(F32)
        ql = lat[:, :QL]
        kl = lat[:, QL:QL + KVL]
        kr = lat[:, QL + KVL:QL + KVL + 128]
        qn = (ql * lax.rsqrt(jnp.mean(ql * ql, axis=-1, keepdims=True) + EPS) * gq_ref[...]).astype(BF16)
        kvn = (kl * lax.rsqrt(jnp.mean(kl * kl, axis=-1, keepdims=True) + EPS) * gkv_ref[...]).astype(BF16)
        qn_ref[...] = qn
        kvn_ref[...] = kvn
        cs, sa, sb = _rope_tables(pos_ref, invf_ref, ma_ref, mb_ref)
        hc, ha, hb = _head_tables(cs, sa, sb)
        q = _dot_nt(qn, wuq_ref[...])
        q_ref[...] = (_rotate(q, hc, ha, hb, 1.0) * (SM_SCALE * LOG2E)).astype(BF16)
        kv = _dot_nt(kvn, wukv_ref[...]).astype(BF16)
        kv_ref[...] = kv
        kpe = _rotate(kr, cs, sa, sb, 1.0).astype(BF16)
        for hh in range(H):
            k_ref[:, hh * DQK:hh * DQK + 128] = kv[:, hh * DQK:hh * DQK + 128]
            k_ref[:, hh * DQK + 128:(hh + 1) * DQK] = kpe

    row = lambda w: pl.BlockSpec((tm, w), lambda i: (i, 0))
    const = lambda a: pl.BlockSpec(a.shape, lambda i: (0,) * a.ndim)
    return pl.pallas_call(
        body, name="mla_prep_fwd", grid=(t // tm,),
        in_specs=[pl.BlockSpec((None, tm, D), lambda i: (SEG_LAT, i, 0)), row(1),
                  const(g_q), const(g_kv), const(wuq), const(wukv)] + [const(a) for a in tabs],
        out_specs=[row(H * DQK), row(H * DQK), row(H * DQK), row(QL), row(KVL)],
        out_shape=[jax.ShapeDtypeStruct((t, H * DQK), BF16)] * 3
        + [jax.ShapeDtypeStruct((t, QL), BF16), jax.ShapeDtypeStruct((t, KVL), BF16)],
        compiler_params=_params(("parallel",)),
    )(proj, pos, g_q, g_kv, wuq, wukv, *tabs)


def mla_prep_bwd(dproj, proj, dq_rot, dk, dv, pos, g_q, g_kv, wuq, wukv, tabs):
    t = proj.shape[1]
    tm = min(MLA_TM, t)

    def body(dp_in_ref, lat_ref, dqr_ref, dk_ref, dv_ref, pos_ref, gq_ref, gkv_ref, wuq_ref, wukv_ref,
             invf_ref, ma_ref, mb_ref, dp_ref, dq_ref, dkv_ref, dgq_ref, dgkv_ref):
        i = pl.program_id(0)
        lat = lat_ref[...].astype(F32)
        ql = lat[:, :QL]
        kl = lat[:, QL:QL + KVL]
        rq = lax.rsqrt(jnp.mean(ql * ql, axis=-1, keepdims=True) + EPS)
        rk = lax.rsqrt(jnp.mean(kl * kl, axis=-1, keepdims=True) + EPS)
        nq = ql * rq
        nk = kl * rk
        cs, sa, sb = _rope_tables(pos_ref, invf_ref, ma_ref, mb_ref)
        hc, ha, hb = _head_tables(cs, sa, sb)
        dq = _rotate(dqr_ref[...] * SM_SCALE, hc, ha, hb, -1.0).astype(BF16)
        dq_ref[...] = dq
        dkpe = jnp.zeros((tm, 128), F32)
        for hh in range(H):
            dkv_ref[:, hh * DQK:hh * DQK + 128] = dk_ref[:, hh * DQK:hh * DQK + 128]
            dkv_ref[:, hh * DQK + 128:(hh + 1) * DQK] = dv_ref[:, hh * DV:(hh + 1) * DV]
            dkpe = dkpe + dk_ref[:, hh * DQK + 128:(hh + 1) * DQK].astype(F32)
        lane = lax.broadcasted_iota(jnp.int32, (tm, 128), 1)
        dkr = jnp.where(lane < ROPE, _rotate(dkpe, cs, sa, sb, -1.0), 0.0)
        dqn = _dot(dq, wuq_ref[...])
        dkvn = _dot(dkv_ref[...], wukv_ref[...])
        gq = gq_ref[...]
        gkv = gkv_ref[...]
        dnq = dqn * gq
        dnk = dkvn * gkv
        dql = rq * (dnq - nq * jnp.mean(dnq * nq, axis=-1, keepdims=True))
        dkl = rk * (dnk - nk * jnp.mean(dnk * nk, axis=-1, keepdims=True))
        dp_ref[:, :QL] = dql.astype(BF16)
        dp_ref[:, QL:QL + KVL] = dkl.astype(BF16)
        dp_ref[:, QL + KVL:QL + KVL + 128] = dkr.astype(BF16)
        dp_ref[:, QL + KVL + 128:] = jnp.zeros((tm, D - QL - KVL - 128), BF16)

        @pl.when(i == 0)
        def _():
            dgq_ref[...] = jnp.zeros_like(dgq_ref)
            dgkv_ref[...] = jnp.zeros_like(dgkv_ref)

        dgq_ref[...] += jnp.sum(dqn * nq, axis=0, keepdims=True)
        dgkv_ref[...] += jnp.sum(dkvn * nk, axis=0, keepdims=True)

    row = lambda w: pl.BlockSpec((tm, w), lambda i: (i, 0))
    const = lambda a: pl.BlockSpec(a.shape, lambda i: (0,) * a.ndim)
    seg = pl.BlockSpec((None, tm, D), lambda i: (SEG_LAT, i, 0))
    return pl.pallas_call(
        body, name="mla_prep_bwd", grid=(t // tm,),
        in_specs=[pl.BlockSpec(memory_space=pl.ANY), seg, row(H * DQK), row(H * DQK), row(H * DV), row(1),
                  const(g_q), const(g_kv), const(wuq), const(wukv)] + [const(a) for a in tabs],
        out_specs=[seg, row(H * DQK), row(H * DQK),
                   pl.BlockSpec((1, QL), lambda i: (0, 0)), pl.BlockSpec((1, KVL), lambda i: (0, 0))],
        out_shape=[jax.ShapeDtypeStruct(dproj.shape, BF16),
                   jax.ShapeDtypeStruct((t, H * DQK), BF16), jax.ShapeDtypeStruct((t, H * DQK), BF16),
                   jax.ShapeDtypeStruct((1, QL), F32), jax.ShapeDtypeStruct((1, KVL), F32)],
        input_output_aliases={0: 0},
        compiler_params=_params(("arbitrary",)),
    )(dproj, proj, dq_rot, dk, dv, pos, g_q, g_kv, wuq, wukv, *tabs)


def _causal_mask(s, n):
    row = lax.broadcasted_iota(jnp.int32, (n, n), 0)
    col = lax.broadcasted_iota(jnp.int32, (n, n), 1)
    return jnp.where(col <= row, s, -1e30)


def flash_fwd(q, k, kv, nb, seq):
    t = q.shape[0]
    tq = min(FLASH_TQ, seq)
    nq = seq // tq

    def body(q_ref, k_ref, v_ref, o_ref, lse_ref):
        for qi in range(nq):
            qs = slice(qi * tq, (qi + 1) * tq)
            qv = q_ref[qs, :]
            m = jnp.full((tq, 1), -1e30, F32)
            l = jnp.zeros((tq, 1), F32)
            acc = jnp.zeros((tq, DV), F32)
            for j in range(qi + 1):
                ks = slice(j * tq, (j + 1) * tq)
                s = _dot_nt(qv, k_ref[ks, :])
                if j == qi:
                    s = _causal_mask(s, tq)
                m_new = jnp.maximum(m, jnp.max(s, axis=1, keepdims=True))
                p = jnp.exp2(s - m_new)
                alpha = jnp.exp2(m - m_new)
                l = alpha * l + jnp.sum(p, axis=1, keepdims=True)
                acc = alpha * acc + _dot(p.astype(BF16), v_ref[ks, :])
                m = m_new
            o_ref[qs, :] = (acc / l).astype(BF16)
            lse_ref[qs, :] = jnp.broadcast_to(m + jnp.log(l) * LOG2E, (tq, DV))

    out_blk = pl.BlockSpec((seq, DV), lambda b, h: (b, h))
    return pl.pallas_call(
        body, name="flash_fwd", grid=(nb, H),
        in_specs=[pl.BlockSpec((seq, DQK), lambda b, h: (b, h)),
                  pl.BlockSpec((seq, DQK), lambda b, h: (b, h)),
                  pl.BlockSpec((seq, DV), lambda b, h: (b, 2 * h + 1))],
        out_specs=[out_blk, out_blk],
        out_shape=[jax.ShapeDtypeStruct((t, H * DV), BF16), jax.ShapeDtypeStruct((t, H * DV), F32)],
        compiler_params=_params(("parallel", "parallel")),
    )(q, k, kv)


def flash_bwd(q, k, kv, o, do, lse, nb, seq, token):
    t = q.shape[0]
    tq = min(FLASH_TQ, seq)
    nq = seq // tq

    def body(q_ref, k_ref, v_ref, o_ref, do_ref, lse_ref, tok_ref, dq_ref, dk_ref, dv_ref):
        delta = []
        for qi in range(nq):
            qs = slice(qi * tq, (qi + 1) * tq)
            delta.append(jnp.sum(do_ref[qs, :].astype(F32) * o_ref[qs, :].astype(F32), axis=1, keepdims=True))
        for ki in range(nq):
            ks = slice(ki * tq, (ki + 1) * tq)
            kb = k_ref[ks, :]
            vb = v_ref[ks, :]
            dk = jnp.zeros((tq, DQK), F32)
            dv = jnp.zeros((tq, DV), F32)
            for qi in range(ki, nq):
                qs = slice(qi * tq, (qi + 1) * tq)
                qv = q_ref[qs, :]
                dov = do_ref[qs, :]
                s = _dot_nt(qv, kb)
                if qi == ki:
                    s = _causal_mask(s, tq)
                p = jnp.exp2(s - lse_ref[qs, :][:, :1])
                dp = _dot_nt(dov, vb)
                dz = (p * (dp - delta[qi])).astype(BF16)
                dv = dv + _dot_tn(p.astype(BF16), dov)
                dk = dk + _dot_tn(dz, qv)
                dqb = _dot(dz, kb)
                if ki == 0:
                    dq_ref[qs, :] = dqb
                else:
                    dq_ref[qs, :] += dqb
            dk_ref[ks, :] = (dk * LN2).astype(BF16)
            dv_ref[ks, :] = dv.astype(BF16)

    full = lambda w, col: pl.BlockSpec((seq, w), col)
    same = lambda b, h: (b, h)
    return pl.pallas_call(
        body, name="flash_bwd", grid=(nb, H),
        in_specs=[full(DQK, same), full(DQK, same), full(DV, lambda b, h: (b, 2 * h + 1)),
                  full(DV, same), full(DV, same), full(DV, same),
                  pl.BlockSpec((8, 128), lambda b, h: (0, 0))],
        out_specs=[full(DQK, same), full(DQK, same), full(DV, same)],
        out_shape=[jax.ShapeDtypeStruct((t, H * DQK), F32), jax.ShapeDtypeStruct((t, H * DQK), BF16),
                   jax.ShapeDtypeStruct((t, H * DV), BF16)],
        compiler_params=_params(("parallel", "parallel")),
    )(q, k, kv, o, do, lse, token)


TAIL_TM = 256


def tail_fwd(y, attn, proj, x2, tgt, gate, g_post, wco, wmo, wout, seq):
    t = y.shape[0]
    nb = t // seq
    tm = min(TAIL_TM, seq)
    tpb = seq // tm

    def body(y_ref, at_ref, p_ref, x_ref, t_ref, gate_ref, gp_ref, wco_ref, wmo_ref, wout_ref,
             o_ref, ya_ref, yb_ref, m_ref, do2_ref, dout_ref, dgate_ref, dgp_ref, loss_ref):
        i = pl.program_id(0)
        bz = p_ref[0].astype(F32)
        ga = p_ref[1].astype(F32)
        gb = p_ref[2].astype(F32)
        ov = (at_ref[...].astype(F32) * (bz * _sig(bz))).astype(BF16)
        o_ref[...] = ov
        ya = _dot(y_ref[...], wco_ref[...])
        yb = _dot(ov, wmo_ref[...])
        ya_ref[...] = ya.astype(BF16)
        yb_ref[...] = yb.astype(BF16)
        mv = (_sig(ga) * ya + _sig(gb) * yb).astype(BF16)
        m_ref[...] = mv
        o2 = _dot(mv, wout_ref[...])
        r = lax.rsqrt(jnp.mean(o2 * o2, axis=-1, keepdims=True) + EPS)
        nrm = o2 * r
        gp = gp_ref[...]
        gate_v = gate_ref[...]
        rn = nrm * gp
        err = x_ref[...] + gate_v * rn - t_ref[...]
        dout = err * (1.0 / D)
        dout_ref[...] = dout
        dn = dout * gate_v * gp
        do2_ref[...] = (r * (dn - nrm * jnp.mean(dn * nrm, axis=-1, keepdims=True))).astype(BF16)

        @pl.when(i % tpb == 0)
        def _():
            dgate_ref[...] = jnp.zeros_like(dgate_ref)

        @pl.when(i == 0)
        def _():
            dgp_ref[...] = jnp.zeros_like(dgp_ref)
            loss_ref[...] = jnp.zeros_like(loss_ref)

        dgate_ref[...] += jnp.sum(dout * rn, axis=0, keepdims=True)
        dgp_ref[...] += jnp.sum(dout * gate_v * nrm, axis=0, keepdims=True)
        loss_ref[...] += 0.5 * jnp.sum(jnp.mean(err * err, axis=-1, keepdims=True), axis=0, keepdims=True)

    row = pl.BlockSpec((tm, D), lambda i: (i, 0))
    per_batch = pl.BlockSpec((None, 1, D), lambda i: (i // tpb, 0, 0))
    vec = pl.BlockSpec((1, D), lambda i: (0, 0))
    wgt = pl.BlockSpec((D, D), lambda i: (0, 0))
    act = jax.ShapeDtypeStruct((t, D), BF16)
    return pl.pallas_call(
        body, name="tail_fwd", grid=(t // tm,),
        in_specs=[row, row, pl.BlockSpec((3, tm, D), lambda i: (0, i, 0)), row, row, per_batch, vec,
                  wgt, wgt, wgt],
        out_specs=[row, row, row, row, row, row, per_batch, vec, pl.BlockSpec((1, 1), lambda i: (0, 0))],
        out_shape=[act, act, act, act, act, jax.ShapeDtypeStruct((t, D), F32),
                   jax.ShapeDtypeStruct((nb, 1, D), F32), jax.ShapeDtypeStruct((1, D), F32),
                   jax.ShapeDtypeStruct((1, 1), F32)],
        compiler_params=_params(("arbitrary",)),
    )(y, attn, proj, x2, tgt, gate, g_post, wco, wmo, wout)


def tail_bwd(do2, proj, ya, yb, attn, wout, wmo, wco):
    t = do2.shape[0]
    tm = min(TAIL_TM, t)

    def body(do2_ref, p_ref, ya_ref, yb_ref, at_ref, wout_ref, wmo_ref, wco_ref,
             dp_ref, dya_ref, dyb_ref, dat_ref, dy_ref):
        bz = p_ref[0].astype(F32)
        ga = p_ref[1].astype(F32)
        gb = p_ref[2].astype(F32)
        dm = _dot_nt(do2_ref[...], wout_ref[...])
        sa = _sig(ga)
        sb = _sig(gb)
        dya = (dm * sa).astype(BF16)
        dyb = (dm * sb).astype(BF16)
        dya_ref[...] = dya
        dyb_ref[...] = dyb
        dp_ref[1] = (dm * ya_ref[...].astype(F32) * (sa * (1.0 - sa))).astype(BF16)
        dp_ref[2] = (dm * yb_ref[...].astype(F32) * (sb * (1.0 - sb))).astype(BF16)
        dov = _dot_nt(dyb, wmo_ref[...])
        sz = _sig(bz)
        dat_ref[...] = (dov * (bz * sz)).astype(BF16)
        dp_ref[0] = (dov * at_ref[...].astype(F32) * (sz * (1.0 + bz * (1.0 - sz)))).astype(BF16)
        dy_ref[...] = _dot_nt(dya, wco_ref[...]).astype(BF16)

    row = pl.BlockSpec((tm, D), lambda i: (i, 0))
    seg3 = pl.BlockSpec((3, tm, D), lambda i: (0, i, 0))
    wgt = pl.BlockSpec((D, D), lambda i: (0, 0))
    act = jax.ShapeDtypeStruct((t, D), BF16)
    return pl.pallas_call(
        body, name="tail_bwd", grid=(t // tm,),
        in_specs=[row, seg3, row, row, row, wgt, wgt, wgt],
        out_specs=[seg3, row, row, row, row],
        out_shape=[jax.ShapeDtypeStruct((NSEG, t, D), BF16), act, act, act, act],
        compiler_params=_params(("parallel",)),
    )(do2, proj, ya, yb, attn, wout, wmo, wco)


def adamw(w, m, v, g, g2, name, token=None):
    rows, cols = w.shape
    tr = rows
    for cand in (256, 128, 64, 32, 16, 8):
        if rows % cand == 0 and rows > cand:
            tr = cand
            break
    has2 = g2 is not None
    n_in = 4 + has2

    def body(*refs):
        w_ref, m_ref, v_ref, g_ref = refs[:4]
        go_ref, d_ref, mo_ref, vo_ref = refs[-4:]
        grad = g_ref[...] + refs[4][...].astype(F32) if has2 else g_ref[...]
        mn = ADAM_B1 * m_ref[...] + (1.0 - ADAM_B1) * grad
        vn = ADAM_B2 * v_ref[...] + (1.0 - ADAM_B2) * (grad * grad)
        m_hat = mn / (1.0 - ADAM_B1 ** ADAM_STEP)
        v_hat = vn / (1.0 - ADAM_B2 ** ADAM_STEP)
        go_ref[...] = grad
        d_ref[...] = -ADAM_LR * (m_hat / (jnp.sqrt(v_hat) + ADAM_EPS) + ADAM_WD * w_ref[...])
        mo_ref[...] = mn
        vo_ref[...] = vn

    blk = pl.BlockSpec((tr, cols), lambda i: (i, 0))
    ins = [w, m, v, g] + ([g2] if has2 else [])
    specs = [blk] * n_in
    if token is not None:
        ins.append(token)
        specs.append(pl.BlockSpec((8, 128), lambda i: (0, 0)))
    return pl.pallas_call(
        body, name=name, grid=(rows // tr,),
        in_specs=specs, out_specs=[blk] * 4,
        out_shape=[jax.ShapeDtypeStruct((rows, cols), F32)] * 4,
        compiler_params=_params(("parallel",)),
    )(*ins)


def adamw_scattered(w, m, v, own, land, me, tr, name, transpose=False):
    slot_rows = land.shape[1]
    cols = land.shape[2]
    rows = slot_rows if transpose else w.shape[0]
    per_slot = slot_rows // tr

    def body(me_ref, w_ref, m_ref, v_ref, own_ref, land_ref, go_ref, d_ref, mo_ref, vo_ref):
        grad = own_ref[...].astype(F32)
        for s in range(8):
            grad = grad + land_ref[s].astype(F32)
        if transpose:
            grad = grad.T
        mn = ADAM_B1 * m_ref[...] + (1.0 - ADAM_B1) * grad
        vn = ADAM_B2 * v_ref[...] + (1.0 - ADAM_B2) * (grad * grad)
        m_hat = mn / (1.0 - ADAM_B1 ** ADAM_STEP)
        v_hat = vn / (1.0 - ADAM_B2 ** ADAM_STEP)
        go_ref[...] = grad
        d_ref[...] = -ADAM_LR * (m_hat / (jnp.sqrt(v_hat) + ADAM_EPS) + ADAM_WD * w_ref[...])
        mo_ref[...] = mn
        vo_ref[...] = vn

    wblk = pl.BlockSpec(w.shape if transpose else (tr, w.shape[1]), lambda i, s: (i, 0))
    return pl.pallas_call(
        body, name=name,
        grid_spec=pltpu.PrefetchScalarGridSpec(
            num_scalar_prefetch=1, grid=(rows // tr,),
            in_specs=[wblk, wblk, wblk,
                      pl.BlockSpec((tr, cols), lambda i, s: (s[0] * per_slot + i, 0)),
                      pl.BlockSpec((8, tr, cols), lambda i, s: (0, i, 0))],
            out_specs=[wblk] * 4),
        out_shape=[jax.ShapeDtypeStruct(w.shape, F32)] * 4,
        compiler_params=_params(),
    )(me, w, m, v, own, land)


def adamw_win(wt, mt, vt, ka, ra, kb, rb):
    rows = wt.shape[0]
    tc = 256
    nh = (D // 2) // tc

    def body(w_ref, m_ref, v_ref, ka_ref, ra_ref, kb_ref, rb_ref, go_ref, d_ref, mo_ref, vo_ref):
        first = pl.program_id(0) < nh
        grad = jnp.where(first, ka_ref[...] + ra_ref[...].astype(F32), kb_ref[...] + rb_ref[...].astype(F32))
        mn = ADAM_B1 * m_ref[...] + (1.0 - ADAM_B1) * grad
        vn = ADAM_B2 * v_ref[...] + (1.0 - ADAM_B2) * (grad * grad)
        m_hat = mn / (1.0 - ADAM_B1 ** ADAM_STEP)
        v_hat = vn / (1.0 - ADAM_B2 ** ADAM_STEP)
        go_ref[...] = grad
        d_ref[...] = -ADAM_LR * (m_hat / (jnp.sqrt(v_hat) + ADAM_EPS) + ADAM_WD * w_ref[...])
        mo_ref[...] = mn
        vo_ref[...] = vn

    blk = pl.BlockSpec((rows, tc), lambda j: (0, j))
    lo = pl.BlockSpec((rows, tc), lambda j: (0, jnp.minimum(j, nh - 1)))
    hi = pl.BlockSpec((rows, tc), lambda j: (0, jnp.maximum(j - nh, 0)))
    return pl.pallas_call(
        body, name="adamw_w_in", grid=(D // tc,),
        in_specs=[blk, blk, blk, lo, lo, hi, hi], out_specs=[blk] * 4,
        out_shape=[jax.ShapeDtypeStruct((rows, D), F32)] * 4,
        compiler_params=_params(("parallel",)),
    )(wt, mt, vt, ka, ra, kb, rb)


_ORD_A = ("x", "y", "c")
_ORD_B = ("y", "x", "c")


def _to_slots(full, order, col_sharded):
    if col_sharded:
        r = full.shape[0]
        cc = full.shape[1] // 8
        g = full.reshape(r, 2, 2, 2, cc).transpose(1, 2, 3, 0, 4)
    else:
        r = full.shape[0] // 8
        cc = full.shape[1]
        g = full.reshape(2, 2, 2, r, cc)
    names = ("x", "y", "c")
    perm = tuple(names.index(a) for a in order)
    return g.transpose(perm + (3, 4))


def _rows128(a, rows):
    flat = a.reshape(-1)
    return jnp.pad(flat, (0, rows * 128 - flat.shape[0])).reshape(rows, 128)


def kernel(x, c, positions, w_ada, b_ada, g_pre, w_in, conv_w, w_conv_out, g_q, w_uq, g_kv, w_ukv, w_mla_out, w_out, g_post, loss_target, m_w_ada, m_b_ada, m_g_pre, m_w_in, m_conv_w, m_w_conv_out, m_g_q, m_w_uq, m_g_kv, m_w_ukv, m_w_mla_out, m_w_out, m_g_post, v_w_ada, v_b_ada, v_g_pre, v_w_in, v_conv_w, v_w_conv_out, v_g_q, v_w_uq, v_g_kv, v_w_ukv, v_w_mla_out, v_w_out, v_g_post):
    nb, seq, _ = x.shape
    t = nb * seq
    mx, my, mc = lax.axis_index("x"), lax.axis_index("y"), lax.axis_index("c")
    me = 4 * mx + 2 * my + mc
    co = {"x": mx, "y": my, "c": mc}

    x2 = x.reshape(t, D)
    tgt2 = loss_target.reshape(t, D)
    pos2 = positions.reshape(t, 1)

    packed = jnp.concatenate([c.reshape(2 * D // 128, 128), _rows128(conv_w[0], 8)], axis=0)
    gath = small_allgather(packed, "gather_cond")
    c_all = gath[:, :16].reshape(8 * nb, D)
    conv_full = gath[:, 16:19].reshape(8, 3, 128).transpose(1, 0, 2).reshape(3, D)
    conv_full8 = jnp.pad(conv_full, ((0, 5), (0, 0)))
    ada_cols = w_ada.shape[2]
    b_cols = lax.dynamic_slice(b_ada, (0, me * ada_cols), (1, ada_cols))
    mod_part = ada_fwd(c_all, w_ada[0], b_cols)
    mod_g = small_allgather(mod_part.reshape(8 * nb * ada_cols // 128, 128), "gather_mod")
    mod_all = mod_g.reshape(8, 8 * nb, ada_cols).transpose(1, 0, 2).reshape(8 * nb, 8 * ada_cols)
    mod = lax.dynamic_slice(mod_all, (me * nb, 0), (nb, 3 * D))
    shift = mod[:, 0:D].reshape(nb, 1, D)
    scale = mod[:, D:2 * D].reshape(nb, 1, D)
    gate = mod[:, 2 * D:3 * D].reshape(nb, 1, D)

    wt = w_in[0].T.astype(BF16)
    lo = lax.bitcast_convert_type(wt[:, :D // 2], jnp.uint16).astype(jnp.uint32)
    hi = lax.bitcast_convert_type(wt[:, D // 2:], jnp.uint16).astype(jnp.uint32)
    wt_bits = lax.bitcast_convert_type(lo | (hi << 16), F32)
    q4 = D // 4
    r3rd = wt_bits.shape[0] // 3
    plan = [(0, (k * r3rd, r3rd), (g * q4, q4), (_ORD_A, _ORD_B)[g]) for k in range(3) for g in range(2)]
    gw = allgather_big([wt_bits], plan, "gather_w_in")
    late = [w_conv_out[0].astype(BF16), w_mla_out[0].astype(BF16), w_out[0].astype(BF16),
            jnp.pad(w_uq[0].T.astype(BF16), ((0, DQK - 192), (0, 0))), w_ukv[0].T.astype(BF16)]
    gw0, late = lax.optimization_barrier((gw[0], late))
    late_state, late_token = gather_start(late, "gather_late_start")
    wt_bits_all = gw0.reshape(N_IN, D // 2)

    inv_freq = ROPE_THETA ** (-jnp.arange(0, ROPE, 2, dtype=F32) / ROPE)
    invf = jnp.concatenate([inv_freq, inv_freq, jnp.zeros((128 - ROPE,), F32)]).reshape(1, 128)
    lane = np.arange(128)
    tabs = (invf,
            jnp.asarray(np.where(lane < HALF, -1.0, 0.0).reshape(1, 128), F32),
            jnp.asarray(np.where((lane >= HALF) & (lane < ROPE), 1.0, 0.0).reshape(1, 128), F32))

    h = prenorm_fwd(x2, scale, shift, g_pre, seq)
    proj, wt_p = proj_matmul(h, wt_bits_all, late_token)
    y = conv_fwd(proj, conv_full8, seq)
    gl = gather_wait(late_state, y, "gather_late_wait")
    wco = gl[0].reshape(D, D)
    wmo = gl[1].reshape(D, D)
    wout = gl[2].reshape(D, D)
    wuq_p = gl[3].reshape(H * DQK, QL)
    wukv = gl[4].reshape(H * 256, KVL)
    q_rot, k_cat, kv, qn, kvn = mla_prep_fwd(proj, pos2, g_q, g_kv, wuq_p, wukv, tabs)
    attn, lse = flash_fwd(q_rot, k_cat, kv, nb, seq)
    o, ya, yb, m, do2, dout, dgate, dg_post, loss_part = tail_fwd(
        y, attn, proj, x2, tgt2, gate, g_post, wco, wmo, wout, seq)

    dproj, dya, dyb, dattn, dy = tail_bwd(do2, proj, ya, yb, attn, wout, wmo, wco)
    g_wout = grad_matmul(m, do2, "grad_w_out")
    g_wmo = grad_matmul(o, dyb, "grad_w_mla_out")
    g_wco = grad_matmul(y, dya, "grad_w_conv_out")
    sc1, sc1_tok = scatter_start([g_wco, g_wmo, g_wout], "scatter_out_grads_start")
    dproj, dconv = conv_bwd(dproj, proj, dy, conv_full8, seq)
    dq_rot, dk, dv = flash_bwd(q_rot, k_cat, kv, attn, dattn, lse, nb, seq, sc1_tok)
    dproj, dq, dkv, dg_q, dg_kv = mla_prep_bwd(dproj, proj, dq_rot, dk, dv, pos2, g_q, g_kv, wuq_p, wukv, tabs)
    g_wuq_t = grad_matmul(dq, qn, "grad_w_uq")
    g_wukv_t = grad_matmul(dkv, kvn, "grad_w_ukv")
    sc2, sc2_tok = scatter_start([g_wuq_t, g_wukv_t], "scatter_mla_grads_start")
    g_win_p = win_grad_matmul(h, dproj, sc2_tok)

    g_wt = g_win_p.reshape(2, 2, 2, N_IN // 8, D)
    ords = [("c", "y", "x"), ("c", "x", "y")]
    hc = D // 2
    win_shape = (2, 2, N_IN // 8, hc)
    pick_w = lambda col: (lambda ref, cc: ref.at[:, :, 1 - cc["c"], :, pl.ds(col * hc, hc)])
    which1 = [0, 0]
    picks1 = [pick_w(0), pick_w(1)]
    st1, tok1 = swap_start([g_wt], which1, ["c"] * 2, picks1, [win_shape] * 2, "rs_c_start")
    assert nb == 2
    dh0 = dh_matmul(dproj, wt_p, tok1, seq, 0)
    (g_wt,), r1 = swap_wait(st1, dh0, which1, ["c"] * 2, picks1, "rs_c_wait")
    sel_xyc = jnp.stack([mx, my, mc]).astype(jnp.int32)
    sel2 = [jnp.stack([co[o[2]]]).astype(jnp.int32) for o in ords]
    first = [rs_win_add_first(g_wt, r1[0], sel_xyc, 1, 0, "rs_add_first_0"),
             rs_win_add_first(g_wt, r1[1], sel_xyc, 0, 1, "rs_add_first_1")]
    keep1, send1 = zip(*first)
    all4 = [0, 1]
    none4 = [None] * 2
    axes2 = [o[1] for o in ords]
    st2, tok2 = swap_start(list(send1), all4, axes2, none4, [s.shape for s in send1], "rs_ici1_start")

    dh1 = dh_matmul(dproj, wt_p, tok2, seq, 1)
    gx0, dsh0, dsc0, dgp0 = prenorm_bwd(dh0, x2, dout, scale, g_pre, seq, tok2, 0, None)
    _, r2 = swap_wait(st2, (gx0, dh1), all4, axes2, none4, "rs_ici1_wait")
    keep2, send2 = zip(*[rs_add_second(keep1[a], r2[a], sel2[a], "rs_add_second_%d" % a) for a in range(2)])
    axes3 = [o[2] for o in ords]
    st3, tok3 = swap_start(list(send2), all4, axes3, none4, [s.shape for s in send2], "rs_ici2_start")
    grad_x2, dsh1, dsc1, dgp1 = prenorm_bwd(dh1, x2, dout, scale, g_pre, seq, tok3, 1, gx0)
    dshift = jnp.stack([dsh0, dsh1])
    dscale = jnp.stack([dsc0, dsc1])
    dg_pre = dgp0 + dgp1

    dmod = jnp.concatenate([dshift, dscale, dgate], axis=2).reshape(nb * 3 * D // 128, 128)
    small = jnp.concatenate([
        dmod, _rows128(dg_pre, 8), _rows128(dg_post, 8), _rows128(dg_q, 8), _rows128(dg_kv, 8),
        dconv[0:3].reshape(24, 128), _rows128(loss_part, 8)], axis=0)
    small_g = small_allgather(small, "gather_small_grads")
    sums = slot_sum(small_g)
    dmod_all = small_g[:, 0:48].reshape(8 * nb, 3 * D)
    g_bada = (sums[0:24] + sums[24:48]).reshape(1, 3 * D)
    g_gpre = sums[48:56].reshape(1, D)
    g_gpost = sums[56:64].reshape(1, D)
    g_gq = sums[64:67].reshape(1, QL)
    g_gkv = sums[72:74].reshape(1, KVL)
    g_conv_full = sums[80:104].reshape(3, D)
    loss = sums[104, 0]
    g_conv = lax.dynamic_slice(g_conv_full, (0, me * 128), (3, 128))
    dmod_cols = lax.dynamic_slice(dmod_all, (0, me * ada_cols), (8 * nb, ada_cols))
    g_wada = ada_bwd(c_all, dmod_cols)

    res = {}
    res["w_ada"] = [o_[None] for o_ in adamw(w_ada[0], m_w_ada[0], v_w_ada[0], g_wada, None, "adamw_w_ada", tok3)]

    def pack(b_, gp_, gpo_, gq_, gkv_, cw_):
        return jnp.concatenate([_rows128(b_, 24), _rows128(gp_, 8), _rows128(gpo_, 8), _rows128(gq_, 8),
                                _rows128(gkv_, 8), _rows128(cw_, 8)], axis=0)

    sw = pack(b_ada, g_pre, g_post, g_q, g_kv, conv_w)
    sm = pack(m_b_ada, m_g_pre, m_g_post, m_g_q, m_g_kv, m_conv_w)
    sv = pack(v_b_ada, v_g_pre, v_g_post, v_g_q, v_g_kv, v_conv_w)
    sg = pack(g_bada, g_gpre, g_gpost, g_gq, g_gkv, g_conv)
    small_out = adamw(sw, sm, sv, sg, None, "adamw_small", tok3)

    _, r3 = swap_wait(st3, small_out[0], all4, axes3, none4, "rs_ici2_wait")

    (g_wco, g_wmo, g_wout), (l_wco, l_wmo, l_wout) = scatter_wait(sc1, small_out[1], "scatter_out_grads_wait")
    (g_wuq_t, g_wukv_t), (l_wuq, l_wukv) = scatter_wait(sc2, small_out[2], "scatter_mla_grads_wait")

    res["w_in"] = [o_.T[None] for o_ in adamw_win(w_in[0].T, m_w_in[0].T, v_w_in[0].T,
                                                  keep2[0], r3[0], keep2[1], r3[1])]
    me1 = me.reshape(1).astype(jnp.int32)
    res["w_uq"] = [o_.T[None] for o_ in adamw_scattered(
        w_uq[0].T, m_w_uq[0].T, v_w_uq[0].T, g_wuq_t, l_wuq, me1, 64, "adamw_w_uq")]
    res["w_ukv"] = [o_[None] for o_ in adamw_scattered(
        w_ukv[0], m_w_ukv[0], v_w_ukv[0], g_wukv_t, l_wukv, me1, KVL, "adamw_w_ukv", transpose=True)]
    for nm, wv, mv, vv, gg, ll in (("w_conv_out", w_conv_out, m_w_conv_out, v_w_conv_out, g_wco, l_wco),
                                   ("w_mla_out", w_mla_out, m_w_mla_out, v_w_mla_out, g_wmo, l_wmo),
                                   ("w_out", w_out, m_w_out, v_w_out, g_wout, l_wout)):
        res[nm] = [o_[None] for o_ in adamw_scattered(wv[0], mv[0], vv[0], gg, ll, me1, 128, "adamw_" + nm)]

    def unpack(a):
        return {"b_ada": a[0:24].reshape(1, 3 * D), "g_pre": a[24:32].reshape(1, D),
                "g_post": a[32:40].reshape(1, D), "g_q": a[40:43].reshape(1, QL),
                "g_kv": a[48:50].reshape(1, KVL), "conv_w": a[56:59].reshape(-1)[:3 * 128].reshape(1, 3, 128)}

    for nm in ("b_ada", "g_pre", "g_post", "g_q", "g_kv", "conv_w"):
        res[nm] = [unpack(a)[nm] for a in small_out]

    order = ["w_ada", "b_ada", "g_pre", "w_in", "conv_w", "w_conv_out", "g_q", "w_uq", "g_kv", "w_ukv",
             "w_mla_out", "w_out", "g_post"]
    out = [loss, grad_x2.reshape(nb, seq, D)]
    for k_ in range(4):
        out += [res[nm][k_] for nm in order]
    return tuple(out)
```

```python
import functools

import numpy as np
import jax
import jax.numpy as jnp
from jax import lax
from jax.experimental import pallas as pl
from jax.experimental.pallas import tpu as pltpu

F32 = jnp.float32
BF16 = jnp.bfloat16
MESH = pl.DeviceIdType.MESH

D = 1024
H = 8
QL = 384
KVL = 256
ROPE = 64
HALF = ROPE // 2
DQK = 256
DV = 128
NSEG = 8
NP = NSEG * D
EPS = 1e-6
ROPE_THETA = 10000.0
SM_SCALE = (128 + ROPE) ** -0.5
LOG2E = 1.4426950408889634
LN2 = 0.6931471805599453
FLASH_TQ = 512

SEG_BZ, SEG_GA, SEG_GB, SEG_LAT, SEG_V = 0, 1, 2, 3, 4

ADAM_LR = 0.001
ADAM_B1 = 0.9
ADAM_B2 = 0.999
ADAM_EPS = 1e-08
ADAM_WD = 0.01
ADAM_STEP = 10

VMEM_LIMIT = 56 * 1024 * 1024


def _params(sem=None, vmem=VMEM_LIMIT):
    kw = dict(vmem_limit_bytes=vmem)
    if sem is not None:
        kw["dimension_semantics"] = sem
    return pltpu.CompilerParams(**kw)


def _sig(v):
    return 1.0 / (1.0 + jnp.exp(-v))


def _dot(a, b):
    return jnp.dot(a, b, preferred_element_type=F32)


def _dot_nt(a, b):
    return lax.dot_general(a, b, (((1,), (1,)), ((), ())), preferred_element_type=F32)


def _dot_tn(a, b):
    return lax.dot_general(a, b, (((0,), (0,)), ((), ())), preferred_element_type=F32)


_AXIS_POS = {"x": 0, "y": 1, "c": 2}


def _coords():
    return lax.axis_index("x"), lax.axis_index("y"), lax.axis_index("c")


def _partner(axis):
    p = list(_coords())
    p[_AXIS_POS[axis]] = 1 - p[_AXIS_POS[axis]]
    return tuple(p)


def small_allgather(v, name):
    rows = v.shape[0]

    def body(v_ref, out_ref, send_sems, recv_sems):
        x, y, c = _coords()
        me = 4 * x + 2 * y + c
        out_ref[me] = v_ref[...]
        copies = []
        for k in range(1, 8):
            peer = (1 - x if k & 4 else x, 1 - y if k & 2 else y, 1 - c if k & 1 else c)
            cp = pltpu.make_async_remote_copy(
                src_ref=v_ref, dst_ref=out_ref.at[me],
                send_sem=send_sems.at[k - 1], recv_sem=recv_sems.at[k - 1],
                device_id=peer, device_id_type=MESH)
            cp.start()
            copies.append(cp)
        for cp in copies:
            cp.wait()

    return pl.pallas_call(
        body, name=name,
        out_shape=jax.ShapeDtypeStruct((8, rows, 128), F32),
        in_specs=[pl.BlockSpec(memory_space=pltpu.VMEM)],
        out_specs=pl.BlockSpec(memory_space=pltpu.VMEM),
        scratch_shapes=[pltpu.SemaphoreType.DMA((7,)), pltpu.SemaphoreType.DMA((7,))],
    )(v)


def _own_block_placed(s):
    x, y, c = _coords()
    return lax.dynamic_update_slice(lax.empty((2, 2, 2) + s.shape, s.dtype), s[None, None, None],
                                    (x, y, c) + (0,) * s.ndim)


def allgather_big(arrs, plan, name):
    n = len(arrs)
    m = len(plan)

    def body(*refs):
        ins, outs = refs[n:2 * n], refs[2 * n:3 * n]
        send_sems, recv_sems = refs[3 * n:]
        x, y, c = _coords()
        co = {"x": x, "y": y, "c": c}

        def window(ref, lead, rows, cols):
            win = tuple(slice(None) if w is None else pl.ds(w[0], w[1]) for w in (rows, cols))
            return ref.at[tuple(lead) + win]

        def held(e, free):
            i, rows, cols, _ = plan[e]
            lead = [slice(None) if ax in free else co[ax] for ax in ("x", "y", "c")]
            return window(outs[i], lead, rows, cols)

        def rcopy(e, stage, src, dst, axis):
            return pltpu.make_async_remote_copy(
                src_ref=src, dst_ref=dst,
                send_sem=send_sems.at[e, stage], recv_sem=recv_sems.at[e, stage],
                device_id=_partner(axis), device_id_type=MESH)

        stages = [[], [], []]
        for e, (i, rows, cols, order) in enumerate(plan):
            cp = rcopy(e, 0, window(ins[i], [], rows, cols), held(e, ()), order[0])
            cp.start()
            stages[0].append(cp)
        for s in (1, 2):
            for e, (i, rows, cols, order) in enumerate(plan):
                stages[s - 1][e].wait_recv()
                blk = held(e, order[:s])
                cp = rcopy(e, s, blk, blk, order[s])
                cp.start()
                stages[s].append(cp)
        for e in range(m):
            stages[2][e].wait_recv()
        for e in range(m):
            for s in range(3):
                stages[s][e].wait_send()

    any_spec = pl.BlockSpec(memory_space=pl.ANY)
    lands = [_own_block_placed(a) for a in arrs]
    return pl.pallas_call(
        body, name=name,
        out_shape=[jax.ShapeDtypeStruct(l.shape, l.dtype) for l in lands],
        in_specs=[any_spec] * (2 * n),
        out_specs=[any_spec] * n,
        input_output_aliases={i: i for i in range(n)},
        scratch_shapes=[pltpu.SemaphoreType.DMA((m, 3)), pltpu.SemaphoreType.DMA((m, 3))],
    )(*lands, *arrs)


def exchange(arrs, axes, picks, out_shapes, name):
    n = len(arrs)

    def body(*refs):
        ins, outs = refs[:n], refs[n:2 * n]
        send_sems, recv_sems = refs[2 * n:]
        x, y, c = _coords()
        co = {"x": x, "y": y, "c": c}
        copies = []
        for a in range(n):
            src = ins[a] if picks[a] is None else picks[a](ins[a], co)
            cp = pltpu.make_async_remote_copy(
                src_ref=src, dst_ref=outs[a],
                send_sem=send_sems.at[a], recv_sem=recv_sems.at[a],
                device_id=_partner(axes[a]), device_id_type=MESH)
            cp.start()
            copies.append(cp)
        for cp in copies:
            cp.wait()

    any_spec = pl.BlockSpec(memory_space=pl.ANY)
    return pl.pallas_call(
        body, name=name,
        out_shape=[jax.ShapeDtypeStruct(s, a.dtype) for s, a in zip(out_shapes, arrs)],
        in_specs=[any_spec] * n,
        out_specs=[any_spec] * n,
        scratch_shapes=[pltpu.SemaphoreType.DMA((n,)), pltpu.SemaphoreType.DMA((n,))],
    )(*arrs)


_HBM = pl.BlockSpec(memory_space=pltpu.HBM)
_SEM = pl.BlockSpec(memory_space=pltpu.SEMAPHORE)


def _swap_copies(srcs, lands, send_sems, recv_sems, axes, picks):
    x, y, c = _coords()
    co = {"x": x, "y": y, "c": c}
    return [pltpu.make_async_remote_copy(
        src_ref=srcs[a] if picks[a] is None else picks[a](srcs[a], co), dst_ref=lands[a],
        send_sem=send_sems.at[a], recv_sem=recv_sems.at[a],
        device_id=_partner(axes[a]), device_id_type=MESH) for a in range(len(srcs))]


def swap_start(arrs, which, axes, picks, out_shapes, name):
    ns, n = len(arrs), len(which)

    def body(*refs):
        srcs, lands = refs[:ns], refs[ns:ns + n]
        send_sems, recv_sems = refs[ns + n:ns + n + 2]
        token = refs[-1]
        for cp in _swap_copies([srcs[i] for i in which], lands, send_sems, recv_sems, axes, picks):
            cp.start()
        token[...] = jnp.zeros_like(token)

    lands = [lax.empty(s, arrs[i].dtype) for s, i in zip(out_shapes, which)]
    ops = [pltpu.with_memory_space_constraint(a, pltpu.HBM) for a in list(arrs) + lands]
    out = pl.pallas_call(
        body, name=name,
        out_shape=[pltpu.SemaphoreType.DMA((n,)), pltpu.SemaphoreType.DMA((n,))]
        + [pltpu.HBM(o.shape, o.dtype) for o in ops] + [jax.ShapeDtypeStruct((8, 128), F32)],
        in_specs=[_HBM] * (ns + n),
        out_specs=[_SEM, _SEM] + [_HBM] * (ns + n) + [pl.BlockSpec(memory_space=pltpu.VMEM)],
        input_output_aliases={i: 2 + i for i in range(ns + n)},
        compiler_params=pltpu.CompilerParams(has_side_effects=pltpu.SideEffectType.DATAFLOW_SIDE_EFFECTING),
    )(*ops)
    return out[:-1], out[-1]


def swap_wait(state, after, which, axes, picks, name):
    n = len(which)
    ns = len(state) - 2 - n

    def body(*refs):
        srcs, lands = refs[:ns], refs[ns:ns + n]
        send_sems, recv_sems = refs[ns + n:ns + n + 2]
        for cp in _swap_copies([srcs[i] for i in which], lands, send_sems, recv_sems, axes, picks):
            cp.wait_send()
            cp.wait_recv()

    thru = list(state[2:])
    after = list(after) if isinstance(after, (list, tuple)) else [after]
    out = pl.pallas_call(
        body, name=name,
        out_shape=[pltpu.HBM(o.shape, o.dtype) for o in thru],
        in_specs=[_HBM] * (ns + n) + [_SEM, _SEM] + [pl.BlockSpec(memory_space=pl.ANY)] * len(after),
        out_specs=[_HBM] * (ns + n),
        input_output_aliases={i: i for i in range(ns + n)},
        compiler_params=pltpu.CompilerParams(has_side_effects=pltpu.SideEffectType.DATAFLOW_SIDE_EFFECTING),
    )(*thru, state[0], state[1], *after)
    return out[:ns], out[ns:]


def _gather_copies(shards, lands, send_sems, recv_sems):
    x, y, c = _coords()
    copies = []
    for a in range(len(shards)):
        for k in range(1, 8):
            peer = (1 - x if k & 4 else x, 1 - y if k & 2 else y, 1 - c if k & 1 else c)
            copies.append(pltpu.make_async_remote_copy(
                src_ref=shards[a], dst_ref=lands[a].at[x, y, c],
                send_sem=send_sems.at[7 * a + k - 1], recv_sem=recv_sems.at[7 * a + k - 1],
                device_id=peer, device_id_type=MESH))
    return copies


def gather_start(shards, name):
    n = len(shards)
    x, y, c = _coords()

    def body(*refs):
        srcs, lands = refs[:n], refs[n:2 * n]
        send_sems, recv_sems = refs[2 * n:2 * n + 2]
        token = refs[-1]
        for cp in _gather_copies(srcs, lands, send_sems, recv_sems):
            cp.start()
        token[...] = jnp.zeros_like(token)

    lands = [_own_block_placed(s) for s in shards]
    ops = [pltpu.with_memory_space_constraint(a, pltpu.HBM) for a in list(shards) + lands]
    out = pl.pallas_call(
        body, name=name,
        out_shape=[pltpu.SemaphoreType.DMA((7 * n,)), pltpu.SemaphoreType.DMA((7 * n,))]
        + [pltpu.HBM(o.shape, o.dtype) for o in ops] + [jax.ShapeDtypeStruct((8, 128), F32)],
        in_specs=[_HBM] * (2 * n),
        out_specs=[_SEM, _SEM] + [_HBM] * (2 * n) + [pl.BlockSpec(memory_space=pltpu.VMEM)],
        input_output_aliases={i: 2 + i for i in range(2 * n)},
        compiler_params=pltpu.CompilerParams(has_side_effects=pltpu.SideEffectType.DATAFLOW_SIDE_EFFECTING),
    )(*ops)
    return out[:-1], out[-1]


def gather_wait(state, after, name):
    n = (len(state) - 2) // 2

    def body(*refs):
        srcs, lands = refs[:n], refs[n:2 * n]
        send_sems, recv_sems = refs[2 * n:2 * n + 2]
        for cp in _gather_copies(srcs, lands, send_sems, recv_sems):
            cp.wait_send()
            cp.wait_recv()

    thru = list(state[2:])
    out = pl.pallas_call(
        body, name=name,
        out_shape=[pltpu.HBM(o.shape, o.dtype) for o in thru],
        in_specs=[_HBM] * (2 * n) + [_SEM, _SEM, pl.BlockSpec(memory_space=pl.ANY)],
        out_specs=[_HBM] * (2 * n),
        input_output_aliases={i: i for i in range(2 * n)},
        compiler_params=pltpu.CompilerParams(has_side_effects=pltpu.SideEffectType.DATAFLOW_SIDE_EFFECTING),
    )(*thru, state[0], state[1], after)
    return out[n:]


def _scatter_copies(grads, lands, send_sems, recv_sems):
    x, y, c = _coords()
    me = 4 * x + 2 * y + c
    copies = []
    for a in range(len(grads)):
        r = grads[a].shape[0] // 8
        for k in range(1, 8):
            px, py, pc = (1 - x if k & 4 else x, 1 - y if k & 2 else y, 1 - c if k & 1 else c)
            rows = pl.ds(pl.multiple_of((4 * px + 2 * py + pc) * r, r), r)
            copies.append(pltpu.make_async_remote_copy(
                src_ref=grads[a].at[rows], dst_ref=lands[a].at[me],
                send_sem=send_sems.at[7 * a + k - 1], recv_sem=recv_sems.at[7 * a + k - 1],
                device_id=(px, py, pc), device_id_type=MESH))
    return copies


def scatter_start(grads, name):
    n = len(grads)

    def body(*refs):
        srcs, lands = refs[:n], refs[n:2 * n]
        send_sems, recv_sems = refs[2 * n:2 * n + 2]
        token = refs[-1]
        for cp in _scatter_copies(srcs, lands, send_sems, recv_sems):
            cp.start()
        token[...] = jnp.zeros_like(token)

    lands = [jnp.zeros((8, g.shape[0] // 8, g.shape[1]), g.dtype) for g in grads]
    ops = [pltpu.with_memory_space_constraint(a, pltpu.HBM) for a in list(grads) + lands]
    out = pl.pallas_call(
        body, name=name,
        out_shape=[pltpu.SemaphoreType.DMA((7 * n,)), pltpu.SemaphoreType.DMA((7 * n,))]
        + [pltpu.HBM(o.shape, o.dtype) for o in ops] + [jax.ShapeDtypeStruct((8, 128), F32)],
        in_specs=[_HBM] * (2 * n),
        out_specs=[_SEM, _SEM] + [_HBM] * (2 * n) + [pl.BlockSpec(memory_space=pltpu.VMEM)],
        input_output_aliases={i: 2 + i for i in range(2 * n)},
        compiler_params=pltpu.CompilerParams(has_side_effects=pltpu.SideEffectType.DATAFLOW_SIDE_EFFECTING),
    )(*ops)
    return out[:-1], out[-1]


def scatter_wait(state, after, name):
    n = (len(state) - 2) // 2

    def body(*refs):
        srcs, lands = refs[:n], refs[n:2 * n]
        send_sems, recv_sems = refs[2 * n:2 * n + 2]
        for cp in _scatter_copies(srcs, lands, send_sems, recv_sems):
            cp.wait_send()
            cp.wait_recv()

    thru = list(state[2:])
    after = list(after) if isinstance(after, (list, tuple)) else [after]
    out = pl.pallas_call(
        body, name=name,
        out_shape=[pltpu.HBM(o.shape, o.dtype) for o in thru],
        in_specs=[_HBM] * (2 * n) + [_SEM, _SEM] + [pl.BlockSpec(memory_space=pl.ANY)] * len(after),
        out_specs=[_HBM] * (2 * n),
        input_output_aliases={i: i for i in range(2 * n)},
        compiler_params=pltpu.CompilerParams(has_side_effects=pltpu.SideEffectType.DATAFLOW_SIDE_EFFECTING),
    )(*thru, state[0], state[1], *after)
    return out[:n], out[n:]


def rs_win_add_first(g, r, sel, next_dim, col, name):
    rows, cols = r.shape[2:]

    def body(sel_ref, gk_ref, rk_ref, gs_ref, rs_ref, keep_ref, send_ref):
        keep_ref[...] = gk_ref[...] + rk_ref[...]
        send_ref[...] = (gs_ref[...] + rs_ref[...]).astype(BF16)

    def g_map(flip):
        def f(j, s):
            nxt = 1 - s[next_dim] if flip else s[next_dim]
            return (nxt, j, s[2], 0, col) if next_dim == 0 else (j, nxt, s[2], 0, col)
        return f

    def r_map(flip):
        def f(j, s):
            nxt = 1 - s[next_dim] if flip else s[next_dim]
            return (nxt, j, 0, 0) if next_dim == 0 else (j, nxt, 0, 0)
        return f

    gblk = (None, None, None, rows, cols)
    rblk = (None, None, rows, cols)
    oblk = (None, rows, cols)
    return pl.pallas_call(
        body, name=name,
        grid_spec=pltpu.PrefetchScalarGridSpec(
            num_scalar_prefetch=1, grid=(2,),
            in_specs=[pl.BlockSpec(gblk, g_map(False)), pl.BlockSpec(rblk, r_map(False)),
                      pl.BlockSpec(gblk, g_map(True)), pl.BlockSpec(rblk, r_map(True))],
            out_specs=[pl.BlockSpec(oblk, lambda j, s: (j, 0, 0)),
                       pl.BlockSpec(oblk, lambda j, s: (j, 0, 0))]),
        out_shape=[jax.ShapeDtypeStruct((2, rows, cols), F32),
                   jax.ShapeDtypeStruct((2, rows, cols), BF16)],
        compiler_params=_params(),
    )(sel, g, r, g, r)


def rs_add_first(g, r, sel, name):
    _, _, _, rows, cols = g.shape
    tr = rows // 2

    def body(sel_ref, gk_ref, rk_ref, gs_ref, rs_ref, keep_ref, send_ref):
        keep_ref[...] = gk_ref[...] + rk_ref[...]
        send_ref[...] = (gs_ref[...] + rs_ref[...]).astype(BF16)

    blk = (None, None, None, tr, cols)
    rblk = (None, None, tr, cols)
    oblk = (None, tr, cols)
    return pl.pallas_call(
        body, name=name,
        grid_spec=pltpu.PrefetchScalarGridSpec(
            num_scalar_prefetch=1, grid=(2, 2),
            in_specs=[
                pl.BlockSpec(blk, lambda j, i, s: (s[0], s[1], j, i, 0)),
                pl.BlockSpec(rblk, lambda j, i, s: (s[1], j, i, 0)),
                pl.BlockSpec(blk, lambda j, i, s: (s[0], 1 - s[1], j, i, 0)),
                pl.BlockSpec(rblk, lambda j, i, s: (1 - s[1], j, i, 0)),
            ],
            out_specs=[pl.BlockSpec(oblk, lambda j, i, s: (j, i, 0)),
                       pl.BlockSpec(oblk, lambda j, i, s: (j, i, 0))]),
        out_shape=[jax.ShapeDtypeStruct((2, rows, cols), F32),
                   jax.ShapeDtypeStruct((2, rows, cols), BF16)],
        compiler_params=_params(),
    )(sel, g, r, g, r)


def rs_add_second(k, r, sel, name):
    _, rows, cols = k.shape
    tr = rows // 2 if rows % 32 == 0 else rows
    nt = rows // tr

    def body(sel_ref, kk_ref, rk_ref, ks_ref, rs_ref, keep_ref, send_ref):
        keep_ref[...] = kk_ref[...] + rk_ref[...].astype(F32)
        send_ref[...] = (ks_ref[...] + rs_ref[...].astype(F32)).astype(BF16)

    blk = (None, tr, cols)
    oblk = (tr, cols)
    return pl.pallas_call(
        body, name=name,
        grid_spec=pltpu.PrefetchScalarGridSpec(
            num_scalar_prefetch=1, grid=(nt,),
            in_specs=[
                pl.BlockSpec(blk, lambda i, s: (s[0], i, 0)),
                pl.BlockSpec(blk, lambda i, s: (s[0], i, 0)),
                pl.BlockSpec(blk, lambda i, s: (1 - s[0], i, 0)),
                pl.BlockSpec(blk, lambda i, s: (1 - s[0], i, 0)),
            ],
            out_specs=[pl.BlockSpec(oblk, lambda i, s: (i, 0)),
                       pl.BlockSpec(oblk, lambda i, s: (i, 0))]),
        out_shape=[jax.ShapeDtypeStruct((rows, cols), F32),
                   jax.ShapeDtypeStruct((rows, cols), BF16)],
        compiler_params=_params(),
    )(sel, k, r, k, r)


SEG_ROWS = (4800, 5824, 6848, 4096, 0, 1024, 2048, 3072)
LAT_ROWS = QL + KVL + ROPE
N_IN = 7872


def _seg_row(j):
    return pl.multiple_of(jnp.where(j < 3, 4800 + 1024 * j, jnp.where(j == 3, 4096, (j - 4) * 1024)), 8)


def proj_matmul(h, wt_bits, token):
    t = h.shape[0]
    tm = min(1024, t)

    def body(h_ref, w_hbm, tok_ref, o_ref, wt_ref, buf, sems):
        j = pl.program_id(0)
        slot = j % 2

        def fetch(seg, into):
            return pltpu.make_async_copy(w_hbm.at[pl.ds(_seg_row(seg), D)], buf.at[into], sems.at[into])

        @pl.when(pl.program_id(1) == 0)
        def _():
            @pl.when(j == 0)
            def _():
                fetch(j, slot).start()

            fetch(j, slot).wait()

            @pl.when(j + 1 < NSEG)
            def _():
                fetch(j + 1, 1 - slot).start()

            bits = pltpu.bitcast(buf[slot], jnp.uint32)
            row = lax.broadcasted_iota(jnp.int32, (D, D // 2), 0)
            live = jnp.logical_or(j != SEG_LAT, row < LAT_ROWS)
            lo = pltpu.bitcast(bits << 16, F32)
            hi = pltpu.bitcast(bits & jnp.uint32(0xFFFF0000), F32)
            wt_ref[:, :D // 2] = jnp.where(live, lo, 0.0).astype(BF16)
            wt_ref[:, D // 2:] = jnp.where(live, hi, 0.0).astype(BF16)

        o_ref[...] = _dot_nt(h_ref[...], wt_ref[...]).astype(BF16)

    return pl.pallas_call(
        body, name="proj_matmul", grid=(NSEG, t // tm),
        in_specs=[pl.BlockSpec((tm, D), lambda j, i: (i, 0)),
                  pl.BlockSpec(memory_space=pl.ANY),
                  pl.BlockSpec((8, 128), lambda j, i: (0, 0))],
        out_specs=[pl.BlockSpec((None, tm, D), lambda j, i: (j, i, 0)),
                   pl.BlockSpec((D, D), lambda j, i: (j, 0))],
        out_shape=[jax.ShapeDtypeStruct((NSEG, t, D), BF16), jax.ShapeDtypeStruct((NP, D), BF16)],
        scratch_shapes=[pltpu.VMEM((2, D, D // 2), F32), pltpu.SemaphoreType.DMA((2,))],
        compiler_params=_params(("arbitrary", "arbitrary")),
    )(h, wt_bits, token)


def dh_matmul(dproj, wt, token, seq, b):
    tm = min(1024, seq)
    nblk = seq // tm

    def body(b_ref, d_ref, w_ref, tok_ref, o_ref, acc_ref):
        k = pl.program_id(1)

        @pl.when(k == 0)
        def _():
            acc_ref[...] = jnp.zeros_like(acc_ref)

        acc_ref[...] += _dot(d_ref[...], w_ref[...])

        @pl.when(k == NSEG - 1)
        def _():
            o_ref[...] = acc_ref[...]

    return pl.pallas_call(
        body, name="dh_matmul",
        grid_spec=pltpu.PrefetchScalarGridSpec(
            num_scalar_prefetch=1, grid=(nblk, NSEG),
            in_specs=[pl.BlockSpec((None, tm, D), lambda i, k, s: (k, s[0] * nblk + i, 0)),
                      pl.BlockSpec((D, D), lambda i, k, s: (k, 0)),
                      pl.BlockSpec((8, 128), lambda i, k, s: (0, 0))],
            out_specs=pl.BlockSpec((tm, D), lambda i, k, s: (i, 0)),
            scratch_shapes=[pltpu.VMEM((tm, D), F32)]),
        out_shape=jax.ShapeDtypeStruct((seq, D), F32),
        compiler_params=_params(("parallel", "arbitrary")),
    )(jnp.full((1,), b, jnp.int32), dproj, wt, token)


def win_grad_matmul(h, dproj, token):
    t = h.shape[0]
    tk = min(1024, t)
    nk = t // tk

    def body(h_ref, d_ref, tok_ref, o_hbm, acc_ref, sem):
        j = pl.program_id(0)
        k = pl.program_id(1)

        @pl.when(k == 0)
        def _():
            acc_ref[...] = jnp.zeros_like(acc_ref)

        acc_ref[...] += _dot_tn(d_ref[...], h_ref[...])

        @pl.when(jnp.logical_and(k == nk - 1, j != SEG_LAT))
        def _():
            cp = pltpu.make_async_copy(acc_ref, o_hbm.at[pl.ds(_seg_row(j), D)], sem)
            cp.start()
            cp.wait()

        @pl.when(jnp.logical_and(k == nk - 1, j == SEG_LAT))
        def _():
            cp = pltpu.make_async_copy(acc_ref.at[pl.ds(0, LAT_ROWS)],
                                       o_hbm.at[pl.ds(SEG_ROWS[SEG_LAT], LAT_ROWS)], sem)
            cp.start()
            cp.wait()

    return pl.pallas_call(
        body, name="win_grad_matmul", grid=(NSEG, nk),
        in_specs=[pl.BlockSpec((tk, D), lambda j, k: (k, 0)),
                  pl.BlockSpec((None, tk, D), lambda j, k: (j, k, 0)),
                  pl.BlockSpec((8, 128), lambda j, k: (0, 0))],
        out_specs=pl.BlockSpec(memory_space=pl.ANY),
        out_shape=jax.ShapeDtypeStruct((N_IN, D), F32),
        scratch_shapes=[pltpu.VMEM((D, D), F32), pltpu.SemaphoreType.DMA],
        compiler_params=_params(("arbitrary", "arbitrary")),
    )(h, dproj, token)


def grad_matmul(a, b, name):
    t, m = a.shape
    n = b.shape[1]
    tk = min(1024, t)
    nk = t // tk

    def body(a_ref, b_ref, o_ref, acc_ref):
        k = pl.program_id(0)

        @pl.when(k == 0)
        def _():
            acc_ref[...] = jnp.zeros_like(acc_ref)

        acc_ref[...] += _dot_tn(a_ref[...], b_ref[...])

        @pl.when(k == nk - 1)
        def _():
            o_ref[...] = acc_ref[...].astype(BF16)

    return pl.pallas_call(
        body, name=name, grid=(nk,),
        in_specs=[pl.BlockSpec((tk, m), lambda k: (k, 0)),
                  pl.BlockSpec((tk, n), lambda k: (k, 0))],
        out_specs=pl.BlockSpec((m, n), lambda k: (0, 0)),
        out_shape=jax.ShapeDtypeStruct((m, n), BF16),
        scratch_shapes=[pltpu.VMEM((m, n), F32)],
        compiler_params=_params(("arbitrary",)),
    )(a, b)


def ada_fwd(c_all, w_ada, b_cols):
    def body(c_ref, w_ref, b_ref, o_ref):
        o_ref[...] = _dot(c_ref[...].astype(BF16), w_ref[...].astype(BF16)) + b_ref[...]

    return pl.pallas_call(
        body, name="ada_fwd",
        out_shape=jax.ShapeDtypeStruct((c_all.shape[0], w_ada.shape[1]), F32),
        compiler_params=_params(),
    )(c_all, w_ada, b_cols)


def ada_bwd(c_all, dmod_cols):
    def body(c_ref, d_ref, o_ref):
        o_ref[...] = _dot_tn(c_ref[...].astype(BF16), d_ref[...].astype(BF16))

    return pl.pallas_call(
        body, name="ada_bwd",
        out_shape=jax.ShapeDtypeStruct((c_all.shape[1], dmod_cols.shape[1]), F32),
        compiler_params=_params(),
    )(c_all, dmod_cols)


def slot_sum(g):
    def body(g_ref, o_ref):
        acc = g_ref[0]
        for s in range(1, 8):
            acc = acc + g_ref[s]
        o_ref[...] = acc

    return pl.pallas_call(
        body, name="slot_sum",
        out_shape=jax.ShapeDtypeStruct(g.shape[1:], F32),
    )(g)


def prenorm_fwd(x2, scale, shift, g_pre, seq):
    t = x2.shape[0]
    tm = min(512, seq)
    tpb = seq // tm

    def body(x_ref, sc_ref, sh_ref, g_ref, h_ref):
        xv = x_ref[...]
        r = lax.rsqrt(jnp.mean(xv * xv, axis=-1, keepdims=True) + EPS)
        hv = (xv * r * g_ref[...]) * (1.0 + sc_ref[...]) + sh_ref[...]
        h_ref[...] = hv.astype(BF16)

    per_batch = pl.BlockSpec((None, 1, D), lambda i: (i // tpb, 0, 0))
    return pl.pallas_call(
        body, name="prenorm_fwd", grid=(t // tm,),
        in_specs=[pl.BlockSpec((tm, D), lambda i: (i, 0)), per_batch, per_batch,
                  pl.BlockSpec((1, D), lambda i: (0, 0))],
        out_specs=pl.BlockSpec((tm, D), lambda i: (i, 0)),
        out_shape=jax.ShapeDtypeStruct((t, D), BF16),
        compiler_params=_params(("parallel",)),
    )(x2, scale, shift, g_pre)


def prenorm_bwd(dh, x2, dout, scale, g_pre, seq, token, b, gx_prev):
    t = x2.shape[0]
    tm = min(512, seq)
    tpb = seq // tm
    if gx_prev is None:
        gx_prev = lax.empty((t, D), F32)

    def body(b_ref, dh_ref, x_ref, do_ref, sc_ref, g_ref, tok_ref, gxp_ref, gx_ref, dsh_ref, dsc_ref, dg_ref):
        i = pl.program_id(0)
        xv = x_ref[...]
        dhv = dh_ref[...]
        g = g_ref[...]
        r = lax.rsqrt(jnp.mean(xv * xv, axis=-1, keepdims=True) + EPS)
        nrm = xv * r
        dxn = dhv * (1.0 + sc_ref[...])
        dn = dxn * g
        dx = r * (dn - nrm * jnp.mean(dn * nrm, axis=-1, keepdims=True))
        gx_ref[...] = dx + do_ref[...]

        @pl.when(i == 0)
        def _():
            dsh_ref[...] = jnp.zeros_like(dsh_ref)
            dsc_ref[...] = jnp.zeros_like(dsc_ref)
            dg_ref[...] = jnp.zeros_like(dg_ref)

        dsh_ref[...] += jnp.sum(dhv, axis=0, keepdims=True)
        dsc_ref[...] += jnp.sum(dhv * (nrm * g), axis=0, keepdims=True)
        dg_ref[...] += jnp.sum(dxn * nrm, axis=0, keepdims=True)

    row = pl.BlockSpec((tm, D), lambda i, s: (i, 0))
    grow = pl.BlockSpec((tm, D), lambda i, s: (s[0] * tpb + i, 0))
    per_batch = pl.BlockSpec((None, 1, D), lambda i, s: (s[0], 0, 0))
    vec = pl.BlockSpec((1, D), lambda i, s: (0, 0))
    return pl.pallas_call(
        body, name="prenorm_bwd",
        grid_spec=pltpu.PrefetchScalarGridSpec(
            num_scalar_prefetch=1, grid=(tpb,),
            in_specs=[row, grow, grow, per_batch, vec, pl.BlockSpec((8, 128), lambda i, s: (0, 0)),
                      pl.BlockSpec(memory_space=pl.ANY)],
            out_specs=[grow, vec, vec, vec]),
        out_shape=[jax.ShapeDtypeStruct((t, D), F32), jax.ShapeDtypeStruct((1, D), F32),
                   jax.ShapeDtypeStruct((1, D), F32), jax.ShapeDtypeStruct((1, D), F32)],
        input_output_aliases={7: 0},
        compiler_params=_params(("arbitrary",)),
    )(jnp.full((1,), b, jnp.int32), dh, x2, dout, scale, g_pre, token, gx_prev)


CONV_TC = 128


def _shift_down(u, k, rows):
    idx = lax.broadcasted_iota(jnp.int32, u.shape, 0)
    return jnp.where(idx >= k, pltpu.roll(u, k, 0), 0.0)


def _shift_up(u, k, rows):
    idx = lax.broadcasted_iota(jnp.int32, u.shape, 0)
    return jnp.where(idx < rows - k, pltpu.roll(u, rows - k, 0), 0.0)


def conv_fwd(proj, conv_w, seq):
    t = proj.shape[1]
    nb = t // seq

    def body(p_ref, w_ref, y_ref):
        av = p_ref[0].astype(F32)
        ab = p_ref[1].astype(F32)
        ac = p_ref[2].astype(F32)
        az = p_ref[3].astype(F32)
        w = w_ref[...]
        u = ac * av
        y1 = _shift_down(u, 2, seq) * w[0:1] + _shift_down(u, 1, seq) * w[1:2] + u * w[2:3]
        y_ref[...] = (ab * y1 * (az * _sig(az))).astype(BF16)

    return pl.pallas_call(
        body, name="conv_fwd", grid=(nb, D // CONV_TC),
        in_specs=[pl.BlockSpec((4, seq, CONV_TC), lambda b, ci: (1, b, ci)),
                  pl.BlockSpec((8, CONV_TC), lambda b, ci: (0, ci))],
        out_specs=pl.BlockSpec((seq, CONV_TC), lambda b, ci: (b, ci)),
        out_shape=jax.ShapeDtypeStruct((t, D), BF16),
        compiler_params=_params(("parallel", "parallel")),
    )(proj, conv_w)


def conv_bwd(dproj, proj, dy, conv_w, seq):
    t = proj.shape[1]
    nb = t // seq

    def body(dp_in_ref, p_ref, dy_ref, w_ref, dp_ref, dw_ref):
        b = pl.program_id(1)
        av = p_ref[0].astype(F32)
        ab = p_ref[1].astype(F32)
        ac = p_ref[2].astype(F32)
        az = p_ref[3].astype(F32)
        dyv = dy_ref[...].astype(F32)
        w = w_ref[...]
        u = ac * av
        u1 = _shift_down(u, 1, seq)
        u2 = _shift_down(u, 2, seq)
        y1 = u2 * w[0:1] + u1 * w[1:2] + u * w[2:3]
        sz = _sig(az)
        silu = az * sz
        dy1 = dyv * ab * silu
        du = dy1 * w[2:3] + _shift_up(dy1, 1, seq) * w[1:2] + _shift_up(dy1, 2, seq) * w[0:1]
        dp_ref[0] = (du * ac).astype(BF16)
        dp_ref[1] = (dyv * y1 * silu).astype(BF16)
        dp_ref[2] = (du * av).astype(BF16)
        dp_ref[3] = (dyv * ab * y1 * (sz * (1.0 + az * (1.0 - sz)))).astype(BF16)

        @pl.when(b == 0)
        def _():
            dw_ref[...] = jnp.zeros_like(dw_ref)

        dw_ref[0:1, :] += jnp.sum(dy1 * u2, axis=0, keepdims=True)
        dw_ref[1:2, :] += jnp.sum(dy1 * u1, axis=0, keepdims=True)
        dw_ref[2:3, :] += jnp.sum(dy1 * u, axis=0, keepdims=True)

    return pl.pallas_call(
        body, name="conv_bwd", grid=(D // CONV_TC, nb),
        in_specs=[pl.BlockSpec(memory_space=pl.ANY),
                  pl.BlockSpec((4, seq, CONV_TC), lambda ci, b: (1, b, ci)),
                  pl.BlockSpec((seq, CONV_TC), lambda ci, b: (b, ci)),
                  pl.BlockSpec((8, CONV_TC), lambda ci, b: (0, ci))],
        out_specs=[pl.BlockSpec((4, seq, CONV_TC), lambda ci, b: (1, b, ci)),
                   pl.BlockSpec((8, CONV_TC), lambda ci, b: (0, ci))],
        out_shape=[jax.ShapeDtypeStruct(dproj.shape, BF16),
                   jax.ShapeDtypeStruct((8, D), F32)],
        input_output_aliases={0: 0},
        compiler_params=_params(("parallel", "arbitrary")),
    )(dproj, proj, dy, conv_w)


def _rope_tables(pos_ref, invf_ref, ma_ref, mb_ref):
    ang = pos_ref[...].astype(F32) * invf_ref[...]
    cs = jnp.cos(ang)
    sn = jnp.sin(ang)
    return cs, sn * ma_ref[...], sn * mb_ref[...]


def _head_tables(cs, sa, sb):
    one = jnp.ones_like(cs)
    zero = jnp.zeros_like(cs)
    return (jnp.tile(jnp.concatenate([one, cs], axis=1), (1, H)),
            jnp.tile(jnp.concatenate([zero, sa], axis=1), (1, H)),
            jnp.tile(jnp.concatenate([zero, sb], axis=1), (1, H)))


def _rotate(v, cs, sa, sb, sign):
    width = v.shape[1]
    return v * cs + sign * (pltpu.roll(v, width - HALF, 1) * sa + pltpu.roll(v, HALF, 1) * sb)


MLA_TM = 256


def mla_prep_fwd(proj, pos, g_q, g_kv, wuq, wukv, tabs):
    t = proj.shape[1]
    tm = min(MLA_TM, t)

    def body(lat_ref, pos_ref, gq_ref, gkv_ref, wuq_ref, wukv_ref, invf_ref, ma_ref, mb_ref,
             q_ref, k_ref, kv_ref, qn_ref, kvn_ref):
        lat = lat_ref[...].astype(F32)
        ql = lat[:, :QL]
        kl = lat[:, QL:QL + KVL]
        kr = lat[:, QL + KVL:QL + KVL + 128]
        qn = (ql * lax.rsqrt(jnp.mean(ql * ql, axis=-1, keepdims=True) + EPS) * gq_ref[...]).astype(BF16)
        kvn = (kl * lax.rsqrt(jnp.mean(kl * kl, axis=-1, keepdims=True) + EPS) * gkv_ref[...]).astype(BF16)
        qn_ref[...] = qn
        kvn_ref[...] = kvn
        cs, sa, sb = _rope_tables(pos_ref, invf_ref, ma_ref, mb_ref)
        hc, ha, hb = _head_tables(cs, sa, sb)
        q = _dot_nt(qn, wuq_ref[...])
        q_ref[...] = (_rotate(q, hc, ha, hb, 1.0) * (SM_SCALE * LOG2E)).astype(BF16)
        kv = _dot_nt(kvn, wukv_ref[...]).astype(BF16)
        kv_ref[...] = kv
        kpe = _rotate(kr, cs, sa, sb, 1.0).astype(BF16)
        for hh in range(H):
            k_ref[:, hh * DQK:hh * DQK + 128] = kv[:, hh * DQK:hh * DQK + 128]
            k_ref[:, hh * DQK + 128:(hh + 1) * DQK] = kpe

    row = lambda w: pl.BlockSpec((tm, w), lambda i: (i, 0))
    const = lambda a: pl.BlockSpec(a.shape, lambda i: (0,) * a.ndim)
    return pl.pallas_call(
        body, name="mla_prep_fwd", grid=(t // tm,),
        in_specs=[pl.BlockSpec((None, tm, D), lambda i: (SEG_LAT, i, 0)), row(1),
                  const(g_q), const(g_kv), const(wuq), const(wukv)] + [const(a) for a in tabs],
        out_specs=[row(H * DQK), row(H * DQK), row(H * DQK), row(QL), row(KVL)],
        out_shape=[jax.ShapeDtypeStruct((t, H * DQK), BF16)] * 3
        + [jax.ShapeDtypeStruct((t, QL), BF16), jax.ShapeDtypeStruct((t, KVL), BF16)],
        compiler_params=_params(("parallel",)),
    )(proj, pos, g_q, g_kv, wuq, wukv, *tabs)


def mla_prep_bwd(dproj, proj, dq_rot, dk, dv, pos, g_q, g_kv, wuq, wukv, tabs):
    t = proj.shape[1]
    tm = min(MLA_TM, t)

    def body(dp_in_ref, lat_ref, dqr_ref, dk_ref, dv_ref, pos_ref, gq_ref, gkv_ref, wuq_ref, wukv_ref,
             invf_ref, ma_ref, mb_ref, dp_ref, dq_ref, dkv_ref, dgq_ref, dgkv_ref):
        i = pl.program_id(0)
        lat = lat_ref[...].astype(F32)
        ql = lat[:, :QL]
        kl = lat[:, QL:QL + KVL]
        rq = lax.rsqrt(jnp.mean(ql * ql, axis=-1, keepdims=True) + EPS)
        rk = lax.rsqrt(jnp.mean(kl * kl, axis=-1, keepdims=True) + EPS)
        nq = ql * rq
        nk = kl * rk
        cs, sa, sb = _rope_tables(pos_ref, invf_ref, ma_ref, mb_ref)
        hc, ha, hb = _head_tables(cs, sa, sb)
        dq = _rotate(dqr_ref[...] * SM_SCALE, hc, ha, hb, -1.0).astype(BF16)
        dq_ref[...] = dq
        dkpe = jnp.zeros((tm, 128), F32)
        for hh in range(H):
            dkv_ref[:, hh * DQK:hh * DQK + 128] = dk_ref[:, hh * DQK:hh * DQK + 128]
            dkv_ref[:, hh * DQK + 128:(hh + 1) * DQK] = dv_ref[:, hh * DV:(hh + 1) * DV]
            dkpe = dkpe + dk_ref[:, hh * DQK + 128:(hh + 1) * DQK].astype(F32)
        lane = lax.broadcasted_iota(jnp.int32, (tm, 128), 1)
        dkr = jnp.where(lane < ROPE, _rotate(dkpe, cs, sa, sb, -1.0), 0.0)
        dqn = _dot(dq, wuq_ref[...])
        dkvn = _dot(dkv_ref[...], wukv_ref[...])
        gq = gq_ref[...]
        gkv = gkv_ref[...]
        dnq = dqn * gq
        dnk = dkvn * gkv
        dql = rq * (dnq - nq * jnp.mean(dnq * nq, axis=-1, keepdims=True))
        dkl = rk * (dnk - nk * jnp.mean(dnk * nk, axis=-1, keepdims=True))
        dp_ref[:, :QL] = dql.astype(BF16)
        dp_ref[:, QL:QL + KVL] = dkl.astype(BF16)
        dp_ref[:, QL + KVL:QL + KVL + 128] = dkr.astype(BF16)
        dp_ref[:, QL + KVL + 128:] = jnp.zeros((tm, D - QL - KVL - 128), BF16)

        @pl.when(i == 0)
        def _():
            dgq_ref[...] = jnp.zeros_like(dgq_ref)
            dgkv_ref[...] = jnp.zeros_like(dgkv_ref)

        dgq_ref[...] += jnp.sum(dqn * nq, axis=0, keepdims=True)
        dgkv_ref[...] += jnp.sum(dkvn * nk, axis=0, keepdims=True)

    row = lambda w: pl.BlockSpec((tm, w), lambda i: (i, 0))
    const = lambda a: pl.BlockSpec(a.shape, lambda i: (0,) * a.ndim)
    seg = pl.BlockSpec((None, tm, D), lambda i: (SEG_LAT, i, 0))
    return pl.pallas_call(
        body, name="mla_prep_bwd", grid=(t // tm,),
        in_specs=[pl.BlockSpec(memory_space=pl.ANY), seg, row(H * DQK), row(H * DQK), row(H * DV), row(1),
                  const(g_q), const(g_kv), const(wuq), const(wukv)] + [const(a) for a in tabs],
        out_specs=[seg, row(H * DQK), row(H * DQK),
                   pl.BlockSpec((1, QL), lambda i: (0, 0)), pl.BlockSpec((1, KVL), lambda i: (0, 0))],
        out_shape=[jax.ShapeDtypeStruct(dproj.shape, BF16),
                   jax.ShapeDtypeStruct((t, H * DQK), BF16), jax.ShapeDtypeStruct((t, H * DQK), BF16),
                   jax.ShapeDtypeStruct((1, QL), F32), jax.ShapeDtypeStruct((1, KVL), F32)],
        input_output_aliases={0: 0},
        compiler_params=_params(("arbitrary",)),
    )(dproj, proj, dq_rot, dk, dv, pos, g_q, g_kv, wuq, wukv, *tabs)


def _causal_mask(s, n):
    row = lax.broadcasted_iota(jnp.int32, (n, n), 0)
    col = lax.broadcasted_iota(jnp.int32, (n, n), 1)
    return jnp.where(col <= row, s, -1e30)


def flash_fwd(q, k, kv, nb, seq):
    t = q.shape[0]
    tq = min(FLASH_TQ, seq)
    nq = seq // tq

    def body(q_ref, k_ref, v_ref, o_ref, lse_ref):
        for qi in range(nq):
            qs = slice(qi * tq, (qi + 1) * tq)
            qv = q_ref[qs, :]
            m = jnp.full((tq, 1), -1e30, F32)
            l = jnp.zeros((tq, 1), F32)
            acc = jnp.zeros((tq, DV), F32)
            for j in range(qi + 1):
                ks = slice(j * tq, (j + 1) * tq)
                s = _dot_nt(qv, k_ref[ks, :])
                if j == qi:
                    s = _causal_mask(s, tq)
                m_new = jnp.maximum(m, jnp.max(s, axis=1, keepdims=True))
                p = jnp.exp2(s - m_new)
                alpha = jnp.exp2(m - m_new)
                l = alpha * l + jnp.sum(p, axis=1, keepdims=True)
                acc = alpha * acc + _dot(p.astype(BF16), v_ref[ks, :])
                m = m_new
            o_ref[qs, :] = (acc / l).astype(BF16)
            lse_ref[qs, :] = jnp.broadcast_to(m + jnp.log(l) * LOG2E, (tq, DV))

    out_blk = pl.BlockSpec((seq, DV), lambda b, h: (b, h))
    return pl.pallas_call(
        body, name="flash_fwd", grid=(nb, H),
        in_specs=[pl.BlockSpec((seq, DQK), lambda b, h: (b, h)),
                  pl.BlockSpec((seq, DQK), lambda b, h: (b, h)),
                  pl.BlockSpec((seq, DV), lambda b, h: (b, 2 * h + 1))],
        out_specs=[out_blk, out_blk],
        out_shape=[jax.ShapeDtypeStruct((t, H * DV), BF16), jax.ShapeDtypeStruct((t, H * DV), F32)],
        compiler_params=_params(("parallel", "parallel")),
    )(q, k, kv)


def flash_bwd(q, k, kv, o, do, lse, nb, seq, token):
    t = q.shape[0]
    tq = min(FLASH_TQ, seq)
    nq = seq // tq

    def body(q_ref, k_ref, v_ref, o_ref, do_ref, lse_ref, tok_ref, dq_ref, dk_ref, dv_ref):
        delta = []
        for qi in range(nq):
            qs = slice(qi * tq, (qi + 1) * tq)
            delta.append(jnp.sum(do_ref[qs, :].astype(F32) * o_ref[qs, :].astype(F32), axis=1, keepdims=True))
        for ki in range(nq):
            ks = slice(ki * tq, (ki + 1) * tq)
            kb = k_ref[ks, :]
            vb = v_ref[ks, :]
            dk = jnp.zeros((tq, DQK), F32)
            dv = jnp.zeros((tq, DV), F32)
            for qi in range(ki, nq):
                qs = slice(qi * tq, (qi + 1) * tq)
                qv = q_ref[qs, :]
                dov = do_ref[qs, :]
                s = _dot_nt(qv, kb)
                if qi == ki:
                    s = _causal_mask(s, tq)
                p = jnp.exp2(s - lse_ref[qs, :][:, :1])
                dp = _dot_nt(dov, vb)
                dz = (p * (dp - delta[qi])).astype(BF16)
                dv = dv + _dot_tn(p.astype(BF16), dov)
                dk = dk + _dot_tn(dz, qv)
                dqb = _dot(dz, kb)
                if ki == 0:
                    dq_ref[qs, :] = dqb
                else:
                    dq_ref[qs, :] += dqb
            dk_ref[ks, :] = (dk * LN2).astype(BF16)
            dv_ref[ks, :] = dv.astype(BF16)

    full = lambda w, col: pl.BlockSpec((seq, w), col)
    same = lambda b, h: (b, h)
    return pl.pallas_call(
        body, name="flash_bwd", grid=(nb, H),
        in_specs=[full(DQK, same), full(DQK, same), full(DV, lambda b, h: (b, 2 * h + 1)),
                  full(DV, same), full(DV, same), full(DV, same),
                  pl.BlockSpec((8, 128), lambda b, h: (0, 0))],
        out_specs=[full(DQK, same), full(DQK, same), full(DV, same)],
        out_shape=[jax.ShapeDtypeStruct((t, H * DQK), F32), jax.ShapeDtypeStruct((t, H * DQK), BF16),
                   jax.ShapeDtypeStruct((t, H * DV), BF16)],
        compiler_params=_params(("parallel", "parallel")),
    )(q, k, kv, o, do, lse, token)


TAIL_TM = 256


def tail_fwd(y, attn, proj, x2, tgt, gate, g_post, wco, wmo, wout, seq):
    t = y.shape[0]
    nb = t // seq
    tm = min(TAIL_TM, seq)
    tpb = seq // tm

    def body(y_ref, at_ref, p_ref, x_ref, t_ref, gate_ref, gp_ref, wco_ref, wmo_ref, wout_ref,
             o_ref, ya_ref, yb_ref, m_ref, do2_ref, dout_ref, dgate_ref, dgp_ref, loss_ref):
        i = pl.program_id(0)
        bz = p_ref[0].astype(F32)
        ga = p_ref[1].astype(F32)
        gb = p_ref[2].astype(F32)
        ov = (at_ref[...].astype(F32) * (bz * _sig(bz))).astype(BF16)
        o_ref[...] = ov
        ya = _dot(y_ref[...], wco_ref[...])
        yb = _dot(ov, wmo_ref[...])
        ya_ref[...] = ya.astype(BF16)
        yb_ref[...] = yb.astype(BF16)
        mv = (_sig(ga) * ya + _sig(gb) * yb).astype(BF16)
        m_ref[...] = mv
        o2 = _dot(mv, wout_ref[...])
        r = lax.rsqrt(jnp.mean(o2 * o2, axis=-1, keepdims=True) + EPS)
        nrm = o2 * r
        gp = gp_ref[...]
        gate_v = gate_ref[...]
        rn = nrm * gp
        err = x_ref[...] + gate_v * rn - t_ref[...]
        dout = err * (1.0 / D)
        dout_ref[...] = dout
        dn = dout * gate_v * gp
        do2_ref[...] = (r * (dn - nrm * jnp.mean(dn * nrm, axis=-1, keepdims=True))).astype(BF16)

        @pl.when(i % tpb == 0)
        def _():
            dgate_ref[...] = jnp.zeros_like(dgate_ref)

        @pl.when(i == 0)
        def _():
            dgp_ref[...] = jnp.zeros_like(dgp_ref)
            loss_ref[...] = jnp.zeros_like(loss_ref)

        dgate_ref[...] += jnp.sum(dout * rn, axis=0, keepdims=True)
        dgp_ref[...] += jnp.sum(dout * gate_v * nrm, axis=0, keepdims=True)
        loss_ref[...] += 0.5 * jnp.sum(jnp.mean(err * err, axis=-1, keepdims=True), axis=0, keepdims=True)

    row = pl.BlockSpec((tm, D), lambda i: (i, 0))
    per_batch = pl.BlockSpec((None, 1, D), lambda i: (i // tpb, 0, 0))
    vec = pl.BlockSpec((1, D), lambda i: (0, 0))
    wgt = pl.BlockSpec((D, D), lambda i: (0, 0))
    act = jax.ShapeDtypeStruct((t, D), BF16)
    return pl.pallas_call(
        body, name="tail_fwd", grid=(t // tm,),
        in_specs=[row, row, pl.BlockSpec((3, tm, D), lambda i: (0, i, 0)), row, row, per_batch, vec,
                  wgt, wgt, wgt],
        out_specs=[row, row, row, row, row, row, per_batch, vec, pl.BlockSpec((1, 1), lambda i: (0, 0))],
        out_shape=[act, act, act, act, act, jax.ShapeDtypeStruct((t, D), F32),
                   jax.ShapeDtypeStruct((nb, 1, D), F32), jax.ShapeDtypeStruct((1, D), F32),
                   jax.ShapeDtypeStruct((1, 1), F32)],
        compiler_params=_params(("arbitrary",)),
    )(y, attn, proj, x2, tgt, gate, g_post, wco, wmo, wout)


def tail_bwd(do2, proj, ya, yb, attn, wout, wmo, wco):
    t = do2.shape[0]
    tm = min(TAIL_TM, t)

    def body(do2_ref, p_ref, ya_ref, yb_ref, at_ref, wout_ref, wmo_ref, wco_ref,
             dp_ref, dya_ref, dyb_ref, dat_ref, dy_ref):
        bz = p_ref[0].astype(F32)
        ga = p_ref[1].astype(F32)
        gb = p_ref[2].astype(F32)
        dm = _dot_nt(do2_ref[...], wout_ref[...])
        sa = _sig(ga)
        sb = _sig(gb)
        dya = (dm * sa).astype(BF16)
        dyb = (dm * sb).astype(BF16)
        dya_ref[...] = dya
        dyb_ref[...] = dyb
        dp_ref[1] = (dm * ya_ref[...].astype(F32) * (sa * (1.0 - sa))).astype(BF16)
        dp_ref[2] = (dm * yb_ref[...].astype(F32) * (sb * (1.0 - sb))).astype(BF16)
        dov = _dot_nt(dyb, wmo_ref[...])
        sz = _sig(bz)
        dat_ref[...] = (dov * (bz * sz)).astype(BF16)
        dp_ref[0] = (dov * at_ref[...].astype(F32) * (sz * (1.0 + bz * (1.0 - sz)))).astype(BF16)
        dy_ref[...] = _dot_nt(dya, wco_ref[...]).astype(BF16)

    row = pl.BlockSpec((tm, D), lambda i: (i, 0))
    seg3 = pl.BlockSpec((3, tm, D), lambda i: (0, i, 0))
    wgt = pl.BlockSpec((D, D), lambda i: (0, 0))
    act = jax.ShapeDtypeStruct((t, D), BF16)
    return pl.pallas_call(
        body, name="tail_bwd", grid=(t // tm,),
        in_specs=[row, seg3, row, row, row, wgt, wgt, wgt],
        out_specs=[seg3, row, row, row, row],
        out_shape=[jax.ShapeDtypeStruct((NSEG, t, D), BF16), act, act, act, act],
        compiler_params=_params(("parallel",)),
    )(do2, proj, ya, yb, attn, wout, wmo, wco)


def adamw(w, m, v, g, g2, name, token=None):
    rows, cols = w.shape
    tr = rows
    for cand in (256, 128, 64, 32, 16, 8):
        if rows % cand == 0 and rows > cand:
            tr = cand
            break
    has2 = g2 is not None
    n_in = 4 + has2

    def body(*refs):
        w_ref, m_ref, v_ref, g_ref = refs[:4]
        go_ref, d_ref, mo_ref, vo_ref = refs[-4:]
        grad = g_ref[...] + refs[4][...].astype(F32) if has2 else g_ref[...]
        mn = ADAM_B1 * m_ref[...] + (1.0 - ADAM_B1) * grad
        vn = ADAM_B2 * v_ref[...] + (1.0 - ADAM_B2) * (grad * grad)
        m_hat = mn / (1.0 - ADAM_B1 ** ADAM_STEP)
        v_hat = vn / (1.0 - ADAM_B2 ** ADAM_STEP)
        go_ref[...] = grad
        d_ref[...] = -ADAM_LR * (m_hat / (jnp.sqrt(v_hat) + ADAM_EPS) + ADAM_WD * w_ref[...])
        mo_ref[...] = mn
        vo_ref[...] = vn

    blk = pl.BlockSpec((tr, cols), lambda i: (i, 0))
    ins = [w, m, v, g] + ([g2] if has2 else [])
    specs = [blk] * n_in
    if token is not None:
        ins.append(token)
        specs.append(pl.BlockSpec((8, 128), lambda i: (0, 0)))
    return pl.pallas_call(
        body, name=name, grid=(rows // tr,),
        in_specs=specs, out_specs=[blk] * 4,
        out_shape=[jax.ShapeDtypeStruct((rows, cols), F32)] * 4,
        compiler_params=_params(("parallel",)),
    )(*ins)


def adamw_scattered(w, m, v, own, land, me, tr, name, transpose=False):
    slot_rows = land.shape[1]
    cols = land.shape[2]
    rows = slot_rows if transpose else w.shape[0]
    per_slot = slot_rows // tr

    def body(me_ref, w_ref, m_ref, v_ref, own_ref, land_ref, go_ref, d_ref, mo_ref, vo_ref):
        grad = own_ref[...].astype(F32)
        for s in range(8):
            grad = grad + land_ref[s].astype(F32)
        if transpose:
            grad = grad.T
        mn = ADAM_B1 * m_ref[...] + (1.0 - ADAM_B1) * grad
        vn = ADAM_B2 * v_ref[...] + (1.0 - ADAM_B2) * (grad * grad)
        m_hat = mn / (1.0 - ADAM_B1 ** ADAM_STEP)
        v_hat = vn / (1.0 - ADAM_B2 ** ADAM_STEP)
        go_ref[...] = grad
        d_ref[...] = -ADAM_LR * (m_hat / (jnp.sqrt(v_hat) + ADAM_EPS) + ADAM_WD * w_ref[...])
        mo_ref[...] = mn
        vo_ref[...] = vn

    wblk = pl.BlockSpec(w.shape if transpose else (tr, w.shape[1]), lambda i, s: (i, 0))
    return pl.pallas_call(
        body, name=name,
        grid_spec=pltpu.PrefetchScalarGridSpec(
            num_scalar_prefetch=1, grid=(rows // tr,),
            in_specs=[wblk, wblk, wblk,
                      pl.BlockSpec((tr, cols), lambda i, s: (s[0] * per_slot + i, 0)),
                      pl.BlockSpec((8, tr, cols), lambda i, s: (0, i, 0))],
            out_specs=[wblk] * 4),
        out_shape=[jax.ShapeDtypeStruct(w.shape, F32)] * 4,
        compiler_params=_params(),
    )(me, w, m, v, own, land)


def adamw_win(wt, mt, vt, ka, ra, kb, rb):
    rows = wt.shape[0]
    tc = 256
    nh = (D // 2) // tc

    def body(w_ref, m_ref, v_ref, ka_ref, ra_ref, kb_ref, rb_ref, go_ref, d_ref, mo_ref, vo_ref):
        first = pl.program_id(0) < nh
        grad = jnp.where(first, ka_ref[...] + ra_ref[...].astype(F32), kb_ref[...] + rb_ref[...].astype(F32))
        mn = ADAM_B1 * m_ref[...] + (1.0 - ADAM_B1) * grad
        vn = ADAM_B2 * v_ref[...] + (1.0 - ADAM_B2) * (grad * grad)
        m_hat = mn / (1.0 - ADAM_B1 ** ADAM_STEP)
        v_hat = vn / (1.0 - ADAM_B2 ** ADAM_STEP)
        go_ref[...] = grad
        d_ref[...] = -ADAM_LR * (m_hat / (jnp.sqrt(v_hat) + ADAM_EPS) + ADAM_WD * w_ref[...])
        mo_ref[...] = mn
        vo_ref[...] = vn

    blk = pl.BlockSpec((rows, tc), lambda j: (0, j))
    lo = pl.BlockSpec((rows, tc), lambda j: (0, jnp.minimum(j, nh - 1)))
    hi = pl.BlockSpec((rows, tc), lambda j: (0, jnp.maximum(j - nh, 0)))
    return pl.pallas_call(
        body, name="adamw_w_in", grid=(D // tc,),
        in_specs=[blk, blk, blk, lo, lo, hi, hi], out_specs=[blk] * 4,
        out_shape=[jax.ShapeDtypeStruct((rows, D), F32)] * 4,
        compiler_params=_params(("parallel",)),
    )(wt, mt, vt, ka, ra, kb, rb)


_ORD_A = ("x", "y", "c")
_ORD_B = ("y", "x", "c")


def _to_slots(full, order, col_sharded):
    if col_sharded:
        r = full.shape[0]
        cc = full.shape[1] // 8
        g = full.reshape(r, 2, 2, 2, cc).transpose(1, 2, 3, 0, 4)
    else:
        r = full.shape[0] // 8
        cc = full.shape[1]
        g = full.reshape(2, 2, 2, r, cc)
    names = ("x", "y", "c")
    perm = tuple(names.index(a) for a in order)
    return g.transpose(perm + (3, 4))


def _rows128(a, rows):
    flat = a.reshape(-1)
    return jnp.pad(flat, (0, rows * 128 - flat.shape[0])).reshape(rows, 128)


def kernel(x, c, positions, w_ada, b_ada, g_pre, w_in, conv_w, w_conv_out, g_q, w_uq, g_kv, w_ukv, w_mla_out, w_out, g_post, loss_target, m_w_ada, m_b_ada, m_g_pre, m_w_in, m_conv_w, m_w_conv_out, m_g_q, m_w_uq, m_g_kv, m_w_ukv, m_w_mla_out, m_w_out, m_g_post, v_w_ada, v_b_ada, v_g_pre, v_w_in, v_conv_w, v_w_conv_out, v_g_q, v_w_uq, v_g_kv, v_w_ukv, v_w_mla_out, v_w_out, v_g_post):
    nb, seq, _ = x.shape
    t = nb * seq
    mx, my, mc = lax.axis_index("x"), lax.axis_index("y"), lax.axis_index("c")
    me = 4 * mx + 2 * my + mc
    co = {"x": mx, "y": my, "c": mc}

    x2 = x.reshape(t, D)
    tgt2 = loss_target.reshape(t, D)
    pos2 = positions.reshape(t, 1)

    packed = jnp.concatenate([c.reshape(2 * D // 128, 128), _rows128(conv_w[0], 8)], axis=0)
    gath = small_allgather(packed, "gather_cond")
    c_all = gath[:, :16].reshape(8 * nb, D)
    conv_full = gath[:, 16:19].reshape(8, 3, 128).transpose(1, 0, 2).reshape(3, D)
    conv_full8 = jnp.pad(conv_full, ((0, 5), (0, 0)))
    ada_cols = w_ada.shape[2]
    b_cols = lax.dynamic_slice(b_ada, (0, me * ada_cols), (1, ada_cols))
    mod_part = ada_fwd(c_all, w_ada[0], b_cols)
    mod_g = small_allgather(mod_part.reshape(8 * nb * ada_cols // 128, 128), "gather_mod")
    mod_all = mod_g.reshape(8, 8 * nb, ada_cols).transpose(1, 0, 2).reshape(8 * nb, 8 * ada_cols)
    mod = lax.dynamic_slice(mod_all, (me * nb, 0), (nb, 3 * D))
    shift = mod[:, 0:D].reshape(nb, 1, D)
    scale = mod[:, D:2 * D].reshape(nb, 1, D)
    gate = mod[:, 2 * D:3 * D].reshape(nb, 1, D)

    wt = w_in[0].T.astype(BF16)
    lo = lax.bitcast_convert_type(wt[:, :D // 2], jnp.uint16).astype(jnp.uint32)
    hi = lax.bitcast_convert_type(wt[:, D // 2:], jnp.uint16).astype(jnp.uint32)
    wt_bits = lax.bitcast_convert_type(lo | (hi << 16), F32)
    q4 = D // 4
    r3rd = wt_bits.shape[0] // 3
    plan = [(0, (k * r3rd, r3rd), (g * q4, q4), (_ORD_A, _ORD_B)[g]) for k in range(3) for g in range(2)]
    gw = allgather_big([wt_bits], plan, "gather_w_in")
    late = [w_conv_out[0].astype(BF16), w_mla_out[0].astype(BF16), w_out[0].astype(BF16),
            jnp.pad(w_uq[0].T.astype(BF16), ((0, DQK - 192), (0, 0))), w_ukv[0].T.astype(BF16)]
    gw0, late = lax.optimization_barrier((gw[0], late))
    late_state, late_token = gather_start(late, "gather_late_start")
    wt_bits_all = gw0.reshape(N_IN, D // 2)

    inv_freq = ROPE_THETA ** (-jnp.arange(0, ROPE, 2, dtype=F32) / ROPE)
    invf = jnp.concatenate([inv_freq, inv_freq, jnp.zeros((128 - ROPE,), F32)]).reshape(1, 128)
    lane = np.arange(128)
    tabs = (invf,
            jnp.asarray(np.where(lane < HALF, -1.0, 0.0).reshape(1, 128), F32),
            jnp.asarray(np.where((lane >= HALF) & (lane < ROPE), 1.0, 0.0).reshape(1, 128), F32))

    h = prenorm_fwd(x2, scale, shift, g_pre, seq)
    proj, wt_p = proj_matmul(h, wt_bits_all, late_token)
    y = conv_fwd(proj, conv_full8, seq)
    gl = gather_wait(late_state, y, "gather_late_wait")
    wco = gl[0].reshape(D, D)
    wmo = gl[1].reshape(D, D)
    wout = gl[2].reshape(D, D)
    wuq_p = gl[3].reshape(H * DQK, QL)
    wukv = gl[4].reshape(H * 256, KVL)
    q_rot, k_cat, kv, qn, kvn = mla_prep_fwd(proj, pos2, g_q, g_kv, wuq_p, wukv, tabs)
    attn, lse = flash_fwd(q_rot, k_cat, kv, nb, seq)
    o, ya, yb, m, do2, dout, dgate, dg_post, loss_part = tail_fwd(
        y, attn, proj, x2, tgt2, gate, g_post, wco, wmo, wout, seq)

    dproj, dya, dyb, dattn, dy = tail_bwd(do2, proj, ya, yb, attn, wout, wmo, wco)
    g_wout = grad_matmul(m, do2, "grad_w_square")
    g_wmo = grad_matmul(o, dyb, "grad_w_square")
    g_wco = grad_matmul(y, dya, "grad_w_square")
    sc1, sc1_tok = scatter_start([g_wco, g_wmo, g_wout], "scatter_out_grads_start")
    dproj, dconv = conv_bwd(dproj, proj, dy, conv_full8, seq)
    dq_rot, dk, dv = flash_bwd(q_rot, k_cat, kv, attn, dattn, lse, nb, seq, sc1_tok)
    dproj, dq, dkv, dg_q, dg_kv = mla_prep_bwd(dproj, proj, dq_rot, dk, dv, pos2, g_q, g_kv, wuq_p, wukv, tabs)
    g_wuq_t = grad_matmul(dq, qn, "grad_w_uq")
    g_wukv_t = grad_matmul(dkv, kvn, "grad_w_ukv")
    sc2, sc2_tok = scatter_start([g_wuq_t, g_wukv_t], "scatter_mla_grads_start")
    g_win_p = win_grad_matmul(h, dproj, sc2_tok)

    g_wt = g_win_p.reshape(2, 2, 2, N_IN // 8, D)
    ords = [("c", "y", "x"), ("c", "x", "y")]
    hc = D // 2
    win_shape = (2, 2, N_IN // 8, hc)
    pick_w = lambda col: (lambda ref, cc: ref.at[:, :, 1 - cc["c"], :, pl.ds(col * hc, hc)])
    which1 = [0, 0]
    picks1 = [pick_w(0), pick_w(1)]
    st1, tok1 = swap_start([g_wt], which1, ["c"] * 2, picks1, [win_shape] * 2, "rs_c_start")
    assert nb == 2
    dh0 = dh_matmul(dproj, wt_p, tok1, seq, 0)
    (g_wt,), r1 = swap_wait(st1, dh0, which1, ["c"] * 2, picks1, "rs_c_wait")
    sel_xyc = jnp.stack([mx, my, mc]).astype(jnp.int32)
    sel2 = [jnp.stack([co[o[2]]]).astype(jnp.int32) for o in ords]
    first = [rs_win_add_first(g_wt, r1[0], sel_xyc, 1, 0, "rs_add_first_0"),
             rs_win_add_first(g_wt, r1[1], sel_xyc, 0, 1, "rs_add_first_1")]
    keep1, send1 = zip(*first)
    all4 = [0, 1]
    none4 = [None] * 2
    axes2 = [o[1] for o in ords]
    st2, tok2 = swap_start(list(send1), all4, axes2, none4, [s.shape for s in send1], "rs_ici1_start")

    dh1 = dh_matmul(dproj, wt_p, tok2, seq, 1)
    gx0, dsh0, dsc0, dgp0 = prenorm_bwd(dh0, x2, dout, scale, g_pre, seq, tok2, 0, None)
    _, r2 = swap_wait(st2, (gx0, dh1), all4, axes2, none4, "rs_ici1_wait")
    keep2, send2 = zip(*[rs_add_second(keep1[a], r2[a], sel2[a], "rs_add_second") for a in range(2)])
    axes3 = [o[2] for o in ords]
    st3, tok3 = swap_start(list(send2), all4, axes3, none4, [s.shape for s in send2], "rs_ici2_start")
    grad_x2, dsh1, dsc1, dgp1 = prenorm_bwd(dh1, x2, dout, scale, g_pre, seq, tok3, 1, gx0)
    dshift = jnp.stack([dsh0, dsh1])
    dscale = jnp.stack([dsc0, dsc1])
    dg_pre = dgp0 + dgp1

    dmod = jnp.concatenate([dshift, dscale, dgate], axis=2).reshape(nb * 3 * D // 128, 128)
    small = jnp.concatenate([
        dmod, _rows128(dg_pre, 8), _rows128(dg_post, 8), _rows128(dg_q, 8), _rows128(dg_kv, 8),
        dconv[0:3].reshape(24, 128), _rows128(loss_part, 8)], axis=0)
    small_g = small_allgather(small, "gather_small_grads")
    sums = slot_sum(small_g)
    dmod_all = small_g[:, 0:48].reshape(8 * nb, 3 * D)
    g_bada = (sums[0:24] + sums[24:48]).reshape(1, 3 * D)
    g_gpre = sums[48:56].reshape(1, D)
    g_gpost = sums[56:64].reshape(1, D)
    g_gq = sums[64:67].reshape(1, QL)
    g_gkv = sums[72:74].reshape(1, KVL)
    g_conv_full = sums[80:104].reshape(3, D)
    loss = sums[104, 0]
    g_conv = lax.dynamic_slice(g_conv_full, (0, me * 128), (3, 128))
    dmod_cols = lax.dynamic_slice(dmod_all, (0, me * ada_cols), (8 * nb, ada_cols))
    g_wada = ada_bwd(c_all, dmod_cols)

    res = {}
    res["w_ada"] = [o_[None] for o_ in adamw(w_ada[0], m_w_ada[0], v_w_ada[0], g_wada, None, "adamw_w_ada", tok3)]

    def pack(b_, gp_, gpo_, gq_, gkv_, cw_):
        return jnp.concatenate([_rows128(b_, 24), _rows128(gp_, 8), _rows128(gpo_, 8), _rows128(gq_, 8),
                                _rows128(gkv_, 8), _rows128(cw_, 8)], axis=0)

    sw = pack(b_ada, g_pre, g_post, g_q, g_kv, conv_w)
    sm = pack(m_b_ada, m_g_pre, m_g_post, m_g_q, m_g_kv, m_conv_w)
    sv = pack(v_b_ada, v_g_pre, v_g_post, v_g_q, v_g_kv, v_conv_w)
    sg = pack(g_bada, g_gpre, g_gpost, g_gq, g_gkv, g_conv)
    small_out = adamw(sw, sm, sv, sg, None, "adamw_small", tok3)

    _, r3 = swap_wait(st3, small_out[0], all4, axes3, none4, "rs_ici2_wait")

    (g_wco, g_wmo, g_wout), (l_wco, l_wmo, l_wout) = scatter_wait(sc1, small_out[1], "scatter_out_grads_wait")
    (g_wuq_t, g_wukv_t), (l_wuq, l_wukv) = scatter_wait(sc2, small_out[2], "scatter_mla_grads_wait")

    res["w_in"] = [o_.T[None] for o_ in adamw_win(w_in[0].T, m_w_in[0].T, v_w_in[0].T,
                                                  keep2[0], r3[0], keep2[1], r3[1])]
    me1 = me.reshape(1).astype(jnp.int32)
    res["w_uq"] = [o_.T[None] for o_ in adamw_scattered(
        w_uq[0].T, m_w_uq[0].T, v_w_uq[0].T, g_wuq_t, l_wuq, me1, 64, "adamw_w_uq")]
    res["w_ukv"] = [o_[None] for o_ in adamw_scattered(
        w_ukv[0], m_w_ukv[0], v_w_ukv[0], g_wukv_t, l_wukv, me1, KVL, "adamw_w_ukv", transpose=True)]
    for nm, wv, mv, vv, gg, ll in (("w_conv_out", w_conv_out, m_w_conv_out, v_w_conv_out, g_wco, l_wco),
                                   ("w_mla_out", w_mla_out, m_w_mla_out, v_w_mla_out, g_wmo, l_wmo),
                                   ("w_out", w_out, m_w_out, v_w_out, g_wout, l_wout)):
        res[nm] = [o_[None] for o_ in adamw_scattered(wv[0], mv[0], vv[0], gg, ll, me1, 128, "adamw_square")]

    def unpack(a):
        return {"b_ada": a[0:24].reshape(1, 3 * D), "g_pre": a[24:32].reshape(1, D),
                "g_post": a[32:40].reshape(1, D), "g_q": a[40:43].reshape(1, QL),
                "g_kv": a[48:50].reshape(1, KVL), "conv_w": a[56:59].reshape(-1)[:3 * 128].reshape(1, 3, 128)}

    for nm in ("b_ada", "g_pre", "g_post", "g_q", "g_kv", "conv_w"):
        res[nm] = [unpack(a)[nm] for a in small_out]

    order = ["w_ada", "b_ada", "g_pre", "w_in", "conv_w", "w_conv_out", "g_q", "w_uq", "g_kv", "w_ukv",
             "w_mla_out", "w_out", "g_post"]
    out = [loss, grad_x2.reshape(nb, seq, D)]
    for k_ in range(4):
        out += [res[nm][k_] for nm in order]
    return tuple(out)
```

```python
import functools

import numpy as np
import jax
import jax.numpy as jnp
from jax import lax
from jax.experimental import pallas as pl
from jax.experimental.pallas import tpu as pltpu

F32 = jnp.float32
BF16 = jnp.bfloat16
MESH = pl.DeviceIdType.MESH

D = 1024
H = 8
QL = 384
KVL = 256
ROPE = 64
HALF = ROPE // 2
DQK = 256
DV = 128
NSEG = 8
NP = NSEG * D
EPS = 1e-6
ROPE_THETA = 10000.0
SM_SCALE = (128 + ROPE) ** -0.5
LOG2E = 1.4426950408889634
LN2 = 0.6931471805599453
FLASH_TQ = 512

SEG_BZ, SEG_GA, SEG_GB, SEG_LAT, SEG_V = 0, 1, 2, 3, 4

ADAM_LR = 0.001
ADAM_B1 = 0.9
ADAM_B2 = 0.999
ADAM_EPS = 1e-08
ADAM_WD = 0.01
ADAM_STEP = 10

VMEM_LIMIT = 56 * 1024 * 1024


def _params(sem=None, vmem=VMEM_LIMIT):
    kw = dict(vmem_limit_bytes=vmem)
    if sem is not None:
        kw["dimension_semantics"] = sem
    return pltpu.CompilerParams(**kw)


def _sig(v):
    return 1.0 / (1.0 + jnp.exp(-v))


def _dot(a, b):
    return jnp.dot(a, b, preferred_element_type=F32)


def _dot_nt(a, b):
    return lax.dot_general(a, b, (((1,), (1,)), ((), ())), preferred_element_type=F32)


def _dot_tn(a, b):
    return lax.dot_general(a, b, (((0,), (0,)), ((), ())), preferred_element_type=F32)


_AXIS_POS = {"x": 0, "y": 1, "c": 2}


def _coords():
    return lax.axis_index("x"), lax.axis_index("y"), lax.axis_index("c")


def _partner(axis):
    p = list(_coords())
    p[_AXIS_POS[axis]] = 1 - p[_AXIS_POS[axis]]
    return tuple(p)


def small_allgather(v, name):
    rows = v.shape[0]

    def body(v_ref, out_ref, send_sems, recv_sems):
        x, y, c = _coords()
        me = 4 * x + 2 * y + c
        out_ref[me] = v_ref[...]
        copies = []
        for k in range(1, 8):
            peer = (1 - x if k & 4 else x, 1 - y if k & 2 else y, 1 - c if k & 1 else c)
            cp = pltpu.make_async_remote_copy(
                src_ref=v_ref, dst_ref=out_ref.at[me],
                send_sem=send_sems.at[k - 1], recv_sem=recv_sems.at[k - 1],
                device_id=peer, device_id_type=MESH)
            cp.start()
            copies.append(cp)
        for cp in copies:
            cp.wait()

    return pl.pallas_call(
        body, name=name,
        out_shape=jax.ShapeDtypeStruct((8, rows, 128), F32),
        in_specs=[pl.BlockSpec(memory_space=pltpu.VMEM)],
        out_specs=pl.BlockSpec(memory_space=pltpu.VMEM),
        scratch_shapes=[pltpu.SemaphoreType.DMA((7,)), pltpu.SemaphoreType.DMA((7,))],
    )(v)


def _own_block_placed(s):
    x, y, c = _coords()
    return lax.dynamic_update_slice(lax.empty((2, 2, 2) + s.shape, s.dtype), s[None, None, None],
                                    (x, y, c) + (0,) * s.ndim)


def allgather_big(arrs, plan, name):
    n = len(arrs)
    m = len(plan)

    def body(*refs):
        ins, outs = refs[n:2 * n], refs[2 * n:3 * n]
        send_sems, recv_sems = refs[3 * n:]
        x, y, c = _coords()
        co = {"x": x, "y": y, "c": c}

        def window(ref, lead, rows, cols):
            win = tuple(slice(None) if w is None else pl.ds(w[0], w[1]) for w in (rows, cols))
            return ref.at[tuple(lead) + win]

        def held(e, free):
            i, rows, cols, _ = plan[e]
            lead = [slice(None) if ax in free else co[ax] for ax in ("x", "y", "c")]
            return window(outs[i], lead, rows, cols)

        def rcopy(e, stage, src, dst, axis):
            return pltpu.make_async_remote_copy(
                src_ref=src, dst_ref=dst,
                send_sem=send_sems.at[e, stage], recv_sem=recv_sems.at[e, stage],
                device_id=_partner(axis), device_id_type=MESH)

        stages = [[], [], []]
        for e, (i, rows, cols, order) in enumerate(plan):
            cp = rcopy(e, 0, window(ins[i], [], rows, cols), held(e, ()), order[0])
            cp.start()
            stages[0].append(cp)
        for s in (1, 2):
            for e, (i, rows, cols, order) in enumerate(plan):
                stages[s - 1][e].wait_recv()
                blk = held(e, order[:s])
                cp = rcopy(e, s, blk, blk, order[s])
                cp.start()
                stages[s].append(cp)
        for e in range(m):
            stages[2][e].wait_recv()
        for e in range(m):
            for s in range(3):
                stages[s][e].wait_send()

    any_spec = pl.BlockSpec(memory_space=pl.ANY)
    lands = [_own_block_placed(a) for a in arrs]
    return pl.pallas_call(
        body, name=name,
        out_shape=[jax.ShapeDtypeStruct(l.shape, l.dtype) for l in lands],
        in_specs=[any_spec] * (2 * n),
        out_specs=[any_spec] * n,
        input_output_aliases={i: i for i in range(n)},
        scratch_shapes=[pltpu.SemaphoreType.DMA((m, 3)), pltpu.SemaphoreType.DMA((m, 3))],
    )(*lands, *arrs)


def exchange(arrs, axes, picks, out_shapes, name):
    n = len(arrs)

    def body(*refs):
        ins, outs = refs[:n], refs[n:2 * n]
        send_sems, recv_sems = refs[2 * n:]
        x, y, c = _coords()
        co = {"x": x, "y": y, "c": c}
        copies = []
        for a in range(n):
            src = ins[a] if picks[a] is None else picks[a](ins[a], co)
            cp = pltpu.make_async_remote_copy(
                src_ref=src, dst_ref=outs[a],
                send_sem=send_sems.at[a], recv_sem=recv_sems.at[a],
                device_id=_partner(axes[a]), device_id_type=MESH)
            cp.start()
            copies.append(cp)
        for cp in copies:
            cp.wait()

    any_spec = pl.BlockSpec(memory_space=pl.ANY)
    return pl.pallas_call(
        body, name=name,
        out_shape=[jax.ShapeDtypeStruct(s, a.dtype) for s, a in zip(out_shapes, arrs)],
        in_specs=[any_spec] * n,
        out_specs=[any_spec] * n,
        scratch_shapes=[pltpu.SemaphoreType.DMA((n,)), pltpu.SemaphoreType.DMA((n,))],
    )(*arrs)


_HBM = pl.BlockSpec(memory_space=pltpu.HBM)
_SEM = pl.BlockSpec(memory_space=pltpu.SEMAPHORE)


def _swap_copies(srcs, lands, send_sems, recv_sems, axes, picks):
    x, y, c = _coords()
    co = {"x": x, "y": y, "c": c}
    return [pltpu.make_async_remote_copy(
        src_ref=srcs[a] if picks[a] is None else picks[a](srcs[a], co), dst_ref=lands[a],
        send_sem=send_sems.at[a], recv_sem=recv_sems.at[a],
        device_id=_partner(axes[a]), device_id_type=MESH) for a in range(len(srcs))]


def swap_start(arrs, which, axes, picks, out_shapes, name):
    ns, n = len(arrs), len(which)

    def body(*refs):
        srcs, lands = refs[:ns], refs[ns:ns + n]
        send_sems, recv_sems = refs[ns + n:ns + n + 2]
        token = refs[-1]
        for cp in _swap_copies([srcs[i] for i in which], lands, send_sems, recv_sems, axes, picks):
            cp.start()
        token[...] = jnp.zeros_like(token)

    lands = [lax.empty(s, arrs[i].dtype) for s, i in zip(out_shapes, which)]
    ops = [pltpu.with_memory_space_constraint(a, pltpu.HBM) for a in list(arrs) + lands]
    out = pl.pallas_call(
        body, name=name,
        out_shape=[pltpu.SemaphoreType.DMA((n,)), pltpu.SemaphoreType.DMA((n,))]
        + [pltpu.HBM(o.shape, o.dtype) for o in ops] + [jax.ShapeDtypeStruct((8, 128), F32)],
        in_specs=[_HBM] * (ns + n),
        out_specs=[_SEM, _SEM] + [_HBM] * (ns + n) + [pl.BlockSpec(memory_space=pltpu.VMEM)],
        input_output_aliases={i: 2 + i for i in range(ns + n)},
        compiler_params=pltpu.CompilerParams(has_side_effects=pltpu.SideEffectType.DATAFLOW_SIDE_EFFECTING),
    )(*ops)
    return out[:-1], out[-1]


def swap_wait(state, after, which, axes, picks, name):
    n = len(which)
    ns = len(state) - 2 - n

    def body(*refs):
        srcs, lands = refs[:ns], refs[ns:ns + n]
        send_sems, recv_sems = refs[ns + n:ns + n + 2]
        for cp in _swap_copies([srcs[i] for i in which], lands, send_sems, recv_sems, axes, picks):
            cp.wait_send()
            cp.wait_recv()

    thru = list(state[2:])
    after = list(after) if isinstance(after, (list, tuple)) else [after]
    out = pl.pallas_call(
        body, name=name,
        out_shape=[pltpu.HBM(o.shape, o.dtype) for o in thru],
        in_specs=[_HBM] * (ns + n) + [_SEM, _SEM] + [pl.BlockSpec(memory_space=pl.ANY)] * len(after),
        out_specs=[_HBM] * (ns + n),
        input_output_aliases={i: i for i in range(ns + n)},
        compiler_params=pltpu.CompilerParams(has_side_effects=pltpu.SideEffectType.DATAFLOW_SIDE_EFFECTING),
    )(*thru, state[0], state[1], *after)
    return out[:ns], out[ns:]


def _gather_copies(shards, lands, send_sems, recv_sems):
    x, y, c = _coords()
    copies = []
    for a in range(len(shards)):
        for k in range(1, 8):
            peer = (1 - x if k & 4 else x, 1 - y if k & 2 else y, 1 - c if k & 1 else c)
            copies.append(pltpu.make_async_remote_copy(
                src_ref=shards[a], dst_ref=lands[a].at[x, y, c],
                send_sem=send_sems.at[7 * a + k - 1], recv_sem=recv_sems.at[7 * a + k - 1],
                device_id=peer, device_id_type=MESH))
    return copies


def gather_start(shards, name):
    n = len(shards)
    x, y, c = _coords()

    def body(*refs):
        srcs, lands = refs[:n], refs[n:2 * n]
        send_sems, recv_sems = refs[2 * n:2 * n + 2]
        token = refs[-1]
        for cp in _gather_copies(srcs, lands, send_sems, recv_sems):
            cp.start()
        token[...] = jnp.zeros_like(token)

    lands = [_own_block_placed(s) for s in shards]
    ops = [pltpu.with_memory_space_constraint(a, pltpu.HBM) for a in list(shards) + lands]
    out = pl.pallas_call(
        body, name=name,
        out_shape=[pltpu.SemaphoreType.DMA((7 * n,)), pltpu.SemaphoreType.DMA((7 * n,))]
        + [pltpu.HBM(o.shape, o.dtype) for o in ops] + [jax.ShapeDtypeStruct((8, 128), F32)],
        in_specs=[_HBM] * (2 * n),
        out_specs=[_SEM, _SEM] + [_HBM] * (2 * n) + [pl.BlockSpec(memory_space=pltpu.VMEM)],
        input_output_aliases={i: 2 + i for i in range(2 * n)},
        compiler_params=pltpu.CompilerParams(has_side_effects=pltpu.SideEffectType.DATAFLOW_SIDE_EFFECTING),
    )(*ops)
    return out[:-1], out[-1]


def gather_wait(state, after, name):
    n = (len(state) - 2) // 2

    def body(*refs):
        srcs, lands = refs[:n], refs[n:2 * n]
        send_sems, recv_sems = refs[2 * n:2 * n + 2]
        for cp in _gather_copies(srcs, lands, send_sems, recv_sems):
            cp.wait_send()
            cp.wait_recv()

    thru = list(state[2:])
    out = pl.pallas_call(
        body, name=name,
        out_shape=[pltpu.HBM(o.shape, o.dtype) for o in thru],
        in_specs=[_HBM] * (2 * n) + [_SEM, _SEM, pl.BlockSpec(memory_space=pl.ANY)],
        out_specs=[_HBM] * (2 * n),
        input_output_aliases={i: i for i in range(2 * n)},
        compiler_params=pltpu.CompilerParams(has_side_effects=pltpu.SideEffectType.DATAFLOW_SIDE_EFFECTING),
    )(*thru, state[0], state[1], after)
    return out[n:]


def _scatter_copies(grads, lands, send_sems, recv_sems):
    x, y, c = _coords()
    me = 4 * x + 2 * y + c
    copies = []
    for a in range(len(grads)):
        r = grads[a].shape[0] // 8
        for k in range(1, 8):
            px, py, pc = (1 - x if k & 4 else x, 1 - y if k & 2 else y, 1 - c if k & 1 else c)
            rows = pl.ds(pl.multiple_of((4 * px + 2 * py + pc) * r, r), r)
            copies.append(pltpu.make_async_remote_copy(
                src_ref=grads[a].at[rows], dst_ref=lands[a].at[me],
                send_sem=send_sems.at[7 * a + k - 1], recv_sem=recv_sems.at[7 * a + k - 1],
                device_id=(px, py, pc), device_id_type=MESH))
    return copies


def scatter_start(grads, name):
    n = len(grads)

    def body(*refs):
        srcs, lands = refs[:n], refs[n:2 * n]
        send_sems, recv_sems = refs[2 * n:2 * n + 2]
        token = refs[-1]
        for cp in _scatter_copies(srcs, lands, send_sems, recv_sems):
            cp.start()
        token[...] = jnp.zeros_like(token)

    lands = [jnp.zeros((8, g.shape[0] // 8, g.shape[1]), g.dtype) for g in grads]
    ops = [pltpu.with_memory_space_constraint(a, pltpu.HBM) for a in list(grads) + lands]
    out = pl.pallas_call(
        body, name=name,
        out_shape=[pltpu.SemaphoreType.DMA((7 * n,)), pltpu.SemaphoreType.DMA((7 * n,))]
        + [pltpu.HBM(o.shape, o.dtype) for o in ops] + [jax.ShapeDtypeStruct((8, 128), F32)],
        in_specs=[_HBM] * (2 * n),
        out_specs=[_SEM, _SEM] + [_HBM] * (2 * n) + [pl.BlockSpec(memory_space=pltpu.VMEM)],
        input_output_aliases={i: 2 + i for i in range(2 * n)},
        compiler_params=pltpu.CompilerParams(has_side_effects=pltpu.SideEffectType.DATAFLOW_SIDE_EFFECTING),
    )(*ops)
    return out[:-1], out[-1]


def scatter_wait(state, after, name):
    n = (len(state) - 2) // 2

    def body(*refs):
        srcs, lands = refs[:n], refs[n:2 * n]
        send_sems, recv_sems = refs[2 * n:2 * n + 2]
        for cp in _scatter_copies(srcs, lands, send_sems, recv_sems):
            cp.wait_send()
            cp.wait_recv()

    thru = list(state[2:])
    after = list(after) if isinstance(after, (list, tuple)) else [after]
    out = pl.pallas_call(
        body, name=name,
        out_shape=[pltpu.HBM(o.shape, o.dtype) for o in thru],
        in_specs=[_HBM] * (2 * n) + [_SEM, _SEM] + [pl.BlockSpec(memory_space=pl.ANY)] * len(after),
        out_specs=[_HBM] * (2 * n),
        input_output_aliases={i: i for i in range(2 * n)},
        compiler_params=pltpu.CompilerParams(has_side_effects=pltpu.SideEffectType.DATAFLOW_SIDE_EFFECTING),
    )(*thru, state[0], state[1], *after)
    return out[:n], out[n:]


def rs_win_add_first(g, r, sel, next_dim, col, name):
    rows, cols = r.shape[2:]

    def body(sel_ref, gk_ref, rk_ref, gs_ref, rs_ref, keep_ref, send_ref):
        keep_ref[...] = gk_ref[...] + rk_ref[...]
        send_ref[...] = (gs_ref[...] + rs_ref[...]).astype(BF16)

    def g_map(flip):
        def f(j, s):
            nxt = 1 - s[next_dim] if flip else s[next_dim]
            return (nxt, j, s[2], 0, col) if next_dim == 0 else (j, nxt, s[2], 0, col)
        return f

    def r_map(flip):
        def f(j, s):
            nxt = 1 - s[next_dim] if flip else s[next_dim]
            return (nxt, j, 0, 0) if next_dim == 0 else (j, nxt, 0, 0)
        return f

    gblk = (None, None, None, rows, cols)
    rblk = (None, None, rows, cols)
    oblk = (None, rows, cols)
    return pl.pallas_call(
        body, name=name,
        grid_spec=pltpu.PrefetchScalarGridSpec(
            num_scalar_prefetch=1, grid=(2,),
            in_specs=[pl.BlockSpec(gblk, g_map(False)), pl.BlockSpec(rblk, r_map(False)),
                      pl.BlockSpec(gblk, g_map(True)), pl.BlockSpec(rblk, r_map(True))],
            out_specs=[pl.BlockSpec(oblk, lambda j, s: (j, 0, 0)),
                       pl.BlockSpec(oblk, lambda j, s: (j, 0, 0))]),
        out_shape=[jax.ShapeDtypeStruct((2, rows, cols), F32),
                   jax.ShapeDtypeStruct((2, rows, cols), BF16)],
        compiler_params=_params(),
    )(sel, g, r, g, r)


def rs_add_second(k, r, sel, name):
    _, rows, cols = k.shape
    tr = rows // 2 if rows % 32 == 0 else rows
    nt = rows // tr

    def body(sel_ref, kk_ref, rk_ref, ks_ref, rs_ref, keep_ref, send_ref):
        keep_ref[...] = kk_ref[...] + rk_ref[...].astype(F32)
        send_ref[...] = (ks_ref[...] + rs_ref[...].astype(F32)).astype(BF16)

    blk = (None, tr, cols)
    oblk = (tr, cols)
    return pl.pallas_call(
        body, name=name,
        grid_spec=pltpu.PrefetchScalarGridSpec(
            num_scalar_prefetch=1, grid=(nt,),
            in_specs=[
                pl.BlockSpec(blk, lambda i, s: (s[0], i, 0)),
                pl.BlockSpec(blk, lambda i, s: (s[0], i, 0)),
                pl.BlockSpec(blk, lambda i, s: (1 - s[0], i, 0)),
                pl.BlockSpec(blk, lambda i, s: (1 - s[0], i, 0)),
            ],
            out_specs=[pl.BlockSpec(oblk, lambda i, s: (i, 0)),
                       pl.BlockSpec(oblk, lambda i, s: (i, 0))]),
        out_shape=[jax.ShapeDtypeStruct((rows, cols), F32),
                   jax.ShapeDtypeStruct((rows, cols), BF16)],
        compiler_params=_params(),
    )(sel, k, r, k, r)


SEG_ROWS = (4800, 5824, 6848, 4096, 0, 1024, 2048, 3072)
LAT_ROWS = QL + KVL + ROPE
N_IN = 7872


def _seg_row(j):
    return pl.multiple_of(jnp.where(j < 3, 4800 + 1024 * j, jnp.where(j == 3, 4096, (j - 4) * 1024)), 8)


def proj_matmul(h, wt_bits, token):
    t = h.shape[0]
    tm = min(1024, t)

    def body(h_ref, w_hbm, tok_ref, o_ref, wt_ref, buf, sems):
        j = pl.program_id(0)
        slot = j % 2

        def fetch(seg, into):
            return pltpu.make_async_copy(w_hbm.at[pl.ds(_seg_row(seg), D)], buf.at[into], sems.at[into])

        @pl.when(pl.program_id(1) == 0)
        def _():
            @pl.when(j == 0)
            def _():
                fetch(j, slot).start()

            fetch(j, slot).wait()

            @pl.when(j + 1 < NSEG)
            def _():
                fetch(j + 1, 1 - slot).start()

            bits = pltpu.bitcast(buf[slot], jnp.uint32)
            row = lax.broadcasted_iota(jnp.int32, (D, D // 2), 0)
            live = jnp.logical_or(j != SEG_LAT, row < LAT_ROWS)
            lo = pltpu.bitcast(bits << 16, F32)
            hi = pltpu.bitcast(bits & jnp.uint32(0xFFFF0000), F32)
            wt_ref[:, :D // 2] = jnp.where(live, lo, 0.0).astype(BF16)
            wt_ref[:, D // 2:] = jnp.where(live, hi, 0.0).astype(BF16)

        o_ref[...] = _dot_nt(h_ref[...], wt_ref[...]).astype(BF16)

    return pl.pallas_call(
        body, name="proj_matmul", grid=(NSEG, t // tm),
        in_specs=[pl.BlockSpec((tm, D), lambda j, i: (i, 0)),
                  pl.BlockSpec(memory_space=pl.ANY),
                  pl.BlockSpec((8, 128), lambda j, i: (0, 0))],
        out_specs=[pl.BlockSpec((None, tm, D), lambda j, i: (j, i, 0)),
                   pl.BlockSpec((D, D), lambda j, i: (j, 0))],
        out_shape=[jax.ShapeDtypeStruct((NSEG, t, D), BF16), jax.ShapeDtypeStruct((NP, D), BF16)],
        scratch_shapes=[pltpu.VMEM((2, D, D // 2), F32), pltpu.SemaphoreType.DMA((2,))],
        compiler_params=_params(("arbitrary", "arbitrary")),
    )(h, wt_bits, token)


def dh_matmul(dproj, wt, token, seq, b):
    tm = min(1024, seq)
    nblk = seq // tm

    def body(b_ref, d_ref, w_ref, tok_ref, o_ref, acc_ref):
        k = pl.program_id(1)

        @pl.when(k == 0)
        def _():
            acc_ref[...] = jnp.zeros_like(acc_ref)

        acc_ref[...] += _dot(d_ref[...], w_ref[...])

        @pl.when(k == NSEG - 1)
        def _():
            o_ref[...] = acc_ref[...]

    return pl.pallas_call(
        body, name="dh_matmul",
        grid_spec=pltpu.PrefetchScalarGridSpec(
            num_scalar_prefetch=1, grid=(nblk, NSEG),
            in_specs=[pl.BlockSpec((None, tm, D), lambda i, k, s: (k, s[0] * nblk + i, 0)),
                      pl.BlockSpec((D, D), lambda i, k, s: (k, 0)),
                      pl.BlockSpec((8, 128), lambda i, k, s: (0, 0))],
            out_specs=pl.BlockSpec((tm, D), lambda i, k, s: (i, 0)),
            scratch_shapes=[pltpu.VMEM((tm, D), F32)]),
        out_shape=jax.ShapeDtypeStruct((seq, D), F32),
        compiler_params=_params(("parallel", "arbitrary")),
    )(jnp.full((1,), b, jnp.int32), dproj, wt, token)


def win_grad_matmul(h, dproj, token):
    t = h.shape[0]
    tk = min(1024, t)
    nk = t // tk

    def body(h_ref, d_ref, tok_ref, o_hbm, acc_ref, sem):
        j = pl.program_id(0)
        k = pl.program_id(1)

        @pl.when(k == 0)
        def _():
            acc_ref[...] = jnp.zeros_like(acc_ref)

        acc_ref[...] += _dot_tn(d_ref[...], h_ref[...])

        @pl.when(jnp.logical_and(k == nk - 1, j != SEG_LAT))
        def _():
            cp = pltpu.make_async_copy(acc_ref, o_hbm.at[pl.ds(_seg_row(j), D)], sem)
            cp.start()
            cp.wait()

        @pl.when(jnp.logical_and(k == nk - 1, j == SEG_LAT))
        def _():
            cp = pltpu.make_async_copy(acc_ref.at[pl.ds(0, LAT_ROWS)],
                                       o_hbm.at[pl.ds(SEG_ROWS[SEG_LAT], LAT_ROWS)], sem)
            cp.start()
            cp.wait()

    return pl.pallas_call(
        body, name="win_grad_matmul", grid=(NSEG, nk),
        in_specs=[pl.BlockSpec((tk, D), lambda j, k: (k, 0)),
                  pl.BlockSpec((None, tk, D), lambda j, k: (j, k, 0)),
                  pl.BlockSpec((8, 128), lambda j, k: (0, 0))],
        out_specs=pl.BlockSpec(memory_space=pl.ANY),
        out_shape=jax.ShapeDtypeStruct((N_IN, D), F32),
        scratch_shapes=[pltpu.VMEM((D, D), F32), pltpu.SemaphoreType.DMA],
        compiler_params=_params(("arbitrary", "arbitrary")),
    )(h, dproj, token)


def grad_matmul(a, b, name):
    t, m = a.shape
    n = b.shape[1]
    tk = min(1024, t)
    nk = t // tk

    def body(a_ref, b_ref, o_ref, acc_ref):
        k = pl.program_id(0)

        @pl.when(k == 0)
        def _():
            acc_ref[...] = jnp.zeros_like(acc_ref)

        acc_ref[...] += _dot_tn(a_ref[...], b_ref[...])

        @pl.when(k == nk - 1)
        def _():
            o_ref[...] = acc_ref[...].astype(BF16)

    return pl.pallas_call(
        body, name=name, grid=(nk,),
        in_specs=[pl.BlockSpec((tk, m), lambda k: (k, 0)),
                  pl.BlockSpec((tk, n), lambda k: (k, 0))],
        out_specs=pl.BlockSpec((m, n), lambda k: (0, 0)),
        out_shape=jax.ShapeDtypeStruct((m, n), BF16),
        scratch_shapes=[pltpu.VMEM((m, n), F32)],
        compiler_params=_params(("arbitrary",)),
    )(a, b)


def ada_fwd(c_all, w_ada, b_cols):
    def body(c_ref, w_ref, b_ref, o_ref):
        o_ref[...] = _dot(c_ref[...].astype(BF16), w_ref[...].astype(BF16)) + b_ref[...]

    return pl.pallas_call(
        body, name="ada_fwd",
        out_shape=jax.ShapeDtypeStruct((c_all.shape[0], w_ada.shape[1]), F32),
        compiler_params=_params(),
    )(c_all, w_ada, b_cols)


def ada_bwd(c_all, dmod_cols):
    def body(c_ref, d_ref, o_ref):
        o_ref[...] = _dot_tn(c_ref[...].astype(BF16), d_ref[...].astype(BF16))

    return pl.pallas_call(
        body, name="ada_bwd",
        out_shape=jax.ShapeDtypeStruct((c_all.shape[1], dmod_cols.shape[1]), F32),
        compiler_params=_params(),
    )(c_all, dmod_cols)


def slot_sum(g):
    def body(g_ref, o_ref):
        acc = g_ref[0]
        for s in range(1, 8):
            acc = acc + g_ref[s]
        o_ref[...] = acc

    return pl.pallas_call(
        body, name="slot_sum",
        out_shape=jax.ShapeDtypeStruct(g.shape[1:], F32),
    )(g)


def prenorm_fwd(x2, scale, shift, g_pre, seq):
    t = x2.shape[0]
    tm = min(512, seq)
    tpb = seq // tm

    def body(x_ref, sc_ref, sh_ref, g_ref, h_ref):
        xv = x_ref[...]
        r = lax.rsqrt(jnp.mean(xv * xv, axis=-1, keepdims=True) + EPS)
        hv = (xv * r * g_ref[...]) * (1.0 + sc_ref[...]) + sh_ref[...]
        h_ref[...] = hv.astype(BF16)

    per_batch = pl.BlockSpec((None, 1, D), lambda i: (i // tpb, 0, 0))
    return pl.pallas_call(
        body, name="prenorm_fwd", grid=(t // tm,),
        in_specs=[pl.BlockSpec((tm, D), lambda i: (i, 0)), per_batch, per_batch,
                  pl.BlockSpec((1, D), lambda i: (0, 0))],
        out_specs=pl.BlockSpec((tm, D), lambda i: (i, 0)),
        out_shape=jax.ShapeDtypeStruct((t, D), BF16),
        compiler_params=_params(("parallel",)),
    )(x2, scale, shift, g_pre)


def prenorm_bwd(dh, x2, dout, scale, g_pre, seq, token, b, gx_prev):
    t = x2.shape[0]
    tm = min(512, seq)
    tpb = seq // tm
    if gx_prev is None:
        gx_prev = lax.empty((t, D), F32)

    def body(b_ref, dh_ref, x_ref, do_ref, sc_ref, g_ref, tok_ref, gxp_ref, gx_ref, dsh_ref, dsc_ref, dg_ref):
        i = pl.program_id(0)
        xv = x_ref[...]
        dhv = dh_ref[...]
        g = g_ref[...]
        r = lax.rsqrt(jnp.mean(xv * xv, axis=-1, keepdims=True) + EPS)
        nrm = xv * r
        dxn = dhv * (1.0 + sc_ref[...])
        dn = dxn * g
        dx = r * (dn - nrm * jnp.mean(dn * nrm, axis=-1, keepdims=True))
        gx_ref[...] = dx + do_ref[...]

        @pl.when(i == 0)
        def _():
            dsh_ref[...] = jnp.zeros_like(dsh_ref)
            dsc_ref[...] = jnp.zeros_like(dsc_ref)
            dg_ref[...] = jnp.zeros_like(dg_ref)

        dsh_ref[...] += jnp.sum(dhv, axis=0, keepdims=True)
        dsc_ref[...] += jnp.sum(dhv * (nrm * g), axis=0, keepdims=True)
        dg_ref[...] += jnp.sum(dxn * nrm, axis=0, keepdims=True)

    row = pl.BlockSpec((tm, D), lambda i, s: (i, 0))
    grow = pl.BlockSpec((tm, D), lambda i, s: (s[0] * tpb + i, 0))
    per_batch = pl.BlockSpec((None, 1, D), lambda i, s: (s[0], 0, 0))
    vec = pl.BlockSpec((1, D), lambda i, s: (0, 0))
    return pl.pallas_call(
        body, name="prenorm_bwd",
        grid_spec=pltpu.PrefetchScalarGridSpec(
            num_scalar_prefetch=1, grid=(tpb,),
            in_specs=[row, grow, grow, per_batch, vec, pl.BlockSpec((8, 128), lambda i, s: (0, 0)),
                      pl.BlockSpec(memory_space=pl.ANY)],
            out_specs=[grow, vec, vec, vec]),
        out_shape=[jax.ShapeDtypeStruct((t, D), F32), jax.ShapeDtypeStruct((1, D), F32),
                   jax.ShapeDtypeStruct((1, D), F32), jax.ShapeDtypeStruct((1, D), F32)],
        input_output_aliases={7: 0},
        compiler_params=_params(("arbitrary",)),
    )(jnp.full((1,), b, jnp.int32), dh, x2, dout, scale, g_pre, token, gx_prev)


CONV_TC = 128


def _shift_down(u, k, rows):
    idx = lax.broadcasted_iota(jnp.int32, u.shape, 0)
    return jnp.where(idx >= k, pltpu.roll(u, k, 0), 0.0)


def _shift_up(u, k, rows):
    idx = lax.broadcasted_iota(jnp.int32, u.shape, 0)
    return jnp.where(idx < rows - k, pltpu.roll(u, rows - k, 0), 0.0)


def conv_fwd(proj, conv_w, seq):
    t = proj.shape[1]
    nb = t // seq

    def body(p_ref, w_ref, y_ref):
        av = p_ref[0].astype(F32)
        ab = p_ref[1].astype(F32)
        ac = p_ref[2].astype(F32)
        az = p_ref[3].astype(F32)
        w = w_ref[...]
        u = ac * av
        y1 = _shift_down(u, 2, seq) * w[0:1] + _shift_down(u, 1, seq) * w[1:2] + u * w[2:3]
        y_ref[...] = (ab * y1 * (az * _sig(az))).astype(BF16)

    return pl.pallas_call(
        body, name="conv_fwd", grid=(nb, D // CONV_TC),
        in_specs=[pl.BlockSpec((4, seq, CONV_TC), lambda b, ci: (1, b, ci)),
                  pl.BlockSpec((8, CONV_TC), lambda b, ci: (0, ci))],
        out_specs=pl.BlockSpec((seq, CONV_TC), lambda b, ci: (b, ci)),
        out_shape=jax.ShapeDtypeStruct((t, D), BF16),
        compiler_params=_params(("parallel", "parallel")),
    )(proj, conv_w)


def conv_bwd(dproj, proj, dy, conv_w, seq):
    t = proj.shape[1]
    nb = t // seq

    def body(dp_in_ref, p_ref, dy_ref, w_ref, dp_ref, dw_ref):
        b = pl.program_id(1)
        av = p_ref[0].astype(F32)
        ab = p_ref[1].astype(F32)
        ac = p_ref[2].astype(F32)
        az = p_ref[3].astype(F32)
        dyv = dy_ref[...].astype(F32)
        w = w_ref[...]
        u = ac * av
        u1 = _shift_down(u, 1, seq)
        u2 = _shift_down(u, 2, seq)
        y1 = u2 * w[0:1] + u1 * w[1:2] + u * w[2:3]
        sz = _sig(az)
        silu = az * sz
        dy1 = dyv * ab * silu
        du = dy1 * w[2:3] + _shift_up(dy1, 1, seq) * w[1:2] + _shift_up(dy1, 2, seq) * w[0:1]
        dp_ref[0] = (du * ac).astype(BF16)
        dp_ref[1] = (dyv * y1 * silu).astype(BF16)
        dp_ref[2] = (du * av).astype(BF16)
        dp_ref[3] = (dyv * ab * y1 * (sz * (1.0 + az * (1.0 - sz)))).astype(BF16)

        @pl.when(b == 0)
        def _():
            dw_ref[...] = jnp.zeros_like(dw_ref)

        dw_ref[0:1, :] += jnp.sum(dy1 * u2, axis=0, keepdims=True)
        dw_ref[1:2, :] += jnp.sum(dy1 * u1, axis=0, keepdims=True)
        dw_ref[2:3, :] += jnp.sum(dy1 * u, axis=0, keepdims=True)

    return pl.pallas_call(
        body, name="conv_bwd", grid=(D // CONV_TC, nb),
        in_specs=[pl.BlockSpec(memory_space=pl.ANY),
                  pl.BlockSpec((4, seq, CONV_TC), lambda ci, b: (1, b, ci)),
                  pl.BlockSpec((seq, CONV_TC), lambda ci, b: (b, ci)),
                  pl.BlockSpec((8, CONV_TC), lambda ci, b: (0, ci))],
        out_specs=[pl.BlockSpec((4, seq, CONV_TC), lambda ci, b: (1, b, ci)),
                   pl.BlockSpec((8, CONV_TC), lambda ci, b: (0, ci))],
        out_shape=[jax.ShapeDtypeStruct(dproj.shape, BF16),
                   jax.ShapeDtypeStruct((8, D), F32)],
        input_output_aliases={0: 0},
        compiler_params=_params(("parallel", "arbitrary")),
    )(dproj, proj, dy, conv_w)


def _rope_tables(pos_ref, invf_ref, ma_ref, mb_ref, sign):
    ang = pos_ref[...].astype(F32) * invf_ref[...]
    cs = jnp.cos(ang)
    sn = jnp.sin(ang) * sign
    return cs, sn * ma_ref[...], sn * mb_ref[...]


def _rotate(v, cs, sa, sb):
    return v * cs + pltpu.roll(v, 128 - HALF, 1) * sa + pltpu.roll(v, HALF, 1) * sb


MLA_TM = 256


def mla_prep_fwd(proj, pos, g_q, g_kv, wuq, wukv, tabs):
    t = proj.shape[1]
    tm = min(MLA_TM, t)

    def body(lat_ref, pos_ref, gq_ref, gkv_ref, wuq_ref, wukv_ref, invf_ref, ma_ref, mb_ref,
             q_ref, k_ref, kv_ref, qn_ref, kvn_ref):
        lat = lat_ref[...].astype(F32)
        ql = lat[:, :QL]
        kl = lat[:, QL:QL + KVL]
        kr = lat[:, QL + KVL:QL + KVL + 128]
        qn = (ql * lax.rsqrt(jnp.mean(ql * ql, axis=-1, keepdims=True) + EPS) * gq_ref[...]).astype(BF16)
        kvn = (kl * lax.rsqrt(jnp.mean(kl * kl, axis=-1, keepdims=True) + EPS) * gkv_ref[...]).astype(BF16)
        qn_ref[...] = qn
        kvn_ref[...] = kvn
        cs, sa, sb = _rope_tables(pos_ref, invf_ref, ma_ref, mb_ref, 1.0)
        q = _dot_nt(qn, wuq_ref[...]) * (SM_SCALE * LOG2E)
        kv = _dot_nt(kvn, wukv_ref[...]).astype(BF16)
        kv_ref[...] = kv
        kpe = _rotate(kr, cs, sa, sb).astype(BF16)
        for hh in range(H):
            lo, mid, hi = hh * DQK, hh * DQK + 128, (hh + 1) * DQK
            q_ref[:, lo:mid] = q[:, lo:mid].astype(BF16)
            q_ref[:, mid:hi] = _rotate(q[:, mid:hi], cs, sa, sb).astype(BF16)
            k_ref[:, lo:mid] = kv[:, lo:mid]
            k_ref[:, mid:hi] = kpe

    row = lambda w: pl.BlockSpec((tm, w), lambda i: (i, 0))
    const = lambda a: pl.BlockSpec(a.shape, lambda i: (0,) * a.ndim)
    return pl.pallas_call(
        body, name="mla_prep_fwd", grid=(t // tm,),
        in_specs=[pl.BlockSpec((None, tm, D), lambda i: (SEG_LAT, i, 0)), row(1),
                  const(g_q), const(g_kv), const(wuq), const(wukv)] + [const(a) for a in tabs],
        out_specs=[row(H * DQK), row(H * DQK), row(H * DQK), row(QL), row(KVL)],
        out_shape=[jax.ShapeDtypeStruct((t, H * DQK), BF16)] * 3
        + [jax.ShapeDtypeStruct((t, QL), BF16), jax.ShapeDtypeStruct((t, KVL), BF16)],
        compiler_params=_params(("parallel",)),
    )(proj, pos, g_q, g_kv, wuq, wukv, *tabs)


def mla_prep_bwd(dproj, proj, dq_rot, dk, dv, pos, g_q, g_kv, wuq, wukv, tabs):
    t = proj.shape[1]
    tm = min(MLA_TM, t)

    def body(dp_in_ref, lat_ref, dqr_ref, dk_ref, dv_ref, pos_ref, gq_ref, gkv_ref, wuq_ref, wukv_ref,
             invf_ref, ma_ref, mb_ref, dp_ref, dq_ref, dkv_ref, dgq_ref, dgkv_ref):
        i = pl.program_id(0)
        lat = lat_ref[...].astype(F32)
        ql = lat[:, :QL]
        kl = lat[:, QL:QL + KVL]
        rq = lax.rsqrt(jnp.mean(ql * ql, axis=-1, keepdims=True) + EPS)
        rk = lax.rsqrt(jnp.mean(kl * kl, axis=-1, keepdims=True) + EPS)
        nq = ql * rq
        nk = kl * rk
        cs, sa, sb = _rope_tables(pos_ref, invf_ref, ma_ref, mb_ref, -1.0)
        dkpe = jnp.zeros((tm, 128), F32)
        for hh in range(H):
            lo, mid, hi = hh * DQK, hh * DQK + 128, (hh + 1) * DQK
            dq_ref[:, lo:mid] = (dqr_ref[:, lo:mid] * SM_SCALE).astype(BF16)
            dq_ref[:, mid:hi] = _rotate(dqr_ref[:, mid:hi] * SM_SCALE, cs, sa, sb).astype(BF16)
            dkv_ref[:, lo:mid] = dk_ref[:, lo:mid]
            dkv_ref[:, mid:hi] = dv_ref[:, hh * DV:(hh + 1) * DV]
            dkpe = dkpe + dk_ref[:, mid:hi].astype(F32)
        lane = lax.broadcasted_iota(jnp.int32, (tm, 128), 1)
        dkr = jnp.where(lane < ROPE, _rotate(dkpe, cs, sa, sb), 0.0)
        dqn = _dot(dq_ref[...], wuq_ref[...])
        dkvn = _dot(dkv_ref[...], wukv_ref[...])
        gq = gq_ref[...]
        gkv = gkv_ref[...]
        dnq = dqn * gq
        dnk = dkvn * gkv
        dql = rq * (dnq - nq * jnp.mean(dnq * nq, axis=-1, keepdims=True))
        dkl = rk * (dnk - nk * jnp.mean(dnk * nk, axis=-1, keepdims=True))
        dp_ref[:, :QL] = dql.astype(BF16)
        dp_ref[:, QL:QL + KVL] = dkl.astype(BF16)
        dp_ref[:, QL + KVL:QL + KVL + 128] = dkr.astype(BF16)
        dp_ref[:, QL + KVL + 128:] = jnp.zeros((tm, D - QL - KVL - 128), BF16)

        @pl.when(i == 0)
        def _():
            dgq_ref[...] = jnp.zeros_like(dgq_ref)
            dgkv_ref[...] = jnp.zeros_like(dgkv_ref)

        dgq_ref[...] += jnp.sum(dqn * nq, axis=0, keepdims=True)
        dgkv_ref[...] += jnp.sum(dkvn * nk, axis=0, keepdims=True)

    row = lambda w: pl.BlockSpec((tm, w), lambda i: (i, 0))
    const = lambda a: pl.BlockSpec(a.shape, lambda i: (0,) * a.ndim)
    seg = pl.BlockSpec((None, tm, D), lambda i: (SEG_LAT, i, 0))
    return pl.pallas_call(
        body, name="mla_prep_bwd", grid=(t // tm,),
        in_specs=[pl.BlockSpec(memory_space=pl.ANY), seg, row(H * DQK), row(H * DQK), row(H * DV), row(1),
                  const(g_q), const(g_kv), const(wuq), const(wukv)] + [const(a) for a in tabs],
        out_specs=[seg, row(H * DQK), row(H * DQK),
                   pl.BlockSpec((1, QL), lambda i: (0, 0)), pl.BlockSpec((1, KVL), lambda i: (0, 0))],
        out_shape=[jax.ShapeDtypeStruct(dproj.shape, BF16),
                   jax.ShapeDtypeStruct((t, H * DQK), BF16), jax.ShapeDtypeStruct((t, H * DQK), BF16),
                   jax.ShapeDtypeStruct((1, QL), F32), jax.ShapeDtypeStruct((1, KVL), F32)],
        input_output_aliases={0: 0},
        compiler_params=_params(("arbitrary",)),
    )(dproj, proj, dq_rot, dk, dv, pos, g_q, g_kv, wuq, wukv, *tabs)


def _causal_mask(s, n):
    row = lax.broadcasted_iota(jnp.int32, (n, n), 0)
    col = lax.broadcasted_iota(jnp.int32, (n, n), 1)
    return jnp.where(col <= row, s, -1e30)


def flash_fwd(q, k, kv, nb, seq):
    t = q.shape[0]
    tq = min(FLASH_TQ, seq)
    nq = seq // tq

    def body(q_ref, k_ref, v_ref, o_ref, lse_ref):
        for qi in range(nq):
            qs = slice(qi * tq, (qi + 1) * tq)
            qv = q_ref[qs, :]
            m = jnp.full((tq, 1), -1e30, F32)
            l = jnp.zeros((tq, 1), F32)
            acc = jnp.zeros((tq, DV), F32)
            for j in range(qi + 1):
                ks = slice(j * tq, (j + 1) * tq)
                s = _dot_nt(qv, k_ref[ks, :])
                if j == qi:
                    s = _causal_mask(s, tq)
                m_new = jnp.maximum(m, jnp.max(s, axis=1, keepdims=True))
                p = jnp.exp2(s - m_new)
                alpha = jnp.exp2(m - m_new)
                l = alpha * l + jnp.sum(p, axis=1, keepdims=True)
                acc = alpha * acc + _dot(p.astype(BF16), v_ref[ks, :])
                m = m_new
            o_ref[qs, :] = (acc / l).astype(BF16)
            lse_ref[qs, :] = jnp.broadcast_to(m + jnp.log(l) * LOG2E, (tq, DV))

    out_blk = pl.BlockSpec((seq, DV), lambda b, h: (b, h))
    return pl.pallas_call(
        body, name="flash_fwd", grid=(nb, H),
        in_specs=[pl.BlockSpec((seq, DQK), lambda b, h: (b, h)),
                  pl.BlockSpec((seq, DQK), lambda b, h: (b, h)),
                  pl.BlockSpec((seq, DV), lambda b, h: (b, 2 * h + 1))],
        out_specs=[out_blk, out_blk],
        out_shape=[jax.ShapeDtypeStruct((t, H * DV), BF16), jax.ShapeDtypeStruct((t, H * DV), F32)],
        compiler_params=_params(("parallel", "parallel")),
    )(q, k, kv)


def flash_bwd(q, k, kv, o, do, lse, nb, seq, token):
    t = q.shape[0]
    tq = min(FLASH_TQ, seq)
    nq = seq // tq

    def body(q_ref, k_ref, v_ref, o_ref, do_ref, lse_ref, tok_ref, dq_ref, dk_ref, dv_ref):
        delta = []
        for qi in range(nq):
            qs = slice(qi * tq, (qi + 1) * tq)
            delta.append(jnp.sum(do_ref[qs, :].astype(F32) * o_ref[qs, :].astype(F32), axis=1, keepdims=True))
        for ki in range(nq):
            ks = slice(ki * tq, (ki + 1) * tq)
            kb = k_ref[ks, :]
            vb = v_ref[ks, :]
            dk = jnp.zeros((tq, DQK), F32)
            dv = jnp.zeros((tq, DV), F32)
            for qi in range(ki, nq):
                qs = slice(qi * tq, (qi + 1) * tq)
                qv = q_ref[qs, :]
                dov = do_ref[qs, :]
                s = _dot_nt(qv, kb)
                if qi == ki:
                    s = _causal_mask(s, tq)
                p = jnp.exp2(s - lse_ref[qs, :][:, :1])
                dp = _dot_nt(dov, vb)
                dz = (p * (dp - delta[qi])).astype(BF16)
                dv = dv + _dot_tn(p.astype(BF16), dov)
                dk = dk + _dot_tn(dz, qv)
                dqb = _dot(dz, kb)
                if ki == 0:
                    dq_ref[qs, :] = dqb
                else:
                    dq_ref[qs, :] += dqb
            dk_ref[ks, :] = (dk * LN2).astype(BF16)
            dv_ref[ks, :] = dv.astype(BF16)

    full = lambda w, col: pl.BlockSpec((seq, w), col)
    same = lambda b, h: (b, h)
    return pl.pallas_call(
        body, name="flash_bwd", grid=(nb, H),
        in_specs=[full(DQK, same), full(DQK, same), full(DV, lambda b, h: (b, 2 * h + 1)),
                  full(DV, same), full(DV, same), full(DV, same),
                  pl.BlockSpec((8, 128), lambda b, h: (0, 0))],
        out_specs=[full(DQK, same), full(DQK, same), full(DV, same)],
        out_shape=[jax.ShapeDtypeStruct((t, H * DQK), F32), jax.ShapeDtypeStruct((t, H * DQK), BF16),
                   jax.ShapeDtypeStruct((t, H * DV), BF16)],
        compiler_params=_params(("parallel", "parallel")),
    )(q, k, kv, o, do, lse, token)


TAIL_TM = 256


def tail_fwd(y, attn, proj, x2, tgt, gate, g_post, wco, wmo, wout, seq):
    t = y.shape[0]
    nb = t // seq
    tm = min(TAIL_TM, seq)
    tpb = seq // tm

    def body(y_ref, at_ref, p_ref, x_ref, t_ref, gate_ref, gp_ref, wco_ref, wmo_ref, wout_ref,
             o_ref, ya_ref, yb_ref, m_ref, do2_ref, dout_ref, dgate_ref, dgp_ref, loss_ref):
        i = pl.program_id(0)
        bz = p_ref[0].astype(F32)
        ga = p_ref[1].astype(F32)
        gb = p_ref[2].astype(F32)
        ov = (at_ref[...].astype(F32) * (bz * _sig(bz))).astype(BF16)
        o_ref[...] = ov
        ya = _dot(y_ref[...], wco_ref[...])
        yb = _dot(ov, wmo_ref[...])
        ya_ref[...] = ya.astype(BF16)
        yb_ref[...] = yb.astype(BF16)
        mv = (_sig(ga) * ya + _sig(gb) * yb).astype(BF16)
        m_ref[...] = mv
        o2 = _dot(mv, wout_ref[...])
        r = lax.rsqrt(jnp.mean(o2 * o2, axis=-1, keepdims=True) + EPS)
        nrm = o2 * r
        gp = gp_ref[...]
        gate_v = gate_ref[...]
        rn = nrm * gp
        err = x_ref[...] + gate_v * rn - t_ref[...]
        dout = err * (1.0 / D)
        dout_ref[...] = dout
        dn = dout * gate_v * gp
        do2_ref[...] = (r * (dn - nrm * jnp.mean(dn * nrm, axis=-1, keepdims=True))).astype(BF16)

        @pl.when(i % tpb == 0)
        def _():
            dgate_ref[...] = jnp.zeros_like(dgate_ref)

        @pl.when(i == 0)
        def _():
            dgp_ref[...] = jnp.zeros_like(dgp_ref)
            loss_ref[...] = jnp.zeros_like(loss_ref)

        dgate_ref[...] += jnp.sum(dout * rn, axis=0, keepdims=True)
        dgp_ref[...] += jnp.sum(dout * gate_v * nrm, axis=0, keepdims=True)
        loss_ref[...] += 0.5 * jnp.sum(jnp.mean(err * err, axis=-1, keepdims=True), axis=0, keepdims=True)

    row = pl.BlockSpec((tm, D), lambda i: (i, 0))
    per_batch = pl.BlockSpec((None, 1, D), lambda i: (i // tpb, 0, 0))
    vec = pl.BlockSpec((1, D), lambda i: (0, 0))
    wgt = pl.BlockSpec((D, D), lambda i: (0, 0))
    act = jax.ShapeDtypeStruct((t, D), BF16)
    return pl.pallas_call(
        body, name="tail_fwd", grid=(t // tm,),
        in_specs=[row, row, pl.BlockSpec((3, tm, D), lambda i: (0, i, 0)), row, row, per_batch, vec,
                  wgt, wgt, wgt],
        out_specs=[row, row, row, row, row, row, per_batch, vec, pl.BlockSpec((1, 1), lambda i: (0, 0))],
        out_shape=[act, act, act, act, act, jax.ShapeDtypeStruct((t, D), F32),
                   jax.ShapeDtypeStruct((nb, 1, D), F32), jax.ShapeDtypeStruct((1, D), F32),
                   jax.ShapeDtypeStruct((1, 1), F32)],
        compiler_params=_params(("arbitrary",)),
    )(y, attn, proj, x2, tgt, gate, g_post, wco, wmo, wout)


def tail_bwd(do2, proj, ya, yb, attn, wout, wmo, wco):
    t = do2.shape[0]
    tm = min(TAIL_TM, t)

    def body(do2_ref, p_ref, ya_ref, yb_ref, at_ref, wout_ref, wmo_ref, wco_ref,
             dp_ref, dya_ref, dyb_ref, dat_ref, dy_ref):
        bz = p_ref[0].astype(F32)
        ga = p_ref[1].astype(F32)
        gb = p_ref[2].astype(F32)
        dm = _dot_nt(do2_ref[...], wout_ref[...])
        sa = _sig(ga)
        sb = _sig(gb)
        dya = (dm * sa).astype(BF16)
        dyb = (dm * sb).astype(BF16)
        dya_ref[...] = dya
        dyb_ref[...] = dyb
        dp_ref[1] = (dm * ya_ref[...].astype(F32) * (sa * (1.0 - sa))).astype(BF16)
        dp_ref[2] = (dm * yb_ref[...].astype(F32) * (sb * (1.0 - sb))).astype(BF16)
        dov = _dot_nt(dyb, wmo_ref[...])
        sz = _sig(bz)
        dat_ref[...] = (dov * (bz * sz)).astype(BF16)
        dp_ref[0] = (dov * at_ref[...].astype(F32) * (sz * (1.0 + bz * (1.0 - sz)))).astype(BF16)
        dy_ref[...] = _dot_nt(dya, wco_ref[...]).astype(BF16)

    row = pl.BlockSpec((tm, D), lambda i: (i, 0))
    seg3 = pl.BlockSpec((3, tm, D), lambda i: (0, i, 0))
    wgt = pl.BlockSpec((D, D), lambda i: (0, 0))
    act = jax.ShapeDtypeStruct((t, D), BF16)
    return pl.pallas_call(
        body, name="tail_bwd", grid=(t // tm,),
        in_specs=[row, seg3, row, row, row, wgt, wgt, wgt],
        out_specs=[seg3, row, row, row, row],
        out_shape=[jax.ShapeDtypeStruct((NSEG, t, D), BF16), act, act, act, act],
        compiler_params=_params(("parallel",)),
    )(do2, proj, ya, yb, attn, wout, wmo, wco)


def adamw(w, m, v, g, g2, name, token=None):
    rows, cols = w.shape
    tr = rows
    for cand in (256, 128, 64, 32, 16, 8):
        if rows % cand == 0 and rows > cand:
            tr = cand
            break
    has2 = g2 is not None
    n_in = 4 + has2

    def body(*refs):
        w_ref, m_ref, v_ref, g_ref = refs[:4]
        go_ref, d_ref, mo_ref, vo_ref = refs[-4:]
        grad = g_ref[...] + refs[4][...].astype(F32) if has2 else g_ref[...]
        mn = ADAM_B1 * m_ref[...] + (1.0 - ADAM_B1) * grad
        vn = ADAM_B2 * v_ref[...] + (1.0 - ADAM_B2) * (grad * grad)
        m_hat = mn / (1.0 - ADAM_B1 ** ADAM_STEP)
        v_hat = vn / (1.0 - ADAM_B2 ** ADAM_STEP)
        go_ref[...] = grad
        d_ref[...] = -ADAM_LR * (m_hat / (jnp.sqrt(v_hat) + ADAM_EPS) + ADAM_WD * w_ref[...])
        mo_ref[...] = mn
        vo_ref[...] = vn

    blk = pl.BlockSpec((tr, cols), lambda i: (i, 0))
    ins = [w, m, v, g] + ([g2] if has2 else [])
    specs = [blk] * n_in
    if token is not None:
        ins.append(token)
        specs.append(pl.BlockSpec((8, 128), lambda i: (0, 0)))
    return pl.pallas_call(
        body, name=name, grid=(rows // tr,),
        in_specs=specs, out_specs=[blk] * 4,
        out_shape=[jax.ShapeDtypeStruct((rows, cols), F32)] * 4,
        compiler_params=_params(("parallel",)),
    )(*ins)


def adamw_scattered(w, m, v, own, land, me, tr, name, transpose=False):
    slot_rows = land.shape[1]
    cols = land.shape[2]
    rows = slot_rows if transpose else w.shape[0]
    per_slot = slot_rows // tr

    def body(me_ref, w_ref, m_ref, v_ref, own_ref, land_ref, go_ref, d_ref, mo_ref, vo_ref):
        grad = own_ref[...].astype(F32)
        for s in range(8):
            grad = grad + land_ref[s].astype(F32)
        if transpose:
            grad = grad.T
        mn = ADAM_B1 * m_ref[...] + (1.0 - ADAM_B1) * grad
        vn = ADAM_B2 * v_ref[...] + (1.0 - ADAM_B2) * (grad * grad)
        m_hat = mn / (1.0 - ADAM_B1 ** ADAM_STEP)
        v_hat = vn / (1.0 - ADAM_B2 ** ADAM_STEP)
        go_ref[...] = grad
        d_ref[...] = -ADAM_LR * (m_hat / (jnp.sqrt(v_hat) + ADAM_EPS) + ADAM_WD * w_ref[...])
        mo_ref[...] = mn
        vo_ref[...] = vn

    wblk = pl.BlockSpec(w.shape if transpose else (tr, w.shape[1]), lambda i, s: (i, 0))
    return pl.pallas_call(
        body, name=name,
        grid_spec=pltpu.PrefetchScalarGridSpec(
            num_scalar_prefetch=1, grid=(rows // tr,),
            in_specs=[wblk, wblk, wblk,
                      pl.BlockSpec((tr, cols), lambda i, s: (s[0] * per_slot + i, 0)),
                      pl.BlockSpec((8, tr, cols), lambda i, s: (0, i, 0))],
            out_specs=[wblk] * 4),
        out_shape=[jax.ShapeDtypeStruct(w.shape, F32)] * 4,
        compiler_params=_params(),
    )(me, w, m, v, own, land)


def adamw_win(wt, mt, vt, ka, ra, kb, rb):
    rows = wt.shape[0]
    tc = 256
    nh = (D // 2) // tc

    def body(w_ref, m_ref, v_ref, ka_ref, ra_ref, kb_ref, rb_ref, go_ref, d_ref, mo_ref, vo_ref):
        first = pl.program_id(0) < nh
        grad = jnp.where(first, ka_ref[...] + ra_ref[...].astype(F32), kb_ref[...] + rb_ref[...].astype(F32))
        mn = ADAM_B1 * m_ref[...] + (1.0 - ADAM_B1) * grad
        vn = ADAM_B2 * v_ref[...] + (1.0 - ADAM_B2) * (grad * grad)
        m_hat = mn / (1.0 - ADAM_B1 ** ADAM_STEP)
        v_hat = vn / (1.0 - ADAM_B2 ** ADAM_STEP)
        go_ref[...] = grad
        d_ref[...] = -ADAM_LR * (m_hat / (jnp.sqrt(v_hat) + ADAM_EPS) + ADAM_WD * w_ref[...])
        mo_ref[...] = mn
        vo_ref[...] = vn

    blk = pl.BlockSpec((rows, tc), lambda j: (0, j))
    lo = pl.BlockSpec((rows, tc), lambda j: (0, jnp.minimum(j, nh - 1)))
    hi = pl.BlockSpec((rows, tc), lambda j: (0, jnp.maximum(j - nh, 0)))
    return pl.pallas_call(
        body, name="adamw_w_in", grid=(D // tc,),
        in_specs=[blk, blk, blk, lo, lo, hi, hi], out_specs=[blk] * 4,
        out_shape=[jax.ShapeDtypeStruct((rows, D), F32)] * 4,
        compiler_params=_params(("parallel",)),
    )(wt, mt, vt, ka, ra, kb, rb)


_ORD_A = ("x", "y", "c")
_ORD_B = ("y", "x", "c")


def _rows128(a, rows):
    flat = a.reshape(-1)
    return jnp.pad(flat, (0, rows * 128 - flat.shape[0])).reshape(rows, 128)


def kernel(x, c, positions, w_ada, b_ada, g_pre, w_in, conv_w, w_conv_out, g_q, w_uq, g_kv, w_ukv, w_mla_out, w_out, g_post, loss_target, m_w_ada, m_b_ada, m_g_pre, m_w_in, m_conv_w, m_w_conv_out, m_g_q, m_w_uq, m_g_kv, m_w_ukv, m_w_mla_out, m_w_out, m_g_post, v_w_ada, v_b_ada, v_g_pre, v_w_in, v_conv_w, v_w_conv_out, v_g_q, v_w_uq, v_g_kv, v_w_ukv, v_w_mla_out, v_w_out, v_g_post):
    nb, seq, _ = x.shape
    t = nb * seq
    mx, my, mc = lax.axis_index("x"), lax.axis_index("y"), lax.axis_index("c")
    me = 4 * mx + 2 * my + mc
    co = {"x": mx, "y": my, "c": mc}

    x2 = x.reshape(t, D)
    tgt2 = loss_target.reshape(t, D)
    pos2 = positions.reshape(t, 1)

    packed = jnp.concatenate([c.reshape(2 * D // 128, 128), _rows128(conv_w[0], 8)], axis=0)
    gath = small_allgather(packed, "gather_cond")
    c_all = gath[:, :16].reshape(8 * nb, D)
    conv_full = gath[:, 16:19].reshape(8, 3, 128).transpose(1, 0, 2).reshape(3, D)
    conv_full8 = jnp.pad(conv_full, ((0, 5), (0, 0)))
    ada_cols = w_ada.shape[2]
    b_cols = lax.dynamic_slice(b_ada, (0, me * ada_cols), (1, ada_cols))
    mod_part = ada_fwd(c_all, w_ada[0], b_cols)
    mod_g = small_allgather(mod_part.reshape(8 * nb * ada_cols // 128, 128), "gather_mod")
    mod_all = mod_g.reshape(8, 8 * nb, ada_cols).transpose(1, 0, 2).reshape(8 * nb, 8 * ada_cols)
    mod = lax.dynamic_slice(mod_all, (me * nb, 0), (nb, 3 * D))
    shift = mod[:, 0:D].reshape(nb, 1, D)
    scale = mod[:, D:2 * D].reshape(nb, 1, D)
    gate = mod[:, 2 * D:3 * D].reshape(nb, 1, D)

    wt = w_in[0].T.astype(BF16)
    lo = lax.bitcast_convert_type(wt[:, :D // 2], jnp.uint16).astype(jnp.uint32)
    hi = lax.bitcast_convert_type(wt[:, D // 2:], jnp.uint16).astype(jnp.uint32)
    wt_bits = lax.bitcast_convert_type(lo | (hi << 16), F32)
    wt_bits, mod = lax.optimization_barrier((wt_bits, mod))
    shift = mod[:, 0:D].reshape(nb, 1, D)
    scale = mod[:, D:2 * D].reshape(nb, 1, D)
    gate = mod[:, 2 * D:3 * D].reshape(nb, 1, D)
    q4 = D // 4
    r3rd = wt_bits.shape[0] // 3
    plan = [(0, (k * r3rd, r3rd), (g * q4, q4), (_ORD_A, _ORD_B)[g]) for k in range(3) for g in range(2)]
    gw = allgather_big([wt_bits], plan, "gather_w_in")
    late = [w_conv_out[0].astype(BF16), w_mla_out[0].astype(BF16), w_out[0].astype(BF16),
            jnp.pad(w_uq[0].T.astype(BF16), ((0, DQK - 192), (0, 0))), w_ukv[0].T.astype(BF16)]
    gw0, late = lax.optimization_barrier((gw[0], late))
    late_state, late_token = gather_start(late, "gather_late_start")
    wt_bits_all = gw0.reshape(N_IN, D // 2)

    inv_freq = ROPE_THETA ** (-jnp.arange(0, ROPE, 2, dtype=F32) / ROPE)
    invf = jnp.concatenate([inv_freq, inv_freq, jnp.zeros((128 - ROPE,), F32)]).reshape(1, 128)
    lane = np.arange(128)
    tabs = (invf,
            jnp.asarray(np.where(lane < HALF, -1.0, 0.0).reshape(1, 128), F32),
            jnp.asarray(np.where((lane >= HALF) & (lane < ROPE), 1.0, 0.0).reshape(1, 128), F32))

    h = prenorm_fwd(x2, scale, shift, g_pre, seq)
    proj, wt_p = proj_matmul(h, wt_bits_all, late_token)
    y = conv_fwd(proj, conv_full8, seq)
    gl = gather_wait(late_state, y, "gather_late_wait")
    wco = gl[0].reshape(D, D)
    wmo = gl[1].reshape(D, D)
    wout = gl[2].reshape(D, D)
    wuq_p = gl[3].reshape(H * DQK, QL)
    wukv = gl[4].reshape(H * 256, KVL)
    q_rot, k_cat, kv, qn, kvn = mla_prep_fwd(proj, pos2, g_q, g_kv, wuq_p, wukv, tabs)
    attn, lse = flash_fwd(q_rot, k_cat, kv, nb, seq)
    o, ya, yb, m, do2, dout, dgate, dg_post, loss_part = tail_fwd(
        y, attn, proj, x2, tgt2, gate, g_post, wco, wmo, wout, seq)

    dproj, dya, dyb, dattn, dy = tail_bwd(do2, proj, ya, yb, attn, wout, wmo, wco)
    g_wout = grad_matmul(m, do2, "grad_w_square")
    g_wmo = grad_matmul(o, dyb, "grad_w_square")
    g_wco = grad_matmul(y, dya, "grad_w_square")
    sc1, sc1_tok = scatter_start([g_wco, g_wmo, g_wout], "scatter_out_grads_start")
    dproj, dconv = conv_bwd(dproj, proj, dy, conv_full8, seq)
    dq_rot, dk, dv = flash_bwd(q_rot, k_cat, kv, attn, dattn, lse, nb, seq, sc1_tok)
    dproj, dq, dkv, dg_q, dg_kv = mla_prep_bwd(dproj, proj, dq_rot, dk, dv, pos2, g_q, g_kv, wuq_p, wukv, tabs)
    g_wuq_t = grad_matmul(dq, qn, "grad_w_uq")
    g_wukv_t = grad_matmul(dkv, kvn, "grad_w_ukv")
    sc2, sc2_tok = scatter_start([g_wuq_t, g_wukv_t], "scatter_mla_grads_start")
    g_win_p = win_grad_matmul(h, dproj, sc2_tok)

    g_wt = g_win_p.reshape(2, 2, 2, N_IN // 8, D)
    ords = [("c", "y", "x"), ("c", "x", "y")]
    hc = D // 2
    win_shape = (2, 2, N_IN // 8, hc)
    pick_w = lambda col: (lambda ref, cc: ref.at[:, :, 1 - cc["c"], :, pl.ds(col * hc, hc)])
    which1 = [0, 0]
    picks1 = [pick_w(0), pick_w(1)]
    st1, tok1 = swap_start([g_wt], which1, ["c"] * 2, picks1, [win_shape] * 2, "rs_c_start")
    assert nb == 2
    dh0 = dh_matmul(dproj, wt_p, tok1, seq, 0)
    (g_wt,), r1 = swap_wait(st1, dh0, which1, ["c"] * 2, picks1, "rs_c_wait")
    sel_xyc = jnp.stack([mx, my, mc]).astype(jnp.int32)
    sel2 = [jnp.stack([co[o[2]]]).astype(jnp.int32) for o in ords]
    first = [rs_win_add_first(g_wt, r1[0], sel_xyc, 1, 0, "rs_add_first_0"),
             rs_win_add_first(g_wt, r1[1], sel_xyc, 0, 1, "rs_add_first_1")]
    keep1, send1 = zip(*first)
    all4 = [0, 1]
    none4 = [None] * 2
    axes2 = [o[1] for o in ords]
    st2, tok2 = swap_start(list(send1), all4, axes2, none4, [s.shape for s in send1], "rs_ici1_start")

    dh1 = dh_matmul(dproj, wt_p, tok2, seq, 1)
    gx0, dsh0, dsc0, dgp0 = prenorm_bwd(dh0, x2, dout, scale, g_pre, seq, tok2, 0, None)
    _, r2 = swap_wait(st2, (gx0, dh1), all4, axes2, none4, "rs_ici1_wait")
    keep2, send2 = zip(*[rs_add_second(keep1[a], r2[a], sel2[a], "rs_add_second") for a in range(2)])
    axes3 = [o[2] for o in ords]
    st3, tok3 = swap_start(list(send2), all4, axes3, none4, [s.shape for s in send2], "rs_ici2_start")
    grad_x2, dsh1, dsc1, dgp1 = prenorm_bwd(dh1, x2, dout, scale, g_pre, seq, tok3, 1, gx0)
    dshift = jnp.stack([dsh0, dsh1])
    dscale = jnp.stack([dsc0, dsc1])
    dg_pre = dgp0 + dgp1

    dmod = jnp.concatenate([dshift, dscale, dgate], axis=2).reshape(nb * 3 * D // 128, 128)
    small = jnp.concatenate([
        dmod, _rows128(dg_pre, 8), _rows128(dg_post, 8), _rows128(dg_q, 8), _rows128(dg_kv, 8),
        dconv[0:3].reshape(24, 128), _rows128(loss_part, 8)], axis=0)
    small_g = small_allgather(small, "gather_small_grads")
    sums = slot_sum(small_g)
    dmod_all = small_g[:, 0:48].reshape(8 * nb, 3 * D)
    g_bada = (sums[0:24] + sums[24:48]).reshape(1, 3 * D)
    g_gpre = sums[48:56].reshape(1, D)
    g_gpost = sums[56:64].reshape(1, D)
    g_gq = sums[64:67].reshape(1, QL)
    g_gkv = sums[72:74].reshape(1, KVL)
    g_conv_full = sums[80:104].reshape(3, D)
    loss = sums[104, 0]
    g_conv = lax.dynamic_slice(g_conv_full, (0, me * 128), (3, 128))
    dmod_cols = lax.dynamic_slice(dmod_all, (0, me * ada_cols), (8 * nb, ada_cols))
    g_wada = ada_bwd(c_all, dmod_cols)

    res = {}
    res["w_ada"] = [o_[None] for o_ in adamw(w_ada[0], m_w_ada[0], v_w_ada[0], g_wada, None, "adamw_w_ada", tok3)]

    def pack(b_, gp_, gpo_, gq_, gkv_, cw_):
        return jnp.concatenate([_rows128(b_, 24), _rows128(gp_, 8), _rows128(gpo_, 8), _rows128(gq_, 8),
                                _rows128(gkv_, 8), _rows128(cw_, 8)], axis=0)

    sw = pack(b_ada, g_pre, g_post, g_q, g_kv, conv_w)
    sm = pack(m_b_ada, m_g_pre, m_g_post, m_g_q, m_g_kv, m_conv_w)
    sv = pack(v_b_ada, v_g_pre, v_g_post, v_g_q, v_g_kv, v_conv_w)
    sg = pack(g_bada, g_gpre, g_gpost, g_gq, g_gkv, g_conv)
    small_out = adamw(sw, sm, sv, sg, None, "adamw_small", tok3)

    _, r3 = swap_wait(st3, small_out[0], all4, axes3, none4, "rs_ici2_wait")

    (g_wco, g_wmo, g_wout), (l_wco, l_wmo, l_wout) = scatter_wait(sc1, small_out[1], "scatter_out_grads_wait")
    (g_wuq_t, g_wukv_t), (l_wuq, l_wukv) = scatter_wait(sc2, small_out[2], "scatter_mla_grads_wait")

    res["w_in"] = [o_.T[None] for o_ in adamw_win(w_in[0].T, m_w_in[0].T, v_w_in[0].T,
                                                  keep2[0], r3[0], keep2[1], r3[1])]
    me1 = me.reshape(1).astype(jnp.int32)
    res["w_uq"] = [o_.T[None] for o_ in adamw_scattered(
        w_uq[0].T, m_w_uq[0].T, v_w_uq[0].T, g_wuq_t, l_wuq, me1, 64, "adamw_w_uq")]
    res["w_ukv"] = [o_[None] for o_ in adamw_scattered(
        w_ukv[0], m_w_ukv[0], v_w_ukv[0], g_wukv_t, l_wukv, me1, KVL, "adamw_w_ukv", transpose=True)]
    for nm, wv, mv, vv, gg, ll in (("w_conv_out", w_conv_out, m_w_conv_out, v_w_conv_out, g_wco, l_wco),
                                   ("w_mla_out", w_mla_out, m_w_mla_out, v_w_mla_out, g_wmo, l_wmo),
                                   ("w_out", w_out, m_w_out, v_w_out, g_wout, l_wout)):
        res[nm] = [o_[None] for o_ in adamw_scattered(wv[0], mv[0], vv[0], gg, ll, me1, 128, "adamw_square")]

    def unpack(a):
        return {"b_ada": a[0:24].reshape(1, 3 * D), "g_pre": a[24:32].reshape(1, D),
                "g_post": a[32:40].reshape(1, D), "g_q": a[40:43].reshape(1, QL),
                "g_kv": a[48:50].reshape(1, KVL), "conv_w": a[56:59].reshape(-1)[:3 * 128].reshape(1, 3, 128)}

    for nm in ("b_ada", "g_pre", "g_post", "g_q", "g_kv", "conv_w"):
        res[nm] = [unpack(a)[nm] for a in small_out]

    order = ["w_ada", "b_ada", "g_pre", "w_in", "conv_w", "w_conv_out", "g_q", "w_uq", "g_kv", "w_ukv",
             "w_mla_out", "w_out", "g_post"]
    out = [loss, grad_x2.reshape(nb, seq, D)]
    for k_ in range(4):
        out += [res[nm][k_] for nm in order]
    return tuple(out)
```

```python
import functools

import numpy as np
import jax
import jax.numpy as jnp
from jax import lax
from jax.experimental import pallas as pl
from jax.experimental.pallas import tpu as pltpu

F32 = jnp.float32
BF16 = jnp.bfloat16
MESH = pl.DeviceIdType.MESH

D = 1024
H = 8
QL = 384
KVL = 256
ROPE = 64
HALF = ROPE // 2
DQK = 256
DV = 128
NSEG = 8
NP = NSEG * D
EPS = 1e-6
ROPE_THETA = 10000.0
SM_SCALE = (128 + ROPE) ** -0.5
LOG2E = 1.4426950408889634
LN2 = 0.6931471805599453
FLASH_TQ = 512

SEG_BZ, SEG_GA, SEG_GB, SEG_LAT, SEG_V = 0, 1, 2, 3, 4

ADAM_LR = 0.001
ADAM_B1 = 0.9
ADAM_B2 = 0.999
ADAM_EPS = 1e-08
ADAM_WD = 0.01
ADAM_STEP = 10

VMEM_LIMIT = 56 * 1024 * 1024


def _params(sem=None, vmem=VMEM_LIMIT):
    kw = dict(vmem_limit_bytes=vmem)
    if sem is not None:
        kw["dimension_semantics"] = sem
    return pltpu.CompilerParams(**kw)


def _sig(v):
    return 0.5 * jnp.tanh(0.5 * v) + 0.5


def _dot(a, b):
    return jnp.dot(a, b, preferred_element_type=F32)


def _dot_nt(a, b):
    return lax.dot_general(a, b, (((1,), (1,)), ((), ())), preferred_element_type=F32)


def _dot_tn(a, b):
    return lax.dot_general(a, b, (((0,), (0,)), ((), ())), preferred_element_type=F32)


_AXIS_POS = {"x": 0, "y": 1, "c": 2}


def _coords():
    return lax.axis_index("x"), lax.axis_index("y"), lax.axis_index("c")


def _partner(axis):
    p = list(_coords())
    p[_AXIS_POS[axis]] = 1 - p[_AXIS_POS[axis]]
    return tuple(p)


def small_allgather(v, name):
    rows = v.shape[0]

    def body(v_ref, out_ref, send_sems, recv_sems):
        x, y, c = _coords()
        me = 4 * x + 2 * y + c
        out_ref[me] = v_ref[...]
        copies = []
        for k in range(1, 8):
            peer = (1 - x if k & 4 else x, 1 - y if k & 2 else y, 1 - c if k & 1 else c)
            cp = pltpu.make_async_remote_copy(
                src_ref=v_ref, dst_ref=out_ref.at[me],
                send_sem=send_sems.at[k - 1], recv_sem=recv_sems.at[k - 1],
                device_id=peer, device_id_type=MESH)
            cp.start()
            copies.append(cp)
        for cp in copies:
            cp.wait()

    return pl.pallas_call(
        body, name=name,
        out_shape=jax.ShapeDtypeStruct((8, rows, 128), F32),
        in_specs=[pl.BlockSpec(memory_space=pltpu.VMEM)],
        out_specs=pl.BlockSpec(memory_space=pltpu.VMEM),
        scratch_shapes=[pltpu.SemaphoreType.DMA((7,)), pltpu.SemaphoreType.DMA((7,))],
    )(v)


def _own_block_placed(s):
    x, y, c = _coords()
    return lax.dynamic_update_slice(lax.empty((2, 2, 2) + s.shape, s.dtype), s[None, None, None],
                                    (x, y, c) + (0,) * s.ndim)


def allgather_big(arrs, plan, name):
    n = len(arrs)
    m = len(plan)

    def body(*refs):
        ins, outs = refs[n:2 * n], refs[2 * n:3 * n]
        send_sems, recv_sems = refs[3 * n:]
        x, y, c = _coords()
        co = {"x": x, "y": y, "c": c}

        def window(ref, lead, rows, cols):
            win = tuple(slice(None) if w is None else pl.ds(w[0], w[1]) for w in (rows, cols))
            return ref.at[tuple(lead) + win]

        def held(e, free):
            i, rows, cols, _ = plan[e]
            lead = [slice(None) if ax in free else co[ax] for ax in ("x", "y", "c")]
            return window(outs[i], lead, rows, cols)

        def rcopy(e, stage, src, dst, axis):
            return pltpu.make_async_remote_copy(
                src_ref=src, dst_ref=dst,
                send_sem=send_sems.at[e, stage], recv_sem=recv_sems.at[e, stage],
                device_id=_partner(axis), device_id_type=MESH)

        stages = [[], [], []]
        for e, (i, rows, cols, order) in enumerate(plan):
            cp = rcopy(e, 0, window(ins[i], [], rows, cols), held(e, ()), order[0])
            cp.start()
            stages[0].append(cp)
        for s in (1, 2):
            for e, (i, rows, cols, order) in enumerate(plan):
                stages[s - 1][e].wait_recv()
                blk = held(e, order[:s])
                cp = rcopy(e, s, blk, blk, order[s])
                cp.start()
                stages[s].append(cp)
        for e in range(m):
            stages[2][e].wait_recv()
        for e in range(m):
            for s in range(3):
                stages[s][e].wait_send()

    any_spec = pl.BlockSpec(memory_space=pl.ANY)
    lands = [_own_block_placed(a) for a in arrs]
    return pl.pallas_call(
        body, name=name,
        out_shape=[jax.ShapeDtypeStruct(l.shape, l.dtype) for l in lands],
        in_specs=[any_spec] * (2 * n),
        out_specs=[any_spec] * n,
        input_output_aliases={i: i for i in range(n)},
        scratch_shapes=[pltpu.SemaphoreType.DMA((m, 3)), pltpu.SemaphoreType.DMA((m, 3))],
    )(*lands, *arrs)


def exchange(arrs, axes, picks, out_shapes, name):
    n = len(arrs)

    def body(*refs):
        ins, outs = refs[:n], refs[n:2 * n]
        send_sems, recv_sems = refs[2 * n:]
        x, y, c = _coords()
        co = {"x": x, "y": y, "c": c}
        copies = []
        for a in range(n):
            src = ins[a] if picks[a] is None else picks[a](ins[a], co)
            cp = pltpu.make_async_remote_copy(
                src_ref=src, dst_ref=outs[a],
                send_sem=send_sems.at[a], recv_sem=recv_sems.at[a],
                device_id=_partner(axes[a]), device_id_type=MESH)
            cp.start()
            copies.append(cp)
        for cp in copies:
            cp.wait()

    any_spec = pl.BlockSpec(memory_space=pl.ANY)
    return pl.pallas_call(
        body, name=name,
        out_shape=[jax.ShapeDtypeStruct(s, a.dtype) for s, a in zip(out_shapes, arrs)],
        in_specs=[any_spec] * n,
        out_specs=[any_spec] * n,
        scratch_shapes=[pltpu.SemaphoreType.DMA((n,)), pltpu.SemaphoreType.DMA((n,))],
    )(*arrs)


_HBM = pl.BlockSpec(memory_space=pltpu.HBM)
_SEM = pl.BlockSpec(memory_space=pltpu.SEMAPHORE)


def _swap_copies(srcs, lands, send_sems, recv_sems, axes, picks):
    x, y, c = _coords()
    co = {"x": x, "y": y, "c": c}
    return [pltpu.make_async_remote_copy(
        src_ref=srcs[a] if picks[a] is None else picks[a](srcs[a], co), dst_ref=lands[a],
        send_sem=send_sems.at[a], recv_sem=recv_sems.at[a],
        device_id=_partner(axes[a]), device_id_type=MESH) for a in range(len(srcs))]


def swap_start(arrs, which, axes, picks, out_shapes, name):
    ns, n = len(arrs), len(which)

    def body(*refs):
        srcs, lands = refs[:ns], refs[ns:ns + n]
        send_sems, recv_sems = refs[ns + n:ns + n + 2]
        token = refs[-1]
        for cp in _swap_copies([srcs[i] for i in which], lands, send_sems, recv_sems, axes, picks):
            cp.start()
        token[...] = jnp.zeros_like(token)

    lands = [lax.empty(s, arrs[i].dtype) for s, i in zip(out_shapes, which)]
    ops = [pltpu.with_memory_space_constraint(a, pltpu.HBM) for a in list(arrs) + lands]
    out = pl.pallas_call(
        body, name=name,
        out_shape=[pltpu.SemaphoreType.DMA((n,)), pltpu.SemaphoreType.DMA((n,))]
        + [pltpu.HBM(o.shape, o.dtype) for o in ops] + [jax.ShapeDtypeStruct((8, 128), F32)],
        in_specs=[_HBM] * (ns + n),
        out_specs=[_SEM, _SEM] + [_HBM] * (ns + n) + [pl.BlockSpec(memory_space=pltpu.VMEM)],
        input_output_aliases={i: 2 + i for i in range(ns + n)},
        compiler_params=pltpu.CompilerParams(has_side_effects=pltpu.SideEffectType.DATAFLOW_SIDE_EFFECTING),
    )(*ops)
    return out[:-1], out[-1]


def swap_wait(state, after, which, axes, picks, name):
    n = len(which)
    ns = len(state) - 2 - n

    def body(*refs):
        srcs, lands = refs[:ns], refs[ns:ns + n]
        send_sems, recv_sems = refs[ns + n:ns + n + 2]
        for cp in _swap_copies([srcs[i] for i in which], lands, send_sems, recv_sems, axes, picks):
            cp.wait_send()
            cp.wait_recv()

    thru = list(state[2:])
    after = list(after) if isinstance(after, (list, tuple)) else [after]
    out = pl.pallas_call(
        body, name=name,
        out_shape=[pltpu.HBM(o.shape, o.dtype) for o in thru],
        in_specs=[_HBM] * (ns + n) + [_SEM, _SEM] + [pl.BlockSpec(memory_space=pl.ANY)] * len(after),
        out_specs=[_HBM] * (ns + n),
        input_output_aliases={i: i for i in range(ns + n)},
        compiler_params=pltpu.CompilerParams(has_side_effects=pltpu.SideEffectType.DATAFLOW_SIDE_EFFECTING),
    )(*thru, state[0], state[1], *after)
    return out[:ns], out[ns:]


def _gather_copies(shards, lands, send_sems, recv_sems):
    x, y, c = _coords()
    copies = []
    for a in range(len(shards)):
        for k in range(1, 8):
            peer = (1 - x if k & 4 else x, 1 - y if k & 2 else y, 1 - c if k & 1 else c)
            copies.append(pltpu.make_async_remote_copy(
                src_ref=shards[a], dst_ref=lands[a].at[x, y, c],
                send_sem=send_sems.at[7 * a + k - 1], recv_sem=recv_sems.at[7 * a + k - 1],
                device_id=peer, device_id_type=MESH))
    return copies


def gather_start(shards, name):
    n = len(shards)
    x, y, c = _coords()

    def body(*refs):
        srcs, lands = refs[:n], refs[n:2 * n]
        send_sems, recv_sems = refs[2 * n:2 * n + 2]
        token = refs[-1]
        for cp in _gather_copies(srcs, lands, send_sems, recv_sems):
            cp.start()
        token[...] = jnp.zeros_like(token)

    lands = [_own_block_placed(s) for s in shards]
    ops = [pltpu.with_memory_space_constraint(a, pltpu.HBM) for a in list(shards) + lands]
    out = pl.pallas_call(
        body, name=name,
        out_shape=[pltpu.SemaphoreType.DMA((7 * n,)), pltpu.SemaphoreType.DMA((7 * n,))]
        + [pltpu.HBM(o.shape, o.dtype) for o in ops] + [jax.ShapeDtypeStruct((8, 128), F32)],
        in_specs=[_HBM] * (2 * n),
        out_specs=[_SEM, _SEM] + [_HBM] * (2 * n) + [pl.BlockSpec(memory_space=pltpu.VMEM)],
        input_output_aliases={i: 2 + i for i in range(2 * n)},
        compiler_params=pltpu.CompilerParams(has_side_effects=pltpu.SideEffectType.DATAFLOW_SIDE_EFFECTING),
    )(*ops)
    return out[:-1], out[-1]


def gather_wait(state, after, name):
    n = (len(state) - 2) // 2

    def body(*refs):
        srcs, lands = refs[:n], refs[n:2 * n]
        send_sems, recv_sems = refs[2 * n:2 * n + 2]
        for cp in _gather_copies(srcs, lands, send_sems, recv_sems):
            cp.wait_send()
            cp.wait_recv()

    thru = list(state[2:])
    out = pl.pallas_call(
        body, name=name,
        out_shape=[pltpu.HBM(o.shape, o.dtype) for o in thru],
        in_specs=[_HBM] * (2 * n) + [_SEM, _SEM, pl.BlockSpec(memory_space=pl.ANY)],
        out_specs=[_HBM] * (2 * n),
        input_output_aliases={i: i for i in range(2 * n)},
        compiler_params=pltpu.CompilerParams(has_side_effects=pltpu.SideEffectType.DATAFLOW_SIDE_EFFECTING),
    )(*thru, state[0], state[1], after)
    return out[n:]


def _scatter_copies(grads, lands, send_sems, recv_sems):
    x, y, c = _coords()
    me = 4 * x + 2 * y + c
    copies = []
    for a in range(len(grads)):
        r = grads[a].shape[0] // 8
        for k in range(1, 8):
            px, py, pc = (1 - x if k & 4 else x, 1 - y if k & 2 else y, 1 - c if k & 1 else c)
            rows = pl.ds(pl.multiple_of((4 * px + 2 * py + pc) * r, r), r)
            copies.append(pltpu.make_async_remote_copy(
                src_ref=grads[a].at[rows], dst_ref=lands[a].at[me],
                send_sem=send_sems.at[7 * a + k - 1], recv_sem=recv_sems.at[7 * a + k - 1],
                device_id=(px, py, pc), device_id_type=MESH))
    return copies


def scatter_start(grads, name):
    n = len(grads)

    def body(*refs):
        srcs, lands = refs[:n], refs[n:2 * n]
        send_sems, recv_sems = refs[2 * n:2 * n + 2]
        token = refs[-1]
        for cp in _scatter_copies(srcs, lands, send_sems, recv_sems):
            cp.start()
        token[...] = jnp.zeros_like(token)

    lands = [jnp.zeros((8, g.shape[0] // 8, g.shape[1]), g.dtype) for g in grads]
    ops = [pltpu.with_memory_space_constraint(a, pltpu.HBM) for a in list(grads) + lands]
    out = pl.pallas_call(
        body, name=name,
        out_shape=[pltpu.SemaphoreType.DMA((7 * n,)), pltpu.SemaphoreType.DMA((7 * n,))]
        + [pltpu.HBM(o.shape, o.dtype) for o in ops] + [jax.ShapeDtypeStruct((8, 128), F32)],
        in_specs=[_HBM] * (2 * n),
        out_specs=[_SEM, _SEM] + [_HBM] * (2 * n) + [pl.BlockSpec(memory_space=pltpu.VMEM)],
        input_output_aliases={i: 2 + i for i in range(2 * n)},
        compiler_params=pltpu.CompilerParams(has_side_effects=pltpu.SideEffectType.DATAFLOW_SIDE_EFFECTING),
    )(*ops)
    return out[:-1], out[-1]


def scatter_wait(state, after, name):
    n = (len(state) - 2) // 2

    def body(*refs):
        srcs, lands = refs[:n], refs[n:2 * n]
        send_sems, recv_sems = refs[2 * n:2 * n + 2]
        for cp in _scatter_copies(srcs, lands, send_sems, recv_sems):
            cp.wait_send()
            cp.wait_recv()

    thru = list(state[2:])
    after = list(after) if isinstance(after, (list, tuple)) else [after]
    out = pl.pallas_call(
        body, name=name,
        out_shape=[pltpu.HBM(o.shape, o.dtype) for o in thru],
        in_specs=[_HBM] * (2 * n) + [_SEM, _SEM] + [pl.BlockSpec(memory_space=pl.ANY)] * len(after),
        out_specs=[_HBM] * (2 * n),
        input_output_aliases={i: i for i in range(2 * n)},
        compiler_params=pltpu.CompilerParams(has_side_effects=pltpu.SideEffectType.DATAFLOW_SIDE_EFFECTING),
    )(*thru, state[0], state[1], *after)
    return out[:n], out[n:]


def rs_win_add_first(g, r, sel, next_dim, col, name):
    rows, cols = r.shape[2:]

    def body(sel_ref, gk_ref, rk_ref, gs_ref, rs_ref, keep_ref, send_ref):
        keep_ref[...] = gk_ref[...] + rk_ref[...]
        send_ref[...] = (gs_ref[...] + rs_ref[...]).astype(BF16)

    def g_map(flip):
        def f(j, s):
            nxt = 1 - s[next_dim] if flip else s[next_dim]
            return (nxt, j, s[2], 0, col) if next_dim == 0 else (j, nxt, s[2], 0, col)
        return f

    def r_map(flip):
        def f(j, s):
            nxt = 1 - s[next_dim] if flip else s[next_dim]
            return (nxt, j, 0, 0) if next_dim == 0 else (j, nxt, 0, 0)
        return f

    gblk = (None, None, None, rows, cols)
    rblk = (None, None, rows, cols)
    oblk = (None, rows, cols)
    return pl.pallas_call(
        body, name=name,
        grid_spec=pltpu.PrefetchScalarGridSpec(
            num_scalar_prefetch=1, grid=(2,),
            in_specs=[pl.BlockSpec(gblk, g_map(False)), pl.BlockSpec(rblk, r_map(False)),
                      pl.BlockSpec(gblk, g_map(True)), pl.BlockSpec(rblk, r_map(True))],
            out_specs=[pl.BlockSpec(oblk, lambda j, s: (j, 0, 0)),
                       pl.BlockSpec(oblk, lambda j, s: (j, 0, 0))]),
        out_shape=[jax.ShapeDtypeStruct((2, rows, cols), F32),
                   jax.ShapeDtypeStruct((2, rows, cols), BF16)],
        compiler_params=_params(),
    )(sel, g, r, g, r)


def rs_add_second(k, r, sel, name):
    _, rows, cols = k.shape
    tr = rows // 2 if rows % 32 == 0 else rows
    nt = rows // tr

    def body(sel_ref, kk_ref, rk_ref, ks_ref, rs_ref, keep_ref, send_ref):
        keep_ref[...] = kk_ref[...] + rk_ref[...].astype(F32)
        send_ref[...] = (ks_ref[...] + rs_ref[...].astype(F32)).astype(BF16)

    blk = (None, tr, cols)
    oblk = (tr, cols)
    return pl.pallas_call(
        body, name=name,
        grid_spec=pltpu.PrefetchScalarGridSpec(
            num_scalar_prefetch=1, grid=(nt,),
            in_specs=[
                pl.BlockSpec(blk, lambda i, s: (s[0], i, 0)),
                pl.BlockSpec(blk, lambda i, s: (s[0], i, 0)),
                pl.BlockSpec(blk, lambda i, s: (1 - s[0], i, 0)),
                pl.BlockSpec(blk, lambda i, s: (1 - s[0], i, 0)),
            ],
            out_specs=[pl.BlockSpec(oblk, lambda i, s: (i, 0)),
                       pl.BlockSpec(oblk, lambda i, s: (i, 0))]),
        out_shape=[jax.ShapeDtypeStruct((rows, cols), F32),
                   jax.ShapeDtypeStruct((rows, cols), BF16)],
        compiler_params=_params(),
    )(sel, k, r, k, r)


SEG_ROWS = (4800, 5824, 6848, 4096, 0, 1024, 2048, 3072)
LAT_ROWS = QL + KVL + ROPE
N_IN = 7872


def _seg_row(j):
    return pl.multiple_of(jnp.where(j < 3, 4800 + 1024 * j, jnp.where(j == 3, 4096, (j - 4) * 1024)), 8)


def proj_matmul(h, wt_bits, token):
    t = h.shape[0]
    tm = min(1024, t)

    def body(h_ref, w_hbm, tok_ref, o_ref, wt_ref, buf, sems):
        j = pl.program_id(0)
        slot = j % 2

        def fetch(seg, into):
            return pltpu.make_async_copy(w_hbm.at[pl.ds(_seg_row(seg), D)], buf.at[into], sems.at[into])

        @pl.when(pl.program_id(1) == 0)
        def _():
            @pl.when(j == 0)
            def _():
                fetch(j, slot).start()

            fetch(j, slot).wait()

            @pl.when(j + 1 < NSEG)
            def _():
                fetch(j + 1, 1 - slot).start()

            bits = pltpu.bitcast(buf[slot], jnp.uint32)
            row = lax.broadcasted_iota(jnp.int32, (D, D // 2), 0)
            live = jnp.logical_or(j != SEG_LAT, row < LAT_ROWS)
            lo = pltpu.bitcast(bits << 16, F32)
            hi = pltpu.bitcast(bits & jnp.uint32(0xFFFF0000), F32)
            wt_ref[:, :D // 2] = jnp.where(live, lo, 0.0).astype(BF16)
            wt_ref[:, D // 2:] = jnp.where(live, hi, 0.0).astype(BF16)

        o_ref[...] = _dot_nt(h_ref[...], wt_ref[...]).astype(BF16)

    return pl.pallas_call(
        body, name="proj_matmul", grid=(NSEG, t // tm),
        in_specs=[pl.BlockSpec((tm, D), lambda j, i: (i, 0)),
                  pl.BlockSpec(memory_space=pl.ANY),
                  pl.BlockSpec((8, 128), lambda j, i: (0, 0))],
        out_specs=[pl.BlockSpec((None, tm, D), lambda j, i: (j, i, 0)),
                   pl.BlockSpec((D, D), lambda j, i: (j, 0))],
        out_shape=[jax.ShapeDtypeStruct((NSEG, t, D), BF16), jax.ShapeDtypeStruct((NP, D), BF16)],
        scratch_shapes=[pltpu.VMEM((2, D, D // 2), F32), pltpu.SemaphoreType.DMA((2,))],
        compiler_params=_params(("arbitrary", "arbitrary")),
    )(h, wt_bits, token)


def dh_matmul(dproj, wt, token, seq, b):
    tm = min(1024, seq)
    nblk = seq // tm

    per = 2

    def body(b_ref, d_ref, w_ref, tok_ref, o_ref, acc_ref):
        k = pl.program_id(1)

        @pl.when(k == 0)
        def _():
            acc_ref[...] = jnp.zeros_like(acc_ref)

        part = _dot(d_ref[0], w_ref[0:D, :])
        for j in range(1, per):
            part = part + _dot(d_ref[j], w_ref[j * D:(j + 1) * D, :])
        acc_ref[...] += part

        @pl.when(k == NSEG // per - 1)
        def _():
            o_ref[...] = acc_ref[...]

    return pl.pallas_call(
        body, name="dh_matmul",
        grid_spec=pltpu.PrefetchScalarGridSpec(
            num_scalar_prefetch=1, grid=(nblk, NSEG // per),
            in_specs=[pl.BlockSpec((per, tm, D), lambda i, k, s: (k, s[0] * nblk + i, 0)),
                      pl.BlockSpec((per * D, D), lambda i, k, s: (k, 0)),
                      pl.BlockSpec((8, 128), lambda i, k, s: (0, 0))],
            out_specs=pl.BlockSpec((tm, D), lambda i, k, s: (i, 0)),
            scratch_shapes=[pltpu.VMEM((tm, D), F32)]),
        out_shape=jax.ShapeDtypeStruct((seq, D), F32),
        compiler_params=_params(("parallel", "arbitrary")),
    )(jnp.full((1,), b, jnp.int32), dproj, wt, token)


def win_grad_matmul(h, dproj, token):
    t = h.shape[0]
    tk = min(2048, t)
    nk = t // tk

    def body(h_ref, d_ref, tok_ref, o_hbm, acc_ref, sem):
        j = pl.program_id(0)
        k = pl.program_id(1)

        @pl.when(k == 0)
        def _():
            acc_ref[...] = jnp.zeros_like(acc_ref)

        acc_ref[...] += _dot_tn(d_ref[...], h_ref[...])

        @pl.when(jnp.logical_and(k == nk - 1, j != SEG_LAT))
        def _():
            cp = pltpu.make_async_copy(acc_ref, o_hbm.at[pl.ds(_seg_row(j), D)], sem)
            cp.start()
            cp.wait()

        @pl.when(jnp.logical_and(k == nk - 1, j == SEG_LAT))
        def _():
            cp = pltpu.make_async_copy(acc_ref.at[pl.ds(0, LAT_ROWS)],
                                       o_hbm.at[pl.ds(SEG_ROWS[SEG_LAT], LAT_ROWS)], sem)
            cp.start()
            cp.wait()

    return pl.pallas_call(
        body, name="win_grad_matmul", grid=(NSEG, nk),
        in_specs=[pl.BlockSpec((tk, D), lambda j, k: (k, 0)),
                  pl.BlockSpec((None, tk, D), lambda j, k: (j, k, 0)),
                  pl.BlockSpec((8, 128), lambda j, k: (0, 0))],
        out_specs=pl.BlockSpec(memory_space=pl.ANY),
        out_shape=jax.ShapeDtypeStruct((N_IN, D), F32),
        scratch_shapes=[pltpu.VMEM((D, D), F32), pltpu.SemaphoreType.DMA],
        compiler_params=_params(("arbitrary", "arbitrary")),
    )(h, dproj, token)


def grad_matmul(a, b, name):
    t, m = a.shape
    n = b.shape[1]
    tk = min(2048, t)
    nk = t // tk

    def body(a_ref, b_ref, o_ref, acc_ref):
        k = pl.program_id(0)

        @pl.when(k == 0)
        def _():
            acc_ref[...] = jnp.zeros_like(acc_ref)

        acc_ref[...] += _dot_tn(a_ref[...], b_ref[...])

        @pl.when(k == nk - 1)
        def _():
            o_ref[...] = acc_ref[...].astype(BF16)

    return pl.pallas_call(
        body, name=name, grid=(nk,),
        in_specs=[pl.BlockSpec((tk, m), lambda k: (k, 0)),
                  pl.BlockSpec((tk, n), lambda k: (k, 0))],
        out_specs=pl.BlockSpec((m, n), lambda k: (0, 0)),
        out_shape=jax.ShapeDtypeStruct((m, n), BF16),
        scratch_shapes=[pltpu.VMEM((m, n), F32)],
        compiler_params=_params(("arbitrary",)),
    )(a, b)


def ada_fwd(c_all, w_ada, b_cols):
    def body(c_ref, w_ref, b_ref, o_ref):
        o_ref[...] = _dot(c_ref[...].astype(BF16), w_ref[...].astype(BF16)) + b_ref[...]

    return pl.pallas_call(
        body, name="ada_fwd",
        out_shape=jax.ShapeDtypeStruct((c_all.shape[0], w_ada.shape[1]), F32),
        compiler_params=_params(),
    )(c_all, w_ada, b_cols)


def ada_bwd(c_all, dmod_cols):
    def body(c_ref, d_ref, o_ref):
        o_ref[...] = _dot_tn(c_ref[...].astype(BF16), d_ref[...].astype(BF16))

    return pl.pallas_call(
        body, name="ada_bwd",
        out_shape=jax.ShapeDtypeStruct((c_all.shape[1], dmod_cols.shape[1]), F32),
        compiler_params=_params(),
    )(c_all, dmod_cols)


def slot_sum(g):
    def body(g_ref, o_ref):
        acc = g_ref[0]
        for s in range(1, 8):
            acc = acc + g_ref[s]
        o_ref[...] = acc

    return pl.pallas_call(
        body, name="slot_sum",
        out_shape=jax.ShapeDtypeStruct(g.shape[1:], F32),
    )(g)


def prenorm_fwd(x2, scale, shift, g_pre, seq):
    t = x2.shape[0]
    tm = min(512, seq)
    tpb = seq // tm

    def body(x_ref, sc_ref, sh_ref, g_ref, h_ref):
        xv = x_ref[...]
        r = lax.rsqrt(jnp.mean(xv * xv, axis=-1, keepdims=True) + EPS)
        hv = (xv * r * g_ref[...]) * (1.0 + sc_ref[...]) + sh_ref[...]
        h_ref[...] = hv.astype(BF16)

    per_batch = pl.BlockSpec((None, 1, D), lambda i: (i // tpb, 0, 0))
    return pl.pallas_call(
        body, name="prenorm_fwd", grid=(t // tm,),
        in_specs=[pl.BlockSpec((tm, D), lambda i: (i, 0)), per_batch, per_batch,
                  pl.BlockSpec((1, D), lambda i: (0, 0))],
        out_specs=pl.BlockSpec((tm, D), lambda i: (i, 0)),
        out_shape=jax.ShapeDtypeStruct((t, D), BF16),
        compiler_params=_params(("parallel",)),
    )(x2, scale, shift, g_pre)


def prenorm_bwd(dh, x2, dout, scale, g_pre, seq, token, b, gx_prev):
    t = x2.shape[0]
    tm = min(512, seq)
    tpb = seq // tm
    if gx_prev is None:
        gx_prev = lax.empty((t, D), F32)

    def body(b_ref, dh_ref, x_ref, do_ref, sc_ref, g_ref, tok_ref, gxp_ref, gx_ref, dsh_ref, dsc_ref, dg_ref):
        i = pl.program_id(0)
        xv = x_ref[...]
        dhv = dh_ref[...]
        g = g_ref[...]
        r = lax.rsqrt(jnp.mean(xv * xv, axis=-1, keepdims=True) + EPS)
        nrm = xv * r
        dxn = dhv * (1.0 + sc_ref[...])
        dn = dxn * g
        dx = r * (dn - nrm * jnp.mean(dn * nrm, axis=-1, keepdims=True))
        gx_ref[...] = dx + do_ref[...]

        @pl.when(i == 0)
        def _():
            dsh_ref[...] = jnp.zeros_like(dsh_ref)
            dsc_ref[...] = jnp.zeros_like(dsc_ref)
            dg_ref[...] = jnp.zeros_like(dg_ref)

        dsh_ref[...] += jnp.sum(dhv, axis=0, keepdims=True)
        dsc_ref[...] += jnp.sum(dhv * (nrm * g), axis=0, keepdims=True)
        dg_ref[...] += jnp.sum(dxn * nrm, axis=0, keepdims=True)

    row = pl.BlockSpec((tm, D), lambda i, s: (i, 0))
    grow = pl.BlockSpec((tm, D), lambda i, s: (s[0] * tpb + i, 0))
    per_batch = pl.BlockSpec((None, 1, D), lambda i, s: (s[0], 0, 0))
    vec = pl.BlockSpec((1, D), lambda i, s: (0, 0))
    return pl.pallas_call(
        body, name="prenorm_bwd",
        grid_spec=pltpu.PrefetchScalarGridSpec(
            num_scalar_prefetch=1, grid=(tpb,),
            in_specs=[row, grow, grow, per_batch, vec, pl.BlockSpec((8, 128), lambda i, s: (0, 0)),
                      pl.BlockSpec(memory_space=pl.ANY)],
            out_specs=[grow, vec, vec, vec]),
        out_shape=[jax.ShapeDtypeStruct((t, D), F32), jax.ShapeDtypeStruct((1, D), F32),
                   jax.ShapeDtypeStruct((1, D), F32), jax.ShapeDtypeStruct((1, D), F32)],
        input_output_aliases={7: 0},
        compiler_params=_params(("arbitrary",)),
    )(jnp.full((1,), b, jnp.int32), dh, x2, dout, scale, g_pre, token, gx_prev)


CONV_TC = 128


def _shift_down(u, k, rows):
    idx = lax.broadcasted_iota(jnp.int32, u.shape, 0)
    return jnp.where(idx >= k, pltpu.roll(u, k, 0), 0.0)


def _shift_up(u, k, rows):
    idx = lax.broadcasted_iota(jnp.int32, u.shape, 0)
    return jnp.where(idx < rows - k, pltpu.roll(u, rows - k, 0), 0.0)


def conv_fwd(proj, conv_w, seq):
    t = proj.shape[1]
    nb = t // seq

    def body(p_ref, w_ref, y_ref):
        av = p_ref[0].astype(F32)
        ab = p_ref[1].astype(F32)
        ac = p_ref[2].astype(F32)
        az = p_ref[3].astype(F32)
        w = w_ref[...]
        u = ac * av
        y1 = _shift_down(u, 2, seq) * w[0:1] + _shift_down(u, 1, seq) * w[1:2] + u * w[2:3]
        y_ref[...] = (ab * y1 * (az * _sig(az))).astype(BF16)

    return pl.pallas_call(
        body, name="conv_fwd", grid=(nb, D // CONV_TC),
        in_specs=[pl.BlockSpec((4, seq, CONV_TC), lambda b, ci: (1, b, ci)),
                  pl.BlockSpec((8, CONV_TC), lambda b, ci: (0, ci))],
        out_specs=pl.BlockSpec((seq, CONV_TC), lambda b, ci: (b, ci)),
        out_shape=jax.ShapeDtypeStruct((t, D), BF16),
        compiler_params=_params(("parallel", "parallel")),
    )(proj, conv_w)


def conv_bwd(dproj, proj, dy, conv_w, seq):
    t = proj.shape[1]
    nb = t // seq

    def body(dp_in_ref, p_ref, dy_ref, w_ref, dp_ref, dw_ref):
        b = pl.program_id(1)
        av = p_ref[0].astype(F32)
        ab = p_ref[1].astype(F32)
        ac = p_ref[2].astype(F32)
        az = p_ref[3].astype(F32)
        dyv = dy_ref[...].astype(F32)
        w = w_ref[...]
        u = ac * av
        u1 = _shift_down(u, 1, seq)
        u2 = _shift_down(u, 2, seq)
        y1 = u2 * w[0:1] + u1 * w[1:2] + u * w[2:3]
        sz = _sig(az)
        silu = az * sz
        dy1 = dyv * ab * silu
        du = dy1 * w[2:3] + _shift_up(dy1, 1, seq) * w[1:2] + _shift_up(dy1, 2, seq) * w[0:1]
        dp_ref[0] = (du * ac).astype(BF16)
        dp_ref[1] = (dyv * y1 * silu).astype(BF16)
        dp_ref[2] = (du * av).astype(BF16)
        dp_ref[3] = (dyv * ab * y1 * (sz * (1.0 + az * (1.0 - sz)))).astype(BF16)

        @pl.when(b == 0)
        def _():
            dw_ref[...] = jnp.zeros_like(dw_ref)

        dw_ref[0:1, :] += jnp.sum(dy1 * u2, axis=0, keepdims=True)
        dw_ref[1:2, :] += jnp.sum(dy1 * u1, axis=0, keepdims=True)
        dw_ref[2:3, :] += jnp.sum(dy1 * u, axis=0, keepdims=True)

    return pl.pallas_call(
        body, name="conv_bwd", grid=(D // CONV_TC, nb),
        in_specs=[pl.BlockSpec(memory_space=pl.ANY),
                  pl.BlockSpec((4, seq, CONV_TC), lambda ci, b: (1, b, ci)),
                  pl.BlockSpec((seq, CONV_TC), lambda ci, b: (b, ci)),
                  pl.BlockSpec((8, CONV_TC), lambda ci, b: (0, ci))],
        out_specs=[pl.BlockSpec((4, seq, CONV_TC), lambda ci, b: (1, b, ci)),
                   pl.BlockSpec((8, CONV_TC), lambda ci, b: (0, ci))],
        out_shape=[jax.ShapeDtypeStruct(dproj.shape, BF16),
                   jax.ShapeDtypeStruct((8, D), F32)],
        input_output_aliases={0: 0},
        compiler_params=_params(("parallel", "arbitrary")),
    )(dproj, proj, dy, conv_w)


def _rope_tables(pos_ref, invf_ref, ma_ref, mb_ref, sign):
    ang = pos_ref[...].astype(F32) * invf_ref[...]
    cs = jnp.cos(ang)
    sn = jnp.sin(ang) * sign
    return cs, sn * ma_ref[...], sn * mb_ref[...]


def _rotate(v, cs, sa, sb):
    return v * cs + pltpu.roll(v, 128 - HALF, 1) * sa + pltpu.roll(v, HALF, 1) * sb


MLA_TM = 256


def mla_prep_fwd(proj, pos, g_q, g_kv, wuq, wukv, tabs):
    t = proj.shape[1]
    tm = min(MLA_TM, t)

    def body(lat_ref, pos_ref, gq_ref, gkv_ref, wuq_ref, wukv_ref, invf_ref, ma_ref, mb_ref,
             q_ref, k_ref, kv_ref, qn_ref, kvn_ref):
        lat = lat_ref[...].astype(F32)
        ql = lat[:, :QL]
        kl = lat[:, QL:QL + KVL]
        kr = lat[:, QL + KVL:QL + KVL + 128]
        qn = (ql * lax.rsqrt(jnp.mean(ql * ql, axis=-1, keepdims=True) + EPS) * gq_ref[...]).astype(BF16)
        kvn = (kl * lax.rsqrt(jnp.mean(kl * kl, axis=-1, keepdims=True) + EPS) * gkv_ref[...]).astype(BF16)
        qn_ref[...] = qn
        kvn_ref[...] = kvn
        cs, sa, sb = _rope_tables(pos_ref, invf_ref, ma_ref, mb_ref, 1.0)
        q = _dot_nt(qn, wuq_ref[...]) * (SM_SCALE * LOG2E)
        kv = _dot_nt(kvn, wukv_ref[...]).astype(BF16)
        kv_ref[...] = kv
        kpe = _rotate(kr, cs, sa, sb).astype(BF16)
        for hh in range(H):
            lo, mid, hi = hh * DQK, hh * DQK + 128, (hh + 1) * DQK
            q_ref[:, lo:mid] = q[:, lo:mid].astype(BF16)
            q_ref[:, mid:hi] = _rotate(q[:, mid:hi], cs, sa, sb).astype(BF16)
            k_ref[:, lo:mid] = kv[:, lo:mid]
            k_ref[:, mid:hi] = kpe

    row = lambda w: pl.BlockSpec((tm, w), lambda i: (i, 0))
    const = lambda a: pl.BlockSpec(a.shape, lambda i: (0,) * a.ndim)
    return pl.pallas_call(
        body, name="mla_prep_fwd", grid=(t // tm,),
        in_specs=[pl.BlockSpec((None, tm, D), lambda i: (SEG_LAT, i, 0)), row(1),
                  const(g_q), const(g_kv), const(wuq), const(wukv)] + [const(a) for a in tabs],
        out_specs=[row(H * DQK), row(H * DQK), row(H * DQK), row(QL), row(KVL)],
        out_shape=[jax.ShapeDtypeStruct((t, H * DQK), BF16)] * 3
        + [jax.ShapeDtypeStruct((t, QL), BF16), jax.ShapeDtypeStruct((t, KVL), BF16)],
        compiler_params=_params(("parallel",)),
    )(proj, pos, g_q, g_kv, wuq, wukv, *tabs)


def mla_prep_bwd(dproj, proj, dq_rot, dk, dv, pos, g_q, g_kv, wuq, wukv, tabs):
    t = proj.shape[1]
    tm = min(MLA_TM, t)

    def body(dp_in_ref, lat_ref, dqr_ref, dk_ref, dv_ref, pos_ref, gq_ref, gkv_ref, wuq_ref, wukv_ref,
             invf_ref, ma_ref, mb_ref, dp_ref, dq_ref, dkv_ref, dgq_ref, dgkv_ref):
        i = pl.program_id(0)
        lat = lat_ref[...].astype(F32)
        ql = lat[:, :QL]
        kl = lat[:, QL:QL + KVL]
        rq = lax.rsqrt(jnp.mean(ql * ql, axis=-1, keepdims=True) + EPS)
        rk = lax.rsqrt(jnp.mean(kl * kl, axis=-1, keepdims=True) + EPS)
        nq = ql * rq
        nk = kl * rk
        cs, sa, sb = _rope_tables(pos_ref, invf_ref, ma_ref, mb_ref, -1.0)
        dkpe = jnp.zeros((tm, 128), F32)
        for hh in range(H):
            lo, mid, hi = hh * DQK, hh * DQK + 128, (hh + 1) * DQK
            dq_ref[:, lo:mid] = (dqr_ref[:, lo:mid] * SM_SCALE).astype(BF16)
            dq_ref[:, mid:hi] = _rotate(dqr_ref[:, mid:hi] * SM_SCALE, cs, sa, sb).astype(BF16)
            dkv_ref[:, lo:mid] = dk_ref[:, lo:mid]
            dkv_ref[:, mid:hi] = dv_ref[:, hh * DV:(hh + 1) * DV]
            dkpe = dkpe + dk_ref[:, mid:hi].astype(F32)
        lane = lax.broadcasted_iota(jnp.int32, (tm, 128), 1)
        dkr = jnp.where(lane < ROPE, _rotate(dkpe, cs, sa, sb), 0.0)
        dqn = _dot(dq_ref[...], wuq_ref[...])
        dkvn = _dot(dkv_ref[...], wukv_ref[...])
        gq = gq_ref[...]
        gkv = gkv_ref[...]
        dnq = dqn * gq
        dnk = dkvn * gkv
        dql = rq * (dnq - nq * jnp.mean(dnq * nq, axis=-1, keepdims=True))
        dkl = rk * (dnk - nk * jnp.mean(dnk * nk, axis=-1, keepdims=True))
        dp_ref[:, :QL] = dql.astype(BF16)
        dp_ref[:, QL:QL + KVL] = dkl.astype(BF16)
        dp_ref[:, QL + KVL:QL + KVL + 128] = dkr.astype(BF16)
        dp_ref[:, QL + KVL + 128:] = jnp.zeros((tm, D - QL - KVL - 128), BF16)

        @pl.when(i == 0)
        def _():
            dgq_ref[...] = jnp.zeros_like(dgq_ref)
            dgkv_ref[...] = jnp.zeros_like(dgkv_ref)

        dgq_ref[...] += jnp.sum(dqn * nq, axis=0, keepdims=True)
        dgkv_ref[...] += jnp.sum(dkvn * nk, axis=0, keepdims=True)

    row = lambda w: pl.BlockSpec((tm, w), lambda i: (i, 0))
    const = lambda a: pl.BlockSpec(a.shape, lambda i: (0,) * a.ndim)
    seg = pl.BlockSpec((None, tm, D), lambda i: (SEG_LAT, i, 0))
    return pl.pallas_call(
        body, name="mla_prep_bwd", grid=(t // tm,),
        in_specs=[pl.BlockSpec(memory_space=pl.ANY), seg, row(H * DQK), row(H * DQK), row(H * DV), row(1),
                  const(g_q), const(g_kv), const(wuq), const(wukv)] + [const(a) for a in tabs],
        out_specs=[seg, row(H * DQK), row(H * DQK),
                   pl.BlockSpec((1, QL), lambda i: (0, 0)), pl.BlockSpec((1, KVL), lambda i: (0, 0))],
        out_shape=[jax.ShapeDtypeStruct(dproj.shape, BF16),
                   jax.ShapeDtypeStruct((t, H * DQK), BF16), jax.ShapeDtypeStruct((t, H * DQK), BF16),
                   jax.ShapeDtypeStruct((1, QL), F32), jax.ShapeDtypeStruct((1, KVL), F32)],
        input_output_aliases={0: 0},
        compiler_params=_params(("arbitrary",)),
    )(dproj, proj, dq_rot, dk, dv, pos, g_q, g_kv, wuq, wukv, *tabs)


def _causal_mask(s, n):
    row = lax.broadcasted_iota(jnp.int32, (n, n), 0)
    col = lax.broadcasted_iota(jnp.int32, (n, n), 1)
    return jnp.where(col <= row, s, -1e30)


def flash_fwd(q, k, kv, nb, seq):
    t = q.shape[0]
    tq = min(FLASH_TQ, seq)
    nq = seq // tq

    def body(q_ref, k_ref, v_ref, o_ref, lse_ref):
        for qi in range(nq):
            qs = slice(qi * tq, (qi + 1) * tq)
            qv = q_ref[qs, :]
            m = jnp.full((tq, 1), -1e30, F32)
            l = jnp.zeros((tq, 1), F32)
            acc = jnp.zeros((tq, DV), F32)
            for j in range(qi + 1):
                ks = slice(j * tq, (j + 1) * tq)
                s = _dot_nt(qv, k_ref[ks, :])
                if j == qi:
                    s = _causal_mask(s, tq)
                m_new = jnp.maximum(m, jnp.max(s, axis=1, keepdims=True))
                p = jnp.exp2(s - m_new)
                alpha = jnp.exp2(m - m_new)
                l = alpha * l + jnp.sum(p, axis=1, keepdims=True)
                acc = alpha * acc + _dot(p.astype(BF16), v_ref[ks, :])
                m = m_new
            o_ref[qs, :] = (acc / l).astype(BF16)
            lse_ref[qs, :] = jnp.broadcast_to(m + jnp.log(l) * LOG2E, (tq, DV))

    out_blk = pl.BlockSpec((seq, DV), lambda b, h: (b, h))
    return pl.pallas_call(
        body, name="flash_fwd", grid=(nb, H),
        in_specs=[pl.BlockSpec((seq, DQK), lambda b, h: (b, h)),
                  pl.BlockSpec((seq, DQK), lambda b, h: (b, h)),
                  pl.BlockSpec((seq, DV), lambda b, h: (b, 2 * h + 1))],
        out_specs=[out_blk, out_blk],
        out_shape=[jax.ShapeDtypeStruct((t, H * DV), BF16), jax.ShapeDtypeStruct((t, H * DV), F32)],
        compiler_params=_params(("parallel", "parallel")),
    )(q, k, kv)


def flash_bwd(q, k, kv, o, do, lse, nb, seq, token):
    t = q.shape[0]
    tq = min(FLASH_TQ, seq)
    nq = seq // tq

    def body(q_ref, k_ref, v_ref, o_ref, do_ref, lse_ref, tok_ref, dq_ref, dk_ref, dv_ref):
        delta = []
        for qi in range(nq):
            qs = slice(qi * tq, (qi + 1) * tq)
            delta.append(jnp.sum(do_ref[qs, :].astype(F32) * o_ref[qs, :].astype(F32), axis=1, keepdims=True))
        for ki in range(nq):
            ks = slice(ki * tq, (ki + 1) * tq)
            kb = k_ref[ks, :]
            vb = v_ref[ks, :]
            dk = jnp.zeros((tq, DQK), F32)
            dv = jnp.zeros((tq, DV), F32)
            for qi in range(ki, nq):
                qs = slice(qi * tq, (qi + 1) * tq)
                qv = q_ref[qs, :]
                dov = do_ref[qs, :]
                s = _dot_nt(qv, kb)
                if qi == ki:
                    s = _causal_mask(s, tq)
                p = jnp.exp2(s - lse_ref[qs, :][:, :1])
                dp = _dot_nt(dov, vb)
                dz = (p * (dp - delta[qi])).astype(BF16)
                dv = dv + _dot_tn(p.astype(BF16), dov)
                dk = dk + _dot_tn(dz, qv)
                dqb = _dot(dz, kb)
                if ki == 0:
                    dq_ref[qs, :] = dqb
                else:
                    dq_ref[qs, :] += dqb
            dk_ref[ks, :] = (dk * LN2).astype(BF16)
            dv_ref[ks, :] = dv.astype(BF16)

    full = lambda w, col: pl.BlockSpec((seq, w), col)
    same = lambda b, h: (b, h)
    return pl.pallas_call(
        body, name="flash_bwd", grid=(nb, H),
        in_specs=[full(DQK, same), full(DQK, same), full(DV, lambda b, h: (b, 2 * h + 1)),
                  full(DV, same), full(DV, same), full(DV, same),
                  pl.BlockSpec((8, 128), lambda b, h: (0, 0))],
        out_specs=[full(DQK, same), full(DQK, same), full(DV, same)],
        out_shape=[jax.ShapeDtypeStruct((t, H * DQK), F32), jax.ShapeDtypeStruct((t, H * DQK), BF16),
                   jax.ShapeDtypeStruct((t, H * DV), BF16)],
        compiler_params=_params(("parallel", "parallel")),
    )(q, k, kv, o, do, lse, token)


TAIL_TM = 256


def tail_fwd(y, attn, proj, x2, tgt, gate, g_post, wco, wmo, wout, seq):
    t = y.shape[0]
    nb = t // seq
    tm = min(TAIL_TM, seq)
    tpb = seq // tm

    def body(y_ref, at_ref, p_ref, x_ref, t_ref, gate_ref, gp_ref, wco_ref, wmo_ref, wout_ref,
             o_ref, ya_ref, yb_ref, m_ref, do2_ref, dout_ref, dgate_ref, dgp_ref, loss_ref):
        i = pl.program_id(0)
        bz = p_ref[0].astype(F32)
        ga = p_ref[1].astype(F32)
        gb = p_ref[2].astype(F32)
        ov = (at_ref[...].astype(F32) * (bz * _sig(bz))).astype(BF16)
        o_ref[...] = ov
        ya = _dot(y_ref[...], wco_ref[...])
        yb = _dot(ov, wmo_ref[...])
        ya_ref[...] = ya.astype(BF16)
        yb_ref[...] = yb.astype(BF16)
        mv = (_sig(ga) * ya + _sig(gb) * yb).astype(BF16)
        m_ref[...] = mv
        o2 = _dot(mv, wout_ref[...])
        r = lax.rsqrt(jnp.mean(o2 * o2, axis=-1, keepdims=True) + EPS)
        nrm = o2 * r
        gp = gp_ref[...]
        gate_v = gate_ref[...]
        rn = nrm * gp
        err = x_ref[...] + gate_v * rn - t_ref[...]
        dout = err * (1.0 / D)
        dout_ref[...] = dout
        dn = dout * gate_v * gp
        do2_ref[...] = (r * (dn - nrm * jnp.mean(dn * nrm, axis=-1, keepdims=True))).astype(BF16)

        @pl.when(i % tpb == 0)
        def _():
            dgate_ref[...] = jnp.zeros_like(dgate_ref)

        @pl.when(i == 0)
        def _():
            dgp_ref[...] = jnp.zeros_like(dgp_ref)
            loss_ref[...] = jnp.zeros_like(loss_ref)

        dgate_ref[...] += jnp.sum(dout * rn, axis=0, keepdims=True)
        dgp_ref[...] += jnp.sum(dout * gate_v * nrm, axis=0, keepdims=True)
        loss_ref[...] += 0.5 * jnp.sum(jnp.mean(err * err, axis=-1, keepdims=True), axis=0, keepdims=True)

    row = pl.BlockSpec((tm, D), lambda i: (i, 0))
    per_batch = pl.BlockSpec((None, 1, D), lambda i: (i // tpb, 0, 0))
    vec = pl.BlockSpec((1, D), lambda i: (0, 0))
    wgt = pl.BlockSpec((D, D), lambda i: (0, 0))
    act = jax.ShapeDtypeStruct((t, D), BF16)
    return pl.pallas_call(
        body, name="tail_fwd", grid=(t // tm,),
        in_specs=[row, row, pl.BlockSpec((3, tm, D), lambda i: (0, i, 0)), row, row, per_batch, vec,
                  wgt, wgt, wgt],
        out_specs=[row, row, row, row, row, row, per_batch, vec, pl.BlockSpec((1, 1), lambda i: (0, 0))],
        out_shape=[act, act, act, act, act, jax.ShapeDtypeStruct((t, D), F32),
                   jax.ShapeDtypeStruct((nb, 1, D), F32), jax.ShapeDtypeStruct((1, D), F32),
                   jax.ShapeDtypeStruct((1, 1), F32)],
        compiler_params=_params(("arbitrary",)),
    )(y, attn, proj, x2, tgt, gate, g_post, wco, wmo, wout)


def tail_bwd(do2, proj, ya, yb, attn, wout, wmo, wco):
    t = do2.shape[0]
    tm = min(TAIL_TM, t)

    def body(do2_ref, p_ref, ya_ref, yb_ref, at_ref, wout_ref, wmo_ref, wco_ref,
             dp_ref, dya_ref, dyb_ref, dat_ref, dy_ref):
        bz = p_ref[0].astype(F32)
        ga = p_ref[1].astype(F32)
        gb = p_ref[2].astype(F32)
        dm = _dot_nt(do2_ref[...], wout_ref[...])
        sa = _sig(ga)
        sb = _sig(gb)
        dya = (dm * sa).astype(BF16)
        dyb = (dm * sb).astype(BF16)
        dya_ref[...] = dya
        dyb_ref[...] = dyb
        dp_ref[1] = (dm * ya_ref[...].astype(F32) * (sa * (1.0 - sa))).astype(BF16)
        dp_ref[2] = (dm * yb_ref[...].astype(F32) * (sb * (1.0 - sb))).astype(BF16)
        dov = _dot_nt(dyb, wmo_ref[...])
        sz = _sig(bz)
        dat_ref[...] = (dov * (bz * sz)).astype(BF16)
        dp_ref[0] = (dov * at_ref[...].astype(F32) * (sz * (1.0 + bz * (1.0 - sz)))).astype(BF16)
        dy_ref[...] = _dot_nt(dya, wco_ref[...]).astype(BF16)

    row = pl.BlockSpec((tm, D), lambda i: (i, 0))
    seg3 = pl.BlockSpec((3, tm, D), lambda i: (0, i, 0))
    wgt = pl.BlockSpec((D, D), lambda i: (0, 0))
    act = jax.ShapeDtypeStruct((t, D), BF16)
    return pl.pallas_call(
        body, name="tail_bwd", grid=(t // tm,),
        in_specs=[row, seg3, row, row, row, wgt, wgt, wgt],
        out_specs=[seg3, row, row, row, row],
        out_shape=[jax.ShapeDtypeStruct((NSEG, t, D), BF16), act, act, act, act],
        compiler_params=_params(("parallel",)),
    )(do2, proj, ya, yb, attn, wout, wmo, wco)


def adamw(w, m, v, g, g2, name, token=None):
    rows, cols = w.shape
    tr = rows
    for cand in (256, 128, 64, 32, 16, 8):
        if rows % cand == 0 and rows > cand:
            tr = cand
            break
    has2 = g2 is not None
    n_in = 4 + has2

    def body(*refs):
        w_ref, m_ref, v_ref, g_ref = refs[:4]
        go_ref, d_ref, mo_ref, vo_ref = refs[-4:]
        grad = g_ref[...] + refs[4][...].astype(F32) if has2 else g_ref[...]
        mn = ADAM_B1 * m_ref[...] + (1.0 - ADAM_B1) * grad
        vn = ADAM_B2 * v_ref[...] + (1.0 - ADAM_B2) * (grad * grad)
        m_hat = mn / (1.0 - ADAM_B1 ** ADAM_STEP)
        v_hat = vn / (1.0 - ADAM_B2 ** ADAM_STEP)
        go_ref[...] = grad
        d_ref[...] = -ADAM_LR * (m_hat / (jnp.sqrt(v_hat) + ADAM_EPS) + ADAM_WD * w_ref[...])
        mo_ref[...] = mn
        vo_ref[...] = vn

    blk = pl.BlockSpec((tr, cols), lambda i: (i, 0))
    ins = [w, m, v, g] + ([g2] if has2 else [])
    specs = [blk] * n_in
    if token is not None:
        ins.append(token)
        specs.append(pl.BlockSpec((8, 128), lambda i: (0, 0)))
    return pl.pallas_call(
        body, name=name, grid=(rows // tr,),
        in_specs=specs, out_specs=[blk] * 4,
        out_shape=[jax.ShapeDtypeStruct((rows, cols), F32)] * 4,
        compiler_params=_params(("parallel",)),
    )(*ins)


def adamw_scattered(w, m, v, own, land, me, tr, name, transpose=False):
    slot_rows = land.shape[1]
    cols = land.shape[2]
    rows = slot_rows if transpose else w.shape[0]
    per_slot = slot_rows // tr

    def body(me_ref, w_ref, m_ref, v_ref, own_ref, land_ref, go_ref, d_ref, mo_ref, vo_ref):
        grad = own_ref[...].astype(F32)
        for s in range(8):
            grad = grad + land_ref[s].astype(F32)
        if transpose:
            grad = grad.T
        mn = ADAM_B1 * m_ref[...] + (1.0 - ADAM_B1) * grad
        vn = ADAM_B2 * v_ref[...] + (1.0 - ADAM_B2) * (grad * grad)
        m_hat = mn / (1.0 - ADAM_B1 ** ADAM_STEP)
        v_hat = vn / (1.0 - ADAM_B2 ** ADAM_STEP)
        go_ref[...] = grad
        d_ref[...] = -ADAM_LR * (m_hat / (jnp.sqrt(v_hat) + ADAM_EPS) + ADAM_WD * w_ref[...])
        mo_ref[...] = mn
        vo_ref[...] = vn

    wblk = pl.BlockSpec(w.shape if transpose else (tr, w.shape[1]), lambda i, s: (i, 0))
    return pl.pallas_call(
        body, name=name,
        grid_spec=pltpu.PrefetchScalarGridSpec(
            num_scalar_prefetch=1, grid=(rows // tr,),
            in_specs=[wblk, wblk, wblk,
                      pl.BlockSpec((tr, cols), lambda i, s: (s[0] * per_slot + i, 0)),
                      pl.BlockSpec((8, tr, cols), lambda i, s: (0, i, 0))],
            out_specs=[wblk] * 4),
        out_shape=[jax.ShapeDtypeStruct(w.shape, F32)] * 4,
        compiler_params=_params(),
    )(me, w, m, v, own, land)


def adamw_win(wt, mt, vt, ka, ra, kb, rb):
    rows = wt.shape[0]
    tc = 256
    nh = (D // 2) // tc

    def body(w_ref, m_ref, v_ref, ka_ref, ra_ref, kb_ref, rb_ref, go_ref, d_ref, mo_ref, vo_ref):
        first = pl.program_id(0) < nh
        grad = jnp.where(first, ka_ref[...] + ra_ref[...].astype(F32), kb_ref[...] + rb_ref[...].astype(F32))
        mn = ADAM_B1 * m_ref[...] + (1.0 - ADAM_B1) * grad
        vn = ADAM_B2 * v_ref[...] + (1.0 - ADAM_B2) * (grad * grad)
        m_hat = mn / (1.0 - ADAM_B1 ** ADAM_STEP)
        v_hat = vn / (1.0 - ADAM_B2 ** ADAM_STEP)
        go_ref[...] = grad
        d_ref[...] = -ADAM_LR * (m_hat / (jnp.sqrt(v_hat) + ADAM_EPS) + ADAM_WD * w_ref[...])
        mo_ref[...] = mn
        vo_ref[...] = vn

    blk = pl.BlockSpec((rows, tc), lambda j: (0, j))
    lo = pl.BlockSpec((rows, tc), lambda j: (0, jnp.minimum(j, nh - 1)))
    hi = pl.BlockSpec((rows, tc), lambda j: (0, jnp.maximum(j - nh, 0)))
    return pl.pallas_call(
        body, name="adamw_w_in", grid=(D // tc,),
        in_specs=[blk, blk, blk, lo, lo, hi, hi], out_specs=[blk] * 4,
        out_shape=[jax.ShapeDtypeStruct((rows, D), F32)] * 4,
        compiler_params=_params(("parallel",)),
    )(wt, mt, vt, ka, ra, kb, rb)


_ORD_A = ("x", "y", "c")
_ORD_B = ("y", "x", "c")


def _rows128(a, rows):
    flat = a.reshape(-1)
    return jnp.pad(flat, (0, rows * 128 - flat.shape[0])).reshape(rows, 128)


def kernel(x, c, positions, w_ada, b_ada, g_pre, w_in, conv_w, w_conv_out, g_q, w_uq, g_kv, w_ukv, w_mla_out, w_out, g_post, loss_target, m_w_ada, m_b_ada, m_g_pre, m_w_in, m_conv_w, m_w_conv_out, m_g_q, m_w_uq, m_g_kv, m_w_ukv, m_w_mla_out, m_w_out, m_g_post, v_w_ada, v_b_ada, v_g_pre, v_w_in, v_conv_w, v_w_conv_out, v_g_q, v_w_uq, v_g_kv, v_w_ukv, v_w_mla_out, v_w_out, v_g_post):
    nb, seq, _ = x.shape
    t = nb * seq
    mx, my, mc = lax.axis_index("x"), lax.axis_index("y"), lax.axis_index("c")
    me = 4 * mx + 2 * my + mc
    co = {"x": mx, "y": my, "c": mc}

    x2 = x.reshape(t, D)
    tgt2 = loss_target.reshape(t, D)
    pos2 = positions.reshape(t, 1)

    packed = jnp.concatenate([c.reshape(2 * D // 128, 128), _rows128(conv_w[0], 8)], axis=0)
    gath = small_allgather(packed, "gather_cond")
    c_all = gath[:, :16].reshape(8 * nb, D)
    conv_full = gath[:, 16:19].reshape(8, 3, 128).transpose(1, 0, 2).reshape(3, D)
    conv_full8 = jnp.pad(conv_full, ((0, 5), (0, 0)))
    ada_cols = w_ada.shape[2]
    b_cols = lax.dynamic_slice(b_ada, (0, me * ada_cols), (1, ada_cols))
    mod_part = ada_fwd(c_all, w_ada[0], b_cols)
    mod_g = small_allgather(mod_part.reshape(8 * nb * ada_cols // 128, 128), "gather_mod")
    mod_all = mod_g.reshape(8, 8 * nb, ada_cols).transpose(1, 0, 2).reshape(8 * nb, 8 * ada_cols)
    mod = lax.dynamic_slice(mod_all, (me * nb, 0), (nb, 3 * D))
    shift = mod[:, 0:D].reshape(nb, 1, D)
    scale = mod[:, D:2 * D].reshape(nb, 1, D)
    gate = mod[:, 2 * D:3 * D].reshape(nb, 1, D)

    wt = w_in[0].T.astype(BF16)
    lo = lax.bitcast_convert_type(wt[:, :D // 2], jnp.uint16).astype(jnp.uint32)
    hi = lax.bitcast_convert_type(wt[:, D // 2:], jnp.uint16).astype(jnp.uint32)
    wt_bits = lax.bitcast_convert_type(lo | (hi << 16), F32)
    wt_bits, mod = lax.optimization_barrier((wt_bits, mod))
    shift = mod[:, 0:D].reshape(nb, 1, D)
    scale = mod[:, D:2 * D].reshape(nb, 1, D)
    gate = mod[:, 2 * D:3 * D].reshape(nb, 1, D)
    q4 = D // 4
    r3rd = wt_bits.shape[0] // 3
    plan = [(0, (k * r3rd, r3rd), (g * q4, q4), (_ORD_A, _ORD_B)[g]) for k in range(3) for g in range(2)]
    gw = allgather_big([wt_bits], plan, "gather_w_in")
    late = [w_conv_out[0].astype(BF16), w_mla_out[0].astype(BF16), w_out[0].astype(BF16),
            jnp.pad(w_uq[0].T.astype(BF16), ((0, DQK - 192), (0, 0))), w_ukv[0].T.astype(BF16)]
    gw0, late = lax.optimization_barrier((gw[0], late))
    late_state, late_token = gather_start(late, "gather_late_start")
    wt_bits_all = gw0.reshape(N_IN, D // 2)

    inv_freq = ROPE_THETA ** (-jnp.arange(0, ROPE, 2, dtype=F32) / ROPE)
    invf = jnp.concatenate([inv_freq, inv_freq, jnp.zeros((128 - ROPE,), F32)]).reshape(1, 128)
    lane = np.arange(128)
    tabs = (invf,
            jnp.asarray(np.where(lane < HALF, -1.0, 0.0).reshape(1, 128), F32),
            jnp.asarray(np.where((lane >= HALF) & (lane < ROPE), 1.0, 0.0).reshape(1, 128), F32))

    h = prenorm_fwd(x2, scale, shift, g_pre, seq)
    proj, wt_p = proj_matmul(h, wt_bits_all, late_token)
    y = conv_fwd(proj, conv_full8, seq)
    gl = gather_wait(late_state, y, "gather_late_wait")
    wco = gl[0].reshape(D, D)
    wmo = gl[1].reshape(D, D)
    wout = gl[2].reshape(D, D)
    wuq_p = gl[3].reshape(H * DQK, QL)
    wukv = gl[4].reshape(H * 256, KVL)
    q_rot, k_cat, kv, qn, kvn = mla_prep_fwd(proj, pos2, g_q, g_kv, wuq_p, wukv, tabs)
    attn, lse = flash_fwd(q_rot, k_cat, kv, nb, seq)
    o, ya, yb, m, do2, dout, dgate, dg_post, loss_part = tail_fwd(
        y, attn, proj, x2, tgt2, gate, g_post, wco, wmo, wout, seq)

    dproj, dya, dyb, dattn, dy = tail_bwd(do2, proj, ya, yb, attn, wout, wmo, wco)
    g_wout = grad_matmul(m, do2, "grad_w_square")
    g_wmo = grad_matmul(o, dyb, "grad_w_square")
    g_wco = grad_matmul(y, dya, "grad_w_square")
    sc1, sc1_tok = scatter_start([g_wco, g_wmo, g_wout], "scatter_out_grads_start")
    dproj, dconv = conv_bwd(dproj, proj, dy, conv_full8, seq)
    dq_rot, dk, dv = flash_bwd(q_rot, k_cat, kv, attn, dattn, lse, nb, seq, sc1_tok)
    dproj, dq, dkv, dg_q, dg_kv = mla_prep_bwd(dproj, proj, dq_rot, dk, dv, pos2, g_q, g_kv, wuq_p, wukv, tabs)
    g_wuq_t = grad_matmul(dq, qn, "grad_w_uq")
    g_wukv_t = grad_matmul(dkv, kvn, "grad_w_ukv")
    sc2, sc2_tok = scatter_start([g_wuq_t, g_wukv_t], "scatter_mla_grads_start")
    g_win_p = win_grad_matmul(h, dproj, sc2_tok)

    g_wt = g_win_p.reshape(2, 2, 2, N_IN // 8, D)
    ords = [("c", "y", "x"), ("c", "x", "y")]
    hc = D // 2
    win_shape = (2, 2, N_IN // 8, hc)
    pick_w = lambda col: (lambda ref, cc: ref.at[:, :, 1 - cc["c"], :, pl.ds(col * hc, hc)])
    which1 = [0, 0]
    picks1 = [pick_w(0), pick_w(1)]
    st1, tok1 = swap_start([g_wt], which1, ["c"] * 2, picks1, [win_shape] * 2, "rs_c_start")
    assert nb == 2
    dh0 = dh_matmul(dproj, wt_p, tok1, seq, 0)
    (g_wt,), r1 = swap_wait(st1, dh0, which1, ["c"] * 2, picks1, "rs_c_wait")
    sel_xyc = jnp.stack([mx, my, mc]).astype(jnp.int32)
    sel2 = [jnp.stack([co[o[2]]]).astype(jnp.int32) for o in ords]
    first = [rs_win_add_first(g_wt, r1[0], sel_xyc, 1, 0, "rs_add_first_0"),
             rs_win_add_first(g_wt, r1[1], sel_xyc, 0, 1, "rs_add_first_1")]
    keep1, send1 = zip(*first)
    all4 = [0, 1]
    none4 = [None] * 2
    axes2 = [o[1] for o in ords]
    st2, tok2 = swap_start(list(send1), all4, axes2, none4, [s.shape for s in send1], "rs_ici1_start")

    dh1 = dh_matmul(dproj, wt_p, tok2, seq, 1)
    gx0, dsh0, dsc0, dgp0 = prenorm_bwd(dh0, x2, dout, scale, g_pre, seq, tok2, 0, None)
    _, r2 = swap_wait(st2, (gx0, dh1), all4, axes2, none4, "rs_ici1_wait")
    keep2, send2 = zip(*[rs_add_second(keep1[a], r2[a], sel2[a], "rs_add_second") for a in range(2)])
    axes3 = [o[2] for o in ords]
    st3, tok3 = swap_start(list(send2), all4, axes3, none4, [s.shape for s in send2], "rs_ici2_start")
    grad_x2, dsh1, dsc1, dgp1 = prenorm_bwd(dh1, x2, dout, scale, g_pre, seq, tok3, 1, gx0)
    dshift = jnp.stack([dsh0, dsh1])
    dscale = jnp.stack([dsc0, dsc1])
    dg_pre = dgp0 + dgp1

    dmod = jnp.concatenate([dshift, dscale, dgate], axis=2).reshape(nb * 3 * D // 128, 128)
    small = jnp.concatenate([
        dmod, _rows128(dg_pre, 8), _rows128(dg_post, 8), _rows128(dg_q, 8), _rows128(dg_kv, 8),
        dconv[0:3].reshape(24, 128), _rows128(loss_part, 8)], axis=0)
    small_g = small_allgather(small, "gather_small_grads")
    sums = slot_sum(small_g)
    dmod_all = small_g[:, 0:48].reshape(8 * nb, 3 * D)
    g_bada = (sums[0:24] + sums[24:48]).reshape(1, 3 * D)
    g_gpre = sums[48:56].reshape(1, D)
    g_gpost = sums[56:64].reshape(1, D)
    g_gq = sums[64:67].reshape(1, QL)
    g_gkv = sums[72:74].reshape(1, KVL)
    g_conv_full = sums[80:104].reshape(3, D)
    loss = sums[104, 0]
    g_conv = lax.dynamic_slice(g_conv_full, (0, me * 128), (3, 128))
    dmod_cols = lax.dynamic_slice(dmod_all, (0, me * ada_cols), (8 * nb, ada_cols))
    g_wada = ada_bwd(c_all, dmod_cols)

    res = {}
    res["w_ada"] = [o_[None] for o_ in adamw(w_ada[0], m_w_ada[0], v_w_ada[0], g_wada, None, "adamw_w_ada", tok3)]

    def pack(b_, gp_, gpo_, gq_, gkv_, cw_):
        return jnp.concatenate([_rows128(b_, 24), _rows128(gp_, 8), _rows128(gpo_, 8), _rows128(gq_, 8),
                                _rows128(gkv_, 8), _rows128(cw_, 8)], axis=0)

    sw = pack(b_ada, g_pre, g_post, g_q, g_kv, conv_w)
    sm = pack(m_b_ada, m_g_pre, m_g_post, m_g_q, m_g_kv, m_conv_w)
    sv = pack(v_b_ada, v_g_pre, v_g_post, v_g_q, v_g_kv, v_conv_w)
    sg = pack(g_bada, g_gpre, g_gpost, g_gq, g_gkv, g_conv)
    small_out = adamw(sw, sm, sv, sg, None, "adamw_small", tok3)

    _, r3 = swap_wait(st3, small_out[0], all4, axes3, none4, "rs_ici2_wait")

    (g_wco, g_wmo, g_wout), (l_wco, l_wmo, l_wout) = scatter_wait(sc1, small_out[1], "scatter_out_grads_wait")
    (g_wuq_t, g_wukv_t), (l_wuq, l_wukv) = scatter_wait(sc2, small_out[2], "scatter_mla_grads_wait")

    res["w_in"] = [o_.T[None] for o_ in adamw_win(w_in[0].T, m_w_in[0].T, v_w_in[0].T,
                                                  keep2[0], r3[0], keep2[1], r3[1])]
    me1 = me.reshape(1).astype(jnp.int32)
    res["w_uq"] = [o_.T[None] for o_ in adamw_scattered(
        w_uq[0].T, m_w_uq[0].T, v_w_uq[0].T, g_wuq_t, l_wuq, me1, 64, "adamw_w_uq")]
    res["w_ukv"] = [o_[None] for o_ in adamw_scattered(
        w_ukv[0], m_w_ukv[0], v_w_ukv[0], g_wukv_t, l_wukv, me1, KVL, "adamw_w_ukv", transpose=True)]
    for nm, wv, mv, vv, gg, ll in (("w_conv_out", w_conv_out, m_w_conv_out, v_w_conv_out, g_wco, l_wco),
                                   ("w_mla_out", w_mla_out, m_w_mla_out, v_w_mla_out, g_wmo, l_wmo),
                                   ("w_out", w_out, m_w_out, v_w_out, g_wout, l_wout)):
        res[nm] = [o_[None] for o_ in adamw_scattered(wv[0], mv[0], vv[0], gg, ll, me1, 128, "adamw_square")]

    def unpack(a):
        return {"b_ada": a[0:24].reshape(1, 3 * D), "g_pre": a[24:32].reshape(1, D),
                "g_post": a[32:40].reshape(1, D), "g_q": a[40:43].reshape(1, QL),
                "g_kv": a[48:50].reshape(1, KVL), "conv_w": a[56:59].reshape(-1)[:3 * 128].reshape(1, 3, 128)}

    for nm in ("b_ada", "g_pre", "g_post", "g_q", "g_kv", "conv_w"):
        res[nm] = [unpack(a)[nm] for a in small_out]

    order = ["w_ada", "b_ada", "g_pre", "w_in", "conv_w", "w_conv_out", "g_q", "w_uq", "g_kv", "w_ukv",
             "w_mla_out", "w_out", "g_post"]
    out = [loss, grad_x2.reshape(nb, seq, D)]
    for k_ in range(4):
        out += [res[nm][k_] for nm in order]
    return tuple(out)
```

```python
import functools

import numpy as np
import jax
import jax.numpy as jnp
from jax import lax
from jax.experimental import pallas as pl
from jax.experimental.pallas import tpu as pltpu

F32 = jnp.float32
BF16 = jnp.bfloat16
MESH = pl.DeviceIdType.MESH

D = 1024
H = 8
QL = 384
KVL = 256
ROPE = 64
HALF = ROPE // 2
DQK = 256
DV = 128
NSEG = 8
NP = NSEG * D
EPS = 1e-6
ROPE_THETA = 10000.0
SM_SCALE = (128 + ROPE) ** -0.5
LOG2E = 1.4426950408889634
LN2 = 0.6931471805599453
FLASH_TQ = 512

SEG_BZ, SEG_GA, SEG_GB, SEG_LAT, SEG_V = 0, 1, 2, 3, 4

ADAM_LR = 0.001
ADAM_B1 = 0.9
ADAM_B2 = 0.999
ADAM_EPS = 1e-08
ADAM_WD = 0.01
ADAM_STEP = 10

VMEM_LIMIT = 56 * 1024 * 1024


def _params(sem=None, vmem=VMEM_LIMIT):
    kw = dict(vmem_limit_bytes=vmem)
    if sem is not None:
        kw["dimension_semantics"] = sem
    return pltpu.CompilerParams(**kw)


def _sig(v):
    return 0.5 * jnp.tanh(0.5 * v) + 0.5


def _dot(a, b):
    return jnp.dot(a, b, preferred_element_type=F32)


def _dot_nt(a, b):
    return lax.dot_general(a, b, (((1,), (1,)), ((), ())), preferred_element_type=F32)


def _dot_tn(a, b):
    return lax.dot_general(a, b, (((0,), (0,)), ((), ())), preferred_element_type=F32)


_AXIS_POS = {"x": 0, "y": 1, "c": 2}


def _coords():
    return lax.axis_index("x"), lax.axis_index("y"), lax.axis_index("c")


def _partner(axis):
    p = list(_coords())
    p[_AXIS_POS[axis]] = 1 - p[_AXIS_POS[axis]]
    return tuple(p)


def small_allgather(v, name):
    rows = v.shape[0]

    def body(v_ref, out_ref, send_sems, recv_sems):
        x, y, c = _coords()
        me = 4 * x + 2 * y + c
        out_ref[me] = v_ref[...]
        copies = []
        for k in range(1, 8):
            peer = (1 - x if k & 4 else x, 1 - y if k & 2 else y, 1 - c if k & 1 else c)
            cp = pltpu.make_async_remote_copy(
                src_ref=v_ref, dst_ref=out_ref.at[me],
                send_sem=send_sems.at[k - 1], recv_sem=recv_sems.at[k - 1],
                device_id=peer, device_id_type=MESH)
            cp.start()
            copies.append(cp)
        for cp in copies:
            cp.wait()

    return pl.pallas_call(
        body, name=name,
        out_shape=jax.ShapeDtypeStruct((8, rows, 128), F32),
        in_specs=[pl.BlockSpec(memory_space=pltpu.VMEM)],
        out_specs=pl.BlockSpec(memory_space=pltpu.VMEM),
        scratch_shapes=[pltpu.SemaphoreType.DMA((7,)), pltpu.SemaphoreType.DMA((7,))],
    )(v)


def _own_block_placed(s):
    x, y, c = _coords()
    return lax.dynamic_update_slice(lax.empty((2, 2, 2) + s.shape, s.dtype), s[None, None, None],
                                    (x, y, c) + (0,) * s.ndim)


def allgather_big(arrs, plan, name):
    n = len(arrs)
    m = len(plan)

    def body(*refs):
        ins, outs = refs[n:2 * n], refs[2 * n:3 * n]
        send_sems, recv_sems = refs[3 * n:]
        x, y, c = _coords()
        co = {"x": x, "y": y, "c": c}

        def window(ref, lead, rows, cols):
            win = tuple(slice(None) if w is None else pl.ds(w[0], w[1]) for w in (rows, cols))
            return ref.at[tuple(lead) + win]

        def held(e, free):
            i, rows, cols, _ = plan[e]
            lead = [slice(None) if ax in free else co[ax] for ax in ("x", "y", "c")]
            return window(outs[i], lead, rows, cols)

        def rcopy(e, stage, src, dst, axis):
            return pltpu.make_async_remote_copy(
                src_ref=src, dst_ref=dst,
                send_sem=send_sems.at[e, stage], recv_sem=recv_sems.at[e, stage],
                device_id=_partner(axis), device_id_type=MESH)

        stages = [[], [], []]
        for e, (i, rows, cols, order) in enumerate(plan):
            cp = rcopy(e, 0, window(ins[i], [], rows, cols), held(e, ()), order[0])
            cp.start()
            stages[0].append(cp)
        for s in (1, 2):
            for e, (i, rows, cols, order) in enumerate(plan):
                stages[s - 1][e].wait_recv()
                blk = held(e, order[:s])
                cp = rcopy(e, s, blk, blk, order[s])
                cp.start()
                stages[s].append(cp)
        for e in range(m):
            stages[2][e].wait_recv()
        for e in range(m):
            for s in range(3):
                stages[s][e].wait_send()

    any_spec = pl.BlockSpec(memory_space=pl.ANY)
    lands = [_own_block_placed(a) for a in arrs]
    return pl.pallas_call(
        body, name=name,
        out_shape=[jax.ShapeDtypeStruct(l.shape, l.dtype) for l in lands],
        in_specs=[any_spec] * (2 * n),
        out_specs=[any_spec] * n,
        input_output_aliases={i: i for i in range(n)},
        scratch_shapes=[pltpu.SemaphoreType.DMA((m, 3)), pltpu.SemaphoreType.DMA((m, 3))],
    )(*lands, *arrs)


def exchange(arrs, axes, picks, out_shapes, name):
    n = len(arrs)

    def body(*refs):
        ins, outs = refs[:n], refs[n:2 * n]
        send_sems, recv_sems = refs[2 * n:]
        x, y, c = _coords()
        co = {"x": x, "y": y, "c": c}
        copies = []
        for a in range(n):
            src = ins[a] if picks[a] is None else picks[a](ins[a], co)
            cp = pltpu.make_async_remote_copy(
                src_ref=src, dst_ref=outs[a],
                send_sem=send_sems.at[a], recv_sem=recv_sems.at[a],
                device_id=_partner(axes[a]), device_id_type=MESH)
            cp.start()
            copies.append(cp)
        for cp in copies:
            cp.wait()

    any_spec = pl.BlockSpec(memory_space=pl.ANY)
    return pl.pallas_call(
        body, name=name,
        out_shape=[jax.ShapeDtypeStruct(s, a.dtype) for s, a in zip(out_shapes, arrs)],
        in_specs=[any_spec] * n,
        out_specs=[any_spec] * n,
        scratch_shapes=[pltpu.SemaphoreType.DMA((n,)), pltpu.SemaphoreType.DMA((n,))],
    )(*arrs)


_HBM = pl.BlockSpec(memory_space=pltpu.HBM)
_SEM = pl.BlockSpec(memory_space=pltpu.SEMAPHORE)


def _swap_copies(srcs, lands, send_sems, recv_sems, axes, picks):
    x, y, c = _coords()
    co = {"x": x, "y": y, "c": c}
    return [pltpu.make_async_remote_copy(
        src_ref=srcs[a] if picks[a] is None else picks[a](srcs[a], co), dst_ref=lands[a],
        send_sem=send_sems.at[a], recv_sem=recv_sems.at[a],
        device_id=_partner(axes[a]), device_id_type=MESH) for a in range(len(srcs))]


def swap_start(arrs, which, axes, picks, out_shapes, name):
    ns, n = len(arrs), len(which)

    def body(*refs):
        srcs, lands = refs[:ns], refs[ns:ns + n]
        send_sems, recv_sems = refs[ns + n:ns + n + 2]
        token = refs[-1]
        for cp in _swap_copies([srcs[i] for i in which], lands, send_sems, recv_sems, axes, picks):
            cp.start()
        token[...] = jnp.zeros_like(token)

    lands = [lax.empty(s, arrs[i].dtype) for s, i in zip(out_shapes, which)]
    ops = [pltpu.with_memory_space_constraint(a, pltpu.HBM) for a in list(arrs) + lands]
    out = pl.pallas_call(
        body, name=name,
        out_shape=[pltpu.SemaphoreType.DMA((n,)), pltpu.SemaphoreType.DMA((n,))]
        + [pltpu.HBM(o.shape, o.dtype) for o in ops] + [jax.ShapeDtypeStruct((8, 128), F32)],
        in_specs=[_HBM] * (ns + n),
        out_specs=[_SEM, _SEM] + [_HBM] * (ns + n) + [pl.BlockSpec(memory_space=pltpu.VMEM)],
        input_output_aliases={i: 2 + i for i in range(ns + n)},
        compiler_params=pltpu.CompilerParams(has_side_effects=pltpu.SideEffectType.DATAFLOW_SIDE_EFFECTING),
    )(*ops)
    return out[:-1], out[-1]


def swap_wait(state, after, which, axes, picks, name):
    n = len(which)
    ns = len(state) - 2 - n

    def body(*refs):
        srcs, lands = refs[:ns], refs[ns:ns + n]
        send_sems, recv_sems = refs[ns + n:ns + n + 2]
        for cp in _swap_copies([srcs[i] for i in which], lands, send_sems, recv_sems, axes, picks):
            cp.wait_send()
            cp.wait_recv()

    thru = list(state[2:])
    after = list(after) if isinstance(after, (list, tuple)) else [after]
    out = pl.pallas_call(
        body, name=name,
        out_shape=[pltpu.HBM(o.shape, o.dtype) for o in thru],
        in_specs=[_HBM] * (ns + n) + [_SEM, _SEM] + [pl.BlockSpec(memory_space=pl.ANY)] * len(after),
        out_specs=[_HBM] * (ns + n),
        input_output_aliases={i: i for i in range(ns + n)},
        compiler_params=pltpu.CompilerParams(has_side_effects=pltpu.SideEffectType.DATAFLOW_SIDE_EFFECTING),
    )(*thru, state[0], state[1], *after)
    return out[:ns], out[ns:]


def _gather_copies(shards, lands, send_sems, recv_sems):
    x, y, c = _coords()
    copies = []
    for a in range(len(shards)):
        for k in range(1, 8):
            peer = (1 - x if k & 4 else x, 1 - y if k & 2 else y, 1 - c if k & 1 else c)
            copies.append(pltpu.make_async_remote_copy(
                src_ref=shards[a], dst_ref=lands[a].at[x, y, c],
                send_sem=send_sems.at[7 * a + k - 1], recv_sem=recv_sems.at[7 * a + k - 1],
                device_id=peer, device_id_type=MESH))
    return copies


def gather_start(shards, name):
    n = len(shards)
    x, y, c = _coords()

    def body(*refs):
        srcs, lands = refs[:n], refs[n:2 * n]
        send_sems, recv_sems = refs[2 * n:2 * n + 2]
        token = refs[-1]
        for cp in _gather_copies(srcs, lands, send_sems, recv_sems):
            cp.start()
        token[...] = jnp.zeros_like(token)

    lands = [_own_block_placed(s) for s in shards]
    ops = [pltpu.with_memory_space_constraint(a, pltpu.HBM) for a in list(shards) + lands]
    out = pl.pallas_call(
        body, name=name,
        out_shape=[pltpu.SemaphoreType.DMA((7 * n,)), pltpu.SemaphoreType.DMA((7 * n,))]
        + [pltpu.HBM(o.shape, o.dtype) for o in ops] + [jax.ShapeDtypeStruct((8, 128), F32)],
        in_specs=[_HBM] * (2 * n),
        out_specs=[_SEM, _SEM] + [_HBM] * (2 * n) + [pl.BlockSpec(memory_space=pltpu.VMEM)],
        input_output_aliases={i: 2 + i for i in range(2 * n)},
        compiler_params=pltpu.CompilerParams(has_side_effects=pltpu.SideEffectType.DATAFLOW_SIDE_EFFECTING),
    )(*ops)
    return out[:-1], out[-1]


def gather_wait(state, after, name):
    n = (len(state) - 2) // 2

    def body(*refs):
        srcs, lands = refs[:n], refs[n:2 * n]
        send_sems, recv_sems = refs[2 * n:2 * n + 2]
        for cp in _gather_copies(srcs, lands, send_sems, recv_sems):
            cp.wait_send()
            cp.wait_recv()

    thru = list(state[2:])
    out = pl.pallas_call(
        body, name=name,
        out_shape=[pltpu.HBM(o.shape, o.dtype) for o in thru],
        in_specs=[_HBM] * (2 * n) + [_SEM, _SEM, pl.BlockSpec(memory_space=pl.ANY)],
        out_specs=[_HBM] * (2 * n),
        input_output_aliases={i: i for i in range(2 * n)},
        compiler_params=pltpu.CompilerParams(has_side_effects=pltpu.SideEffectType.DATAFLOW_SIDE_EFFECTING),
    )(*thru, state[0], state[1], after)
    return out[n:]


def _scatter_copies(grads, lands, send_sems, recv_sems):
    x, y, c = _coords()
    me = 4 * x + 2 * y + c
    copies = []
    for a in range(len(grads)):
        r = grads[a].shape[0] // 8
        for k in range(1, 8):
            px, py, pc = (1 - x if k & 4 else x, 1 - y if k & 2 else y, 1 - c if k & 1 else c)
            rows = pl.ds(pl.multiple_of((4 * px + 2 * py + pc) * r, r), r)
            copies.append(pltpu.make_async_remote_copy(
                src_ref=grads[a].at[rows], dst_ref=lands[a].at[me],
                send_sem=send_sems.at[7 * a + k - 1], recv_sem=recv_sems.at[7 * a + k - 1],
                device_id=(px, py, pc), device_id_type=MESH))
    return copies


def scatter_start(grads, name):
    n = len(grads)

    def body(*refs):
        srcs, lands = refs[:n], refs[n:2 * n]
        send_sems, recv_sems = refs[2 * n:2 * n + 2]
        token = refs[-1]
        for cp in _scatter_copies(srcs, lands, send_sems, recv_sems):
            cp.start()
        token[...] = jnp.zeros_like(token)

    lands = [jnp.zeros((8, g.shape[0] // 8, g.shape[1]), g.dtype) for g in grads]
    ops = [pltpu.with_memory_space_constraint(a, pltpu.HBM) for a in list(grads) + lands]
    out = pl.pallas_call(
        body, name=name,
        out_shape=[pltpu.SemaphoreType.DMA((7 * n,)), pltpu.SemaphoreType.DMA((7 * n,))]
        + [pltpu.HBM(o.shape, o.dtype) for o in ops] + [jax.ShapeDtypeStruct((8, 128), F32)],
        in_specs=[_HBM] * (2 * n),
        out_specs=[_SEM, _SEM] + [_HBM] * (2 * n) + [pl.BlockSpec(memory_space=pltpu.VMEM)],
        input_output_aliases={i: 2 + i for i in range(2 * n)},
        compiler_params=pltpu.CompilerParams(has_side_effects=pltpu.SideEffectType.DATAFLOW_SIDE_EFFECTING),
    )(*ops)
    return out[:-1], out[-1]


def scatter_wait(state, after, name):
    n = (len(state) - 2) // 2

    def body(*refs):
        srcs, lands = refs[:n], refs[n:2 * n]
        send_sems, recv_sems = refs[2 * n:2 * n + 2]
        for cp in _scatter_copies(srcs, lands, send_sems, recv_sems):
            cp.wait_send()
            cp.wait_recv()

    thru = list(state[2:])
    after = list(after) if isinstance(after, (list, tuple)) else [after]
    out = pl.pallas_call(
        body, name=name,
        out_shape=[pltpu.HBM(o.shape, o.dtype) for o in thru],
        in_specs=[_HBM] * (2 * n) + [_SEM, _SEM] + [pl.BlockSpec(memory_space=pl.ANY)] * len(after),
        out_specs=[_HBM] * (2 * n),
        input_output_aliases={i: i for i in range(2 * n)},
        compiler_params=pltpu.CompilerParams(has_side_effects=pltpu.SideEffectType.DATAFLOW_SIDE_EFFECTING),
    )(*thru, state[0], state[1], *after)
    return out[:n], out[n:]


def rs_win_add_first(g, r, sel, next_dim, col, name):
    rows, cols = r.shape[2:]

    def body(sel_ref, gk_ref, rk_ref, gs_ref, rs_ref, keep_ref, send_ref):
        keep_ref[...] = gk_ref[...] + rk_ref[...]
        send_ref[...] = (gs_ref[...] + rs_ref[...]).astype(BF16)

    def g_map(flip):
        def f(j, s):
            nxt = 1 - s[next_dim] if flip else s[next_dim]
            return (nxt, j, s[2], 0, col) if next_dim == 0 else (j, nxt, s[2], 0, col)
        return f

    def r_map(flip):
        def f(j, s):
            nxt = 1 - s[next_dim] if flip else s[next_dim]
            return (nxt, j, 0, 0) if next_dim == 0 else (j, nxt, 0, 0)
        return f

    gblk = (None, None, None, rows, cols)
    rblk = (None, None, rows, cols)
    oblk = (None, rows, cols)
    return pl.pallas_call(
        body, name=name,
        grid_spec=pltpu.PrefetchScalarGridSpec(
            num_scalar_prefetch=1, grid=(2,),
            in_specs=[pl.BlockSpec(gblk, g_map(False)), pl.BlockSpec(rblk, r_map(False)),
                      pl.BlockSpec(gblk, g_map(True)), pl.BlockSpec(rblk, r_map(True))],
            out_specs=[pl.BlockSpec(oblk, lambda j, s: (j, 0, 0)),
                       pl.BlockSpec(oblk, lambda j, s: (j, 0, 0))]),
        out_shape=[jax.ShapeDtypeStruct((2, rows, cols), F32),
                   jax.ShapeDtypeStruct((2, rows, cols), BF16)],
        compiler_params=_params(),
    )(sel, g, r, g, r)


def rs_add_second(k, r, sel, name):
    _, rows, cols = k.shape
    tr = rows // 2 if rows % 32 == 0 else rows
    nt = rows // tr

    def body(sel_ref, kk_ref, rk_ref, ks_ref, rs_ref, keep_ref, send_ref):
        keep_ref[...] = kk_ref[...] + rk_ref[...].astype(F32)
        send_ref[...] = (ks_ref[...] + rs_ref[...].astype(F32)).astype(BF16)

    blk = (None, tr, cols)
    oblk = (tr, cols)
    return pl.pallas_call(
        body, name=name,
        grid_spec=pltpu.PrefetchScalarGridSpec(
            num_scalar_prefetch=1, grid=(nt,),
            in_specs=[
                pl.BlockSpec(blk, lambda i, s: (s[0], i, 0)),
                pl.BlockSpec(blk, lambda i, s: (s[0], i, 0)),
                pl.BlockSpec(blk, lambda i, s: (1 - s[0], i, 0)),
                pl.BlockSpec(blk, lambda i, s: (1 - s[0], i, 0)),
            ],
            out_specs=[pl.BlockSpec(oblk, lambda i, s: (i, 0)),
                       pl.BlockSpec(oblk, lambda i, s: (i, 0))]),
        out_shape=[jax.ShapeDtypeStruct((rows, cols), F32),
                   jax.ShapeDtypeStruct((rows, cols), BF16)],
        compiler_params=_params(),
    )(sel, k, r, k, r)


SEG_ROWS = (4800, 5824, 6848, 4096, 0, 1024, 2048, 3072)
LAT_ROWS = QL + KVL + ROPE
N_IN = 7872


def _seg_row(j):
    return pl.multiple_of(jnp.where(j < 3, 4800 + 1024 * j, jnp.where(j == 3, 4096, (j - 4) * 1024)), 8)


def proj_matmul(h, wt_bits, token):
    t = h.shape[0]
    tm = min(1024, t)

    def body(h_ref, w_hbm, tok_ref, o_ref, wt_ref, buf, sems):
        j = pl.program_id(0)
        slot = j % 2

        def fetch(seg, into):
            return pltpu.make_async_copy(w_hbm.at[pl.ds(_seg_row(seg), D)], buf.at[into], sems.at[into])

        @pl.when(pl.program_id(1) == 0)
        def _():
            @pl.when(j == 0)
            def _():
                fetch(j, slot).start()

            fetch(j, slot).wait()

            @pl.when(j + 1 < NSEG)
            def _():
                fetch(j + 1, 1 - slot).start()

            bits = pltpu.bitcast(buf[slot], jnp.uint32)
            row = lax.broadcasted_iota(jnp.int32, (D, D // 2), 0)
            live = jnp.logical_or(j != SEG_LAT, row < LAT_ROWS)
            lo = pltpu.bitcast(bits << 16, F32)
            hi = pltpu.bitcast(bits & jnp.uint32(0xFFFF0000), F32)
            wt_ref[:, :D // 2] = jnp.where(live, lo, 0.0).astype(BF16)
            wt_ref[:, D // 2:] = jnp.where(live, hi, 0.0).astype(BF16)

        o_ref[...] = _dot_nt(h_ref[...], wt_ref[...]).astype(BF16)

    return pl.pallas_call(
        body, name="proj_matmul", grid=(NSEG, t // tm),
        in_specs=[pl.BlockSpec((tm, D), lambda j, i: (i, 0)),
                  pl.BlockSpec(memory_space=pl.ANY),
                  pl.BlockSpec((8, 128), lambda j, i: (0, 0))],
        out_specs=[pl.BlockSpec((None, tm, D), lambda j, i: (j, i, 0)),
                   pl.BlockSpec((D, D), lambda j, i: (j, 0))],
        out_shape=[jax.ShapeDtypeStruct((NSEG, t, D), BF16), jax.ShapeDtypeStruct((NP, D), BF16)],
        scratch_shapes=[pltpu.VMEM((2, D, D // 2), F32), pltpu.SemaphoreType.DMA((2,))],
        compiler_params=_params(("arbitrary", "arbitrary")),
    )(h, wt_bits, token)


def dh_matmul(dproj, wt, token, seq, b):
    tm = min(1024, seq)
    nblk = seq // tm

    per = 2

    def body(b_ref, d_ref, w_ref, tok_ref, o_ref, acc_ref):
        k = pl.program_id(1)

        @pl.when(k == 0)
        def _():
            acc_ref[...] = jnp.zeros_like(acc_ref)

        part = _dot(d_ref[0], w_ref[0:D, :])
        for j in range(1, per):
            part = part + _dot(d_ref[j], w_ref[j * D:(j + 1) * D, :])
        acc_ref[...] += part

        @pl.when(k == NSEG // per - 1)
        def _():
            o_ref[...] = acc_ref[...]

    return pl.pallas_call(
        body, name="dh_matmul",
        grid_spec=pltpu.PrefetchScalarGridSpec(
            num_scalar_prefetch=1, grid=(nblk, NSEG // per),
            in_specs=[pl.BlockSpec((per, tm, D), lambda i, k, s: (k, s[0] * nblk + i, 0)),
                      pl.BlockSpec((per * D, D), lambda i, k, s: (k, 0)),
                      pl.BlockSpec((8, 128), lambda i, k, s: (0, 0))],
            out_specs=pl.BlockSpec((tm, D), lambda i, k, s: (i, 0)),
            scratch_shapes=[pltpu.VMEM((tm, D), F32)]),
        out_shape=jax.ShapeDtypeStruct((seq, D), F32),
        compiler_params=_params(("parallel", "arbitrary")),
    )(jnp.full((1,), b, jnp.int32), dproj, wt, token)


def win_grad_matmul(h, dproj, token):
    t = h.shape[0]
    tk = min(2048, t)
    nk = t // tk

    def body(h_ref, d_ref, tok_ref, o_hbm, acc_ref, sem):
        j = pl.program_id(0)
        k = pl.program_id(1)

        @pl.when(k == 0)
        def _():
            acc_ref[...] = jnp.zeros_like(acc_ref)

        acc_ref[...] += _dot_tn(d_ref[...], h_ref[...])

        @pl.when(jnp.logical_and(k == nk - 1, j != SEG_LAT))
        def _():
            cp = pltpu.make_async_copy(acc_ref, o_hbm.at[pl.ds(_seg_row(j), D)], sem)
            cp.start()
            cp.wait()

        @pl.when(jnp.logical_and(k == nk - 1, j == SEG_LAT))
        def _():
            cp = pltpu.make_async_copy(acc_ref.at[pl.ds(0, LAT_ROWS)],
                                       o_hbm.at[pl.ds(SEG_ROWS[SEG_LAT], LAT_ROWS)], sem)
            cp.start()
            cp.wait()

    return pl.pallas_call(
        body, name="win_grad_matmul", grid=(NSEG, nk),
        in_specs=[pl.BlockSpec((tk, D), lambda j, k: (k, 0)),
                  pl.BlockSpec((None, tk, D), lambda j, k: (j, k, 0)),
                  pl.BlockSpec((8, 128), lambda j, k: (0, 0))],
        out_specs=pl.BlockSpec(memory_space=pl.ANY),
        out_shape=jax.ShapeDtypeStruct((N_IN, D), F32),
        scratch_shapes=[pltpu.VMEM((D, D), F32), pltpu.SemaphoreType.DMA],
        compiler_params=_params(("arbitrary", "arbitrary")),
    )(h, dproj, token)


def grad_matmul(a, b, name):
    t, m = a.shape
    n = b.shape[1]
    tk = min(1024, t)
    nk = t // tk

    def body(a_ref, b_ref, o_ref, acc_ref):
        k = pl.program_id(0)

        @pl.when(k == 0)
        def _():
            acc_ref[...] = jnp.zeros_like(acc_ref)

        acc_ref[...] += _dot_tn(a_ref[...], b_ref[...])

        @pl.when(k == nk - 1)
        def _():
            o_ref[...] = acc_ref[...].astype(BF16)

    return pl.pallas_call(
        body, name=name, grid=(nk,),
        in_specs=[pl.BlockSpec((tk, m), lambda k: (k, 0)),
                  pl.BlockSpec((tk, n), lambda k: (k, 0))],
        out_specs=pl.BlockSpec((m, n), lambda k: (0, 0)),
        out_shape=jax.ShapeDtypeStruct((m, n), BF16),
        scratch_shapes=[pltpu.VMEM((m, n), F32)],
        compiler_params=_params(("arbitrary",)),
    )(a, b)


def ada_fwd(c_all, w_ada, b_cols):
    def body(c_ref, w_ref, b_ref, o_ref):
        o_ref[...] = _dot(c_ref[...].astype(BF16), w_ref[...].astype(BF16)) + b_ref[...]

    return pl.pallas_call(
        body, name="ada_fwd",
        out_shape=jax.ShapeDtypeStruct((c_all.shape[0], w_ada.shape[1]), F32),
        compiler_params=_params(),
    )(c_all, w_ada, b_cols)


def ada_bwd(c_all, dmod_cols):
    def body(c_ref, d_ref, o_ref):
        o_ref[...] = _dot_tn(c_ref[...].astype(BF16), d_ref[...].astype(BF16))

    return pl.pallas_call(
        body, name="ada_bwd",
        out_shape=jax.ShapeDtypeStruct((c_all.shape[1], dmod_cols.shape[1]), F32),
        compiler_params=_params(),
    )(c_all, dmod_cols)


def slot_sum(g):
    def body(g_ref, o_ref):
        acc = g_ref[0]
        for s in range(1, 8):
            acc = acc + g_ref[s]
        o_ref[...] = acc

    return pl.pallas_call(
        body, name="slot_sum",
        out_shape=jax.ShapeDtypeStruct(g.shape[1:], F32),
    )(g)


def prenorm_fwd(x2, scale, shift, g_pre, seq):
    t = x2.shape[0]
    tm = min(512, seq)
    tpb = seq // tm

    def body(x_ref, sc_ref, sh_ref, g_ref, h_ref):
        xv = x_ref[...]
        r = lax.rsqrt(jnp.mean(xv * xv, axis=-1, keepdims=True) + EPS)
        hv = (xv * r * g_ref[...]) * (1.0 + sc_ref[...]) + sh_ref[...]
        h_ref[...] = hv.astype(BF16)

    per_batch = pl.BlockSpec((None, 1, D), lambda i: (i // tpb, 0, 0))
    return pl.pallas_call(
        body, name="prenorm_fwd", grid=(t // tm,),
        in_specs=[pl.BlockSpec((tm, D), lambda i: (i, 0)), per_batch, per_batch,
                  pl.BlockSpec((1, D), lambda i: (0, 0))],
        out_specs=pl.BlockSpec((tm, D), lambda i: (i, 0)),
        out_shape=jax.ShapeDtypeStruct((t, D), BF16),
        compiler_params=_params(("parallel",)),
    )(x2, scale, shift, g_pre)


def prenorm_bwd(dh, x2, dout, scale, g_pre, seq, token, b, gx_prev):
    t = x2.shape[0]
    tm = min(512, seq)
    tpb = seq // tm
    if gx_prev is None:
        gx_prev = lax.empty((t, D), F32)

    def body(b_ref, dh_ref, x_ref, do_ref, sc_ref, g_ref, tok_ref, gxp_ref, gx_ref, dsh_ref, dsc_ref, dg_ref):
        i = pl.program_id(0)
        xv = x_ref[...]
        dhv = dh_ref[...]
        g = g_ref[...]
        r = lax.rsqrt(jnp.mean(xv * xv, axis=-1, keepdims=True) + EPS)
        nrm = xv * r
        dxn = dhv * (1.0 + sc_ref[...])
        dn = dxn * g
        dx = r * (dn - nrm * jnp.mean(dn * nrm, axis=-1, keepdims=True))
        gx_ref[...] = dx + do_ref[...]

        @pl.when(i == 0)
        def _():
            dsh_ref[...] = jnp.zeros_like(dsh_ref)
            dsc_ref[...] = jnp.zeros_like(dsc_ref)
            dg_ref[...] = jnp.zeros_like(dg_ref)

        dsh_ref[...] += jnp.sum(dhv, axis=0, keepdims=True)
        dsc_ref[...] += jnp.sum(dhv * (nrm * g), axis=0, keepdims=True)
        dg_ref[...] += jnp.sum(dxn * nrm, axis=0, keepdims=True)

    row = pl.BlockSpec((tm, D), lambda i, s: (i, 0))
    grow = pl.BlockSpec((tm, D), lambda i, s: (s[0] * tpb + i, 0))
    per_batch = pl.BlockSpec((None, 1, D), lambda i, s: (s[0], 0, 0))
    vec = pl.BlockSpec((1, D), lambda i, s: (0, 0))
    return pl.pallas_call(
        body, name="prenorm_bwd",
        grid_spec=pltpu.PrefetchScalarGridSpec(
            num_scalar_prefetch=1, grid=(tpb,),
            in_specs=[row, grow, grow, per_batch, vec, pl.BlockSpec((8, 128), lambda i, s: (0, 0)),
                      pl.BlockSpec(memory_space=pl.ANY)],
            out_specs=[grow, vec, vec, vec]),
        out_shape=[jax.ShapeDtypeStruct((t, D), F32), jax.ShapeDtypeStruct((1, D), F32),
                   jax.ShapeDtypeStruct((1, D), F32), jax.ShapeDtypeStruct((1, D), F32)],
        input_output_aliases={7: 0},
        compiler_params=_params(("arbitrary",)),
    )(jnp.full((1,), b, jnp.int32), dh, x2, dout, scale, g_pre, token, gx_prev)


CONV_TC = 128


def _shift_down(u, k, rows):
    idx = lax.broadcasted_iota(jnp.int32, u.shape, 0)
    return jnp.where(idx >= k, pltpu.roll(u, k, 0), 0.0)


def _shift_up(u, k, rows):
    idx = lax.broadcasted_iota(jnp.int32, u.shape, 0)
    return jnp.where(idx < rows - k, pltpu.roll(u, rows - k, 0), 0.0)


def conv_fwd(proj, conv_w, seq):
    t = proj.shape[1]
    nb = t // seq

    def body(p_ref, w_ref, y_ref):
        av = p_ref[0].astype(F32)
        ab = p_ref[1].astype(F32)
        ac = p_ref[2].astype(F32)
        az = p_ref[3].astype(F32)
        w = w_ref[...]
        u = ac * av
        y1 = _shift_down(u, 2, seq) * w[0:1] + _shift_down(u, 1, seq) * w[1:2] + u * w[2:3]
        y_ref[...] = (ab * y1 * (az * _sig(az))).astype(BF16)

    return pl.pallas_call(
        body, name="conv_fwd", grid=(nb, D // CONV_TC),
        in_specs=[pl.BlockSpec((4, seq, CONV_TC), lambda b, ci: (1, b, ci)),
                  pl.BlockSpec((8, CONV_TC), lambda b, ci: (0, ci))],
        out_specs=pl.BlockSpec((seq, CONV_TC), lambda b, ci: (b, ci)),
        out_shape=jax.ShapeDtypeStruct((t, D), BF16),
        compiler_params=_params(("parallel", "parallel")),
    )(proj, conv_w)


def conv_bwd(dproj, proj, dy, conv_w, seq):
    t = proj.shape[1]
    nb = t // seq

    def body(dp_in_ref, p_ref, dy_ref, w_ref, dp_ref, dw_ref):
        b = pl.program_id(1)
        av = p_ref[0].astype(F32)
        ab = p_ref[1].astype(F32)
        ac = p_ref[2].astype(F32)
        az = p_ref[3].astype(F32)
        dyv = dy_ref[...].astype(F32)
        w = w_ref[...]
        u = ac * av
        u1 = _shift_down(u, 1, seq)
        u2 = _shift_down(u, 2, seq)
        y1 = u2 * w[0:1] + u1 * w[1:2] + u * w[2:3]
        sz = _sig(az)
        silu = az * sz
        dy1 = dyv * ab * silu
        du = dy1 * w[2:3] + _shift_up(dy1, 1, seq) * w[1:2] + _shift_up(dy1, 2, seq) * w[0:1]
        dp_ref[0] = (du * ac).astype(BF16)
        dp_ref[1] = (dyv * y1 * silu).astype(BF16)
        dp_ref[2] = (du * av).astype(BF16)
        dp_ref[3] = (dyv * ab * y1 * (sz * (1.0 + az * (1.0 - sz)))).astype(BF16)

        @pl.when(b == 0)
        def _():
            dw_ref[...] = jnp.zeros_like(dw_ref)

        dw_ref[0:1, :] += jnp.sum(dy1 * u2, axis=0, keepdims=True)
        dw_ref[1:2, :] += jnp.sum(dy1 * u1, axis=0, keepdims=True)
        dw_ref[2:3, :] += jnp.sum(dy1 * u, axis=0, keepdims=True)

    return pl.pallas_call(
        body, name="conv_bwd", grid=(D // CONV_TC, nb),
        in_specs=[pl.BlockSpec(memory_space=pl.ANY),
                  pl.BlockSpec((4, seq, CONV_TC), lambda ci, b: (1, b, ci)),
                  pl.BlockSpec((seq, CONV_TC), lambda ci, b: (b, ci)),
                  pl.BlockSpec((8, CONV_TC), lambda ci, b: (0, ci))],
        out_specs=[pl.BlockSpec((4, seq, CONV_TC), lambda ci, b: (1, b, ci)),
                   pl.BlockSpec((8, CONV_TC), lambda ci, b: (0, ci))],
        out_shape=[jax.ShapeDtypeStruct(dproj.shape, BF16),
                   jax.ShapeDtypeStruct((8, D), F32)],
        input_output_aliases={0: 0},
        compiler_params=_params(("parallel", "arbitrary")),
    )(dproj, proj, dy, conv_w)


def _rope_tables(pos_ref, invf_ref, ma_ref, mb_ref, sign):
    ang = pos_ref[...].astype(F32) * invf_ref[...]
    cs = jnp.cos(ang)
    sn = jnp.sin(ang) * sign
    return cs, sn * ma_ref[...], sn * mb_ref[...]


def _rotate(v, cs, sa, sb):
    return v * cs + pltpu.roll(v, 128 - HALF, 1) * sa + pltpu.roll(v, HALF, 1) * sb


MLA_TM = 512


def mla_prep_fwd(proj, pos, g_q, g_kv, wuq, wukv, tabs):
    t = proj.shape[1]
    tm = min(MLA_TM, t)

    def body(lat_ref, pos_ref, gq_ref, gkv_ref, wuq_ref, wukv_ref, invf_ref, ma_ref, mb_ref,
             q_ref, k_ref, kv_ref, qn_ref, kvn_ref):
        lat = lat_ref[...].astype(F32)
        ql = lat[:, :QL]
        kl = lat[:, QL:QL + KVL]
        kr = lat[:, QL + KVL:QL + KVL + 128]
        qn = (ql * lax.rsqrt(jnp.mean(ql * ql, axis=-1, keepdims=True) + EPS) * gq_ref[...]).astype(BF16)
        kvn = (kl * lax.rsqrt(jnp.mean(kl * kl, axis=-1, keepdims=True) + EPS) * gkv_ref[...]).astype(BF16)
        qn_ref[...] = qn
        kvn_ref[...] = kvn
        cs, sa, sb = _rope_tables(pos_ref, invf_ref, ma_ref, mb_ref, 1.0)
        q = _dot_nt(qn, wuq_ref[...]) * (SM_SCALE * LOG2E)
        kv = _dot_nt(kvn, wukv_ref[...]).astype(BF16)
        kv_ref[...] = kv
        kpe = _rotate(kr, cs, sa, sb).astype(BF16)
        for hh in range(H):
            lo, mid, hi = hh * DQK, hh * DQK + 128, (hh + 1) * DQK
            q_ref[:, lo:mid] = q[:, lo:mid].astype(BF16)
            q_ref[:, mid:hi] = _rotate(q[:, mid:hi], cs, sa, sb).astype(BF16)
            k_ref[:, lo:mid] = kv[:, lo:mid]
            k_ref[:, mid:hi] = kpe

    row = lambda w: pl.BlockSpec((tm, w), lambda i: (i, 0))
    const = lambda a: pl.BlockSpec(a.shape, lambda i: (0,) * a.ndim)
    return pl.pallas_call(
        body, name="mla_prep_fwd", grid=(t // tm,),
        in_specs=[pl.BlockSpec((None, tm, D), lambda i: (SEG_LAT, i, 0)), row(1),
                  const(g_q), const(g_kv), const(wuq), const(wukv)] + [const(a) for a in tabs],
        out_specs=[row(H * DQK), row(H * DQK), row(H * DQK), row(QL), row(KVL)],
        out_shape=[jax.ShapeDtypeStruct((t, H * DQK), BF16)] * 3
        + [jax.ShapeDtypeStruct((t, QL), BF16), jax.ShapeDtypeStruct((t, KVL), BF16)],
        compiler_params=_params(("parallel",)),
    )(proj, pos, g_q, g_kv, wuq, wukv, *tabs)


def mla_prep_bwd(dproj, proj, dq_rot, dk, dv, pos, g_q, g_kv, wuq, wukv, tabs):
    t = proj.shape[1]
    tm = min(MLA_TM, t)

    def body(dp_in_ref, lat_ref, dqr_ref, dk_ref, dv_ref, pos_ref, gq_ref, gkv_ref, wuq_ref, wukv_ref,
             invf_ref, ma_ref, mb_ref, dp_ref, dq_ref, dkv_ref, dgq_ref, dgkv_ref):
        i = pl.program_id(0)
        lat = lat_ref[...].astype(F32)
        ql = lat[:, :QL]
        kl = lat[:, QL:QL + KVL]
        rq = lax.rsqrt(jnp.mean(ql * ql, axis=-1, keepdims=True) + EPS)
        rk = lax.rsqrt(jnp.mean(kl * kl, axis=-1, keepdims=True) + EPS)
        nq = ql * rq
        nk = kl * rk
        cs, sa, sb = _rope_tables(pos_ref, invf_ref, ma_ref, mb_ref, -1.0)
        dkpe = jnp.zeros((tm, 128), F32)
        for hh in range(H):
            lo, mid, hi = hh * DQK, hh * DQK + 128, (hh + 1) * DQK
            dq_ref[:, lo:mid] = (dqr_ref[:, lo:mid] * SM_SCALE).astype(BF16)
            dq_ref[:, mid:hi] = _rotate(dqr_ref[:, mid:hi] * SM_SCALE, cs, sa, sb).astype(BF16)
            dkv_ref[:, lo:mid] = dk_ref[:, lo:mid]
            dkv_ref[:, mid:hi] = dv_ref[:, hh * DV:(hh + 1) * DV]
            dkpe = dkpe + dk_ref[:, mid:hi].astype(F32)
        lane = lax.broadcasted_iota(jnp.int32, (tm, 128), 1)
        dkr = jnp.where(lane < ROPE, _rotate(dkpe, cs, sa, sb), 0.0)
        dqn = _dot(dq_ref[...], wuq_ref[...])
        dkvn = _dot(dkv_ref[...], wukv_ref[...])
        gq = gq_ref[...]
        gkv = gkv_ref[...]
        dnq = dqn * gq
        dnk = dkvn * gkv
        dql = rq * (dnq - nq * jnp.mean(dnq * nq, axis=-1, keepdims=True))
        dkl = rk * (dnk - nk * jnp.mean(dnk * nk, axis=-1, keepdims=True))
        dp_ref[:, :QL] = dql.astype(BF16)
        dp_ref[:, QL:QL + KVL] = dkl.astype(BF16)
        dp_ref[:, QL + KVL:QL + KVL + 128] = dkr.astype(BF16)
        dp_ref[:, QL + KVL + 128:] = jnp.zeros((tm, D - QL - KVL - 128), BF16)

        @pl.when(i == 0)
        def _():
            dgq_ref[...] = jnp.zeros_like(dgq_ref)
            dgkv_ref[...] = jnp.zeros_like(dgkv_ref)

        dgq_ref[...] += jnp.sum(dqn * nq, axis=0, keepdims=True)
        dgkv_ref[...] += jnp.sum(dkvn * nk, axis=0, keepdims=True)

    row = lambda w: pl.BlockSpec((tm, w), lambda i: (i, 0))
    const = lambda a: pl.BlockSpec(a.shape, lambda i: (0,) * a.ndim)
    seg = pl.BlockSpec((None, tm, D), lambda i: (SEG_LAT, i, 0))
    return pl.pallas_call(
        body, name="mla_prep_bwd", grid=(t // tm,),
        in_specs=[pl.BlockSpec(memory_space=pl.ANY), seg, row(H * DQK), row(H * DQK), row(H * DV), row(1),
                  const(g_q), const(g_kv), const(wuq), const(wukv)] + [const(a) for a in tabs],
        out_specs=[seg, row(H * DQK), row(H * DQK),
                   pl.BlockSpec((1, QL), lambda i: (0, 0)), pl.BlockSpec((1, KVL), lambda i: (0, 0))],
        out_shape=[jax.ShapeDtypeStruct(dproj.shape, BF16),
                   jax.ShapeDtypeStruct((t, H * DQK), BF16), jax.ShapeDtypeStruct((t, H * DQK), BF16),
                   jax.ShapeDtypeStruct((1, QL), F32), jax.ShapeDtypeStruct((1, KVL), F32)],
        input_output_aliases={0: 0},
        compiler_params=_params(("arbitrary",)),
    )(dproj, proj, dq_rot, dk, dv, pos, g_q, g_kv, wuq, wukv, *tabs)


def _causal_mask(s, shift):
    row = lax.broadcasted_iota(jnp.int32, s.shape, 0)
    col = lax.broadcasted_iota(jnp.int32, s.shape, 1)
    return jnp.where(col <= row + shift, s, -1e30)


def flash_fwd(q, k, kv, nb, seq):
    t = q.shape[0]
    tq = min(FLASH_TQ, seq)
    nq = seq // tq

    def body(q_ref, k_ref, v_ref, o_ref, lse_ref):
        for qi in range(nq):
            qs = slice(qi * tq, (qi + 1) * tq)
            qv = q_ref[qs, :]
            m = jnp.full((tq, 1), -1e30, F32)
            l = jnp.zeros((tq, 1), F32)
            acc = jnp.zeros((tq, DV), F32)
            for j in range(qi + 1):
                ks = slice(j * tq, (j + 1) * tq)
                s = _dot_nt(qv, k_ref[ks, :])
                if j == qi:
                    s = _causal_mask(s, 0)
                m_new = jnp.maximum(m, jnp.max(s, axis=1, keepdims=True))
                p = jnp.exp2(s - m_new)
                alpha = jnp.exp2(m - m_new)
                l = alpha * l + jnp.sum(p, axis=1, keepdims=True)
                acc = alpha * acc + _dot(p.astype(BF16), v_ref[ks, :])
                m = m_new
            o_ref[qs, :] = (acc / l).astype(BF16)
            lse_ref[qs, :] = jnp.broadcast_to(m + jnp.log(l) * LOG2E, (tq, DV))

    out_blk = pl.BlockSpec((seq, DV), lambda b, h: (b, h))
    return pl.pallas_call(
        body, name="flash_fwd", grid=(nb, H),
        in_specs=[pl.BlockSpec((seq, DQK), lambda b, h: (b, h)),
                  pl.BlockSpec((seq, DQK), lambda b, h: (b, h)),
                  pl.BlockSpec((seq, DV), lambda b, h: (b, 2 * h + 1))],
        out_specs=[out_blk, out_blk],
        out_shape=[jax.ShapeDtypeStruct((t, H * DV), BF16), jax.ShapeDtypeStruct((t, H * DV), F32)],
        compiler_params=_params(("parallel", "parallel")),
    )(q, k, kv)


def flash_bwd(q, k, kv, o, do, lse, nb, seq, token):
    t = q.shape[0]
    tq = min(FLASH_TQ, seq)
    nq = seq // tq

    def body(q_ref, k_ref, v_ref, o_ref, do_ref, lse_ref, tok_ref, dq_ref, dk_ref, dv_ref):
        delta, lse = [], []
        for qi in range(nq):
            qs = slice(qi * tq, (qi + 1) * tq)
            dl = jnp.sum(do_ref[qs, :].astype(F32) * o_ref[qs, :].astype(F32), axis=1, keepdims=True)
            delta.append(jnp.broadcast_to(dl, (tq, DV)).T[:1, :])
            lse.append(lse_ref[qs, :].T[:1, :])
        for ki in range(nq):
            ks = slice(ki * tq, (ki + 1) * tq)
            kb = k_ref[ks, :]
            vb = v_ref[ks, :]
            dk = jnp.zeros((tq, DQK), F32)
            dv = jnp.zeros((tq, DV), F32)
            for qi in range(ki, nq):
                qs = slice(qi * tq, (qi + 1) * tq)
                qv = q_ref[qs, :]
                dov = do_ref[qs, :]
                st = _dot_nt(kb, qv)
                if qi == ki:
                    row = lax.broadcasted_iota(jnp.int32, st.shape, 0)
                    col = lax.broadcasted_iota(jnp.int32, st.shape, 1)
                    st = jnp.where(row <= col, st, -1e30)
                pt = jnp.exp2(st - lse[qi])
                dpt = _dot_nt(vb, dov)
                dzt = (pt * (dpt - delta[qi])).astype(BF16)
                dv = dv + _dot(pt.astype(BF16), dov)
                dk = dk + _dot(dzt, qv)
                dqb = _dot_tn(dzt, kb)
                if ki == 0:
                    dq_ref[qs, :] = dqb
                else:
                    dq_ref[qs, :] += dqb
            dk_ref[ks, :] = (dk * LN2).astype(BF16)
            dv_ref[ks, :] = dv.astype(BF16)

    full = lambda w, col: pl.BlockSpec((seq, w), col)
    same = lambda b, h: (b, h)
    return pl.pallas_call(
        body, name="flash_bwd", grid=(nb, H),
        in_specs=[full(DQK, same), full(DQK, same), full(DV, lambda b, h: (b, 2 * h + 1)),
                  full(DV, same), full(DV, same), full(DV, same),
                  pl.BlockSpec((8, 128), lambda b, h: (0, 0))],
        out_specs=[full(DQK, same), full(DQK, same), full(DV, same)],
        out_shape=[jax.ShapeDtypeStruct((t, H * DQK), F32), jax.ShapeDtypeStruct((t, H * DQK), BF16),
                   jax.ShapeDtypeStruct((t, H * DV), BF16)],
        compiler_params=_params(("parallel", "parallel")),
    )(q, k, kv, o, do, lse, token)


TAIL_TM = 512


def tail_fwd(y, attn, proj, x2, tgt, gate, g_post, wco, wmo, wout, seq):
    t = y.shape[0]
    nb = t // seq
    tm = min(TAIL_TM, seq)
    tpb = seq // tm

    def body(y_ref, at_ref, p_ref, x_ref, t_ref, gate_ref, gp_ref, wco_ref, wmo_ref, wout_ref,
             o_ref, ya_ref, yb_ref, m_ref, do2_ref, dout_ref, dgate_ref, dgp_ref, loss_ref):
        i = pl.program_id(0)
        bz = p_ref[0].astype(F32)
        ga = p_ref[1].astype(F32)
        gb = p_ref[2].astype(F32)
        ov = (at_ref[...].astype(F32) * (bz * _sig(bz))).astype(BF16)
        o_ref[...] = ov
        ya = _dot(y_ref[...], wco_ref[...])
        yb = _dot(ov, wmo_ref[...])
        ya_ref[...] = ya.astype(BF16)
        yb_ref[...] = yb.astype(BF16)
        mv = (_sig(ga) * ya + _sig(gb) * yb).astype(BF16)
        m_ref[...] = mv
        o2 = _dot(mv, wout_ref[...])
        r = lax.rsqrt(jnp.mean(o2 * o2, axis=-1, keepdims=True) + EPS)
        nrm = o2 * r
        gp = gp_ref[...]
        gate_v = gate_ref[...]
        rn = nrm * gp
        err = x_ref[...] + gate_v * rn - t_ref[...]
        dout = err * (1.0 / D)
        dout_ref[...] = dout
        dn = dout * gate_v * gp
        do2_ref[...] = (r * (dn - nrm * jnp.mean(dn * nrm, axis=-1, keepdims=True))).astype(BF16)

        @pl.when(i % tpb == 0)
        def _():
            dgate_ref[...] = jnp.zeros_like(dgate_ref)

        @pl.when(i == 0)
        def _():
            dgp_ref[...] = jnp.zeros_like(dgp_ref)
            loss_ref[...] = jnp.zeros_like(loss_ref)

        dgate_ref[...] += jnp.sum(dout * rn, axis=0, keepdims=True)
        dgp_ref[...] += jnp.sum(dout * gate_v * nrm, axis=0, keepdims=True)
        loss_ref[...] += 0.5 * jnp.sum(jnp.mean(err * err, axis=-1, keepdims=True), axis=0, keepdims=True)

    row = pl.BlockSpec((tm, D), lambda i: (i, 0))
    per_batch = pl.BlockSpec((None, 1, D), lambda i: (i // tpb, 0, 0))
    vec = pl.BlockSpec((1, D), lambda i: (0, 0))
    wgt = pl.BlockSpec((D, D), lambda i: (0, 0))
    act = jax.ShapeDtypeStruct((t, D), BF16)
    return pl.pallas_call(
        body, name="tail_fwd", grid=(t // tm,),
        in_specs=[row, row, pl.BlockSpec((3, tm, D), lambda i: (0, i, 0)), row, row, per_batch, vec,
                  wgt, wgt, wgt],
        out_specs=[row, row, row, row, row, row, per_batch, vec, pl.BlockSpec((1, 1), lambda i: (0, 0))],
        out_shape=[act, act, act, act, act, jax.ShapeDtypeStruct((t, D), F32),
                   jax.ShapeDtypeStruct((nb, 1, D), F32), jax.ShapeDtypeStruct((1, D), F32),
                   jax.ShapeDtypeStruct((1, 1), F32)],
        compiler_params=_params(("arbitrary",)),
    )(y, attn, proj, x2, tgt, gate, g_post, wco, wmo, wout)


def tail_bwd(do2, proj, ya, yb, attn, wout, wmo, wco):
    t = do2.shape[0]
    tm = min(TAIL_TM, t)

    def body(do2_ref, p_ref, ya_ref, yb_ref, at_ref, wout_ref, wmo_ref, wco_ref,
             dp_ref, dya_ref, dyb_ref, dat_ref, dy_ref):
        bz = p_ref[0].astype(F32)
        ga = p_ref[1].astype(F32)
        gb = p_ref[2].astype(F32)
        dm = _dot_nt(do2_ref[...], wout_ref[...])
        sa = _sig(ga)
        sb = _sig(gb)
        dya = (dm * sa).astype(BF16)
        dyb = (dm * sb).astype(BF16)
        dya_ref[...] = dya
        dyb_ref[...] = dyb
        dp_ref[1] = (dm * ya_ref[...].astype(F32) * (sa * (1.0 - sa))).astype(BF16)
        dp_ref[2] = (dm * yb_ref[...].astype(F32) * (sb * (1.0 - sb))).astype(BF16)
        dov = _dot_nt(dyb, wmo_ref[...])
        sz = _sig(bz)
        dat_ref[...] = (dov * (bz * sz)).astype(BF16)
        dp_ref[0] = (dov * at_ref[...].astype(F32) * (sz * (1.0 + bz * (1.0 - sz)))).astype(BF16)
        dy_ref[...] = _dot_nt(dya, wco_ref[...]).astype(BF16)

    row = pl.BlockSpec((tm, D), lambda i: (i, 0))
    seg3 = pl.BlockSpec((3, tm, D), lambda i: (0, i, 0))
    wgt = pl.BlockSpec((D, D), lambda i: (0, 0))
    act = jax.ShapeDtypeStruct((t, D), BF16)
    return pl.pallas_call(
        body, name="tail_bwd", grid=(t // tm,),
        in_specs=[row, seg3, row, row, row, wgt, wgt, wgt],
        out_specs=[seg3, row, row, row, row],
        out_shape=[jax.ShapeDtypeStruct((NSEG, t, D), BF16), act, act, act, act],
        compiler_params=_params(("parallel",)),
    )(do2, proj, ya, yb, attn, wout, wmo, wco)


def adamw(w, m, v, g, g2, name, token=None):
    rows, cols = w.shape
    tr = rows
    for cand in (256, 128, 64, 32, 16, 8):
        if rows % cand == 0 and rows > cand:
            tr = cand
            break
    has2 = g2 is not None
    n_in = 4 + has2

    def body(*refs):
        w_ref, m_ref, v_ref, g_ref = refs[:4]
        go_ref, d_ref, mo_ref, vo_ref = refs[-4:]
        grad = g_ref[...] + refs[4][...].astype(F32) if has2 else g_ref[...]
        mn = ADAM_B1 * m_ref[...] + (1.0 - ADAM_B1) * grad
        vn = ADAM_B2 * v_ref[...] + (1.0 - ADAM_B2) * (grad * grad)
        m_hat = mn / (1.0 - ADAM_B1 ** ADAM_STEP)
        v_hat = vn / (1.0 - ADAM_B2 ** ADAM_STEP)
        go_ref[...] = grad
        d_ref[...] = -ADAM_LR * (m_hat / (jnp.sqrt(v_hat) + ADAM_EPS) + ADAM_WD * w_ref[...])
        mo_ref[...] = mn
        vo_ref[...] = vn

    blk = pl.BlockSpec((tr, cols), lambda i: (i, 0))
    ins = [w, m, v, g] + ([g2] if has2 else [])
    specs = [blk] * n_in
    if token is not None:
        ins.append(token)
        specs.append(pl.BlockSpec((8, 128), lambda i: (0, 0)))
    return pl.pallas_call(
        body, name=name, grid=(rows // tr,),
        in_specs=specs, out_specs=[blk] * 4,
        out_shape=[jax.ShapeDtypeStruct((rows, cols), F32)] * 4,
        compiler_params=_params(("parallel",)),
    )(*ins)


def adamw_scattered(w, m, v, own, land, me, tr, name, transpose=False):
    slot_rows = land.shape[1]
    cols = land.shape[2]
    rows = slot_rows if transpose else w.shape[0]
    per_slot = slot_rows // tr

    def body(me_ref, w_ref, m_ref, v_ref, own_ref, land_ref, go_ref, d_ref, mo_ref, vo_ref):
        grad = own_ref[...].astype(F32)
        for s in range(8):
            grad = grad + land_ref[s].astype(F32)
        if transpose:
            grad = grad.T
        mn = ADAM_B1 * m_ref[...] + (1.0 - ADAM_B1) * grad
        vn = ADAM_B2 * v_ref[...] + (1.0 - ADAM_B2) * (grad * grad)
        m_hat = mn / (1.0 - ADAM_B1 ** ADAM_STEP)
        v_hat = vn / (1.0 - ADAM_B2 ** ADAM_STEP)
        go_ref[...] = grad
        d_ref[...] = -ADAM_LR * (m_hat / (jnp.sqrt(v_hat) + ADAM_EPS) + ADAM_WD * w_ref[...])
        mo_ref[...] = mn
        vo_ref[...] = vn

    wblk = pl.BlockSpec(w.shape if transpose else (tr, w.shape[1]), lambda i, s: (i, 0))
    return pl.pallas_call(
        body, name=name,
        grid_spec=pltpu.PrefetchScalarGridSpec(
            num_scalar_prefetch=1, grid=(rows // tr,),
            in_specs=[wblk, wblk, wblk,
                      pl.BlockSpec((tr, cols), lambda i, s: (s[0] * per_slot + i, 0)),
                      pl.BlockSpec((8, tr, cols), lambda i, s: (0, i, 0))],
            out_specs=[wblk] * 4),
        out_shape=[jax.ShapeDtypeStruct(w.shape, F32)] * 4,
        compiler_params=_params(),
    )(me, w, m, v, own, land)


def adamw_win(wt, mt, vt, ka, ra, kb, rb):
    rows = wt.shape[0]
    tc = 256
    nh = (D // 2) // tc

    def body(w_ref, m_ref, v_ref, ka_ref, ra_ref, kb_ref, rb_ref, go_ref, d_ref, mo_ref, vo_ref):
        first = pl.program_id(0) < nh
        grad = jnp.where(first, ka_ref[...] + ra_ref[...].astype(F32), kb_ref[...] + rb_ref[...].astype(F32))
        mn = ADAM_B1 * m_ref[...] + (1.0 - ADAM_B1) * grad
        vn = ADAM_B2 * v_ref[...] + (1.0 - ADAM_B2) * (grad * grad)
        m_hat = mn / (1.0 - ADAM_B1 ** ADAM_STEP)
        v_hat = vn / (1.0 - ADAM_B2 ** ADAM_STEP)
        go_ref[...] = grad
        d_ref[...] = -ADAM_LR * (m_hat / (jnp.sqrt(v_hat) + ADAM_EPS) + ADAM_WD * w_ref[...])
        mo_ref[...] = mn
        vo_ref[...] = vn

    blk = pl.BlockSpec((rows, tc), lambda j: (0, j))
    lo = pl.BlockSpec((rows, tc), lambda j: (0, jnp.minimum(j, nh - 1)))
    hi = pl.BlockSpec((rows, tc), lambda j: (0, jnp.maximum(j - nh, 0)))
    return pl.pallas_call(
        body, name="adamw_w_in", grid=(D // tc,),
        in_specs=[blk, blk, blk, lo, lo, hi, hi], out_specs=[blk] * 4,
        out_shape=[jax.ShapeDtypeStruct((rows, D), F32)] * 4,
        compiler_params=_params(("parallel",)),
    )(wt, mt, vt, ka, ra, kb, rb)


_ORD_A = ("x", "y", "c")
_ORD_B = ("y", "x", "c")


def _rows128(a, rows):
    flat = a.reshape(-1)
    return jnp.pad(flat, (0, rows * 128 - flat.shape[0])).reshape(rows, 128)


def kernel(x, c, positions, w_ada, b_ada, g_pre, w_in, conv_w, w_conv_out, g_q, w_uq, g_kv, w_ukv, w_mla_out, w_out, g_post, loss_target, m_w_ada, m_b_ada, m_g_pre, m_w_in, m_conv_w, m_w_conv_out, m_g_q, m_w_uq, m_g_kv, m_w_ukv, m_w_mla_out, m_w_out, m_g_post, v_w_ada, v_b_ada, v_g_pre, v_w_in, v_conv_w, v_w_conv_out, v_g_q, v_w_uq, v_g_kv, v_w_ukv, v_w_mla_out, v_w_out, v_g_post):
    nb, seq, _ = x.shape
    t = nb * seq
    mx, my, mc = lax.axis_index("x"), lax.axis_index("y"), lax.axis_index("c")
    me = 4 * mx + 2 * my + mc
    co = {"x": mx, "y": my, "c": mc}

    x2 = x.reshape(t, D)
    tgt2 = loss_target.reshape(t, D)
    pos2 = positions.reshape(t, 1)

    packed = jnp.concatenate([c.reshape(2 * D // 128, 128), _rows128(conv_w[0], 8)], axis=0)
    gath = small_allgather(packed, "gather_cond")
    c_all = gath[:, :16].reshape(8 * nb, D)
    conv_full = gath[:, 16:19].reshape(8, 3, 128).transpose(1, 0, 2).reshape(3, D)
    conv_full8 = jnp.pad(conv_full, ((0, 5), (0, 0)))
    ada_cols = w_ada.shape[2]
    b_cols = lax.dynamic_slice(b_ada, (0, me * ada_cols), (1, ada_cols))
    mod_part = ada_fwd(c_all, w_ada[0], b_cols)
    mod_g = small_allgather(mod_part.reshape(8 * nb * ada_cols // 128, 128), "gather_mod")
    mod_all = mod_g.reshape(8, 8 * nb, ada_cols).transpose(1, 0, 2).reshape(8 * nb, 8 * ada_cols)
    mod = lax.dynamic_slice(mod_all, (me * nb, 0), (nb, 3 * D))
    shift = mod[:, 0:D].reshape(nb, 1, D)
    scale = mod[:, D:2 * D].reshape(nb, 1, D)
    gate = mod[:, 2 * D:3 * D].reshape(nb, 1, D)

    wt = w_in[0].T.astype(BF16)
    lo = lax.bitcast_convert_type(wt[:, :D // 2], jnp.uint16).astype(jnp.uint32)
    hi = lax.bitcast_convert_type(wt[:, D // 2:], jnp.uint16).astype(jnp.uint32)
    wt_bits = lax.bitcast_convert_type(lo | (hi << 16), F32)
    wt_bits, mod = lax.optimization_barrier((wt_bits, mod))
    shift = mod[:, 0:D].reshape(nb, 1, D)
    scale = mod[:, D:2 * D].reshape(nb, 1, D)
    gate = mod[:, 2 * D:3 * D].reshape(nb, 1, D)
    q4 = D // 4
    r3rd = wt_bits.shape[0] // 3
    plan = [(0, (k * r3rd, r3rd), (g * q4, q4), (_ORD_A, _ORD_B)[g]) for k in range(3) for g in range(2)]
    gw = allgather_big([wt_bits], plan, "gather_w_in")
    late = [w_conv_out[0].astype(BF16), w_mla_out[0].astype(BF16), w_out[0].astype(BF16),
            jnp.pad(w_uq[0].T.astype(BF16), ((0, DQK - 192), (0, 0))), w_ukv[0].T.astype(BF16)]
    gw0, late = lax.optimization_barrier((gw[0], late))
    late_state, late_token = gather_start(late, "gather_late_start")
    wt_bits_all = gw0.reshape(N_IN, D // 2)

    inv_freq = ROPE_THETA ** (-jnp.arange(0, ROPE, 2, dtype=F32) / ROPE)
    invf = jnp.concatenate([inv_freq, inv_freq, jnp.zeros((128 - ROPE,), F32)]).reshape(1, 128)
    lane = np.arange(128)
    tabs = (invf,
            jnp.asarray(np.where(lane < HALF, -1.0, 0.0).reshape(1, 128), F32),
            jnp.asarray(np.where((lane >= HALF) & (lane < ROPE), 1.0, 0.0).reshape(1, 128), F32))

    h = prenorm_fwd(x2, scale, shift, g_pre, seq)
    proj, wt_p = proj_matmul(h, wt_bits_all, late_token)
    y = conv_fwd(proj, conv_full8, seq)
    gl = gather_wait(late_state, y, "gather_late_wait")
    wco = gl[0].reshape(D, D)
    wmo = gl[1].reshape(D, D)
    wout = gl[2].reshape(D, D)
    wuq_p = gl[3].reshape(H * DQK, QL)
    wukv = gl[4].reshape(H * 256, KVL)
    q_rot, k_cat, kv, qn, kvn = mla_prep_fwd(proj, pos2, g_q, g_kv, wuq_p, wukv, tabs)
    attn, lse = flash_fwd(q_rot, k_cat, kv, nb, seq)
    o, ya, yb, m, do2, dout, dgate, dg_post, loss_part = tail_fwd(
        y, attn, proj, x2, tgt2, gate, g_post, wco, wmo, wout, seq)

    dproj, dya, dyb, dattn, dy = tail_bwd(do2, proj, ya, yb, attn, wout, wmo, wco)
    g_wout = grad_matmul(m, do2, "grad_w_square")
    g_wmo = grad_matmul(o, dyb, "grad_w_square")
    g_wco = grad_matmul(y, dya, "grad_w_square")
    sc1, sc1_tok = scatter_start([g_wco, g_wmo, g_wout], "scatter_out_grads_start")
    dproj, dconv = conv_bwd(dproj, proj, dy, conv_full8, seq)
    dq_rot, dk, dv = flash_bwd(q_rot, k_cat, kv, attn, dattn, lse, nb, seq, sc1_tok)
    dproj, dq, dkv, dg_q, dg_kv = mla_prep_bwd(dproj, proj, dq_rot, dk, dv, pos2, g_q, g_kv, wuq_p, wukv, tabs)
    g_wuq_t = grad_matmul(dq, qn, "grad_w_uq")
    g_wukv_t = grad_matmul(dkv, kvn, "grad_w_ukv")
    sc2, sc2_tok = scatter_start([g_wuq_t, g_wukv_t], "scatter_mla_grads_start")
    g_win_p = win_grad_matmul(h, dproj, sc2_tok)

    g_wt = g_win_p.reshape(2, 2, 2, N_IN // 8, D)
    ords = [("c", "y", "x"), ("c", "x", "y")]
    hc = D // 2
    win_shape = (2, 2, N_IN // 8, hc)
    pick_w = lambda col: (lambda ref, cc: ref.at[:, :, 1 - cc["c"], :, pl.ds(col * hc, hc)])
    which1 = [0, 0]
    picks1 = [pick_w(0), pick_w(1)]
    st1, tok1 = swap_start([g_wt], which1, ["c"] * 2, picks1, [win_shape] * 2, "rs_c_start")
    assert nb == 2
    dh0 = dh_matmul(dproj, wt_p, tok1, seq, 0)
    (g_wt,), r1 = swap_wait(st1, dh0, which1, ["c"] * 2, picks1, "rs_c_wait")
    sel_xyc = jnp.stack([mx, my, mc]).astype(jnp.int32)
    sel2 = [jnp.stack([co[o[2]]]).astype(jnp.int32) for o in ords]
    first = [rs_win_add_first(g_wt, r1[0], sel_xyc, 1, 0, "rs_add_first_0"),
             rs_win_add_first(g_wt, r1[1], sel_xyc, 0, 1, "rs_add_first_1")]
    keep1, send1 = zip(*first)
    all4 = [0, 1]
    none4 = [None] * 2
    axes2 = [o[1] for o in ords]
    st2, tok2 = swap_start(list(send1), all4, axes2, none4, [s.shape for s in send1], "rs_ici1_start")

    dh1 = dh_matmul(dproj, wt_p, tok2, seq, 1)
    gx0, dsh0, dsc0, dgp0 = prenorm_bwd(dh0, x2, dout, scale, g_pre, seq, tok2, 0, None)
    _, r2 = swap_wait(st2, (gx0, dh1), all4, axes2, none4, "rs_ici1_wait")
    keep2, send2 = zip(*[rs_add_second(keep1[a], r2[a], sel2[a], "rs_add_second") for a in range(2)])
    axes3 = [o[2] for o in ords]
    st3, tok3 = swap_start(list(send2), all4, axes3, none4, [s.shape for s in send2], "rs_ici2_start")
    grad_x2, dsh1, dsc1, dgp1 = prenorm_bwd(dh1, x2, dout, scale, g_pre, seq, tok3, 1, gx0)
    dshift = jnp.stack([dsh0, dsh1])
    dscale = jnp.stack([dsc0, dsc1])
    dg_pre = dgp0 + dgp1

    dmod = jnp.concatenate([dshift, dscale, dgate], axis=2).reshape(nb * 3 * D // 128, 128)
    small = jnp.concatenate([
        dmod, _rows128(dg_pre, 8), _rows128(dg_post, 8), _rows128(dg_q, 8), _rows128(dg_kv, 8),
        dconv[0:3].reshape(24, 128), _rows128(loss_part, 8)], axis=0)
    small_g = small_allgather(small, "gather_small_grads")
    sums = slot_sum(small_g)
    dmod_all = small_g[:, 0:48].reshape(8 * nb, 3 * D)
    g_bada = (sums[0:24] + sums[24:48]).reshape(1, 3 * D)
    g_gpre = sums[48:56].reshape(1, D)
    g_gpost = sums[56:64].reshape(1, D)
    g_gq = sums[64:67].reshape(1, QL)
    g_gkv = sums[72:74].reshape(1, KVL)
    g_conv_full = sums[80:104].reshape(3, D)
    loss = sums[104, 0]
    g_conv = lax.dynamic_slice(g_conv_full, (0, me * 128), (3, 128))
    dmod_cols = lax.dynamic_slice(dmod_all, (0, me * ada_cols), (8 * nb, ada_cols))
    g_wada = ada_bwd(c_all, dmod_cols)

    res = {}
    res["w_ada"] = [o_[None] for o_ in adamw(w_ada[0], m_w_ada[0], v_w_ada[0], g_wada, None, "adamw_w_ada", tok3)]

    def pack(b_, gp_, gpo_, gq_, gkv_, cw_):
        return jnp.concatenate([_rows128(b_, 24), _rows128(gp_, 8), _rows128(gpo_, 8), _rows128(gq_, 8),
                                _rows128(gkv_, 8), _rows128(cw_, 8)], axis=0)

    sw = pack(b_ada, g_pre, g_post, g_q, g_kv, conv_w)
    sm = pack(m_b_ada, m_g_pre, m_g_post, m_g_q, m_g_kv, m_conv_w)
    sv = pack(v_b_ada, v_g_pre, v_g_post, v_g_q, v_g_kv, v_conv_w)
    sg = pack(g_bada, g_gpre, g_gpost, g_gq, g_gkv, g_conv)
    small_out = adamw(sw, sm, sv, sg, None, "adamw_small", tok3)

    _, r3 = swap_wait(st3, small_out[0], all4, axes3, none4, "rs_ici2_wait")

    (g_wco, g_wmo, g_wout), (l_wco, l_wmo, l_wout) = scatter_wait(sc1, small_out[1], "scatter_out_grads_wait")
    (g_wuq_t, g_wukv_t), (l_wuq, l_wukv) = scatter_wait(sc2, small_out[2], "scatter_mla_grads_wait")

    res["w_in"] = [o_.T[None] for o_ in adamw_win(w_in[0].T, m_w_in[0].T, v_w_in[0].T,
                                                  keep2[0], r3[0], keep2[1], r3[1])]
    me1 = me.reshape(1).astype(jnp.int32)
    res["w_uq"] = [o_.T[None] for o_ in adamw_scattered(
        w_uq[0].T, m_w_uq[0].T, v_w_uq[0].T, g_wuq_t, l_wuq, me1, 64, "adamw_w_uq")]
    res["w_ukv"] = [o_[None] for o_ in adamw_scattered(
        w_ukv[0], m_w_ukv[0], v_w_ukv[0], g_wukv_t, l_wukv, me1, KVL, "adamw_w_ukv", transpose=True)]
    for nm, wv, mv, vv, gg, ll in (("w_conv_out", w_conv_out, m_w_conv_out, v_w_conv_out, g_wco, l_wco),
                                   ("w_mla_out", w_mla_out, m_w_mla_out, v_w_mla_out, g_wmo, l_wmo),
                                   ("w_out", w_out, m_w_out, v_w_out, g_wout, l_wout)):
        res[nm] = [o_[None] for o_ in adamw_scattered(wv[0], mv[0], vv[0], gg, ll, me1, 128, "adamw_square")]

    def unpack(a):
        return {"b_ada": a[0:24].reshape(1, 3 * D), "g_pre": a[24:32].reshape(1, D),
                "g_post": a[32:40].reshape(1, D), "g_q": a[40:43].reshape(1, QL),
                "g_kv": a[48:50].reshape(1, KVL), "conv_w": a[56:59].reshape(-1)[:3 * 128].reshape(1, 3, 128)}

    for nm in ("b_ada", "g_pre", "g_post", "g_q", "g_kv", "conv_w"):
        res[nm] = [unpack(a)[nm] for a in small_out]

    order = ["w_ada", "b_ada", "g_pre", "w_in", "conv_w", "w_conv_out", "g_q", "w_uq", "g_kv", "w_ukv",
             "w_mla_out", "w_out", "g_post"]
    out = [loss, grad_x2.reshape(nb, seq, D)]
    for k_ in range(4):
        out += [res[nm][k_] for nm in order]
    return tuple(out)
```

```python
import functools

import numpy as np
import jax
import jax.numpy as jnp
from jax import lax
from jax.experimental import pallas as pl
from jax.experimental.pallas import tpu as pltpu

F32 = jnp.float32
BF16 = jnp.bfloat16
MESH = pl.DeviceIdType.MESH

D = 1024
H = 8
QL = 384
KVL = 256
ROPE = 64
HALF = ROPE // 2
DQK = 256
DV = 128
NSEG = 8
NP = NSEG * D
EPS = 1e-6
ROPE_THETA = 10000.0
SM_SCALE = (128 + ROPE) ** -0.5
LOG2E = 1.4426950408889634
LN2 = 0.6931471805599453
FLASH_TQ = 512

SEG_BZ, SEG_GA, SEG_GB, SEG_LAT, SEG_V = 0, 1, 2, 3, 4

ADAM_LR = 0.001
ADAM_B1 = 0.9
ADAM_B2 = 0.999
ADAM_EPS = 1e-08
ADAM_WD = 0.01
ADAM_STEP = 10

VMEM_LIMIT = 56 * 1024 * 1024


def _params(sem=None, vmem=VMEM_LIMIT):
    kw = dict(vmem_limit_bytes=vmem)
    if sem is not None:
        kw["dimension_semantics"] = sem
    return pltpu.CompilerParams(**kw)


def _sig(v):
    return 0.5 * jnp.tanh(0.5 * v) + 0.5


def _dot(a, b):
    return jnp.dot(a, b, preferred_element_type=F32)


def _dot_nt(a, b):
    return lax.dot_general(a, b, (((1,), (1,)), ((), ())), preferred_element_type=F32)


def _dot_tn(a, b):
    return lax.dot_general(a, b, (((0,), (0,)), ((), ())), preferred_element_type=F32)


_AXIS_POS = {"x": 0, "y": 1, "c": 2}


def _coords():
    return lax.axis_index("x"), lax.axis_index("y"), lax.axis_index("c")


def _partner(axis):
    p = list(_coords())
    p[_AXIS_POS[axis]] = 1 - p[_AXIS_POS[axis]]
    return tuple(p)


def small_allgather(v, name):
    rows = v.shape[0]

    def body(v_ref, out_ref, send_sems, recv_sems):
        x, y, c = _coords()
        me = 4 * x + 2 * y + c
        out_ref[me] = v_ref[...]
        copies = []
        for k in range(1, 8):
            peer = (1 - x if k & 4 else x, 1 - y if k & 2 else y, 1 - c if k & 1 else c)
            cp = pltpu.make_async_remote_copy(
                src_ref=v_ref, dst_ref=out_ref.at[me],
                send_sem=send_sems.at[k - 1], recv_sem=recv_sems.at[k - 1],
                device_id=peer, device_id_type=MESH)
            cp.start()
            copies.append(cp)
        for cp in copies:
            cp.wait()

    return pl.pallas_call(
        body, name=name,
        out_shape=jax.ShapeDtypeStruct((8, rows, 128), F32),
        in_specs=[pl.BlockSpec(memory_space=pltpu.VMEM)],
        out_specs=pl.BlockSpec(memory_space=pltpu.VMEM),
        scratch_shapes=[pltpu.SemaphoreType.DMA((7,)), pltpu.SemaphoreType.DMA((7,))],
    )(v)


def _own_block_placed(s):
    x, y, c = _coords()
    return lax.dynamic_update_slice(lax.empty((2, 2, 2) + s.shape, s.dtype), s[None, None, None],
                                    (x, y, c) + (0,) * s.ndim)


def allgather_big(arrs, plan, name):
    n = len(arrs)
    m = len(plan)
    nst = len(plan[0][3])

    def body(*refs):
        ins, outs = refs[n:2 * n], refs[2 * n:3 * n]
        send_sems, recv_sems = refs[3 * n:]
        x, y, c = _coords()
        co = {"x": x, "y": y, "c": c}

        def window(ref, lead, rows, cols):
            win = tuple(slice(None) if w is None else pl.ds(w[0], w[1]) for w in (rows, cols))
            return ref.at[tuple(lead) + win]

        def held(e, free):
            i, rows, cols, _ = plan[e]
            lead = [slice(None) if ax in free else co[ax] for ax in ("x", "y", "c")]
            return window(outs[i], lead, rows, cols)

        def rcopy(e, stage, src, dst, axis):
            return pltpu.make_async_remote_copy(
                src_ref=src, dst_ref=dst,
                send_sem=send_sems.at[e, stage], recv_sem=recv_sems.at[e, stage],
                device_id=_partner(axis), device_id_type=MESH)

        stages = [[] for _ in range(nst)]
        for e, (i, rows, cols, order) in enumerate(plan):
            cp = rcopy(e, 0, window(ins[i], [], rows, cols), held(e, ()), order[0])
            cp.start()
            stages[0].append(cp)
        for s in range(1, nst):
            for e, (i, rows, cols, order) in enumerate(plan):
                stages[s - 1][e].wait_recv()
                blk = held(e, order[:s])
                cp = rcopy(e, s, blk, blk, order[s])
                cp.start()
                stages[s].append(cp)
        for e in range(m):
            stages[nst - 1][e].wait_recv()
        for e in range(m):
            for s in range(nst):
                stages[s][e].wait_send()

    any_spec = pl.BlockSpec(memory_space=pl.ANY)
    lands = [_own_block_placed(a) for a in arrs]
    return pl.pallas_call(
        body, name=name,
        out_shape=[jax.ShapeDtypeStruct(l.shape, l.dtype) for l in lands],
        in_specs=[any_spec] * (2 * n),
        out_specs=[any_spec] * n,
        input_output_aliases={i: i for i in range(n)},
        scratch_shapes=[pltpu.SemaphoreType.DMA((m, nst)), pltpu.SemaphoreType.DMA((m, nst))],
    )(*lands, *arrs)


def plane_swap_start(arr, name):
    def body(a_ref, send_sem, recv_sem, a_thru, token):
        c = lax.axis_index("c")
        mine = a_ref.at[:, :, c]
        pltpu.make_async_remote_copy(src_ref=mine, dst_ref=mine, send_sem=send_sem.at[0],
                                     recv_sem=recv_sem.at[0], device_id=_partner("c"),
                                     device_id_type=MESH).start()
        token[...] = jnp.zeros_like(token)

    out = pl.pallas_call(
        body, name=name,
        out_shape=[pltpu.SemaphoreType.DMA((1,)), pltpu.SemaphoreType.DMA((1,)),
                   pltpu.HBM(arr.shape, arr.dtype), jax.ShapeDtypeStruct((8, 128), F32)],
        in_specs=[_HBM],
        out_specs=[_SEM, _SEM, _HBM, pl.BlockSpec(memory_space=pltpu.VMEM)],
        input_output_aliases={0: 2},
        compiler_params=pltpu.CompilerParams(has_side_effects=pltpu.SideEffectType.DATAFLOW_SIDE_EFFECTING),
    )(pltpu.with_memory_space_constraint(arr, pltpu.HBM))
    return out[:3], out[3]


def plane_swap_wait(state, after, name):
    def body(a_ref, send_sem, recv_sem, after_ref, a_out):
        c = lax.axis_index("c")
        cp = pltpu.make_async_remote_copy(src_ref=a_ref.at[:, :, c], dst_ref=a_ref.at[:, :, 1 - c],
                                          send_sem=send_sem.at[0], recv_sem=recv_sem.at[0],
                                          device_id=_partner("c"), device_id_type=MESH)
        cp.wait_send()
        cp.wait_recv()

    arr = state[2]
    return pl.pallas_call(
        body, name=name,
        out_shape=pltpu.HBM(arr.shape, arr.dtype),
        in_specs=[_HBM, _SEM, _SEM, pl.BlockSpec(memory_space=pl.ANY)],
        out_specs=_HBM,
        input_output_aliases={0: 0},
        compiler_params=pltpu.CompilerParams(has_side_effects=pltpu.SideEffectType.DATAFLOW_SIDE_EFFECTING),
    )(arr, state[0], state[1], after)


def exchange(arrs, axes, picks, out_shapes, name):
    n = len(arrs)

    def body(*refs):
        ins, outs = refs[:n], refs[n:2 * n]
        send_sems, recv_sems = refs[2 * n:]
        x, y, c = _coords()
        co = {"x": x, "y": y, "c": c}
        copies = []
        for a in range(n):
            src = ins[a] if picks[a] is None else picks[a](ins[a], co)
            cp = pltpu.make_async_remote_copy(
                src_ref=src, dst_ref=outs[a],
                send_sem=send_sems.at[a], recv_sem=recv_sems.at[a],
                device_id=_partner(axes[a]), device_id_type=MESH)
            cp.start()
            copies.append(cp)
        for cp in copies:
            cp.wait()

    any_spec = pl.BlockSpec(memory_space=pl.ANY)
    return pl.pallas_call(
        body, name=name,
        out_shape=[jax.ShapeDtypeStruct(s, a.dtype) for s, a in zip(out_shapes, arrs)],
        in_specs=[any_spec] * n,
        out_specs=[any_spec] * n,
        scratch_shapes=[pltpu.SemaphoreType.DMA((n,)), pltpu.SemaphoreType.DMA((n,))],
    )(*arrs)


_HBM = pl.BlockSpec(memory_space=pltpu.HBM)
_SEM = pl.BlockSpec(memory_space=pltpu.SEMAPHORE)


def _swap_copies(srcs, lands, send_sems, recv_sems, axes, picks):
    x, y, c = _coords()
    co = {"x": x, "y": y, "c": c}
    return [pltpu.make_async_remote_copy(
        src_ref=srcs[a] if picks[a] is None else picks[a](srcs[a], co), dst_ref=lands[a],
        send_sem=send_sems.at[a], recv_sem=recv_sems.at[a],
        device_id=_partner(axes[a]), device_id_type=MESH) for a in range(len(srcs))]


def swap_start(arrs, which, axes, picks, out_shapes, name):
    ns, n = len(arrs), len(which)

    def body(*refs):
        srcs, lands = refs[:ns], refs[ns:ns + n]
        send_sems, recv_sems = refs[ns + n:ns + n + 2]
        token = refs[-1]
        for cp in _swap_copies([srcs[i] for i in which], lands, send_sems, recv_sems, axes, picks):
            cp.start()
        token[...] = jnp.zeros_like(token)

    lands = [lax.empty(s, arrs[i].dtype) for s, i in zip(out_shapes, which)]
    ops = [pltpu.with_memory_space_constraint(a, pltpu.HBM) for a in list(arrs) + lands]
    out = pl.pallas_call(
        body, name=name,
        out_shape=[pltpu.SemaphoreType.DMA((n,)), pltpu.SemaphoreType.DMA((n,))]
        + [pltpu.HBM(o.shape, o.dtype) for o in ops] + [jax.ShapeDtypeStruct((8, 128), F32)],
        in_specs=[_HBM] * (ns + n),
        out_specs=[_SEM, _SEM] + [_HBM] * (ns + n) + [pl.BlockSpec(memory_space=pltpu.VMEM)],
        input_output_aliases={i: 2 + i for i in range(ns + n)},
        compiler_params=pltpu.CompilerParams(has_side_effects=pltpu.SideEffectType.DATAFLOW_SIDE_EFFECTING),
    )(*ops)
    return out[:-1], out[-1]


def swap_wait(state, after, which, axes, picks, name):
    n = len(which)
    ns = len(state) - 2 - n

    def body(*refs):
        srcs, lands = refs[:ns], refs[ns:ns + n]
        send_sems, recv_sems = refs[ns + n:ns + n + 2]
        for cp in _swap_copies([srcs[i] for i in which], lands, send_sems, recv_sems, axes, picks):
            cp.wait_send()
            cp.wait_recv()

    thru = list(state[2:])
    after = list(after) if isinstance(after, (list, tuple)) else [after]
    out = pl.pallas_call(
        body, name=name,
        out_shape=[pltpu.HBM(o.shape, o.dtype) for o in thru],
        in_specs=[_HBM] * (ns + n) + [_SEM, _SEM] + [pl.BlockSpec(memory_space=pl.ANY)] * len(after),
        out_specs=[_HBM] * (ns + n),
        input_output_aliases={i: i for i in range(ns + n)},
        compiler_params=pltpu.CompilerParams(has_side_effects=pltpu.SideEffectType.DATAFLOW_SIDE_EFFECTING),
    )(*thru, state[0], state[1], *after)
    return out[:ns], out[ns:]


def _gather_copies(shards, lands, send_sems, recv_sems):
    x, y, c = _coords()
    copies = []
    for a in range(len(shards)):
        for k in range(1, 8):
            peer = (1 - x if k & 4 else x, 1 - y if k & 2 else y, 1 - c if k & 1 else c)
            copies.append(pltpu.make_async_remote_copy(
                src_ref=shards[a], dst_ref=lands[a].at[x, y, c],
                send_sem=send_sems.at[7 * a + k - 1], recv_sem=recv_sems.at[7 * a + k - 1],
                device_id=peer, device_id_type=MESH))
    return copies


def gather_start(shards, name):
    n = len(shards)
    x, y, c = _coords()

    def body(*refs):
        srcs, lands = refs[:n], refs[n:2 * n]
        send_sems, recv_sems = refs[2 * n:2 * n + 2]
        token = refs[-1]
        for cp in _gather_copies(srcs, lands, send_sems, recv_sems):
            cp.start()
        token[...] = jnp.zeros_like(token)

    lands = [_own_block_placed(s) for s in shards]
    ops = [pltpu.with_memory_space_constraint(a, pltpu.HBM) for a in list(shards) + lands]
    out = pl.pallas_call(
        body, name=name,
        out_shape=[pltpu.SemaphoreType.DMA((7 * n,)), pltpu.SemaphoreType.DMA((7 * n,))]
        + [pltpu.HBM(o.shape, o.dtype) for o in ops] + [jax.ShapeDtypeStruct((8, 128), F32)],
        in_specs=[_HBM] * (2 * n),
        out_specs=[_SEM, _SEM] + [_HBM] * (2 * n) + [pl.BlockSpec(memory_space=pltpu.VMEM)],
        input_output_aliases={i: 2 + i for i in range(2 * n)},
        compiler_params=pltpu.CompilerParams(has_side_effects=pltpu.SideEffectType.DATAFLOW_SIDE_EFFECTING),
    )(*ops)
    return out[:-1], out[-1]


def gather_wait(state, after, name):
    n = (len(state) - 2) // 2

    def body(*refs):
        srcs, lands = refs[:n], refs[n:2 * n]
        send_sems, recv_sems = refs[2 * n:2 * n + 2]
        for cp in _gather_copies(srcs, lands, send_sems, recv_sems):
            cp.wait_send()
            cp.wait_recv()

    thru = list(state[2:])
    out = pl.pallas_call(
        body, name=name,
        out_shape=[pltpu.HBM(o.shape, o.dtype) for o in thru],
        in_specs=[_HBM] * (2 * n) + [_SEM, _SEM, pl.BlockSpec(memory_space=pl.ANY)],
        out_specs=[_HBM] * (2 * n),
        input_output_aliases={i: i for i in range(2 * n)},
        compiler_params=pltpu.CompilerParams(has_side_effects=pltpu.SideEffectType.DATAFLOW_SIDE_EFFECTING),
    )(*thru, state[0], state[1], after)
    return out[n:]


def _scatter_copies(grads, lands, send_sems, recv_sems):
    x, y, c = _coords()
    me = 4 * x + 2 * y + c
    copies = []
    for a in range(len(grads)):
        r = grads[a].shape[0] // 8
        for k in range(1, 8):
            px, py, pc = (1 - x if k & 4 else x, 1 - y if k & 2 else y, 1 - c if k & 1 else c)
            rows = pl.ds(pl.multiple_of((4 * px + 2 * py + pc) * r, r), r)
            copies.append(pltpu.make_async_remote_copy(
                src_ref=grads[a].at[rows], dst_ref=lands[a].at[me],
                send_sem=send_sems.at[7 * a + k - 1], recv_sem=recv_sems.at[7 * a + k - 1],
                device_id=(px, py, pc), device_id_type=MESH))
    return copies


def scatter_start(grads, name):
    n = len(grads)

    def body(*refs):
        srcs, lands = refs[:n], refs[n:2 * n]
        send_sems, recv_sems = refs[2 * n:2 * n + 2]
        token = refs[-1]
        for cp in _scatter_copies(srcs, lands, send_sems, recv_sems):
            cp.start()
        token[...] = jnp.zeros_like(token)

    lands = [jnp.zeros((8, g.shape[0] // 8, g.shape[1]), g.dtype) for g in grads]
    ops = [pltpu.with_memory_space_constraint(a, pltpu.HBM) for a in list(grads) + lands]
    out = pl.pallas_call(
        body, name=name,
        out_shape=[pltpu.SemaphoreType.DMA((7 * n,)), pltpu.SemaphoreType.DMA((7 * n,))]
        + [pltpu.HBM(o.shape, o.dtype) for o in ops] + [jax.ShapeDtypeStruct((8, 128), F32)],
        in_specs=[_HBM] * (2 * n),
        out_specs=[_SEM, _SEM] + [_HBM] * (2 * n) + [pl.BlockSpec(memory_space=pltpu.VMEM)],
        input_output_aliases={i: 2 + i for i in range(2 * n)},
        compiler_params=pltpu.CompilerParams(has_side_effects=pltpu.SideEffectType.DATAFLOW_SIDE_EFFECTING),
    )(*ops)
    return out[:-1], out[-1]


def scatter_wait(state, after, name):
    n = (len(state) - 2) // 2

    def body(*refs):
        srcs, lands = refs[:n], refs[n:2 * n]
        send_sems, recv_sems = refs[2 * n:2 * n + 2]
        for cp in _scatter_copies(srcs, lands, send_sems, recv_sems):
            cp.wait_send()
            cp.wait_recv()

    thru = list(state[2:])
    after = list(after) if isinstance(after, (list, tuple)) else [after]
    out = pl.pallas_call(
        body, name=name,
        out_shape=[pltpu.HBM(o.shape, o.dtype) for o in thru],
        in_specs=[_HBM] * (2 * n) + [_SEM, _SEM] + [pl.BlockSpec(memory_space=pl.ANY)] * len(after),
        out_specs=[_HBM] * (2 * n),
        input_output_aliases={i: i for i in range(2 * n)},
        compiler_params=pltpu.CompilerParams(has_side_effects=pltpu.SideEffectType.DATAFLOW_SIDE_EFFECTING),
    )(*thru, state[0], state[1], *after)
    return out[:n], out[n:]


def rs_win_add_first(g, r, sel, next_dim, col, name):
    rows, cols = r.shape[2:]

    def body(sel_ref, gk_ref, rk_ref, gs_ref, rs_ref, keep_ref, send_ref):
        keep_ref[...] = gk_ref[...] + rk_ref[...]
        send_ref[...] = (gs_ref[...] + rs_ref[...]).astype(BF16)

    def g_map(flip):
        def f(j, s):
            nxt = 1 - s[next_dim] if flip else s[next_dim]
            return (nxt, j, s[2], 0, col) if next_dim == 0 else (j, nxt, s[2], 0, col)
        return f

    def r_map(flip):
        def f(j, s):
            nxt = 1 - s[next_dim] if flip else s[next_dim]
            return (nxt, j, 0, 0) if next_dim == 0 else (j, nxt, 0, 0)
        return f

    gblk = (None, None, None, rows, cols)
    rblk = (None, None, rows, cols)
    oblk = (None, rows, cols)
    return pl.pallas_call(
        body, name=name,
        grid_spec=pltpu.PrefetchScalarGridSpec(
            num_scalar_prefetch=1, grid=(2,),
            in_specs=[pl.BlockSpec(gblk, g_map(False)), pl.BlockSpec(rblk, r_map(False)),
                      pl.BlockSpec(gblk, g_map(True)), pl.BlockSpec(rblk, r_map(True))],
            out_specs=[pl.BlockSpec(oblk, lambda j, s: (j, 0, 0)),
                       pl.BlockSpec(oblk, lambda j, s: (j, 0, 0))]),
        out_shape=[jax.ShapeDtypeStruct((2, rows, cols), F32),
                   jax.ShapeDtypeStruct((2, rows, cols), BF16)],
        compiler_params=_params(),
    )(sel, g, r, g, r)


def rs_add_second(k, r, sel, name):
    _, rows, cols = k.shape
    tr = rows // 2 if rows % 32 == 0 else rows
    nt = rows // tr

    def body(sel_ref, kk_ref, rk_ref, ks_ref, rs_ref, keep_ref, send_ref):
        keep_ref[...] = kk_ref[...] + rk_ref[...].astype(F32)
        send_ref[...] = (ks_ref[...] + rs_ref[...].astype(F32)).astype(BF16)

    blk = (None, tr, cols)
    oblk = (tr, cols)
    return pl.pallas_call(
        body, name=name,
        grid_spec=pltpu.PrefetchScalarGridSpec(
            num_scalar_prefetch=1, grid=(nt,),
            in_specs=[
                pl.BlockSpec(blk, lambda i, s: (s[0], i, 0)),
                pl.BlockSpec(blk, lambda i, s: (s[0], i, 0)),
                pl.BlockSpec(blk, lambda i, s: (1 - s[0], i, 0)),
                pl.BlockSpec(blk, lambda i, s: (1 - s[0], i, 0)),
            ],
            out_specs=[pl.BlockSpec(oblk, lambda i, s: (i, 0)),
                       pl.BlockSpec(oblk, lambda i, s: (i, 0))]),
        out_shape=[jax.ShapeDtypeStruct((rows, cols), F32),
                   jax.ShapeDtypeStruct((rows, cols), BF16)],
        compiler_params=_params(),
    )(sel, k, r, k, r)


SEG_ROWS = (4800, 5824, 6848, 4096, 0, 1024, 2048, 3072)
LAT_ROWS = QL + KVL + ROPE
N_IN = 7872


def _seg_row(j):
    return pl.multiple_of(jnp.where(j < 3, 4800 + 1024 * j, jnp.where(j == 3, 4096, (j - 4) * 1024)), 8)


def proj_matmul(h, wt_bits, token):
    t = h.shape[0]
    tm = min(1024, t)

    def body(h_ref, w_hbm, tok_ref, o_ref, wt_ref, buf, sems):
        j = pl.program_id(0)
        slot = j % 2

        def fetch(seg, into):
            return pltpu.make_async_copy(w_hbm.at[pl.ds(_seg_row(seg), D)], buf.at[into], sems.at[into])

        @pl.when(pl.program_id(1) == 0)
        def _():
            @pl.when(j == 0)
            def _():
                fetch(j, slot).start()

            fetch(j, slot).wait()

            @pl.when(j + 1 < NSEG)
            def _():
                fetch(j + 1, 1 - slot).start()

            bits = pltpu.bitcast(buf[slot], jnp.uint32)
            row = lax.broadcasted_iota(jnp.int32, (D, D // 2), 0)
            live = jnp.logical_or(j != SEG_LAT, row < LAT_ROWS)
            lo = pltpu.bitcast(bits << 16, F32)
            hi = pltpu.bitcast(bits & jnp.uint32(0xFFFF0000), F32)
            wt_ref[:, :D // 2] = jnp.where(live, lo, 0.0).astype(BF16)
            wt_ref[:, D // 2:] = jnp.where(live, hi, 0.0).astype(BF16)

        o_ref[...] = _dot_nt(h_ref[...], wt_ref[...]).astype(BF16)

    return pl.pallas_call(
        body, name="proj_matmul", grid=(NSEG, t // tm),
        in_specs=[pl.BlockSpec((tm, D), lambda j, i: (i, 0)),
                  pl.BlockSpec(memory_space=pl.ANY),
                  pl.BlockSpec((8, 128), lambda j, i: (0, 0))],
        out_specs=[pl.BlockSpec((None, tm, D), lambda j, i: (j, i, 0)),
                   pl.BlockSpec((D, D), lambda j, i: (j, 0))],
        out_shape=[jax.ShapeDtypeStruct((NSEG, t, D), BF16), jax.ShapeDtypeStruct((NP, D), BF16)],
        scratch_shapes=[pltpu.VMEM((2, D, D // 2), F32), pltpu.SemaphoreType.DMA((2,))],
        compiler_params=_params(("arbitrary", "arbitrary")),
    )(h, wt_bits, token)


def dh_matmul(dproj, wt, token, seq, b):
    tm = min(1024, seq)
    nblk = seq // tm

    per = 2

    def body(b_ref, d_ref, w_ref, tok_ref, o_ref, acc_ref):
        k = pl.program_id(1)

        @pl.when(k == 0)
        def _():
            acc_ref[...] = jnp.zeros_like(acc_ref)

        part = _dot(d_ref[0], w_ref[0:D, :])
        for j in range(1, per):
            part = part + _dot(d_ref[j], w_ref[j * D:(j + 1) * D, :])
        acc_ref[...] += part

        @pl.when(k == NSEG // per - 1)
        def _():
            o_ref[...] = acc_ref[...]

    return pl.pallas_call(
        body, name="dh_matmul",
        grid_spec=pltpu.PrefetchScalarGridSpec(
            num_scalar_prefetch=1, grid=(nblk, NSEG // per),
            in_specs=[pl.BlockSpec((per, tm, D), lambda i, k, s: (k, s[0] * nblk + i, 0)),
                      pl.BlockSpec((per * D, D), lambda i, k, s: (k, 0)),
                      pl.BlockSpec((8, 128), lambda i, k, s: (0, 0))],
            out_specs=pl.BlockSpec((tm, D), lambda i, k, s: (i, 0)),
            scratch_shapes=[pltpu.VMEM((tm, D), F32)]),
        out_shape=jax.ShapeDtypeStruct((seq, D), F32),
        compiler_params=_params(("parallel", "arbitrary")),
    )(jnp.full((1,), b, jnp.int32), dproj, wt, token)


def win_grad_matmul(h, dproj, token):
    t = h.shape[0]
    tk = min(2048, t)
    nk = t // tk

    def body(h_ref, d_ref, tok_ref, o_hbm, acc_ref, sem):
        j = pl.program_id(0)
        k = pl.program_id(1)

        @pl.when(k == 0)
        def _():
            acc_ref[...] = jnp.zeros_like(acc_ref)

        acc_ref[...] += _dot_tn(d_ref[...], h_ref[...])

        @pl.when(jnp.logical_and(k == nk - 1, j != SEG_LAT))
        def _():
            cp = pltpu.make_async_copy(acc_ref, o_hbm.at[pl.ds(_seg_row(j), D)], sem)
            cp.start()
            cp.wait()

        @pl.when(jnp.logical_and(k == nk - 1, j == SEG_LAT))
        def _():
            cp = pltpu.make_async_copy(acc_ref.at[pl.ds(0, LAT_ROWS)],
                                       o_hbm.at[pl.ds(SEG_ROWS[SEG_LAT], LAT_ROWS)], sem)
            cp.start()
            cp.wait()

    return pl.pallas_call(
        body, name="win_grad_matmul", grid=(NSEG, nk),
        in_specs=[pl.BlockSpec((tk, D), lambda j, k: (k, 0)),
                  pl.BlockSpec((None, tk, D), lambda j, k: (j, k, 0)),
                  pl.BlockSpec((8, 128), lambda j, k: (0, 0))],
        out_specs=pl.BlockSpec(memory_space=pl.ANY),
        out_shape=jax.ShapeDtypeStruct((N_IN, D), F32),
        scratch_shapes=[pltpu.VMEM((D, D), F32), pltpu.SemaphoreType.DMA],
        compiler_params=_params(("arbitrary", "arbitrary")),
    )(h, dproj, token)


def grad_matmul(a, b, name):
    t, m = a.shape
    n = b.shape[1]
    tk = min(1024, t)
    nk = t // tk

    def body(a_ref, b_ref, o_ref, acc_ref):
        k = pl.program_id(0)

        @pl.when(k == 0)
        def _():
            acc_ref[...] = jnp.zeros_like(acc_ref)

        acc_ref[...] += _dot_tn(a_ref[...], b_ref[...])

        @pl.when(k == nk - 1)
        def _():
            o_ref[...] = acc_ref[...].astype(BF16)

    return pl.pallas_call(
        body, name=name, grid=(nk,),
        in_specs=[pl.BlockSpec((tk, m), lambda k: (k, 0)),
                  pl.BlockSpec((tk, n), lambda k: (k, 0))],
        out_specs=pl.BlockSpec((m, n), lambda k: (0, 0)),
        out_shape=jax.ShapeDtypeStruct((m, n), BF16),
        scratch_shapes=[pltpu.VMEM((m, n), F32)],
        compiler_params=_params(("arbitrary",)),
    )(a, b)


def ada_fwd(c_all, w_ada, b_cols):
    def body(c_ref, w_ref, b_ref, o_ref):
        o_ref[...] = _dot(c_ref[...].astype(BF16), w_ref[...].astype(BF16)) + b_ref[...]

    return pl.pallas_call(
        body, name="ada_fwd",
        out_shape=jax.ShapeDtypeStruct((c_all.shape[0], w_ada.shape[1]), F32),
        compiler_params=_params(),
    )(c_all, w_ada, b_cols)


def ada_bwd(c_all, dmod_cols):
    def body(c_ref, d_ref, o_ref):
        o_ref[...] = _dot_tn(c_ref[...].astype(BF16), d_ref[...].astype(BF16))

    return pl.pallas_call(
        body, name="ada_bwd",
        out_shape=jax.ShapeDtypeStruct((c_all.shape[1], dmod_cols.shape[1]), F32),
        compiler_params=_params(),
    )(c_all, dmod_cols)


def slot_sum(g):
    def body(g_ref, o_ref):
        acc = g_ref[0]
        for s in range(1, 8):
            acc = acc + g_ref[s]
        o_ref[...] = acc

    return pl.pallas_call(
        body, name="slot_sum",
        out_shape=jax.ShapeDtypeStruct(g.shape[1:], F32),
    )(g)


def prenorm_fwd(x2, scale, shift, g_pre, seq, token):
    t = x2.shape[0]
    tm = min(512, seq)
    tpb = seq // tm

    def body(x_ref, sc_ref, sh_ref, g_ref, tok_ref, h_ref):
        xv = x_ref[...]
        r = lax.rsqrt(jnp.mean(xv * xv, axis=-1, keepdims=True) + EPS)
        hv = (xv * r * g_ref[...]) * (1.0 + sc_ref[...]) + sh_ref[...]
        h_ref[...] = hv.astype(BF16)

    per_batch = pl.BlockSpec((None, 1, D), lambda i: (i // tpb, 0, 0))
    return pl.pallas_call(
        body, name="prenorm_fwd", grid=(t // tm,),
        in_specs=[pl.BlockSpec((tm, D), lambda i: (i, 0)), per_batch, per_batch,
                  pl.BlockSpec((1, D), lambda i: (0, 0)), pl.BlockSpec((8, 128), lambda i: (0, 0))],
        out_specs=pl.BlockSpec((tm, D), lambda i: (i, 0)),
        out_shape=jax.ShapeDtypeStruct((t, D), BF16),
        compiler_params=_params(("parallel",)),
    )(x2, scale, shift, g_pre, token)


def prenorm_bwd(dh, x2, dout, scale, g_pre, seq, token, b, gx_prev):
    t = x2.shape[0]
    tm = min(512, seq)
    tpb = seq // tm
    if gx_prev is None:
        gx_prev = lax.empty((t, D), F32)

    def body(b_ref, dh_ref, x_ref, do_ref, sc_ref, g_ref, tok_ref, gxp_ref, gx_ref, dsh_ref, dsc_ref, dg_ref):
        i = pl.program_id(0)
        xv = x_ref[...]
        dhv = dh_ref[...]
        g = g_ref[...]
        r = lax.rsqrt(jnp.mean(xv * xv, axis=-1, keepdims=True) + EPS)
        nrm = xv * r
        dxn = dhv * (1.0 + sc_ref[...])
        dn = dxn * g
        dx = r * (dn - nrm * jnp.mean(dn * nrm, axis=-1, keepdims=True))
        gx_ref[...] = dx + do_ref[...]

        @pl.when(i == 0)
        def _():
            dsh_ref[...] = jnp.zeros_like(dsh_ref)
            dsc_ref[...] = jnp.zeros_like(dsc_ref)
            dg_ref[...] = jnp.zeros_like(dg_ref)

        dsh_ref[...] += jnp.sum(dhv, axis=0, keepdims=True)
        dsc_ref[...] += jnp.sum(dhv * (nrm * g), axis=0, keepdims=True)
        dg_ref[...] += jnp.sum(dxn * nrm, axis=0, keepdims=True)

    row = pl.BlockSpec((tm, D), lambda i, s: (i, 0))
    grow = pl.BlockSpec((tm, D), lambda i, s: (s[0] * tpb + i, 0))
    per_batch = pl.BlockSpec((None, 1, D), lambda i, s: (s[0], 0, 0))
    vec = pl.BlockSpec((1, D), lambda i, s: (0, 0))
    return pl.pallas_call(
        body, name="prenorm_bwd",
        grid_spec=pltpu.PrefetchScalarGridSpec(
            num_scalar_prefetch=1, grid=(tpb,),
            in_specs=[row, grow, grow, per_batch, vec, pl.BlockSpec((8, 128), lambda i, s: (0, 0)),
                      pl.BlockSpec(memory_space=pl.ANY)],
            out_specs=[grow, vec, vec, vec]),
        out_shape=[jax.ShapeDtypeStruct((t, D), F32), jax.ShapeDtypeStruct((1, D), F32),
                   jax.ShapeDtypeStruct((1, D), F32), jax.ShapeDtypeStruct((1, D), F32)],
        input_output_aliases={7: 0},
        compiler_params=_params(("arbitrary",)),
    )(jnp.full((1,), b, jnp.int32), dh, x2, dout, scale, g_pre, token, gx_prev)


CONV_TC = 128


def _shift_down(u, k, rows):
    idx = lax.broadcasted_iota(jnp.int32, u.shape, 0)
    return jnp.where(idx >= k, pltpu.roll(u, k, 0), 0.0)


def _shift_up(u, k, rows):
    idx = lax.broadcasted_iota(jnp.int32, u.shape, 0)
    return jnp.where(idx < rows - k, pltpu.roll(u, rows - k, 0), 0.0)


def conv_fwd(proj, conv_w, seq):
    t = proj.shape[1]
    nb = t // seq

    def body(p_ref, w_ref, y_ref):
        av = p_ref[0].astype(F32)
        ab = p_ref[1].astype(F32)
        ac = p_ref[2].astype(F32)
        az = p_ref[3].astype(F32)
        w = w_ref[...]
        u = ac * av
        y1 = _shift_down(u, 2, seq) * w[0:1] + _shift_down(u, 1, seq) * w[1:2] + u * w[2:3]
        y_ref[...] = (ab * y1 * (az * _sig(az))).astype(BF16)

    return pl.pallas_call(
        body, name="conv_fwd", grid=(nb, D // CONV_TC),
        in_specs=[pl.BlockSpec((4, seq, CONV_TC), lambda b, ci: (1, b, ci)),
                  pl.BlockSpec((8, CONV_TC), lambda b, ci: (0, ci))],
        out_specs=pl.BlockSpec((seq, CONV_TC), lambda b, ci: (b, ci)),
        out_shape=jax.ShapeDtypeStruct((t, D), BF16),
        compiler_params=_params(("parallel", "parallel")),
    )(proj, conv_w)


def conv_bwd(dproj, proj, dy, conv_w, seq):
    t = proj.shape[1]
    nb = t // seq

    def body(dp_in_ref, p_ref, dy_ref, w_ref, dp_ref, dw_ref):
        b = pl.program_id(1)
        av = p_ref[0].astype(F32)
        ab = p_ref[1].astype(F32)
        ac = p_ref[2].astype(F32)
        az = p_ref[3].astype(F32)
        dyv = dy_ref[...].astype(F32)
        w = w_ref[...]
        u = ac * av
        u1 = _shift_down(u, 1, seq)
        u2 = _shift_down(u, 2, seq)
        y1 = u2 * w[0:1] + u1 * w[1:2] + u * w[2:3]
        sz = _sig(az)
        silu = az * sz
        dy1 = dyv * ab * silu
        du = dy1 * w[2:3] + _shift_up(dy1, 1, seq) * w[1:2] + _shift_up(dy1, 2, seq) * w[0:1]
        dp_ref[0] = (du * ac).astype(BF16)
        dp_ref[1] = (dyv * y1 * silu).astype(BF16)
        dp_ref[2] = (du * av).astype(BF16)
        dp_ref[3] = (dyv * ab * y1 * (sz * (1.0 + az * (1.0 - sz)))).astype(BF16)

        @pl.when(b == 0)
        def _():
            dw_ref[...] = jnp.zeros_like(dw_ref)

        dw_ref[0:1, :] += jnp.sum(dy1 * u2, axis=0, keepdims=True)
        dw_ref[1:2, :] += jnp.sum(dy1 * u1, axis=0, keepdims=True)
        dw_ref[2:3, :] += jnp.sum(dy1 * u, axis=0, keepdims=True)

    return pl.pallas_call(
        body, name="conv_bwd", grid=(D // CONV_TC, nb),
        in_specs=[pl.BlockSpec(memory_space=pl.ANY),
                  pl.BlockSpec((4, seq, CONV_TC), lambda ci, b: (1, b, ci)),
                  pl.BlockSpec((seq, CONV_TC), lambda ci, b: (b, ci)),
                  pl.BlockSpec((8, CONV_TC), lambda ci, b: (0, ci))],
        out_specs=[pl.BlockSpec((4, seq, CONV_TC), lambda ci, b: (1, b, ci)),
                   pl.BlockSpec((8, CONV_TC), lambda ci, b: (0, ci))],
        out_shape=[jax.ShapeDtypeStruct(dproj.shape, BF16),
                   jax.ShapeDtypeStruct((8, D), F32)],
        input_output_aliases={0: 0},
        compiler_params=_params(("parallel", "arbitrary")),
    )(dproj, proj, dy, conv_w)


def _rope_tables(pos_ref, invf_ref, ma_ref, mb_ref, sign):
    ang = pos_ref[...].astype(F32) * invf_ref[...]
    cs = jnp.cos(ang)
    sn = jnp.sin(ang) * sign
    return cs, sn * ma_ref[...], sn * mb_ref[...]


def _rotate(v, cs, sa, sb):
    return v * cs + pltpu.roll(v, 128 - HALF, 1) * sa + pltpu.roll(v, HALF, 1) * sb


MLA_TM = 512


def mla_prep_fwd(proj, pos, g_q, g_kv, wuq, wukv, tabs):
    t = proj.shape[1]
    tm = min(MLA_TM, t)

    def body(lat_ref, pos_ref, gq_ref, gkv_ref, wuq_ref, wukv_ref, invf_ref, ma_ref, mb_ref,
             q_ref, k_ref, kv_ref, qn_ref, kvn_ref):
        lat = lat_ref[...].astype(F32)
        ql = lat[:, :QL]
        kl = lat[:, QL:QL + KVL]
        kr = lat[:, QL + KVL:QL + KVL + 128]
        qn = (ql * lax.rsqrt(jnp.mean(ql * ql, axis=-1, keepdims=True) + EPS) * gq_ref[...]).astype(BF16)
        kvn = (kl * lax.rsqrt(jnp.mean(kl * kl, axis=-1, keepdims=True) + EPS) * gkv_ref[...]).astype(BF16)
        qn_ref[...] = qn
        kvn_ref[...] = kvn
        cs, sa, sb = _rope_tables(pos_ref, invf_ref, ma_ref, mb_ref, 1.0)
        q = _dot_nt(qn, wuq_ref[...]) * (SM_SCALE * LOG2E)
        kv = _dot_nt(kvn, wukv_ref[...]).astype(BF16)
        kv_ref[...] = kv
        kpe = _rotate(kr, cs, sa, sb).astype(BF16)
        for hh in range(H):
            lo, mid, hi = hh * DQK, hh * DQK + 128, (hh + 1) * DQK
            q_ref[:, lo:mid] = q[:, lo:mid].astype(BF16)
            q_ref[:, mid:hi] = _rotate(q[:, mid:hi], cs, sa, sb).astype(BF16)
            k_ref[:, lo:mid] = kv[:, lo:mid]
            k_ref[:, mid:hi] = kpe

    row = lambda w: pl.BlockSpec((tm, w), lambda i: (i, 0))
    const = lambda a: pl.BlockSpec(a.shape, lambda i: (0,) * a.ndim)
    return pl.pallas_call(
        body, name="mla_prep_fwd", grid=(t // tm,),
        in_specs=[pl.BlockSpec((None, tm, D), lambda i: (SEG_LAT, i, 0)), row(1),
                  const(g_q), const(g_kv), const(wuq), const(wukv)] + [const(a) for a in tabs],
        out_specs=[row(H * DQK), row(H * DQK), row(H * DQK), row(QL), row(KVL)],
        out_shape=[jax.ShapeDtypeStruct((t, H * DQK), BF16)] * 3
        + [jax.ShapeDtypeStruct((t, QL), BF16), jax.ShapeDtypeStruct((t, KVL), BF16)],
        compiler_params=_params(("parallel",)),
    )(proj, pos, g_q, g_kv, wuq, wukv, *tabs)


def mla_prep_bwd(dproj, proj, dq_rot, dk, dv, pos, g_q, g_kv, wuq, wukv, tabs):
    t = proj.shape[1]
    tm = min(MLA_TM, t)

    def body(dp_in_ref, lat_ref, dqr_ref, dk_ref, dv_ref, pos_ref, gq_ref, gkv_ref, wuq_ref, wukv_ref,
             invf_ref, ma_ref, mb_ref, dp_ref, dq_ref, dkv_ref, dgq_ref, dgkv_ref):
        i = pl.program_id(0)
        lat = lat_ref[...].astype(F32)
        ql = lat[:, :QL]
        kl = lat[:, QL:QL + KVL]
        rq = lax.rsqrt(jnp.mean(ql * ql, axis=-1, keepdims=True) + EPS)
        rk = lax.rsqrt(jnp.mean(kl * kl, axis=-1, keepdims=True) + EPS)
        nq = ql * rq
        nk = kl * rk
        cs, sa, sb = _rope_tables(pos_ref, invf_ref, ma_ref, mb_ref, -1.0)
        dkpe = jnp.zeros((tm, 128), F32)
        for hh in range(H):
            lo, mid, hi = hh * DQK, hh * DQK + 128, (hh + 1) * DQK
            dq_ref[:, lo:mid] = (dqr_ref[:, lo:mid] * SM_SCALE).astype(BF16)
            dq_ref[:, mid:hi] = _rotate(dqr_ref[:, mid:hi] * SM_SCALE, cs, sa, sb).astype(BF16)
            dkv_ref[:, lo:mid] = dk_ref[:, lo:mid]
            dkv_ref[:, mid:hi] = dv_ref[:, hh * DV:(hh + 1) * DV]
            dkpe = dkpe + dk_ref[:, mid:hi].astype(F32)
        lane = lax.broadcasted_iota(jnp.int32, (tm, 128), 1)
        dkr = jnp.where(lane < ROPE, _rotate(dkpe, cs, sa, sb), 0.0)
        dqn = _dot(dq_ref[...], wuq_ref[...])
        dkvn = _dot(dkv_ref[...], wukv_ref[...])
        gq = gq_ref[...]
        gkv = gkv_ref[...]
        dnq = dqn * gq
        dnk = dkvn * gkv
        dql = rq * (dnq - nq * jnp.mean(dnq * nq, axis=-1, keepdims=True))
        dkl = rk * (dnk - nk * jnp.mean(dnk * nk, axis=-1, keepdims=True))
        dp_ref[:, :QL] = dql.astype(BF16)
        dp_ref[:, QL:QL + KVL] = dkl.astype(BF16)
        dp_ref[:, QL + KVL:QL + KVL + 128] = dkr.astype(BF16)
        dp_ref[:, QL + KVL + 128:] = jnp.zeros((tm, D - QL - KVL - 128), BF16)

        @pl.when(i == 0)
        def _():
            dgq_ref[...] = jnp.zeros_like(dgq_ref)
            dgkv_ref[...] = jnp.zeros_like(dgkv_ref)

        dgq_ref[...] += jnp.sum(dqn * nq, axis=0, keepdims=True)
        dgkv_ref[...] += jnp.sum(dkvn * nk, axis=0, keepdims=True)

    row = lambda w: pl.BlockSpec((tm, w), lambda i: (i, 0))
    const = lambda a: pl.BlockSpec(a.shape, lambda i: (0,) * a.ndim)
    seg = pl.BlockSpec((None, tm, D), lambda i: (SEG_LAT, i, 0))
    return pl.pallas_call(
        body, name="mla_prep_bwd", grid=(t // tm,),
        in_specs=[pl.BlockSpec(memory_space=pl.ANY), seg, row(H * DQK), row(H * DQK), row(H * DV), row(1),
                  const(g_q), const(g_kv), const(wuq), const(wukv)] + [const(a) for a in tabs],
        out_specs=[seg, row(H * DQK), row(H * DQK),
                   pl.BlockSpec((1, QL), lambda i: (0, 0)), pl.BlockSpec((1, KVL), lambda i: (0, 0))],
        out_shape=[jax.ShapeDtypeStruct(dproj.shape, BF16),
                   jax.ShapeDtypeStruct((t, H * DQK), BF16), jax.ShapeDtypeStruct((t, H * DQK), BF16),
                   jax.ShapeDtypeStruct((1, QL), F32), jax.ShapeDtypeStruct((1, KVL), F32)],
        input_output_aliases={0: 0},
        compiler_params=_params(("arbitrary",)),
    )(dproj, proj, dq_rot, dk, dv, pos, g_q, g_kv, wuq, wukv, *tabs)


def _causal_mask(s, shift):
    row = lax.broadcasted_iota(jnp.int32, s.shape, 0)
    col = lax.broadcasted_iota(jnp.int32, s.shape, 1)
    return jnp.where(col <= row + shift, s, -1e30)


def flash_fwd(q, k, kv, nb, seq):
    t = q.shape[0]
    tq = min(FLASH_TQ, seq)
    nq = seq // tq

    def body(q_ref, k_ref, v_ref, o_ref, lse_ref):
        for qi in range(nq):
            qs = slice(qi * tq, (qi + 1) * tq)
            qv = q_ref[qs, :]
            m = jnp.full((tq, 1), -1e30, F32)
            l = jnp.zeros((tq, 1), F32)
            acc = jnp.zeros((tq, DV), F32)
            for j in range(qi + 1):
                ks = slice(j * tq, (j + 1) * tq)
                s = _dot_nt(qv, k_ref[ks, :])
                if j == qi:
                    s = _causal_mask(s, 0)
                m_new = jnp.maximum(m, jnp.max(s, axis=1, keepdims=True))
                p = jnp.exp2(s - m_new)
                alpha = jnp.exp2(m - m_new)
                l = alpha * l + jnp.sum(p, axis=1, keepdims=True)
                acc = alpha * acc + _dot(p.astype(BF16), v_ref[ks, :])
                m = m_new
            o_ref[qs, :] = (acc / l).astype(BF16)
            lse_ref[qs, :] = jnp.broadcast_to(m + jnp.log(l) * LOG2E, (tq, DV))

    out_blk = pl.BlockSpec((seq, DV), lambda b, h: (b, h))
    return pl.pallas_call(
        body, name="flash_fwd", grid=(nb, H),
        in_specs=[pl.BlockSpec((seq, DQK), lambda b, h: (b, h)),
                  pl.BlockSpec((seq, DQK), lambda b, h: (b, h)),
                  pl.BlockSpec((seq, DV), lambda b, h: (b, 2 * h + 1))],
        out_specs=[out_blk, out_blk],
        out_shape=[jax.ShapeDtypeStruct((t, H * DV), BF16), jax.ShapeDtypeStruct((t, H * DV), F32)],
        compiler_params=_params(("parallel", "parallel")),
    )(q, k, kv)


def flash_bwd(q, k, kv, o, do, lse, nb, seq, token):
    t = q.shape[0]
    tq = min(FLASH_TQ, seq)
    nq = seq // tq

    def body(q_ref, k_ref, v_ref, o_ref, do_ref, lse_ref, tok_ref, dq_ref, dk_ref, dv_ref):
        delta, lse = [], []
        for qi in range(nq):
            qs = slice(qi * tq, (qi + 1) * tq)
            dl = jnp.sum(do_ref[qs, :].astype(F32) * o_ref[qs, :].astype(F32), axis=1, keepdims=True)
            delta.append(jnp.broadcast_to(dl, (tq, DV)).T[:1, :])
            lse.append(lse_ref[qs, :].T[:1, :])
        for ki in range(nq):
            ks = slice(ki * tq, (ki + 1) * tq)
            kb = k_ref[ks, :]
            vb = v_ref[ks, :]
            dk = jnp.zeros((tq, DQK), F32)
            dv = jnp.zeros((tq, DV), F32)
            for qi in range(ki, nq):
                qs = slice(qi * tq, (qi + 1) * tq)
                qv = q_ref[qs, :]
                dov = do_ref[qs, :]
                st = _dot_nt(kb, qv)
                if qi == ki:
                    row = lax.broadcasted_iota(jnp.int32, st.shape, 0)
                    col = lax.broadcasted_iota(jnp.int32, st.shape, 1)
                    st = jnp.where(row <= col, st, -1e30)
                pt = jnp.exp2(st - lse[qi])
                dpt = _dot_nt(vb, dov)
                dzt = (pt * (dpt - delta[qi])).astype(BF16)
                dv = dv + _dot(pt.astype(BF16), dov)
                dk = dk + _dot(dzt, qv)
                dqb = _dot_tn(dzt, kb)
                if ki == 0:
                    dq_ref[qs, :] = dqb
                else:
                    dq_ref[qs, :] += dqb
            dk_ref[ks, :] = (dk * LN2).astype(BF16)
            dv_ref[ks, :] = dv.astype(BF16)

    full = lambda w, col: pl.BlockSpec((seq, w), col)
    same = lambda b, h: (b, h)
    return pl.pallas_call(
        body, name="flash_bwd", grid=(nb, H),
        in_specs=[full(DQK, same), full(DQK, same), full(DV, lambda b, h: (b, 2 * h + 1)),
                  full(DV, same), full(DV, same), full(DV, same),
                  pl.BlockSpec((8, 128), lambda b, h: (0, 0))],
        out_specs=[full(DQK, same), full(DQK, same), full(DV, same)],
        out_shape=[jax.ShapeDtypeStruct((t, H * DQK), F32), jax.ShapeDtypeStruct((t, H * DQK), BF16),
                   jax.ShapeDtypeStruct((t, H * DV), BF16)],
        compiler_params=_params(("parallel", "parallel")),
    )(q, k, kv, o, do, lse, token)


TAIL_TM = 512


def tail_fwd(y, attn, proj, x2, tgt, gate, g_post, wco, wmo, wout, seq):
    t = y.shape[0]
    nb = t // seq
    tm = min(TAIL_TM, seq)
    tpb = seq // tm

    def body(y_ref, at_ref, p_ref, x_ref, t_ref, gate_ref, gp_ref, wco_ref, wmo_ref, wout_ref,
             o_ref, ya_ref, yb_ref, m_ref, do2_ref, dout_ref, dgate_ref, dgp_ref, loss_ref):
        i = pl.program_id(0)
        bz = p_ref[0].astype(F32)
        ga = p_ref[1].astype(F32)
        gb = p_ref[2].astype(F32)
        ov = (at_ref[...].astype(F32) * (bz * _sig(bz))).astype(BF16)
        o_ref[...] = ov
        ya = _dot(y_ref[...], wco_ref[...])
        yb = _dot(ov, wmo_ref[...])
        ya_ref[...] = ya.astype(BF16)
        yb_ref[...] = yb.astype(BF16)
        mv = (_sig(ga) * ya + _sig(gb) * yb).astype(BF16)
        m_ref[...] = mv
        o2 = _dot(mv, wout_ref[...])
        r = lax.rsqrt(jnp.mean(o2 * o2, axis=-1, keepdims=True) + EPS)
        nrm = o2 * r
        gp = gp_ref[...]
        gate_v = gate_ref[...]
        rn = nrm * gp
        err = x_ref[...] + gate_v * rn - t_ref[...]
        dout = err * (1.0 / D)
        dout_ref[...] = dout
        dn = dout * gate_v * gp
        do2_ref[...] = (r * (dn - nrm * jnp.mean(dn * nrm, axis=-1, keepdims=True))).astype(BF16)

        @pl.when(i % tpb == 0)
        def _():
            dgate_ref[...] = jnp.zeros_like(dgate_ref)

        @pl.when(i == 0)
        def _():
            dgp_ref[...] = jnp.zeros_like(dgp_ref)
            loss_ref[...] = jnp.zeros_like(loss_ref)

        dgate_ref[...] += jnp.sum(dout * rn, axis=0, keepdims=True)
        dgp_ref[...] += jnp.sum(dout * gate_v * nrm, axis=0, keepdims=True)
        loss_ref[...] += 0.5 * jnp.sum(jnp.mean(err * err, axis=-1, keepdims=True), axis=0, keepdims=True)

    row = pl.BlockSpec((tm, D), lambda i: (i, 0))
    per_batch = pl.BlockSpec((None, 1, D), lambda i: (i // tpb, 0, 0))
    vec = pl.BlockSpec((1, D), lambda i: (0, 0))
    wgt = pl.BlockSpec((D, D), lambda i: (0, 0))
    act = jax.ShapeDtypeStruct((t, D), BF16)
    return pl.pallas_call(
        body, name="tail_fwd", grid=(t // tm,),
        in_specs=[row, row, pl.BlockSpec((3, tm, D), lambda i: (0, i, 0)), row, row, per_batch, vec,
                  wgt, wgt, wgt],
        out_specs=[row, row, row, row, row, row, per_batch, vec, pl.BlockSpec((1, 1), lambda i: (0, 0))],
        out_shape=[act, act, act, act, act, jax.ShapeDtypeStruct((t, D), F32),
                   jax.ShapeDtypeStruct((nb, 1, D), F32), jax.ShapeDtypeStruct((1, D), F32),
                   jax.ShapeDtypeStruct((1, 1), F32)],
        compiler_params=_params(("arbitrary",)),
    )(y, attn, proj, x2, tgt, gate, g_post, wco, wmo, wout)


def tail_bwd(do2, proj, ya, yb, attn, wout, wmo, wco):
    t = do2.shape[0]
    tm = min(TAIL_TM, t)

    def body(do2_ref, p_ref, ya_ref, yb_ref, at_ref, wout_ref, wmo_ref, wco_ref,
             dp_ref, dya_ref, dyb_ref, dat_ref, dy_ref):
        bz = p_ref[0].astype(F32)
        ga = p_ref[1].astype(F32)
        gb = p_ref[2].astype(F32)
        dm = _dot_nt(do2_ref[...], wout_ref[...])
        sa = _sig(ga)
        sb = _sig(gb)
        dya = (dm * sa).astype(BF16)
        dyb = (dm * sb).astype(BF16)
        dya_ref[...] = dya
        dyb_ref[...] = dyb
        dp_ref[1] = (dm * ya_ref[...].astype(F32) * (sa * (1.0 - sa))).astype(BF16)
        dp_ref[2] = (dm * yb_ref[...].astype(F32) * (sb * (1.0 - sb))).astype(BF16)
        dov = _dot_nt(dyb, wmo_ref[...])
        sz = _sig(bz)
        dat_ref[...] = (dov * (bz * sz)).astype(BF16)
        dp_ref[0] = (dov * at_ref[...].astype(F32) * (sz * (1.0 + bz * (1.0 - sz)))).astype(BF16)
        dy_ref[...] = _dot_nt(dya, wco_ref[...]).astype(BF16)

    row = pl.BlockSpec((tm, D), lambda i: (i, 0))
    seg3 = pl.BlockSpec((3, tm, D), lambda i: (0, i, 0))
    wgt = pl.BlockSpec((D, D), lambda i: (0, 0))
    act = jax.ShapeDtypeStruct((t, D), BF16)
    return pl.pallas_call(
        body, name="tail_bwd", grid=(t // tm,),
        in_specs=[row, seg3, row, row, row, wgt, wgt, wgt],
        out_specs=[seg3, row, row, row, row],
        out_shape=[jax.ShapeDtypeStruct((NSEG, t, D), BF16), act, act, act, act],
        compiler_params=_params(("parallel",)),
    )(do2, proj, ya, yb, attn, wout, wmo, wco)


def adamw(w, m, v, g, g2, name, token=None):
    rows, cols = w.shape
    tr = rows
    for cand in (256, 128, 64, 32, 16, 8):
        if rows % cand == 0 and rows > cand:
            tr = cand
            break
    has2 = g2 is not None
    n_in = 4 + has2

    def body(*refs):
        w_ref, m_ref, v_ref, g_ref = refs[:4]
        go_ref, d_ref, mo_ref, vo_ref = refs[-4:]
        grad = g_ref[...] + refs[4][...].astype(F32) if has2 else g_ref[...]
        mn = ADAM_B1 * m_ref[...] + (1.0 - ADAM_B1) * grad
        vn = ADAM_B2 * v_ref[...] + (1.0 - ADAM_B2) * (grad * grad)
        m_hat = mn / (1.0 - ADAM_B1 ** ADAM_STEP)
        v_hat = vn / (1.0 - ADAM_B2 ** ADAM_STEP)
        go_ref[...] = grad
        d_ref[...] = -ADAM_LR * (m_hat / (jnp.sqrt(v_hat) + ADAM_EPS) + ADAM_WD * w_ref[...])
        mo_ref[...] = mn
        vo_ref[...] = vn

    blk = pl.BlockSpec((tr, cols), lambda i: (i, 0))
    ins = [w, m, v, g] + ([g2] if has2 else [])
    specs = [blk] * n_in
    if token is not None:
        ins.append(token)
        specs.append(pl.BlockSpec((8, 128), lambda i: (0, 0)))
    return pl.pallas_call(
        body, name=name, grid=(rows // tr,),
        in_specs=specs, out_specs=[blk] * 4,
        out_shape=[jax.ShapeDtypeStruct((rows, cols), F32)] * 4,
        compiler_params=_params(("parallel",)),
    )(*ins)


def adamw_scattered(w, m, v, own, land, me, tr, name, transpose=False):
    slot_rows = land.shape[1]
    cols = land.shape[2]
    rows = slot_rows if transpose else w.shape[0]
    per_slot = slot_rows // tr

    def body(me_ref, w_ref, m_ref, v_ref, own_ref, land_ref, go_ref, d_ref, mo_ref, vo_ref):
        grad = own_ref[...].astype(F32)
        for s in range(8):
            grad = grad + land_ref[s].astype(F32)
        if transpose:
            grad = grad.T
        mn = ADAM_B1 * m_ref[...] + (1.0 - ADAM_B1) * grad
        vn = ADAM_B2 * v_ref[...] + (1.0 - ADAM_B2) * (grad * grad)
        m_hat = mn / (1.0 - ADAM_B1 ** ADAM_STEP)
        v_hat = vn / (1.0 - ADAM_B2 ** ADAM_STEP)
        go_ref[...] = grad
        d_ref[...] = -ADAM_LR * (m_hat / (jnp.sqrt(v_hat) + ADAM_EPS) + ADAM_WD * w_ref[...])
        mo_ref[...] = mn
        vo_ref[...] = vn

    wblk = pl.BlockSpec(w.shape if transpose else (tr, w.shape[1]), lambda i, s: (i, 0))
    return pl.pallas_call(
        body, name=name,
        grid_spec=pltpu.PrefetchScalarGridSpec(
            num_scalar_prefetch=1, grid=(rows // tr,),
            in_specs=[wblk, wblk, wblk,
                      pl.BlockSpec((tr, cols), lambda i, s: (s[0] * per_slot + i, 0)),
                      pl.BlockSpec((8, tr, cols), lambda i, s: (0, i, 0))],
            out_specs=[wblk] * 4),
        out_shape=[jax.ShapeDtypeStruct(w.shape, F32)] * 4,
        compiler_params=_params(),
    )(me, w, m, v, own, land)


def adamw_win(wt, mt, vt, ka, ra, kb, rb):
    rows = wt.shape[0]
    tc = 256
    nh = (D // 2) // tc

    def body(w_ref, m_ref, v_ref, ka_ref, ra_ref, kb_ref, rb_ref, go_ref, d_ref, mo_ref, vo_ref):
        first = pl.program_id(0) < nh
        grad = jnp.where(first, ka_ref[...] + ra_ref[...].astype(F32), kb_ref[...] + rb_ref[...].astype(F32))
        mn = ADAM_B1 * m_ref[...] + (1.0 - ADAM_B1) * grad
        vn = ADAM_B2 * v_ref[...] + (1.0 - ADAM_B2) * (grad * grad)
        m_hat = mn / (1.0 - ADAM_B1 ** ADAM_STEP)
        v_hat = vn / (1.0 - ADAM_B2 ** ADAM_STEP)
        go_ref[...] = grad
        d_ref[...] = -ADAM_LR * (m_hat / (jnp.sqrt(v_hat) + ADAM_EPS) + ADAM_WD * w_ref[...])
        mo_ref[...] = mn
        vo_ref[...] = vn

    blk = pl.BlockSpec((rows, tc), lambda j: (0, j))
    lo = pl.BlockSpec((rows, tc), lambda j: (0, jnp.minimum(j, nh - 1)))
    hi = pl.BlockSpec((rows, tc), lambda j: (0, jnp.maximum(j - nh, 0)))
    return pl.pallas_call(
        body, name="adamw_w_in", grid=(D // tc,),
        in_specs=[blk, blk, blk, lo, lo, hi, hi], out_specs=[blk] * 4,
        out_shape=[jax.ShapeDtypeStruct((rows, D), F32)] * 4,
        compiler_params=_params(("parallel",)),
    )(wt, mt, vt, ka, ra, kb, rb)


_ORD_A = ("x", "y", "c")
_ORD_B = ("y", "x", "c")


def _rows128(a, rows):
    flat = a.reshape(-1)
    return jnp.pad(flat, (0, rows * 128 - flat.shape[0])).reshape(rows, 128)


def kernel(x, c, positions, w_ada, b_ada, g_pre, w_in, conv_w, w_conv_out, g_q, w_uq, g_kv, w_ukv, w_mla_out, w_out, g_post, loss_target, m_w_ada, m_b_ada, m_g_pre, m_w_in, m_conv_w, m_w_conv_out, m_g_q, m_w_uq, m_g_kv, m_w_ukv, m_w_mla_out, m_w_out, m_g_post, v_w_ada, v_b_ada, v_g_pre, v_w_in, v_conv_w, v_w_conv_out, v_g_q, v_w_uq, v_g_kv, v_w_ukv, v_w_mla_out, v_w_out, v_g_post):
    nb, seq, _ = x.shape
    t = nb * seq
    mx, my, mc = lax.axis_index("x"), lax.axis_index("y"), lax.axis_index("c")
    me = 4 * mx + 2 * my + mc
    co = {"x": mx, "y": my, "c": mc}

    x2 = x.reshape(t, D)
    tgt2 = loss_target.reshape(t, D)
    pos2 = positions.reshape(t, 1)

    packed = jnp.concatenate([c.reshape(2 * D // 128, 128), _rows128(conv_w[0], 8)], axis=0)
    gath = small_allgather(packed, "gather_cond")
    c_all = gath[:, :16].reshape(8 * nb, D)
    conv_full = gath[:, 16:19].reshape(8, 3, 128).transpose(1, 0, 2).reshape(3, D)
    conv_full8 = jnp.pad(conv_full, ((0, 5), (0, 0)))
    ada_cols = w_ada.shape[2]
    b_cols = lax.dynamic_slice(b_ada, (0, me * ada_cols), (1, ada_cols))
    mod_part = ada_fwd(c_all, w_ada[0], b_cols)
    mod_g = small_allgather(mod_part.reshape(8 * nb * ada_cols // 128, 128), "gather_mod")
    mod_all = mod_g.reshape(8, 8 * nb, ada_cols).transpose(1, 0, 2).reshape(8 * nb, 8 * ada_cols)
    mod = lax.dynamic_slice(mod_all, (me * nb, 0), (nb, 3 * D))
    shift = mod[:, 0:D].reshape(nb, 1, D)
    scale = mod[:, D:2 * D].reshape(nb, 1, D)
    gate = mod[:, 2 * D:3 * D].reshape(nb, 1, D)

    wt = w_in[0].T.astype(BF16)
    lo = lax.bitcast_convert_type(wt[:, :D // 2], jnp.uint16).astype(jnp.uint32)
    hi = lax.bitcast_convert_type(wt[:, D // 2:], jnp.uint16).astype(jnp.uint32)
    wt_bits = lax.bitcast_convert_type(lo | (hi << 16), F32)
    wt_bits, mod = lax.optimization_barrier((wt_bits, mod))
    shift = mod[:, 0:D].reshape(nb, 1, D)
    scale = mod[:, D:2 * D].reshape(nb, 1, D)
    gate = mod[:, 2 * D:3 * D].reshape(nb, 1, D)
    q4 = D // 4
    r3rd = wt_bits.shape[0] // 3
    plan = [(0, (k * r3rd, r3rd), (g * q4, q4), (("x", "y"), ("y", "x"))[g]) for k in range(3) for g in range(2)]
    gw = allgather_big([wt_bits], plan, "gather_w_in")
    late = [w_conv_out[0].astype(BF16), w_mla_out[0].astype(BF16), w_out[0].astype(BF16),
            jnp.pad(w_uq[0].T.astype(BF16), ((0, DQK - 192), (0, 0))), w_ukv[0].T.astype(BF16)]
    gw0, late = lax.optimization_barrier((gw[0], late))
    plane_state, plane_token = plane_swap_start(gw0, "gather_w_in_d2d_start")
    late_state, late_token = gather_start(late, "gather_late_start")

    inv_freq = ROPE_THETA ** (-jnp.arange(0, ROPE, 2, dtype=F32) / ROPE)
    invf = jnp.concatenate([inv_freq, inv_freq, jnp.zeros((128 - ROPE,), F32)]).reshape(1, 128)
    lane = np.arange(128)
    tabs = (invf,
            jnp.asarray(np.where(lane < HALF, -1.0, 0.0).reshape(1, 128), F32),
            jnp.asarray(np.where((lane >= HALF) & (lane < ROPE), 1.0, 0.0).reshape(1, 128), F32))

    h = prenorm_fwd(x2, scale, shift, g_pre, seq, plane_token)
    wt_bits_all = plane_swap_wait(plane_state, h, "gather_w_in_d2d_wait").reshape(N_IN, D // 2)
    proj, wt_p = proj_matmul(h, wt_bits_all, late_token)
    y = conv_fwd(proj, conv_full8, seq)
    gl = gather_wait(late_state, y, "gather_late_wait")
    wco = gl[0].reshape(D, D)
    wmo = gl[1].reshape(D, D)
    wout = gl[2].reshape(D, D)
    wuq_p = gl[3].reshape(H * DQK, QL)
    wukv = gl[4].reshape(H * 256, KVL)
    q_rot, k_cat, kv, qn, kvn = mla_prep_fwd(proj, pos2, g_q, g_kv, wuq_p, wukv, tabs)
    attn, lse = flash_fwd(q_rot, k_cat, kv, nb, seq)
    o, ya, yb, m, do2, dout, dgate, dg_post, loss_part = tail_fwd(
        y, attn, proj, x2, tgt2, gate, g_post, wco, wmo, wout, seq)

    dproj, dya, dyb, dattn, dy = tail_bwd(do2, proj, ya, yb, attn, wout, wmo, wco)
    g_wout = grad_matmul(m, do2, "grad_w_square")
    g_wmo = grad_matmul(o, dyb, "grad_w_square")
    g_wco = grad_matmul(y, dya, "grad_w_square")
    sc1, sc1_tok = scatter_start([g_wco, g_wmo, g_wout], "scatter_out_grads_start")
    dproj, dconv = conv_bwd(dproj, proj, dy, conv_full8, seq)
    dq_rot, dk, dv = flash_bwd(q_rot, k_cat, kv, attn, dattn, lse, nb, seq, sc1_tok)
    dproj, dq, dkv, dg_q, dg_kv = mla_prep_bwd(dproj, proj, dq_rot, dk, dv, pos2, g_q, g_kv, wuq_p, wukv, tabs)
    g_wuq_t = grad_matmul(dq, qn, "grad_w_uq")
    g_wukv_t = grad_matmul(dkv, kvn, "grad_w_ukv")
    sc2, sc2_tok = scatter_start([g_wuq_t, g_wukv_t], "scatter_mla_grads_start")
    g_win_p = win_grad_matmul(h, dproj, sc2_tok)

    g_wt = g_win_p.reshape(2, 2, 2, N_IN // 8, D)
    ords = [("c", "y", "x"), ("c", "x", "y")]
    hc = D // 2
    win_shape = (2, 2, N_IN // 8, hc)
    pick_w = lambda col: (lambda ref, cc: ref.at[:, :, 1 - cc["c"], :, pl.ds(col * hc, hc)])
    which1 = [0, 0]
    picks1 = [pick_w(0), pick_w(1)]
    st1, tok1 = swap_start([g_wt], which1, ["c"] * 2, picks1, [win_shape] * 2, "rs_c_start")
    assert nb == 2
    dh0 = dh_matmul(dproj, wt_p, tok1, seq, 0)
    (g_wt,), r1 = swap_wait(st1, dh0, which1, ["c"] * 2, picks1, "rs_c_wait")
    sel_xyc = jnp.stack([mx, my, mc]).astype(jnp.int32)
    sel2 = [jnp.stack([co[o[2]]]).astype(jnp.int32) for o in ords]
    first = [rs_win_add_first(g_wt, r1[0], sel_xyc, 1, 0, "rs_add_first_0"),
             rs_win_add_first(g_wt, r1[1], sel_xyc, 0, 1, "rs_add_first_1")]
    keep1, send1 = zip(*first)
    all4 = [0, 1]
    none4 = [None] * 2
    axes2 = [o[1] for o in ords]
    st2, tok2 = swap_start(list(send1), all4, axes2, none4, [s.shape for s in send1], "rs_ici1_start")

    dh1 = dh_matmul(dproj, wt_p, tok2, seq, 1)
    gx0, dsh0, dsc0, dgp0 = prenorm_bwd(dh0, x2, dout, scale, g_pre, seq, tok2, 0, None)
    _, r2 = swap_wait(st2, (gx0, dh1), all4, axes2, none4, "rs_ici1_wait")
    keep2, send2 = zip(*[rs_add_second(keep1[a], r2[a], sel2[a], "rs_add_second") for a in range(2)])
    axes3 = [o[2] for o in ords]
    st3, tok3 = swap_start(list(send2), all4, axes3, none4, [s.shape for s in send2], "rs_ici2_start")
    grad_x2, dsh1, dsc1, dgp1 = prenorm_bwd(dh1, x2, dout, scale, g_pre, seq, tok3, 1, gx0)
    dshift = jnp.stack([dsh0, dsh1])
    dscale = jnp.stack([dsc0, dsc1])
    dg_pre = dgp0 + dgp1

    dmod = jnp.concatenate([dshift, dscale, dgate], axis=2).reshape(nb * 3 * D // 128, 128)
    small = jnp.concatenate([
        dmod, _rows128(dg_pre, 8), _rows128(dg_post, 8), _rows128(dg_q, 8), _rows128(dg_kv, 8),
        dconv[0:3].reshape(24, 128), _rows128(loss_part, 8)], axis=0)
    small_g = small_allgather(small, "gather_small_grads")
    sums = slot_sum(small_g)
    dmod_all = small_g[:, 0:48].reshape(8 * nb, 3 * D)
    g_bada = (sums[0:24] + sums[24:48]).reshape(1, 3 * D)
    g_gpre = sums[48:56].reshape(1, D)
    g_gpost = sums[56:64].reshape(1, D)
    g_gq = sums[64:67].reshape(1, QL)
    g_gkv = sums[72:74].reshape(1, KVL)
    g_conv_full = sums[80:104].reshape(3, D)
    loss = sums[104, 0]
    g_conv = lax.dynamic_slice(g_conv_full, (0, me * 128), (3, 128))
    dmod_cols = lax.dynamic_slice(dmod_all, (0, me * ada_cols), (8 * nb, ada_cols))
    g_wada = ada_bwd(c_all, dmod_cols)

    res = {}
    res["w_ada"] = [o_[None] for o_ in adamw(w_ada[0], m_w_ada[0], v_w_ada[0], g_wada, None, "adamw_w_ada", tok3)]

    def pack(b_, gp_, gpo_, gq_, gkv_, cw_):
        return jnp.concatenate([_rows128(b_, 24), _rows128(gp_, 8), _rows128(gpo_, 8), _rows128(gq_, 8),
                                _rows128(gkv_, 8), _rows128(cw_, 8)], axis=0)

    sw = pack(b_ada, g_pre, g_post, g_q, g_kv, conv_w)
    sm = pack(m_b_ada, m_g_pre, m_g_post, m_g_q, m_g_kv, m_conv_w)
    sv = pack(v_b_ada, v_g_pre, v_g_post, v_g_q, v_g_kv, v_conv_w)
    sg = pack(g_bada, g_gpre, g_gpost, g_gq, g_gkv, g_conv)
    small_out = adamw(sw, sm, sv, sg, None, "adamw_small", tok3)

    _, r3 = swap_wait(st3, small_out[0], all4, axes3, none4, "rs_ici2_wait")

    (g_wco, g_wmo, g_wout), (l_wco, l_wmo, l_wout) = scatter_wait(sc1, small_out[1], "scatter_out_grads_wait")
    (g_wuq_t, g_wukv_t), (l_wuq, l_wukv) = scatter_wait(sc2, small_out[2], "scatter_mla_grads_wait")

    res["w_in"] = [o_.T[None] for o_ in adamw_win(w_in[0].T, m_w_in[0].T, v_w_in[0].T,
                                                  keep2[0], r3[0], keep2[1], r3[1])]
    me1 = me.reshape(1).astype(jnp.int32)
    res["w_uq"] = [o_.T[None] for o_ in adamw_scattered(
        w_uq[0].T, m_w_uq[0].T, v_w_uq[0].T, g_wuq_t, l_wuq, me1, 64, "adamw_w_uq")]
    res["w_ukv"] = [o_[None] for o_ in adamw_scattered(
        w_ukv[0], m_w_ukv[0], v_w_ukv[0], g_wukv_t, l_wukv, me1, KVL, "adamw_w_ukv", transpose=True)]
    for nm, wv, mv, vv, gg, ll in (("w_conv_out", w_conv_out, m_w_conv_out, v_w_conv_out, g_wco, l_wco),
                                   ("w_mla_out", w_mla_out, m_w_mla_out, v_w_mla_out, g_wmo, l_wmo),
                                   ("w_out", w_out, m_w_out, v_w_out, g_wout, l_wout)):
        res[nm] = [o_[None] for o_ in adamw_scattered(wv[0], mv[0], vv[0], gg, ll, me1, 128, "adamw_square")]

    def unpack(a):
        return {"b_ada": a[0:24].reshape(1, 3 * D), "g_pre": a[24:32].reshape(1, D),
                "g_post": a[32:40].reshape(1, D), "g_q": a[40:43].reshape(1, QL),
                "g_kv": a[48:50].reshape(1, KVL), "conv_w": a[56:59].reshape(-1)[:3 * 128].reshape(1, 3, 128)}

    for nm in ("b_ada", "g_pre", "g_post", "g_q", "g_kv", "conv_w"):
        res[nm] = [unpack(a)[nm] for a in small_out]

    order = ["w_ada", "b_ada", "g_pre", "w_in", "conv_w", "w_conv_out", "g_q", "w_uq", "g_kv", "w_ukv",
             "w_mla_out", "w_out", "g_post"]
    out = [loss, grad_x2.reshape(nb, seq, D)]
    for k_ in range(4):
        out += [res[nm][k_] for nm in order]
    return tuple(out)
```

```python
import functools

import numpy as np
import jax
import jax.numpy as jnp
from jax import lax
from jax.experimental import pallas as pl
from jax.experimental.pallas import tpu as pltpu

F32 = jnp.float32
BF16 = jnp.bfloat16
MESH = pl.DeviceIdType.MESH

D = 1024
H = 8
QL = 384
KVL = 256
ROPE = 64
HALF = ROPE // 2
DQK = 256
DV = 128
NSEG = 8
NP = NSEG * D
EPS = 1e-6
ROPE_THETA = 10000.0
SM_SCALE = (128 + ROPE) ** -0.5
LOG2E = 1.4426950408889634
LN2 = 0.6931471805599453
FLASH_TQ = 512

SEG_BZ, SEG_GA, SEG_GB, SEG_LAT, SEG_V = 0, 1, 2, 3, 4

ADAM_LR = 0.001
ADAM_B1 = 0.9
ADAM_B2 = 0.999
ADAM_EPS = 1e-08
ADAM_WD = 0.01
ADAM_STEP = 10

VMEM_LIMIT = 56 * 1024 * 1024


def _params(sem=None, vmem=VMEM_LIMIT):
    kw = dict(vmem_limit_bytes=vmem)
    if sem is not None:
        kw["dimension_semantics"] = sem
    return pltpu.CompilerParams(**kw)


def _sig(v):
    return 0.5 * jnp.tanh(0.5 * v) + 0.5


def _dot(a, b):
    return jnp.dot(a, b, preferred_element_type=F32)


def _dot_nt(a, b):
    return lax.dot_general(a, b, (((1,), (1,)), ((), ())), preferred_element_type=F32)


def _dot_tn(a, b):
    return lax.dot_general(a, b, (((0,), (0,)), ((), ())), preferred_element_type=F32)


_AXIS_POS = {"x": 0, "y": 1, "c": 2}


def _coords():
    return lax.axis_index("x"), lax.axis_index("y"), lax.axis_index("c")


def _partner(axis):
    p = list(_coords())
    p[_AXIS_POS[axis]] = 1 - p[_AXIS_POS[axis]]
    return tuple(p)


def small_allgather(v, name):
    rows = v.shape[0]

    def body(v_ref, out_ref, send_sems, recv_sems):
        x, y, c = _coords()
        me = 4 * x + 2 * y + c
        out_ref[me] = v_ref[...]
        copies = []
        for k in range(1, 8):
            peer = (1 - x if k & 4 else x, 1 - y if k & 2 else y, 1 - c if k & 1 else c)
            cp = pltpu.make_async_remote_copy(
                src_ref=v_ref, dst_ref=out_ref.at[me],
                send_sem=send_sems.at[k - 1], recv_sem=recv_sems.at[k - 1],
                device_id=peer, device_id_type=MESH)
            cp.start()
            copies.append(cp)
        for cp in copies:
            cp.wait()

    return pl.pallas_call(
        body, name=name,
        out_shape=jax.ShapeDtypeStruct((8, rows, 128), F32),
        in_specs=[pl.BlockSpec(memory_space=pltpu.VMEM)],
        out_specs=pl.BlockSpec(memory_space=pltpu.VMEM),
        scratch_shapes=[pltpu.SemaphoreType.DMA((7,)), pltpu.SemaphoreType.DMA((7,))],
    )(v)


def _own_block_placed(s):
    x, y, c = _coords()
    return lax.dynamic_update_slice(lax.empty((2, 2, 2) + s.shape, s.dtype), s[None, None, None],
                                    (x, y, c) + (0,) * s.ndim)


def allgather_big(arrs, plan, name):
    n = len(arrs)
    m = len(plan)
    nst = len(plan[0][3])

    def body(*refs):
        ins, outs = refs[n:2 * n], refs[2 * n:3 * n]
        send_sems, recv_sems = refs[3 * n:]
        x, y, c = _coords()
        co = {"x": x, "y": y, "c": c}

        def window(ref, lead, rows, cols):
            win = tuple(slice(None) if w is None else pl.ds(w[0], w[1]) for w in (rows, cols))
            return ref.at[tuple(lead) + win]

        def held(e, free):
            i, rows, cols, _ = plan[e]
            lead = [slice(None) if ax in free else co[ax] for ax in ("x", "y", "c")]
            return window(outs[i], lead, rows, cols)

        def rcopy(e, stage, src, dst, axis):
            return pltpu.make_async_remote_copy(
                src_ref=src, dst_ref=dst,
                send_sem=send_sems.at[e, stage], recv_sem=recv_sems.at[e, stage],
                device_id=_partner(axis), device_id_type=MESH)

        stages = [[] for _ in range(nst)]
        for e, (i, rows, cols, order) in enumerate(plan):
            cp = rcopy(e, 0, window(ins[i], [], rows, cols), held(e, ()), order[0])
            cp.start()
            stages[0].append(cp)
        for s in range(1, nst):
            for e, (i, rows, cols, order) in enumerate(plan):
                stages[s - 1][e].wait_recv()
                blk = held(e, order[:s])
                cp = rcopy(e, s, blk, blk, order[s])
                cp.start()
                stages[s].append(cp)
        for e in range(m):
            stages[nst - 1][e].wait_recv()
        for e in range(m):
            for s in range(nst):
                stages[s][e].wait_send()

    any_spec = pl.BlockSpec(memory_space=pl.ANY)
    lands = [_own_block_placed(a) for a in arrs]
    return pl.pallas_call(
        body, name=name,
        out_shape=[jax.ShapeDtypeStruct(l.shape, l.dtype) for l in lands],
        in_specs=[any_spec] * (2 * n),
        out_specs=[any_spec] * n,
        input_output_aliases={i: i for i in range(n)},
        scratch_shapes=[pltpu.SemaphoreType.DMA((m, nst)), pltpu.SemaphoreType.DMA((m, nst))],
    )(*lands, *arrs)


def exchange(arrs, axes, picks, out_shapes, name):
    n = len(arrs)

    def body(*refs):
        ins, outs = refs[:n], refs[n:2 * n]
        send_sems, recv_sems = refs[2 * n:]
        x, y, c = _coords()
        co = {"x": x, "y": y, "c": c}
        copies = []
        for a in range(n):
            src = ins[a] if picks[a] is None else picks[a](ins[a], co)
            cp = pltpu.make_async_remote_copy(
                src_ref=src, dst_ref=outs[a],
                send_sem=send_sems.at[a], recv_sem=recv_sems.at[a],
                device_id=_partner(axes[a]), device_id_type=MESH)
            cp.start()
            copies.append(cp)
        for cp in copies:
            cp.wait()

    any_spec = pl.BlockSpec(memory_space=pl.ANY)
    return pl.pallas_call(
        body, name=name,
        out_shape=[jax.ShapeDtypeStruct(s, a.dtype) for s, a in zip(out_shapes, arrs)],
        in_specs=[any_spec] * n,
        out_specs=[any_spec] * n,
        scratch_shapes=[pltpu.SemaphoreType.DMA((n,)), pltpu.SemaphoreType.DMA((n,))],
    )(*arrs)


_HBM = pl.BlockSpec(memory_space=pltpu.HBM)
_SEM = pl.BlockSpec(memory_space=pltpu.SEMAPHORE)


def _swap_copies(srcs, lands, send_sems, recv_sems, axes, picks):
    x, y, c = _coords()
    co = {"x": x, "y": y, "c": c}
    return [pltpu.make_async_remote_copy(
        src_ref=srcs[a] if picks[a] is None else picks[a](srcs[a], co), dst_ref=lands[a],
        send_sem=send_sems.at[a], recv_sem=recv_sems.at[a],
        device_id=_partner(axes[a]), device_id_type=MESH) for a in range(len(srcs))]


def swap_start(arrs, which, axes, picks, out_shapes, name):
    ns, n = len(arrs), len(which)

    def body(*refs):
        srcs, lands = refs[:ns], refs[ns:ns + n]
        send_sems, recv_sems = refs[ns + n:ns + n + 2]
        token = refs[-1]
        for cp in _swap_copies([srcs[i] for i in which], lands, send_sems, recv_sems, axes, picks):
            cp.start()
        token[...] = jnp.zeros_like(token)

    lands = [lax.empty(s, arrs[i].dtype) for s, i in zip(out_shapes, which)]
    ops = [pltpu.with_memory_space_constraint(a, pltpu.HBM) for a in list(arrs) + lands]
    out = pl.pallas_call(
        body, name=name,
        out_shape=[pltpu.SemaphoreType.DMA((n,)), pltpu.SemaphoreType.DMA((n,))]
        + [pltpu.HBM(o.shape, o.dtype) for o in ops] + [jax.ShapeDtypeStruct((8, 128), F32)],
        in_specs=[_HBM] * (ns + n),
        out_specs=[_SEM, _SEM] + [_HBM] * (ns + n) + [pl.BlockSpec(memory_space=pltpu.VMEM)],
        input_output_aliases={i: 2 + i for i in range(ns + n)},
        compiler_params=pltpu.CompilerParams(has_side_effects=pltpu.SideEffectType.DATAFLOW_SIDE_EFFECTING),
    )(*ops)
    return out[:-1], out[-1]


def swap_wait(state, after, which, axes, picks, name):
    n = len(which)
    ns = len(state) - 2 - n

    def body(*refs):
        srcs, lands = refs[:ns], refs[ns:ns + n]
        send_sems, recv_sems = refs[ns + n:ns + n + 2]
        for cp in _swap_copies([srcs[i] for i in which], lands, send_sems, recv_sems, axes, picks):
            cp.wait_send()
            cp.wait_recv()

    thru = list(state[2:])
    after = list(after) if isinstance(after, (list, tuple)) else [after]
    out = pl.pallas_call(
        body, name=name,
        out_shape=[pltpu.HBM(o.shape, o.dtype) for o in thru],
        in_specs=[_HBM] * (ns + n) + [_SEM, _SEM] + [pl.BlockSpec(memory_space=pl.ANY)] * len(after),
        out_specs=[_HBM] * (ns + n),
        input_output_aliases={i: i for i in range(ns + n)},
        compiler_params=pltpu.CompilerParams(has_side_effects=pltpu.SideEffectType.DATAFLOW_SIDE_EFFECTING),
    )(*thru, state[0], state[1], *after)
    return out[:ns], out[ns:]


def _gather_copies(shards, lands, send_sems, recv_sems):
    x, y, c = _coords()
    copies = []
    for a in range(len(shards)):
        for k in range(1, 8):
            peer = (1 - x if k & 4 else x, 1 - y if k & 2 else y, 1 - c if k & 1 else c)
            copies.append(pltpu.make_async_remote_copy(
                src_ref=shards[a], dst_ref=lands[a].at[x, y, c],
                send_sem=send_sems.at[7 * a + k - 1], recv_sem=recv_sems.at[7 * a + k - 1],
                device_id=peer, device_id_type=MESH))
    return copies


def gather_start(shards, name):
    n = len(shards)
    x, y, c = _coords()

    def body(*refs):
        srcs, lands = refs[:n], refs[n:2 * n]
        send_sems, recv_sems = refs[2 * n:2 * n + 2]
        token = refs[-1]
        for cp in _gather_copies(srcs, lands, send_sems, recv_sems):
            cp.start()
        token[...] = jnp.zeros_like(token)

    lands = [_own_block_placed(s) for s in shards]
    ops = [pltpu.with_memory_space_constraint(a, pltpu.HBM) for a in list(shards) + lands]
    out = pl.pallas_call(
        body, name=name,
        out_shape=[pltpu.SemaphoreType.DMA((7 * n,)), pltpu.SemaphoreType.DMA((7 * n,))]
        + [pltpu.HBM(o.shape, o.dtype) for o in ops] + [jax.ShapeDtypeStruct((8, 128), F32)],
        in_specs=[_HBM] * (2 * n),
        out_specs=[_SEM, _SEM] + [_HBM] * (2 * n) + [pl.BlockSpec(memory_space=pltpu.VMEM)],
        input_output_aliases={i: 2 + i for i in range(2 * n)},
        compiler_params=pltpu.CompilerParams(has_side_effects=pltpu.SideEffectType.DATAFLOW_SIDE_EFFECTING),
    )(*ops)
    return out[:-1], out[-1]


def gather_wait(state, after, name):
    n = (len(state) - 2) // 2

    def body(*refs):
        srcs, lands = refs[:n], refs[n:2 * n]
        send_sems, recv_sems = refs[2 * n:2 * n + 2]
        for cp in _gather_copies(srcs, lands, send_sems, recv_sems):
            cp.wait_send()
            cp.wait_recv()

    thru = list(state[2:])
    out = pl.pallas_call(
        body, name=name,
        out_shape=[pltpu.HBM(o.shape, o.dtype) for o in thru],
        in_specs=[_HBM] * (2 * n) + [_SEM, _SEM, pl.BlockSpec(memory_space=pl.ANY)],
        out_specs=[_HBM] * (2 * n),
        input_output_aliases={i: i for i in range(2 * n)},
        compiler_params=pltpu.CompilerParams(has_side_effects=pltpu.SideEffectType.DATAFLOW_SIDE_EFFECTING),
    )(*thru, state[0], state[1], after)
    return out[n:]


def _scatter_copies(grads, lands, send_sems, recv_sems):
    x, y, c = _coords()
    me = 4 * x + 2 * y + c
    copies = []
    for a in range(len(grads)):
        r = grads[a].shape[0] // 8
        for k in range(1, 8):
            px, py, pc = (1 - x if k & 4 else x, 1 - y if k & 2 else y, 1 - c if k & 1 else c)
            rows = pl.ds(pl.multiple_of((4 * px + 2 * py + pc) * r, r), r)
            copies.append(pltpu.make_async_remote_copy(
                src_ref=grads[a].at[rows], dst_ref=lands[a].at[me],
                send_sem=send_sems.at[7 * a + k - 1], recv_sem=recv_sems.at[7 * a + k - 1],
                device_id=(px, py, pc), device_id_type=MESH))
    return copies


def scatter_start(grads, name):
    n = len(grads)

    def body(*refs):
        srcs, lands = refs[:n], refs[n:2 * n]
        send_sems, recv_sems = refs[2 * n:2 * n + 2]
        token = refs[-1]
        for cp in _scatter_copies(srcs, lands, send_sems, recv_sems):
            cp.start()
        token[...] = jnp.zeros_like(token)

    lands = [jnp.zeros((8, g.shape[0] // 8, g.shape[1]), g.dtype) for g in grads]
    ops = [pltpu.with_memory_space_constraint(a, pltpu.HBM) for a in list(grads) + lands]
    out = pl.pallas_call(
        body, name=name,
        out_shape=[pltpu.SemaphoreType.DMA((7 * n,)), pltpu.SemaphoreType.DMA((7 * n,))]
        + [pltpu.HBM(o.shape, o.dtype) for o in ops] + [jax.ShapeDtypeStruct((8, 128), F32)],
        in_specs=[_HBM] * (2 * n),
        out_specs=[_SEM, _SEM] + [_HBM] * (2 * n) + [pl.BlockSpec(memory_space=pltpu.VMEM)],
        input_output_aliases={i: 2 + i for i in range(2 * n)},
        compiler_params=pltpu.CompilerParams(has_side_effects=pltpu.SideEffectType.DATAFLOW_SIDE_EFFECTING),
    )(*ops)
    return out[:-1], out[-1]


def scatter_wait(state, after, name):
    n = (len(state) - 2) // 2

    def body(*refs):
        srcs, lands = refs[:n], refs[n:2 * n]
        send_sems, recv_sems = refs[2 * n:2 * n + 2]
        for cp in _scatter_copies(srcs, lands, send_sems, recv_sems):
            cp.wait_send()
            cp.wait_recv()

    thru = list(state[2:])
    after = list(after) if isinstance(after, (list, tuple)) else [after]
    out = pl.pallas_call(
        body, name=name,
        out_shape=[pltpu.HBM(o.shape, o.dtype) for o in thru],
        in_specs=[_HBM] * (2 * n) + [_SEM, _SEM] + [pl.BlockSpec(memory_space=pl.ANY)] * len(after),
        out_specs=[_HBM] * (2 * n),
        input_output_aliases={i: i for i in range(2 * n)},
        compiler_params=pltpu.CompilerParams(has_side_effects=pltpu.SideEffectType.DATAFLOW_SIDE_EFFECTING),
    )(*thru, state[0], state[1], *after)
    return out[:n], out[n:]


def rs_win_add_first(g, r, sel, next_dim, col, name):
    rows, cols = r.shape[2:]

    def body(sel_ref, gk_ref, rk_ref, gs_ref, rs_ref, keep_ref, send_ref):
        keep_ref[...] = gk_ref[...] + rk_ref[...]
        send_ref[...] = (gs_ref[...] + rs_ref[...]).astype(BF16)

    def g_map(flip):
        def f(j, s):
            nxt = 1 - s[next_dim] if flip else s[next_dim]
            return (nxt, j, s[2], 0, col) if next_dim == 0 else (j, nxt, s[2], 0, col)
        return f

    def r_map(flip):
        def f(j, s):
            nxt = 1 - s[next_dim] if flip else s[next_dim]
            return (nxt, j, 0, 0) if next_dim == 0 else (j, nxt, 0, 0)
        return f

    gblk = (None, None, None, rows, cols)
    rblk = (None, None, rows, cols)
    oblk = (None, rows, cols)
    return pl.pallas_call(
        body, name=name,
        grid_spec=pltpu.PrefetchScalarGridSpec(
            num_scalar_prefetch=1, grid=(2,),
            in_specs=[pl.BlockSpec(gblk, g_map(False)), pl.BlockSpec(rblk, r_map(False)),
                      pl.BlockSpec(gblk, g_map(True)), pl.BlockSpec(rblk, r_map(True))],
            out_specs=[pl.BlockSpec(oblk, lambda j, s: (j, 0, 0)),
                       pl.BlockSpec(oblk, lambda j, s: (j, 0, 0))]),
        out_shape=[jax.ShapeDtypeStruct((2, rows, cols), F32),
                   jax.ShapeDtypeStruct((2, rows, cols), BF16)],
        compiler_params=_params(),
    )(sel, g, r, g, r)


def rs_add_second(k, r, sel, name):
    _, rows, cols = k.shape
    tr = rows // 2 if rows % 32 == 0 else rows
    nt = rows // tr

    def body(sel_ref, kk_ref, rk_ref, ks_ref, rs_ref, keep_ref, send_ref):
        keep_ref[...] = kk_ref[...] + rk_ref[...].astype(F32)
        send_ref[...] = (ks_ref[...] + rs_ref[...].astype(F32)).astype(BF16)

    blk = (None, tr, cols)
    oblk = (tr, cols)
    return pl.pallas_call(
        body, name=name,
        grid_spec=pltpu.PrefetchScalarGridSpec(
            num_scalar_prefetch=1, grid=(nt,),
            in_specs=[
                pl.BlockSpec(blk, lambda i, s: (s[0], i, 0)),
                pl.BlockSpec(blk, lambda i, s: (s[0], i, 0)),
                pl.BlockSpec(blk, lambda i, s: (1 - s[0], i, 0)),
                pl.BlockSpec(blk, lambda i, s: (1 - s[0], i, 0)),
            ],
            out_specs=[pl.BlockSpec(oblk, lambda i, s: (i, 0)),
                       pl.BlockSpec(oblk, lambda i, s: (i, 0))]),
        out_shape=[jax.ShapeDtypeStruct((rows, cols), F32),
                   jax.ShapeDtypeStruct((rows, cols), BF16)],
        compiler_params=_params(),
    )(sel, k, r, k, r)


SEG_ROWS = (4800, 5824, 6848, 4096, 0, 1024, 2048, 3072)
LAT_ROWS = QL + KVL + ROPE
N_IN = 7872


def _seg_row(j):
    return pl.multiple_of(jnp.where(j < 3, 4800 + 1024 * j, jnp.where(j == 3, 4096, (j - 4) * 1024)), 8)


def proj_matmul(h, wt_bits, token):
    t = h.shape[0]
    tm = min(2048, t)

    def body(h_ref, w_hbm, tok_ref, o_ref, wt_ref, buf, sems):
        j = pl.program_id(0)
        slot = j % 2

        def fetch(seg, into):
            return pltpu.make_async_copy(w_hbm.at[pl.ds(_seg_row(seg), D)], buf.at[into], sems.at[into])

        @pl.when(pl.program_id(1) == 0)
        def _():
            @pl.when(j == 0)
            def _():
                fetch(j, slot).start()

            fetch(j, slot).wait()

            @pl.when(j + 1 < NSEG)
            def _():
                fetch(j + 1, 1 - slot).start()

            bits = pltpu.bitcast(buf[slot], jnp.uint32)
            row = lax.broadcasted_iota(jnp.int32, (D, D // 2), 0)
            live = jnp.logical_or(j != SEG_LAT, row < LAT_ROWS)
            lo = pltpu.bitcast(bits << 16, F32)
            hi = pltpu.bitcast(bits & jnp.uint32(0xFFFF0000), F32)
            wt_ref[:, :D // 2] = jnp.where(live, lo, 0.0).astype(BF16)
            wt_ref[:, D // 2:] = jnp.where(live, hi, 0.0).astype(BF16)

        o_ref[...] = _dot_nt(h_ref[...], wt_ref[...]).astype(BF16)

    return pl.pallas_call(
        body, name="proj_matmul", grid=(NSEG, t // tm),
        in_specs=[pl.BlockSpec((tm, D), lambda j, i: (i, 0)),
                  pl.BlockSpec(memory_space=pl.ANY),
                  pl.BlockSpec((8, 128), lambda j, i: (0, 0))],
        out_specs=[pl.BlockSpec((None, tm, D), lambda j, i: (j, i, 0)),
                   pl.BlockSpec((D, D), lambda j, i: (j, 0))],
        out_shape=[jax.ShapeDtypeStruct((NSEG, t, D), BF16), jax.ShapeDtypeStruct((NP, D), BF16)],
        scratch_shapes=[pltpu.VMEM((2, D, D // 2), F32), pltpu.SemaphoreType.DMA((2,))],
        compiler_params=_params(("arbitrary", "arbitrary")),
    )(h, wt_bits, token)


def dh_matmul(dproj, wt, token, seq, b):
    tm = min(1024, seq)
    nblk = seq // tm

    per = 2

    def body(b_ref, d_ref, w_ref, tok_ref, o_ref, acc_ref):
        k = pl.program_id(1)

        @pl.when(k == 0)
        def _():
            acc_ref[...] = jnp.zeros_like(acc_ref)

        part = _dot(d_ref[0], w_ref[0:D, :])
        for j in range(1, per):
            part = part + _dot(d_ref[j], w_ref[j * D:(j + 1) * D, :])
        acc_ref[...] += part

        @pl.when(k == NSEG // per - 1)
        def _():
            o_ref[...] = acc_ref[...]

    return pl.pallas_call(
        body, name="dh_matmul",
        grid_spec=pltpu.PrefetchScalarGridSpec(
            num_scalar_prefetch=1, grid=(nblk, NSEG // per),
            in_specs=[pl.BlockSpec((per, tm, D), lambda i, k, s: (k, s[0] * nblk + i, 0)),
                      pl.BlockSpec((per * D, D), lambda i, k, s: (k, 0)),
                      pl.BlockSpec((8, 128), lambda i, k, s: (0, 0))],
            out_specs=pl.BlockSpec((tm, D), lambda i, k, s: (i, 0)),
            scratch_shapes=[pltpu.VMEM((tm, D), F32)]),
        out_shape=jax.ShapeDtypeStruct((seq, D), F32),
        compiler_params=_params(("parallel", "arbitrary")),
    )(jnp.full((1,), b, jnp.int32), dproj, wt, token)


def win_grad_matmul(h, dproj, token):
    t = h.shape[0]
    tk = min(2048, t)
    nk = t // tk

    def body(h_ref, d_ref, tok_ref, o_hbm, acc_ref, sem):
        j = pl.program_id(0)
        k = pl.program_id(1)

        @pl.when(k == 0)
        def _():
            acc_ref[...] = jnp.zeros_like(acc_ref)

        acc_ref[...] += _dot_tn(d_ref[...], h_ref[...])

        @pl.when(jnp.logical_and(k == nk - 1, j != SEG_LAT))
        def _():
            cp = pltpu.make_async_copy(acc_ref, o_hbm.at[pl.ds(_seg_row(j), D)], sem)
            cp.start()
            cp.wait()

        @pl.when(jnp.logical_and(k == nk - 1, j == SEG_LAT))
        def _():
            cp = pltpu.make_async_copy(acc_ref.at[pl.ds(0, LAT_ROWS)],
                                       o_hbm.at[pl.ds(SEG_ROWS[SEG_LAT], LAT_ROWS)], sem)
            cp.start()
            cp.wait()

    return pl.pallas_call(
        body, name="win_grad_matmul", grid=(NSEG, nk),
        in_specs=[pl.BlockSpec((tk, D), lambda j, k: (k, 0)),
                  pl.BlockSpec((None, tk, D), lambda j, k: (j, k, 0)),
                  pl.BlockSpec((8, 128), lambda j, k: (0, 0))],
        out_specs=pl.BlockSpec(memory_space=pl.ANY),
        out_shape=jax.ShapeDtypeStruct((N_IN, D), F32),
        scratch_shapes=[pltpu.VMEM((D, D), F32), pltpu.SemaphoreType.DMA],
        compiler_params=_params(("arbitrary", "arbitrary")),
    )(h, dproj, token)


def grad_matmul(a, b, name):
    t, m = a.shape
    n = b.shape[1]
    tk = min(1024, t)
    nk = t // tk

    def body(a_ref, b_ref, o_ref, acc_ref):
        k = pl.program_id(0)

        @pl.when(k == 0)
        def _():
            acc_ref[...] = jnp.zeros_like(acc_ref)

        acc_ref[...] += _dot_tn(a_ref[...], b_ref[...])

        @pl.when(k == nk - 1)
        def _():
            o_ref[...] = acc_ref[...].astype(BF16)

    return pl.pallas_call(
        body, name=name, grid=(nk,),
        in_specs=[pl.BlockSpec((tk, m), lambda k: (k, 0)),
                  pl.BlockSpec((tk, n), lambda k: (k, 0))],
        out_specs=pl.BlockSpec((m, n), lambda k: (0, 0)),
        out_shape=jax.ShapeDtypeStruct((m, n), BF16),
        scratch_shapes=[pltpu.VMEM((m, n), F32)],
        compiler_params=_params(("arbitrary",)),
    )(a, b)


def ada_fwd(c_all, w_ada, b_cols):
    def body(c_ref, w_ref, b_ref, o_ref):
        o_ref[...] = _dot(c_ref[...].astype(BF16), w_ref[...].astype(BF16)) + b_ref[...]

    return pl.pallas_call(
        body, name="ada_fwd",
        out_shape=jax.ShapeDtypeStruct((c_all.shape[0], w_ada.shape[1]), F32),
        compiler_params=_params(),
    )(c_all, w_ada, b_cols)


def ada_bwd(c_all, dmod_cols):
    def body(c_ref, d_ref, o_ref):
        o_ref[...] = _dot_tn(c_ref[...].astype(BF16), d_ref[...].astype(BF16))

    return pl.pallas_call(
        body, name="ada_bwd",
        out_shape=jax.ShapeDtypeStruct((c_all.shape[1], dmod_cols.shape[1]), F32),
        compiler_params=_params(),
    )(c_all, dmod_cols)


def slot_sum(g):
    def body(g_ref, o_ref):
        acc = g_ref[0]
        for s in range(1, 8):
            acc = acc + g_ref[s]
        o_ref[...] = acc

    return pl.pallas_call(
        body, name="slot_sum",
        out_shape=jax.ShapeDtypeStruct(g.shape[1:], F32),
    )(g)


def prenorm_fwd(x2, scale, shift, g_pre, seq):
    t = x2.shape[0]
    tm = min(512, seq)
    tpb = seq // tm

    def body(x_ref, sc_ref, sh_ref, g_ref, h_ref):
        xv = x_ref[...]
        r = lax.rsqrt(jnp.mean(xv * xv, axis=-1, keepdims=True) + EPS)
        hv = (xv * r * g_ref[...]) * (1.0 + sc_ref[...]) + sh_ref[...]
        h_ref[...] = hv.astype(BF16)

    per_batch = pl.BlockSpec((None, 1, D), lambda i: (i // tpb, 0, 0))
    return pl.pallas_call(
        body, name="prenorm_fwd", grid=(t // tm,),
        in_specs=[pl.BlockSpec((tm, D), lambda i: (i, 0)), per_batch, per_batch,
                  pl.BlockSpec((1, D), lambda i: (0, 0))],
        out_specs=pl.BlockSpec((tm, D), lambda i: (i, 0)),
        out_shape=jax.ShapeDtypeStruct((t, D), BF16),
        compiler_params=_params(("parallel",)),
    )(x2, scale, shift, g_pre)


def prenorm_bwd(dh, x2, dout, scale, g_pre, seq, token, b, gx_prev):
    t = x2.shape[0]
    tm = min(512, seq)
    tpb = seq // tm
    if gx_prev is None:
        gx_prev = lax.empty((t, D), F32)

    def body(b_ref, dh_ref, x_ref, do_ref, sc_ref, g_ref, tok_ref, gxp_ref, gx_ref, dsh_ref, dsc_ref, dg_ref):
        i = pl.program_id(0)
        xv = x_ref[...]
        dhv = dh_ref[...]
        g = g_ref[...]
        r = lax.rsqrt(jnp.mean(xv * xv, axis=-1, keepdims=True) + EPS)
        nrm = xv * r
        dxn = dhv * (1.0 + sc_ref[...])
        dn = dxn * g
        dx = r * (dn - nrm * jnp.mean(dn * nrm, axis=-1, keepdims=True))
        gx_ref[...] = dx + do_ref[...]

        @pl.when(i == 0)
        def _():
            dsh_ref[...] = jnp.zeros_like(dsh_ref)
            dsc_ref[...] = jnp.zeros_like(dsc_ref)
            dg_ref[...] = jnp.zeros_like(dg_ref)

        dsh_ref[...] += jnp.sum(dhv, axis=0, keepdims=True)
        dsc_ref[...] += jnp.sum(dhv * (nrm * g), axis=0, keepdims=True)
        dg_ref[...] += jnp.sum(dxn * nrm, axis=0, keepdims=True)

    row = pl.BlockSpec((tm, D), lambda i, s: (i, 0))
    grow = pl.BlockSpec((tm, D), lambda i, s: (s[0] * tpb + i, 0))
    per_batch = pl.BlockSpec((None, 1, D), lambda i, s: (s[0], 0, 0))
    vec = pl.BlockSpec((1, D), lambda i, s: (0, 0))
    return pl.pallas_call(
        body, name="prenorm_bwd",
        grid_spec=pltpu.PrefetchScalarGridSpec(
            num_scalar_prefetch=1, grid=(tpb,),
            in_specs=[row, grow, grow, per_batch, vec, pl.BlockSpec((8, 128), lambda i, s: (0, 0)),
                      pl.BlockSpec(memory_space=pl.ANY)],
            out_specs=[grow, vec, vec, vec]),
        out_shape=[jax.ShapeDtypeStruct((t, D), F32), jax.ShapeDtypeStruct((1, D), F32),
                   jax.ShapeDtypeStruct((1, D), F32), jax.ShapeDtypeStruct((1, D), F32)],
        input_output_aliases={7: 0},
        compiler_params=_params(("arbitrary",)),
    )(jnp.full((1,), b, jnp.int32), dh, x2, dout, scale, g_pre, token, gx_prev)


CONV_TC = 128


def _shift_down(u, k, rows):
    idx = lax.broadcasted_iota(jnp.int32, u.shape, 0)
    return jnp.where(idx >= k, pltpu.roll(u, k, 0), 0.0)


def _shift_up(u, k, rows):
    idx = lax.broadcasted_iota(jnp.int32, u.shape, 0)
    return jnp.where(idx < rows - k, pltpu.roll(u, rows - k, 0), 0.0)


def conv_fwd(proj, conv_w, seq):
    t = proj.shape[1]
    nb = t // seq

    def body(p_ref, w_ref, y_ref):
        av = p_ref[0].astype(F32)
        ab = p_ref[1].astype(F32)
        ac = p_ref[2].astype(F32)
        az = p_ref[3].astype(F32)
        w = w_ref[...]
        u = ac * av
        y1 = _shift_down(u, 2, seq) * w[0:1] + _shift_down(u, 1, seq) * w[1:2] + u * w[2:3]
        y_ref[...] = (ab * y1 * (az * _sig(az))).astype(BF16)

    return pl.pallas_call(
        body, name="conv_fwd", grid=(nb, D // CONV_TC),
        in_specs=[pl.BlockSpec((4, seq, CONV_TC), lambda b, ci: (1, b, ci)),
                  pl.BlockSpec((8, CONV_TC), lambda b, ci: (0, ci))],
        out_specs=pl.BlockSpec((seq, CONV_TC), lambda b, ci: (b, ci)),
        out_shape=jax.ShapeDtypeStruct((t, D), BF16),
        compiler_params=_params(("parallel", "parallel")),
    )(proj, conv_w)


def conv_bwd(dproj, proj, dy, conv_w, seq):
    t = proj.shape[1]
    nb = t // seq

    def body(dp_in_ref, p_ref, dy_ref, w_ref, dp_ref, dw_ref):
        b = pl.program_id(1)
        av = p_ref[0].astype(F32)
        ab = p_ref[1].astype(F32)
        ac = p_ref[2].astype(F32)
        az = p_ref[3].astype(F32)
        dyv = dy_ref[...].astype(F32)
        w = w_ref[...]
        u = ac * av
        u1 = _shift_down(u, 1, seq)
        u2 = _shift_down(u, 2, seq)
        y1 = u2 * w[0:1] + u1 * w[1:2] + u * w[2:3]
        sz = _sig(az)
        silu = az * sz
        dy1 = dyv * ab * silu
        du = dy1 * w[2:3] + _shift_up(dy1, 1, seq) * w[1:2] + _shift_up(dy1, 2, seq) * w[0:1]
        dp_ref[0] = (du * ac).astype(BF16)
        dp_ref[1] = (dyv * y1 * silu).astype(BF16)
        dp_ref[2] = (du * av).astype(BF16)
        dp_ref[3] = (dyv * ab * y1 * (sz * (1.0 + az * (1.0 - sz)))).astype(BF16)

        @pl.when(b == 0)
        def _():
            dw_ref[...] = jnp.zeros_like(dw_ref)

        dw_ref[0:1, :] += jnp.sum(dy1 * u2, axis=0, keepdims=True)
        dw_ref[1:2, :] += jnp.sum(dy1 * u1, axis=0, keepdims=True)
        dw_ref[2:3, :] += jnp.sum(dy1 * u, axis=0, keepdims=True)

    return pl.pallas_call(
        body, name="conv_bwd", grid=(D // CONV_TC, nb),
        in_specs=[pl.BlockSpec(memory_space=pl.ANY),
                  pl.BlockSpec((4, seq, CONV_TC), lambda ci, b: (1, b, ci)),
                  pl.BlockSpec((seq, CONV_TC), lambda ci, b: (b, ci)),
                  pl.BlockSpec((8, CONV_TC), lambda ci, b: (0, ci))],
        out_specs=[pl.BlockSpec((4, seq, CONV_TC), lambda ci, b: (1, b, ci)),
                   pl.BlockSpec((8, CONV_TC), lambda ci, b: (0, ci))],
        out_shape=[jax.ShapeDtypeStruct(dproj.shape, BF16),
                   jax.ShapeDtypeStruct((8, D), F32)],
        input_output_aliases={0: 0},
        compiler_params=_params(("parallel", "arbitrary")),
    )(dproj, proj, dy, conv_w)


def _rope_tables(pos_ref, invf_ref, ma_ref, mb_ref, sign):
    ang = pos_ref[...].astype(F32) * invf_ref[...]
    cs = jnp.cos(ang)
    sn = jnp.sin(ang) * sign
    return cs, sn * ma_ref[...], sn * mb_ref[...]


def _rotate(v, cs, sa, sb):
    return v * cs + pltpu.roll(v, 128 - HALF, 1) * sa + pltpu.roll(v, HALF, 1) * sb


MLA_TM = 512


def mla_prep_fwd(proj, pos, g_q, g_kv, wuq, wukv, tabs):
    t = proj.shape[1]
    tm = min(MLA_TM, t)

    def body(lat_ref, pos_ref, gq_ref, gkv_ref, wuq_ref, wukv_ref, invf_ref, ma_ref, mb_ref,
             q_ref, k_ref, kv_ref, qn_ref, kvn_ref):
        lat = lat_ref[...].astype(F32)
        ql = lat[:, :QL]
        kl = lat[:, QL:QL + KVL]
        kr = lat[:, QL + KVL:QL + KVL + 128]
        qn = (ql * lax.rsqrt(jnp.mean(ql * ql, axis=-1, keepdims=True) + EPS) * gq_ref[...]).astype(BF16)
        kvn = (kl * lax.rsqrt(jnp.mean(kl * kl, axis=-1, keepdims=True) + EPS) * gkv_ref[...]).astype(BF16)
        qn_ref[...] = qn
        kvn_ref[...] = kvn
        cs, sa, sb = _rope_tables(pos_ref, invf_ref, ma_ref, mb_ref, 1.0)
        q = _dot_nt(qn, wuq_ref[...]) * (SM_SCALE * LOG2E)
        kv = _dot_nt(kvn, wukv_ref[...]).astype(BF16)
        kv_ref[...] = kv
        kpe = _rotate(kr, cs, sa, sb).astype(BF16)
        for hh in range(H):
            lo, mid, hi = hh * DQK, hh * DQK + 128, (hh + 1) * DQK
            q_ref[:, lo:mid] = q[:, lo:mid].astype(BF16)
            q_ref[:, mid:hi] = _rotate(q[:, mid:hi], cs, sa, sb).astype(BF16)
            k_ref[:, lo:mid] = kv[:, lo:mid]
            k_ref[:, mid:hi] = kpe

    row = lambda w: pl.BlockSpec((tm, w), lambda i: (i, 0))
    const = lambda a: pl.BlockSpec(a.shape, lambda i: (0,) * a.ndim)
    return pl.pallas_call(
        body, name="mla_prep_fwd", grid=(t // tm,),
        in_specs=[pl.BlockSpec((None, tm, D), lambda i: (SEG_LAT, i, 0)), row(1),
                  const(g_q), const(g_kv), const(wuq), const(wukv)] + [const(a) for a in tabs],
        out_specs=[row(H * DQK), row(H * DQK), row(H * DQK), row(QL), row(KVL)],
        out_shape=[jax.ShapeDtypeStruct((t, H * DQK), BF16)] * 3
        + [jax.ShapeDtypeStruct((t, QL), BF16), jax.ShapeDtypeStruct((t, KVL), BF16)],
        compiler_params=_params(("parallel",)),
    )(proj, pos, g_q, g_kv, wuq, wukv, *tabs)


def mla_prep_bwd(dproj, proj, dq_rot, dk, dv, pos, g_q, g_kv, wuq, wukv, tabs):
    t = proj.shape[1]
    tm = min(MLA_TM, t)

    def body(dp_in_ref, lat_ref, dqr_ref, dk_ref, dv_ref, pos_ref, gq_ref, gkv_ref, wuq_ref, wukv_ref,
             invf_ref, ma_ref, mb_ref, dp_ref, dq_ref, dkv_ref, dgq_ref, dgkv_ref):
        i = pl.program_id(0)
        lat = lat_ref[...].astype(F32)
        ql = lat[:, :QL]
        kl = lat[:, QL:QL + KVL]
        rq = lax.rsqrt(jnp.mean(ql * ql, axis=-1, keepdims=True) + EPS)
        rk = lax.rsqrt(jnp.mean(kl * kl, axis=-1, keepdims=True) + EPS)
        nq = ql * rq
        nk = kl * rk
        cs, sa, sb = _rope_tables(pos_ref, invf_ref, ma_ref, mb_ref, -1.0)
        dkpe = jnp.zeros((tm, 128), F32)
        for hh in range(H):
            lo, mid, hi = hh * DQK, hh * DQK + 128, (hh + 1) * DQK
            dq_ref[:, lo:mid] = (dqr_ref[:, lo:mid] * SM_SCALE).astype(BF16)
            dq_ref[:, mid:hi] = _rotate(dqr_ref[:, mid:hi] * SM_SCALE, cs, sa, sb).astype(BF16)
            dkv_ref[:, lo:mid] = dk_ref[:, lo:mid]
            dkv_ref[:, mid:hi] = dv_ref[:, hh * DV:(hh + 1) * DV]
            dkpe = dkpe + dk_ref[:, mid:hi].astype(F32)
        lane = lax.broadcasted_iota(jnp.int32, (tm, 128), 1)
        dkr = jnp.where(lane < ROPE, _rotate(dkpe, cs, sa, sb), 0.0)
        dqn = _dot(dq_ref[...], wuq_ref[...])
        dkvn = _dot(dkv_ref[...], wukv_ref[...])
        gq = gq_ref[...]
        gkv = gkv_ref[...]
        dnq = dqn * gq
        dnk = dkvn * gkv
        dql = rq * (dnq - nq * jnp.mean(dnq * nq, axis=-1, keepdims=True))
        dkl = rk * (dnk - nk * jnp.mean(dnk * nk, axis=-1, keepdims=True))
        dp_ref[:, :QL] = dql.astype(BF16)
        dp_ref[:, QL:QL + KVL] = dkl.astype(BF16)
        dp_ref[:, QL + KVL:QL + KVL + 128] = dkr.astype(BF16)
        dp_ref[:, QL + KVL + 128:] = jnp.zeros((tm, D - QL - KVL - 128), BF16)

        @pl.when(i == 0)
        def _():
            dgq_ref[...] = jnp.zeros_like(dgq_ref)
            dgkv_ref[...] = jnp.zeros_like(dgkv_ref)

        dgq_ref[...] += jnp.sum(dqn * nq, axis=0, keepdims=True)
        dgkv_ref[...] += jnp.sum(dkvn * nk, axis=0, keepdims=True)

    row = lambda w: pl.BlockSpec((tm, w), lambda i: (i, 0))
    const = lambda a: pl.BlockSpec(a.shape, lambda i: (0,) * a.ndim)
    seg = pl.BlockSpec((None, tm, D), lambda i: (SEG_LAT, i, 0))
    return pl.pallas_call(
        body, name="mla_prep_bwd", grid=(t // tm,),
        in_specs=[pl.BlockSpec(memory_space=pl.ANY), seg, row(H * DQK), row(H * DQK), row(H * DV), row(1),
                  const(g_q), const(g_kv), const(wuq), const(wukv)] + [const(a) for a in tabs],
        out_specs=[seg, row(H * DQK), row(H * DQK),
                   pl.BlockSpec((1, QL), lambda i: (0, 0)), pl.BlockSpec((1, KVL), lambda i: (0, 0))],
        out_shape=[jax.ShapeDtypeStruct(dproj.shape, BF16),
                   jax.ShapeDtypeStruct((t, H * DQK), BF16), jax.ShapeDtypeStruct((t, H * DQK), BF16),
                   jax.ShapeDtypeStruct((1, QL), F32), jax.ShapeDtypeStruct((1, KVL), F32)],
        input_output_aliases={0: 0},
        compiler_params=_params(("arbitrary",)),
    )(dproj, proj, dq_rot, dk, dv, pos, g_q, g_kv, wuq, wukv, *tabs)


def _causal_mask(s, shift):
    row = lax.broadcasted_iota(jnp.int32, s.shape, 0)
    col = lax.broadcasted_iota(jnp.int32, s.shape, 1)
    return jnp.where(col <= row + shift, s, -1e30)


def flash_fwd(q, k, kv, nb, seq):
    t = q.shape[0]
    tq = min(FLASH_TQ, seq)
    nq = seq // tq

    def body(q_ref, k_ref, v_ref, o_ref, lse_ref):
        for qi in range(nq):
            qs = slice(qi * tq, (qi + 1) * tq)
            qv = q_ref[qs, :]
            m = jnp.full((tq, 1), -1e30, F32)
            l = jnp.zeros((tq, 1), F32)
            acc = jnp.zeros((tq, DV), F32)
            for j in range(qi + 1):
                ks = slice(j * tq, (j + 1) * tq)
                s = _dot_nt(qv, k_ref[ks, :])
                if j == qi:
                    s = _causal_mask(s, 0)
                m_new = jnp.maximum(m, jnp.max(s, axis=1, keepdims=True))
                p = jnp.exp2(s - m_new)
                alpha = jnp.exp2(m - m_new)
                l = alpha * l + jnp.sum(p, axis=1, keepdims=True)
                acc = alpha * acc + _dot(p.astype(BF16), v_ref[ks, :])
                m = m_new
            o_ref[qs, :] = (acc / l).astype(BF16)
            lse_ref[qs, :] = jnp.broadcast_to(m + jnp.log(l) * LOG2E, (tq, DV))

    out_blk = pl.BlockSpec((seq, DV), lambda b, h: (b, h))
    return pl.pallas_call(
        body, name="flash_fwd", grid=(nb, H),
        in_specs=[pl.BlockSpec((seq, DQK), lambda b, h: (b, h)),
                  pl.BlockSpec((seq, DQK), lambda b, h: (b, h)),
                  pl.BlockSpec((seq, DV), lambda b, h: (b, 2 * h + 1))],
        out_specs=[out_blk, out_blk],
        out_shape=[jax.ShapeDtypeStruct((t, H * DV), BF16), jax.ShapeDtypeStruct((t, H * DV), F32)],
        compiler_params=_params(("parallel", "parallel")),
    )(q, k, kv)


def flash_bwd(q, k, kv, o, do, lse, nb, seq, token):
    t = q.shape[0]
    tq = min(FLASH_TQ, seq)
    nq = seq // tq

    def body(q_ref, k_ref, v_ref, o_ref, do_ref, lse_ref, tok_ref, dq_ref, dk_ref, dv_ref):
        delta, lse = [], []
        for qi in range(nq):
            qs = slice(qi * tq, (qi + 1) * tq)
            dl = jnp.sum(do_ref[qs, :].astype(F32) * o_ref[qs, :].astype(F32), axis=1, keepdims=True)
            delta.append(jnp.broadcast_to(dl, (tq, DV)).T[:1, :])
            lse.append(lse_ref[qs, :].T[:1, :])
        for ki in range(nq):
            ks = slice(ki * tq, (ki + 1) * tq)
            kb = k_ref[ks, :]
            vb = v_ref[ks, :]
            dk = jnp.zeros((tq, DQK), F32)
            dv = jnp.zeros((tq, DV), F32)
            for qi in range(ki, nq):
                qs = slice(qi * tq, (qi + 1) * tq)
                qv = q_ref[qs, :]
                dov = do_ref[qs, :]
                st = _dot_nt(kb, qv)
                if qi == ki:
                    row = lax.broadcasted_iota(jnp.int32, st.shape, 0)
                    col = lax.broadcasted_iota(jnp.int32, st.shape, 1)
                    st = jnp.where(row <= col, st, -1e30)
                pt = jnp.exp2(st - lse[qi])
                dpt = _dot_nt(vb, dov)
                dzt = (pt * (dpt - delta[qi])).astype(BF16)
                dv = dv + _dot(pt.astype(BF16), dov)
                dk = dk + _dot(dzt, qv)
                dqb = _dot_tn(dzt, kb)
                if ki == 0:
                    dq_ref[qs, :] = dqb
                else:
                    dq_ref[qs, :] += dqb
            dk_ref[ks, :] = (dk * LN2).astype(BF16)
            dv_ref[ks, :] = dv.astype(BF16)

    full = lambda w, col: pl.BlockSpec((seq, w), col)
    same = lambda b, h: (b, h)
    return pl.pallas_call(
        body, name="flash_bwd", grid=(nb, H),
        in_specs=[full(DQK, same), full(DQK, same), full(DV, lambda b, h: (b, 2 * h + 1)),
                  full(DV, same), full(DV, same), full(DV, same),
                  pl.BlockSpec((8, 128), lambda b, h: (0, 0))],
        out_specs=[full(DQK, same), full(DQK, same), full(DV, same)],
        out_shape=[jax.ShapeDtypeStruct((t, H * DQK), F32), jax.ShapeDtypeStruct((t, H * DQK), BF16),
                   jax.ShapeDtypeStruct((t, H * DV), BF16)],
        compiler_params=_params(("parallel", "parallel")),
    )(q, k, kv, o, do, lse, token)


TAIL_TM = 512


def tail_fwd(y, attn, proj, x2, tgt, gate, g_post, wco, wmo, wout, seq):
    t = y.shape[0]
    nb = t // seq
    tm = min(TAIL_TM, seq)
    tpb = seq // tm

    def body(y_ref, at_ref, p_ref, x_ref, t_ref, gate_ref, gp_ref, wco_ref, wmo_ref, wout_ref,
             o_ref, ya_ref, yb_ref, m_ref, do2_ref, dout_ref, dgate_ref, dgp_ref, loss_ref):
        i = pl.program_id(0)
        bz = p_ref[0].astype(F32)
        ga = p_ref[1].astype(F32)
        gb = p_ref[2].astype(F32)
        ov = (at_ref[...].astype(F32) * (bz * _sig(bz))).astype(BF16)
        o_ref[...] = ov
        ya = _dot(y_ref[...], wco_ref[...])
        yb = _dot(ov, wmo_ref[...])
        ya_ref[...] = ya.astype(BF16)
        yb_ref[...] = yb.astype(BF16)
        mv = (_sig(ga) * ya + _sig(gb) * yb).astype(BF16)
        m_ref[...] = mv
        o2 = _dot(mv, wout_ref[...])
        r = lax.rsqrt(jnp.mean(o2 * o2, axis=-1, keepdims=True) + EPS)
        nrm = o2 * r
        gp = gp_ref[...]
        gate_v = gate_ref[...]
        rn = nrm * gp
        err = x_ref[...] + gate_v * rn - t_ref[...]
        dout = err * (1.0 / D)
        dout_ref[...] = dout
        dn = dout * gate_v * gp
        do2_ref[...] = (r * (dn - nrm * jnp.mean(dn * nrm, axis=-1, keepdims=True))).astype(BF16)

        @pl.when(i % tpb == 0)
        def _():
            dgate_ref[...] = jnp.zeros_like(dgate_ref)

        @pl.when(i == 0)
        def _():
            dgp_ref[...] = jnp.zeros_like(dgp_ref)
            loss_ref[...] = jnp.zeros_like(loss_ref)

        dgate_ref[...] += jnp.sum(dout * rn, axis=0, keepdims=True)
        dgp_ref[...] += jnp.sum(dout * gate_v * nrm, axis=0, keepdims=True)
        loss_ref[...] += 0.5 * jnp.sum(jnp.mean(err * err, axis=-1, keepdims=True), axis=0, keepdims=True)

    row = pl.BlockSpec((tm, D), lambda i: (i, 0))
    per_batch = pl.BlockSpec((None, 1, D), lambda i: (i // tpb, 0, 0))
    vec = pl.BlockSpec((1, D), lambda i: (0, 0))
    wgt = pl.BlockSpec((D, D), lambda i: (0, 0))
    act = jax.ShapeDtypeStruct((t, D), BF16)
    return pl.pallas_call(
        body, name="tail_fwd", grid=(t // tm,),
        in_specs=[row, row, pl.BlockSpec((3, tm, D), lambda i: (0, i, 0)), row, row, per_batch, vec,
                  wgt, wgt, wgt],
        out_specs=[row, row, row, row, row, row, per_batch, vec, pl.BlockSpec((1, 1), lambda i: (0, 0))],
        out_shape=[act, act, act, act, act, jax.ShapeDtypeStruct((t, D), F32),
                   jax.ShapeDtypeStruct((nb, 1, D), F32), jax.ShapeDtypeStruct((1, D), F32),
                   jax.ShapeDtypeStruct((1, 1), F32)],
        compiler_params=_params(("arbitrary",)),
    )(y, attn, proj, x2, tgt, gate, g_post, wco, wmo, wout)


def tail_bwd(do2, proj, ya, yb, attn, wout, wmo, wco):
    t = do2.shape[0]
    tm = min(TAIL_TM, t)

    def body(do2_ref, p_ref, ya_ref, yb_ref, at_ref, wout_ref, wmo_ref, wco_ref,
             dp_ref, dya_ref, dyb_ref, dat_ref, dy_ref):
        bz = p_ref[0].astype(F32)
        ga = p_ref[1].astype(F32)
        gb = p_ref[2].astype(F32)
        dm = _dot_nt(do2_ref[...], wout_ref[...])
        sa = _sig(ga)
        sb = _sig(gb)
        dya = (dm * sa).astype(BF16)
        dyb = (dm * sb).astype(BF16)
        dya_ref[...] = dya
        dyb_ref[...] = dyb
        dp_ref[1] = (dm * ya_ref[...].astype(F32) * (sa * (1.0 - sa))).astype(BF16)
        dp_ref[2] = (dm * yb_ref[...].astype(F32) * (sb * (1.0 - sb))).astype(BF16)
        dov = _dot_nt(dyb, wmo_ref[...])
        sz = _sig(bz)
        dat_ref[...] = (dov * (bz * sz)).astype(BF16)
        dp_ref[0] = (dov * at_ref[...].astype(F32) * (sz * (1.0 + bz * (1.0 - sz)))).astype(BF16)
        dy_ref[...] = _dot_nt(dya, wco_ref[...]).astype(BF16)

    row = pl.BlockSpec((tm, D), lambda i: (i, 0))
    seg3 = pl.BlockSpec((3, tm, D), lambda i: (0, i, 0))
    wgt = pl.BlockSpec((D, D), lambda i: (0, 0))
    act = jax.ShapeDtypeStruct((t, D), BF16)
    return pl.pallas_call(
        body, name="tail_bwd", grid=(t // tm,),
        in_specs=[row, seg3, row, row, row, wgt, wgt, wgt],
        out_specs=[seg3, row, row, row, row],
        out_shape=[jax.ShapeDtypeStruct((NSEG, t, D), BF16), act, act, act, act],
        compiler_params=_params(("parallel",)),
    )(do2, proj, ya, yb, attn, wout, wmo, wco)


def adamw(w, m, v, g, g2, name, token=None):
    rows, cols = w.shape
    tr = rows
    for cand in (256, 128, 64, 32, 16, 8):
        if rows % cand == 0 and rows > cand:
            tr = cand
            break
    has2 = g2 is not None
    n_in = 4 + has2

    def body(*refs):
        w_ref, m_ref, v_ref, g_ref = refs[:4]
        go_ref, d_ref, mo_ref, vo_ref = refs[-4:]
        grad = g_ref[...] + refs[4][...].astype(F32) if has2 else g_ref[...]
        mn = ADAM_B1 * m_ref[...] + (1.0 - ADAM_B1) * grad
        vn = ADAM_B2 * v_ref[...] + (1.0 - ADAM_B2) * (grad * grad)
        m_hat = mn / (1.0 - ADAM_B1 ** ADAM_STEP)
        v_hat = vn / (1.0 - ADAM_B2 ** ADAM_STEP)
        go_ref[...] = grad
        d_ref[...] = -ADAM_LR * (m_hat / (jnp.sqrt(v_hat) + ADAM_EPS) + ADAM_WD * w_ref[...])
        mo_ref[...] = mn
        vo_ref[...] = vn

    blk = pl.BlockSpec((tr, cols), lambda i: (i, 0))
    ins = [w, m, v, g] + ([g2] if has2 else [])
    specs = [blk] * n_in
    if token is not None:
        ins.append(token)
        specs.append(pl.BlockSpec((8, 128), lambda i: (0, 0)))
    return pl.pallas_call(
        body, name=name, grid=(rows // tr,),
        in_specs=specs, out_specs=[blk] * 4,
        out_shape=[jax.ShapeDtypeStruct((rows, cols), F32)] * 4,
        compiler_params=_params(("parallel",)),
    )(*ins)


def adamw_scattered(w, m, v, own, land, me, tr, name, transpose=False):
    slot_rows = land.shape[1]
    cols = land.shape[2]
    rows = slot_rows if transpose else w.shape[0]
    per_slot = slot_rows // tr

    def body(me_ref, w_ref, m_ref, v_ref, own_ref, land_ref, go_ref, d_ref, mo_ref, vo_ref):
        grad = own_ref[...].astype(F32)
        for s in range(8):
            grad = grad + land_ref[s].astype(F32)
        if transpose:
            grad = grad.T
        mn = ADAM_B1 * m_ref[...] + (1.0 - ADAM_B1) * grad
        vn = ADAM_B2 * v_ref[...] + (1.0 - ADAM_B2) * (grad * grad)
        m_hat = mn / (1.0 - ADAM_B1 ** ADAM_STEP)
        v_hat = vn / (1.0 - ADAM_B2 ** ADAM_STEP)
        go_ref[...] = grad
        d_ref[...] = -ADAM_LR * (m_hat / (jnp.sqrt(v_hat) + ADAM_EPS) + ADAM_WD * w_ref[...])
        mo_ref[...] = mn
        vo_ref[...] = vn

    wblk = pl.BlockSpec(w.shape if transpose else (tr, w.shape[1]), lambda i, s: (i, 0))
    return pl.pallas_call(
        body, name=name,
        grid_spec=pltpu.PrefetchScalarGridSpec(
            num_scalar_prefetch=1, grid=(rows // tr,),
            in_specs=[wblk, wblk, wblk,
                      pl.BlockSpec((tr, cols), lambda i, s: (s[0] * per_slot + i, 0)),
                      pl.BlockSpec((8, tr, cols), lambda i, s: (0, i, 0))],
            out_specs=[wblk] * 4),
        out_shape=[jax.ShapeDtypeStruct(w.shape, F32)] * 4,
        compiler_params=_params(),
    )(me, w, m, v, own, land)


def adamw_win(wt, mt, vt, ka, ra, kb, rb):
    rows = wt.shape[0]
    tc = 256
    nh = (D // 2) // tc

    def body(w_ref, m_ref, v_ref, ka_ref, ra_ref, kb_ref, rb_ref, go_ref, d_ref, mo_ref, vo_ref):
        first = pl.program_id(0) < nh
        grad = jnp.where(first, ka_ref[...] + ra_ref[...].astype(F32), kb_ref[...] + rb_ref[...].astype(F32))
        mn = ADAM_B1 * m_ref[...] + (1.0 - ADAM_B1) * grad
        vn = ADAM_B2 * v_ref[...] + (1.0 - ADAM_B2) * (grad * grad)
        m_hat = mn / (1.0 - ADAM_B1 ** ADAM_STEP)
        v_hat = vn / (1.0 - ADAM_B2 ** ADAM_STEP)
        go_ref[...] = grad
        d_ref[...] = -ADAM_LR * (m_hat / (jnp.sqrt(v_hat) + ADAM_EPS) + ADAM_WD * w_ref[...])
        mo_ref[...] = mn
        vo_ref[...] = vn

    blk = pl.BlockSpec((rows, tc), lambda j: (0, j))
    lo = pl.BlockSpec((rows, tc), lambda j: (0, jnp.minimum(j, nh - 1)))
    hi = pl.BlockSpec((rows, tc), lambda j: (0, jnp.maximum(j - nh, 0)))
    return pl.pallas_call(
        body, name="adamw_w_in", grid=(D // tc,),
        in_specs=[blk, blk, blk, lo, lo, hi, hi], out_specs=[blk] * 4,
        out_shape=[jax.ShapeDtypeStruct((rows, D), F32)] * 4,
        compiler_params=_params(("parallel",)),
    )(wt, mt, vt, ka, ra, kb, rb)


_ORD_A = ("x", "y", "c")
_ORD_B = ("y", "x", "c")


def _rows128(a, rows):
    flat = a.reshape(-1)
    return jnp.pad(flat, (0, rows * 128 - flat.shape[0])).reshape(rows, 128)


def kernel(x, c, positions, w_ada, b_ada, g_pre, w_in, conv_w, w_conv_out, g_q, w_uq, g_kv, w_ukv, w_mla_out, w_out, g_post, loss_target, m_w_ada, m_b_ada, m_g_pre, m_w_in, m_conv_w, m_w_conv_out, m_g_q, m_w_uq, m_g_kv, m_w_ukv, m_w_mla_out, m_w_out, m_g_post, v_w_ada, v_b_ada, v_g_pre, v_w_in, v_conv_w, v_w_conv_out, v_g_q, v_w_uq, v_g_kv, v_w_ukv, v_w_mla_out, v_w_out, v_g_post):
    nb, seq, _ = x.shape
    t = nb * seq
    mx, my, mc = lax.axis_index("x"), lax.axis_index("y"), lax.axis_index("c")
    me = 4 * mx + 2 * my + mc
    co = {"x": mx, "y": my, "c": mc}

    x2 = x.reshape(t, D)
    tgt2 = loss_target.reshape(t, D)
    pos2 = positions.reshape(t, 1)

    packed = jnp.concatenate([c.reshape(2 * D // 128, 128), _rows128(conv_w[0], 8)], axis=0)
    gath = small_allgather(packed, "gather_cond")
    c_all = gath[:, :16].reshape(8 * nb, D)
    conv_full = gath[:, 16:19].reshape(8, 3, 128).transpose(1, 0, 2).reshape(3, D)
    conv_full8 = jnp.pad(conv_full, ((0, 5), (0, 0)))
    ada_cols = w_ada.shape[2]
    b_cols = lax.dynamic_slice(b_ada, (0, me * ada_cols), (1, ada_cols))
    mod_part = ada_fwd(c_all, w_ada[0], b_cols)
    mod_g = small_allgather(mod_part.reshape(8 * nb * ada_cols // 128, 128), "gather_mod")
    mod_all = mod_g.reshape(8, 8 * nb, ada_cols).transpose(1, 0, 2).reshape(8 * nb, 8 * ada_cols)
    mod = lax.dynamic_slice(mod_all, (me * nb, 0), (nb, 3 * D))
    shift = mod[:, 0:D].reshape(nb, 1, D)
    scale = mod[:, D:2 * D].reshape(nb, 1, D)
    gate = mod[:, 2 * D:3 * D].reshape(nb, 1, D)

    wt = w_in[0].T.astype(BF16)
    lo = lax.bitcast_convert_type(wt[:, :D // 2], jnp.uint16).astype(jnp.uint32)
    hi = lax.bitcast_convert_type(wt[:, D // 2:], jnp.uint16).astype(jnp.uint32)
    wt_bits = lax.bitcast_convert_type(lo | (hi << 16), F32)
    wt_bits, mod = lax.optimization_barrier((wt_bits, mod))
    shift = mod[:, 0:D].reshape(nb, 1, D)
    scale = mod[:, D:2 * D].reshape(nb, 1, D)
    gate = mod[:, 2 * D:3 * D].reshape(nb, 1, D)
    q4 = D // 4
    r3rd = wt_bits.shape[0] // 3
    plan = [(0, (k * r3rd, r3rd), (g * q4, q4), (_ORD_A, _ORD_B)[g]) for k in range(3) for g in range(2)]
    gw = allgather_big([wt_bits], plan, "gather_w_in")
    late = [w_conv_out[0].astype(BF16), w_mla_out[0].astype(BF16), w_out[0].astype(BF16),
            jnp.pad(w_uq[0].T.astype(BF16), ((0, DQK - 192), (0, 0))), w_ukv[0].T.astype(BF16)]
    gw0, late = lax.optimization_barrier((gw[0], late))
    late_state, late_token = gather_start(late, "gather_late_start")
    wt_bits_all = gw0.reshape(N_IN, D // 2)

    inv_freq = ROPE_THETA ** (-jnp.arange(0, ROPE, 2, dtype=F32) / ROPE)
    invf = jnp.concatenate([inv_freq, inv_freq, jnp.zeros((128 - ROPE,), F32)]).reshape(1, 128)
    lane = np.arange(128)
    tabs = (invf,
            jnp.asarray(np.where(lane < HALF, -1.0, 0.0).reshape(1, 128), F32),
            jnp.asarray(np.where((lane >= HALF) & (lane < ROPE), 1.0, 0.0).reshape(1, 128), F32))

    h = prenorm_fwd(x2, scale, shift, g_pre, seq)
    proj, wt_p = proj_matmul(h, wt_bits_all, late_token)
    y = conv_fwd(proj, conv_full8, seq)
    gl = gather_wait(late_state, y, "gather_late_wait")
    wco = gl[0].reshape(D, D)
    wmo = gl[1].reshape(D, D)
    wout = gl[2].reshape(D, D)
    wuq_p = gl[3].reshape(H * DQK, QL)
    wukv = gl[4].reshape(H * 256, KVL)
    q_rot, k_cat, kv, qn, kvn = mla_prep_fwd(proj, pos2, g_q, g_kv, wuq_p, wukv, tabs)
    attn, lse = flash_fwd(q_rot, k_cat, kv, nb, seq)
    o, ya, yb, m, do2, dout, dgate, dg_post, loss_part = tail_fwd(
        y, attn, proj, x2, tgt2, gate, g_post, wco, wmo, wout, seq)

    dproj, dya, dyb, dattn, dy = tail_bwd(do2, proj, ya, yb, attn, wout, wmo, wco)
    g_wout = grad_matmul(m, do2, "grad_w_square")
    g_wmo = grad_matmul(o, dyb, "grad_w_square")
    g_wco = grad_matmul(y, dya, "grad_w_square")
    sc1, sc1_tok = scatter_start([g_wco, g_wmo, g_wout], "scatter_out_grads_start")
    dproj, dconv = conv_bwd(dproj, proj, dy, conv_full8, seq)
    dq_rot, dk, dv = flash_bwd(q_rot, k_cat, kv, attn, dattn, lse, nb, seq, sc1_tok)
    dproj, dq, dkv, dg_q, dg_kv = mla_prep_bwd(dproj, proj, dq_rot, dk, dv, pos2, g_q, g_kv, wuq_p, wukv, tabs)
    g_wuq_t = grad_matmul(dq, qn, "grad_w_uq")
    g_wukv_t = grad_matmul(dkv, kvn, "grad_w_ukv")
    sc2, sc2_tok = scatter_start([g_wuq_t, g_wukv_t], "scatter_mla_grads_start")
    g_win_p = win_grad_matmul(h, dproj, sc2_tok)

    g_wt = g_win_p.reshape(2, 2, 2, N_IN // 8, D)
    ords = [("c", "y", "x"), ("c", "x", "y")]
    hc = D // 2
    win_shape = (2, 2, N_IN // 8, hc)
    pick_w = lambda col: (lambda ref, cc: ref.at[:, :, 1 - cc["c"], :, pl.ds(col * hc, hc)])
    which1 = [0, 0]
    picks1 = [pick_w(0), pick_w(1)]
    st1, tok1 = swap_start([g_wt], which1, ["c"] * 2, picks1, [win_shape] * 2, "rs_c_start")
    assert nb == 2
    dh0 = dh_matmul(dproj, wt_p, tok1, seq, 0)
    (g_wt,), r1 = swap_wait(st1, dh0, which1, ["c"] * 2, picks1, "rs_c_wait")
    sel_xyc = jnp.stack([mx, my, mc]).astype(jnp.int32)
    sel2 = [jnp.stack([co[o[2]]]).astype(jnp.int32) for o in ords]
    first = [rs_win_add_first(g_wt, r1[0], sel_xyc, 1, 0, "rs_add_first_0"),
             rs_win_add_first(g_wt, r1[1], sel_xyc, 0, 1, "rs_add_first_1")]
    keep1, send1 = zip(*first)
    all4 = [0, 1]
    none4 = [None] * 2
    axes2 = [o[1] for o in ords]
    st2, tok2 = swap_start(list(send1), all4, axes2, none4, [s.shape for s in send1], "rs_ici1_start")

    dh1 = dh_matmul(dproj, wt_p, tok2, seq, 1)
    gx0, dsh0, dsc0, dgp0 = prenorm_bwd(dh0, x2, dout, scale, g_pre, seq, tok2, 0, None)
    _, r2 = swap_wait(st2, (gx0, dh1), all4, axes2, none4, "rs_ici1_wait")
    keep2, send2 = zip(*[rs_add_second(keep1[a], r2[a], sel2[a], "rs_add_second") for a in range(2)])
    axes3 = [o[2] for o in ords]
    st3, tok3 = swap_start(list(send2), all4, axes3, none4, [s.shape for s in send2], "rs_ici2_start")
    grad_x2, dsh1, dsc1, dgp1 = prenorm_bwd(dh1, x2, dout, scale, g_pre, seq, tok3, 1, gx0)
    dshift = jnp.stack([dsh0, dsh1])
    dscale = jnp.stack([dsc0, dsc1])
    dg_pre = dgp0 + dgp1

    dmod = jnp.concatenate([dshift, dscale, dgate], axis=2).reshape(nb * 3 * D // 128, 128)
    small = jnp.concatenate([
        dmod, _rows128(dg_pre, 8), _rows128(dg_post, 8), _rows128(dg_q, 8), _rows128(dg_kv, 8),
        dconv[0:3].reshape(24, 128), _rows128(loss_part, 8)], axis=0)
    small_g = small_allgather(small, "gather_small_grads")
    sums = slot_sum(small_g)
    dmod_all = small_g[:, 0:48].reshape(8 * nb, 3 * D)
    g_bada = (sums[0:24] + sums[24:48]).reshape(1, 3 * D)
    g_gpre = sums[48:56].reshape(1, D)
    g_gpost = sums[56:64].reshape(1, D)
    g_gq = sums[64:67].reshape(1, QL)
    g_gkv = sums[72:74].reshape(1, KVL)
    g_conv_full = sums[80:104].reshape(3, D)
    loss = sums[104, 0]
    g_conv = lax.dynamic_slice(g_conv_full, (0, me * 128), (3, 128))
    dmod_cols = lax.dynamic_slice(dmod_all, (0, me * ada_cols), (8 * nb, ada_cols))
    g_wada = ada_bwd(c_all, dmod_cols)

    res = {}
    res["w_ada"] = [o_[None] for o_ in adamw(w_ada[0], m_w_ada[0], v_w_ada[0], g_wada, None, "adamw_w_ada", tok3)]

    def pack(b_, gp_, gpo_, gq_, gkv_, cw_):
        return jnp.concatenate([_rows128(b_, 24), _rows128(gp_, 8), _rows128(gpo_, 8), _rows128(gq_, 8),
                                _rows128(gkv_, 8), _rows128(cw_, 8)], axis=0)

    sw = pack(b_ada, g_pre, g_post, g_q, g_kv, conv_w)
    sm = pack(m_b_ada, m_g_pre, m_g_post, m_g_q, m_g_kv, m_conv_w)
    sv = pack(v_b_ada, v_g_pre, v_g_post, v_g_q, v_g_kv, v_conv_w)
    sg = pack(g_bada, g_gpre, g_gpost, g_gq, g_gkv, g_conv)
    small_out = adamw(sw, sm, sv, sg, None, "adamw_small", tok3)

    _, r3 = swap_wait(st3, small_out[0], all4, axes3, none4, "rs_ici2_wait")

    (g_wco, g_wmo, g_wout), (l_wco, l_wmo, l_wout) = scatter_wait(sc1, small_out[1], "scatter_out_grads_wait")
    (g_wuq_t, g_wukv_t), (l_wuq, l_wukv) = scatter_wait(sc2, small_out[2], "scatter_mla_grads_wait")

    res["w_in"] = [o_.T[None] for o_ in adamw_win(w_in[0].T, m_w_in[0].T, v_w_in[0].T,
                                                  keep2[0], r3[0], keep2[1], r3[1])]
    me1 = me.reshape(1).astype(jnp.int32)
    res["w_uq"] = [o_.T[None] for o_ in adamw_scattered(
        w_uq[0].T, m_w_uq[0].T, v_w_uq[0].T, g_wuq_t, l_wuq, me1, 64, "adamw_w_uq")]
    res["w_ukv"] = [o_[None] for o_ in adamw_scattered(
        w_ukv[0], m_w_ukv[0], v_w_ukv[0], g_wukv_t, l_wukv, me1, KVL, "adamw_w_ukv", transpose=True)]
    for nm, wv, mv, vv, gg, ll in (("w_conv_out", w_conv_out, m_w_conv_out, v_w_conv_out, g_wco, l_wco),
                                   ("w_mla_out", w_mla_out, m_w_mla_out, v_w_mla_out, g_wmo, l_wmo),
                                   ("w_out", w_out, m_w_out, v_w_out, g_wout, l_wout)):
        res[nm] = [o_[None] for o_ in adamw_scattered(wv[0], mv[0], vv[0], gg, ll, me1, 128, "adamw_square")]

    def unpack(a):
        return {"b_ada": a[0:24].reshape(1, 3 * D), "g_pre": a[24:32].reshape(1, D),
                "g_post": a[32:40].reshape(1, D), "g_q": a[40:43].reshape(1, QL),
                "g_kv": a[48:50].reshape(1, KVL), "conv_w": a[56:59].reshape(-1)[:3 * 128].reshape(1, 3, 128)}

    for nm in ("b_ada", "g_pre", "g_post", "g_q", "g_kv", "conv_w"):
        res[nm] = [unpack(a)[nm] for a in small_out]

    order = ["w_ada", "b_ada", "g_pre", "w_in", "conv_w", "w_conv_out", "g_q", "w_uq", "g_kv", "w_ukv",
             "w_mla_out", "w_out", "g_post"]
    out = [loss, grad_x2.reshape(nb, seq, D)]
    for k_ in range(4):
        out += [res[nm][k_] for nm in order]
    return tuple(out)
```

```python
import functools

import numpy as np
import jax
import jax.numpy as jnp
from jax import lax
from jax.experimental import pallas as pl
from jax.experimental.pallas import tpu as pltpu

F32 = jnp.float32
BF16 = jnp.bfloat16
MESH = pl.DeviceIdType.MESH

D = 1024
H = 8
QL = 384
KVL = 256
ROPE = 64
HALF = ROPE // 2
DQK = 256
DV = 128
NSEG = 8
NP = NSEG * D
EPS = 1e-6
ROPE_THETA = 10000.0
SM_SCALE = (128 + ROPE) ** -0.5
LOG2E = 1.4426950408889634
LN2 = 0.6931471805599453
FLASH_TQ = 512

SEG_BZ, SEG_GA, SEG_GB, SEG_LAT, SEG_V = 0, 1, 2, 3, 4

ADAM_LR = 0.001
ADAM_B1 = 0.9
ADAM_B2 = 0.999
ADAM_EPS = 1e-08
ADAM_WD = 0.01
ADAM_STEP = 10

VMEM_LIMIT = 56 * 1024 * 1024


def _params(sem=None, vmem=VMEM_LIMIT):
    kw = dict(vmem_limit_bytes=vmem)
    if sem is not None:
        kw["dimension_semantics"] = sem
    return pltpu.CompilerParams(**kw)


def _sig(v):
    return 0.5 * jnp.tanh(0.5 * v) + 0.5


def _dot(a, b):
    return jnp.dot(a, b, preferred_element_type=F32)


def _dot_nt(a, b):
    return lax.dot_general(a, b, (((1,), (1,)), ((), ())), preferred_element_type=F32)


def _dot_tn(a, b):
    return lax.dot_general(a, b, (((0,), (0,)), ((), ())), preferred_element_type=F32)


_AXIS_POS = {"x": 0, "y": 1, "c": 2}


def _coords():
    return lax.axis_index("x"), lax.axis_index("y"), lax.axis_index("c")


def _partner(axis):
    p = list(_coords())
    p[_AXIS_POS[axis]] = 1 - p[_AXIS_POS[axis]]
    return tuple(p)


def small_allgather(v, name):
    rows = v.shape[0]

    def body(v_ref, out_ref, send_sems, recv_sems):
        x, y, c = _coords()
        me = 4 * x + 2 * y + c
        out_ref[me] = v_ref[...]
        copies = []
        for k in range(1, 8):
            peer = (1 - x if k & 4 else x, 1 - y if k & 2 else y, 1 - c if k & 1 else c)
            cp = pltpu.make_async_remote_copy(
                src_ref=v_ref, dst_ref=out_ref.at[me],
                send_sem=send_sems.at[k - 1], recv_sem=recv_sems.at[k - 1],
                device_id=peer, device_id_type=MESH)
            cp.start()
            copies.append(cp)
        for cp in copies:
            cp.wait()

    return pl.pallas_call(
        body, name=name,
        out_shape=jax.ShapeDtypeStruct((8, rows, 128), F32),
        in_specs=[pl.BlockSpec(memory_space=pltpu.VMEM)],
        out_specs=pl.BlockSpec(memory_space=pltpu.VMEM),
        scratch_shapes=[pltpu.SemaphoreType.DMA((7,)), pltpu.SemaphoreType.DMA((7,))],
    )(v)


def _own_block_placed(s):
    x, y, c = _coords()
    return lax.dynamic_update_slice(lax.empty((2, 2, 2) + s.shape, s.dtype), s[None, None, None],
                                    (x, y, c) + (0,) * s.ndim)


def allgather_big(arrs, plan, name):
    n = len(arrs)
    m = len(plan)
    nst = len(plan[0][3])

    def body(*refs):
        ins, outs = refs[n:2 * n], refs[2 * n:3 * n]
        send_sems, recv_sems = refs[3 * n:]
        x, y, c = _coords()
        co = {"x": x, "y": y, "c": c}

        def window(ref, lead, rows, cols):
            win = tuple(slice(None) if w is None else pl.ds(w[0], w[1]) for w in (rows, cols))
            return ref.at[tuple(lead) + win]

        def held(e, free):
            i, rows, cols, _ = plan[e]
            lead = [slice(None) if ax in free else co[ax] for ax in ("x", "y", "c")]
            return window(outs[i], lead, rows, cols)

        def rcopy(e, stage, src, dst, axis):
            return pltpu.make_async_remote_copy(
                src_ref=src, dst_ref=dst,
                send_sem=send_sems.at[e, stage], recv_sem=recv_sems.at[e, stage],
                device_id=_partner(axis), device_id_type=MESH)

        stages = [[] for _ in range(nst)]
        for e, (i, rows, cols, order) in enumerate(plan):
            cp = rcopy(e, 0, window(ins[i], [], rows, cols), held(e, ()), order[0])
            cp.start()
            stages[0].append(cp)
        for s in range(1, nst):
            for e, (i, rows, cols, order) in enumerate(plan):
                stages[s - 1][e].wait_recv()
                blk = held(e, order[:s])
                cp = rcopy(e, s, blk, blk, order[s])
                cp.start()
                stages[s].append(cp)
        for e in range(m):
            stages[nst - 1][e].wait_recv()
        for e in range(m):
            for s in range(nst):
                stages[s][e].wait_send()

    any_spec = pl.BlockSpec(memory_space=pl.ANY)
    lands = [_own_block_placed(a) for a in arrs]
    return pl.pallas_call(
        body, name=name,
        out_shape=[jax.ShapeDtypeStruct(l.shape, l.dtype) for l in lands],
        in_specs=[any_spec] * (2 * n),
        out_specs=[any_spec] * n,
        input_output_aliases={i: i for i in range(n)},
        scratch_shapes=[pltpu.SemaphoreType.DMA((m, nst)), pltpu.SemaphoreType.DMA((m, nst))],
    )(*lands, *arrs)


def exchange(arrs, axes, picks, out_shapes, name):
    n = len(arrs)

    def body(*refs):
        ins, outs = refs[:n], refs[n:2 * n]
        send_sems, recv_sems = refs[2 * n:]
        x, y, c = _coords()
        co = {"x": x, "y": y, "c": c}
        copies = []
        for a in range(n):
            src = ins[a] if picks[a] is None else picks[a](ins[a], co)
            cp = pltpu.make_async_remote_copy(
                src_ref=src, dst_ref=outs[a],
                send_sem=send_sems.at[a], recv_sem=recv_sems.at[a],
                device_id=_partner(axes[a]), device_id_type=MESH)
            cp.start()
            copies.append(cp)
        for cp in copies:
            cp.wait()

    any_spec = pl.BlockSpec(memory_space=pl.ANY)
    return pl.pallas_call(
        body, name=name,
        out_shape=[jax.ShapeDtypeStruct(s, a.dtype) for s, a in zip(out_shapes, arrs)],
        in_specs=[any_spec] * n,
        out_specs=[any_spec] * n,
        scratch_shapes=[pltpu.SemaphoreType.DMA((n,)), pltpu.SemaphoreType.DMA((n,))],
    )(*arrs)


_HBM = pl.BlockSpec(memory_space=pltpu.HBM)
_SEM = pl.BlockSpec(memory_space=pltpu.SEMAPHORE)


def _swap_copies(srcs, lands, send_sems, recv_sems, axes, picks):
    x, y, c = _coords()
    co = {"x": x, "y": y, "c": c}
    return [pltpu.make_async_remote_copy(
        src_ref=srcs[a] if picks[a] is None else picks[a](srcs[a], co), dst_ref=lands[a],
        send_sem=send_sems.at[a], recv_sem=recv_sems.at[a],
        device_id=_partner(axes[a]), device_id_type=MESH) for a in range(len(srcs))]


def swap_start(arrs, which, axes, picks, out_shapes, name):
    ns, n = len(arrs), len(which)

    def body(*refs):
        srcs, lands = refs[:ns], refs[ns:ns + n]
        send_sems, recv_sems = refs[ns + n:ns + n + 2]
        token = refs[-1]
        for cp in _swap_copies([srcs[i] for i in which], lands, send_sems, recv_sems, axes, picks):
            cp.start()
        token[...] = jnp.zeros_like(token)

    lands = [lax.empty(s, arrs[i].dtype) for s, i in zip(out_shapes, which)]
    ops = [pltpu.with_memory_space_constraint(a, pltpu.HBM) for a in list(arrs) + lands]
    out = pl.pallas_call(
        body, name=name,
        out_shape=[pltpu.SemaphoreType.DMA((n,)), pltpu.SemaphoreType.DMA((n,))]
        + [pltpu.HBM(o.shape, o.dtype) for o in ops] + [jax.ShapeDtypeStruct((8, 128), F32)],
        in_specs=[_HBM] * (ns + n),
        out_specs=[_SEM, _SEM] + [_HBM] * (ns + n) + [pl.BlockSpec(memory_space=pltpu.VMEM)],
        input_output_aliases={i: 2 + i for i in range(ns + n)},
        compiler_params=pltpu.CompilerParams(has_side_effects=pltpu.SideEffectType.DATAFLOW_SIDE_EFFECTING),
    )(*ops)
    return out[:-1], out[-1]


def swap_wait(state, after, which, axes, picks, name):
    n = len(which)
    ns = len(state) - 2 - n

    def body(*refs):
        srcs, lands = refs[:ns], refs[ns:ns + n]
        send_sems, recv_sems = refs[ns + n:ns + n + 2]
        for cp in _swap_copies([srcs[i] for i in which], lands, send_sems, recv_sems, axes, picks):
            cp.wait_send()
            cp.wait_recv()

    thru = list(state[2:])
    after = list(after) if isinstance(after, (list, tuple)) else [after]
    out = pl.pallas_call(
        body, name=name,
        out_shape=[pltpu.HBM(o.shape, o.dtype) for o in thru],
        in_specs=[_HBM] * (ns + n) + [_SEM, _SEM] + [pl.BlockSpec(memory_space=pl.ANY)] * len(after),
        out_specs=[_HBM] * (ns + n),
        input_output_aliases={i: i for i in range(ns + n)},
        compiler_params=pltpu.CompilerParams(has_side_effects=pltpu.SideEffectType.DATAFLOW_SIDE_EFFECTING),
    )(*thru, state[0], state[1], *after)
    return out[:ns], out[ns:]


def _gather_copies(shards, lands, send_sems, recv_sems):
    x, y, c = _coords()
    copies = []
    for a in range(len(shards)):
        for k in range(1, 8):
            peer = (1 - x if k & 4 else x, 1 - y if k & 2 else y, 1 - c if k & 1 else c)
            copies.append(pltpu.make_async_remote_copy(
                src_ref=shards[a], dst_ref=lands[a].at[x, y, c],
                send_sem=send_sems.at[7 * a + k - 1], recv_sem=recv_sems.at[7 * a + k - 1],
                device_id=peer, device_id_type=MESH))
    return copies


def gather_start(shards, name):
    n = len(shards)
    x, y, c = _coords()

    def body(*refs):
        srcs, lands = refs[:n], refs[n:2 * n]
        send_sems, recv_sems = refs[2 * n:2 * n + 2]
        token = refs[-1]
        for cp in _gather_copies(srcs, lands, send_sems, recv_sems):
            cp.start()
        token[...] = jnp.zeros_like(token)

    lands = [_own_block_placed(s) for s in shards]
    ops = [pltpu.with_memory_space_constraint(a, pltpu.HBM) for a in list(shards) + lands]
    out = pl.pallas_call(
        body, name=name,
        out_shape=[pltpu.SemaphoreType.DMA((7 * n,)), pltpu.SemaphoreType.DMA((7 * n,))]
        + [pltpu.HBM(o.shape, o.dtype) for o in ops] + [jax.ShapeDtypeStruct((8, 128), F32)],
        in_specs=[_HBM] * (2 * n),
        out_specs=[_SEM, _SEM] + [_HBM] * (2 * n) + [pl.BlockSpec(memory_space=pltpu.VMEM)],
        input_output_aliases={i: 2 + i for i in range(2 * n)},
        compiler_params=pltpu.CompilerParams(has_side_effects=pltpu.SideEffectType.DATAFLOW_SIDE_EFFECTING),
    )(*ops)
    return out[:-1], out[-1]


def gather_wait(state, after, name):
    n = (len(state) - 2) // 2

    def body(*refs):
        srcs, lands = refs[:n], refs[n:2 * n]
        send_sems, recv_sems = refs[2 * n:2 * n + 2]
        for cp in _gather_copies(srcs, lands, send_sems, recv_sems):
            cp.wait_send()
            cp.wait_recv()

    thru = list(state[2:])
    out = pl.pallas_call(
        body, name=name,
        out_shape=[pltpu.HBM(o.shape, o.dtype) for o in thru],
        in_specs=[_HBM] * (2 * n) + [_SEM, _SEM, pl.BlockSpec(memory_space=pl.ANY)],
        out_specs=[_HBM] * (2 * n),
        input_output_aliases={i: i for i in range(2 * n)},
        compiler_params=pltpu.CompilerParams(has_side_effects=pltpu.SideEffectType.DATAFLOW_SIDE_EFFECTING),
    )(*thru, state[0], state[1], after)
    return out[n:]


def _scatter_copies(grads, lands, send_sems, recv_sems):
    x, y, c = _coords()
    me = 4 * x + 2 * y + c
    copies = []
    for a in range(len(grads)):
        r = grads[a].shape[0] // 8
        for k in range(1, 8):
            px, py, pc = (1 - x if k & 4 else x, 1 - y if k & 2 else y, 1 - c if k & 1 else c)
            rows = pl.ds(pl.multiple_of((4 * px + 2 * py + pc) * r, r), r)
            copies.append(pltpu.make_async_remote_copy(
                src_ref=grads[a].at[rows], dst_ref=lands[a].at[me],
                send_sem=send_sems.at[7 * a + k - 1], recv_sem=recv_sems.at[7 * a + k - 1],
                device_id=(px, py, pc), device_id_type=MESH))
    return copies


def scatter_start(grads, name):
    n = len(grads)

    def body(*refs):
        srcs, lands = refs[:n], refs[n:2 * n]
        send_sems, recv_sems = refs[2 * n:2 * n + 2]
        token = refs[-1]
        for cp in _scatter_copies(srcs, lands, send_sems, recv_sems):
            cp.start()
        token[...] = jnp.zeros_like(token)

    lands = [jnp.zeros((8, g.shape[0] // 8, g.shape[1]), g.dtype) for g in grads]
    ops = [pltpu.with_memory_space_constraint(a, pltpu.HBM) for a in list(grads) + lands]
    out = pl.pallas_call(
        body, name=name,
        out_shape=[pltpu.SemaphoreType.DMA((7 * n,)), pltpu.SemaphoreType.DMA((7 * n,))]
        + [pltpu.HBM(o.shape, o.dtype) for o in ops] + [jax.ShapeDtypeStruct((8, 128), F32)],
        in_specs=[_HBM] * (2 * n),
        out_specs=[_SEM, _SEM] + [_HBM] * (2 * n) + [pl.BlockSpec(memory_space=pltpu.VMEM)],
        input_output_aliases={i: 2 + i for i in range(2 * n)},
        compiler_params=pltpu.CompilerParams(has_side_effects=pltpu.SideEffectType.DATAFLOW_SIDE_EFFECTING),
    )(*ops)
    return out[:-1], out[-1]


def scatter_wait(state, after, name):
    n = (len(state) - 2) // 2

    def body(*refs):
        srcs, lands = refs[:n], refs[n:2 * n]
        send_sems, recv_sems = refs[2 * n:2 * n + 2]
        for cp in _scatter_copies(srcs, lands, send_sems, recv_sems):
            cp.wait_send()
            cp.wait_recv()

    thru = list(state[2:])
    after = list(after) if isinstance(after, (list, tuple)) else [after]
    out = pl.pallas_call(
        body, name=name,
        out_shape=[pltpu.HBM(o.shape, o.dtype) for o in thru],
        in_specs=[_HBM] * (2 * n) + [_SEM, _SEM] + [pl.BlockSpec(memory_space=pl.ANY)] * len(after),
        out_specs=[_HBM] * (2 * n),
        input_output_aliases={i: i for i in range(2 * n)},
        compiler_params=pltpu.CompilerParams(has_side_effects=pltpu.SideEffectType.DATAFLOW_SIDE_EFFECTING),
    )(*thru, state[0], state[1], *after)
    return out[:n], out[n:]


def rs_win_add_first(g, r, sel, next_dim, col, name):
    rows, cols = r.shape[2:]

    def body(sel_ref, gk_ref, rk_ref, gs_ref, rs_ref, keep_ref, send_ref):
        keep_ref[...] = gk_ref[...] + rk_ref[...]
        send_ref[...] = (gs_ref[...] + rs_ref[...]).astype(BF16)

    def g_map(flip):
        def f(j, s):
            nxt = 1 - s[next_dim] if flip else s[next_dim]
            return (nxt, j, s[2], 0, col) if next_dim == 0 else (j, nxt, s[2], 0, col)
        return f

    def r_map(flip):
        def f(j, s):
            nxt = 1 - s[next_dim] if flip else s[next_dim]
            return (nxt, j, 0, 0) if next_dim == 0 else (j, nxt, 0, 0)
        return f

    gblk = (None, None, None, rows, cols)
    rblk = (None, None, rows, cols)
    oblk = (None, rows, cols)
    return pl.pallas_call(
        body, name=name,
        grid_spec=pltpu.PrefetchScalarGridSpec(
            num_scalar_prefetch=1, grid=(2,),
            in_specs=[pl.BlockSpec(gblk, g_map(False)), pl.BlockSpec(rblk, r_map(False)),
                      pl.BlockSpec(gblk, g_map(True)), pl.BlockSpec(rblk, r_map(True))],
            out_specs=[pl.BlockSpec(oblk, lambda j, s: (j, 0, 0)),
                       pl.BlockSpec(oblk, lambda j, s: (j, 0, 0))]),
        out_shape=[jax.ShapeDtypeStruct((2, rows, cols), F32),
                   jax.ShapeDtypeStruct((2, rows, cols), BF16)],
        compiler_params=_params(),
    )(sel, g, r, g, r)


def rs_add_second(k, r, sel, name):
    _, rows, cols = k.shape
    tr = rows // 2 if rows % 32 == 0 else rows
    nt = rows // tr

    def body(sel_ref, kk_ref, rk_ref, ks_ref, rs_ref, keep_ref, send_ref):
        keep_ref[...] = kk_ref[...] + rk_ref[...].astype(F32)
        send_ref[...] = (ks_ref[...] + rs_ref[...].astype(F32)).astype(BF16)

    blk = (None, tr, cols)
    oblk = (tr, cols)
    return pl.pallas_call(
        body, name=name,
        grid_spec=pltpu.PrefetchScalarGridSpec(
            num_scalar_prefetch=1, grid=(nt,),
            in_specs=[
                pl.BlockSpec(blk, lambda i, s: (s[0], i, 0)),
                pl.BlockSpec(blk, lambda i, s: (s[0], i, 0)),
                pl.BlockSpec(blk, lambda i, s: (1 - s[0], i, 0)),
                pl.BlockSpec(blk, lambda i, s: (1 - s[0], i, 0)),
            ],
            out_specs=[pl.BlockSpec(oblk, lambda i, s: (i, 0)),
                       pl.BlockSpec(oblk, lambda i, s: (i, 0))]),
        out_shape=[jax.ShapeDtypeStruct((rows, cols), F32),
                   jax.ShapeDtypeStruct((rows, cols), BF16)],
        compiler_params=_params(),
    )(sel, k, r, k, r)


SEG_ROWS = (4800, 5824, 6848, 4096, 0, 1024, 2048, 3072)
LAT_ROWS = QL + KVL + ROPE
N_IN = 7872


def _seg_row(j):
    return pl.multiple_of(jnp.where(j < 3, 4800 + 1024 * j, jnp.where(j == 3, 4096, (j - 4) * 1024)), 8)


def proj_matmul(h, wt_bits, token):
    t = h.shape[0]
    tm = min(2048, t)

    def body(h_ref, w_hbm, tok_ref, o_ref, wt_ref, buf, sems):
        j = pl.program_id(0)
        slot = j % 2

        def fetch(seg, into):
            return pltpu.make_async_copy(w_hbm.at[pl.ds(_seg_row(seg), D)], buf.at[into], sems.at[into])

        @pl.when(pl.program_id(1) == 0)
        def _():
            @pl.when(j == 0)
            def _():
                fetch(j, slot).start()

            fetch(j, slot).wait()

            @pl.when(j + 1 < NSEG)
            def _():
                fetch(j + 1, 1 - slot).start()

            bits = pltpu.bitcast(buf[slot], jnp.uint32)
            row = lax.broadcasted_iota(jnp.int32, (D, D // 2), 0)
            live = jnp.logical_or(j != SEG_LAT, row < LAT_ROWS)
            lo = pltpu.bitcast(bits << 16, F32)
            hi = pltpu.bitcast(bits & jnp.uint32(0xFFFF0000), F32)
            wt_ref[:, :D // 2] = jnp.where(live, lo, 0.0).astype(BF16)
            wt_ref[:, D // 2:] = jnp.where(live, hi, 0.0).astype(BF16)

        o_ref[...] = _dot_nt(h_ref[...], wt_ref[...]).astype(BF16)

    return pl.pallas_call(
        body, name="proj_matmul", grid=(NSEG, t // tm),
        in_specs=[pl.BlockSpec((tm, D), lambda j, i: (i, 0)),
                  pl.BlockSpec(memory_space=pl.ANY),
                  pl.BlockSpec((8, 128), lambda j, i: (0, 0))],
        out_specs=[pl.BlockSpec((None, tm, D), lambda j, i: (j, i, 0)),
                   pl.BlockSpec((D, D), lambda j, i: (j, 0))],
        out_shape=[jax.ShapeDtypeStruct((NSEG, t, D), BF16), jax.ShapeDtypeStruct((NP, D), BF16)],
        scratch_shapes=[pltpu.VMEM((2, D, D // 2), F32), pltpu.SemaphoreType.DMA((2,))],
        compiler_params=_params(("arbitrary", "arbitrary")),
    )(h, wt_bits, token)


def dh_matmul(dproj, wt, token, seq, b):
    tm = min(1024, seq)
    nblk = seq // tm

    per = 2

    def body(b_ref, d_ref, w_ref, tok_ref, o_ref, acc_ref):
        k = pl.program_id(1)

        @pl.when(k == 0)
        def _():
            acc_ref[...] = jnp.zeros_like(acc_ref)

        part = _dot(d_ref[0], w_ref[0:D, :])
        for j in range(1, per):
            part = part + _dot(d_ref[j], w_ref[j * D:(j + 1) * D, :])
        acc_ref[...] += part

        @pl.when(k == NSEG // per - 1)
        def _():
            o_ref[...] = acc_ref[...]

    return pl.pallas_call(
        body, name="dh_matmul",
        grid_spec=pltpu.PrefetchScalarGridSpec(
            num_scalar_prefetch=1, grid=(nblk, NSEG // per),
            in_specs=[pl.BlockSpec((per, tm, D), lambda i, k, s: (k, s[0] * nblk + i, 0)),
                      pl.BlockSpec((per * D, D), lambda i, k, s: (k, 0)),
                      pl.BlockSpec((8, 128), lambda i, k, s: (0, 0))],
            out_specs=pl.BlockSpec((tm, D), lambda i, k, s: (i, 0)),
            scratch_shapes=[pltpu.VMEM((tm, D), F32)]),
        out_shape=jax.ShapeDtypeStruct((seq, D), F32),
        compiler_params=_params(("parallel", "arbitrary")),
    )(jnp.full((1,), b, jnp.int32), dproj, wt, token)


def win_grad_matmul(h, dproj, token):
    t = h.shape[0]
    tk = min(2048, t)
    nk = t // tk

    def body(h_ref, d_ref, tok_ref, o_hbm, acc_ref, sem):
        j = pl.program_id(0)
        k = pl.program_id(1)

        @pl.when(k == 0)
        def _():
            acc_ref[...] = jnp.zeros_like(acc_ref)

        acc_ref[...] += _dot_tn(d_ref[...], h_ref[...])

        @pl.when(jnp.logical_and(k == nk - 1, j != SEG_LAT))
        def _():
            cp = pltpu.make_async_copy(acc_ref, o_hbm.at[pl.ds(_seg_row(j), D)], sem)
            cp.start()
            cp.wait()

        @pl.when(jnp.logical_and(k == nk - 1, j == SEG_LAT))
        def _():
            cp = pltpu.make_async_copy(acc_ref.at[pl.ds(0, LAT_ROWS)],
                                       o_hbm.at[pl.ds(SEG_ROWS[SEG_LAT], LAT_ROWS)], sem)
            cp.start()
            cp.wait()

    return pl.pallas_call(
        body, name="win_grad_matmul", grid=(NSEG, nk),
        in_specs=[pl.BlockSpec((tk, D), lambda j, k: (k, 0)),
                  pl.BlockSpec((None, tk, D), lambda j, k: (j, k, 0)),
                  pl.BlockSpec((8, 128), lambda j, k: (0, 0))],
        out_specs=pl.BlockSpec(memory_space=pl.ANY),
        out_shape=jax.ShapeDtypeStruct((N_IN, D), F32),
        scratch_shapes=[pltpu.VMEM((D, D), F32), pltpu.SemaphoreType.DMA],
        compiler_params=_params(("arbitrary", "arbitrary")),
    )(h, dproj, token)


def grad_matmul(a, b, name):
    t, m = a.shape
    n = b.shape[1]
    tk = min(1024, t)
    nk = t // tk

    def body(a_ref, b_ref, o_ref, acc_ref):
        k = pl.program_id(0)

        @pl.when(k == 0)
        def _():
            acc_ref[...] = jnp.zeros_like(acc_ref)

        acc_ref[...] += _dot_tn(a_ref[...], b_ref[...])

        @pl.when(k == nk - 1)
        def _():
            o_ref[...] = acc_ref[...].astype(BF16)

    return pl.pallas_call(
        body, name=name, grid=(nk,),
        in_specs=[pl.BlockSpec((tk, m), lambda k: (k, 0)),
                  pl.BlockSpec((tk, n), lambda k: (k, 0))],
        out_specs=pl.BlockSpec((m, n), lambda k: (0, 0)),
        out_shape=jax.ShapeDtypeStruct((m, n), BF16),
        scratch_shapes=[pltpu.VMEM((m, n), F32)],
        compiler_params=_params(("arbitrary",)),
    )(a, b)


def ada_gather(c8, taps8, w_ada, b_cols):
    cols = w_ada.shape[1]

    def body(c_ref, t_ref, w_ref, b_ref, call_ref, tall_ref, mod_ref, part_ref, send_sems, recv_sems):
        x, y, c = _coords()
        me = 4 * x + 2 * y + c
        peers = [(1 - x if k & 4 else x, 1 - y if k & 2 else y, 1 - c if k & 1 else c) for k in range(1, 8)]

        def rcopy(n, src, dst, peer):
            return pltpu.make_async_remote_copy(src_ref=src, dst_ref=dst, send_sem=send_sems.at[n],
                                                recv_sem=recv_sems.at[n], device_id=peer, device_id_type=MESH)

        call_ref[me] = c_ref[...]
        tall_ref[me] = t_ref[...]
        first = []
        for k, peer in enumerate(peers):
            first += [rcopy(k, c_ref, call_ref.at[me], peer), rcopy(7 + k, t_ref, tall_ref.at[me], peer)]
        for cp in first:
            cp.start()
        for cp in first:
            cp.wait()
        rows = call_ref[...].reshape(64, D).astype(BF16)
        part_ref[...] = _dot(rows, w_ref[...].astype(BF16)) + b_ref[...]
        mod_ref[me] = part_ref[pl.ds(pl.multiple_of(8 * me, 8), 8), :]
        second = []
        for k, (px, py, pc) in enumerate(peers):
            theirs = part_ref.at[pl.ds(pl.multiple_of(8 * (4 * px + 2 * py + pc), 8), 8)]
            second.append(rcopy(14 + k, theirs, mod_ref.at[me], (px, py, pc)))
        for cp in second:
            cp.start()
        for cp in second:
            cp.wait()

    vm = pl.BlockSpec(memory_space=pltpu.VMEM)
    return pl.pallas_call(
        body, name="ada_gather",
        out_shape=[jax.ShapeDtypeStruct((8, 8, D), F32), jax.ShapeDtypeStruct((8, 8, 128), F32),
                   jax.ShapeDtypeStruct((8, 8, cols), F32)],
        in_specs=[vm] * 4, out_specs=[vm] * 3,
        scratch_shapes=[pltpu.VMEM((64, cols), F32), pltpu.SemaphoreType.DMA((21,)),
                        pltpu.SemaphoreType.DMA((21,))],
        compiler_params=_params(),
    )(c8, taps8, w_ada, b_cols)


def ada_bwd(c_all, dmod_cols):
    def body(c_ref, d_ref, o_ref):
        o_ref[...] = _dot_tn(c_ref[...].astype(BF16), d_ref[...].astype(BF16))

    return pl.pallas_call(
        body, name="ada_bwd",
        out_shape=jax.ShapeDtypeStruct((c_all.shape[1], dmod_cols.shape[1]), F32),
        compiler_params=_params(),
    )(c_all, dmod_cols)


def slot_sum(g):
    def body(g_ref, o_ref):
        acc = g_ref[0]
        for s in range(1, 8):
            acc = acc + g_ref[s]
        o_ref[...] = acc

    return pl.pallas_call(
        body, name="slot_sum",
        out_shape=jax.ShapeDtypeStruct(g.shape[1:], F32),
    )(g)


def prenorm_fwd(x2, scale, shift, g_pre, seq):
    t = x2.shape[0]
    tm = min(512, seq)
    tpb = seq // tm

    def body(x_ref, sc_ref, sh_ref, g_ref, h_ref):
        xv = x_ref[...]
        r = lax.rsqrt(jnp.mean(xv * xv, axis=-1, keepdims=True) + EPS)
        hv = (xv * r * g_ref[...]) * (1.0 + sc_ref[...]) + sh_ref[...]
        h_ref[...] = hv.astype(BF16)

    per_batch = pl.BlockSpec((None, 1, D), lambda i: (i // tpb, 0, 0))
    return pl.pallas_call(
        body, name="prenorm_fwd", grid=(t // tm,),
        in_specs=[pl.BlockSpec((tm, D), lambda i: (i, 0)), per_batch, per_batch,
                  pl.BlockSpec((1, D), lambda i: (0, 0))],
        out_specs=pl.BlockSpec((tm, D), lambda i: (i, 0)),
        out_shape=jax.ShapeDtypeStruct((t, D), BF16),
        compiler_params=_params(("parallel",)),
    )(x2, scale, shift, g_pre)


def prenorm_bwd(dh, x2, dout, scale, g_pre, seq, token, b, gx_prev):
    t = x2.shape[0]
    tm = min(512, seq)
    tpb = seq // tm
    if gx_prev is None:
        gx_prev = lax.empty((t, D), F32)

    def body(b_ref, dh_ref, x_ref, do_ref, sc_ref, g_ref, tok_ref, gxp_ref, gx_ref, dsh_ref, dsc_ref, dg_ref):
        i = pl.program_id(0)
        xv = x_ref[...]
        dhv = dh_ref[...]
        g = g_ref[...]
        r = lax.rsqrt(jnp.mean(xv * xv, axis=-1, keepdims=True) + EPS)
        nrm = xv * r
        dxn = dhv * (1.0 + sc_ref[...])
        dn = dxn * g
        dx = r * (dn - nrm * jnp.mean(dn * nrm, axis=-1, keepdims=True))
        gx_ref[...] = dx + do_ref[...]

        @pl.when(i == 0)
        def _():
            dsh_ref[...] = jnp.zeros_like(dsh_ref)
            dsc_ref[...] = jnp.zeros_like(dsc_ref)
            dg_ref[...] = jnp.zeros_like(dg_ref)

        dsh_ref[...] += jnp.sum(dhv, axis=0, keepdims=True)
        dsc_ref[...] += jnp.sum(dhv * (nrm * g), axis=0, keepdims=True)
        dg_ref[...] += jnp.sum(dxn * nrm, axis=0, keepdims=True)

    row = pl.BlockSpec((tm, D), lambda i, s: (i, 0))
    grow = pl.BlockSpec((tm, D), lambda i, s: (s[0] * tpb + i, 0))
    per_batch = pl.BlockSpec((None, 1, D), lambda i, s: (s[0], 0, 0))
    vec = pl.BlockSpec((1, D), lambda i, s: (0, 0))
    return pl.pallas_call(
        body, name="prenorm_bwd",
        grid_spec=pltpu.PrefetchScalarGridSpec(
            num_scalar_prefetch=1, grid=(tpb,),
            in_specs=[row, grow, grow, per_batch, vec, pl.BlockSpec((8, 128), lambda i, s: (0, 0)),
                      pl.BlockSpec(memory_space=pl.ANY)],
            out_specs=[grow, vec, vec, vec]),
        out_shape=[jax.ShapeDtypeStruct((t, D), F32), jax.ShapeDtypeStruct((1, D), F32),
                   jax.ShapeDtypeStruct((1, D), F32), jax.ShapeDtypeStruct((1, D), F32)],
        input_output_aliases={7: 0},
        compiler_params=_params(("arbitrary",)),
    )(jnp.full((1,), b, jnp.int32), dh, x2, dout, scale, g_pre, token, gx_prev)


CONV_TC = 128


def _shift_down(u, k, rows):
    idx = lax.broadcasted_iota(jnp.int32, u.shape, 0)
    return jnp.where(idx >= k, pltpu.roll(u, k, 0), 0.0)


def _shift_up(u, k, rows):
    idx = lax.broadcasted_iota(jnp.int32, u.shape, 0)
    return jnp.where(idx < rows - k, pltpu.roll(u, rows - k, 0), 0.0)


def conv_fwd(proj, conv_w, seq):
    t = proj.shape[1]
    nb = t // seq

    def body(p_ref, w_ref, y_ref):
        av = p_ref[0].astype(F32)
        ab = p_ref[1].astype(F32)
        ac = p_ref[2].astype(F32)
        az = p_ref[3].astype(F32)
        w = w_ref[...]
        u = ac * av
        y1 = _shift_down(u, 2, seq) * w[0:1] + _shift_down(u, 1, seq) * w[1:2] + u * w[2:3]
        y_ref[...] = (ab * y1 * (az * _sig(az))).astype(BF16)

    return pl.pallas_call(
        body, name="conv_fwd", grid=(nb, D // CONV_TC),
        in_specs=[pl.BlockSpec((4, seq, CONV_TC), lambda b, ci: (1, b, ci)),
                  pl.BlockSpec((8, CONV_TC), lambda b, ci: (0, ci))],
        out_specs=pl.BlockSpec((seq, CONV_TC), lambda b, ci: (b, ci)),
        out_shape=jax.ShapeDtypeStruct((t, D), BF16),
        compiler_params=_params(("parallel", "parallel")),
    )(proj, conv_w)


def conv_bwd(dproj, proj, dy, conv_w, seq):
    t = proj.shape[1]
    nb = t // seq

    def body(dp_in_ref, p_ref, dy_ref, w_ref, dp_ref, dw_ref):
        b = pl.program_id(1)
        av = p_ref[0].astype(F32)
        ab = p_ref[1].astype(F32)
        ac = p_ref[2].astype(F32)
        az = p_ref[3].astype(F32)
        dyv = dy_ref[...].astype(F32)
        w = w_ref[...]
        u = ac * av
        u1 = _shift_down(u, 1, seq)
        u2 = _shift_down(u, 2, seq)
        y1 = u2 * w[0:1] + u1 * w[1:2] + u * w[2:3]
        sz = _sig(az)
        silu = az * sz
        dy1 = dyv * ab * silu
        du = dy1 * w[2:3] + _shift_up(dy1, 1, seq) * w[1:2] + _shift_up(dy1, 2, seq) * w[0:1]
        dp_ref[0] = (du * ac).astype(BF16)
        dp_ref[1] = (dyv * y1 * silu).astype(BF16)
        dp_ref[2] = (du * av).astype(BF16)
        dp_ref[3] = (dyv * ab * y1 * (sz * (1.0 + az * (1.0 - sz)))).astype(BF16)

        @pl.when(b == 0)
        def _():
            dw_ref[...] = jnp.zeros_like(dw_ref)

        dw_ref[0:1, :] += jnp.sum(dy1 * u2, axis=0, keepdims=True)
        dw_ref[1:2, :] += jnp.sum(dy1 * u1, axis=0, keepdims=True)
        dw_ref[2:3, :] += jnp.sum(dy1 * u, axis=0, keepdims=True)

    return pl.pallas_call(
        body, name="conv_bwd", grid=(D // CONV_TC, nb),
        in_specs=[pl.BlockSpec(memory_space=pl.ANY),
                  pl.BlockSpec((4, seq, CONV_TC), lambda ci, b: (1, b, ci)),
                  pl.BlockSpec((seq, CONV_TC), lambda ci, b: (b, ci)),
                  pl.BlockSpec((8, CONV_TC), lambda ci, b: (0, ci))],
        out_specs=[pl.BlockSpec((4, seq, CONV_TC), lambda ci, b: (1, b, ci)),
                   pl.BlockSpec((8, CONV_TC), lambda ci, b: (0, ci))],
        out_shape=[jax.ShapeDtypeStruct(dproj.shape, BF16),
                   jax.ShapeDtypeStruct((8, D), F32)],
        input_output_aliases={0: 0},
        compiler_params=_params(("parallel", "arbitrary")),
    )(dproj, proj, dy, conv_w)


def _rope_tables(pos_ref, invf_ref, ma_ref, mb_ref, sign):
    ang = pos_ref[...].astype(F32) * invf_ref[...]
    cs = jnp.cos(ang)
    sn = jnp.sin(ang) * sign
    return cs, sn * ma_ref[...], sn * mb_ref[...]


def _rotate(v, cs, sa, sb):
    return v * cs + pltpu.roll(v, 128 - HALF, 1) * sa + pltpu.roll(v, HALF, 1) * sb


MLA_TM = 512


def mla_prep_fwd(proj, pos, g_q, g_kv, wuq, wukv, tabs):
    t = proj.shape[1]
    tm = min(MLA_TM, t)

    def body(lat_ref, pos_ref, gq_ref, gkv_ref, wuq_ref, wukv_ref, invf_ref, ma_ref, mb_ref,
             q_ref, k_ref, kv_ref, qn_ref, kvn_ref):
        lat = lat_ref[...].astype(F32)
        ql = lat[:, :QL]
        kl = lat[:, QL:QL + KVL]
        kr = lat[:, QL + KVL:QL + KVL + 128]
        qn = (ql * lax.rsqrt(jnp.mean(ql * ql, axis=-1, keepdims=True) + EPS) * gq_ref[...]).astype(BF16)
        kvn = (kl * lax.rsqrt(jnp.mean(kl * kl, axis=-1, keepdims=True) + EPS) * gkv_ref[...]).astype(BF16)
        qn_ref[...] = qn
        kvn_ref[...] = kvn
        cs, sa, sb = _rope_tables(pos_ref, invf_ref, ma_ref, mb_ref, 1.0)
        q = _dot_nt(qn, wuq_ref[...]) * (SM_SCALE * LOG2E)
        kv = _dot_nt(kvn, wukv_ref[...]).astype(BF16)
        kv_ref[...] = kv
        kpe = _rotate(kr, cs, sa, sb).astype(BF16)
        for hh in range(H):
            lo, mid, hi = hh * DQK, hh * DQK + 128, (hh + 1) * DQK
            q_ref[:, lo:mid] = q[:, lo:mid].astype(BF16)
            q_ref[:, mid:hi] = _rotate(q[:, mid:hi], cs, sa, sb).astype(BF16)
            k_ref[:, lo:mid] = kv[:, lo:mid]
            k_ref[:, mid:hi] = kpe

    row = lambda w: pl.BlockSpec((tm, w), lambda i: (i, 0))
    const = lambda a: pl.BlockSpec(a.shape, lambda i: (0,) * a.ndim)
    return pl.pallas_call(
        body, name="mla_prep_fwd", grid=(t // tm,),
        in_specs=[pl.BlockSpec((None, tm, D), lambda i: (SEG_LAT, i, 0)), row(1),
                  const(g_q), const(g_kv), const(wuq), const(wukv)] + [const(a) for a in tabs],
        out_specs=[row(H * DQK), row(H * DQK), row(H * DQK), row(QL), row(KVL)],
        out_shape=[jax.ShapeDtypeStruct((t, H * DQK), BF16)] * 3
        + [jax.ShapeDtypeStruct((t, QL), BF16), jax.ShapeDtypeStruct((t, KVL), BF16)],
        compiler_params=_params(("parallel",)),
    )(proj, pos, g_q, g_kv, wuq, wukv, *tabs)


def mla_prep_bwd(dproj, proj, dq_rot, dk, dv, pos, g_q, g_kv, wuq, wukv, tabs):
    t = proj.shape[1]
    tm = min(MLA_TM, t)

    def body(dp_in_ref, lat_ref, dqr_ref, dk_ref, dv_ref, pos_ref, gq_ref, gkv_ref, wuq_ref, wukv_ref,
             invf_ref, ma_ref, mb_ref, dp_ref, dq_ref, dkv_ref, dgq_ref, dgkv_ref):
        i = pl.program_id(0)
        lat = lat_ref[...].astype(F32)
        ql = lat[:, :QL]
        kl = lat[:, QL:QL + KVL]
        rq = lax.rsqrt(jnp.mean(ql * ql, axis=-1, keepdims=True) + EPS)
        rk = lax.rsqrt(jnp.mean(kl * kl, axis=-1, keepdims=True) + EPS)
        nq = ql * rq
        nk = kl * rk
        cs, sa, sb = _rope_tables(pos_ref, invf_ref, ma_ref, mb_ref, -1.0)
        dkpe = jnp.zeros((tm, 128), F32)
        for hh in range(H):
            lo, mid, hi = hh * DQK, hh * DQK + 128, (hh + 1) * DQK
            dq_ref[:, lo:mid] = (dqr_ref[:, lo:mid] * SM_SCALE).astype(BF16)
            dq_ref[:, mid:hi] = _rotate(dqr_ref[:, mid:hi] * SM_SCALE, cs, sa, sb).astype(BF16)
            dkv_ref[:, lo:mid] = dk_ref[:, lo:mid]
            dkv_ref[:, mid:hi] = dv_ref[:, hh * DV:(hh + 1) * DV]
            dkpe = dkpe + dk_ref[:, mid:hi].astype(F32)
        lane = lax.broadcasted_iota(jnp.int32, (tm, 128), 1)
        dkr = jnp.where(lane < ROPE, _rotate(dkpe, cs, sa, sb), 0.0)
        dqn = _dot(dq_ref[...], wuq_ref[...])
        dkvn = _dot(dkv_ref[...], wukv_ref[...])
        gq = gq_ref[...]
        gkv = gkv_ref[...]
        dnq = dqn * gq
        dnk = dkvn * gkv
        dql = rq * (dnq - nq * jnp.mean(dnq * nq, axis=-1, keepdims=True))
        dkl = rk * (dnk - nk * jnp.mean(dnk * nk, axis=-1, keepdims=True))
        dp_ref[:, :QL] = dql.astype(BF16)
        dp_ref[:, QL:QL + KVL] = dkl.astype(BF16)
        dp_ref[:, QL + KVL:QL + KVL + 128] = dkr.astype(BF16)
        dp_ref[:, QL + KVL + 128:] = jnp.zeros((tm, D - QL - KVL - 128), BF16)

        @pl.when(i == 0)
        def _():
            dgq_ref[...] = jnp.zeros_like(dgq_ref)
            dgkv_ref[...] = jnp.zeros_like(dgkv_ref)

        dgq_ref[...] += jnp.sum(dqn * nq, axis=0, keepdims=True)
        dgkv_ref[...] += jnp.sum(dkvn * nk, axis=0, keepdims=True)

    row = lambda w: pl.BlockSpec((tm, w), lambda i: (i, 0))
    const = lambda a: pl.BlockSpec(a.shape, lambda i: (0,) * a.ndim)
    seg = pl.BlockSpec((None, tm, D), lambda i: (SEG_LAT, i, 0))
    return pl.pallas_call(
        body, name="mla_prep_bwd", grid=(t // tm,),
        in_specs=[pl.BlockSpec(memory_space=pl.ANY), seg, row(H * DQK), row(H * DQK), row(H * DV), row(1),
                  const(g_q), const(g_kv), const(wuq), const(wukv)] + [const(a) for a in tabs],
        out_specs=[seg, row(H * DQK), row(H * DQK),
                   pl.BlockSpec((1, QL), lambda i: (0, 0)), pl.BlockSpec((1, KVL), lambda i: (0, 0))],
        out_shape=[jax.ShapeDtypeStruct(dproj.shape, BF16),
                   jax.ShapeDtypeStruct((t, H * DQK), BF16), jax.ShapeDtypeStruct((t, H * DQK), BF16),
                   jax.ShapeDtypeStruct((1, QL), F32), jax.ShapeDtypeStruct((1, KVL), F32)],
        input_output_aliases={0: 0},
        compiler_params=_params(("arbitrary",)),
    )(dproj, proj, dq_rot, dk, dv, pos, g_q, g_kv, wuq, wukv, *tabs)


def _causal_mask(s, shift):
    row = lax.broadcasted_iota(jnp.int32, s.shape, 0)
    col = lax.broadcasted_iota(jnp.int32, s.shape, 1)
    return jnp.where(col <= row + shift, s, -1e30)


def flash_fwd(q, k, kv, nb, seq):
    t = q.shape[0]
    tq = min(FLASH_TQ, seq)
    nq = seq // tq

    def body(q_ref, k_ref, v_ref, o_ref, lse_ref):
        for qi in range(nq):
            qs = slice(qi * tq, (qi + 1) * tq)
            qv = q_ref[qs, :]
            m = jnp.full((tq, 1), -1e30, F32)
            l = jnp.zeros((tq, 1), F32)
            acc = jnp.zeros((tq, DV), F32)
            for j in range(qi + 1):
                ks = slice(j * tq, (j + 1) * tq)
                s = _dot_nt(qv, k_ref[ks, :])
                if j == qi:
                    s = _causal_mask(s, 0)
                m_new = jnp.maximum(m, jnp.max(s, axis=1, keepdims=True))
                p = jnp.exp2(s - m_new)
                alpha = jnp.exp2(m - m_new)
                l = alpha * l + jnp.sum(p, axis=1, keepdims=True)
                acc = alpha * acc + _dot(p.astype(BF16), v_ref[ks, :])
                m = m_new
            o_ref[qs, :] = (acc / l).astype(BF16)
            lse_ref[qs, :] = jnp.broadcast_to(m + jnp.log(l) * LOG2E, (tq, DV))

    out_blk = pl.BlockSpec((seq, DV), lambda b, h: (b, h))
    return pl.pallas_call(
        body, name="flash_fwd", grid=(nb, H),
        in_specs=[pl.BlockSpec((seq, DQK), lambda b, h: (b, h)),
                  pl.BlockSpec((seq, DQK), lambda b, h: (b, h)),
                  pl.BlockSpec((seq, DV), lambda b, h: (b, 2 * h + 1))],
        out_specs=[out_blk, out_blk],
        out_shape=[jax.ShapeDtypeStruct((t, H * DV), BF16), jax.ShapeDtypeStruct((t, H * DV), F32)],
        compiler_params=_params(("parallel", "parallel")),
    )(q, k, kv)


def flash_bwd(q, k, kv, o, do, lse, nb, seq, token):
    t = q.shape[0]
    tq = min(FLASH_TQ, seq)
    nq = seq // tq

    def body(q_ref, k_ref, v_ref, o_ref, do_ref, lse_ref, tok_ref, dq_ref, dk_ref, dv_ref):
        delta, lse = [], []
        for qi in range(nq):
            qs = slice(qi * tq, (qi + 1) * tq)
            dl = jnp.sum(do_ref[qs, :].astype(F32) * o_ref[qs, :].astype(F32), axis=1, keepdims=True)
            delta.append(jnp.broadcast_to(dl, (tq, DV)).T[:1, :])
            lse.append(lse_ref[qs, :].T[:1, :])
        for ki in range(nq):
            ks = slice(ki * tq, (ki + 1) * tq)
            kb = k_ref[ks, :]
            vb = v_ref[ks, :]
            dk = jnp.zeros((tq, DQK), F32)
            dv = jnp.zeros((tq, DV), F32)
            for qi in range(ki, nq):
                qs = slice(qi * tq, (qi + 1) * tq)
                qv = q_ref[qs, :]
                dov = do_ref[qs, :]
                st = _dot_nt(kb, qv)
                if qi == ki:
                    row = lax.broadcasted_iota(jnp.int32, st.shape, 0)
                    col = lax.broadcasted_iota(jnp.int32, st.shape, 1)
                    st = jnp.where(row <= col, st, -1e30)
                pt = jnp.exp2(st - lse[qi])
                dpt = _dot_nt(vb, dov)
                dzt = (pt * (dpt - delta[qi])).astype(BF16)
                dv = dv + _dot(pt.astype(BF16), dov)
                dk = dk + _dot(dzt, qv)
                dqb = _dot_tn(dzt, kb)
                if ki == 0:
                    dq_ref[qs, :] = dqb
                else:
                    dq_ref[qs, :] += dqb
            dk_ref[ks, :] = (dk * LN2).astype(BF16)
            dv_ref[ks, :] = dv.astype(BF16)

    full = lambda w, col: pl.BlockSpec((seq, w), col)
    same = lambda b, h: (b, h)
    return pl.pallas_call(
        body, name="flash_bwd", grid=(nb, H),
        in_specs=[full(DQK, same), full(DQK, same), full(DV, lambda b, h: (b, 2 * h + 1)),
                  full(DV, same), full(DV, same), full(DV, same),
                  pl.BlockSpec((8, 128), lambda b, h: (0, 0))],
        out_specs=[full(DQK, same), full(DQK, same), full(DV, same)],
        out_shape=[jax.ShapeDtypeStruct((t, H * DQK), F32), jax.ShapeDtypeStruct((t, H * DQK), BF16),
                   jax.ShapeDtypeStruct((t, H * DV), BF16)],
        compiler_params=_params(("parallel", "parallel")),
    )(q, k, kv, o, do, lse, token)


TAIL_TM = 512


def tail_fwd(y, attn, proj, x2, tgt, gate, g_post, wco, wmo, wout, seq):
    t = y.shape[0]
    nb = t // seq
    tm = min(TAIL_TM, seq)
    tpb = seq // tm

    def body(y_ref, at_ref, p_ref, x_ref, t_ref, gate_ref, gp_ref, wco_ref, wmo_ref, wout_ref,
             o_ref, ya_ref, yb_ref, m_ref, do2_ref, dout_ref, dgate_ref, dgp_ref, loss_ref):
        i = pl.program_id(0)
        bz = p_ref[0].astype(F32)
        ga = p_ref[1].astype(F32)
        gb = p_ref[2].astype(F32)
        ov = (at_ref[...].astype(F32) * (bz * _sig(bz))).astype(BF16)
        o_ref[...] = ov
        ya = _dot(y_ref[...], wco_ref[...])
        yb = _dot(ov, wmo_ref[...])
        ya_ref[...] = ya.astype(BF16)
        yb_ref[...] = yb.astype(BF16)
        mv = (_sig(ga) * ya + _sig(gb) * yb).astype(BF16)
        m_ref[...] = mv
        o2 = _dot(mv, wout_ref[...])
        r = lax.rsqrt(jnp.mean(o2 * o2, axis=-1, keepdims=True) + EPS)
        nrm = o2 * r
        gp = gp_ref[...]
        gate_v = gate_ref[...]
        rn = nrm * gp
        err = x_ref[...] + gate_v * rn - t_ref[...]
        dout = err * (1.0 / D)
        dout_ref[...] = dout
        dn = dout * gate_v * gp
        do2_ref[...] = (r * (dn - nrm * jnp.mean(dn * nrm, axis=-1, keepdims=True))).astype(BF16)

        @pl.when(i % tpb == 0)
        def _():
            dgate_ref[...] = jnp.zeros_like(dgate_ref)

        @pl.when(i == 0)
        def _():
            dgp_ref[...] = jnp.zeros_like(dgp_ref)
            loss_ref[...] = jnp.zeros_like(loss_ref)

        dgate_ref[...] += jnp.sum(dout * rn, axis=0, keepdims=True)
        dgp_ref[...] += jnp.sum(dout * gate_v * nrm, axis=0, keepdims=True)
        loss_ref[...] += 0.5 * jnp.sum(jnp.mean(err * err, axis=-1, keepdims=True), axis=0, keepdims=True)

    row = pl.BlockSpec((tm, D), lambda i: (i, 0))
    per_batch = pl.BlockSpec((None, 1, D), lambda i: (i // tpb, 0, 0))
    vec = pl.BlockSpec((1, D), lambda i: (0, 0))
    wgt = pl.BlockSpec((D, D), lambda i: (0, 0))
    act = jax.ShapeDtypeStruct((t, D), BF16)
    return pl.pallas_call(
        body, name="tail_fwd", grid=(t // tm,),
        in_specs=[row, row, pl.BlockSpec((3, tm, D), lambda i: (0, i, 0)), row, row, per_batch, vec,
                  wgt, wgt, wgt],
        out_specs=[row, row, row, row, row, row, per_batch, vec, pl.BlockSpec((1, 1), lambda i: (0, 0))],
        out_shape=[act, act, act, act, act, jax.ShapeDtypeStruct((t, D), F32),
                   jax.ShapeDtypeStruct((nb, 1, D), F32), jax.ShapeDtypeStruct((1, D), F32),
                   jax.ShapeDtypeStruct((1, 1), F32)],
        compiler_params=_params(("arbitrary",)),
    )(y, attn, proj, x2, tgt, gate, g_post, wco, wmo, wout)


def tail_bwd(do2, proj, ya, yb, attn, wout, wmo, wco):
    t = do2.shape[0]
    tm = min(TAIL_TM, t)

    def body(do2_ref, p_ref, ya_ref, yb_ref, at_ref, wout_ref, wmo_ref, wco_ref,
             dp_ref, dya_ref, dyb_ref, dat_ref, dy_ref):
        bz = p_ref[0].astype(F32)
        ga = p_ref[1].astype(F32)
        gb = p_ref[2].astype(F32)
        dm = _dot_nt(do2_ref[...], wout_ref[...])
        sa = _sig(ga)
        sb = _sig(gb)
        dya = (dm * sa).astype(BF16)
        dyb = (dm * sb).astype(BF16)
        dya_ref[...] = dya
        dyb_ref[...] = dyb
        dp_ref[1] = (dm * ya_ref[...].astype(F32) * (sa * (1.0 - sa))).astype(BF16)
        dp_ref[2] = (dm * yb_ref[...].astype(F32) * (sb * (1.0 - sb))).astype(BF16)
        dov = _dot_nt(dyb, wmo_ref[...])
        sz = _sig(bz)
        dat_ref[...] = (dov * (bz * sz)).astype(BF16)
        dp_ref[0] = (dov * at_ref[...].astype(F32) * (sz * (1.0 + bz * (1.0 - sz)))).astype(BF16)
        dy_ref[...] = _dot_nt(dya, wco_ref[...]).astype(BF16)

    row = pl.BlockSpec((tm, D), lambda i: (i, 0))
    seg3 = pl.BlockSpec((3, tm, D), lambda i: (0, i, 0))
    wgt = pl.BlockSpec((D, D), lambda i: (0, 0))
    act = jax.ShapeDtypeStruct((t, D), BF16)
    return pl.pallas_call(
        body, name="tail_bwd", grid=(t // tm,),
        in_specs=[row, seg3, row, row, row, wgt, wgt, wgt],
        out_specs=[seg3, row, row, row, row],
        out_shape=[jax.ShapeDtypeStruct((NSEG, t, D), BF16), act, act, act, act],
        compiler_params=_params(("parallel",)),
    )(do2, proj, ya, yb, attn, wout, wmo, wco)


def adamw(w, m, v, g, g2, name, token=None):
    rows, cols = w.shape
    tr = rows
    for cand in (256, 128, 64, 32, 16, 8):
        if rows % cand == 0 and rows > cand:
            tr = cand
            break
    has2 = g2 is not None
    n_in = 4 + has2

    def body(*refs):
        w_ref, m_ref, v_ref, g_ref = refs[:4]
        go_ref, d_ref, mo_ref, vo_ref = refs[-4:]
        grad = g_ref[...] + refs[4][...].astype(F32) if has2 else g_ref[...]
        mn = ADAM_B1 * m_ref[...] + (1.0 - ADAM_B1) * grad
        vn = ADAM_B2 * v_ref[...] + (1.0 - ADAM_B2) * (grad * grad)
        m_hat = mn / (1.0 - ADAM_B1 ** ADAM_STEP)
        v_hat = vn / (1.0 - ADAM_B2 ** ADAM_STEP)
        go_ref[...] = grad
        d_ref[...] = -ADAM_LR * (m_hat / (jnp.sqrt(v_hat) + ADAM_EPS) + ADAM_WD * w_ref[...])
        mo_ref[...] = mn
        vo_ref[...] = vn

    blk = pl.BlockSpec((tr, cols), lambda i: (i, 0))
    ins = [w, m, v, g] + ([g2] if has2 else [])
    specs = [blk] * n_in
    if token is not None:
        ins.append(token)
        specs.append(pl.BlockSpec((8, 128), lambda i: (0, 0)))
    return pl.pallas_call(
        body, name=name, grid=(rows // tr,),
        in_specs=specs, out_specs=[blk] * 4,
        out_shape=[jax.ShapeDtypeStruct((rows, cols), F32)] * 4,
        compiler_params=_params(("parallel",)),
    )(*ins)


def adamw_scattered(w, m, v, own, land, me, tr, name, transpose=False):
    slot_rows = land.shape[1]
    cols = land.shape[2]
    rows = slot_rows if transpose else w.shape[0]
    per_slot = slot_rows // tr

    def body(me_ref, w_ref, m_ref, v_ref, own_ref, land_ref, go_ref, d_ref, mo_ref, vo_ref):
        grad = own_ref[...].astype(F32)
        for s in range(8):
            grad = grad + land_ref[s].astype(F32)
        if transpose:
            grad = grad.T
        mn = ADAM_B1 * m_ref[...] + (1.0 - ADAM_B1) * grad
        vn = ADAM_B2 * v_ref[...] + (1.0 - ADAM_B2) * (grad * grad)
        m_hat = mn / (1.0 - ADAM_B1 ** ADAM_STEP)
        v_hat = vn / (1.0 - ADAM_B2 ** ADAM_STEP)
        go_ref[...] = grad
        d_ref[...] = -ADAM_LR * (m_hat / (jnp.sqrt(v_hat) + ADAM_EPS) + ADAM_WD * w_ref[...])
        mo_ref[...] = mn
        vo_ref[...] = vn

    wblk = pl.BlockSpec(w.shape if transpose else (tr, w.shape[1]), lambda i, s: (i, 0))
    return pl.pallas_call(
        body, name=name,
        grid_spec=pltpu.PrefetchScalarGridSpec(
            num_scalar_prefetch=1, grid=(rows // tr,),
            in_specs=[wblk, wblk, wblk,
                      pl.BlockSpec((tr, cols), lambda i, s: (s[0] * per_slot + i, 0)),
                      pl.BlockSpec((8, tr, cols), lambda i, s: (0, i, 0))],
            out_specs=[wblk] * 4),
        out_shape=[jax.ShapeDtypeStruct(w.shape, F32)] * 4,
        compiler_params=_params(),
    )(me, w, m, v, own, land)


def adamw_win(wt, mt, vt, ka, ra, kb, rb):
    rows = wt.shape[0]
    tc = 256
    nh = (D // 2) // tc

    def body(w_ref, m_ref, v_ref, ka_ref, ra_ref, kb_ref, rb_ref, go_ref, d_ref, mo_ref, vo_ref):
        first = pl.program_id(0) < nh
        grad = jnp.where(first, ka_ref[...] + ra_ref[...].astype(F32), kb_ref[...] + rb_ref[...].astype(F32))
        mn = ADAM_B1 * m_ref[...] + (1.0 - ADAM_B1) * grad
        vn = ADAM_B2 * v_ref[...] + (1.0 - ADAM_B2) * (grad * grad)
        m_hat = mn / (1.0 - ADAM_B1 ** ADAM_STEP)
        v_hat = vn / (1.0 - ADAM_B2 ** ADAM_STEP)
        go_ref[...] = grad
        d_ref[...] = -ADAM_LR * (m_hat / (jnp.sqrt(v_hat) + ADAM_EPS) + ADAM_WD * w_ref[...])
        mo_ref[...] = mn
        vo_ref[...] = vn

    blk = pl.BlockSpec((rows, tc), lambda j: (0, j))
    lo = pl.BlockSpec((rows, tc), lambda j: (0, jnp.minimum(j, nh - 1)))
    hi = pl.BlockSpec((rows, tc), lambda j: (0, jnp.maximum(j - nh, 0)))
    return pl.pallas_call(
        body, name="adamw_w_in", grid=(D // tc,),
        in_specs=[blk, blk, blk, lo, lo, hi, hi], out_specs=[blk] * 4,
        out_shape=[jax.ShapeDtypeStruct((rows, D), F32)] * 4,
        compiler_params=_params(("parallel",)),
    )(wt, mt, vt, ka, ra, kb, rb)


_ORD_A = ("x", "y", "c")
_ORD_B = ("y", "x", "c")


def _rows128(a, rows):
    flat = a.reshape(-1)
    return jnp.pad(flat, (0, rows * 128 - flat.shape[0])).reshape(rows, 128)


def kernel(x, c, positions, w_ada, b_ada, g_pre, w_in, conv_w, w_conv_out, g_q, w_uq, g_kv, w_ukv, w_mla_out, w_out, g_post, loss_target, m_w_ada, m_b_ada, m_g_pre, m_w_in, m_conv_w, m_w_conv_out, m_g_q, m_w_uq, m_g_kv, m_w_ukv, m_w_mla_out, m_w_out, m_g_post, v_w_ada, v_b_ada, v_g_pre, v_w_in, v_conv_w, v_w_conv_out, v_g_q, v_w_uq, v_g_kv, v_w_ukv, v_w_mla_out, v_w_out, v_g_post):
    nb, seq, _ = x.shape
    t = nb * seq
    mx, my, mc = lax.axis_index("x"), lax.axis_index("y"), lax.axis_index("c")
    me = 4 * mx + 2 * my + mc
    co = {"x": mx, "y": my, "c": mc}

    x2 = x.reshape(t, D)
    tgt2 = loss_target.reshape(t, D)
    pos2 = positions.reshape(t, 1)

    ada_cols = w_ada.shape[2]
    b_cols = lax.dynamic_slice(b_ada, (0, me * ada_cols), (1, ada_cols))
    c_g, taps_g, mod_g = ada_gather(jnp.pad(c, ((0, 8 - nb), (0, 0))), _rows128(conv_w[0], 8), w_ada[0], b_cols)
    c_all = c_g[:, :nb].reshape(8 * nb, D)
    conv_full = taps_g[:, 0:3].transpose(1, 0, 2).reshape(3, D)
    conv_full8 = jnp.pad(conv_full, ((0, 5), (0, 0)))
    mod = mod_g[:, :nb].transpose(1, 0, 2).reshape(nb, 8 * ada_cols)
    shift = mod[:, 0:D].reshape(nb, 1, D)
    scale = mod[:, D:2 * D].reshape(nb, 1, D)
    gate = mod[:, 2 * D:3 * D].reshape(nb, 1, D)

    wt = w_in[0].T.astype(BF16)
    lo = lax.bitcast_convert_type(wt[:, :D // 2], jnp.uint16).astype(jnp.uint32)
    hi = lax.bitcast_convert_type(wt[:, D // 2:], jnp.uint16).astype(jnp.uint32)
    wt_bits = lax.bitcast_convert_type(lo | (hi << 16), F32)
    wt_bits, mod = lax.optimization_barrier((wt_bits, mod))
    shift = mod[:, 0:D].reshape(nb, 1, D)
    scale = mod[:, D:2 * D].reshape(nb, 1, D)
    gate = mod[:, 2 * D:3 * D].reshape(nb, 1, D)
    q4 = D // 4
    r3rd = wt_bits.shape[0] // 3
    plan = [(0, (k * r3rd, r3rd), (g * q4, q4), (_ORD_A, _ORD_B)[g]) for k in range(3) for g in range(2)]
    gw = allgather_big([wt_bits], plan, "gather_w_in")
    late = [w_conv_out[0].astype(BF16), w_mla_out[0].astype(BF16), w_out[0].astype(BF16),
            jnp.pad(w_uq[0].T.astype(BF16), ((0, DQK - 192), (0, 0))), w_ukv[0].T.astype(BF16)]
    gw0, late = lax.optimization_barrier((gw[0], late))
    late_state, late_token = gather_start(late, "gather_late_start")
    wt_bits_all = gw0.reshape(N_IN, D // 2)

    inv_freq = ROPE_THETA ** (-jnp.arange(0, ROPE, 2, dtype=F32) / ROPE)
    invf = jnp.concatenate([inv_freq, inv_freq, jnp.zeros((128 - ROPE,), F32)]).reshape(1, 128)
    lane = np.arange(128)
    tabs = (invf,
            jnp.asarray(np.where(lane < HALF, -1.0, 0.0).reshape(1, 128), F32),
            jnp.asarray(np.where((lane >= HALF) & (lane < ROPE), 1.0, 0.0).reshape(1, 128), F32))

    h = prenorm_fwd(x2, scale, shift, g_pre, seq)
    proj, wt_p = proj_matmul(h, wt_bits_all, late_token)
    y = conv_fwd(proj, conv_full8, seq)
    gl = gather_wait(late_state, y, "gather_late_wait")
    wco = gl[0].reshape(D, D)
    wmo = gl[1].reshape(D, D)
    wout = gl[2].reshape(D, D)
    wuq_p = gl[3].reshape(H * DQK, QL)
    wukv = gl[4].reshape(H * 256, KVL)
    q_rot, k_cat, kv, qn, kvn = mla_prep_fwd(proj, pos2, g_q, g_kv, wuq_p, wukv, tabs)
    attn, lse = flash_fwd(q_rot, k_cat, kv, nb, seq)
    o, ya, yb, m, do2, dout, dgate, dg_post, loss_part = tail_fwd(
        y, attn, proj, x2, tgt2, gate, g_post, wco, wmo, wout, seq)

    dproj, dya, dyb, dattn, dy = tail_bwd(do2, proj, ya, yb, attn, wout, wmo, wco)
    g_wout = grad_matmul(m, do2, "grad_w_square")
    g_wmo = grad_matmul(o, dyb, "grad_w_square")
    g_wco = grad_matmul(y, dya, "grad_w_square")
    sc1, sc1_tok = scatter_start([g_wco, g_wmo, g_wout], "scatter_out_grads_start")
    dproj, dconv = conv_bwd(dproj, proj, dy, conv_full8, seq)
    dq_rot, dk, dv = flash_bwd(q_rot, k_cat, kv, attn, dattn, lse, nb, seq, sc1_tok)
    dproj, dq, dkv, dg_q, dg_kv = mla_prep_bwd(dproj, proj, dq_rot, dk, dv, pos2, g_q, g_kv, wuq_p, wukv, tabs)
    g_wuq_t = grad_matmul(dq, qn, "grad_w_uq")
    g_wukv_t = grad_matmul(dkv, kvn, "grad_w_ukv")
    sc2, sc2_tok = scatter_start([g_wuq_t, g_wukv_t], "scatter_mla_grads_start")
    g_win_p = win_grad_matmul(h, dproj, sc2_tok)

    g_wt = g_win_p.reshape(2, 2, 2, N_IN // 8, D)
    ords = [("c", "y", "x"), ("c", "x", "y")]
    hc = D // 2
    win_shape = (2, 2, N_IN // 8, hc)
    pick_w = lambda col: (lambda ref, cc: ref.at[:, :, 1 - cc["c"], :, pl.ds(col * hc, hc)])
    which1 = [0, 0]
    picks1 = [pick_w(0), pick_w(1)]
    st1, tok1 = swap_start([g_wt], which1, ["c"] * 2, picks1, [win_shape] * 2, "rs_c_start")
    assert nb == 2
    dh0 = dh_matmul(dproj, wt_p, tok1, seq, 0)
    (g_wt,), r1 = swap_wait(st1, dh0, which1, ["c"] * 2, picks1, "rs_c_wait")
    sel_xyc = jnp.stack([mx, my, mc]).astype(jnp.int32)
    sel2 = [jnp.stack([co[o[2]]]).astype(jnp.int32) for o in ords]
    first = [rs_win_add_first(g_wt, r1[0], sel_xyc, 1, 0, "rs_add_first_0"),
             rs_win_add_first(g_wt, r1[1], sel_xyc, 0, 1, "rs_add_first_1")]
    keep1, send1 = zip(*first)
    all4 = [0, 1]
    none4 = [None] * 2
    axes2 = [o[1] for o in ords]
    st2, tok2 = swap_start(list(send1), all4, axes2, none4, [s.shape for s in send1], "rs_ici1_start")

    dh1 = dh_matmul(dproj, wt_p, tok2, seq, 1)
    gx0, dsh0, dsc0, dgp0 = prenorm_bwd(dh0, x2, dout, scale, g_pre, seq, tok2, 0, None)
    _, r2 = swap_wait(st2, (gx0, dh1), all4, axes2, none4, "rs_ici1_wait")
    keep2, send2 = zip(*[rs_add_second(keep1[a], r2[a], sel2[a], "rs_add_second") for a in range(2)])
    axes3 = [o[2] for o in ords]
    st3, tok3 = swap_start(list(send2), all4, axes3, none4, [s.shape for s in send2], "rs_ici2_start")
    grad_x2, dsh1, dsc1, dgp1 = prenorm_bwd(dh1, x2, dout, scale, g_pre, seq, tok3, 1, gx0)
    dshift = jnp.stack([dsh0, dsh1])
    dscale = jnp.stack([dsc0, dsc1])
    dg_pre = dgp0 + dgp1

    dmod = jnp.concatenate([dshift, dscale, dgate], axis=2).reshape(nb * 3 * D // 128, 128)
    small = jnp.concatenate([
        dmod, _rows128(dg_pre, 8), _rows128(dg_post, 8), _rows128(dg_q, 8), _rows128(dg_kv, 8),
        dconv[0:3].reshape(24, 128), _rows128(loss_part, 8)], axis=0)
    small_g = small_allgather(small, "gather_small_grads")
    sums = slot_sum(small_g)
    dmod_all = small_g[:, 0:48].reshape(8 * nb, 3 * D)
    g_bada = (sums[0:24] + sums[24:48]).reshape(1, 3 * D)
    g_gpre = sums[48:56].reshape(1, D)
    g_gpost = sums[56:64].reshape(1, D)
    g_gq = sums[64:67].reshape(1, QL)
    g_gkv = sums[72:74].reshape(1, KVL)
    g_conv_full = sums[80:104].reshape(3, D)
    loss = sums[104, 0]
    g_conv = lax.dynamic_slice(g_conv_full, (0, me * 128), (3, 128))
    dmod_cols = lax.dynamic_slice(dmod_all, (0, me * ada_cols), (8 * nb, ada_cols))
    g_wada = ada_bwd(c_all, dmod_cols)

    res = {}
    res["w_ada"] = [o_[None] for o_ in adamw(w_ada[0], m_w_ada[0], v_w_ada[0], g_wada, None, "adamw_w_ada", tok3)]

    def pack(b_, gp_, gpo_, gq_, gkv_, cw_):
        return jnp.concatenate([_rows128(b_, 24), _rows128(gp_, 8), _rows128(gpo_, 8), _rows128(gq_, 8),
                                _rows128(gkv_, 8), _rows128(cw_, 8)], axis=0)

    sw = pack(b_ada, g_pre, g_post, g_q, g_kv, conv_w)
    sm = pack(m_b_ada, m_g_pre, m_g_post, m_g_q, m_g_kv, m_conv_w)
    sv = pack(v_b_ada, v_g_pre, v_g_post, v_g_q, v_g_kv, v_conv_w)
    sg = pack(g_bada, g_gpre, g_gpost, g_gq, g_gkv, g_conv)
    small_out = adamw(sw, sm, sv, sg, None, "adamw_small", tok3)

    _, r3 = swap_wait(st3, small_out[0], all4, axes3, none4, "rs_ici2_wait")

    (g_wco, g_wmo, g_wout), (l_wco, l_wmo, l_wout) = scatter_wait(sc1, small_out[1], "scatter_out_grads_wait")
    (g_wuq_t, g_wukv_t), (l_wuq, l_wukv) = scatter_wait(sc2, small_out[2], "scatter_mla_grads_wait")

    res["w_in"] = [o_.T[None] for o_ in adamw_win(w_in[0].T, m_w_in[0].T, v_w_in[0].T,
                                                  keep2[0], r3[0], keep2[1], r3[1])]
    me1 = me.reshape(1).astype(jnp.int32)
    res["w_uq"] = [o_.T[None] for o_ in adamw_scattered(
        w_uq[0].T, m_w_uq[0].T, v_w_uq[0].T, g_wuq_t, l_wuq, me1, 64, "adamw_w_uq")]
    res["w_ukv"] = [o_[None] for o_ in adamw_scattered(
        w_ukv[0], m_w_ukv[0], v_w_ukv[0], g_wukv_t, l_wukv, me1, KVL, "adamw_w_ukv", transpose=True)]
    for nm, wv, mv, vv, gg, ll in (("w_conv_out", w_conv_out, m_w_conv_out, v_w_conv_out, g_wco, l_wco),
                                   ("w_mla_out", w_mla_out, m_w_mla_out, v_w_mla_out, g_wmo, l_wmo),
                                   ("w_out", w_out, m_w_out, v_w_out, g_wout, l_wout)):
        res[nm] = [o_[None] for o_ in adamw_scattered(wv[0], mv[0], vv[0], gg, ll, me1, 128, "adamw_square")]

    def unpack(a):
        return {"b_ada": a[0:24].reshape(1, 3 * D), "g_pre": a[24:32].reshape(1, D),
                "g_post": a[32:40].reshape(1, D), "g_q": a[40:43].reshape(1, QL),
                "g_kv": a[48:50].reshape(1, KVL), "conv_w": a[56:59].reshape(-1)[:3 * 128].reshape(1, 3, 128)}

    for nm in ("b_ada", "g_pre", "g_post", "g_q", "g_kv", "conv_w"):
        res[nm] = [unpack(a)[nm] for a in small_out]

    order = ["w_ada", "b_ada", "g_pre", "w_in", "conv_w", "w_conv_out", "g_q", "w_uq", "g_kv", "w_ukv",
             "w_mla_out", "w_out", "g_post"]
    out = [loss, grad_x2.reshape(nb, seq, D)]
    for k_ in range(4):
        out += [res[nm][k_] for nm in order]
    return tuple(out)
```

```python
import numpy as np
import jax
import jax.numpy as jnp
from jax import lax
from jax.experimental import pallas as pl
from jax.experimental.pallas import tpu as pltpu

F32 = jnp.float32
BF16 = jnp.bfloat16
MESH = pl.DeviceIdType.MESH

D = 1024
H = 8
QL = 384
KVL = 256
ROPE = 64
HALF = ROPE // 2
DQK = 256
DV = 128
NSEG = 8
NP = NSEG * D
EPS = 1e-6
ROPE_THETA = 10000.0
SM_SCALE = (128 + ROPE) ** -0.5
LOG2E = 1.4426950408889634
LN2 = 0.6931471805599453
FLASH_TQ = 512

SEG_BZ, SEG_GA, SEG_GB, SEG_LAT, SEG_V = 0, 1, 2, 3, 4

ADAM_LR = 0.001
ADAM_B1 = 0.9
ADAM_B2 = 0.999
ADAM_EPS = 1e-08
ADAM_WD = 0.01
ADAM_STEP = 10

VMEM_LIMIT = 56 * 1024 * 1024


def _params(sem=None, vmem=VMEM_LIMIT):
    kw = dict(vmem_limit_bytes=vmem)
    if sem is not None:
        kw["dimension_semantics"] = sem
    return pltpu.CompilerParams(**kw)


def _sig(v):
    return 0.5 * jnp.tanh(0.5 * v) + 0.5


def _dot(a, b):
    return jnp.dot(a, b, preferred_element_type=F32)


def _dot_nt(a, b):
    return lax.dot_general(a, b, (((1,), (1,)), ((), ())), preferred_element_type=F32)


def _dot_tn(a, b):
    return lax.dot_general(a, b, (((0,), (0,)), ((), ())), preferred_element_type=F32)


_AXIS_POS = {"x": 0, "y": 1, "c": 2}


def _coords():
    return lax.axis_index("x"), lax.axis_index("y"), lax.axis_index("c")


def _partner(axis):
    p = list(_coords())
    p[_AXIS_POS[axis]] = 1 - p[_AXIS_POS[axis]]
    return tuple(p)


def small_allgather(v, name):
    rows = v.shape[0]

    def body(v_ref, out_ref, send_sems, recv_sems):
        x, y, c = _coords()
        me = 4 * x + 2 * y + c
        out_ref[me] = v_ref[...]
        copies = []
        for k in range(1, 8):
            peer = (1 - x if k & 4 else x, 1 - y if k & 2 else y, 1 - c if k & 1 else c)
            cp = pltpu.make_async_remote_copy(
                src_ref=v_ref, dst_ref=out_ref.at[me],
                send_sem=send_sems.at[k - 1], recv_sem=recv_sems.at[k - 1],
                device_id=peer, device_id_type=MESH)
            cp.start()
            copies.append(cp)
        for cp in copies:
            cp.wait()

    return pl.pallas_call(
        body, name=name,
        out_shape=jax.ShapeDtypeStruct((8, rows, 128), F32),
        in_specs=[pl.BlockSpec(memory_space=pltpu.VMEM)],
        out_specs=pl.BlockSpec(memory_space=pltpu.VMEM),
        scratch_shapes=[pltpu.SemaphoreType.DMA((7,)), pltpu.SemaphoreType.DMA((7,))],
    )(v)


def _own_block_placed(s):
    x, y, c = _coords()
    return lax.dynamic_update_slice(lax.empty((2, 2, 2) + s.shape, s.dtype), s[None, None, None],
                                    (x, y, c) + (0,) * s.ndim)


def allgather_big(arrs, plan, name):
    n = len(arrs)
    m = len(plan)
    nst = len(plan[0][3])

    def body(*refs):
        ins, outs = refs[n:2 * n], refs[2 * n:3 * n]
        send_sems, recv_sems = refs[3 * n:]
        x, y, c = _coords()
        co = {"x": x, "y": y, "c": c}

        def window(ref, lead, rows, cols):
            win = tuple(slice(None) if w is None else pl.ds(w[0], w[1]) for w in (rows, cols))
            return ref.at[tuple(lead) + win]

        def held(e, free):
            i, rows, cols, _ = plan[e]
            lead = [slice(None) if ax in free else co[ax] for ax in ("x", "y", "c")]
            return window(outs[i], lead, rows, cols)

        def rcopy(e, stage, src, dst, axis):
            return pltpu.make_async_remote_copy(
                src_ref=src, dst_ref=dst,
                send_sem=send_sems.at[e, stage], recv_sem=recv_sems.at[e, stage],
                device_id=_partner(axis), device_id_type=MESH)

        stages = [[] for _ in range(nst)]
        for e, (i, rows, cols, order) in enumerate(plan):
            cp = rcopy(e, 0, window(ins[i], [], rows, cols), held(e, ()), order[0])
            cp.start()
            stages[0].append(cp)
        for s in range(1, nst):
            for e, (i, rows, cols, order) in enumerate(plan):
                stages[s - 1][e].wait_recv()
                blk = held(e, order[:s])
                cp = rcopy(e, s, blk, blk, order[s])
                cp.start()
                stages[s].append(cp)
        for e in range(m):
            stages[nst - 1][e].wait_recv()
        for e in range(m):
            for s in range(nst):
                stages[s][e].wait_send()

    any_spec = pl.BlockSpec(memory_space=pl.ANY)
    lands = [_own_block_placed(a) for a in arrs]
    return pl.pallas_call(
        body, name=name,
        out_shape=[jax.ShapeDtypeStruct(l.shape, l.dtype) for l in lands],
        in_specs=[any_spec] * (2 * n),
        out_specs=[any_spec] * n,
        input_output_aliases={i: i for i in range(n)},
        scratch_shapes=[pltpu.SemaphoreType.DMA((m, nst)), pltpu.SemaphoreType.DMA((m, nst))],
    )(*lands, *arrs)


_HBM =pl.BlockSpec(memory_space=pltpu.HBM)
_SEM = pl.BlockSpec(memory_space=pltpu.SEMAPHORE)


def _swap_copies(srcs, lands, send_sems, recv_sems, axes, picks):
    x, y, c = _coords()
    co = {"x": x, "y": y, "c": c}
    return [pltpu.make_async_remote_copy(
        src_ref=srcs[a] if picks[a] is None else picks[a](srcs[a], co), dst_ref=lands[a],
        send_sem=send_sems.at[a], recv_sem=recv_sems.at[a],
        device_id=_partner(axes[a]), device_id_type=MESH) for a in range(len(srcs))]


def swap_start(arrs, which, axes, picks, out_shapes, name):
    ns, n = len(arrs), len(which)

    def body(*refs):
        srcs, lands = refs[:ns], refs[ns:ns + n]
        send_sems, recv_sems = refs[ns + n:ns + n + 2]
        token = refs[-1]
        for cp in _swap_copies([srcs[i] for i in which], lands, send_sems, recv_sems, axes, picks):
            cp.start()
        token[...] = jnp.zeros_like(token)

    lands = [lax.empty(s, arrs[i].dtype) for s, i in zip(out_shapes, which)]
    ops = [pltpu.with_memory_space_constraint(a, pltpu.HBM) for a in list(arrs) + lands]
    out = pl.pallas_call(
        body, name=name,
        out_shape=[pltpu.SemaphoreType.DMA((n,)), pltpu.SemaphoreType.DMA((n,))]
        + [pltpu.HBM(o.shape, o.dtype) for o in ops] + [jax.ShapeDtypeStruct((8, 128), F32)],
        in_specs=[_HBM] * (ns + n),
        out_specs=[_SEM, _SEM] + [_HBM] * (ns + n) + [pl.BlockSpec(memory_space=pltpu.VMEM)],
        input_output_aliases={i: 2 + i for i in range(ns + n)},
        compiler_params=pltpu.CompilerParams(has_side_effects=pltpu.SideEffectType.DATAFLOW_SIDE_EFFECTING),
    )(*ops)
    return out[:-1], out[-1]


def swap_wait(state, after, which, axes, picks, name):
    n = len(which)
    ns = len(state) - 2 - n

    def body(*refs):
        srcs, lands = refs[:ns], refs[ns:ns + n]
        send_sems, recv_sems = refs[ns + n:ns + n + 2]
        for cp in _swap_copies([srcs[i] for i in which], lands, send_sems, recv_sems, axes, picks):
            cp.wait_send()
            cp.wait_recv()

    thru = list(state[2:])
    after = list(after) if isinstance(after, (list, tuple)) else [after]
    out = pl.pallas_call(
        body, name=name,
        out_shape=[pltpu.HBM(o.shape, o.dtype) for o in thru],
        in_specs=[_HBM] * (ns + n) + [_SEM, _SEM] + [pl.BlockSpec(memory_space=pl.ANY)] * len(after),
        out_specs=[_HBM] * (ns + n),
        input_output_aliases={i: i for i in range(ns + n)},
        compiler_params=pltpu.CompilerParams(has_side_effects=pltpu.SideEffectType.DATAFLOW_SIDE_EFFECTING),
    )(*thru, state[0], state[1], *after)
    return out[:ns], out[ns:]


def _gather_copies(shards, lands, send_sems, recv_sems):
    x, y, c = _coords()
    copies = []
    for a in range(len(shards)):
        for k in range(1, 8):
            peer = (1 - x if k & 4 else x, 1 - y if k & 2 else y, 1 - c if k & 1 else c)
            copies.append(pltpu.make_async_remote_copy(
                src_ref=shards[a], dst_ref=lands[a].at[x, y, c],
                send_sem=send_sems.at[7 * a + k - 1], recv_sem=recv_sems.at[7 * a + k - 1],
                device_id=peer, device_id_type=MESH))
    return copies


def gather_start(shards, name):
    n = len(shards)
    x, y, c = _coords()

    def body(*refs):
        srcs, lands = refs[:n], refs[n:2 * n]
        send_sems, recv_sems = refs[2 * n:2 * n + 2]
        token = refs[-1]
        for cp in _gather_copies(srcs, lands, send_sems, recv_sems):
            cp.start()
        token[...] = jnp.zeros_like(token)

    lands = [_own_block_placed(s) for s in shards]
    ops = [pltpu.with_memory_space_constraint(a, pltpu.HBM) for a in list(shards) + lands]
    out = pl.pallas_call(
        body, name=name,
        out_shape=[pltpu.SemaphoreType.DMA((7 * n,)), pltpu.SemaphoreType.DMA((7 * n,))]
        + [pltpu.HBM(o.shape, o.dtype) for o in ops] + [jax.ShapeDtypeStruct((8, 128), F32)],
        in_specs=[_HBM] * (2 * n),
        out_specs=[_SEM, _SEM] + [_HBM] * (2 * n) + [pl.BlockSpec(memory_space=pltpu.VMEM)],
        input_output_aliases={i: 2 + i for i in range(2 * n)},
        compiler_params=pltpu.CompilerParams(has_side_effects=pltpu.SideEffectType.DATAFLOW_SIDE_EFFECTING),
    )(*ops)
    return out[:-1], out[-1]


def gather_wait(state, after, name):
    n = (len(state) - 2) // 2

    def body(*refs):
        srcs, lands = refs[:n], refs[n:2 * n]
        send_sems, recv_sems = refs[2 * n:2 * n + 2]
        for cp in _gather_copies(srcs, lands, send_sems, recv_sems):
            cp.wait_send()
            cp.wait_recv()

    thru = list(state[2:])
    out = pl.pallas_call(
        body, name=name,
        out_shape=[pltpu.HBM(o.shape, o.dtype) for o in thru],
        in_specs=[_HBM] * (2 * n) + [_SEM, _SEM, pl.BlockSpec(memory_space=pl.ANY)],
        out_specs=[_HBM] * (2 * n),
        input_output_aliases={i: i for i in range(2 * n)},
        compiler_params=pltpu.CompilerParams(has_side_effects=pltpu.SideEffectType.DATAFLOW_SIDE_EFFECTING),
    )(*thru, state[0], state[1], after)
    return out[n:]


def _scatter_copies(grads, lands, send_sems, recv_sems):
    x, y, c = _coords()
    me = 4 * x + 2 * y + c
    copies = []
    for a in range(len(grads)):
        r = grads[a].shape[0] // 8
        for k in range(1, 8):
            px, py, pc = (1 - x if k & 4 else x, 1 - y if k & 2 else y, 1 - c if k & 1 else c)
            rows = pl.ds(pl.multiple_of((4 * px + 2 * py + pc) * r, r), r)
            copies.append(pltpu.make_async_remote_copy(
                src_ref=grads[a].at[rows], dst_ref=lands[a].at[me],
                send_sem=send_sems.at[7 * a + k - 1], recv_sem=recv_sems.at[7 * a + k - 1],
                device_id=(px, py, pc), device_id_type=MESH))
    return copies


def scatter_start(grads, name):
    n = len(grads)

    def body(*refs):
        srcs, lands = refs[:n], refs[n:2 * n]
        send_sems, recv_sems = refs[2 * n:2 * n + 2]
        token = refs[-1]
        for cp in _scatter_copies(srcs, lands, send_sems, recv_sems):
            cp.start()
        token[...] = jnp.zeros_like(token)

    lands = [jnp.zeros((8, g.shape[0] // 8, g.shape[1]), g.dtype) for g in grads]
    ops = [pltpu.with_memory_space_constraint(a, pltpu.HBM) for a in list(grads) + lands]
    out = pl.pallas_call(
        body, name=name,
        out_shape=[pltpu.SemaphoreType.DMA((7 * n,)), pltpu.SemaphoreType.DMA((7 * n,))]
        + [pltpu.HBM(o.shape, o.dtype) for o in ops] + [jax.ShapeDtypeStruct((8, 128), F32)],
        in_specs=[_HBM] * (2 * n),
        out_specs=[_SEM, _SEM] + [_HBM] * (2 * n) + [pl.BlockSpec(memory_space=pltpu.VMEM)],
        input_output_aliases={i: 2 + i for i in range(2 * n)},
        compiler_params=pltpu.CompilerParams(has_side_effects=pltpu.SideEffectType.DATAFLOW_SIDE_EFFECTING),
    )(*ops)
    return out[:-1], out[-1]


def scatter_wait(state, after, name):
    n = (len(state) - 2) // 2

    def body(*refs):
        srcs, lands = refs[:n], refs[n:2 * n]
        send_sems, recv_sems = refs[2 * n:2 * n + 2]
        for cp in _scatter_copies(srcs, lands, send_sems, recv_sems):
            cp.wait_send()
            cp.wait_recv()

    thru = list(state[2:])
    after = list(after) if isinstance(after, (list, tuple)) else [after]
    out = pl.pallas_call(
        body, name=name,
        out_shape=[pltpu.HBM(o.shape, o.dtype) for o in thru],
        in_specs=[_HBM] * (2 * n) + [_SEM, _SEM] + [pl.BlockSpec(memory_space=pl.ANY)] * len(after),
        out_specs=[_HBM] * (2 * n),
        input_output_aliases={i: i for i in range(2 * n)},
        compiler_params=pltpu.CompilerParams(has_side_effects=pltpu.SideEffectType.DATAFLOW_SIDE_EFFECTING),
    )(*thru, state[0], state[1], *after)
    return out[:n], out[n:]


def rs_win_add_first(g, r, sel, next_dim, col, name):
    rows, cols = r.shape[2:]

    def body(sel_ref, gk_ref, rk_ref, gs_ref, rs_ref, keep_ref, send_ref):
        keep_ref[...] = gk_ref[...] + rk_ref[...]
        send_ref[...] = (gs_ref[...] + rs_ref[...]).astype(BF16)

    def g_map(flip):
        def f(j, s):
            nxt = 1 - s[next_dim] if flip else s[next_dim]
            return (nxt, j, s[2], 0, col) if next_dim == 0 else (j, nxt, s[2], 0, col)
        return f

    def r_map(flip):
        def f(j, s):
            nxt = 1 - s[next_dim] if flip else s[next_dim]
            return (nxt, j, 0, 0) if next_dim == 0 else (j, nxt, 0, 0)
        return f

    gblk = (None, None, None, rows, cols)
    rblk = (None, None, rows, cols)
    oblk = (None, rows, cols)
    return pl.pallas_call(
        body, name=name,
        grid_spec=pltpu.PrefetchScalarGridSpec(
            num_scalar_prefetch=1, grid=(2,),
            in_specs=[pl.BlockSpec(gblk, g_map(False)), pl.BlockSpec(rblk, r_map(False)),
                      pl.BlockSpec(gblk, g_map(True)), pl.BlockSpec(rblk, r_map(True))],
            out_specs=[pl.BlockSpec(oblk, lambda j, s: (j, 0, 0)),
                       pl.BlockSpec(oblk, lambda j, s: (j, 0, 0))]),
        out_shape=[jax.ShapeDtypeStruct((2, rows, cols), F32),
                   jax.ShapeDtypeStruct((2, rows, cols), BF16)],
        compiler_params=_params(),
    )(sel, g, r, g, r)


def rs_add_second(k, r, sel, name):
    _, rows, cols = k.shape
    tr = rows // 2 if rows % 32 == 0 else rows
    nt = rows // tr

    def body(sel_ref, kk_ref, rk_ref, ks_ref, rs_ref, keep_ref, send_ref):
        keep_ref[...] = kk_ref[...] + rk_ref[...].astype(F32)
        send_ref[...] = (ks_ref[...] + rs_ref[...].astype(F32)).astype(BF16)

    blk = (None, tr, cols)
    oblk = (tr, cols)
    return pl.pallas_call(
        body, name=name,
        grid_spec=pltpu.PrefetchScalarGridSpec(
            num_scalar_prefetch=1, grid=(nt,),
            in_specs=[
                pl.BlockSpec(blk, lambda i, s: (s[0], i, 0)),
                pl.BlockSpec(blk, lambda i, s: (s[0], i, 0)),
                pl.BlockSpec(blk, lambda i, s: (1 - s[0], i, 0)),
                pl.BlockSpec(blk, lambda i, s: (1 - s[0], i, 0)),
            ],
            out_specs=[pl.BlockSpec(oblk, lambda i, s: (i, 0)),
                       pl.BlockSpec(oblk, lambda i, s: (i, 0))]),
        out_shape=[jax.ShapeDtypeStruct((rows, cols), F32),
                   jax.ShapeDtypeStruct((rows, cols), BF16)],
        compiler_params=_params(),
    )(sel, k, r, k, r)


SEG_ROWS = (4800, 5824, 6848, 4096, 0, 1024, 2048, 3072)
LAT_ROWS = QL + KVL + ROPE
N_IN = 7872


def _seg_row(j):
    return pl.multiple_of(jnp.where(j < 3, 4800 + 1024 * j, jnp.where(j == 3, 4096, (j - 4) * 1024)), 8)


def proj_matmul(h, wt_bits, token):
    t = h.shape[0]
    tm = min(2048, t)

    def body(h_ref, w_hbm, tok_ref, o_ref, wt_ref, buf, sems):
        j = pl.program_id(0)
        slot = j % 2

        def fetch(seg, into):
            return pltpu.make_async_copy(w_hbm.at[pl.ds(_seg_row(seg), D)], buf.at[into], sems.at[into])

        @pl.when(pl.program_id(1) == 0)
        def _():
            @pl.when(j == 0)
            def _():
                fetch(j, slot).start()

            fetch(j, slot).wait()

            @pl.when(j + 1 < NSEG)
            def _():
                fetch(j + 1, 1 - slot).start()

            bits = pltpu.bitcast(buf[slot], jnp.uint32)
            row = lax.broadcasted_iota(jnp.int32, (D, D // 2), 0)
            live = jnp.logical_or(j != SEG_LAT, row < LAT_ROWS)
            lo = pltpu.bitcast(bits << 16, F32)
            hi = pltpu.bitcast(bits & jnp.uint32(0xFFFF0000), F32)
            wt_ref[:, :D // 2] = jnp.where(live, lo, 0.0).astype(BF16)
            wt_ref[:, D // 2:] = jnp.where(live, hi, 0.0).astype(BF16)

        o_ref[...] = _dot_nt(h_ref[...], wt_ref[...]).astype(BF16)

    return pl.pallas_call(
        body, name="proj_matmul", grid=(NSEG, t // tm),
        in_specs=[pl.BlockSpec((tm, D), lambda j, i: (i, 0)),
                  pl.BlockSpec(memory_space=pl.ANY),
                  pl.BlockSpec((8, 128), lambda j, i: (0, 0))],
        out_specs=[pl.BlockSpec((None, tm, D), lambda j, i: (j, i, 0)),
                   pl.BlockSpec((D, D), lambda j, i: (j, 0))],
        out_shape=[jax.ShapeDtypeStruct((NSEG, t, D), BF16), jax.ShapeDtypeStruct((NP, D), BF16)],
        scratch_shapes=[pltpu.VMEM((2, D, D // 2), F32), pltpu.SemaphoreType.DMA((2,))],
        compiler_params=_params(("arbitrary", "arbitrary")),
    )(h, wt_bits, token)


def dh_matmul(dproj, wt, token, seq, b):
    tm = min(1024, seq)
    nblk = seq // tm

    per = 2

    def body(b_ref, d_ref, w_ref, tok_ref, o_ref, acc_ref):
        k = pl.program_id(1)

        @pl.when(k == 0)
        def _():
            acc_ref[...] = jnp.zeros_like(acc_ref)

        part = _dot(d_ref[0], w_ref[0:D, :])
        for j in range(1, per):
            part = part + _dot(d_ref[j], w_ref[j * D:(j + 1) * D, :])
        acc_ref[...] += part

        @pl.when(k == NSEG // per - 1)
        def _():
            o_ref[...] = acc_ref[...]

    return pl.pallas_call(
        body, name="dh_matmul",
        grid_spec=pltpu.PrefetchScalarGridSpec(
            num_scalar_prefetch=1, grid=(nblk, NSEG // per),
            in_specs=[pl.BlockSpec((per, tm, D), lambda i, k, s: (k, s[0] * nblk + i, 0)),
                      pl.BlockSpec((per * D, D), lambda i, k, s: (k, 0)),
                      pl.BlockSpec((8, 128), lambda i, k, s: (0, 0))],
            out_specs=pl.BlockSpec((tm, D), lambda i, k, s: (i, 0)),
            scratch_shapes=[pltpu.VMEM((tm, D), F32)]),
        out_shape=jax.ShapeDtypeStruct((seq, D), F32),
        compiler_params=_params(("parallel", "arbitrary")),
    )(jnp.full((1,), b, jnp.int32), dproj, wt, token)


def win_grad_matmul(h, dproj, token):
    t = h.shape[0]
    tk = min(2048, t)
    nk = t // tk

    def body(h_ref, d_ref, tok_ref, o_hbm, acc_ref, sem):
        j = pl.program_id(0)
        k = pl.program_id(1)

        @pl.when(k == 0)
        def _():
            acc_ref[...] = jnp.zeros_like(acc_ref)

        acc_ref[...] += _dot_tn(d_ref[...], h_ref[...])

        @pl.when(jnp.logical_and(k == nk - 1, j != SEG_LAT))
        def _():
            cp = pltpu.make_async_copy(acc_ref, o_hbm.at[pl.ds(_seg_row(j), D)], sem)
            cp.start()
            cp.wait()

        @pl.when(jnp.logical_and(k == nk - 1, j == SEG_LAT))
        def _():
            cp = pltpu.make_async_copy(acc_ref.at[pl.ds(0, LAT_ROWS)],
                                       o_hbm.at[pl.ds(SEG_ROWS[SEG_LAT], LAT_ROWS)], sem)
            cp.start()
            cp.wait()

    return pl.pallas_call(
        body, name="win_grad_matmul", grid=(NSEG, nk),
        in_specs=[pl.BlockSpec((tk, D), lambda j, k: (k, 0)),
                  pl.BlockSpec((None, tk, D), lambda j, k: (j, k, 0)),
                  pl.BlockSpec((8, 128), lambda j, k: (0, 0))],
        out_specs=pl.BlockSpec(memory_space=pl.ANY),
        out_shape=jax.ShapeDtypeStruct((N_IN, D), F32),
        scratch_shapes=[pltpu.VMEM((D, D), F32), pltpu.SemaphoreType.DMA],
        compiler_params=_params(("arbitrary", "arbitrary")),
    )(h, dproj, token)


def grad_matmul(a, b, name):
    t, m = a.shape
    n = b.shape[1]
    tk = min(1024, t)
    nk = t // tk

    def body(a_ref, b_ref, o_ref, acc_ref):
        k = pl.program_id(0)

        @pl.when(k == 0)
        def _():
            acc_ref[...] = jnp.zeros_like(acc_ref)

        acc_ref[...] += _dot_tn(a_ref[...], b_ref[...])

        @pl.when(k == nk - 1)
        def _():
            o_ref[...] = acc_ref[...].astype(BF16)

    return pl.pallas_call(
        body, name=name, grid=(nk,),
        in_specs=[pl.BlockSpec((tk, m), lambda k: (k, 0)),
                  pl.BlockSpec((tk, n), lambda k: (k, 0))],
        out_specs=pl.BlockSpec((m, n), lambda k: (0, 0)),
        out_shape=jax.ShapeDtypeStruct((m, n), BF16),
        scratch_shapes=[pltpu.VMEM((m, n), F32)],
        compiler_params=_params(("arbitrary",)),
    )(a, b)


def ada_gather(c8, taps8, w_ada, b_cols):
    cols = w_ada.shape[1]

    def body(c_ref, t_ref, w_ref, b_ref, call_ref, tall_ref, mod_ref, part_ref, send_sems, recv_sems):
        x, y, c = _coords()
        me = 4 * x + 2 * y + c
        peers = [(1 - x if k & 4 else x, 1 - y if k & 2 else y, 1 - c if k & 1 else c) for k in range(1, 8)]

        def rcopy(n, src, dst, peer):
            return pltpu.make_async_remote_copy(src_ref=src, dst_ref=dst, send_sem=send_sems.at[n],
                                                recv_sem=recv_sems.at[n], device_id=peer, device_id_type=MESH)

        call_ref[me] = c_ref[...]
        tall_ref[me] = t_ref[...]
        first = []
        for k, peer in enumerate(peers):
            first += [rcopy(k, c_ref, call_ref.at[me], peer), rcopy(7 + k, t_ref, tall_ref.at[me], peer)]
        for cp in first:
            cp.start()
        for cp in first:
            cp.wait()
        rows = call_ref[...].reshape(64, D).astype(BF16)
        part_ref[...] = _dot(rows, w_ref[...].astype(BF16)) + b_ref[...]
        mod_ref[me] = part_ref[pl.ds(pl.multiple_of(8 * me, 8), 8), :]
        second = []
        for k, (px, py, pc) in enumerate(peers):
            theirs = part_ref.at[pl.ds(pl.multiple_of(8 * (4 * px + 2 * py + pc), 8), 8)]
            second.append(rcopy(14 + k, theirs, mod_ref.at[me], (px, py, pc)))
        for cp in second:
            cp.start()
        for cp in second:
            cp.wait()

    vm = pl.BlockSpec(memory_space=pltpu.VMEM)
    return pl.pallas_call(
        body, name="ada_gather",
        out_shape=[jax.ShapeDtypeStruct((8, 8, D), F32), jax.ShapeDtypeStruct((8, 8, 128), F32),
                   jax.ShapeDtypeStruct((8, 8, cols), F32)],
        in_specs=[vm] * 4, out_specs=[vm] * 3,
        scratch_shapes=[pltpu.VMEM((64, cols), F32), pltpu.SemaphoreType.DMA((21,)),
                        pltpu.SemaphoreType.DMA((21,))],
        compiler_params=_params(),
    )(c8, taps8, w_ada, b_cols)


def ada_bwd(c_all, dmod_cols):
    def body(c_ref, d_ref, o_ref):
        o_ref[...] = _dot_tn(c_ref[...].astype(BF16), d_ref[...].astype(BF16))

    return pl.pallas_call(
        body, name="ada_bwd",
        out_shape=jax.ShapeDtypeStruct((c_all.shape[1], dmod_cols.shape[1]), F32),
        compiler_params=_params(),
    )(c_all, dmod_cols)


def slot_sum(g):
    def body(g_ref, o_ref):
        acc = g_ref[0]
        for s in range(1, 8):
            acc = acc + g_ref[s]
        o_ref[...] = acc

    return pl.pallas_call(
        body, name="slot_sum",
        out_shape=jax.ShapeDtypeStruct(g.shape[1:], F32),
    )(g)


def prenorm_fwd(x2, scale, shift, g_pre, seq):
    t = x2.shape[0]
    tm = min(512, seq)
    tpb = seq // tm

    def body(x_ref, sc_ref, sh_ref, g_ref, h_ref):
        xv = x_ref[...]
        r = lax.rsqrt(jnp.mean(xv * xv, axis=-1, keepdims=True) + EPS)
        hv = (xv * r * g_ref[...]) * (1.0 + sc_ref[...]) + sh_ref[...]
        h_ref[...] = hv.astype(BF16)

    per_batch = pl.BlockSpec((None, 1, D), lambda i: (i // tpb, 0, 0))
    return pl.pallas_call(
        body, name="prenorm_fwd", grid=(t // tm,),
        in_specs=[pl.BlockSpec((tm, D), lambda i: (i, 0)), per_batch, per_batch,
                  pl.BlockSpec((1, D), lambda i: (0, 0))],
        out_specs=pl.BlockSpec((tm, D), lambda i: (i, 0)),
        out_shape=jax.ShapeDtypeStruct((t, D), BF16),
        compiler_params=_params(("parallel",)),
    )(x2, scale, shift, g_pre)


def prenorm_bwd(dh, x2, dout, scale, g_pre, seq, token, b, gx_prev):
    t = x2.shape[0]
    tm = min(512, seq)
    tpb = seq // tm
    if gx_prev is None:
        gx_prev = lax.empty((t, D), F32)

    def body(b_ref, dh_ref, x_ref, do_ref, sc_ref, g_ref, tok_ref, gxp_ref, gx_ref, dsh_ref, dsc_ref, dg_ref):
        i = pl.program_id(0)
        xv = x_ref[...]
        dhv = dh_ref[...]
        g = g_ref[...]
        r = lax.rsqrt(jnp.mean(xv * xv, axis=-1, keepdims=True) + EPS)
        nrm = xv * r
        dxn = dhv * (1.0 + sc_ref[...])
        dn = dxn * g
        dx = r * (dn - nrm * jnp.mean(dn * nrm, axis=-1, keepdims=True))
        gx_ref[...] = dx + do_ref[...]

        @pl.when(i == 0)
        def _():
            dsh_ref[...] = jnp.zeros_like(dsh_ref)
            dsc_ref[...] = jnp.zeros_like(dsc_ref)
            dg_ref[...] = jnp.zeros_like(dg_ref)

        dsh_ref[...] += jnp.sum(dhv, axis=0, keepdims=True)
        dsc_ref[...] += jnp.sum(dhv * (nrm * g), axis=0, keepdims=True)
        dg_ref[...] += jnp.sum(dxn * nrm, axis=0, keepdims=True)

    row = pl.BlockSpec((tm, D), lambda i, s: (i, 0))
    grow = pl.BlockSpec((tm, D), lambda i, s: (s[0] * tpb + i, 0))
    per_batch = pl.BlockSpec((None, 1, D), lambda i, s: (s[0], 0, 0))
    vec = pl.BlockSpec((1, D), lambda i, s: (0, 0))
    return pl.pallas_call(
        body, name="prenorm_bwd",
        grid_spec=pltpu.PrefetchScalarGridSpec(
            num_scalar_prefetch=1, grid=(tpb,),
            in_specs=[row, grow, grow, per_batch, vec, pl.BlockSpec((8, 128), lambda i, s: (0, 0)),
                      pl.BlockSpec(memory_space=pl.ANY)],
            out_specs=[grow, vec, vec, vec]),
        out_shape=[jax.ShapeDtypeStruct((t, D), F32), jax.ShapeDtypeStruct((1, D), F32),
                   jax.ShapeDtypeStruct((1, D), F32), jax.ShapeDtypeStruct((1, D), F32)],
        input_output_aliases={7: 0},
        compiler_params=_params(("arbitrary",)),
    )(jnp.full((1,), b, jnp.int32), dh, x2, dout, scale, g_pre, token, gx_prev)


CONV_TC = 128


def _shift_down(u, k, rows):
    idx = lax.broadcasted_iota(jnp.int32, u.shape, 0)
    return jnp.where(idx >= k, pltpu.roll(u, k, 0), 0.0)


def _shift_up(u, k, rows):
    idx = lax.broadcasted_iota(jnp.int32, u.shape, 0)
    return jnp.where(idx < rows - k, pltpu.roll(u, rows - k, 0), 0.0)


def conv_fwd(proj, conv_w, seq):
    t = proj.shape[1]
    nb = t // seq

    def body(p_ref, w_ref, y_ref):
        av = p_ref[0].astype(F32)
        ab = p_ref[1].astype(F32)
        ac = p_ref[2].astype(F32)
        az = p_ref[3].astype(F32)
        w = w_ref[...]
        u = ac * av
        y1 = _shift_down(u, 2, seq) * w[0:1] + _shift_down(u, 1, seq) * w[1:2] + u * w[2:3]
        y_ref[...] = (ab * y1 * (az * _sig(az))).astype(BF16)

    return pl.pallas_call(
        body, name="conv_fwd", grid=(nb, D // CONV_TC),
        in_specs=[pl.BlockSpec((4, seq, CONV_TC), lambda b, ci: (1, b, ci)),
                  pl.BlockSpec((8, CONV_TC), lambda b, ci: (0, ci))],
        out_specs=pl.BlockSpec((seq, CONV_TC), lambda b, ci: (b, ci)),
        out_shape=jax.ShapeDtypeStruct((t, D), BF16),
        compiler_params=_params(("parallel", "parallel")),
    )(proj, conv_w)


def conv_bwd(dproj, proj, dy, conv_w, seq):
    t = proj.shape[1]
    nb = t // seq

    def body(dp_in_ref, p_ref, dy_ref, w_ref, dp_ref, dw_ref):
        b = pl.program_id(1)
        av = p_ref[0].astype(F32)
        ab = p_ref[1].astype(F32)
        ac = p_ref[2].astype(F32)
        az = p_ref[3].astype(F32)
        dyv = dy_ref[...].astype(F32)
        w = w_ref[...]
        u = ac * av
        u1 = _shift_down(u, 1, seq)
        u2 = _shift_down(u, 2, seq)
        y1 = u2 * w[0:1] + u1 * w[1:2] + u * w[2:3]
        sz = _sig(az)
        silu = az * sz
        dy1 = dyv * ab * silu
        du = dy1 * w[2:3] + _shift_up(dy1, 1, seq) * w[1:2] + _shift_up(dy1, 2, seq) * w[0:1]
        dp_ref[0] = (du * ac).astype(BF16)
        dp_ref[1] = (dyv * y1 * silu).astype(BF16)
        dp_ref[2] = (du * av).astype(BF16)
        dp_ref[3] = (dyv * ab * y1 * (sz * (1.0 + az * (1.0 - sz)))).astype(BF16)

        @pl.when(b == 0)
        def _():
            dw_ref[...] = jnp.zeros_like(dw_ref)

        dw_ref[0:1, :] += jnp.sum(dy1 * u2, axis=0, keepdims=True)
        dw_ref[1:2, :] += jnp.sum(dy1 * u1, axis=0, keepdims=True)
        dw_ref[2:3, :] += jnp.sum(dy1 * u, axis=0, keepdims=True)

    return pl.pallas_call(
        body, name="conv_bwd", grid=(D // CONV_TC, nb),
        in_specs=[pl.BlockSpec(memory_space=pl.ANY),
                  pl.BlockSpec((4, seq, CONV_TC), lambda ci, b: (1, b, ci)),
                  pl.BlockSpec((seq, CONV_TC), lambda ci, b: (b, ci)),
                  pl.BlockSpec((8, CONV_TC), lambda ci, b: (0, ci))],
        out_specs=[pl.BlockSpec((4, seq, CONV_TC), lambda ci, b: (1, b, ci)),
                   pl.BlockSpec((8, CONV_TC), lambda ci, b: (0, ci))],
        out_shape=[jax.ShapeDtypeStruct(dproj.shape, BF16),
                   jax.ShapeDtypeStruct((8, D), F32)],
        input_output_aliases={0: 0},
        compiler_params=_params(("parallel", "arbitrary")),
    )(dproj, proj, dy, conv_w)


def _rope_tables(pos_ref, invf_ref, ma_ref, mb_ref, sign):
    ang = pos_ref[...].astype(F32) * invf_ref[...]
    cs = jnp.cos(ang)
    sn = jnp.sin(ang) * sign
    return cs, sn * ma_ref[...], sn * mb_ref[...]


def _rotate(v, cs, sa, sb):
    return v * cs + pltpu.roll(v, 128 - HALF, 1) * sa + pltpu.roll(v, HALF, 1) * sb


MLA_TM = 512


def mla_prep_fwd(proj, pos, g_q, g_kv, wuq, wukv, tabs):
    t = proj.shape[1]
    tm = min(MLA_TM, t)

    def body(lat_ref, pos_ref, gq_ref, gkv_ref, wuq_ref, wukv_ref, invf_ref, ma_ref, mb_ref,
             q_ref, k_ref, kv_ref, qn_ref, kvn_ref):
        lat = lat_ref[...].astype(F32)
        ql = lat[:, :QL]
        kl = lat[:, QL:QL + KVL]
        kr = lat[:, QL + KVL:QL + KVL + 128]
        qn = (ql * lax.rsqrt(jnp.mean(ql * ql, axis=-1, keepdims=True) + EPS) * gq_ref[...]).astype(BF16)
        kvn = (kl * lax.rsqrt(jnp.mean(kl * kl, axis=-1, keepdims=True) + EPS) * gkv_ref[...]).astype(BF16)
        qn_ref[...] = qn
        kvn_ref[...] = kvn
        cs, sa, sb = _rope_tables(pos_ref, invf_ref, ma_ref, mb_ref, 1.0)
        q = _dot_nt(qn, wuq_ref[...]) * (SM_SCALE * LOG2E)
        kv = _dot_nt(kvn, wukv_ref[...]).astype(BF16)
        kv_ref[...] = kv
        kpe = _rotate(kr, cs, sa, sb).astype(BF16)
        for hh in range(H):
            lo, mid, hi = hh * DQK, hh * DQK + 128, (hh + 1) * DQK
            q_ref[:, lo:mid] = q[:, lo:mid].astype(BF16)
            q_ref[:, mid:hi] = _rotate(q[:, mid:hi], cs, sa, sb).astype(BF16)
            k_ref[:, lo:mid] = kv[:, lo:mid]
            k_ref[:, mid:hi] = kpe

    row = lambda w: pl.BlockSpec((tm, w), lambda i: (i, 0))
    const = lambda a: pl.BlockSpec(a.shape, lambda i: (0,) * a.ndim)
    return pl.pallas_call(
        body, name="mla_prep_fwd", grid=(t // tm,),
        in_specs=[pl.BlockSpec((None, tm, D), lambda i: (SEG_LAT, i, 0)), row(1),
                  const(g_q), const(g_kv), const(wuq), const(wukv)] + [const(a) for a in tabs],
        out_specs=[row(H * DQK), row(H * DQK), row(H * DQK), row(QL), row(KVL)],
        out_shape=[jax.ShapeDtypeStruct((t, H * DQK), BF16)] * 3
        + [jax.ShapeDtypeStruct((t, QL), BF16), jax.ShapeDtypeStruct((t, KVL), BF16)],
        compiler_params=_params(("parallel",)),
    )(proj, pos, g_q, g_kv, wuq, wukv, *tabs)


def mla_prep_bwd(dproj, proj, dq_rot, dk, dv, pos, g_q, g_kv, wuq, wukv, tabs):
    t = proj.shape[1]
    tm = min(MLA_TM, t)

    def body(dp_in_ref, lat_ref, dqr_ref, dk_ref, dv_ref, pos_ref, gq_ref, gkv_ref, wuq_ref, wukv_ref,
             invf_ref, ma_ref, mb_ref, dp_ref, dq_ref, dkv_ref, dgq_ref, dgkv_ref):
        i = pl.program_id(0)
        lat = lat_ref[...].astype(F32)
        ql = lat[:, :QL]
        kl = lat[:, QL:QL + KVL]
        rq = lax.rsqrt(jnp.mean(ql * ql, axis=-1, keepdims=True) + EPS)
        rk = lax.rsqrt(jnp.mean(kl * kl, axis=-1, keepdims=True) + EPS)
        nq = ql * rq
        nk = kl * rk
        cs, sa, sb = _rope_tables(pos_ref, invf_ref, ma_ref, mb_ref, -1.0)
        dkpe = jnp.zeros((tm, 128), F32)
        for hh in range(H):
            lo, mid, hi = hh * DQK, hh * DQK + 128, (hh + 1) * DQK
            dq_ref[:, lo:mid] = (dqr_ref[:, lo:mid] * SM_SCALE).astype(BF16)
            dq_ref[:, mid:hi] = _rotate(dqr_ref[:, mid:hi] * SM_SCALE, cs, sa, sb).astype(BF16)
            dkv_ref[:, lo:mid] = dk_ref[:, lo:mid]
            dkv_ref[:, mid:hi] = dv_ref[:, hh * DV:(hh + 1) * DV]
            dkpe = dkpe + dk_ref[:, mid:hi].astype(F32)
        lane = lax.broadcasted_iota(jnp.int32, (tm, 128), 1)
        dkr = jnp.where(lane < ROPE, _rotate(dkpe, cs, sa, sb), 0.0)
        dqn = _dot(dq_ref[...], wuq_ref[...])
        dkvn = _dot(dkv_ref[...], wukv_ref[...])
        gq = gq_ref[...]
        gkv = gkv_ref[...]
        dnq = dqn * gq
        dnk = dkvn * gkv
        dql = rq * (dnq - nq * jnp.mean(dnq * nq, axis=-1, keepdims=True))
        dkl = rk * (dnk - nk * jnp.mean(dnk * nk, axis=-1, keepdims=True))
        dp_ref[:, :QL] = dql.astype(BF16)
        dp_ref[:, QL:QL + KVL] = dkl.astype(BF16)
        dp_ref[:, QL + KVL:QL + KVL + 128] = dkr.astype(BF16)
        dp_ref[:, QL + KVL + 128:] = jnp.zeros((tm, D - QL - KVL - 128), BF16)

        @pl.when(i == 0)
        def _():
            dgq_ref[...] = jnp.zeros_like(dgq_ref)
            dgkv_ref[...] = jnp.zeros_like(dgkv_ref)

        dgq_ref[...] += jnp.sum(dqn * nq, axis=0, keepdims=True)
        dgkv_ref[...] += jnp.sum(dkvn * nk, axis=0, keepdims=True)

    row = lambda w: pl.BlockSpec((tm, w), lambda i: (i, 0))
    const = lambda a: pl.BlockSpec(a.shape, lambda i: (0,) * a.ndim)
    seg = pl.BlockSpec((None, tm, D), lambda i: (SEG_LAT, i, 0))
    return pl.pallas_call(
        body, name="mla_prep_bwd", grid=(t // tm,),
        in_specs=[pl.BlockSpec(memory_space=pl.ANY), seg, row(H * DQK), row(H * DQK), row(H * DV), row(1),
                  const(g_q), const(g_kv), const(wuq), const(wukv)] + [const(a) for a in tabs],
        out_specs=[seg, row(H * DQK), row(H * DQK),
                   pl.BlockSpec((1, QL), lambda i: (0, 0)), pl.BlockSpec((1, KVL), lambda i: (0, 0))],
        out_shape=[jax.ShapeDtypeStruct(dproj.shape, BF16),
                   jax.ShapeDtypeStruct((t, H * DQK), BF16), jax.ShapeDtypeStruct((t, H * DQK), BF16),
                   jax.ShapeDtypeStruct((1, QL), F32), jax.ShapeDtypeStruct((1, KVL), F32)],
        input_output_aliases={0: 0},
        compiler_params=_params(("arbitrary",)),
    )(dproj, proj, dq_rot, dk, dv, pos, g_q, g_kv, wuq, wukv, *tabs)


def _causal_mask(s, shift):
    row = lax.broadcasted_iota(jnp.int32, s.shape, 0)
    col = lax.broadcasted_iota(jnp.int32, s.shape, 1)
    return jnp.where(col <= row + shift, s, -1e30)


def flash_fwd(q, k, kv, nb, seq):
    t = q.shape[0]
    tq = min(FLASH_TQ, seq)
    nq = seq // tq

    def body(q_ref, k_ref, v_ref, o_ref, lse_ref):
        for qi in range(nq):
            qs = slice(qi * tq, (qi + 1) * tq)
            qv = q_ref[qs, :]
            m = jnp.full((tq, 1), -1e30, F32)
            l = jnp.zeros((tq, 1), F32)
            acc = jnp.zeros((tq, DV), F32)
            for j in range(qi + 1):
                ks = slice(j * tq, (j + 1) * tq)
                s = _dot_nt(qv, k_ref[ks, :])
                if j == qi:
                    s = _causal_mask(s, 0)
                m_new = jnp.maximum(m, jnp.max(s, axis=1, keepdims=True))
                p = jnp.exp2(s - m_new)
                alpha = jnp.exp2(m - m_new)
                l = alpha * l + jnp.sum(p, axis=1, keepdims=True)
                acc = alpha * acc + _dot(p.astype(BF16), v_ref[ks, :])
                m = m_new
            o_ref[qs, :] = (acc / l).astype(BF16)
            lse_ref[qs, :] = jnp.broadcast_to(m + jnp.log(l) * LOG2E, (tq, DV))

    out_blk = pl.BlockSpec((seq, DV), lambda b, h: (b, h))
    return pl.pallas_call(
        body, name="flash_fwd", grid=(nb, H),
        in_specs=[pl.BlockSpec((seq, DQK), lambda b, h: (b, h)),
                  pl.BlockSpec((seq, DQK), lambda b, h: (b, h)),
                  pl.BlockSpec((seq, DV), lambda b, h: (b, 2 * h + 1))],
        out_specs=[out_blk, out_blk],
        out_shape=[jax.ShapeDtypeStruct((t, H * DV), BF16), jax.ShapeDtypeStruct((t, H * DV), F32)],
        compiler_params=_params(("parallel", "parallel")),
    )(q, k, kv)


def flash_bwd(q, k, kv, o, do, lse, nb, seq, token):
    t = q.shape[0]
    tq = min(FLASH_TQ, seq)
    nq = seq // tq

    def body(q_ref, k_ref, v_ref, o_ref, do_ref, lse_ref, tok_ref, dq_ref, dk_ref, dv_ref):
        delta, lse = [], []
        for qi in range(nq):
            qs = slice(qi * tq, (qi + 1) * tq)
            dl = jnp.sum(do_ref[qs, :].astype(F32) * o_ref[qs, :].astype(F32), axis=1, keepdims=True)
            delta.append(jnp.broadcast_to(dl, (tq, DV)).T[:1, :])
            lse.append(lse_ref[qs, :].T[:1, :])
        for ki in range(nq):
            ks = slice(ki * tq, (ki + 1) * tq)
            kb = k_ref[ks, :]
            vb = v_ref[ks, :]
            dk = jnp.zeros((tq, DQK), F32)
            dv = jnp.zeros((tq, DV), F32)
            for qi in range(ki, nq):
                qs = slice(qi * tq, (qi + 1) * tq)
                qv = q_ref[qs, :]
                dov = do_ref[qs, :]
                st = _dot_nt(kb, qv)
                if qi == ki:
                    row = lax.broadcasted_iota(jnp.int32, st.shape, 0)
                    col = lax.broadcasted_iota(jnp.int32, st.shape, 1)
                    st = jnp.where(row <= col, st, -1e30)
                pt = jnp.exp2(st - lse[qi])
                dpt = _dot_nt(vb, dov)
                dzt = (pt * (dpt - delta[qi])).astype(BF16)
                dv = dv + _dot(pt.astype(BF16), dov)
                dk = dk + _dot(dzt, qv)
                dqb = _dot_tn(dzt, kb)
                if ki == 0:
                    dq_ref[qs, :] = dqb
                else:
                    dq_ref[qs, :] += dqb
            dk_ref[ks, :] = (dk * LN2).astype(BF16)
            dv_ref[ks, :] = dv.astype(BF16)

    full = lambda w, col: pl.BlockSpec((seq, w), col)
    same = lambda b, h: (b, h)
    return pl.pallas_call(
        body, name="flash_bwd", grid=(nb, H),
        in_specs=[full(DQK, same), full(DQK, same), full(DV, lambda b, h: (b, 2 * h + 1)),
                  full(DV, same), full(DV, same), full(DV, same),
                  pl.BlockSpec((8, 128), lambda b, h: (0, 0))],
        out_specs=[full(DQK, same), full(DQK, same), full(DV, same)],
        out_shape=[jax.ShapeDtypeStruct((t, H * DQK), F32), jax.ShapeDtypeStruct((t, H * DQK), BF16),
                   jax.ShapeDtypeStruct((t, H * DV), BF16)],
        compiler_params=_params(("parallel", "parallel")),
    )(q, k, kv, o, do, lse, token)


TAIL_TM = 512


def tail_fwd(y, attn, proj, x2, tgt, gate, g_post, wco, wmo, wout, seq):
    t = y.shape[0]
    nb = t // seq
    tm = min(TAIL_TM, seq)
    tpb = seq // tm

    def body(y_ref, at_ref, p_ref, x_ref, t_ref, gate_ref, gp_ref, wco_ref, wmo_ref, wout_ref,
             o_ref, ya_ref, yb_ref, m_ref, do2_ref, dout_ref, dgate_ref, dgp_ref, loss_ref):
        i = pl.program_id(0)
        bz = p_ref[0].astype(F32)
        ga = p_ref[1].astype(F32)
        gb = p_ref[2].astype(F32)
        ov = (at_ref[...].astype(F32) * (bz * _sig(bz))).astype(BF16)
        o_ref[...] = ov
        ya = _dot(y_ref[...], wco_ref[...])
        yb = _dot(ov, wmo_ref[...])
        ya_ref[...] = ya.astype(BF16)
        yb_ref[...] = yb.astype(BF16)
        mv = (_sig(ga) * ya + _sig(gb) * yb).astype(BF16)
        m_ref[...] = mv
        o2 = _dot(mv, wout_ref[...])
        r = lax.rsqrt(jnp.mean(o2 * o2, axis=-1, keepdims=True) + EPS)
        nrm = o2 * r
        gp = gp_ref[...]
        gate_v = gate_ref[...]
        rn = nrm * gp
        err = x_ref[...] + gate_v * rn - t_ref[...]
        dout = err * (1.0 / D)
        dout_ref[...] = dout
        dn = dout * gate_v * gp
        do2_ref[...] = (r * (dn - nrm * jnp.mean(dn * nrm, axis=-1, keepdims=True))).astype(BF16)

        @pl.when(i % tpb == 0)
        def _():
            dgate_ref[...] = jnp.zeros_like(dgate_ref)

        @pl.when(i == 0)
        def _():
            dgp_ref[...] = jnp.zeros_like(dgp_ref)
            loss_ref[...] = jnp.zeros_like(loss_ref)

        dgate_ref[...] += jnp.sum(dout * rn, axis=0, keepdims=True)
        dgp_ref[...] += jnp.sum(dout * gate_v * nrm, axis=0, keepdims=True)
        loss_ref[...] += 0.5 * jnp.sum(jnp.mean(err * err, axis=-1, keepdims=True), axis=0, keepdims=True)

    row = pl.BlockSpec((tm, D), lambda i: (i, 0))
    per_batch = pl.BlockSpec((None, 1, D), lambda i: (i // tpb, 0, 0))
    vec = pl.BlockSpec((1, D), lambda i: (0, 0))
    wgt = pl.BlockSpec((D, D), lambda i: (0, 0))
    act = jax.ShapeDtypeStruct((t, D), BF16)
    return pl.pallas_call(
        body, name="tail_fwd", grid=(t // tm,),
        in_specs=[row, row, pl.BlockSpec((3, tm, D), lambda i: (0, i, 0)), row, row, per_batch, vec,
                  wgt, wgt, wgt],
        out_specs=[row, row, row, row, row, row, per_batch, vec, pl.BlockSpec((1, 1), lambda i: (0, 0))],
        out_shape=[act, act, act, act, act, jax.ShapeDtypeStruct((t, D), F32),
                   jax.ShapeDtypeStruct((nb, 1, D), F32), jax.ShapeDtypeStruct((1, D), F32),
                   jax.ShapeDtypeStruct((1, 1), F32)],
        compiler_params=_params(("arbitrary",)),
    )(y, attn, proj, x2, tgt, gate, g_post, wco, wmo, wout)


def tail_bwd(do2, proj, ya, yb, attn, wout, wmo, wco):
    t = do2.shape[0]
    tm = min(TAIL_TM, t)

    def body(do2_ref, p_ref, ya_ref, yb_ref, at_ref, wout_ref, wmo_ref, wco_ref,
             dp_ref, dya_ref, dyb_ref, dat_ref, dy_ref):
        bz = p_ref[0].astype(F32)
        ga = p_ref[1].astype(F32)
        gb = p_ref[2].astype(F32)
        dm = _dot_nt(do2_ref[...], wout_ref[...])
        sa = _sig(ga)
        sb = _sig(gb)
        dya = (dm * sa).astype(BF16)
        dyb = (dm * sb).astype(BF16)
        dya_ref[...] = dya
        dyb_ref[...] = dyb
        dp_ref[1] = (dm * ya_ref[...].astype(F32) * (sa * (1.0 - sa))).astype(BF16)
        dp_ref[2] = (dm * yb_ref[...].astype(F32) * (sb * (1.0 - sb))).astype(BF16)
        dov = _dot_nt(dyb, wmo_ref[...])
        sz = _sig(bz)
        dat_ref[...] = (dov * (bz * sz)).astype(BF16)
        dp_ref[0] = (dov * at_ref[...].astype(F32) * (sz * (1.0 + bz * (1.0 - sz)))).astype(BF16)
        dy_ref[...] = _dot_nt(dya, wco_ref[...]).astype(BF16)

    row = pl.BlockSpec((tm, D), lambda i: (i, 0))
    seg3 = pl.BlockSpec((3, tm, D), lambda i: (0, i, 0))
    wgt = pl.BlockSpec((D, D), lambda i: (0, 0))
    act = jax.ShapeDtypeStruct((t, D), BF16)
    return pl.pallas_call(
        body, name="tail_bwd", grid=(t // tm,),
        in_specs=[row, seg3, row, row, row, wgt, wgt, wgt],
        out_specs=[seg3, row, row, row, row],
        out_shape=[jax.ShapeDtypeStruct((NSEG, t, D), BF16), act, act, act, act],
        compiler_params=_params(("parallel",)),
    )(do2, proj, ya, yb, attn, wout, wmo, wco)


def _adam_update(w, m, v, grad):
    mn = ADAM_B1 * m + (1.0 - ADAM_B1) * grad
    vn = ADAM_B2 * v + (1.0 - ADAM_B2) * (grad * grad)
    m_hat = mn / (1.0 - ADAM_B1 ** ADAM_STEP)
    v_hat = vn / (1.0 - ADAM_B2 ** ADAM_STEP)
    return -ADAM_LR * (m_hat / (jnp.sqrt(v_hat) + ADAM_EPS) + ADAM_WD * w), mn, vn


def adamw(w, m, v, g, name, token):
    rows, cols = w.shape
    tr = rows
    for cand in (256, 128, 64, 32, 16, 8):
        if rows % cand == 0 and rows > cand:
            tr = cand
            break

    def body(w_ref, m_ref, v_ref, g_ref, tok_ref, d_ref, mo_ref, vo_ref):
        d_ref[...], mo_ref[...], vo_ref[...] = _adam_update(w_ref[...], m_ref[...], v_ref[...], g_ref[...])

    blk = pl.BlockSpec((tr, cols), lambda i: (i, 0))
    return pl.pallas_call(
        body, name=name, grid=(rows // tr,),
        in_specs=[blk] * 4 + [pl.BlockSpec((8, 128), lambda i: (0, 0))], out_specs=[blk] * 3,
        out_shape=[jax.ShapeDtypeStruct((rows, cols), F32)] * 3,
        compiler_params=_params(("parallel",)),
    )(w, m, v, g, token)


def adamw_scattered(w, m, v, own, land, me, tr, name, transpose=False):
    slot_rows = land.shape[1]
    cols = land.shape[2]
    rows = slot_rows if transpose else w.shape[0]
    per_slot = slot_rows // tr

    def body(me_ref, w_ref, m_ref, v_ref, own_ref, land_ref, go_ref, d_ref, mo_ref, vo_ref):
        grad = own_ref[...].astype(F32)
        for s in range(8):
            grad = grad + land_ref[s].astype(F32)
        if transpose:
            grad = grad.T
        go_ref[...] = grad
        d_ref[...], mo_ref[...], vo_ref[...] = _adam_update(w_ref[...], m_ref[...], v_ref[...], grad)

    wblk = pl.BlockSpec(w.shape if transpose else (tr, w.shape[1]), lambda i, s: (i, 0))
    return pl.pallas_call(
        body, name=name,
        grid_spec=pltpu.PrefetchScalarGridSpec(
            num_scalar_prefetch=1, grid=(rows // tr,),
            in_specs=[wblk, wblk, wblk,
                      pl.BlockSpec((tr, cols), lambda i, s: (s[0] * per_slot + i, 0)),
                      pl.BlockSpec((8, tr, cols), lambda i, s: (0, i, 0))],
            out_specs=[wblk] * 4),
        out_shape=[jax.ShapeDtypeStruct(w.shape, F32)] * 4,
        compiler_params=_params(),
    )(me, w, m, v, own, land)


def adamw_win(wt, mt, vt, ka, ra, kb, rb):
    rows = wt.shape[0]
    tc = 256
    nh = (D // 2) // tc

    def body(w_ref, m_ref, v_ref, ka_ref, ra_ref, kb_ref, rb_ref, go_ref, d_ref, mo_ref, vo_ref):
        first = pl.program_id(0) < nh
        grad = jnp.where(first, ka_ref[...] + ra_ref[...].astype(F32), kb_ref[...] + rb_ref[...].astype(F32))
        go_ref[...] = grad
        d_ref[...], mo_ref[...], vo_ref[...] = _adam_update(w_ref[...], m_ref[...], v_ref[...], grad)

    blk = pl.BlockSpec((rows, tc), lambda j: (0, j))
    lo = pl.BlockSpec((rows, tc), lambda j: (0, jnp.minimum(j, nh - 1)))
    hi = pl.BlockSpec((rows, tc), lambda j: (0, jnp.maximum(j - nh, 0)))
    return pl.pallas_call(
        body, name="adamw_w_in", grid=(D // tc,),
        in_specs=[blk, blk, blk, lo, lo, hi, hi], out_specs=[blk] * 4,
        out_shape=[jax.ShapeDtypeStruct((rows, D), F32)] * 4,
        compiler_params=_params(("parallel",)),
    )(wt, mt, vt, ka, ra, kb, rb)


_ORD_A = ("x", "y", "c")
_ORD_B = ("y", "x", "c")


def _rows128(a, rows):
    flat = a.reshape(-1)
    return jnp.pad(flat, (0, rows * 128 - flat.shape[0])).reshape(rows, 128)


def kernel(x, c, positions, w_ada, b_ada, g_pre, w_in, conv_w, w_conv_out, g_q, w_uq, g_kv, w_ukv, w_mla_out, w_out, g_post, loss_target, m_w_ada, m_b_ada, m_g_pre, m_w_in, m_conv_w, m_w_conv_out, m_g_q, m_w_uq, m_g_kv, m_w_ukv, m_w_mla_out, m_w_out, m_g_post, v_w_ada, v_b_ada, v_g_pre, v_w_in, v_conv_w, v_w_conv_out, v_g_q, v_w_uq, v_g_kv, v_w_ukv, v_w_mla_out, v_w_out, v_g_post):
    nb, seq, _ = x.shape
    t = nb * seq
    mx, my, mc = lax.axis_index("x"), lax.axis_index("y"), lax.axis_index("c")
    me = 4 * mx + 2 * my + mc
    co = {"x": mx, "y": my, "c": mc}

    x2 = x.reshape(t, D)
    tgt2 = loss_target.reshape(t, D)
    pos2 = positions.reshape(t, 1)

    ada_cols = w_ada.shape[2]
    b_cols = lax.dynamic_slice(b_ada, (0, me * ada_cols), (1, ada_cols))
    c_g, taps_g, mod_g = ada_gather(jnp.pad(c, ((0, 8 - nb), (0, 0))), _rows128(conv_w[0], 8), w_ada[0], b_cols)
    c_all = c_g[:, :nb].reshape(8 * nb, D)
    conv_full = taps_g[:, 0:3].transpose(1, 0, 2).reshape(3, D)
    conv_full8 = jnp.pad(conv_full, ((0, 5), (0, 0)))
    mod = mod_g[:, :nb].transpose(1, 0, 2).reshape(nb, 8 * ada_cols)
    shift = mod[:, 0:D].reshape(nb, 1, D)
    scale = mod[:, D:2 * D].reshape(nb, 1, D)
    gate = mod[:, 2 * D:3 * D].reshape(nb, 1, D)

    wt = w_in[0].T.astype(BF16)
    lo = lax.bitcast_convert_type(wt[:, :D // 2], jnp.uint16).astype(jnp.uint32)
    hi = lax.bitcast_convert_type(wt[:, D // 2:], jnp.uint16).astype(jnp.uint32)
    wt_bits = lax.bitcast_convert_type(lo | (hi << 16), F32)
    wt_bits, mod = lax.optimization_barrier((wt_bits, mod))
    shift = mod[:, 0:D].reshape(nb, 1, D)
    scale = mod[:, D:2 * D].reshape(nb, 1, D)
    gate = mod[:, 2 * D:3 * D].reshape(nb, 1, D)
    q4 = D // 4
    r3rd = wt_bits.shape[0] // 3
    plan = [(0, (k * r3rd, r3rd), (g * q4, q4), (_ORD_A, _ORD_B)[g]) for k in range(3) for g in range(2)]
    gw = allgather_big([wt_bits], plan, "gather_w_in")
    late = [w_conv_out[0].astype(BF16), w_mla_out[0].astype(BF16), w_out[0].astype(BF16),
            jnp.pad(w_uq[0].T.astype(BF16), ((0, DQK - 192), (0, 0))), w_ukv[0].T.astype(BF16)]
    gw0, late = lax.optimization_barrier((gw[0], late))
    late_state, late_token = gather_start(late, "gather_late_start")
    wt_bits_all = gw0.reshape(N_IN, D // 2)

    inv_freq = ROPE_THETA ** (-jnp.arange(0, ROPE, 2, dtype=F32) / ROPE)
    invf = jnp.concatenate([inv_freq, inv_freq, jnp.zeros((128 - ROPE,), F32)]).reshape(1, 128)
    lane = np.arange(128)
    tabs = (invf,
            jnp.asarray(np.where(lane < HALF, -1.0, 0.0).reshape(1, 128), F32),
            jnp.asarray(np.where((lane >= HALF) & (lane < ROPE), 1.0, 0.0).reshape(1, 128), F32))

    h = prenorm_fwd(x2, scale, shift, g_pre, seq)
    proj, wt_p = proj_matmul(h, wt_bits_all, late_token)
    y = conv_fwd(proj, conv_full8, seq)
    gl = gather_wait(late_state, y, "gather_late_wait")
    wco = gl[0].reshape(D, D)
    wmo = gl[1].reshape(D, D)
    wout = gl[2].reshape(D, D)
    wuq_p = gl[3].reshape(H * DQK, QL)
    wukv = gl[4].reshape(H * 256, KVL)
    q_rot, k_cat, kv, qn, kvn = mla_prep_fwd(proj, pos2, g_q, g_kv, wuq_p, wukv, tabs)
    attn, lse = flash_fwd(q_rot, k_cat, kv, nb, seq)
    o, ya, yb, m, do2, dout, dgate, dg_post, loss_part = tail_fwd(
        y, attn, proj, x2, tgt2, gate, g_post, wco, wmo, wout, seq)

    dproj, dya, dyb, dattn, dy = tail_bwd(do2, proj, ya, yb, attn, wout, wmo, wco)
    g_wout = grad_matmul(m, do2, "grad_w_square")
    g_wmo = grad_matmul(o, dyb, "grad_w_square")
    g_wco = grad_matmul(y, dya, "grad_w_square")
    sc1, sc1_tok = scatter_start([g_wco, g_wmo, g_wout], "scatter_out_grads_start")
    dproj, dconv = conv_bwd(dproj, proj, dy, conv_full8, seq)
    dq_rot, dk, dv = flash_bwd(q_rot, k_cat, kv, attn, dattn, lse, nb, seq, sc1_tok)
    dproj, dq, dkv, dg_q, dg_kv = mla_prep_bwd(dproj, proj, dq_rot, dk, dv, pos2, g_q, g_kv, wuq_p, wukv, tabs)
    g_wuq_t = grad_matmul(dq, qn, "grad_w_uq")
    g_wukv_t = grad_matmul(dkv, kvn, "grad_w_ukv")
    sc2, sc2_tok = scatter_start([g_wuq_t, g_wukv_t], "scatter_mla_grads_start")
    g_win_p = win_grad_matmul(h, dproj, sc2_tok)

    g_wt = g_win_p.reshape(2, 2, 2, N_IN // 8, D)
    ords = [("c", "y", "x"), ("c", "x", "y")]
    hc = D // 2
    win_shape = (2, 2, N_IN // 8, hc)
    pick_w = lambda col: (lambda ref, cc: ref.at[:, :, 1 - cc["c"], :, pl.ds(col * hc, hc)])
    which1 = [0, 0]
    picks1 = [pick_w(0), pick_w(1)]
    st1, tok1 = swap_start([g_wt], which1, ["c"] * 2, picks1, [win_shape] * 2, "rs_c_start")
    assert nb == 2
    dh0 = dh_matmul(dproj, wt_p, tok1, seq, 0)
    (g_wt,), r1 = swap_wait(st1, dh0, which1, ["c"] * 2, picks1, "rs_c_wait")
    sel_xyc = jnp.stack([mx, my, mc]).astype(jnp.int32)
    sel2 = [jnp.stack([co[o[2]]]).astype(jnp.int32) for o in ords]
    first = [rs_win_add_first(g_wt, r1[0], sel_xyc, 1, 0, "rs_add_first_0"),
             rs_win_add_first(g_wt, r1[1], sel_xyc, 0, 1, "rs_add_first_1")]
    keep1, send1 = zip(*first)
    all4 = [0, 1]
    none4 = [None] * 2
    axes2 = [o[1] for o in ords]
    st2, tok2 = swap_start(list(send1), all4, axes2, none4, [s.shape for s in send1], "rs_ici1_start")

    dh1 = dh_matmul(dproj, wt_p, tok2, seq, 1)
    gx0, dsh0, dsc0, dgp0 = prenorm_bwd(dh0, x2, dout, scale, g_pre, seq, tok2, 0, None)
    _, r2 = swap_wait(st2, (gx0, dh1), all4, axes2, none4, "rs_ici1_wait")
    keep2, send2 = zip(*[rs_add_second(keep1[a], r2[a], sel2[a], "rs_add_second") for a in range(2)])
    axes3 = [o[2] for o in ords]
    st3, tok3 = swap_start(list(send2), all4, axes3, none4, [s.shape for s in send2], "rs_ici2_start")
    grad_x2, dsh1, dsc1, dgp1 = prenorm_bwd(dh1, x2, dout, scale, g_pre, seq, tok3, 1, gx0)
    dshift = jnp.stack([dsh0, dsh1])
    dscale = jnp.stack([dsc0, dsc1])
    dg_pre = dgp0 + dgp1

    dmod = jnp.concatenate([dshift, dscale, dgate], axis=2).reshape(nb * 3 * D // 128, 128)
    small = jnp.concatenate([
        dmod, _rows128(dg_pre, 8), _rows128(dg_post, 8), _rows128(dg_q, 8), _rows128(dg_kv, 8),
        dconv[0:3].reshape(24, 128), _rows128(loss_part, 8)], axis=0)
    small_g = small_allgather(small, "gather_small_grads")
    sums = slot_sum(small_g)
    dmod_all = small_g[:, 0:48].reshape(8 * nb, 3 * D)
    g_bada = (sums[0:24] + sums[24:48]).reshape(1, 3 * D)
    g_gpre = sums[48:56].reshape(1, D)
    g_gpost = sums[56:64].reshape(1, D)
    g_gq = sums[64:67].reshape(1, QL)
    g_gkv = sums[72:74].reshape(1, KVL)
    g_conv_full = sums[80:104].reshape(3, D)
    loss = sums[104, 0]
    g_conv = lax.dynamic_slice(g_conv_full, (0, me * 128), (3, 128))
    dmod_cols = lax.dynamic_slice(dmod_all, (0, me * ada_cols), (8 * nb, ada_cols))
    g_wada = ada_bwd(c_all, dmod_cols)

    res = {}
    res["w_ada"] = [o_[None] for o_ in (g_wada, *adamw(w_ada[0], m_w_ada[0], v_w_ada[0], g_wada, "adamw_w_ada", tok3))]

    def pack(b_, gp_, gpo_, gq_, gkv_, cw_):
        return jnp.concatenate([_rows128(b_, 24), _rows128(gp_, 8), _rows128(gpo_, 8), _rows128(gq_, 8),
                                _rows128(gkv_, 8), _rows128(cw_, 8)], axis=0)

    sw = pack(b_ada, g_pre, g_post, g_q, g_kv, conv_w)
    sm = pack(m_b_ada, m_g_pre, m_g_post, m_g_q, m_g_kv, m_conv_w)
    sv = pack(v_b_ada, v_g_pre, v_g_post, v_g_q, v_g_kv, v_conv_w)
    sg = pack(g_bada, g_gpre, g_gpost, g_gq, g_gkv, g_conv)
    small_out = (sg, *adamw(sw, sm, sv, sg, "adamw_small", tok3))

    _, r3 = swap_wait(st3, small_out[1], all4, axes3, none4, "rs_ici2_wait")

    (g_wco, g_wmo, g_wout), (l_wco, l_wmo, l_wout) = scatter_wait(sc1, small_out[2], "scatter_out_grads_wait")
    (g_wuq_t, g_wukv_t), (l_wuq, l_wukv) = scatter_wait(sc2, small_out[3], "scatter_mla_grads_wait")

    res["w_in"] = [o_.T[None] for o_ in adamw_win(w_in[0].T, m_w_in[0].T, v_w_in[0].T,
                                                  keep2[0], r3[0], keep2[1], r3[1])]
    me1 = me.reshape(1).astype(jnp.int32)
    res["w_uq"] = [o_.T[None] for o_ in adamw_scattered(
        w_uq[0].T, m_w_uq[0].T, v_w_uq[0].T, g_wuq_t, l_wuq, me1, 64, "adamw_w_uq")]
    res["w_ukv"] = [o_[None] for o_ in adamw_scattered(
        w_ukv[0], m_w_ukv[0], v_w_ukv[0], g_wukv_t, l_wukv, me1, KVL, "adamw_w_ukv", transpose=True)]
    for nm, wv, mv, vv, gg, ll in (("w_conv_out", w_conv_out, m_w_conv_out, v_w_conv_out, g_wco, l_wco),
                                   ("w_mla_out", w_mla_out, m_w_mla_out, v_w_mla_out, g_wmo, l_wmo),
                                   ("w_out", w_out, m_w_out, v_w_out, g_wout, l_wout)):
        res[nm] = [o_[None] for o_ in adamw_scattered(wv[0], mv[0], vv[0], gg, ll, me1, 128, "adamw_square")]

    def unpack(a):
        return {"b_ada": a[0:24].reshape(1, 3 * D), "g_pre": a[24:32].reshape(1, D),
                "g_post": a[32:40].reshape(1, D), "g_q": a[40:43].reshape(1, QL),
                "g_kv": a[48:50].reshape(1, KVL), "conv_w": a[56:59].reshape(-1)[:3 * 128].reshape(1, 3, 128)}

    for nm in ("b_ada", "g_pre", "g_post", "g_q", "g_kv", "conv_w"):
        res[nm] = [unpack(a)[nm] for a in small_out]

    order = ["w_ada", "b_ada", "g_pre", "w_in", "conv_w", "w_conv_out", "g_q", "w_uq", "g_kv", "w_ukv",
             "w_mla_out", "w_out", "g_post"]
    out = [loss, grad_x2.reshape(nb, seq, D)]
    for k_ in range(4):
        out += [res[nm][k_] for nm in order]
    return tuple(out)
```

```python
import numpy as np
import jax
import jax.numpy as jnp
from jax import lax
from jax.experimental import pallas as pl
from jax.experimental.pallas import tpu as pltpu

F32 = jnp.float32
BF16 = jnp.bfloat16
MESH = pl.DeviceIdType.MESH

D = 1024
H = 8
QL = 384
KVL = 256
ROPE = 64
HALF = ROPE // 2
DQK = 256
DV = 128
NSEG = 8
NP = NSEG * D
EPS = 1e-6
ROPE_THETA = 10000.0
SM_SCALE = (128 + ROPE) ** -0.5
LOG2E = 1.4426950408889634
LN2 = 0.6931471805599453
FLASH_TQ = 512

SEG_BZ, SEG_GA, SEG_GB, SEG_LAT, SEG_V = 0, 1, 2, 3, 4

ADAM_LR = 0.001
ADAM_B1 = 0.9
ADAM_B2 = 0.999
ADAM_EPS = 1e-08
ADAM_WD = 0.01
ADAM_STEP = 10

VMEM_LIMIT = 56 * 1024 * 1024


def _params(sem=None, vmem=VMEM_LIMIT):
    kw = dict(vmem_limit_bytes=vmem)
    if sem is not None:
        kw["dimension_semantics"] = sem
    return pltpu.CompilerParams(**kw)


def _sig(v):
    return 0.5 * jnp.tanh(0.5 * v) + 0.5


def _dot(a, b):
    return jnp.dot(a, b, preferred_element_type=F32)


def _dot_nt(a, b):
    return lax.dot_general(a, b, (((1,), (1,)), ((), ())), preferred_element_type=F32)


def _dot_tn(a, b):
    return lax.dot_general(a, b, (((0,), (0,)), ((), ())), preferred_element_type=F32)


_AXIS_POS = {"x": 0, "y": 1, "c": 2}


def _coords():
    return lax.axis_index("x"), lax.axis_index("y"), lax.axis_index("c")


def _partner(axis):
    p = list(_coords())
    p[_AXIS_POS[axis]] = 1 - p[_AXIS_POS[axis]]
    return tuple(p)


def small_allgather(v, name):
    rows = v.shape[0]

    def body(v_ref, out_ref, send_sems, recv_sems):
        x, y, c = _coords()
        me = 4 * x + 2 * y + c
        out_ref[me] = v_ref[...]
        copies = []
        for k in range(1, 8):
            peer = (1 - x if k & 4 else x, 1 - y if k & 2 else y, 1 - c if k & 1 else c)
            cp = pltpu.make_async_remote_copy(
                src_ref=v_ref, dst_ref=out_ref.at[me],
                send_sem=send_sems.at[k - 1], recv_sem=recv_sems.at[k - 1],
                device_id=peer, device_id_type=MESH)
            cp.start()
            copies.append(cp)
        for cp in copies:
            cp.wait()

    return pl.pallas_call(
        body, name=name,
        out_shape=jax.ShapeDtypeStruct((8, rows, 128), F32),
        in_specs=[pl.BlockSpec(memory_space=pltpu.VMEM)],
        out_specs=pl.BlockSpec(memory_space=pltpu.VMEM),
        scratch_shapes=[pltpu.SemaphoreType.DMA((7,)), pltpu.SemaphoreType.DMA((7,))],
    )(v)


def _own_block_placed(s):
    x, y, c = _coords()
    return lax.dynamic_update_slice(lax.empty((2, 2, 2) + s.shape, s.dtype), s[None, None, None],
                                    (x, y, c) + (0,) * s.ndim)


def allgather_big(arrs, plan, name):
    n = len(arrs)
    m = len(plan)
    nst = len(plan[0][3])

    def body(*refs):
        ins, outs = refs[n:2 * n], refs[2 * n:3 * n]
        send_sems, recv_sems = refs[3 * n:]
        x, y, c = _coords()
        co = {"x": x, "y": y, "c": c}

        def window(ref, lead, rows, cols):
            win = tuple(slice(None) if w is None else pl.ds(w[0], w[1]) for w in (rows, cols))
            return ref.at[tuple(lead) + win]

        def held(e, free):
            i, rows, cols, _ = plan[e]
            lead = [slice(None) if ax in free else co[ax] for ax in ("x", "y", "c")]
            return window(outs[i], lead, rows, cols)

        def rcopy(e, stage, src, dst, axis):
            return pltpu.make_async_remote_copy(
                src_ref=src, dst_ref=dst,
                send_sem=send_sems.at[e, stage], recv_sem=recv_sems.at[e, stage],
                device_id=_partner(axis), device_id_type=MESH)

        stages = [[] for _ in range(nst)]
        for e, (i, rows, cols, order) in enumerate(plan):
            cp = rcopy(e, 0, window(ins[i], [], rows, cols), held(e, ()), order[0])
            cp.start()
            stages[0].append(cp)
        for s in range(1, nst):
            for e, (i, rows, cols, order) in enumerate(plan):
                stages[s - 1][e].wait_recv()
                blk = held(e, order[:s])
                cp = rcopy(e, s, blk, blk, order[s])
                cp.start()
                stages[s].append(cp)
        for e in range(m):
            stages[nst - 1][e].wait_recv()
        for e in range(m):
            for s in range(nst):
                stages[s][e].wait_send()

    any_spec = pl.BlockSpec(memory_space=pl.ANY)
    lands = [_own_block_placed(a) for a in arrs]
    return pl.pallas_call(
        body, name=name,
        out_shape=[jax.ShapeDtypeStruct(l.shape, l.dtype) for l in lands],
        in_specs=[any_spec] * (2 * n),
        out_specs=[any_spec] * n,
        input_output_aliases={i: i for i in range(n)},
        scratch_shapes=[pltpu.SemaphoreType.DMA((m, nst)), pltpu.SemaphoreType.DMA((m, nst))],
    )(*lands, *arrs)


_HBM =pl.BlockSpec(memory_space=pltpu.HBM)
_SEM = pl.BlockSpec(memory_space=pltpu.SEMAPHORE)


def _swap_copies(srcs, lands, send_sems, recv_sems, axes, picks):
    x, y, c = _coords()
    co = {"x": x, "y": y, "c": c}
    return [pltpu.make_async_remote_copy(
        src_ref=srcs[a] if picks[a] is None else picks[a](srcs[a], co), dst_ref=lands[a],
        send_sem=send_sems.at[a], recv_sem=recv_sems.at[a],
        device_id=_partner(axes[a]), device_id_type=MESH) for a in range(len(srcs))]


def swap_start(arrs, which, axes, picks, out_shapes, name):
    ns, n = len(arrs), len(which)

    def body(*refs):
        srcs, lands = refs[:ns], refs[ns:ns + n]
        send_sems, recv_sems = refs[ns + n:ns + n + 2]
        token = refs[-1]
        for cp in _swap_copies([srcs[i] for i in which], lands, send_sems, recv_sems, axes, picks):
            cp.start()
        token[...] = jnp.zeros_like(token)

    lands = [lax.empty(s, arrs[i].dtype) for s, i in zip(out_shapes, which)]
    ops = [pltpu.with_memory_space_constraint(a, pltpu.HBM) for a in list(arrs) + lands]
    out = pl.pallas_call(
        body, name=name,
        out_shape=[pltpu.SemaphoreType.DMA((n,)), pltpu.SemaphoreType.DMA((n,))]
        + [pltpu.HBM(o.shape, o.dtype) for o in ops] + [jax.ShapeDtypeStruct((8, 128), F32)],
        in_specs=[_HBM] * (ns + n),
        out_specs=[_SEM, _SEM] + [_HBM] * (ns + n) + [pl.BlockSpec(memory_space=pltpu.VMEM)],
        input_output_aliases={i: 2 + i for i in range(ns + n)},
        compiler_params=pltpu.CompilerParams(has_side_effects=pltpu.SideEffectType.DATAFLOW_SIDE_EFFECTING),
    )(*ops)
    return out[:-1], out[-1]


def swap_wait(state, after, which, axes, picks, name):
    n = len(which)
    ns = len(state) - 2 - n

    def body(*refs):
        srcs, lands = refs[:ns], refs[ns:ns + n]
        send_sems, recv_sems = refs[ns + n:ns + n + 2]
        for cp in _swap_copies([srcs[i] for i in which], lands, send_sems, recv_sems, axes, picks):
            cp.wait_send()
            cp.wait_recv()

    thru = list(state[2:])
    after = list(after) if isinstance(after, (list, tuple)) else [after]
    out = pl.pallas_call(
        body, name=name,
        out_shape=[pltpu.HBM(o.shape, o.dtype) for o in thru],
        in_specs=[_HBM] * (ns + n) + [_SEM, _SEM] + [pl.BlockSpec(memory_space=pl.ANY)] * len(after),
        out_specs=[_HBM] * (ns + n),
        input_output_aliases={i: i for i in range(ns + n)},
        compiler_params=pltpu.CompilerParams(has_side_effects=pltpu.SideEffectType.DATAFLOW_SIDE_EFFECTING),
    )(*thru, state[0], state[1], *after)
    return out[:ns], out[ns:]


def _gather_copies(shards, lands, send_sems, recv_sems):
    x, y, c = _coords()
    copies = []
    for a in range(len(shards)):
        for k in range(1, 8):
            peer = (1 - x if k & 4 else x, 1 - y if k & 2 else y, 1 - c if k & 1 else c)
            copies.append(pltpu.make_async_remote_copy(
                src_ref=shards[a], dst_ref=lands[a].at[x, y, c],
                send_sem=send_sems.at[7 * a + k - 1], recv_sem=recv_sems.at[7 * a + k - 1],
                device_id=peer, device_id_type=MESH))
    return copies


def gather_start(shards, name):
    n = len(shards)
    x, y, c = _coords()

    def body(*refs):
        srcs, lands = refs[:n], refs[n:2 * n]
        send_sems, recv_sems = refs[2 * n:2 * n + 2]
        token = refs[-1]
        for cp in _gather_copies(srcs, lands, send_sems, recv_sems):
            cp.start()
        token[...] = jnp.zeros_like(token)

    lands = [_own_block_placed(s) for s in shards]
    ops = [pltpu.with_memory_space_constraint(a, pltpu.HBM) for a in list(shards) + lands]
    out = pl.pallas_call(
        body, name=name,
        out_shape=[pltpu.SemaphoreType.DMA((7 * n,)), pltpu.SemaphoreType.DMA((7 * n,))]
        + [pltpu.HBM(o.shape, o.dtype) for o in ops] + [jax.ShapeDtypeStruct((8, 128), F32)],
        in_specs=[_HBM] * (2 * n),
        out_specs=[_SEM, _SEM] + [_HBM] * (2 * n) + [pl.BlockSpec(memory_space=pltpu.VMEM)],
        input_output_aliases={i: 2 + i for i in range(2 * n)},
        compiler_params=pltpu.CompilerParams(has_side_effects=pltpu.SideEffectType.DATAFLOW_SIDE_EFFECTING),
    )(*ops)
    return out[:-1], out[-1]


def gather_wait(state, after, name):
    n = (len(state) - 2) // 2

    def body(*refs):
        srcs, lands = refs[:n], refs[n:2 * n]
        send_sems, recv_sems = refs[2 * n:2 * n + 2]
        for cp in _gather_copies(srcs, lands, send_sems, recv_sems):
            cp.wait_send()
            cp.wait_recv()

    thru = list(state[2:])
    out = pl.pallas_call(
        body, name=name,
        out_shape=[pltpu.HBM(o.shape, o.dtype) for o in thru],
        in_specs=[_HBM] * (2 * n) + [_SEM, _SEM, pl.BlockSpec(memory_space=pl.ANY)],
        out_specs=[_HBM] * (2 * n),
        input_output_aliases={i: i for i in range(2 * n)},
        compiler_params=pltpu.CompilerParams(has_side_effects=pltpu.SideEffectType.DATAFLOW_SIDE_EFFECTING),
    )(*thru, state[0], state[1], after)
    return out[n:]


def _scatter_copies(grads, lands, send_sems, recv_sems):
    x, y, c = _coords()
    me = 4 * x + 2 * y + c
    copies = []
    for a in range(len(grads)):
        r = grads[a].shape[0] // 8
        for k in range(1, 8):
            px, py, pc = (1 - x if k & 4 else x, 1 - y if k & 2 else y, 1 - c if k & 1 else c)
            rows = pl.ds(pl.multiple_of((4 * px + 2 * py + pc) * r, r), r)
            copies.append(pltpu.make_async_remote_copy(
                src_ref=grads[a].at[rows], dst_ref=lands[a].at[me],
                send_sem=send_sems.at[7 * a + k - 1], recv_sem=recv_sems.at[7 * a + k - 1],
                device_id=(px, py, pc), device_id_type=MESH))
    return copies


def scatter_start(grads, name):
    n = len(grads)

    def body(*refs):
        srcs, lands = refs[:n], refs[n:2 * n]
        send_sems, recv_sems = refs[2 * n:2 * n + 2]
        token = refs[-1]
        for cp in _scatter_copies(srcs, lands, send_sems, recv_sems):
            cp.start()
        token[...] = jnp.zeros_like(token)

    lands = [jnp.zeros((8, g.shape[0] // 8, g.shape[1]), g.dtype) for g in grads]
    ops = [pltpu.with_memory_space_constraint(a, pltpu.HBM) for a in list(grads) + lands]
    out = pl.pallas_call(
        body, name=name,
        out_shape=[pltpu.SemaphoreType.DMA((7 * n,)), pltpu.SemaphoreType.DMA((7 * n,))]
        + [pltpu.HBM(o.shape, o.dtype) for o in ops] + [jax.ShapeDtypeStruct((8, 128), F32)],
        in_specs=[_HBM] * (2 * n),
        out_specs=[_SEM, _SEM] + [_HBM] * (2 * n) + [pl.BlockSpec(memory_space=pltpu.VMEM)],
        input_output_aliases={i: 2 + i for i in range(2 * n)},
        compiler_params=pltpu.CompilerParams(has_side_effects=pltpu.SideEffectType.DATAFLOW_SIDE_EFFECTING),
    )(*ops)
    return out[:-1], out[-1]


def scatter_wait(state, after, name):
    n = (len(state) - 2) // 2

    def body(*refs):
        srcs, lands = refs[:n], refs[n:2 * n]
        send_sems, recv_sems = refs[2 * n:2 * n + 2]
        for cp in _scatter_copies(srcs, lands, send_sems, recv_sems):
            cp.wait_send()
            cp.wait_recv()

    thru = list(state[2:])
    after = list(after) if isinstance(after, (list, tuple)) else [after]
    out = pl.pallas_call(
        body, name=name,
        out_shape=[pltpu.HBM(o.shape, o.dtype) for o in thru],
        in_specs=[_HBM] * (2 * n) + [_SEM, _SEM] + [pl.BlockSpec(memory_space=pl.ANY)] * len(after),
        out_specs=[_HBM] * (2 * n),
        input_output_aliases={i: i for i in range(2 * n)},
        compiler_params=pltpu.CompilerParams(has_side_effects=pltpu.SideEffectType.DATAFLOW_SIDE_EFFECTING),
    )(*thru, state[0], state[1], *after)
    return out[:n], out[n:]


def rs_win_add_first(g, r, sel, next_dim, col, name):
    rows, cols = r.shape[2:]

    def body(sel_ref, gk_ref, rk_ref, gs_ref, rs_ref, keep_ref, send_ref):
        keep_ref[...] = gk_ref[...] + rk_ref[...]
        send_ref[...] = (gs_ref[...] + rs_ref[...]).astype(BF16)

    def g_map(flip):
        def f(j, s):
            nxt = 1 - s[next_dim] if flip else s[next_dim]
            return (nxt, j, s[2], 0, col) if next_dim == 0 else (j, nxt, s[2], 0, col)
        return f

    def r_map(flip):
        def f(j, s):
            nxt = 1 - s[next_dim] if flip else s[next_dim]
            return (nxt, j, 0, 0) if next_dim == 0 else (j, nxt, 0, 0)
        return f

    gblk = (None, None, None, rows, cols)
    rblk = (None, None, rows, cols)
    oblk = (None, rows, cols)
    return pl.pallas_call(
        body, name=name,
        grid_spec=pltpu.PrefetchScalarGridSpec(
            num_scalar_prefetch=1, grid=(2,),
            in_specs=[pl.BlockSpec(gblk, g_map(False)), pl.BlockSpec(rblk, r_map(False)),
                      pl.BlockSpec(gblk, g_map(True)), pl.BlockSpec(rblk, r_map(True))],
            out_specs=[pl.BlockSpec(oblk, lambda j, s: (j, 0, 0)),
                       pl.BlockSpec(oblk, lambda j, s: (j, 0, 0))]),
        out_shape=[jax.ShapeDtypeStruct((2, rows, cols), F32),
                   jax.ShapeDtypeStruct((2, rows, cols), BF16)],
        compiler_params=_params(),
    )(sel, g, r, g, r)


def rs_add_second(k, r, sel, name):
    _, rows, cols = k.shape
    tr = rows // 2 if rows % 32 == 0 else rows
    nt = rows // tr

    def body(sel_ref, kk_ref, rk_ref, ks_ref, rs_ref, keep_ref, send_ref):
        keep_ref[...] = kk_ref[...] + rk_ref[...].astype(F32)
        send_ref[...] = (ks_ref[...] + rs_ref[...].astype(F32)).astype(BF16)

    blk = (None, tr, cols)
    oblk = (tr, cols)
    return pl.pallas_call(
        body, name=name,
        grid_spec=pltpu.PrefetchScalarGridSpec(
            num_scalar_prefetch=1, grid=(nt,),
            in_specs=[
                pl.BlockSpec(blk, lambda i, s: (s[0], i, 0)),
                pl.BlockSpec(blk, lambda i, s: (s[0], i, 0)),
                pl.BlockSpec(blk, lambda i, s: (1 - s[0], i, 0)),
                pl.BlockSpec(blk, lambda i, s: (1 - s[0], i, 0)),
            ],
            out_specs=[pl.BlockSpec(oblk, lambda i, s: (i, 0)),
                       pl.BlockSpec(oblk, lambda i, s: (i, 0))]),
        out_shape=[jax.ShapeDtypeStruct((rows, cols), F32),
                   jax.ShapeDtypeStruct((rows, cols), BF16)],
        compiler_params=_params(),
    )(sel, k, r, k, r)


SEG_ROWS = (4800, 5824, 6848, 4096, 0, 1024, 2048, 3072)
LAT_ROWS = QL + KVL + ROPE
N_IN = 7872


def _seg_row(j):
    return pl.multiple_of(jnp.where(j < 3, 4800 + 1024 * j, jnp.where(j == 3, 4096, (j - 4) * 1024)), 8)


def proj_matmul(h, wt_bits, token):
    t = h.shape[0]
    tm = min(2048, t)

    def body(h_ref, w_hbm, tok_ref, o_ref, wt_ref, buf, sems):
        j = pl.program_id(0)
        slot = j % 2

        def fetch(seg, into):
            return pltpu.make_async_copy(w_hbm.at[pl.ds(_seg_row(seg), D)], buf.at[into], sems.at[into])

        @pl.when(pl.program_id(1) == 0)
        def _():
            @pl.when(j == 0)
            def _():
                fetch(j, slot).start()

            fetch(j, slot).wait()

            @pl.when(j + 1 < NSEG)
            def _():
                fetch(j + 1, 1 - slot).start()

            bits = pltpu.bitcast(buf[slot], jnp.uint32)
            row = lax.broadcasted_iota(jnp.int32, (D, D // 2), 0)
            live = jnp.logical_or(j != SEG_LAT, row < LAT_ROWS)
            lo = pltpu.bitcast(bits << 16, F32)
            hi = pltpu.bitcast(bits & jnp.uint32(0xFFFF0000), F32)
            wt_ref[:, :D // 2] = jnp.where(live, lo, 0.0).astype(BF16)
            wt_ref[:, D // 2:] = jnp.where(live, hi, 0.0).astype(BF16)

        o_ref[...] = _dot_nt(h_ref[...], wt_ref[...]).astype(BF16)

    return pl.pallas_call(
        body, name="proj_matmul", grid=(NSEG, t // tm),
        in_specs=[pl.BlockSpec((tm, D), lambda j, i: (i, 0)),
                  pl.BlockSpec(memory_space=pl.ANY),
                  pl.BlockSpec((8, 128), lambda j, i: (0, 0))],
        out_specs=[pl.BlockSpec((None, tm, D), lambda j, i: (j, i, 0)),
                   pl.BlockSpec((D, D), lambda j, i: (j, 0))],
        out_shape=[jax.ShapeDtypeStruct((NSEG, t, D), BF16), jax.ShapeDtypeStruct((NP, D), BF16)],
        scratch_shapes=[pltpu.VMEM((2, D, D // 2), F32), pltpu.SemaphoreType.DMA((2,))],
        compiler_params=_params(("arbitrary", "arbitrary")),
    )(h, wt_bits, token)


def dh_matmul(dproj, wt, token, seq, b):
    tm = min(1024, seq)
    nblk = seq // tm

    per = 2

    def body(b_ref, d_ref, w_ref, tok_ref, o_ref, acc_ref):
        k = pl.program_id(1)

        @pl.when(k == 0)
        def _():
            acc_ref[...] = jnp.zeros_like(acc_ref)

        part = _dot(d_ref[0], w_ref[0:D, :])
        for j in range(1, per):
            part = part + _dot(d_ref[j], w_ref[j * D:(j + 1) * D, :])
        acc_ref[...] += part

        @pl.when(k == NSEG // per - 1)
        def _():
            o_ref[...] = acc_ref[...]

    return pl.pallas_call(
        body, name="dh_matmul",
        grid_spec=pltpu.PrefetchScalarGridSpec(
            num_scalar_prefetch=1, grid=(nblk, NSEG // per),
            in_specs=[pl.BlockSpec((per, tm, D), lambda i, k, s: (k, s[0] * nblk + i, 0)),
                      pl.BlockSpec((per * D, D), lambda i, k, s: (k, 0)),
                      pl.BlockSpec((8, 128), lambda i, k, s: (0, 0))],
            out_specs=pl.BlockSpec((tm, D), lambda i, k, s: (i, 0)),
            scratch_shapes=[pltpu.VMEM((tm, D), F32)]),
        out_shape=jax.ShapeDtypeStruct((seq, D), F32),
        compiler_params=_params(("parallel", "arbitrary")),
    )(jnp.full((1,), b, jnp.int32), dproj, wt, token)


def win_grad_matmul(h, dproj, token):
    t = h.shape[0]
    tk = min(2048, t)
    nk = t // tk

    def body(h_ref, d_ref, tok_ref, o_hbm, acc_ref, sem):
        j = pl.program_id(0)
        k = pl.program_id(1)

        @pl.when(k == 0)
        def _():
            acc_ref[...] = jnp.zeros_like(acc_ref)

        acc_ref[...] += _dot_tn(d_ref[...], h_ref[...])

        @pl.when(jnp.logical_and(k == nk - 1, j != SEG_LAT))
        def _():
            cp = pltpu.make_async_copy(acc_ref, o_hbm.at[pl.ds(_seg_row(j), D)], sem)
            cp.start()
            cp.wait()

        @pl.when(jnp.logical_and(k == nk - 1, j == SEG_LAT))
        def _():
            cp = pltpu.make_async_copy(acc_ref.at[pl.ds(0, LAT_ROWS)],
                                       o_hbm.at[pl.ds(SEG_ROWS[SEG_LAT], LAT_ROWS)], sem)
            cp.start()
            cp.wait()

    return pl.pallas_call(
        body, name="win_grad_matmul", grid=(NSEG, nk),
        in_specs=[pl.BlockSpec((tk, D), lambda j, k: (k, 0)),
                  pl.BlockSpec((None, tk, D), lambda j, k: (j, k, 0)),
                  pl.BlockSpec((8, 128), lambda j, k: (0, 0))],
        out_specs=pl.BlockSpec(memory_space=pl.ANY),
        out_shape=jax.ShapeDtypeStruct((N_IN, D), F32),
        scratch_shapes=[pltpu.VMEM((D, D), F32), pltpu.SemaphoreType.DMA],
        compiler_params=_params(("arbitrary", "arbitrary")),
    )(h, dproj, token)


def grad_matmul(a, b, name):
    t, m = a.shape
    n = b.shape[1]
    tk = min(1024, t)
    nk = t // tk

    def body(a_ref, b_ref, o_ref, acc_ref):
        k = pl.program_id(0)

        @pl.when(k == 0)
        def _():
            acc_ref[...] = jnp.zeros_like(acc_ref)

        acc_ref[...] += _dot_tn(a_ref[...], b_ref[...])

        @pl.when(k == nk - 1)
        def _():
            o_ref[...] = acc_ref[...].astype(BF16)

    return pl.pallas_call(
        body, name=name, grid=(nk,),
        in_specs=[pl.BlockSpec((tk, m), lambda k: (k, 0)),
                  pl.BlockSpec((tk, n), lambda k: (k, 0))],
        out_specs=pl.BlockSpec((m, n), lambda k: (0, 0)),
        out_shape=jax.ShapeDtypeStruct((m, n), BF16),
        scratch_shapes=[pltpu.VMEM((m, n), F32)],
        compiler_params=_params(("arbitrary",)),
    )(a, b)


def ada_gather(c8, taps8, w_ada, b_cols):
    cols = w_ada.shape[1]

    def body(c_ref, t_ref, w_ref, b_ref, call_ref, tall_ref, mod_ref, part_ref, send_sems, recv_sems):
        x, y, c = _coords()
        me = 4 * x + 2 * y + c
        peers = [(1 - x if k & 4 else x, 1 - y if k & 2 else y, 1 - c if k & 1 else c) for k in range(1, 8)]

        def rcopy(n, src, dst, peer):
            return pltpu.make_async_remote_copy(src_ref=src, dst_ref=dst, send_sem=send_sems.at[n],
                                                recv_sem=recv_sems.at[n], device_id=peer, device_id_type=MESH)

        call_ref[me] = c_ref[...]
        tall_ref[me] = t_ref[...]
        first = []
        for k, peer in enumerate(peers):
            first += [rcopy(k, c_ref, call_ref.at[me], peer), rcopy(7 + k, t_ref, tall_ref.at[me], peer)]
        for cp in first:
            cp.start()
        for cp in first:
            cp.wait()
        rows = call_ref[...].reshape(64, D).astype(BF16)
        part_ref[...] = _dot(rows, w_ref[...].astype(BF16)) + b_ref[...]
        mod_ref[me] = part_ref[pl.ds(pl.multiple_of(8 * me, 8), 8), :]
        second = []
        for k, (px, py, pc) in enumerate(peers):
            theirs = part_ref.at[pl.ds(pl.multiple_of(8 * (4 * px + 2 * py + pc), 8), 8)]
            second.append(rcopy(14 + k, theirs, mod_ref.at[me], (px, py, pc)))
        for cp in second:
            cp.start()
        for cp in second:
            cp.wait()

    vm = pl.BlockSpec(memory_space=pltpu.VMEM)
    return pl.pallas_call(
        body, name="ada_gather",
        out_shape=[jax.ShapeDtypeStruct((8, 8, D), F32), jax.ShapeDtypeStruct((8, 8, 128), F32),
                   jax.ShapeDtypeStruct((8, 8, cols), F32)],
        in_specs=[vm] * 4, out_specs=[vm] * 3,
        scratch_shapes=[pltpu.VMEM((64, cols), F32), pltpu.SemaphoreType.DMA((21,)),
                        pltpu.SemaphoreType.DMA((21,))],
        compiler_params=_params(),
    )(c8, taps8, w_ada, b_cols)


def ada_bwd(c_all, dmod_cols):
    def body(c_ref, d_ref, o_ref):
        o_ref[...] = _dot_tn(c_ref[...].astype(BF16), d_ref[...].astype(BF16))

    return pl.pallas_call(
        body, name="ada_bwd",
        out_shape=jax.ShapeDtypeStruct((c_all.shape[1], dmod_cols.shape[1]), F32),
        compiler_params=_params(),
    )(c_all, dmod_cols)


def slot_sum(g):
    def body(g_ref, o_ref):
        acc = g_ref[0]
        for s in range(1, 8):
            acc = acc + g_ref[s]
        o_ref[...] = acc

    return pl.pallas_call(
        body, name="slot_sum",
        out_shape=jax.ShapeDtypeStruct(g.shape[1:], F32),
    )(g)


def prenorm_fwd(x2, scale, shift, g_pre, seq):
    t = x2.shape[0]
    tm = min(512, seq)
    tpb = seq // tm

    def body(x_ref, sc_ref, sh_ref, g_ref, h_ref):
        xv = x_ref[...]
        r = lax.rsqrt(jnp.mean(xv * xv, axis=-1, keepdims=True) + EPS)
        hv = (xv * r * g_ref[...]) * (1.0 + sc_ref[...]) + sh_ref[...]
        h_ref[...] = hv.astype(BF16)

    per_batch = pl.BlockSpec((None, 1, D), lambda i: (i // tpb, 0, 0))
    return pl.pallas_call(
        body, name="prenorm_fwd", grid=(t // tm,),
        in_specs=[pl.BlockSpec((tm, D), lambda i: (i, 0)), per_batch, per_batch,
                  pl.BlockSpec((1, D), lambda i: (0, 0))],
        out_specs=pl.BlockSpec((tm, D), lambda i: (i, 0)),
        out_shape=jax.ShapeDtypeStruct((t, D), BF16),
        compiler_params=_params(("parallel",)),
    )(x2, scale, shift, g_pre)


def prenorm_bwd(dh, x2, dout, scale, g_pre, seq, token, b, gx_prev):
    t = x2.shape[0]
    tm = min(512, seq)
    tpb = seq // tm
    if gx_prev is None:
        gx_prev = lax.empty((t, D), F32)

    def body(b_ref, dh_ref, x_ref, do_ref, sc_ref, g_ref, tok_ref, gxp_ref, gx_ref, dsh_ref, dsc_ref, dg_ref):
        i = pl.program_id(0)
        xv = x_ref[...]
        dhv = dh_ref[...]
        g = g_ref[...]
        r = lax.rsqrt(jnp.mean(xv * xv, axis=-1, keepdims=True) + EPS)
        nrm = xv * r
        dxn = dhv * (1.0 + sc_ref[...])
        dn = dxn * g
        dx = r * (dn - nrm * jnp.mean(dn * nrm, axis=-1, keepdims=True))
        gx_ref[...] = dx + do_ref[...]

        @pl.when(i == 0)
        def _():
            dsh_ref[...] = jnp.zeros_like(dsh_ref)
            dsc_ref[...] = jnp.zeros_like(dsc_ref)
            dg_ref[...] = jnp.zeros_like(dg_ref)

        dsh_ref[...] += jnp.sum(dhv, axis=0, keepdims=True)
        dsc_ref[...] += jnp.sum(dhv * (nrm * g), axis=0, keepdims=True)
        dg_ref[...] += jnp.sum(dxn * nrm, axis=0, keepdims=True)

    row = pl.BlockSpec((tm, D), lambda i, s: (i, 0))
    grow = pl.BlockSpec((tm, D), lambda i, s: (s[0] * tpb + i, 0))
    per_batch = pl.BlockSpec((None, 1, D), lambda i, s: (s[0], 0, 0))
    vec = pl.BlockSpec((1, D), lambda i, s: (0, 0))
    return pl.pallas_call(
        body, name="prenorm_bwd",
        grid_spec=pltpu.PrefetchScalarGridSpec(
            num_scalar_prefetch=1, grid=(tpb,),
            in_specs=[row, grow, grow, per_batch, vec, pl.BlockSpec((8, 128), lambda i, s: (0, 0)),
                      pl.BlockSpec(memory_space=pl.ANY)],
            out_specs=[grow, vec, vec, vec]),
        out_shape=[jax.ShapeDtypeStruct((t, D), F32), jax.ShapeDtypeStruct((1, D), F32),
                   jax.ShapeDtypeStruct((1, D), F32), jax.ShapeDtypeStruct((1, D), F32)],
        input_output_aliases={7: 0},
        compiler_params=_params(("arbitrary",)),
    )(jnp.full((1,), b, jnp.int32), dh, x2, dout, scale, g_pre, token, gx_prev)


CONV_TC = 128


def _shift_down(u, k, rows):
    idx = lax.broadcasted_iota(jnp.int32, u.shape, 0)
    return jnp.where(idx >= k, pltpu.roll(u, k, 0), 0.0)


def _shift_up(u, k, rows):
    idx = lax.broadcasted_iota(jnp.int32, u.shape, 0)
    return jnp.where(idx < rows - k, pltpu.roll(u, rows - k, 0), 0.0)


def conv_fwd(proj, conv_w, seq):
    t = proj.shape[1]
    nb = t // seq

    def body(p_ref, w_ref, y_ref):
        av = p_ref[0].astype(F32)
        ab = p_ref[1].astype(F32)
        ac = p_ref[2].astype(F32)
        az = p_ref[3].astype(F32)
        w = w_ref[...]
        u = ac * av
        y1 = _shift_down(u, 2, seq) * w[0:1] + _shift_down(u, 1, seq) * w[1:2] + u * w[2:3]
        y_ref[...] = (ab * y1 * (az * _sig(az))).astype(BF16)

    return pl.pallas_call(
        body, name="conv_fwd", grid=(nb, D // CONV_TC),
        in_specs=[pl.BlockSpec((4, seq, CONV_TC), lambda b, ci: (1, b, ci)),
                  pl.BlockSpec((8, CONV_TC), lambda b, ci: (0, ci))],
        out_specs=pl.BlockSpec((seq, CONV_TC), lambda b, ci: (b, ci)),
        out_shape=jax.ShapeDtypeStruct((t, D), BF16),
        compiler_params=_params(("parallel", "parallel")),
    )(proj, conv_w)


def conv_bwd(dproj, proj, dy, conv_w, seq):
    t = proj.shape[1]
    nb = t // seq

    def body(dp_in_ref, p_ref, dy_ref, w_ref, dp_ref, dw_ref):
        b = pl.program_id(1)
        av = p_ref[0].astype(F32)
        ab = p_ref[1].astype(F32)
        ac = p_ref[2].astype(F32)
        az = p_ref[3].astype(F32)
        dyv = dy_ref[...].astype(F32)
        w = w_ref[...]
        u = ac * av
        u1 = _shift_down(u, 1, seq)
        u2 = _shift_down(u, 2, seq)
        y1 = u2 * w[0:1] + u1 * w[1:2] + u * w[2:3]
        sz = _sig(az)
        silu = az * sz
        dy1 = dyv * ab * silu
        du = dy1 * w[2:3] + _shift_up(dy1, 1, seq) * w[1:2] + _shift_up(dy1, 2, seq) * w[0:1]
        dp_ref[0] = (du * ac).astype(BF16)
        dp_ref[1] = (dyv * y1 * silu).astype(BF16)
        dp_ref[2] = (du * av).astype(BF16)
        dp_ref[3] = (dyv * ab * y1 * (sz * (1.0 + az * (1.0 - sz)))).astype(BF16)

        @pl.when(b == 0)
        def _():
            dw_ref[...] = jnp.zeros_like(dw_ref)

        dw_ref[0:1, :] += jnp.sum(dy1 * u2, axis=0, keepdims=True)
        dw_ref[1:2, :] += jnp.sum(dy1 * u1, axis=0, keepdims=True)
        dw_ref[2:3, :] += jnp.sum(dy1 * u, axis=0, keepdims=True)

    return pl.pallas_call(
        body, name="conv_bwd", grid=(D // CONV_TC, nb),
        in_specs=[pl.BlockSpec(memory_space=pl.ANY),
                  pl.BlockSpec((4, seq, CONV_TC), lambda ci, b: (1, b, ci)),
                  pl.BlockSpec((seq, CONV_TC), lambda ci, b: (b, ci)),
                  pl.BlockSpec((8, CONV_TC), lambda ci, b: (0, ci))],
        out_specs=[pl.BlockSpec((4, seq, CONV_TC), lambda ci, b: (1, b, ci)),
                   pl.BlockSpec((8, CONV_TC), lambda ci, b: (0, ci))],
        out_shape=[jax.ShapeDtypeStruct(dproj.shape, BF16),
                   jax.ShapeDtypeStruct((8, D), F32)],
        input_output_aliases={0: 0},
        compiler_params=_params(("parallel", "arbitrary")),
    )(dproj, proj, dy, conv_w)


def _rope_tables(pos_ref, invf_ref, ma_ref, mb_ref, sign):
    ang = pos_ref[...].astype(F32) * invf_ref[...]
    cs = jnp.cos(ang)
    sn = jnp.sin(ang) * sign
    return cs, sn * ma_ref[...], sn * mb_ref[...]


def _rotate(v, cs, sa, sb):
    return v * cs + pltpu.roll(v, 128 - HALF, 1) * sa + pltpu.roll(v, HALF, 1) * sb


MLA_TM = 512


def mla_prep_fwd(proj, pos, g_q, g_kv, wuq, wukv, tabs):
    t = proj.shape[1]
    tm = min(MLA_TM, t)

    def body(lat_ref, pos_ref, gq_ref, gkv_ref, wuq_ref, wukv_ref, invf_ref, ma_ref, mb_ref,
             q_ref, k_ref, kv_ref, qn_ref, kvn_ref):
        lat = lat_ref[...].astype(F32)
        ql = lat[:, :QL]
        kl = lat[:, QL:QL + KVL]
        kr = lat[:, QL + KVL:QL + KVL + 128]
        qn = (ql * lax.rsqrt(jnp.mean(ql * ql, axis=-1, keepdims=True) + EPS) * gq_ref[...]).astype(BF16)
        kvn = (kl * lax.rsqrt(jnp.mean(kl * kl, axis=-1, keepdims=True) + EPS) * gkv_ref[...]).astype(BF16)
        qn_ref[...] = qn
        kvn_ref[...] = kvn
        cs, sa, sb = _rope_tables(pos_ref, invf_ref, ma_ref, mb_ref, 1.0)
        q = _dot_nt(qn, wuq_ref[...]) * (SM_SCALE * LOG2E)
        kv = _dot_nt(kvn, wukv_ref[...]).astype(BF16)
        kv_ref[...] = kv
        kpe = _rotate(kr, cs, sa, sb).astype(BF16)
        for hh in range(H):
            lo, mid, hi = hh * DQK, hh * DQK + 128, (hh + 1) * DQK
            q_ref[:, lo:mid] = q[:, lo:mid].astype(BF16)
            q_ref[:, mid:hi] = _rotate(q[:, mid:hi], cs, sa, sb).astype(BF16)
            k_ref[:, lo:mid] = kv[:, lo:mid]
            k_ref[:, mid:hi] = kpe

    row = lambda w: pl.BlockSpec((tm, w), lambda i: (i, 0))
    const = lambda a: pl.BlockSpec(a.shape, lambda i: (0,) * a.ndim)
    return pl.pallas_call(
        body, name="mla_prep_fwd", grid=(t // tm,),
        in_specs=[pl.BlockSpec((None, tm, D), lambda i: (SEG_LAT, i, 0)), row(1),
                  const(g_q), const(g_kv), const(wuq), const(wukv)] + [const(a) for a in tabs],
        out_specs=[row(H * DQK), row(H * DQK), row(H * DQK), row(QL), row(KVL)],
        out_shape=[jax.ShapeDtypeStruct((t, H * DQK), BF16)] * 3
        + [jax.ShapeDtypeStruct((t, QL), BF16), jax.ShapeDtypeStruct((t, KVL), BF16)],
        compiler_params=_params(("parallel",)),
    )(proj, pos, g_q, g_kv, wuq, wukv, *tabs)


def mla_prep_bwd(dproj, proj, dq_rot, dk, dv, pos, g_q, g_kv, wuq, wukv, tabs):
    t = proj.shape[1]
    tm = min(MLA_TM, t)

    def body(dp_in_ref, lat_ref, dqr_ref, dk_ref, dv_ref, pos_ref, gq_ref, gkv_ref, wuq_ref, wukv_ref,
             invf_ref, ma_ref, mb_ref, dp_ref, dq_ref, dkv_ref, dgq_ref, dgkv_ref):
        i = pl.program_id(0)
        lat = lat_ref[...].astype(F32)
        ql = lat[:, :QL]
        kl = lat[:, QL:QL + KVL]
        rq = lax.rsqrt(jnp.mean(ql * ql, axis=-1, keepdims=True) + EPS)
        rk = lax.rsqrt(jnp.mean(kl * kl, axis=-1, keepdims=True) + EPS)
        nq = ql * rq
        nk = kl * rk
        cs, sa, sb = _rope_tables(pos_ref, invf_ref, ma_ref, mb_ref, -1.0)
        dkpe = jnp.zeros((tm, 128), F32)
        for hh in range(H):
            lo, mid, hi = hh * DQK, hh * DQK + 128, (hh + 1) * DQK
            dq_ref[:, lo:mid] = (dqr_ref[:, lo:mid] * SM_SCALE).astype(BF16)
            dq_ref[:, mid:hi] = _rotate(dqr_ref[:, mid:hi] * SM_SCALE, cs, sa, sb).astype(BF16)
            dkv_ref[:, lo:mid] = dk_ref[:, lo:mid]
            dkv_ref[:, mid:hi] = dv_ref[:, hh * DV:(hh + 1) * DV]
            dkpe = dkpe + dk_ref[:, mid:hi].astype(F32)
        lane = lax.broadcasted_iota(jnp.int32, (tm, 128), 1)
        dkr = jnp.where(lane < ROPE, _rotate(dkpe, cs, sa, sb), 0.0)
        dqn = _dot(dq_ref[...], wuq_ref[...])
        dkvn = _dot(dkv_ref[...], wukv_ref[...])
        gq = gq_ref[...]
        gkv = gkv_ref[...]
        dnq = dqn * gq
        dnk = dkvn * gkv
        dql = rq * (dnq - nq * jnp.mean(dnq * nq, axis=-1, keepdims=True))
        dkl = rk * (dnk - nk * jnp.mean(dnk * nk, axis=-1, keepdims=True))
        dp_ref[:, :QL] = dql.astype(BF16)
        dp_ref[:, QL:QL + KVL] = dkl.astype(BF16)
        dp_ref[:, QL + KVL:QL + KVL + 128] = dkr.astype(BF16)
        dp_ref[:, QL + KVL + 128:] = jnp.zeros((tm, D - QL - KVL - 128), BF16)

        @pl.when(i == 0)
        def _():
            dgq_ref[...] = jnp.zeros_like(dgq_ref)
            dgkv_ref[...] = jnp.zeros_like(dgkv_ref)

        dgq_ref[...] += jnp.sum(dqn * nq, axis=0, keepdims=True)
        dgkv_ref[...] += jnp.sum(dkvn * nk, axis=0, keepdims=True)

    row = lambda w: pl.BlockSpec((tm, w), lambda i: (i, 0))
    const = lambda a: pl.BlockSpec(a.shape, lambda i: (0,) * a.ndim)
    seg = pl.BlockSpec((None, tm, D), lambda i: (SEG_LAT, i, 0))
    return pl.pallas_call(
        body, name="mla_prep_bwd", grid=(t // tm,),
        in_specs=[pl.BlockSpec(memory_space=pl.ANY), seg, row(H * DQK), row(H * DQK), row(H * DV), row(1),
                  const(g_q), const(g_kv), const(wuq), const(wukv)] + [const(a) for a in tabs],
        out_specs=[seg, row(H * DQK), row(H * DQK),
                   pl.BlockSpec((1, QL), lambda i: (0, 0)), pl.BlockSpec((1, KVL), lambda i: (0, 0))],
        out_shape=[jax.ShapeDtypeStruct(dproj.shape, BF16),
                   jax.ShapeDtypeStruct((t, H * DQK), BF16), jax.ShapeDtypeStruct((t, H * DQK), BF16),
                   jax.ShapeDtypeStruct((1, QL), F32), jax.ShapeDtypeStruct((1, KVL), F32)],
        input_output_aliases={0: 0},
        compiler_params=_params(("arbitrary",)),
    )(dproj, proj, dq_rot, dk, dv, pos, g_q, g_kv, wuq, wukv, *tabs)


def _causal_mask(s, shift):
    row = lax.broadcasted_iota(jnp.int32, s.shape, 0)
    col = lax.broadcasted_iota(jnp.int32, s.shape, 1)
    return jnp.where(col <= row + shift, s, -1e30)


def flash_fwd(q, k, kv, nb, seq):
    t = q.shape[0]
    tq = min(FLASH_TQ, seq // 2)
    nq = seq // tq
    assert nq % 2 == 0, "blocks are processed in pairs"

    def update(state, s, vblk):
        m, l, acc = state
        m_new = jnp.maximum(m, jnp.max(s, axis=1, keepdims=True))
        p = jnp.exp2(s - m_new)
        alpha = jnp.exp2(m - m_new)
        return (m_new, alpha * l + jnp.sum(p, axis=1, keepdims=True),
                alpha * acc + _dot(p.astype(BF16), vblk))

    def finish(state, rows, o_ref, lse_ref):
        m, l, acc = state
        o_ref[rows, :] = (acc / l).astype(BF16)
        lse_ref[rows, :] = jnp.broadcast_to(m + jnp.log(l) * LOG2E, (m.shape[0], DV))

    def body(q_ref, k_ref, v_ref, o_ref, lse_ref):
        for qp in range(0, nq, 2):
            rows = 2 * tq
            q0 = qp * tq
            qv = q_ref[q0:q0 + rows, :]
            state = (jnp.full((rows, 1), -1e30, F32), jnp.zeros((rows, 1), F32), jnp.zeros((rows, DV), F32))
            for j in range(qp + 1):
                ks = slice(j * tq, (j + 1) * tq)
                s = _dot_nt(qv, k_ref[ks, :])
                if j == qp:
                    s = _causal_mask(s, 0)
                state = update(state, s, v_ref[ks, :])
            finish(tuple(a[:tq] for a in state), slice(q0, q0 + tq), o_ref, lse_ref)
            ks = slice(q0 + tq, q0 + 2 * tq)
            low = tuple(a[tq:] for a in state)
            low = update(low, _causal_mask(_dot_nt(qv[tq:], k_ref[ks, :]), 0), v_ref[ks, :])
            finish(low, slice(q0 + tq, q0 + 2 * tq), o_ref, lse_ref)

    out_blk = pl.BlockSpec((seq, DV), lambda b, h: (b, h))
    return pl.pallas_call(
        body, name="flash_fwd", grid=(nb, H),
        in_specs=[pl.BlockSpec((seq, DQK), lambda b, h: (b, h)),
                  pl.BlockSpec((seq, DQK), lambda b, h: (b, h)),
                  pl.BlockSpec((seq, DV), lambda b, h: (b, 2 * h + 1))],
        out_specs=[out_blk, out_blk],
        out_shape=[jax.ShapeDtypeStruct((t, H * DV), BF16), jax.ShapeDtypeStruct((t, H * DV), F32)],
        compiler_params=_params(("parallel", "parallel")),
    )(q, k, kv)


def flash_bwd(q, k, kv, o, do, lse, nb, seq, token):
    t = q.shape[0]
    tq = min(FLASH_TQ, seq)
    nq = seq // tq

    def body(q_ref, k_ref, v_ref, o_ref, do_ref, lse_ref, tok_ref, dq_ref, dk_ref, dv_ref):
        delta, lse = [], []
        for qi in range(nq):
            qs = slice(qi * tq, (qi + 1) * tq)
            dl = jnp.sum(do_ref[qs, :].astype(F32) * o_ref[qs, :].astype(F32), axis=1, keepdims=True)
            delta.append(jnp.broadcast_to(dl, (tq, DV)).T[:1, :])
            lse.append(lse_ref[qs, :].T[:1, :])
        for ki in range(nq):
            ks = slice(ki * tq, (ki + 1) * tq)
            kb = k_ref[ks, :]
            vb = v_ref[ks, :]
            dk = jnp.zeros((tq, DQK), F32)
            dv = jnp.zeros((tq, DV), F32)
            for qi in range(ki, nq):
                qs = slice(qi * tq, (qi + 1) * tq)
                qv = q_ref[qs, :]
                dov = do_ref[qs, :]
                st = _dot_nt(kb, qv)
                if qi == ki:
                    row = lax.broadcasted_iota(jnp.int32, st.shape, 0)
                    col = lax.broadcasted_iota(jnp.int32, st.shape, 1)
                    st = jnp.where(row <= col, st, -1e30)
                pt = jnp.exp2(st - lse[qi])
                dpt = _dot_nt(vb, dov)
                dzt = (pt * (dpt - delta[qi])).astype(BF16)
                dv = dv + _dot(pt.astype(BF16), dov)
                dk = dk + _dot(dzt, qv)
                dqb = _dot_tn(dzt, kb)
                if ki == 0:
                    dq_ref[qs, :] = dqb
                else:
                    dq_ref[qs, :] += dqb
            dk_ref[ks, :] = (dk * LN2).astype(BF16)
            dv_ref[ks, :] = dv.astype(BF16)

    full = lambda w, col: pl.BlockSpec((seq, w), col)
    same = lambda b, h: (b, h)
    return pl.pallas_call(
        body, name="flash_bwd", grid=(nb, H),
        in_specs=[full(DQK, same), full(DQK, same), full(DV, lambda b, h: (b, 2 * h + 1)),
                  full(DV, same), full(DV, same), full(DV, same),
                  pl.BlockSpec((8, 128), lambda b, h: (0, 0))],
        out_specs=[full(DQK, same), full(DQK, same), full(DV, same)],
        out_shape=[jax.ShapeDtypeStruct((t, H * DQK), F32), jax.ShapeDtypeStruct((t, H * DQK), BF16),
                   jax.ShapeDtypeStruct((t, H * DV), BF16)],
        compiler_params=_params(("parallel", "parallel")),
    )(q, k, kv, o, do, lse, token)


TAIL_TM = 512


def tail_fwd(y, attn, proj, x2, tgt, gate, g_post, wco, wmo, wout, seq):
    t = y.shape[0]
    nb = t // seq
    tm = min(TAIL_TM, seq)
    tpb = seq // tm

    def body(y_ref, at_ref, p_ref, x_ref, t_ref, gate_ref, gp_ref, wco_ref, wmo_ref, wout_ref,
             o_ref, ya_ref, yb_ref, m_ref, do2_ref, dout_ref, dgate_ref, dgp_ref, loss_ref):
        i = pl.program_id(0)
        bz = p_ref[0].astype(F32)
        ga = p_ref[1].astype(F32)
        gb = p_ref[2].astype(F32)
        ov = (at_ref[...].astype(F32) * (bz * _sig(bz))).astype(BF16)
        o_ref[...] = ov
        ya = _dot(y_ref[...], wco_ref[...])
        yb = _dot(ov, wmo_ref[...])
        ya_ref[...] = ya.astype(BF16)
        yb_ref[...] = yb.astype(BF16)
        mv = (_sig(ga) * ya + _sig(gb) * yb).astype(BF16)
        m_ref[...] = mv
        o2 = _dot(mv, wout_ref[...])
        r = lax.rsqrt(jnp.mean(o2 * o2, axis=-1, keepdims=True) + EPS)
        nrm = o2 * r
        gp = gp_ref[...]
        gate_v = gate_ref[...]
        rn = nrm * gp
        err = x_ref[...] + gate_v * rn - t_ref[...]
        dout = err * (1.0 / D)
        dout_ref[...] = dout
        dn = dout * gate_v * gp
        do2_ref[...] = (r * (dn - nrm * jnp.mean(dn * nrm, axis=-1, keepdims=True))).astype(BF16)

        @pl.when(i % tpb == 0)
        def _():
            dgate_ref[...] = jnp.zeros_like(dgate_ref)

        @pl.when(i == 0)
        def _():
            dgp_ref[...] = jnp.zeros_like(dgp_ref)
            loss_ref[...] = jnp.zeros_like(loss_ref)

        dgate_ref[...] += jnp.sum(dout * rn, axis=0, keepdims=True)
        dgp_ref[...] += jnp.sum(dout * gate_v * nrm, axis=0, keepdims=True)
        loss_ref[...] += 0.5 * jnp.sum(jnp.mean(err * err, axis=-1, keepdims=True), axis=0, keepdims=True)

    row = pl.BlockSpec((tm, D), lambda i: (i, 0))
    per_batch = pl.BlockSpec((None, 1, D), lambda i: (i // tpb, 0, 0))
    vec = pl.BlockSpec((1, D), lambda i: (0, 0))
    wgt = pl.BlockSpec((D, D), lambda i: (0, 0))
    act = jax.ShapeDtypeStruct((t, D), BF16)
    return pl.pallas_call(
        body, name="tail_fwd", grid=(t // tm,),
        in_specs=[row, row, pl.BlockSpec((3, tm, D), lambda i: (0, i, 0)), row, row, per_batch, vec,
                  wgt, wgt, wgt],
        out_specs=[row, row, row, row, row, row, per_batch, vec, pl.BlockSpec((1, 1), lambda i: (0, 0))],
        out_shape=[act, act, act, act, act, jax.ShapeDtypeStruct((t, D), F32),
                   jax.ShapeDtypeStruct((nb, 1, D), F32), jax.ShapeDtypeStruct((1, D), F32),
                   jax.ShapeDtypeStruct((1, 1), F32)],
        compiler_params=_params(("arbitrary",)),
    )(y, attn, proj, x2, tgt, gate, g_post, wco, wmo, wout)


def tail_bwd(do2, proj, ya, yb, attn, wout, wmo, wco):
    t = do2.shape[0]
    tm = min(TAIL_TM, t)

    def body(do2_ref, p_ref, ya_ref, yb_ref, at_ref, wout_ref, wmo_ref, wco_ref,
             dp_ref, dya_ref, dyb_ref, dat_ref, dy_ref):
        bz = p_ref[0].astype(F32)
        ga = p_ref[1].astype(F32)
        gb = p_ref[2].astype(F32)
        dm = _dot_nt(do2_ref[...], wout_ref[...])
        sa = _sig(ga)
        sb = _sig(gb)
        dya = (dm * sa).astype(BF16)
        dyb = (dm * sb).astype(BF16)
        dya_ref[...] = dya
        dyb_ref[...] = dyb
        dp_ref[1] = (dm * ya_ref[...].astype(F32) * (sa * (1.0 - sa))).astype(BF16)
        dp_ref[2] = (dm * yb_ref[...].astype(F32) * (sb * (1.0 - sb))).astype(BF16)
        dov = _dot_nt(dyb, wmo_ref[...])
        sz = _sig(bz)
        dat_ref[...] = (dov * (bz * sz)).astype(BF16)
        dp_ref[0] = (dov * at_ref[...].astype(F32) * (sz * (1.0 + bz * (1.0 - sz)))).astype(BF16)
        dy_ref[...] = _dot_nt(dya, wco_ref[...]).astype(BF16)

    row = pl.BlockSpec((tm, D), lambda i: (i, 0))
    seg3 = pl.BlockSpec((3, tm, D), lambda i: (0, i, 0))
    wgt = pl.BlockSpec((D, D), lambda i: (0, 0))
    act = jax.ShapeDtypeStruct((t, D), BF16)
    return pl.pallas_call(
        body, name="tail_bwd", grid=(t // tm,),
        in_specs=[row, seg3, row, row, row, wgt, wgt, wgt],
        out_specs=[seg3, row, row, row, row],
        out_shape=[jax.ShapeDtypeStruct((NSEG, t, D), BF16), act, act, act, act],
        compiler_params=_params(("parallel",)),
    )(do2, proj, ya, yb, attn, wout, wmo, wco)


def _adam_update(w, m, v, grad):
    mn = ADAM_B1 * m + (1.0 - ADAM_B1) * grad
    vn = ADAM_B2 * v + (1.0 - ADAM_B2) * (grad * grad)
    m_hat = mn / (1.0 - ADAM_B1 ** ADAM_STEP)
    v_hat = vn / (1.0 - ADAM_B2 ** ADAM_STEP)
    return -ADAM_LR * (m_hat / (jnp.sqrt(v_hat) + ADAM_EPS) + ADAM_WD * w), mn, vn


def adamw(w, m, v, g, name, token):
    rows, cols = w.shape
    tr = rows
    for cand in (256, 128, 64, 32, 16, 8):
        if rows % cand == 0 and rows > cand:
            tr = cand
            break

    def body(w_ref, m_ref, v_ref, g_ref, tok_ref, d_ref, mo_ref, vo_ref):
        d_ref[...], mo_ref[...], vo_ref[...] = _adam_update(w_ref[...], m_ref[...], v_ref[...], g_ref[...])

    blk = pl.BlockSpec((tr, cols), lambda i: (i, 0))
    return pl.pallas_call(
        body, name=name, grid=(rows // tr,),
        in_specs=[blk] * 4 + [pl.BlockSpec((8, 128), lambda i: (0, 0))], out_specs=[blk] * 3,
        out_shape=[jax.ShapeDtypeStruct((rows, cols), F32)] * 3,
        compiler_params=_params(("parallel",)),
    )(w, m, v, g, token)


def adamw_scattered(w, m, v, own, land, me, tr, name, transpose=False):
    slot_rows = land.shape[1]
    cols = land.shape[2]
    rows = slot_rows if transpose else w.shape[0]
    per_slot = slot_rows // tr

    def body(me_ref, w_ref, m_ref, v_ref, own_ref, land_ref, go_ref, d_ref, mo_ref, vo_ref):
        grad = own_ref[...].astype(F32)
        for s in range(8):
            grad = grad + land_ref[s].astype(F32)
        if transpose:
            grad = grad.T
        go_ref[...] = grad
        d_ref[...], mo_ref[...], vo_ref[...] = _adam_update(w_ref[...], m_ref[...], v_ref[...], grad)

    wblk = pl.BlockSpec(w.shape if transpose else (tr, w.shape[1]), lambda i, s: (i, 0))
    return pl.pallas_call(
        body, name=name,
        grid_spec=pltpu.PrefetchScalarGridSpec(
            num_scalar_prefetch=1, grid=(rows // tr,),
            in_specs=[wblk, wblk, wblk,
                      pl.BlockSpec((tr, cols), lambda i, s: (s[0] * per_slot + i, 0)),
                      pl.BlockSpec((8, tr, cols), lambda i, s: (0, i, 0))],
            out_specs=[wblk] * 4),
        out_shape=[jax.ShapeDtypeStruct(w.shape, F32)] * 4,
        compiler_params=_params(),
    )(me, w, m, v, own, land)


def adamw_win(wt, mt, vt, ka, ra, kb, rb):
    rows = wt.shape[0]
    tc = 256
    nh = (D // 2) // tc

    def body(w_ref, m_ref, v_ref, ka_ref, ra_ref, kb_ref, rb_ref, go_ref, d_ref, mo_ref, vo_ref):
        first = pl.program_id(0) < nh
        grad = jnp.where(first, ka_ref[...] + ra_ref[...].astype(F32), kb_ref[...] + rb_ref[...].astype(F32))
        go_ref[...] = grad
        d_ref[...], mo_ref[...], vo_ref[...] = _adam_update(w_ref[...], m_ref[...], v_ref[...], grad)

    blk = pl.BlockSpec((rows, tc), lambda j: (0, j))
    lo = pl.BlockSpec((rows, tc), lambda j: (0, jnp.minimum(j, nh - 1)))
    hi = pl.BlockSpec((rows, tc), lambda j: (0, jnp.maximum(j - nh, 0)))
    return pl.pallas_call(
        body, name="adamw_w_in", grid=(D // tc,),
        in_specs=[blk, blk, blk, lo, lo, hi, hi], out_specs=[blk] * 4,
        out_shape=[jax.ShapeDtypeStruct((rows, D), F32)] * 4,
        compiler_params=_params(("parallel",)),
    )(wt, mt, vt, ka, ra, kb, rb)


_ORD_A = ("x", "y", "c")
_ORD_B = ("y", "x", "c")


def _rows128(a, rows):
    flat = a.reshape(-1)
    return jnp.pad(flat, (0, rows * 128 - flat.shape[0])).reshape(rows, 128)


def kernel(x, c, positions, w_ada, b_ada, g_pre, w_in, conv_w, w_conv_out, g_q, w_uq, g_kv, w_ukv, w_mla_out, w_out, g_post, loss_target, m_w_ada, m_b_ada, m_g_pre, m_w_in, m_conv_w, m_w_conv_out, m_g_q, m_w_uq, m_g_kv, m_w_ukv, m_w_mla_out, m_w_out, m_g_post, v_w_ada, v_b_ada, v_g_pre, v_w_in, v_conv_w, v_w_conv_out, v_g_q, v_w_uq, v_g_kv, v_w_ukv, v_w_mla_out, v_w_out, v_g_post):
    nb, seq, _ = x.shape
    t = nb * seq
    mx, my, mc = lax.axis_index("x"), lax.axis_index("y"), lax.axis_index("c")
    me = 4 * mx + 2 * my + mc
    co = {"x": mx, "y": my, "c": mc}

    x2 = x.reshape(t, D)
    tgt2 = loss_target.reshape(t, D)
    pos2 = positions.reshape(t, 1)

    ada_cols = w_ada.shape[2]
    b_cols = lax.dynamic_slice(b_ada, (0, me * ada_cols), (1, ada_cols))
    c_g, taps_g, mod_g = ada_gather(jnp.pad(c, ((0, 8 - nb), (0, 0))), _rows128(conv_w[0], 8), w_ada[0], b_cols)
    c_all = c_g[:, :nb].reshape(8 * nb, D)
    conv_full = taps_g[:, 0:3].transpose(1, 0, 2).reshape(3, D)
    conv_full8 = jnp.pad(conv_full, ((0, 5), (0, 0)))
    mod = mod_g[:, :nb].transpose(1, 0, 2).reshape(nb, 8 * ada_cols)
    shift = mod[:, 0:D].reshape(nb, 1, D)
    scale = mod[:, D:2 * D].reshape(nb, 1, D)
    gate = mod[:, 2 * D:3 * D].reshape(nb, 1, D)

    wt = w_in[0].T.astype(BF16)
    lo = lax.bitcast_convert_type(wt[:, :D // 2], jnp.uint16).astype(jnp.uint32)
    hi = lax.bitcast_convert_type(wt[:, D // 2:], jnp.uint16).astype(jnp.uint32)
    wt_bits = lax.bitcast_convert_type(lo | (hi << 16), F32)
    wt_bits, mod = lax.optimization_barrier((wt_bits, mod))
    shift = mod[:, 0:D].reshape(nb, 1, D)
    scale = mod[:, D:2 * D].reshape(nb, 1, D)
    gate = mod[:, 2 * D:3 * D].reshape(nb, 1, D)
    q4 = D // 4
    r3rd = wt_bits.shape[0] // 3
    plan = [(0, (k * r3rd, r3rd), (g * q4, q4), (_ORD_A, _ORD_B)[g]) for k in range(3) for g in range(2)]
    gw = allgather_big([wt_bits], plan, "gather_w_in")
    late = [w_conv_out[0].astype(BF16), w_mla_out[0].astype(BF16), w_out[0].astype(BF16),
            jnp.pad(w_uq[0].T.astype(BF16), ((0, DQK - 192), (0, 0))), w_ukv[0].T.astype(BF16)]
    gw0, late = lax.optimization_barrier((gw[0], late))
    late_state, late_token = gather_start(late, "gather_late_start")
    wt_bits_all = gw0.reshape(N_IN, D // 2)

    inv_freq = ROPE_THETA ** (-jnp.arange(0, ROPE, 2, dtype=F32) / ROPE)
    invf = jnp.concatenate([inv_freq, inv_freq, jnp.zeros((128 - ROPE,), F32)]).reshape(1, 128)
    lane = np.arange(128)
    tabs = (invf,
            jnp.asarray(np.where(lane < HALF, -1.0, 0.0).reshape(1, 128), F32),
            jnp.asarray(np.where((lane >= HALF) & (lane < ROPE), 1.0, 0.0).reshape(1, 128), F32))

    h = prenorm_fwd(x2, scale, shift, g_pre, seq)
    proj, wt_p = proj_matmul(h, wt_bits_all, late_token)
    y = conv_fwd(proj, conv_full8, seq)
    gl = gather_wait(late_state, y, "gather_late_wait")
    wco = gl[0].reshape(D, D)
    wmo = gl[1].reshape(D, D)
    wout = gl[2].reshape(D, D)
    wuq_p = gl[3].reshape(H * DQK, QL)
    wukv = gl[4].reshape(H * 256, KVL)
    q_rot, k_cat, kv, qn, kvn = mla_prep_fwd(proj, pos2, g_q, g_kv, wuq_p, wukv, tabs)
    attn, lse = flash_fwd(q_rot, k_cat, kv, nb, seq)
    o, ya, yb, m, do2, dout, dgate, dg_post, loss_part = tail_fwd(
        y, attn, proj, x2, tgt2, gate, g_post, wco, wmo, wout, seq)

    dproj, dya, dyb, dattn, dy = tail_bwd(do2, proj, ya, yb, attn, wout, wmo, wco)
    g_wout = grad_matmul(m, do2, "grad_w_square")
    g_wmo = grad_matmul(o, dyb, "grad_w_square")
    g_wco = grad_matmul(y, dya, "grad_w_square")
    sc1, sc1_tok = scatter_start([g_wco, g_wmo, g_wout], "scatter_out_grads_start")
    dproj, dconv = conv_bwd(dproj, proj, dy, conv_full8, seq)
    dq_rot, dk, dv = flash_bwd(q_rot, k_cat, kv, attn, dattn, lse, nb, seq, sc1_tok)
    dproj, dq, dkv, dg_q, dg_kv = mla_prep_bwd(dproj, proj, dq_rot, dk, dv, pos2, g_q, g_kv, wuq_p, wukv, tabs)
    g_wuq_t = grad_matmul(dq, qn, "grad_w_uq")
    g_wukv_t = grad_matmul(dkv, kvn, "grad_w_ukv")
    sc2, sc2_tok = scatter_start([g_wuq_t, g_wukv_t], "scatter_mla_grads_start")
    g_win_p = win_grad_matmul(h, dproj, sc2_tok)

    g_wt = g_win_p.reshape(2, 2, 2, N_IN // 8, D)
    ords = [("c", "y", "x"), ("c", "x", "y")]
    hc = D // 2
    win_shape = (2, 2, N_IN // 8, hc)
    pick_w = lambda col: (lambda ref, cc: ref.at[:, :, 1 - cc["c"], :, pl.ds(col * hc, hc)])
    which1 = [0, 0]
    picks1 = [pick_w(0), pick_w(1)]
    st1, tok1 = swap_start([g_wt], which1, ["c"] * 2, picks1, [win_shape] * 2, "rs_c_start")
    assert nb == 2
    dh0 = dh_matmul(dproj, wt_p, tok1, seq, 0)
    (g_wt,), r1 = swap_wait(st1, dh0, which1, ["c"] * 2, picks1, "rs_c_wait")
    sel_xyc = jnp.stack([mx, my, mc]).astype(jnp.int32)
    sel2 = [jnp.stack([co[o[2]]]).astype(jnp.int32) for o in ords]
    first = [rs_win_add_first(g_wt, r1[0], sel_xyc, 1, 0, "rs_add_first_0"),
             rs_win_add_first(g_wt, r1[1], sel_xyc, 0, 1, "rs_add_first_1")]
    keep1, send1 = zip(*first)
    all4 = [0, 1]
    none4 = [None] * 2
    axes2 = [o[1] for o in ords]
    st2, tok2 = swap_start(list(send1), all4, axes2, none4, [s.shape for s in send1], "rs_ici1_start")

    dh1 = dh_matmul(dproj, wt_p, tok2, seq, 1)
    gx0, dsh0, dsc0, dgp0 = prenorm_bwd(dh0, x2, dout, scale, g_pre, seq, tok2, 0, None)
    _, r2 = swap_wait(st2, (gx0, dh1), all4, axes2, none4, "rs_ici1_wait")
    keep2, send2 = zip(*[rs_add_second(keep1[a], r2[a], sel2[a], "rs_add_second") for a in range(2)])
    axes3 = [o[2] for o in ords]
    st3, tok3 = swap_start(list(send2), all4, axes3, none4, [s.shape for s in send2], "rs_ici2_start")
    grad_x2, dsh1, dsc1, dgp1 = prenorm_bwd(dh1, x2, dout, scale, g_pre, seq, tok3, 1, gx0)
    dshift = jnp.stack([dsh0, dsh1])
    dscale = jnp.stack([dsc0, dsc1])
    dg_pre = dgp0 + dgp1

    dmod = jnp.concatenate([dshift, dscale, dgate], axis=2).reshape(nb * 3 * D // 128, 128)
    small = jnp.concatenate([
        dmod, _rows128(dg_pre, 8), _rows128(dg_post, 8), _rows128(dg_q, 8), _rows128(dg_kv, 8),
        dconv[0:3].reshape(24, 128), _rows128(loss_part, 8)], axis=0)
    small_g = small_allgather(small, "gather_small_grads")
    sums = slot_sum(small_g)
    dmod_all = small_g[:, 0:48].reshape(8 * nb, 3 * D)
    g_bada = (sums[0:24] + sums[24:48]).reshape(1, 3 * D)
    g_gpre = sums[48:56].reshape(1, D)
    g_gpost = sums[56:64].reshape(1, D)
    g_gq = sums[64:67].reshape(1, QL)
    g_gkv = sums[72:74].reshape(1, KVL)
    g_conv_full = sums[80:104].reshape(3, D)
    loss = sums[104, 0]
    g_conv = lax.dynamic_slice(g_conv_full, (0, me * 128), (3, 128))
    dmod_cols = lax.dynamic_slice(dmod_all, (0, me * ada_cols), (8 * nb, ada_cols))
    g_wada = ada_bwd(c_all, dmod_cols)

    res = {}
    res["w_ada"] = [o_[None] for o_ in (g_wada, *adamw(w_ada[0], m_w_ada[0], v_w_ada[0], g_wada, "adamw_w_ada", tok3))]

    def pack(b_, gp_, gpo_, gq_, gkv_, cw_):
        return jnp.concatenate([_rows128(b_, 24), _rows128(gp_, 8), _rows128(gpo_, 8), _rows128(gq_, 8),
                                _rows128(gkv_, 8), _rows128(cw_, 8)], axis=0)

    sw = pack(b_ada, g_pre, g_post, g_q, g_kv, conv_w)
    sm = pack(m_b_ada, m_g_pre, m_g_post, m_g_q, m_g_kv, m_conv_w)
    sv = pack(v_b_ada, v_g_pre, v_g_post, v_g_q, v_g_kv, v_conv_w)
    sg = pack(g_bada, g_gpre, g_gpost, g_gq, g_gkv, g_conv)
    small_out = (sg, *adamw(sw, sm, sv, sg, "adamw_small", tok3))

    _, r3 = swap_wait(st3, small_out[1], all4, axes3, none4, "rs_ici2_wait")

    (g_wco, g_wmo, g_wout), (l_wco, l_wmo, l_wout) = scatter_wait(sc1, small_out[2], "scatter_out_grads_wait")
    (g_wuq_t, g_wukv_t), (l_wuq, l_wukv) = scatter_wait(sc2, small_out[3], "scatter_mla_grads_wait")

    res["w_in"] = [o_.T[None] for o_ in adamw_win(w_in[0].T, m_w_in[0].T, v_w_in[0].T,
                                                  keep2[0], r3[0], keep2[1], r3[1])]
    me1 = me.reshape(1).astype(jnp.int32)
    res["w_uq"] = [o_.T[None] for o_ in adamw_scattered(
        w_uq[0].T, m_w_uq[0].T, v_w_uq[0].T, g_wuq_t, l_wuq, me1, 64, "adamw_w_uq")]
    res["w_ukv"] = [o_[None] for o_ in adamw_scattered(
        w_ukv[0], m_w_ukv[0], v_w_ukv[0], g_wukv_t, l_wukv, me1, KVL, "adamw_w_ukv", transpose=True)]
    for nm, wv, mv, vv, gg, ll in (("w_conv_out", w_conv_out, m_w_conv_out, v_w_conv_out, g_wco, l_wco),
                                   ("w_mla_out", w_mla_out, m_w_mla_out, v_w_mla_out, g_wmo, l_wmo),
                                   ("w_out", w_out, m_w_out, v_w_out, g_wout, l_wout)):
        res[nm] = [o_[None] for o_ in adamw_scattered(wv[0], mv[0], vv[0], gg, ll, me1, 128, "adamw_square")]

    def unpack(a):
        return {"b_ada": a[0:24].reshape(1, 3 * D), "g_pre": a[24:32].reshape(1, D),
                "g_post": a[32:40].reshape(1, D), "g_q": a[40:43].reshape(1, QL),
                "g_kv": a[48:50].reshape(1, KVL), "conv_w": a[56:59].reshape(-1)[:3 * 128].reshape(1, 3, 128)}

    for nm in ("b_ada", "g_pre", "g_post", "g_q", "g_kv", "conv_w"):
        res[nm] = [unpack(a)[nm] for a in small_out]

    order = ["w_ada", "b_ada", "g_pre", "w_in", "conv_w", "w_conv_out", "g_q", "w_uq", "g_kv", "w_ukv",
             "w_mla_out", "w_out", "g_post"]
    out = [loss, grad_x2.reshape(nb, seq, D)]
    for k_ in range(4):
        out += [res[nm][k_] for nm in order]
    return tuple(out)
```

```python
import numpy as np
import jax
import jax.numpy as jnp
from jax import lax
from jax.experimental import pallas as pl
from jax.experimental.pallas import tpu as pltpu

F32 = jnp.float32
BF16 = jnp.bfloat16
MESH = pl.DeviceIdType.MESH

D = 1024
H = 8
QL = 384
KVL = 256
ROPE = 64
HALF = ROPE // 2
DQK = 256
DV = 128
NSEG = 8
NP = NSEG * D
EPS = 1e-6
ROPE_THETA = 10000.0
SM_SCALE = (128 + ROPE) ** -0.5
LOG2E = 1.4426950408889634
LN2 = 0.6931471805599453
FLASH_TQ = 512

SEG_BZ, SEG_GA, SEG_GB, SEG_LAT, SEG_V = 0, 1, 2, 3, 4

ADAM_LR = 0.001
ADAM_B1 = 0.9
ADAM_B2 = 0.999
ADAM_EPS = 1e-08
ADAM_WD = 0.01
ADAM_STEP = 10

VMEM_LIMIT = 56 * 1024 * 1024


def _params(sem=None, vmem=VMEM_LIMIT):
    kw = dict(vmem_limit_bytes=vmem)
    if sem is not None:
        kw["dimension_semantics"] = sem
    return pltpu.CompilerParams(**kw)


def _sig(v):
    return 0.5 * jnp.tanh(0.5 * v) + 0.5


def _dot(a, b):
    return jnp.dot(a, b, preferred_element_type=F32)


def _dot_nt(a, b):
    return lax.dot_general(a, b, (((1,), (1,)), ((), ())), preferred_element_type=F32)


def _dot_tn(a, b):
    return lax.dot_general(a, b, (((0,), (0,)), ((), ())), preferred_element_type=F32)


_AXIS_POS = {"x": 0, "y": 1, "c": 2}


def _coords():
    return lax.axis_index("x"), lax.axis_index("y"), lax.axis_index("c")


def _partner(axis):
    p = list(_coords())
    p[_AXIS_POS[axis]] = 1 - p[_AXIS_POS[axis]]
    return tuple(p)


def small_allgather(v, name):
    rows = v.shape[0]

    def body(v_ref, out_ref, send_sems, recv_sems):
        x, y, c = _coords()
        me = 4 * x + 2 * y + c
        out_ref[me] = v_ref[...]
        copies = []
        for k in range(1, 8):
            peer = (1 - x if k & 4 else x, 1 - y if k & 2 else y, 1 - c if k & 1 else c)
            cp = pltpu.make_async_remote_copy(
                src_ref=v_ref, dst_ref=out_ref.at[me],
                send_sem=send_sems.at[k - 1], recv_sem=recv_sems.at[k - 1],
                device_id=peer, device_id_type=MESH)
            cp.start()
            copies.append(cp)
        for cp in copies:
            cp.wait()

    return pl.pallas_call(
        body, name=name,
        out_shape=jax.ShapeDtypeStruct((8, rows, 128), F32),
        in_specs=[pl.BlockSpec(memory_space=pltpu.VMEM)],
        out_specs=pl.BlockSpec(memory_space=pltpu.VMEM),
        scratch_shapes=[pltpu.SemaphoreType.DMA((7,)), pltpu.SemaphoreType.DMA((7,))],
    )(v)


def _own_block_placed(s):
    x, y, c = _coords()
    return lax.dynamic_update_slice(lax.empty((2, 2, 2) + s.shape, s.dtype), s[None, None, None],
                                    (x, y, c) + (0,) * s.ndim)


def allgather_big(arrs, plan, name):
    n = len(arrs)
    m = len(plan)
    nst = len(plan[0][3])

    def body(*refs):
        ins, outs = refs[n:2 * n], refs[2 * n:3 * n]
        send_sems, recv_sems = refs[3 * n:]
        x, y, c = _coords()
        co = {"x": x, "y": y, "c": c}

        def window(ref, lead, rows, cols):
            win = tuple(slice(None) if w is None else pl.ds(w[0], w[1]) for w in (rows, cols))
            return ref.at[tuple(lead) + win]

        def held(e, free):
            i, rows, cols, _ = plan[e]
            lead = [slice(None) if ax in free else co[ax] for ax in ("x", "y", "c")]
            return window(outs[i], lead, rows, cols)

        def rcopy(e, stage, src, dst, axis):
            return pltpu.make_async_remote_copy(
                src_ref=src, dst_ref=dst,
                send_sem=send_sems.at[e, stage], recv_sem=recv_sems.at[e, stage],
                device_id=_partner(axis), device_id_type=MESH)

        stages = [[] for _ in range(nst)]
        for e, (i, rows, cols, order) in enumerate(plan):
            cp = rcopy(e, 0, window(ins[i], [], rows, cols), held(e, ()), order[0])
            cp.start()
            stages[0].append(cp)
        for s in range(1, nst):
            for e, (i, rows, cols, order) in enumerate(plan):
                stages[s - 1][e].wait_recv()
                blk = held(e, order[:s])
                cp = rcopy(e, s, blk, blk, order[s])
                cp.start()
                stages[s].append(cp)
        for e in range(m):
            stages[nst - 1][e].wait_recv()
        for e in range(m):
            for s in range(nst):
                stages[s][e].wait_send()

    any_spec = pl.BlockSpec(memory_space=pl.ANY)
    lands = [_own_block_placed(a) for a in arrs]
    return pl.pallas_call(
        body, name=name,
        out_shape=[jax.ShapeDtypeStruct(l.shape, l.dtype) for l in lands],
        in_specs=[any_spec] * (2 * n),
        out_specs=[any_spec] * n,
        input_output_aliases={i: i for i in range(n)},
        scratch_shapes=[pltpu.SemaphoreType.DMA((m, nst)), pltpu.SemaphoreType.DMA((m, nst))],
    )(*lands, *arrs)


_HBM =pl.BlockSpec(memory_space=pltpu.HBM)
_SEM = pl.BlockSpec(memory_space=pltpu.SEMAPHORE)


def _swap_copies(srcs, lands, send_sems, recv_sems, axes, picks):
    x, y, c = _coords()
    co = {"x": x, "y": y, "c": c}
    return [pltpu.make_async_remote_copy(
        src_ref=srcs[a] if picks[a] is None else picks[a](srcs[a], co), dst_ref=lands[a],
        send_sem=send_sems.at[a], recv_sem=recv_sems.at[a],
        device_id=_partner(axes[a]), device_id_type=MESH) for a in range(len(srcs))]


def swap_start(arrs, which, axes, picks, out_shapes, name):
    ns, n = len(arrs), len(which)

    def body(*refs):
        srcs, lands = refs[:ns], refs[ns:ns + n]
        send_sems, recv_sems = refs[ns + n:ns + n + 2]
        token = refs[-1]
        for cp in _swap_copies([srcs[i] for i in which], lands, send_sems, recv_sems, axes, picks):
            cp.start()
        token[...] = jnp.zeros_like(token)

    lands = [lax.empty(s, arrs[i].dtype) for s, i in zip(out_shapes, which)]
    ops = [pltpu.with_memory_space_constraint(a, pltpu.HBM) for a in list(arrs) + lands]
    out = pl.pallas_call(
        body, name=name,
        out_shape=[pltpu.SemaphoreType.DMA((n,)), pltpu.SemaphoreType.DMA((n,))]
        + [pltpu.HBM(o.shape, o.dtype) for o in ops] + [jax.ShapeDtypeStruct((8, 128), F32)],
        in_specs=[_HBM] * (ns + n),
        out_specs=[_SEM, _SEM] + [_HBM] * (ns + n) + [pl.BlockSpec(memory_space=pltpu.VMEM)],
        input_output_aliases={i: 2 + i for i in range(ns + n)},
        compiler_params=pltpu.CompilerParams(has_side_effects=pltpu.SideEffectType.DATAFLOW_SIDE_EFFECTING),
    )(*ops)
    return out[:-1], out[-1]


def swap_wait(state, after, which, axes, picks, name):
    n = len(which)
    ns = len(state) - 2 - n

    def body(*refs):
        srcs, lands = refs[:ns], refs[ns:ns + n]
        send_sems, recv_sems = refs[ns + n:ns + n + 2]
        for cp in _swap_copies([srcs[i] for i in which], lands, send_sems, recv_sems, axes, picks):
            cp.wait_send()
            cp.wait_recv()

    thru = list(state[2:])
    after = list(after) if isinstance(after, (list, tuple)) else [after]
    out = pl.pallas_call(
        body, name=name,
        out_shape=[pltpu.HBM(o.shape, o.dtype) for o in thru],
        in_specs=[_HBM] * (ns + n) + [_SEM, _SEM] + [pl.BlockSpec(memory_space=pl.ANY)] * len(after),
        out_specs=[_HBM] * (ns + n),
        input_output_aliases={i: i for i in range(ns + n)},
        compiler_params=pltpu.CompilerParams(has_side_effects=pltpu.SideEffectType.DATAFLOW_SIDE_EFFECTING),
    )(*thru, state[0], state[1], *after)
    return out[:ns], out[ns:]


def _gather_copies(shards, lands, send_sems, recv_sems):
    x, y, c = _coords()
    copies = []
    for a in range(len(shards)):
        for k in range(1, 8):
            peer = (1 - x if k & 4 else x, 1 - y if k & 2 else y, 1 - c if k & 1 else c)
            copies.append(pltpu.make_async_remote_copy(
                src_ref=shards[a], dst_ref=lands[a].at[x, y, c],
                send_sem=send_sems.at[7 * a + k - 1], recv_sem=recv_sems.at[7 * a + k - 1],
                device_id=peer, device_id_type=MESH))
    return copies


def gather_start(shards, name):
    n = len(shards)
    x, y, c = _coords()

    def body(*refs):
        srcs, lands = refs[:n], refs[n:2 * n]
        send_sems, recv_sems = refs[2 * n:2 * n + 2]
        token = refs[-1]
        for cp in _gather_copies(srcs, lands, send_sems, recv_sems):
            cp.start()
        token[...] = jnp.zeros_like(token)

    lands = [_own_block_placed(s) for s in shards]
    ops = [pltpu.with_memory_space_constraint(a, pltpu.HBM) for a in list(shards) + lands]
    out = pl.pallas_call(
        body, name=name,
        out_shape=[pltpu.SemaphoreType.DMA((7 * n,)), pltpu.SemaphoreType.DMA((7 * n,))]
        + [pltpu.HBM(o.shape, o.dtype) for o in ops] + [jax.ShapeDtypeStruct((8, 128), F32)],
        in_specs=[_HBM] * (2 * n),
        out_specs=[_SEM, _SEM] + [_HBM] * (2 * n) + [pl.BlockSpec(memory_space=pltpu.VMEM)],
        input_output_aliases={i: 2 + i for i in range(2 * n)},
        compiler_params=pltpu.CompilerParams(has_side_effects=pltpu.SideEffectType.DATAFLOW_SIDE_EFFECTING),
    )(*ops)
    return out[:-1], out[-1]


def gather_wait(state, after, name):
    n = (len(state) - 2) // 2

    def body(*refs):
        srcs, lands = refs[:n], refs[n:2 * n]
        send_sems, recv_sems = refs[2 * n:2 * n + 2]
        for cp in _gather_copies(srcs, lands, send_sems, recv_sems):
            cp.wait_send()
            cp.wait_recv()

    thru = list(state[2:])
    out = pl.pallas_call(
        body, name=name,
        out_shape=[pltpu.HBM(o.shape, o.dtype) for o in thru],
        in_specs=[_HBM] * (2 * n) + [_SEM, _SEM, pl.BlockSpec(memory_space=pl.ANY)],
        out_specs=[_HBM] * (2 * n),
        input_output_aliases={i: i for i in range(2 * n)},
        compiler_params=pltpu.CompilerParams(has_side_effects=pltpu.SideEffectType.DATAFLOW_SIDE_EFFECTING),
    )(*thru, state[0], state[1], after)
    return out[n:]


def _scatter_copies(grads, lands, send_sems, recv_sems):
    x, y, c = _coords()
    me = 4 * x + 2 * y + c
    copies = []
    for a in range(len(grads)):
        r = grads[a].shape[0] // 8
        for k in range(1, 8):
            px, py, pc = (1 - x if k & 4 else x, 1 - y if k & 2 else y, 1 - c if k & 1 else c)
            rows = pl.ds(pl.multiple_of((4 * px + 2 * py + pc) * r, r), r)
            copies.append(pltpu.make_async_remote_copy(
                src_ref=grads[a].at[rows], dst_ref=lands[a].at[me],
                send_sem=send_sems.at[7 * a + k - 1], recv_sem=recv_sems.at[7 * a + k - 1],
                device_id=(px, py, pc), device_id_type=MESH))
    return copies


def scatter_start(grads, name):
    n = len(grads)

    def body(*refs):
        srcs, lands = refs[:n], refs[n:2 * n]
        send_sems, recv_sems = refs[2 * n:2 * n + 2]
        token = refs[-1]
        for cp in _scatter_copies(srcs, lands, send_sems, recv_sems):
            cp.start()
        token[...] = jnp.zeros_like(token)

    lands = [jnp.zeros((8, g.shape[0] // 8, g.shape[1]), g.dtype) for g in grads]
    ops = [pltpu.with_memory_space_constraint(a, pltpu.HBM) for a in list(grads) + lands]
    out = pl.pallas_call(
        body, name=name,
        out_shape=[pltpu.SemaphoreType.DMA((7 * n,)), pltpu.SemaphoreType.DMA((7 * n,))]
        + [pltpu.HBM(o.shape, o.dtype) for o in ops] + [jax.ShapeDtypeStruct((8, 128), F32)],
        in_specs=[_HBM] * (2 * n),
        out_specs=[_SEM, _SEM] + [_HBM] * (2 * n) + [pl.BlockSpec(memory_space=pltpu.VMEM)],
        input_output_aliases={i: 2 + i for i in range(2 * n)},
        compiler_params=pltpu.CompilerParams(has_side_effects=pltpu.SideEffectType.DATAFLOW_SIDE_EFFECTING),
    )(*ops)
    return out[:-1], out[-1]


def scatter_wait(state, after, name):
    n = (len(state) - 2) // 2

    def body(*refs):
        srcs, lands = refs[:n], refs[n:2 * n]
        send_sems, recv_sems = refs[2 * n:2 * n + 2]
        for cp in _scatter_copies(srcs, lands, send_sems, recv_sems):
            cp.wait_send()
            cp.wait_recv()

    thru = list(state[2:])
    after = list(after) if isinstance(after, (list, tuple)) else [after]
    out = pl.pallas_call(
        body, name=name,
        out_shape=[pltpu.HBM(o.shape, o.dtype) for o in thru],
        in_specs=[_HBM] * (2 * n) + [_SEM, _SEM] + [pl.BlockSpec(memory_space=pl.ANY)] * len(after),
        out_specs=[_HBM] * (2 * n),
        input_output_aliases={i: i for i in range(2 * n)},
        compiler_params=pltpu.CompilerParams(has_side_effects=pltpu.SideEffectType.DATAFLOW_SIDE_EFFECTING),
    )(*thru, state[0], state[1], *after)
    return out[:n], out[n:]


def rs_win_add_first(g, r, sel, next_dim, col, name):
    rows, cols = r.shape[2:]

    def body(sel_ref, gk_ref, rk_ref, gs_ref, rs_ref, keep_ref, send_ref):
        keep_ref[...] = gk_ref[...] + rk_ref[...]
        send_ref[...] = (gs_ref[...] + rs_ref[...]).astype(BF16)

    def g_map(flip):
        def f(j, s):
            nxt = 1 - s[next_dim] if flip else s[next_dim]
            return (nxt, j, s[2], 0, col) if next_dim == 0 else (j, nxt, s[2], 0, col)
        return f

    def r_map(flip):
        def f(j, s):
            nxt = 1 - s[next_dim] if flip else s[next_dim]
            return (nxt, j, 0, 0) if next_dim == 0 else (j, nxt, 0, 0)
        return f

    gblk = (None, None, None, rows, cols)
    rblk = (None, None, rows, cols)
    oblk = (None, rows, cols)
    return pl.pallas_call(
        body, name=name,
        grid_spec=pltpu.PrefetchScalarGridSpec(
            num_scalar_prefetch=1, grid=(2,),
            in_specs=[pl.BlockSpec(gblk, g_map(False)), pl.BlockSpec(rblk, r_map(False)),
                      pl.BlockSpec(gblk, g_map(True)), pl.BlockSpec(rblk, r_map(True))],
            out_specs=[pl.BlockSpec(oblk, lambda j, s: (j, 0, 0)),
                       pl.BlockSpec(oblk, lambda j, s: (j, 0, 0))]),
        out_shape=[jax.ShapeDtypeStruct((2, rows, cols), F32),
                   jax.ShapeDtypeStruct((2, rows, cols), BF16)],
        compiler_params=_params(),
    )(sel, g, r, g, r)


def rs_add_second(k, r, sel, name):
    _, rows, cols = k.shape
    tr = rows // 2 if rows % 32 == 0 else rows
    nt = rows // tr

    def body(sel_ref, kk_ref, rk_ref, ks_ref, rs_ref, keep_ref, send_ref):
        keep_ref[...] = kk_ref[...] + rk_ref[...].astype(F32)
        send_ref[...] = (ks_ref[...] + rs_ref[...].astype(F32)).astype(BF16)

    blk = (None, tr, cols)
    oblk = (tr, cols)
    return pl.pallas_call(
        body, name=name,
        grid_spec=pltpu.PrefetchScalarGridSpec(
            num_scalar_prefetch=1, grid=(nt,),
            in_specs=[
                pl.BlockSpec(blk, lambda i, s: (s[0], i, 0)),
                pl.BlockSpec(blk, lambda i, s: (s[0], i, 0)),
                pl.BlockSpec(blk, lambda i, s: (1 - s[0], i, 0)),
                pl.BlockSpec(blk, lambda i, s: (1 - s[0], i, 0)),
            ],
            out_specs=[pl.BlockSpec(oblk, lambda i, s: (i, 0)),
                       pl.BlockSpec(oblk, lambda i, s: (i, 0))]),
        out_shape=[jax.ShapeDtypeStruct((rows, cols), F32),
                   jax.ShapeDtypeStruct((rows, cols), BF16)],
        compiler_params=_params(),
    )(sel, k, r, k, r)


SEG_ROWS = (4800, 5824, 6848, 4096, 0, 1024, 2048, 3072)
LAT_ROWS = QL + KVL + ROPE
N_IN = 7872


def _seg_row(j):
    return pl.multiple_of(jnp.where(j < 3, 4800 + 1024 * j, jnp.where(j == 3, 4096, (j - 4) * 1024)), 8)


def proj_matmul(h, wt_bits, token):
    t = h.shape[0]
    tm = min(2048, t)

    def body(h_ref, w_hbm, tok_ref, o_ref, wt_ref, buf, sems):
        j = pl.program_id(0)
        slot = j % 2

        def fetch(seg, into):
            return pltpu.make_async_copy(w_hbm.at[pl.ds(_seg_row(seg), D)], buf.at[into], sems.at[into])

        @pl.when(pl.program_id(1) == 0)
        def _():
            @pl.when(j == 0)
            def _():
                fetch(j, slot).start()

            fetch(j, slot).wait()

            @pl.when(j + 1 < NSEG)
            def _():
                fetch(j + 1, 1 - slot).start()

            bits = pltpu.bitcast(buf[slot], jnp.uint32)
            row = lax.broadcasted_iota(jnp.int32, (D, D // 2), 0)
            live = jnp.logical_or(j != SEG_LAT, row < LAT_ROWS)
            lo = pltpu.bitcast(bits << 16, F32)
            hi = pltpu.bitcast(bits & jnp.uint32(0xFFFF0000), F32)
            wt_ref[:, :D // 2] = jnp.where(live, lo, 0.0).astype(BF16)
            wt_ref[:, D // 2:] = jnp.where(live, hi, 0.0).astype(BF16)

        o_ref[...] = _dot_nt(h_ref[...], wt_ref[...]).astype(BF16)

    return pl.pallas_call(
        body, name="proj_matmul", grid=(NSEG, t // tm),
        in_specs=[pl.BlockSpec((tm, D), lambda j, i: (i, 0)),
                  pl.BlockSpec(memory_space=pl.ANY),
                  pl.BlockSpec((8, 128), lambda j, i: (0, 0))],
        out_specs=[pl.BlockSpec((None, tm, D), lambda j, i: (j, i, 0)),
                   pl.BlockSpec((D, D), lambda j, i: (j, 0))],
        out_shape=[jax.ShapeDtypeStruct((NSEG, t, D), BF16), jax.ShapeDtypeStruct((NP, D), BF16)],
        scratch_shapes=[pltpu.VMEM((2, D, D // 2), F32), pltpu.SemaphoreType.DMA((2,))],
        compiler_params=_params(("arbitrary", "arbitrary")),
    )(h, wt_bits, token)


def dh_matmul(dproj, wt, token, seq, b):
    tm = min(1024, seq)
    nblk = seq // tm

    per = 2

    def body(b_ref, d_ref, w_ref, tok_ref, o_ref, acc_ref):
        k = pl.program_id(1)

        @pl.when(k == 0)
        def _():
            acc_ref[...] = jnp.zeros_like(acc_ref)

        part = _dot(d_ref[0], w_ref[0:D, :])
        for j in range(1, per):
            part = part + _dot(d_ref[j], w_ref[j * D:(j + 1) * D, :])
        acc_ref[...] += part

        @pl.when(k == NSEG // per - 1)
        def _():
            o_ref[...] = acc_ref[...]

    return pl.pallas_call(
        body, name="dh_matmul",
        grid_spec=pltpu.PrefetchScalarGridSpec(
            num_scalar_prefetch=1, grid=(nblk, NSEG // per),
            in_specs=[pl.BlockSpec((per, tm, D), lambda i, k, s: (k, s[0] * nblk + i, 0)),
                      pl.BlockSpec((per * D, D), lambda i, k, s: (k, 0)),
                      pl.BlockSpec((8, 128), lambda i, k, s: (0, 0))],
            out_specs=pl.BlockSpec((tm, D), lambda i, k, s: (i, 0)),
            scratch_shapes=[pltpu.VMEM((tm, D), F32)]),
        out_shape=jax.ShapeDtypeStruct((seq, D), F32),
        compiler_params=_params(("parallel", "arbitrary")),
    )(jnp.full((1,), b, jnp.int32), dproj, wt, token)


def win_grad_matmul(h, dproj, token):
    t = h.shape[0]

    def body(h_ref, d_ref, tok_ref, o_hbm, acc_ref, sems):
        j = pl.program_id(0)

        def out_copy(jj, action):
            slot = lax.rem(jj, 2)

            @pl.when(jj != SEG_LAT)
            def _():
                action(pltpu.make_async_copy(acc_ref.at[slot], o_hbm.at[pl.ds(_seg_row(jj), D)],
                                             sems.at[slot]))

            @pl.when(jj == SEG_LAT)
            def _():
                action(pltpu.make_async_copy(acc_ref.at[slot, pl.ds(0, LAT_ROWS)],
                                             o_hbm.at[pl.ds(SEG_ROWS[SEG_LAT], LAT_ROWS)], sems.at[slot]))

        acc_ref[lax.rem(j, 2)] = _dot_tn(d_ref[...], h_ref[...])
        out_copy(j, lambda cp: cp.start())

        @pl.when(j > 0)
        def _():
            out_copy(j - 1, lambda cp: cp.wait())

        @pl.when(j == NSEG - 1)
        def _():
            out_copy(j, lambda cp: cp.wait())

    return pl.pallas_call(
        body, name="win_grad_matmul", grid=(NSEG,),
        in_specs=[pl.BlockSpec((t, D), lambda j: (0, 0)),
                  pl.BlockSpec((None, t, D), lambda j: (j, 0, 0)),
                  pl.BlockSpec((8, 128), lambda j: (0, 0))],
        out_specs=pl.BlockSpec(memory_space=pl.ANY),
        out_shape=jax.ShapeDtypeStruct((N_IN, D), F32),
        scratch_shapes=[pltpu.VMEM((2, D, D), F32), pltpu.SemaphoreType.DMA((2,))],
        compiler_params=_params(("arbitrary",)),
    )(h, dproj, token)


def grad_matmul(a, b, name):
    t, m = a.shape
    n = b.shape[1]
    tk = min(1024, t)
    nk = t // tk

    def body(a_ref, b_ref, o_ref, acc_ref):
        k = pl.program_id(0)

        @pl.when(k == 0)
        def _():
            acc_ref[...] = jnp.zeros_like(acc_ref)

        acc_ref[...] += _dot_tn(a_ref[...], b_ref[...])

        @pl.when(k == nk - 1)
        def _():
            o_ref[...] = acc_ref[...].astype(BF16)

    return pl.pallas_call(
        body, name=name, grid=(nk,),
        in_specs=[pl.BlockSpec((tk, m), lambda k: (k, 0)),
                  pl.BlockSpec((tk, n), lambda k: (k, 0))],
        out_specs=pl.BlockSpec((m, n), lambda k: (0, 0)),
        out_shape=jax.ShapeDtypeStruct((m, n), BF16),
        scratch_shapes=[pltpu.VMEM((m, n), F32)],
        compiler_params=_params(("arbitrary",)),
    )(a, b)


def ada_gather(c8, taps8, w_ada, b_cols):
    cols = w_ada.shape[1]

    def body(c_ref, t_ref, w_ref, b_ref, call_ref, tall_ref, mod_ref, part_ref, send_sems, recv_sems):
        x, y, c = _coords()
        me = 4 * x + 2 * y + c
        peers = [(1 - x if k & 4 else x, 1 - y if k & 2 else y, 1 - c if k & 1 else c) for k in range(1, 8)]

        def rcopy(n, src, dst, peer):
            return pltpu.make_async_remote_copy(src_ref=src, dst_ref=dst, send_sem=send_sems.at[n],
                                                recv_sem=recv_sems.at[n], device_id=peer, device_id_type=MESH)

        call_ref[me] = c_ref[...]
        tall_ref[me] = t_ref[...]
        first = []
        for k, peer in enumerate(peers):
            first += [rcopy(k, c_ref, call_ref.at[me], peer), rcopy(7 + k, t_ref, tall_ref.at[me], peer)]
        for cp in first:
            cp.start()
        for cp in first:
            cp.wait()
        rows = call_ref[...].reshape(64, D).astype(BF16)
        part_ref[...] = _dot(rows, w_ref[...].astype(BF16)) + b_ref[...]
        mod_ref[me] = part_ref[pl.ds(pl.multiple_of(8 * me, 8), 8), :]
        second = []
        for k, (px, py, pc) in enumerate(peers):
            theirs = part_ref.at[pl.ds(pl.multiple_of(8 * (4 * px + 2 * py + pc), 8), 8)]
            second.append(rcopy(14 + k, theirs, mod_ref.at[me], (px, py, pc)))
        for cp in second:
            cp.start()
        for cp in second:
            cp.wait()

    vm = pl.BlockSpec(memory_space=pltpu.VMEM)
    return pl.pallas_call(
        body, name="ada_gather",
        out_shape=[jax.ShapeDtypeStruct((8, 8, D), F32), jax.ShapeDtypeStruct((8, 8, 128), F32),
                   jax.ShapeDtypeStruct((8, 8, cols), F32)],
        in_specs=[vm] * 4, out_specs=[vm] * 3,
        scratch_shapes=[pltpu.VMEM((64, cols), F32), pltpu.SemaphoreType.DMA((21,)),
                        pltpu.SemaphoreType.DMA((21,))],
        compiler_params=_params(),
    )(c8, taps8, w_ada, b_cols)


def ada_bwd(c_all, dmod_cols):
    def body(c_ref, d_ref, o_ref):
        o_ref[...] = _dot_tn(c_ref[...].astype(BF16), d_ref[...].astype(BF16))

    return pl.pallas_call(
        body, name="ada_bwd",
        out_shape=jax.ShapeDtypeStruct((c_all.shape[1], dmod_cols.shape[1]), F32),
        compiler_params=_params(),
    )(c_all, dmod_cols)


def slot_sum(g):
    def body(g_ref, o_ref):
        acc = g_ref[0]
        for s in range(1, 8):
            acc = acc + g_ref[s]
        o_ref[...] = acc

    return pl.pallas_call(
        body, name="slot_sum",
        out_shape=jax.ShapeDtypeStruct(g.shape[1:], F32),
    )(g)


def prenorm_fwd(x2, scale, shift, g_pre, seq):
    t = x2.shape[0]
    tm = min(512, seq)
    tpb = seq // tm

    def body(x_ref, sc_ref, sh_ref, g_ref, h_ref):
        xv = x_ref[...]
        r = lax.rsqrt(jnp.mean(xv * xv, axis=-1, keepdims=True) + EPS)
        hv = (xv * r * g_ref[...]) * (1.0 + sc_ref[...]) + sh_ref[...]
        h_ref[...] = hv.astype(BF16)

    per_batch = pl.BlockSpec((None, 1, D), lambda i: (i // tpb, 0, 0))
    return pl.pallas_call(
        body, name="prenorm_fwd", grid=(t // tm,),
        in_specs=[pl.BlockSpec((tm, D), lambda i: (i, 0)), per_batch, per_batch,
                  pl.BlockSpec((1, D), lambda i: (0, 0))],
        out_specs=pl.BlockSpec((tm, D), lambda i: (i, 0)),
        out_shape=jax.ShapeDtypeStruct((t, D), BF16),
        compiler_params=_params(("parallel",)),
    )(x2, scale, shift, g_pre)


def prenorm_bwd(dh, x2, dout, scale, g_pre, seq, token, b, gx_prev):
    t = x2.shape[0]
    tm = min(512, seq)
    tpb = seq // tm
    if gx_prev is None:
        gx_prev = lax.empty((t, D), F32)

    def body(b_ref, dh_ref, x_ref, do_ref, sc_ref, g_ref, tok_ref, gxp_ref, gx_ref, dsh_ref, dsc_ref, dg_ref):
        i = pl.program_id(0)
        xv = x_ref[...]
        dhv = dh_ref[...]
        g = g_ref[...]
        r = lax.rsqrt(jnp.mean(xv * xv, axis=-1, keepdims=True) + EPS)
        nrm = xv * r
        dxn = dhv * (1.0 + sc_ref[...])
        dn = dxn * g
        dx = r * (dn - nrm * jnp.mean(dn * nrm, axis=-1, keepdims=True))
        gx_ref[...] = dx + do_ref[...]

        @pl.when(i == 0)
        def _():
            dsh_ref[...] = jnp.zeros_like(dsh_ref)
            dsc_ref[...] = jnp.zeros_like(dsc_ref)
            dg_ref[...] = jnp.zeros_like(dg_ref)

        dsh_ref[...] += jnp.sum(dhv, axis=0, keepdims=True)
        dsc_ref[...] += jnp.sum(dhv * (nrm * g), axis=0, keepdims=True)
        dg_ref[...] += jnp.sum(dxn * nrm, axis=0, keepdims=True)

    row = pl.BlockSpec((tm, D), lambda i, s: (i, 0))
    grow = pl.BlockSpec((tm, D), lambda i, s: (s[0] * tpb + i, 0))
    per_batch = pl.BlockSpec((None, 1, D), lambda i, s: (s[0], 0, 0))
    vec = pl.BlockSpec((1, D), lambda i, s: (0, 0))
    return pl.pallas_call(
        body, name="prenorm_bwd",
        grid_spec=pltpu.PrefetchScalarGridSpec(
            num_scalar_prefetch=1, grid=(tpb,),
            in_specs=[row, grow, grow, per_batch, vec, pl.BlockSpec((8, 128), lambda i, s: (0, 0)),
                      pl.BlockSpec(memory_space=pl.ANY)],
            out_specs=[grow, vec, vec, vec]),
        out_shape=[jax.ShapeDtypeStruct((t, D), F32), jax.ShapeDtypeStruct((1, D), F32),
                   jax.ShapeDtypeStruct((1, D), F32), jax.ShapeDtypeStruct((1, D), F32)],
        input_output_aliases={7: 0},
        compiler_params=_params(("arbitrary",)),
    )(jnp.full((1,), b, jnp.int32), dh, x2, dout, scale, g_pre, token, gx_prev)


CONV_TC = 128


def _shift_down(u, k, rows):
    idx = lax.broadcasted_iota(jnp.int32, u.shape, 0)
    return jnp.where(idx >= k, pltpu.roll(u, k, 0), 0.0)


def _shift_up(u, k, rows):
    idx = lax.broadcasted_iota(jnp.int32, u.shape, 0)
    return jnp.where(idx < rows - k, pltpu.roll(u, rows - k, 0), 0.0)


def conv_fwd(proj, conv_w, seq):
    t = proj.shape[1]
    nb = t // seq

    def body(p_ref, w_ref, y_ref):
        av = p_ref[0].astype(F32)
        ab = p_ref[1].astype(F32)
        ac = p_ref[2].astype(F32)
        az = p_ref[3].astype(F32)
        w = w_ref[...]
        u = ac * av
        y1 = _shift_down(u, 2, seq) * w[0:1] + _shift_down(u, 1, seq) * w[1:2] + u * w[2:3]
        y_ref[...] = (ab * y1 * (az * _sig(az))).astype(BF16)

    return pl.pallas_call(
        body, name="conv_fwd", grid=(nb, D // CONV_TC),
        in_specs=[pl.BlockSpec((4, seq, CONV_TC), lambda b, ci: (1, b, ci)),
                  pl.BlockSpec((8, CONV_TC), lambda b, ci: (0, ci))],
        out_specs=pl.BlockSpec((seq, CONV_TC), lambda b, ci: (b, ci)),
        out_shape=jax.ShapeDtypeStruct((t, D), BF16),
        compiler_params=_params(("parallel", "parallel")),
    )(proj, conv_w)


def conv_bwd(dproj, proj, dy, conv_w, seq):
    t = proj.shape[1]
    nb = t // seq

    def body(dp_in_ref, p_ref, dy_ref, w_ref, dp_ref, dw_ref):
        b = pl.program_id(1)
        av = p_ref[0].astype(F32)
        ab = p_ref[1].astype(F32)
        ac = p_ref[2].astype(F32)
        az = p_ref[3].astype(F32)
        dyv = dy_ref[...].astype(F32)
        w = w_ref[...]
        u = ac * av
        u1 = _shift_down(u, 1, seq)
        u2 = _shift_down(u, 2, seq)
        y1 = u2 * w[0:1] + u1 * w[1:2] + u * w[2:3]
        sz = _sig(az)
        silu = az * sz
        dy1 = dyv * ab * silu
        du = dy1 * w[2:3] + _shift_up(dy1, 1, seq) * w[1:2] + _shift_up(dy1, 2, seq) * w[0:1]
        dp_ref[0] = (du * ac).astype(BF16)
        dp_ref[1] = (dyv * y1 * silu).astype(BF16)
        dp_ref[2] = (du * av).astype(BF16)
        dp_ref[3] = (dyv * ab * y1 * (sz * (1.0 + az * (1.0 - sz)))).astype(BF16)

        @pl.when(b == 0)
        def _():
            dw_ref[...] = jnp.zeros_like(dw_ref)

        dw_ref[0:1, :] += jnp.sum(dy1 * u2, axis=0, keepdims=True)
        dw_ref[1:2, :] += jnp.sum(dy1 * u1, axis=0, keepdims=True)
        dw_ref[2:3, :] += jnp.sum(dy1 * u, axis=0, keepdims=True)

    return pl.pallas_call(
        body, name="conv_bwd", grid=(D // CONV_TC, nb),
        in_specs=[pl.BlockSpec(memory_space=pl.ANY),
                  pl.BlockSpec((4, seq, CONV_TC), lambda ci, b: (1, b, ci)),
                  pl.BlockSpec((seq, CONV_TC), lambda ci, b: (b, ci)),
                  pl.BlockSpec((8, CONV_TC), lambda ci, b: (0, ci))],
        out_specs=[pl.BlockSpec((4, seq, CONV_TC), lambda ci, b: (1, b, ci)),
                   pl.BlockSpec((8, CONV_TC), lambda ci, b: (0, ci))],
        out_shape=[jax.ShapeDtypeStruct(dproj.shape, BF16),
                   jax.ShapeDtypeStruct((8, D), F32)],
        input_output_aliases={0: 0},
        compiler_params=_params(("parallel", "arbitrary")),
    )(dproj, proj, dy, conv_w)


def _rope_tables(pos_ref, invf_ref, ma_ref, mb_ref, sign):
    ang = pos_ref[...].astype(F32) * invf_ref[...]
    cs = jnp.cos(ang)
    sn = jnp.sin(ang) * sign
    return cs, sn * ma_ref[...], sn * mb_ref[...]


def _rotate(v, cs, sa, sb):
    return v * cs + pltpu.roll(v, 128 - HALF, 1) * sa + pltpu.roll(v, HALF, 1) * sb


MLA_TM = 512


def mla_prep_fwd(proj, pos, g_q, g_kv, wuq, wukv, tabs):
    t = proj.shape[1]
    tm = min(MLA_TM, t)

    def body(lat_ref, pos_ref, gq_ref, gkv_ref, wuq_ref, wukv_ref, invf_ref, ma_ref, mb_ref,
             q_ref, k_ref, kv_ref, qn_ref, kvn_ref):
        lat = lat_ref[...].astype(F32)
        ql = lat[:, :QL]
        kl = lat[:, QL:QL + KVL]
        kr = lat[:, QL + KVL:QL + KVL + 128]
        qn = (ql * lax.rsqrt(jnp.mean(ql * ql, axis=-1, keepdims=True) + EPS) * gq_ref[...]).astype(BF16)
        kvn = (kl * lax.rsqrt(jnp.mean(kl * kl, axis=-1, keepdims=True) + EPS) * gkv_ref[...]).astype(BF16)
        qn_ref[...] = qn
        kvn_ref[...] = kvn
        cs, sa, sb = _rope_tables(pos_ref, invf_ref, ma_ref, mb_ref, 1.0)
        q = _dot_nt(qn, wuq_ref[...]) * (SM_SCALE * LOG2E)
        kv = _dot_nt(kvn, wukv_ref[...]).astype(BF16)
        kv_ref[...] = kv
        kpe = _rotate(kr, cs, sa, sb).astype(BF16)
        for hh in range(H):
            lo, mid, hi = hh * DQK, hh * DQK + 128, (hh + 1) * DQK
            q_ref[:, lo:mid] = q[:, lo:mid].astype(BF16)
            q_ref[:, mid:hi] = _rotate(q[:, mid:hi], cs, sa, sb).astype(BF16)
            k_ref[:, lo:mid] = kv[:, lo:mid]
            k_ref[:, mid:hi] = kpe

    row = lambda w: pl.BlockSpec((tm, w), lambda i: (i, 0))
    const = lambda a: pl.BlockSpec(a.shape, lambda i: (0,) * a.ndim)
    return pl.pallas_call(
        body, name="mla_prep_fwd", grid=(t // tm,),
        in_specs=[pl.BlockSpec((None, tm, D), lambda i: (SEG_LAT, i, 0)), row(1),
                  const(g_q), const(g_kv), const(wuq), const(wukv)] + [const(a) for a in tabs],
        out_specs=[row(H * DQK), row(H * DQK), row(H * DQK), row(QL), row(KVL)],
        out_shape=[jax.ShapeDtypeStruct((t, H * DQK), BF16)] * 3
        + [jax.ShapeDtypeStruct((t, QL), BF16), jax.ShapeDtypeStruct((t, KVL), BF16)],
        compiler_params=_params(("parallel",)),
    )(proj, pos, g_q, g_kv, wuq, wukv, *tabs)


def mla_prep_bwd(dproj, proj, dq_rot, dk, dv, pos, g_q, g_kv, wuq, wukv, tabs):
    t = proj.shape[1]
    tm = min(MLA_TM, t)

    def body(dp_in_ref, lat_ref, dqr_ref, dk_ref, dv_ref, pos_ref, gq_ref, gkv_ref, wuq_ref, wukv_ref,
             invf_ref, ma_ref, mb_ref, dp_ref, dq_ref, dkv_ref, dgq_ref, dgkv_ref):
        i = pl.program_id(0)
        lat = lat_ref[...].astype(F32)
        ql = lat[:, :QL]
        kl = lat[:, QL:QL + KVL]
        rq = lax.rsqrt(jnp.mean(ql * ql, axis=-1, keepdims=True) + EPS)
        rk = lax.rsqrt(jnp.mean(kl * kl, axis=-1, keepdims=True) + EPS)
        nq = ql * rq
        nk = kl * rk
        cs, sa, sb = _rope_tables(pos_ref, invf_ref, ma_ref, mb_ref, -1.0)
        dkpe = jnp.zeros((tm, 128), F32)
        for hh in range(H):
            lo, mid, hi = hh * DQK, hh * DQK + 128, (hh + 1) * DQK
            dq_ref[:, lo:mid] = (dqr_ref[:, lo:mid] * SM_SCALE).astype(BF16)
            dq_ref[:, mid:hi] = _rotate(dqr_ref[:, mid:hi] * SM_SCALE, cs, sa, sb).astype(BF16)
            dkv_ref[:, lo:mid] = dk_ref[:, lo:mid]
            dkv_ref[:, mid:hi] = dv_ref[:, hh * DV:(hh + 1) * DV]
            dkpe = dkpe + dk_ref[:, mid:hi].astype(F32)
        lane = lax.broadcasted_iota(jnp.int32, (tm, 128), 1)
        dkr = jnp.where(lane < ROPE, _rotate(dkpe, cs, sa, sb), 0.0)
        dqn = _dot(dq_ref[...], wuq_ref[...])
        dkvn = _dot(dkv_ref[...], wukv_ref[...])
        gq = gq_ref[...]
        gkv = gkv_ref[...]
        dnq = dqn * gq
        dnk = dkvn * gkv
        dql = rq * (dnq - nq * jnp.mean(dnq * nq, axis=-1, keepdims=True))
        dkl = rk * (dnk - nk * jnp.mean(dnk * nk, axis=-1, keepdims=True))
        dp_ref[:, :QL] = dql.astype(BF16)
        dp_ref[:, QL:QL + KVL] = dkl.astype(BF16)
        dp_ref[:, QL + KVL:QL + KVL + 128] = dkr.astype(BF16)
        dp_ref[:, QL + KVL + 128:] = jnp.zeros((tm, D - QL - KVL - 128), BF16)

        @pl.when(i == 0)
        def _():
            dgq_ref[...] = jnp.zeros_like(dgq_ref)
            dgkv_ref[...] = jnp.zeros_like(dgkv_ref)

        dgq_ref[...] += jnp.sum(dqn * nq, axis=0, keepdims=True)
        dgkv_ref[...] += jnp.sum(dkvn * nk, axis=0, keepdims=True)

    row = lambda w: pl.BlockSpec((tm, w), lambda i: (i, 0))
    const = lambda a: pl.BlockSpec(a.shape, lambda i: (0,) * a.ndim)
    seg = pl.BlockSpec((None, tm, D), lambda i: (SEG_LAT, i, 0))
    return pl.pallas_call(
        body, name="mla_prep_bwd", grid=(t // tm,),
        in_specs=[pl.BlockSpec(memory_space=pl.ANY), seg, row(H * DQK), row(H * DQK), row(H * DV), row(1),
                  const(g_q), const(g_kv), const(wuq), const(wukv)] + [const(a) for a in tabs],
        out_specs=[seg, row(H * DQK), row(H * DQK),
                   pl.BlockSpec((1, QL), lambda i: (0, 0)), pl.BlockSpec((1, KVL), lambda i: (0, 0))],
        out_shape=[jax.ShapeDtypeStruct(dproj.shape, BF16),
                   jax.ShapeDtypeStruct((t, H * DQK), BF16), jax.ShapeDtypeStruct((t, H * DQK), BF16),
                   jax.ShapeDtypeStruct((1, QL), F32), jax.ShapeDtypeStruct((1, KVL), F32)],
        input_output_aliases={0: 0},
        compiler_params=_params(("arbitrary",)),
    )(dproj, proj, dq_rot, dk, dv, pos, g_q, g_kv, wuq, wukv, *tabs)


def _causal_mask(s, shift):
    row = lax.broadcasted_iota(jnp.int32, s.shape, 0)
    col = lax.broadcasted_iota(jnp.int32, s.shape, 1)
    return jnp.where(col <= row + shift, s, -1e30)


def flash_fwd(q, k, kv, nb, seq):
    t = q.shape[0]
    tq = min(FLASH_TQ, seq // 2)
    nq = seq // tq
    assert nq % 2 == 0, "blocks are processed in pairs"

    def update(state, s, vblk):
        m, l, acc = state
        m_new = jnp.maximum(m, jnp.max(s, axis=1, keepdims=True))
        p = jnp.exp2(s - m_new)
        alpha = jnp.exp2(m - m_new)
        return (m_new, alpha * l + jnp.sum(p, axis=1, keepdims=True),
                alpha * acc + _dot(p.astype(BF16), vblk))

    def finish(state, rows, o_ref, lse_ref):
        m, l, acc = state
        o_ref[rows, :] = (acc / l).astype(BF16)
        lse_ref[rows, :] = jnp.broadcast_to(m + jnp.log(l) * LOG2E, (m.shape[0], DV))

    def body(q_ref, k_ref, v_ref, o_ref, lse_ref):
        for qp in range(0, nq, 2):
            rows = 2 * tq
            q0 = qp * tq
            qv = q_ref[q0:q0 + rows, :]
            state = (jnp.full((rows, 1), -1e30, F32), jnp.zeros((rows, 1), F32), jnp.zeros((rows, DV), F32))
            for j in range(qp + 1):
                ks = slice(j * tq, (j + 1) * tq)
                s = _dot_nt(qv, k_ref[ks, :])
                if j == qp:
                    s = _causal_mask(s, 0)
                state = update(state, s, v_ref[ks, :])
            finish(tuple(a[:tq] for a in state), slice(q0, q0 + tq), o_ref, lse_ref)
            ks = slice(q0 + tq, q0 + 2 * tq)
            low = tuple(a[tq:] for a in state)
            low = update(low, _causal_mask(_dot_nt(qv[tq:], k_ref[ks, :]), 0), v_ref[ks, :])
            finish(low, slice(q0 + tq, q0 + 2 * tq), o_ref, lse_ref)

    out_blk = pl.BlockSpec((seq, DV), lambda b, h: (b, h))
    return pl.pallas_call(
        body, name="flash_fwd", grid=(nb, H),
        in_specs=[pl.BlockSpec((seq, DQK), lambda b, h: (b, h)),
                  pl.BlockSpec((seq, DQK), lambda b, h: (b, h)),
                  pl.BlockSpec((seq, DV), lambda b, h: (b, 2 * h + 1))],
        out_specs=[out_blk, out_blk],
        out_shape=[jax.ShapeDtypeStruct((t, H * DV), BF16), jax.ShapeDtypeStruct((t, H * DV), F32)],
        compiler_params=_params(("parallel", "parallel")),
    )(q, k, kv)


def flash_bwd(q, k, kv, o, do, lse, nb, seq, token):
    t = q.shape[0]
    tq = min(FLASH_TQ, seq)
    nq = seq // tq

    def body(q_ref, k_ref, v_ref, o_ref, do_ref, lse_ref, tok_ref, dq_ref, dk_ref, dv_ref):
        delta, lse = [], []
        for qi in range(nq):
            qs = slice(qi * tq, (qi + 1) * tq)
            dl = jnp.sum(do_ref[qs, :].astype(F32) * o_ref[qs, :].astype(F32), axis=1, keepdims=True)
            delta.append(jnp.broadcast_to(dl, (tq, DV)).T[:1, :])
            lse.append(lse_ref[qs, :].T[:1, :])
        for ki in range(nq):
            ks = slice(ki * tq, (ki + 1) * tq)
            kb = k_ref[ks, :]
            vb = v_ref[ks, :]
            dk = jnp.zeros((tq, DQK), F32)
            dv = jnp.zeros((tq, DV), F32)
            for qi in range(ki, nq):
                qs = slice(qi * tq, (qi + 1) * tq)
                qv = q_ref[qs, :]
                dov = do_ref[qs, :]
                st = _dot_nt(kb, qv)
                if qi == ki:
                    row = lax.broadcasted_iota(jnp.int32, st.shape, 0)
                    col = lax.broadcasted_iota(jnp.int32, st.shape, 1)
                    st = jnp.where(row <= col, st, -1e30)
                pt = jnp.exp2(st - lse[qi])
                dpt = _dot_nt(vb, dov)
                dzt = (pt * (dpt - delta[qi])).astype(BF16)
                dv = dv + _dot(pt.astype(BF16), dov)
                dk = dk + _dot(dzt, qv)
                dqb = _dot_tn(dzt, kb)
                if ki == 0:
                    dq_ref[qs, :] = dqb
                else:
                    dq_ref[qs, :] += dqb
            dk_ref[ks, :] = (dk * LN2).astype(BF16)
            dv_ref[ks, :] = dv.astype(BF16)

    full = lambda w, col: pl.BlockSpec((seq, w), col)
    same = lambda b, h: (b, h)
    return pl.pallas_call(
        body, name="flash_bwd", grid=(nb, H),
        in_specs=[full(DQK, same), full(DQK, same), full(DV, lambda b, h: (b, 2 * h + 1)),
                  full(DV, same), full(DV, same), full(DV, same),
                  pl.BlockSpec((8, 128), lambda b, h: (0, 0))],
        out_specs=[full(DQK, same), full(DQK, same), full(DV, same)],
        out_shape=[jax.ShapeDtypeStruct((t, H * DQK), F32), jax.ShapeDtypeStruct((t, H * DQK), BF16),
                   jax.ShapeDtypeStruct((t, H * DV), BF16)],
        compiler_params=_params(("parallel", "parallel")),
    )(q, k, kv, o, do, lse, token)


TAIL_TM = 512


def tail_fwd(y, attn, proj, x2, tgt, gate, g_post, wco, wmo, wout, seq):
    t = y.shape[0]
    nb = t // seq
    tm = min(TAIL_TM, seq)
    tpb = seq // tm

    def body(y_ref, at_ref, p_ref, x_ref, t_ref, gate_ref, gp_ref, wco_ref, wmo_ref, wout_ref,
             o_ref, ya_ref, yb_ref, m_ref, do2_ref, dout_ref, dgate_ref, dgp_ref, loss_ref):
        i = pl.program_id(0)
        bz = p_ref[0].astype(F32)
        ga = p_ref[1].astype(F32)
        gb = p_ref[2].astype(F32)
        ov = (at_ref[...].astype(F32) * (bz * _sig(bz))).astype(BF16)
        o_ref[...] = ov
        ya = _dot(y_ref[...], wco_ref[...])
        yb = _dot(ov, wmo_ref[...])
        ya_ref[...] = ya.astype(BF16)
        yb_ref[...] = yb.astype(BF16)
        mv = (_sig(ga) * ya + _sig(gb) * yb).astype(BF16)
        m_ref[...] = mv
        o2 = _dot(mv, wout_ref[...])
        r = lax.rsqrt(jnp.mean(o2 * o2, axis=-1, keepdims=True) + EPS)
        nrm = o2 * r
        gp = gp_ref[...]
        gate_v = gate_ref[...]
        rn = nrm * gp
        err = x_ref[...] + gate_v * rn - t_ref[...]
        dout = err * (1.0 / D)
        dout_ref[...] = dout
        dn = dout * gate_v * gp
        do2_ref[...] = (r * (dn - nrm * jnp.mean(dn * nrm, axis=-1, keepdims=True))).astype(BF16)

        @pl.when(i % tpb == 0)
        def _():
            dgate_ref[...] = jnp.zeros_like(dgate_ref)

        @pl.when(i == 0)
        def _():
            dgp_ref[...] = jnp.zeros_like(dgp_ref)
            loss_ref[...] = jnp.zeros_like(loss_ref)

        dgate_ref[...] += jnp.sum(dout * rn, axis=0, keepdims=True)
        dgp_ref[...] += jnp.sum(dout * gate_v * nrm, axis=0, keepdims=True)
        loss_ref[...] += 0.5 * jnp.sum(jnp.mean(err * err, axis=-1, keepdims=True), axis=0, keepdims=True)

    row = pl.BlockSpec((tm, D), lambda i: (i, 0))
    per_batch = pl.BlockSpec((None, 1, D), lambda i: (i // tpb, 0, 0))
    vec = pl.BlockSpec((1, D), lambda i: (0, 0))
    wgt = pl.BlockSpec((D, D), lambda i: (0, 0))
    act = jax.ShapeDtypeStruct((t, D), BF16)
    return pl.pallas_call(
        body, name="tail_fwd", grid=(t // tm,),
        in_specs=[row, row, pl.BlockSpec((3, tm, D), lambda i: (0, i, 0)), row, row, per_batch, vec,
                  wgt, wgt, wgt],
        out_specs=[row, row, row, row, row, row, per_batch, vec, pl.BlockSpec((1, 1), lambda i: (0, 0))],
        out_shape=[act, act, act, act, act, jax.ShapeDtypeStruct((t, D), F32),
                   jax.ShapeDtypeStruct((nb, 1, D), F32), jax.ShapeDtypeStruct((1, D), F32),
                   jax.ShapeDtypeStruct((1, 1), F32)],
        compiler_params=_params(("arbitrary",)),
    )(y, attn, proj, x2, tgt, gate, g_post, wco, wmo, wout)


def tail_bwd(do2, proj, ya, yb, attn, wout, wmo, wco):
    t = do2.shape[0]
    tm = min(TAIL_TM, t)

    def body(do2_ref, p_ref, ya_ref, yb_ref, at_ref, wout_ref, wmo_ref, wco_ref,
             dp_ref, dya_ref, dyb_ref, dat_ref, dy_ref):
        bz = p_ref[0].astype(F32)
        ga = p_ref[1].astype(F32)
        gb = p_ref[2].astype(F32)
        dm = _dot_nt(do2_ref[...], wout_ref[...])
        sa = _sig(ga)
        sb = _sig(gb)
        dya = (dm * sa).astype(BF16)
        dyb = (dm * sb).astype(BF16)
        dya_ref[...] = dya
        dyb_ref[...] = dyb
        dp_ref[1] = (dm * ya_ref[...].astype(F32) * (sa * (1.0 - sa))).astype(BF16)
        dp_ref[2] = (dm * yb_ref[...].astype(F32) * (sb * (1.0 - sb))).astype(BF16)
        dov = _dot_nt(dyb, wmo_ref[...])
        sz = _sig(bz)
        dat_ref[...] = (dov * (bz * sz)).astype(BF16)
        dp_ref[0] = (dov * at_ref[...].astype(F32) * (sz * (1.0 + bz * (1.0 - sz)))).astype(BF16)
        dy_ref[...] = _dot_nt(dya, wco_ref[...]).astype(BF16)

    row = pl.BlockSpec((tm, D), lambda i: (i, 0))
    seg3 = pl.BlockSpec((3, tm, D), lambda i: (0, i, 0))
    wgt = pl.BlockSpec((D, D), lambda i: (0, 0))
    act = jax.ShapeDtypeStruct((t, D), BF16)
    return pl.pallas_call(
        body, name="tail_bwd", grid=(t // tm,),
        in_specs=[row, seg3, row, row, row, wgt, wgt, wgt],
        out_specs=[seg3, row, row, row, row],
        out_shape=[jax.ShapeDtypeStruct((NSEG, t, D), BF16), act, act, act, act],
        compiler_params=_params(("parallel",)),
    )(do2, proj, ya, yb, attn, wout, wmo, wco)


def _adam_update(w, m, v, grad):
    mn = ADAM_B1 * m + (1.0 - ADAM_B1) * grad
    vn = ADAM_B2 * v + (1.0 - ADAM_B2) * (grad * grad)
    m_hat = mn / (1.0 - ADAM_B1 ** ADAM_STEP)
    v_hat = vn / (1.0 - ADAM_B2 ** ADAM_STEP)
    return -ADAM_LR * (m_hat / (jnp.sqrt(v_hat) + ADAM_EPS) + ADAM_WD * w), mn, vn


def adamw(w, m, v, g, name, token):
    rows, cols = w.shape
    tr = rows
    for cand in (256, 128, 64, 32, 16, 8):
        if rows % cand == 0 and rows > cand:
            tr = cand
            break

    def body(w_ref, m_ref, v_ref, g_ref, tok_ref, d_ref, mo_ref, vo_ref):
        d_ref[...], mo_ref[...], vo_ref[...] = _adam_update(w_ref[...], m_ref[...], v_ref[...], g_ref[...])

    blk = pl.BlockSpec((tr, cols), lambda i: (i, 0))
    return pl.pallas_call(
        body, name=name, grid=(rows // tr,),
        in_specs=[blk] * 4 + [pl.BlockSpec((8, 128), lambda i: (0, 0))], out_specs=[blk] * 3,
        out_shape=[jax.ShapeDtypeStruct((rows, cols), F32)] * 3,
        compiler_params=_params(("parallel",)),
    )(w, m, v, g, token)


def adamw_scattered(w, m, v, own, land, me, tr, name, transpose=False):
    slot_rows = land.shape[1]
    cols = land.shape[2]
    rows = slot_rows if transpose else w.shape[0]
    per_slot = slot_rows // tr

    def body(me_ref, w_ref, m_ref, v_ref, own_ref, land_ref, go_ref, d_ref, mo_ref, vo_ref):
        grad = own_ref[...].astype(F32)
        for s in range(8):
            grad = grad + land_ref[s].astype(F32)
        if transpose:
            grad = grad.T
        go_ref[...] = grad
        d_ref[...], mo_ref[...], vo_ref[...] = _adam_update(w_ref[...], m_ref[...], v_ref[...], grad)

    wblk = pl.BlockSpec(w.shape if transpose else (tr, w.shape[1]), lambda i, s: (i, 0))
    return pl.pallas_call(
        body, name=name,
        grid_spec=pltpu.PrefetchScalarGridSpec(
            num_scalar_prefetch=1, grid=(rows // tr,),
            in_specs=[wblk, wblk, wblk,
                      pl.BlockSpec((tr, cols), lambda i, s: (s[0] * per_slot + i, 0)),
                      pl.BlockSpec((8, tr, cols), lambda i, s: (0, i, 0))],
            out_specs=[wblk] * 4),
        out_shape=[jax.ShapeDtypeStruct(w.shape, F32)] * 4,
        compiler_params=_params(),
    )(me, w, m, v, own, land)


def adamw_win(wt, mt, vt, ka, ra, kb, rb):
    rows = wt.shape[0]
    tc = 256
    nh = (D // 2) // tc

    def body(w_ref, m_ref, v_ref, ka_ref, ra_ref, kb_ref, rb_ref, go_ref, d_ref, mo_ref, vo_ref):
        first = pl.program_id(0) < nh
        grad = jnp.where(first, ka_ref[...] + ra_ref[...].astype(F32), kb_ref[...] + rb_ref[...].astype(F32))
        go_ref[...] = grad
        d_ref[...], mo_ref[...], vo_ref[...] = _adam_update(w_ref[...], m_ref[...], v_ref[...], grad)

    blk = pl.BlockSpec((rows, tc), lambda j: (0, j))
    lo = pl.BlockSpec((rows, tc), lambda j: (0, jnp.minimum(j, nh - 1)))
    hi = pl.BlockSpec((rows, tc), lambda j: (0, jnp.maximum(j - nh, 0)))
    return pl.pallas_call(
        body, name="adamw_w_in", grid=(D // tc,),
        in_specs=[blk, blk, blk, lo, lo, hi, hi], out_specs=[blk] * 4,
        out_shape=[jax.ShapeDtypeStruct((rows, D), F32)] * 4,
        compiler_params=_params(("parallel",)),
    )(wt, mt, vt, ka, ra, kb, rb)


_ORD_A = ("x", "y", "c")
_ORD_B = ("y", "x", "c")


def _rows128(a, rows):
    flat = a.reshape(-1)
    return jnp.pad(flat, (0, rows * 128 - flat.shape[0])).reshape(rows, 128)


def kernel(x, c, positions, w_ada, b_ada, g_pre, w_in, conv_w, w_conv_out, g_q, w_uq, g_kv, w_ukv, w_mla_out, w_out, g_post, loss_target, m_w_ada, m_b_ada, m_g_pre, m_w_in, m_conv_w, m_w_conv_out, m_g_q, m_w_uq, m_g_kv, m_w_ukv, m_w_mla_out, m_w_out, m_g_post, v_w_ada, v_b_ada, v_g_pre, v_w_in, v_conv_w, v_w_conv_out, v_g_q, v_w_uq, v_g_kv, v_w_ukv, v_w_mla_out, v_w_out, v_g_post):
    nb, seq, _ = x.shape
    t = nb * seq
    mx, my, mc = lax.axis_index("x"), lax.axis_index("y"), lax.axis_index("c")
    me = 4 * mx + 2 * my + mc
    co = {"x": mx, "y": my, "c": mc}

    x2 = x.reshape(t, D)
    tgt2 = loss_target.reshape(t, D)
    pos2 = positions.reshape(t, 1)

    ada_cols = w_ada.shape[2]
    b_cols = lax.dynamic_slice(b_ada, (0, me * ada_cols), (1, ada_cols))
    c_g, taps_g, mod_g = ada_gather(jnp.pad(c, ((0, 8 - nb), (0, 0))), _rows128(conv_w[0], 8), w_ada[0], b_cols)
    c_all = c_g[:, :nb].reshape(8 * nb, D)
    conv_full = taps_g[:, 0:3].transpose(1, 0, 2).reshape(3, D)
    conv_full8 = jnp.pad(conv_full, ((0, 5), (0, 0)))
    mod = mod_g[:, :nb].transpose(1, 0, 2).reshape(nb, 8 * ada_cols)
    shift = mod[:, 0:D].reshape(nb, 1, D)
    scale = mod[:, D:2 * D].reshape(nb, 1, D)
    gate = mod[:, 2 * D:3 * D].reshape(nb, 1, D)

    wt = w_in[0].T.astype(BF16)
    lo = lax.bitcast_convert_type(wt[:, :D // 2], jnp.uint16).astype(jnp.uint32)
    hi = lax.bitcast_convert_type(wt[:, D // 2:], jnp.uint16).astype(jnp.uint32)
    wt_bits = lax.bitcast_convert_type(lo | (hi << 16), F32)
    wt_bits, mod = lax.optimization_barrier((wt_bits, mod))
    shift = mod[:, 0:D].reshape(nb, 1, D)
    scale = mod[:, D:2 * D].reshape(nb, 1, D)
    gate = mod[:, 2 * D:3 * D].reshape(nb, 1, D)
    q4 = D // 4
    r3rd = wt_bits.shape[0] // 3
    plan = [(0, (k * r3rd, r3rd), (g * q4, q4), (_ORD_A, _ORD_B)[g]) for k in range(3) for g in range(2)]
    gw = allgather_big([wt_bits], plan, "gather_w_in")
    late = [w_conv_out[0].astype(BF16), w_mla_out[0].astype(BF16), w_out[0].astype(BF16),
            jnp.pad(w_uq[0].T.astype(BF16), ((0, DQK - 192), (0, 0))), w_ukv[0].T.astype(BF16)]
    gw0, late = lax.optimization_barrier((gw[0], late))
    late_state, late_token = gather_start(late, "gather_late_start")
    wt_bits_all = gw0.reshape(N_IN, D // 2)

    inv_freq = ROPE_THETA ** (-jnp.arange(0, ROPE, 2, dtype=F32) / ROPE)
    invf = jnp.concatenate([inv_freq, inv_freq, jnp.zeros((128 - ROPE,), F32)]).reshape(1, 128)
    lane = np.arange(128)
    tabs = (invf,
            jnp.asarray(np.where(lane < HALF, -1.0, 0.0).reshape(1, 128), F32),
            jnp.asarray(np.where((lane >= HALF) & (lane < ROPE), 1.0, 0.0).reshape(1, 128), F32))

    h = prenorm_fwd(x2, scale, shift, g_pre, seq)
    proj, wt_p = proj_matmul(h, wt_bits_all, late_token)
    y = conv_fwd(proj, conv_full8, seq)
    gl = gather_wait(late_state, y, "gather_late_wait")
    wco = gl[0].reshape(D, D)
    wmo = gl[1].reshape(D, D)
    wout = gl[2].reshape(D, D)
    wuq_p = gl[3].reshape(H * DQK, QL)
    wukv = gl[4].reshape(H * 256, KVL)
    q_rot, k_cat, kv, qn, kvn = mla_prep_fwd(proj, pos2, g_q, g_kv, wuq_p, wukv, tabs)
    attn, lse = flash_fwd(q_rot, k_cat, kv, nb, seq)
    o, ya, yb, m, do2, dout, dgate, dg_post, loss_part = tail_fwd(
        y, attn, proj, x2, tgt2, gate, g_post, wco, wmo, wout, seq)

    dproj, dya, dyb, dattn, dy = tail_bwd(do2, proj, ya, yb, attn, wout, wmo, wco)
    g_wout = grad_matmul(m, do2, "grad_w_square")
    g_wmo = grad_matmul(o, dyb, "grad_w_square")
    g_wco = grad_matmul(y, dya, "grad_w_square")
    sc1, sc1_tok = scatter_start([g_wco, g_wmo, g_wout], "scatter_out_grads_start")
    dproj, dconv = conv_bwd(dproj, proj, dy, conv_full8, seq)
    dq_rot, dk, dv = flash_bwd(q_rot, k_cat, kv, attn, dattn, lse, nb, seq, sc1_tok)
    dproj, dq, dkv, dg_q, dg_kv = mla_prep_bwd(dproj, proj, dq_rot, dk, dv, pos2, g_q, g_kv, wuq_p, wukv, tabs)
    g_wuq_t = grad_matmul(dq, qn, "grad_w_uq")
    g_wukv_t = grad_matmul(dkv, kvn, "grad_w_ukv")
    sc2, sc2_tok = scatter_start([g_wuq_t, g_wukv_t], "scatter_mla_grads_start")
    g_win_p = win_grad_matmul(h, dproj, sc2_tok)

    g_wt = g_win_p.reshape(2, 2, 2, N_IN // 8, D)
    ords = [("c", "y", "x"), ("c", "x", "y")]
    hc = D // 2
    win_shape = (2, 2, N_IN // 8, hc)
    pick_w = lambda col: (lambda ref, cc: ref.at[:, :, 1 - cc["c"], :, pl.ds(col * hc, hc)])
    which1 = [0, 0]
    picks1 = [pick_w(0), pick_w(1)]
    st1, tok1 = swap_start([g_wt], which1, ["c"] * 2, picks1, [win_shape] * 2, "rs_c_start")
    assert nb == 2
    dh0 = dh_matmul(dproj, wt_p, tok1, seq, 0)
    (g_wt,), r1 = swap_wait(st1, dh0, which1, ["c"] * 2, picks1, "rs_c_wait")
    sel_xyc = jnp.stack([mx, my, mc]).astype(jnp.int32)
    sel2 = [jnp.stack([co[o[2]]]).astype(jnp.int32) for o in ords]
    first = [rs_win_add_first(g_wt, r1[0], sel_xyc, 1, 0, "rs_add_first_0"),
             rs_win_add_first(g_wt, r1[1], sel_xyc, 0, 1, "rs_add_first_1")]
    keep1, send1 = zip(*first)
    all4 = [0, 1]
    none4 = [None] * 2
    axes2 = [o[1] for o in ords]
    st2, tok2 = swap_start(list(send1), all4, axes2, none4, [s.shape for s in send1], "rs_ici1_start")

    dh1 = dh_matmul(dproj, wt_p, tok2, seq, 1)
    gx0, dsh0, dsc0, dgp0 = prenorm_bwd(dh0, x2, dout, scale, g_pre, seq, tok2, 0, None)
    _, r2 = swap_wait(st2, (gx0, dh1), all4, axes2, none4, "rs_ici1_wait")
    keep2, send2 = zip(*[rs_add_second(keep1[a], r2[a], sel2[a], "rs_add_second") for a in range(2)])
    axes3 = [o[2] for o in ords]
    st3, tok3 = swap_start(list(send2), all4, axes3, none4, [s.shape for s in send2], "rs_ici2_start")
    grad_x2, dsh1, dsc1, dgp1 = prenorm_bwd(dh1, x2, dout, scale, g_pre, seq, tok3, 1, gx0)
    dshift = jnp.stack([dsh0, dsh1])
    dscale = jnp.stack([dsc0, dsc1])
    dg_pre = dgp0 + dgp1

    dmod = jnp.concatenate([dshift, dscale, dgate], axis=2).reshape(nb * 3 * D // 128, 128)
    small = jnp.concatenate([
        dmod, _rows128(dg_pre, 8), _rows128(dg_post, 8), _rows128(dg_q, 8), _rows128(dg_kv, 8),
        dconv[0:3].reshape(24, 128), _rows128(loss_part, 8)], axis=0)
    small_g = small_allgather(small, "gather_small_grads")
    sums = slot_sum(small_g)
    dmod_all = small_g[:, 0:48].reshape(8 * nb, 3 * D)
    g_bada = (sums[0:24] + sums[24:48]).reshape(1, 3 * D)
    g_gpre = sums[48:56].reshape(1, D)
    g_gpost = sums[56:64].reshape(1, D)
    g_gq = sums[64:67].reshape(1, QL)
    g_gkv = sums[72:74].reshape(1, KVL)
    g_conv_full = sums[80:104].reshape(3, D)
    loss = sums[104, 0]
    g_conv = lax.dynamic_slice(g_conv_full, (0, me * 128), (3, 128))
    dmod_cols = lax.dynamic_slice(dmod_all, (0, me * ada_cols), (8 * nb, ada_cols))
    g_wada = ada_bwd(c_all, dmod_cols)

    res = {}
    res["w_ada"] = [o_[None] for o_ in (g_wada, *adamw(w_ada[0], m_w_ada[0], v_w_ada[0], g_wada, "adamw_w_ada", tok3))]

    def pack(b_, gp_, gpo_, gq_, gkv_, cw_):
        return jnp.concatenate([_rows128(b_, 24), _rows128(gp_, 8), _rows128(gpo_, 8), _rows128(gq_, 8),
                                _rows128(gkv_, 8), _rows128(cw_, 8)], axis=0)

    sw = pack(b_ada, g_pre, g_post, g_q, g_kv, conv_w)
    sm = pack(m_b_ada, m_g_pre, m_g_post, m_g_q, m_g_kv, m_conv_w)
    sv = pack(v_b_ada, v_g_pre, v_g_post, v_g_q, v_g_kv, v_conv_w)
    sg = pack(g_bada, g_gpre, g_gpost, g_gq, g_gkv, g_conv)
    small_out = (sg, *adamw(sw, sm, sv, sg, "adamw_small", tok3))

    _, r3 = swap_wait(st3, small_out[1], all4, axes3, none4, "rs_ici2_wait")

    (g_wco, g_wmo, g_wout), (l_wco, l_wmo, l_wout) = scatter_wait(sc1, small_out[2], "scatter_out_grads_wait")
    (g_wuq_t, g_wukv_t), (l_wuq, l_wukv) = scatter_wait(sc2, small_out[3], "scatter_mla_grads_wait")

    res["w_in"] = [o_.T[None] for o_ in adamw_win(w_in[0].T, m_w_in[0].T, v_w_in[0].T,
                                                  keep2[0], r3[0], keep2[1], r3[1])]
    me1 = me.reshape(1).astype(jnp.int32)
    res["w_uq"] = [o_.T[None] for o_ in adamw_scattered(
        w_uq[0].T, m_w_uq[0].T, v_w_uq[0].T, g_wuq_t, l_wuq, me1, 64, "adamw_w_uq")]
    res["w_ukv"] = [o_[None] for o_ in adamw_scattered(
        w_ukv[0], m_w_ukv[0], v_w_ukv[0], g_wukv_t, l_wukv, me1, KVL, "adamw_w_ukv", transpose=True)]
    for nm, wv, mv, vv, gg, ll in (("w_conv_out", w_conv_out, m_w_conv_out, v_w_conv_out, g_wco, l_wco),
                                   ("w_mla_out", w_mla_out, m_w_mla_out, v_w_mla_out, g_wmo, l_wmo),
                                   ("w_out", w_out, m_w_out, v_w_out, g_wout, l_wout)):
        res[nm] = [o_[None] for o_ in adamw_scattered(wv[0], mv[0], vv[0], gg, ll, me1, 128, "adamw_square")]

    def unpack(a):
        return {"b_ada": a[0:24].reshape(1, 3 * D), "g_pre": a[24:32].reshape(1, D),
                "g_post": a[32:40].reshape(1, D), "g_q": a[40:43].reshape(1, QL),
                "g_kv": a[48:50].reshape(1, KVL), "conv_w": a[56:59].reshape(-1)[:3 * 128].reshape(1, 3, 128)}

    for nm in ("b_ada", "g_pre", "g_post", "g_q", "g_kv", "conv_w"):
        res[nm] = [unpack(a)[nm] for a in small_out]

    order = ["w_ada", "b_ada", "g_pre", "w_in", "conv_w", "w_conv_out", "g_q", "w_uq", "g_kv", "w_ukv",
             "w_mla_out", "w_out", "g_post"]
    out = [loss, grad_x2.reshape(nb, seq, D)]
    for k_ in range(4):
        out += [res[nm][k_] for nm in order]
    return tuple(out)
```

```python
import numpy as np
import jax
import jax.numpy as jnp
from jax import lax
from jax.experimental import pallas as pl
from jax.experimental.pallas import tpu as pltpu

F32 = jnp.float32
BF16 = jnp.bfloat16
MESH = pl.DeviceIdType.MESH

D = 1024
H = 8
QL = 384
KVL = 256
ROPE = 64
HALF = ROPE // 2
DQK = 256
DV = 128
NSEG = 8
NP = NSEG * D
EPS = 1e-6
ROPE_THETA = 10000.0
SM_SCALE = (128 + ROPE) ** -0.5
LOG2E = 1.4426950408889634
LN2 = 0.6931471805599453
FLASH_TQ = 512

SEG_BZ, SEG_GA, SEG_GB, SEG_LAT, SEG_V = 0, 1, 2, 3, 4

ADAM_LR = 0.001
ADAM_B1 = 0.9
ADAM_B2 = 0.999
ADAM_EPS = 1e-08
ADAM_WD = 0.01
ADAM_STEP = 10

VMEM_LIMIT = 56 * 1024 * 1024


def _params(sem=None, vmem=VMEM_LIMIT):
    kw = dict(vmem_limit_bytes=vmem)
    if sem is not None:
        kw["dimension_semantics"] = sem
    return pltpu.CompilerParams(**kw)


def _sig(v):
    return 0.5 * jnp.tanh(0.5 * v) + 0.5


def _dot(a, b):
    return jnp.dot(a, b, preferred_element_type=F32)


def _dot_nt(a, b):
    return lax.dot_general(a, b, (((1,), (1,)), ((), ())), preferred_element_type=F32)


def _dot_tn(a, b):
    return lax.dot_general(a, b, (((0,), (0,)), ((), ())), preferred_element_type=F32)


_AXIS_POS = {"x": 0, "y": 1, "c": 2}


def _coords():
    return lax.axis_index("x"), lax.axis_index("y"), lax.axis_index("c")


def _partner(axis):
    p = list(_coords())
    p[_AXIS_POS[axis]] = 1 - p[_AXIS_POS[axis]]
    return tuple(p)


def small_allgather(v, name):
    rows = v.shape[0]

    def body(v_ref, out_ref, send_sems, recv_sems):
        x, y, c = _coords()
        me = 4 * x + 2 * y + c
        out_ref[me] = v_ref[...]
        copies = []
        for k in range(1, 8):
            peer = (1 - x if k & 4 else x, 1 - y if k & 2 else y, 1 - c if k & 1 else c)
            cp = pltpu.make_async_remote_copy(
                src_ref=v_ref, dst_ref=out_ref.at[me],
                send_sem=send_sems.at[k - 1], recv_sem=recv_sems.at[k - 1],
                device_id=peer, device_id_type=MESH)
            cp.start()
            copies.append(cp)
        for cp in copies:
            cp.wait()

    return pl.pallas_call(
        body, name=name,
        out_shape=jax.ShapeDtypeStruct((8, rows, 128), F32),
        in_specs=[pl.BlockSpec(memory_space=pltpu.VMEM)],
        out_specs=pl.BlockSpec(memory_space=pltpu.VMEM),
        scratch_shapes=[pltpu.SemaphoreType.DMA((7,)), pltpu.SemaphoreType.DMA((7,))],
    )(v)


def _own_block_placed(s):
    x, y, c = _coords()
    return lax.dynamic_update_slice(lax.empty((2, 2, 2) + s.shape, s.dtype), s[None, None, None],
                                    (x, y, c) + (0,) * s.ndim)


def allgather_big(arrs, plan, name):
    n = len(arrs)
    m = len(plan)
    nst = len(plan[0][3])

    def body(*refs):
        ins, outs = refs[n:2 * n], refs[2 * n:3 * n]
        send_sems, recv_sems = refs[3 * n:]
        x, y, c = _coords()
        co = {"x": x, "y": y, "c": c}

        def window(ref, lead, rows, cols):
            win = tuple(slice(None) if w is None else pl.ds(w[0], w[1]) for w in (rows, cols))
            return ref.at[tuple(lead) + win]

        def held(e, free):
            i, rows, cols, _ = plan[e]
            lead = [slice(None) if ax in free else co[ax] for ax in ("x", "y", "c")]
            return window(outs[i], lead, rows, cols)

        def rcopy(e, stage, src, dst, axis):
            return pltpu.make_async_remote_copy(
                src_ref=src, dst_ref=dst,
                send_sem=send_sems.at[e, stage], recv_sem=recv_sems.at[e, stage],
                device_id=_partner(axis), device_id_type=MESH)

        stages = [[] for _ in range(nst)]
        for e, (i, rows, cols, order) in enumerate(plan):
            cp = rcopy(e, 0, window(ins[i], [], rows, cols), held(e, ()), order[0])
            cp.start()
            stages[0].append(cp)
        for s in range(1, nst):
            for e, (i, rows, cols, order) in enumerate(plan):
                stages[s - 1][e].wait_recv()
                blk = held(e, order[:s])
                cp = rcopy(e, s, blk, blk, order[s])
                cp.start()
                stages[s].append(cp)
        for e in range(m):
            stages[nst - 1][e].wait_recv()
        for e in range(m):
            for s in range(nst):
                stages[s][e].wait_send()

    any_spec = pl.BlockSpec(memory_space=pl.ANY)
    lands = [_own_block_placed(a) for a in arrs]
    return pl.pallas_call(
        body, name=name,
        out_shape=[jax.ShapeDtypeStruct(l.shape, l.dtype) for l in lands],
        in_specs=[any_spec] * (2 * n),
        out_specs=[any_spec] * n,
        input_output_aliases={i: i for i in range(n)},
        scratch_shapes=[pltpu.SemaphoreType.DMA((m, nst)), pltpu.SemaphoreType.DMA((m, nst))],
    )(*lands, *arrs)


_HBM =pl.BlockSpec(memory_space=pltpu.HBM)
_SEM = pl.BlockSpec(memory_space=pltpu.SEMAPHORE)


def _swap_copies(srcs, lands, send_sems, recv_sems, axes, picks):
    x, y, c = _coords()
    co = {"x": x, "y": y, "c": c}
    return [pltpu.make_async_remote_copy(
        src_ref=srcs[a] if picks[a] is None else picks[a](srcs[a], co), dst_ref=lands[a],
        send_sem=send_sems.at[a], recv_sem=recv_sems.at[a],
        device_id=_partner(axes[a]), device_id_type=MESH) for a in range(len(srcs))]


def swap_start(arrs, which, axes, picks, out_shapes, name):
    ns, n = len(arrs), len(which)

    def body(*refs):
        srcs, lands = refs[:ns], refs[ns:ns + n]
        send_sems, recv_sems = refs[ns + n:ns + n + 2]
        token = refs[-1]
        for cp in _swap_copies([srcs[i] for i in which], lands, send_sems, recv_sems, axes, picks):
            cp.start()
        token[...] = jnp.zeros_like(token)

    lands = [lax.empty(s, arrs[i].dtype) for s, i in zip(out_shapes, which)]
    ops = [pltpu.with_memory_space_constraint(a, pltpu.HBM) for a in list(arrs) + lands]
    out = pl.pallas_call(
        body, name=name,
        out_shape=[pltpu.SemaphoreType.DMA((n,)), pltpu.SemaphoreType.DMA((n,))]
        + [pltpu.HBM(o.shape, o.dtype) for o in ops] + [jax.ShapeDtypeStruct((8, 128), F32)],
        in_specs=[_HBM] * (ns + n),
        out_specs=[_SEM, _SEM] + [_HBM] * (ns + n) + [pl.BlockSpec(memory_space=pltpu.VMEM)],
        input_output_aliases={i: 2 + i for i in range(ns + n)},
        compiler_params=pltpu.CompilerParams(has_side_effects=pltpu.SideEffectType.DATAFLOW_SIDE_EFFECTING),
    )(*ops)
    return out[:-1], out[-1]


def swap_wait(state, after, which, axes, picks, name):
    n = len(which)
    ns = len(state) - 2 - n

    def body(*refs):
        srcs, lands = refs[:ns], refs[ns:ns + n]
        send_sems, recv_sems = refs[ns + n:ns + n + 2]
        for cp in _swap_copies([srcs[i] for i in which], lands, send_sems, recv_sems, axes, picks):
            cp.wait_send()
            cp.wait_recv()

    thru = list(state[2:])
    after = list(after) if isinstance(after, (list, tuple)) else [after]
    out = pl.pallas_call(
        body, name=name,
        out_shape=[pltpu.HBM(o.shape, o.dtype) for o in thru],
        in_specs=[_HBM] * (ns + n) + [_SEM, _SEM] + [pl.BlockSpec(memory_space=pl.ANY)] * len(after),
        out_specs=[_HBM] * (ns + n),
        input_output_aliases={i: i for i in range(ns + n)},
        compiler_params=pltpu.CompilerParams(has_side_effects=pltpu.SideEffectType.DATAFLOW_SIDE_EFFECTING),
    )(*thru, state[0], state[1], *after)
    return out[:ns], out[ns:]


def _gather_copies(shards, lands, send_sems, recv_sems):
    x, y, c = _coords()
    copies = []
    for a in range(len(shards)):
        for k in range(1, 8):
            peer = (1 - x if k & 4 else x, 1 - y if k & 2 else y, 1 - c if k & 1 else c)
            copies.append(pltpu.make_async_remote_copy(
                src_ref=shards[a], dst_ref=lands[a].at[x, y, c],
                send_sem=send_sems.at[7 * a + k - 1], recv_sem=recv_sems.at[7 * a + k - 1],
                device_id=peer, device_id_type=MESH))
    return copies


def gather_start(shards, name):
    n = len(shards)
    x, y, c = _coords()

    def body(*refs):
        srcs, lands = refs[:n], refs[n:2 * n]
        send_sems, recv_sems = refs[2 * n:2 * n + 2]
        token = refs[-1]
        for cp in _gather_copies(srcs, lands, send_sems, recv_sems):
            cp.start()
        token[...] = jnp.zeros_like(token)

    lands = [_own_block_placed(s) for s in shards]
    ops = [pltpu.with_memory_space_constraint(a, pltpu.HBM) for a in list(shards) + lands]
    out = pl.pallas_call(
        body, name=name,
        out_shape=[pltpu.SemaphoreType.DMA((7 * n,)), pltpu.SemaphoreType.DMA((7 * n,))]
        + [pltpu.HBM(o.shape, o.dtype) for o in ops] + [jax.ShapeDtypeStruct((8, 128), F32)],
        in_specs=[_HBM] * (2 * n),
        out_specs=[_SEM, _SEM] + [_HBM] * (2 * n) + [pl.BlockSpec(memory_space=pltpu.VMEM)],
        input_output_aliases={i: 2 + i for i in range(2 * n)},
        compiler_params=pltpu.CompilerParams(has_side_effects=pltpu.SideEffectType.DATAFLOW_SIDE_EFFECTING),
    )(*ops)
    return out[:-1], out[-1]


def gather_wait(state, after, name):
    n = (len(state) - 2) // 2

    def body(*refs):
        srcs, lands = refs[:n], refs[n:2 * n]
        send_sems, recv_sems = refs[2 * n:2 * n + 2]
        for cp in _gather_copies(srcs, lands, send_sems, recv_sems):
            cp.wait_send()
            cp.wait_recv()

    thru = list(state[2:])
    out = pl.pallas_call(
        body, name=name,
        out_shape=[pltpu.HBM(o.shape, o.dtype) for o in thru],
        in_specs=[_HBM] * (2 * n) + [_SEM, _SEM, pl.BlockSpec(memory_space=pl.ANY)],
        out_specs=[_HBM] * (2 * n),
        input_output_aliases={i: i for i in range(2 * n)},
        compiler_params=pltpu.CompilerParams(has_side_effects=pltpu.SideEffectType.DATAFLOW_SIDE_EFFECTING),
    )(*thru, state[0], state[1], after)
    return out[n:]


def _scatter_copies(grads, lands, send_sems, recv_sems):
    x, y, c = _coords()
    me = 4 * x + 2 * y + c
    copies = []
    for a in range(len(grads)):
        r = grads[a].shape[0] // 8
        for k in range(1, 8):
            px, py, pc = (1 - x if k & 4 else x, 1 - y if k & 2 else y, 1 - c if k & 1 else c)
            rows = pl.ds(pl.multiple_of((4 * px + 2 * py + pc) * r, r), r)
            copies.append(pltpu.make_async_remote_copy(
                src_ref=grads[a].at[rows], dst_ref=lands[a].at[me],
                send_sem=send_sems.at[7 * a + k - 1], recv_sem=recv_sems.at[7 * a + k - 1],
                device_id=(px, py, pc), device_id_type=MESH))
    return copies


def scatter_start(grads, name):
    n = len(grads)

    def body(*refs):
        srcs, lands = refs[:n], refs[n:2 * n]
        send_sems, recv_sems = refs[2 * n:2 * n + 2]
        token = refs[-1]
        for cp in _scatter_copies(srcs, lands, send_sems, recv_sems):
            cp.start()
        token[...] = jnp.zeros_like(token)

    lands = [jnp.zeros((8, g.shape[0] // 8, g.shape[1]), g.dtype) for g in grads]
    ops = [pltpu.with_memory_space_constraint(a, pltpu.HBM) for a in list(grads) + lands]
    out = pl.pallas_call(
        body, name=name,
        out_shape=[pltpu.SemaphoreType.DMA((7 * n,)), pltpu.SemaphoreType.DMA((7 * n,))]
        + [pltpu.HBM(o.shape, o.dtype) for o in ops] + [jax.ShapeDtypeStruct((8, 128), F32)],
        in_specs=[_HBM] * (2 * n),
        out_specs=[_SEM, _SEM] + [_HBM] * (2 * n) + [pl.BlockSpec(memory_space=pltpu.VMEM)],
        input_output_aliases={i: 2 + i for i in range(2 * n)},
        compiler_params=pltpu.CompilerParams(has_side_effects=pltpu.SideEffectType.DATAFLOW_SIDE_EFFECTING),
    )(*ops)
    return out[:-1], out[-1]


def scatter_wait(state, after, name):
    n = (len(state) - 2) // 2

    def body(*refs):
        srcs, lands = refs[:n], refs[n:2 * n]
        send_sems, recv_sems = refs[2 * n:2 * n + 2]
        for cp in _scatter_copies(srcs, lands, send_sems, recv_sems):
            cp.wait_send()
            cp.wait_recv()

    thru = list(state[2:])
    after = list(after) if isinstance(after, (list, tuple)) else [after]
    out = pl.pallas_call(
        body, name=name,
        out_shape=[pltpu.HBM(o.shape, o.dtype) for o in thru],
        in_specs=[_HBM] * (2 * n) + [_SEM, _SEM] + [pl.BlockSpec(memory_space=pl.ANY)] * len(after),
        out_specs=[_HBM] * (2 * n),
        input_output_aliases={i: i for i in range(2 * n)},
        compiler_params=pltpu.CompilerParams(has_side_effects=pltpu.SideEffectType.DATAFLOW_SIDE_EFFECTING),
    )(*thru, state[0], state[1], *after)
    return out[:n], out[n:]


def rs_win_add_first(g, r, sel, next_dim, col, name):
    rows, cols = r.shape[2:]

    def body(sel_ref, gk_ref, rk_ref, gs_ref, rs_ref, keep_ref, send_ref):
        keep_ref[...] = gk_ref[...] + rk_ref[...]
        send_ref[...] = (gs_ref[...] + rs_ref[...]).astype(BF16)

    def g_map(flip):
        def f(j, s):
            nxt = 1 - s[next_dim] if flip else s[next_dim]
            return (nxt, j, s[2], 0, col) if next_dim == 0 else (j, nxt, s[2], 0, col)
        return f

    def r_map(flip):
        def f(j, s):
            nxt = 1 - s[next_dim] if flip else s[next_dim]
            return (nxt, j, 0, 0) if next_dim == 0 else (j, nxt, 0, 0)
        return f

    gblk = (None, None, None, rows, cols)
    rblk = (None, None, rows, cols)
    oblk = (None, rows, cols)
    return pl.pallas_call(
        body, name=name,
        grid_spec=pltpu.PrefetchScalarGridSpec(
            num_scalar_prefetch=1, grid=(2,),
            in_specs=[pl.BlockSpec(gblk, g_map(False)), pl.BlockSpec(rblk, r_map(False)),
                      pl.BlockSpec(gblk, g_map(True)), pl.BlockSpec(rblk, r_map(True))],
            out_specs=[pl.BlockSpec(oblk, lambda j, s: (j, 0, 0)),
                       pl.BlockSpec(oblk, lambda j, s: (j, 0, 0))]),
        out_shape=[jax.ShapeDtypeStruct((2, rows, cols), F32),
                   jax.ShapeDtypeStruct((2, rows, cols), BF16)],
        compiler_params=_params(),
    )(sel, g, r, g, r)


def rs_add_second(k, r, sel, name):
    _, rows, cols = k.shape
    tr = rows // 2 if rows % 32 == 0 else rows
    nt = rows // tr

    def body(sel_ref, kk_ref, rk_ref, ks_ref, rs_ref, keep_ref, send_ref):
        keep_ref[...] = kk_ref[...] + rk_ref[...].astype(F32)
        send_ref[...] = (ks_ref[...] + rs_ref[...].astype(F32)).astype(BF16)

    blk = (None, tr, cols)
    oblk = (tr, cols)
    return pl.pallas_call(
        body, name=name,
        grid_spec=pltpu.PrefetchScalarGridSpec(
            num_scalar_prefetch=1, grid=(nt,),
            in_specs=[
                pl.BlockSpec(blk, lambda i, s: (s[0], i, 0)),
                pl.BlockSpec(blk, lambda i, s: (s[0], i, 0)),
                pl.BlockSpec(blk, lambda i, s: (1 - s[0], i, 0)),
                pl.BlockSpec(blk, lambda i, s: (1 - s[0], i, 0)),
            ],
            out_specs=[pl.BlockSpec(oblk, lambda i, s: (i, 0)),
                       pl.BlockSpec(oblk, lambda i, s: (i, 0))]),
        out_shape=[jax.ShapeDtypeStruct((rows, cols), F32),
                   jax.ShapeDtypeStruct((rows, cols), BF16)],
        compiler_params=_params(),
    )(sel, k, r, k, r)


SEG_ROWS = (4800, 5824, 6848, 4096, 0, 1024, 2048, 3072)
LAT_ROWS = QL + KVL + ROPE
N_IN = 7872


def _seg_row(j):
    return pl.multiple_of(jnp.where(j < 3, 4800 + 1024 * j, jnp.where(j == 3, 4096, (j - 4) * 1024)), 8)


def proj_matmul(h, wt_bits, token):
    t = h.shape[0]
    tm = min(2048, t)

    def body(h_ref, w_hbm, tok_ref, o_ref, wt_ref, buf, sems):
        j = pl.program_id(0)
        slot = j % 2

        def fetch(seg, into):
            return pltpu.make_async_copy(w_hbm.at[pl.ds(_seg_row(seg), D)], buf.at[into], sems.at[into])

        @pl.when(pl.program_id(1) == 0)
        def _():
            @pl.when(j == 0)
            def _():
                fetch(j, slot).start()

            fetch(j, slot).wait()

            @pl.when(j + 1 < NSEG)
            def _():
                fetch(j + 1, 1 - slot).start()

            bits = pltpu.bitcast(buf[slot], jnp.uint32)
            row = lax.broadcasted_iota(jnp.int32, (D, D // 2), 0)
            live = jnp.logical_or(j != SEG_LAT, row < LAT_ROWS)
            lo = pltpu.bitcast(bits << 16, F32)
            hi = pltpu.bitcast(bits & jnp.uint32(0xFFFF0000), F32)
            wt_ref[:, :D // 2] = jnp.where(live, lo, 0.0).astype(BF16)
            wt_ref[:, D // 2:] = jnp.where(live, hi, 0.0).astype(BF16)

        o_ref[...] = _dot_nt(h_ref[...], wt_ref[...]).astype(BF16)

    return pl.pallas_call(
        body, name="proj_matmul", grid=(NSEG, t // tm),
        in_specs=[pl.BlockSpec((tm, D), lambda j, i: (i, 0)),
                  pl.BlockSpec(memory_space=pl.ANY),
                  pl.BlockSpec((8, 128), lambda j, i: (0, 0))],
        out_specs=[pl.BlockSpec((None, tm, D), lambda j, i: (j, i, 0)),
                   pl.BlockSpec((D, D), lambda j, i: (j, 0))],
        out_shape=[jax.ShapeDtypeStruct((NSEG, t, D), BF16), jax.ShapeDtypeStruct((NP, D), BF16)],
        scratch_shapes=[pltpu.VMEM((2, D, D // 2), F32), pltpu.SemaphoreType.DMA((2,))],
        compiler_params=_params(("arbitrary", "arbitrary")),
    )(h, wt_bits, token)


def dh_matmul(dproj, wt, token, seq, b):
    tm = min(1024, seq)
    nblk = seq // tm

    per = 2

    def body(b_ref, d_ref, w_ref, tok_ref, o_ref, acc_ref):
        k = pl.program_id(1)
        last = NSEG // per - 1

        def part():
            p = _dot(d_ref[0], w_ref[0:D, :])
            for j in range(1, per):
                p = p + _dot(d_ref[j], w_ref[j * D:(j + 1) * D, :])
            return p

        @pl.when(k == 0)
        def _():
            acc_ref[...] = part()

        @pl.when(jnp.logical_and(k > 0, k < last))
        def _():
            acc_ref[...] += part()

        @pl.when(k == last)
        def _():
            o_ref[...] = acc_ref[...] + part()

    return pl.pallas_call(
        body, name="dh_matmul",
        grid_spec=pltpu.PrefetchScalarGridSpec(
            num_scalar_prefetch=1, grid=(nblk, NSEG // per),
            in_specs=[pl.BlockSpec((per, tm, D), lambda i, k, s: (k, s[0] * nblk + i, 0)),
                      pl.BlockSpec((per * D, D), lambda i, k, s: (k, 0)),
                      pl.BlockSpec((8, 128), lambda i, k, s: (0, 0))],
            out_specs=pl.BlockSpec((tm, D), lambda i, k, s: (i, 0)),
            scratch_shapes=[pltpu.VMEM((tm, D), F32)]),
        out_shape=jax.ShapeDtypeStruct((seq, D), F32),
        compiler_params=_params(("parallel", "arbitrary")),
    )(jnp.full((1,), b, jnp.int32), dproj, wt, token)


def win_grad_matmul(h, dproj, token):
    t = h.shape[0]

    def body(h_ref, d_ref, tok_ref, o_hbm, acc_ref, sems):
        j = pl.program_id(0)

        def out_copy(jj, action):
            slot = lax.rem(jj, 2)

            @pl.when(jj != SEG_LAT)
            def _():
                action(pltpu.make_async_copy(acc_ref.at[slot], o_hbm.at[pl.ds(_seg_row(jj), D)],
                                             sems.at[slot]))

            @pl.when(jj == SEG_LAT)
            def _():
                action(pltpu.make_async_copy(acc_ref.at[slot, pl.ds(0, LAT_ROWS)],
                                             o_hbm.at[pl.ds(SEG_ROWS[SEG_LAT], LAT_ROWS)], sems.at[slot]))

        acc_ref[lax.rem(j, 2)] = _dot_tn(d_ref[...], h_ref[...])
        out_copy(j, lambda cp: cp.start())

        @pl.when(j > 0)
        def _():
            out_copy(j - 1, lambda cp: cp.wait())

        @pl.when(j == NSEG - 1)
        def _():
            out_copy(j, lambda cp: cp.wait())

    return pl.pallas_call(
        body, name="win_grad_matmul", grid=(NSEG,),
        in_specs=[pl.BlockSpec((t, D), lambda j: (0, 0)),
                  pl.BlockSpec((None, t, D), lambda j: (j, 0, 0)),
                  pl.BlockSpec((8, 128), lambda j: (0, 0))],
        out_specs=pl.BlockSpec(memory_space=pl.ANY),
        out_shape=jax.ShapeDtypeStruct((N_IN, D), F32),
        scratch_shapes=[pltpu.VMEM((2, D, D), F32), pltpu.SemaphoreType.DMA((2,))],
        compiler_params=_params(("arbitrary",)),
    )(h, dproj, token)


def grad_matmul(a, b, name):
    t, m = a.shape
    n = b.shape[1]
    tk = min(1024, t)
    nk = t // tk

    def body(a_ref, b_ref, o_ref, acc_ref):
        k = pl.program_id(0)
        part = lambda: _dot_tn(a_ref[...], b_ref[...])
        if nk == 1:
            o_ref[...] = part().astype(BF16)
            return

        @pl.when(k == 0)
        def _():
            acc_ref[...] = part()

        @pl.when(jnp.logical_and(k > 0, k < nk - 1))
        def _():
            acc_ref[...] += part()

        @pl.when(k == nk - 1)
        def _():
            o_ref[...] = (acc_ref[...] + part()).astype(BF16)

    return pl.pallas_call(
        body, name=name, grid=(nk,),
        in_specs=[pl.BlockSpec((tk, m), lambda k: (k, 0)),
                  pl.BlockSpec((tk, n), lambda k: (k, 0))],
        out_specs=pl.BlockSpec((m, n), lambda k: (0, 0)),
        out_shape=jax.ShapeDtypeStruct((m, n), BF16),
        scratch_shapes=[pltpu.VMEM((m, n), F32)],
        compiler_params=_params(("arbitrary",)),
    )(a, b)


def ada_gather(c8, taps8, w_ada, b_cols):
    cols = w_ada.shape[1]

    def body(c_ref, t_ref, w_ref, b_ref, call_ref, tall_ref, mod_ref, part_ref, send_sems, recv_sems):
        x, y, c = _coords()
        me = 4 * x + 2 * y + c
        peers = [(1 - x if k & 4 else x, 1 - y if k & 2 else y, 1 - c if k & 1 else c) for k in range(1, 8)]

        def rcopy(n, src, dst, peer):
            return pltpu.make_async_remote_copy(src_ref=src, dst_ref=dst, send_sem=send_sems.at[n],
                                                recv_sem=recv_sems.at[n], device_id=peer, device_id_type=MESH)

        call_ref[me] = c_ref[...]
        tall_ref[me] = t_ref[...]
        first = []
        for k, peer in enumerate(peers):
            first += [rcopy(k, c_ref, call_ref.at[me], peer), rcopy(7 + k, t_ref, tall_ref.at[me], peer)]
        for cp in first:
            cp.start()
        for cp in first:
            cp.wait()
        rows = call_ref[...].reshape(64, D).astype(BF16)
        part_ref[...] = _dot(rows, w_ref[...].astype(BF16)) + b_ref[...]
        mod_ref[me] = part_ref[pl.ds(pl.multiple_of(8 * me, 8), 8), :]
        second = []
        for k, (px, py, pc) in enumerate(peers):
            theirs = part_ref.at[pl.ds(pl.multiple_of(8 * (4 * px + 2 * py + pc), 8), 8)]
            second.append(rcopy(14 + k, theirs, mod_ref.at[me], (px, py, pc)))
        for cp in second:
            cp.start()
        for cp in second:
            cp.wait()

    vm = pl.BlockSpec(memory_space=pltpu.VMEM)
    return pl.pallas_call(
        body, name="ada_gather",
        out_shape=[jax.ShapeDtypeStruct((8, 8, D), F32), jax.ShapeDtypeStruct((8, 8, 128), F32),
                   jax.ShapeDtypeStruct((8, 8, cols), F32)],
        in_specs=[vm] * 4, out_specs=[vm] * 3,
        scratch_shapes=[pltpu.VMEM((64, cols), F32), pltpu.SemaphoreType.DMA((21,)),
                        pltpu.SemaphoreType.DMA((21,))],
        compiler_params=_params(),
    )(c8, taps8, w_ada, b_cols)


def ada_bwd(c_all, dmod_cols):
    def body(c_ref, d_ref, o_ref):
        o_ref[...] = _dot_tn(c_ref[...].astype(BF16), d_ref[...].astype(BF16))

    return pl.pallas_call(
        body, name="ada_bwd",
        out_shape=jax.ShapeDtypeStruct((c_all.shape[1], dmod_cols.shape[1]), F32),
        compiler_params=_params(),
    )(c_all, dmod_cols)


def slot_sum(g):
    def body(g_ref, o_ref):
        acc = g_ref[0]
        for s in range(1, 8):
            acc = acc + g_ref[s]
        o_ref[...] = acc

    return pl.pallas_call(
        body, name="slot_sum",
        out_shape=jax.ShapeDtypeStruct(g.shape[1:], F32),
    )(g)


def prenorm_fwd(x2, scale, shift, g_pre, seq):
    t = x2.shape[0]
    tm = min(512, seq)
    tpb = seq // tm

    def body(x_ref, sc_ref, sh_ref, g_ref, h_ref):
        xv = x_ref[...]
        r = lax.rsqrt(jnp.mean(xv * xv, axis=-1, keepdims=True) + EPS)
        hv = (xv * r * g_ref[...]) * (1.0 + sc_ref[...]) + sh_ref[...]
        h_ref[...] = hv.astype(BF16)

    per_batch = pl.BlockSpec((None, 1, D), lambda i: (i // tpb, 0, 0))
    return pl.pallas_call(
        body, name="prenorm_fwd", grid=(t // tm,),
        in_specs=[pl.BlockSpec((tm, D), lambda i: (i, 0)), per_batch, per_batch,
                  pl.BlockSpec((1, D), lambda i: (0, 0))],
        out_specs=pl.BlockSpec((tm, D), lambda i: (i, 0)),
        out_shape=jax.ShapeDtypeStruct((t, D), BF16),
        compiler_params=_params(("parallel",)),
    )(x2, scale, shift, g_pre)


def prenorm_bwd(dh, x2, dout, scale, g_pre, seq, token, b, gx_prev):
    t = x2.shape[0]
    tm = min(512, seq)
    tpb = seq // tm
    if gx_prev is None:
        gx_prev = lax.empty((t, D), F32)

    def body(b_ref, dh_ref, x_ref, do_ref, sc_ref, g_ref, tok_ref, gxp_ref, gx_ref, dsh_ref, dsc_ref, dg_ref):
        i = pl.program_id(0)
        xv = x_ref[...]
        dhv = dh_ref[...]
        g = g_ref[...]
        r = lax.rsqrt(jnp.mean(xv * xv, axis=-1, keepdims=True) + EPS)
        nrm = xv * r
        dxn = dhv * (1.0 + sc_ref[...])
        dn = dxn * g
        dx = r * (dn - nrm * jnp.mean(dn * nrm, axis=-1, keepdims=True))
        gx_ref[...] = dx + do_ref[...]

        @pl.when(i == 0)
        def _():
            dsh_ref[...] = jnp.zeros_like(dsh_ref)
            dsc_ref[...] = jnp.zeros_like(dsc_ref)
            dg_ref[...] = jnp.zeros_like(dg_ref)

        dsh_ref[...] += jnp.sum(dhv, axis=0, keepdims=True)
        dsc_ref[...] += jnp.sum(dhv * (nrm * g), axis=0, keepdims=True)
        dg_ref[...] += jnp.sum(dxn * nrm, axis=0, keepdims=True)

    row = pl.BlockSpec((tm, D), lambda i, s: (i, 0))
    grow = pl.BlockSpec((tm, D), lambda i, s: (s[0] * tpb + i, 0))
    per_batch = pl.BlockSpec((None, 1, D), lambda i, s: (s[0], 0, 0))
    vec = pl.BlockSpec((1, D), lambda i, s: (0, 0))
    return pl.pallas_call(
        body, name="prenorm_bwd",
        grid_spec=pltpu.PrefetchScalarGridSpec(
            num_scalar_prefetch=1, grid=(tpb,),
            in_specs=[row, grow, grow, per_batch, vec, pl.BlockSpec((8, 128), lambda i, s: (0, 0)),
                      pl.BlockSpec(memory_space=pl.ANY)],
            out_specs=[grow, vec, vec, vec]),
        out_shape=[jax.ShapeDtypeStruct((t, D), F32), jax.ShapeDtypeStruct((1, D), F32),
                   jax.ShapeDtypeStruct((1, D), F32), jax.ShapeDtypeStruct((1, D), F32)],
        input_output_aliases={7: 0},
        compiler_params=_params(("arbitrary",)),
    )(jnp.full((1,), b, jnp.int32), dh, x2, dout, scale, g_pre, token, gx_prev)


CONV_TC = 128


def _shift_down(u, k, rows):
    idx = lax.broadcasted_iota(jnp.int32, u.shape, 0)
    return jnp.where(idx >= k, pltpu.roll(u, k, 0), 0.0)


def _shift_up(u, k, rows):
    idx = lax.broadcasted_iota(jnp.int32, u.shape, 0)
    return jnp.where(idx < rows - k, pltpu.roll(u, rows - k, 0), 0.0)


def conv_fwd(proj, conv_w, seq):
    t = proj.shape[1]
    nb = t // seq

    def body(p_ref, w_ref, y_ref):
        av = p_ref[0].astype(F32)
        ab = p_ref[1].astype(F32)
        ac = p_ref[2].astype(F32)
        az = p_ref[3].astype(F32)
        w = w_ref[...]
        u = ac * av
        y1 = _shift_down(u, 2, seq) * w[0:1] + _shift_down(u, 1, seq) * w[1:2] + u * w[2:3]
        y_ref[...] = (ab * y1 * (az * _sig(az))).astype(BF16)

    return pl.pallas_call(
        body, name="conv_fwd", grid=(nb, D // CONV_TC),
        in_specs=[pl.BlockSpec((4, seq, CONV_TC), lambda b, ci: (1, b, ci)),
                  pl.BlockSpec((8, CONV_TC), lambda b, ci: (0, ci))],
        out_specs=pl.BlockSpec((seq, CONV_TC), lambda b, ci: (b, ci)),
        out_shape=jax.ShapeDtypeStruct((t, D), BF16),
        compiler_params=_params(("parallel", "parallel")),
    )(proj, conv_w)


def conv_bwd(dproj, proj, dy, conv_w, seq):
    t = proj.shape[1]
    nb = t // seq

    def body(dp_in_ref, p_ref, dy_ref, w_ref, dp_ref, dw_ref):
        b = pl.program_id(1)
        av = p_ref[0].astype(F32)
        ab = p_ref[1].astype(F32)
        ac = p_ref[2].astype(F32)
        az = p_ref[3].astype(F32)
        dyv = dy_ref[...].astype(F32)
        w = w_ref[...]
        u = ac * av
        u1 = _shift_down(u, 1, seq)
        u2 = _shift_down(u, 2, seq)
        y1 = u2 * w[0:1] + u1 * w[1:2] + u * w[2:3]
        sz = _sig(az)
        silu = az * sz
        dy1 = dyv * ab * silu
        du = dy1 * w[2:3] + _shift_up(dy1, 1, seq) * w[1:2] + _shift_up(dy1, 2, seq) * w[0:1]
        dp_ref[0] = (du * ac).astype(BF16)
        dp_ref[1] = (dyv * y1 * silu).astype(BF16)
        dp_ref[2] = (du * av).astype(BF16)
        dp_ref[3] = (dyv * ab * y1 * (sz * (1.0 + az * (1.0 - sz)))).astype(BF16)

        @pl.when(b == 0)
        def _():
            dw_ref[...] = jnp.zeros_like(dw_ref)

        dw_ref[0:1, :] += jnp.sum(dy1 * u2, axis=0, keepdims=True)
        dw_ref[1:2, :] += jnp.sum(dy1 * u1, axis=0, keepdims=True)
        dw_ref[2:3, :] += jnp.sum(dy1 * u, axis=0, keepdims=True)

    return pl.pallas_call(
        body, name="conv_bwd", grid=(D // CONV_TC, nb),
        in_specs=[pl.BlockSpec(memory_space=pl.ANY),
                  pl.BlockSpec((4, seq, CONV_TC), lambda ci, b: (1, b, ci)),
                  pl.BlockSpec((seq, CONV_TC), lambda ci, b: (b, ci)),
                  pl.BlockSpec((8, CONV_TC), lambda ci, b: (0, ci))],
        out_specs=[pl.BlockSpec((4, seq, CONV_TC), lambda ci, b: (1, b, ci)),
                   pl.BlockSpec((8, CONV_TC), lambda ci, b: (0, ci))],
        out_shape=[jax.ShapeDtypeStruct(dproj.shape, BF16),
                   jax.ShapeDtypeStruct((8, D), F32)],
        input_output_aliases={0: 0},
        compiler_params=_params(("parallel", "arbitrary")),
    )(dproj, proj, dy, conv_w)


def _rope_tables(pos_ref, invf_ref, ma_ref, mb_ref, sign):
    ang = pos_ref[...].astype(F32) * invf_ref[...]
    cs = jnp.cos(ang)
    sn = jnp.sin(ang) * sign
    return cs, sn * ma_ref[...], sn * mb_ref[...]


def _rotate(v, cs, sa, sb):
    return v * cs + pltpu.roll(v, 128 - HALF, 1) * sa + pltpu.roll(v, HALF, 1) * sb


MLA_TM = 512


def mla_prep_fwd(proj, pos, g_q, g_kv, wuq, wukv, tabs):
    t = proj.shape[1]
    tm = min(MLA_TM, t)

    def body(lat_ref, pos_ref, gq_ref, gkv_ref, wuq_ref, wukv_ref, invf_ref, ma_ref, mb_ref,
             q_ref, k_ref, kv_ref, qn_ref, kvn_ref):
        lat = lat_ref[...].astype(F32)
        ql = lat[:, :QL]
        kl = lat[:, QL:QL + KVL]
        kr = lat[:, QL + KVL:QL + KVL + 128]
        qn = (ql * lax.rsqrt(jnp.mean(ql * ql, axis=-1, keepdims=True) + EPS) * gq_ref[...]).astype(BF16)
        kvn = (kl * lax.rsqrt(jnp.mean(kl * kl, axis=-1, keepdims=True) + EPS) * gkv_ref[...]).astype(BF16)
        qn_ref[...] = qn
        kvn_ref[...] = kvn
        cs, sa, sb = _rope_tables(pos_ref, invf_ref, ma_ref, mb_ref, 1.0)
        q = _dot_nt(qn, wuq_ref[...]) * (SM_SCALE * LOG2E)
        kv = _dot_nt(kvn, wukv_ref[...]).astype(BF16)
        kv_ref[...] = kv
        kpe = _rotate(kr, cs, sa, sb).astype(BF16)
        for hh in range(H):
            lo, mid, hi = hh * DQK, hh * DQK + 128, (hh + 1) * DQK
            q_ref[:, lo:mid] = q[:, lo:mid].astype(BF16)
            q_ref[:, mid:hi] = _rotate(q[:, mid:hi], cs, sa, sb).astype(BF16)
            k_ref[:, lo:mid] = kv[:, lo:mid]
            k_ref[:, mid:hi] = kpe

    row = lambda w: pl.BlockSpec((tm, w), lambda i: (i, 0))
    const = lambda a: pl.BlockSpec(a.shape, lambda i: (0,) * a.ndim)
    return pl.pallas_call(
        body, name="mla_prep_fwd", grid=(t // tm,),
        in_specs=[pl.BlockSpec((None, tm, D), lambda i: (SEG_LAT, i, 0)), row(1),
                  const(g_q), const(g_kv), const(wuq), const(wukv)] + [const(a) for a in tabs],
        out_specs=[row(H * DQK), row(H * DQK), row(H * DQK), row(QL), row(KVL)],
        out_shape=[jax.ShapeDtypeStruct((t, H * DQK), BF16)] * 3
        + [jax.ShapeDtypeStruct((t, QL), BF16), jax.ShapeDtypeStruct((t, KVL), BF16)],
        compiler_params=_params(("parallel",)),
    )(proj, pos, g_q, g_kv, wuq, wukv, *tabs)


def mla_prep_bwd(dproj, proj, dq_rot, dk, dv, pos, g_q, g_kv, wuq, wukv, tabs):
    t = proj.shape[1]
    tm = min(MLA_TM, t)

    def body(dp_in_ref, lat_ref, dqr_ref, dk_ref, dv_ref, pos_ref, gq_ref, gkv_ref, wuq_ref, wukv_ref,
             invf_ref, ma_ref, mb_ref, dp_ref, dq_ref, dkv_ref, dgq_ref, dgkv_ref):
        i = pl.program_id(0)
        lat = lat_ref[...].astype(F32)
        ql = lat[:, :QL]
        kl = lat[:, QL:QL + KVL]
        rq = lax.rsqrt(jnp.mean(ql * ql, axis=-1, keepdims=True) + EPS)
        rk = lax.rsqrt(jnp.mean(kl * kl, axis=-1, keepdims=True) + EPS)
        nq = ql * rq
        nk = kl * rk
        cs, sa, sb = _rope_tables(pos_ref, invf_ref, ma_ref, mb_ref, -1.0)
        dkpe = jnp.zeros((tm, 128), F32)
        for hh in range(H):
            lo, mid, hi = hh * DQK, hh * DQK + 128, (hh + 1) * DQK
            dq_ref[:, lo:mid] = (dqr_ref[:, lo:mid] * SM_SCALE).astype(BF16)
            dq_ref[:, mid:hi] = _rotate(dqr_ref[:, mid:hi] * SM_SCALE, cs, sa, sb).astype(BF16)
            dkv_ref[:, lo:mid] = dk_ref[:, lo:mid]
            dkv_ref[:, mid:hi] = dv_ref[:, hh * DV:(hh + 1) * DV]
            dkpe = dkpe + dk_ref[:, mid:hi].astype(F32)
        lane = lax.broadcasted_iota(jnp.int32, (tm, 128), 1)
        dkr = jnp.where(lane < ROPE, _rotate(dkpe, cs, sa, sb), 0.0)
        dqn = _dot(dq_ref[...], wuq_ref[...])
        dkvn = _dot(dkv_ref[...], wukv_ref[...])
        gq = gq_ref[...]
        gkv = gkv_ref[...]
        dnq = dqn * gq
        dnk = dkvn * gkv
        dql = rq * (dnq - nq * jnp.mean(dnq * nq, axis=-1, keepdims=True))
        dkl = rk * (dnk - nk * jnp.mean(dnk * nk, axis=-1, keepdims=True))
        dp_ref[:, :QL] = dql.astype(BF16)
        dp_ref[:, QL:QL + KVL] = dkl.astype(BF16)
        dp_ref[:, QL + KVL:QL + KVL + 128] = dkr.astype(BF16)
        dp_ref[:, QL + KVL + 128:] = jnp.zeros((tm, D - QL - KVL - 128), BF16)

        @pl.when(i == 0)
        def _():
            dgq_ref[...] = jnp.zeros_like(dgq_ref)
            dgkv_ref[...] = jnp.zeros_like(dgkv_ref)

        dgq_ref[...] += jnp.sum(dqn * nq, axis=0, keepdims=True)
        dgkv_ref[...] += jnp.sum(dkvn * nk, axis=0, keepdims=True)

    row = lambda w: pl.BlockSpec((tm, w), lambda i: (i, 0))
    const = lambda a: pl.BlockSpec(a.shape, lambda i: (0,) * a.ndim)
    seg = pl.BlockSpec((None, tm, D), lambda i: (SEG_LAT, i, 0))
    return pl.pallas_call(
        body, name="mla_prep_bwd", grid=(t // tm,),
        in_specs=[pl.BlockSpec(memory_space=pl.ANY), seg, row(H * DQK), row(H * DQK), row(H * DV), row(1),
                  const(g_q), const(g_kv), const(wuq), const(wukv)] + [const(a) for a in tabs],
        out_specs=[seg, row(H * DQK), row(H * DQK),
                   pl.BlockSpec((1, QL), lambda i: (0, 0)), pl.BlockSpec((1, KVL), lambda i: (0, 0))],
        out_shape=[jax.ShapeDtypeStruct(dproj.shape, BF16),
                   jax.ShapeDtypeStruct((t, H * DQK), BF16), jax.ShapeDtypeStruct((t, H * DQK), BF16),
                   jax.ShapeDtypeStruct((1, QL), F32), jax.ShapeDtypeStruct((1, KVL), F32)],
        input_output_aliases={0: 0},
        compiler_params=_params(("arbitrary",)),
    )(dproj, proj, dq_rot, dk, dv, pos, g_q, g_kv, wuq, wukv, *tabs)


def _causal_mask(s, shift):
    row = lax.broadcasted_iota(jnp.int32, s.shape, 0)
    col = lax.broadcasted_iota(jnp.int32, s.shape, 1)
    return jnp.where(col <= row + shift, s, -1e30)


def flash_fwd(q, k, kv, nb, seq):
    t = q.shape[0]
    tq = min(FLASH_TQ, seq // 2)
    nq = seq // tq
    assert nq % 2 == 0, "blocks are processed in pairs"

    def update(state, s, vblk):
        m, l, acc = state
        m_new = jnp.maximum(m, jnp.max(s, axis=1, keepdims=True))
        p = jnp.exp2(s - m_new)
        alpha = jnp.exp2(m - m_new)
        return (m_new, alpha * l + jnp.sum(p, axis=1, keepdims=True),
                alpha * acc + _dot(p.astype(BF16), vblk))

    def finish(state, rows, o_ref, lse_ref):
        m, l, acc = state
        o_ref[rows, :] = (acc / l).astype(BF16)
        lse_ref[rows, :] = jnp.broadcast_to(m + jnp.log(l) * LOG2E, (m.shape[0], DV))

    def body(q_ref, k_ref, v_ref, o_ref, lse_ref):
        for qp in range(0, nq, 2):
            rows = 2 * tq
            q0 = qp * tq
            qv = q_ref[q0:q0 + rows, :]
            state = (jnp.full((rows, 1), -1e30, F32), jnp.zeros((rows, 1), F32), jnp.zeros((rows, DV), F32))
            for j in range(qp + 1):
                ks = slice(j * tq, (j + 1) * tq)
                s = _dot_nt(qv, k_ref[ks, :])
                if j == qp:
                    s = _causal_mask(s, 0)
                state = update(state, s, v_ref[ks, :])
            finish(tuple(a[:tq] for a in state), slice(q0, q0 + tq), o_ref, lse_ref)
            ks = slice(q0 + tq, q0 + 2 * tq)
            low = tuple(a[tq:] for a in state)
            low = update(low, _causal_mask(_dot_nt(qv[tq:], k_ref[ks, :]), 0), v_ref[ks, :])
            finish(low, slice(q0 + tq, q0 + 2 * tq), o_ref, lse_ref)

    out_blk = pl.BlockSpec((seq, DV), lambda b, h: (b, h))
    return pl.pallas_call(
        body, name="flash_fwd", grid=(nb, H),
        in_specs=[pl.BlockSpec((seq, DQK), lambda b, h: (b, h)),
                  pl.BlockSpec((seq, DQK), lambda b, h: (b, h)),
                  pl.BlockSpec((seq, DV), lambda b, h: (b, 2 * h + 1))],
        out_specs=[out_blk, out_blk],
        out_shape=[jax.ShapeDtypeStruct((t, H * DV), BF16), jax.ShapeDtypeStruct((t, H * DV), F32)],
        compiler_params=_params(("parallel", "parallel")),
    )(q, k, kv)


def flash_bwd(q, k, kv, o, do, lse, nb, seq, token):
    t = q.shape[0]
    tq = min(FLASH_TQ, seq)
    nq = seq // tq

    def body(q_ref, k_ref, v_ref, o_ref, do_ref, lse_ref, tok_ref, dq_ref, dk_ref, dv_ref):
        delta, lse = [], []
        for qi in range(nq):
            qs = slice(qi * tq, (qi + 1) * tq)
            dl = jnp.sum(do_ref[qs, :].astype(F32) * o_ref[qs, :].astype(F32), axis=1, keepdims=True)
            delta.append(jnp.broadcast_to(dl, (tq, DV)).T[:1, :])
            lse.append(lse_ref[qs, :].T[:1, :])
        for ki in range(nq):
            ks = slice(ki * tq, (ki + 1) * tq)
            kb = k_ref[ks, :]
            vb = v_ref[ks, :]
            dk = jnp.zeros((tq, DQK), F32)
            dv = jnp.zeros((tq, DV), F32)
            for qi in range(ki, nq):
                qs = slice(qi * tq, (qi + 1) * tq)
                qv = q_ref[qs, :]
                dov = do_ref[qs, :]
                st = _dot_nt(kb, qv)
                if qi == ki:
                    row = lax.broadcasted_iota(jnp.int32, st.shape, 0)
                    col = lax.broadcasted_iota(jnp.int32, st.shape, 1)
                    st = jnp.where(row <= col, st, -1e30)
                pt = jnp.exp2(st - lse[qi])
                dpt = _dot_nt(vb, dov)
                dzt = (pt * (dpt - delta[qi])).astype(BF16)
                dv = dv + _dot(pt.astype(BF16), dov)
                dk = dk + _dot(dzt, qv)
                dqb = _dot_tn(dzt, kb)
                if ki == 0:
                    dq_ref[qs, :] = dqb
                else:
                    dq_ref[qs, :] += dqb
            dk_ref[ks, :] = (dk * LN2).astype(BF16)
            dv_ref[ks, :] = dv.astype(BF16)

    full = lambda w, col: pl.BlockSpec((seq, w), col)
    same = lambda b, h: (b, h)
    return pl.pallas_call(
        body, name="flash_bwd", grid=(nb, H),
        in_specs=[full(DQK, same), full(DQK, same), full(DV, lambda b, h: (b, 2 * h + 1)),
                  full(DV, same), full(DV, same), full(DV, same),
                  pl.BlockSpec((8, 128), lambda b, h: (0, 0))],
        out_specs=[full(DQK, same), full(DQK, same), full(DV, same)],
        out_shape=[jax.ShapeDtypeStruct((t, H * DQK), F32), jax.ShapeDtypeStruct((t, H * DQK), BF16),
                   jax.ShapeDtypeStruct((t, H * DV), BF16)],
        compiler_params=_params(("parallel", "parallel")),
    )(q, k, kv, o, do, lse, token)


TAIL_TM = 512


def tail_fwd(y, attn, proj, x2, tgt, gate, g_post, wco, wmo, wout, seq):
    t = y.shape[0]
    nb = t // seq
    tm = min(TAIL_TM, seq)
    tpb = seq // tm

    def body(y_ref, at_ref, p_ref, x_ref, t_ref, gate_ref, gp_ref, wco_ref, wmo_ref, wout_ref,
             o_ref, ya_ref, yb_ref, m_ref, do2_ref, dout_ref, dgate_ref, dgp_ref, loss_ref):
        i = pl.program_id(0)
        bz = p_ref[0].astype(F32)
        ga = p_ref[1].astype(F32)
        gb = p_ref[2].astype(F32)
        ov = (at_ref[...].astype(F32) * (bz * _sig(bz))).astype(BF16)
        o_ref[...] = ov
        ya = _dot(y_ref[...], wco_ref[...])
        yb = _dot(ov, wmo_ref[...])
        ya_ref[...] = ya.astype(BF16)
        yb_ref[...] = yb.astype(BF16)
        mv = (_sig(ga) * ya + _sig(gb) * yb).astype(BF16)
        m_ref[...] = mv
        o2 = _dot(mv, wout_ref[...])
        r = lax.rsqrt(jnp.mean(o2 * o2, axis=-1, keepdims=True) + EPS)
        nrm = o2 * r
        gp = gp_ref[...]
        gate_v = gate_ref[...]
        rn = nrm * gp
        err = x_ref[...] + gate_v * rn - t_ref[...]
        dout = err * (1.0 / D)
        dout_ref[...] = dout
        dn = dout * gate_v * gp
        do2_ref[...] = (r * (dn - nrm * jnp.mean(dn * nrm, axis=-1, keepdims=True))).astype(BF16)

        @pl.when(i % tpb == 0)
        def _():
            dgate_ref[...] = jnp.zeros_like(dgate_ref)

        @pl.when(i == 0)
        def _():
            dgp_ref[...] = jnp.zeros_like(dgp_ref)
            loss_ref[...] = jnp.zeros_like(loss_ref)

        dgate_ref[...] += jnp.sum(dout * rn, axis=0, keepdims=True)
        dgp_ref[...] += jnp.sum(dout * gate_v * nrm, axis=0, keepdims=True)
        loss_ref[...] += 0.5 * jnp.sum(jnp.mean(err * err, axis=-1, keepdims=True), axis=0, keepdims=True)

    row = pl.BlockSpec((tm, D), lambda i: (i, 0))
    per_batch = pl.BlockSpec((None, 1, D), lambda i: (i // tpb, 0, 0))
    vec = pl.BlockSpec((1, D), lambda i: (0, 0))
    wgt = pl.BlockSpec((D, D), lambda i: (0, 0))
    act = jax.ShapeDtypeStruct((t, D), BF16)
    return pl.pallas_call(
        body, name="tail_fwd", grid=(t // tm,),
        in_specs=[row, row, pl.BlockSpec((3, tm, D), lambda i: (0, i, 0)), row, row, per_batch, vec,
                  wgt, wgt, wgt],
        out_specs=[row, row, row, row, row, row, per_batch, vec, pl.BlockSpec((1, 1), lambda i: (0, 0))],
        out_shape=[act, act, act, act, act, jax.ShapeDtypeStruct((t, D), F32),
                   jax.ShapeDtypeStruct((nb, 1, D), F32), jax.ShapeDtypeStruct((1, D), F32),
                   jax.ShapeDtypeStruct((1, 1), F32)],
        compiler_params=_params(("arbitrary",)),
    )(y, attn, proj, x2, tgt, gate, g_post, wco, wmo, wout)


def tail_bwd(do2, proj, ya, yb, attn, wout, wmo, wco):
    t = do2.shape[0]
    tm = min(TAIL_TM, t)

    def body(do2_ref, p_ref, ya_ref, yb_ref, at_ref, wout_ref, wmo_ref, wco_ref,
             dp_ref, dya_ref, dyb_ref, dat_ref, dy_ref):
        bz = p_ref[0].astype(F32)
        ga = p_ref[1].astype(F32)
        gb = p_ref[2].astype(F32)
        dm = _dot_nt(do2_ref[...], wout_ref[...])
        sa = _sig(ga)
        sb = _sig(gb)
        dya = (dm * sa).astype(BF16)
        dyb = (dm * sb).astype(BF16)
        dya_ref[...] = dya
        dyb_ref[...] = dyb
        dp_ref[1] = (dm * ya_ref[...].astype(F32) * (sa * (1.0 - sa))).astype(BF16)
        dp_ref[2] = (dm * yb_ref[...].astype(F32) * (sb * (1.0 - sb))).astype(BF16)
        dov = _dot_nt(dyb, wmo_ref[...])
        sz = _sig(bz)
        dat_ref[...] = (dov * (bz * sz)).astype(BF16)
        dp_ref[0] = (dov * at_ref[...].astype(F32) * (sz * (1.0 + bz * (1.0 - sz)))).astype(BF16)
        dy_ref[...] = _dot_nt(dya, wco_ref[...]).astype(BF16)

    row = pl.BlockSpec((tm, D), lambda i: (i, 0))
    seg3 = pl.BlockSpec((3, tm, D), lambda i: (0, i, 0))
    wgt = pl.BlockSpec((D, D), lambda i: (0, 0))
    act = jax.ShapeDtypeStruct((t, D), BF16)
    return pl.pallas_call(
        body, name="tail_bwd", grid=(t // tm,),
        in_specs=[row, seg3, row, row, row, wgt, wgt, wgt],
        out_specs=[seg3, row, row, row, row],
        out_shape=[jax.ShapeDtypeStruct((NSEG, t, D), BF16), act, act, act, act],
        compiler_params=_params(("parallel",)),
    )(do2, proj, ya, yb, attn, wout, wmo, wco)


def _adam_update(w, m, v, grad):
    mn = ADAM_B1 * m + (1.0 - ADAM_B1) * grad
    vn = ADAM_B2 * v + (1.0 - ADAM_B2) * (grad * grad)
    m_hat = mn / (1.0 - ADAM_B1 ** ADAM_STEP)
    v_hat = vn / (1.0 - ADAM_B2 ** ADAM_STEP)
    return -ADAM_LR * (m_hat / (jnp.sqrt(v_hat) + ADAM_EPS) + ADAM_WD * w), mn, vn


def adamw(w, m, v, g, name, token):
    rows, cols = w.shape
    tr = rows
    for cand in (256, 128, 64, 32, 16, 8):
        if rows % cand == 0 and rows > cand:
            tr = cand
            break

    def body(w_ref, m_ref, v_ref, g_ref, tok_ref, d_ref, mo_ref, vo_ref):
        d_ref[...], mo_ref[...], vo_ref[...] = _adam_update(w_ref[...], m_ref[...], v_ref[...], g_ref[...])

    blk = pl.BlockSpec((tr, cols), lambda i: (i, 0))
    return pl.pallas_call(
        body, name=name, grid=(rows // tr,),
        in_specs=[blk] * 4 + [pl.BlockSpec((8, 128), lambda i: (0, 0))], out_specs=[blk] * 3,
        out_shape=[jax.ShapeDtypeStruct((rows, cols), F32)] * 3,
        compiler_params=_params(("parallel",)),
    )(w, m, v, g, token)


def adamw_scattered(w, m, v, own, land, me, tr, name, transpose=False):
    slot_rows = land.shape[1]
    cols = land.shape[2]
    rows = slot_rows if transpose else w.shape[0]
    per_slot = slot_rows // tr

    def body(me_ref, w_ref, m_ref, v_ref, own_ref, land_ref, go_ref, d_ref, mo_ref, vo_ref):
        grad = own_ref[...].astype(F32)
        for s in range(8):
            grad = grad + land_ref[s].astype(F32)
        if transpose:
            grad = grad.T
        go_ref[...] = grad
        d_ref[...], mo_ref[...], vo_ref[...] = _adam_update(w_ref[...], m_ref[...], v_ref[...], grad)

    wblk = pl.BlockSpec(w.shape if transpose else (tr, w.shape[1]), lambda i, s: (i, 0))
    return pl.pallas_call(
        body, name=name,
        grid_spec=pltpu.PrefetchScalarGridSpec(
            num_scalar_prefetch=1, grid=(rows // tr,),
            in_specs=[wblk, wblk, wblk,
                      pl.BlockSpec((tr, cols), lambda i, s: (s[0] * per_slot + i, 0)),
                      pl.BlockSpec((8, tr, cols), lambda i, s: (0, i, 0))],
            out_specs=[wblk] * 4),
        out_shape=[jax.ShapeDtypeStruct(w.shape, F32)] * 4,
        compiler_params=_params(),
    )(me, w, m, v, own, land)


def adamw_win(wt, mt, vt, ka, ra, kb, rb):
    rows = wt.shape[0]
    tc = 256
    nh = (D // 2) // tc

    def body(w_ref, m_ref, v_ref, ka_ref, ra_ref, kb_ref, rb_ref, go_ref, d_ref, mo_ref, vo_ref):
        first = pl.program_id(0) < nh
        grad = jnp.where(first, ka_ref[...] + ra_ref[...].astype(F32), kb_ref[...] + rb_ref[...].astype(F32))
        go_ref[...] = grad
        d_ref[...], mo_ref[...], vo_ref[...] = _adam_update(w_ref[...], m_ref[...], v_ref[...], grad)

    blk = pl.BlockSpec((rows, tc), lambda j: (0, j))
    lo = pl.BlockSpec((rows, tc), lambda j: (0, jnp.minimum(j, nh - 1)))
    hi = pl.BlockSpec((rows, tc), lambda j: (0, jnp.maximum(j - nh, 0)))
    return pl.pallas_call(
        body, name="adamw_w_in", grid=(D // tc,),
        in_specs=[blk, blk, blk, lo, lo, hi, hi], out_specs=[blk] * 4,
        out_shape=[jax.ShapeDtypeStruct((rows, D), F32)] * 4,
        compiler_params=_params(("parallel",)),
    )(wt, mt, vt, ka, ra, kb, rb)


_ORD_A = ("x", "y", "c")
_ORD_B = ("y", "x", "c")


def _rows128(a, rows):
    flat = a.reshape(-1)
    return jnp.pad(flat, (0, rows * 128 - flat.shape[0])).reshape(rows, 128)


def kernel(x, c, positions, w_ada, b_ada, g_pre, w_in, conv_w, w_conv_out, g_q, w_uq, g_kv, w_ukv, w_mla_out, w_out, g_post, loss_target, m_w_ada, m_b_ada, m_g_pre, m_w_in, m_conv_w, m_w_conv_out, m_g_q, m_w_uq, m_g_kv, m_w_ukv, m_w_mla_out, m_w_out, m_g_post, v_w_ada, v_b_ada, v_g_pre, v_w_in, v_conv_w, v_w_conv_out, v_g_q, v_w_uq, v_g_kv, v_w_ukv, v_w_mla_out, v_w_out, v_g_post):
    nb, seq, _ = x.shape
    t = nb * seq
    mx, my, mc = lax.axis_index("x"), lax.axis_index("y"), lax.axis_index("c")
    me = 4 * mx + 2 * my + mc
    co = {"x": mx, "y": my, "c": mc}

    x2 = x.reshape(t, D)
    tgt2 = loss_target.reshape(t, D)
    pos2 = positions.reshape(t, 1)

    ada_cols = w_ada.shape[2]
    b_cols = lax.dynamic_slice(b_ada, (0, me * ada_cols), (1, ada_cols))
    c_g, taps_g, mod_g = ada_gather(jnp.pad(c, ((0, 8 - nb), (0, 0))), _rows128(conv_w[0], 8), w_ada[0], b_cols)
    c_all = c_g[:, :nb].reshape(8 * nb, D)
    conv_full = taps_g[:, 0:3].transpose(1, 0, 2).reshape(3, D)
    conv_full8 = jnp.pad(conv_full, ((0, 5), (0, 0)))
    mod = mod_g[:, :nb].transpose(1, 0, 2).reshape(nb, 8 * ada_cols)
    shift = mod[:, 0:D].reshape(nb, 1, D)
    scale = mod[:, D:2 * D].reshape(nb, 1, D)
    gate = mod[:, 2 * D:3 * D].reshape(nb, 1, D)

    wt = w_in[0].T.astype(BF16)
    lo = lax.bitcast_convert_type(wt[:, :D // 2], jnp.uint16).astype(jnp.uint32)
    hi = lax.bitcast_convert_type(wt[:, D // 2:], jnp.uint16).astype(jnp.uint32)
    wt_bits = lax.bitcast_convert_type(lo | (hi << 16), F32)
    wt_bits, mod = lax.optimization_barrier((wt_bits, mod))
    shift = mod[:, 0:D].reshape(nb, 1, D)
    scale = mod[:, D:2 * D].reshape(nb, 1, D)
    gate = mod[:, 2 * D:3 * D].reshape(nb, 1, D)
    q4 = D // 4
    r3rd = wt_bits.shape[0] // 3
    plan = [(0, (k * r3rd, r3rd), (g * q4, q4), (_ORD_A, _ORD_B)[g]) for k in range(3) for g in range(2)]
    gw = allgather_big([wt_bits], plan, "gather_w_in")
    late = [w_conv_out[0].astype(BF16), w_mla_out[0].astype(BF16), w_out[0].astype(BF16),
            jnp.pad(w_uq[0].T.astype(BF16), ((0, DQK - 192), (0, 0))), w_ukv[0].T.astype(BF16)]
    gw0, late = lax.optimization_barrier((gw[0], late))
    late_state, late_token = gather_start(late, "gather_late_start")
    wt_bits_all = gw0.reshape(N_IN, D // 2)

    inv_freq = ROPE_THETA ** (-jnp.arange(0, ROPE, 2, dtype=F32) / ROPE)
    invf = jnp.concatenate([inv_freq, inv_freq, jnp.zeros((128 - ROPE,), F32)]).reshape(1, 128)
    lane = np.arange(128)
    tabs = (invf,
            jnp.asarray(np.where(lane < HALF, -1.0, 0.0).reshape(1, 128), F32),
            jnp.asarray(np.where((lane >= HALF) & (lane < ROPE), 1.0, 0.0).reshape(1, 128), F32))

    h = prenorm_fwd(x2, scale, shift, g_pre, seq)
    proj, wt_p = proj_matmul(h, wt_bits_all, late_token)
    y = conv_fwd(proj, conv_full8, seq)
    gl = gather_wait(late_state, y, "gather_late_wait")
    wco = gl[0].reshape(D, D)
    wmo = gl[1].reshape(D, D)
    wout = gl[2].reshape(D, D)
    wuq_p = gl[3].reshape(H * DQK, QL)
    wukv = gl[4].reshape(H * 256, KVL)
    q_rot, k_cat, kv, qn, kvn = mla_prep_fwd(proj, pos2, g_q, g_kv, wuq_p, wukv, tabs)
    attn, lse = flash_fwd(q_rot, k_cat, kv, nb, seq)
    o, ya, yb, m, do2, dout, dgate, dg_post, loss_part = tail_fwd(
        y, attn, proj, x2, tgt2, gate, g_post, wco, wmo, wout, seq)

    dproj, dya, dyb, dattn, dy = tail_bwd(do2, proj, ya, yb, attn, wout, wmo, wco)
    g_wout = grad_matmul(m, do2, "grad_w_square")
    g_wmo = grad_matmul(o, dyb, "grad_w_square")
    g_wco = grad_matmul(y, dya, "grad_w_square")
    sc1, sc1_tok = scatter_start([g_wco, g_wmo, g_wout], "scatter_out_grads_start")
    dproj, dconv = conv_bwd(dproj, proj, dy, conv_full8, seq)
    dq_rot, dk, dv = flash_bwd(q_rot, k_cat, kv, attn, dattn, lse, nb, seq, sc1_tok)
    dproj, dq, dkv, dg_q, dg_kv = mla_prep_bwd(dproj, proj, dq_rot, dk, dv, pos2, g_q, g_kv, wuq_p, wukv, tabs)
    g_wuq_t = grad_matmul(dq, qn, "grad_w_uq")
    g_wukv_t = grad_matmul(dkv, kvn, "grad_w_ukv")
    sc2, sc2_tok = scatter_start([g_wuq_t, g_wukv_t], "scatter_mla_grads_start")
    g_win_p = win_grad_matmul(h, dproj, sc2_tok)

    g_wt = g_win_p.reshape(2, 2, 2, N_IN // 8, D)
    ords = [("c", "y", "x"), ("c", "x", "y")]
    hc = D // 2
    win_shape = (2, 2, N_IN // 8, hc)
    pick_w = lambda col: (lambda ref, cc: ref.at[:, :, 1 - cc["c"], :, pl.ds(col * hc, hc)])
    which1 = [0, 0]
    picks1 = [pick_w(0), pick_w(1)]
    st1, tok1 = swap_start([g_wt], which1, ["c"] * 2, picks1, [win_shape] * 2, "rs_c_start")
    assert nb == 2
    dh0 = dh_matmul(dproj, wt_p, tok1, seq, 0)
    (g_wt,), r1 = swap_wait(st1, dh0, which1, ["c"] * 2, picks1, "rs_c_wait")
    sel_xyc = jnp.stack([mx, my, mc]).astype(jnp.int32)
    sel2 = [jnp.stack([co[o[2]]]).astype(jnp.int32) for o in ords]
    first = [rs_win_add_first(g_wt, r1[0], sel_xyc, 1, 0, "rs_add_first_0"),
             rs_win_add_first(g_wt, r1[1], sel_xyc, 0, 1, "rs_add_first_1")]
    keep1, send1 = zip(*first)
    all4 = [0, 1]
    none4 = [None] * 2
    axes2 = [o[1] for o in ords]
    st2, tok2 = swap_start(list(send1), all4, axes2, none4, [s.shape for s in send1], "rs_ici1_start")

    dh1 = dh_matmul(dproj, wt_p, tok2, seq, 1)
    gx0, dsh0, dsc0, dgp0 = prenorm_bwd(dh0, x2, dout, scale, g_pre, seq, tok2, 0, None)
    _, r2 = swap_wait(st2, (gx0, dh1), all4, axes2, none4, "rs_ici1_wait")
    keep2, send2 = zip(*[rs_add_second(keep1[a], r2[a], sel2[a], "rs_add_second") for a in range(2)])
    axes3 = [o[2] for o in ords]
    st3, tok3 = swap_start(list(send2), all4, axes3, none4, [s.shape for s in send2], "rs_ici2_start")
    grad_x2, dsh1, dsc1, dgp1 = prenorm_bwd(dh1, x2, dout, scale, g_pre, seq, tok3, 1, gx0)
    dshift = jnp.stack([dsh0, dsh1])
    dscale = jnp.stack([dsc0, dsc1])
    dg_pre = dgp0 + dgp1

    dmod = jnp.concatenate([dshift, dscale, dgate], axis=2).reshape(nb * 3 * D // 128, 128)
    small = jnp.concatenate([
        dmod, _rows128(dg_pre, 8), _rows128(dg_post, 8), _rows128(dg_q, 8), _rows128(dg_kv, 8),
        dconv[0:3].reshape(24, 128), _rows128(loss_part, 8)], axis=0)
    small_g = small_allgather(small, "gather_small_grads")
    sums = slot_sum(small_g)
    dmod_all = small_g[:, 0:48].reshape(8 * nb, 3 * D)
    g_bada = (sums[0:24] + sums[24:48]).reshape(1, 3 * D)
    g_gpre = sums[48:56].reshape(1, D)
    g_gpost = sums[56:64].reshape(1, D)
    g_gq = sums[64:67].reshape(1, QL)
    g_gkv = sums[72:74].reshape(1, KVL)
    g_conv_full = sums[80:104].reshape(3, D)
    loss = sums[104, 0]
    g_conv = lax.dynamic_slice(g_conv_full, (0, me * 128), (3, 128))
    dmod_cols = lax.dynamic_slice(dmod_all, (0, me * ada_cols), (8 * nb, ada_cols))
    g_wada = ada_bwd(c_all, dmod_cols)

    res = {}
    res["w_ada"] = [o_[None] for o_ in (g_wada, *adamw(w_ada[0], m_w_ada[0], v_w_ada[0], g_wada, "adamw_w_ada", tok3))]

    def pack(b_, gp_, gpo_, gq_, gkv_, cw_):
        return jnp.concatenate([_rows128(b_, 24), _rows128(gp_, 8), _rows128(gpo_, 8), _rows128(gq_, 8),
                                _rows128(gkv_, 8), _rows128(cw_, 8)], axis=0)

    sw = pack(b_ada, g_pre, g_post, g_q, g_kv, conv_w)
    sm = pack(m_b_ada, m_g_pre, m_g_post, m_g_q, m_g_kv, m_conv_w)
    sv = pack(v_b_ada, v_g_pre, v_g_post, v_g_q, v_g_kv, v_conv_w)
    sg = pack(g_bada, g_gpre, g_gpost, g_gq, g_gkv, g_conv)
    small_out = (sg, *adamw(sw, sm, sv, sg, "adamw_small", tok3))

    _, r3 = swap_wait(st3, small_out[1], all4, axes3, none4, "rs_ici2_wait")

    (g_wco, g_wmo, g_wout), (l_wco, l_wmo, l_wout) = scatter_wait(sc1, small_out[2], "scatter_out_grads_wait")
    (g_wuq_t, g_wukv_t), (l_wuq, l_wukv) = scatter_wait(sc2, small_out[3], "scatter_mla_grads_wait")

    res["w_in"] = [o_.T[None] for o_ in adamw_win(w_in[0].T, m_w_in[0].T, v_w_in[0].T,
                                                  keep2[0], r3[0], keep2[1], r3[1])]
    me1 = me.reshape(1).astype(jnp.int32)
    res["w_uq"] = [o_.T[None] for o_ in adamw_scattered(
        w_uq[0].T, m_w_uq[0].T, v_w_uq[0].T, g_wuq_t, l_wuq, me1, 64, "adamw_w_uq")]
    res["w_ukv"] = [o_[None] for o_ in adamw_scattered(
        w_ukv[0], m_w_ukv[0], v_w_ukv[0], g_wukv_t, l_wukv, me1, KVL, "adamw_w_ukv", transpose=True)]
    for nm, wv, mv, vv, gg, ll in (("w_conv_out", w_conv_out, m_w_conv_out, v_w_conv_out, g_wco, l_wco),
                                   ("w_mla_out", w_mla_out, m_w_mla_out, v_w_mla_out, g_wmo, l_wmo),
                                   ("w_out", w_out, m_w_out, v_w_out, g_wout, l_wout)):
        res[nm] = [o_[None] for o_ in adamw_scattered(wv[0], mv[0], vv[0], gg, ll, me1, 128, "adamw_square")]

    def unpack(a):
        return {"b_ada": a[0:24].reshape(1, 3 * D), "g_pre": a[24:32].reshape(1, D),
                "g_post": a[32:40].reshape(1, D), "g_q": a[40:43].reshape(1, QL),
                "g_kv": a[48:50].reshape(1, KVL), "conv_w": a[56:59].reshape(-1)[:3 * 128].reshape(1, 3, 128)}

    for nm in ("b_ada", "g_pre", "g_post", "g_q", "g_kv", "conv_w"):
        res[nm] = [unpack(a)[nm] for a in small_out]

    order = ["w_ada", "b_ada", "g_pre", "w_in", "conv_w", "w_conv_out", "g_q", "w_uq", "g_kv", "w_ukv",
             "w_mla_out", "w_out", "g_post"]
    out = [loss, grad_x2.reshape(nb, seq, D)]
    for k_ in range(4):
        out += [res[nm][k_] for nm in order]
    return tuple(out)
```

```python
import numpy as np
import jax
import jax.numpy as jnp
from jax import lax
from jax.experimental import pallas as pl
from jax.experimental.pallas import tpu as pltpu

F32 = jnp.float32
BF16 = jnp.bfloat16
MESH = pl.DeviceIdType.MESH

D = 1024
H = 8
QL = 384
KVL = 256
ROPE = 64
HALF = ROPE // 2
DQK = 256
DV = 128
NSEG = 8
NP = NSEG * D
EPS = 1e-6
ROPE_THETA = 10000.0
SM_SCALE = (128 + ROPE) ** -0.5
LOG2E = 1.4426950408889634
LN2 = 0.6931471805599453
FLASH_TQ = 512

SEG_BZ, SEG_GA, SEG_GB, SEG_LAT, SEG_V = 0, 1, 2, 3, 4

ADAM_LR = 0.001
ADAM_B1 = 0.9
ADAM_B2 = 0.999
ADAM_EPS = 1e-08
ADAM_WD = 0.01
ADAM_STEP = 10

VMEM_LIMIT = 56 * 1024 * 1024


def _params(sem=None, vmem=VMEM_LIMIT):
    kw = dict(vmem_limit_bytes=vmem)
    if sem is not None:
        kw["dimension_semantics"] = sem
    return pltpu.CompilerParams(**kw)


def _sig(v):
    return 0.5 * jnp.tanh(0.5 * v) + 0.5


def _dot(a, b):
    return jnp.dot(a, b, preferred_element_type=F32)


def _dot_nt(a, b):
    return lax.dot_general(a, b, (((1,), (1,)), ((), ())), preferred_element_type=F32)


def _dot_tn(a, b):
    return lax.dot_general(a, b, (((0,), (0,)), ((), ())), preferred_element_type=F32)


_AXIS_POS = {"x": 0, "y": 1, "c": 2}


def _coords():
    return lax.axis_index("x"), lax.axis_index("y"), lax.axis_index("c")


def _partner(axis):
    p = list(_coords())
    p[_AXIS_POS[axis]] = 1 - p[_AXIS_POS[axis]]
    return tuple(p)


def small_allgather(v, name):
    rows = v.shape[0]

    def body(v_ref, out_ref, send_sems, recv_sems):
        x, y, c = _coords()
        me = 4 * x + 2 * y + c
        out_ref[me] = v_ref[...]
        copies = []
        for k in range(1, 8):
            peer = (1 - x if k & 4 else x, 1 - y if k & 2 else y, 1 - c if k & 1 else c)
            cp = pltpu.make_async_remote_copy(
                src_ref=v_ref, dst_ref=out_ref.at[me],
                send_sem=send_sems.at[k - 1], recv_sem=recv_sems.at[k - 1],
                device_id=peer, device_id_type=MESH)
            cp.start()
            copies.append(cp)
        for cp in copies:
            cp.wait()

    return pl.pallas_call(
        body, name=name,
        out_shape=jax.ShapeDtypeStruct((8, rows, 128), F32),
        in_specs=[pl.BlockSpec(memory_space=pltpu.VMEM)],
        out_specs=pl.BlockSpec(memory_space=pltpu.VMEM),
        scratch_shapes=[pltpu.SemaphoreType.DMA((7,)), pltpu.SemaphoreType.DMA((7,))],
    )(v)


def _own_block_placed(s):
    x, y, c = _coords()
    return lax.dynamic_update_slice(lax.empty((2, 2, 2) + s.shape, s.dtype), s[None, None, None],
                                    (x, y, c) + (0,) * s.ndim)


def allgather_big(arrs, plan, name):
    n = len(arrs)
    m = len(plan)
    nst = len(plan[0][3])

    def body(*refs):
        ins, outs = refs[n:2 * n], refs[2 * n:3 * n]
        send_sems, recv_sems = refs[3 * n:]
        x, y, c = _coords()
        co = {"x": x, "y": y, "c": c}

        def window(ref, lead, rows, cols):
            win = tuple(slice(None) if w is None else pl.ds(w[0], w[1]) for w in (rows, cols))
            return ref.at[tuple(lead) + win]

        def held(e, free):
            i, rows, cols, _ = plan[e]
            lead = [slice(None) if ax in free else co[ax] for ax in ("x", "y", "c")]
            return window(outs[i], lead, rows, cols)

        def rcopy(e, stage, src, dst, axis):
            return pltpu.make_async_remote_copy(
                src_ref=src, dst_ref=dst,
                send_sem=send_sems.at[e, stage], recv_sem=recv_sems.at[e, stage],
                device_id=_partner(axis), device_id_type=MESH)

        stages = [[] for _ in range(nst)]
        for e, (i, rows, cols, order) in enumerate(plan):
            cp = rcopy(e, 0, window(ins[i], [], rows, cols), held(e, ()), order[0])
            cp.start()
            stages[0].append(cp)
        for s in range(1, nst):
            for e, (i, rows, cols, order) in enumerate(plan):
                stages[s - 1][e].wait_recv()
                blk = held(e, order[:s])
                cp = rcopy(e, s, blk, blk, order[s])
                cp.start()
                stages[s].append(cp)
        for e in range(m):
            stages[nst - 1][e].wait_recv()
        for e in range(m):
            for s in range(nst):
                stages[s][e].wait_send()

    any_spec = pl.BlockSpec(memory_space=pl.ANY)
    lands = [_own_block_placed(a) for a in arrs]
    return pl.pallas_call(
        body, name=name,
        out_shape=[jax.ShapeDtypeStruct(l.shape, l.dtype) for l in lands],
        in_specs=[any_spec] * (2 * n),
        out_specs=[any_spec] * n,
        input_output_aliases={i: i for i in range(n)},
        scratch_shapes=[pltpu.SemaphoreType.DMA((m, nst)), pltpu.SemaphoreType.DMA((m, nst))],
    )(*lands, *arrs)


_HBM =pl.BlockSpec(memory_space=pltpu.HBM)
_SEM = pl.BlockSpec(memory_space=pltpu.SEMAPHORE)


def _swap_copies(srcs, lands, send_sems, recv_sems, axes, picks):
    x, y, c = _coords()
    co = {"x": x, "y": y, "c": c}
    return [pltpu.make_async_remote_copy(
        src_ref=srcs[a] if picks[a] is None else picks[a](srcs[a], co), dst_ref=lands[a],
        send_sem=send_sems.at[a], recv_sem=recv_sems.at[a],
        device_id=_partner(axes[a]), device_id_type=MESH) for a in range(len(srcs))]


def swap_start(arrs, which, axes, picks, out_shapes, name):
    ns, n = len(arrs), len(which)

    def body(*refs):
        srcs, lands = refs[:ns], refs[ns:ns + n]
        send_sems, recv_sems = refs[ns + n:ns + n + 2]
        token = refs[-1]
        for cp in _swap_copies([srcs[i] for i in which], lands, send_sems, recv_sems, axes, picks):
            cp.start()
        token[...] = jnp.zeros_like(token)

    lands = [lax.empty(s, arrs[i].dtype) for s, i in zip(out_shapes, which)]
    ops = [pltpu.with_memory_space_constraint(a, pltpu.HBM) for a in list(arrs) + lands]
    out = pl.pallas_call(
        body, name=name,
        out_shape=[pltpu.SemaphoreType.DMA((n,)), pltpu.SemaphoreType.DMA((n,))]
        + [pltpu.HBM(o.shape, o.dtype) for o in ops] + [jax.ShapeDtypeStruct((8, 128), F32)],
        in_specs=[_HBM] * (ns + n),
        out_specs=[_SEM, _SEM] + [_HBM] * (ns + n) + [pl.BlockSpec(memory_space=pltpu.VMEM)],
        input_output_aliases={i: 2 + i for i in range(ns + n)},
        compiler_params=pltpu.CompilerParams(has_side_effects=pltpu.SideEffectType.DATAFLOW_SIDE_EFFECTING),
    )(*ops)
    return out[:-1], out[-1]


def swap_wait(state, after, which, axes, picks, name):
    n = len(which)
    ns = len(state) - 2 - n

    def body(*refs):
        srcs, lands = refs[:ns], refs[ns:ns + n]
        send_sems, recv_sems = refs[ns + n:ns + n + 2]
        for cp in _swap_copies([srcs[i] for i in which], lands, send_sems, recv_sems, axes, picks):
            cp.wait_send()
            cp.wait_recv()

    thru = list(state[2:])
    after = list(after) if isinstance(after, (list, tuple)) else [after]
    out = pl.pallas_call(
        body, name=name,
        out_shape=[pltpu.HBM(o.shape, o.dtype) for o in thru],
        in_specs=[_HBM] * (ns + n) + [_SEM, _SEM] + [pl.BlockSpec(memory_space=pl.ANY)] * len(after),
        out_specs=[_HBM] * (ns + n),
        input_output_aliases={i: i for i in range(ns + n)},
        compiler_params=pltpu.CompilerParams(has_side_effects=pltpu.SideEffectType.DATAFLOW_SIDE_EFFECTING),
    )(*thru, state[0], state[1], *after)
    return out[:ns], out[ns:]


def _gather_copies(shards, lands, send_sems, recv_sems):
    x, y, c = _coords()
    copies = []
    for a in range(len(shards)):
        for k in range(1, 8):
            peer = (1 - x if k & 4 else x, 1 - y if k & 2 else y, 1 - c if k & 1 else c)
            copies.append(pltpu.make_async_remote_copy(
                src_ref=shards[a], dst_ref=lands[a].at[x, y, c],
                send_sem=send_sems.at[7 * a + k - 1], recv_sem=recv_sems.at[7 * a + k - 1],
                device_id=peer, device_id_type=MESH))
    return copies


def gather_start(shards, name):
    n = len(shards)
    x, y, c = _coords()

    def body(*refs):
        srcs, lands = refs[:n], refs[n:2 * n]
        send_sems, recv_sems = refs[2 * n:2 * n + 2]
        token = refs[-1]
        for cp in _gather_copies(srcs, lands, send_sems, recv_sems):
            cp.start()
        token[...] = jnp.zeros_like(token)

    lands = [_own_block_placed(s) for s in shards]
    ops = [pltpu.with_memory_space_constraint(a, pltpu.HBM) for a in list(shards) + lands]
    out = pl.pallas_call(
        body, name=name,
        out_shape=[pltpu.SemaphoreType.DMA((7 * n,)), pltpu.SemaphoreType.DMA((7 * n,))]
        + [pltpu.HBM(o.shape, o.dtype) for o in ops] + [jax.ShapeDtypeStruct((8, 128), F32)],
        in_specs=[_HBM] * (2 * n),
        out_specs=[_SEM, _SEM] + [_HBM] * (2 * n) + [pl.BlockSpec(memory_space=pltpu.VMEM)],
        input_output_aliases={i: 2 + i for i in range(2 * n)},
        compiler_params=pltpu.CompilerParams(has_side_effects=pltpu.SideEffectType.DATAFLOW_SIDE_EFFECTING),
    )(*ops)
    return out[:-1], out[-1]


def gather_wait(state, after, name):
    n = (len(state) - 2) // 2

    def body(*refs):
        srcs, lands = refs[:n], refs[n:2 * n]
        send_sems, recv_sems = refs[2 * n:2 * n + 2]
        for cp in _gather_copies(srcs, lands, send_sems, recv_sems):
            cp.wait_send()
            cp.wait_recv()

    thru = list(state[2:])
    out = pl.pallas_call(
        body, name=name,
        out_shape=[pltpu.HBM(o.shape, o.dtype) for o in thru],
        in_specs=[_HBM] * (2 * n) + [_SEM, _SEM, pl.BlockSpec(memory_space=pl.ANY)],
        out_specs=[_HBM] * (2 * n),
        input_output_aliases={i: i for i in range(2 * n)},
        compiler_params=pltpu.CompilerParams(has_side_effects=pltpu.SideEffectType.DATAFLOW_SIDE_EFFECTING),
    )(*thru, state[0], state[1], after)
    return out[n:]


def _scatter_copies(grads, lands, send_sems, recv_sems):
    x, y, c = _coords()
    me = 4 * x + 2 * y + c
    copies = []
    for a in range(len(grads)):
        r = grads[a].shape[0] // 8
        for k in range(1, 8):
            px, py, pc = (1 - x if k & 4 else x, 1 - y if k & 2 else y, 1 - c if k & 1 else c)
            rows = pl.ds(pl.multiple_of((4 * px + 2 * py + pc) * r, r), r)
            copies.append(pltpu.make_async_remote_copy(
                src_ref=grads[a].at[rows], dst_ref=lands[a].at[me],
                send_sem=send_sems.at[7 * a + k - 1], recv_sem=recv_sems.at[7 * a + k - 1],
                device_id=(px, py, pc), device_id_type=MESH))
    return copies


def scatter_start(grads, name):
    n = len(grads)

    def body(*refs):
        srcs, lands = refs[:n], refs[n:2 * n]
        send_sems, recv_sems = refs[2 * n:2 * n + 2]
        token = refs[-1]
        for cp in _scatter_copies(srcs, lands, send_sems, recv_sems):
            cp.start()
        token[...] = jnp.zeros_like(token)

    lands = [lax.empty((8, g.shape[0] // 8, g.shape[1]), g.dtype) for g in grads]
    ops = [pltpu.with_memory_space_constraint(a, pltpu.HBM) for a in list(grads) + lands]
    out = pl.pallas_call(
        body, name=name,
        out_shape=[pltpu.SemaphoreType.DMA((7 * n,)), pltpu.SemaphoreType.DMA((7 * n,))]
        + [pltpu.HBM(o.shape, o.dtype) for o in ops] + [jax.ShapeDtypeStruct((8, 128), F32)],
        in_specs=[_HBM] * (2 * n),
        out_specs=[_SEM, _SEM] + [_HBM] * (2 * n) + [pl.BlockSpec(memory_space=pltpu.VMEM)],
        input_output_aliases={i: 2 + i for i in range(2 * n)},
        compiler_params=pltpu.CompilerParams(has_side_effects=pltpu.SideEffectType.DATAFLOW_SIDE_EFFECTING),
    )(*ops)
    return out[:-1], out[-1]


def scatter_wait(state, after, name):
    n = (len(state) - 2) // 2

    def body(*refs):
        srcs, lands = refs[:n], refs[n:2 * n]
        send_sems, recv_sems = refs[2 * n:2 * n + 2]
        for cp in _scatter_copies(srcs, lands, send_sems, recv_sems):
            cp.wait_send()
            cp.wait_recv()

    thru = list(state[2:])
    after = list(after) if isinstance(after, (list, tuple)) else [after]
    out = pl.pallas_call(
        body, name=name,
        out_shape=[pltpu.HBM(o.shape, o.dtype) for o in thru],
        in_specs=[_HBM] * (2 * n) + [_SEM, _SEM] + [pl.BlockSpec(memory_space=pl.ANY)] * len(after),
        out_specs=[_HBM] * (2 * n),
        input_output_aliases={i: i for i in range(2 * n)},
        compiler_params=pltpu.CompilerParams(has_side_effects=pltpu.SideEffectType.DATAFLOW_SIDE_EFFECTING),
    )(*thru, state[0], state[1], *after)
    return out[:n], out[n:]


def rs_win_add_first(g, r, sel, next_dim, col, name):
    rows, cols = r.shape[2:]

    def body(sel_ref, gk_ref, rk_ref, gs_ref, rs_ref, keep_ref, send_ref):
        keep_ref[...] = gk_ref[...] + rk_ref[...]
        send_ref[...] = (gs_ref[...] + rs_ref[...]).astype(BF16)

    def g_map(flip):
        def f(j, s):
            nxt = 1 - s[next_dim] if flip else s[next_dim]
            return (nxt, j, s[2], 0, col) if next_dim == 0 else (j, nxt, s[2], 0, col)
        return f

    def r_map(flip):
        def f(j, s):
            nxt = 1 - s[next_dim] if flip else s[next_dim]
            return (nxt, j, 0, 0) if next_dim == 0 else (j, nxt, 0, 0)
        return f

    gblk = (None, None, None, rows, cols)
    rblk = (None, None, rows, cols)
    oblk = (None, rows, cols)
    return pl.pallas_call(
        body, name=name,
        grid_spec=pltpu.PrefetchScalarGridSpec(
            num_scalar_prefetch=1, grid=(2,),
            in_specs=[pl.BlockSpec(gblk, g_map(False)), pl.BlockSpec(rblk, r_map(False)),
                      pl.BlockSpec(gblk, g_map(True)), pl.BlockSpec(rblk, r_map(True))],
            out_specs=[pl.BlockSpec(oblk, lambda j, s: (j, 0, 0)),
                       pl.BlockSpec(oblk, lambda j, s: (j, 0, 0))]),
        out_shape=[jax.ShapeDtypeStruct((2, rows, cols), F32),
                   jax.ShapeDtypeStruct((2, rows, cols), BF16)],
        compiler_params=_params(),
    )(sel, g, r, g, r)


def rs_add_second(k, r, sel, name):
    _, rows, cols = k.shape
    tr = rows // 2 if rows % 32 == 0 else rows
    nt = rows // tr

    def body(sel_ref, kk_ref, rk_ref, ks_ref, rs_ref, keep_ref, send_ref):
        keep_ref[...] = kk_ref[...] + rk_ref[...].astype(F32)
        send_ref[...] = (ks_ref[...] + rs_ref[...].astype(F32)).astype(BF16)

    blk = (None, tr, cols)
    oblk = (tr, cols)
    return pl.pallas_call(
        body, name=name,
        grid_spec=pltpu.PrefetchScalarGridSpec(
            num_scalar_prefetch=1, grid=(nt,),
            in_specs=[
                pl.BlockSpec(blk, lambda i, s: (s[0], i, 0)),
                pl.BlockSpec(blk, lambda i, s: (s[0], i, 0)),
                pl.BlockSpec(blk, lambda i, s: (1 - s[0], i, 0)),
                pl.BlockSpec(blk, lambda i, s: (1 - s[0], i, 0)),
            ],
            out_specs=[pl.BlockSpec(oblk, lambda i, s: (i, 0)),
                       pl.BlockSpec(oblk, lambda i, s: (i, 0))]),
        out_shape=[jax.ShapeDtypeStruct((rows, cols), F32),
                   jax.ShapeDtypeStruct((rows, cols), BF16)],
        compiler_params=_params(),
    )(sel, k, r, k, r)


SEG_ROWS = (4800, 5824, 6848, 4096, 0, 1024, 2048, 3072)
LAT_ROWS = QL + KVL + ROPE
N_IN = 7872


def _seg_row(j):
    return pl.multiple_of(jnp.where(j < 3, 4800 + 1024 * j, jnp.where(j == 3, 4096, (j - 4) * 1024)), 8)


def proj_matmul(h, wt_bits, token):
    t = h.shape[0]
    tm = min(2048, t)

    def body(h_ref, w_hbm, tok_ref, o_ref, wt_ref, buf, sems):
        j = pl.program_id(0)
        slot = j % 2

        def fetch(seg, into):
            return pltpu.make_async_copy(w_hbm.at[pl.ds(_seg_row(seg), D)], buf.at[into], sems.at[into])

        @pl.when(pl.program_id(1) == 0)
        def _():
            @pl.when(j == 0)
            def _():
                fetch(j, slot).start()

            fetch(j, slot).wait()

            @pl.when(j + 1 < NSEG)
            def _():
                fetch(j + 1, 1 - slot).start()

            bits = pltpu.bitcast(buf[slot], jnp.uint32)
            row = lax.broadcasted_iota(jnp.int32, (D, D // 2), 0)
            live = jnp.logical_or(j != SEG_LAT, row < LAT_ROWS)
            lo = pltpu.bitcast(bits << 16, F32)
            hi = pltpu.bitcast(bits & jnp.uint32(0xFFFF0000), F32)
            wt_ref[:, :D // 2] = jnp.where(live, lo, 0.0).astype(BF16)
            wt_ref[:, D // 2:] = jnp.where(live, hi, 0.0).astype(BF16)

        o_ref[...] = _dot_nt(h_ref[...], wt_ref[...]).astype(BF16)

    return pl.pallas_call(
        body, name="proj_matmul", grid=(NSEG, t // tm),
        in_specs=[pl.BlockSpec((tm, D), lambda j, i: (i, 0)),
                  pl.BlockSpec(memory_space=pl.ANY),
                  pl.BlockSpec((8, 128), lambda j, i: (0, 0))],
        out_specs=[pl.BlockSpec((None, tm, D), lambda j, i: (j, i, 0)),
                   pl.BlockSpec((D, D), lambda j, i: (j, 0))],
        out_shape=[jax.ShapeDtypeStruct((NSEG, t, D), BF16), jax.ShapeDtypeStruct((NP, D), BF16)],
        scratch_shapes=[pltpu.VMEM((2, D, D // 2), F32), pltpu.SemaphoreType.DMA((2,))],
        compiler_params=_params(("arbitrary", "arbitrary")),
    )(h, wt_bits, token)


def dh_matmul(dproj, wt, token, seq, b):
    tm = min(1024, seq)
    nblk = seq // tm

    per = 2

    def body(b_ref, d_ref, w_ref, tok_ref, o_ref, acc_ref):
        k = pl.program_id(1)
        last = NSEG // per - 1

        def part():
            p = _dot(d_ref[0], w_ref[0:D, :])
            for j in range(1, per):
                p = p + _dot(d_ref[j], w_ref[j * D:(j + 1) * D, :])
            return p

        @pl.when(k == 0)
        def _():
            acc_ref[...] = part()

        @pl.when(jnp.logical_and(k > 0, k < last))
        def _():
            acc_ref[...] += part()

        @pl.when(k == last)
        def _():
            o_ref[...] = acc_ref[...] + part()

    return pl.pallas_call(
        body, name="dh_matmul",
        grid_spec=pltpu.PrefetchScalarGridSpec(
            num_scalar_prefetch=1, grid=(nblk, NSEG // per),
            in_specs=[pl.BlockSpec((per, tm, D), lambda i, k, s: (k, s[0] * nblk + i, 0)),
                      pl.BlockSpec((per * D, D), lambda i, k, s: (k, 0)),
                      pl.BlockSpec((8, 128), lambda i, k, s: (0, 0))],
            out_specs=pl.BlockSpec((tm, D), lambda i, k, s: (i, 0)),
            scratch_shapes=[pltpu.VMEM((tm, D), F32)]),
        out_shape=jax.ShapeDtypeStruct((seq, D), F32),
        compiler_params=_params(("parallel", "arbitrary")),
    )(jnp.full((1,), b, jnp.int32), dproj, wt, token)


def win_grad_matmul(h, dproj, token):
    t = h.shape[0]

    def body(h_ref, d_ref, tok_ref, o_hbm, acc_ref, sems):
        j = pl.program_id(0)

        def out_copy(jj, action):
            slot = lax.rem(jj, 2)

            @pl.when(jj != SEG_LAT)
            def _():
                action(pltpu.make_async_copy(acc_ref.at[slot], o_hbm.at[pl.ds(_seg_row(jj), D)],
                                             sems.at[slot]))

            @pl.when(jj == SEG_LAT)
            def _():
                action(pltpu.make_async_copy(acc_ref.at[slot, pl.ds(0, LAT_ROWS)],
                                             o_hbm.at[pl.ds(SEG_ROWS[SEG_LAT], LAT_ROWS)], sems.at[slot]))

        acc_ref[lax.rem(j, 2)] = _dot_tn(d_ref[...], h_ref[...])
        out_copy(j, lambda cp: cp.start())

        @pl.when(j > 0)
        def _():
            out_copy(j - 1, lambda cp: cp.wait())

        @pl.when(j == NSEG - 1)
        def _():
            out_copy(j, lambda cp: cp.wait())

    return pl.pallas_call(
        body, name="win_grad_matmul", grid=(NSEG,),
        in_specs=[pl.BlockSpec((t, D), lambda j: (0, 0)),
                  pl.BlockSpec((None, t, D), lambda j: (j, 0, 0)),
                  pl.BlockSpec((8, 128), lambda j: (0, 0))],
        out_specs=pl.BlockSpec(memory_space=pl.ANY),
        out_shape=jax.ShapeDtypeStruct((N_IN, D), F32),
        scratch_shapes=[pltpu.VMEM((2, D, D), F32), pltpu.SemaphoreType.DMA((2,))],
        compiler_params=_params(("arbitrary",)),
    )(h, dproj, token)


def grad_matmul(a, b, name):
    t, m = a.shape
    n = b.shape[1]
    tk = min(1024, t)
    nk = t // tk

    def body(a_ref, b_ref, o_ref, acc_ref):
        k = pl.program_id(0)
        part = lambda: _dot_tn(a_ref[...], b_ref[...])
        if nk == 1:
            o_ref[...] = part().astype(BF16)
            return

        @pl.when(k == 0)
        def _():
            acc_ref[...] = part()

        @pl.when(jnp.logical_and(k > 0, k < nk - 1))
        def _():
            acc_ref[...] += part()

        @pl.when(k == nk - 1)
        def _():
            o_ref[...] = (acc_ref[...] + part()).astype(BF16)

    return pl.pallas_call(
        body, name=name, grid=(nk,),
        in_specs=[pl.BlockSpec((tk, m), lambda k: (k, 0)),
                  pl.BlockSpec((tk, n), lambda k: (k, 0))],
        out_specs=pl.BlockSpec((m, n), lambda k: (0, 0)),
        out_shape=jax.ShapeDtypeStruct((m, n), BF16),
        scratch_shapes=[pltpu.VMEM((m, n), F32)],
        compiler_params=_params(("arbitrary",)),
    )(a, b)


def ada_gather(c8, taps8, w_ada, b_cols):
    cols = w_ada.shape[1]

    def body(c_ref, t_ref, w_ref, b_ref, call_ref, tall_ref, mod_ref, part_ref, send_sems, recv_sems):
        x, y, c = _coords()
        me = 4 * x + 2 * y + c
        peers = [(1 - x if k & 4 else x, 1 - y if k & 2 else y, 1 - c if k & 1 else c) for k in range(1, 8)]

        def rcopy(n, src, dst, peer):
            return pltpu.make_async_remote_copy(src_ref=src, dst_ref=dst, send_sem=send_sems.at[n],
                                                recv_sem=recv_sems.at[n], device_id=peer, device_id_type=MESH)

        call_ref[me] = c_ref[...]
        tall_ref[me] = t_ref[...]
        first = []
        for k, peer in enumerate(peers):
            first += [rcopy(k, c_ref, call_ref.at[me], peer), rcopy(7 + k, t_ref, tall_ref.at[me], peer)]
        for cp in first:
            cp.start()
        for cp in first:
            cp.wait()
        rows = call_ref[...].reshape(64, D).astype(BF16)
        part_ref[...] = _dot(rows, w_ref[...].astype(BF16)) + b_ref[...]
        mod_ref[me] = part_ref[pl.ds(pl.multiple_of(8 * me, 8), 8), :]
        second = []
        for k, (px, py, pc) in enumerate(peers):
            theirs = part_ref.at[pl.ds(pl.multiple_of(8 * (4 * px + 2 * py + pc), 8), 8)]
            second.append(rcopy(14 + k, theirs, mod_ref.at[me], (px, py, pc)))
        for cp in second:
            cp.start()
        for cp in second:
            cp.wait()

    vm = pl.BlockSpec(memory_space=pltpu.VMEM)
    return pl.pallas_call(
        body, name="ada_gather",
        out_shape=[jax.ShapeDtypeStruct((8, 8, D), F32), jax.ShapeDtypeStruct((8, 8, 128), F32),
                   jax.ShapeDtypeStruct((8, 8, cols), F32)],
        in_specs=[vm] * 4, out_specs=[vm] * 3,
        scratch_shapes=[pltpu.VMEM((64, cols), F32), pltpu.SemaphoreType.DMA((21,)),
                        pltpu.SemaphoreType.DMA((21,))],
        compiler_params=_params(),
    )(c8, taps8, w_ada, b_cols)


def ada_bwd(c_all, dmod_cols):
    def body(c_ref, d_ref, o_ref):
        o_ref[...] = _dot_tn(c_ref[...].astype(BF16), d_ref[...].astype(BF16))

    return pl.pallas_call(
        body, name="ada_bwd",
        out_shape=jax.ShapeDtypeStruct((c_all.shape[1], dmod_cols.shape[1]), F32),
        compiler_params=_params(),
    )(c_all, dmod_cols)


def slot_sum(g):
    def body(g_ref, o_ref):
        acc = g_ref[0]
        for s in range(1, 8):
            acc = acc + g_ref[s]
        o_ref[...] = acc

    return pl.pallas_call(
        body, name="slot_sum",
        out_shape=jax.ShapeDtypeStruct(g.shape[1:], F32),
    )(g)


def prenorm_fwd(x2, scale, shift, g_pre, seq):
    t = x2.shape[0]
    tm = min(512, seq)
    tpb = seq // tm

    def body(x_ref, sc_ref, sh_ref, g_ref, h_ref):
        xv = x_ref[...]
        r = lax.rsqrt(jnp.mean(xv * xv, axis=-1, keepdims=True) + EPS)
        hv = (xv * r * g_ref[...]) * (1.0 + sc_ref[...]) + sh_ref[...]
        h_ref[...] = hv.astype(BF16)

    per_batch = pl.BlockSpec((None, 1, D), lambda i: (i // tpb, 0, 0))
    return pl.pallas_call(
        body, name="prenorm_fwd", grid=(t // tm,),
        in_specs=[pl.BlockSpec((tm, D), lambda i: (i, 0)), per_batch, per_batch,
                  pl.BlockSpec((1, D), lambda i: (0, 0))],
        out_specs=pl.BlockSpec((tm, D), lambda i: (i, 0)),
        out_shape=jax.ShapeDtypeStruct((t, D), BF16),
        compiler_params=_params(("parallel",)),
    )(x2, scale, shift, g_pre)


def prenorm_bwd(dh, x2, dout, scale, g_pre, seq, token, b, gx_prev):
    t = x2.shape[0]
    tm = min(512, seq)
    tpb = seq // tm
    if gx_prev is None:
        gx_prev = lax.empty((t, D), F32)

    def body(b_ref, dh_ref, x_ref, do_ref, sc_ref, g_ref, tok_ref, gxp_ref, gx_ref, dsh_ref, dsc_ref, dg_ref):
        i = pl.program_id(0)
        xv = x_ref[...]
        dhv = dh_ref[...]
        g = g_ref[...]
        r = lax.rsqrt(jnp.mean(xv * xv, axis=-1, keepdims=True) + EPS)
        nrm = xv * r
        dxn = dhv * (1.0 + sc_ref[...])
        dn = dxn * g
        dx = r * (dn - nrm * jnp.mean(dn * nrm, axis=-1, keepdims=True))
        gx_ref[...] = dx + do_ref[...]

        @pl.when(i == 0)
        def _():
            dsh_ref[...] = jnp.zeros_like(dsh_ref)
            dsc_ref[...] = jnp.zeros_like(dsc_ref)
            dg_ref[...] = jnp.zeros_like(dg_ref)

        dsh_ref[...] += jnp.sum(dhv, axis=0, keepdims=True)
        dsc_ref[...] += jnp.sum(dhv * (nrm * g), axis=0, keepdims=True)
        dg_ref[...] += jnp.sum(dxn * nrm, axis=0, keepdims=True)

    row = pl.BlockSpec((tm, D), lambda i, s: (i, 0))
    grow = pl.BlockSpec((tm, D), lambda i, s: (s[0] * tpb + i, 0))
    per_batch = pl.BlockSpec((None, 1, D), lambda i, s: (s[0], 0, 0))
    vec = pl.BlockSpec((1, D), lambda i, s: (0, 0))
    return pl.pallas_call(
        body, name="prenorm_bwd",
        grid_spec=pltpu.PrefetchScalarGridSpec(
            num_scalar_prefetch=1, grid=(tpb,),
            in_specs=[row, grow, grow, per_batch, vec, pl.BlockSpec((8, 128), lambda i, s: (0, 0)),
                      pl.BlockSpec(memory_space=pl.ANY)],
            out_specs=[grow, vec, vec, vec]),
        out_shape=[jax.ShapeDtypeStruct((t, D), F32), jax.ShapeDtypeStruct((1, D), F32),
                   jax.ShapeDtypeStruct((1, D), F32), jax.ShapeDtypeStruct((1, D), F32)],
        input_output_aliases={7: 0},
        compiler_params=_params(("arbitrary",)),
    )(jnp.full((1,), b, jnp.int32), dh, x2, dout, scale, g_pre, token, gx_prev)


CONV_TC = 128


def _shift_down(u, k, rows):
    idx = lax.broadcasted_iota(jnp.int32, u.shape, 0)
    return jnp.where(idx >= k, pltpu.roll(u, k, 0), 0.0)


def _shift_up(u, k, rows):
    idx = lax.broadcasted_iota(jnp.int32, u.shape, 0)
    return jnp.where(idx < rows - k, pltpu.roll(u, rows - k, 0), 0.0)


def conv_fwd(proj, conv_w, seq):
    t = proj.shape[1]
    nb = t // seq

    def body(p_ref, w_ref, y_ref):
        av = p_ref[0].astype(F32)
        ab = p_ref[1].astype(F32)
        ac = p_ref[2].astype(F32)
        az = p_ref[3].astype(F32)
        w = w_ref[...]
        u = ac * av
        y1 = _shift_down(u, 2, seq) * w[0:1] + _shift_down(u, 1, seq) * w[1:2] + u * w[2:3]
        y_ref[...] = (ab * y1 * (az * _sig(az))).astype(BF16)

    return pl.pallas_call(
        body, name="conv_fwd", grid=(nb, D // CONV_TC),
        in_specs=[pl.BlockSpec((4, seq, CONV_TC), lambda b, ci: (1, b, ci)),
                  pl.BlockSpec((8, CONV_TC), lambda b, ci: (0, ci))],
        out_specs=pl.BlockSpec((seq, CONV_TC), lambda b, ci: (b, ci)),
        out_shape=jax.ShapeDtypeStruct((t, D), BF16),
        compiler_params=_params(("parallel", "parallel")),
    )(proj, conv_w)


def conv_bwd(dproj, proj, dy, conv_w, seq):
    t = proj.shape[1]
    nb = t // seq

    def body(dp_in_ref, p_ref, dy_ref, w_ref, dp_ref, dw_ref):
        b = pl.program_id(1)
        av = p_ref[0].astype(F32)
        ab = p_ref[1].astype(F32)
        ac = p_ref[2].astype(F32)
        az = p_ref[3].astype(F32)
        dyv = dy_ref[...].astype(F32)
        w = w_ref[...]
        u = ac * av
        u1 = _shift_down(u, 1, seq)
        u2 = _shift_down(u, 2, seq)
        y1 = u2 * w[0:1] + u1 * w[1:2] + u * w[2:3]
        sz = _sig(az)
        silu = az * sz
        dy1 = dyv * ab * silu
        du = dy1 * w[2:3] + _shift_up(dy1, 1, seq) * w[1:2] + _shift_up(dy1, 2, seq) * w[0:1]
        dp_ref[0] = (du * ac).astype(BF16)
        dp_ref[1] = (dyv * y1 * silu).astype(BF16)
        dp_ref[2] = (du * av).astype(BF16)
        dp_ref[3] = (dyv * ab * y1 * (sz * (1.0 + az * (1.0 - sz)))).astype(BF16)

        @pl.when(b == 0)
        def _():
            dw_ref[...] = jnp.zeros_like(dw_ref)

        dw_ref[0:1, :] += jnp.sum(dy1 * u2, axis=0, keepdims=True)
        dw_ref[1:2, :] += jnp.sum(dy1 * u1, axis=0, keepdims=True)
        dw_ref[2:3, :] += jnp.sum(dy1 * u, axis=0, keepdims=True)

    return pl.pallas_call(
        body, name="conv_bwd", grid=(D // CONV_TC, nb),
        in_specs=[pl.BlockSpec(memory_space=pl.ANY),
                  pl.BlockSpec((4, seq, CONV_TC), lambda ci, b: (1, b, ci)),
                  pl.BlockSpec((seq, CONV_TC), lambda ci, b: (b, ci)),
                  pl.BlockSpec((8, CONV_TC), lambda ci, b: (0, ci))],
        out_specs=[pl.BlockSpec((4, seq, CONV_TC), lambda ci, b: (1, b, ci)),
                   pl.BlockSpec((8, CONV_TC), lambda ci, b: (0, ci))],
        out_shape=[jax.ShapeDtypeStruct(dproj.shape, BF16),
                   jax.ShapeDtypeStruct((8, D), F32)],
        input_output_aliases={0: 0},
        compiler_params=_params(("parallel", "arbitrary")),
    )(dproj, proj, dy, conv_w)


def _rope_tables(pos_ref, invf_ref, ma_ref, mb_ref, sign):
    ang = pos_ref[...].astype(F32) * invf_ref[...]
    cs = jnp.cos(ang)
    sn = jnp.sin(ang) * sign
    return cs, sn * ma_ref[...], sn * mb_ref[...]


def _rotate(v, cs, sa, sb):
    return v * cs + pltpu.roll(v, 128 - HALF, 1) * sa + pltpu.roll(v, HALF, 1) * sb


MLA_TM = 512


def mla_prep_fwd(proj, pos, g_q, g_kv, wuq, wukv, tabs):
    t = proj.shape[1]
    tm = min(MLA_TM, t)

    def body(lat_ref, pos_ref, gq_ref, gkv_ref, wuq_ref, wukv_ref, invf_ref, ma_ref, mb_ref,
             q_ref, k_ref, kv_ref, qn_ref, kvn_ref):
        lat = lat_ref[...].astype(F32)
        ql = lat[:, :QL]
        kl = lat[:, QL:QL + KVL]
        kr = lat[:, QL + KVL:QL + KVL + 128]
        qn = (ql * lax.rsqrt(jnp.mean(ql * ql, axis=-1, keepdims=True) + EPS) * gq_ref[...]).astype(BF16)
        kvn = (kl * lax.rsqrt(jnp.mean(kl * kl, axis=-1, keepdims=True) + EPS) * gkv_ref[...]).astype(BF16)
        qn_ref[...] = qn
        kvn_ref[...] = kvn
        cs, sa, sb = _rope_tables(pos_ref, invf_ref, ma_ref, mb_ref, 1.0)
        q = _dot_nt(qn, wuq_ref[...]) * (SM_SCALE * LOG2E)
        kv = _dot_nt(kvn, wukv_ref[...]).astype(BF16)
        kv_ref[...] = kv
        kpe = _rotate(kr, cs, sa, sb).astype(BF16)
        for hh in range(H):
            lo, mid, hi = hh * DQK, hh * DQK + 128, (hh + 1) * DQK
            q_ref[:, lo:mid] = q[:, lo:mid].astype(BF16)
            q_ref[:, mid:hi] = _rotate(q[:, mid:hi], cs, sa, sb).astype(BF16)
            k_ref[:, lo:mid] = kv[:, lo:mid]
            k_ref[:, mid:hi] = kpe

    row = lambda w: pl.BlockSpec((tm, w), lambda i: (i, 0))
    const = lambda a: pl.BlockSpec(a.shape, lambda i: (0,) * a.ndim)
    return pl.pallas_call(
        body, name="mla_prep_fwd", grid=(t // tm,),
        in_specs=[pl.BlockSpec((None, tm, D), lambda i: (SEG_LAT, i, 0)), row(1),
                  const(g_q), const(g_kv), const(wuq), const(wukv)] + [const(a) for a in tabs],
        out_specs=[row(H * DQK), row(H * DQK), row(H * DQK), row(QL), row(KVL)],
        out_shape=[jax.ShapeDtypeStruct((t, H * DQK), BF16)] * 3
        + [jax.ShapeDtypeStruct((t, QL), BF16), jax.ShapeDtypeStruct((t, KVL), BF16)],
        compiler_params=_params(("parallel",)),
    )(proj, pos, g_q, g_kv, wuq, wukv, *tabs)


def mla_prep_bwd(dproj, proj, dq_rot, dk, dv, pos, g_q, g_kv, wuq, wukv, tabs):
    t = proj.shape[1]
    tm = min(MLA_TM, t)

    def body(dp_in_ref, lat_ref, dqr_ref, dk_ref, dv_ref, pos_ref, gq_ref, gkv_ref, wuq_ref, wukv_ref,
             invf_ref, ma_ref, mb_ref, dp_ref, dq_ref, dkv_ref, dgq_ref, dgkv_ref):
        i = pl.program_id(0)
        lat = lat_ref[...].astype(F32)
        ql = lat[:, :QL]
        kl = lat[:, QL:QL + KVL]
        rq = lax.rsqrt(jnp.mean(ql * ql, axis=-1, keepdims=True) + EPS)
        rk = lax.rsqrt(jnp.mean(kl * kl, axis=-1, keepdims=True) + EPS)
        nq = ql * rq
        nk = kl * rk
        cs, sa, sb = _rope_tables(pos_ref, invf_ref, ma_ref, mb_ref, -1.0)
        dkpe = jnp.zeros((tm, 128), F32)
        for hh in range(H):
            lo, mid, hi = hh * DQK, hh * DQK + 128, (hh + 1) * DQK
            dq_ref[:, lo:mid] = (dqr_ref[:, lo:mid] * SM_SCALE).astype(BF16)
            dq_ref[:, mid:hi] = _rotate(dqr_ref[:, mid:hi] * SM_SCALE, cs, sa, sb).astype(BF16)
            dkv_ref[:, lo:mid] = dk_ref[:, lo:mid]
            dkv_ref[:, mid:hi] = dv_ref[:, hh * DV:(hh + 1) * DV]
            dkpe = dkpe + dk_ref[:, mid:hi].astype(F32)
        lane = lax.broadcasted_iota(jnp.int32, (tm, 128), 1)
        dkr = jnp.where(lane < ROPE, _rotate(dkpe, cs, sa, sb), 0.0)
        dqn = _dot(dq_ref[...], wuq_ref[...])
        dkvn = _dot(dkv_ref[...], wukv_ref[...])
        gq = gq_ref[...]
        gkv = gkv_ref[...]
        dnq = dqn * gq
        dnk = dkvn * gkv
        dql = rq * (dnq - nq * jnp.mean(dnq * nq, axis=-1, keepdims=True))
        dkl = rk * (dnk - nk * jnp.mean(dnk * nk, axis=-1, keepdims=True))
        dp_ref[:, :QL] = dql.astype(BF16)
        dp_ref[:, QL:QL + KVL] = dkl.astype(BF16)
        dp_ref[:, QL + KVL:QL + KVL + 128] = dkr.astype(BF16)
        dp_ref[:, QL + KVL + 128:] = jnp.zeros((tm, D - QL - KVL - 128), BF16)

        @pl.when(i == 0)
        def _():
            dgq_ref[...] = jnp.zeros_like(dgq_ref)
            dgkv_ref[...] = jnp.zeros_like(dgkv_ref)

        dgq_ref[...] += jnp.sum(dqn * nq, axis=0, keepdims=True)
        dgkv_ref[...] += jnp.sum(dkvn * nk, axis=0, keepdims=True)

    row = lambda w: pl.BlockSpec((tm, w), lambda i: (i, 0))
    const = lambda a: pl.BlockSpec(a.shape, lambda i: (0,) * a.ndim)
    seg = pl.BlockSpec((None, tm, D), lambda i: (SEG_LAT, i, 0))
    return pl.pallas_call(
        body, name="mla_prep_bwd", grid=(t // tm,),
        in_specs=[pl.BlockSpec(memory_space=pl.ANY), seg, row(H * DQK), row(H * DQK), row(H * DV), row(1),
                  const(g_q), const(g_kv), const(wuq), const(wukv)] + [const(a) for a in tabs],
        out_specs=[seg, row(H * DQK), row(H * DQK),
                   pl.BlockSpec((1, QL), lambda i: (0, 0)), pl.BlockSpec((1, KVL), lambda i: (0, 0))],
        out_shape=[jax.ShapeDtypeStruct(dproj.shape, BF16),
                   jax.ShapeDtypeStruct((t, H * DQK), BF16), jax.ShapeDtypeStruct((t, H * DQK), BF16),
                   jax.ShapeDtypeStruct((1, QL), F32), jax.ShapeDtypeStruct((1, KVL), F32)],
        input_output_aliases={0: 0},
        compiler_params=_params(("arbitrary",)),
    )(dproj, proj, dq_rot, dk, dv, pos, g_q, g_kv, wuq, wukv, *tabs)


def _causal_mask(s, shift):
    row = lax.broadcasted_iota(jnp.int32, s.shape, 0)
    col = lax.broadcasted_iota(jnp.int32, s.shape, 1)
    return jnp.where(col <= row + shift, s, -1e30)


def flash_fwd(q, k, kv, nb, seq):
    t = q.shape[0]
    tq = min(FLASH_TQ, seq // 2)
    nq = seq // tq
    assert nq % 2 == 0, "blocks are processed in pairs"

    def update(state, s, vblk):
        m, l, acc = state
        m_new = jnp.maximum(m, jnp.max(s, axis=1, keepdims=True))
        p = jnp.exp2(s - m_new)
        alpha = jnp.exp2(m - m_new)
        return (m_new, alpha * l + jnp.sum(p, axis=1, keepdims=True),
                alpha * acc + _dot(p.astype(BF16), vblk))

    def finish(state, rows, o_ref, lse_ref):
        m, l, acc = state
        o_ref[rows, :] = (acc / l).astype(BF16)
        lse_ref[rows, :] = jnp.broadcast_to(m + jnp.log(l) * LOG2E, (m.shape[0], DV))

    def body(q_ref, k_ref, v_ref, o_ref, lse_ref):
        for qp in range(0, nq, 2):
            rows = 2 * tq
            q0 = qp * tq
            qv = q_ref[q0:q0 + rows, :]
            state = (jnp.full((rows, 1), -1e30, F32), jnp.zeros((rows, 1), F32), jnp.zeros((rows, DV), F32))
            for j in range(qp + 1):
                ks = slice(j * tq, (j + 1) * tq)
                s = _dot_nt(qv, k_ref[ks, :])
                if j == qp:
                    s = _causal_mask(s, 0)
                state = update(state, s, v_ref[ks, :])
            finish(tuple(a[:tq] for a in state), slice(q0, q0 + tq), o_ref, lse_ref)
            ks = slice(q0 + tq, q0 + 2 * tq)
            low = tuple(a[tq:] for a in state)
            low = update(low, _causal_mask(_dot_nt(qv[tq:], k_ref[ks, :]), 0), v_ref[ks, :])
            finish(low, slice(q0 + tq, q0 + 2 * tq), o_ref, lse_ref)

    out_blk = pl.BlockSpec((seq, DV), lambda b, h: (b, h))
    return pl.pallas_call(
        body, name="flash_fwd", grid=(nb, H),
        in_specs=[pl.BlockSpec((seq, DQK), lambda b, h: (b, h)),
                  pl.BlockSpec((seq, DQK), lambda b, h: (b, h)),
                  pl.BlockSpec((seq, DV), lambda b, h: (b, 2 * h + 1))],
        out_specs=[out_blk, out_blk],
        out_shape=[jax.ShapeDtypeStruct((t, H * DV), BF16), jax.ShapeDtypeStruct((t, H * DV), F32)],
        compiler_params=_params(("parallel", "parallel")),
    )(q, k, kv)


def flash_bwd(q, k, kv, o, do, lse, nb, seq, token):
    t = q.shape[0]
    tq = min(FLASH_TQ, seq)
    nq = seq // tq

    def body(q_ref, k_ref, v_ref, o_ref, do_ref, lse_ref, tok_ref, dq_ref, dk_ref, dv_ref):
        delta, lse = [], []
        for qi in range(nq):
            qs = slice(qi * tq, (qi + 1) * tq)
            dl = jnp.sum(do_ref[qs, :].astype(F32) * o_ref[qs, :].astype(F32), axis=1, keepdims=True)
            delta.append(jnp.broadcast_to(dl, (tq, DV)).T[:1, :])
            lse.append(lse_ref[qs, :].T[:1, :])
        for ki in range(nq):
            ks = slice(ki * tq, (ki + 1) * tq)
            kb = k_ref[ks, :]
            vb = v_ref[ks, :]
            dk = jnp.zeros((tq, DQK), F32)
            dv = jnp.zeros((tq, DV), F32)
            for qi in range(ki, nq):
                qs = slice(qi * tq, (qi + 1) * tq)
                qv = q_ref[qs, :]
                dov = do_ref[qs, :]
                st = _dot_nt(kb, qv)
                if qi == ki:
                    row = lax.broadcasted_iota(jnp.int32, st.shape, 0)
                    col = lax.broadcasted_iota(jnp.int32, st.shape, 1)
                    st = jnp.where(row <= col, st, -1e30)
                pt = jnp.exp2(st - lse[qi])
                dpt = _dot_nt(vb, dov)
                dzt = (pt * (dpt - delta[qi])).astype(BF16)
                dv = dv + _dot(pt.astype(BF16), dov)
                dk = dk + _dot(dzt, qv)
                dqb = _dot_tn(dzt, kb)
                if ki == 0:
                    dq_ref[qs, :] = dqb
                else:
                    dq_ref[qs, :] += dqb
            dk_ref[ks, :] = (dk * LN2).astype(BF16)
            dv_ref[ks, :] = dv.astype(BF16)

    full = lambda w, col: pl.BlockSpec((seq, w), col)
    same = lambda b, h: (b, h)
    return pl.pallas_call(
        body, name="flash_bwd", grid=(nb, H),
        in_specs=[full(DQK, same), full(DQK, same), full(DV, lambda b, h: (b, 2 * h + 1)),
                  full(DV, same), full(DV, same), full(DV, same),
                  pl.BlockSpec((8, 128), lambda b, h: (0, 0))],
        out_specs=[full(DQK, same), full(DQK, same), full(DV, same)],
        out_shape=[jax.ShapeDtypeStruct((t, H * DQK), F32), jax.ShapeDtypeStruct((t, H * DQK), BF16),
                   jax.ShapeDtypeStruct((t, H * DV), BF16)],
        compiler_params=_params(("parallel", "parallel")),
    )(q, k, kv, o, do, lse, token)


TAIL_TM = 512


def tail_fwd(y, attn, proj, x2, tgt, gate, g_post, wco, wmo, wout, seq):
    t = y.shape[0]
    nb = t // seq
    tm = min(TAIL_TM, seq)
    tpb = seq // tm

    def body(y_ref, at_ref, p_ref, x_ref, t_ref, gate_ref, gp_ref, wco_ref, wmo_ref, wout_ref,
             o_ref, ya_ref, yb_ref, m_ref, do2_ref, dout_ref, dgate_ref, dgp_ref, loss_ref):
        i = pl.program_id(0)
        bz = p_ref[0].astype(F32)
        ga = p_ref[1].astype(F32)
        gb = p_ref[2].astype(F32)
        ov = (at_ref[...].astype(F32) * (bz * _sig(bz))).astype(BF16)
        o_ref[...] = ov
        ya = _dot(y_ref[...], wco_ref[...])
        yb = _dot(ov, wmo_ref[...])
        ya_ref[...] = ya.astype(BF16)
        yb_ref[...] = yb.astype(BF16)
        mv = (_sig(ga) * ya + _sig(gb) * yb).astype(BF16)
        m_ref[...] = mv
        o2 = _dot(mv, wout_ref[...])
        r = lax.rsqrt(jnp.mean(o2 * o2, axis=-1, keepdims=True) + EPS)
        nrm = o2 * r
        gp = gp_ref[...]
        gate_v = gate_ref[...]
        rn = nrm * gp
        err = x_ref[...] + gate_v * rn - t_ref[...]
        dout = err * (1.0 / D)
        dout_ref[...] = dout
        dn = dout * gate_v * gp
        do2_ref[...] = (r * (dn - nrm * jnp.mean(dn * nrm, axis=-1, keepdims=True))).astype(BF16)

        @pl.when(i % tpb == 0)
        def _():
            dgate_ref[...] = jnp.zeros_like(dgate_ref)

        @pl.when(i == 0)
        def _():
            dgp_ref[...] = jnp.zeros_like(dgp_ref)
            loss_ref[...] = jnp.zeros_like(loss_ref)

        dgate_ref[...] += jnp.sum(dout * rn, axis=0, keepdims=True)
        dgp_ref[...] += jnp.sum(dout * gate_v * nrm, axis=0, keepdims=True)
        loss_ref[...] += 0.5 * jnp.sum(jnp.mean(err * err, axis=-1, keepdims=True), axis=0, keepdims=True)

    row = pl.BlockSpec((tm, D), lambda i: (i, 0))
    per_batch = pl.BlockSpec((None, 1, D), lambda i: (i // tpb, 0, 0))
    vec = pl.BlockSpec((1, D), lambda i: (0, 0))
    wgt = pl.BlockSpec((D, D), lambda i: (0, 0))
    act = jax.ShapeDtypeStruct((t, D), BF16)
    return pl.pallas_call(
        body, name="tail_fwd", grid=(t // tm,),
        in_specs=[row, row, pl.BlockSpec((3, tm, D), lambda i: (0, i, 0)), row, row, per_batch, vec,
                  wgt, wgt, wgt],
        out_specs=[row, row, row, row, row, row, per_batch, vec, pl.BlockSpec((1, 1), lambda i: (0, 0))],
        out_shape=[act, act, act, act, act, jax.ShapeDtypeStruct((t, D), F32),
                   jax.ShapeDtypeStruct((nb, 1, D), F32), jax.ShapeDtypeStruct((1, D), F32),
                   jax.ShapeDtypeStruct((1, 1), F32)],
        compiler_params=_params(("arbitrary",)),
    )(y, attn, proj, x2, tgt, gate, g_post, wco, wmo, wout)


def tail_bwd(do2, proj, ya, yb, attn, wout, wmo, wco):
    t = do2.shape[0]
    tm = min(TAIL_TM, t)

    def body(do2_ref, p_ref, ya_ref, yb_ref, at_ref, wout_ref, wmo_ref, wco_ref,
             dp_ref, dya_ref, dyb_ref, dat_ref, dy_ref):
        bz = p_ref[0].astype(F32)
        ga = p_ref[1].astype(F32)
        gb = p_ref[2].astype(F32)
        dm = _dot_nt(do2_ref[...], wout_ref[...])
        sa = _sig(ga)
        sb = _sig(gb)
        dya = (dm * sa).astype(BF16)
        dyb = (dm * sb).astype(BF16)
        dya_ref[...] = dya
        dyb_ref[...] = dyb
        dp_ref[1] = (dm * ya_ref[...].astype(F32) * (sa * (1.0 - sa))).astype(BF16)
        dp_ref[2] = (dm * yb_ref[...].astype(F32) * (sb * (1.0 - sb))).astype(BF16)
        dov = _dot_nt(dyb, wmo_ref[...])
        sz = _sig(bz)
        dat_ref[...] = (dov * (bz * sz)).astype(BF16)
        dp_ref[0] = (dov * at_ref[...].astype(F32) * (sz * (1.0 + bz * (1.0 - sz)))).astype(BF16)
        dy_ref[...] = _dot_nt(dya, wco_ref[...]).astype(BF16)

    row = pl.BlockSpec((tm, D), lambda i: (i, 0))
    seg3 = pl.BlockSpec((3, tm, D), lambda i: (0, i, 0))
    wgt = pl.BlockSpec((D, D), lambda i: (0, 0))
    act = jax.ShapeDtypeStruct((t, D), BF16)
    return pl.pallas_call(
        body, name="tail_bwd", grid=(t // tm,),
        in_specs=[row, seg3, row, row, row, wgt, wgt, wgt],
        out_specs=[seg3, row, row, row, row],
        out_shape=[jax.ShapeDtypeStruct((NSEG, t, D), BF16), act, act, act, act],
        compiler_params=_params(("parallel",)),
    )(do2, proj, ya, yb, attn, wout, wmo, wco)


def _adam_update(w, m, v, grad):
    mn = ADAM_B1 * m + (1.0 - ADAM_B1) * grad
    vn = ADAM_B2 * v + (1.0 - ADAM_B2) * (grad * grad)
    m_hat = mn / (1.0 - ADAM_B1 ** ADAM_STEP)
    v_hat = vn / (1.0 - ADAM_B2 ** ADAM_STEP)
    return -ADAM_LR * (m_hat / (jnp.sqrt(v_hat) + ADAM_EPS) + ADAM_WD * w), mn, vn


def adamw(w, m, v, g, name, token):
    rows, cols = w.shape
    tr = rows
    for cand in (256, 128, 64, 32, 16, 8):
        if rows % cand == 0 and rows > cand:
            tr = cand
            break

    def body(w_ref, m_ref, v_ref, g_ref, tok_ref, d_ref, mo_ref, vo_ref):
        d_ref[...], mo_ref[...], vo_ref[...] = _adam_update(w_ref[...], m_ref[...], v_ref[...], g_ref[...])

    blk = pl.BlockSpec((tr, cols), lambda i: (i, 0))
    return pl.pallas_call(
        body, name=name, grid=(rows // tr,),
        in_specs=[blk] * 4 + [pl.BlockSpec((8, 128), lambda i: (0, 0))], out_specs=[blk] * 3,
        out_shape=[jax.ShapeDtypeStruct((rows, cols), F32)] * 3,
        compiler_params=_params(("parallel",)),
    )(w, m, v, g, token)


def adamw_scattered(w, m, v, own, land, me, tr, name, transpose=False):
    slot_rows = land.shape[1]
    cols = land.shape[2]
    rows = slot_rows if transpose else w.shape[0]
    per_slot = slot_rows // tr

    def body(me_ref, w_ref, m_ref, v_ref, own_ref, land_ref, go_ref, d_ref, mo_ref, vo_ref):
        grad = own_ref[...].astype(F32)
        for s in range(8):
            grad = grad + jnp.where(me_ref[0] == s, 0.0, land_ref[s].astype(F32))
        if transpose:
            grad = grad.T
        go_ref[...] = grad
        d_ref[...], mo_ref[...], vo_ref[...] = _adam_update(w_ref[...], m_ref[...], v_ref[...], grad)

    wblk = pl.BlockSpec(w.shape if transpose else (tr, w.shape[1]), lambda i, s: (i, 0))
    return pl.pallas_call(
        body, name=name,
        grid_spec=pltpu.PrefetchScalarGridSpec(
            num_scalar_prefetch=1, grid=(rows // tr,),
            in_specs=[wblk, wblk, wblk,
                      pl.BlockSpec((tr, cols), lambda i, s: (s[0] * per_slot + i, 0)),
                      pl.BlockSpec((8, tr, cols), lambda i, s: (0, i, 0))],
            out_specs=[wblk] * 4),
        out_shape=[jax.ShapeDtypeStruct(w.shape, F32)] * 4,
        compiler_params=_params(),
    )(me, w, m, v, own, land)


def adamw_win(wt, mt, vt, ka, ra, kb, rb):
    rows = wt.shape[0]
    tc = 256
    nh = (D // 2) // tc

    def body(w_ref, m_ref, v_ref, ka_ref, ra_ref, kb_ref, rb_ref, go_ref, d_ref, mo_ref, vo_ref):
        first = pl.program_id(0) < nh
        grad = jnp.where(first, ka_ref[...] + ra_ref[...].astype(F32), kb_ref[...] + rb_ref[...].astype(F32))
        go_ref[...] = grad
        d_ref[...], mo_ref[...], vo_ref[...] = _adam_update(w_ref[...], m_ref[...], v_ref[...], grad)

    blk = pl.BlockSpec((rows, tc), lambda j: (0, j))
    lo = pl.BlockSpec((rows, tc), lambda j: (0, jnp.minimum(j, nh - 1)))
    hi = pl.BlockSpec((rows, tc), lambda j: (0, jnp.maximum(j - nh, 0)))
    return pl.pallas_call(
        body, name="adamw_w_in", grid=(D // tc,),
        in_specs=[blk, blk, blk, lo, lo, hi, hi], out_specs=[blk] * 4,
        out_shape=[jax.ShapeDtypeStruct((rows, D), F32)] * 4,
        compiler_params=_params(("parallel",)),
    )(wt, mt, vt, ka, ra, kb, rb)


_ORD_A = ("x", "y", "c")
_ORD_B = ("y", "x", "c")


def _rows128(a, rows):
    flat = a.reshape(-1)
    return jnp.pad(flat, (0, rows * 128 - flat.shape[0])).reshape(rows, 128)


def kernel(x, c, positions, w_ada, b_ada, g_pre, w_in, conv_w, w_conv_out, g_q, w_uq, g_kv, w_ukv, w_mla_out, w_out, g_post, loss_target, m_w_ada, m_b_ada, m_g_pre, m_w_in, m_conv_w, m_w_conv_out, m_g_q, m_w_uq, m_g_kv, m_w_ukv, m_w_mla_out, m_w_out, m_g_post, v_w_ada, v_b_ada, v_g_pre, v_w_in, v_conv_w, v_w_conv_out, v_g_q, v_w_uq, v_g_kv, v_w_ukv, v_w_mla_out, v_w_out, v_g_post):
    nb, seq, _ = x.shape
    t = nb * seq
    mx, my, mc = lax.axis_index("x"), lax.axis_index("y"), lax.axis_index("c")
    me = 4 * mx + 2 * my + mc
    co = {"x": mx, "y": my, "c": mc}

    x2 = x.reshape(t, D)
    tgt2 = loss_target.reshape(t, D)
    pos2 = positions.reshape(t, 1)

    ada_cols = w_ada.shape[2]
    b_cols = lax.dynamic_slice(b_ada, (0, me * ada_cols), (1, ada_cols))
    c_g, taps_g, mod_g = ada_gather(jnp.pad(c, ((0, 8 - nb), (0, 0))), _rows128(conv_w[0], 8), w_ada[0], b_cols)
    c_all = c_g[:, :nb].reshape(8 * nb, D)
    conv_full = taps_g[:, 0:3].transpose(1, 0, 2).reshape(3, D)
    conv_full8 = jnp.pad(conv_full, ((0, 5), (0, 0)))
    mod = mod_g[:, :nb].transpose(1, 0, 2).reshape(nb, 8 * ada_cols)
    shift = mod[:, 0:D].reshape(nb, 1, D)
    scale = mod[:, D:2 * D].reshape(nb, 1, D)
    gate = mod[:, 2 * D:3 * D].reshape(nb, 1, D)

    wt = w_in[0].T.astype(BF16)
    lo = lax.bitcast_convert_type(wt[:, :D // 2], jnp.uint16).astype(jnp.uint32)
    hi = lax.bitcast_convert_type(wt[:, D // 2:], jnp.uint16).astype(jnp.uint32)
    wt_bits = lax.bitcast_convert_type(lo | (hi << 16), F32)
    wt_bits, mod = lax.optimization_barrier((wt_bits, mod))
    shift = mod[:, 0:D].reshape(nb, 1, D)
    scale = mod[:, D:2 * D].reshape(nb, 1, D)
    gate = mod[:, 2 * D:3 * D].reshape(nb, 1, D)
    q4 = D // 4
    r3rd = wt_bits.shape[0] // 3
    plan = [(0, (k * r3rd, r3rd), (g * q4, q4), (_ORD_A, _ORD_B)[g]) for k in range(3) for g in range(2)]
    gw = allgather_big([wt_bits], plan, "gather_w_in")
    late = [w_conv_out[0].astype(BF16), w_mla_out[0].astype(BF16), w_out[0].astype(BF16),
            jnp.pad(w_uq[0].T.astype(BF16), ((0, DQK - 192), (0, 0))), w_ukv[0].T.astype(BF16)]
    gw0, late = lax.optimization_barrier((gw[0], late))
    late_state, late_token = gather_start(late, "gather_late_start")
    wt_bits_all = gw0.reshape(N_IN, D // 2)

    inv_freq = ROPE_THETA ** (-jnp.arange(0, ROPE, 2, dtype=F32) / ROPE)
    invf = jnp.concatenate([inv_freq, inv_freq, jnp.zeros((128 - ROPE,), F32)]).reshape(1, 128)
    lane = np.arange(128)
    tabs = (invf,
            jnp.asarray(np.where(lane < HALF, -1.0, 0.0).reshape(1, 128), F32),
            jnp.asarray(np.where((lane >= HALF) & (lane < ROPE), 1.0, 0.0).reshape(1, 128), F32))

    h = prenorm_fwd(x2, scale, shift, g_pre, seq)
    proj, wt_p = proj_matmul(h, wt_bits_all, late_token)
    y = conv_fwd(proj, conv_full8, seq)
    gl = gather_wait(late_state, y, "gather_late_wait")
    wco = gl[0].reshape(D, D)
    wmo = gl[1].reshape(D, D)
    wout = gl[2].reshape(D, D)
    wuq_p = gl[3].reshape(H * DQK, QL)
    wukv = gl[4].reshape(H * 256, KVL)
    q_rot, k_cat, kv, qn, kvn = mla_prep_fwd(proj, pos2, g_q, g_kv, wuq_p, wukv, tabs)
    attn, lse = flash_fwd(q_rot, k_cat, kv, nb, seq)
    o, ya, yb, m, do2, dout, dgate, dg_post, loss_part = tail_fwd(
        y, attn, proj, x2, tgt2, gate, g_post, wco, wmo, wout, seq)

    dproj, dya, dyb, dattn, dy = tail_bwd(do2, proj, ya, yb, attn, wout, wmo, wco)
    g_wout = grad_matmul(m, do2, "grad_w_square")
    g_wmo = grad_matmul(o, dyb, "grad_w_square")
    g_wco = grad_matmul(y, dya, "grad_w_square")
    sc1, sc1_tok = scatter_start([g_wco, g_wmo, g_wout], "scatter_out_grads_start")
    dproj, dconv = conv_bwd(dproj, proj, dy, conv_full8, seq)
    dq_rot, dk, dv = flash_bwd(q_rot, k_cat, kv, attn, dattn, lse, nb, seq, sc1_tok)
    dproj, dq, dkv, dg_q, dg_kv = mla_prep_bwd(dproj, proj, dq_rot, dk, dv, pos2, g_q, g_kv, wuq_p, wukv, tabs)
    g_wuq_t = grad_matmul(dq, qn, "grad_w_uq")
    g_wukv_t = grad_matmul(dkv, kvn, "grad_w_ukv")
    sc2, sc2_tok = scatter_start([g_wuq_t, g_wukv_t], "scatter_mla_grads_start")
    g_win_p = win_grad_matmul(h, dproj, sc2_tok)

    g_wt = g_win_p.reshape(2, 2, 2, N_IN // 8, D)
    ords = [("c", "y", "x"), ("c", "x", "y")]
    hc = D // 2
    win_shape = (2, 2, N_IN // 8, hc)
    pick_w = lambda col: (lambda ref, cc: ref.at[:, :, 1 - cc["c"], :, pl.ds(col * hc, hc)])
    which1 = [0, 0]
    picks1 = [pick_w(0), pick_w(1)]
    st1, tok1 = swap_start([g_wt], which1, ["c"] * 2, picks1, [win_shape] * 2, "rs_c_start")
    assert nb == 2
    dh0 = dh_matmul(dproj, wt_p, tok1, seq, 0)
    (g_wt,), r1 = swap_wait(st1, dh0, which1, ["c"] * 2, picks1, "rs_c_wait")
    sel_xyc = jnp.stack([mx, my, mc]).astype(jnp.int32)
    sel2 = [jnp.stack([co[o[2]]]).astype(jnp.int32) for o in ords]
    first = [rs_win_add_first(g_wt, r1[0], sel_xyc, 1, 0, "rs_add_first_0"),
             rs_win_add_first(g_wt, r1[1], sel_xyc, 0, 1, "rs_add_first_1")]
    keep1, send1 = zip(*first)
    all4 = [0, 1]
    none4 = [None] * 2
    axes2 = [o[1] for o in ords]
    st2, tok2 = swap_start(list(send1), all4, axes2, none4, [s.shape for s in send1], "rs_ici1_start")

    dh1 = dh_matmul(dproj, wt_p, tok2, seq, 1)
    gx0, dsh0, dsc0, dgp0 = prenorm_bwd(dh0, x2, dout, scale, g_pre, seq, tok2, 0, None)
    _, r2 = swap_wait(st2, (gx0, dh1), all4, axes2, none4, "rs_ici1_wait")
    keep2, send2 = zip(*[rs_add_second(keep1[a], r2[a], sel2[a], "rs_add_second") for a in range(2)])
    axes3 = [o[2] for o in ords]
    st3, tok3 = swap_start(list(send2), all4, axes3, none4, [s.shape for s in send2], "rs_ici2_start")
    grad_x2, dsh1, dsc1, dgp1 = prenorm_bwd(dh1, x2, dout, scale, g_pre, seq, tok3, 1, gx0)
    dshift = jnp.stack([dsh0, dsh1])
    dscale = jnp.stack([dsc0, dsc1])
    dg_pre = dgp0 + dgp1

    dmod = jnp.concatenate([dshift, dscale, dgate], axis=2).reshape(nb * 3 * D // 128, 128)
    small = jnp.concatenate([
        dmod, _rows128(dg_pre, 8), _rows128(dg_post, 8), _rows128(dg_q, 8), _rows128(dg_kv, 8),
        dconv[0:3].reshape(24, 128), _rows128(loss_part, 8)], axis=0)
    small_g = small_allgather(small, "gather_small_grads")
    sums = slot_sum(small_g)
    dmod_all = small_g[:, 0:48].reshape(8 * nb, 3 * D)
    g_bada = (sums[0:24] + sums[24:48]).reshape(1, 3 * D)
    g_gpre = sums[48:56].reshape(1, D)
    g_gpost = sums[56:64].reshape(1, D)
    g_gq = sums[64:67].reshape(1, QL)
    g_gkv = sums[72:74].reshape(1, KVL)
    g_conv_full = sums[80:104].reshape(3, D)
    loss = sums[104, 0]
    g_conv = lax.dynamic_slice(g_conv_full, (0, me * 128), (3, 128))
    dmod_cols = lax.dynamic_slice(dmod_all, (0, me * ada_cols), (8 * nb, ada_cols))
    g_wada = ada_bwd(c_all, dmod_cols)

    res = {}
    res["w_ada"] = [o_[None] for o_ in (g_wada, *adamw(w_ada[0], m_w_ada[0], v_w_ada[0], g_wada, "adamw_w_ada", tok3))]

    def pack(b_, gp_, gpo_, gq_, gkv_, cw_):
        return jnp.concatenate([_rows128(b_, 24), _rows128(gp_, 8), _rows128(gpo_, 8), _rows128(gq_, 8),
                                _rows128(gkv_, 8), _rows128(cw_, 8)], axis=0)

    sw = pack(b_ada, g_pre, g_post, g_q, g_kv, conv_w)
    sm = pack(m_b_ada, m_g_pre, m_g_post, m_g_q, m_g_kv, m_conv_w)
    sv = pack(v_b_ada, v_g_pre, v_g_post, v_g_q, v_g_kv, v_conv_w)
    sg = pack(g_bada, g_gpre, g_gpost, g_gq, g_gkv, g_conv)
    small_out = (sg, *adamw(sw, sm, sv, sg, "adamw_small", tok3))

    _, r3 = swap_wait(st3, small_out[1], all4, axes3, none4, "rs_ici2_wait")

    (g_wco, g_wmo, g_wout), (l_wco, l_wmo, l_wout) = scatter_wait(sc1, small_out[2], "scatter_out_grads_wait")
    (g_wuq_t, g_wukv_t), (l_wuq, l_wukv) = scatter_wait(sc2, small_out[3], "scatter_mla_grads_wait")

    res["w_in"] = [o_.T[None] for o_ in adamw_win(w_in[0].T, m_w_in[0].T, v_w_in[0].T,
                                                  keep2[0], r3[0], keep2[1], r3[1])]
    me1 = me.reshape(1).astype(jnp.int32)
    res["w_uq"] = [o_.T[None] for o_ in adamw_scattered(
        w_uq[0].T, m_w_uq[0].T, v_w_uq[0].T, g_wuq_t, l_wuq, me1, 64, "adamw_w_uq")]
    res["w_ukv"] = [o_[None] for o_ in adamw_scattered(
        w_ukv[0], m_w_ukv[0], v_w_ukv[0], g_wukv_t, l_wukv, me1, KVL, "adamw_w_ukv", transpose=True)]
    for nm, wv, mv, vv, gg, ll in (("w_conv_out", w_conv_out, m_w_conv_out, v_w_conv_out, g_wco, l_wco),
                                   ("w_mla_out", w_mla_out, m_w_mla_out, v_w_mla_out, g_wmo, l_wmo),
                                   ("w_out", w_out, m_w_out, v_w_out, g_wout, l_wout)):
        res[nm] = [o_[None] for o_ in adamw_scattered(wv[0], mv[0], vv[0], gg, ll, me1, 128, "adamw_square")]

    def unpack(a):
        return {"b_ada": a[0:24].reshape(1, 3 * D), "g_pre": a[24:32].reshape(1, D),
                "g_post": a[32:40].reshape(1, D), "g_q": a[40:43].reshape(1, QL),
                "g_kv": a[48:50].reshape(1, KVL), "conv_w": a[56:59].reshape(-1)[:3 * 128].reshape(1, 3, 128)}

    for nm in ("b_ada", "g_pre", "g_post", "g_q", "g_kv", "conv_w"):
        res[nm] = [unpack(a)[nm] for a in small_out]

    order = ["w_ada", "b_ada", "g_pre", "w_in", "conv_w", "w_conv_out", "g_q", "w_uq", "g_kv", "w_ukv",
             "w_mla_out", "w_out", "g_post"]
    out = [loss, grad_x2.reshape(nb, seq, D)]
    for k_ in range(4):
        out += [res[nm][k_] for nm in order]
    return tuple(out)
```

```python
import numpy as np
import jax
import jax.numpy as jnp
from jax import lax
from jax.experimental import pallas as pl
from jax.experimental.pallas import tpu as pltpu

F32 = jnp.float32
BF16 = jnp.bfloat16
MESH = pl.DeviceIdType.MESH

D = 1024
H = 8
QL = 384
KVL = 256
ROPE = 64
HALF = ROPE // 2
DQK = 256
DV = 128
NSEG = 8
NP = NSEG * D
EPS = 1e-6
ROPE_THETA = 10000.0
SM_SCALE = (128 + ROPE) ** -0.5
LOG2E = 1.4426950408889634
LN2 = 0.6931471805599453
FLASH_TQ = 512

SEG_BZ, SEG_GA, SEG_GB, SEG_LAT, SEG_V = 0, 1, 2, 3, 4

ADAM_LR = 0.001
ADAM_B1 = 0.9
ADAM_B2 = 0.999
ADAM_EPS = 1e-08
ADAM_WD = 0.01
ADAM_STEP = 10

VMEM_LIMIT = 56 * 1024 * 1024


def _params(sem=None, vmem=VMEM_LIMIT):
    kw = dict(vmem_limit_bytes=vmem)
    if sem is not None:
        kw["dimension_semantics"] = sem
    return pltpu.CompilerParams(**kw)


def _sig(v):
    return 0.5 * jnp.tanh(0.5 * v) + 0.5


def _dot(a, b):
    return jnp.dot(a, b, preferred_element_type=F32)


def _dot_nt(a, b):
    return lax.dot_general(a, b, (((1,), (1,)), ((), ())), preferred_element_type=F32)


def _dot_tn(a, b):
    return lax.dot_general(a, b, (((0,), (0,)), ((), ())), preferred_element_type=F32)


_AXIS_POS = {"x": 0, "y": 1, "c": 2}


def _coords():
    return lax.axis_index("x"), lax.axis_index("y"), lax.axis_index("c")


def _partner(axis):
    p = list(_coords())
    p[_AXIS_POS[axis]] = 1 - p[_AXIS_POS[axis]]
    return tuple(p)


def small_allgather(v, name):
    rows = v.shape[0]

    def body(v_ref, out_ref, send_sems, recv_sems):
        x, y, c = _coords()
        me = 4 * x + 2 * y + c
        out_ref[me] = v_ref[...]
        copies = []
        for k in range(1, 8):
            peer = (1 - x if k & 4 else x, 1 - y if k & 2 else y, 1 - c if k & 1 else c)
            cp = pltpu.make_async_remote_copy(
                src_ref=v_ref, dst_ref=out_ref.at[me],
                send_sem=send_sems.at[k - 1], recv_sem=recv_sems.at[k - 1],
                device_id=peer, device_id_type=MESH)
            cp.start()
            copies.append(cp)
        for cp in copies:
            cp.wait()

    return pl.pallas_call(
        body, name=name,
        out_shape=jax.ShapeDtypeStruct((8, rows, 128), F32),
        in_specs=[pl.BlockSpec(memory_space=pltpu.VMEM)],
        out_specs=pl.BlockSpec(memory_space=pltpu.VMEM),
        scratch_shapes=[pltpu.SemaphoreType.DMA((7,)), pltpu.SemaphoreType.DMA((7,))],
    )(v)


def _own_block_placed(s):
    x, y, c = _coords()
    return lax.dynamic_update_slice(lax.empty((2, 2, 2) + s.shape, s.dtype), s[None, None, None],
                                    (x, y, c) + (0,) * s.ndim)


def allgather_big(arrs, plan, name):
    n = len(arrs)
    m = len(plan)
    nst = len(plan[0][3])

    def body(*refs):
        ins, outs = refs[n:2 * n], refs[2 * n:3 * n]
        send_sems, recv_sems = refs[3 * n:]
        x, y, c = _coords()
        co = {"x": x, "y": y, "c": c}

        def window(ref, lead, rows, cols):
            win = tuple(slice(None) if w is None else pl.ds(w[0], w[1]) for w in (rows, cols))
            return ref.at[tuple(lead) + win]

        def held(e, free):
            i, rows, cols, _ = plan[e]
            lead = [slice(None) if ax in free else co[ax] for ax in ("x", "y", "c")]
            return window(outs[i], lead, rows, cols)

        def rcopy(e, stage, src, dst, axis):
            return pltpu.make_async_remote_copy(
                src_ref=src, dst_ref=dst,
                send_sem=send_sems.at[e, stage], recv_sem=recv_sems.at[e, stage],
                device_id=_partner(axis), device_id_type=MESH)

        stages = [[] for _ in range(nst)]
        for e, (i, rows, cols, order) in enumerate(plan):
            cp = rcopy(e, 0, window(ins[i], [], rows, cols), held(e, ()), order[0])
            cp.start()
            stages[0].append(cp)
        for s in range(1, nst):
            for e, (i, rows, cols, order) in enumerate(plan):
                stages[s - 1][e].wait_recv()
                blk = held(e, order[:s])
                cp = rcopy(e, s, blk, blk, order[s])
                cp.start()
                stages[s].append(cp)
        for e in range(m):
            stages[nst - 1][e].wait_recv()
        for e in range(m):
            for s in range(nst):
                stages[s][e].wait_send()

    any_spec = pl.BlockSpec(memory_space=pl.ANY)
    lands = [_own_block_placed(a) for a in arrs]
    return pl.pallas_call(
        body, name=name,
        out_shape=[jax.ShapeDtypeStruct(l.shape, l.dtype) for l in lands],
        in_specs=[any_spec] * (2 * n),
        out_specs=[any_spec] * n,
        input_output_aliases={i: i for i in range(n)},
        scratch_shapes=[pltpu.SemaphoreType.DMA((m, nst)), pltpu.SemaphoreType.DMA((m, nst))],
    )(*lands, *arrs)


_HBM =pl.BlockSpec(memory_space=pltpu.HBM)
_SEM = pl.BlockSpec(memory_space=pltpu.SEMAPHORE)


def _swap_copies(srcs, lands, send_sems, recv_sems, axes, picks):
    x, y, c = _coords()
    co = {"x": x, "y": y, "c": c}
    return [pltpu.make_async_remote_copy(
        src_ref=srcs[a] if picks[a] is None else picks[a](srcs[a], co), dst_ref=lands[a],
        send_sem=send_sems.at[a], recv_sem=recv_sems.at[a],
        device_id=_partner(axes[a]), device_id_type=MESH) for a in range(len(srcs))]


def swap_start(arrs, which, axes, picks, out_shapes, name):
    ns, n = len(arrs), len(which)

    def body(*refs):
        srcs, lands = refs[:ns], refs[ns:ns + n]
        send_sems, recv_sems = refs[ns + n:ns + n + 2]
        token = refs[-1]
        for cp in _swap_copies([srcs[i] for i in which], lands, send_sems, recv_sems, axes, picks):
            cp.start()
        token[...] = jnp.zeros_like(token)

    lands = [lax.empty(s, arrs[i].dtype) for s, i in zip(out_shapes, which)]
    ops = [pltpu.with_memory_space_constraint(a, pltpu.HBM) for a in list(arrs) + lands]
    out = pl.pallas_call(
        body, name=name,
        out_shape=[pltpu.SemaphoreType.DMA((n,)), pltpu.SemaphoreType.DMA((n,))]
        + [pltpu.HBM(o.shape, o.dtype) for o in ops] + [jax.ShapeDtypeStruct((8, 128), F32)],
        in_specs=[_HBM] * (ns + n),
        out_specs=[_SEM, _SEM] + [_HBM] * (ns + n) + [pl.BlockSpec(memory_space=pltpu.VMEM)],
        input_output_aliases={i: 2 + i for i in range(ns + n)},
        compiler_params=pltpu.CompilerParams(has_side_effects=pltpu.SideEffectType.DATAFLOW_SIDE_EFFECTING),
    )(*ops)
    return out[:-1], out[-1]


def swap_wait(state, after, which, axes, picks, name):
    n = len(which)
    ns = len(state) - 2 - n

    def body(*refs):
        srcs, lands = refs[:ns], refs[ns:ns + n]
        send_sems, recv_sems = refs[ns + n:ns + n + 2]
        for cp in _swap_copies([srcs[i] for i in which], lands, send_sems, recv_sems, axes, picks):
            cp.wait_send()
            cp.wait_recv()

    thru = list(state[2:])
    after = list(after) if isinstance(after, (list, tuple)) else [after]
    out = pl.pallas_call(
        body, name=name,
        out_shape=[pltpu.HBM(o.shape, o.dtype) for o in thru],
        in_specs=[_HBM] * (ns + n) + [_SEM, _SEM] + [pl.BlockSpec(memory_space=pl.ANY)] * len(after),
        out_specs=[_HBM] * (ns + n),
        input_output_aliases={i: i for i in range(ns + n)},
        compiler_params=pltpu.CompilerParams(has_side_effects=pltpu.SideEffectType.DATAFLOW_SIDE_EFFECTING),
    )(*thru, state[0], state[1], *after)
    return out[:ns], out[ns:]


def _gather_copies(shards, lands, send_sems, recv_sems):
    x, y, c = _coords()
    copies = []
    for a in range(len(shards)):
        for k in range(1, 8):
            peer = (1 - x if k & 4 else x, 1 - y if k & 2 else y, 1 - c if k & 1 else c)
            copies.append(pltpu.make_async_remote_copy(
                src_ref=shards[a], dst_ref=lands[a].at[x, y, c],
                send_sem=send_sems.at[7 * a + k - 1], recv_sem=recv_sems.at[7 * a + k - 1],
                device_id=peer, device_id_type=MESH))
    return copies


def gather_start(shards, name):
    n = len(shards)
    x, y, c = _coords()

    def body(*refs):
        srcs, lands = refs[:n], refs[n:2 * n]
        send_sems, recv_sems = refs[2 * n:2 * n + 2]
        token = refs[-1]
        for cp in _gather_copies(srcs, lands, send_sems, recv_sems):
            cp.start()
        token[...] = jnp.zeros_like(token)

    lands = [_own_block_placed(s) for s in shards]
    ops = [pltpu.with_memory_space_constraint(a, pltpu.HBM) for a in list(shards) + lands]
    out = pl.pallas_call(
        body, name=name,
        out_shape=[pltpu.SemaphoreType.DMA((7 * n,)), pltpu.SemaphoreType.DMA((7 * n,))]
        + [pltpu.HBM(o.shape, o.dtype) for o in ops] + [jax.ShapeDtypeStruct((8, 128), F32)],
        in_specs=[_HBM] * (2 * n),
        out_specs=[_SEM, _SEM] + [_HBM] * (2 * n) + [pl.BlockSpec(memory_space=pltpu.VMEM)],
        input_output_aliases={i: 2 + i for i in range(2 * n)},
        compiler_params=pltpu.CompilerParams(has_side_effects=pltpu.SideEffectType.DATAFLOW_SIDE_EFFECTING),
    )(*ops)
    return out[:-1], out[-1]


def gather_wait(state, after, name):
    n = (len(state) - 2) // 2

    def body(*refs):
        srcs, lands = refs[:n], refs[n:2 * n]
        send_sems, recv_sems = refs[2 * n:2 * n + 2]
        for cp in _gather_copies(srcs, lands, send_sems, recv_sems):
            cp.wait_send()
            cp.wait_recv()

    thru = list(state[2:])
    out = pl.pallas_call(
        body, name=name,
        out_shape=[pltpu.HBM(o.shape, o.dtype) for o in thru],
        in_specs=[_HBM] * (2 * n) + [_SEM, _SEM, pl.BlockSpec(memory_space=pl.ANY)],
        out_specs=[_HBM] * (2 * n),
        input_output_aliases={i: i for i in range(2 * n)},
        compiler_params=pltpu.CompilerParams(has_side_effects=pltpu.SideEffectType.DATAFLOW_SIDE_EFFECTING),
    )(*thru, state[0], state[1], after)
    return out[n:]


def _scatter_copies(grads, lands, send_sems, recv_sems):
    x, y, c = _coords()
    me = 4 * x + 2 * y + c
    copies = []
    for a in range(len(grads)):
        r = grads[a].shape[0] // 8
        for k in range(1, 8):
            px, py, pc = (1 - x if k & 4 else x, 1 - y if k & 2 else y, 1 - c if k & 1 else c)
            rows = pl.ds(pl.multiple_of((4 * px + 2 * py + pc) * r, r), r)
            copies.append(pltpu.make_async_remote_copy(
                src_ref=grads[a].at[rows], dst_ref=lands[a].at[me],
                send_sem=send_sems.at[7 * a + k - 1], recv_sem=recv_sems.at[7 * a + k - 1],
                device_id=(px, py, pc), device_id_type=MESH))
    return copies


def scatter_start(grads, name):
    n = len(grads)

    def body(*refs):
        srcs, lands = refs[:n], refs[n:2 * n]
        send_sems, recv_sems = refs[2 * n:2 * n + 2]
        token = refs[-1]
        for cp in _scatter_copies(srcs, lands, send_sems, recv_sems):
            cp.start()
        token[...] = jnp.zeros_like(token)

    lands = [lax.empty((8, g.shape[0] // 8, g.shape[1]), g.dtype) for g in grads]
    ops = [pltpu.with_memory_space_constraint(a, pltpu.HBM) for a in list(grads) + lands]
    out = pl.pallas_call(
        body, name=name,
        out_shape=[pltpu.SemaphoreType.DMA((7 * n,)), pltpu.SemaphoreType.DMA((7 * n,))]
        + [pltpu.HBM(o.shape, o.dtype) for o in ops] + [jax.ShapeDtypeStruct((8, 128), F32)],
        in_specs=[_HBM] * (2 * n),
        out_specs=[_SEM, _SEM] + [_HBM] * (2 * n) + [pl.BlockSpec(memory_space=pltpu.VMEM)],
        input_output_aliases={i: 2 + i for i in range(2 * n)},
        compiler_params=pltpu.CompilerParams(has_side_effects=pltpu.SideEffectType.DATAFLOW_SIDE_EFFECTING),
    )(*ops)
    return out[:-1], out[-1]


def scatter_wait(state, after, name):
    n = (len(state) - 2) // 2

    def body(*refs):
        srcs, lands = refs[:n], refs[n:2 * n]
        send_sems, recv_sems = refs[2 * n:2 * n + 2]
        for cp in _scatter_copies(srcs, lands, send_sems, recv_sems):
            cp.wait_send()
            cp.wait_recv()

    thru = list(state[2:])
    after = list(after) if isinstance(after, (list, tuple)) else [after]
    out = pl.pallas_call(
        body, name=name,
        out_shape=[pltpu.HBM(o.shape, o.dtype) for o in thru],
        in_specs=[_HBM] * (2 * n) + [_SEM, _SEM] + [pl.BlockSpec(memory_space=pl.ANY)] * len(after),
        out_specs=[_HBM] * (2 * n),
        input_output_aliases={i: i for i in range(2 * n)},
        compiler_params=pltpu.CompilerParams(has_side_effects=pltpu.SideEffectType.DATAFLOW_SIDE_EFFECTING),
    )(*thru, state[0], state[1], *after)
    return out[:n], out[n:]


def rs_win_add_first(g, r, sel, next_dim, col, name):
    rows, cols = r.shape[2:]

    def body(sel_ref, gk_ref, rk_ref, gs_ref, rs_ref, keep_ref, send_ref):
        keep_ref[...] = gk_ref[...] + rk_ref[...]
        send_ref[...] = (gs_ref[...] + rs_ref[...]).astype(BF16)

    def g_map(flip):
        def f(j, s):
            nxt = 1 - s[next_dim] if flip else s[next_dim]
            return (nxt, j, s[2], 0, col) if next_dim == 0 else (j, nxt, s[2], 0, col)
        return f

    def r_map(flip):
        def f(j, s):
            nxt = 1 - s[next_dim] if flip else s[next_dim]
            return (nxt, j, 0, 0) if next_dim == 0 else (j, nxt, 0, 0)
        return f

    gblk = (None, None, None, rows, cols)
    rblk = (None, None, rows, cols)
    oblk = (None, rows, cols)
    return pl.pallas_call(
        body, name=name,
        grid_spec=pltpu.PrefetchScalarGridSpec(
            num_scalar_prefetch=1, grid=(2,),
            in_specs=[pl.BlockSpec(gblk, g_map(False)), pl.BlockSpec(rblk, r_map(False)),
                      pl.BlockSpec(gblk, g_map(True)), pl.BlockSpec(rblk, r_map(True))],
            out_specs=[pl.BlockSpec(oblk, lambda j, s: (j, 0, 0)),
                       pl.BlockSpec(oblk, lambda j, s: (j, 0, 0))]),
        out_shape=[jax.ShapeDtypeStruct((2, rows, cols), F32),
                   jax.ShapeDtypeStruct((2, rows, cols), BF16)],
        compiler_params=_params(),
    )(sel, g, r, g, r)


def rs_add_second(k, r, sel, name):
    _, rows, cols = k.shape
    tr = rows // 2 if rows % 32 == 0 else rows
    nt = rows // tr

    def body(sel_ref, kk_ref, rk_ref, ks_ref, rs_ref, keep_ref, send_ref):
        keep_ref[...] = kk_ref[...] + rk_ref[...].astype(F32)
        send_ref[...] = (ks_ref[...] + rs_ref[...].astype(F32)).astype(BF16)

    blk = (None, tr, cols)
    oblk = (tr, cols)
    return pl.pallas_call(
        body, name=name,
        grid_spec=pltpu.PrefetchScalarGridSpec(
            num_scalar_prefetch=1, grid=(nt,),
            in_specs=[
                pl.BlockSpec(blk, lambda i, s: (s[0], i, 0)),
                pl.BlockSpec(blk, lambda i, s: (s[0], i, 0)),
                pl.BlockSpec(blk, lambda i, s: (1 - s[0], i, 0)),
                pl.BlockSpec(blk, lambda i, s: (1 - s[0], i, 0)),
            ],
            out_specs=[pl.BlockSpec(oblk, lambda i, s: (i, 0)),
                       pl.BlockSpec(oblk, lambda i, s: (i, 0))]),
        out_shape=[jax.ShapeDtypeStruct((rows, cols), F32),
                   jax.ShapeDtypeStruct((rows, cols), BF16)],
        compiler_params=_params(),
    )(sel, k, r, k, r)


SEG_ROWS = (4800, 5824, 6848, 4096, 0, 1024, 2048, 3072)
LAT_ROWS = QL + KVL + ROPE
N_IN = 7872


def _seg_row(j):
    return pl.multiple_of(jnp.where(j < 3, 4800 + 1024 * j, jnp.where(j == 3, 4096, (j - 4) * 1024)), 8)


def proj_matmul(h, wt_bits, token):
    t = h.shape[0]
    tm = min(2048, t)

    def body(h_ref, w_hbm, tok_ref, o_ref, wt_ref, buf, sems):
        j = pl.program_id(0)
        slot = j % 2

        def fetch(seg, into):
            return pltpu.make_async_copy(w_hbm.at[pl.ds(_seg_row(seg), D)], buf.at[into], sems.at[into])

        @pl.when(pl.program_id(1) == 0)
        def _():
            @pl.when(j == 0)
            def _():
                fetch(j, slot).start()

            fetch(j, slot).wait()

            @pl.when(j + 1 < NSEG)
            def _():
                fetch(j + 1, 1 - slot).start()

            bits = pltpu.bitcast(buf[slot], jnp.uint32)
            row = lax.broadcasted_iota(jnp.int32, (D, D // 2), 0)
            live = jnp.logical_or(j != SEG_LAT, row < LAT_ROWS)
            lo = pltpu.bitcast(bits << 16, F32)
            hi = pltpu.bitcast(bits & jnp.uint32(0xFFFF0000), F32)
            wt_ref[:, :D // 2] = jnp.where(live, lo, 0.0).astype(BF16)
            wt_ref[:, D // 2:] = jnp.where(live, hi, 0.0).astype(BF16)

        o_ref[...] = _dot_nt(h_ref[...], wt_ref[...]).astype(BF16)

    return pl.pallas_call(
        body, name="proj_matmul", grid=(NSEG, t // tm),
        in_specs=[pl.BlockSpec((tm, D), lambda j, i: (i, 0)),
                  pl.BlockSpec(memory_space=pl.ANY),
                  pl.BlockSpec((8, 128), lambda j, i: (0, 0))],
        out_specs=[pl.BlockSpec((None, tm, D), lambda j, i: (j, i, 0)),
                   pl.BlockSpec((D, D), lambda j, i: (j, 0))],
        out_shape=[jax.ShapeDtypeStruct((NSEG, t, D), BF16), jax.ShapeDtypeStruct((NP, D), BF16)],
        scratch_shapes=[pltpu.VMEM((2, D, D // 2), F32), pltpu.SemaphoreType.DMA((2,))],
        compiler_params=_params(("arbitrary", "arbitrary")),
    )(h, wt_bits, token)


def dh_matmul(dproj, wt, token, seq, b):
    tm = min(1024, seq)
    nblk = seq // tm

    per = 2

    def body(b_ref, d_ref, w_ref, tok_ref, o_ref, acc_ref):
        k = pl.program_id(1)
        last = NSEG // per - 1

        def part():
            p = _dot(d_ref[0], w_ref[0:D, :])
            for j in range(1, per):
                p = p + _dot(d_ref[j], w_ref[j * D:(j + 1) * D, :])
            return p

        @pl.when(k == 0)
        def _():
            acc_ref[...] = part()

        @pl.when(jnp.logical_and(k > 0, k < last))
        def _():
            acc_ref[...] += part()

        @pl.when(k == last)
        def _():
            o_ref[...] = acc_ref[...] + part()

    return pl.pallas_call(
        body, name="dh_matmul",
        grid_spec=pltpu.PrefetchScalarGridSpec(
            num_scalar_prefetch=1, grid=(nblk, NSEG // per),
            in_specs=[pl.BlockSpec((per, tm, D), lambda i, k, s: (k, s[0] * nblk + i, 0)),
                      pl.BlockSpec((per * D, D), lambda i, k, s: (k, 0)),
                      pl.BlockSpec((8, 128), lambda i, k, s: (0, 0))],
            out_specs=pl.BlockSpec((tm, D), lambda i, k, s: (i, 0)),
            scratch_shapes=[pltpu.VMEM((tm, D), F32)]),
        out_shape=jax.ShapeDtypeStruct((seq, D), F32),
        compiler_params=_params(("parallel", "arbitrary")),
    )(jnp.full((1,), b, jnp.int32), dproj, wt, token)


def win_grad_matmul(h, dproj, token):
    t = h.shape[0]

    def body(h_ref, d_ref, tok_ref, o_hbm, acc_ref, sems):
        j = pl.program_id(0)

        def out_copy(jj, action):
            slot = lax.rem(jj, 2)

            @pl.when(jj != SEG_LAT)
            def _():
                action(pltpu.make_async_copy(acc_ref.at[slot], o_hbm.at[pl.ds(_seg_row(jj), D)],
                                             sems.at[slot]))

            @pl.when(jj == SEG_LAT)
            def _():
                action(pltpu.make_async_copy(acc_ref.at[slot, pl.ds(0, LAT_ROWS)],
                                             o_hbm.at[pl.ds(SEG_ROWS[SEG_LAT], LAT_ROWS)], sems.at[slot]))

        acc_ref[lax.rem(j, 2)] = _dot_tn(d_ref[...], h_ref[...])
        out_copy(j, lambda cp: cp.start())

        @pl.when(j > 0)
        def _():
            out_copy(j - 1, lambda cp: cp.wait())

        @pl.when(j == NSEG - 1)
        def _():
            out_copy(j, lambda cp: cp.wait())

    return pl.pallas_call(
        body, name="win_grad_matmul", grid=(NSEG,),
        in_specs=[pl.BlockSpec((t, D), lambda j: (0, 0)),
                  pl.BlockSpec((None, t, D), lambda j: (j, 0, 0)),
                  pl.BlockSpec((8, 128), lambda j: (0, 0))],
        out_specs=pl.BlockSpec(memory_space=pl.ANY),
        out_shape=jax.ShapeDtypeStruct((N_IN, D), F32),
        scratch_shapes=[pltpu.VMEM((2, D, D), F32), pltpu.SemaphoreType.DMA((2,))],
        compiler_params=_params(("arbitrary",)),
    )(h, dproj, token)


def grad_matmul(a, b, name):
    t, m = a.shape
    n = b.shape[1]
    tk = min(1024, t)
    nk = t // tk

    def body(a_ref, b_ref, o_ref, acc_ref):
        k = pl.program_id(0)
        part = lambda: _dot_tn(a_ref[...], b_ref[...])
        if nk == 1:
            o_ref[...] = part().astype(BF16)
            return

        @pl.when(k == 0)
        def _():
            acc_ref[...] = part()

        @pl.when(jnp.logical_and(k > 0, k < nk - 1))
        def _():
            acc_ref[...] += part()

        @pl.when(k == nk - 1)
        def _():
            o_ref[...] = (acc_ref[...] + part()).astype(BF16)

    return pl.pallas_call(
        body, name=name, grid=(nk,),
        in_specs=[pl.BlockSpec((tk, m), lambda k: (k, 0)),
                  pl.BlockSpec((tk, n), lambda k: (k, 0))],
        out_specs=pl.BlockSpec((m, n), lambda k: (0, 0)),
        out_shape=jax.ShapeDtypeStruct((m, n), BF16),
        scratch_shapes=[pltpu.VMEM((m, n), F32)],
        compiler_params=_params(("arbitrary",)),
    )(a, b)


def ada_gather(c8, taps8, w_ada, b_cols):
    cols = w_ada.shape[1]

    def body(c_ref, t_ref, w_ref, b_ref, call_ref, tall_ref, mod_ref, part_ref, send_sems, recv_sems):
        x, y, c = _coords()
        me = 4 * x + 2 * y + c
        peers = [(1 - x if k & 4 else x, 1 - y if k & 2 else y, 1 - c if k & 1 else c) for k in range(1, 8)]

        def rcopy(n, src, dst, peer):
            return pltpu.make_async_remote_copy(src_ref=src, dst_ref=dst, send_sem=send_sems.at[n],
                                                recv_sem=recv_sems.at[n], device_id=peer, device_id_type=MESH)

        call_ref[me] = c_ref[...]
        tall_ref[me] = t_ref[...]
        first = []
        for k, peer in enumerate(peers):
            first += [rcopy(k, c_ref, call_ref.at[me], peer), rcopy(7 + k, t_ref, tall_ref.at[me], peer)]
        for cp in first:
            cp.start()
        for cp in first:
            cp.wait()
        rows = call_ref[...].reshape(64, D).astype(BF16)
        part_ref[...] = _dot(rows, w_ref[...].astype(BF16)) + b_ref[...]
        mod_ref[me] = part_ref[pl.ds(pl.multiple_of(8 * me, 8), 8), :]
        second = []
        for k, (px, py, pc) in enumerate(peers):
            theirs = part_ref.at[pl.ds(pl.multiple_of(8 * (4 * px + 2 * py + pc), 8), 8)]
            second.append(rcopy(14 + k, theirs, mod_ref.at[me], (px, py, pc)))
        for cp in second:
            cp.start()
        for cp in second:
            cp.wait()

    vm = pl.BlockSpec(memory_space=pltpu.VMEM)
    return pl.pallas_call(
        body, name="ada_gather",
        out_shape=[jax.ShapeDtypeStruct((8, 8, D), F32), jax.ShapeDtypeStruct((8, 8, 128), F32),
                   jax.ShapeDtypeStruct((8, 8, cols), F32)],
        in_specs=[vm] * 4, out_specs=[vm] * 3,
        scratch_shapes=[pltpu.VMEM((64, cols), F32), pltpu.SemaphoreType.DMA((21,)),
                        pltpu.SemaphoreType.DMA((21,))],
        compiler_params=_params(),
    )(c8, taps8, w_ada, b_cols)


def ada_bwd(c_all, dmod_cols):
    def body(c_ref, d_ref, o_ref):
        o_ref[...] = _dot_tn(c_ref[...].astype(BF16), d_ref[...].astype(BF16))

    return pl.pallas_call(
        body, name="ada_bwd",
        out_shape=jax.ShapeDtypeStruct((c_all.shape[1], dmod_cols.shape[1]), F32),
        compiler_params=_params(),
    )(c_all, dmod_cols)


def slot_sum(g):
    def body(g_ref, o_ref):
        acc = g_ref[0]
        for s in range(1, 8):
            acc = acc + g_ref[s]
        o_ref[...] = acc

    return pl.pallas_call(
        body, name="slot_sum",
        out_shape=jax.ShapeDtypeStruct(g.shape[1:], F32),
    )(g)


def prenorm_fwd(x2, scale, shift, g_pre, seq):
    t = x2.shape[0]
    tm = min(512, seq)
    tpb = seq // tm

    def body(x_ref, sc_ref, sh_ref, g_ref, h_ref):
        xv = x_ref[...]
        r = lax.rsqrt(jnp.mean(xv * xv, axis=-1, keepdims=True) + EPS)
        hv = (xv * r * g_ref[...]) * (1.0 + sc_ref[...]) + sh_ref[...]
        h_ref[...] = hv.astype(BF16)

    per_batch = pl.BlockSpec((None, 1, D), lambda i: (i // tpb, 0, 0))
    return pl.pallas_call(
        body, name="prenorm_fwd", grid=(t // tm,),
        in_specs=[pl.BlockSpec((tm, D), lambda i: (i, 0)), per_batch, per_batch,
                  pl.BlockSpec((1, D), lambda i: (0, 0))],
        out_specs=pl.BlockSpec((tm, D), lambda i: (i, 0)),
        out_shape=jax.ShapeDtypeStruct((t, D), BF16),
        compiler_params=_params(("parallel",)),
    )(x2, scale, shift, g_pre)


def prenorm_bwd(dh, x2, dout, scale, g_pre, seq, token, b, gx_prev):
    t = x2.shape[0]
    tm = min(512, seq)
    tpb = seq // tm
    if gx_prev is None:
        gx_prev = lax.empty((t, D), F32)

    def body(b_ref, dh_ref, x_ref, do_ref, sc_ref, g_ref, tok_ref, gxp_ref, gx_ref, dsh_ref, dsc_ref, dg_ref):
        i = pl.program_id(0)
        xv = x_ref[...]
        dhv = dh_ref[...]
        g = g_ref[...]
        r = lax.rsqrt(jnp.mean(xv * xv, axis=-1, keepdims=True) + EPS)
        nrm = xv * r
        dxn = dhv * (1.0 + sc_ref[...])
        dn = dxn * g
        dx = r * (dn - nrm * jnp.mean(dn * nrm, axis=-1, keepdims=True))
        gx_ref[...] = dx + do_ref[...]

        @pl.when(i == 0)
        def _():
            dsh_ref[...] = jnp.zeros_like(dsh_ref)
            dsc_ref[...] = jnp.zeros_like(dsc_ref)
            dg_ref[...] = jnp.zeros_like(dg_ref)

        dsh_ref[...] += jnp.sum(dhv, axis=0, keepdims=True)
        dsc_ref[...] += jnp.sum(dhv * (nrm * g), axis=0, keepdims=True)
        dg_ref[...] += jnp.sum(dxn * nrm, axis=0, keepdims=True)

    row = pl.BlockSpec((tm, D), lambda i, s: (i, 0))
    grow = pl.BlockSpec((tm, D), lambda i, s: (s[0] * tpb + i, 0))
    per_batch = pl.BlockSpec((None, 1, D), lambda i, s: (s[0], 0, 0))
    vec = pl.BlockSpec((1, D), lambda i, s: (0, 0))
    return pl.pallas_call(
        body, name="prenorm_bwd",
        grid_spec=pltpu.PrefetchScalarGridSpec(
            num_scalar_prefetch=1, grid=(tpb,),
            in_specs=[row, grow, grow, per_batch, vec, pl.BlockSpec((8, 128), lambda i, s: (0, 0)),
                      pl.BlockSpec(memory_space=pl.ANY)],
            out_specs=[grow, vec, vec, vec]),
        out_shape=[jax.ShapeDtypeStruct((t, D), F32), jax.ShapeDtypeStruct((1, D), F32),
                   jax.ShapeDtypeStruct((1, D), F32), jax.ShapeDtypeStruct((1, D), F32)],
        input_output_aliases={7: 0},
        compiler_params=_params(("arbitrary",)),
    )(jnp.full((1,), b, jnp.int32), dh, x2, dout, scale, g_pre, token, gx_prev)


CONV_TC = 128


def _shift_down(u, k, rows):
    idx = lax.broadcasted_iota(jnp.int32, u.shape, 0)
    return jnp.where(idx >= k, pltpu.roll(u, k, 0), 0.0)


def _shift_up(u, k, rows):
    idx = lax.broadcasted_iota(jnp.int32, u.shape, 0)
    return jnp.where(idx < rows - k, pltpu.roll(u, rows - k, 0), 0.0)


def conv_fwd(proj, conv_w, seq):
    t = proj.shape[1]
    nb = t // seq

    def body(p_ref, w_ref, y_ref):
        av = p_ref[0].astype(F32)
        ab = p_ref[1].astype(F32)
        ac = p_ref[2].astype(F32)
        az = p_ref[3].astype(F32)
        w = w_ref[...]
        u = ac * av
        y1 = _shift_down(u, 2, seq) * w[0:1] + _shift_down(u, 1, seq) * w[1:2] + u * w[2:3]
        y_ref[...] = (ab * y1 * (az * _sig(az))).astype(BF16)

    return pl.pallas_call(
        body, name="conv_fwd", grid=(nb, D // CONV_TC),
        in_specs=[pl.BlockSpec((4, seq, CONV_TC), lambda b, ci: (1, b, ci)),
                  pl.BlockSpec((8, CONV_TC), lambda b, ci: (0, ci))],
        out_specs=pl.BlockSpec((seq, CONV_TC), lambda b, ci: (b, ci)),
        out_shape=jax.ShapeDtypeStruct((t, D), BF16),
        compiler_params=_params(("parallel", "parallel")),
    )(proj, conv_w)


def conv_bwd(dproj, proj, dy, conv_w, seq):
    t = proj.shape[1]
    nb = t // seq

    def body(dp_in_ref, p_ref, dy_ref, w_ref, dp_ref, dw_ref):
        b = pl.program_id(1)
        av = p_ref[0].astype(F32)
        ab = p_ref[1].astype(F32)
        ac = p_ref[2].astype(F32)
        az = p_ref[3].astype(F32)
        dyv = dy_ref[...].astype(F32)
        w = w_ref[...]
        u = ac * av
        u1 = _shift_down(u, 1, seq)
        u2 = _shift_down(u, 2, seq)
        y1 = u2 * w[0:1] + u1 * w[1:2] + u * w[2:3]
        sz = _sig(az)
        silu = az * sz
        dy1 = dyv * ab * silu
        du = dy1 * w[2:3] + _shift_up(dy1, 1, seq) * w[1:2] + _shift_up(dy1, 2, seq) * w[0:1]
        dp_ref[0] = (du * ac).astype(BF16)
        dp_ref[1] = (dyv * y1 * silu).astype(BF16)
        dp_ref[2] = (du * av).astype(BF16)
        dp_ref[3] = (dyv * ab * y1 * (sz * (1.0 + az * (1.0 - sz)))).astype(BF16)

        @pl.when(b == 0)
        def _():
            dw_ref[...] = jnp.zeros_like(dw_ref)

        dw_ref[0:1, :] += jnp.sum(dy1 * u2, axis=0, keepdims=True)
        dw_ref[1:2, :] += jnp.sum(dy1 * u1, axis=0, keepdims=True)
        dw_ref[2:3, :] += jnp.sum(dy1 * u, axis=0, keepdims=True)

    return pl.pallas_call(
        body, name="conv_bwd", grid=(D // CONV_TC, nb),
        in_specs=[pl.BlockSpec(memory_space=pl.ANY),
                  pl.BlockSpec((4, seq, CONV_TC), lambda ci, b: (1, b, ci)),
                  pl.BlockSpec((seq, CONV_TC), lambda ci, b: (b, ci)),
                  pl.BlockSpec((8, CONV_TC), lambda ci, b: (0, ci))],
        out_specs=[pl.BlockSpec((4, seq, CONV_TC), lambda ci, b: (1, b, ci)),
                   pl.BlockSpec((8, CONV_TC), lambda ci, b: (0, ci))],
        out_shape=[jax.ShapeDtypeStruct(dproj.shape, BF16),
                   jax.ShapeDtypeStruct((8, D), F32)],
        input_output_aliases={0: 0},
        compiler_params=_params(("parallel", "arbitrary")),
    )(dproj, proj, dy, conv_w)


def _rope_tables(pos_ref, invf_ref, ma_ref, mb_ref, sign):
    ang = pos_ref[...].astype(F32) * invf_ref[...]
    cs = jnp.cos(ang)
    sn = jnp.sin(ang) * sign
    return cs, sn * ma_ref[...], sn * mb_ref[...]


def _rotate(v, cs, sa, sb):
    return v * cs + pltpu.roll(v, 128 - HALF, 1) * sa + pltpu.roll(v, HALF, 1) * sb


MLA_TM = 512


def mla_prep_fwd(proj, pos, g_q, g_kv, wuq, wukv, tabs):
    t = proj.shape[1]
    tm = min(MLA_TM, t)

    def body(lat_ref, pos_ref, gq_ref, gkv_ref, wuq_ref, wukv_ref, invf_ref, ma_ref, mb_ref,
             q_ref, k_ref, kv_ref, qn_ref, kvn_ref):
        lat = lat_ref[...].astype(F32)
        ql = lat[:, :QL]
        kl = lat[:, QL:QL + KVL]
        kr = lat[:, QL + KVL:QL + KVL + 128]
        qn = (ql * lax.rsqrt(jnp.mean(ql * ql, axis=-1, keepdims=True) + EPS) * gq_ref[...]).astype(BF16)
        kvn = (kl * lax.rsqrt(jnp.mean(kl * kl, axis=-1, keepdims=True) + EPS) * gkv_ref[...]).astype(BF16)
        qn_ref[...] = qn
        kvn_ref[...] = kvn
        cs, sa, sb = _rope_tables(pos_ref, invf_ref, ma_ref, mb_ref, 1.0)
        q = _dot_nt(qn, wuq_ref[...]) * (SM_SCALE * LOG2E)
        kv = _dot_nt(kvn, wukv_ref[...]).astype(BF16)
        kv_ref[...] = kv
        kpe = _rotate(kr, cs, sa, sb).astype(BF16)
        for hh in range(H):
            lo, mid, hi = hh * DQK, hh * DQK + 128, (hh + 1) * DQK
            q_ref[:, lo:mid] = q[:, lo:mid].astype(BF16)
            q_ref[:, mid:hi] = _rotate(q[:, mid:hi], cs, sa, sb).astype(BF16)
            k_ref[:, lo:mid] = kv[:, lo:mid]
            k_ref[:, mid:hi] = kpe

    row = lambda w: pl.BlockSpec((tm, w), lambda i: (i, 0))
    const = lambda a: pl.BlockSpec(a.shape, lambda i: (0,) * a.ndim)
    return pl.pallas_call(
        body, name="mla_prep_fwd", grid=(t // tm,),
        in_specs=[pl.BlockSpec((None, tm, D), lambda i: (SEG_LAT, i, 0)), row(1),
                  const(g_q), const(g_kv), const(wuq), const(wukv)] + [const(a) for a in tabs],
        out_specs=[row(H * DQK), row(H * DQK), row(H * DQK), row(QL), row(KVL)],
        out_shape=[jax.ShapeDtypeStruct((t, H * DQK), BF16)] * 3
        + [jax.ShapeDtypeStruct((t, QL), BF16), jax.ShapeDtypeStruct((t, KVL), BF16)],
        compiler_params=_params(("parallel",)),
    )(proj, pos, g_q, g_kv, wuq, wukv, *tabs)


def mla_prep_bwd(dproj, proj, dq_rot, dk, dv, pos, g_q, g_kv, wuq, wukv, tabs):
    t = proj.shape[1]
    tm = min(MLA_TM, t)

    def body(dp_in_ref, lat_ref, dqr_ref, dk_ref, dv_ref, pos_ref, gq_ref, gkv_ref, wuq_ref, wukv_ref,
             invf_ref, ma_ref, mb_ref, dp_ref, dq_ref, dkv_ref, dgq_ref, dgkv_ref):
        i = pl.program_id(0)
        lat = lat_ref[...].astype(F32)
        ql = lat[:, :QL]
        kl = lat[:, QL:QL + KVL]
        rq = lax.rsqrt(jnp.mean(ql * ql, axis=-1, keepdims=True) + EPS)
        rk = lax.rsqrt(jnp.mean(kl * kl, axis=-1, keepdims=True) + EPS)
        nq = ql * rq
        nk = kl * rk
        cs, sa, sb = _rope_tables(pos_ref, invf_ref, ma_ref, mb_ref, -1.0)
        dkpe = jnp.zeros((tm, 128), F32)
        for hh in range(H):
            lo, mid, hi = hh * DQK, hh * DQK + 128, (hh + 1) * DQK
            dq_ref[:, lo:mid] = (dqr_ref[:, lo:mid] * SM_SCALE).astype(BF16)
            dq_ref[:, mid:hi] = _rotate(dqr_ref[:, mid:hi] * SM_SCALE, cs, sa, sb).astype(BF16)
            dkv_ref[:, lo:mid] = dk_ref[:, lo:mid]
            dkv_ref[:, mid:hi] = dv_ref[:, hh * DV:(hh + 1) * DV]
            dkpe = dkpe + dk_ref[:, mid:hi].astype(F32)
        lane = lax.broadcasted_iota(jnp.int32, (tm, 128), 1)
        dkr = jnp.where(lane < ROPE, _rotate(dkpe, cs, sa, sb), 0.0)
        dqn = _dot(dq_ref[...], wuq_ref[...])
        dkvn = _dot(dkv_ref[...], wukv_ref[...])
        gq = gq_ref[...]
        gkv = gkv_ref[...]
        dnq = dqn * gq
        dnk = dkvn * gkv
        dql = rq * (dnq - nq * jnp.mean(dnq * nq, axis=-1, keepdims=True))
        dkl = rk * (dnk - nk * jnp.mean(dnk * nk, axis=-1, keepdims=True))
        dp_ref[:, :QL] = dql.astype(BF16)
        dp_ref[:, QL:QL + KVL] = dkl.astype(BF16)
        dp_ref[:, QL + KVL:QL + KVL + 128] = dkr.astype(BF16)
        dp_ref[:, QL + KVL + 128:] = jnp.zeros((tm, D - QL - KVL - 128), BF16)

        @pl.when(i == 0)
        def _():
            dgq_ref[...] = jnp.zeros_like(dgq_ref)
            dgkv_ref[...] = jnp.zeros_like(dgkv_ref)

        dgq_ref[...] += jnp.sum(dqn * nq, axis=0, keepdims=True)
        dgkv_ref[...] += jnp.sum(dkvn * nk, axis=0, keepdims=True)

    row = lambda w: pl.BlockSpec((tm, w), lambda i: (i, 0))
    const = lambda a: pl.BlockSpec(a.shape, lambda i: (0,) * a.ndim)
    seg = pl.BlockSpec((None, tm, D), lambda i: (SEG_LAT, i, 0))
    return pl.pallas_call(
        body, name="mla_prep_bwd", grid=(t // tm,),
        in_specs=[pl.BlockSpec(memory_space=pl.ANY), seg, row(H * DQK), row(H * DQK), row(H * DV), row(1),
                  const(g_q), const(g_kv), const(wuq), const(wukv)] + [const(a) for a in tabs],
        out_specs=[seg, row(H * DQK), row(H * DQK),
                   pl.BlockSpec((1, QL), lambda i: (0, 0)), pl.BlockSpec((1, KVL), lambda i: (0, 0))],
        out_shape=[jax.ShapeDtypeStruct(dproj.shape, BF16),
                   jax.ShapeDtypeStruct((t, H * DQK), BF16), jax.ShapeDtypeStruct((t, H * DQK), BF16),
                   jax.ShapeDtypeStruct((1, QL), F32), jax.ShapeDtypeStruct((1, KVL), F32)],
        input_output_aliases={0: 0},
        compiler_params=_params(("arbitrary",)),
    )(dproj, proj, dq_rot, dk, dv, pos, g_q, g_kv, wuq, wukv, *tabs)


def _causal_mask(s, shift):
    row = lax.broadcasted_iota(jnp.int32, s.shape, 0)
    col = lax.broadcasted_iota(jnp.int32, s.shape, 1)
    return jnp.where(col <= row + shift, s, -1e30)


def flash_fwd(q, k, kv, nb, seq):
    t = q.shape[0]
    tq = min(FLASH_TQ, seq // 2)
    nq = seq // tq
    assert nq % 2 == 0, "blocks are processed in pairs"

    def update(state, s, vblk):
        m, l, acc = state
        m_new = jnp.maximum(m, jnp.max(s, axis=1, keepdims=True))
        p = jnp.exp2(s - m_new)
        alpha = jnp.exp2(m - m_new)
        return (m_new, alpha * l + jnp.sum(p, axis=1, keepdims=True),
                alpha * acc + _dot(p.astype(BF16), vblk))

    def finish(state, rows, o_ref, lse_ref):
        m, l, acc = state
        o_ref[rows, :] = (acc / l).astype(BF16)
        lse_ref[rows, :] = jnp.broadcast_to(m + jnp.log(l) * LOG2E, (m.shape[0], DV))

    def body(q_ref, k_ref, v_ref, o_ref, lse_ref):
        for qp in range(0, nq, 2):
            rows = 2 * tq
            q0 = qp * tq
            qv = q_ref[q0:q0 + rows, :]
            state = (jnp.full((rows, 1), -1e30, F32), jnp.zeros((rows, 1), F32), jnp.zeros((rows, DV), F32))
            for j in range(0, qp, 2):
                ks = slice(j * tq, (j + 2) * tq)
                state = update(state, _dot_nt(qv, k_ref[ks, :]), v_ref[ks, :])
            ks = slice(q0, q0 + tq)
            state = update(state, _causal_mask(_dot_nt(qv, k_ref[ks, :]), 0), v_ref[ks, :])
            finish(tuple(a[:tq] for a in state), slice(q0, q0 + tq), o_ref, lse_ref)
            ks = slice(q0 + tq, q0 + 2 * tq)
            low = tuple(a[tq:] for a in state)
            low = update(low, _causal_mask(_dot_nt(qv[tq:], k_ref[ks, :]), 0), v_ref[ks, :])
            finish(low, slice(q0 + tq, q0 + 2 * tq), o_ref, lse_ref)

    out_blk = pl.BlockSpec((seq, DV), lambda b, h: (b, h))
    return pl.pallas_call(
        body, name="flash_fwd", grid=(nb, H),
        in_specs=[pl.BlockSpec((seq, DQK), lambda b, h: (b, h)),
                  pl.BlockSpec((seq, DQK), lambda b, h: (b, h)),
                  pl.BlockSpec((seq, DV), lambda b, h: (b, 2 * h + 1))],
        out_specs=[out_blk, out_blk],
        out_shape=[jax.ShapeDtypeStruct((t, H * DV), BF16), jax.ShapeDtypeStruct((t, H * DV), F32)],
        compiler_params=_params(("parallel", "parallel")),
    )(q, k, kv)


def flash_bwd(q, k, kv, o, do, lse, nb, seq, token):
    t = q.shape[0]
    tq = min(FLASH_TQ, seq)
    nq = seq // tq

    def body(q_ref, k_ref, v_ref, o_ref, do_ref, lse_ref, tok_ref, dq_ref, dk_ref, dv_ref):
        delta, lse = [], []
        for qi in range(nq):
            qs = slice(qi * tq, (qi + 1) * tq)
            dl = jnp.sum(do_ref[qs, :].astype(F32) * o_ref[qs, :].astype(F32), axis=1, keepdims=True)
            delta.append(jnp.broadcast_to(dl, (tq, DV)).T[:1, :])
            lse.append(lse_ref[qs, :].T[:1, :])
        for ki in range(nq):
            ks = slice(ki * tq, (ki + 1) * tq)
            kb = k_ref[ks, :]
            vb = v_ref[ks, :]
            dk = jnp.zeros((tq, DQK), F32)
            dv = jnp.zeros((tq, DV), F32)
            for qi in range(ki, nq):
                qs = slice(qi * tq, (qi + 1) * tq)
                qv = q_ref[qs, :]
                dov = do_ref[qs, :]
                st = _dot_nt(kb, qv)
                if qi == ki:
                    row = lax.broadcasted_iota(jnp.int32, st.shape, 0)
                    col = lax.broadcasted_iota(jnp.int32, st.shape, 1)
                    st = jnp.where(row <= col, st, -1e30)
                pt = jnp.exp2(st - lse[qi])
                dpt = _dot_nt(vb, dov)
                dzt = (pt * (dpt - delta[qi])).astype(BF16)
                dv = dv + _dot(pt.astype(BF16), dov)
                dk = dk + _dot(dzt, qv)
                dqb = _dot_tn(dzt, kb)
                if ki == 0:
                    dq_ref[qs, :] = dqb
                else:
                    dq_ref[qs, :] += dqb
            dk_ref[ks, :] = (dk * LN2).astype(BF16)
            dv_ref[ks, :] = dv.astype(BF16)

    full = lambda w, col: pl.BlockSpec((seq, w), col)
    same = lambda b, h: (b, h)
    return pl.pallas_call(
        body, name="flash_bwd", grid=(nb, H),
        in_specs=[full(DQK, same), full(DQK, same), full(DV, lambda b, h: (b, 2 * h + 1)),
                  full(DV, same), full(DV, same), full(DV, same),
                  pl.BlockSpec((8, 128), lambda b, h: (0, 0))],
        out_specs=[full(DQK, same), full(DQK, same), full(DV, same)],
        out_shape=[jax.ShapeDtypeStruct((t, H * DQK), F32), jax.ShapeDtypeStruct((t, H * DQK), BF16),
                   jax.ShapeDtypeStruct((t, H * DV), BF16)],
        compiler_params=_params(("parallel", "parallel")),
    )(q, k, kv, o, do, lse, token)


TAIL_TM = 512


def tail_fwd(y, attn, proj, x2, tgt, gate, g_post, wco, wmo, wout, seq):
    t = y.shape[0]
    nb = t // seq
    tm = min(TAIL_TM, seq)
    tpb = seq // tm

    def body(y_ref, at_ref, p_ref, x_ref, t_ref, gate_ref, gp_ref, wco_ref, wmo_ref, wout_ref,
             o_ref, ya_ref, yb_ref, m_ref, do2_ref, dout_ref, dgate_ref, dgp_ref, loss_ref):
        i = pl.program_id(0)
        bz = p_ref[0].astype(F32)
        ga = p_ref[1].astype(F32)
        gb = p_ref[2].astype(F32)
        ov = (at_ref[...].astype(F32) * (bz * _sig(bz))).astype(BF16)
        o_ref[...] = ov
        ya = _dot(y_ref[...], wco_ref[...])
        yb = _dot(ov, wmo_ref[...])
        ya_ref[...] = ya.astype(BF16)
        yb_ref[...] = yb.astype(BF16)
        mv = (_sig(ga) * ya + _sig(gb) * yb).astype(BF16)
        m_ref[...] = mv
        o2 = _dot(mv, wout_ref[...])
        r = lax.rsqrt(jnp.mean(o2 * o2, axis=-1, keepdims=True) + EPS)
        nrm = o2 * r
        gp = gp_ref[...]
        gate_v = gate_ref[...]
        rn = nrm * gp
        err = x_ref[...] + gate_v * rn - t_ref[...]
        dout = err * (1.0 / D)
        dout_ref[...] = dout
        dn = dout * gate_v * gp
        do2_ref[...] = (r * (dn - nrm * jnp.mean(dn * nrm, axis=-1, keepdims=True))).astype(BF16)

        @pl.when(i % tpb == 0)
        def _():
            dgate_ref[...] = jnp.zeros_like(dgate_ref)

        @pl.when(i == 0)
        def _():
            dgp_ref[...] = jnp.zeros_like(dgp_ref)
            loss_ref[...] = jnp.zeros_like(loss_ref)

        dgate_ref[...] += jnp.sum(dout * rn, axis=0, keepdims=True)
        dgp_ref[...] += jnp.sum(dout * gate_v * nrm, axis=0, keepdims=True)
        loss_ref[...] += 0.5 * jnp.sum(jnp.mean(err * err, axis=-1, keepdims=True), axis=0, keepdims=True)

    row = pl.BlockSpec((tm, D), lambda i: (i, 0))
    per_batch = pl.BlockSpec((None, 1, D), lambda i: (i // tpb, 0, 0))
    vec = pl.BlockSpec((1, D), lambda i: (0, 0))
    wgt = pl.BlockSpec((D, D), lambda i: (0, 0))
    act = jax.ShapeDtypeStruct((t, D), BF16)
    return pl.pallas_call(
        body, name="tail_fwd", grid=(t // tm,),
        in_specs=[row, row, pl.BlockSpec((3, tm, D), lambda i: (0, i, 0)), row, row, per_batch, vec,
                  wgt, wgt, wgt],
        out_specs=[row, row, row, row, row, row, per_batch, vec, pl.BlockSpec((1, 1), lambda i: (0, 0))],
        out_shape=[act, act, act, act, act, jax.ShapeDtypeStruct((t, D), F32),
                   jax.ShapeDtypeStruct((nb, 1, D), F32), jax.ShapeDtypeStruct((1, D), F32),
                   jax.ShapeDtypeStruct((1, 1), F32)],
        compiler_params=_params(("arbitrary",)),
    )(y, attn, proj, x2, tgt, gate, g_post, wco, wmo, wout)


def tail_bwd(do2, proj, ya, yb, attn, wout, wmo, wco):
    t = do2.shape[0]
    tm = min(TAIL_TM, t)

    def body(do2_ref, p_ref, ya_ref, yb_ref, at_ref, wout_ref, wmo_ref, wco_ref,
             dp_ref, dya_ref, dyb_ref, dat_ref, dy_ref):
        bz = p_ref[0].astype(F32)
        ga = p_ref[1].astype(F32)
        gb = p_ref[2].astype(F32)
        dm = _dot_nt(do2_ref[...], wout_ref[...])
        sa = _sig(ga)
        sb = _sig(gb)
        dya = (dm * sa).astype(BF16)
        dyb = (dm * sb).astype(BF16)
        dya_ref[...] = dya
        dyb_ref[...] = dyb
        dp_ref[1] = (dm * ya_ref[...].astype(F32) * (sa * (1.0 - sa))).astype(BF16)
        dp_ref[2] = (dm * yb_ref[...].astype(F32) * (sb * (1.0 - sb))).astype(BF16)
        dov = _dot_nt(dyb, wmo_ref[...])
        sz = _sig(bz)
        dat_ref[...] = (dov * (bz * sz)).astype(BF16)
        dp_ref[0] = (dov * at_ref[...].astype(F32) * (sz * (1.0 + bz * (1.0 - sz)))).astype(BF16)
        dy_ref[...] = _dot_nt(dya, wco_ref[...]).astype(BF16)

    row = pl.BlockSpec((tm, D), lambda i: (i, 0))
    seg3 = pl.BlockSpec((3, tm, D), lambda i: (0, i, 0))
    wgt = pl.BlockSpec((D, D), lambda i: (0, 0))
    act = jax.ShapeDtypeStruct((t, D), BF16)
    return pl.pallas_call(
        body, name="tail_bwd", grid=(t // tm,),
        in_specs=[row, seg3, row, row, row, wgt, wgt, wgt],
        out_specs=[seg3, row, row, row, row],
        out_shape=[jax.ShapeDtypeStruct((NSEG, t, D), BF16), act, act, act, act],
        compiler_params=_params(("parallel",)),
    )(do2, proj, ya, yb, attn, wout, wmo, wco)


def _adam_update(w, m, v, grad):
    mn = ADAM_B1 * m + (1.0 - ADAM_B1) * grad
    vn = ADAM_B2 * v + (1.0 - ADAM_B2) * (grad * grad)
    m_hat = mn / (1.0 - ADAM_B1 ** ADAM_STEP)
    v_hat = vn / (1.0 - ADAM_B2 ** ADAM_STEP)
    return -ADAM_LR * (m_hat / (jnp.sqrt(v_hat) + ADAM_EPS) + ADAM_WD * w), mn, vn


def adamw(w, m, v, g, name, token):
    rows, cols = w.shape
    tr = rows
    for cand in (256, 128, 64, 32, 16, 8):
        if rows % cand == 0 and rows > cand:
            tr = cand
            break

    def body(w_ref, m_ref, v_ref, g_ref, tok_ref, d_ref, mo_ref, vo_ref):
        d_ref[...], mo_ref[...], vo_ref[...] = _adam_update(w_ref[...], m_ref[...], v_ref[...], g_ref[...])

    blk = pl.BlockSpec((tr, cols), lambda i: (i, 0))
    return pl.pallas_call(
        body, name=name, grid=(rows // tr,),
        in_specs=[blk] * 4 + [pl.BlockSpec((8, 128), lambda i: (0, 0))], out_specs=[blk] * 3,
        out_shape=[jax.ShapeDtypeStruct((rows, cols), F32)] * 3,
        compiler_params=_params(("parallel",)),
    )(w, m, v, g, token)


def adamw_scattered(w, m, v, own, land, me, tr, name, transpose=False):
    slot_rows = land.shape[1]
    cols = land.shape[2]
    rows = slot_rows if transpose else w.shape[0]
    per_slot = slot_rows // tr

    def body(me_ref, w_ref, m_ref, v_ref, own_ref, land_ref, go_ref, d_ref, mo_ref, vo_ref):
        grad = own_ref[...].astype(F32)
        for s in range(8):
            grad = grad + jnp.where(me_ref[0] == s, 0.0, land_ref[s].astype(F32))
        if transpose:
            grad = grad.T
        go_ref[...] = grad
        d_ref[...], mo_ref[...], vo_ref[...] = _adam_update(w_ref[...], m_ref[...], v_ref[...], grad)

    wblk = pl.BlockSpec(w.shape if transpose else (tr, w.shape[1]), lambda i, s: (i, 0))
    return pl.pallas_call(
        body, name=name,
        grid_spec=pltpu.PrefetchScalarGridSpec(
            num_scalar_prefetch=1, grid=(rows // tr,),
            in_specs=[wblk, wblk, wblk,
                      pl.BlockSpec((tr, cols), lambda i, s: (s[0] * per_slot + i, 0)),
                      pl.BlockSpec((8, tr, cols), lambda i, s: (0, i, 0))],
            out_specs=[wblk] * 4),
        out_shape=[jax.ShapeDtypeStruct(w.shape, F32)] * 4,
        compiler_params=_params(),
    )(me, w, m, v, own, land)


def adamw_win(wt, mt, vt, ka, ra, kb, rb):
    rows = wt.shape[0]
    tc = 256
    nh = (D // 2) // tc

    def body(w_ref, m_ref, v_ref, ka_ref, ra_ref, kb_ref, rb_ref, go_ref, d_ref, mo_ref, vo_ref):
        first = pl.program_id(0) < nh
        grad = jnp.where(first, ka_ref[...] + ra_ref[...].astype(F32), kb_ref[...] + rb_ref[...].astype(F32))
        go_ref[...] = grad
        d_ref[...], mo_ref[...], vo_ref[...] = _adam_update(w_ref[...], m_ref[...], v_ref[...], grad)

    blk = pl.BlockSpec((rows, tc), lambda j: (0, j))
    lo = pl.BlockSpec((rows, tc), lambda j: (0, jnp.minimum(j, nh - 1)))
    hi = pl.BlockSpec((rows, tc), lambda j: (0, jnp.maximum(j - nh, 0)))
    return pl.pallas_call(
        body, name="adamw_w_in", grid=(D // tc,),
        in_specs=[blk, blk, blk, lo, lo, hi, hi], out_specs=[blk] * 4,
        out_shape=[jax.ShapeDtypeStruct((rows, D), F32)] * 4,
        compiler_params=_params(("parallel",)),
    )(wt, mt, vt, ka, ra, kb, rb)


_ORD_A = ("x", "y", "c")
_ORD_B = ("y", "x", "c")


def _rows128(a, rows):
    flat = a.reshape(-1)
    return jnp.pad(flat, (0, rows * 128 - flat.shape[0])).reshape(rows, 128)


def kernel(x, c, positions, w_ada, b_ada, g_pre, w_in, conv_w, w_conv_out, g_q, w_uq, g_kv, w_ukv, w_mla_out, w_out, g_post, loss_target, m_w_ada, m_b_ada, m_g_pre, m_w_in, m_conv_w, m_w_conv_out, m_g_q, m_w_uq, m_g_kv, m_w_ukv, m_w_mla_out, m_w_out, m_g_post, v_w_ada, v_b_ada, v_g_pre, v_w_in, v_conv_w, v_w_conv_out, v_g_q, v_w_uq, v_g_kv, v_w_ukv, v_w_mla_out, v_w_out, v_g_post):
    nb, seq, _ = x.shape
    t = nb * seq
    mx, my, mc = lax.axis_index("x"), lax.axis_index("y"), lax.axis_index("c")
    me = 4 * mx + 2 * my + mc
    co = {"x": mx, "y": my, "c": mc}

    x2 = x.reshape(t, D)
    tgt2 = loss_target.reshape(t, D)
    pos2 = positions.reshape(t, 1)

    ada_cols = w_ada.shape[2]
    b_cols = lax.dynamic_slice(b_ada, (0, me * ada_cols), (1, ada_cols))
    c_g, taps_g, mod_g = ada_gather(jnp.pad(c, ((0, 8 - nb), (0, 0))), _rows128(conv_w[0], 8), w_ada[0], b_cols)
    c_all = c_g[:, :nb].reshape(8 * nb, D)
    conv_full = taps_g[:, 0:3].transpose(1, 0, 2).reshape(3, D)
    conv_full8 = jnp.pad(conv_full, ((0, 5), (0, 0)))
    mod = mod_g[:, :nb].transpose(1, 0, 2).reshape(nb, 8 * ada_cols)
    shift = mod[:, 0:D].reshape(nb, 1, D)
    scale = mod[:, D:2 * D].reshape(nb, 1, D)
    gate = mod[:, 2 * D:3 * D].reshape(nb, 1, D)

    wt = w_in[0].T.astype(BF16)
    lo = lax.bitcast_convert_type(wt[:, :D // 2], jnp.uint16).astype(jnp.uint32)
    hi = lax.bitcast_convert_type(wt[:, D // 2:], jnp.uint16).astype(jnp.uint32)
    wt_bits = lax.bitcast_convert_type(lo | (hi << 16), F32)
    wt_bits, mod = lax.optimization_barrier((wt_bits, mod))
    shift = mod[:, 0:D].reshape(nb, 1, D)
    scale = mod[:, D:2 * D].reshape(nb, 1, D)
    gate = mod[:, 2 * D:3 * D].reshape(nb, 1, D)
    q4 = D // 4
    r3rd = wt_bits.shape[0] // 3
    plan = [(0, (k * r3rd, r3rd), (g * q4, q4), (_ORD_A, _ORD_B)[g]) for k in range(3) for g in range(2)]
    gw = allgather_big([wt_bits], plan, "gather_w_in")
    late = [w_conv_out[0].astype(BF16), w_mla_out[0].astype(BF16), w_out[0].astype(BF16),
            jnp.pad(w_uq[0].T.astype(BF16), ((0, DQK - 192), (0, 0))), w_ukv[0].T.astype(BF16)]
    gw0, late = lax.optimization_barrier((gw[0], late))
    late_state, late_token = gather_start(late, "gather_late_start")
    wt_bits_all = gw0.reshape(N_IN, D // 2)

    inv_freq = ROPE_THETA ** (-jnp.arange(0, ROPE, 2, dtype=F32) / ROPE)
    invf = jnp.concatenate([inv_freq, inv_freq, jnp.zeros((128 - ROPE,), F32)]).reshape(1, 128)
    lane = np.arange(128)
    tabs = (invf,
            jnp.asarray(np.where(lane < HALF, -1.0, 0.0).reshape(1, 128), F32),
            jnp.asarray(np.where((lane >= HALF) & (lane < ROPE), 1.0, 0.0).reshape(1, 128), F32))

    h = prenorm_fwd(x2, scale, shift, g_pre, seq)
    proj, wt_p = proj_matmul(h, wt_bits_all, late_token)
    y = conv_fwd(proj, conv_full8, seq)
    gl = gather_wait(late_state, y, "gather_late_wait")
    wco = gl[0].reshape(D, D)
    wmo = gl[1].reshape(D, D)
    wout = gl[2].reshape(D, D)
    wuq_p = gl[3].reshape(H * DQK, QL)
    wukv = gl[4].reshape(H * 256, KVL)
    q_rot, k_cat, kv, qn, kvn = mla_prep_fwd(proj, pos2, g_q, g_kv, wuq_p, wukv, tabs)
    attn, lse = flash_fwd(q_rot, k_cat, kv, nb, seq)
    o, ya, yb, m, do2, dout, dgate, dg_post, loss_part = tail_fwd(
        y, attn, proj, x2, tgt2, gate, g_post, wco, wmo, wout, seq)

    dproj, dya, dyb, dattn, dy = tail_bwd(do2, proj, ya, yb, attn, wout, wmo, wco)
    g_wout = grad_matmul(m, do2, "grad_w_square")
    g_wmo = grad_matmul(o, dyb, "grad_w_square")
    g_wco = grad_matmul(y, dya, "grad_w_square")
    sc1, sc1_tok = scatter_start([g_wco, g_wmo, g_wout], "scatter_out_grads_start")
    dproj, dconv = conv_bwd(dproj, proj, dy, conv_full8, seq)
    dq_rot, dk, dv = flash_bwd(q_rot, k_cat, kv, attn, dattn, lse, nb, seq, sc1_tok)
    dproj, dq, dkv, dg_q, dg_kv = mla_prep_bwd(dproj, proj, dq_rot, dk, dv, pos2, g_q, g_kv, wuq_p, wukv, tabs)
    g_wuq_t = grad_matmul(dq, qn, "grad_w_uq")
    g_wukv_t = grad_matmul(dkv, kvn, "grad_w_ukv")
    sc2, sc2_tok = scatter_start([g_wuq_t, g_wukv_t], "scatter_mla_grads_start")
    g_win_p = win_grad_matmul(h, dproj, sc2_tok)

    g_wt = g_win_p.reshape(2, 2, 2, N_IN // 8, D)
    ords = [("c", "y", "x"), ("c", "x", "y")]
    hc = D // 2
    win_shape = (2, 2, N_IN // 8, hc)
    pick_w = lambda col: (lambda ref, cc: ref.at[:, :, 1 - cc["c"], :, pl.ds(col * hc, hc)])
    which1 = [0, 0]
    picks1 = [pick_w(0), pick_w(1)]
    st1, tok1 = swap_start([g_wt], which1, ["c"] * 2, picks1, [win_shape] * 2, "rs_c_start")
    assert nb == 2
    dh0 = dh_matmul(dproj, wt_p, tok1, seq, 0)
    (g_wt,), r1 = swap_wait(st1, dh0, which1, ["c"] * 2, picks1, "rs_c_wait")
    sel_xyc = jnp.stack([mx, my, mc]).astype(jnp.int32)
    sel2 = [jnp.stack([co[o[2]]]).astype(jnp.int32) for o in ords]
    first = [rs_win_add_first(g_wt, r1[0], sel_xyc, 1, 0, "rs_add_first_0"),
             rs_win_add_first(g_wt, r1[1], sel_xyc, 0, 1, "rs_add_first_1")]
    keep1, send1 = zip(*first)
    all4 = [0, 1]
    none4 = [None] * 2
    axes2 = [o[1] for o in ords]
    st2, tok2 = swap_start(list(send1), all4, axes2, none4, [s.shape for s in send1], "rs_ici1_start")

    dh1 = dh_matmul(dproj, wt_p, tok2, seq, 1)
    gx0, dsh0, dsc0, dgp0 = prenorm_bwd(dh0, x2, dout, scale, g_pre, seq, tok2, 0, None)
    _, r2 = swap_wait(st2, (gx0, dh1), all4, axes2, none4, "rs_ici1_wait")
    keep2, send2 = zip(*[rs_add_second(keep1[a], r2[a], sel2[a], "rs_add_second") for a in range(2)])
    axes3 = [o[2] for o in ords]
    st3, tok3 = swap_start(list(send2), all4, axes3, none4, [s.shape for s in send2], "rs_ici2_start")
    grad_x2, dsh1, dsc1, dgp1 = prenorm_bwd(dh1, x2, dout, scale, g_pre, seq, tok3, 1, gx0)
    dshift = jnp.stack([dsh0, dsh1])
    dscale = jnp.stack([dsc0, dsc1])
    dg_pre = dgp0 + dgp1

    dmod = jnp.concatenate([dshift, dscale, dgate], axis=2).reshape(nb * 3 * D // 128, 128)
    small = jnp.concatenate([
        dmod, _rows128(dg_pre, 8), _rows128(dg_post, 8), _rows128(dg_q, 8), _rows128(dg_kv, 8),
        dconv[0:3].reshape(24, 128), _rows128(loss_part, 8)], axis=0)
    small_g = small_allgather(small, "gather_small_grads")
    sums = slot_sum(small_g)
    dmod_all = small_g[:, 0:48].reshape(8 * nb, 3 * D)
    g_bada = (sums[0:24] + sums[24:48]).reshape(1, 3 * D)
    g_gpre = sums[48:56].reshape(1, D)
    g_gpost = sums[56:64].reshape(1, D)
    g_gq = sums[64:67].reshape(1, QL)
    g_gkv = sums[72:74].reshape(1, KVL)
    g_conv_full = sums[80:104].reshape(3, D)
    loss = sums[104, 0]
    g_conv = lax.dynamic_slice(g_conv_full, (0, me * 128), (3, 128))
    dmod_cols = lax.dynamic_slice(dmod_all, (0, me * ada_cols), (8 * nb, ada_cols))
    g_wada = ada_bwd(c_all, dmod_cols)

    res = {}
    res["w_ada"] = [o_[None] for o_ in (g_wada, *adamw(w_ada[0], m_w_ada[0], v_w_ada[0], g_wada, "adamw_w_ada", tok3))]

    def pack(b_, gp_, gpo_, gq_, gkv_, cw_):
        return jnp.concatenate([_rows128(b_, 24), _rows128(gp_, 8), _rows128(gpo_, 8), _rows128(gq_, 8),
                                _rows128(gkv_, 8), _rows128(cw_, 8)], axis=0)

    sw = pack(b_ada, g_pre, g_post, g_q, g_kv, conv_w)
    sm = pack(m_b_ada, m_g_pre, m_g_post, m_g_q, m_g_kv, m_conv_w)
    sv = pack(v_b_ada, v_g_pre, v_g_post, v_g_q, v_g_kv, v_conv_w)
    sg = pack(g_bada, g_gpre, g_gpost, g_gq, g_gkv, g_conv)
    small_out = (sg, *adamw(sw, sm, sv, sg, "adamw_small", tok3))

    _, r3 = swap_wait(st3, small_out[1], all4, axes3, none4, "rs_ici2_wait")

    (g_wco, g_wmo, g_wout), (l_wco, l_wmo, l_wout) = scatter_wait(sc1, small_out[2], "scatter_out_grads_wait")
    (g_wuq_t, g_wukv_t), (l_wuq, l_wukv) = scatter_wait(sc2, small_out[3], "scatter_mla_grads_wait")

    res["w_in"] = [o_.T[None] for o_ in adamw_win(w_in[0].T, m_w_in[0].T, v_w_in[0].T,
                                                  keep2[0], r3[0], keep2[1], r3[1])]
    me1 = me.reshape(1).astype(jnp.int32)
    res["w_uq"] = [o_.T[None] for o_ in adamw_scattered(
        w_uq[0].T, m_w_uq[0].T, v_w_uq[0].T, g_wuq_t, l_wuq, me1, 64, "adamw_w_uq")]
    res["w_ukv"] = [o_[None] for o_ in adamw_scattered(
        w_ukv[0], m_w_ukv[0], v_w_ukv[0], g_wukv_t, l_wukv, me1, KVL, "adamw_w_ukv", transpose=True)]
    for nm, wv, mv, vv, gg, ll in (("w_conv_out", w_conv_out, m_w_conv_out, v_w_conv_out, g_wco, l_wco),
                                   ("w_mla_out", w_mla_out, m_w_mla_out, v_w_mla_out, g_wmo, l_wmo),
                                   ("w_out", w_out, m_w_out, v_w_out, g_wout, l_wout)):
        res[nm] = [o_[None] for o_ in adamw_scattered(wv[0], mv[0], vv[0], gg, ll, me1, 128, "adamw_square")]

    def unpack(a):
        return {"b_ada": a[0:24].reshape(1, 3 * D), "g_pre": a[24:32].reshape(1, D),
                "g_post": a[32:40].reshape(1, D), "g_q": a[40:43].reshape(1, QL),
                "g_kv": a[48:50].reshape(1, KVL), "conv_w": a[56:59].reshape(-1)[:3 * 128].reshape(1, 3, 128)}

    for nm in ("b_ada", "g_pre", "g_post", "g_q", "g_kv", "conv_w"):
        res[nm] = [unpack(a)[nm] for a in small_out]

    order = ["w_ada", "b_ada", "g_pre", "w_in", "conv_w", "w_conv_out", "g_q", "w_uq", "g_kv", "w_ukv",
             "w_mla_out", "w_out", "g_post"]
    out = [loss, grad_x2.reshape(nb, seq, D)]
    for k_ in range(4):
        out += [res[nm][k_] for nm in order]
    return tuple(out)
```

```python
import numpy as np
import jax
import jax.numpy as jnp
from jax import lax
from jax.experimental import pallas as pl
from jax.experimental.pallas import tpu as pltpu

F32 = jnp.float32
BF16 = jnp.bfloat16
MESH = pl.DeviceIdType.MESH

D = 1024
H = 8
QL = 384
KVL = 256
ROPE = 64
HALF = ROPE // 2
DQK = 256
DV = 128
NSEG = 8
NP = NSEG * D
EPS = 1e-6
ROPE_THETA = 10000.0
SM_SCALE = (128 + ROPE) ** -0.5
LOG2E = 1.4426950408889634
LN2 = 0.6931471805599453
FLASH_TQ = 512

SEG_BZ, SEG_GA, SEG_GB, SEG_LAT, SEG_V = 0, 1, 2, 3, 4

ADAM_LR = 0.001
ADAM_B1 = 0.9
ADAM_B2 = 0.999
ADAM_EPS = 1e-08
ADAM_WD = 0.01
ADAM_STEP = 10

VMEM_LIMIT = 56 * 1024 * 1024


def _params(sem=None, vmem=VMEM_LIMIT):
    kw = dict(vmem_limit_bytes=vmem)
    if sem is not None:
        kw["dimension_semantics"] = sem
    return pltpu.CompilerParams(**kw)


def _sig(v):
    return 0.5 * jnp.tanh(0.5 * v) + 0.5


def _dot(a, b):
    return jnp.dot(a, b, preferred_element_type=F32)


def _dot_nt(a, b):
    return lax.dot_general(a, b, (((1,), (1,)), ((), ())), preferred_element_type=F32)


def _dot_tn(a, b):
    return lax.dot_general(a, b, (((0,), (0,)), ((), ())), preferred_element_type=F32)


_AXIS_POS = {"x": 0, "y": 1, "c": 2}


def _coords():
    return lax.axis_index("x"), lax.axis_index("y"), lax.axis_index("c")


def _partner(axis):
    p = list(_coords())
    p[_AXIS_POS[axis]] = 1 - p[_AXIS_POS[axis]]
    return tuple(p)


def small_allgather(v, name):
    rows = v.shape[0]

    def body(v_ref, out_ref, send_sems, recv_sems):
        x, y, c = _coords()
        me = 4 * x + 2 * y + c
        out_ref[me] = v_ref[...]
        copies = []
        for k in range(1, 8):
            peer = (1 - x if k & 4 else x, 1 - y if k & 2 else y, 1 - c if k & 1 else c)
            cp = pltpu.make_async_remote_copy(
                src_ref=v_ref, dst_ref=out_ref.at[me],
                send_sem=send_sems.at[k - 1], recv_sem=recv_sems.at[k - 1],
                device_id=peer, device_id_type=MESH)
            cp.start()
            copies.append(cp)
        for cp in copies:
            cp.wait()

    return pl.pallas_call(
        body, name=name,
        out_shape=jax.ShapeDtypeStruct((8, rows, 128), F32),
        in_specs=[pl.BlockSpec(memory_space=pltpu.VMEM)],
        out_specs=pl.BlockSpec(memory_space=pltpu.VMEM),
        scratch_shapes=[pltpu.SemaphoreType.DMA((7,)), pltpu.SemaphoreType.DMA((7,))],
    )(v)


def _own_block_placed(s):
    x, y, c = _coords()
    return lax.dynamic_update_slice(lax.empty((2, 2, 2) + s.shape, s.dtype), s[None, None, None],
                                    (x, y, c) + (0,) * s.ndim)


def allgather_big(arrs, plan, name, per_wave=None):
    n = len(arrs)
    m = len(plan)
    nst = len(plan[0][3])
    per_wave = m if per_wave is None else per_wave
    assert m % per_wave == 0

    def body(*refs):
        ins, outs = refs[n:2 * n], refs[2 * n:3 * n]
        send_sems, recv_sems = refs[3 * n:]
        x, y, c = _coords()
        co = {"x": x, "y": y, "c": c}

        def window(ref, lead, rows, cols):
            win = tuple(slice(None) if w is None else pl.ds(w[0], w[1]) for w in (rows, cols))
            return ref.at[tuple(lead) + win]

        def held(e, free):
            i, rows, cols, _ = plan[e]
            lead = [slice(None) if ax in free else co[ax] for ax in ("x", "y", "c")]
            return window(outs[i], lead, rows, cols)

        def rcopy(e, stage, src, dst, axis):
            return pltpu.make_async_remote_copy(
                src_ref=src, dst_ref=dst,
                send_sem=send_sems.at[e, stage], recv_sem=recv_sems.at[e, stage],
                device_id=_partner(axis), device_id_type=MESH)

        copies = {}
        nwaves = m // per_wave
        for t in range(nwaves + nst - 1):
            for w in range(nwaves):
                s = t - w
                if not 0 <= s < nst:
                    continue
                for e in range(w * per_wave, (w + 1) * per_wave):
                    i, rows, cols, order = plan[e]
                    if s == 0:
                        cp = rcopy(e, 0, window(ins[i], [], rows, cols), held(e, ()), order[0])
                    else:
                        copies[e, s - 1].wait_recv()
                        blk = held(e, order[:s])
                        cp = rcopy(e, s, blk, blk, order[s])
                    cp.start()
                    copies[e, s] = cp
        for e in range(m):
            copies[e, nst - 1].wait_recv()
        for e in range(m):
            for s in range(nst):
                copies[e, s].wait_send()

    any_spec = pl.BlockSpec(memory_space=pl.ANY)
    lands = [_own_block_placed(a) for a in arrs]
    return pl.pallas_call(
        body, name=name,
        out_shape=[jax.ShapeDtypeStruct(l.shape, l.dtype) for l in lands],
        in_specs=[any_spec] * (2 * n),
        out_specs=[any_spec] * n,
        input_output_aliases={i: i for i in range(n)},
        scratch_shapes=[pltpu.SemaphoreType.DMA((m, nst)), pltpu.SemaphoreType.DMA((m, nst))],
    )(*lands, *arrs)


_HBM =pl.BlockSpec(memory_space=pltpu.HBM)
_SEM = pl.BlockSpec(memory_space=pltpu.SEMAPHORE)


def _swap_copies(srcs, lands, send_sems, recv_sems, axes, picks):
    x, y, c = _coords()
    co = {"x": x, "y": y, "c": c}
    return [pltpu.make_async_remote_copy(
        src_ref=srcs[a] if picks[a] is None else picks[a](srcs[a], co), dst_ref=lands[a],
        send_sem=send_sems.at[a], recv_sem=recv_sems.at[a],
        device_id=_partner(axes[a]), device_id_type=MESH) for a in range(len(srcs))]


def swap_start(arrs, which, axes, picks, out_shapes, name):
    ns, n = len(arrs), len(which)

    def body(*refs):
        srcs, lands = refs[:ns], refs[ns:ns + n]
        send_sems, recv_sems = refs[ns + n:ns + n + 2]
        token = refs[-1]
        for cp in _swap_copies([srcs[i] for i in which], lands, send_sems, recv_sems, axes, picks):
            cp.start()
        token[...] = jnp.zeros_like(token)

    lands = [lax.empty(s, arrs[i].dtype) for s, i in zip(out_shapes, which)]
    ops = [pltpu.with_memory_space_constraint(a, pltpu.HBM) for a in list(arrs) + lands]
    out = pl.pallas_call(
        body, name=name,
        out_shape=[pltpu.SemaphoreType.DMA((n,)), pltpu.SemaphoreType.DMA((n,))]
        + [pltpu.HBM(o.shape, o.dtype) for o in ops] + [jax.ShapeDtypeStruct((8, 128), F32)],
        in_specs=[_HBM] * (ns + n),
        out_specs=[_SEM, _SEM] + [_HBM] * (ns + n) + [pl.BlockSpec(memory_space=pltpu.VMEM)],
        input_output_aliases={i: 2 + i for i in range(ns + n)},
        compiler_params=pltpu.CompilerParams(has_side_effects=pltpu.SideEffectType.DATAFLOW_SIDE_EFFECTING),
    )(*ops)
    return out[:-1], out[-1]


def swap_wait(state, after, which, axes, picks, name):
    n = len(which)
    ns = len(state) - 2 - n

    def body(*refs):
        srcs, lands = refs[:ns], refs[ns:ns + n]
        send_sems, recv_sems = refs[ns + n:ns + n + 2]
        for cp in _swap_copies([srcs[i] for i in which], lands, send_sems, recv_sems, axes, picks):
            cp.wait_send()
            cp.wait_recv()

    thru = list(state[2:])
    after = list(after) if isinstance(after, (list, tuple)) else [after]
    out = pl.pallas_call(
        body, name=name,
        out_shape=[pltpu.HBM(o.shape, o.dtype) for o in thru],
        in_specs=[_HBM] * (ns + n) + [_SEM, _SEM] + [pl.BlockSpec(memory_space=pl.ANY)] * len(after),
        out_specs=[_HBM] * (ns + n),
        input_output_aliases={i: i for i in range(ns + n)},
        compiler_params=pltpu.CompilerParams(has_side_effects=pltpu.SideEffectType.DATAFLOW_SIDE_EFFECTING),
    )(*thru, state[0], state[1], *after)
    return out[:ns], out[ns:]


def _gather_copies(shards, lands, send_sems, recv_sems):
    x, y, c = _coords()
    copies = []
    for a in range(len(shards)):
        for k in range(1, 8):
            peer = (1 - x if k & 4 else x, 1 - y if k & 2 else y, 1 - c if k & 1 else c)
            copies.append(pltpu.make_async_remote_copy(
                src_ref=shards[a], dst_ref=lands[a].at[x, y, c],
                send_sem=send_sems.at[7 * a + k - 1], recv_sem=recv_sems.at[7 * a + k - 1],
                device_id=peer, device_id_type=MESH))
    return copies


def gather_start(shards, name):
    n = len(shards)
    x, y, c = _coords()

    def body(*refs):
        srcs, lands = refs[:n], refs[n:2 * n]
        send_sems, recv_sems = refs[2 * n:2 * n + 2]
        token = refs[-1]
        for cp in _gather_copies(srcs, lands, send_sems, recv_sems):
            cp.start()
        token[...] = jnp.zeros_like(token)

    lands = [_own_block_placed(s) for s in shards]
    ops = [pltpu.with_memory_space_constraint(a, pltpu.HBM) for a in list(shards) + lands]
    out = pl.pallas_call(
        body, name=name,
        out_shape=[pltpu.SemaphoreType.DMA((7 * n,)), pltpu.SemaphoreType.DMA((7 * n,))]
        + [pltpu.HBM(o.shape, o.dtype) for o in ops] + [jax.ShapeDtypeStruct((8, 128), F32)],
        in_specs=[_HBM] * (2 * n),
        out_specs=[_SEM, _SEM] + [_HBM] * (2 * n) + [pl.BlockSpec(memory_space=pltpu.VMEM)],
        input_output_aliases={i: 2 + i for i in range(2 * n)},
        compiler_params=pltpu.CompilerParams(has_side_effects=pltpu.SideEffectType.DATAFLOW_SIDE_EFFECTING),
    )(*ops)
    return out[:-1], out[-1]


def gather_wait(state, after, name):
    n = (len(state) - 2) // 2

    def body(*refs):
        srcs, lands = refs[:n], refs[n:2 * n]
        send_sems, recv_sems = refs[2 * n:2 * n + 2]
        for cp in _gather_copies(srcs, lands, send_sems, recv_sems):
            cp.wait_send()
            cp.wait_recv()

    thru = list(state[2:])
    out = pl.pallas_call(
        body, name=name,
        out_shape=[pltpu.HBM(o.shape, o.dtype) for o in thru],
        in_specs=[_HBM] * (2 * n) + [_SEM, _SEM, pl.BlockSpec(memory_space=pl.ANY)],
        out_specs=[_HBM] * (2 * n),
        input_output_aliases={i: i for i in range(2 * n)},
        compiler_params=pltpu.CompilerParams(has_side_effects=pltpu.SideEffectType.DATAFLOW_SIDE_EFFECTING),
    )(*thru, state[0], state[1], after)
    return out[n:]


def _scatter_copies(grads, lands, send_sems, recv_sems):
    x, y, c = _coords()
    me = 4 * x + 2 * y + c
    copies = []
    for a in range(len(grads)):
        r = grads[a].shape[0] // 8
        for k in range(1, 8):
            px, py, pc = (1 - x if k & 4 else x, 1 - y if k & 2 else y, 1 - c if k & 1 else c)
            rows = pl.ds(pl.multiple_of((4 * px + 2 * py + pc) * r, r), r)
            copies.append(pltpu.make_async_remote_copy(
                src_ref=grads[a].at[rows], dst_ref=lands[a].at[me],
                send_sem=send_sems.at[7 * a + k - 1], recv_sem=recv_sems.at[7 * a + k - 1],
                device_id=(px, py, pc), device_id_type=MESH))
    return copies


def scatter_start(grads, name):
    n = len(grads)

    def body(*refs):
        srcs, lands = refs[:n], refs[n:2 * n]
        send_sems, recv_sems = refs[2 * n:2 * n + 2]
        token = refs[-1]
        for cp in _scatter_copies(srcs, lands, send_sems, recv_sems):
            cp.start()
        token[...] = jnp.zeros_like(token)

    lands = [lax.empty((8, g.shape[0] // 8, g.shape[1]), g.dtype) for g in grads]
    ops = [pltpu.with_memory_space_constraint(a, pltpu.HBM) for a in list(grads) + lands]
    out = pl.pallas_call(
        body, name=name,
        out_shape=[pltpu.SemaphoreType.DMA((7 * n,)), pltpu.SemaphoreType.DMA((7 * n,))]
        + [pltpu.HBM(o.shape, o.dtype) for o in ops] + [jax.ShapeDtypeStruct((8, 128), F32)],
        in_specs=[_HBM] * (2 * n),
        out_specs=[_SEM, _SEM] + [_HBM] * (2 * n) + [pl.BlockSpec(memory_space=pltpu.VMEM)],
        input_output_aliases={i: 2 + i for i in range(2 * n)},
        compiler_params=pltpu.CompilerParams(has_side_effects=pltpu.SideEffectType.DATAFLOW_SIDE_EFFECTING),
    )(*ops)
    return out[:-1], out[-1]


def scatter_wait(state, after, name):
    n = (len(state) - 2) // 2

    def body(*refs):
        srcs, lands = refs[:n], refs[n:2 * n]
        send_sems, recv_sems = refs[2 * n:2 * n + 2]
        for cp in _scatter_copies(srcs, lands, send_sems, recv_sems):
            cp.wait_send()
            cp.wait_recv()

    thru = list(state[2:])
    after = list(after) if isinstance(after, (list, tuple)) else [after]
    out = pl.pallas_call(
        body, name=name,
        out_shape=[pltpu.HBM(o.shape, o.dtype) for o in thru],
        in_specs=[_HBM] * (2 * n) + [_SEM, _SEM] + [pl.BlockSpec(memory_space=pl.ANY)] * len(after),
        out_specs=[_HBM] * (2 * n),
        input_output_aliases={i: i for i in range(2 * n)},
        compiler_params=pltpu.CompilerParams(has_side_effects=pltpu.SideEffectType.DATAFLOW_SIDE_EFFECTING),
    )(*thru, state[0], state[1], *after)
    return out[:n], out[n:]


def rs_win_add_first(g, r, sel, next_dim, col, name):
    rows, cols = r.shape[2:]

    def body(sel_ref, gk_ref, rk_ref, gs_ref, rs_ref, keep_ref, send_ref):
        keep_ref[...] = gk_ref[...] + rk_ref[...]
        send_ref[...] = (gs_ref[...] + rs_ref[...]).astype(BF16)

    def g_map(flip):
        def f(j, s):
            nxt = 1 - s[next_dim] if flip else s[next_dim]
            return (nxt, j, s[2], 0, col) if next_dim == 0 else (j, nxt, s[2], 0, col)
        return f

    def r_map(flip):
        def f(j, s):
            nxt = 1 - s[next_dim] if flip else s[next_dim]
            return (nxt, j, 0, 0) if next_dim == 0 else (j, nxt, 0, 0)
        return f

    gblk = (None, None, None, rows, cols)
    rblk = (None, None, rows, cols)
    oblk = (None, rows, cols)
    return pl.pallas_call(
        body, name=name,
        grid_spec=pltpu.PrefetchScalarGridSpec(
            num_scalar_prefetch=1, grid=(2,),
            in_specs=[pl.BlockSpec(gblk, g_map(False)), pl.BlockSpec(rblk, r_map(False)),
                      pl.BlockSpec(gblk, g_map(True)), pl.BlockSpec(rblk, r_map(True))],
            out_specs=[pl.BlockSpec(oblk, lambda j, s: (j, 0, 0)),
                       pl.BlockSpec(oblk, lambda j, s: (j, 0, 0))]),
        out_shape=[jax.ShapeDtypeStruct((2, rows, cols), F32),
                   jax.ShapeDtypeStruct((2, rows, cols), BF16)],
        compiler_params=_params(),
    )(sel, g, r, g, r)


def rs_add_second(k, r, sel, name):
    _, rows, cols = k.shape
    tr = rows // 2 if rows % 32 == 0 else rows
    nt = rows // tr

    def body(sel_ref, kk_ref, rk_ref, ks_ref, rs_ref, keep_ref, send_ref):
        keep_ref[...] = kk_ref[...] + rk_ref[...].astype(F32)
        send_ref[...] = (ks_ref[...] + rs_ref[...].astype(F32)).astype(BF16)

    blk = (None, tr, cols)
    oblk = (tr, cols)
    return pl.pallas_call(
        body, name=name,
        grid_spec=pltpu.PrefetchScalarGridSpec(
            num_scalar_prefetch=1, grid=(nt,),
            in_specs=[
                pl.BlockSpec(blk, lambda i, s: (s[0], i, 0)),
                pl.BlockSpec(blk, lambda i, s: (s[0], i, 0)),
                pl.BlockSpec(blk, lambda i, s: (1 - s[0], i, 0)),
                pl.BlockSpec(blk, lambda i, s: (1 - s[0], i, 0)),
            ],
            out_specs=[pl.BlockSpec(oblk, lambda i, s: (i, 0)),
                       pl.BlockSpec(oblk, lambda i, s: (i, 0))]),
        out_shape=[jax.ShapeDtypeStruct((rows, cols), F32),
                   jax.ShapeDtypeStruct((rows, cols), BF16)],
        compiler_params=_params(),
    )(sel, k, r, k, r)


SEG_ROWS = (4800, 5824, 6848, 4096, 0, 1024, 2048, 3072)
LAT_ROWS = QL + KVL + ROPE
N_IN = 7872


def _seg_row(j):
    return pl.multiple_of(jnp.where(j < 3, 4800 + 1024 * j, jnp.where(j == 3, 4096, (j - 4) * 1024)), 8)


def proj_matmul(h, wt_bits, token):
    t = h.shape[0]
    tm = min(2048, t)

    def body(h_ref, w_hbm, tok_ref, o_ref, wt_ref, buf, sems):
        j = pl.program_id(0)
        slot = j % 2

        def fetch(seg, into):
            return pltpu.make_async_copy(w_hbm.at[pl.ds(_seg_row(seg), D)], buf.at[into], sems.at[into])

        @pl.when(pl.program_id(1) == 0)
        def _():
            @pl.when(j == 0)
            def _():
                fetch(j, slot).start()

            fetch(j, slot).wait()

            @pl.when(j + 1 < NSEG)
            def _():
                fetch(j + 1, 1 - slot).start()

            bits = pltpu.bitcast(buf[slot], jnp.uint32)
            row = lax.broadcasted_iota(jnp.int32, (D, D // 2), 0)
            live = jnp.logical_or(j != SEG_LAT, row < LAT_ROWS)
            lo = pltpu.bitcast(bits << 16, F32)
            hi = pltpu.bitcast(bits & jnp.uint32(0xFFFF0000), F32)
            wt_ref[:, :D // 2] = jnp.where(live, lo, 0.0).astype(BF16)
            wt_ref[:, D // 2:] = jnp.where(live, hi, 0.0).astype(BF16)

        o_ref[...] = _dot_nt(h_ref[...], wt_ref[...]).astype(BF16)

    return pl.pallas_call(
        body, name="proj_matmul", grid=(NSEG, t // tm),
        in_specs=[pl.BlockSpec((tm, D), lambda j, i: (i, 0)),
                  pl.BlockSpec(memory_space=pl.ANY),
                  pl.BlockSpec((8, 128), lambda j, i: (0, 0))],
        out_specs=[pl.BlockSpec((None, tm, D), lambda j, i: (j, i, 0)),
                   pl.BlockSpec((D, D), lambda j, i: (j, 0))],
        out_shape=[jax.ShapeDtypeStruct((NSEG, t, D), BF16), jax.ShapeDtypeStruct((NP, D), BF16)],
        scratch_shapes=[pltpu.VMEM((2, D, D // 2), F32), pltpu.SemaphoreType.DMA((2,))],
        compiler_params=_params(("arbitrary", "arbitrary")),
    )(h, wt_bits, token)


def dh_matmul(dproj, wt, token, seq, b):
    tm = min(1024, seq)
    nblk = seq // tm

    per = 2

    def body(b_ref, d_ref, w_ref, tok_ref, o_ref, acc_ref):
        k = pl.program_id(1)
        last = NSEG // per - 1

        def part():
            p = _dot(d_ref[0], w_ref[0:D, :])
            for j in range(1, per):
                p = p + _dot(d_ref[j], w_ref[j * D:(j + 1) * D, :])
            return p

        @pl.when(k == 0)
        def _():
            acc_ref[...] = part()

        @pl.when(jnp.logical_and(k > 0, k < last))
        def _():
            acc_ref[...] += part()

        @pl.when(k == last)
        def _():
            o_ref[...] = acc_ref[...] + part()

    return pl.pallas_call(
        body, name="dh_matmul",
        grid_spec=pltpu.PrefetchScalarGridSpec(
            num_scalar_prefetch=1, grid=(nblk, NSEG // per),
            in_specs=[pl.BlockSpec((per, tm, D), lambda i, k, s: (k, s[0] * nblk + i, 0)),
                      pl.BlockSpec((per * D, D), lambda i, k, s: (k, 0)),
                      pl.BlockSpec((8, 128), lambda i, k, s: (0, 0))],
            out_specs=pl.BlockSpec((tm, D), lambda i, k, s: (i, 0)),
            scratch_shapes=[pltpu.VMEM((tm, D), F32)]),
        out_shape=jax.ShapeDtypeStruct((seq, D), F32),
        compiler_params=_params(("parallel", "arbitrary")),
    )(jnp.full((1,), b, jnp.int32), dproj, wt, token)


def win_grad_matmul(h, dproj, token):
    t = h.shape[0]

    def body(h_ref, d_ref, tok_ref, o_hbm, acc_ref, sems):
        j = pl.program_id(0)

        def out_copy(jj, action):
            slot = lax.rem(jj, 2)

            @pl.when(jj != SEG_LAT)
            def _():
                action(pltpu.make_async_copy(acc_ref.at[slot], o_hbm.at[pl.ds(_seg_row(jj), D)],
                                             sems.at[slot]))

            @pl.when(jj == SEG_LAT)
            def _():
                action(pltpu.make_async_copy(acc_ref.at[slot, pl.ds(0, LAT_ROWS)],
                                             o_hbm.at[pl.ds(SEG_ROWS[SEG_LAT], LAT_ROWS)], sems.at[slot]))

        acc_ref[lax.rem(j, 2)] = _dot_tn(d_ref[...], h_ref[...])
        out_copy(j, lambda cp: cp.start())

        @pl.when(j > 0)
        def _():
            out_copy(j - 1, lambda cp: cp.wait())

        @pl.when(j == NSEG - 1)
        def _():
            out_copy(j, lambda cp: cp.wait())

    return pl.pallas_call(
        body, name="win_grad_matmul", grid=(NSEG,),
        in_specs=[pl.BlockSpec((t, D), lambda j: (0, 0)),
                  pl.BlockSpec((None, t, D), lambda j: (j, 0, 0)),
                  pl.BlockSpec((8, 128), lambda j: (0, 0))],
        out_specs=pl.BlockSpec(memory_space=pl.ANY),
        out_shape=jax.ShapeDtypeStruct((N_IN, D), F32),
        scratch_shapes=[pltpu.VMEM((2, D, D), F32), pltpu.SemaphoreType.DMA((2,))],
        compiler_params=_params(("arbitrary",)),
    )(h, dproj, token)


def grad_matmul(a, b, name):
    t, m = a.shape
    n = b.shape[1]
    tk = min(1024, t)
    nk = t // tk

    def body(a_ref, b_ref, o_ref, acc_ref):
        k = pl.program_id(0)
        part = lambda: _dot_tn(a_ref[...], b_ref[...])
        if nk == 1:
            o_ref[...] = part().astype(BF16)
            return

        @pl.when(k == 0)
        def _():
            acc_ref[...] = part()

        @pl.when(jnp.logical_and(k > 0, k < nk - 1))
        def _():
            acc_ref[...] += part()

        @pl.when(k == nk - 1)
        def _():
            o_ref[...] = (acc_ref[...] + part()).astype(BF16)

    return pl.pallas_call(
        body, name=name, grid=(nk,),
        in_specs=[pl.BlockSpec((tk, m), lambda k: (k, 0)),
                  pl.BlockSpec((tk, n), lambda k: (k, 0))],
        out_specs=pl.BlockSpec((m, n), lambda k: (0, 0)),
        out_shape=jax.ShapeDtypeStruct((m, n), BF16),
        scratch_shapes=[pltpu.VMEM((m, n), F32)],
        compiler_params=_params(("arbitrary",)),
    )(a, b)


def ada_gather(c8, taps8, w_ada, b_cols):
    cols = w_ada.shape[1]

    def body(c_ref, t_ref, w_ref, b_ref, call_ref, tall_ref, mod_ref, part_ref, send_sems, recv_sems):
        x, y, c = _coords()
        me = 4 * x + 2 * y + c
        peers = [(1 - x if k & 4 else x, 1 - y if k & 2 else y, 1 - c if k & 1 else c) for k in range(1, 8)]

        def rcopy(n, src, dst, peer):
            return pltpu.make_async_remote_copy(src_ref=src, dst_ref=dst, send_sem=send_sems.at[n],
                                                recv_sem=recv_sems.at[n], device_id=peer, device_id_type=MESH)

        call_ref[me] = c_ref[...]
        tall_ref[me] = t_ref[...]
        first = []
        for k, peer in enumerate(peers):
            first += [rcopy(k, c_ref, call_ref.at[me], peer), rcopy(7 + k, t_ref, tall_ref.at[me], peer)]
        for cp in first:
            cp.start()
        for cp in first:
            cp.wait()
        rows = call_ref[...].reshape(64, D).astype(BF16)
        part_ref[...] = _dot(rows, w_ref[...].astype(BF16)) + b_ref[...]
        mod_ref[me] = part_ref[pl.ds(pl.multiple_of(8 * me, 8), 8), :]
        second = []
        for k, (px, py, pc) in enumerate(peers):
            theirs = part_ref.at[pl.ds(pl.multiple_of(8 * (4 * px + 2 * py + pc), 8), 8)]
            second.append(rcopy(14 + k, theirs, mod_ref.at[me], (px, py, pc)))
        for cp in second:
            cp.start()
        for cp in second:
            cp.wait()

    vm = pl.BlockSpec(memory_space=pltpu.VMEM)
    return pl.pallas_call(
        body, name="ada_gather",
        out_shape=[jax.ShapeDtypeStruct((8, 8, D), F32), jax.ShapeDtypeStruct((8, 8, 128), F32),
                   jax.ShapeDtypeStruct((8, 8, cols), F32)],
        in_specs=[vm] * 4, out_specs=[vm] * 3,
        scratch_shapes=[pltpu.VMEM((64, cols), F32), pltpu.SemaphoreType.DMA((21,)),
                        pltpu.SemaphoreType.DMA((21,))],
        compiler_params=_params(),
    )(c8, taps8, w_ada, b_cols)


def ada_bwd(c_all, dmod_cols):
    def body(c_ref, d_ref, o_ref):
        o_ref[...] = _dot_tn(c_ref[...].astype(BF16), d_ref[...].astype(BF16))

    return pl.pallas_call(
        body, name="ada_bwd",
        out_shape=jax.ShapeDtypeStruct((c_all.shape[1], dmod_cols.shape[1]), F32),
        compiler_params=_params(),
    )(c_all, dmod_cols)


def slot_sum(g):
    def body(g_ref, o_ref):
        acc = g_ref[0]
        for s in range(1, 8):
            acc = acc + g_ref[s]
        o_ref[...] = acc

    return pl.pallas_call(
        body, name="slot_sum",
        out_shape=jax.ShapeDtypeStruct(g.shape[1:], F32),
    )(g)


def prenorm_fwd(x2, scale, shift, g_pre, seq):
    t = x2.shape[0]
    tm = min(512, seq)
    tpb = seq // tm

    def body(x_ref, sc_ref, sh_ref, g_ref, h_ref):
        xv = x_ref[...]
        r = lax.rsqrt(jnp.mean(xv * xv, axis=-1, keepdims=True) + EPS)
        hv = (xv * r * g_ref[...]) * (1.0 + sc_ref[...]) + sh_ref[...]
        h_ref[...] = hv.astype(BF16)

    per_batch = pl.BlockSpec((None, 1, D), lambda i: (i // tpb, 0, 0))
    return pl.pallas_call(
        body, name="prenorm_fwd", grid=(t // tm,),
        in_specs=[pl.BlockSpec((tm, D), lambda i: (i, 0)), per_batch, per_batch,
                  pl.BlockSpec((1, D), lambda i: (0, 0))],
        out_specs=pl.BlockSpec((tm, D), lambda i: (i, 0)),
        out_shape=jax.ShapeDtypeStruct((t, D), BF16),
        compiler_params=_params(("parallel",)),
    )(x2, scale, shift, g_pre)


def prenorm_bwd(dh, x2, dout, scale, g_pre, seq, token, b, gx_prev):
    t = x2.shape[0]
    tm = min(512, seq)
    tpb = seq // tm
    if gx_prev is None:
        gx_prev = lax.empty((t, D), F32)

    def body(b_ref, dh_ref, x_ref, do_ref, sc_ref, g_ref, tok_ref, gxp_ref, gx_ref, dsh_ref, dsc_ref, dg_ref):
        i = pl.program_id(0)
        xv = x_ref[...]
        dhv = dh_ref[...]
        g = g_ref[...]
        r = lax.rsqrt(jnp.mean(xv * xv, axis=-1, keepdims=True) + EPS)
        nrm = xv * r
        dxn = dhv * (1.0 + sc_ref[...])
        dn = dxn * g
        dx = r * (dn - nrm * jnp.mean(dn * nrm, axis=-1, keepdims=True))
        gx_ref[...] = dx + do_ref[...]

        @pl.when(i == 0)
        def _():
            dsh_ref[...] = jnp.zeros_like(dsh_ref)
            dsc_ref[...] = jnp.zeros_like(dsc_ref)
            dg_ref[...] = jnp.zeros_like(dg_ref)

        dsh_ref[...] += jnp.sum(dhv, axis=0, keepdims=True)
        dsc_ref[...] += jnp.sum(dhv * (nrm * g), axis=0, keepdims=True)
        dg_ref[...] += jnp.sum(dxn * nrm, axis=0, keepdims=True)

    row = pl.BlockSpec((tm, D), lambda i, s: (i, 0))
    grow = pl.BlockSpec((tm, D), lambda i, s: (s[0] * tpb + i, 0))
    per_batch = pl.BlockSpec((None, 1, D), lambda i, s: (s[0], 0, 0))
    vec = pl.BlockSpec((1, D), lambda i, s: (0, 0))
    return pl.pallas_call(
        body, name="prenorm_bwd",
        grid_spec=pltpu.PrefetchScalarGridSpec(
            num_scalar_prefetch=1, grid=(tpb,),
            in_specs=[row, grow, grow, per_batch, vec, pl.BlockSpec((8, 128), lambda i, s: (0, 0)),
                      pl.BlockSpec(memory_space=pl.ANY)],
            out_specs=[grow, vec, vec, vec]),
        out_shape=[jax.ShapeDtypeStruct((t, D), F32), jax.ShapeDtypeStruct((1, D), F32),
                   jax.ShapeDtypeStruct((1, D), F32), jax.ShapeDtypeStruct((1, D), F32)],
        input_output_aliases={7: 0},
        compiler_params=_params(("arbitrary",)),
    )(jnp.full((1,), b, jnp.int32), dh, x2, dout, scale, g_pre, token, gx_prev)


CONV_TC = 128


def _shift_down(u, k, rows):
    idx = lax.broadcasted_iota(jnp.int32, u.shape, 0)
    return jnp.where(idx >= k, pltpu.roll(u, k, 0), 0.0)


def _shift_up(u, k, rows):
    idx = lax.broadcasted_iota(jnp.int32, u.shape, 0)
    return jnp.where(idx < rows - k, pltpu.roll(u, rows - k, 0), 0.0)


def conv_fwd(proj, conv_w, seq):
    t = proj.shape[1]
    nb = t // seq

    def body(p_ref, w_ref, y_ref):
        av = p_ref[0].astype(F32)
        ab = p_ref[1].astype(F32)
        ac = p_ref[2].astype(F32)
        az = p_ref[3].astype(F32)
        w = w_ref[...]
        u = ac * av
        y1 = _shift_down(u, 2, seq) * w[0:1] + _shift_down(u, 1, seq) * w[1:2] + u * w[2:3]
        y_ref[...] = (ab * y1 * (az * _sig(az))).astype(BF16)

    return pl.pallas_call(
        body, name="conv_fwd", grid=(nb, D // CONV_TC),
        in_specs=[pl.BlockSpec((4, seq, CONV_TC), lambda b, ci: (1, b, ci)),
                  pl.BlockSpec((8, CONV_TC), lambda b, ci: (0, ci))],
        out_specs=pl.BlockSpec((seq, CONV_TC), lambda b, ci: (b, ci)),
        out_shape=jax.ShapeDtypeStruct((t, D), BF16),
        compiler_params=_params(("parallel", "parallel")),
    )(proj, conv_w)


def conv_bwd(dproj, proj, dy, conv_w, seq):
    t = proj.shape[1]
    nb = t // seq

    def body(dp_in_ref, p_ref, dy_ref, w_ref, dp_ref, dw_ref):
        b = pl.program_id(1)
        av = p_ref[0].astype(F32)
        ab = p_ref[1].astype(F32)
        ac = p_ref[2].astype(F32)
        az = p_ref[3].astype(F32)
        dyv = dy_ref[...].astype(F32)
        w = w_ref[...]
        u = ac * av
        u1 = _shift_down(u, 1, seq)
        u2 = _shift_down(u, 2, seq)
        y1 = u2 * w[0:1] + u1 * w[1:2] + u * w[2:3]
        sz = _sig(az)
        silu = az * sz
        dy1 = dyv * ab * silu
        du = dy1 * w[2:3] + _shift_up(dy1, 1, seq) * w[1:2] + _shift_up(dy1, 2, seq) * w[0:1]
        dp_ref[0] = (du * ac).astype(BF16)
        dp_ref[1] = (dyv * y1 * silu).astype(BF16)
        dp_ref[2] = (du * av).astype(BF16)
        dp_ref[3] = (dyv * ab * y1 * (sz * (1.0 + az * (1.0 - sz)))).astype(BF16)

        @pl.when(b == 0)
        def _():
            dw_ref[...] = jnp.zeros_like(dw_ref)

        dw_ref[0:1, :] += jnp.sum(dy1 * u2, axis=0, keepdims=True)
        dw_ref[1:2, :] += jnp.sum(dy1 * u1, axis=0, keepdims=True)
        dw_ref[2:3, :] += jnp.sum(dy1 * u, axis=0, keepdims=True)

    return pl.pallas_call(
        body, name="conv_bwd", grid=(D // CONV_TC, nb),
        in_specs=[pl.BlockSpec(memory_space=pl.ANY),
                  pl.BlockSpec((4, seq, CONV_TC), lambda ci, b: (1, b, ci)),
                  pl.BlockSpec((seq, CONV_TC), lambda ci, b: (b, ci)),
                  pl.BlockSpec((8, CONV_TC), lambda ci, b: (0, ci))],
        out_specs=[pl.BlockSpec((4, seq, CONV_TC), lambda ci, b: (1, b, ci)),
                   pl.BlockSpec((8, CONV_TC), lambda ci, b: (0, ci))],
        out_shape=[jax.ShapeDtypeStruct(dproj.shape, BF16),
                   jax.ShapeDtypeStruct((8, D), F32)],
        input_output_aliases={0: 0},
        compiler_params=_params(("parallel", "arbitrary")),
    )(dproj, proj, dy, conv_w)


def _rope_tables(pos_ref, invf_ref, ma_ref, mb_ref, sign):
    ang = pos_ref[...].astype(F32) * invf_ref[...]
    cs = jnp.cos(ang)
    sn = jnp.sin(ang) * sign
    return cs, sn * ma_ref[...], sn * mb_ref[...]


def _rotate(v, cs, sa, sb):
    return v * cs + pltpu.roll(v, 128 - HALF, 1) * sa + pltpu.roll(v, HALF, 1) * sb


MLA_TM = 512


def mla_prep_fwd(proj, pos, g_q, g_kv, wuq, wukv, tabs):
    t = proj.shape[1]
    tm = min(MLA_TM, t)

    def body(lat_ref, pos_ref, gq_ref, gkv_ref, wuq_ref, wukv_ref, invf_ref, ma_ref, mb_ref,
             q_ref, k_ref, kv_ref, qn_ref, kvn_ref):
        lat = lat_ref[...].astype(F32)
        ql = lat[:, :QL]
        kl = lat[:, QL:QL + KVL]
        kr = lat[:, QL + KVL:QL + KVL + 128]
        qn = (ql * lax.rsqrt(jnp.mean(ql * ql, axis=-1, keepdims=True) + EPS) * gq_ref[...]).astype(BF16)
        kvn = (kl * lax.rsqrt(jnp.mean(kl * kl, axis=-1, keepdims=True) + EPS) * gkv_ref[...]).astype(BF16)
        qn_ref[...] = qn
        kvn_ref[...] = kvn
        cs, sa, sb = _rope_tables(pos_ref, invf_ref, ma_ref, mb_ref, 1.0)
        q = _dot_nt(qn, wuq_ref[...]) * (SM_SCALE * LOG2E)
        kv = _dot_nt(kvn, wukv_ref[...]).astype(BF16)
        kv_ref[...] = kv
        kpe = _rotate(kr, cs, sa, sb).astype(BF16)
        for hh in range(H):
            lo, mid, hi = hh * DQK, hh * DQK + 128, (hh + 1) * DQK
            q_ref[:, lo:mid] = q[:, lo:mid].astype(BF16)
            q_ref[:, mid:hi] = _rotate(q[:, mid:hi], cs, sa, sb).astype(BF16)
            k_ref[:, lo:mid] = kv[:, lo:mid]
            k_ref[:, mid:hi] = kpe

    row = lambda w: pl.BlockSpec((tm, w), lambda i: (i, 0))
    const = lambda a: pl.BlockSpec(a.shape, lambda i: (0,) * a.ndim)
    return pl.pallas_call(
        body, name="mla_prep_fwd", grid=(t // tm,),
        in_specs=[pl.BlockSpec((None, tm, D), lambda i: (SEG_LAT, i, 0)), row(1),
                  const(g_q), const(g_kv), const(wuq), const(wukv)] + [const(a) for a in tabs],
        out_specs=[row(H * DQK), row(H * DQK), row(H * DQK), row(QL), row(KVL)],
        out_shape=[jax.ShapeDtypeStruct((t, H * DQK), BF16)] * 3
        + [jax.ShapeDtypeStruct((t, QL), BF16), jax.ShapeDtypeStruct((t, KVL), BF16)],
        compiler_params=_params(("parallel",)),
    )(proj, pos, g_q, g_kv, wuq, wukv, *tabs)


def mla_prep_bwd(dproj, proj, dq_rot, dk, dv, pos, g_q, g_kv, wuq, wukv, tabs):
    t = proj.shape[1]
    tm = min(MLA_TM, t)

    def body(dp_in_ref, lat_ref, dqr_ref, dk_ref, dv_ref, pos_ref, gq_ref, gkv_ref, wuq_ref, wukv_ref,
             invf_ref, ma_ref, mb_ref, dp_ref, dq_ref, dkv_ref, dgq_ref, dgkv_ref):
        i = pl.program_id(0)
        lat = lat_ref[...].astype(F32)
        ql = lat[:, :QL]
        kl = lat[:, QL:QL + KVL]
        rq = lax.rsqrt(jnp.mean(ql * ql, axis=-1, keepdims=True) + EPS)
        rk = lax.rsqrt(jnp.mean(kl * kl, axis=-1, keepdims=True) + EPS)
        nq = ql * rq
        nk = kl * rk
        cs, sa, sb = _rope_tables(pos_ref, invf_ref, ma_ref, mb_ref, -1.0)
        dkpe = jnp.zeros((tm, 128), F32)
        for hh in range(H):
            lo, mid, hi = hh * DQK, hh * DQK + 128, (hh + 1) * DQK
            dq_ref[:, lo:mid] = (dqr_ref[:, lo:mid] * SM_SCALE).astype(BF16)
            dq_ref[:, mid:hi] = _rotate(dqr_ref[:, mid:hi] * SM_SCALE, cs, sa, sb).astype(BF16)
            dkv_ref[:, lo:mid] = dk_ref[:, lo:mid]
            dkv_ref[:, mid:hi] = dv_ref[:, hh * DV:(hh + 1) * DV]
            dkpe = dkpe + dk_ref[:, mid:hi].astype(F32)
        lane = lax.broadcasted_iota(jnp.int32, (tm, 128), 1)
        dkr = jnp.where(lane < ROPE, _rotate(dkpe, cs, sa, sb), 0.0)
        dqn = _dot(dq_ref[...], wuq_ref[...])
        dkvn = _dot(dkv_ref[...], wukv_ref[...])
        gq = gq_ref[...]
        gkv = gkv_ref[...]
        dnq = dqn * gq
        dnk = dkvn * gkv
        dql = rq * (dnq - nq * jnp.mean(dnq * nq, axis=-1, keepdims=True))
        dkl = rk * (dnk - nk * jnp.mean(dnk * nk, axis=-1, keepdims=True))
        dp_ref[:, :QL] = dql.astype(BF16)
        dp_ref[:, QL:QL + KVL] = dkl.astype(BF16)
        dp_ref[:, QL + KVL:QL + KVL + 128] = dkr.astype(BF16)
        dp_ref[:, QL + KVL + 128:] = jnp.zeros((tm, D - QL - KVL - 128), BF16)

        @pl.when(i == 0)
        def _():
            dgq_ref[...] = jnp.zeros_like(dgq_ref)
            dgkv_ref[...] = jnp.zeros_like(dgkv_ref)

        dgq_ref[...] += jnp.sum(dqn * nq, axis=0, keepdims=True)
        dgkv_ref[...] += jnp.sum(dkvn * nk, axis=0, keepdims=True)

    row = lambda w: pl.BlockSpec((tm, w), lambda i: (i, 0))
    const = lambda a: pl.BlockSpec(a.shape, lambda i: (0,) * a.ndim)
    seg = pl.BlockSpec((None, tm, D), lambda i: (SEG_LAT, i, 0))
    return pl.pallas_call(
        body, name="mla_prep_bwd", grid=(t // tm,),
        in_specs=[pl.BlockSpec(memory_space=pl.ANY), seg, row(H * DQK), row(H * DQK), row(H * DV), row(1),
                  const(g_q), const(g_kv), const(wuq), const(wukv)] + [const(a) for a in tabs],
        out_specs=[seg, row(H * DQK), row(H * DQK),
                   pl.BlockSpec((1, QL), lambda i: (0, 0)), pl.BlockSpec((1, KVL), lambda i: (0, 0))],
        out_shape=[jax.ShapeDtypeStruct(dproj.shape, BF16),
                   jax.ShapeDtypeStruct((t, H * DQK), BF16), jax.ShapeDtypeStruct((t, H * DQK), BF16),
                   jax.ShapeDtypeStruct((1, QL), F32), jax.ShapeDtypeStruct((1, KVL), F32)],
        input_output_aliases={0: 0},
        compiler_params=_params(("arbitrary",)),
    )(dproj, proj, dq_rot, dk, dv, pos, g_q, g_kv, wuq, wukv, *tabs)


def _causal_mask(s, shift):
    row = lax.broadcasted_iota(jnp.int32, s.shape, 0)
    col = lax.broadcasted_iota(jnp.int32, s.shape, 1)
    return jnp.where(col <= row + shift, s, -1e30)


def flash_fwd(q, k, kv, nb, seq):
    t = q.shape[0]
    tq = min(FLASH_TQ, seq // 2)
    nq = seq // tq
    assert nq % 2 == 0, "blocks are processed in pairs"

    def update(state, s, vblk):
        m, l, acc = state
        m_new = jnp.maximum(m, jnp.max(s, axis=1, keepdims=True))
        p = jnp.exp2(s - m_new)
        alpha = jnp.exp2(m - m_new)
        return (m_new, alpha * l + jnp.sum(p, axis=1, keepdims=True),
                alpha * acc + _dot(p.astype(BF16), vblk))

    def finish(state, rows, o_ref, lse_ref):
        m, l, acc = state
        o_ref[rows, :] = (acc / l).astype(BF16)
        lse_ref[rows, :] = jnp.broadcast_to(m + jnp.log(l) * LOG2E, (m.shape[0], DV))

    def body(q_ref, k_ref, v_ref, o_ref, lse_ref):
        for qp in range(0, nq, 2):
            rows = 2 * tq
            q0 = qp * tq
            qv = q_ref[q0:q0 + rows, :]
            state = (jnp.full((rows, 1), -1e30, F32), jnp.zeros((rows, 1), F32), jnp.zeros((rows, DV), F32))
            for j in range(qp + 1):
                ks = slice(j * tq, (j + 1) * tq)
                s = _dot_nt(qv, k_ref[ks, :])
                if j == qp:
                    s = _causal_mask(s, 0)
                state = update(state, s, v_ref[ks, :])
            finish(tuple(a[:tq] for a in state), slice(q0, q0 + tq), o_ref, lse_ref)
            ks = slice(q0 + tq, q0 + 2 * tq)
            low = tuple(a[tq:] for a in state)
            low = update(low, _causal_mask(_dot_nt(qv[tq:], k_ref[ks, :]), 0), v_ref[ks, :])
            finish(low, slice(q0 + tq, q0 + 2 * tq), o_ref, lse_ref)

    out_blk = pl.BlockSpec((seq, DV), lambda b, h: (b, h))
    return pl.pallas_call(
        body, name="flash_fwd", grid=(nb, H),
        in_specs=[pl.BlockSpec((seq, DQK), lambda b, h: (b, h)),
                  pl.BlockSpec((seq, DQK), lambda b, h: (b, h)),
                  pl.BlockSpec((seq, DV), lambda b, h: (b, 2 * h + 1))],
        out_specs=[out_blk, out_blk],
        out_shape=[jax.ShapeDtypeStruct((t, H * DV), BF16), jax.ShapeDtypeStruct((t, H * DV), F32)],
        compiler_params=_params(("parallel", "parallel")),
    )(q, k, kv)


def flash_bwd(q, k, kv, o, do, lse, nb, seq, token):
    t = q.shape[0]
    tq = min(FLASH_TQ, seq)
    nq = seq // tq

    def body(q_ref, k_ref, v_ref, o_ref, do_ref, lse_ref, tok_ref, dq_ref, dk_ref, dv_ref):
        delta, lse = [], []
        for qi in range(nq):
            qs = slice(qi * tq, (qi + 1) * tq)
            dl = jnp.sum(do_ref[qs, :].astype(F32) * o_ref[qs, :].astype(F32), axis=1, keepdims=True)
            delta.append(jnp.broadcast_to(dl, (tq, DV)).T[:1, :])
            lse.append(lse_ref[qs, :].T[:1, :])
        for ki in range(nq):
            ks = slice(ki * tq, (ki + 1) * tq)
            kb = k_ref[ks, :]
            vb = v_ref[ks, :]
            dk = jnp.zeros((tq, DQK), F32)
            dv = jnp.zeros((tq, DV), F32)
            for qi in range(ki, nq):
                qs = slice(qi * tq, (qi + 1) * tq)
                qv = q_ref[qs, :]
                dov = do_ref[qs, :]
                st = _dot_nt(kb, qv)
                if qi == ki:
                    row = lax.broadcasted_iota(jnp.int32, st.shape, 0)
                    col = lax.broadcasted_iota(jnp.int32, st.shape, 1)
                    st = jnp.where(row <= col, st, -1e30)
                pt = jnp.exp2(st - lse[qi])
                dpt = _dot_nt(vb, dov)
                dzt = (pt * (dpt - delta[qi])).astype(BF16)
                dv = dv + _dot(pt.astype(BF16), dov)
                dk = dk + _dot(dzt, qv)
                dqb = _dot_tn(dzt, kb)
                if ki == 0:
                    dq_ref[qs, :] = dqb
                else:
                    dq_ref[qs, :] += dqb
            dk_ref[ks, :] = (dk * LN2).astype(BF16)
            dv_ref[ks, :] = dv.astype(BF16)

    full = lambda w, col: pl.BlockSpec((seq, w), col)
    same = lambda b, h: (b, h)
    return pl.pallas_call(
        body, name="flash_bwd", grid=(nb, H),
        in_specs=[full(DQK, same), full(DQK, same), full(DV, lambda b, h: (b, 2 * h + 1)),
                  full(DV, same), full(DV, same), full(DV, same),
                  pl.BlockSpec((8, 128), lambda b, h: (0, 0))],
        out_specs=[full(DQK, same), full(DQK, same), full(DV, same)],
        out_shape=[jax.ShapeDtypeStruct((t, H * DQK), F32), jax.ShapeDtypeStruct((t, H * DQK), BF16),
                   jax.ShapeDtypeStruct((t, H * DV), BF16)],
        compiler_params=_params(("parallel", "parallel")),
    )(q, k, kv, o, do, lse, token)


TAIL_TM = 512


def tail_fwd(y, attn, proj, x2, tgt, gate, g_post, wco, wmo, wout, seq):
    t = y.shape[0]
    nb = t // seq
    tm = min(TAIL_TM, seq)
    tpb = seq // tm

    def body(y_ref, at_ref, p_ref, x_ref, t_ref, gate_ref, gp_ref, wco_ref, wmo_ref, wout_ref,
             o_ref, ya_ref, yb_ref, m_ref, do2_ref, dout_ref, dgate_ref, dgp_ref, loss_ref):
        i = pl.program_id(0)
        bz = p_ref[0].astype(F32)
        ga = p_ref[1].astype(F32)
        gb = p_ref[2].astype(F32)
        ov = (at_ref[...].astype(F32) * (bz * _sig(bz))).astype(BF16)
        o_ref[...] = ov
        ya = _dot(y_ref[...], wco_ref[...])
        yb = _dot(ov, wmo_ref[...])
        ya_ref[...] = ya.astype(BF16)
        yb_ref[...] = yb.astype(BF16)
        mv = (_sig(ga) * ya + _sig(gb) * yb).astype(BF16)
        m_ref[...] = mv
        o2 = _dot(mv, wout_ref[...])
        r = lax.rsqrt(jnp.mean(o2 * o2, axis=-1, keepdims=True) + EPS)
        nrm = o2 * r
        gp = gp_ref[...]
        gate_v = gate_ref[...]
        rn = nrm * gp
        err = x_ref[...] + gate_v * rn - t_ref[...]
        dout = err * (1.0 / D)
        dout_ref[...] = dout
        dn = dout * gate_v * gp
        do2_ref[...] = (r * (dn - nrm * jnp.mean(dn * nrm, axis=-1, keepdims=True))).astype(BF16)

        @pl.when(i % tpb == 0)
        def _():
            dgate_ref[...] = jnp.zeros_like(dgate_ref)

        @pl.when(i == 0)
        def _():
            dgp_ref[...] = jnp.zeros_like(dgp_ref)
            loss_ref[...] = jnp.zeros_like(loss_ref)

        dgate_ref[...] += jnp.sum(dout * rn, axis=0, keepdims=True)
        dgp_ref[...] += jnp.sum(dout * gate_v * nrm, axis=0, keepdims=True)
        loss_ref[...] += 0.5 * jnp.sum(jnp.mean(err * err, axis=-1, keepdims=True), axis=0, keepdims=True)

    row = pl.BlockSpec((tm, D), lambda i: (i, 0))
    per_batch = pl.BlockSpec((None, 1, D), lambda i: (i // tpb, 0, 0))
    vec = pl.BlockSpec((1, D), lambda i: (0, 0))
    wgt = pl.BlockSpec((D, D), lambda i: (0, 0))
    act = jax.ShapeDtypeStruct((t, D), BF16)
    return pl.pallas_call(
        body, name="tail_fwd", grid=(t // tm,),
        in_specs=[row, row, pl.BlockSpec((3, tm, D), lambda i: (0, i, 0)), row, row, per_batch, vec,
                  wgt, wgt, wgt],
        out_specs=[row, row, row, row, row, row, per_batch, vec, pl.BlockSpec((1, 1), lambda i: (0, 0))],
        out_shape=[act, act, act, act, act, jax.ShapeDtypeStruct((t, D), F32),
                   jax.ShapeDtypeStruct((nb, 1, D), F32), jax.ShapeDtypeStruct((1, D), F32),
                   jax.ShapeDtypeStruct((1, 1), F32)],
        compiler_params=_params(("arbitrary",)),
    )(y, attn, proj, x2, tgt, gate, g_post, wco, wmo, wout)


def tail_bwd(do2, proj, ya, yb, attn, wout, wmo, wco):
    t = do2.shape[0]
    tm = min(TAIL_TM, t)

    def body(do2_ref, p_ref, ya_ref, yb_ref, at_ref, wout_ref, wmo_ref, wco_ref,
             dp_ref, dya_ref, dyb_ref, dat_ref, dy_ref):
        bz = p_ref[0].astype(F32)
        ga = p_ref[1].astype(F32)
        gb = p_ref[2].astype(F32)
        dm = _dot_nt(do2_ref[...], wout_ref[...])
        sa = _sig(ga)
        sb = _sig(gb)
        dya = (dm * sa).astype(BF16)
        dyb = (dm * sb).astype(BF16)
        dya_ref[...] = dya
        dyb_ref[...] = dyb
        dp_ref[1] = (dm * ya_ref[...].astype(F32) * (sa * (1.0 - sa))).astype(BF16)
        dp_ref[2] = (dm * yb_ref[...].astype(F32) * (sb * (1.0 - sb))).astype(BF16)
        dov = _dot_nt(dyb, wmo_ref[...])
        sz = _sig(bz)
        dat_ref[...] = (dov * (bz * sz)).astype(BF16)
        dp_ref[0] = (dov * at_ref[...].astype(F32) * (sz * (1.0 + bz * (1.0 - sz)))).astype(BF16)
        dy_ref[...] = _dot_nt(dya, wco_ref[...]).astype(BF16)

    row = pl.BlockSpec((tm, D), lambda i: (i, 0))
    seg3 = pl.BlockSpec((3, tm, D), lambda i: (0, i, 0))
    wgt = pl.BlockSpec((D, D), lambda i: (0, 0))
    act = jax.ShapeDtypeStruct((t, D), BF16)
    return pl.pallas_call(
        body, name="tail_bwd", grid=(t // tm,),
        in_specs=[row, seg3, row, row, row, wgt, wgt, wgt],
        out_specs=[seg3, row, row, row, row],
        out_shape=[jax.ShapeDtypeStruct((NSEG, t, D), BF16), act, act, act, act],
        compiler_params=_params(("parallel",)),
    )(do2, proj, ya, yb, attn, wout, wmo, wco)


def _adam_update(w, m, v, grad):
    mn = ADAM_B1 * m + (1.0 - ADAM_B1) * grad
    vn = ADAM_B2 * v + (1.0 - ADAM_B2) * (grad * grad)
    m_hat = mn / (1.0 - ADAM_B1 ** ADAM_STEP)
    v_hat = vn / (1.0 - ADAM_B2 ** ADAM_STEP)
    return -ADAM_LR * (m_hat / (jnp.sqrt(v_hat) + ADAM_EPS) + ADAM_WD * w), mn, vn


def adamw(w, m, v, g, name, token):
    rows, cols = w.shape
    tr = rows
    for cand in (256, 128, 64, 32, 16, 8):
        if rows % cand == 0 and rows > cand:
            tr = cand
            break

    def body(w_ref, m_ref, v_ref, g_ref, tok_ref, d_ref, mo_ref, vo_ref):
        d_ref[...], mo_ref[...], vo_ref[...] = _adam_update(w_ref[...], m_ref[...], v_ref[...], g_ref[...])

    blk = pl.BlockSpec((tr, cols), lambda i: (i, 0))
    return pl.pallas_call(
        body, name=name, grid=(rows // tr,),
        in_specs=[blk] * 4 + [pl.BlockSpec((8, 128), lambda i: (0, 0))], out_specs=[blk] * 3,
        out_shape=[jax.ShapeDtypeStruct((rows, cols), F32)] * 3,
        compiler_params=_params(("parallel",)),
    )(w, m, v, g, token)


def adamw_scattered(w, m, v, own, land, me, tr, name, transpose=False):
    slot_rows = land.shape[1]
    cols = land.shape[2]
    rows = slot_rows if transpose else w.shape[0]
    per_slot = slot_rows // tr

    def body(me_ref, w_ref, m_ref, v_ref, own_ref, land_ref, go_ref, d_ref, mo_ref, vo_ref):
        grad = own_ref[...].astype(F32)
        for s in range(8):
            grad = grad + jnp.where(me_ref[0] == s, 0.0, land_ref[s].astype(F32))
        if transpose:
            grad = grad.T
        go_ref[...] = grad
        d_ref[...], mo_ref[...], vo_ref[...] = _adam_update(w_ref[...], m_ref[...], v_ref[...], grad)

    wblk = pl.BlockSpec(w.shape if transpose else (tr, w.shape[1]), lambda i, s: (i, 0))
    return pl.pallas_call(
        body, name=name,
        grid_spec=pltpu.PrefetchScalarGridSpec(
            num_scalar_prefetch=1, grid=(rows // tr,),
            in_specs=[wblk, wblk, wblk,
                      pl.BlockSpec((tr, cols), lambda i, s: (s[0] * per_slot + i, 0)),
                      pl.BlockSpec((8, tr, cols), lambda i, s: (0, i, 0))],
            out_specs=[wblk] * 4),
        out_shape=[jax.ShapeDtypeStruct(w.shape, F32)] * 4,
        compiler_params=_params(),
    )(me, w, m, v, own, land)


def adamw_win(wt, mt, vt, ka, ra, kb, rb):
    rows = wt.shape[0]
    tc = 256
    nh = (D // 2) // tc

    def body(w_ref, m_ref, v_ref, ka_ref, ra_ref, kb_ref, rb_ref, go_ref, d_ref, mo_ref, vo_ref):
        first = pl.program_id(0) < nh
        grad = jnp.where(first, ka_ref[...] + ra_ref[...].astype(F32), kb_ref[...] + rb_ref[...].astype(F32))
        go_ref[...] = grad
        d_ref[...], mo_ref[...], vo_ref[...] = _adam_update(w_ref[...], m_ref[...], v_ref[...], grad)

    blk = pl.BlockSpec((rows, tc), lambda j: (0, j))
    lo = pl.BlockSpec((rows, tc), lambda j: (0, jnp.minimum(j, nh - 1)))
    hi = pl.BlockSpec((rows, tc), lambda j: (0, jnp.maximum(j - nh, 0)))
    return pl.pallas_call(
        body, name="adamw_w_in", grid=(D // tc,),
        in_specs=[blk, blk, blk, lo, lo, hi, hi], out_specs=[blk] * 4,
        out_shape=[jax.ShapeDtypeStruct((rows, D), F32)] * 4,
        compiler_params=_params(("parallel",)),
    )(wt, mt, vt, ka, ra, kb, rb)


_ORD_A = ("x", "y", "c")
_ORD_B = ("y", "x", "c")


def _rows128(a, rows):
    flat = a.reshape(-1)
    return jnp.pad(flat, (0, rows * 128 - flat.shape[0])).reshape(rows, 128)


def kernel(x, c, positions, w_ada, b_ada, g_pre, w_in, conv_w, w_conv_out, g_q, w_uq, g_kv, w_ukv, w_mla_out, w_out, g_post, loss_target, m_w_ada, m_b_ada, m_g_pre, m_w_in, m_conv_w, m_w_conv_out, m_g_q, m_w_uq, m_g_kv, m_w_ukv, m_w_mla_out, m_w_out, m_g_post, v_w_ada, v_b_ada, v_g_pre, v_w_in, v_conv_w, v_w_conv_out, v_g_q, v_w_uq, v_g_kv, v_w_ukv, v_w_mla_out, v_w_out, v_g_post):
    nb, seq, _ = x.shape
    t = nb * seq
    mx, my, mc = lax.axis_index("x"), lax.axis_index("y"), lax.axis_index("c")
    me = 4 * mx + 2 * my + mc
    co = {"x": mx, "y": my, "c": mc}

    x2 = x.reshape(t, D)
    tgt2 = loss_target.reshape(t, D)
    pos2 = positions.reshape(t, 1)

    ada_cols = w_ada.shape[2]
    b_cols = lax.dynamic_slice(b_ada, (0, me * ada_cols), (1, ada_cols))
    c_g, taps_g, mod_g = ada_gather(jnp.pad(c, ((0, 8 - nb), (0, 0))), _rows128(conv_w[0], 8), w_ada[0], b_cols)
    c_all = c_g[:, :nb].reshape(8 * nb, D)
    conv_full = taps_g[:, 0:3].transpose(1, 0, 2).reshape(3, D)
    conv_full8 = jnp.pad(conv_full, ((0, 5), (0, 0)))
    mod = mod_g[:, :nb].transpose(1, 0, 2).reshape(nb, 8 * ada_cols)
    shift = mod[:, 0:D].reshape(nb, 1, D)
    scale = mod[:, D:2 * D].reshape(nb, 1, D)
    gate = mod[:, 2 * D:3 * D].reshape(nb, 1, D)

    wt = w_in[0].T.astype(BF16)
    lo = lax.bitcast_convert_type(wt[:, :D // 2], jnp.uint16).astype(jnp.uint32)
    hi = lax.bitcast_convert_type(wt[:, D // 2:], jnp.uint16).astype(jnp.uint32)
    wt_bits = lax.bitcast_convert_type(lo | (hi << 16), F32)
    wt_bits, mod = lax.optimization_barrier((wt_bits, mod))
    shift = mod[:, 0:D].reshape(nb, 1, D)
    scale = mod[:, D:2 * D].reshape(nb, 1, D)
    gate = mod[:, 2 * D:3 * D].reshape(nb, 1, D)
    q4 = D // 4
    r3rd = wt_bits.shape[0] // 3
    plan = [(0, (k * r3rd, r3rd), (g * q4, q4), (_ORD_A, _ORD_B)[g]) for k in range(3) for g in range(2)]
    gw = allgather_big([wt_bits], plan, "gather_w_in", per_wave=2)
    late = [w_conv_out[0].astype(BF16), w_mla_out[0].astype(BF16), w_out[0].astype(BF16),
            jnp.pad(w_uq[0].T.astype(BF16), ((0, DQK - 192), (0, 0))), w_ukv[0].T.astype(BF16)]
    gw0, late = lax.optimization_barrier((gw[0], late))
    late_state, late_token = gather_start(late, "gather_late_start")
    wt_bits_all = gw0.reshape(N_IN, D // 2)

    inv_freq = ROPE_THETA ** (-jnp.arange(0, ROPE, 2, dtype=F32) / ROPE)
    invf = jnp.concatenate([inv_freq, inv_freq, jnp.zeros((128 - ROPE,), F32)]).reshape(1, 128)
    lane = np.arange(128)
    tabs = (invf,
            jnp.asarray(np.where(lane < HALF, -1.0, 0.0).reshape(1, 128), F32),
            jnp.asarray(np.where((lane >= HALF) & (lane < ROPE), 1.0, 0.0).reshape(1, 128), F32))

    h = prenorm_fwd(x2, scale, shift, g_pre, seq)
    proj, wt_p = proj_matmul(h, wt_bits_all, late_token)
    y = conv_fwd(proj, conv_full8, seq)
    gl = gather_wait(late_state, y, "gather_late_wait")
    wco = gl[0].reshape(D, D)
    wmo = gl[1].reshape(D, D)
    wout = gl[2].reshape(D, D)
    wuq_p = gl[3].reshape(H * DQK, QL)
    wukv = gl[4].reshape(H * 256, KVL)
    q_rot, k_cat, kv, qn, kvn = mla_prep_fwd(proj, pos2, g_q, g_kv, wuq_p, wukv, tabs)
    attn, lse = flash_fwd(q_rot, k_cat, kv, nb, seq)
    o, ya, yb, m, do2, dout, dgate, dg_post, loss_part = tail_fwd(
        y, attn, proj, x2, tgt2, gate, g_post, wco, wmo, wout, seq)

    dproj, dya, dyb, dattn, dy = tail_bwd(do2, proj, ya, yb, attn, wout, wmo, wco)
    g_wout = grad_matmul(m, do2, "grad_w_square")
    g_wmo = grad_matmul(o, dyb, "grad_w_square")
    g_wco = grad_matmul(y, dya, "grad_w_square")
    sc1, sc1_tok = scatter_start([g_wco, g_wmo, g_wout], "scatter_out_grads_start")
    dproj, dconv = conv_bwd(dproj, proj, dy, conv_full8, seq)
    dq_rot, dk, dv = flash_bwd(q_rot, k_cat, kv, attn, dattn, lse, nb, seq, sc1_tok)
    dproj, dq, dkv, dg_q, dg_kv = mla_prep_bwd(dproj, proj, dq_rot, dk, dv, pos2, g_q, g_kv, wuq_p, wukv, tabs)
    g_wuq_t = grad_matmul(dq, qn, "grad_w_uq")
    g_wukv_t = grad_matmul(dkv, kvn, "grad_w_ukv")
    sc2, sc2_tok = scatter_start([g_wuq_t, g_wukv_t], "scatter_mla_grads_start")
    g_win_p = win_grad_matmul(h, dproj, sc2_tok)

    g_wt = g_win_p.reshape(2, 2, 2, N_IN // 8, D)
    ords = [("c", "y", "x"), ("c", "x", "y")]
    hc = D // 2
    win_shape = (2, 2, N_IN // 8, hc)
    pick_w = lambda col: (lambda ref, cc: ref.at[:, :, 1 - cc["c"], :, pl.ds(col * hc, hc)])
    which1 = [0, 0]
    picks1 = [pick_w(0), pick_w(1)]
    st1, tok1 = swap_start([g_wt], which1, ["c"] * 2, picks1, [win_shape] * 2, "rs_c_start")
    assert nb == 2
    dh0 = dh_matmul(dproj, wt_p, tok1, seq, 0)
    (g_wt,), r1 = swap_wait(st1, dh0, which1, ["c"] * 2, picks1, "rs_c_wait")
    sel_xyc = jnp.stack([mx, my, mc]).astype(jnp.int32)
    sel2 = [jnp.stack([co[o[2]]]).astype(jnp.int32) for o in ords]
    first = [rs_win_add_first(g_wt, r1[0], sel_xyc, 1, 0, "rs_add_first_0"),
             rs_win_add_first(g_wt, r1[1], sel_xyc, 0, 1, "rs_add_first_1")]
    keep1, send1 = zip(*first)
    all4 = [0, 1]
    none4 = [None] * 2
    axes2 = [o[1] for o in ords]
    st2, tok2 = swap_start(list(send1), all4, axes2, none4, [s.shape for s in send1], "rs_ici1_start")

    dh1 = dh_matmul(dproj, wt_p, tok2, seq, 1)
    gx0, dsh0, dsc0, dgp0 = prenorm_bwd(dh0, x2, dout, scale, g_pre, seq, tok2, 0, None)
    _, r2 = swap_wait(st2, (gx0, dh1), all4, axes2, none4, "rs_ici1_wait")
    keep2, send2 = zip(*[rs_add_second(keep1[a], r2[a], sel2[a], "rs_add_second") for a in range(2)])
    axes3 = [o[2] for o in ords]
    st3, tok3 = swap_start(list(send2), all4, axes3, none4, [s.shape for s in send2], "rs_ici2_start")
    grad_x2, dsh1, dsc1, dgp1 = prenorm_bwd(dh1, x2, dout, scale, g_pre, seq, tok3, 1, gx0)
    dshift = jnp.stack([dsh0, dsh1])
    dscale = jnp.stack([dsc0, dsc1])
    dg_pre = dgp0 + dgp1

    dmod = jnp.concatenate([dshift, dscale, dgate], axis=2).reshape(nb * 3 * D // 128, 128)
    small = jnp.concatenate([
        dmod, _rows128(dg_pre, 8), _rows128(dg_post, 8), _rows128(dg_q, 8), _rows128(dg_kv, 8),
        dconv[0:3].reshape(24, 128), _rows128(loss_part, 8)], axis=0)
    small_g = small_allgather(small, "gather_small_grads")
    sums = slot_sum(small_g)
    dmod_all = small_g[:, 0:48].reshape(8 * nb, 3 * D)
    g_bada = (sums[0:24] + sums[24:48]).reshape(1, 3 * D)
    g_gpre = sums[48:56].reshape(1, D)
    g_gpost = sums[56:64].reshape(1, D)
    g_gq = sums[64:67].reshape(1, QL)
    g_gkv = sums[72:74].reshape(1, KVL)
    g_conv_full = sums[80:104].reshape(3, D)
    loss = sums[104, 0]
    g_conv = lax.dynamic_slice(g_conv_full, (0, me * 128), (3, 128))
    dmod_cols = lax.dynamic_slice(dmod_all, (0, me * ada_cols), (8 * nb, ada_cols))
    g_wada = ada_bwd(c_all, dmod_cols)

    res = {}
    res["w_ada"] = [o_[None] for o_ in (g_wada, *adamw(w_ada[0], m_w_ada[0], v_w_ada[0], g_wada, "adamw_w_ada", tok3))]

    def pack(b_, gp_, gpo_, gq_, gkv_, cw_):
        return jnp.concatenate([_rows128(b_, 24), _rows128(gp_, 8), _rows128(gpo_, 8), _rows128(gq_, 8),
                                _rows128(gkv_, 8), _rows128(cw_, 8)], axis=0)

    sw = pack(b_ada, g_pre, g_post, g_q, g_kv, conv_w)
    sm = pack(m_b_ada, m_g_pre, m_g_post, m_g_q, m_g_kv, m_conv_w)
    sv = pack(v_b_ada, v_g_pre, v_g_post, v_g_q, v_g_kv, v_conv_w)
    sg = pack(g_bada, g_gpre, g_gpost, g_gq, g_gkv, g_conv)
    small_out = (sg, *adamw(sw, sm, sv, sg, "adamw_small", tok3))

    _, r3 = swap_wait(st3, small_out[1], all4, axes3, none4, "rs_ici2_wait")

    (g_wco, g_wmo, g_wout), (l_wco, l_wmo, l_wout) = scatter_wait(sc1, small_out[2], "scatter_out_grads_wait")
    (g_wuq_t, g_wukv_t), (l_wuq, l_wukv) = scatter_wait(sc2, small_out[3], "scatter_mla_grads_wait")

    res["w_in"] = [o_.T[None] for o_ in adamw_win(w_in[0].T, m_w_in[0].T, v_w_in[0].T,
                                                  keep2[0], r3[0], keep2[1], r3[1])]
    me1 = me.reshape(1).astype(jnp.int32)
    res["w_uq"] = [o_.T[None] for o_ in adamw_scattered(
        w_uq[0].T, m_w_uq[0].T, v_w_uq[0].T, g_wuq_t, l_wuq, me1, 64, "adamw_w_uq")]
    res["w_ukv"] = [o_[None] for o_ in adamw_scattered(
        w_ukv[0], m_w_ukv[0], v_w_ukv[0], g_wukv_t, l_wukv, me1, KVL, "adamw_w_ukv", transpose=True)]
    for nm, wv, mv, vv, gg, ll in (("w_conv_out", w_conv_out, m_w_conv_out, v_w_conv_out, g_wco, l_wco),
                                   ("w_mla_out", w_mla_out, m_w_mla_out, v_w_mla_out, g_wmo, l_wmo),
                                   ("w_out", w_out, m_w_out, v_w_out, g_wout, l_wout)):
        res[nm] = [o_[None] for o_ in adamw_scattered(wv[0], mv[0], vv[0], gg, ll, me1, 128, "adamw_square")]

    def unpack(a):
        return {"b_ada": a[0:24].reshape(1, 3 * D), "g_pre": a[24:32].reshape(1, D),
                "g_post": a[32:40].reshape(1, D), "g_q": a[40:43].reshape(1, QL),
                "g_kv": a[48:50].reshape(1, KVL), "conv_w": a[56:59].reshape(-1)[:3 * 128].reshape(1, 3, 128)}

    for nm in ("b_ada", "g_pre", "g_post", "g_q", "g_kv", "conv_w"):
        res[nm] = [unpack(a)[nm] for a in small_out]

    order = ["w_ada", "b_ada", "g_pre", "w_in", "conv_w", "w_conv_out", "g_q", "w_uq", "g_kv", "w_ukv",
             "w_mla_out", "w_out", "g_post"]
    out = [loss, grad_x2.reshape(nb, seq, D)]
    for k_ in range(4):
        out += [res[nm][k_] for nm in order]
    return tuple(out)
```

```python
import numpy as np
import jax
import jax.numpy as jnp
from jax import lax
from jax.experimental import pallas as pl
from jax.experimental.pallas import tpu as pltpu

F32 = jnp.float32
BF16 = jnp.bfloat16
MESH = pl.DeviceIdType.MESH

D = 1024
H = 8
QL = 384
KVL = 256
ROPE = 64
HALF = ROPE // 2
DQK = 256
DV = 128
NSEG = 8
NP = NSEG * D
EPS = 1e-6
ROPE_THETA = 10000.0
SM_SCALE = (128 + ROPE) ** -0.5
LOG2E = 1.4426950408889634
LN2 = 0.6931471805599453
FLASH_TQ = 512

SEG_BZ, SEG_GA, SEG_GB, SEG_LAT, SEG_V = 0, 1, 2, 3, 4

ADAM_LR = 0.001
ADAM_B1 = 0.9
ADAM_B2 = 0.999
ADAM_EPS = 1e-08
ADAM_WD = 0.01
ADAM_STEP = 10

VMEM_LIMIT = 56 * 1024 * 1024


def _params(sem=None, vmem=VMEM_LIMIT):
    kw = dict(vmem_limit_bytes=vmem)
    if sem is not None:
        kw["dimension_semantics"] = sem
    return pltpu.CompilerParams(**kw)


def _sig(v):
    return 0.5 * jnp.tanh(0.5 * v) + 0.5


def _dot(a, b):
    return jnp.dot(a, b, preferred_element_type=F32)


def _dot_nt(a, b):
    return lax.dot_general(a, b, (((1,), (1,)), ((), ())), preferred_element_type=F32)


def _dot_tn(a, b):
    return lax.dot_general(a, b, (((0,), (0,)), ((), ())), preferred_element_type=F32)


_AXIS_POS = {"x": 0, "y": 1, "c": 2}


def _coords():
    return lax.axis_index("x"), lax.axis_index("y"), lax.axis_index("c")


def _partner(axis):
    p = list(_coords())
    p[_AXIS_POS[axis]] = 1 - p[_AXIS_POS[axis]]
    return tuple(p)


def small_allgather(v, name):
    rows = v.shape[0]

    def body(v_ref, out_ref, send_sems, recv_sems):
        x, y, c = _coords()
        me = 4 * x + 2 * y + c
        out_ref[me] = v_ref[...]
        copies = []
        for k in range(1, 8):
            peer = (1 - x if k & 4 else x, 1 - y if k & 2 else y, 1 - c if k & 1 else c)
            cp = pltpu.make_async_remote_copy(
                src_ref=v_ref, dst_ref=out_ref.at[me],
                send_sem=send_sems.at[k - 1], recv_sem=recv_sems.at[k - 1],
                device_id=peer, device_id_type=MESH)
            cp.start()
            copies.append(cp)
        for cp in copies:
            cp.wait()

    return pl.pallas_call(
        body, name=name,
        out_shape=jax.ShapeDtypeStruct((8, rows, 128), F32),
        in_specs=[pl.BlockSpec(memory_space=pltpu.VMEM)],
        out_specs=pl.BlockSpec(memory_space=pltpu.VMEM),
        scratch_shapes=[pltpu.SemaphoreType.DMA((7,)), pltpu.SemaphoreType.DMA((7,))],
    )(v)


def _own_block_placed(s):
    x, y, c = _coords()
    return lax.dynamic_update_slice(lax.empty((2, 2, 2) + s.shape, s.dtype), s[None, None, None],
                                    (x, y, c) + (0,) * s.ndim)


def allgather_big(arrs, plan, name, during=None, extra_in=(), extra_out=()):
    n = len(arrs)
    m = len(plan)
    nst = len(plan[0][3])
    k_in, k_out = len(extra_in), len(extra_out)

    def body(*refs):
        ins, outs = refs[n:2 * n], refs[2 * n + k_in:3 * n + k_in]
        extra = refs[2 * n:2 * n + k_in] + refs[3 * n + k_in:3 * n + k_in + k_out]
        send_sems, recv_sems = refs[3 * n + k_in + k_out:]
        x, y, c = _coords()
        co = {"x": x, "y": y, "c": c}

        def window(ref, lead, rows, cols):
            win = tuple(slice(None) if w is None else pl.ds(w[0], w[1]) for w in (rows, cols))
            return ref.at[tuple(lead) + win]

        def held(e, free):
            i, rows, cols, _ = plan[e]
            lead = [slice(None) if ax in free else co[ax] for ax in ("x", "y", "c")]
            return window(outs[i], lead, rows, cols)

        def rcopy(e, stage, src, dst, axis):
            return pltpu.make_async_remote_copy(
                src_ref=src, dst_ref=dst,
                send_sem=send_sems.at[e, stage], recv_sem=recv_sems.at[e, stage],
                device_id=_partner(axis), device_id_type=MESH)

        stages = [[] for _ in range(nst)]
        for e, (i, rows, cols, order) in enumerate(plan):
            cp = rcopy(e, 0, window(ins[i], [], rows, cols), held(e, ()), order[0])
            cp.start()
            stages[0].append(cp)
        for s in range(1, nst):
            if during is not None:
                during(s - 1, *extra)
            for e, (i, rows, cols, order) in enumerate(plan):
                stages[s - 1][e].wait_recv()
                blk = held(e, order[:s])
                cp = rcopy(e, s, blk, blk, order[s])
                cp.start()
                stages[s].append(cp)
        for e in range(m):
            stages[nst - 1][e].wait_recv()
        for e in range(m):
            for s in range(nst):
                stages[s][e].wait_send()

    any_spec = pl.BlockSpec(memory_space=pl.ANY)
    lands = [_own_block_placed(a) for a in arrs]
    return pl.pallas_call(
        body, name=name,
        out_shape=[jax.ShapeDtypeStruct(l.shape, l.dtype) for l in lands] + [o for o, _ in extra_out],
        in_specs=[any_spec] * (2 * n) + [sp for _, sp in extra_in],
        out_specs=[any_spec] * n + [sp for _, sp in extra_out],
        input_output_aliases={i: i for i in range(n)},
        scratch_shapes=[pltpu.SemaphoreType.DMA((m, nst)), pltpu.SemaphoreType.DMA((m, nst))],
        compiler_params=_params(),
    )(*lands, *arrs, *[a for a, _ in extra_in])


_HBM =pl.BlockSpec(memory_space=pltpu.HBM)
_SEM = pl.BlockSpec(memory_space=pltpu.SEMAPHORE)


def _swap_copies(srcs, lands, send_sems, recv_sems, axes, picks):
    x, y, c = _coords()
    co = {"x": x, "y": y, "c": c}
    return [pltpu.make_async_remote_copy(
        src_ref=srcs[a] if picks[a] is None else picks[a](srcs[a], co), dst_ref=lands[a],
        send_sem=send_sems.at[a], recv_sem=recv_sems.at[a],
        device_id=_partner(axes[a]), device_id_type=MESH) for a in range(len(srcs))]


def swap_start(arrs, which, axes, picks, out_shapes, name):
    ns, n = len(arrs), len(which)

    def body(*refs):
        srcs, lands = refs[:ns], refs[ns:ns + n]
        send_sems, recv_sems = refs[ns + n:ns + n + 2]
        token = refs[-1]
        for cp in _swap_copies([srcs[i] for i in which], lands, send_sems, recv_sems, axes, picks):
            cp.start()
        token[...] = jnp.zeros_like(token)

    lands = [lax.empty(s, arrs[i].dtype) for s, i in zip(out_shapes, which)]
    ops = [pltpu.with_memory_space_constraint(a, pltpu.HBM) for a in list(arrs) + lands]
    out = pl.pallas_call(
        body, name=name,
        out_shape=[pltpu.SemaphoreType.DMA((n,)), pltpu.SemaphoreType.DMA((n,))]
        + [pltpu.HBM(o.shape, o.dtype) for o in ops] + [jax.ShapeDtypeStruct((8, 128), F32)],
        in_specs=[_HBM] * (ns + n),
        out_specs=[_SEM, _SEM] + [_HBM] * (ns + n) + [pl.BlockSpec(memory_space=pltpu.VMEM)],
        input_output_aliases={i: 2 + i for i in range(ns + n)},
        compiler_params=pltpu.CompilerParams(has_side_effects=pltpu.SideEffectType.DATAFLOW_SIDE_EFFECTING),
    )(*ops)
    return out[:-1], out[-1]


def swap_wait(state, after, which, axes, picks, name):
    n = len(which)
    ns = len(state) - 2 - n

    def body(*refs):
        srcs, lands = refs[:ns], refs[ns:ns + n]
        send_sems, recv_sems = refs[ns + n:ns + n + 2]
        for cp in _swap_copies([srcs[i] for i in which], lands, send_sems, recv_sems, axes, picks):
            cp.wait_send()
            cp.wait_recv()

    thru = list(state[2:])
    after = list(after) if isinstance(after, (list, tuple)) else [after]
    out = pl.pallas_call(
        body, name=name,
        out_shape=[pltpu.HBM(o.shape, o.dtype) for o in thru],
        in_specs=[_HBM] * (ns + n) + [_SEM, _SEM] + [pl.BlockSpec(memory_space=pl.ANY)] * len(after),
        out_specs=[_HBM] * (ns + n),
        input_output_aliases={i: i for i in range(ns + n)},
        compiler_params=pltpu.CompilerParams(has_side_effects=pltpu.SideEffectType.DATAFLOW_SIDE_EFFECTING),
    )(*thru, state[0], state[1], *after)
    return out[:ns], out[ns:]


def _gather_copies(shards, lands, send_sems, recv_sems):
    x, y, c = _coords()
    copies = []
    for a in range(len(shards)):
        for k in range(1, 8):
            peer = (1 - x if k & 4 else x, 1 - y if k & 2 else y, 1 - c if k & 1 else c)
            copies.append(pltpu.make_async_remote_copy(
                src_ref=shards[a], dst_ref=lands[a].at[x, y, c],
                send_sem=send_sems.at[7 * a + k - 1], recv_sem=recv_sems.at[7 * a + k - 1],
                device_id=peer, device_id_type=MESH))
    return copies


def gather_start(shards, name):
    n = len(shards)
    x, y, c = _coords()

    def body(*refs):
        srcs, lands = refs[:n], refs[n:2 * n]
        send_sems, recv_sems = refs[2 * n:2 * n + 2]
        token = refs[-1]
        for cp in _gather_copies(srcs, lands, send_sems, recv_sems):
            cp.start()
        token[...] = jnp.zeros_like(token)

    lands = [_own_block_placed(s) for s in shards]
    ops = [pltpu.with_memory_space_constraint(a, pltpu.HBM) for a in list(shards) + lands]
    out = pl.pallas_call(
        body, name=name,
        out_shape=[pltpu.SemaphoreType.DMA((7 * n,)), pltpu.SemaphoreType.DMA((7 * n,))]
        + [pltpu.HBM(o.shape, o.dtype) for o in ops] + [jax.ShapeDtypeStruct((8, 128), F32)],
        in_specs=[_HBM] * (2 * n),
        out_specs=[_SEM, _SEM] + [_HBM] * (2 * n) + [pl.BlockSpec(memory_space=pltpu.VMEM)],
        input_output_aliases={i: 2 + i for i in range(2 * n)},
        compiler_params=pltpu.CompilerParams(has_side_effects=pltpu.SideEffectType.DATAFLOW_SIDE_EFFECTING),
    )(*ops)
    return out[:-1], out[-1]


def gather_wait(state, after, name):
    n = (len(state) - 2) // 2

    def body(*refs):
        srcs, lands = refs[:n], refs[n:2 * n]
        send_sems, recv_sems = refs[2 * n:2 * n + 2]
        for cp in _gather_copies(srcs, lands, send_sems, recv_sems):
            cp.wait_send()
            cp.wait_recv()

    thru = list(state[2:])
    out = pl.pallas_call(
        body, name=name,
        out_shape=[pltpu.HBM(o.shape, o.dtype) for o in thru],
        in_specs=[_HBM] * (2 * n) + [_SEM, _SEM, pl.BlockSpec(memory_space=pl.ANY)],
        out_specs=[_HBM] * (2 * n),
        input_output_aliases={i: i for i in range(2 * n)},
        compiler_params=pltpu.CompilerParams(has_side_effects=pltpu.SideEffectType.DATAFLOW_SIDE_EFFECTING),
    )(*thru, state[0], state[1], after)
    return out[n:]


def _scatter_copies(grads, lands, send_sems, recv_sems):
    x, y, c = _coords()
    me = 4 * x + 2 * y + c
    copies = []
    for a in range(len(grads)):
        r = grads[a].shape[0] // 8
        for k in range(1, 8):
            px, py, pc = (1 - x if k & 4 else x, 1 - y if k & 2 else y, 1 - c if k & 1 else c)
            rows = pl.ds(pl.multiple_of((4 * px + 2 * py + pc) * r, r), r)
            copies.append(pltpu.make_async_remote_copy(
                src_ref=grads[a].at[rows], dst_ref=lands[a].at[me],
                send_sem=send_sems.at[7 * a + k - 1], recv_sem=recv_sems.at[7 * a + k - 1],
                device_id=(px, py, pc), device_id_type=MESH))
    return copies


def scatter_start(grads, name):
    n = len(grads)

    def body(*refs):
        srcs, lands = refs[:n], refs[n:2 * n]
        send_sems, recv_sems = refs[2 * n:2 * n + 2]
        token = refs[-1]
        for cp in _scatter_copies(srcs, lands, send_sems, recv_sems):
            cp.start()
        token[...] = jnp.zeros_like(token)

    lands = [lax.empty((8, g.shape[0] // 8, g.shape[1]), g.dtype) for g in grads]
    ops = [pltpu.with_memory_space_constraint(a, pltpu.HBM) for a in list(grads) + lands]
    out = pl.pallas_call(
        body, name=name,
        out_shape=[pltpu.SemaphoreType.DMA((7 * n,)), pltpu.SemaphoreType.DMA((7 * n,))]
        + [pltpu.HBM(o.shape, o.dtype) for o in ops] + [jax.ShapeDtypeStruct((8, 128), F32)],
        in_specs=[_HBM] * (2 * n),
        out_specs=[_SEM, _SEM] + [_HBM] * (2 * n) + [pl.BlockSpec(memory_space=pltpu.VMEM)],
        input_output_aliases={i: 2 + i for i in range(2 * n)},
        compiler_params=pltpu.CompilerParams(has_side_effects=pltpu.SideEffectType.DATAFLOW_SIDE_EFFECTING),
    )(*ops)
    return out[:-1], out[-1]


def scatter_wait(state, after, name):
    n = (len(state) - 2) // 2

    def body(*refs):
        srcs, lands = refs[:n], refs[n:2 * n]
        send_sems, recv_sems = refs[2 * n:2 * n + 2]
        for cp in _scatter_copies(srcs, lands, send_sems, recv_sems):
            cp.wait_send()
            cp.wait_recv()

    thru = list(state[2:])
    after = list(after) if isinstance(after, (list, tuple)) else [after]
    out = pl.pallas_call(
        body, name=name,
        out_shape=[pltpu.HBM(o.shape, o.dtype) for o in thru],
        in_specs=[_HBM] * (2 * n) + [_SEM, _SEM] + [pl.BlockSpec(memory_space=pl.ANY)] * len(after),
        out_specs=[_HBM] * (2 * n),
        input_output_aliases={i: i for i in range(2 * n)},
        compiler_params=pltpu.CompilerParams(has_side_effects=pltpu.SideEffectType.DATAFLOW_SIDE_EFFECTING),
    )(*thru, state[0], state[1], *after)
    return out[:n], out[n:]


def rs_win_add_first(g, r, sel, next_dim, col, name):
    rows, cols = r.shape[2:]

    def body(sel_ref, gk_ref, rk_ref, gs_ref, rs_ref, keep_ref, send_ref):
        keep_ref[...] = gk_ref[...] + rk_ref[...]
        send_ref[...] = (gs_ref[...] + rs_ref[...]).astype(BF16)

    def g_map(flip):
        def f(j, s):
            nxt = 1 - s[next_dim] if flip else s[next_dim]
            return (nxt, j, s[2], 0, col) if next_dim == 0 else (j, nxt, s[2], 0, col)
        return f

    def r_map(flip):
        def f(j, s):
            nxt = 1 - s[next_dim] if flip else s[next_dim]
            return (nxt, j, 0, 0) if next_dim == 0 else (j, nxt, 0, 0)
        return f

    gblk = (None, None, None, rows, cols)
    rblk = (None, None, rows, cols)
    oblk = (None, rows, cols)
    return pl.pallas_call(
        body, name=name,
        grid_spec=pltpu.PrefetchScalarGridSpec(
            num_scalar_prefetch=1, grid=(2,),
            in_specs=[pl.BlockSpec(gblk, g_map(False)), pl.BlockSpec(rblk, r_map(False)),
                      pl.BlockSpec(gblk, g_map(True)), pl.BlockSpec(rblk, r_map(True))],
            out_specs=[pl.BlockSpec(oblk, lambda j, s: (j, 0, 0)),
                       pl.BlockSpec(oblk, lambda j, s: (j, 0, 0))]),
        out_shape=[jax.ShapeDtypeStruct((2, rows, cols), F32),
                   jax.ShapeDtypeStruct((2, rows, cols), BF16)],
        compiler_params=_params(),
    )(sel, g, r, g, r)


def rs_add_second(k, r, sel, name):
    _, rows, cols = k.shape
    tr = rows // 2 if rows % 32 == 0 else rows
    nt = rows // tr

    def body(sel_ref, kk_ref, rk_ref, ks_ref, rs_ref, keep_ref, send_ref):
        keep_ref[...] = kk_ref[...] + rk_ref[...].astype(F32)
        send_ref[...] = (ks_ref[...] + rs_ref[...].astype(F32)).astype(BF16)

    blk = (None, tr, cols)
    oblk = (tr, cols)
    return pl.pallas_call(
        body, name=name,
        grid_spec=pltpu.PrefetchScalarGridSpec(
            num_scalar_prefetch=1, grid=(nt,),
            in_specs=[
                pl.BlockSpec(blk, lambda i, s: (s[0], i, 0)),
                pl.BlockSpec(blk, lambda i, s: (s[0], i, 0)),
                pl.BlockSpec(blk, lambda i, s: (1 - s[0], i, 0)),
                pl.BlockSpec(blk, lambda i, s: (1 - s[0], i, 0)),
            ],
            out_specs=[pl.BlockSpec(oblk, lambda i, s: (i, 0)),
                       pl.BlockSpec(oblk, lambda i, s: (i, 0))]),
        out_shape=[jax.ShapeDtypeStruct((rows, cols), F32),
                   jax.ShapeDtypeStruct((rows, cols), BF16)],
        compiler_params=_params(),
    )(sel, k, r, k, r)


SEG_ROWS = (4800, 5824, 6848, 4096, 0, 1024, 2048, 3072)
LAT_ROWS = QL + KVL + ROPE
N_IN = 7872


def _seg_row(j):
    return pl.multiple_of(jnp.where(j < 3, 4800 + 1024 * j, jnp.where(j == 3, 4096, (j - 4) * 1024)), 8)


def proj_matmul(h, wt_bits, token):
    t = h.shape[0]
    tm = min(2048, t)

    def body(h_ref, w_hbm, tok_ref, o_ref, wt_ref, buf, sems):
        j = pl.program_id(0)
        slot = j % 2

        def fetch(seg, into):
            return pltpu.make_async_copy(w_hbm.at[pl.ds(_seg_row(seg), D)], buf.at[into], sems.at[into])

        @pl.when(pl.program_id(1) == 0)
        def _():
            @pl.when(j == 0)
            def _():
                fetch(j, slot).start()

            fetch(j, slot).wait()

            @pl.when(j + 1 < NSEG)
            def _():
                fetch(j + 1, 1 - slot).start()

            bits = pltpu.bitcast(buf[slot], jnp.uint32)
            row = lax.broadcasted_iota(jnp.int32, (D, D // 2), 0)
            live = jnp.logical_or(j != SEG_LAT, row < LAT_ROWS)
            lo = pltpu.bitcast(bits << 16, F32)
            hi = pltpu.bitcast(bits & jnp.uint32(0xFFFF0000), F32)
            wt_ref[:, :D // 2] = jnp.where(live, lo, 0.0).astype(BF16)
            wt_ref[:, D // 2:] = jnp.where(live, hi, 0.0).astype(BF16)

        o_ref[...] = _dot_nt(h_ref[...], wt_ref[...]).astype(BF16)

    return pl.pallas_call(
        body, name="proj_matmul", grid=(NSEG, t // tm),
        in_specs=[pl.BlockSpec((tm, D), lambda j, i: (i, 0)),
                  pl.BlockSpec(memory_space=pl.ANY),
                  pl.BlockSpec((8, 128), lambda j, i: (0, 0))],
        out_specs=[pl.BlockSpec((None, tm, D), lambda j, i: (j, i, 0)),
                   pl.BlockSpec((D, D), lambda j, i: (j, 0))],
        out_shape=[jax.ShapeDtypeStruct((NSEG, t, D), BF16), jax.ShapeDtypeStruct((NP, D), BF16)],
        scratch_shapes=[pltpu.VMEM((2, D, D // 2), F32), pltpu.SemaphoreType.DMA((2,))],
        compiler_params=_params(("arbitrary", "arbitrary")),
    )(h, wt_bits, token)


def dh_matmul(dproj, wt, token, seq, b):
    tm = min(1024, seq)
    nblk = seq // tm

    per = 2

    def body(b_ref, d_ref, w_ref, tok_ref, o_ref, acc_ref):
        k = pl.program_id(1)
        last = NSEG // per - 1

        def part():
            p = _dot(d_ref[0], w_ref[0:D, :])
            for j in range(1, per):
                p = p + _dot(d_ref[j], w_ref[j * D:(j + 1) * D, :])
            return p

        @pl.when(k == 0)
        def _():
            acc_ref[...] = part()

        @pl.when(jnp.logical_and(k > 0, k < last))
        def _():
            acc_ref[...] += part()

        @pl.when(k == last)
        def _():
            o_ref[...] = acc_ref[...] + part()

    return pl.pallas_call(
        body, name="dh_matmul",
        grid_spec=pltpu.PrefetchScalarGridSpec(
            num_scalar_prefetch=1, grid=(nblk, NSEG // per),
            in_specs=[pl.BlockSpec((per, tm, D), lambda i, k, s: (k, s[0] * nblk + i, 0)),
                      pl.BlockSpec((per * D, D), lambda i, k, s: (k, 0)),
                      pl.BlockSpec((8, 128), lambda i, k, s: (0, 0))],
            out_specs=pl.BlockSpec((tm, D), lambda i, k, s: (i, 0)),
            scratch_shapes=[pltpu.VMEM((tm, D), F32)]),
        out_shape=jax.ShapeDtypeStruct((seq, D), F32),
        compiler_params=_params(("parallel", "arbitrary")),
    )(jnp.full((1,), b, jnp.int32), dproj, wt, token)


def win_grad_matmul(h, dproj, token):
    t = h.shape[0]

    def body(h_ref, d_ref, tok_ref, o_hbm, acc_ref, sems):
        j = pl.program_id(0)

        def out_copy(jj, action):
            slot = lax.rem(jj, 2)

            @pl.when(jj != SEG_LAT)
            def _():
                action(pltpu.make_async_copy(acc_ref.at[slot], o_hbm.at[pl.ds(_seg_row(jj), D)],
                                             sems.at[slot]))

            @pl.when(jj == SEG_LAT)
            def _():
                action(pltpu.make_async_copy(acc_ref.at[slot, pl.ds(0, LAT_ROWS)],
                                             o_hbm.at[pl.ds(SEG_ROWS[SEG_LAT], LAT_ROWS)], sems.at[slot]))

        acc_ref[lax.rem(j, 2)] = _dot_tn(d_ref[...], h_ref[...])
        out_copy(j, lambda cp: cp.start())

        @pl.when(j > 0)
        def _():
            out_copy(j - 1, lambda cp: cp.wait())

        @pl.when(j == NSEG - 1)
        def _():
            out_copy(j, lambda cp: cp.wait())

    return pl.pallas_call(
        body, name="win_grad_matmul", grid=(NSEG,),
        in_specs=[pl.BlockSpec((t, D), lambda j: (0, 0)),
                  pl.BlockSpec((None, t, D), lambda j: (j, 0, 0)),
                  pl.BlockSpec((8, 128), lambda j: (0, 0))],
        out_specs=pl.BlockSpec(memory_space=pl.ANY),
        out_shape=jax.ShapeDtypeStruct((N_IN, D), F32),
        scratch_shapes=[pltpu.VMEM((2, D, D), F32), pltpu.SemaphoreType.DMA((2,))],
        compiler_params=_params(("arbitrary",)),
    )(h, dproj, token)


def grad_matmul(a, b, name):
    t, m = a.shape
    n = b.shape[1]
    tk = min(1024, t)
    nk = t // tk

    def body(a_ref, b_ref, o_ref, acc_ref):
        k = pl.program_id(0)
        part = lambda: _dot_tn(a_ref[...], b_ref[...])
        if nk == 1:
            o_ref[...] = part().astype(BF16)
            return

        @pl.when(k == 0)
        def _():
            acc_ref[...] = part()

        @pl.when(jnp.logical_and(k > 0, k < nk - 1))
        def _():
            acc_ref[...] += part()

        @pl.when(k == nk - 1)
        def _():
            o_ref[...] = (acc_ref[...] + part()).astype(BF16)

    return pl.pallas_call(
        body, name=name, grid=(nk,),
        in_specs=[pl.BlockSpec((tk, m), lambda k: (k, 0)),
                  pl.BlockSpec((tk, n), lambda k: (k, 0))],
        out_specs=pl.BlockSpec((m, n), lambda k: (0, 0)),
        out_shape=jax.ShapeDtypeStruct((m, n), BF16),
        scratch_shapes=[pltpu.VMEM((m, n), F32)],
        compiler_params=_params(("arbitrary",)),
    )(a, b)


def ada_gather(c8, taps8, w_ada, b_cols):
    cols = w_ada.shape[1]

    def body(c_ref, t_ref, w_ref, b_ref, call_ref, tall_ref, mod_ref, part_ref, send_sems, recv_sems):
        x, y, c = _coords()
        me = 4 * x + 2 * y + c
        peers = [(1 - x if k & 4 else x, 1 - y if k & 2 else y, 1 - c if k & 1 else c) for k in range(1, 8)]

        def rcopy(n, src, dst, peer):
            return pltpu.make_async_remote_copy(src_ref=src, dst_ref=dst, send_sem=send_sems.at[n],
                                                recv_sem=recv_sems.at[n], device_id=peer, device_id_type=MESH)

        call_ref[me] = c_ref[...]
        tall_ref[me] = t_ref[...]
        first = []
        for k, peer in enumerate(peers):
            first += [rcopy(k, c_ref, call_ref.at[me], peer), rcopy(7 + k, t_ref, tall_ref.at[me], peer)]
        for cp in first:
            cp.start()
        for cp in first:
            cp.wait()
        rows = call_ref[...].reshape(64, D).astype(BF16)
        part_ref[...] = _dot(rows, w_ref[...].astype(BF16)) + b_ref[...]
        mod_ref[me] = part_ref[pl.ds(pl.multiple_of(8 * me, 8), 8), :]
        second = []
        for k, (px, py, pc) in enumerate(peers):
            theirs = part_ref.at[pl.ds(pl.multiple_of(8 * (4 * px + 2 * py + pc), 8), 8)]
            second.append(rcopy(14 + k, theirs, mod_ref.at[me], (px, py, pc)))
        for cp in second:
            cp.start()
        for cp in second:
            cp.wait()

    vm = pl.BlockSpec(memory_space=pltpu.VMEM)
    return pl.pallas_call(
        body, name="ada_gather",
        out_shape=[jax.ShapeDtypeStruct((8, 8, D), F32), jax.ShapeDtypeStruct((8, 8, 128), F32),
                   jax.ShapeDtypeStruct((8, 8, cols), F32)],
        in_specs=[vm] * 4, out_specs=[vm] * 3,
        scratch_shapes=[pltpu.VMEM((64, cols), F32), pltpu.SemaphoreType.DMA((21,)),
                        pltpu.SemaphoreType.DMA((21,))],
        compiler_params=_params(),
    )(c8, taps8, w_ada, b_cols)


def ada_bwd(c_all, dmod_cols):
    def body(c_ref, d_ref, o_ref):
        o_ref[...] = _dot_tn(c_ref[...].astype(BF16), d_ref[...].astype(BF16))

    return pl.pallas_call(
        body, name="ada_bwd",
        out_shape=jax.ShapeDtypeStruct((c_all.shape[1], dmod_cols.shape[1]), F32),
        compiler_params=_params(),
    )(c_all, dmod_cols)


def slot_sum(g):
    def body(g_ref, o_ref):
        acc = g_ref[0]
        for s in range(1, 8):
            acc = acc + g_ref[s]
        o_ref[...] = acc

    return pl.pallas_call(
        body, name="slot_sum",
        out_shape=jax.ShapeDtypeStruct(g.shape[1:], F32),
    )(g)


def prenorm_fwd(x2, scale, shift, g_pre, seq):
    t = x2.shape[0]
    tm = min(512, seq)
    tpb = seq // tm

    def body(x_ref, sc_ref, sh_ref, g_ref, h_ref):
        xv = x_ref[...]
        r = lax.rsqrt(jnp.mean(xv * xv, axis=-1, keepdims=True) + EPS)
        hv = (xv * r * g_ref[...]) * (1.0 + sc_ref[...]) + sh_ref[...]
        h_ref[...] = hv.astype(BF16)

    per_batch = pl.BlockSpec((None, 1, D), lambda i: (i // tpb, 0, 0))
    return pl.pallas_call(
        body, name="prenorm_fwd", grid=(t // tm,),
        in_specs=[pl.BlockSpec((tm, D), lambda i: (i, 0)), per_batch, per_batch,
                  pl.BlockSpec((1, D), lambda i: (0, 0))],
        out_specs=pl.BlockSpec((tm, D), lambda i: (i, 0)),
        out_shape=jax.ShapeDtypeStruct((t, D), BF16),
        compiler_params=_params(("parallel",)),
    )(x2, scale, shift, g_pre)


def prenorm_during(x2, scale, shift, g_pre, seq):
    tm = min(512, seq)
    assert x2.shape[0] == 2 * seq, "one sequence per exchange stage before the last"

    def during(b, x_hbm, sc_ref, sh_ref, g_ref, h_hbm):
        def tile(x_ref, h_ref):
            xv = x_ref[...]
            r = lax.rsqrt(jnp.mean(xv * xv, axis=-1, keepdims=True) + EPS)
            h_ref[...] = ((xv * r * g_ref[...]) * (1.0 + sc_ref[b]) + sh_ref[b]).astype(BF16)

        rows = pl.BlockSpec((tm, D), lambda i: (i, 0))
        pltpu.emit_pipeline(tile, grid=(seq // tm,), in_specs=[rows], out_specs=[rows])(
            x_hbm.at[pl.ds(b * seq, seq)], h_hbm.at[pl.ds(b * seq, seq)])

    vmem = pl.BlockSpec(memory_space=pltpu.VMEM)
    hbm = pl.BlockSpec(memory_space=pl.ANY)
    return dict(during=during,
                extra_in=[(x2, hbm), (scale, vmem), (shift, vmem), (g_pre, vmem)],
                extra_out=[(jax.ShapeDtypeStruct(x2.shape, BF16), hbm)])


def prenorm_bwd(dh, x2, dout, scale, g_pre, seq, token, b, gx_prev):
    t = x2.shape[0]
    tm = min(512, seq)
    tpb = seq // tm
    if gx_prev is None:
        gx_prev = lax.empty((t, D), F32)

    def body(b_ref, dh_ref, x_ref, do_ref, sc_ref, g_ref, tok_ref, gxp_ref, gx_ref, dsh_ref, dsc_ref, dg_ref):
        i = pl.program_id(0)
        xv = x_ref[...]
        dhv = dh_ref[...]
        g = g_ref[...]
        r = lax.rsqrt(jnp.mean(xv * xv, axis=-1, keepdims=True) + EPS)
        nrm = xv * r
        dxn = dhv * (1.0 + sc_ref[...])
        dn = dxn * g
        dx = r * (dn - nrm * jnp.mean(dn * nrm, axis=-1, keepdims=True))
        gx_ref[...] = dx + do_ref[...]

        @pl.when(i == 0)
        def _():
            dsh_ref[...] = jnp.zeros_like(dsh_ref)
            dsc_ref[...] = jnp.zeros_like(dsc_ref)
            dg_ref[...] = jnp.zeros_like(dg_ref)

        dsh_ref[...] += jnp.sum(dhv, axis=0, keepdims=True)
        dsc_ref[...] += jnp.sum(dhv * (nrm * g), axis=0, keepdims=True)
        dg_ref[...] += jnp.sum(dxn * nrm, axis=0, keepdims=True)

    row = pl.BlockSpec((tm, D), lambda i, s: (i, 0))
    grow = pl.BlockSpec((tm, D), lambda i, s: (s[0] * tpb + i, 0))
    per_batch = pl.BlockSpec((None, 1, D), lambda i, s: (s[0], 0, 0))
    vec = pl.BlockSpec((1, D), lambda i, s: (0, 0))
    return pl.pallas_call(
        body, name="prenorm_bwd",
        grid_spec=pltpu.PrefetchScalarGridSpec(
            num_scalar_prefetch=1, grid=(tpb,),
            in_specs=[row, grow, grow, per_batch, vec, pl.BlockSpec((8, 128), lambda i, s: (0, 0)),
                      pl.BlockSpec(memory_space=pl.ANY)],
            out_specs=[grow, vec, vec, vec]),
        out_shape=[jax.ShapeDtypeStruct((t, D), F32), jax.ShapeDtypeStruct((1, D), F32),
                   jax.ShapeDtypeStruct((1, D), F32), jax.ShapeDtypeStruct((1, D), F32)],
        input_output_aliases={7: 0},
        compiler_params=_params(("arbitrary",)),
    )(jnp.full((1,), b, jnp.int32), dh, x2, dout, scale, g_pre, token, gx_prev)


CONV_TC = 128


def _shift_down(u, k, rows):
    idx = lax.broadcasted_iota(jnp.int32, u.shape, 0)
    return jnp.where(idx >= k, pltpu.roll(u, k, 0), 0.0)


def _shift_up(u, k, rows):
    idx = lax.broadcasted_iota(jnp.int32, u.shape, 0)
    return jnp.where(idx < rows - k, pltpu.roll(u, rows - k, 0), 0.0)


def conv_fwd(proj, conv_w, seq):
    t = proj.shape[1]
    nb = t // seq

    def body(p_ref, w_ref, y_ref):
        av = p_ref[0].astype(F32)
        ab = p_ref[1].astype(F32)
        ac = p_ref[2].astype(F32)
        az = p_ref[3].astype(F32)
        w = w_ref[...]
        u = ac * av
        y1 = _shift_down(u, 2, seq) * w[0:1] + _shift_down(u, 1, seq) * w[1:2] + u * w[2:3]
        y_ref[...] = (ab * y1 * (az * _sig(az))).astype(BF16)

    return pl.pallas_call(
        body, name="conv_fwd", grid=(nb, D // CONV_TC),
        in_specs=[pl.BlockSpec((4, seq, CONV_TC), lambda b, ci: (1, b, ci)),
                  pl.BlockSpec((8, CONV_TC), lambda b, ci: (0, ci))],
        out_specs=pl.BlockSpec((seq, CONV_TC), lambda b, ci: (b, ci)),
        out_shape=jax.ShapeDtypeStruct((t, D), BF16),
        compiler_params=_params(("parallel", "parallel")),
    )(proj, conv_w)


def conv_bwd(dproj, proj, dy, conv_w, seq):
    t = proj.shape[1]
    nb = t // seq

    def body(dp_in_ref, p_ref, dy_ref, w_ref, dp_ref, dw_ref):
        b = pl.program_id(1)
        av = p_ref[0].astype(F32)
        ab = p_ref[1].astype(F32)
        ac = p_ref[2].astype(F32)
        az = p_ref[3].astype(F32)
        dyv = dy_ref[...].astype(F32)
        w = w_ref[...]
        u = ac * av
        u1 = _shift_down(u, 1, seq)
        u2 = _shift_down(u, 2, seq)
        y1 = u2 * w[0:1] + u1 * w[1:2] + u * w[2:3]
        sz = _sig(az)
        silu = az * sz
        dy1 = dyv * ab * silu
        du = dy1 * w[2:3] + _shift_up(dy1, 1, seq) * w[1:2] + _shift_up(dy1, 2, seq) * w[0:1]
        dp_ref[0] = (du * ac).astype(BF16)
        dp_ref[1] = (dyv * y1 * silu).astype(BF16)
        dp_ref[2] = (du * av).astype(BF16)
        dp_ref[3] = (dyv * ab * y1 * (sz * (1.0 + az * (1.0 - sz)))).astype(BF16)

        @pl.when(b == 0)
        def _():
            dw_ref[...] = jnp.zeros_like(dw_ref)

        dw_ref[0:1, :] += jnp.sum(dy1 * u2, axis=0, keepdims=True)
        dw_ref[1:2, :] += jnp.sum(dy1 * u1, axis=0, keepdims=True)
        dw_ref[2:3, :] += jnp.sum(dy1 * u, axis=0, keepdims=True)

    return pl.pallas_call(
        body, name="conv_bwd", grid=(D // CONV_TC, nb),
        in_specs=[pl.BlockSpec(memory_space=pl.ANY),
                  pl.BlockSpec((4, seq, CONV_TC), lambda ci, b: (1, b, ci)),
                  pl.BlockSpec((seq, CONV_TC), lambda ci, b: (b, ci)),
                  pl.BlockSpec((8, CONV_TC), lambda ci, b: (0, ci))],
        out_specs=[pl.BlockSpec((4, seq, CONV_TC), lambda ci, b: (1, b, ci)),
                   pl.BlockSpec((8, CONV_TC), lambda ci, b: (0, ci))],
        out_shape=[jax.ShapeDtypeStruct(dproj.shape, BF16),
                   jax.ShapeDtypeStruct((8, D), F32)],
        input_output_aliases={0: 0},
        compiler_params=_params(("parallel", "arbitrary")),
    )(dproj, proj, dy, conv_w)


def _rope_tables(pos_ref, invf_ref, ma_ref, mb_ref, sign):
    ang = pos_ref[...].astype(F32) * invf_ref[...]
    cs = jnp.cos(ang)
    sn = jnp.sin(ang) * sign
    return cs, sn * ma_ref[...], sn * mb_ref[...]


def _rotate(v, cs, sa, sb):
    return v * cs + pltpu.roll(v, 128 - HALF, 1) * sa + pltpu.roll(v, HALF, 1) * sb


MLA_TM = 512


def mla_prep_fwd(proj, pos, g_q, g_kv, wuq, wukv, tabs):
    t = proj.shape[1]
    tm = min(MLA_TM, t)

    def body(lat_ref, pos_ref, gq_ref, gkv_ref, wuq_ref, wukv_ref, invf_ref, ma_ref, mb_ref,
             q_ref, k_ref, kv_ref, qn_ref, kvn_ref):
        lat = lat_ref[...].astype(F32)
        ql = lat[:, :QL]
        kl = lat[:, QL:QL + KVL]
        kr = lat[:, QL + KVL:QL + KVL + 128]
        qn = (ql * lax.rsqrt(jnp.mean(ql * ql, axis=-1, keepdims=True) + EPS) * gq_ref[...]).astype(BF16)
        kvn = (kl * lax.rsqrt(jnp.mean(kl * kl, axis=-1, keepdims=True) + EPS) * gkv_ref[...]).astype(BF16)
        qn_ref[...] = qn
        kvn_ref[...] = kvn
        cs, sa, sb = _rope_tables(pos_ref, invf_ref, ma_ref, mb_ref, 1.0)
        q = _dot_nt(qn, wuq_ref[...]) * (SM_SCALE * LOG2E)
        kv = _dot_nt(kvn, wukv_ref[...]).astype(BF16)
        kv_ref[...] = kv
        kpe = _rotate(kr, cs, sa, sb).astype(BF16)
        for hh in range(H):
            lo, mid, hi = hh * DQK, hh * DQK + 128, (hh + 1) * DQK
            q_ref[:, lo:mid] = q[:, lo:mid].astype(BF16)
            q_ref[:, mid:hi] = _rotate(q[:, mid:hi], cs, sa, sb).astype(BF16)
            k_ref[:, lo:mid] = kv[:, lo:mid]
            k_ref[:, mid:hi] = kpe

    row = lambda w: pl.BlockSpec((tm, w), lambda i: (i, 0))
    const = lambda a: pl.BlockSpec(a.shape, lambda i: (0,) * a.ndim)
    return pl.pallas_call(
        body, name="mla_prep_fwd", grid=(t // tm,),
        in_specs=[pl.BlockSpec((None, tm, D), lambda i: (SEG_LAT, i, 0)), row(1),
                  const(g_q), const(g_kv), const(wuq), const(wukv)] + [const(a) for a in tabs],
        out_specs=[row(H * DQK), row(H * DQK), row(H * DQK), row(QL), row(KVL)],
        out_shape=[jax.ShapeDtypeStruct((t, H * DQK), BF16)] * 3
        + [jax.ShapeDtypeStruct((t, QL), BF16), jax.ShapeDtypeStruct((t, KVL), BF16)],
        compiler_params=_params(("parallel",)),
    )(proj, pos, g_q, g_kv, wuq, wukv, *tabs)


def mla_prep_bwd(dproj, proj, dq_rot, dk, dv, pos, g_q, g_kv, wuq, wukv, tabs):
    t = proj.shape[1]
    tm = min(MLA_TM, t)

    def body(dp_in_ref, lat_ref, dqr_ref, dk_ref, dv_ref, pos_ref, gq_ref, gkv_ref, wuq_ref, wukv_ref,
             invf_ref, ma_ref, mb_ref, dp_ref, dq_ref, dkv_ref, dgq_ref, dgkv_ref):
        i = pl.program_id(0)
        lat = lat_ref[...].astype(F32)
        ql = lat[:, :QL]
        kl = lat[:, QL:QL + KVL]
        rq = lax.rsqrt(jnp.mean(ql * ql, axis=-1, keepdims=True) + EPS)
        rk = lax.rsqrt(jnp.mean(kl * kl, axis=-1, keepdims=True) + EPS)
        nq = ql * rq
        nk = kl * rk
        cs, sa, sb = _rope_tables(pos_ref, invf_ref, ma_ref, mb_ref, -1.0)
        dkpe = jnp.zeros((tm, 128), F32)
        for hh in range(H):
            lo, mid, hi = hh * DQK, hh * DQK + 128, (hh + 1) * DQK
            dq_ref[:, lo:mid] = (dqr_ref[:, lo:mid] * SM_SCALE).astype(BF16)
            dq_ref[:, mid:hi] = _rotate(dqr_ref[:, mid:hi] * SM_SCALE, cs, sa, sb).astype(BF16)
            dkv_ref[:, lo:mid] = dk_ref[:, lo:mid]
            dkv_ref[:, mid:hi] = dv_ref[:, hh * DV:(hh + 1) * DV]
            dkpe = dkpe + dk_ref[:, mid:hi].astype(F32)
        lane = lax.broadcasted_iota(jnp.int32, (tm, 128), 1)
        dkr = jnp.where(lane < ROPE, _rotate(dkpe, cs, sa, sb), 0.0)
        dqn = _dot(dq_ref[...], wuq_ref[...])
        dkvn = _dot(dkv_ref[...], wukv_ref[...])
        gq = gq_ref[...]
        gkv = gkv_ref[...]
        dnq = dqn * gq
        dnk = dkvn * gkv
        dql = rq * (dnq - nq * jnp.mean(dnq * nq, axis=-1, keepdims=True))
        dkl = rk * (dnk - nk * jnp.mean(dnk * nk, axis=-1, keepdims=True))
        dp_ref[:, :QL] = dql.astype(BF16)
        dp_ref[:, QL:QL + KVL] = dkl.astype(BF16)
        dp_ref[:, QL + KVL:QL + KVL + 128] = dkr.astype(BF16)
        dp_ref[:, QL + KVL + 128:] = jnp.zeros((tm, D - QL - KVL - 128), BF16)

        @pl.when(i == 0)
        def _():
            dgq_ref[...] = jnp.zeros_like(dgq_ref)
            dgkv_ref[...] = jnp.zeros_like(dgkv_ref)

        dgq_ref[...] += jnp.sum(dqn * nq, axis=0, keepdims=True)
        dgkv_ref[...] += jnp.sum(dkvn * nk, axis=0, keepdims=True)

    row = lambda w: pl.BlockSpec((tm, w), lambda i: (i, 0))
    const = lambda a: pl.BlockSpec(a.shape, lambda i: (0,) * a.ndim)
    seg = pl.BlockSpec((None, tm, D), lambda i: (SEG_LAT, i, 0))
    return pl.pallas_call(
        body, name="mla_prep_bwd", grid=(t // tm,),
        in_specs=[pl.BlockSpec(memory_space=pl.ANY), seg, row(H * DQK), row(H * DQK), row(H * DV), row(1),
                  const(g_q), const(g_kv), const(wuq), const(wukv)] + [const(a) for a in tabs],
        out_specs=[seg, row(H * DQK), row(H * DQK),
                   pl.BlockSpec((1, QL), lambda i: (0, 0)), pl.BlockSpec((1, KVL), lambda i: (0, 0))],
        out_shape=[jax.ShapeDtypeStruct(dproj.shape, BF16),
                   jax.ShapeDtypeStruct((t, H * DQK), BF16), jax.ShapeDtypeStruct((t, H * DQK), BF16),
                   jax.ShapeDtypeStruct((1, QL), F32), jax.ShapeDtypeStruct((1, KVL), F32)],
        input_output_aliases={0: 0},
        compiler_params=_params(("arbitrary",)),
    )(dproj, proj, dq_rot, dk, dv, pos, g_q, g_kv, wuq, wukv, *tabs)


def _causal_mask(s, shift):
    row = lax.broadcasted_iota(jnp.int32, s.shape, 0)
    col = lax.broadcasted_iota(jnp.int32, s.shape, 1)
    return jnp.where(col <= row + shift, s, -1e30)


def flash_fwd(q, k, kv, nb, seq):
    t = q.shape[0]
    tq = min(FLASH_TQ, seq // 2)
    nq = seq // tq
    assert nq % 2 == 0, "blocks are processed in pairs"

    def update(state, s, vblk):
        m, l, acc = state
        m_new = jnp.maximum(m, jnp.max(s, axis=1, keepdims=True))
        p = jnp.exp2(s - m_new)
        alpha = jnp.exp2(m - m_new)
        return (m_new, alpha * l + jnp.sum(p, axis=1, keepdims=True),
                alpha * acc + _dot(p.astype(BF16), vblk))

    def finish(state, rows, o_ref, lse_ref):
        m, l, acc = state
        o_ref[rows, :] = (acc / l).astype(BF16)
        lse_ref[rows, :] = jnp.broadcast_to(m + jnp.log(l) * LOG2E, (m.shape[0], DV))

    def body(q_ref, k_ref, v_ref, o_ref, lse_ref):
        for qp in range(0, nq, 2):
            rows = 2 * tq
            q0 = qp * tq
            qv = q_ref[q0:q0 + rows, :]
            state = (jnp.full((rows, 1), -1e30, F32), jnp.zeros((rows, 1), F32), jnp.zeros((rows, DV), F32))
            for j in range(qp + 1):
                ks = slice(j * tq, (j + 1) * tq)
                s = _dot_nt(qv, k_ref[ks, :])
                if j == qp:
                    s = _causal_mask(s, 0)
                state = update(state, s, v_ref[ks, :])
            finish(tuple(a[:tq] for a in state), slice(q0, q0 + tq), o_ref, lse_ref)
            ks = slice(q0 + tq, q0 + 2 * tq)
            low = tuple(a[tq:] for a in state)
            low = update(low, _causal_mask(_dot_nt(qv[tq:], k_ref[ks, :]), 0), v_ref[ks, :])
            finish(low, slice(q0 + tq, q0 + 2 * tq), o_ref, lse_ref)

    out_blk = pl.BlockSpec((seq, DV), lambda b, h: (b, h))
    return pl.pallas_call(
        body, name="flash_fwd", grid=(nb, H),
        in_specs=[pl.BlockSpec((seq, DQK), lambda b, h: (b, h)),
                  pl.BlockSpec((seq, DQK), lambda b, h: (b, h)),
                  pl.BlockSpec((seq, DV), lambda b, h: (b, 2 * h + 1))],
        out_specs=[out_blk, out_blk],
        out_shape=[jax.ShapeDtypeStruct((t, H * DV), BF16), jax.ShapeDtypeStruct((t, H * DV), F32)],
        compiler_params=_params(("parallel", "parallel")),
    )(q, k, kv)


def flash_bwd(q, k, kv, o, do, lse, nb, seq, token):
    t = q.shape[0]
    tq = min(FLASH_TQ, seq)
    nq = seq // tq

    def body(q_ref, k_ref, v_ref, o_ref, do_ref, lse_ref, tok_ref, dq_ref, dk_ref, dv_ref):
        delta, lse = [], []
        for qi in range(nq):
            qs = slice(qi * tq, (qi + 1) * tq)
            dl = jnp.sum(do_ref[qs, :].astype(F32) * o_ref[qs, :].astype(F32), axis=1, keepdims=True)
            delta.append(jnp.broadcast_to(dl, (tq, DV)).T[:1, :])
            lse.append(lse_ref[qs, :].T[:1, :])
        for ki in range(nq):
            ks = slice(ki * tq, (ki + 1) * tq)
            kb = k_ref[ks, :]
            vb = v_ref[ks, :]
            dk = jnp.zeros((tq, DQK), F32)
            dv = jnp.zeros((tq, DV), F32)
            for qi in range(ki, nq):
                qs = slice(qi * tq, (qi + 1) * tq)
                qv = q_ref[qs, :]
                dov = do_ref[qs, :]
                st = _dot_nt(kb, qv)
                if qi == ki:
                    row = lax.broadcasted_iota(jnp.int32, st.shape, 0)
                    col = lax.broadcasted_iota(jnp.int32, st.shape, 1)
                    st = jnp.where(row <= col, st, -1e30)
                pt = jnp.exp2(st - lse[qi])
                dpt = _dot_nt(vb, dov)
                dzt = (pt * (dpt - delta[qi])).astype(BF16)
                dv = dv + _dot(pt.astype(BF16), dov)
                dk = dk + _dot(dzt, qv)
                dqb = _dot_tn(dzt, kb)
                if ki == 0:
                    dq_ref[qs, :] = dqb
                else:
                    dq_ref[qs, :] += dqb
            dk_ref[ks, :] = (dk * LN2).astype(BF16)
            dv_ref[ks, :] = dv.astype(BF16)

    full = lambda w, col: pl.BlockSpec((seq, w), col)
    same = lambda b, h: (b, h)
    return pl.pallas_call(
        body, name="flash_bwd", grid=(nb, H),
        in_specs=[full(DQK, same), full(DQK, same), full(DV, lambda b, h: (b, 2 * h + 1)),
                  full(DV, same), full(DV, same), full(DV, same),
                  pl.BlockSpec((8, 128), lambda b, h: (0, 0))],
        out_specs=[full(DQK, same), full(DQK, same), full(DV, same)],
        out_shape=[jax.ShapeDtypeStruct((t, H * DQK), F32), jax.ShapeDtypeStruct((t, H * DQK), BF16),
                   jax.ShapeDtypeStruct((t, H * DV), BF16)],
        compiler_params=_params(("parallel", "parallel")),
    )(q, k, kv, o, do, lse, token)


TAIL_TM = 512


def tail_fwd(y, attn, proj, x2, tgt, gate, g_post, wco, wmo, wout, seq):
    t = y.shape[0]
    nb = t // seq
    tm = min(TAIL_TM, seq)
    tpb = seq // tm

    def body(y_ref, at_ref, p_ref, x_ref, t_ref, gate_ref, gp_ref, wco_ref, wmo_ref, wout_ref,
             o_ref, ya_ref, yb_ref, m_ref, do2_ref, dout_ref, dgate_ref, dgp_ref, loss_ref):
        i = pl.program_id(0)
        bz = p_ref[0].astype(F32)
        ga = p_ref[1].astype(F32)
        gb = p_ref[2].astype(F32)
        ov = (at_ref[...].astype(F32) * (bz * _sig(bz))).astype(BF16)
        o_ref[...] = ov
        ya = _dot(y_ref[...], wco_ref[...])
        yb = _dot(ov, wmo_ref[...])
        ya_ref[...] = ya.astype(BF16)
        yb_ref[...] = yb.astype(BF16)
        mv = (_sig(ga) * ya + _sig(gb) * yb).astype(BF16)
        m_ref[...] = mv
        o2 = _dot(mv, wout_ref[...])
        r = lax.rsqrt(jnp.mean(o2 * o2, axis=-1, keepdims=True) + EPS)
        nrm = o2 * r
        gp = gp_ref[...]
        gate_v = gate_ref[...]
        rn = nrm * gp
        err = x_ref[...] + gate_v * rn - t_ref[...]
        dout = err * (1.0 / D)
        dout_ref[...] = dout
        dn = dout * gate_v * gp
        do2_ref[...] = (r * (dn - nrm * jnp.mean(dn * nrm, axis=-1, keepdims=True))).astype(BF16)

        @pl.when(i % tpb == 0)
        def _():
            dgate_ref[...] = jnp.zeros_like(dgate_ref)

        @pl.when(i == 0)
        def _():
            dgp_ref[...] = jnp.zeros_like(dgp_ref)
            loss_ref[...] = jnp.zeros_like(loss_ref)

        dgate_ref[...] += jnp.sum(dout * rn, axis=0, keepdims=True)
        dgp_ref[...] += jnp.sum(dout * gate_v * nrm, axis=0, keepdims=True)
        loss_ref[...] += 0.5 * jnp.sum(jnp.mean(err * err, axis=-1, keepdims=True), axis=0, keepdims=True)

    row = pl.BlockSpec((tm, D), lambda i: (i, 0))
    per_batch = pl.BlockSpec((None, 1, D), lambda i: (i // tpb, 0, 0))
    vec = pl.BlockSpec((1, D), lambda i: (0, 0))
    wgt = pl.BlockSpec((D, D), lambda i: (0, 0))
    act = jax.ShapeDtypeStruct((t, D), BF16)
    return pl.pallas_call(
        body, name="tail_fwd", grid=(t // tm,),
        in_specs=[row, row, pl.BlockSpec((3, tm, D), lambda i: (0, i, 0)), row, row, per_batch, vec,
                  wgt, wgt, wgt],
        out_specs=[row, row, row, row, row, row, per_batch, vec, pl.BlockSpec((1, 1), lambda i: (0, 0))],
        out_shape=[act, act, act, act, act, jax.ShapeDtypeStruct((t, D), F32),
                   jax.ShapeDtypeStruct((nb, 1, D), F32), jax.ShapeDtypeStruct((1, D), F32),
                   jax.ShapeDtypeStruct((1, 1), F32)],
        compiler_params=_params(("arbitrary",)),
    )(y, attn, proj, x2, tgt, gate, g_post, wco, wmo, wout)


def tail_bwd(do2, proj, ya, yb, attn, wout, wmo, wco):
    t = do2.shape[0]
    tm = min(TAIL_TM, t)

    def body(do2_ref, p_ref, ya_ref, yb_ref, at_ref, wout_ref, wmo_ref, wco_ref,
             dp_ref, dya_ref, dyb_ref, dat_ref, dy_ref):
        bz = p_ref[0].astype(F32)
        ga = p_ref[1].astype(F32)
        gb = p_ref[2].astype(F32)
        dm = _dot_nt(do2_ref[...], wout_ref[...])
        sa = _sig(ga)
        sb = _sig(gb)
        dya = (dm * sa).astype(BF16)
        dyb = (dm * sb).astype(BF16)
        dya_ref[...] = dya
        dyb_ref[...] = dyb
        dp_ref[1] = (dm * ya_ref[...].astype(F32) * (sa * (1.0 - sa))).astype(BF16)
        dp_ref[2] = (dm * yb_ref[...].astype(F32) * (sb * (1.0 - sb))).astype(BF16)
        dov = _dot_nt(dyb, wmo_ref[...])
        sz = _sig(bz)
        dat_ref[...] = (dov * (bz * sz)).astype(BF16)
        dp_ref[0] = (dov * at_ref[...].astype(F32) * (sz * (1.0 + bz * (1.0 - sz)))).astype(BF16)
        dy_ref[...] = _dot_nt(dya, wco_ref[...]).astype(BF16)

    row = pl.BlockSpec((tm, D), lambda i: (i, 0))
    seg3 = pl.BlockSpec((3, tm, D), lambda i: (0, i, 0))
    wgt = pl.BlockSpec((D, D), lambda i: (0, 0))
    act = jax.ShapeDtypeStruct((t, D), BF16)
    return pl.pallas_call(
        body, name="tail_bwd", grid=(t // tm,),
        in_specs=[row, seg3, row, row, row, wgt, wgt, wgt],
        out_specs=[seg3, row, row, row, row],
        out_shape=[jax.ShapeDtypeStruct((NSEG, t, D), BF16), act, act, act, act],
        compiler_params=_params(("parallel",)),
    )(do2, proj, ya, yb, attn, wout, wmo, wco)


def _adam_update(w, m, v, grad):
    mn = ADAM_B1 * m + (1.0 - ADAM_B1) * grad
    vn = ADAM_B2 * v + (1.0 - ADAM_B2) * (grad * grad)
    m_hat = mn / (1.0 - ADAM_B1 ** ADAM_STEP)
    v_hat = vn / (1.0 - ADAM_B2 ** ADAM_STEP)
    return -ADAM_LR * (m_hat / (jnp.sqrt(v_hat) + ADAM_EPS) + ADAM_WD * w), mn, vn


def adamw(w, m, v, g, name, token):
    rows, cols = w.shape
    tr = rows
    for cand in (256, 128, 64, 32, 16, 8):
        if rows % cand == 0 and rows > cand:
            tr = cand
            break

    def body(w_ref, m_ref, v_ref, g_ref, tok_ref, d_ref, mo_ref, vo_ref):
        d_ref[...], mo_ref[...], vo_ref[...] = _adam_update(w_ref[...], m_ref[...], v_ref[...], g_ref[...])

    blk = pl.BlockSpec((tr, cols), lambda i: (i, 0))
    return pl.pallas_call(
        body, name=name, grid=(rows // tr,),
        in_specs=[blk] * 4 + [pl.BlockSpec((8, 128), lambda i: (0, 0))], out_specs=[blk] * 3,
        out_shape=[jax.ShapeDtypeStruct((rows, cols), F32)] * 3,
        compiler_params=_params(("parallel",)),
    )(w, m, v, g, token)


def adamw_scattered(w, m, v, own, land, me, tr, name, transpose=False):
    slot_rows = land.shape[1]
    cols = land.shape[2]
    rows = slot_rows if transpose else w.shape[0]
    per_slot = slot_rows // tr

    def body(me_ref, w_ref, m_ref, v_ref, own_ref, land_ref, go_ref, d_ref, mo_ref, vo_ref):
        grad = own_ref[...].astype(F32)
        for s in range(8):
            grad = grad + jnp.where(me_ref[0] == s, 0.0, land_ref[s].astype(F32))
        if transpose:
            grad = grad.T
        go_ref[...] = grad
        d_ref[...], mo_ref[...], vo_ref[...] = _adam_update(w_ref[...], m_ref[...], v_ref[...], grad)

    wblk = pl.BlockSpec(w.shape if transpose else (tr, w.shape[1]), lambda i, s: (i, 0))
    return pl.pallas_call(
        body, name=name,
        grid_spec=pltpu.PrefetchScalarGridSpec(
            num_scalar_prefetch=1, grid=(rows // tr,),
            in_specs=[wblk, wblk, wblk,
                      pl.BlockSpec((tr, cols), lambda i, s: (s[0] * per_slot + i, 0)),
                      pl.BlockSpec((8, tr, cols), lambda i, s: (0, i, 0))],
            out_specs=[wblk] * 4),
        out_shape=[jax.ShapeDtypeStruct(w.shape, F32)] * 4,
        compiler_params=_params(),
    )(me, w, m, v, own, land)


def adamw_win(wt, mt, vt, ka, ra, kb, rb):
    rows = wt.shape[0]
    tc = 256
    nh = (D // 2) // tc

    def body(w_ref, m_ref, v_ref, ka_ref, ra_ref, kb_ref, rb_ref, go_ref, d_ref, mo_ref, vo_ref):
        first = pl.program_id(0) < nh
        grad = jnp.where(first, ka_ref[...] + ra_ref[...].astype(F32), kb_ref[...] + rb_ref[...].astype(F32))
        go_ref[...] = grad
        d_ref[...], mo_ref[...], vo_ref[...] = _adam_update(w_ref[...], m_ref[...], v_ref[...], grad)

    blk = pl.BlockSpec((rows, tc), lambda j: (0, j))
    lo = pl.BlockSpec((rows, tc), lambda j: (0, jnp.minimum(j, nh - 1)))
    hi = pl.BlockSpec((rows, tc), lambda j: (0, jnp.maximum(j - nh, 0)))
    return pl.pallas_call(
        body, name="adamw_w_in", grid=(D // tc,),
        in_specs=[blk, blk, blk, lo, lo, hi, hi], out_specs=[blk] * 4,
        out_shape=[jax.ShapeDtypeStruct((rows, D), F32)] * 4,
        compiler_params=_params(("parallel",)),
    )(wt, mt, vt, ka, ra, kb, rb)


_ORD_A = ("x", "y", "c")
_ORD_B = ("y", "x", "c")


def _rows128(a, rows):
    flat = a.reshape(-1)
    return jnp.pad(flat, (0, rows * 128 - flat.shape[0])).reshape(rows, 128)


def kernel(x, c, positions, w_ada, b_ada, g_pre, w_in, conv_w, w_conv_out, g_q, w_uq, g_kv, w_ukv, w_mla_out, w_out, g_post, loss_target, m_w_ada, m_b_ada, m_g_pre, m_w_in, m_conv_w, m_w_conv_out, m_g_q, m_w_uq, m_g_kv, m_w_ukv, m_w_mla_out, m_w_out, m_g_post, v_w_ada, v_b_ada, v_g_pre, v_w_in, v_conv_w, v_w_conv_out, v_g_q, v_w_uq, v_g_kv, v_w_ukv, v_w_mla_out, v_w_out, v_g_post):
    nb, seq, _ = x.shape
    t = nb * seq
    mx, my, mc = lax.axis_index("x"), lax.axis_index("y"), lax.axis_index("c")
    me = 4 * mx + 2 * my + mc
    co = {"x": mx, "y": my, "c": mc}

    x2 = x.reshape(t, D)
    tgt2 = loss_target.reshape(t, D)
    pos2 = positions.reshape(t, 1)

    ada_cols = w_ada.shape[2]
    b_cols = lax.dynamic_slice(b_ada, (0, me * ada_cols), (1, ada_cols))
    c_g, taps_g, mod_g = ada_gather(jnp.pad(c, ((0, 8 - nb), (0, 0))), _rows128(conv_w[0], 8), w_ada[0], b_cols)
    c_all = c_g[:, :nb].reshape(8 * nb, D)
    conv_full = taps_g[:, 0:3].transpose(1, 0, 2).reshape(3, D)
    conv_full8 = jnp.pad(conv_full, ((0, 5), (0, 0)))
    mod = mod_g[:, :nb].transpose(1, 0, 2).reshape(nb, 8 * ada_cols)
    shift = mod[:, 0:D].reshape(nb, 1, D)
    scale = mod[:, D:2 * D].reshape(nb, 1, D)
    gate = mod[:, 2 * D:3 * D].reshape(nb, 1, D)

    wt = w_in[0].T.astype(BF16)
    lo = lax.bitcast_convert_type(wt[:, :D // 2], jnp.uint16).astype(jnp.uint32)
    hi = lax.bitcast_convert_type(wt[:, D // 2:], jnp.uint16).astype(jnp.uint32)
    wt_bits = lax.bitcast_convert_type(lo | (hi << 16), F32)
    wt_bits, mod = lax.optimization_barrier((wt_bits, mod))
    shift = mod[:, 0:D].reshape(nb, 1, D)
    scale = mod[:, D:2 * D].reshape(nb, 1, D)
    gate = mod[:, 2 * D:3 * D].reshape(nb, 1, D)
    q4 = D // 4
    r3rd = wt_bits.shape[0] // 3
    plan = [(0, (k * r3rd, r3rd), (g * q4, q4), (_ORD_A, _ORD_B)[g]) for k in range(3) for g in range(2)]
    gw = allgather_big([wt_bits], plan, "gather_w_in", **prenorm_during(x2, scale, shift, g_pre, seq))
    gw, h = gw[:1], gw[1]
    late = [w_conv_out[0].astype(BF16), w_mla_out[0].astype(BF16), w_out[0].astype(BF16),
            jnp.pad(w_uq[0].T.astype(BF16), ((0, DQK - 192), (0, 0))), w_ukv[0].T.astype(BF16)]
    gw0, late = lax.optimization_barrier((gw[0], late))
    late_state, late_token = gather_start(late, "gather_late_start")
    wt_bits_all = gw0.reshape(N_IN, D // 2)

    inv_freq = ROPE_THETA ** (-jnp.arange(0, ROPE, 2, dtype=F32) / ROPE)
    invf = jnp.concatenate([inv_freq, inv_freq, jnp.zeros((128 - ROPE,), F32)]).reshape(1, 128)
    lane = np.arange(128)
    tabs = (invf,
            jnp.asarray(np.where(lane < HALF, -1.0, 0.0).reshape(1, 128), F32),
            jnp.asarray(np.where((lane >= HALF) & (lane < ROPE), 1.0, 0.0).reshape(1, 128), F32))

    proj, wt_p = proj_matmul(h, wt_bits_all, late_token)
    y = conv_fwd(proj, conv_full8, seq)
    gl = gather_wait(late_state, y, "gather_late_wait")
    wco = gl[0].reshape(D, D)
    wmo = gl[1].reshape(D, D)
    wout = gl[2].reshape(D, D)
    wuq_p = gl[3].reshape(H * DQK, QL)
    wukv = gl[4].reshape(H * 256, KVL)
    q_rot, k_cat, kv, qn, kvn = mla_prep_fwd(proj, pos2, g_q, g_kv, wuq_p, wukv, tabs)
    attn, lse = flash_fwd(q_rot, k_cat, kv, nb, seq)
    o, ya, yb, m, do2, dout, dgate, dg_post, loss_part = tail_fwd(
        y, attn, proj, x2, tgt2, gate, g_post, wco, wmo, wout, seq)

    dproj, dya, dyb, dattn, dy = tail_bwd(do2, proj, ya, yb, attn, wout, wmo, wco)
    g_wout = grad_matmul(m, do2, "grad_w_square")
    g_wmo = grad_matmul(o, dyb, "grad_w_square")
    g_wco = grad_matmul(y, dya, "grad_w_square")
    sc1, sc1_tok = scatter_start([g_wco, g_wmo, g_wout], "scatter_out_grads_start")
    dproj, dconv = conv_bwd(dproj, proj, dy, conv_full8, seq)
    dq_rot, dk, dv = flash_bwd(q_rot, k_cat, kv, attn, dattn, lse, nb, seq, sc1_tok)
    dproj, dq, dkv, dg_q, dg_kv = mla_prep_bwd(dproj, proj, dq_rot, dk, dv, pos2, g_q, g_kv, wuq_p, wukv, tabs)
    g_wuq_t = grad_matmul(dq, qn, "grad_w_uq")
    g_wukv_t = grad_matmul(dkv, kvn, "grad_w_ukv")
    sc2, sc2_tok = scatter_start([g_wuq_t, g_wukv_t], "scatter_mla_grads_start")
    g_win_p = win_grad_matmul(h, dproj, sc2_tok)

    g_wt = g_win_p.reshape(2, 2, 2, N_IN // 8, D)
    ords = [("c", "y", "x"), ("c", "x", "y")]
    hc = D // 2
    win_shape = (2, 2, N_IN // 8, hc)
    pick_w = lambda col: (lambda ref, cc: ref.at[:, :, 1 - cc["c"], :, pl.ds(col * hc, hc)])
    which1 = [0, 0]
    picks1 = [pick_w(0), pick_w(1)]
    st1, tok1 = swap_start([g_wt], which1, ["c"] * 2, picks1, [win_shape] * 2, "rs_c_start")
    assert nb == 2
    dh0 = dh_matmul(dproj, wt_p, tok1, seq, 0)
    (g_wt,), r1 = swap_wait(st1, dh0, which1, ["c"] * 2, picks1, "rs_c_wait")
    sel_xyc = jnp.stack([mx, my, mc]).astype(jnp.int32)
    sel2 = [jnp.stack([co[o[2]]]).astype(jnp.int32) for o in ords]
    first = [rs_win_add_first(g_wt, r1[0], sel_xyc, 1, 0, "rs_add_first_0"),
             rs_win_add_first(g_wt, r1[1], sel_xyc, 0, 1, "rs_add_first_1")]
    keep1, send1 = zip(*first)
    all4 = [0, 1]
    none4 = [None] * 2
    axes2 = [o[1] for o in ords]
    st2, tok2 = swap_start(list(send1), all4, axes2, none4, [s.shape for s in send1], "rs_ici1_start")

    dh1 = dh_matmul(dproj, wt_p, tok2, seq, 1)
    gx0, dsh0, dsc0, dgp0 = prenorm_bwd(dh0, x2, dout, scale, g_pre, seq, tok2, 0, None)
    _, r2 = swap_wait(st2, (gx0, dh1), all4, axes2, none4, "rs_ici1_wait")
    keep2, send2 = zip(*[rs_add_second(keep1[a], r2[a], sel2[a], "rs_add_second") for a in range(2)])
    axes3 = [o[2] for o in ords]
    st3, tok3 = swap_start(list(send2), all4, axes3, none4, [s.shape for s in send2], "rs_ici2_start")
    grad_x2, dsh1, dsc1, dgp1 = prenorm_bwd(dh1, x2, dout, scale, g_pre, seq, tok3, 1, gx0)
    dshift = jnp.stack([dsh0, dsh1])
    dscale = jnp.stack([dsc0, dsc1])
    dg_pre = dgp0 + dgp1

    dmod = jnp.concatenate([dshift, dscale, dgate], axis=2).reshape(nb * 3 * D // 128, 128)
    small = jnp.concatenate([
        dmod, _rows128(dg_pre, 8), _rows128(dg_post, 8), _rows128(dg_q, 8), _rows128(dg_kv, 8),
        dconv[0:3].reshape(24, 128), _rows128(loss_part, 8)], axis=0)
    small_g = small_allgather(small, "gather_small_grads")
    sums = slot_sum(small_g)
    dmod_all = small_g[:, 0:48].reshape(8 * nb, 3 * D)
    g_bada = (sums[0:24] + sums[24:48]).reshape(1, 3 * D)
    g_gpre = sums[48:56].reshape(1, D)
    g_gpost = sums[56:64].reshape(1, D)
    g_gq = sums[64:67].reshape(1, QL)
    g_gkv = sums[72:74].reshape(1, KVL)
    g_conv_full = sums[80:104].reshape(3, D)
    loss = sums[104, 0]
    g_conv = lax.dynamic_slice(g_conv_full, (0, me * 128), (3, 128))
    dmod_cols = lax.dynamic_slice(dmod_all, (0, me * ada_cols), (8 * nb, ada_cols))
    g_wada = ada_bwd(c_all, dmod_cols)

    res = {}
    res["w_ada"] = [o_[None] for o_ in (g_wada, *adamw(w_ada[0], m_w_ada[0], v_w_ada[0], g_wada, "adamw_w_ada", tok3))]

    def pack(b_, gp_, gpo_, gq_, gkv_, cw_):
        return jnp.concatenate([_rows128(b_, 24), _rows128(gp_, 8), _rows128(gpo_, 8), _rows128(gq_, 8),
                                _rows128(gkv_, 8), _rows128(cw_, 8)], axis=0)

    sw = pack(b_ada, g_pre, g_post, g_q, g_kv, conv_w)
    sm = pack(m_b_ada, m_g_pre, m_g_post, m_g_q, m_g_kv, m_conv_w)
    sv = pack(v_b_ada, v_g_pre, v_g_post, v_g_q, v_g_kv, v_conv_w)
    sg = pack(g_bada, g_gpre, g_gpost, g_gq, g_gkv, g_conv)
    small_out = (sg, *adamw(sw, sm, sv, sg, "adamw_small", tok3))

    _, r3 = swap_wait(st3, small_out[1], all4, axes3, none4, "rs_ici2_wait")

    (g_wco, g_wmo, g_wout), (l_wco, l_wmo, l_wout) = scatter_wait(sc1, small_out[2], "scatter_out_grads_wait")
    (g_wuq_t, g_wukv_t), (l_wuq, l_wukv) = scatter_wait(sc2, small_out[3], "scatter_mla_grads_wait")

    res["w_in"] = [o_.T[None] for o_ in adamw_win(w_in[0].T, m_w_in[0].T, v_w_in[0].T,
                                                  keep2[0], r3[0], keep2[1], r3[1])]
    me1 = me.reshape(1).astype(jnp.int32)
    res["w_uq"] = [o_.T[None] for o_ in adamw_scattered(
        w_uq[0].T, m_w_uq[0].T, v_w_uq[0].T, g_wuq_t, l_wuq, me1, 64, "adamw_w_uq")]
    res["w_ukv"] = [o_[None] for o_ in adamw_scattered(
        w_ukv[0], m_w_ukv[0], v_w_ukv[0], g_wukv_t, l_wukv, me1, KVL, "adamw_w_ukv", transpose=True)]
    for nm, wv, mv, vv, gg, ll in (("w_conv_out", w_conv_out, m_w_conv_out, v_w_conv_out, g_wco, l_wco),
                                   ("w_mla_out", w_mla_out, m_w_mla_out, v_w_mla_out, g_wmo, l_wmo),
                                   ("w_out", w_out, m_w_out, v_w_out, g_wout, l_wout)):
        res[nm] = [o_[None] for o_ in adamw_scattered(wv[0], mv[0], vv[0], gg, ll, me1, 128, "adamw_square")]

    def unpack(a):
        return {"b_ada": a[0:24].reshape(1, 3 * D), "g_pre": a[24:32].reshape(1, D),
                "g_post": a[32:40].reshape(1, D), "g_q": a[40:43].reshape(1, QL),
                "g_kv": a[48:50].reshape(1, KVL), "conv_w": a[56:59].reshape(-1)[:3 * 128].reshape(1, 3, 128)}

    for nm in ("b_ada", "g_pre", "g_post", "g_q", "g_kv", "conv_w"):
        res[nm] = [unpack(a)[nm] for a in small_out]

    order = ["w_ada", "b_ada", "g_pre", "w_in", "conv_w", "w_conv_out", "g_q", "w_uq", "g_kv", "w_ukv",
             "w_mla_out", "w_out", "g_post"]
    out = [loss, grad_x2.reshape(nb, seq, D)]
    for k_ in range(4):
        out += [res[nm][k_] for nm in order]
    return tuple(out)
```

```python
import numpy as np
import jax
import jax.numpy as jnp
from jax import lax
from jax.experimental import pallas as pl
from jax.experimental.pallas import tpu as pltpu

F32 = jnp.float32
BF16 = jnp.bfloat16
MESH = pl.DeviceIdType.MESH

D = 1024
H = 8
QL = 384
KVL = 256
ROPE = 64
HALF = ROPE // 2
DQK = 256
DV = 128
NSEG = 8
NP = NSEG * D
EPS = 1e-6
ROPE_THETA = 10000.0
SM_SCALE = (128 + ROPE) ** -0.5
LOG2E = 1.4426950408889634
LN2 = 0.6931471805599453
FLASH_TQ = 512

SEG_BZ, SEG_GA, SEG_GB, SEG_LAT, SEG_V = 0, 1, 2, 3, 4

ADAM_LR = 0.001
ADAM_B1 = 0.9
ADAM_B2 = 0.999
ADAM_EPS = 1e-08
ADAM_WD = 0.01
ADAM_STEP = 10

VMEM_LIMIT = 56 * 1024 * 1024


def _params(sem=None, vmem=VMEM_LIMIT):
    kw = dict(vmem_limit_bytes=vmem)
    if sem is not None:
        kw["dimension_semantics"] = sem
    return pltpu.CompilerParams(**kw)


def _sig(v):
    return 0.5 * jnp.tanh(0.5 * v) + 0.5


def _dot(a, b):
    return jnp.dot(a, b, preferred_element_type=F32)


def _dot_nt(a, b):
    return lax.dot_general(a, b, (((1,), (1,)), ((), ())), preferred_element_type=F32)


def _dot_tn(a, b):
    return lax.dot_general(a, b, (((0,), (0,)), ((), ())), preferred_element_type=F32)


_AXIS_POS = {"x": 0, "y": 1, "c": 2}


def _coords():
    return lax.axis_index("x"), lax.axis_index("y"), lax.axis_index("c")


def _partner(axis):
    p = list(_coords())
    p[_AXIS_POS[axis]] = 1 - p[_AXIS_POS[axis]]
    return tuple(p)


def small_allgather(v, name):
    rows = v.shape[0]

    def body(v_ref, out_ref, sum_ref, send_sems, recv_sems):
        x, y, c = _coords()
        me = 4 * x + 2 * y + c
        out_ref[me] = v_ref[...]
        copies = []
        for k in range(1, 8):
            peer = (1 - x if k & 4 else x, 1 - y if k & 2 else y, 1 - c if k & 1 else c)
            cp = pltpu.make_async_remote_copy(
                src_ref=v_ref, dst_ref=out_ref.at[me],
                send_sem=send_sems.at[k - 1], recv_sem=recv_sems.at[k - 1],
                device_id=peer, device_id_type=MESH)
            cp.start()
            copies.append(cp)
        for cp in copies:
            cp.wait()
        acc = out_ref[0]
        for s in range(1, 8):
            acc = acc + out_ref[s]
        sum_ref[...] = acc

    vmem = pl.BlockSpec(memory_space=pltpu.VMEM)
    return pl.pallas_call(
        body, name=name,
        out_shape=[jax.ShapeDtypeStruct((8, rows, 128), F32), jax.ShapeDtypeStruct((rows, 128), F32)],
        in_specs=[vmem], out_specs=[vmem, vmem],
        scratch_shapes=[pltpu.SemaphoreType.DMA((7,)), pltpu.SemaphoreType.DMA((7,))],
    )(v)


def _own_block_placed(s):
    x, y, c = _coords()
    return lax.dynamic_update_slice(lax.empty((2, 2, 2) + s.shape, s.dtype), s[None, None, None],
                                    (x, y, c) + (0,) * s.ndim)


def allgather_big(arrs, plan, name, during=None, extra_in=(), extra_out=()):
    n = len(arrs)
    m = len(plan)
    nst = len(plan[0][3])
    k_in, k_out = len(extra_in), len(extra_out)

    def body(*refs):
        ins, outs = refs[n:2 * n], refs[2 * n + k_in:3 * n + k_in]
        extra = refs[2 * n:2 * n + k_in] + refs[3 * n + k_in:3 * n + k_in + k_out]
        send_sems, recv_sems = refs[3 * n + k_in + k_out:]
        x, y, c = _coords()
        co = {"x": x, "y": y, "c": c}

        def window(ref, lead, rows, cols):
            win = tuple(slice(None) if w is None else pl.ds(w[0], w[1]) for w in (rows, cols))
            return ref.at[tuple(lead) + win]

        def held(e, free):
            i, rows, cols, _ = plan[e]
            lead = [slice(None) if ax in free else co[ax] for ax in ("x", "y", "c")]
            return window(outs[i], lead, rows, cols)

        def rcopy(e, stage, src, dst, axis):
            return pltpu.make_async_remote_copy(
                src_ref=src, dst_ref=dst,
                send_sem=send_sems.at[e, stage], recv_sem=recv_sems.at[e, stage],
                device_id=_partner(axis), device_id_type=MESH)

        stages = [[] for _ in range(nst)]
        for e, (i, rows, cols, order) in enumerate(plan):
            cp = rcopy(e, 0, window(ins[i], [], rows, cols), held(e, ()), order[0])
            cp.start()
            stages[0].append(cp)
        for s in range(1, nst):
            if during is not None:
                during(s - 1, *extra)
            for e, (i, rows, cols, order) in enumerate(plan):
                stages[s - 1][e].wait_recv()
                blk = held(e, order[:s])
                cp = rcopy(e, s, blk, blk, order[s])
                cp.start()
                stages[s].append(cp)
        for e in range(m):
            stages[nst - 1][e].wait_recv()
        for e in range(m):
            for s in range(nst):
                stages[s][e].wait_send()

    any_spec = pl.BlockSpec(memory_space=pl.ANY)
    lands = [_own_block_placed(a) for a in arrs]
    return pl.pallas_call(
        body, name=name,
        out_shape=[jax.ShapeDtypeStruct(l.shape, l.dtype) for l in lands] + [o for o, _ in extra_out],
        in_specs=[any_spec] * (2 * n) + [sp for _, sp in extra_in],
        out_specs=[any_spec] * n + [sp for _, sp in extra_out],
        input_output_aliases={i: i for i in range(n)},
        scratch_shapes=[pltpu.SemaphoreType.DMA((m, nst)), pltpu.SemaphoreType.DMA((m, nst))],
        compiler_params=_params(),
    )(*lands, *arrs, *[a for a, _ in extra_in])


_HBM =pl.BlockSpec(memory_space=pltpu.HBM)
_SEM = pl.BlockSpec(memory_space=pltpu.SEMAPHORE)


def _swap_copies(srcs, lands, send_sems, recv_sems, axes, picks):
    x, y, c = _coords()
    co = {"x": x, "y": y, "c": c}
    return [pltpu.make_async_remote_copy(
        src_ref=srcs[a] if picks[a] is None else picks[a](srcs[a], co), dst_ref=lands[a],
        send_sem=send_sems.at[a], recv_sem=recv_sems.at[a],
        device_id=_partner(axes[a]), device_id_type=MESH) for a in range(len(srcs))]


def swap_start(arrs, which, axes, picks, out_shapes, name):
    ns, n = len(arrs), len(which)

    def body(*refs):
        srcs, lands = refs[:ns], refs[ns:ns + n]
        send_sems, recv_sems = refs[ns + n:ns + n + 2]
        token = refs[-1]
        for cp in _swap_copies([srcs[i] for i in which], lands, send_sems, recv_sems, axes, picks):
            cp.start()
        token[...] = jnp.zeros_like(token)

    lands = [lax.empty(s, arrs[i].dtype) for s, i in zip(out_shapes, which)]
    ops = [pltpu.with_memory_space_constraint(a, pltpu.HBM) for a in list(arrs) + lands]
    out = pl.pallas_call(
        body, name=name,
        out_shape=[pltpu.SemaphoreType.DMA((n,)), pltpu.SemaphoreType.DMA((n,))]
        + [pltpu.HBM(o.shape, o.dtype) for o in ops] + [jax.ShapeDtypeStruct((8, 128), F32)],
        in_specs=[_HBM] * (ns + n),
        out_specs=[_SEM, _SEM] + [_HBM] * (ns + n) + [pl.BlockSpec(memory_space=pltpu.VMEM)],
        input_output_aliases={i: 2 + i for i in range(ns + n)},
        compiler_params=pltpu.CompilerParams(has_side_effects=pltpu.SideEffectType.DATAFLOW_SIDE_EFFECTING),
    )(*ops)
    return out[:-1], out[-1]


def swap_wait(state, after, which, axes, picks, name):
    n = len(which)
    ns = len(state) - 2 - n

    def body(*refs):
        srcs, lands = refs[:ns], refs[ns:ns + n]
        send_sems, recv_sems = refs[ns + n:ns + n + 2]
        for cp in _swap_copies([srcs[i] for i in which], lands, send_sems, recv_sems, axes, picks):
            cp.wait_send()
            cp.wait_recv()

    thru = list(state[2:])
    after = list(after) if isinstance(after, (list, tuple)) else [after]
    out = pl.pallas_call(
        body, name=name,
        out_shape=[pltpu.HBM(o.shape, o.dtype) for o in thru],
        in_specs=[_HBM] * (ns + n) + [_SEM, _SEM] + [pl.BlockSpec(memory_space=pl.ANY)] * len(after),
        out_specs=[_HBM] * (ns + n),
        input_output_aliases={i: i for i in range(ns + n)},
        compiler_params=pltpu.CompilerParams(has_side_effects=pltpu.SideEffectType.DATAFLOW_SIDE_EFFECTING),
    )(*thru, state[0], state[1], *after)
    return out[:ns], out[ns:]


def _gather_copies(shards, lands, send_sems, recv_sems):
    x, y, c = _coords()
    copies = []
    for a in range(len(shards)):
        for k in range(1, 8):
            peer = (1 - x if k & 4 else x, 1 - y if k & 2 else y, 1 - c if k & 1 else c)
            copies.append(pltpu.make_async_remote_copy(
                src_ref=shards[a], dst_ref=lands[a].at[x, y, c],
                send_sem=send_sems.at[7 * a + k - 1], recv_sem=recv_sems.at[7 * a + k - 1],
                device_id=peer, device_id_type=MESH))
    return copies


def gather_start(shards, name):
    n = len(shards)
    x, y, c = _coords()

    def body(*refs):
        srcs, lands = refs[:n], refs[n:2 * n]
        send_sems, recv_sems = refs[2 * n:2 * n + 2]
        token = refs[-1]
        for cp in _gather_copies(srcs, lands, send_sems, recv_sems):
            cp.start()
        token[...] = jnp.zeros_like(token)

    lands = [_own_block_placed(s) for s in shards]
    ops = [pltpu.with_memory_space_constraint(a, pltpu.HBM) for a in list(shards) + lands]
    out = pl.pallas_call(
        body, name=name,
        out_shape=[pltpu.SemaphoreType.DMA((7 * n,)), pltpu.SemaphoreType.DMA((7 * n,))]
        + [pltpu.HBM(o.shape, o.dtype) for o in ops] + [jax.ShapeDtypeStruct((8, 128), F32)],
        in_specs=[_HBM] * (2 * n),
        out_specs=[_SEM, _SEM] + [_HBM] * (2 * n) + [pl.BlockSpec(memory_space=pltpu.VMEM)],
        input_output_aliases={i: 2 + i for i in range(2 * n)},
        compiler_params=pltpu.CompilerParams(has_side_effects=pltpu.SideEffectType.DATAFLOW_SIDE_EFFECTING),
    )(*ops)
    return out[:-1], out[-1]


def gather_wait(state, after, name):
    n = (len(state) - 2) // 2

    def body(*refs):
        srcs, lands = refs[:n], refs[n:2 * n]
        send_sems, recv_sems = refs[2 * n:2 * n + 2]
        for cp in _gather_copies(srcs, lands, send_sems, recv_sems):
            cp.wait_send()
            cp.wait_recv()

    thru = list(state[2:])
    out = pl.pallas_call(
        body, name=name,
        out_shape=[pltpu.HBM(o.shape, o.dtype) for o in thru],
        in_specs=[_HBM] * (2 * n) + [_SEM, _SEM, pl.BlockSpec(memory_space=pl.ANY)],
        out_specs=[_HBM] * (2 * n),
        input_output_aliases={i: i for i in range(2 * n)},
        compiler_params=pltpu.CompilerParams(has_side_effects=pltpu.SideEffectType.DATAFLOW_SIDE_EFFECTING),
    )(*thru, state[0], state[1], after)
    return out[n:]


def _scatter_copies(grads, lands, send_sems, recv_sems):
    x, y, c = _coords()
    me = 4 * x + 2 * y + c
    copies = []
    for a in range(len(grads)):
        r = grads[a].shape[0] // 8
        for k in range(1, 8):
            px, py, pc = (1 - x if k & 4 else x, 1 - y if k & 2 else y, 1 - c if k & 1 else c)
            rows = pl.ds(pl.multiple_of((4 * px + 2 * py + pc) * r, r), r)
            copies.append(pltpu.make_async_remote_copy(
                src_ref=grads[a].at[rows], dst_ref=lands[a].at[me],
                send_sem=send_sems.at[7 * a + k - 1], recv_sem=recv_sems.at[7 * a + k - 1],
                device_id=(px, py, pc), device_id_type=MESH))
    return copies


def scatter_start(grads, name):
    n = len(grads)

    def body(*refs):
        srcs, lands = refs[:n], refs[n:2 * n]
        send_sems, recv_sems = refs[2 * n:2 * n + 2]
        token = refs[-1]
        for cp in _scatter_copies(srcs, lands, send_sems, recv_sems):
            cp.start()
        token[...] = jnp.zeros_like(token)

    lands = [lax.empty((8, g.shape[0] // 8, g.shape[1]), g.dtype) for g in grads]
    ops = [pltpu.with_memory_space_constraint(a, pltpu.HBM) for a in list(grads) + lands]
    out = pl.pallas_call(
        body, name=name,
        out_shape=[pltpu.SemaphoreType.DMA((7 * n,)), pltpu.SemaphoreType.DMA((7 * n,))]
        + [pltpu.HBM(o.shape, o.dtype) for o in ops] + [jax.ShapeDtypeStruct((8, 128), F32)],
        in_specs=[_HBM] * (2 * n),
        out_specs=[_SEM, _SEM] + [_HBM] * (2 * n) + [pl.BlockSpec(memory_space=pltpu.VMEM)],
        input_output_aliases={i: 2 + i for i in range(2 * n)},
        compiler_params=pltpu.CompilerParams(has_side_effects=pltpu.SideEffectType.DATAFLOW_SIDE_EFFECTING),
    )(*ops)
    return out[:-1], out[-1]


def scatter_wait(state, after, name):
    n = (len(state) - 2) // 2

    def body(*refs):
        srcs, lands = refs[:n], refs[n:2 * n]
        send_sems, recv_sems = refs[2 * n:2 * n + 2]
        for cp in _scatter_copies(srcs, lands, send_sems, recv_sems):
            cp.wait_send()
            cp.wait_recv()

    thru = list(state[2:])
    after = list(after) if isinstance(after, (list, tuple)) else [after]
    out = pl.pallas_call(
        body, name=name,
        out_shape=[pltpu.HBM(o.shape, o.dtype) for o in thru],
        in_specs=[_HBM] * (2 * n) + [_SEM, _SEM] + [pl.BlockSpec(memory_space=pl.ANY)] * len(after),
        out_specs=[_HBM] * (2 * n),
        input_output_aliases={i: i for i in range(2 * n)},
        compiler_params=pltpu.CompilerParams(has_side_effects=pltpu.SideEffectType.DATAFLOW_SIDE_EFFECTING),
    )(*thru, state[0], state[1], *after)
    return out[:n], out[n:]


def rs_win_add_first(g, r, sel, next_dim, col, name):
    rows, cols = r.shape[2:]

    def body(sel_ref, gk_ref, rk_ref, gs_ref, rs_ref, keep_ref, send_ref):
        keep_ref[...] = gk_ref[...] + rk_ref[...]
        send_ref[...] = (gs_ref[...] + rs_ref[...]).astype(BF16)

    def g_map(flip):
        def f(j, s):
            nxt = 1 - s[next_dim] if flip else s[next_dim]
            return (nxt, j, s[2], 0, col) if next_dim == 0 else (j, nxt, s[2], 0, col)
        return f

    def r_map(flip):
        def f(j, s):
            nxt = 1 - s[next_dim] if flip else s[next_dim]
            return (nxt, j, 0, 0) if next_dim == 0 else (j, nxt, 0, 0)
        return f

    gblk = (None, None, None, rows, cols)
    rblk = (None, None, rows, cols)
    oblk = (None, rows, cols)
    return pl.pallas_call(
        body, name=name,
        grid_spec=pltpu.PrefetchScalarGridSpec(
            num_scalar_prefetch=1, grid=(2,),
            in_specs=[pl.BlockSpec(gblk, g_map(False)), pl.BlockSpec(rblk, r_map(False)),
                      pl.BlockSpec(gblk, g_map(True)), pl.BlockSpec(rblk, r_map(True))],
            out_specs=[pl.BlockSpec(oblk, lambda j, s: (j, 0, 0)),
                       pl.BlockSpec(oblk, lambda j, s: (j, 0, 0))]),
        out_shape=[jax.ShapeDtypeStruct((2, rows, cols), F32),
                   jax.ShapeDtypeStruct((2, rows, cols), BF16)],
        compiler_params=_params(),
    )(sel, g, r, g, r)


def rs_add_second(k, r, sel, name):
    _, rows, cols = k.shape
    tr = rows // 2 if rows % 32 == 0 else rows
    nt = rows // tr

    def body(sel_ref, kk_ref, rk_ref, ks_ref, rs_ref, keep_ref, send_ref):
        keep_ref[...] = kk_ref[...] + rk_ref[...].astype(F32)
        send_ref[...] = (ks_ref[...] + rs_ref[...].astype(F32)).astype(BF16)

    blk = (None, tr, cols)
    oblk = (tr, cols)
    return pl.pallas_call(
        body, name=name,
        grid_spec=pltpu.PrefetchScalarGridSpec(
            num_scalar_prefetch=1, grid=(nt,),
            in_specs=[
                pl.BlockSpec(blk, lambda i, s: (s[0], i, 0)),
                pl.BlockSpec(blk, lambda i, s: (s[0], i, 0)),
                pl.BlockSpec(blk, lambda i, s: (1 - s[0], i, 0)),
                pl.BlockSpec(blk, lambda i, s: (1 - s[0], i, 0)),
            ],
            out_specs=[pl.BlockSpec(oblk, lambda i, s: (i, 0)),
                       pl.BlockSpec(oblk, lambda i, s: (i, 0))]),
        out_shape=[jax.ShapeDtypeStruct((rows, cols), F32),
                   jax.ShapeDtypeStruct((rows, cols), BF16)],
        compiler_params=_params(),
    )(sel, k, r, k, r)


SEG_ROWS = (4800, 5824, 6848, 4096, 0, 1024, 2048, 3072)
LAT_ROWS = QL + KVL + ROPE
N_IN = 7872


def _seg_row(j):
    return pl.multiple_of(jnp.where(j < 3, 4800 + 1024 * j, jnp.where(j == 3, 4096, (j - 4) * 1024)), 8)


def proj_matmul(h, wt_bits, token):
    t = h.shape[0]
    tm = min(2048, t)

    def body(h_ref, w_hbm, tok_ref, o_ref, wt_ref, buf, sems):
        j = pl.program_id(0)
        slot = j % 2

        def fetch(seg, into):
            return pltpu.make_async_copy(w_hbm.at[pl.ds(_seg_row(seg), D)], buf.at[into], sems.at[into])

        @pl.when(pl.program_id(1) == 0)
        def _():
            @pl.when(j == 0)
            def _():
                fetch(j, slot).start()

            fetch(j, slot).wait()

            @pl.when(j + 1 < NSEG)
            def _():
                fetch(j + 1, 1 - slot).start()

            bits = pltpu.bitcast(buf[slot], jnp.uint32)
            row = lax.broadcasted_iota(jnp.int32, (D, D // 2), 0)
            live = jnp.logical_or(j != SEG_LAT, row < LAT_ROWS)
            lo = pltpu.bitcast(bits << 16, F32)
            hi = pltpu.bitcast(bits & jnp.uint32(0xFFFF0000), F32)
            wt_ref[:, :D // 2] = jnp.where(live, lo, 0.0).astype(BF16)
            wt_ref[:, D // 2:] = jnp.where(live, hi, 0.0).astype(BF16)

        o_ref[...] = _dot_nt(h_ref[...], wt_ref[...]).astype(BF16)

    return pl.pallas_call(
        body, name="proj_matmul", grid=(NSEG, t // tm),
        in_specs=[pl.BlockSpec((tm, D), lambda j, i: (i, 0)),
                  pl.BlockSpec(memory_space=pl.ANY),
                  pl.BlockSpec((8, 128), lambda j, i: (0, 0))],
        out_specs=[pl.BlockSpec((None, tm, D), lambda j, i: (j, i, 0)),
                   pl.BlockSpec((D, D), lambda j, i: (j, 0))],
        out_shape=[jax.ShapeDtypeStruct((NSEG, t, D), BF16), jax.ShapeDtypeStruct((NP, D), BF16)],
        scratch_shapes=[pltpu.VMEM((2, D, D // 2), F32), pltpu.SemaphoreType.DMA((2,))],
        compiler_params=_params(("arbitrary", "arbitrary")),
    )(h, wt_bits, token)


def dh_matmul(dproj, wt, token, seq, b):
    tm = min(1024, seq)
    nblk = seq // tm

    per = 2

    def body(b_ref, d_ref, w_ref, tok_ref, o_ref, acc_ref):
        k = pl.program_id(1)
        last = NSEG // per - 1

        def part():
            p = _dot(d_ref[0], w_ref[0:D, :])
            for j in range(1, per):
                p = p + _dot(d_ref[j], w_ref[j * D:(j + 1) * D, :])
            return p

        @pl.when(k == 0)
        def _():
            acc_ref[...] = part()

        @pl.when(jnp.logical_and(k > 0, k < last))
        def _():
            acc_ref[...] += part()

        @pl.when(k == last)
        def _():
            o_ref[...] = acc_ref[...] + part()

    return pl.pallas_call(
        body, name="dh_matmul",
        grid_spec=pltpu.PrefetchScalarGridSpec(
            num_scalar_prefetch=1, grid=(nblk, NSEG // per),
            in_specs=[pl.BlockSpec((per, tm, D), lambda i, k, s: (k, s[0] * nblk + i, 0)),
                      pl.BlockSpec((per * D, D), lambda i, k, s: (k, 0)),
                      pl.BlockSpec((8, 128), lambda i, k, s: (0, 0))],
            out_specs=pl.BlockSpec((tm, D), lambda i, k, s: (i, 0)),
            scratch_shapes=[pltpu.VMEM((tm, D), F32)]),
        out_shape=jax.ShapeDtypeStruct((seq, D), F32),
        compiler_params=_params(("parallel", "arbitrary")),
    )(jnp.full((1,), b, jnp.int32), dproj, wt, token)


def win_grad_matmul(h, dproj, token):
    t = h.shape[0]

    def body(h_ref, d_ref, tok_ref, o_hbm, acc_ref, sems):
        j = pl.program_id(0)

        def out_copy(jj, action):
            slot = lax.rem(jj, 2)

            @pl.when(jj != SEG_LAT)
            def _():
                action(pltpu.make_async_copy(acc_ref.at[slot], o_hbm.at[pl.ds(_seg_row(jj), D)],
                                             sems.at[slot]))

            @pl.when(jj == SEG_LAT)
            def _():
                action(pltpu.make_async_copy(acc_ref.at[slot, pl.ds(0, LAT_ROWS)],
                                             o_hbm.at[pl.ds(SEG_ROWS[SEG_LAT], LAT_ROWS)], sems.at[slot]))

        acc_ref[lax.rem(j, 2)] = _dot_tn(d_ref[...], h_ref[...])
        out_copy(j, lambda cp: cp.start())

        @pl.when(j > 0)
        def _():
            out_copy(j - 1, lambda cp: cp.wait())

        @pl.when(j == NSEG - 1)
        def _():
            out_copy(j, lambda cp: cp.wait())

    return pl.pallas_call(
        body, name="win_grad_matmul", grid=(NSEG,),
        in_specs=[pl.BlockSpec((t, D), lambda j: (0, 0)),
                  pl.BlockSpec((None, t, D), lambda j: (j, 0, 0)),
                  pl.BlockSpec((8, 128), lambda j: (0, 0))],
        out_specs=pl.BlockSpec(memory_space=pl.ANY),
        out_shape=jax.ShapeDtypeStruct((N_IN, D), F32),
        scratch_shapes=[pltpu.VMEM((2, D, D), F32), pltpu.SemaphoreType.DMA((2,))],
        compiler_params=_params(("arbitrary",)),
    )(h, dproj, token)


def grad_matmul(a, b, name):
    t, m = a.shape
    n = b.shape[1]
    tk = min(1024, t)
    nk = t // tk

    def body(a_ref, b_ref, o_ref, acc_ref):
        k = pl.program_id(0)
        part = lambda: _dot_tn(a_ref[...], b_ref[...])
        if nk == 1:
            o_ref[...] = part().astype(BF16)
            return

        @pl.when(k == 0)
        def _():
            acc_ref[...] = part()

        @pl.when(jnp.logical_and(k > 0, k < nk - 1))
        def _():
            acc_ref[...] += part()

        @pl.when(k == nk - 1)
        def _():
            o_ref[...] = (acc_ref[...] + part()).astype(BF16)

    return pl.pallas_call(
        body, name=name, grid=(nk,),
        in_specs=[pl.BlockSpec((tk, m), lambda k: (k, 0)),
                  pl.BlockSpec((tk, n), lambda k: (k, 0))],
        out_specs=pl.BlockSpec((m, n), lambda k: (0, 0)),
        out_shape=jax.ShapeDtypeStruct((m, n), BF16),
        scratch_shapes=[pltpu.VMEM((m, n), F32)],
        compiler_params=_params(("arbitrary",)),
    )(a, b)


def ada_gather(c8, taps8, w_ada, b_cols):
    cols = w_ada.shape[1]

    def body(c_ref, t_ref, w_ref, b_ref, call_ref, tall_ref, mod_ref, part_ref, send_sems, recv_sems):
        x, y, c = _coords()
        me = 4 * x + 2 * y + c
        peers = [(1 - x if k & 4 else x, 1 - y if k & 2 else y, 1 - c if k & 1 else c) for k in range(1, 8)]

        def rcopy(n, src, dst, peer):
            return pltpu.make_async_remote_copy(src_ref=src, dst_ref=dst, send_sem=send_sems.at[n],
                                                recv_sem=recv_sems.at[n], device_id=peer, device_id_type=MESH)

        call_ref[me] = c_ref[...]
        tall_ref[me] = t_ref[...]
        first = []
        for k, peer in enumerate(peers):
            first += [rcopy(k, c_ref, call_ref.at[me], peer), rcopy(7 + k, t_ref, tall_ref.at[me], peer)]
        for cp in first:
            cp.start()
        for cp in first:
            cp.wait()
        rows = call_ref[...].reshape(64, D).astype(BF16)
        part_ref[...] = _dot(rows, w_ref[...].astype(BF16)) + b_ref[...]
        mod_ref[me] = part_ref[pl.ds(pl.multiple_of(8 * me, 8), 8), :]
        second = []
        for k, (px, py, pc) in enumerate(peers):
            theirs = part_ref.at[pl.ds(pl.multiple_of(8 * (4 * px + 2 * py + pc), 8), 8)]
            second.append(rcopy(14 + k, theirs, mod_ref.at[me], (px, py, pc)))
        for cp in second:
            cp.start()
        for cp in second:
            cp.wait()

    vm = pl.BlockSpec(memory_space=pltpu.VMEM)
    return pl.pallas_call(
        body, name="ada_gather",
        out_shape=[jax.ShapeDtypeStruct((8, 8, D), F32), jax.ShapeDtypeStruct((8, 8, 128), F32),
                   jax.ShapeDtypeStruct((8, 8, cols), F32)],
        in_specs=[vm] * 4, out_specs=[vm] * 3,
        scratch_shapes=[pltpu.VMEM((64, cols), F32), pltpu.SemaphoreType.DMA((21,)),
                        pltpu.SemaphoreType.DMA((21,))],
        compiler_params=_params(),
    )(c8, taps8, w_ada, b_cols)


def ada_bwd(c_all, dmod_cols):
    def body(c_ref, d_ref, o_ref):
        o_ref[...] = _dot_tn(c_ref[...].astype(BF16), d_ref[...].astype(BF16))

    return pl.pallas_call(
        body, name="ada_bwd",
        out_shape=jax.ShapeDtypeStruct((c_all.shape[1], dmod_cols.shape[1]), F32),
        compiler_params=_params(),
    )(c_all, dmod_cols)


def prenorm_during(x2, scale, shift, g_pre, seq):
    tm = min(512, seq)
    assert x2.shape[0] == 2 * seq, "one sequence per exchange stage before the last"

    def during(b, x_hbm, sc_ref, sh_ref, g_ref, h_hbm):
        def tile(x_ref, h_ref):
            xv = x_ref[...]
            r = lax.rsqrt(jnp.mean(xv * xv, axis=-1, keepdims=True) + EPS)
            h_ref[...] = ((xv * r * g_ref[...]) * (1.0 + sc_ref[b]) + sh_ref[b]).astype(BF16)

        rows = pl.BlockSpec((tm, D), lambda i: (i, 0))
        pltpu.emit_pipeline(tile, grid=(seq // tm,), in_specs=[rows], out_specs=[rows])(
            x_hbm.at[pl.ds(b * seq, seq)], h_hbm.at[pl.ds(b * seq, seq)])

    vmem = pl.BlockSpec(memory_space=pltpu.VMEM)
    hbm = pl.BlockSpec(memory_space=pl.ANY)
    return dict(during=during,
                extra_in=[(x2, hbm), (scale, vmem), (shift, vmem), (g_pre, vmem)],
                extra_out=[(jax.ShapeDtypeStruct(x2.shape, BF16), hbm)])


def prenorm_bwd(dh, x2, dout, scale, g_pre, seq, token, b, gx_prev):
    t = x2.shape[0]
    tm = min(512, seq)
    tpb = seq // tm
    if gx_prev is None:
        gx_prev = lax.empty((t, D), F32)

    def body(b_ref, dh_ref, x_ref, do_ref, sc_ref, g_ref, tok_ref, gxp_ref, gx_ref, dsh_ref, dsc_ref, dg_ref):
        i = pl.program_id(0)
        xv = x_ref[...]
        dhv = dh_ref[...]
        g = g_ref[...]
        r = lax.rsqrt(jnp.mean(xv * xv, axis=-1, keepdims=True) + EPS)
        nrm = xv * r
        dxn = dhv * (1.0 + sc_ref[...])
        dn = dxn * g
        dx = r * (dn - nrm * jnp.mean(dn * nrm, axis=-1, keepdims=True))
        gx_ref[...] = dx + do_ref[...]

        @pl.when(i == 0)
        def _():
            dsh_ref[...] = jnp.zeros_like(dsh_ref)
            dsc_ref[...] = jnp.zeros_like(dsc_ref)
            dg_ref[...] = jnp.zeros_like(dg_ref)

        dsh_ref[...] += jnp.sum(dhv, axis=0, keepdims=True)
        dsc_ref[...] += jnp.sum(dhv * (nrm * g), axis=0, keepdims=True)
        dg_ref[...] += jnp.sum(dxn * nrm, axis=0, keepdims=True)

    row = pl.BlockSpec((tm, D), lambda i, s: (i, 0))
    grow = pl.BlockSpec((tm, D), lambda i, s: (s[0] * tpb + i, 0))
    per_batch = pl.BlockSpec((None, 1, D), lambda i, s: (s[0], 0, 0))
    vec = pl.BlockSpec((1, D), lambda i, s: (0, 0))
    return pl.pallas_call(
        body, name="prenorm_bwd",
        grid_spec=pltpu.PrefetchScalarGridSpec(
            num_scalar_prefetch=1, grid=(tpb,),
            in_specs=[row, grow, grow, per_batch, vec, pl.BlockSpec((8, 128), lambda i, s: (0, 0)),
                      pl.BlockSpec(memory_space=pl.ANY)],
            out_specs=[grow, vec, vec, vec]),
        out_shape=[jax.ShapeDtypeStruct((t, D), F32), jax.ShapeDtypeStruct((1, D), F32),
                   jax.ShapeDtypeStruct((1, D), F32), jax.ShapeDtypeStruct((1, D), F32)],
        input_output_aliases={7: 0},
        compiler_params=_params(("arbitrary",)),
    )(jnp.full((1,), b, jnp.int32), dh, x2, dout, scale, g_pre, token, gx_prev)


CONV_TC = 128


def _shift_down(u, k, rows):
    idx = lax.broadcasted_iota(jnp.int32, u.shape, 0)
    return jnp.where(idx >= k, pltpu.roll(u, k, 0), 0.0)


def _shift_up(u, k, rows):
    idx = lax.broadcasted_iota(jnp.int32, u.shape, 0)
    return jnp.where(idx < rows - k, pltpu.roll(u, rows - k, 0), 0.0)


def conv_fwd(proj, conv_w, seq):
    t = proj.shape[1]
    nb = t // seq

    def body(p_ref, w_ref, y_ref):
        av = p_ref[0].astype(F32)
        ab = p_ref[1].astype(F32)
        ac = p_ref[2].astype(F32)
        az = p_ref[3].astype(F32)
        w = w_ref[...]
        u = ac * av
        y1 = _shift_down(u, 2, seq) * w[0:1] + _shift_down(u, 1, seq) * w[1:2] + u * w[2:3]
        y_ref[...] = (ab * y1 * (az * _sig(az))).astype(BF16)

    return pl.pallas_call(
        body, name="conv_fwd", grid=(nb, D // CONV_TC),
        in_specs=[pl.BlockSpec((4, seq, CONV_TC), lambda b, ci: (1, b, ci)),
                  pl.BlockSpec((8, CONV_TC), lambda b, ci: (0, ci))],
        out_specs=pl.BlockSpec((seq, CONV_TC), lambda b, ci: (b, ci)),
        out_shape=jax.ShapeDtypeStruct((t, D), BF16),
        compiler_params=_params(("parallel", "parallel")),
    )(proj, conv_w)


def conv_bwd(dproj, proj, dy, conv_w, seq):
    t = proj.shape[1]
    nb = t // seq

    def body(dp_in_ref, p_ref, dy_ref, w_ref, dp_ref, dw_ref):
        b = pl.program_id(1)
        av = p_ref[0].astype(F32)
        ab = p_ref[1].astype(F32)
        ac = p_ref[2].astype(F32)
        az = p_ref[3].astype(F32)
        dyv = dy_ref[...].astype(F32)
        w = w_ref[...]
        u = ac * av
        u1 = _shift_down(u, 1, seq)
        u2 = _shift_down(u, 2, seq)
        y1 = u2 * w[0:1] + u1 * w[1:2] + u * w[2:3]
        sz = _sig(az)
        silu = az * sz
        dy1 = dyv * ab * silu
        du = dy1 * w[2:3] + _shift_up(dy1, 1, seq) * w[1:2] + _shift_up(dy1, 2, seq) * w[0:1]
        dp_ref[0] = (du * ac).astype(BF16)
        dp_ref[1] = (dyv * y1 * silu).astype(BF16)
        dp_ref[2] = (du * av).astype(BF16)
        dp_ref[3] = (dyv * ab * y1 * (sz * (1.0 + az * (1.0 - sz)))).astype(BF16)

        @pl.when(b == 0)
        def _():
            dw_ref[...] = jnp.zeros_like(dw_ref)

        dw_ref[0:1, :] += jnp.sum(dy1 * u2, axis=0, keepdims=True)
        dw_ref[1:2, :] += jnp.sum(dy1 * u1, axis=0, keepdims=True)
        dw_ref[2:3, :] += jnp.sum(dy1 * u, axis=0, keepdims=True)

    return pl.pallas_call(
        body, name="conv_bwd", grid=(D // CONV_TC, nb),
        in_specs=[pl.BlockSpec(memory_space=pl.ANY),
                  pl.BlockSpec((4, seq, CONV_TC), lambda ci, b: (1, b, ci)),
                  pl.BlockSpec((seq, CONV_TC), lambda ci, b: (b, ci)),
                  pl.BlockSpec((8, CONV_TC), lambda ci, b: (0, ci))],
        out_specs=[pl.BlockSpec((4, seq, CONV_TC), lambda ci, b: (1, b, ci)),
                   pl.BlockSpec((8, CONV_TC), lambda ci, b: (0, ci))],
        out_shape=[jax.ShapeDtypeStruct(dproj.shape, BF16),
                   jax.ShapeDtypeStruct((8, D), F32)],
        input_output_aliases={0: 0},
        compiler_params=_params(("parallel", "arbitrary")),
    )(dproj, proj, dy, conv_w)


def _rope_tables(pos_ref, invf_ref, ma_ref, mb_ref, sign):
    ang = pos_ref[...].astype(F32) * invf_ref[...]
    cs = jnp.cos(ang)
    sn = jnp.sin(ang) * sign
    return cs, sn * ma_ref[...], sn * mb_ref[...]


def _rotate(v, cs, sa, sb):
    return v * cs + pltpu.roll(v, 128 - HALF, 1) * sa + pltpu.roll(v, HALF, 1) * sb


MLA_TM = 512


def mla_prep_fwd(proj, pos, g_q, g_kv, wuq, wukv, tabs):
    t = proj.shape[1]
    tm = min(MLA_TM, t)

    def body(lat_ref, pos_ref, gq_ref, gkv_ref, wuq_ref, wukv_ref, invf_ref, ma_ref, mb_ref,
             q_ref, k_ref, kv_ref, qn_ref, kvn_ref):
        lat = lat_ref[...].astype(F32)
        ql = lat[:, :QL]
        kl = lat[:, QL:QL + KVL]
        kr = lat[:, QL + KVL:QL + KVL + 128]
        qn = (ql * lax.rsqrt(jnp.mean(ql * ql, axis=-1, keepdims=True) + EPS) * gq_ref[...]).astype(BF16)
        kvn = (kl * lax.rsqrt(jnp.mean(kl * kl, axis=-1, keepdims=True) + EPS) * gkv_ref[...]).astype(BF16)
        qn_ref[...] = qn
        kvn_ref[...] = kvn
        cs, sa, sb = _rope_tables(pos_ref, invf_ref, ma_ref, mb_ref, 1.0)
        q = _dot_nt(qn, wuq_ref[...]) * (SM_SCALE * LOG2E)
        kv = _dot_nt(kvn, wukv_ref[...]).astype(BF16)
        kv_ref[...] = kv
        kpe = _rotate(kr, cs, sa, sb).astype(BF16)
        for hh in range(H):
            lo, mid, hi = hh * DQK, hh * DQK + 128, (hh + 1) * DQK
            q_ref[:, lo:mid] = q[:, lo:mid].astype(BF16)
            q_ref[:, mid:hi] = _rotate(q[:, mid:hi], cs, sa, sb).astype(BF16)
            k_ref[:, lo:mid] = kv[:, lo:mid]
            k_ref[:, mid:hi] = kpe

    row = lambda w: pl.BlockSpec((tm, w), lambda i: (i, 0))
    const = lambda a: pl.BlockSpec(a.shape, lambda i: (0,) * a.ndim)
    return pl.pallas_call(
        body, name="mla_prep_fwd", grid=(t // tm,),
        in_specs=[pl.BlockSpec((None, tm, D), lambda i: (SEG_LAT, i, 0)), row(1),
                  const(g_q), const(g_kv), const(wuq), const(wukv)] + [const(a) for a in tabs],
        out_specs=[row(H * DQK), row(H * DQK), row(H * DQK), row(QL), row(KVL)],
        out_shape=[jax.ShapeDtypeStruct((t, H * DQK), BF16)] * 3
        + [jax.ShapeDtypeStruct((t, QL), BF16), jax.ShapeDtypeStruct((t, KVL), BF16)],
        compiler_params=_params(("parallel",)),
    )(proj, pos, g_q, g_kv, wuq, wukv, *tabs)


def mla_prep_bwd(dproj, proj, dq_rot, dk, dv, pos, g_q, g_kv, wuq, wukv, tabs):
    t = proj.shape[1]
    tm = min(MLA_TM, t)

    def body(dp_in_ref, lat_ref, dqr_ref, dk_ref, dv_ref, pos_ref, gq_ref, gkv_ref, wuq_ref, wukv_ref,
             invf_ref, ma_ref, mb_ref, dp_ref, dq_ref, dkv_ref, dgq_ref, dgkv_ref):
        i = pl.program_id(0)
        lat = lat_ref[...].astype(F32)
        ql = lat[:, :QL]
        kl = lat[:, QL:QL + KVL]
        rq = lax.rsqrt(jnp.mean(ql * ql, axis=-1, keepdims=True) + EPS)
        rk = lax.rsqrt(jnp.mean(kl * kl, axis=-1, keepdims=True) + EPS)
        nq = ql * rq
        nk = kl * rk
        cs, sa, sb = _rope_tables(pos_ref, invf_ref, ma_ref, mb_ref, -1.0)
        dkpe = jnp.zeros((tm, 128), F32)
        for hh in range(H):
            lo, mid, hi = hh * DQK, hh * DQK + 128, (hh + 1) * DQK
            dq_ref[:, lo:mid] = (dqr_ref[:, lo:mid] * SM_SCALE).astype(BF16)
            dq_ref[:, mid:hi] = _rotate(dqr_ref[:, mid:hi] * SM_SCALE, cs, sa, sb).astype(BF16)
            dkv_ref[:, lo:mid] = dk_ref[:, lo:mid]
            dkv_ref[:, mid:hi] = dv_ref[:, hh * DV:(hh + 1) * DV]
            dkpe = dkpe + dk_ref[:, mid:hi].astype(F32)
        lane = lax.broadcasted_iota(jnp.int32, (tm, 128), 1)
        dkr = jnp.where(lane < ROPE, _rotate(dkpe, cs, sa, sb), 0.0)
        dqn = _dot(dq_ref[...], wuq_ref[...])
        dkvn = _dot(dkv_ref[...], wukv_ref[...])
        gq = gq_ref[...]
        gkv = gkv_ref[...]
        dnq = dqn * gq
        dnk = dkvn * gkv
        dql = rq * (dnq - nq * jnp.mean(dnq * nq, axis=-1, keepdims=True))
        dkl = rk * (dnk - nk * jnp.mean(dnk * nk, axis=-1, keepdims=True))
        dp_ref[:, :QL] = dql.astype(BF16)
        dp_ref[:, QL:QL + KVL] = dkl.astype(BF16)
        dp_ref[:, QL + KVL:QL + KVL + 128] = dkr.astype(BF16)
        dp_ref[:, QL + KVL + 128:] = jnp.zeros((tm, D - QL - KVL - 128), BF16)

        @pl.when(i == 0)
        def _():
            dgq_ref[...] = jnp.zeros_like(dgq_ref)
            dgkv_ref[...] = jnp.zeros_like(dgkv_ref)

        dgq_ref[...] += jnp.sum(dqn * nq, axis=0, keepdims=True)
        dgkv_ref[...] += jnp.sum(dkvn * nk, axis=0, keepdims=True)

    row = lambda w: pl.BlockSpec((tm, w), lambda i: (i, 0))
    const = lambda a: pl.BlockSpec(a.shape, lambda i: (0,) * a.ndim)
    seg = pl.BlockSpec((None, tm, D), lambda i: (SEG_LAT, i, 0))
    return pl.pallas_call(
        body, name="mla_prep_bwd", grid=(t // tm,),
        in_specs=[pl.BlockSpec(memory_space=pl.ANY), seg, row(H * DQK), row(H * DQK), row(H * DV), row(1),
                  const(g_q), const(g_kv), const(wuq), const(wukv)] + [const(a) for a in tabs],
        out_specs=[seg, row(H * DQK), row(H * DQK),
                   pl.BlockSpec((1, QL), lambda i: (0, 0)), pl.BlockSpec((1, KVL), lambda i: (0, 0))],
        out_shape=[jax.ShapeDtypeStruct(dproj.shape, BF16),
                   jax.ShapeDtypeStruct((t, H * DQK), BF16), jax.ShapeDtypeStruct((t, H * DQK), BF16),
                   jax.ShapeDtypeStruct((1, QL), F32), jax.ShapeDtypeStruct((1, KVL), F32)],
        input_output_aliases={0: 0},
        compiler_params=_params(("arbitrary",)),
    )(dproj, proj, dq_rot, dk, dv, pos, g_q, g_kv, wuq, wukv, *tabs)


def _causal_mask(s, shift):
    row = lax.broadcasted_iota(jnp.int32, s.shape, 0)
    col = lax.broadcasted_iota(jnp.int32, s.shape, 1)
    return jnp.where(col <= row + shift, s, -1e30)


def flash_fwd(q, k, kv, nb, seq):
    t = q.shape[0]
    tq = min(FLASH_TQ, seq // 2)
    nq = seq // tq
    assert nq % 2 == 0, "blocks are processed in pairs"

    def update(state, s, vblk):
        m, l, acc = state
        m_new = jnp.maximum(m, jnp.max(s, axis=1, keepdims=True))
        p = jnp.exp2(s - m_new)
        alpha = jnp.exp2(m - m_new)
        return (m_new, alpha * l + jnp.sum(p, axis=1, keepdims=True),
                alpha * acc + _dot(p.astype(BF16), vblk))

    def finish(state, rows, o_ref, lse_ref):
        m, l, acc = state
        o_ref[rows, :] = (acc / l).astype(BF16)
        lse_ref[rows, :] = jnp.broadcast_to(m + jnp.log(l) * LOG2E, (m.shape[0], DV))

    def body(q_ref, k_ref, v_ref, o_ref, lse_ref):
        for qp in range(0, nq, 2):
            rows = 2 * tq
            q0 = qp * tq
            qv = q_ref[q0:q0 + rows, :]
            state = (jnp.full((rows, 1), -1e30, F32), jnp.zeros((rows, 1), F32), jnp.zeros((rows, DV), F32))
            for j in range(qp + 1):
                ks = slice(j * tq, (j + 1) * tq)
                s = _dot_nt(qv, k_ref[ks, :])
                if j == qp:
                    s = _causal_mask(s, 0)
                state = update(state, s, v_ref[ks, :])
            finish(tuple(a[:tq] for a in state), slice(q0, q0 + tq), o_ref, lse_ref)
            ks = slice(q0 + tq, q0 + 2 * tq)
            low = tuple(a[tq:] for a in state)
            low = update(low, _causal_mask(_dot_nt(qv[tq:], k_ref[ks, :]), 0), v_ref[ks, :])
            finish(low, slice(q0 + tq, q0 + 2 * tq), o_ref, lse_ref)

    out_blk = pl.BlockSpec((seq, DV), lambda b, h: (b, h))
    return pl.pallas_call(
        body, name="flash_fwd", grid=(nb, H),
        in_specs=[pl.BlockSpec((seq, DQK), lambda b, h: (b, h)),
                  pl.BlockSpec((seq, DQK), lambda b, h: (b, h)),
                  pl.BlockSpec((seq, DV), lambda b, h: (b, 2 * h + 1))],
        out_specs=[out_blk, out_blk],
        out_shape=[jax.ShapeDtypeStruct((t, H * DV), BF16), jax.ShapeDtypeStruct((t, H * DV), F32)],
        compiler_params=_params(("parallel", "parallel")),
    )(q, k, kv)


def flash_bwd(q, k, kv, o, do, lse, nb, seq, token):
    t = q.shape[0]
    tq = min(FLASH_TQ, seq)
    nq = seq // tq

    def body(q_ref, k_ref, v_ref, o_ref, do_ref, lse_ref, tok_ref, dq_ref, dk_ref, dv_ref):
        delta, lse = [], []
        for qi in range(nq):
            qs = slice(qi * tq, (qi + 1) * tq)
            dl = jnp.sum(do_ref[qs, :].astype(F32) * o_ref[qs, :].astype(F32), axis=1, keepdims=True)
            delta.append(jnp.broadcast_to(dl, (tq, DV)).T[:1, :])
            lse.append(lse_ref[qs, :].T[:1, :])
        for ki in range(nq):
            ks = slice(ki * tq, (ki + 1) * tq)
            kb = k_ref[ks, :]
            vb = v_ref[ks, :]
            dk = jnp.zeros((tq, DQK), F32)
            dv = jnp.zeros((tq, DV), F32)
            for qi in range(ki, nq):
                qs = slice(qi * tq, (qi + 1) * tq)
                qv = q_ref[qs, :]
                dov = do_ref[qs, :]
                st = _dot_nt(kb, qv)
                if qi == ki:
                    row = lax.broadcasted_iota(jnp.int32, st.shape, 0)
                    col = lax.broadcasted_iota(jnp.int32, st.shape, 1)
                    st = jnp.where(row <= col, st, -1e30)
                pt = jnp.exp2(st - lse[qi])
                dpt = _dot_nt(vb, dov)
                dzt = (pt * (dpt - delta[qi])).astype(BF16)
                dv = dv + _dot(pt.astype(BF16), dov)
                dk = dk + _dot(dzt, qv)
                dqb = _dot_tn(dzt, kb)
                if ki == 0:
                    dq_ref[qs, :] = dqb
                else:
                    dq_ref[qs, :] += dqb
            dk_ref[ks, :] = (dk * LN2).astype(BF16)
            dv_ref[ks, :] = dv.astype(BF16)

    full = lambda w, col: pl.BlockSpec((seq, w), col)
    same = lambda b, h: (b, h)
    return pl.pallas_call(
        body, name="flash_bwd", grid=(nb, H),
        in_specs=[full(DQK, same), full(DQK, same), full(DV, lambda b, h: (b, 2 * h + 1)),
                  full(DV, same), full(DV, same), full(DV, same),
                  pl.BlockSpec((8, 128), lambda b, h: (0, 0))],
        out_specs=[full(DQK, same), full(DQK, same), full(DV, same)],
        out_shape=[jax.ShapeDtypeStruct((t, H * DQK), F32), jax.ShapeDtypeStruct((t, H * DQK), BF16),
                   jax.ShapeDtypeStruct((t, H * DV), BF16)],
        compiler_params=_params(("parallel", "parallel")),
    )(q, k, kv, o, do, lse, token)


TAIL_TM = 512


def tail_fwd(y, attn, proj, x2, tgt, gate, g_post, wco, wmo, wout, seq):
    t = y.shape[0]
    nb = t // seq
    tm = min(TAIL_TM, seq)
    tpb = seq // tm

    def body(y_ref, at_ref, p_ref, x_ref, t_ref, gate_ref, gp_ref, wco_ref, wmo_ref, wout_ref,
             o_ref, ya_ref, yb_ref, m_ref, do2_ref, dout_ref, dgate_ref, dgp_ref, loss_ref):
        i = pl.program_id(0)
        bz = p_ref[0].astype(F32)
        ga = p_ref[1].astype(F32)
        gb = p_ref[2].astype(F32)
        ov = (at_ref[...].astype(F32) * (bz * _sig(bz))).astype(BF16)
        o_ref[...] = ov
        ya = _dot(y_ref[...], wco_ref[...])
        yb = _dot(ov, wmo_ref[...])
        ya_ref[...] = ya.astype(BF16)
        yb_ref[...] = yb.astype(BF16)
        mv = (_sig(ga) * ya + _sig(gb) * yb).astype(BF16)
        m_ref[...] = mv
        o2 = _dot(mv, wout_ref[...])
        r = lax.rsqrt(jnp.mean(o2 * o2, axis=-1, keepdims=True) + EPS)
        nrm = o2 * r
        gp = gp_ref[...]
        gate_v = gate_ref[...]
        rn = nrm * gp
        err = x_ref[...] + gate_v * rn - t_ref[...]
        dout = err * (1.0 / D)
        dout_ref[...] = dout
        dn = dout * gate_v * gp
        do2_ref[...] = (r * (dn - nrm * jnp.mean(dn * nrm, axis=-1, keepdims=True))).astype(BF16)

        @pl.when(i % tpb == 0)
        def _():
            dgate_ref[...] = jnp.zeros_like(dgate_ref)

        @pl.when(i == 0)
        def _():
            dgp_ref[...] = jnp.zeros_like(dgp_ref)
            loss_ref[...] = jnp.zeros_like(loss_ref)

        dgate_ref[...] += jnp.sum(dout * rn, axis=0, keepdims=True)
        dgp_ref[...] += jnp.sum(dout * gate_v * nrm, axis=0, keepdims=True)
        loss_ref[...] += 0.5 * jnp.sum(jnp.mean(err * err, axis=-1, keepdims=True), axis=0, keepdims=True)

    row = pl.BlockSpec((tm, D), lambda i: (i, 0))
    per_batch = pl.BlockSpec((None, 1, D), lambda i: (i // tpb, 0, 0))
    vec = pl.BlockSpec((1, D), lambda i: (0, 0))
    wgt = pl.BlockSpec((D, D), lambda i: (0, 0))
    act = jax.ShapeDtypeStruct((t, D), BF16)
    return pl.pallas_call(
        body, name="tail_fwd", grid=(t // tm,),
        in_specs=[row, row, pl.BlockSpec((3, tm, D), lambda i: (0, i, 0)), row, row, per_batch, vec,
                  wgt, wgt, wgt],
        out_specs=[row, row, row, row, row, row, per_batch, vec, pl.BlockSpec((1, 1), lambda i: (0, 0))],
        out_shape=[act, act, act, act, act, jax.ShapeDtypeStruct((t, D), F32),
                   jax.ShapeDtypeStruct((nb, 1, D), F32), jax.ShapeDtypeStruct((1, D), F32),
                   jax.ShapeDtypeStruct((1, 1), F32)],
        compiler_params=_params(("arbitrary",)),
    )(y, attn, proj, x2, tgt, gate, g_post, wco, wmo, wout)


def tail_bwd(do2, proj, ya, yb, attn, wout, wmo, wco):
    t = do2.shape[0]
    tm = min(TAIL_TM, t)

    def body(do2_ref, p_ref, ya_ref, yb_ref, at_ref, wout_ref, wmo_ref, wco_ref,
             dp_ref, dya_ref, dyb_ref, dat_ref, dy_ref):
        bz = p_ref[0].astype(F32)
        ga = p_ref[1].astype(F32)
        gb = p_ref[2].astype(F32)
        dm = _dot_nt(do2_ref[...], wout_ref[...])
        sa = _sig(ga)
        sb = _sig(gb)
        dya = (dm * sa).astype(BF16)
        dyb = (dm * sb).astype(BF16)
        dya_ref[...] = dya
        dyb_ref[...] = dyb
        dp_ref[1] = (dm * ya_ref[...].astype(F32) * (sa * (1.0 - sa))).astype(BF16)
        dp_ref[2] = (dm * yb_ref[...].astype(F32) * (sb * (1.0 - sb))).astype(BF16)
        dov = _dot_nt(dyb, wmo_ref[...])
        sz = _sig(bz)
        dat_ref[...] = (dov * (bz * sz)).astype(BF16)
        dp_ref[0] = (dov * at_ref[...].astype(F32) * (sz * (1.0 + bz * (1.0 - sz)))).astype(BF16)
        dy_ref[...] = _dot_nt(dya, wco_ref[...]).astype(BF16)

    row = pl.BlockSpec((tm, D), lambda i: (i, 0))
    seg3 = pl.BlockSpec((3, tm, D), lambda i: (0, i, 0))
    wgt = pl.BlockSpec((D, D), lambda i: (0, 0))
    act = jax.ShapeDtypeStruct((t, D), BF16)
    return pl.pallas_call(
        body, name="tail_bwd", grid=(t // tm,),
        in_specs=[row, seg3, row, row, row, wgt, wgt, wgt],
        out_specs=[seg3, row, row, row, row],
        out_shape=[jax.ShapeDtypeStruct((NSEG, t, D), BF16), act, act, act, act],
        compiler_params=_params(("parallel",)),
    )(do2, proj, ya, yb, attn, wout, wmo, wco)


def _adam_update(w, m, v, grad):
    mn = ADAM_B1 * m + (1.0 - ADAM_B1) * grad
    vn = ADAM_B2 * v + (1.0 - ADAM_B2) * (grad * grad)
    m_hat = mn / (1.0 - ADAM_B1 ** ADAM_STEP)
    v_hat = vn / (1.0 - ADAM_B2 ** ADAM_STEP)
    return -ADAM_LR * (m_hat / (jnp.sqrt(v_hat) + ADAM_EPS) + ADAM_WD * w), mn, vn


def adamw(w, m, v, g, name, token):
    rows, cols = w.shape
    tr = rows
    for cand in (256, 128, 64, 32, 16, 8):
        if rows % cand == 0 and rows > cand:
            tr = cand
            break

    def body(w_ref, m_ref, v_ref, g_ref, tok_ref, d_ref, mo_ref, vo_ref):
        d_ref[...], mo_ref[...], vo_ref[...] = _adam_update(w_ref[...], m_ref[...], v_ref[...], g_ref[...])

    blk = pl.BlockSpec((tr, cols), lambda i: (i, 0))
    return pl.pallas_call(
        body, name=name, grid=(rows // tr,),
        in_specs=[blk] * 4 + [pl.BlockSpec((8, 128), lambda i: (0, 0))], out_specs=[blk] * 3,
        out_shape=[jax.ShapeDtypeStruct((rows, cols), F32)] * 3,
        compiler_params=_params(("parallel",)),
    )(w, m, v, g, token)


def adamw_scattered(w, m, v, own, land, me, tr, name, transpose=False):
    slot_rows = land.shape[1]
    cols = land.shape[2]
    rows = slot_rows if transpose else w.shape[0]
    per_slot = slot_rows // tr

    def body(me_ref, w_ref, m_ref, v_ref, own_ref, land_ref, go_ref, d_ref, mo_ref, vo_ref):
        grad = own_ref[...].astype(F32)
        for s in range(8):
            grad = grad + jnp.where(me_ref[0] == s, 0.0, land_ref[s].astype(F32))
        if transpose:
            grad = grad.T
        go_ref[...] = grad
        d_ref[...], mo_ref[...], vo_ref[...] = _adam_update(w_ref[...], m_ref[...], v_ref[...], grad)

    wblk = pl.BlockSpec(w.shape if transpose else (tr, w.shape[1]), lambda i, s: (i, 0))
    return pl.pallas_call(
        body, name=name,
        grid_spec=pltpu.PrefetchScalarGridSpec(
            num_scalar_prefetch=1, grid=(rows // tr,),
            in_specs=[wblk, wblk, wblk,
                      pl.BlockSpec((tr, cols), lambda i, s: (s[0] * per_slot + i, 0)),
                      pl.BlockSpec((8, tr, cols), lambda i, s: (0, i, 0))],
            out_specs=[wblk] * 4),
        out_shape=[jax.ShapeDtypeStruct(w.shape, F32)] * 4,
        compiler_params=_params(),
    )(me, w, m, v, own, land)


def adamw_win(wt, mt, vt, ka, ra, kb, rb):
    rows = wt.shape[0]
    tc = 256
    nh = (D // 2) // tc

    def body(w_ref, m_ref, v_ref, ka_ref, ra_ref, kb_ref, rb_ref, go_ref, d_ref, mo_ref, vo_ref):
        first = pl.program_id(0) < nh
        grad = jnp.where(first, ka_ref[...] + ra_ref[...].astype(F32), kb_ref[...] + rb_ref[...].astype(F32))
        go_ref[...] = grad
        d_ref[...], mo_ref[...], vo_ref[...] = _adam_update(w_ref[...], m_ref[...], v_ref[...], grad)

    blk = pl.BlockSpec((rows, tc), lambda j: (0, j))
    lo = pl.BlockSpec((rows, tc), lambda j: (0, jnp.minimum(j, nh - 1)))
    hi = pl.BlockSpec((rows, tc), lambda j: (0, jnp.maximum(j - nh, 0)))
    return pl.pallas_call(
        body, name="adamw_w_in", grid=(D // tc,),
        in_specs=[blk, blk, blk, lo, lo, hi, hi], out_specs=[blk] * 4,
        out_shape=[jax.ShapeDtypeStruct((rows, D), F32)] * 4,
        compiler_params=_params(("parallel",)),
    )(wt, mt, vt, ka, ra, kb, rb)


_ORD_A = ("x", "y", "c")
_ORD_B = ("y", "x", "c")


def _rows128(a, rows):
    flat = a.reshape(-1)
    return jnp.pad(flat, (0, rows * 128 - flat.shape[0])).reshape(rows, 128)


def kernel(x, c, positions, w_ada, b_ada, g_pre, w_in, conv_w, w_conv_out, g_q, w_uq, g_kv, w_ukv, w_mla_out, w_out, g_post, loss_target, m_w_ada, m_b_ada, m_g_pre, m_w_in, m_conv_w, m_w_conv_out, m_g_q, m_w_uq, m_g_kv, m_w_ukv, m_w_mla_out, m_w_out, m_g_post, v_w_ada, v_b_ada, v_g_pre, v_w_in, v_conv_w, v_w_conv_out, v_g_q, v_w_uq, v_g_kv, v_w_ukv, v_w_mla_out, v_w_out, v_g_post):
    nb, seq, _ = x.shape
    t = nb * seq
    mx, my, mc = lax.axis_index("x"), lax.axis_index("y"), lax.axis_index("c")
    me = 4 * mx + 2 * my + mc
    co = {"x": mx, "y": my, "c": mc}

    x2 = x.reshape(t, D)
    tgt2 = loss_target.reshape(t, D)
    pos2 = positions.reshape(t, 1)

    ada_cols = w_ada.shape[2]
    b_cols = lax.dynamic_slice(b_ada, (0, me * ada_cols), (1, ada_cols))
    c_g, taps_g, mod_g = ada_gather(jnp.pad(c, ((0, 8 - nb), (0, 0))), _rows128(conv_w[0], 8), w_ada[0], b_cols)
    c_all = c_g[:, :nb].reshape(8 * nb, D)
    conv_full = taps_g[:, 0:3].transpose(1, 0, 2).reshape(3, D)
    conv_full8 = jnp.pad(conv_full, ((0, 5), (0, 0)))
    mod = mod_g[:, :nb].transpose(1, 0, 2).reshape(nb, 8 * ada_cols)
    shift = mod[:, 0:D].reshape(nb, 1, D)
    scale = mod[:, D:2 * D].reshape(nb, 1, D)
    gate = mod[:, 2 * D:3 * D].reshape(nb, 1, D)

    wt = w_in[0].T.astype(BF16)
    lo = lax.bitcast_convert_type(wt[:, :D // 2], jnp.uint16).astype(jnp.uint32)
    hi = lax.bitcast_convert_type(wt[:, D // 2:], jnp.uint16).astype(jnp.uint32)
    wt_bits = lax.bitcast_convert_type(lo | (hi << 16), F32)
    wt_bits, mod = lax.optimization_barrier((wt_bits, mod))
    shift = mod[:, 0:D].reshape(nb, 1, D)
    scale = mod[:, D:2 * D].reshape(nb, 1, D)
    gate = mod[:, 2 * D:3 * D].reshape(nb, 1, D)
    q4 = D // 4
    r3rd = wt_bits.shape[0] // 3
    plan = [(0, (k * r3rd, r3rd), (g * q4, q4), (_ORD_A, _ORD_B)[g]) for k in range(3) for g in range(2)]
    gw = allgather_big([wt_bits], plan, "gather_w_in", **prenorm_during(x2, scale, shift, g_pre, seq))
    gw, h = gw[:1], gw[1]
    late = [w_conv_out[0].astype(BF16), w_mla_out[0].astype(BF16), w_out[0].astype(BF16),
            jnp.pad(w_uq[0].T.astype(BF16), ((0, DQK - 192), (0, 0))), w_ukv[0].T.astype(BF16)]
    gw0, late = lax.optimization_barrier((gw[0], late))
    late_state, late_token = gather_start(late, "gather_late_start")
    wt_bits_all = gw0.reshape(N_IN, D // 2)

    inv_freq = ROPE_THETA ** (-jnp.arange(0, ROPE, 2, dtype=F32) / ROPE)
    invf = jnp.concatenate([inv_freq, inv_freq, jnp.zeros((128 - ROPE,), F32)]).reshape(1, 128)
    lane = np.arange(128)
    tabs = (invf,
            jnp.asarray(np.where(lane < HALF, -1.0, 0.0).reshape(1, 128), F32),
            jnp.asarray(np.where((lane >= HALF) & (lane < ROPE), 1.0, 0.0).reshape(1, 128), F32))

    proj, wt_p = proj_matmul(h, wt_bits_all, late_token)
    y = conv_fwd(proj, conv_full8, seq)
    gl = gather_wait(late_state, y, "gather_late_wait")
    wco = gl[0].reshape(D, D)
    wmo = gl[1].reshape(D, D)
    wout = gl[2].reshape(D, D)
    wuq_p = gl[3].reshape(H * DQK, QL)
    wukv = gl[4].reshape(H * 256, KVL)
    q_rot, k_cat, kv, qn, kvn = mla_prep_fwd(proj, pos2, g_q, g_kv, wuq_p, wukv, tabs)
    attn, lse = flash_fwd(q_rot, k_cat, kv, nb, seq)
    o, ya, yb, m, do2, dout, dgate, dg_post, loss_part = tail_fwd(
        y, attn, proj, x2, tgt2, gate, g_post, wco, wmo, wout, seq)

    dproj, dya, dyb, dattn, dy = tail_bwd(do2, proj, ya, yb, attn, wout, wmo, wco)
    g_wout = grad_matmul(m, do2, "grad_w_square")
    g_wmo = grad_matmul(o, dyb, "grad_w_square")
    g_wco = grad_matmul(y, dya, "grad_w_square")
    sc1, sc1_tok = scatter_start([g_wco, g_wmo, g_wout], "scatter_out_grads_start")
    dproj, dconv = conv_bwd(dproj, proj, dy, conv_full8, seq)
    dq_rot, dk, dv = flash_bwd(q_rot, k_cat, kv, attn, dattn, lse, nb, seq, sc1_tok)
    dproj, dq, dkv, dg_q, dg_kv = mla_prep_bwd(dproj, proj, dq_rot, dk, dv, pos2, g_q, g_kv, wuq_p, wukv, tabs)
    g_wuq_t = grad_matmul(dq, qn, "grad_w_uq")
    g_wukv_t = grad_matmul(dkv, kvn, "grad_w_ukv")
    sc2, sc2_tok = scatter_start([g_wuq_t, g_wukv_t], "scatter_mla_grads_start")
    g_win_p = win_grad_matmul(h, dproj, sc2_tok)

    g_wt = g_win_p.reshape(2, 2, 2, N_IN // 8, D)
    ords = [("c", "y", "x"), ("c", "x", "y")]
    hc = D // 2
    win_shape = (2, 2, N_IN // 8, hc)
    pick_w = lambda col: (lambda ref, cc: ref.at[:, :, 1 - cc["c"], :, pl.ds(col * hc, hc)])
    which1 = [0, 0]
    picks1 = [pick_w(0), pick_w(1)]
    st1, tok1 = swap_start([g_wt], which1, ["c"] * 2, picks1, [win_shape] * 2, "rs_c_start")
    assert nb == 2
    dh0 = dh_matmul(dproj, wt_p, tok1, seq, 0)
    (g_wt,), r1 = swap_wait(st1, dh0, which1, ["c"] * 2, picks1, "rs_c_wait")
    sel_xyc = jnp.stack([mx, my, mc]).astype(jnp.int32)
    sel2 = [jnp.stack([co[o[2]]]).astype(jnp.int32) for o in ords]
    first = [rs_win_add_first(g_wt, r1[0], sel_xyc, 1, 0, "rs_add_first_0"),
             rs_win_add_first(g_wt, r1[1], sel_xyc, 0, 1, "rs_add_first_1")]
    keep1, send1 = zip(*first)
    all4 = [0, 1]
    none4 = [None] * 2
    axes2 = [o[1] for o in ords]
    st2, tok2 = swap_start(list(send1), all4, axes2, none4, [s.shape for s in send1], "rs_ici1_start")

    dh1 = dh_matmul(dproj, wt_p, tok2, seq, 1)
    gx0, dsh0, dsc0, dgp0 = prenorm_bwd(dh0, x2, dout, scale, g_pre, seq, tok2, 0, None)
    _, r2 = swap_wait(st2, (gx0, dh1), all4, axes2, none4, "rs_ici1_wait")
    keep2, send2 = zip(*[rs_add_second(keep1[a], r2[a], sel2[a], "rs_add_second") for a in range(2)])
    axes3 = [o[2] for o in ords]
    st3, tok3 = swap_start(list(send2), all4, axes3, none4, [s.shape for s in send2], "rs_ici2_start")
    grad_x2, dsh1, dsc1, dgp1 = prenorm_bwd(dh1, x2, dout, scale, g_pre, seq, tok3, 1, gx0)
    dshift = jnp.stack([dsh0, dsh1])
    dscale = jnp.stack([dsc0, dsc1])
    dg_pre = dgp0 + dgp1

    dmod = jnp.concatenate([dshift, dscale, dgate], axis=2).reshape(nb * 3 * D // 128, 128)
    small = jnp.concatenate([
        dmod, _rows128(dg_pre, 8), _rows128(dg_post, 8), _rows128(dg_q, 8), _rows128(dg_kv, 8),
        dconv[0:3].reshape(24, 128), _rows128(loss_part, 8)], axis=0)
    small_g, sums = small_allgather(small, "gather_small_grads")
    dmod_all = small_g[:, 0:48].reshape(8 * nb, 3 * D)
    g_bada = (sums[0:24] + sums[24:48]).reshape(1, 3 * D)
    g_gpre = sums[48:56].reshape(1, D)
    g_gpost = sums[56:64].reshape(1, D)
    g_gq = sums[64:67].reshape(1, QL)
    g_gkv = sums[72:74].reshape(1, KVL)
    g_conv_full = sums[80:104].reshape(3, D)
    loss = sums[104, 0]
    g_conv = lax.dynamic_slice(g_conv_full, (0, me * 128), (3, 128))
    dmod_cols = lax.dynamic_slice(dmod_all, (0, me * ada_cols), (8 * nb, ada_cols))
    g_wada = ada_bwd(c_all, dmod_cols)

    res = {}
    res["w_ada"] = [o_[None] for o_ in (g_wada, *adamw(w_ada[0], m_w_ada[0], v_w_ada[0], g_wada, "adamw_w_ada", tok3))]

    def pack(b_, gp_, gpo_, gq_, gkv_, cw_):
        return jnp.concatenate([_rows128(b_, 24), _rows128(gp_, 8), _rows128(gpo_, 8), _rows128(gq_, 8),
                                _rows128(gkv_, 8), _rows128(cw_, 8)], axis=0)

    sw = pack(b_ada, g_pre, g_post, g_q, g_kv, conv_w)
    sm = pack(m_b_ada, m_g_pre, m_g_post, m_g_q, m_g_kv, m_conv_w)
    sv = pack(v_b_ada, v_g_pre, v_g_post, v_g_q, v_g_kv, v_conv_w)
    sg = pack(g_bada, g_gpre, g_gpost, g_gq, g_gkv, g_conv)
    small_out = (sg, *adamw(sw, sm, sv, sg, "adamw_small", tok3))

    _, r3 = swap_wait(st3, small_out[1], all4, axes3, none4, "rs_ici2_wait")

    (g_wco, g_wmo, g_wout), (l_wco, l_wmo, l_wout) = scatter_wait(sc1, small_out[2], "scatter_out_grads_wait")
    (g_wuq_t, g_wukv_t), (l_wuq, l_wukv) = scatter_wait(sc2, small_out[3], "scatter_mla_grads_wait")

    res["w_in"] = [o_.T[None] for o_ in adamw_win(w_in[0].T, m_w_in[0].T, v_w_in[0].T,
                                                  keep2[0], r3[0], keep2[1], r3[1])]
    me1 = me.reshape(1).astype(jnp.int32)
    res["w_uq"] = [o_.T[None] for o_ in adamw_scattered(
        w_uq[0].T, m_w_uq[0].T, v_w_uq[0].T, g_wuq_t, l_wuq, me1, 64, "adamw_w_uq")]
    res["w_ukv"] = [o_[None] for o_ in adamw_scattered(
        w_ukv[0], m_w_ukv[0], v_w_ukv[0], g_wukv_t, l_wukv, me1, KVL, "adamw_w_ukv", transpose=True)]
    for nm, wv, mv, vv, gg, ll in (("w_conv_out", w_conv_out, m_w_conv_out, v_w_conv_out, g_wco, l_wco),
                                   ("w_mla_out", w_mla_out, m_w_mla_out, v_w_mla_out, g_wmo, l_wmo),
                                   ("w_out", w_out, m_w_out, v_w_out, g_wout, l_wout)):
        res[nm] = [o_[None] for o_ in adamw_scattered(wv[0], mv[0], vv[0], gg, ll, me1, 128, "adamw_square")]

    def unpack(a):
        return {"b_ada": a[0:24].reshape(1, 3 * D), "g_pre": a[24:32].reshape(1, D),
                "g_post": a[32:40].reshape(1, D), "g_q": a[40:43].reshape(1, QL),
                "g_kv": a[48:50].reshape(1, KVL), "conv_w": a[56:59].reshape(-1)[:3 * 128].reshape(1, 3, 128)}

    for nm in ("b_ada", "g_pre", "g_post", "g_q", "g_kv", "conv_w"):
        res[nm] = [unpack(a)[nm] for a in small_out]

    order = ["w_ada", "b_ada", "g_pre", "w_in", "conv_w", "w_conv_out", "g_q", "w_uq", "g_kv", "w_ukv",
             "w_mla_out", "w_out", "g_post"]
    out = [loss, grad_x2.reshape(nb, seq, D)]
    for k_ in range(4):
        out += [res[nm][k_] for nm in order]
    return tuple(out)
```
